```python
import jax, jax.numpy as jnp
from jax import lax
import numpy as np

D_MODEL = 1024
BATCH = 16
SEQ = 4096
DEPTH = 1

HEAD_DIM = 64
N_HEADS = D_MODEL // HEAD_DIM
N_HEADS_A = N_HEADS // 2
N_KV_A = N_HEADS_A // 4
N_HEADS_B = N_HEADS - N_HEADS_A
N_KV_B = N_HEADS_B // 4
D_FF = 4 * D_MODEL
GRID_W = 64
BLOCK = 128
WINDOW = 128
N_BUCKETS = 32
MAX_DISTANCE = 128
ROPE_THETA = 10000.0
EPS = 1e-6
NEG_INF = -1e30
IN_WIDTHS = (N_HEADS_A * HEAD_DIM, N_KV_A * HEAD_DIM, N_KV_A * HEAD_DIM,
             N_HEADS_B * HEAD_DIM, N_KV_B * HEAD_DIM, N_KV_B * HEAD_DIM)
IN_TOTAL = sum(IN_WIDTHS)

kernel_name = "hybrid_axial_global_window_sink_encoder"


def rmsnorm(x, g):
    xf = x.astype(jnp.float32)
    y = xf * lax.rsqrt(jnp.mean(xf * xf, axis=-1, keepdims=True) + EPS)
    return (y * g.astype(jnp.float32)).astype(x.dtype)


def _rope_half(x, ang):
    cos = jnp.cos(ang).astype(x.dtype)
    sin = jnp.sin(ang).astype(x.dtype)
    x1, x2 = jnp.split(x, 2, axis=-1)
    return jnp.concatenate([x1 * cos - x2 * sin, x2 * cos + x1 * sin], axis=-1)


def axial_rope(x, row, col):
    half = HEAD_DIM // 2
    nf = half // 2
    freqs = ROPE_THETA ** (-jnp.arange(nf, dtype=jnp.float32) / nf)
    ang_r = row.astype(jnp.float32)[:, None] * freqs[None, :]
    ang_c = col.astype(jnp.float32)[:, None] * freqs[None, :]
    return jnp.concatenate([_rope_half(x[..., :half], ang_r),
                            _rope_half(x[..., half:], ang_c)], axis=-1)


def t5_bucket(rel):
    nb = N_BUCKETS // 2
    ret = (rel > 0).astype(jnp.int32) * nb
    n = jnp.abs(rel)
    max_exact = nb // 2
    nf = jnp.maximum(n, 1).astype(jnp.float32)
    large = max_exact + (jnp.log(nf / max_exact) / np.float32(np.log(MAX_DISTANCE / max_exact))
                         * (nb - max_exact)).astype(jnp.int32)
    large = jnp.minimum(large, nb - 1)
    return ret + jnp.where(n < max_exact, n, large)


def global_axial_gqa(q, k, v, gq, gk):
    bsz, s_len = q.shape[0], q.shape[1]
    rows = s_len // GRID_W
    row = jnp.repeat(jnp.arange(rows, dtype=jnp.int32), GRID_W)
    col = jnp.tile(jnp.arange(GRID_W, dtype=jnp.int32), rows)
    q = rmsnorm(q, gq).transpose(0, 2, 1, 3)
    k = rmsnorm(k, gk).transpose(0, 2, 1, 3)
    v = v.transpose(0, 2, 1, 3)
    q = axial_rope(q, row, col)
    k = axial_rope(k, row, col)
    grp = N_HEADS_A // N_KV_A
    nblk = s_len // BLOCK
    qb = jnp.moveaxis(q.reshape(bsz, N_KV_A, grp, nblk, BLOCK, HEAD_DIM), 3, 0)
    scale = HEAD_DIM ** -0.5

    def one_block(qblk):
        s = jnp.einsum('bkgqd,bksd->bkgqs', qblk, k).astype(jnp.float32) * scale
        p = jax.nn.softmax(s, axis=-1).astype(v.dtype)
        return jnp.einsum('bkgqs,bksd->bkgqd', p, v)

    o = lax.map(one_block, qb)
    return o.transpose(1, 0, 4, 2, 3, 5).reshape(bsz, s_len, N_HEADS_A * HEAD_DIM)


def window_sink_gqa(q, k, v, sink, rel_table):
    bsz, s_len = q.shape[0], q.shape[1]
    grp = N_HEADS_B // N_KV_B
    nblk = s_len // BLOCK
    span = 3 * BLOCK
    qb = jnp.moveaxis(q.transpose(0, 2, 1, 3).reshape(bsz, N_KV_B, grp, nblk, BLOCK, HEAD_DIM), 3, 0)
    pad = ((0, 0), (0, 0), (BLOCK, BLOCK), (0, 0))
    kp = jnp.pad(k.transpose(0, 2, 1, 3), pad)
    vp = jnp.pad(v.transpose(0, 2, 1, 3), pad)
    a = jnp.arange(BLOCK, dtype=jnp.int32)
    c = jnp.arange(span, dtype=jnp.int32)
    rel = c[None, :] - BLOCK - a[:, None]
    band = jnp.abs(rel) <= WINDOW
    bias = rel_table[t5_bucket(rel)]
    bias = bias.transpose(2, 0, 1).reshape(N_KV_B, grp, BLOCK, span).astype(jnp.float32)
    sink_l = sink.reshape(N_KV_B, grp, 1).astype(jnp.float32)
    scale = HEAD_DIM ** -0.5

    def one_block(args):
        qblk, n = args
        start = n * BLOCK
        kblk = lax.dynamic_slice_in_dim(kp, start, span, axis=2)
        vblk = lax.dynamic_slice_in_dim(vp, start, span, axis=2)
        kpos = start - BLOCK + c
        valid = band & ((kpos >= 0) & (kpos < s_len))[None, :]
        s = jnp.einsum('bkgqd,bksd->bkgqs', qblk, kblk).astype(jnp.float32) * scale + bias
        s = jnp.where(valid, s, NEG_INF)
        m = jnp.maximum(jnp.max(s, axis=-1), sink_l)
        p = jnp.exp(s - m[..., None])
        denom = jnp.sum(p, axis=-1) + jnp.exp(sink_l - m)
        w = (p / denom[..., None]).astype(vblk.dtype)
        return jnp.einsum('bkgqs,bksd->bkgqd', w, vblk)

    o = lax.map(one_block, (qb, jnp.arange(nblk, dtype=jnp.int32)))
    return o.transpose(1, 0, 4, 2, 3, 5).reshape(bsz, s_len, N_HEADS_B * HEAD_DIM)


def _fwd_setup_inputs(seed: int = 0) -> dict:
    key = jax.random.key(seed)
    ks = jax.random.split(key, 16)
    f32 = jnp.float32

    def gain(k, shape):
        return 1.0 + 0.02 * jax.random.normal(k, shape, f32)

    return {
        "x": jax.random.normal(ks[0], (BATCH, SEQ, D_MODEL), f32),
        "w_in": jax.random.normal(ks[1], (DEPTH, D_MODEL, IN_TOTAL), f32) * D_MODEL ** -0.5,
        "w_o": jax.random.normal(ks[2], (DEPTH, D_MODEL, D_MODEL), f32) * D_MODEL ** -0.5,
        "g_pre_mix": gain(ks[3], (DEPTH, D_MODEL)),
        "g_post_mix": gain(ks[4], (DEPTH, D_MODEL)),
        "q_norm_a": gain(ks[5], (DEPTH, HEAD_DIM)),
        "k_norm_a": gain(ks[6], (DEPTH, HEAD_DIM)),
        "sink_b": 0.5 * jax.random.normal(ks[7], (DEPTH, N_HEADS_B), f32),
        "rel_bias": 0.5 * jax.random.normal(ks[8], (N_BUCKETS, N_HEADS_B), f32),
        "g_pre_ffn": gain(ks[9], (DEPTH, D_MODEL)),
        "w_ffn_up": jax.random.normal(ks[10], (DEPTH, D_MODEL, D_FF), f32) * D_MODEL ** -0.5,
        "w_ffn_down": jax.random.normal(ks[11], (DEPTH, D_FF, D_MODEL), f32) * D_FF ** -0.5,
        "g_post_ffn": gain(ks[12], (DEPTH, D_MODEL)),
    }


def _fwd_reference(x, w_in, w_o, g_pre_mix, g_post_mix, q_norm_a, k_norm_a, sink_b, rel_bias,
              g_pre_ffn, w_ffn_up, w_ffn_down, g_post_ffn):
    bsz, s_len, _ = x.shape
    split_idx = [int(i) for i in np.cumsum(IN_WIDTHS)[:-1]]
    for l in range(DEPTH):
        h = rmsnorm(x, g_pre_mix[l])
        proj = h @ w_in[l]
        qa, ka, va, qb, kb, vb = jnp.split(proj, split_idx, axis=-1)
        qa = qa.reshape(bsz, s_len, N_HEADS_A, HEAD_DIM)
        ka = ka.reshape(bsz, s_len, N_KV_A, HEAD_DIM)
        va = va.reshape(bsz, s_len, N_KV_A, HEAD_DIM)
        qb = qb.reshape(bsz, s_len, N_HEADS_B, HEAD_DIM)
        kb = kb.reshape(bsz, s_len, N_KV_B, HEAD_DIM)
        vb = vb.reshape(bsz, s_len, N_KV_B, HEAD_DIM)
        out_a = global_axial_gqa(qa, ka, va, q_norm_a[l], k_norm_a[l])
        out_b = window_sink_gqa(qb, kb, vb, sink_b[l], rel_bias)
        mix = jnp.concatenate([out_a, out_b], axis=-1) @ w_o[l]
        x = x + rmsnorm(mix, g_post_mix[l])
        h = rmsnorm(x, g_pre_ffn[l])
        u = jax.nn.relu(h @ w_ffn_up[l])
        f = (u * u) @ w_ffn_down[l]
        x = x + rmsnorm(f, g_post_ffn[l])
    return x


import jax as _jax
import jax.numpy as _jnp

TWIN_FORMAT = 'train_step'
FWD_PARAMS = ['x', 'w_in', 'w_o', 'g_pre_mix', 'g_post_mix', 'q_norm_a', 'k_norm_a', 'sink_b', 'rel_bias', 'g_pre_ffn', 'w_ffn_up', 'w_ffn_down', 'g_post_ffn']
TWIN_WEIGHTS = ['w_in', 'w_o', 'g_pre_mix', 'g_post_mix', 'q_norm_a', 'k_norm_a', 'sink_b', 'rel_bias', 'g_pre_ffn', 'w_ffn_up', 'w_ffn_down', 'g_post_ffn']
TWIN_DIFF_INPUT = 'x'
TWIN_INPUTS = ['x', 'w_in', 'w_o', 'g_pre_mix', 'g_post_mix', 'q_norm_a', 'k_norm_a', 'sink_b', 'rel_bias', 'g_pre_ffn', 'w_ffn_up', 'w_ffn_down', 'g_post_ffn', 'loss_target', 'm_w_in', 'm_w_o', 'm_g_pre_mix', 'm_g_post_mix', 'm_q_norm_a', 'm_k_norm_a', 'm_sink_b', 'm_rel_bias', 'm_g_pre_ffn', 'm_w_ffn_up', 'm_w_ffn_down', 'm_g_post_ffn', 'v_w_in', 'v_w_o', 'v_g_pre_mix', 'v_g_post_mix', 'v_q_norm_a', 'v_k_norm_a', 'v_sink_b', 'v_rel_bias', 'v_g_pre_ffn', 'v_w_ffn_up', 'v_w_ffn_down', 'v_g_post_ffn']
TWIN_OUTPUTS = ['loss', 'grad_x', 'grad_w_in', 'grad_w_o', 'grad_g_pre_mix', 'grad_g_post_mix', 'grad_q_norm_a', 'grad_k_norm_a', 'grad_sink_b', 'grad_rel_bias', 'grad_g_pre_ffn', 'grad_w_ffn_up', 'grad_w_ffn_down', 'grad_g_post_ffn', 'delta_w_in', 'delta_w_o', 'delta_g_pre_mix', 'delta_g_post_mix', 'delta_q_norm_a', 'delta_k_norm_a', 'delta_sink_b', 'delta_rel_bias', 'delta_g_pre_ffn', 'delta_w_ffn_up', 'delta_w_ffn_down', 'delta_g_post_ffn', 'new_m_w_in', 'new_m_w_o', 'new_m_g_pre_mix', 'new_m_g_post_mix', 'new_m_q_norm_a', 'new_m_k_norm_a', 'new_m_sink_b', 'new_m_rel_bias', 'new_m_g_pre_ffn', 'new_m_w_ffn_up', 'new_m_w_ffn_down', 'new_m_g_post_ffn', 'new_v_w_in', 'new_v_w_o', 'new_v_g_pre_mix', 'new_v_g_post_mix', 'new_v_q_norm_a', 'new_v_k_norm_a', 'new_v_sink_b', 'new_v_rel_bias', 'new_v_g_pre_ffn', 'new_v_w_ffn_up', 'new_v_w_ffn_down', 'new_v_g_post_ffn']
TWIN_LEAF_KINDS = {'loss': 'loss', 'grad_x': 'grad_x', 'grad_w_in': 'grad_w', 'grad_w_o': 'grad_w', 'grad_g_pre_mix': 'grad_w', 'grad_g_post_mix': 'grad_w', 'grad_q_norm_a': 'grad_w', 'grad_k_norm_a': 'grad_w', 'grad_sink_b': 'grad_w', 'grad_rel_bias': 'grad_w', 'grad_g_pre_ffn': 'grad_w', 'grad_w_ffn_up': 'grad_w', 'grad_w_ffn_down': 'grad_w', 'grad_g_post_ffn': 'grad_w', 'delta_w_in': 'delta_w', 'delta_w_o': 'delta_w', 'delta_g_pre_mix': 'delta_w', 'delta_g_post_mix': 'delta_w', 'delta_q_norm_a': 'delta_w', 'delta_k_norm_a': 'delta_w', 'delta_sink_b': 'delta_w', 'delta_rel_bias': 'delta_w', 'delta_g_pre_ffn': 'delta_w', 'delta_w_ffn_up': 'delta_w', 'delta_w_ffn_down': 'delta_w', 'delta_g_post_ffn': 'delta_w', 'new_m_w_in': 'new_m', 'new_m_w_o': 'new_m', 'new_m_g_pre_mix': 'new_m', 'new_m_g_post_mix': 'new_m', 'new_m_q_norm_a': 'new_m', 'new_m_k_norm_a': 'new_m', 'new_m_sink_b': 'new_m', 'new_m_rel_bias': 'new_m', 'new_m_g_pre_ffn': 'new_m', 'new_m_w_ffn_up': 'new_m', 'new_m_w_ffn_down': 'new_m', 'new_m_g_post_ffn': 'new_m', 'new_v_w_in': 'new_v', 'new_v_w_o': 'new_v', 'new_v_g_pre_mix': 'new_v', 'new_v_g_post_mix': 'new_v', 'new_v_q_norm_a': 'new_v', 'new_v_k_norm_a': 'new_v', 'new_v_sink_b': 'new_v', 'new_v_rel_bias': 'new_v', 'new_v_g_pre_ffn': 'new_v', 'new_v_w_ffn_up': 'new_v', 'new_v_w_ffn_down': 'new_v', 'new_v_g_post_ffn': 'new_v'}


def _forward(args):
    return _fwd_reference(*[args[k] for k in FWD_PARAMS])


def _output_shape():
    out = _jax.eval_shape(lambda: _forward(_fwd_setup_inputs(0)))
    return out.shape, out.dtype

N_MICROBATCH = 1
ADAM_LR = 0.001
ADAM_B1 = 0.9
ADAM_B2 = 0.999
ADAM_EPS = 1e-08
ADAM_WD = 0.01
ADAM_STEP = 10
PER_EXAMPLE_BATCH_AXIS = {'x': 0, 'loss_target': 0}
SHARED_INPUTS = []
_WEIGHT_DTYPES = {'w_in': _jnp.float32, 'w_o': _jnp.float32, 'g_pre_mix': _jnp.float32, 'g_post_mix': _jnp.float32, 'q_norm_a': _jnp.float32, 'k_norm_a': _jnp.float32, 'sink_b': _jnp.float32, 'rel_bias': _jnp.float32, 'g_pre_ffn': _jnp.float32, 'w_ffn_up': _jnp.float32, 'w_ffn_down': _jnp.float32, 'g_post_ffn': _jnp.float32}
MOMENT_SCALE = {'w_in': 1.628611e+00, 'w_o': 2.097774e+00, 'g_pre_mix': 2.107637e+00, 'g_post_mix': 6.166620e+01, 'q_norm_a': 1.498554e+00, 'k_norm_a': 1.647077e+00, 'sink_b': 5.256339e-02, 'rel_bias': 9.869879e-01, 'g_pre_ffn': 1.265839e+00, 'w_ffn_up': 6.741900e-01, 'w_ffn_down': 2.138215e+00, 'g_post_ffn': 6.509986e+01}


def _to_microbatches(a, axis):
    t = _jnp.moveaxis(a, axis, 0)
    t = t.reshape((N_MICROBATCH, t.shape[0] // N_MICROBATCH) + t.shape[1:])
    return _jnp.moveaxis(t, 1, axis + 1)


def setup_inputs(seed: int = 0) -> dict:
    inp = _fwd_setup_inputs(seed)
    key = _jax.random.fold_in(_jax.random.key(seed), 7919)
    shape, _ = _output_shape()
    out = dict(inp)
    out["loss_target"] = _jax.random.normal(_jax.random.fold_in(key, 0), shape, _jnp.float32)
    for i, name in enumerate(TWIN_WEIGHTS):
        w = inp[name].astype(_jnp.float32)
        if MOMENT_SCALE is None:
            s = _jnp.sqrt(_jnp.mean(_jnp.square(w)) + 1e-30)
        else:
            s = MOMENT_SCALE[name]
        km, kv = _jax.random.split(_jax.random.fold_in(key, i + 1))
        out[name] = w
        out["m_" + name] = s * _jax.random.normal(km, w.shape, _jnp.float32)
        out["v_" + name] = (s * s) * _jax.random.uniform(kv, w.shape, _jnp.float32, 0.5, 1.5)
    if N_MICROBATCH > 1:
        for name, axis in PER_EXAMPLE_BATCH_AXIS.items():
            out[name] = _to_microbatches(out[name], axis)
    return {'x': out['x'], 'w_in': out['w_in'], 'w_o': out['w_o'], 'g_pre_mix': out['g_pre_mix'], 'g_post_mix': out['g_post_mix'], 'q_norm_a': out['q_norm_a'], 'k_norm_a': out['k_norm_a'], 'sink_b': out['sink_b'], 'rel_bias': out['rel_bias'], 'g_pre_ffn': out['g_pre_ffn'], 'w_ffn_up': out['w_ffn_up'], 'w_ffn_down': out['w_ffn_down'], 'g_post_ffn': out['g_post_ffn'], 'loss_target': out['loss_target'], 'm_w_in': out['m_w_in'], 'm_w_o': out['m_w_o'], 'm_g_pre_mix': out['m_g_pre_mix'], 'm_g_post_mix': out['m_g_post_mix'], 'm_q_norm_a': out['m_q_norm_a'], 'm_k_norm_a': out['m_k_norm_a'], 'm_sink_b': out['m_sink_b'], 'm_rel_bias': out['m_rel_bias'], 'm_g_pre_ffn': out['m_g_pre_ffn'], 'm_w_ffn_up': out['m_w_ffn_up'], 'm_w_ffn_down': out['m_w_ffn_down'], 'm_g_post_ffn': out['m_g_post_ffn'], 'v_w_in': out['v_w_in'], 'v_w_o': out['v_w_o'], 'v_g_pre_mix': out['v_g_pre_mix'], 'v_g_post_mix': out['v_g_post_mix'], 'v_q_norm_a': out['v_q_norm_a'], 'v_k_norm_a': out['v_k_norm_a'], 'v_sink_b': out['v_sink_b'], 'v_rel_bias': out['v_rel_bias'], 'v_g_pre_ffn': out['v_g_pre_ffn'], 'v_w_ffn_up': out['v_w_ffn_up'], 'v_w_ffn_down': out['v_w_ffn_down'], 'v_g_post_ffn': out['v_g_post_ffn']}


def _loss(weights, diff, rest, loss_target):
    with _jax.named_scope("forward"):
        args = {**rest, TWIN_DIFF_INPUT: diff, **{k: w.astype(_WEIGHT_DTYPES[k]) for k, w in weights.items()}}
        y = _forward(args)
    with _jax.named_scope("loss_head"):
        err = _jnp.square(y.astype(_jnp.float32) - loss_target)
        return 0.5 * _jnp.sum(_jnp.mean(err, axis=-1)) if err.ndim else 0.5 * err


def _adamw(w, g, m, v):
    m = ADAM_B1 * m + (1.0 - ADAM_B1) * g
    v = ADAM_B2 * v + (1.0 - ADAM_B2) * _jnp.square(g)
    m_hat = m / (1.0 - ADAM_B1 ** ADAM_STEP)
    v_hat = v / (1.0 - ADAM_B2 ** ADAM_STEP)
    delta = -ADAM_LR * (m_hat / (_jnp.sqrt(v_hat) + ADAM_EPS) + ADAM_WD * w)
    return delta, m, v


def reference(x, w_in, w_o, g_pre_mix, g_post_mix, q_norm_a, k_norm_a, sink_b, rel_bias, g_pre_ffn, w_ffn_up, w_ffn_down, g_post_ffn, loss_target, m_w_in, m_w_o, m_g_pre_mix, m_g_post_mix, m_q_norm_a, m_k_norm_a, m_sink_b, m_rel_bias, m_g_pre_ffn, m_w_ffn_up, m_w_ffn_down, m_g_post_ffn, v_w_in, v_w_o, v_g_pre_mix, v_g_post_mix, v_q_norm_a, v_k_norm_a, v_sink_b, v_rel_bias, v_g_pre_ffn, v_w_ffn_up, v_w_ffn_down, v_g_post_ffn):
    given = dict(x=x, w_in=w_in, w_o=w_o, g_pre_mix=g_pre_mix, g_post_mix=g_post_mix, q_norm_a=q_norm_a, k_norm_a=k_norm_a, sink_b=sink_b, rel_bias=rel_bias, g_pre_ffn=g_pre_ffn, w_ffn_up=w_ffn_up, w_ffn_down=w_ffn_down, g_post_ffn=g_post_ffn, loss_target=loss_target, m_w_in=m_w_in, m_w_o=m_w_o, m_g_pre_mix=m_g_pre_mix, m_g_post_mix=m_g_post_mix, m_q_norm_a=m_q_norm_a, m_k_norm_a=m_k_norm_a, m_sink_b=m_sink_b, m_rel_bias=m_rel_bias, m_g_pre_ffn=m_g_pre_ffn, m_w_ffn_up=m_w_ffn_up, m_w_ffn_down=m_w_ffn_down, m_g_post_ffn=m_g_post_ffn, v_w_in=v_w_in, v_w_o=v_w_o, v_g_pre_mix=v_g_pre_mix, v_g_post_mix=v_g_post_mix, v_q_norm_a=v_q_norm_a, v_k_norm_a=v_k_norm_a, v_sink_b=v_sink_b, v_rel_bias=v_rel_bias, v_g_pre_ffn=v_g_pre_ffn, v_w_ffn_up=v_w_ffn_up, v_w_ffn_down=v_w_ffn_down, v_g_post_ffn=v_g_post_ffn)
    weights = {n: given[n] for n in TWIN_WEIGHTS}
    shared = {n: given[n] for n in SHARED_INPUTS}
    per_example = {n: given[n] for n in ['x']}
    grad_fn = _jax.value_and_grad(_loss, argnums=(0, 1))

    def one_microbatch(ex, loss_target):
        ex = dict(ex)
        diff = ex.pop(TWIN_DIFF_INPUT)
        return grad_fn(weights, diff, {**shared, **ex}, loss_target)

    if N_MICROBATCH == 1:
        loss, (grad_w, grad_x) = one_microbatch(per_example, given["loss_target"])
    else:
        def body(carry, xs):
            loss_sum, grad_sum = carry
            l_k, (gw_k, gx_k) = one_microbatch(xs[0], xs[1])
            with _jax.named_scope("update"):
                return (loss_sum + l_k, _jax.tree.map(_jnp.add, grad_sum, gw_k)), gx_k

        init = (_jnp.zeros((), _jnp.float32), _jax.tree.map(_jnp.zeros_like, weights))
        (loss, grad_w), grad_x = _jax.lax.scan(body, init, (per_example, given["loss_target"]))
    with _jax.named_scope("update"):
        delta_w, new_m, new_v = {}, {}, {}
        for n in TWIN_WEIGHTS:
            delta_w[n], new_m[n], new_v[n] = _adamw(weights[n], grad_w[n], given["m_" + n], given["v_" + n])
    return (loss, grad_x, *[grad_w[n] for n in TWIN_WEIGHTS], *[delta_w[n] for n in TWIN_WEIGHTS],
            *[new_m[n] for n in TWIN_WEIGHTS], *[new_v[n] for n in TWIN_WEIGHTS])
```

```python
import functools

import jax
import jax.numpy as jnp
import numpy as np
from jax import lax
from jax.experimental import pallas as pl
from jax.experimental.pallas import tpu as pltpu

F32 = jnp.float32
BF16 = jnp.bfloat16
SDS = jax.ShapeDtypeStruct

N_DEV = 8
HEAD_DIM = 64
GROUP = 4
BLOCK = 128
SPAN = 3 * BLOCK
GRID_W = 64
N_BUCKETS = 32
MAX_DISTANCE = 128
ROPE_THETA = 10000.0
EPS = 1e-6
NEG_INF = -1e30
SCALE = HEAD_DIM ** -0.5

ADAM_LR = 0.001
ADAM_B1 = 0.9
ADAM_B2 = 0.999
ADAM_EPS = 1e-08
ADAM_WD = 0.01
ADAM_STEP = 10

VMEM_LIMIT = 56 * 1024 * 1024
MESH = pl.DeviceIdType.MESH


def _cp(*sem):
    return pltpu.CompilerParams(dimension_semantics=sem, vmem_limit_bytes=VMEM_LIMIT)


def _dot(a, b):
    return jnp.dot(a, b, preferred_element_type=F32)


def _dot_nt(a, b):
    return lax.dot_general(a, b, (((1,), (1,)), ((), ())), preferred_element_type=F32)


def _dot_tn(a, b):
    return lax.dot_general(a, b, (((0,), (0,)), ((), ())), preferred_element_type=F32)


def _rms_fwd(x, g):
    r = lax.rsqrt(jnp.mean(x * x, axis=-1, keepdims=True) + EPS)
    n = x * r
    return n * g, n, r


def _rms_bwd(n, r, g, dy):
    gd = g * dy
    dx = r * (gd - n * jnp.mean(n * gd, axis=-1, keepdims=True))
    return dx, dy * n


def _col_to_row(col, rows):
    r = col.shape[0]
    return jnp.broadcast_to(col, (r, 128)).T[0:rows, :]


def _rope(n, cos, sin_signed, first):
    partner = jnp.where(first, jnp.roll(n, -16, axis=1), jnp.roll(n, 16, axis=1))
    return n * cos + partner * sin_signed


def _rope_tables(s_len):
    rows = s_len // GRID_W
    row = jnp.repeat(jnp.arange(rows, dtype=jnp.int32), GRID_W)
    col = jnp.tile(jnp.arange(GRID_W, dtype=jnp.int32), rows)
    nf = HEAD_DIM // 4
    freqs = ROPE_THETA ** (-jnp.arange(nf, dtype=F32) / nf)
    ang_r = row.astype(F32)[:, None] * freqs[None, :]
    ang_c = col.astype(F32)[:, None] * freqs[None, :]
    cr, sr, cc, sc = jnp.cos(ang_r), jnp.sin(ang_r), jnp.cos(ang_c), jnp.sin(ang_c)
    cos = jnp.concatenate([cr, cr, cc, cc], axis=-1)
    sin_signed = jnp.concatenate([-sr, sr, -sc, sc], axis=-1)
    return cos, sin_signed


def _t5_bucket(rel):
    nb = N_BUCKETS // 2
    ret = (rel > 0).astype(jnp.int32) * nb
    n = jnp.abs(rel)
    max_exact = nb // 2
    nf = jnp.maximum(n, 1).astype(F32)
    large = max_exact + (jnp.log(nf / max_exact) / np.float32(np.log(MAX_DISTANCE / max_exact))
                         * (nb - max_exact)).astype(jnp.int32)
    large = jnp.minimum(large, nb - 1)
    return ret + jnp.where(n < max_exact, n, large)


def _mesh_pos():
    return lax.axis_index("x"), lax.axis_index("y"), lax.axis_index("c")


def _lin(p):
    return 4 * p[0] + 2 * p[1] + p[2]


def _weight_gather(shards):
    n = len(shards)

    def body(*refs):
        xs, outs = refs[:n], refs[n:2 * n]
        send_sems, recv_sems, local_sems = refs[2 * n:]
        x, y, c = _mesh_pos()
        me, sibling = (x, y, c), (x, y, 1 - c)
        chips = [(1 - x, y), (x, 1 - y), (1 - x, 1 - y)]

        def copy(a, k, block, to, src=None):
            slot = outs[a].at[_lin(block)]
            return pltpu.make_async_remote_copy(
                src_ref=slot if src is None else src, dst_ref=slot,
                send_sem=send_sems.at[a, k], recv_sem=recv_sems.at[a, k],
                device_id=to, device_id_type=MESH)

        started = []
        for a in range(n):
            mine = pltpu.make_async_copy(xs[a], outs[a].at[_lin(me)], local_sems.at[a])
            mine.start()
            started.append(mine)
        sends = []
        for a in range(n):
            first = [copy(a, 0, me, sibling, src=xs[a])]
            first += [copy(a, 1 + j, me, (*chip, c), src=xs[a]) for j, chip in enumerate(chips)]
            for cp in first:
                cp.start()
            sends += first
        for a in range(n):
            for j, chip in enumerate(chips):
                copy(a, 1 + j, (*chip, c), me).wait_recv()
                fwd = copy(a, 4 + j, (*chip, c), sibling)
                fwd.start()
                sends.append(fwd)
        for a in range(n):
            copy(a, 0, sibling, me).wait_recv()
            for j, chip in enumerate(chips):
                copy(a, 4 + j, (*chip, 1 - c), me).wait_recv()
        for cp in sends:
            cp.wait_send()
        for mine in started:
            mine.wait()

    anyspec = pl.BlockSpec(memory_space=pl.ANY)
    return pl.pallas_call(
        body,
        out_shape=[SDS((N_DEV,) + s.shape, s.dtype) for s in shards],
        in_specs=[anyspec] * n,
        out_specs=[anyspec] * n,
        scratch_shapes=[pltpu.SemaphoreType.DMA((n, 7)), pltpu.SemaphoreType.DMA((n, 7)),
                        pltpu.SemaphoreType.DMA((n,))],
        name="weight_gather",
    )(*shards)


def _grad_exchange(grads):
    n = len(grads)

    def body(*refs):
        gs, outs = refs[:n], refs[n:2 * n]
        send_sems, recv_sems, local_sems = refs[2 * n:]
        x, y, c = _mesh_pos()
        me = (x, y, c)
        peers = [(x, y, 1 - c), (1 - x, y, c), (x, 1 - y, c), (1 - x, 1 - y, c),
                 (1 - x, y, 1 - c), (x, 1 - y, 1 - c), (1 - x, 1 - y, 1 - c)]

        def copy(a, k, to):
            return pltpu.make_async_remote_copy(
                src_ref=gs[a].at[_lin(to)], dst_ref=outs[a].at[_lin(me)],
                send_sem=send_sems.at[a, k], recv_sem=recv_sems.at[a, k],
                device_id=to, device_id_type=MESH)

        def incoming(a, k, frm):
            return pltpu.make_async_remote_copy(
                src_ref=gs[a].at[_lin(frm)], dst_ref=outs[a].at[_lin(frm)],
                send_sem=send_sems.at[a, k], recv_sem=recv_sems.at[a, k],
                device_id=frm, device_id_type=MESH)

        local = []
        for a in range(n):
            mine = pltpu.make_async_copy(gs[a].at[_lin(me)], outs[a].at[_lin(me)], local_sems.at[a])
            mine.start()
            local.append(mine)
        sends = []
        for a in range(n):
            for k, p in enumerate(peers):
                cp = copy(a, k, p)
                cp.start()
                sends.append(cp)
        for a in range(n):
            for k, p in enumerate(peers):
                incoming(a, k, p).wait_recv()
        for cp in sends:
            cp.wait_send()
        for mine in local:
            mine.wait()

    anyspec = pl.BlockSpec(memory_space=pl.ANY)
    return pl.pallas_call(
        body,
        out_shape=[SDS(g.shape, g.dtype) for g in grads],
        in_specs=[anyspec] * n,
        out_specs=[anyspec] * n,
        scratch_shapes=[pltpu.SemaphoreType.DMA((n, 7)), pltpu.SemaphoreType.DMA((n, 7)),
                        pltpu.SemaphoreType.DMA((n,))],
        name="grad_exchange",
    )(*grads)


def _small_allreduce(v):
    rows, cols = v.shape

    def body(v_ref, out_ref, land_ref, send_sems, recv_sems):
        x, y, c = _mesh_pos()
        me = (x, y, c)
        peers = [(x, y, 1 - c), (1 - x, y, c), (x, 1 - y, c), (1 - x, 1 - y, c),
                 (1 - x, y, 1 - c), (x, 1 - y, 1 - c), (1 - x, 1 - y, 1 - c)]

        def copy(k, to, frm):
            return pltpu.make_async_remote_copy(
                src_ref=v_ref, dst_ref=land_ref.at[_lin(frm)],
                send_sem=send_sems.at[k], recv_sem=recv_sems.at[k],
                device_id=to, device_id_type=MESH)

        sends = [copy(k, p, me) for k, p in enumerate(peers)]
        for cp in sends:
            cp.start()
        land_ref[_lin(me)] = v_ref[...]
        for k, p in enumerate(peers):
            copy(k, p, p).wait_recv()
        for cp in sends:
            cp.wait_send()
        acc = land_ref[0]
        for s in range(1, N_DEV):
            acc = acc + land_ref[s]
        out_ref[...] = acc

    vm = pl.BlockSpec(memory_space=pltpu.VMEM)
    return pl.pallas_call(
        body,
        out_shape=SDS((rows, cols), F32),
        in_specs=[vm],
        out_specs=vm,
        scratch_shapes=[pltpu.VMEM((N_DEV, rows, cols), F32),
                        pltpu.SemaphoreType.DMA((7,)), pltpu.SemaphoreType.DMA((7,))],
        name="small_allreduce",
    )(v)


def _inproj(x2, g1, win_g, tm):
    t, d = x2.shape
    nd, _, pb = win_g.shape

    def body(x_ref, g_ref, w_ref, h_ref, p_ref):
        y, _, _ = _rms_fwd(x_ref[...], g_ref[...])
        h = y.astype(BF16)
        h_ref[...] = h
        for j in range(nd):
            p_ref[j] = _dot(h, w_ref[j])

    return pl.pallas_call(
        body,
        grid=(t // tm,),
        in_specs=[pl.BlockSpec((tm, d), lambda i: (i, 0)),
                  pl.BlockSpec((1, d), lambda i: (0, 0)),
                  pl.BlockSpec((nd, d, pb), lambda i: (0, 0, 0))],
        out_specs=[pl.BlockSpec((tm, d), lambda i: (i, 0)),
                   pl.BlockSpec((nd, tm, pb), lambda i: (0, i, 0))],
        out_shape=[SDS((t, d), BF16), SDS((nd, t, pb), F32)],
        compiler_params=_cp("parallel"),
        name="inproj",
    )(x2, g1, win_g)


def _head_slice(p_ref, hh, hpb):
    return p_ref[hh // hpb, :, pl.ds((hh % hpb) * HEAD_DIM, HEAD_DIM)]


def _qkprep(proj, cos, sin_signed, gq, gk, bl, s_len, ha, kva, hb, kvb, ts):
    nd, t, pb = proj.shape
    hpb = pb // HEAD_DIM
    ns = s_len // ts
    sp = s_len + 2 * BLOCK

    def body(p_ref, cos_ref, sin_ref, gq_ref, gk_ref, qa_ref, ka_ref, va_ref, qb_ref, kb_ref, vb_ref):
        i = pl.program_id(1)
        cs, sn = cos_ref[...], sin_ref[...]
        lane = lax.broadcasted_iota(jnp.int32, (ts, HEAD_DIM), 1)
        first = (lane % 32) < 16

        def normrope(xh, g):
            y, _, _ = _rms_fwd(xh, g)
            return _rope(y, cs, sn, first)

        for h in range(ha):
            qa_ref[0, h] = (normrope(_head_slice(p_ref, h, hpb), gq_ref[...]) * SCALE).astype(BF16)
        for h in range(kva):
            ka_ref[0, h] = normrope(_head_slice(p_ref, ha + h, hpb), gk_ref[...]).astype(BF16)
            va_ref[0, h] = _head_slice(p_ref, ha + kva + h, hpb).astype(BF16)
        base = ha + 2 * kva
        for h in range(hb):
            qb_ref[0, h] = (_head_slice(p_ref, base + h, hpb) * SCALE).astype(BF16)

        @pl.when(i == 0)
        def _():
            zeros = jnp.zeros((kvb, BLOCK, HEAD_DIM), BF16)
            kb_ref[0, :, 0:BLOCK, :] = zeros
            kb_ref[0, :, sp - BLOCK:sp, :] = zeros
            vb_ref[0, :, 0:BLOCK, :] = zeros
            vb_ref[0, :, sp - BLOCK:sp, :] = zeros

        row0 = pl.multiple_of(BLOCK + i * ts, BLOCK)
        for h in range(kvb):
            kb_ref[0, h, pl.ds(row0, ts), :] = _head_slice(p_ref, base + hb + h, hpb).astype(BF16)
            vb_ref[0, h, pl.ds(row0, ts), :] = _head_slice(p_ref, base + hb + kvb + h, hpb).astype(BF16)

    def hm(nh):
        return pl.BlockSpec((1, nh, ts, HEAD_DIM), lambda b, i: (b, 0, i, 0))

    def padded(nh):
        return pl.BlockSpec((1, nh, sp, HEAD_DIM), lambda b, i: (b, 0, 0, 0))

    return pl.pallas_call(
        body,
        grid=(bl, ns),
        in_specs=[pl.BlockSpec((nd, ts, pb), lambda b, i: (0, b * ns + i, 0)),
                  pl.BlockSpec((ts, HEAD_DIM), lambda b, i: (i, 0)),
                  pl.BlockSpec((ts, HEAD_DIM), lambda b, i: (i, 0)),
                  pl.BlockSpec((1, HEAD_DIM), lambda b, i: (0, 0)),
                  pl.BlockSpec((1, HEAD_DIM), lambda b, i: (0, 0))],
        out_specs=[hm(ha), hm(kva), hm(kva), hm(hb), padded(kvb), padded(kvb)],
        out_shape=[SDS((bl, ha, s_len, HEAD_DIM), BF16), SDS((bl, kva, s_len, HEAD_DIM), BF16),
                   SDS((bl, kva, s_len, HEAD_DIM), BF16), SDS((bl, hb, s_len, HEAD_DIM), BF16),
                   SDS((bl, kvb, sp, HEAD_DIM), BF16), SDS((bl, kvb, sp, HEAD_DIM), BF16)],
        compiler_params=_cp("parallel", "arbitrary"),
        name="qkprep",
    )(proj, cos, sin_signed, gq, gk)


def _bias_build(bucket, bucket_t, rel_bias, hb):
    kvb = hb // GROUP

    def body(bk_ref, bkt_ref, tbl_ref, bias_ref, biast_ref):
        bk, bkt = bk_ref[...], bkt_ref[...]
        for h in range(hb):
            acc = jnp.zeros((BLOCK, SPAN), F32)
            acct = jnp.zeros((SPAN, BLOCK), F32)
            for b in range(N_BUCKETS):
                val = tbl_ref[b, h]
                acc = jnp.where(bk == b, val, acc)
                acct = jnp.where(bkt == b, val, acct)
            bias_ref[h] = acc
            biast_ref[h // GROUP, :, (h % GROUP) * BLOCK:(h % GROUP + 1) * BLOCK] = acct

    vm = pl.BlockSpec(memory_space=pltpu.VMEM)
    return pl.pallas_call(
        body,
        in_specs=[vm, vm, pl.BlockSpec(memory_space=pltpu.SMEM)],
        out_specs=[vm, vm],
        out_shape=[SDS((hb, BLOCK, SPAN), F32), SDS((kvb, SPAN, GROUP * BLOCK), F32)],
        name="bias_build",
    )(bucket, bucket_t, rel_bias)


def _attn_a_fwd(qa, ka, va, tq, tk):
    bl, ha, s_len, _ = qa.shape
    kv = ka.shape[1]
    nq, nk = s_len // tq, s_len // tk
    r = GROUP * tq

    def body(q_ref, k_ref, v_ref, o_ref, l_ref, m_sc, l_sc, acc_sc):
        q = q_ref[0].reshape(r, HEAD_DIM)
        m_sc[...] = jnp.full((r, 1), -jnp.inf, F32)
        l_sc[...] = jnp.zeros((r, 1), F32)
        acc_sc[...] = jnp.zeros((r, HEAD_DIM), F32)

        def step(c, carry):
            rows = pl.ds(pl.multiple_of(c * tk, tk), tk)
            s = _dot_nt(q, k_ref[0, 0, rows, :])
            m_old = m_sc[...]
            m_new = jnp.maximum(m_old, jnp.max(s, axis=1, keepdims=True))
            alpha = jnp.exp(m_old - m_new)
            p = jnp.exp(s - m_new)
            l_sc[...] = alpha * l_sc[...] + jnp.sum(p, axis=1, keepdims=True)
            acc_sc[...] = alpha * acc_sc[...] + _dot(p.astype(BF16), v_ref[0, 0, rows, :])
            m_sc[...] = m_new
            return carry

        lax.fori_loop(0, nk, step, 0)
        o = acc_sc[...] / l_sc[...]
        for h in range(GROUP):
            o_ref[:, h * HEAD_DIM:(h + 1) * HEAD_DIM] = o[h * tq:(h + 1) * tq].astype(BF16)
        l_ref[0, 0, 0] = _col_to_row(m_sc[...] + jnp.log(l_sc[...]), 8)

    return pl.pallas_call(
        body,
        grid=(bl, kv, nq),
        in_specs=[pl.BlockSpec((1, GROUP, tq, HEAD_DIM), lambda b, g, i: (b, g, i, 0)),
                  pl.BlockSpec((1, 1, s_len, HEAD_DIM), lambda b, g, i: (b, g, 0, 0)),
                  pl.BlockSpec((1, 1, s_len, HEAD_DIM), lambda b, g, i: (b, g, 0, 0))],
        out_specs=[pl.BlockSpec((tq, GROUP * HEAD_DIM), lambda b, g, i: (b * nq + i, g)),
                   pl.BlockSpec((1, 1, 1, 8, r), lambda b, g, i: (b, g, i, 0, 0))],
        out_shape=[SDS((bl * s_len, ha * HEAD_DIM), BF16), SDS((bl, kv, nq, 8, r), F32)],
        scratch_shapes=[pltpu.VMEM((r, 1), F32), pltpu.VMEM((r, 1), F32), pltpu.VMEM((r, HEAD_DIM), F32)],
        compiler_params=_cp("parallel", "parallel", "arbitrary"),
        name="attn_a_fwd",
    )(qa, ka, va)


def _band_mask(shape, q_axis, n, s_len):
    k_axis = 1 - q_axis
    qi = lax.broadcasted_iota(jnp.int32, shape, q_axis) % BLOCK
    ci = lax.broadcasted_iota(jnp.int32, shape, k_axis)
    rel = ci - BLOCK - qi
    kpos = n * BLOCK - BLOCK + ci
    return (jnp.abs(rel) <= BLOCK) & (kpos >= 0) & (kpos < s_len)


def _attn_b_fwd(qb, kb, vb, bias, sink, s_len):
    bl, hb, _, _ = qb.shape
    kv = kb.shape[1]
    sp = kb.shape[2]
    nb = s_len // BLOCK
    r = GROUP * BLOCK

    def body(q_ref, k_ref, v_ref, bias_ref, sink_ref, o_ref, l_ref):
        g, n = pl.program_id(1), pl.program_id(2)
        q = q_ref[0].reshape(r, HEAD_DIM)
        rows = pl.ds(pl.multiple_of(n * BLOCK, BLOCK), SPAN)
        s = _dot_nt(q, k_ref[0, 0, rows, :]) + bias_ref[...].reshape(r, SPAN)
        s = jnp.where(_band_mask((r, SPAN), 0, n, s_len), s, NEG_INF)
        sinkc = jnp.concatenate([jnp.full((BLOCK, 1), sink_ref[0, g * GROUP + h], F32) for h in range(GROUP)], axis=0)
        m = jnp.maximum(jnp.max(s, axis=1, keepdims=True), sinkc)
        p = jnp.exp(s - m)
        denom = jnp.sum(p, axis=1, keepdims=True) + jnp.exp(sinkc - m)
        w = (p / denom).astype(BF16)
        o = _dot(w, v_ref[0, 0, rows, :])
        for h in range(GROUP):
            o_ref[:, h * HEAD_DIM:(h + 1) * HEAD_DIM] = o[h * BLOCK:(h + 1) * BLOCK].astype(BF16)
        l_ref[0, 0, 0] = _col_to_row(m + jnp.log(denom), 8)

    return pl.pallas_call(
        body,
        grid=(bl, kv, nb),
        in_specs=[pl.BlockSpec((1, GROUP, BLOCK, HEAD_DIM), lambda b, g, n: (b, g, n, 0)),
                  pl.BlockSpec((1, 1, sp, HEAD_DIM), lambda b, g, n: (b, g, 0, 0)),
                  pl.BlockSpec((1, 1, sp, HEAD_DIM), lambda b, g, n: (b, g, 0, 0)),
                  pl.BlockSpec((GROUP, BLOCK, SPAN), lambda b, g, n: (g, 0, 0)),
                  pl.BlockSpec(memory_space=pltpu.SMEM)],
        out_specs=[pl.BlockSpec((BLOCK, GROUP * HEAD_DIM), lambda b, g, n: (b * nb + n, g)),
                   pl.BlockSpec((1, 1, 1, 8, r), lambda b, g, n: (b, g, n, 0, 0))],
        out_shape=[SDS((bl * s_len, hb * HEAD_DIM), BF16), SDS((bl, kv, nb, 8, r), F32)],
        compiler_params=_cp("parallel", "parallel", "arbitrary"),
        name="attn_b_fwd",
    )(qb, kb, vb, bias, sink)


def _mixout(oa, ob, wo, x2, g2, g3, tm):
    t, d = x2.shape
    ca = oa.shape[1]

    def body(oa_ref, ob_ref, w_ref, x_ref, g2_ref, g3_ref, mix_ref, x1_ref, h2_ref):
        mix = _dot(oa_ref[...], w_ref[0:ca, :]) + _dot(ob_ref[...], w_ref[ca:, :])
        mix_ref[...] = mix
        y2, _, _ = _rms_fwd(mix, g2_ref[...])
        x1 = x_ref[...] + y2
        x1_ref[...] = x1
        y3, _, _ = _rms_fwd(x1, g3_ref[...])
        h2_ref[...] = y3.astype(BF16)

    tile = lambda w: pl.BlockSpec((tm, w), lambda i: (i, 0))
    vec = pl.BlockSpec((1, d), lambda i: (0, 0))
    return pl.pallas_call(
        body,
        grid=(t // tm,),
        in_specs=[tile(ca), tile(ob.shape[1]), pl.BlockSpec(wo.shape, lambda i: (0, 0)), tile(d), vec, vec],
        out_specs=[tile(d), tile(d), tile(d)],
        out_shape=[SDS((t, d), F32), SDS((t, d), F32), SDS((t, d), BF16)],
        compiler_params=_cp("parallel"),
        name="mixout",
    )(oa, ob, wo, x2, g2, g3)


def _ffn_fwd(h2, wup_g, wdn, x1, target, g4, tm):
    t, d = x1.shape
    nj, _, tf = wup_g.shape
    ff = nj * tf
    nt = t // tm

    def body(h_ref, wu_ref, wd_ref, x1_ref, tg_ref, g_ref, u_ref, df_ref, dy_ref, dg_ref, loss_ref, acc_sc):
        i, j = pl.program_id(0), pl.program_id(1)

        @pl.when(j == 0)
        def _():
            acc_sc[...] = jnp.zeros_like(acc_sc)

        @pl.when((i == 0) & (j == 0))
        def _():
            dg_ref[...] = jnp.zeros_like(dg_ref)
            loss_ref[...] = jnp.zeros_like(loss_ref)

        u = jnp.maximum(_dot(h_ref[...], wu_ref[0]), 0.0)
        u_ref[...] = u.astype(BF16)
        acc_sc[...] += _dot((u * u).astype(BF16), wd_ref[...])

        @pl.when(j == nj - 1)
        def _():
            g = g_ref[...]
            y4, n, r = _rms_fwd(acc_sc[...], g)
            e = (x1_ref[...] + y4) - tg_ref[...]
            loss_ref[...] += jnp.sum(e * e) * (0.5 / d)
            dy = e * (1.0 / d)
            dy_ref[...] = dy
            df, dgt = _rms_bwd(n, r, g, dy)
            df_ref[...] = df.astype(BF16)
            dg_ref[0:1, :] += jnp.sum(dgt, axis=0, keepdims=True)

    tile = pl.BlockSpec((tm, d), lambda i, j: (i, 0))
    return pl.pallas_call(
        body,
        grid=(nt, nj),
        in_specs=[tile,
                  pl.BlockSpec((1, d, tf), lambda i, j: (j, 0, 0)),
                  pl.BlockSpec((tf, d), lambda i, j: (j, 0)),
                  tile, tile,
                  pl.BlockSpec((1, d), lambda i, j: (0, 0))],
        out_specs=[pl.BlockSpec((tm, tf), lambda i, j: (i, j)), tile, tile,
                   pl.BlockSpec((8, d), lambda i, j: (0, 0)),
                   pl.BlockSpec((8, 128), lambda i, j: (0, 0))],
        out_shape=[SDS((t, ff), BF16), SDS((t, d), BF16), SDS((t, d), F32), SDS((8, d), F32), SDS((8, 128), F32)],
        scratch_shapes=[pltpu.VMEM((tm, d), F32)],
        compiler_params=_cp("arbitrary", "arbitrary"),
        name="ffn_fwd",
    )(h2, wup_g, wdn, x1, target, g4)


def _ffn_bwd(df, u, wdn, wup_g, x1, dy, mix, g3, g2, tm):
    t, d = x1.shape
    nj, _, tf = wup_g.shape
    nt = t // tm

    def body(df_ref, u_ref, wd_ref, wu_ref, x1_ref, dy_ref, mix_ref, g3_ref, g2_ref,
             dpre_ref, dx1_ref, dmix_ref, dg3_ref, dg2_ref, acc_sc):
        i, j = pl.program_id(0), pl.program_id(1)

        @pl.when(j == 0)
        def _():
            acc_sc[...] = jnp.zeros_like(acc_sc)

        @pl.when((i == 0) & (j == 0))
        def _():
            dg3_ref[...] = jnp.zeros_like(dg3_ref)
            dg2_ref[...] = jnp.zeros_like(dg2_ref)

        du2 = _dot_nt(df_ref[...], wd_ref[...])
        dpre = (2.0 * u_ref[...].astype(F32) * du2).astype(BF16)
        dpre_ref[...] = dpre
        acc_sc[...] += _dot_nt(dpre, wu_ref[0])

        @pl.when(j == nj - 1)
        def _():
            g3, g2 = g3_ref[...], g2_ref[...]
            _, n3, r3 = _rms_fwd(x1_ref[...], g3)
            dx, dgt3 = _rms_bwd(n3, r3, g3, acc_sc[...])
            dx1 = dy_ref[...] + dx
            dx1_ref[...] = dx1
            dg3_ref[0:1, :] += jnp.sum(dgt3, axis=0, keepdims=True)
            _, n2, r2 = _rms_fwd(mix_ref[...], g2)
            dmix, dgt2 = _rms_bwd(n2, r2, g2, dx1)
            dmix_ref[...] = dmix.astype(BF16)
            dg2_ref[0:1, :] += jnp.sum(dgt2, axis=0, keepdims=True)

    tile = pl.BlockSpec((tm, d), lambda i, j: (i, 0))
    vec = pl.BlockSpec((1, d), lambda i, j: (0, 0))
    acc8 = pl.BlockSpec((8, d), lambda i, j: (0, 0))
    return pl.pallas_call(
        body,
        grid=(nt, nj),
        in_specs=[tile,
                  pl.BlockSpec((tm, tf), lambda i, j: (i, j)),
                  pl.BlockSpec((tf, d), lambda i, j: (j, 0)),
                  pl.BlockSpec((1, d, tf), lambda i, j: (j, 0, 0)),
                  tile, tile, tile, vec, vec],
        out_specs=[pl.BlockSpec((tm, tf), lambda i, j: (i, j)), tile, tile, acc8, acc8],
        out_shape=[SDS(u.shape, BF16), SDS((t, d), F32), SDS((t, d), BF16), SDS((8, d), F32), SDS((8, d), F32)],
        scratch_shapes=[pltpu.VMEM((tm, d), F32)],
        compiler_params=_cp("arbitrary", "arbitrary"),
        name="ffn_bwd",
    )(df, u, wdn, wup_g, x1, dy, mix, g3, g2)


def _wgrad_cols(a, b, nj, tt, name):
    t, m = a.shape
    n = b.shape[1]
    bn = n // nj

    def body(a_ref, b_ref, o_ref):
        @pl.when(pl.program_id(1) == 0)
        def _():
            o_ref[...] = jnp.zeros_like(o_ref)

        o_ref[0] += _dot_tn(a_ref[...], b_ref[...])

    return pl.pallas_call(
        body,
        grid=(nj, t // tt),
        in_specs=[pl.BlockSpec((tt, m), lambda j, k: (k, 0)),
                  pl.BlockSpec((tt, bn), lambda j, k: (k, j))],
        out_specs=pl.BlockSpec((1, m, bn), lambda j, k: (j, 0, 0)),
        out_shape=SDS((nj, m, bn), F32),
        compiler_params=_cp("parallel", "arbitrary"),
        name=name,
    )(a, b)


def _wgrad_cols_blocked(a, b3, tt, name):
    t, m = a.shape
    nj, _, bn = b3.shape

    def body(a_ref, b_ref, o_ref):
        @pl.when(pl.program_id(1) == 0)
        def _():
            o_ref[...] = jnp.zeros_like(o_ref)

        o_ref[0] += _dot_tn(a_ref[...], b_ref[0])

    return pl.pallas_call(
        body,
        grid=(nj, t // tt),
        in_specs=[pl.BlockSpec((tt, m), lambda j, k: (k, 0)),
                  pl.BlockSpec((1, tt, bn), lambda j, k: (j, k, 0))],
        out_specs=pl.BlockSpec((1, m, bn), lambda j, k: (j, 0, 0)),
        out_shape=SDS((nj, m, bn), F32),
        compiler_params=_cp("parallel", "arbitrary"),
        name=name,
    )(a, b3)


def _wgrad_rows(a_parts, b, nj, tt, square, name):
    t, n = b.shape
    widths = [p.shape[1] for p in a_parts]
    m = sum(widths)
    bm = m // nj
    per = [w // bm for w in widths]
    starts = [sum(per[:q]) for q in range(len(per))]
    np_ = len(a_parts)

    def body(*refs):
        a_refs, b_ref, o_ref = refs[:np_], refs[np_], refs[np_ + 1]
        j = pl.program_id(0)

        @pl.when(pl.program_id(1) == 0)
        def _():
            o_ref[...] = jnp.zeros_like(o_ref)

        for q in range(np_):
            @pl.when((j >= starts[q]) & (j < starts[q] + per[q]))
            def _(q=q):
                a = a_refs[q][...]
                if square:
                    af = a.astype(F32)
                    a = (af * af).astype(BF16)
                o_ref[0] += _dot_tn(a, b_ref[...])

    def a_spec(q):
        return pl.BlockSpec((tt, bm), lambda j, k: (k, jnp.clip(j - starts[q], 0, per[q] - 1)))

    return pl.pallas_call(
        body,
        grid=(nj, t // tt),
        in_specs=[a_spec(q) for q in range(np_)] + [pl.BlockSpec((tt, n), lambda j, k: (k, 0))],
        out_specs=pl.BlockSpec((1, bm, n), lambda j, k: (j, 0, 0)),
        out_shape=SDS((nj, bm, n), F32),
        compiler_params=_cp("parallel", "arbitrary"),
        name=name,
    )(*a_parts, b)


def _attn_out_bwd(dmix, wo, ca, tm):
    t, d = dmix.shape
    cb = wo.shape[0] - ca

    def body(dm_ref, w_ref, da_ref, db_ref):
        dm = dm_ref[...]
        da_ref[...] = _dot_nt(dm, w_ref[0:ca, :]).astype(BF16)
        db_ref[...] = _dot_nt(dm, w_ref[ca:, :]).astype(BF16)

    return pl.pallas_call(
        body,
        grid=(t // tm,),
        in_specs=[pl.BlockSpec((tm, d), lambda i: (i, 0)), pl.BlockSpec(wo.shape, lambda i: (0, 0))],
        out_specs=[pl.BlockSpec((tm, ca), lambda i: (i, 0)), pl.BlockSpec((tm, cb), lambda i: (i, 0))],
        out_shape=[SDS((t, ca), BF16), SDS((t, cb), BF16)],
        compiler_params=_cp("parallel"),
        name="attn_out_bwd",
    )(dmix, wo)


def _stack_heads(ref, rows):
    return jnp.concatenate([ref[:, h * HEAD_DIM:(h + 1) * HEAD_DIM] for h in range(GROUP)], axis=0)


def _attn_a_bwd(qa, ka, va, do, o, lse, tq, tk):
    bl, ha, s_len, _ = qa.shape
    kv = ka.shape[1]
    nq, nk = s_len // tq, s_len // tk
    r = GROUP * tq

    def body(q_ref, k_ref, v_ref, do_ref, o_ref, l_ref, dq_ref, dk_ref, dv_ref, dq_sc):
        i = pl.program_id(2)
        q = q_ref[0].reshape(r, HEAD_DIM)
        do2 = _stack_heads(do_ref, tq)
        o2 = _stack_heads(o_ref, tq)
        drow = _col_to_row(jnp.sum(do2.astype(F32) * o2.astype(F32), axis=1, keepdims=True), 1)
        lrow = l_ref[0, 0, 0, 0:1, :]

        @pl.when(i == 0)
        def _():
            dk_ref[...] = jnp.zeros_like(dk_ref)
            dv_ref[...] = jnp.zeros_like(dv_ref)

        dq_sc[...] = jnp.zeros_like(dq_sc)

        def step(c, carry):
            rows = pl.ds(pl.multiple_of(c * tk, tk), tk)
            kc, vc = k_ref[0, 0, rows, :], v_ref[0, 0, rows, :]
            pt = jnp.exp(_dot_nt(kc, q) - lrow)
            dst = pt * (_dot_nt(vc, do2) - drow)
            ptb, dsb = pt.astype(BF16), dst.astype(BF16)
            dv_ref[0, 0, rows, :] += _dot(ptb, do2)
            dk_ref[0, 0, rows, :] += _dot(dsb, q)
            dq_sc[...] += _dot_tn(dsb, kc)
            return carry

        lax.fori_loop(0, nk, step, 0)
        dq_ref[0] = dq_sc[...].reshape(GROUP, tq, HEAD_DIM)

    kvspec = pl.BlockSpec((1, 1, s_len, HEAD_DIM), lambda b, g, i: (b, g, 0, 0))
    qspec = pl.BlockSpec((1, GROUP, tq, HEAD_DIM), lambda b, g, i: (b, g, i, 0))
    tok = pl.BlockSpec((tq, GROUP * HEAD_DIM), lambda b, g, i: (b * nq + i, g))
    return pl.pallas_call(
        body,
        grid=(bl, kv, nq),
        in_specs=[qspec, kvspec, kvspec, tok, tok,
                  pl.BlockSpec((1, 1, 1, 8, r), lambda b, g, i: (b, g, i, 0, 0))],
        out_specs=[qspec, kvspec, kvspec],
        out_shape=[SDS(qa.shape, F32), SDS(ka.shape, F32), SDS(va.shape, F32)],
        scratch_shapes=[pltpu.VMEM((r, HEAD_DIM), F32)],
        compiler_params=_cp("parallel", "parallel", "arbitrary"),
        name="attn_a_bwd",
    )(qa, ka, va, do, o, lse)


def _attn_b_bwd(qb, kb, vb, do, o, lse, bias_t, sink, s_len):
    bl, hb, _, _ = qb.shape
    kv, sp = kb.shape[1], kb.shape[2]
    nb = s_len // BLOCK
    r = GROUP * BLOCK

    def body(q_ref, k_ref, v_ref, do_ref, o_ref, l_ref, bt_ref, sink_ref,
             dq_ref, dk_ref, dv_ref, dsum_ref, dsink_ref):
        g, b, n = pl.program_id(0), pl.program_id(1), pl.program_id(2)
        q = q_ref[0].reshape(r, HEAD_DIM)
        do2 = _stack_heads(do_ref, BLOCK)
        o2 = _stack_heads(o_ref, BLOCK)
        drow = _col_to_row(jnp.sum(do2.astype(F32) * o2.astype(F32), axis=1, keepdims=True), 1)
        lrow = l_ref[0, 0, 0, 0:1, :]

        @pl.when(n == 0)
        def _():
            dk_ref[...] = jnp.zeros_like(dk_ref)
            dv_ref[...] = jnp.zeros_like(dv_ref)

        @pl.when((b == 0) & (n == 0))
        def _():
            dsum_ref[...] = jnp.zeros_like(dsum_ref)
            dsink_ref[...] = jnp.zeros_like(dsink_ref)

        rows = pl.ds(pl.multiple_of(n * BLOCK, BLOCK), SPAN)
        ks, vs = k_ref[0, 0, rows, :], v_ref[0, 0, rows, :]
        st = _dot_nt(ks, q) + bt_ref[0]
        st = jnp.where(_band_mask((SPAN, r), 1, n, s_len), st, NEG_INF)
        pt = jnp.exp(st - lrow)
        dst = pt * (_dot_nt(vs, do2) - drow)
        dsum_ref[0] += dst
        sink_row = jnp.concatenate(
            [jnp.full((1, BLOCK), sink_ref[0, g * GROUP + h], F32) for h in range(GROUP)], axis=1)
        dsink_ref[0, 0:1, :] += -(jnp.exp(sink_row - lrow) * drow)
        ptb, dsb = pt.astype(BF16), dst.astype(BF16)
        dv_ref[0, 0, rows, :] += _dot(ptb, do2)
        dk_ref[0, 0, rows, :] += _dot(dsb, q)
        dq_ref[0] = _dot_tn(dsb, ks).reshape(GROUP, BLOCK, HEAD_DIM)

    kvspec = pl.BlockSpec((1, 1, sp, HEAD_DIM), lambda g, b, n: (b, g, 0, 0))
    qspec = pl.BlockSpec((1, GROUP, BLOCK, HEAD_DIM), lambda g, b, n: (b, g, n, 0))
    tok = pl.BlockSpec((BLOCK, GROUP * HEAD_DIM), lambda g, b, n: (b * nb + n, g))
    return pl.pallas_call(
        body,
        grid=(kv, bl, nb),
        in_specs=[qspec, kvspec, kvspec, tok, tok,
                  pl.BlockSpec((1, 1, 1, 8, r), lambda g, b, n: (b, g, n, 0, 0)),
                  pl.BlockSpec((1, SPAN, r), lambda g, b, n: (g, 0, 0)),
                  pl.BlockSpec(memory_space=pltpu.SMEM)],
        out_specs=[qspec, kvspec, kvspec,
                   pl.BlockSpec((1, SPAN, r), lambda g, b, n: (g, 0, 0)),
                   pl.BlockSpec((1, 8, r), lambda g, b, n: (g, 0, 0))],
        out_shape=[SDS(qb.shape, F32), SDS(kb.shape, F32), SDS(vb.shape, F32),
                   SDS((kv, SPAN, r), F32), SDS((kv, 8, r), F32)],
        compiler_params=_cp("arbitrary", "arbitrary", "arbitrary"),
        name="attn_b_bwd",
    )(qb, kb, vb, do, o, lse, bias_t, sink)


def _bias_reduce(dsum, dsink, bucket_t4):
    kv, _, r = dsum.shape

    def body(ds_ref, dk_ref, bk_ref, rel_ref, sink_ref):
        lane = lax.broadcasted_iota(jnp.int32, (N_BUCKETS, 128), 1)
        lane8 = lax.broadcasted_iota(jnp.int32, (8, 128), 1)
        bk = bk_ref[...]
        for g in range(kv):
            ds = ds_ref[g]
            rowi = lax.broadcasted_iota(jnp.int32, (N_BUCKETS, r), 0)
            red = jnp.zeros((N_BUCKETS, r), F32)
            for b in range(N_BUCKETS):
                red = jnp.where(rowi == b, jnp.sum(jnp.where(bk == b, ds, 0.0), axis=0, keepdims=True), red)
            out = jnp.zeros((N_BUCKETS, 128), F32)
            so = jnp.zeros((8, 128), F32)
            for h in range(GROUP):
                col = jnp.sum(red[:, h * BLOCK:(h + 1) * BLOCK], axis=1, keepdims=True)
                out = jnp.where(lane == h, col, out)
                sc = jnp.sum(dk_ref[g][:, h * BLOCK:(h + 1) * BLOCK], axis=1, keepdims=True)
                so = jnp.where(lane8 == h, sc, so)
            rel_ref[g] = out
            sink_ref[g] = so

    vm = pl.BlockSpec(memory_space=pltpu.VMEM)
    return pl.pallas_call(
        body,
        in_specs=[vm, vm, vm],
        out_specs=[vm, vm],
        out_shape=[SDS((kv, N_BUCKETS, 128), F32), SDS((kv, 8, 128), F32)],
        name="bias_reduce",
    )(dsum, dsink, bucket_t4)


def _dqkprep(dqa, dka, dva, dqb, dkb, dvb, proj, cos, sin_signed, gq, gk, s_len, ts):
    nd, t, pb = proj.shape
    bl, ha = dqa.shape[0], dqa.shape[1]
    kva, hb, kvb = dka.shape[1], dqb.shape[1], dkb.shape[1]
    hpb = pb // HEAD_DIM
    ns = s_len // ts

    def body(dqa_ref, dka_ref, dva_ref, dqb_ref, dkb_ref, dvb_ref, p_ref, cos_ref, sin_ref, gq_ref, gk_ref,
             dp_ref, dgq_ref, dgk_ref):
        b, i = pl.program_id(0), pl.program_id(1)
        cs, sn = cos_ref[...], sin_ref[...]
        lane = lax.broadcasted_iota(jnp.int32, (ts, HEAD_DIM), 1)
        first = (lane % 32) < 16

        @pl.when((b == 0) & (i == 0))
        def _():
            dgq_ref[...] = jnp.zeros_like(dgq_ref)
            dgk_ref[...] = jnp.zeros_like(dgk_ref)

        def put(hh, val):
            dp_ref[hh // hpb, :, pl.ds((hh % hpb) * HEAD_DIM, HEAD_DIM)] = val.astype(BF16)

        def unrope_norm(d_rot, hh, g, dg_ref):
            ds = d_rot * sn
            dn = d_rot * cs + jnp.where(first, jnp.roll(ds, -16, axis=1), jnp.roll(ds, 16, axis=1))
            _, n, r = _rms_fwd(_head_slice(p_ref, hh, hpb), g)
            dx, dgt = _rms_bwd(n, r, g, dn)
            dg_ref[0:1, :] += jnp.sum(dgt, axis=0, keepdims=True)
            put(hh, dx)

        for h in range(ha):
            unrope_norm(dqa_ref[0, h] * SCALE, h, gq_ref[...], dgq_ref)
        for h in range(kva):
            unrope_norm(dka_ref[0, h], ha + h, gk_ref[...], dgk_ref)
            put(ha + kva + h, dva_ref[0, h])
        base = ha + 2 * kva
        for h in range(hb):
            put(base + h, dqb_ref[0, h] * SCALE)
        for h in range(kvb):
            put(base + hb + h, dkb_ref[0, h])
            put(base + hb + kvb + h, dvb_ref[0, h])

    def hm(nh):
        return pl.BlockSpec((1, nh, ts, HEAD_DIM), lambda b, i: (b, 0, i, 0))

    vec = pl.BlockSpec((1, HEAD_DIM), lambda b, i: (0, 0))
    tab = pl.BlockSpec((ts, HEAD_DIM), lambda b, i: (i, 0))
    acc = pl.BlockSpec((8, HEAD_DIM), lambda b, i: (0, 0))
    pspec = pl.BlockSpec((nd, ts, pb), lambda b, i: (0, b * ns + i, 0))
    return pl.pallas_call(
        body,
        grid=(bl, ns),
        in_specs=[hm(ha), hm(kva), hm(kva), hm(hb), hm(kvb), hm(kvb), pspec, tab, tab, vec, vec],
        out_specs=[pspec, acc, acc],
        out_shape=[SDS((nd, t, pb), BF16), SDS((8, HEAD_DIM), F32), SDS((8, HEAD_DIM), F32)],
        compiler_params=_cp("arbitrary", "arbitrary"),
        name="dqkprep",
    )(dqa, dka, dva, dqb, dkb, dvb, proj, cos, sin_signed, gq, gk)


def _dx_final(dproj, win_g, x2, dx1, g1, tm):
    t, d = x2.shape
    nd, _, pb = win_g.shape

    def body(dp_ref, w_ref, x_ref, dx1_ref, g_ref, dx_ref, dg_ref):
        @pl.when(pl.program_id(0) == 0)
        def _():
            dg_ref[...] = jnp.zeros_like(dg_ref)

        dh = _dot_nt(dp_ref[0], w_ref[0])
        for j in range(1, nd):
            dh = dh + _dot_nt(dp_ref[j], w_ref[j])
        g = g_ref[...]
        _, n, r = _rms_fwd(x_ref[...], g)
        dx, dgt = _rms_bwd(n, r, g, dh)
        dx_ref[...] = dx1_ref[...] + dx
        dg_ref[0:1, :] += jnp.sum(dgt, axis=0, keepdims=True)

    tile = pl.BlockSpec((tm, d), lambda i: (i, 0))
    return pl.pallas_call(
        body,
        grid=(t // tm,),
        in_specs=[pl.BlockSpec((nd, tm, pb), lambda i: (0, i, 0)),
                  pl.BlockSpec((nd, d, pb), lambda i: (0, 0, 0)),
                  tile, tile, pl.BlockSpec((1, d), lambda i: (0, 0))],
        out_specs=[tile, pl.BlockSpec((8, d), lambda i: (0, 0))],
        out_shape=[SDS((t, d), F32), SDS((8, d), F32)],
        compiler_params=_cp("arbitrary"),
        name="dx_final",
    )(dproj, win_g, x2, dx1, g1)


def _adamw_math(w, g, m, v):
    m = ADAM_B1 * m + (1.0 - ADAM_B1) * g
    v = ADAM_B2 * v + (1.0 - ADAM_B2) * (g * g)
    m_hat = m / (1.0 - ADAM_B1 ** ADAM_STEP)
    v_hat = v / (1.0 - ADAM_B2 ** ADAM_STEP)
    delta = -ADAM_LR * (m_hat / (jnp.sqrt(v_hat) + ADAM_EPS) + ADAM_WD * w)
    return delta, m, v


def _adamw_sum(parts, w, m, v, tr, name):
    rows, cols = w.shape

    def body(p_ref, w_ref, m_ref, v_ref, g_ref, d_ref, nm_ref, nv_ref):
        g = p_ref[0]
        for s in range(1, N_DEV):
            g = g + p_ref[s]
        g_ref[...] = g
        d_ref[...], nm_ref[...], nv_ref[...] = _adamw_math(w_ref[...], g, m_ref[...], v_ref[...])

    tr = min(tr, rows)
    tile = pl.BlockSpec((tr, cols), lambda i: (i, 0))
    return pl.pallas_call(
        body,
        grid=(rows // tr,),
        in_specs=[pl.BlockSpec((N_DEV, tr, cols), lambda i: (0, i, 0)), tile, tile, tile],
        out_specs=[tile] * 4,
        out_shape=[SDS((rows, cols), F32)] * 4,
        compiler_params=_cp("parallel"),
        name=name,
    )(parts, w, m, v)


def _adamw_small(g, w, m, v):
    def body(g_ref, w_ref, m_ref, v_ref, d_ref, nm_ref, nv_ref):
        d_ref[...], nm_ref[...], nv_ref[...] = _adamw_math(w_ref[...], g_ref[...], m_ref[...], v_ref[...])

    vm = pl.BlockSpec(memory_space=pltpu.VMEM)
    return pl.pallas_call(
        body,
        in_specs=[vm] * 4,
        out_specs=[vm] * 3,
        out_shape=[SDS(w.shape, F32)] * 3,
        name="adamw_small",
    )(g, w, m, v)


SMALL_ROWS = 8
SMALL_COLS = 1024


def _pack_small(g1, g2, g3, g4, gq, gk, sink, rel):
    row4 = jnp.concatenate([gq.reshape(-1), gk.reshape(-1), sink.reshape(-1)])
    row4 = jnp.pad(row4, (0, SMALL_COLS - row4.shape[0]))
    row5 = jnp.pad(rel.reshape(-1), (0, SMALL_COLS - rel.size))
    zero = jnp.zeros((SMALL_COLS,), F32)
    return jnp.stack([g1.reshape(-1), g2.reshape(-1), g3.reshape(-1), g4.reshape(-1), row4, row5, zero, zero])


def _unpack_small(p, hb):
    hd = HEAD_DIM
    return (p[0:1], p[1:2], p[2:3], p[3:4], p[4:5, 0:hd], p[4:5, hd:2 * hd], p[4:5, 2 * hd:2 * hd + hb],
            p[5, 0:N_BUCKETS * hb].reshape(N_BUCKETS, hb))


def _local_step(x, loss_target, win_g, wo, wup_g, wdn, g_pre_mix, g_post_mix, q_norm_a, k_norm_a, sink_b,
                rel_bias, g_pre_ffn, g_post_ffn):
    bl, s_len, d = x.shape
    t = bl * s_len
    nh = d // HEAD_DIM
    ha = nh // 2
    kva = ha // GROUP
    hb = nh - ha
    kvb = hb // GROUP
    tm = 512
    ts = min(512, s_len)
    tq, tk = BLOCK, min(512, s_len)

    x2 = x.reshape(t, d)
    tg2 = loss_target.reshape(t, d)
    cos, sin_signed = _rope_tables(s_len)
    a = jnp.arange(BLOCK, dtype=jnp.int32)
    c = jnp.arange(SPAN, dtype=jnp.int32)
    bucket = _t5_bucket(c[None, :] - BLOCK - a[:, None])
    bucket_t = bucket.T
    bucket_t4 = jnp.tile(bucket_t, (1, GROUP))

    h1, proj = _inproj(x2, g_pre_mix, win_g, tm)
    qa, ka, va, qb, kb, vb = _qkprep(proj, cos, sin_signed, q_norm_a, k_norm_a, bl, s_len, ha, kva, hb, kvb, ts)
    bias, bias_t = _bias_build(bucket, bucket_t, rel_bias, hb)
    oa, lse_a = _attn_a_fwd(qa, ka, va, tq, tk)
    ob, lse_b = _attn_b_fwd(qb, kb, vb, bias, sink_b, s_len)
    mix, x1, h2 = _mixout(oa, ob, wo, x2, g_post_mix, g_pre_ffn, tm)
    u, df, dy, dg4, loss8 = _ffn_fwd(h2, wup_g, wdn, x1, tg2, g_post_ffn, tm)

    dpre, dx1, dmix, dg3, dg2 = _ffn_bwd(df, u, wdn, wup_g, x1, dy, mix, g_pre_ffn, g_post_mix, tm)
    gw_dn = _wgrad_rows([u], df, N_DEV, tm, True, "wgrad_down")
    gw_up = _wgrad_cols(h2, dpre, N_DEV, tm, "wgrad_up")
    gw_o = _wgrad_rows([oa, ob], dmix, N_DEV, tm, False, "wgrad_o")
    doa, dob = _attn_out_bwd(dmix, wo, oa.shape[1], tm)
    dqa, dka, dva = _attn_a_bwd(qa, ka, va, doa, oa, lse_a, tq, tk)
    dqb, dkb_p, dvb_p, dsum, dsink = _attn_b_bwd(qb, kb, vb, dob, ob, lse_b, bias_t, sink_b, s_len)
    drel_g, dsink_g = _bias_reduce(dsum, dsink, bucket_t4)
    dkb = dkb_p[:, :, BLOCK:BLOCK + s_len, :]
    dvb = dvb_p[:, :, BLOCK:BLOCK + s_len, :]
    dproj, dgq, dgk = _dqkprep(dqa, dka, dva, dqb, dkb, dvb, proj, cos, sin_signed, q_norm_a, k_norm_a, s_len, ts)
    gw_in = _wgrad_cols_blocked(h1, dproj, tm, "wgrad_in")
    grad_x, dg1 = _dx_final(dproj, win_g, x2, dx1, g_pre_mix, tm)

    drel = jnp.transpose(drel_g[:, :, 0:GROUP], (1, 0, 2)).reshape(N_BUCKETS, hb)
    dsink_v = dsink_g[:, 0, 0:GROUP].reshape(1, hb)
    small = _pack_small(dg1[0], dg2[0], dg3[0], dg4[0], dgq[0], dgk[0], dsink_v, drel)
    small = small.at[6, 0].set(loss8[0, 0])
    return grad_x.reshape(bl, s_len, d), gw_in, gw_o, gw_up, gw_dn, small


def kernel(x, w_in, w_o, g_pre_mix, g_post_mix, q_norm_a, k_norm_a, sink_b, rel_bias, g_pre_ffn, w_ffn_up, w_ffn_down, g_post_ffn, loss_target, m_w_in, m_w_o, m_g_pre_mix, m_g_post_mix, m_q_norm_a, m_k_norm_a, m_sink_b, m_rel_bias, m_g_pre_ffn, m_w_ffn_up, m_w_ffn_down, m_g_post_ffn, v_w_in, v_w_o, v_g_pre_mix, v_g_post_mix, v_q_norm_a, v_k_norm_a, v_sink_b, v_rel_bias, v_g_pre_ffn, v_w_ffn_up, v_w_ffn_down, v_g_post_ffn):
    hb = sink_b.shape[1]
    d = x.shape[-1]
    win_g, wo_g, wup_g, wdn_g = _weight_gather(
        [w_in[0].astype(BF16), w_o[0].astype(BF16), w_ffn_up[0].astype(BF16), w_ffn_down[0].astype(BF16)])
    wo = wo_g.reshape(-1, d)
    wdn = wdn_g.reshape(-1, d)

    grad_x, gw_in, gw_o, gw_up, gw_dn, small = _local_step(
        x, loss_target, win_g, wo, wup_g, wdn, g_pre_mix, g_post_mix, q_norm_a, k_norm_a, sink_b, rel_bias,
        g_pre_ffn, g_post_ffn)

    p_in, p_o, p_up, p_dn = _grad_exchange([gw_in, gw_o, gw_up, gw_dn])
    small = _small_allreduce(small)

    g_in, d_in, nm_in, nv_in = _adamw_sum(p_in, w_in[0], m_w_in[0], v_w_in[0], 256, "adamw_in")
    g_o, d_o, nm_o, nv_o = _adamw_sum(p_o, w_o[0], m_w_o[0], v_w_o[0], 128, "adamw_o")
    g_up, d_up, nm_up, nv_up = _adamw_sum(p_up, w_ffn_up[0], m_w_ffn_up[0], v_w_ffn_up[0], 256, "adamw_up")
    g_dn, d_dn, nm_dn, nv_dn = _adamw_sum(p_dn, w_ffn_down[0], m_w_ffn_down[0], v_w_ffn_down[0], 256, "adamw_down")

    pack = lambda *a: _pack_small(*a)
    w_s = pack(g_pre_mix, g_post_mix, g_pre_ffn, g_post_ffn, q_norm_a, k_norm_a, sink_b, rel_bias)
    m_s = pack(m_g_pre_mix, m_g_post_mix, m_g_pre_ffn, m_g_post_ffn, m_q_norm_a, m_k_norm_a, m_sink_b, m_rel_bias)
    v_s = pack(v_g_pre_mix, v_g_post_mix, v_g_pre_ffn, v_g_post_ffn, v_q_norm_a, v_k_norm_a, v_sink_b, v_rel_bias)
    d_s, nm_s, nv_s = _adamw_small(small, w_s, m_s, v_s)

    loss = small[6, 0]

    def outs(big_in, big_o, sm, big_up, big_dn):
        s1, s2, s3, s4, sq, sk, ss, sr = _unpack_small(sm, hb)
        return [big_in[None], big_o[None], s1, s2, sq, sk, ss, sr, s3, big_up[None], big_dn[None], s4]

    return (loss, grad_x,
            *outs(g_in, g_o, small, g_up, g_dn),
            *outs(d_in, d_o, d_s, d_up, d_dn),
            *outs(nm_in, nm_o, nm_s, nm_up, nm_dn),
            *outs(nv_in, nv_o, nv_s, nv_up, nv_dn))
```

```python
import functools

import jax
import jax.numpy as jnp
import numpy as np
from jax import lax
from jax.experimental import pallas as pl
from jax.experimental.pallas import tpu as pltpu

F32 = jnp.float32
BF16 = jnp.bfloat16
SDS = jax.ShapeDtypeStruct

N_DEV = 8
HEAD_DIM = 64
GROUP = 4
BLOCK = 128
SPAN = 3 * BLOCK
GRID_W = 64
N_BUCKETS = 32
MAX_DISTANCE = 128
ROPE_THETA = 10000.0
EPS = 1e-6
NEG_INF = -1e30
SCALE = HEAD_DIM ** -0.5
VT_PAD = 16

ADAM_LR = 0.001
ADAM_B1 = 0.9
ADAM_B2 = 0.999
ADAM_EPS = 1e-08
ADAM_WD = 0.01
ADAM_STEP = 10

VMEM_LIMIT = 56 * 1024 * 1024
MESH = pl.DeviceIdType.MESH


def _cp(*sem):
    return pltpu.CompilerParams(dimension_semantics=sem, vmem_limit_bytes=VMEM_LIMIT)


def _dot(a, b):
    return jnp.dot(a, b, preferred_element_type=F32)


def _dot_nt(a, b):
    return lax.dot_general(a, b, (((1,), (1,)), ((), ())), preferred_element_type=F32)


def _dot_tn(a, b):
    return lax.dot_general(a, b, (((0,), (0,)), ((), ())), preferred_element_type=F32)


def _rms_fwd(x, g):
    r = lax.rsqrt(jnp.mean(x * x, axis=-1, keepdims=True) + EPS)
    n = x * r
    return n * g, n, r


def _rms_bwd(n, r, g, dy):
    gd = g * dy
    dx = r * (gd - n * jnp.mean(n * gd, axis=-1, keepdims=True))
    return dx, dy * n


def _col_to_row(col, rows):
    r = col.shape[0]
    return jnp.broadcast_to(col, (r, 128)).T[0:rows, :]


def _rope(n, cos, sin_signed, first):
    partner = jnp.where(first, jnp.roll(n, -16, axis=1), jnp.roll(n, 16, axis=1))
    return n * cos + partner * sin_signed


def _rope_tables(s_len):
    rows = s_len // GRID_W
    row = jnp.repeat(jnp.arange(rows, dtype=jnp.int32), GRID_W)
    col = jnp.tile(jnp.arange(GRID_W, dtype=jnp.int32), rows)
    nf = HEAD_DIM // 4
    freqs = ROPE_THETA ** (-jnp.arange(nf, dtype=F32) / nf)
    ang_r = row.astype(F32)[:, None] * freqs[None, :]
    ang_c = col.astype(F32)[:, None] * freqs[None, :]
    cr, sr, cc, sc = jnp.cos(ang_r), jnp.sin(ang_r), jnp.cos(ang_c), jnp.sin(ang_c)
    cos = jnp.concatenate([cr, cr, cc, cc], axis=-1)
    sin_signed = jnp.concatenate([-sr, sr, -sc, sc], axis=-1)
    return cos, sin_signed


def _t5_bucket(rel):
    nb = N_BUCKETS // 2
    ret = (rel > 0).astype(jnp.int32) * nb
    n = jnp.abs(rel)
    max_exact = nb // 2
    nf = jnp.maximum(n, 1).astype(F32)
    large = max_exact + (jnp.log(nf / max_exact) / np.float32(np.log(MAX_DISTANCE / max_exact))
                         * (nb - max_exact)).astype(jnp.int32)
    large = jnp.minimum(large, nb - 1)
    return ret + jnp.where(n < max_exact, n, large)


def _mesh_pos():
    return lax.axis_index("x"), lax.axis_index("y"), lax.axis_index("c")


def _lin(p):
    return 4 * p[0] + 2 * p[1] + p[2]


def _weight_gather(shards):
    n = len(shards)

    def body(*refs):
        xs, outs = refs[:n], refs[n:2 * n]
        send_sems, recv_sems, local_sems = refs[2 * n:]
        x, y, c = _mesh_pos()
        me, sibling = (x, y, c), (x, y, 1 - c)
        chips = [(1 - x, y), (x, 1 - y), (1 - x, 1 - y)]

        def copy(a, k, block, to, src=None):
            slot = outs[a].at[_lin(block)]
            return pltpu.make_async_remote_copy(
                src_ref=slot if src is None else src, dst_ref=slot,
                send_sem=send_sems.at[a, k], recv_sem=recv_sems.at[a, k],
                device_id=to, device_id_type=MESH)

        started = []
        for a in range(n):
            mine = pltpu.make_async_copy(xs[a], outs[a].at[_lin(me)], local_sems.at[a])
            mine.start()
            started.append(mine)
        sends = []
        for a in range(n):
            first = [copy(a, 0, me, sibling, src=xs[a])]
            first += [copy(a, 1 + j, me, (*chip, c), src=xs[a]) for j, chip in enumerate(chips)]
            for cp in first:
                cp.start()
            sends += first
        for a in range(n):
            for j, chip in enumerate(chips):
                copy(a, 1 + j, (*chip, c), me).wait_recv()
                fwd = copy(a, 4 + j, (*chip, c), sibling)
                fwd.start()
                sends.append(fwd)
        for a in range(n):
            copy(a, 0, sibling, me).wait_recv()
            for j, chip in enumerate(chips):
                copy(a, 4 + j, (*chip, 1 - c), me).wait_recv()
        for cp in sends:
            cp.wait_send()
        for mine in started:
            mine.wait()

    anyspec = pl.BlockSpec(memory_space=pl.ANY)
    return pl.pallas_call(
        body,
        out_shape=[SDS((N_DEV,) + s.shape, s.dtype) for s in shards],
        in_specs=[anyspec] * n,
        out_specs=[anyspec] * n,
        scratch_shapes=[pltpu.SemaphoreType.DMA((n, 7)), pltpu.SemaphoreType.DMA((n, 7)),
                        pltpu.SemaphoreType.DMA((n,))],
        name="weight_gather",
    )(*shards)


def _direct_exchange(kind, ins, outs, send_sems, recv_sems, local_sems):
    x, y, c = _mesh_pos()
    me = (x, y, c)
    peers = [(x, y, 1 - c), (1 - x, y, c), (x, 1 - y, c), (1 - x, 1 - y, c),
             (1 - x, y, 1 - c), (x, 1 - y, 1 - c), (1 - x, 1 - y, 1 - c)]

    def src(a, to):
        return ins[a] if kind == "gather" else ins[a].at[_lin(to)]

    def remote(a, k, to, frm):
        return pltpu.make_async_remote_copy(
            src_ref=src(a, to), dst_ref=outs[a].at[_lin(frm)],
            send_sem=send_sems.at[a, k], recv_sem=recv_sems.at[a, k],
            device_id=to, device_id_type=MESH)

    n = len(ins)
    sends = [remote(a, k, p, me) for a in range(n) for k, p in enumerate(peers)]
    arrivals = [remote(a, k, p, p) for a in range(n) for k, p in enumerate(peers)]
    local = [pltpu.make_async_copy(src(a, me), outs[a].at[_lin(me)], local_sems.at[a]) for a in range(n)]

    def start():
        for cp in local + sends:
            cp.start()

    def wait():
        for cp in arrivals:
            cp.wait_recv()
        for cp in sends:
            cp.wait_send()
        for cp in local:
            cp.wait()

    return start, wait


def _exchange_scratch(n):
    return [pltpu.SemaphoreType.DMA((n, 7)), pltpu.SemaphoreType.DMA((n, 7)), pltpu.SemaphoreType.DMA((n,))]


def _grad_exchange(grads):
    n = len(grads)

    def body(*refs):
        start, wait = _direct_exchange("scatter", refs[:n], refs[n:2 * n], *refs[2 * n:])
        start()
        wait()

    anyspec = pl.BlockSpec(memory_space=pl.ANY)
    return pl.pallas_call(
        body,
        out_shape=[SDS(g.shape, g.dtype) for g in grads],
        in_specs=[anyspec] * n,
        out_specs=[anyspec] * n,
        scratch_shapes=_exchange_scratch(n),
        name="grad_exchange",
    )(*grads)


def _small_allreduce(v):
    rows, cols = v.shape

    def body(v_ref, out_ref, land_ref, send_sems, recv_sems):
        x, y, c = _mesh_pos()
        me = (x, y, c)
        peers = [(x, y, 1 - c), (1 - x, y, c), (x, 1 - y, c), (1 - x, 1 - y, c),
                 (1 - x, y, 1 - c), (x, 1 - y, 1 - c), (1 - x, 1 - y, 1 - c)]

        def copy(k, to, frm):
            return pltpu.make_async_remote_copy(
                src_ref=v_ref, dst_ref=land_ref.at[_lin(frm)],
                send_sem=send_sems.at[k], recv_sem=recv_sems.at[k],
                device_id=to, device_id_type=MESH)

        sends = [copy(k, p, me) for k, p in enumerate(peers)]
        for cp in sends:
            cp.start()
        land_ref[_lin(me)] = v_ref[...]
        for k, p in enumerate(peers):
            copy(k, p, p).wait_recv()
        for cp in sends:
            cp.wait_send()
        acc = land_ref[0]
        for s in range(1, N_DEV):
            acc = acc + land_ref[s]
        out_ref[...] = acc

    vm = pl.BlockSpec(memory_space=pltpu.VMEM)
    return pl.pallas_call(
        body,
        out_shape=SDS((rows, cols), F32),
        in_specs=[vm],
        out_specs=vm,
        scratch_shapes=[pltpu.VMEM((N_DEV, rows, cols), F32),
                        pltpu.SemaphoreType.DMA((7,)), pltpu.SemaphoreType.DMA((7,))],
        name="small_allreduce",
    )(v)


def _inproj(x2, g1, win_g, tm):
    t, d = x2.shape
    nd, _, pb = win_g.shape

    def body(x_ref, g_ref, w_ref, h_ref, p_ref):
        y, _, _ = _rms_fwd(x_ref[...], g_ref[...])
        h = y.astype(BF16)
        h_ref[...] = h
        for j in range(nd):
            p_ref[j] = _dot(h, w_ref[j])

    return pl.pallas_call(
        body,
        grid=(t // tm,),
        in_specs=[pl.BlockSpec((tm, d), lambda i: (i, 0)),
                  pl.BlockSpec((1, d), lambda i: (0, 0)),
                  pl.BlockSpec((nd, d, pb), lambda i: (0, 0, 0))],
        out_specs=[pl.BlockSpec((tm, d), lambda i: (i, 0)),
                   pl.BlockSpec((nd, tm, pb), lambda i: (0, i, 0))],
        out_shape=[SDS((t, d), BF16), SDS((nd, t, pb), F32)],
        compiler_params=_cp("parallel"),
        name="inproj",
    )(x2, g1, win_g)


def _head_slice(p_ref, hh, hpb):
    return p_ref[hh // hpb, :, pl.ds((hh % hpb) * HEAD_DIM, HEAD_DIM)]


def _qkprep(proj, cos, sin_signed, gq, gk, bl, s_len, ha, kva, hb, kvb, ts):
    nd, t, pb = proj.shape
    hpb = pb // HEAD_DIM
    ns = s_len // ts
    sp = s_len + 2 * BLOCK

    def body(p_ref, cos_ref, sin_ref, gq_ref, gk_ref, qa_ref, ka_ref, kat_ref, va_ref, vat_ref, qb_ref, kb_ref,
             vb_ref):
        i = pl.program_id(1)
        cs, sn = cos_ref[...], sin_ref[...]
        lane = lax.broadcasted_iota(jnp.int32, (ts, HEAD_DIM), 1)
        first = (lane % 32) < 16
        ones_row = (lax.broadcasted_iota(jnp.int32, (VT_PAD, ts), 0) == 0).astype(BF16)

        def normrope(xh, g):
            y, _, _ = _rms_fwd(xh, g)
            return _rope(y, cs, sn, first)

        for h in range(ha):
            qa_ref[0, h] = (normrope(_head_slice(p_ref, h, hpb), gq_ref[...]) * SCALE).astype(BF16)
        for h in range(kva):
            kh = normrope(_head_slice(p_ref, ha + h, hpb), gk_ref[...])
            ka_ref[0, h] = kh.astype(BF16)
            kat_ref[0, h] = kh.T.astype(BF16)
            vh = _head_slice(p_ref, ha + kva + h, hpb)
            va_ref[0, h] = vh.astype(BF16)
            vat_ref[0, h, 0:HEAD_DIM, :] = vh.T.astype(BF16)
            vat_ref[0, h, HEAD_DIM:HEAD_DIM + VT_PAD, :] = ones_row
        base = ha + 2 * kva
        for h in range(hb):
            qb_ref[0, h] = (_head_slice(p_ref, base + h, hpb) * SCALE).astype(BF16)

        @pl.when(i == 0)
        def _():
            zeros = jnp.zeros((kvb, BLOCK, HEAD_DIM), BF16)
            kb_ref[0, :, 0:BLOCK, :] = zeros
            kb_ref[0, :, sp - BLOCK:sp, :] = zeros
            vb_ref[0, :, 0:BLOCK, :] = zeros
            vb_ref[0, :, sp - BLOCK:sp, :] = zeros

        row0 = pl.multiple_of(BLOCK + i * ts, BLOCK)
        for h in range(kvb):
            kb_ref[0, h, pl.ds(row0, ts), :] = _head_slice(p_ref, base + hb + h, hpb).astype(BF16)
            vb_ref[0, h, pl.ds(row0, ts), :] = _head_slice(p_ref, base + hb + kvb + h, hpb).astype(BF16)

    def hm(nh):
        return pl.BlockSpec((1, nh, ts, HEAD_DIM), lambda b, i: (b, 0, i, 0))

    def padded(nh):
        return pl.BlockSpec((1, nh, sp, HEAD_DIM), lambda b, i: (b, 0, 0, 0))

    return pl.pallas_call(
        body,
        grid=(bl, ns),
        in_specs=[pl.BlockSpec((nd, ts, pb), lambda b, i: (0, b * ns + i, 0)),
                  pl.BlockSpec((ts, HEAD_DIM), lambda b, i: (i, 0)),
                  pl.BlockSpec((ts, HEAD_DIM), lambda b, i: (i, 0)),
                  pl.BlockSpec((1, HEAD_DIM), lambda b, i: (0, 0)),
                  pl.BlockSpec((1, HEAD_DIM), lambda b, i: (0, 0))],
        out_specs=[hm(ha), hm(kva), pl.BlockSpec((1, kva, HEAD_DIM, ts), lambda b, i: (b, 0, 0, i)), hm(kva),
                   pl.BlockSpec((1, kva, HEAD_DIM + VT_PAD, ts), lambda b, i: (b, 0, 0, i)),
                   hm(hb), padded(kvb), padded(kvb)],
        out_shape=[SDS((bl, ha, s_len, HEAD_DIM), BF16), SDS((bl, kva, s_len, HEAD_DIM), BF16),
                   SDS((bl, kva, HEAD_DIM, s_len), BF16),
                   SDS((bl, kva, s_len, HEAD_DIM), BF16), SDS((bl, kva, HEAD_DIM + VT_PAD, s_len), BF16),
                   SDS((bl, hb, s_len, HEAD_DIM), BF16),
                   SDS((bl, kvb, sp, HEAD_DIM), BF16), SDS((bl, kvb, sp, HEAD_DIM), BF16)],
        compiler_params=_cp("parallel", "arbitrary"),
        name="qkprep",
    )(proj, cos, sin_signed, gq, gk)


def _bias_build(bucket, bucket_t, rel_bias, hb):
    kvb = hb // GROUP

    def body(bk_ref, bkt_ref, tbl_ref, bias_ref, biast_ref):
        bk, bkt = bk_ref[...], bkt_ref[...]
        for h in range(hb):
            acc = jnp.zeros((BLOCK, SPAN), F32)
            acct = jnp.zeros((SPAN, BLOCK), F32)
            for b in range(N_BUCKETS):
                val = tbl_ref[b, h]
                acc = jnp.where(bk == b, val, acc)
                acct = jnp.where(bkt == b, val, acct)
            bias_ref[h] = acc
            biast_ref[h // GROUP, :, (h % GROUP) * BLOCK:(h % GROUP + 1) * BLOCK] = acct

    vm = pl.BlockSpec(memory_space=pltpu.VMEM)
    return pl.pallas_call(
        body,
        in_specs=[vm, vm, pl.BlockSpec(memory_space=pltpu.SMEM)],
        out_specs=[vm, vm],
        out_shape=[SDS((hb, BLOCK, SPAN), F32), SDS((kvb, SPAN, GROUP * BLOCK), F32)],
        name="bias_build",
    )(bucket, bucket_t, rel_bias)


def _attn_a_fwd(qa, ka, vat, tq, tk, shards):
    bl, ha, s_len, _ = qa.shape
    kv = ka.shape[1]
    va_rows = vat.shape[2]
    nq, nk = s_len // tq, s_len // tk
    assert nk % 2 == 0
    r = GROUP * tq
    ns = len(shards)

    def body(q_ref, k_ref, v_ref, *rest):
        shard_refs, (o_ref, l_ref), gathered = rest[:ns], rest[ns:ns + 2], rest[ns + 2:2 * ns + 2]
        st_sc, send_sems, recv_sems, local_sems = rest[2 * ns + 2:]
        step_id = (pl.program_id(0) * kv + pl.program_id(1)) * nq + pl.program_id(2)
        start, wait = _direct_exchange("gather", shard_refs, gathered, send_sems, recv_sems, local_sems)
        pl.when(step_id == 0)(start)

        q = q_ref[0].reshape(r, HEAD_DIM)

        def scores(c):
            return _dot_nt(k_ref[0, 0, pl.ds(pl.multiple_of(c * tk, tk), tk), :], q)

        def fold(st, c, carry):
            m_old, acc = carry
            m_new = jnp.maximum(m_old, jnp.max(st, axis=0, keepdims=True))
            pt = jnp.exp(st - m_new).astype(BF16)
            vt = v_ref[0, 0, :, pl.ds(pl.multiple_of(c * tk, tk), tk)]
            return m_new, jnp.exp(m_old - m_new) * acc + _dot(vt, pt)

        st_sc[0] = scores(0)

        def step(c2, carry):
            c = 2 * c2
            st_sc[1] = scores(c + 1)
            carry = fold(st_sc[0], c, carry)
            st_sc[0] = scores(jnp.minimum(c + 2, nk - 1))
            return fold(st_sc[1], c + 1, carry)

        m, acc = lax.fori_loop(0, nk // 2, step,
                               (jnp.full((1, r), -jnp.inf, F32), jnp.zeros((va_rows, r), F32)))
        l = acc[HEAD_DIM:HEAD_DIM + 1, :]
        o = (acc[0:HEAD_DIM, :] / l).T
        for h in range(GROUP):
            o_ref[:, h * HEAD_DIM:(h + 1) * HEAD_DIM] = o[h * tq:(h + 1) * tq].astype(BF16)
        l_ref[0, 0, 0] = jnp.broadcast_to(m + jnp.log(l), (8, r))
        pl.when(step_id == bl * kv * nq - 1)(wait)

    anyspec = pl.BlockSpec(memory_space=pl.ANY)
    res = pl.pallas_call(
        body,
        grid=(bl, kv, nq),
        in_specs=[pl.BlockSpec((1, GROUP, tq, HEAD_DIM), lambda b, g, i: (b, g, i, 0)),
                  pl.BlockSpec((1, 1, s_len, HEAD_DIM), lambda b, g, i: (b, g, 0, 0)),
                  pl.BlockSpec((1, 1, va_rows, s_len), lambda b, g, i: (b, g, 0, 0))] + [anyspec] * ns,
        out_specs=[pl.BlockSpec((tq, GROUP * HEAD_DIM), lambda b, g, i: (b * nq + i, g)),
                   pl.BlockSpec((1, 1, 1, 8, r), lambda b, g, i: (b, g, i, 0, 0))] + [anyspec] * ns,
        out_shape=[SDS((bl * s_len, ha * HEAD_DIM), BF16), SDS((bl, kv, nq, 8, r), F32)]
        + [SDS((N_DEV,) + s.shape, s.dtype) for s in shards],
        scratch_shapes=[pltpu.VMEM((2, tk, r), F32)] + _exchange_scratch(ns),
        compiler_params=_cp("arbitrary", "arbitrary", "arbitrary"),
        name="attn_a_fwd",
    )(qa, ka, vat, *shards)
    return res[0], res[1], res[2:]


def _band_mask(shape, q_axis, n, s_len):
    k_axis = 1 - q_axis
    qi = lax.broadcasted_iota(jnp.int32, shape, q_axis) % BLOCK
    ci = lax.broadcasted_iota(jnp.int32, shape, k_axis)
    rel = ci - BLOCK - qi
    kpos = n * BLOCK - BLOCK + ci
    return (jnp.abs(rel) <= BLOCK) & (kpos >= 0) & (kpos < s_len)


def _attn_b_fwd(qb, kb, vb, bias, sink, s_len):
    bl, hb, _, _ = qb.shape
    kv = kb.shape[1]
    sp = kb.shape[2]
    nb = s_len // BLOCK
    r = GROUP * BLOCK

    def body(q_ref, k_ref, v_ref, bias_ref, sink_ref, o_ref, l_ref):
        g, n = pl.program_id(1), pl.program_id(2)
        q = q_ref[0].reshape(r, HEAD_DIM)
        rows = pl.ds(pl.multiple_of(n * BLOCK, BLOCK), SPAN)
        s = _dot_nt(q, k_ref[0, 0, rows, :]) + bias_ref[...].reshape(r, SPAN)
        s = jnp.where(_band_mask((r, SPAN), 0, n, s_len), s, NEG_INF)
        sinkc = jnp.concatenate([jnp.full((BLOCK, 1), sink_ref[0, g * GROUP + h], F32) for h in range(GROUP)], axis=0)
        m = jnp.maximum(jnp.max(s, axis=1, keepdims=True), sinkc)
        p = jnp.exp(s - m)
        denom = jnp.sum(p, axis=1, keepdims=True) + jnp.exp(sinkc - m)
        w = (p / denom).astype(BF16)
        o = _dot(w, v_ref[0, 0, rows, :])
        for h in range(GROUP):
            o_ref[:, h * HEAD_DIM:(h + 1) * HEAD_DIM] = o[h * BLOCK:(h + 1) * BLOCK].astype(BF16)
        l_ref[0, 0, 0] = _col_to_row(m + jnp.log(denom), 8)

    return pl.pallas_call(
        body,
        grid=(bl, kv, nb),
        in_specs=[pl.BlockSpec((1, GROUP, BLOCK, HEAD_DIM), lambda b, g, n: (b, g, n, 0)),
                  pl.BlockSpec((1, 1, sp, HEAD_DIM), lambda b, g, n: (b, g, 0, 0)),
                  pl.BlockSpec((1, 1, sp, HEAD_DIM), lambda b, g, n: (b, g, 0, 0)),
                  pl.BlockSpec((GROUP, BLOCK, SPAN), lambda b, g, n: (g, 0, 0)),
                  pl.BlockSpec(memory_space=pltpu.SMEM)],
        out_specs=[pl.BlockSpec((BLOCK, GROUP * HEAD_DIM), lambda b, g, n: (b * nb + n, g)),
                   pl.BlockSpec((1, 1, 1, 8, r), lambda b, g, n: (b, g, n, 0, 0))],
        out_shape=[SDS((bl * s_len, hb * HEAD_DIM), BF16), SDS((bl, kv, nb, 8, r), F32)],
        compiler_params=_cp("parallel", "parallel", "arbitrary"),
        name="attn_b_fwd",
    )(qb, kb, vb, bias, sink)


def _mixout(oa, ob, wo, x2, g2, g3, tm):
    t, d = x2.shape
    ca = oa.shape[1]

    def body(oa_ref, ob_ref, w_ref, x_ref, g2_ref, g3_ref, mix_ref, x1_ref, h2_ref):
        mix = _dot(oa_ref[...], w_ref[0:ca, :]) + _dot(ob_ref[...], w_ref[ca:, :])
        mix_ref[...] = mix
        y2, _, _ = _rms_fwd(mix, g2_ref[...])
        x1 = x_ref[...] + y2
        x1_ref[...] = x1
        y3, _, _ = _rms_fwd(x1, g3_ref[...])
        h2_ref[...] = y3.astype(BF16)

    tile = lambda w: pl.BlockSpec((tm, w), lambda i: (i, 0))
    vec = pl.BlockSpec((1, d), lambda i: (0, 0))
    return pl.pallas_call(
        body,
        grid=(t // tm,),
        in_specs=[tile(ca), tile(ob.shape[1]), pl.BlockSpec(wo.shape, lambda i: (0, 0)), tile(d), vec, vec],
        out_specs=[tile(d), tile(d), tile(d)],
        out_shape=[SDS((t, d), F32), SDS((t, d), F32), SDS((t, d), BF16)],
        compiler_params=_cp("parallel"),
        name="mixout",
    )(oa, ob, wo, x2, g2, g3)


def _ffn_fwd(h2, wup_g, wdn, x1, target, g4, tm):
    t, d = x1.shape
    nj, _, tf = wup_g.shape
    ff = nj * tf
    nt = t // tm

    def body(h_ref, wu_ref, wd_ref, x1_ref, tg_ref, g_ref, u_ref, df_ref, dy_ref, dg_ref, loss_ref, acc_sc):
        i, j = pl.program_id(0), pl.program_id(1)

        @pl.when(j == 0)
        def _():
            acc_sc[...] = jnp.zeros_like(acc_sc)

        @pl.when((i == 0) & (j == 0))
        def _():
            dg_ref[...] = jnp.zeros_like(dg_ref)
            loss_ref[...] = jnp.zeros_like(loss_ref)

        u = jnp.maximum(_dot(h_ref[...], wu_ref[0]), 0.0)
        u_ref[...] = u.astype(BF16)
        acc_sc[...] += _dot((u * u).astype(BF16), wd_ref[...])

        @pl.when(j == nj - 1)
        def _():
            g = g_ref[...]
            y4, n, r = _rms_fwd(acc_sc[...], g)
            e = (x1_ref[...] + y4) - tg_ref[...]
            loss_ref[...] += jnp.sum(e * e) * (0.5 / d)
            dy = e * (1.0 / d)
            dy_ref[...] = dy
            df, dgt = _rms_bwd(n, r, g, dy)
            df_ref[...] = df.astype(BF16)
            dg_ref[0:1, :] += jnp.sum(dgt, axis=0, keepdims=True)

    tile = pl.BlockSpec((tm, d), lambda i, j: (i, 0))
    return pl.pallas_call(
        body,
        grid=(nt, nj),
        in_specs=[tile,
                  pl.BlockSpec((1, d, tf), lambda i, j: (j, 0, 0)),
                  pl.BlockSpec((tf, d), lambda i, j: (j, 0)),
                  tile, tile,
                  pl.BlockSpec((1, d), lambda i, j: (0, 0))],
        out_specs=[pl.BlockSpec((tm, tf), lambda i, j: (i, j)), tile, tile,
                   pl.BlockSpec((8, d), lambda i, j: (0, 0)),
                   pl.BlockSpec((8, 128), lambda i, j: (0, 0))],
        out_shape=[SDS((t, ff), BF16), SDS((t, d), BF16), SDS((t, d), F32), SDS((8, d), F32), SDS((8, 128), F32)],
        scratch_shapes=[pltpu.VMEM((tm, d), F32)],
        compiler_params=_cp("arbitrary", "arbitrary"),
        name="ffn_fwd",
    )(h2, wup_g, wdn, x1, target, g4)


def _ffn_bwd(df, u, wdn, wup_g, x1, dy, mix, g3, g2, tm):
    t, d = x1.shape
    nj, _, tf = wup_g.shape
    nt = t // tm

    def body(df_ref, u_ref, wd_ref, wu_ref, x1_ref, dy_ref, mix_ref, g3_ref, g2_ref,
             dpre_ref, dx1_ref, dmix_ref, dg3_ref, dg2_ref, acc_sc):
        i, j = pl.program_id(0), pl.program_id(1)

        @pl.when(j == 0)
        def _():
            acc_sc[...] = jnp.zeros_like(acc_sc)

        @pl.when((i == 0) & (j == 0))
        def _():
            dg3_ref[...] = jnp.zeros_like(dg3_ref)
            dg2_ref[...] = jnp.zeros_like(dg2_ref)

        du2 = _dot_nt(df_ref[...], wd_ref[...])
        dpre = (2.0 * u_ref[...].astype(F32) * du2).astype(BF16)
        dpre_ref[...] = dpre
        acc_sc[...] += _dot_nt(dpre, wu_ref[0])

        @pl.when(j == nj - 1)
        def _():
            g3, g2 = g3_ref[...], g2_ref[...]
            _, n3, r3 = _rms_fwd(x1_ref[...], g3)
            dx, dgt3 = _rms_bwd(n3, r3, g3, acc_sc[...])
            dx1 = dy_ref[...] + dx
            dx1_ref[...] = dx1
            dg3_ref[0:1, :] += jnp.sum(dgt3, axis=0, keepdims=True)
            _, n2, r2 = _rms_fwd(mix_ref[...], g2)
            dmix, dgt2 = _rms_bwd(n2, r2, g2, dx1)
            dmix_ref[...] = dmix.astype(BF16)
            dg2_ref[0:1, :] += jnp.sum(dgt2, axis=0, keepdims=True)

    tile = pl.BlockSpec((tm, d), lambda i, j: (i, 0))
    vec = pl.BlockSpec((1, d), lambda i, j: (0, 0))
    acc8 = pl.BlockSpec((8, d), lambda i, j: (0, 0))
    return pl.pallas_call(
        body,
        grid=(nt, nj),
        in_specs=[tile,
                  pl.BlockSpec((tm, tf), lambda i, j: (i, j)),
                  pl.BlockSpec((tf, d), lambda i, j: (j, 0)),
                  pl.BlockSpec((1, d, tf), lambda i, j: (j, 0, 0)),
                  tile, tile, tile, vec, vec],
        out_specs=[pl.BlockSpec((tm, tf), lambda i, j: (i, j)), tile, tile, acc8, acc8],
        out_shape=[SDS(u.shape, BF16), SDS((t, d), F32), SDS((t, d), BF16), SDS((8, d), F32), SDS((8, d), F32)],
        scratch_shapes=[pltpu.VMEM((tm, d), F32)],
        compiler_params=_cp("arbitrary", "arbitrary"),
        name="ffn_bwd",
    )(df, u, wdn, wup_g, x1, dy, mix, g3, g2)


def _wgrad_cols(a, b, nj, tt, name):
    t, m = a.shape
    n = b.shape[1]
    bn = n // nj

    def body(a_ref, b_ref, o_ref):
        @pl.when(pl.program_id(1) == 0)
        def _():
            o_ref[...] = jnp.zeros_like(o_ref)

        o_ref[0] += _dot_tn(a_ref[...], b_ref[...])

    return pl.pallas_call(
        body,
        grid=(nj, t // tt),
        in_specs=[pl.BlockSpec((tt, m), lambda j, k: (k, 0)),
                  pl.BlockSpec((tt, bn), lambda j, k: (k, j))],
        out_specs=pl.BlockSpec((1, m, bn), lambda j, k: (j, 0, 0)),
        out_shape=SDS((nj, m, bn), F32),
        compiler_params=_cp("parallel", "arbitrary"),
        name=name,
    )(a, b)


def _wgrad_cols_blocked(a, b3, tt, name):
    t, m = a.shape
    nj, _, bn = b3.shape

    def body(a_ref, b_ref, o_ref):
        @pl.when(pl.program_id(1) == 0)
        def _():
            o_ref[...] = jnp.zeros_like(o_ref)

        o_ref[0] += _dot_tn(a_ref[...], b_ref[0])

    return pl.pallas_call(
        body,
        grid=(nj, t // tt),
        in_specs=[pl.BlockSpec((tt, m), lambda j, k: (k, 0)),
                  pl.BlockSpec((1, tt, bn), lambda j, k: (j, k, 0))],
        out_specs=pl.BlockSpec((1, m, bn), lambda j, k: (j, 0, 0)),
        out_shape=SDS((nj, m, bn), F32),
        compiler_params=_cp("parallel", "arbitrary"),
        name=name,
    )(a, b3)


def _wgrad_rows(a_parts, b, nj, tt, square, name):
    t, n = b.shape
    widths = [p.shape[1] for p in a_parts]
    m = sum(widths)
    bm = m // nj
    per = [w // bm for w in widths]
    starts = [sum(per[:q]) for q in range(len(per))]
    np_ = len(a_parts)

    def body(*refs):
        a_refs, b_ref, o_ref = refs[:np_], refs[np_], refs[np_ + 1]
        j = pl.program_id(0)

        @pl.when(pl.program_id(1) == 0)
        def _():
            o_ref[...] = jnp.zeros_like(o_ref)

        for q in range(np_):
            @pl.when((j >= starts[q]) & (j < starts[q] + per[q]))
            def _(q=q):
                a = a_refs[q][...]
                if square:
                    af = a.astype(F32)
                    a = (af * af).astype(BF16)
                o_ref[0] += _dot_tn(a, b_ref[...])

    def a_spec(q):
        return pl.BlockSpec((tt, bm), lambda j, k: (k, jnp.clip(j - starts[q], 0, per[q] - 1)))

    return pl.pallas_call(
        body,
        grid=(nj, t // tt),
        in_specs=[a_spec(q) for q in range(np_)] + [pl.BlockSpec((tt, n), lambda j, k: (k, 0))],
        out_specs=pl.BlockSpec((1, bm, n), lambda j, k: (j, 0, 0)),
        out_shape=SDS((nj, bm, n), F32),
        compiler_params=_cp("parallel", "arbitrary"),
        name=name,
    )(*a_parts, b)


def _attn_out_bwd(dmix, wo, ca, tm):
    t, d = dmix.shape
    cb = wo.shape[0] - ca

    def body(dm_ref, w_ref, da_ref, db_ref):
        dm = dm_ref[...]
        da_ref[...] = _dot_nt(dm, w_ref[0:ca, :]).astype(BF16)
        db_ref[...] = _dot_nt(dm, w_ref[ca:, :]).astype(BF16)

    return pl.pallas_call(
        body,
        grid=(t // tm,),
        in_specs=[pl.BlockSpec((tm, d), lambda i: (i, 0)), pl.BlockSpec(wo.shape, lambda i: (0, 0))],
        out_specs=[pl.BlockSpec((tm, ca), lambda i: (i, 0)), pl.BlockSpec((tm, cb), lambda i: (i, 0))],
        out_shape=[SDS((t, ca), BF16), SDS((t, cb), BF16)],
        compiler_params=_cp("parallel"),
        name="attn_out_bwd",
    )(dmix, wo)


def _stack_heads(ref, rows):
    return jnp.concatenate([ref[:, h * HEAD_DIM:(h + 1) * HEAD_DIM] for h in range(GROUP)], axis=0)


def _attn_a_bwd(qa, ka, kat, va, do, o, lse, tq, tk, grads):
    bl, ha, s_len, _ = qa.shape
    kv = ka.shape[1]
    nq, nk = s_len // tq, s_len // tk
    assert nk % 2 == 0
    r = GROUP * tq
    ng = len(grads)

    def body(q_ref, k_ref, kt_ref, v_ref, do_ref, o_ref, l_ref, *rest):
        grad_refs, (dq_ref, dk_ref, dv_ref), parts = rest[:ng], rest[ng:ng + 3], rest[ng + 3:2 * ng + 3]
        st_sc, dp_sc, dkt_sc, dvt_sc, send_sems, recv_sems, local_sems = rest[2 * ng + 3:]
        i = pl.program_id(2)
        step_id = (pl.program_id(0) * kv + pl.program_id(1)) * nq + i
        start, wait = _direct_exchange("scatter", grad_refs, parts, send_sems, recv_sems, local_sems)
        pl.when(step_id == 0)(start)

        q = q_ref[0].reshape(r, HEAD_DIM)
        do2 = _stack_heads(do_ref, tq)
        qt = q.astype(F32).T
        dot32 = do2.astype(F32).T
        ot32 = _stack_heads(o_ref, tq).astype(F32).T
        drow = jnp.sum(dot32 * ot32, axis=0, keepdims=True)
        qt, dot = qt.astype(BF16), dot32.astype(BF16)
        lrow = l_ref[0, 0, 0, 0:1, :]

        @pl.when(i == 0)
        def _():
            dkt_sc[...] = jnp.zeros_like(dkt_sc)
            dvt_sc[...] = jnp.zeros_like(dvt_sc)

        def chunk(c):
            return pl.ds(pl.multiple_of(c * tk, tk), tk)

        def scores(c, slot):
            st_sc[slot] = _dot_nt(k_ref[0, 0, chunk(c), :], q)
            dp_sc[slot] = _dot_nt(v_ref[0, 0, chunk(c), :], do2)

        def fold(slot, c, dqt):
            pt = jnp.exp(st_sc[slot] - lrow)
            dsb = (pt * (dp_sc[slot] - drow)).astype(BF16)
            dvt_sc[:, chunk(c)] += _dot_nt(dot, pt.astype(BF16))
            dkt_sc[:, chunk(c)] += _dot_nt(qt, dsb)
            return dqt + _dot(kt_ref[0, 0, :, chunk(c)], dsb)

        scores(0, 0)

        def step(c2, dqt):
            c = 2 * c2
            scores(c + 1, 1)
            dqt = fold(0, c, dqt)
            scores(jnp.minimum(c + 2, nk - 1), 0)
            return fold(1, c + 1, dqt)

        dqt = lax.fori_loop(0, nk // 2, step, jnp.zeros((HEAD_DIM, r), F32))
        dq_ref[0] = dqt.T.reshape(GROUP, tq, HEAD_DIM)

        @pl.when(i == nq - 1)
        def _():
            dk_ref[0, 0] = dkt_sc[...].T
            dv_ref[0, 0] = dvt_sc[...].T

        pl.when(step_id == bl * kv * nq - 1)(wait)

    kvspec = pl.BlockSpec((1, 1, s_len, HEAD_DIM), lambda b, g, i: (b, g, 0, 0))
    qspec = pl.BlockSpec((1, GROUP, tq, HEAD_DIM), lambda b, g, i: (b, g, i, 0))
    tok = pl.BlockSpec((tq, GROUP * HEAD_DIM), lambda b, g, i: (b * nq + i, g))
    anyspec = pl.BlockSpec(memory_space=pl.ANY)
    res = pl.pallas_call(
        body,
        grid=(bl, kv, nq),
        in_specs=[qspec, kvspec, pl.BlockSpec((1, 1, HEAD_DIM, s_len), lambda b, g, i: (b, g, 0, 0)), kvspec,
                  tok, tok, pl.BlockSpec((1, 1, 1, 8, r), lambda b, g, i: (b, g, i, 0, 0))] + [anyspec] * ng,
        out_specs=[qspec, kvspec, kvspec] + [anyspec] * ng,
        out_shape=[SDS(qa.shape, F32), SDS(ka.shape, F32), SDS(va.shape, F32)]
        + [SDS(g.shape, g.dtype) for g in grads],
        scratch_shapes=[pltpu.VMEM((2, tk, r), F32), pltpu.VMEM((2, tk, r), F32),
                        pltpu.VMEM((HEAD_DIM, s_len), F32), pltpu.VMEM((HEAD_DIM, s_len), F32)]
        + _exchange_scratch(ng),
        compiler_params=_cp("arbitrary", "arbitrary", "arbitrary"),
        name="attn_a_bwd",
    )(qa, ka, kat, va, do, o, lse, *grads)
    return res[0], res[1], res[2], res[3:]


def _attn_b_bwd(qb, kb, vb, do, o, lse, bias_t, sink, s_len):
    bl, hb, _, _ = qb.shape
    kv, sp = kb.shape[1], kb.shape[2]
    nb = s_len // BLOCK
    r = GROUP * BLOCK

    def body(q_ref, k_ref, v_ref, do_ref, o_ref, l_ref, bt_ref, sink_ref,
             dq_ref, dk_ref, dv_ref, dsum_ref, dsink_ref):
        g, b, n = pl.program_id(0), pl.program_id(1), pl.program_id(2)
        q = q_ref[0].reshape(r, HEAD_DIM)
        do2 = _stack_heads(do_ref, BLOCK)
        o2 = _stack_heads(o_ref, BLOCK)
        drow = _col_to_row(jnp.sum(do2.astype(F32) * o2.astype(F32), axis=1, keepdims=True), 1)
        lrow = l_ref[0, 0, 0, 0:1, :]

        @pl.when(n == 0)
        def _():
            dk_ref[...] = jnp.zeros_like(dk_ref)
            dv_ref[...] = jnp.zeros_like(dv_ref)

        @pl.when((b == 0) & (n == 0))
        def _():
            dsum_ref[...] = jnp.zeros_like(dsum_ref)
            dsink_ref[...] = jnp.zeros_like(dsink_ref)

        rows = pl.ds(pl.multiple_of(n * BLOCK, BLOCK), SPAN)
        ks, vs = k_ref[0, 0, rows, :], v_ref[0, 0, rows, :]
        st = _dot_nt(ks, q) + bt_ref[0]
        st = jnp.where(_band_mask((SPAN, r), 1, n, s_len), st, NEG_INF)
        pt = jnp.exp(st - lrow)
        dst = pt * (_dot_nt(vs, do2) - drow)
        dsum_ref[0] += dst
        sink_row = jnp.concatenate(
            [jnp.full((1, BLOCK), sink_ref[0, g * GROUP + h], F32) for h in range(GROUP)], axis=1)
        dsink_ref[0, 0:1, :] += -(jnp.exp(sink_row - lrow) * drow)
        ptb, dsb = pt.astype(BF16), dst.astype(BF16)
        dv_ref[0, 0, rows, :] += _dot(ptb, do2)
        dk_ref[0, 0, rows, :] += _dot(dsb, q)
        dq_ref[0] = _dot_tn(dsb, ks).reshape(GROUP, BLOCK, HEAD_DIM)

    kvspec = pl.BlockSpec((1, 1, sp, HEAD_DIM), lambda g, b, n: (b, g, 0, 0))
    qspec = pl.BlockSpec((1, GROUP, BLOCK, HEAD_DIM), lambda g, b, n: (b, g, n, 0))
    tok = pl.BlockSpec((BLOCK, GROUP * HEAD_DIM), lambda g, b, n: (b * nb + n, g))
    return pl.pallas_call(
        body,
        grid=(kv, bl, nb),
        in_specs=[qspec, kvspec, kvspec, tok, tok,
                  pl.BlockSpec((1, 1, 1, 8, r), lambda g, b, n: (b, g, n, 0, 0)),
                  pl.BlockSpec((1, SPAN, r), lambda g, b, n: (g, 0, 0)),
                  pl.BlockSpec(memory_space=pltpu.SMEM)],
        out_specs=[qspec, kvspec, kvspec,
                   pl.BlockSpec((1, SPAN, r), lambda g, b, n: (g, 0, 0)),
                   pl.BlockSpec((1, 8, r), lambda g, b, n: (g, 0, 0))],
        out_shape=[SDS(qb.shape, F32), SDS(kb.shape, F32), SDS(vb.shape, F32),
                   SDS((kv, SPAN, r), F32), SDS((kv, 8, r), F32)],
        compiler_params=_cp("arbitrary", "arbitrary", "arbitrary"),
        name="attn_b_bwd",
    )(qb, kb, vb, do, o, lse, bias_t, sink)


def _bias_reduce(dsum, dsink, bucket_t4):
    kv, _, r = dsum.shape

    def body(ds_ref, dk_ref, bk_ref, rel_ref, sink_ref):
        lane = lax.broadcasted_iota(jnp.int32, (N_BUCKETS, 128), 1)
        lane8 = lax.broadcasted_iota(jnp.int32, (8, 128), 1)
        bk = bk_ref[...]
        for g in range(kv):
            ds = ds_ref[g]
            rowi = lax.broadcasted_iota(jnp.int32, (N_BUCKETS, r), 0)
            red = jnp.zeros((N_BUCKETS, r), F32)
            for b in range(N_BUCKETS):
                red = jnp.where(rowi == b, jnp.sum(jnp.where(bk == b, ds, 0.0), axis=0, keepdims=True), red)
            out = jnp.zeros((N_BUCKETS, 128), F32)
            so = jnp.zeros((8, 128), F32)
            for h in range(GROUP):
                col = jnp.sum(red[:, h * BLOCK:(h + 1) * BLOCK], axis=1, keepdims=True)
                out = jnp.where(lane == h, col, out)
                sc = jnp.sum(dk_ref[g][:, h * BLOCK:(h + 1) * BLOCK], axis=1, keepdims=True)
                so = jnp.where(lane8 == h, sc, so)
            rel_ref[g] = out
            sink_ref[g] = so

    vm = pl.BlockSpec(memory_space=pltpu.VMEM)
    return pl.pallas_call(
        body,
        in_specs=[vm, vm, vm],
        out_specs=[vm, vm],
        out_shape=[SDS((kv, N_BUCKETS, 128), F32), SDS((kv, 8, 128), F32)],
        name="bias_reduce",
    )(dsum, dsink, bucket_t4)


def _dqkprep(dqa, dka, dva, dqb, dkb, dvb, proj, cos, sin_signed, gq, gk, s_len, ts):
    nd, t, pb = proj.shape
    bl, ha = dqa.shape[0], dqa.shape[1]
    kva, hb, kvb = dka.shape[1], dqb.shape[1], dkb.shape[1]
    hpb = pb // HEAD_DIM
    ns = s_len // ts

    def body(dqa_ref, dka_ref, dva_ref, dqb_ref, dkb_ref, dvb_ref, p_ref, cos_ref, sin_ref, gq_ref, gk_ref,
             dp_ref, dgq_ref, dgk_ref):
        b, i = pl.program_id(0), pl.program_id(1)
        cs, sn = cos_ref[...], sin_ref[...]
        lane = lax.broadcasted_iota(jnp.int32, (ts, HEAD_DIM), 1)
        first = (lane % 32) < 16

        @pl.when((b == 0) & (i == 0))
        def _():
            dgq_ref[...] = jnp.zeros_like(dgq_ref)
            dgk_ref[...] = jnp.zeros_like(dgk_ref)

        def put(hh, val):
            dp_ref[hh // hpb, :, pl.ds((hh % hpb) * HEAD_DIM, HEAD_DIM)] = val.astype(BF16)

        def unrope_norm(d_rot, hh, g, dg_ref):
            ds = d_rot * sn
            dn = d_rot * cs + jnp.where(first, jnp.roll(ds, -16, axis=1), jnp.roll(ds, 16, axis=1))
            _, n, r = _rms_fwd(_head_slice(p_ref, hh, hpb), g)
            dx, dgt = _rms_bwd(n, r, g, dn)
            dg_ref[0:1, :] += jnp.sum(dgt, axis=0, keepdims=True)
            put(hh, dx)

        for h in range(ha):
            unrope_norm(dqa_ref[0, h] * SCALE, h, gq_ref[...], dgq_ref)
        for h in range(kva):
            unrope_norm(dka_ref[0, h], ha + h, gk_ref[...], dgk_ref)
            put(ha + kva + h, dva_ref[0, h])
        base = ha + 2 * kva
        for h in range(hb):
            put(base + h, dqb_ref[0, h] * SCALE)
        for h in range(kvb):
            put(base + hb + h, dkb_ref[0, h])
            put(base + hb + kvb + h, dvb_ref[0, h])

    def hm(nh):
        return pl.BlockSpec((1, nh, ts, HEAD_DIM), lambda b, i: (b, 0, i, 0))

    vec = pl.BlockSpec((1, HEAD_DIM), lambda b, i: (0, 0))
    tab = pl.BlockSpec((ts, HEAD_DIM), lambda b, i: (i, 0))
    acc = pl.BlockSpec((8, HEAD_DIM), lambda b, i: (0, 0))
    pspec = pl.BlockSpec((nd, ts, pb), lambda b, i: (0, b * ns + i, 0))
    return pl.pallas_call(
        body,
        grid=(bl, ns),
        in_specs=[hm(ha), hm(kva), hm(kva), hm(hb), hm(kvb), hm(kvb), pspec, tab, tab, vec, vec],
        out_specs=[pspec, acc, acc],
        out_shape=[SDS((nd, t, pb), BF16), SDS((8, HEAD_DIM), F32), SDS((8, HEAD_DIM), F32)],
        compiler_params=_cp("arbitrary", "arbitrary"),
        name="dqkprep",
    )(dqa, dka, dva, dqb, dkb, dvb, proj, cos, sin_signed, gq, gk)


def _dx_final(dproj, win_g, x2, dx1, g1, tm):
    t, d = x2.shape
    nd, _, pb = win_g.shape

    def body(dp_ref, w_ref, x_ref, dx1_ref, g_ref, dx_ref, dg_ref):
        @pl.when(pl.program_id(0) == 0)
        def _():
            dg_ref[...] = jnp.zeros_like(dg_ref)

        dh = _dot_nt(dp_ref[0], w_ref[0])
        for j in range(1, nd):
            dh = dh + _dot_nt(dp_ref[j], w_ref[j])
        g = g_ref[...]
        _, n, r = _rms_fwd(x_ref[...], g)
        dx, dgt = _rms_bwd(n, r, g, dh)
        dx_ref[...] = dx1_ref[...] + dx
        dg_ref[0:1, :] += jnp.sum(dgt, axis=0, keepdims=True)

    tile = pl.BlockSpec((tm, d), lambda i: (i, 0))
    return pl.pallas_call(
        body,
        grid=(t // tm,),
        in_specs=[pl.BlockSpec((nd, tm, pb), lambda i: (0, i, 0)),
                  pl.BlockSpec((nd, d, pb), lambda i: (0, 0, 0)),
                  tile, tile, pl.BlockSpec((1, d), lambda i: (0, 0))],
        out_specs=[tile, pl.BlockSpec((8, d), lambda i: (0, 0))],
        out_shape=[SDS((t, d), F32), SDS((8, d), F32)],
        compiler_params=_cp("arbitrary"),
        name="dx_final",
    )(dproj, win_g, x2, dx1, g1)


def _adamw_math(w, g, m, v):
    m = ADAM_B1 * m + (1.0 - ADAM_B1) * g
    v = ADAM_B2 * v + (1.0 - ADAM_B2) * (g * g)
    m_hat = m / (1.0 - ADAM_B1 ** ADAM_STEP)
    v_hat = v / (1.0 - ADAM_B2 ** ADAM_STEP)
    delta = -ADAM_LR * (m_hat / (jnp.sqrt(v_hat) + ADAM_EPS) + ADAM_WD * w)
    return delta, m, v


def _adamw_sum(parts, w, m, v, tr, name):
    rows, cols = w.shape

    def body(p_ref, w_ref, m_ref, v_ref, g_ref, d_ref, nm_ref, nv_ref):
        g = p_ref[0]
        for s in range(1, N_DEV):
            g = g + p_ref[s]
        g_ref[...] = g
        d_ref[...], nm_ref[...], nv_ref[...] = _adamw_math(w_ref[...], g, m_ref[...], v_ref[...])

    tr = min(tr, rows)
    tile = pl.BlockSpec((tr, cols), lambda i: (i, 0))
    return pl.pallas_call(
        body,
        grid=(rows // tr,),
        in_specs=[pl.BlockSpec((N_DEV, tr, cols), lambda i: (0, i, 0)), tile, tile, tile],
        out_specs=[tile] * 4,
        out_shape=[SDS((rows, cols), F32)] * 4,
        compiler_params=_cp("parallel"),
        name=name,
    )(parts, w, m, v)


def _adamw_small(g, w, m, v):
    def body(g_ref, w_ref, m_ref, v_ref, d_ref, nm_ref, nv_ref):
        d_ref[...], nm_ref[...], nv_ref[...] = _adamw_math(w_ref[...], g_ref[...], m_ref[...], v_ref[...])

    vm = pl.BlockSpec(memory_space=pltpu.VMEM)
    return pl.pallas_call(
        body,
        in_specs=[vm] * 4,
        out_specs=[vm] * 3,
        out_shape=[SDS(w.shape, F32)] * 3,
        name="adamw_small",
    )(g, w, m, v)


SMALL_ROWS = 8
SMALL_COLS = 1024


def _pack_small(g1, g2, g3, g4, gq, gk, sink, rel):
    row4 = jnp.concatenate([gq.reshape(-1), gk.reshape(-1), sink.reshape(-1)])
    row4 = jnp.pad(row4, (0, SMALL_COLS - row4.shape[0]))
    row5 = jnp.pad(rel.reshape(-1), (0, SMALL_COLS - rel.size))
    zero = jnp.zeros((SMALL_COLS,), F32)
    return jnp.stack([g1.reshape(-1), g2.reshape(-1), g3.reshape(-1), g4.reshape(-1), row4, row5, zero, zero])


def _unpack_small(p, hb):
    hd = HEAD_DIM
    return (p[0:1], p[1:2], p[2:3], p[3:4], p[4:5, 0:hd], p[4:5, hd:2 * hd], p[4:5, 2 * hd:2 * hd + hb],
            p[5, 0:N_BUCKETS * hb].reshape(N_BUCKETS, hb))


def _local_step(x, loss_target, win_g, wo_s, wup_s, wdn_s, g_pre_mix, g_post_mix, q_norm_a, k_norm_a, sink_b,
                rel_bias, g_pre_ffn, g_post_ffn):
    bl, s_len, d = x.shape
    t = bl * s_len
    nh = d // HEAD_DIM
    ha = nh // 2
    kva = ha // GROUP
    hb = nh - ha
    kvb = hb // GROUP
    tm = 512
    ts = min(512, s_len)
    tq, tk = BLOCK, min(512, s_len // 2)

    x2 = x.reshape(t, d)
    tg2 = loss_target.reshape(t, d)
    cos, sin_signed = _rope_tables(s_len)
    a = jnp.arange(BLOCK, dtype=jnp.int32)
    c = jnp.arange(SPAN, dtype=jnp.int32)
    bucket = _t5_bucket(c[None, :] - BLOCK - a[:, None])
    bucket_t = bucket.T
    bucket_t4 = jnp.tile(bucket_t, (1, GROUP))

    h1, proj = _inproj(x2, g_pre_mix, win_g, tm)
    qa, ka, kat, va, vat, qb, kb, vb = _qkprep(
        proj, cos, sin_signed, q_norm_a, k_norm_a, bl, s_len, ha, kva, hb, kvb, ts)
    bias, bias_t = _bias_build(bucket, bucket_t, rel_bias, hb)
    oa, lse_a, (wo_g, wup_g, wdn_g) = _attn_a_fwd(qa, ka, vat, tq, tk, [wo_s, wup_s, wdn_s])
    wo = wo_g.reshape(-1, d)
    wdn = wdn_g.reshape(-1, d)
    ob, lse_b = _attn_b_fwd(qb, kb, vb, bias, sink_b, s_len)
    mix, x1, h2 = _mixout(oa, ob, wo, x2, g_post_mix, g_pre_ffn, tm)
    u, df, dy, dg4, loss8 = _ffn_fwd(h2, wup_g, wdn, x1, tg2, g_post_ffn, tm)

    dpre, dx1, dmix, dg3, dg2 = _ffn_bwd(df, u, wdn, wup_g, x1, dy, mix, g_pre_ffn, g_post_mix, tm)
    gw_dn = _wgrad_rows([u], df, N_DEV, tm, True, "wgrad_down")
    gw_up = _wgrad_cols(h2, dpre, N_DEV, tm, "wgrad_up")
    gw_o = _wgrad_rows([oa, ob], dmix, N_DEV, tm, False, "wgrad_o")
    doa, dob = _attn_out_bwd(dmix, wo, oa.shape[1], tm)
    dqa, dka, dva, (p_o, p_up, p_dn) = _attn_a_bwd(qa, ka, kat, va, doa, oa, lse_a, tq, tk, [gw_o, gw_up, gw_dn])
    dqb, dkb_p, dvb_p, dsum, dsink = _attn_b_bwd(qb, kb, vb, dob, ob, lse_b, bias_t, sink_b, s_len)
    drel_g, dsink_g = _bias_reduce(dsum, dsink, bucket_t4)
    dkb = dkb_p[:, :, BLOCK:BLOCK + s_len, :]
    dvb = dvb_p[:, :, BLOCK:BLOCK + s_len, :]
    dproj, dgq, dgk = _dqkprep(dqa, dka, dva, dqb, dkb, dvb, proj, cos, sin_signed, q_norm_a, k_norm_a, s_len, ts)
    gw_in = _wgrad_cols_blocked(h1, dproj, tm, "wgrad_in")
    grad_x, dg1 = _dx_final(dproj, win_g, x2, dx1, g_pre_mix, tm)

    drel = jnp.transpose(drel_g[:, :, 0:GROUP], (1, 0, 2)).reshape(N_BUCKETS, hb)
    dsink_v = dsink_g[:, 0, 0:GROUP].reshape(1, hb)
    small = _pack_small(dg1[0], dg2[0], dg3[0], dg4[0], dgq[0], dgk[0], dsink_v, drel)
    small = small.at[6, 0].set(loss8[0, 0])
    return grad_x.reshape(bl, s_len, d), gw_in, p_o, p_up, p_dn, small


def kernel(x, w_in, w_o, g_pre_mix, g_post_mix, q_norm_a, k_norm_a, sink_b, rel_bias, g_pre_ffn, w_ffn_up, w_ffn_down, g_post_ffn, loss_target, m_w_in, m_w_o, m_g_pre_mix, m_g_post_mix, m_q_norm_a, m_k_norm_a, m_sink_b, m_rel_bias, m_g_pre_ffn, m_w_ffn_up, m_w_ffn_down, m_g_post_ffn, v_w_in, v_w_o, v_g_pre_mix, v_g_post_mix, v_q_norm_a, v_k_norm_a, v_sink_b, v_rel_bias, v_g_pre_ffn, v_w_ffn_up, v_w_ffn_down, v_g_post_ffn):
    hb = sink_b.shape[1]
    d = x.shape[-1]
    (win_g,) = _weight_gather([w_in[0].astype(BF16)])

    grad_x, gw_in, p_o, p_up, p_dn, small = _local_step(
        x, loss_target, win_g, w_o[0].astype(BF16), w_ffn_up[0].astype(BF16), w_ffn_down[0].astype(BF16),
        g_pre_mix, g_post_mix, q_norm_a, k_norm_a, sink_b, rel_bias, g_pre_ffn, g_post_ffn)

    (p_in,) = _grad_exchange([gw_in])
    small = _small_allreduce(small)

    g_in, d_in, nm_in, nv_in = _adamw_sum(p_in, w_in[0], m_w_in[0], v_w_in[0], 256, "adamw_in")
    g_o, d_o, nm_o, nv_o = _adamw_sum(p_o, w_o[0], m_w_o[0], v_w_o[0], 128, "adamw_o")
    g_up, d_up, nm_up, nv_up = _adamw_sum(p_up, w_ffn_up[0], m_w_ffn_up[0], v_w_ffn_up[0], 256, "adamw_up")
    g_dn, d_dn, nm_dn, nv_dn = _adamw_sum(p_dn, w_ffn_down[0], m_w_ffn_down[0], v_w_ffn_down[0], 256, "adamw_down")

    pack = lambda *a: _pack_small(*a)
    w_s = pack(g_pre_mix, g_post_mix, g_pre_ffn, g_post_ffn, q_norm_a, k_norm_a, sink_b, rel_bias)
    m_s = pack(m_g_pre_mix, m_g_post_mix, m_g_pre_ffn, m_g_post_ffn, m_q_norm_a, m_k_norm_a, m_sink_b, m_rel_bias)
    v_s = pack(v_g_pre_mix, v_g_post_mix, v_g_pre_ffn, v_g_post_ffn, v_q_norm_a, v_k_norm_a, v_sink_b, v_rel_bias)
    d_s, nm_s, nv_s = _adamw_small(small, w_s, m_s, v_s)

    loss = small[6, 0]

    def outs(big_in, big_o, sm, big_up, big_dn):
        s1, s2, s3, s4, sq, sk, ss, sr = _unpack_small(sm, hb)
        return [big_in[None], big_o[None], s1, s2, sq, sk, ss, sr, s3, big_up[None], big_dn[None], s4]

    return (loss, grad_x,
            *outs(g_in, g_o, small, g_up, g_dn),
            *outs(d_in, d_o, d_s, d_up, d_dn),
            *outs(nm_in, nm_o, nm_s, nm_up, nm_dn),
            *outs(nv_in, nv_o, nv_s, nv_up, nv_dn))
```

```python
import functools

import jax
import jax.numpy as jnp
import numpy as np
from jax import lax
from jax.experimental import pallas as pl
from jax.experimental.pallas import tpu as pltpu

F32 = jnp.float32
BF16 = jnp.bfloat16
SDS = jax.ShapeDtypeStruct

N_DEV = 8
HEAD_DIM = 64
GROUP = 4
BLOCK = 128
SPAN = 3 * BLOCK
GRID_W = 64
N_BUCKETS = 32
MAX_DISTANCE = 128
ROPE_THETA = 10000.0
EPS = 1e-6
NEG_INF = -1e30
SCALE = HEAD_DIM ** -0.5
VT_PAD = 16

ADAM_LR = 0.001
ADAM_B1 = 0.9
ADAM_B2 = 0.999
ADAM_EPS = 1e-08
ADAM_WD = 0.01
ADAM_STEP = 10

VMEM_LIMIT = 56 * 1024 * 1024
MESH = pl.DeviceIdType.MESH


def _cp(*sem):
    return pltpu.CompilerParams(dimension_semantics=sem, vmem_limit_bytes=VMEM_LIMIT)


def _dot(a, b):
    return jnp.dot(a, b, preferred_element_type=F32)


def _dot_nt(a, b):
    return lax.dot_general(a, b, (((1,), (1,)), ((), ())), preferred_element_type=F32)


def _dot_tn(a, b):
    return lax.dot_general(a, b, (((0,), (0,)), ((), ())), preferred_element_type=F32)


def _rms_fwd(x, g):
    r = lax.rsqrt(jnp.mean(x * x, axis=-1, keepdims=True) + EPS)
    n = x * r
    return n * g, n, r


def _rms_bwd(n, r, g, dy):
    gd = g * dy
    dx = r * (gd - n * jnp.mean(n * gd, axis=-1, keepdims=True))
    return dx, dy * n


def _rope(n, cos, sin_signed, first):
    partner = jnp.where(first, jnp.roll(n, -16, axis=1), jnp.roll(n, 16, axis=1))
    return n * cos + partner * sin_signed


def _rope_tables(s_len):
    rows = s_len // GRID_W
    row = jnp.repeat(jnp.arange(rows, dtype=jnp.int32), GRID_W)
    col = jnp.tile(jnp.arange(GRID_W, dtype=jnp.int32), rows)
    nf = HEAD_DIM // 4
    freqs = ROPE_THETA ** (-jnp.arange(nf, dtype=F32) / nf)
    ang_r = row.astype(F32)[:, None] * freqs[None, :]
    ang_c = col.astype(F32)[:, None] * freqs[None, :]
    cr, sr, cc, sc = jnp.cos(ang_r), jnp.sin(ang_r), jnp.cos(ang_c), jnp.sin(ang_c)
    cos = jnp.concatenate([cr, cr, cc, cc], axis=-1)
    sin_signed = jnp.concatenate([-sr, sr, -sc, sc], axis=-1)
    return cos, sin_signed


def _t5_bucket(rel):
    nb = N_BUCKETS // 2
    ret = (rel > 0).astype(jnp.int32) * nb
    n = jnp.abs(rel)
    max_exact = nb // 2
    nf = jnp.maximum(n, 1).astype(F32)
    large = max_exact + (jnp.log(nf / max_exact) / np.float32(np.log(MAX_DISTANCE / max_exact))
                         * (nb - max_exact)).astype(jnp.int32)
    large = jnp.minimum(large, nb - 1)
    return ret + jnp.where(n < max_exact, n, large)


def _mesh_pos():
    return lax.axis_index("x"), lax.axis_index("y"), lax.axis_index("c")


def _lin(p):
    return 4 * p[0] + 2 * p[1] + p[2]


def _weight_gather(shards):
    n = len(shards)

    def body(*refs):
        xs, outs = refs[:n], refs[n:2 * n]
        send_sems, recv_sems, local_sems = refs[2 * n:]
        x, y, c = _mesh_pos()
        me, sibling = (x, y, c), (x, y, 1 - c)
        chips = [(1 - x, y), (x, 1 - y), (1 - x, 1 - y)]

        def copy(a, k, block, to, src=None):
            slot = outs[a].at[_lin(block)]
            return pltpu.make_async_remote_copy(
                src_ref=slot if src is None else src, dst_ref=slot,
                send_sem=send_sems.at[a, k], recv_sem=recv_sems.at[a, k],
                device_id=to, device_id_type=MESH)

        started = []
        for a in range(n):
            mine = pltpu.make_async_copy(xs[a], outs[a].at[_lin(me)], local_sems.at[a])
            mine.start()
            started.append(mine)
        sends = []
        for a in range(n):
            first = [copy(a, 0, me, sibling, src=xs[a])]
            first += [copy(a, 1 + j, me, (*chip, c), src=xs[a]) for j, chip in enumerate(chips)]
            for cp in first:
                cp.start()
            sends += first
        for a in range(n):
            for j, chip in enumerate(chips):
                copy(a, 1 + j, (*chip, c), me).wait_recv()
                fwd = copy(a, 4 + j, (*chip, c), sibling)
                fwd.start()
                sends.append(fwd)
        for a in range(n):
            copy(a, 0, sibling, me).wait_recv()
            for j, chip in enumerate(chips):
                copy(a, 4 + j, (*chip, 1 - c), me).wait_recv()
        for cp in sends:
            cp.wait_send()
        for mine in started:
            mine.wait()

    anyspec = pl.BlockSpec(memory_space=pl.ANY)
    return pl.pallas_call(
        body,
        out_shape=[SDS((N_DEV,) + s.shape, s.dtype) for s in shards],
        in_specs=[anyspec] * n,
        out_specs=[anyspec] * n,
        scratch_shapes=[pltpu.SemaphoreType.DMA((n, 7)), pltpu.SemaphoreType.DMA((n, 7)),
                        pltpu.SemaphoreType.DMA((n,))],
        name="weight_gather",
    )(*shards)


def _direct_exchange(kind, ins, outs, send_sems, recv_sems, local_sems):
    x, y, c = _mesh_pos()
    me = (x, y, c)
    peers = [(x, y, 1 - c), (1 - x, y, c), (x, 1 - y, c), (1 - x, 1 - y, c),
             (1 - x, y, 1 - c), (x, 1 - y, 1 - c), (1 - x, 1 - y, 1 - c)]

    def src(a, to):
        return ins[a] if kind == "gather" else ins[a].at[_lin(to)]

    def remote(a, k, to, frm):
        return pltpu.make_async_remote_copy(
            src_ref=src(a, to), dst_ref=outs[a].at[_lin(frm)],
            send_sem=send_sems.at[a, k], recv_sem=recv_sems.at[a, k],
            device_id=to, device_id_type=MESH)

    n = len(ins)
    sends = [remote(a, k, p, me) for a in range(n) for k, p in enumerate(peers)]
    arrivals = [remote(a, k, p, p) for a in range(n) for k, p in enumerate(peers)]
    local = [pltpu.make_async_copy(src(a, me), outs[a].at[_lin(me)], local_sems.at[a]) for a in range(n)]

    def start():
        for cp in local + sends:
            cp.start()

    def wait():
        for cp in arrivals:
            cp.wait_recv()
        for cp in sends:
            cp.wait_send()
        for cp in local:
            cp.wait()

    return start, wait


def _exchange_scratch(n):
    return [pltpu.SemaphoreType.DMA((n, 7)), pltpu.SemaphoreType.DMA((n, 7)), pltpu.SemaphoreType.DMA((n,))]


def _grad_exchange(grads):
    n = len(grads)

    def body(*refs):
        start, wait = _direct_exchange("scatter", refs[:n], refs[n:2 * n], *refs[2 * n:])
        start()
        wait()

    anyspec = pl.BlockSpec(memory_space=pl.ANY)
    return pl.pallas_call(
        body,
        out_shape=[SDS(g.shape, g.dtype) for g in grads],
        in_specs=[anyspec] * n,
        out_specs=[anyspec] * n,
        scratch_shapes=_exchange_scratch(n),
        name="grad_exchange",
    )(*grads)


def _small_allreduce(v):
    rows, cols = v.shape

    def body(v_ref, out_ref, land_ref, send_sems, recv_sems):
        x, y, c = _mesh_pos()
        me = (x, y, c)
        peers = [(x, y, 1 - c), (1 - x, y, c), (x, 1 - y, c), (1 - x, 1 - y, c),
                 (1 - x, y, 1 - c), (x, 1 - y, 1 - c), (1 - x, 1 - y, 1 - c)]

        def copy(k, to, frm):
            return pltpu.make_async_remote_copy(
                src_ref=v_ref, dst_ref=land_ref.at[_lin(frm)],
                send_sem=send_sems.at[k], recv_sem=recv_sems.at[k],
                device_id=to, device_id_type=MESH)

        sends = [copy(k, p, me) for k, p in enumerate(peers)]
        for cp in sends:
            cp.start()
        land_ref[_lin(me)] = v_ref[...]
        for k, p in enumerate(peers):
            copy(k, p, p).wait_recv()
        for cp in sends:
            cp.wait_send()
        acc = land_ref[0]
        for s in range(1, N_DEV):
            acc = acc + land_ref[s]
        out_ref[...] = acc

    vm = pl.BlockSpec(memory_space=pltpu.VMEM)
    return pl.pallas_call(
        body,
        out_shape=SDS((rows, cols), F32),
        in_specs=[vm],
        out_specs=vm,
        scratch_shapes=[pltpu.VMEM((N_DEV, rows, cols), F32),
                        pltpu.SemaphoreType.DMA((7,)), pltpu.SemaphoreType.DMA((7,))],
        name="small_allreduce",
    )(v)


def _inproj(x2, g1, win_g, tm):
    t, d = x2.shape
    nd, _, pb = win_g.shape

    def body(x_ref, g_ref, w_ref, h_ref, p_ref):
        y, _, _ = _rms_fwd(x_ref[...], g_ref[...])
        h = y.astype(BF16)
        h_ref[...] = h
        for j in range(nd):
            p_ref[j] = _dot(h, w_ref[j])

    return pl.pallas_call(
        body,
        grid=(t // tm,),
        in_specs=[pl.BlockSpec((tm, d), lambda i: (i, 0)),
                  pl.BlockSpec((1, d), lambda i: (0, 0)),
                  pl.BlockSpec((nd, d, pb), lambda i: (0, 0, 0))],
        out_specs=[pl.BlockSpec((tm, d), lambda i: (i, 0)),
                   pl.BlockSpec((nd, tm, pb), lambda i: (0, i, 0))],
        out_shape=[SDS((t, d), BF16), SDS((nd, t, pb), F32)],
        compiler_params=_cp("parallel"),
        name="inproj",
    )(x2, g1, win_g)


def _head_slice(p_ref, hh, hpb):
    return p_ref[hh // hpb, :, pl.ds((hh % hpb) * HEAD_DIM, HEAD_DIM)]


def _qkprep(proj, cos, sin_signed, gq, gk, bl, s_len, ha, kva, hb, kvb, ts):
    nd, t, pb = proj.shape
    hpb = pb // HEAD_DIM
    ns = s_len // ts
    sp = s_len + 2 * BLOCK

    def body(p_ref, cos_ref, sin_ref, gq_ref, gk_ref, qa_ref, ka_ref, kat_ref, va_ref, vat_ref, qb_ref, kb_ref,
             kbt_ref, vb_ref, vbt_ref):
        i = pl.program_id(1)
        cs, sn = cos_ref[...], sin_ref[...]
        lane = lax.broadcasted_iota(jnp.int32, (ts, HEAD_DIM), 1)
        first = (lane % 32) < 16
        ones_row = (lax.broadcasted_iota(jnp.int32, (VT_PAD, ts), 0) == 0).astype(BF16)

        def normrope(xh, g):
            y, _, _ = _rms_fwd(xh, g)
            return _rope(y, cs, sn, first)

        for h in range(ha):
            qa_ref[0, h] = (normrope(_head_slice(p_ref, h, hpb), gq_ref[...]) * SCALE).astype(BF16)
        for h in range(kva):
            kh = normrope(_head_slice(p_ref, ha + h, hpb), gk_ref[...])
            ka_ref[0, h] = kh.astype(BF16)
            kat_ref[0, h] = kh.T.astype(BF16)
            vh = _head_slice(p_ref, ha + kva + h, hpb)
            va_ref[0, h] = vh.astype(BF16)
            vat_ref[0, h, 0:HEAD_DIM, :] = vh.T.astype(BF16)
            vat_ref[0, h, HEAD_DIM:HEAD_DIM + VT_PAD, :] = ones_row
        base = ha + 2 * kva
        for h in range(hb):
            qb_ref[0, h] = (_head_slice(p_ref, base + h, hpb) * SCALE).astype(BF16)

        @pl.when(i == 0)
        def _():
            zeros = jnp.zeros((kvb, BLOCK, HEAD_DIM), BF16)
            zeros_t = jnp.zeros((kvb, HEAD_DIM + VT_PAD, BLOCK), BF16)
            for ref in (kb_ref, vb_ref):
                ref[0, :, 0:BLOCK, :] = zeros
                ref[0, :, sp - BLOCK:sp, :] = zeros
            kbt_ref[0, :, :, 0:BLOCK] = zeros_t[:, 0:HEAD_DIM]
            kbt_ref[0, :, :, sp - BLOCK:sp] = zeros_t[:, 0:HEAD_DIM]
            vbt_ref[0, :, :, 0:BLOCK] = zeros_t
            vbt_ref[0, :, :, sp - BLOCK:sp] = zeros_t

        row0 = pl.multiple_of(BLOCK + i * ts, BLOCK)
        for h in range(kvb):
            kh = _head_slice(p_ref, base + hb + h, hpb)
            vh = _head_slice(p_ref, base + hb + kvb + h, hpb)
            kb_ref[0, h, pl.ds(row0, ts), :] = kh.astype(BF16)
            vb_ref[0, h, pl.ds(row0, ts), :] = vh.astype(BF16)
            kbt_ref[0, h, :, pl.ds(row0, ts)] = kh.T.astype(BF16)
            vbt_ref[0, h, 0:HEAD_DIM, pl.ds(row0, ts)] = vh.T.astype(BF16)
            vbt_ref[0, h, HEAD_DIM:HEAD_DIM + VT_PAD, pl.ds(row0, ts)] = ones_row

    def hm(nh):
        return pl.BlockSpec((1, nh, ts, HEAD_DIM), lambda b, i: (b, 0, i, 0))

    def padded(nh):
        return pl.BlockSpec((1, nh, sp, HEAD_DIM), lambda b, i: (b, 0, 0, 0))

    def padded_t(nh, rows):
        return pl.BlockSpec((1, nh, rows, sp), lambda b, i: (b, 0, 0, 0))

    return pl.pallas_call(
        body,
        grid=(bl, ns),
        in_specs=[pl.BlockSpec((nd, ts, pb), lambda b, i: (0, b * ns + i, 0)),
                  pl.BlockSpec((ts, HEAD_DIM), lambda b, i: (i, 0)),
                  pl.BlockSpec((ts, HEAD_DIM), lambda b, i: (i, 0)),
                  pl.BlockSpec((1, HEAD_DIM), lambda b, i: (0, 0)),
                  pl.BlockSpec((1, HEAD_DIM), lambda b, i: (0, 0))],
        out_specs=[hm(ha), hm(kva), pl.BlockSpec((1, kva, HEAD_DIM, ts), lambda b, i: (b, 0, 0, i)), hm(kva),
                   pl.BlockSpec((1, kva, HEAD_DIM + VT_PAD, ts), lambda b, i: (b, 0, 0, i)),
                   hm(hb), padded(kvb), padded_t(kvb, HEAD_DIM), padded(kvb), padded_t(kvb, HEAD_DIM + VT_PAD)],
        out_shape=[SDS((bl, ha, s_len, HEAD_DIM), BF16), SDS((bl, kva, s_len, HEAD_DIM), BF16),
                   SDS((bl, kva, HEAD_DIM, s_len), BF16),
                   SDS((bl, kva, s_len, HEAD_DIM), BF16), SDS((bl, kva, HEAD_DIM + VT_PAD, s_len), BF16),
                   SDS((bl, hb, s_len, HEAD_DIM), BF16),
                   SDS((bl, kvb, sp, HEAD_DIM), BF16), SDS((bl, kvb, HEAD_DIM, sp), BF16),
                   SDS((bl, kvb, sp, HEAD_DIM), BF16), SDS((bl, kvb, HEAD_DIM + VT_PAD, sp), BF16)],
        compiler_params=_cp("parallel", "arbitrary"),
        name="qkprep",
    )(proj, cos, sin_signed, gq, gk)


def _bias_build(bucket_t, rel_bias, hb):
    kvb = hb // GROUP

    def body(bkt_ref, tbl_ref, out_ref):
        bkt = bkt_ref[...]
        ci = lax.broadcasted_iota(jnp.int32, (SPAN, BLOCK), 0)
        qi = lax.broadcasted_iota(jnp.int32, (SPAN, BLOCK), 1)
        band = jnp.abs(ci - BLOCK - qi) <= BLOCK
        masks = (band, band & (ci >= BLOCK), band & (ci < 2 * BLOCK))
        for h in range(hb):
            acct = jnp.zeros((SPAN, BLOCK), F32)
            for b in range(N_BUCKETS):
                acct = jnp.where(bkt == b, tbl_ref[b, h], acct)
            lanes = slice((h % GROUP) * BLOCK, (h % GROUP + 1) * BLOCK)
            for var, mask in enumerate(masks):
                out_ref[var, h // GROUP, :, lanes] = jnp.where(mask, acct, NEG_INF)

    vm = pl.BlockSpec(memory_space=pltpu.VMEM)
    return pl.pallas_call(
        body,
        in_specs=[vm, pl.BlockSpec(memory_space=pltpu.SMEM)],
        out_specs=vm,
        out_shape=SDS((3, kvb, SPAN, GROUP * BLOCK), F32),
        name="bias_build",
    )(bucket_t, rel_bias)


def _attn_a_fwd(qa, ka, vat, tq, tk, shards):
    bl, ha, s_len, _ = qa.shape
    kv = ka.shape[1]
    va_rows = vat.shape[2]
    nq, nk = s_len // tq, s_len // tk
    assert nk % 2 == 0
    r = GROUP * tq
    ns = len(shards)

    def body(q_ref, k_ref, v_ref, *rest):
        shard_refs, (o_ref, l_ref), gathered = rest[:ns], rest[ns:ns + 2], rest[ns + 2:2 * ns + 2]
        st_sc, send_sems, recv_sems, local_sems = rest[2 * ns + 2:]
        step_id = (pl.program_id(0) * kv + pl.program_id(1)) * nq + pl.program_id(2)
        start, wait = _direct_exchange("gather", shard_refs, gathered, send_sems, recv_sems, local_sems)
        pl.when(step_id == 0)(start)

        q = q_ref[0].reshape(r, HEAD_DIM)

        def scores(c):
            return _dot_nt(k_ref[0, 0, pl.ds(pl.multiple_of(c * tk, tk), tk), :], q)

        def fold(st, c, carry):
            m_old, acc = carry
            m_new = jnp.maximum(m_old, jnp.max(st, axis=0, keepdims=True))
            pt = jnp.exp(st - m_new).astype(BF16)
            vt = v_ref[0, 0, :, pl.ds(pl.multiple_of(c * tk, tk), tk)]
            return m_new, jnp.exp(m_old - m_new) * acc + _dot(vt, pt)

        st_sc[0] = scores(0)

        def step(c2, carry):
            c = 2 * c2
            st_sc[1] = scores(c + 1)
            carry = fold(st_sc[0], c, carry)
            st_sc[0] = scores(jnp.minimum(c + 2, nk - 1))
            return fold(st_sc[1], c + 1, carry)

        m, acc = lax.fori_loop(0, nk // 2, step,
                               (jnp.full((1, r), -jnp.inf, F32), jnp.zeros((va_rows, r), F32)))
        l = acc[HEAD_DIM:HEAD_DIM + 1, :]
        o = (acc[0:HEAD_DIM, :] / l).T
        for h in range(GROUP):
            o_ref[:, h * HEAD_DIM:(h + 1) * HEAD_DIM] = o[h * tq:(h + 1) * tq].astype(BF16)
        l_ref[0, 0, 0] = jnp.broadcast_to(m + jnp.log(l), (8, r))
        pl.when(step_id == bl * kv * nq - 1)(wait)

    anyspec = pl.BlockSpec(memory_space=pl.ANY)
    res = pl.pallas_call(
        body,
        grid=(bl, kv, nq),
        in_specs=[pl.BlockSpec((1, GROUP, tq, HEAD_DIM), lambda b, g, i: (b, g, i, 0)),
                  pl.BlockSpec((1, 1, s_len, HEAD_DIM), lambda b, g, i: (b, g, 0, 0)),
                  pl.BlockSpec((1, 1, va_rows, s_len), lambda b, g, i: (b, g, 0, 0))] + [anyspec] * ns,
        out_specs=[pl.BlockSpec((tq, GROUP * HEAD_DIM), lambda b, g, i: (b * nq + i, g)),
                   pl.BlockSpec((1, 1, 1, 8, r), lambda b, g, i: (b, g, i, 0, 0))] + [anyspec] * ns,
        out_shape=[SDS((bl * s_len, ha * HEAD_DIM), BF16), SDS((bl, kv, nq, 8, r), F32)]
        + [SDS((N_DEV,) + s.shape, s.dtype) for s in shards],
        scratch_shapes=[pltpu.VMEM((2, tk, r), F32)] + _exchange_scratch(ns),
        compiler_params=_cp("arbitrary", "arbitrary", "arbitrary"),
        name="attn_a_fwd",
    )(qa, ka, vat, *shards)
    return res[0], res[1], res[2:]


QB_PER_STEP = 4


def _bias_variant(n, nb):
    return jnp.where(n == 0, 1, jnp.where(n == nb - 1, 2, 0))


def _sink_row(sink_ref, g):
    return jnp.concatenate([jnp.full((1, BLOCK), sink_ref[0, g * GROUP + h], F32) for h in range(GROUP)], axis=1)


def _attn_b_fwd(qb, kb, vbt, bias_t, sink, s_len):
    bl, hb, _, _ = qb.shape
    kv = kb.shape[1]
    sp = kb.shape[2]
    vt_rows = vbt.shape[2]
    nb = s_len // BLOCK
    nbs = min(QB_PER_STEP, nb)
    r = GROUP * BLOCK

    def body(q_ref, k_ref, vt_ref, bt_ref, sink_ref, o_ref, l_ref):
        g, n0 = pl.program_id(1), pl.program_id(2) * nbs
        sink_row = _sink_row(sink_ref, g)
        for j in range(nbs):
            n = n0 + j
            span = pl.ds(pl.multiple_of(n * BLOCK, BLOCK), SPAN)
            q = q_ref[0, :, j * BLOCK:(j + 1) * BLOCK, :].reshape(r, HEAD_DIM)
            st = _dot_nt(k_ref[0, 0, span, :], q) + bt_ref[_bias_variant(n, nb), 0]
            m = jnp.maximum(jnp.max(st, axis=0, keepdims=True), sink_row)
            acc = _dot(vt_ref[0, 0, :, span], jnp.exp(st - m).astype(BF16))
            l = acc[HEAD_DIM:HEAD_DIM + 1, :] + jnp.exp(sink_row - m)
            o = (acc[0:HEAD_DIM, :] / l).T
            for h in range(GROUP):
                o_ref[j * BLOCK:(j + 1) * BLOCK, h * HEAD_DIM:(h + 1) * HEAD_DIM] = (
                    o[h * BLOCK:(h + 1) * BLOCK].astype(BF16))
            l_ref[0, 0, j] = jnp.broadcast_to(m + jnp.log(l), (8, r))

    return pl.pallas_call(
        body,
        grid=(bl, kv, nb // nbs),
        in_specs=[pl.BlockSpec((1, GROUP, nbs * BLOCK, HEAD_DIM), lambda b, g, n: (b, g, n, 0)),
                  pl.BlockSpec((1, 1, sp, HEAD_DIM), lambda b, g, n: (b, g, 0, 0)),
                  pl.BlockSpec((1, 1, vt_rows, sp), lambda b, g, n: (b, g, 0, 0)),
                  pl.BlockSpec((3, 1, SPAN, r), lambda b, g, n: (0, g, 0, 0)),
                  pl.BlockSpec(memory_space=pltpu.SMEM)],
        out_specs=[pl.BlockSpec((nbs * BLOCK, GROUP * HEAD_DIM), lambda b, g, n: (b * (nb // nbs) + n, g)),
                   pl.BlockSpec((1, 1, nbs, 8, r), lambda b, g, n: (b, g, n, 0, 0))],
        out_shape=[SDS((bl * s_len, hb * HEAD_DIM), BF16), SDS((bl, kv, nb, 8, r), F32)],
        compiler_params=_cp("parallel", "parallel", "arbitrary"),
        name="attn_b_fwd",
    )(qb, kb, vbt, bias_t, sink)


def _mixout(oa, ob, wo, x2, g2, g3, tm):
    t, d = x2.shape
    ca = oa.shape[1]

    def body(oa_ref, ob_ref, w_ref, x_ref, g2_ref, g3_ref, mix_ref, x1_ref, h2_ref):
        mix = _dot(oa_ref[...], w_ref[0:ca, :]) + _dot(ob_ref[...], w_ref[ca:, :])
        mix_ref[...] = mix
        y2, _, _ = _rms_fwd(mix, g2_ref[...])
        x1 = x_ref[...] + y2
        x1_ref[...] = x1
        y3, _, _ = _rms_fwd(x1, g3_ref[...])
        h2_ref[...] = y3.astype(BF16)

    tile = lambda w: pl.BlockSpec((tm, w), lambda i: (i, 0))
    vec = pl.BlockSpec((1, d), lambda i: (0, 0))
    return pl.pallas_call(
        body,
        grid=(t // tm,),
        in_specs=[tile(ca), tile(ob.shape[1]), pl.BlockSpec(wo.shape, lambda i: (0, 0)), tile(d), vec, vec],
        out_specs=[tile(d), tile(d), tile(d)],
        out_shape=[SDS((t, d), F32), SDS((t, d), F32), SDS((t, d), BF16)],
        compiler_params=_cp("parallel"),
        name="mixout",
    )(oa, ob, wo, x2, g2, g3)


def _ffn_fwd(h2, wup_g, wdn, x1, target, g4, tm):
    t, d = x1.shape
    nj, _, tf = wup_g.shape
    ff = nj * tf
    nt = t // tm

    def body(h_ref, wu_ref, wd_ref, x1_ref, tg_ref, g_ref, u_ref, df_ref, dy_ref, dg_ref, loss_ref, acc_sc):
        i, j = pl.program_id(0), pl.program_id(1)

        @pl.when(j == 0)
        def _():
            acc_sc[...] = jnp.zeros_like(acc_sc)

        @pl.when((i == 0) & (j == 0))
        def _():
            dg_ref[...] = jnp.zeros_like(dg_ref)
            loss_ref[...] = jnp.zeros_like(loss_ref)

        u = jnp.maximum(_dot(h_ref[...], wu_ref[0]), 0.0)
        u_ref[...] = u.astype(BF16)
        acc_sc[...] += _dot((u * u).astype(BF16), wd_ref[...])

        @pl.when(j == nj - 1)
        def _():
            g = g_ref[...]
            y4, n, r = _rms_fwd(acc_sc[...], g)
            e = (x1_ref[...] + y4) - tg_ref[...]
            loss_ref[...] += jnp.sum(e * e) * (0.5 / d)
            dy = e * (1.0 / d)
            dy_ref[...] = dy
            df, dgt = _rms_bwd(n, r, g, dy)
            df_ref[...] = df.astype(BF16)
            dg_ref[0:1, :] += jnp.sum(dgt, axis=0, keepdims=True)

    tile = pl.BlockSpec((tm, d), lambda i, j: (i, 0))
    return pl.pallas_call(
        body,
        grid=(nt, nj),
        in_specs=[tile,
                  pl.BlockSpec((1, d, tf), lambda i, j: (j, 0, 0)),
                  pl.BlockSpec((tf, d), lambda i, j: (j, 0)),
                  tile, tile,
                  pl.BlockSpec((1, d), lambda i, j: (0, 0))],
        out_specs=[pl.BlockSpec((tm, tf), lambda i, j: (i, j)), tile, tile,
                   pl.BlockSpec((8, d), lambda i, j: (0, 0)),
                   pl.BlockSpec((8, 128), lambda i, j: (0, 0))],
        out_shape=[SDS((t, ff), BF16), SDS((t, d), BF16), SDS((t, d), F32), SDS((8, d), F32), SDS((8, 128), F32)],
        scratch_shapes=[pltpu.VMEM((tm, d), F32)],
        compiler_params=_cp("arbitrary", "arbitrary"),
        name="ffn_fwd",
    )(h2, wup_g, wdn, x1, target, g4)


def _ffn_bwd(df, u, wdn, wup_g, x1, dy, mix, g3, g2, tm):
    t, d = x1.shape
    nj, _, tf = wup_g.shape
    nt = t // tm

    def body(df_ref, u_ref, wd_ref, wu_ref, x1_ref, dy_ref, mix_ref, g3_ref, g2_ref,
             dpre_ref, dx1_ref, dmix_ref, dg3_ref, dg2_ref, acc_sc):
        i, j = pl.program_id(0), pl.program_id(1)

        @pl.when(j == 0)
        def _():
            acc_sc[...] = jnp.zeros_like(acc_sc)

        @pl.when((i == 0) & (j == 0))
        def _():
            dg3_ref[...] = jnp.zeros_like(dg3_ref)
            dg2_ref[...] = jnp.zeros_like(dg2_ref)

        du2 = _dot_nt(df_ref[...], wd_ref[...])
        dpre = (2.0 * u_ref[...].astype(F32) * du2).astype(BF16)
        dpre_ref[...] = dpre
        acc_sc[...] += _dot_nt(dpre, wu_ref[0])

        @pl.when(j == nj - 1)
        def _():
            g3, g2 = g3_ref[...], g2_ref[...]
            _, n3, r3 = _rms_fwd(x1_ref[...], g3)
            dx, dgt3 = _rms_bwd(n3, r3, g3, acc_sc[...])
            dx1 = dy_ref[...] + dx
            dx1_ref[...] = dx1
            dg3_ref[0:1, :] += jnp.sum(dgt3, axis=0, keepdims=True)
            _, n2, r2 = _rms_fwd(mix_ref[...], g2)
            dmix, dgt2 = _rms_bwd(n2, r2, g2, dx1)
            dmix_ref[...] = dmix.astype(BF16)
            dg2_ref[0:1, :] += jnp.sum(dgt2, axis=0, keepdims=True)

    tile = pl.BlockSpec((tm, d), lambda i, j: (i, 0))
    vec = pl.BlockSpec((1, d), lambda i, j: (0, 0))
    acc8 = pl.BlockSpec((8, d), lambda i, j: (0, 0))
    return pl.pallas_call(
        body,
        grid=(nt, nj),
        in_specs=[tile,
                  pl.BlockSpec((tm, tf), lambda i, j: (i, j)),
                  pl.BlockSpec((tf, d), lambda i, j: (j, 0)),
                  pl.BlockSpec((1, d, tf), lambda i, j: (j, 0, 0)),
                  tile, tile, tile, vec, vec],
        out_specs=[pl.BlockSpec((tm, tf), lambda i, j: (i, j)), tile, tile, acc8, acc8],
        out_shape=[SDS(u.shape, BF16), SDS((t, d), F32), SDS((t, d), BF16), SDS((8, d), F32), SDS((8, d), F32)],
        scratch_shapes=[pltpu.VMEM((tm, d), F32)],
        compiler_params=_cp("arbitrary", "arbitrary"),
        name="ffn_bwd",
    )(df, u, wdn, wup_g, x1, dy, mix, g3, g2)


def _wgrad_cols(a, b, nj, tt, name):
    t, m = a.shape
    n = b.shape[1]
    bn = n // nj

    def body(a_ref, b_ref, o_ref):
        @pl.when(pl.program_id(1) == 0)
        def _():
            o_ref[...] = jnp.zeros_like(o_ref)

        o_ref[0] += _dot_tn(a_ref[...], b_ref[...])

    return pl.pallas_call(
        body,
        grid=(nj, t // tt),
        in_specs=[pl.BlockSpec((tt, m), lambda j, k: (k, 0)),
                  pl.BlockSpec((tt, bn), lambda j, k: (k, j))],
        out_specs=pl.BlockSpec((1, m, bn), lambda j, k: (j, 0, 0)),
        out_shape=SDS((nj, m, bn), F32),
        compiler_params=_cp("parallel", "arbitrary"),
        name=name,
    )(a, b)


def _wgrad_cols_blocked(a, b3, tt, name):
    t, m = a.shape
    nj, _, bn = b3.shape

    def body(a_ref, b_ref, o_ref):
        @pl.when(pl.program_id(1) == 0)
        def _():
            o_ref[...] = jnp.zeros_like(o_ref)

        o_ref[0] += _dot_tn(a_ref[...], b_ref[0])

    return pl.pallas_call(
        body,
        grid=(nj, t // tt),
        in_specs=[pl.BlockSpec((tt, m), lambda j, k: (k, 0)),
                  pl.BlockSpec((1, tt, bn), lambda j, k: (j, k, 0))],
        out_specs=pl.BlockSpec((1, m, bn), lambda j, k: (j, 0, 0)),
        out_shape=SDS((nj, m, bn), F32),
        compiler_params=_cp("parallel", "arbitrary"),
        name=name,
    )(a, b3)


def _wgrad_rows(a_parts, b, nj, tt, square, name):
    t, n = b.shape
    widths = [p.shape[1] for p in a_parts]
    m = sum(widths)
    bm = m // nj
    per = [w // bm for w in widths]
    starts = [sum(per[:q]) for q in range(len(per))]
    np_ = len(a_parts)

    def body(*refs):
        a_refs, b_ref, o_ref = refs[:np_], refs[np_], refs[np_ + 1]
        j = pl.program_id(0)

        @pl.when(pl.program_id(1) == 0)
        def _():
            o_ref[...] = jnp.zeros_like(o_ref)

        for q in range(np_):
            @pl.when((j >= starts[q]) & (j < starts[q] + per[q]))
            def _(q=q):
                a = a_refs[q][...]
                if square:
                    af = a.astype(F32)
                    a = (af * af).astype(BF16)
                o_ref[0] += _dot_tn(a, b_ref[...])

    def a_spec(q):
        return pl.BlockSpec((tt, bm), lambda j, k: (k, jnp.clip(j - starts[q], 0, per[q] - 1)))

    return pl.pallas_call(
        body,
        grid=(nj, t // tt),
        in_specs=[a_spec(q) for q in range(np_)] + [pl.BlockSpec((tt, n), lambda j, k: (k, 0))],
        out_specs=pl.BlockSpec((1, bm, n), lambda j, k: (j, 0, 0)),
        out_shape=SDS((nj, bm, n), F32),
        compiler_params=_cp("parallel", "arbitrary"),
        name=name,
    )(*a_parts, b)


def _attn_out_bwd(dmix, wo, ca, tm):
    t, d = dmix.shape
    cb = wo.shape[0] - ca

    def body(dm_ref, w_ref, da_ref, db_ref):
        dm = dm_ref[...]
        da_ref[...] = _dot_nt(dm, w_ref[0:ca, :]).astype(BF16)
        db_ref[...] = _dot_nt(dm, w_ref[ca:, :]).astype(BF16)

    return pl.pallas_call(
        body,
        grid=(t // tm,),
        in_specs=[pl.BlockSpec((tm, d), lambda i: (i, 0)), pl.BlockSpec(wo.shape, lambda i: (0, 0))],
        out_specs=[pl.BlockSpec((tm, ca), lambda i: (i, 0)), pl.BlockSpec((tm, cb), lambda i: (i, 0))],
        out_shape=[SDS((t, ca), BF16), SDS((t, cb), BF16)],
        compiler_params=_cp("parallel"),
        name="attn_out_bwd",
    )(dmix, wo)


def _stack_heads(ref, rows):
    return jnp.concatenate([ref[:, h * HEAD_DIM:(h + 1) * HEAD_DIM] for h in range(GROUP)], axis=0)


def _attn_a_bwd(qa, ka, kat, va, do, o, lse, tq, tk, grads):
    bl, ha, s_len, _ = qa.shape
    kv = ka.shape[1]
    nq, nk = s_len // tq, s_len // tk
    assert nk % 2 == 0
    r = GROUP * tq
    ng = len(grads)

    def body(q_ref, k_ref, kt_ref, v_ref, do_ref, o_ref, l_ref, *rest):
        grad_refs, (dq_ref, dk_ref, dv_ref), parts = rest[:ng], rest[ng:ng + 3], rest[ng + 3:2 * ng + 3]
        st_sc, dp_sc, dkt_sc, dvt_sc, send_sems, recv_sems, local_sems = rest[2 * ng + 3:]
        i = pl.program_id(2)
        step_id = (pl.program_id(0) * kv + pl.program_id(1)) * nq + i
        start, wait = _direct_exchange("scatter", grad_refs, parts, send_sems, recv_sems, local_sems)
        pl.when(step_id == 0)(start)

        q = q_ref[0].reshape(r, HEAD_DIM)
        do2 = _stack_heads(do_ref, tq)
        qt = q.astype(F32).T
        dot32 = do2.astype(F32).T
        ot32 = _stack_heads(o_ref, tq).astype(F32).T
        drow = jnp.sum(dot32 * ot32, axis=0, keepdims=True)
        qt, dot = qt.astype(BF16), dot32.astype(BF16)
        lrow = l_ref[0, 0, 0, 0:1, :]

        @pl.when(i == 0)
        def _():
            dkt_sc[...] = jnp.zeros_like(dkt_sc)
            dvt_sc[...] = jnp.zeros_like(dvt_sc)

        def chunk(c):
            return pl.ds(pl.multiple_of(c * tk, tk), tk)

        def scores(c, slot):
            st_sc[slot] = _dot_nt(k_ref[0, 0, chunk(c), :], q)
            dp_sc[slot] = _dot_nt(v_ref[0, 0, chunk(c), :], do2)

        def fold(slot, c, dqt):
            pt = jnp.exp(st_sc[slot] - lrow)
            dsb = (pt * (dp_sc[slot] - drow)).astype(BF16)
            dvt_sc[:, chunk(c)] += _dot_nt(dot, pt.astype(BF16))
            dkt_sc[:, chunk(c)] += _dot_nt(qt, dsb)
            return dqt + _dot(kt_ref[0, 0, :, chunk(c)], dsb)

        scores(0, 0)

        def step(c2, dqt):
            c = 2 * c2
            scores(c + 1, 1)
            dqt = fold(0, c, dqt)
            scores(jnp.minimum(c + 2, nk - 1), 0)
            return fold(1, c + 1, dqt)

        dqt = lax.fori_loop(0, nk // 2, step, jnp.zeros((HEAD_DIM, r), F32))
        dq_ref[0] = dqt.T.reshape(GROUP, tq, HEAD_DIM)

        @pl.when(i == nq - 1)
        def _():
            dk_ref[0, 0] = dkt_sc[...].T
            dv_ref[0, 0] = dvt_sc[...].T

        pl.when(step_id == bl * kv * nq - 1)(wait)

    kvspec = pl.BlockSpec((1, 1, s_len, HEAD_DIM), lambda b, g, i: (b, g, 0, 0))
    qspec = pl.BlockSpec((1, GROUP, tq, HEAD_DIM), lambda b, g, i: (b, g, i, 0))
    tok = pl.BlockSpec((tq, GROUP * HEAD_DIM), lambda b, g, i: (b * nq + i, g))
    anyspec = pl.BlockSpec(memory_space=pl.ANY)
    res = pl.pallas_call(
        body,
        grid=(bl, kv, nq),
        in_specs=[qspec, kvspec, pl.BlockSpec((1, 1, HEAD_DIM, s_len), lambda b, g, i: (b, g, 0, 0)), kvspec,
                  tok, tok, pl.BlockSpec((1, 1, 1, 8, r), lambda b, g, i: (b, g, i, 0, 0))] + [anyspec] * ng,
        out_specs=[qspec, kvspec, kvspec] + [anyspec] * ng,
        out_shape=[SDS(qa.shape, F32), SDS(ka.shape, F32), SDS(va.shape, F32)]
        + [SDS(g.shape, g.dtype) for g in grads],
        scratch_shapes=[pltpu.VMEM((2, tk, r), F32), pltpu.VMEM((2, tk, r), F32),
                        pltpu.VMEM((HEAD_DIM, s_len), F32), pltpu.VMEM((HEAD_DIM, s_len), F32)]
        + _exchange_scratch(ng),
        compiler_params=_cp("arbitrary", "arbitrary", "arbitrary"),
        name="attn_a_bwd",
    )(qa, ka, kat, va, do, o, lse, *grads)
    return res[0], res[1], res[2], res[3:]


def _attn_b_bwd(qb, kb, kbt, vb, do, o, lse, bias_t, sink, s_len):
    bl, hb, _, _ = qb.shape
    kv, sp = kb.shape[1], kb.shape[2]
    nb = s_len // BLOCK
    nbs = min(QB_PER_STEP, nb)
    r = GROUP * BLOCK

    def body(q_ref, k_ref, kt_ref, v_ref, do_ref, o_ref, l_ref, bt_ref, sink_ref,
             dq_ref, dk_ref, dv_ref, dsum_ref, dsink_ref, dkt_sc, dvt_sc):
        g, b, ns = pl.program_id(0), pl.program_id(1), pl.program_id(2)
        sink_row = _sink_row(sink_ref, g)

        @pl.when(ns == 0)
        def _():
            dkt_sc[...] = jnp.zeros_like(dkt_sc)
            dvt_sc[...] = jnp.zeros_like(dvt_sc)

        @pl.when((b == 0) & (ns == 0))
        def _():
            dsum_ref[...] = jnp.zeros_like(dsum_ref)
            dsink_ref[...] = jnp.zeros_like(dsink_ref)

        dsum = jnp.zeros((SPAN, r), F32)
        dsink = jnp.zeros((1, r), F32)
        for j in range(nbs):
            n = ns * nbs + j
            span = pl.ds(pl.multiple_of(n * BLOCK, BLOCK), SPAN)
            rows = slice(j * BLOCK, (j + 1) * BLOCK)
            q = q_ref[0, :, rows, :].reshape(r, HEAD_DIM)
            do2 = jnp.concatenate([do_ref[rows, h * HEAD_DIM:(h + 1) * HEAD_DIM] for h in range(GROUP)], axis=0)
            o2 = jnp.concatenate([o_ref[rows, h * HEAD_DIM:(h + 1) * HEAD_DIM] for h in range(GROUP)], axis=0)
            dot32 = do2.astype(F32).T
            drow = jnp.sum(dot32 * o2.astype(F32).T, axis=0, keepdims=True)
            qt, dot = q.astype(F32).T.astype(BF16), dot32.astype(BF16)
            lrow = l_ref[0, 0, j, 0:1, :]
            st = _dot_nt(k_ref[0, 0, span, :], q) + bt_ref[_bias_variant(n, nb), 0]
            pt = jnp.exp(st - lrow)
            dst = pt * (_dot_nt(v_ref[0, 0, span, :], do2) - drow)
            dsum = dsum + dst
            dsink = dsink - jnp.exp(sink_row - lrow) * drow
            dsb = dst.astype(BF16)
            dvt_sc[:, span] += _dot_nt(dot, pt.astype(BF16))
            dkt_sc[:, span] += _dot_nt(qt, dsb)
            dq_ref[0, :, rows, :] = _dot(kt_ref[0, 0, :, span], dsb).T.reshape(GROUP, BLOCK, HEAD_DIM)
        dsum_ref[0] += dsum
        dsink_ref[0, 0:1, :] += dsink

        @pl.when(ns == nb // nbs - 1)
        def _():
            dk_ref[0, 0] = dkt_sc[:, BLOCK:BLOCK + s_len].T
            dv_ref[0, 0] = dvt_sc[:, BLOCK:BLOCK + s_len].T

    kvspec = pl.BlockSpec((1, 1, sp, HEAD_DIM), lambda g, b, n: (b, g, 0, 0))
    kvout = pl.BlockSpec((1, 1, s_len, HEAD_DIM), lambda g, b, n: (b, g, 0, 0))
    qspec = pl.BlockSpec((1, GROUP, nbs * BLOCK, HEAD_DIM), lambda g, b, n: (b, g, n, 0))
    tok = pl.BlockSpec((nbs * BLOCK, GROUP * HEAD_DIM), lambda g, b, n: (b * (nb // nbs) + n, g))
    return pl.pallas_call(
        body,
        grid=(kv, bl, nb // nbs),
        in_specs=[qspec, kvspec, pl.BlockSpec((1, 1, HEAD_DIM, sp), lambda g, b, n: (b, g, 0, 0)), kvspec, tok, tok,
                  pl.BlockSpec((1, 1, nbs, 8, r), lambda g, b, n: (b, g, n, 0, 0)),
                  pl.BlockSpec((3, 1, SPAN, r), lambda g, b, n: (0, g, 0, 0)),
                  pl.BlockSpec(memory_space=pltpu.SMEM)],
        out_specs=[qspec, kvout, kvout,
                   pl.BlockSpec((1, SPAN, r), lambda g, b, n: (g, 0, 0)),
                   pl.BlockSpec((1, 8, r), lambda g, b, n: (g, 0, 0))],
        out_shape=[SDS(qb.shape, F32), SDS((bl, kv, s_len, HEAD_DIM), F32), SDS((bl, kv, s_len, HEAD_DIM), F32),
                   SDS((kv, SPAN, r), F32), SDS((kv, 8, r), F32)],
        scratch_shapes=[pltpu.VMEM((HEAD_DIM, sp), F32), pltpu.VMEM((HEAD_DIM, sp), F32)],
        compiler_params=_cp("arbitrary", "arbitrary", "arbitrary"),
        name="attn_b_bwd",
    )(qb, kb, kbt, vb, do, o, lse, bias_t, sink)


def _bias_reduce(dsum, dsink, bucket_t4):
    kv, _, r = dsum.shape

    def body(ds_ref, dk_ref, bk_ref, rel_ref, sink_ref):
        lane = lax.broadcasted_iota(jnp.int32, (N_BUCKETS, 128), 1)
        lane8 = lax.broadcasted_iota(jnp.int32, (8, 128), 1)
        bk = bk_ref[...]
        for g in range(kv):
            ds = ds_ref[g]
            rowi = lax.broadcasted_iota(jnp.int32, (N_BUCKETS, r), 0)
            red = jnp.zeros((N_BUCKETS, r), F32)
            for b in range(N_BUCKETS):
                red = jnp.where(rowi == b, jnp.sum(jnp.where(bk == b, ds, 0.0), axis=0, keepdims=True), red)
            out = jnp.zeros((N_BUCKETS, 128), F32)
            so = jnp.zeros((8, 128), F32)
            for h in range(GROUP):
                col = jnp.sum(red[:, h * BLOCK:(h + 1) * BLOCK], axis=1, keepdims=True)
                out = jnp.where(lane == h, col, out)
                sc = jnp.sum(dk_ref[g][:, h * BLOCK:(h + 1) * BLOCK], axis=1, keepdims=True)
                so = jnp.where(lane8 == h, sc, so)
            rel_ref[g] = out
            sink_ref[g] = so

    vm = pl.BlockSpec(memory_space=pltpu.VMEM)
    return pl.pallas_call(
        body,
        in_specs=[vm, vm, vm],
        out_specs=[vm, vm],
        out_shape=[SDS((kv, N_BUCKETS, 128), F32), SDS((kv, 8, 128), F32)],
        name="bias_reduce",
    )(dsum, dsink, bucket_t4)


def _dqkprep(dqa, dka, dva, dqb, dkb, dvb, proj, cos, sin_signed, gq, gk, s_len, ts):
    nd, t, pb = proj.shape
    bl, ha = dqa.shape[0], dqa.shape[1]
    kva, hb, kvb = dka.shape[1], dqb.shape[1], dkb.shape[1]
    hpb = pb // HEAD_DIM
    ns = s_len // ts

    def body(dqa_ref, dka_ref, dva_ref, dqb_ref, dkb_ref, dvb_ref, p_ref, cos_ref, sin_ref, gq_ref, gk_ref,
             dp_ref, dgq_ref, dgk_ref):
        b, i = pl.program_id(0), pl.program_id(1)
        cs, sn = cos_ref[...], sin_ref[...]
        lane = lax.broadcasted_iota(jnp.int32, (ts, HEAD_DIM), 1)
        first = (lane % 32) < 16

        @pl.when((b == 0) & (i == 0))
        def _():
            dgq_ref[...] = jnp.zeros_like(dgq_ref)
            dgk_ref[...] = jnp.zeros_like(dgk_ref)

        def put(hh, val):
            dp_ref[hh // hpb, :, pl.ds((hh % hpb) * HEAD_DIM, HEAD_DIM)] = val.astype(BF16)

        def unrope_norm(d_rot, hh, g, dg_ref):
            ds = d_rot * sn
            dn = d_rot * cs + jnp.where(first, jnp.roll(ds, -16, axis=1), jnp.roll(ds, 16, axis=1))
            _, n, r = _rms_fwd(_head_slice(p_ref, hh, hpb), g)
            dx, dgt = _rms_bwd(n, r, g, dn)
            dg_ref[0:1, :] += jnp.sum(dgt, axis=0, keepdims=True)
            put(hh, dx)

        for h in range(ha):
            unrope_norm(dqa_ref[0, h] * SCALE, h, gq_ref[...], dgq_ref)
        for h in range(kva):
            unrope_norm(dka_ref[0, h], ha + h, gk_ref[...], dgk_ref)
            put(ha + kva + h, dva_ref[0, h])
        base = ha + 2 * kva
        for h in range(hb):
            put(base + h, dqb_ref[0, h] * SCALE)
        for h in range(kvb):
            put(base + hb + h, dkb_ref[0, h])
            put(base + hb + kvb + h, dvb_ref[0, h])

    def hm(nh):
        return pl.BlockSpec((1, nh, ts, HEAD_DIM), lambda b, i: (b, 0, i, 0))

    vec = pl.BlockSpec((1, HEAD_DIM), lambda b, i: (0, 0))
    tab = pl.BlockSpec((ts, HEAD_DIM), lambda b, i: (i, 0))
    acc = pl.BlockSpec((8, HEAD_DIM), lambda b, i: (0, 0))
    pspec = pl.BlockSpec((nd, ts, pb), lambda b, i: (0, b * ns + i, 0))
    return pl.pallas_call(
        body,
        grid=(bl, ns),
        in_specs=[hm(ha), hm(kva), hm(kva), hm(hb), hm(kvb), hm(kvb), pspec, tab, tab, vec, vec],
        out_specs=[pspec, acc, acc],
        out_shape=[SDS((nd, t, pb), BF16), SDS((8, HEAD_DIM), F32), SDS((8, HEAD_DIM), F32)],
        compiler_params=_cp("arbitrary", "arbitrary"),
        name="dqkprep",
    )(dqa, dka, dva, dqb, dkb, dvb, proj, cos, sin_signed, gq, gk)


def _dx_final(dproj, win_g, x2, dx1, g1, tm):
    t, d = x2.shape
    nd, _, pb = win_g.shape

    def body(dp_ref, w_ref, x_ref, dx1_ref, g_ref, dx_ref, dg_ref):
        @pl.when(pl.program_id(0) == 0)
        def _():
            dg_ref[...] = jnp.zeros_like(dg_ref)

        dh = _dot_nt(dp_ref[0], w_ref[0])
        for j in range(1, nd):
            dh = dh + _dot_nt(dp_ref[j], w_ref[j])
        g = g_ref[...]
        _, n, r = _rms_fwd(x_ref[...], g)
        dx, dgt = _rms_bwd(n, r, g, dh)
        dx_ref[...] = dx1_ref[...] + dx
        dg_ref[0:1, :] += jnp.sum(dgt, axis=0, keepdims=True)

    tile = pl.BlockSpec((tm, d), lambda i: (i, 0))
    return pl.pallas_call(
        body,
        grid=(t // tm,),
        in_specs=[pl.BlockSpec((nd, tm, pb), lambda i: (0, i, 0)),
                  pl.BlockSpec((nd, d, pb), lambda i: (0, 0, 0)),
                  tile, tile, pl.BlockSpec((1, d), lambda i: (0, 0))],
        out_specs=[tile, pl.BlockSpec((8, d), lambda i: (0, 0))],
        out_shape=[SDS((t, d), F32), SDS((8, d), F32)],
        compiler_params=_cp("arbitrary"),
        name="dx_final",
    )(dproj, win_g, x2, dx1, g1)


def _adamw_math(w, g, m, v):
    m = ADAM_B1 * m + (1.0 - ADAM_B1) * g
    v = ADAM_B2 * v + (1.0 - ADAM_B2) * (g * g)
    m_hat = m / (1.0 - ADAM_B1 ** ADAM_STEP)
    v_hat = v / (1.0 - ADAM_B2 ** ADAM_STEP)
    delta = -ADAM_LR * (m_hat / (jnp.sqrt(v_hat) + ADAM_EPS) + ADAM_WD * w)
    return delta, m, v


def _adamw_sum(parts, w, m, v, tr, name):
    rows, cols = w.shape

    def body(p_ref, w_ref, m_ref, v_ref, g_ref, d_ref, nm_ref, nv_ref):
        g = p_ref[0]
        for s in range(1, N_DEV):
            g = g + p_ref[s]
        g_ref[...] = g
        d_ref[...], nm_ref[...], nv_ref[...] = _adamw_math(w_ref[...], g, m_ref[...], v_ref[...])

    tr = min(tr, rows)
    tile = pl.BlockSpec((tr, cols), lambda i: (i, 0))
    return pl.pallas_call(
        body,
        grid=(rows // tr,),
        in_specs=[pl.BlockSpec((N_DEV, tr, cols), lambda i: (0, i, 0)), tile, tile, tile],
        out_specs=[tile] * 4,
        out_shape=[SDS((rows, cols), F32)] * 4,
        compiler_params=_cp("parallel"),
        name=name,
    )(parts, w, m, v)


def _adamw_small(g, w, m, v):
    def body(g_ref, w_ref, m_ref, v_ref, d_ref, nm_ref, nv_ref):
        d_ref[...], nm_ref[...], nv_ref[...] = _adamw_math(w_ref[...], g_ref[...], m_ref[...], v_ref[...])

    vm = pl.BlockSpec(memory_space=pltpu.VMEM)
    return pl.pallas_call(
        body,
        in_specs=[vm] * 4,
        out_specs=[vm] * 3,
        out_shape=[SDS(w.shape, F32)] * 3,
        name="adamw_small",
    )(g, w, m, v)


SMALL_ROWS = 8
SMALL_COLS = 1024


def _pack_small(g1, g2, g3, g4, gq, gk, sink, rel):
    row4 = jnp.concatenate([gq.reshape(-1), gk.reshape(-1), sink.reshape(-1)])
    row4 = jnp.pad(row4, (0, SMALL_COLS - row4.shape[0]))
    row5 = jnp.pad(rel.reshape(-1), (0, SMALL_COLS - rel.size))
    zero = jnp.zeros((SMALL_COLS,), F32)
    return jnp.stack([g1.reshape(-1), g2.reshape(-1), g3.reshape(-1), g4.reshape(-1), row4, row5, zero, zero])


def _unpack_small(p, hb):
    hd = HEAD_DIM
    return (p[0:1], p[1:2], p[2:3], p[3:4], p[4:5, 0:hd], p[4:5, hd:2 * hd], p[4:5, 2 * hd:2 * hd + hb],
            p[5, 0:N_BUCKETS * hb].reshape(N_BUCKETS, hb))


def _local_step(x, loss_target, win_g, wo_s, wup_s, wdn_s, g_pre_mix, g_post_mix, q_norm_a, k_norm_a, sink_b,
                rel_bias, g_pre_ffn, g_post_ffn):
    bl, s_len, d = x.shape
    t = bl * s_len
    nh = d // HEAD_DIM
    ha = nh // 2
    kva = ha // GROUP
    hb = nh - ha
    kvb = hb // GROUP
    tm = 512
    tw = min(2048, t)
    ts = min(512, s_len)
    tq, tk = BLOCK, min(512, s_len // 2)

    x2 = x.reshape(t, d)
    tg2 = loss_target.reshape(t, d)
    cos, sin_signed = _rope_tables(s_len)
    a = jnp.arange(BLOCK, dtype=jnp.int32)
    c = jnp.arange(SPAN, dtype=jnp.int32)
    bucket_t = _t5_bucket(c[:, None] - BLOCK - a[None, :])
    bucket_t4 = jnp.tile(bucket_t, (1, GROUP))

    h1, proj = _inproj(x2, g_pre_mix, win_g, tm)
    qa, ka, kat, va, vat, qb, kb, kbt, vb, vbt = _qkprep(
        proj, cos, sin_signed, q_norm_a, k_norm_a, bl, s_len, ha, kva, hb, kvb, ts)
    bias_t = _bias_build(bucket_t, rel_bias, hb)
    oa, lse_a, (wo_g, wup_g, wdn_g) = _attn_a_fwd(qa, ka, vat, tq, tk, [wo_s, wup_s, wdn_s])
    wo = wo_g.reshape(-1, d)
    wdn = wdn_g.reshape(-1, d)
    ob, lse_b = _attn_b_fwd(qb, kb, vbt, bias_t, sink_b, s_len)
    mix, x1, h2 = _mixout(oa, ob, wo, x2, g_post_mix, g_pre_ffn, tm)
    u, df, dy, dg4, loss8 = _ffn_fwd(h2, wup_g, wdn, x1, tg2, g_post_ffn, tm)

    dpre, dx1, dmix, dg3, dg2 = _ffn_bwd(df, u, wdn, wup_g, x1, dy, mix, g_pre_ffn, g_post_mix, tm)
    gw_dn = _wgrad_rows([u], df, N_DEV, tw, True, "wgrad_down")
    gw_up = _wgrad_cols(h2, dpre, N_DEV, tw, "wgrad_up")
    gw_o = _wgrad_rows([oa, ob], dmix, N_DEV, tw, False, "wgrad_o")
    doa, dob = _attn_out_bwd(dmix, wo, oa.shape[1], tm)
    dqa, dka, dva, (p_o, p_up, p_dn) = _attn_a_bwd(qa, ka, kat, va, doa, oa, lse_a, tq, tk, [gw_o, gw_up, gw_dn])
    dqb, dkb, dvb, dsum, dsink = _attn_b_bwd(qb, kb, kbt, vb, dob, ob, lse_b, bias_t, sink_b, s_len)
    drel_g, dsink_g = _bias_reduce(dsum, dsink, bucket_t4)
    dproj, dgq, dgk = _dqkprep(dqa, dka, dva, dqb, dkb, dvb, proj, cos, sin_signed, q_norm_a, k_norm_a, s_len, ts)
    gw_in = _wgrad_cols_blocked(h1, dproj, tw, "wgrad_in")
    grad_x, dg1 = _dx_final(dproj, win_g, x2, dx1, g_pre_mix, tm)

    drel = jnp.transpose(drel_g[:, :, 0:GROUP], (1, 0, 2)).reshape(N_BUCKETS, hb)
    dsink_v = dsink_g[:, 0, 0:GROUP].reshape(1, hb)
    small = _pack_small(dg1[0], dg2[0], dg3[0], dg4[0], dgq[0], dgk[0], dsink_v, drel)
    small = small.at[6, 0].set(loss8[0, 0])
    return grad_x.reshape(bl, s_len, d), gw_in, p_o, p_up, p_dn, small


def kernel(x, w_in, w_o, g_pre_mix, g_post_mix, q_norm_a, k_norm_a, sink_b, rel_bias, g_pre_ffn, w_ffn_up, w_ffn_down, g_post_ffn, loss_target, m_w_in, m_w_o, m_g_pre_mix, m_g_post_mix, m_q_norm_a, m_k_norm_a, m_sink_b, m_rel_bias, m_g_pre_ffn, m_w_ffn_up, m_w_ffn_down, m_g_post_ffn, v_w_in, v_w_o, v_g_pre_mix, v_g_post_mix, v_q_norm_a, v_k_norm_a, v_sink_b, v_rel_bias, v_g_pre_ffn, v_w_ffn_up, v_w_ffn_down, v_g_post_ffn):
    hb = sink_b.shape[1]
    d = x.shape[-1]
    (win_g,) = _weight_gather([w_in[0].astype(BF16)])

    grad_x, gw_in, p_o, p_up, p_dn, small = _local_step(
        x, loss_target, win_g, w_o[0].astype(BF16), w_ffn_up[0].astype(BF16), w_ffn_down[0].astype(BF16),
        g_pre_mix, g_post_mix, q_norm_a, k_norm_a, sink_b, rel_bias, g_pre_ffn, g_post_ffn)

    (p_in,) = _grad_exchange([gw_in])
    small = _small_allreduce(small)

    g_in, d_in, nm_in, nv_in = _adamw_sum(p_in, w_in[0], m_w_in[0], v_w_in[0], 256, "adamw_in")
    g_o, d_o, nm_o, nv_o = _adamw_sum(p_o, w_o[0], m_w_o[0], v_w_o[0], 128, "adamw_o")
    g_up, d_up, nm_up, nv_up = _adamw_sum(p_up, w_ffn_up[0], m_w_ffn_up[0], v_w_ffn_up[0], 256, "adamw_up")
    g_dn, d_dn, nm_dn, nv_dn = _adamw_sum(p_dn, w_ffn_down[0], m_w_ffn_down[0], v_w_ffn_down[0], 256, "adamw_down")

    pack = lambda *a: _pack_small(*a)
    w_s = pack(g_pre_mix, g_post_mix, g_pre_ffn, g_post_ffn, q_norm_a, k_norm_a, sink_b, rel_bias)
    m_s = pack(m_g_pre_mix, m_g_post_mix, m_g_pre_ffn, m_g_post_ffn, m_q_norm_a, m_k_norm_a, m_sink_b, m_rel_bias)
    v_s = pack(v_g_pre_mix, v_g_post_mix, v_g_pre_ffn, v_g_post_ffn, v_q_norm_a, v_k_norm_a, v_sink_b, v_rel_bias)
    d_s, nm_s, nv_s = _adamw_small(small, w_s, m_s, v_s)

    loss = small[6, 0]

    def outs(big_in, big_o, sm, big_up, big_dn):
        s1, s2, s3, s4, sq, sk, ss, sr = _unpack_small(sm, hb)
        return [big_in[None], big_o[None], s1, s2, sq, sk, ss, sr, s3, big_up[None], big_dn[None], s4]

    return (loss, grad_x,
            *outs(g_in, g_o, small, g_up, g_dn),
            *outs(d_in, d_o, d_s, d_up, d_dn),
            *outs(nm_in, nm_o, nm_s, nm_up, nm_dn),
            *outs(nv_in, nv_o, nv_s, nv_up, nv_dn))
```

```python
import functools

import jax
import jax.numpy as jnp
import numpy as np
from jax import lax
from jax.experimental import pallas as pl
from jax.experimental.pallas import tpu as pltpu

F32 = jnp.float32
BF16 = jnp.bfloat16
SDS = jax.ShapeDtypeStruct

N_DEV = 8
HEAD_DIM = 64
GROUP = 4
BLOCK = 128
SPAN = 3 * BLOCK
GRID_W = 64
N_BUCKETS = 32
MAX_DISTANCE = 128
ROPE_THETA = 10000.0
EPS = 1e-6
NEG_INF = -1e30
SCALE = HEAD_DIM ** -0.5
VT_PAD = 16

ADAM_LR = 0.001
ADAM_B1 = 0.9
ADAM_B2 = 0.999
ADAM_EPS = 1e-08
ADAM_WD = 0.01
ADAM_STEP = 10

VMEM_LIMIT = 56 * 1024 * 1024
MESH = pl.DeviceIdType.MESH


def _cp(*sem):
    return pltpu.CompilerParams(dimension_semantics=sem, vmem_limit_bytes=VMEM_LIMIT)


def _dot(a, b):
    return jnp.dot(a, b, preferred_element_type=F32)


def _dot_nt(a, b):
    return lax.dot_general(a, b, (((1,), (1,)), ((), ())), preferred_element_type=F32)


def _dot_tn(a, b):
    return lax.dot_general(a, b, (((0,), (0,)), ((), ())), preferred_element_type=F32)


def _rms_fwd(x, g):
    r = lax.rsqrt(jnp.mean(x * x, axis=-1, keepdims=True) + EPS)
    n = x * r
    return n * g, n, r


def _rms_bwd(n, r, g, dy):
    gd = g * dy
    dx = r * (gd - n * jnp.mean(n * gd, axis=-1, keepdims=True))
    return dx, dy * n


def _rope(n, cos, sin_signed, first):
    partner = jnp.where(first, jnp.roll(n, -16, axis=1), jnp.roll(n, 16, axis=1))
    return n * cos + partner * sin_signed


def _rope_tables(s_len):
    rows = s_len // GRID_W
    row = jnp.repeat(jnp.arange(rows, dtype=jnp.int32), GRID_W)
    col = jnp.tile(jnp.arange(GRID_W, dtype=jnp.int32), rows)
    nf = HEAD_DIM // 4
    freqs = ROPE_THETA ** (-jnp.arange(nf, dtype=F32) / nf)
    ang_r = row.astype(F32)[:, None] * freqs[None, :]
    ang_c = col.astype(F32)[:, None] * freqs[None, :]
    cr, sr, cc, sc = jnp.cos(ang_r), jnp.sin(ang_r), jnp.cos(ang_c), jnp.sin(ang_c)
    cos = jnp.concatenate([cr, cr, cc, cc], axis=-1)
    sin_signed = jnp.concatenate([-sr, sr, -sc, sc], axis=-1)
    return cos, sin_signed


def _t5_bucket(rel):
    nb = N_BUCKETS // 2
    ret = (rel > 0).astype(jnp.int32) * nb
    n = jnp.abs(rel)
    max_exact = nb // 2
    nf = jnp.maximum(n, 1).astype(F32)
    large = max_exact + (jnp.log(nf / max_exact) / np.float32(np.log(MAX_DISTANCE / max_exact))
                         * (nb - max_exact)).astype(jnp.int32)
    large = jnp.minimum(large, nb - 1)
    return ret + jnp.where(n < max_exact, n, large)


def _mesh_pos():
    return lax.axis_index("x"), lax.axis_index("y"), lax.axis_index("c")


def _lin(p):
    return 4 * p[0] + 2 * p[1] + p[2]


def _weight_gather(shards):
    n = len(shards)

    def body(*refs):
        xs, outs = refs[:n], refs[n:2 * n]
        send_sems, recv_sems, local_sems = refs[2 * n:]
        x, y, c = _mesh_pos()
        me, sibling = (x, y, c), (x, y, 1 - c)
        chips = [(1 - x, y), (x, 1 - y), (1 - x, 1 - y)]

        def copy(a, k, block, to, src=None):
            slot = outs[a].at[_lin(block)]
            return pltpu.make_async_remote_copy(
                src_ref=slot if src is None else src, dst_ref=slot,
                send_sem=send_sems.at[a, k], recv_sem=recv_sems.at[a, k],
                device_id=to, device_id_type=MESH)

        started = []
        for a in range(n):
            mine = pltpu.make_async_copy(xs[a], outs[a].at[_lin(me)], local_sems.at[a])
            mine.start()
            started.append(mine)
        sends = []
        for a in range(n):
            first = [copy(a, 0, me, sibling, src=xs[a])]
            first += [copy(a, 1 + j, me, (*chip, c), src=xs[a]) for j, chip in enumerate(chips)]
            for cp in first:
                cp.start()
            sends += first
        for a in range(n):
            for j, chip in enumerate(chips):
                copy(a, 1 + j, (*chip, c), me).wait_recv()
                fwd = copy(a, 4 + j, (*chip, c), sibling)
                fwd.start()
                sends.append(fwd)
        for a in range(n):
            copy(a, 0, sibling, me).wait_recv()
            for j, chip in enumerate(chips):
                copy(a, 4 + j, (*chip, 1 - c), me).wait_recv()
        for cp in sends:
            cp.wait_send()
        for mine in started:
            mine.wait()

    anyspec = pl.BlockSpec(memory_space=pl.ANY)
    return pl.pallas_call(
        body,
        out_shape=[SDS((N_DEV,) + s.shape, s.dtype) for s in shards],
        in_specs=[anyspec] * n,
        out_specs=[anyspec] * n,
        scratch_shapes=[pltpu.SemaphoreType.DMA((n, 7)), pltpu.SemaphoreType.DMA((n, 7)),
                        pltpu.SemaphoreType.DMA((n,))],
        name="weight_gather",
    )(*shards)


def _direct_exchange(kind, ins, outs, send_sems, recv_sems, local_sems):
    x, y, c = _mesh_pos()
    me = (x, y, c)
    peers = [(x, y, 1 - c), (1 - x, y, c), (x, 1 - y, c), (1 - x, 1 - y, c),
             (1 - x, y, 1 - c), (x, 1 - y, 1 - c), (1 - x, 1 - y, 1 - c)]

    def src(a, to):
        return ins[a] if kind == "gather" else ins[a].at[_lin(to)]

    def remote(a, k, to, frm):
        return pltpu.make_async_remote_copy(
            src_ref=src(a, to), dst_ref=outs[a].at[_lin(frm)],
            send_sem=send_sems.at[a, k], recv_sem=recv_sems.at[a, k],
            device_id=to, device_id_type=MESH)

    n = len(ins)
    sends = [remote(a, k, p, me) for a in range(n) for k, p in enumerate(peers)]
    arrivals = [remote(a, k, p, p) for a in range(n) for k, p in enumerate(peers)]
    local = [pltpu.make_async_copy(src(a, me), outs[a].at[_lin(me)], local_sems.at[a]) for a in range(n)]

    def start():
        for cp in local + sends:
            cp.start()

    def wait():
        for cp in arrivals:
            cp.wait_recv()
        for cp in sends:
            cp.wait_send()
        for cp in local:
            cp.wait()

    return start, wait


def _exchange_scratch(n):
    return [pltpu.SemaphoreType.DMA((n, 7)), pltpu.SemaphoreType.DMA((n, 7)), pltpu.SemaphoreType.DMA((n,))]


def _grad_exchange(grads):
    n = len(grads)

    def body(*refs):
        start, wait = _direct_exchange("scatter", refs[:n], refs[n:2 * n], *refs[2 * n:])
        start()
        wait()

    anyspec = pl.BlockSpec(memory_space=pl.ANY)
    return pl.pallas_call(
        body,
        out_shape=[SDS(g.shape, g.dtype) for g in grads],
        in_specs=[anyspec] * n,
        out_specs=[anyspec] * n,
        scratch_shapes=_exchange_scratch(n),
        name="grad_exchange",
    )(*grads)


def _small_allreduce(v):
    rows, cols = v.shape

    def body(v_ref, out_ref, land_ref, send_sems, recv_sems):
        x, y, c = _mesh_pos()
        me = (x, y, c)
        peers = [(x, y, 1 - c), (1 - x, y, c), (x, 1 - y, c), (1 - x, 1 - y, c),
                 (1 - x, y, 1 - c), (x, 1 - y, 1 - c), (1 - x, 1 - y, 1 - c)]

        def copy(k, to, frm):
            return pltpu.make_async_remote_copy(
                src_ref=v_ref, dst_ref=land_ref.at[_lin(frm)],
                send_sem=send_sems.at[k], recv_sem=recv_sems.at[k],
                device_id=to, device_id_type=MESH)

        sends = [copy(k, p, me) for k, p in enumerate(peers)]
        for cp in sends:
            cp.start()
        land_ref[_lin(me)] = v_ref[...]
        for k, p in enumerate(peers):
            copy(k, p, p).wait_recv()
        for cp in sends:
            cp.wait_send()
        acc = land_ref[0]
        for s in range(1, N_DEV):
            acc = acc + land_ref[s]
        out_ref[...] = acc

    vm = pl.BlockSpec(memory_space=pltpu.VMEM)
    return pl.pallas_call(
        body,
        out_shape=SDS((rows, cols), F32),
        in_specs=[vm],
        out_specs=vm,
        scratch_shapes=[pltpu.VMEM((N_DEV, rows, cols), F32),
                        pltpu.SemaphoreType.DMA((7,)), pltpu.SemaphoreType.DMA((7,))],
        name="small_allreduce",
    )(v)


def _inproj(x2, g1, win_g, tm):
    t, d = x2.shape
    nd, _, pb = win_g.shape

    def body(x_ref, g_ref, w_ref, h_ref, p_ref):
        y, _, _ = _rms_fwd(x_ref[...], g_ref[...])
        h = y.astype(BF16)
        h_ref[...] = h
        for j in range(nd):
            p_ref[j] = _dot(h, w_ref[j])

    return pl.pallas_call(
        body,
        grid=(t // tm,),
        in_specs=[pl.BlockSpec((tm, d), lambda i: (i, 0)),
                  pl.BlockSpec((1, d), lambda i: (0, 0)),
                  pl.BlockSpec((nd, d, pb), lambda i: (0, 0, 0))],
        out_specs=[pl.BlockSpec((tm, d), lambda i: (i, 0)),
                   pl.BlockSpec((nd, tm, pb), lambda i: (0, i, 0))],
        out_shape=[SDS((t, d), BF16), SDS((nd, t, pb), F32)],
        compiler_params=_cp("parallel"),
        name="inproj",
    )(x2, g1, win_g)


def _head_slice(p_ref, hh, hpb):
    return p_ref[hh // hpb, :, pl.ds((hh % hpb) * HEAD_DIM, HEAD_DIM)]


def _qkprep(proj, cos, sin_signed, gq, gk, bl, s_len, ha, kva, hb, kvb, ts):
    nd, t, pb = proj.shape
    hpb = pb // HEAD_DIM
    ns = s_len // ts
    sp = s_len + 2 * BLOCK

    def body(p_ref, cos_ref, sin_ref, gq_ref, gk_ref, qa_ref, ka_ref, kat_ref, va_ref, vat_ref, qb_ref, kb_ref,
             kbt_ref, vb_ref, vbt_ref):
        i = pl.program_id(1)
        cs, sn = cos_ref[...], sin_ref[...]
        lane = lax.broadcasted_iota(jnp.int32, (ts, HEAD_DIM), 1)
        first = (lane % 32) < 16
        ones_row = (lax.broadcasted_iota(jnp.int32, (VT_PAD, ts), 0) == 0).astype(BF16)

        def normrope(xh, g):
            y, _, _ = _rms_fwd(xh, g)
            return _rope(y, cs, sn, first)

        for h in range(ha):
            qa_ref[0, h] = (normrope(_head_slice(p_ref, h, hpb), gq_ref[...]) * SCALE).astype(BF16)
        for h in range(kva):
            kh = normrope(_head_slice(p_ref, ha + h, hpb), gk_ref[...])
            ka_ref[0, h] = kh.astype(BF16)
            kat_ref[0, h] = kh.T.astype(BF16)
            vh = _head_slice(p_ref, ha + kva + h, hpb)
            va_ref[0, h] = vh.astype(BF16)
            vat_ref[0, h, 0:HEAD_DIM, :] = vh.T.astype(BF16)
            vat_ref[0, h, HEAD_DIM:HEAD_DIM + VT_PAD, :] = ones_row
        base = ha + 2 * kva
        for h in range(hb):
            qb_ref[0, h] = (_head_slice(p_ref, base + h, hpb) * SCALE).astype(BF16)

        @pl.when(i == 0)
        def _():
            zeros = jnp.zeros((kvb, BLOCK, HEAD_DIM), BF16)
            zeros_t = jnp.zeros((kvb, HEAD_DIM + VT_PAD, BLOCK), BF16)
            for ref in (kb_ref, vb_ref):
                ref[0, :, 0:BLOCK, :] = zeros
                ref[0, :, sp - BLOCK:sp, :] = zeros
            kbt_ref[0, :, :, 0:BLOCK] = zeros_t[:, 0:HEAD_DIM]
            kbt_ref[0, :, :, sp - BLOCK:sp] = zeros_t[:, 0:HEAD_DIM]
            vbt_ref[0, :, :, 0:BLOCK] = zeros_t
            vbt_ref[0, :, :, sp - BLOCK:sp] = zeros_t

        row0 = pl.multiple_of(BLOCK + i * ts, BLOCK)
        for h in range(kvb):
            kh = _head_slice(p_ref, base + hb + h, hpb)
            vh = _head_slice(p_ref, base + hb + kvb + h, hpb)
            kb_ref[0, h, pl.ds(row0, ts), :] = kh.astype(BF16)
            vb_ref[0, h, pl.ds(row0, ts), :] = vh.astype(BF16)
            kbt_ref[0, h, :, pl.ds(row0, ts)] = kh.T.astype(BF16)
            vbt_ref[0, h, 0:HEAD_DIM, pl.ds(row0, ts)] = vh.T.astype(BF16)
            vbt_ref[0, h, HEAD_DIM:HEAD_DIM + VT_PAD, pl.ds(row0, ts)] = ones_row

    def hm(nh):
        return pl.BlockSpec((1, nh, ts, HEAD_DIM), lambda b, i: (b, 0, i, 0))

    def padded(nh):
        return pl.BlockSpec((1, nh, sp, HEAD_DIM), lambda b, i: (b, 0, 0, 0))

    def padded_t(nh, rows):
        return pl.BlockSpec((1, nh, rows, sp), lambda b, i: (b, 0, 0, 0))

    return pl.pallas_call(
        body,
        grid=(bl, ns),
        in_specs=[pl.BlockSpec((nd, ts, pb), lambda b, i: (0, b * ns + i, 0)),
                  pl.BlockSpec((ts, HEAD_DIM), lambda b, i: (i, 0)),
                  pl.BlockSpec((ts, HEAD_DIM), lambda b, i: (i, 0)),
                  pl.BlockSpec((1, HEAD_DIM), lambda b, i: (0, 0)),
                  pl.BlockSpec((1, HEAD_DIM), lambda b, i: (0, 0))],
        out_specs=[hm(ha), hm(kva), pl.BlockSpec((1, kva, HEAD_DIM, ts), lambda b, i: (b, 0, 0, i)), hm(kva),
                   pl.BlockSpec((1, kva, HEAD_DIM + VT_PAD, ts), lambda b, i: (b, 0, 0, i)),
                   hm(hb), padded(kvb), padded_t(kvb, HEAD_DIM), padded(kvb), padded_t(kvb, HEAD_DIM + VT_PAD)],
        out_shape=[SDS((bl, ha, s_len, HEAD_DIM), BF16), SDS((bl, kva, s_len, HEAD_DIM), BF16),
                   SDS((bl, kva, HEAD_DIM, s_len), BF16),
                   SDS((bl, kva, s_len, HEAD_DIM), BF16), SDS((bl, kva, HEAD_DIM + VT_PAD, s_len), BF16),
                   SDS((bl, hb, s_len, HEAD_DIM), BF16),
                   SDS((bl, kvb, sp, HEAD_DIM), BF16), SDS((bl, kvb, HEAD_DIM, sp), BF16),
                   SDS((bl, kvb, sp, HEAD_DIM), BF16), SDS((bl, kvb, HEAD_DIM + VT_PAD, sp), BF16)],
        compiler_params=_cp("parallel", "arbitrary"),
        name="qkprep",
    )(proj, cos, sin_signed, gq, gk)


def _bias_build(bucket_t, rel_bias, hb):
    kvb = hb // GROUP

    def body(bkt_ref, tbl_ref, out_ref):
        bkt = bkt_ref[...]
        ci = lax.broadcasted_iota(jnp.int32, (SPAN, BLOCK), 0)
        qi = lax.broadcasted_iota(jnp.int32, (SPAN, BLOCK), 1)
        band = jnp.abs(ci - BLOCK - qi) <= BLOCK
        masks = (band, band & (ci >= BLOCK), band & (ci < 2 * BLOCK))
        for h in range(hb):
            acct = jnp.zeros((SPAN, BLOCK), F32)
            for b in range(N_BUCKETS):
                acct = jnp.where(bkt == b, tbl_ref[b, h], acct)
            lanes = slice((h % GROUP) * BLOCK, (h % GROUP + 1) * BLOCK)
            for var, mask in enumerate(masks):
                out_ref[var, h // GROUP, :, lanes] = jnp.where(mask, acct, NEG_INF)

    vm = pl.BlockSpec(memory_space=pltpu.VMEM)
    return pl.pallas_call(
        body,
        in_specs=[vm, pl.BlockSpec(memory_space=pltpu.SMEM)],
        out_specs=vm,
        out_shape=SDS((3, kvb, SPAN, GROUP * BLOCK), F32),
        name="bias_build",
    )(bucket_t, rel_bias)


def _attn_a_fwd(qa, ka, vat, tq, tk, shards):
    bl, ha, s_len, _ = qa.shape
    kv = ka.shape[1]
    va_rows = vat.shape[2]
    nq, nk = s_len // tq, s_len // tk
    assert nk % 2 == 0
    r = GROUP * tq
    ns = len(shards)

    def body(q_ref, k_ref, v_ref, *rest):
        shard_refs, (o_ref, l_ref), gathered = rest[:ns], rest[ns:ns + 2], rest[ns + 2:2 * ns + 2]
        st_sc, send_sems, recv_sems, local_sems = rest[2 * ns + 2:]
        step_id = (pl.program_id(0) * kv + pl.program_id(1)) * nq + pl.program_id(2)
        start, wait = _direct_exchange("gather", shard_refs, gathered, send_sems, recv_sems, local_sems)
        pl.when(step_id == 0)(start)

        q = q_ref[0].reshape(r, HEAD_DIM)

        def scores(c):
            return _dot_nt(k_ref[0, 0, pl.ds(pl.multiple_of(c * tk, tk), tk), :], q)

        def fold(st, c, carry):
            m_old, acc = carry
            m_new = jnp.maximum(m_old, jnp.max(st, axis=0, keepdims=True))
            pt = jnp.exp(st - m_new).astype(BF16)
            vt = v_ref[0, 0, :, pl.ds(pl.multiple_of(c * tk, tk), tk)]
            return m_new, jnp.exp(m_old - m_new) * acc + _dot(vt, pt)

        st_sc[0] = scores(0)

        def step(c2, carry):
            c = 2 * c2
            st_sc[1] = scores(c + 1)
            carry = fold(st_sc[0], c, carry)
            st_sc[0] = scores(jnp.minimum(c + 2, nk - 1))
            return fold(st_sc[1], c + 1, carry)

        m, acc = lax.fori_loop(0, nk // 2, step,
                               (jnp.full((1, r), -jnp.inf, F32), jnp.zeros((va_rows, r), F32)))
        l = acc[HEAD_DIM:HEAD_DIM + 1, :]
        o = (acc[0:HEAD_DIM, :] / l).T
        for h in range(GROUP):
            o_ref[:, h * HEAD_DIM:(h + 1) * HEAD_DIM] = o[h * tq:(h + 1) * tq].astype(BF16)
        l_ref[0, 0, 0] = jnp.broadcast_to(m + jnp.log(l), (8, r))
        pl.when(step_id == bl * kv * nq - 1)(wait)

    anyspec = pl.BlockSpec(memory_space=pl.ANY)
    res = pl.pallas_call(
        body,
        grid=(bl, kv, nq),
        in_specs=[pl.BlockSpec((1, GROUP, tq, HEAD_DIM), lambda b, g, i: (b, g, i, 0)),
                  pl.BlockSpec((1, 1, s_len, HEAD_DIM), lambda b, g, i: (b, g, 0, 0)),
                  pl.BlockSpec((1, 1, va_rows, s_len), lambda b, g, i: (b, g, 0, 0))] + [anyspec] * ns,
        out_specs=[pl.BlockSpec((tq, GROUP * HEAD_DIM), lambda b, g, i: (b * nq + i, g)),
                   pl.BlockSpec((1, 1, 1, 8, r), lambda b, g, i: (b, g, i, 0, 0))] + [anyspec] * ns,
        out_shape=[SDS((bl * s_len, ha * HEAD_DIM), BF16), SDS((bl, kv, nq, 8, r), F32)]
        + [SDS((N_DEV,) + s.shape, s.dtype) for s in shards],
        scratch_shapes=[pltpu.VMEM((2, tk, r), F32)] + _exchange_scratch(ns),
        compiler_params=_cp("arbitrary", "arbitrary", "arbitrary"),
        name="attn_a_fwd",
    )(qa, ka, vat, *shards)
    return res[0], res[1], res[2:]


FFN_BLOCKS_PER_STEP = 4
QB_PER_STEP = 4


def _bias_variant(n, nb):
    return jnp.where(n == 0, 1, jnp.where(n == nb - 1, 2, 0))


def _sink_row(sink_ref, g):
    return jnp.concatenate([jnp.full((1, BLOCK), sink_ref[0, g * GROUP + h], F32) for h in range(GROUP)], axis=1)


def _attn_b_fwd(qb, kb, vbt, bias_t, sink, s_len):
    bl, hb, _, _ = qb.shape
    kv = kb.shape[1]
    sp = kb.shape[2]
    vt_rows = vbt.shape[2]
    nb = s_len // BLOCK
    nbs = min(QB_PER_STEP, nb)
    r = GROUP * BLOCK

    def body(q_ref, k_ref, vt_ref, bt_ref, sink_ref, o_ref, l_ref):
        g, n0 = pl.program_id(1), pl.program_id(2) * nbs
        sink_row = _sink_row(sink_ref, g)
        for j in range(nbs):
            n = n0 + j
            span = pl.ds(pl.multiple_of(n * BLOCK, BLOCK), SPAN)
            q = q_ref[0, :, j * BLOCK:(j + 1) * BLOCK, :].reshape(r, HEAD_DIM)
            st = _dot_nt(k_ref[0, 0, span, :], q) + bt_ref[_bias_variant(n, nb), 0]
            m = jnp.maximum(jnp.max(st, axis=0, keepdims=True), sink_row)
            acc = _dot(vt_ref[0, 0, :, span], jnp.exp(st - m).astype(BF16))
            l = acc[HEAD_DIM:HEAD_DIM + 1, :] + jnp.exp(sink_row - m)
            o = (acc[0:HEAD_DIM, :] / l).T
            for h in range(GROUP):
                o_ref[j * BLOCK:(j + 1) * BLOCK, h * HEAD_DIM:(h + 1) * HEAD_DIM] = (
                    o[h * BLOCK:(h + 1) * BLOCK].astype(BF16))
            l_ref[0, 0, j] = jnp.broadcast_to(m + jnp.log(l), (8, r))

    return pl.pallas_call(
        body,
        grid=(bl, kv, nb // nbs),
        in_specs=[pl.BlockSpec((1, GROUP, nbs * BLOCK, HEAD_DIM), lambda b, g, n: (b, g, n, 0)),
                  pl.BlockSpec((1, 1, sp, HEAD_DIM), lambda b, g, n: (b, g, 0, 0)),
                  pl.BlockSpec((1, 1, vt_rows, sp), lambda b, g, n: (b, g, 0, 0)),
                  pl.BlockSpec((3, 1, SPAN, r), lambda b, g, n: (0, g, 0, 0)),
                  pl.BlockSpec(memory_space=pltpu.SMEM)],
        out_specs=[pl.BlockSpec((nbs * BLOCK, GROUP * HEAD_DIM), lambda b, g, n: (b * (nb // nbs) + n, g)),
                   pl.BlockSpec((1, 1, nbs, 8, r), lambda b, g, n: (b, g, n, 0, 0))],
        out_shape=[SDS((bl * s_len, hb * HEAD_DIM), BF16), SDS((bl, kv, nb, 8, r), F32)],
        compiler_params=_cp("parallel", "parallel", "arbitrary"),
        name="attn_b_fwd",
    )(qb, kb, vbt, bias_t, sink)


def _mixout(oa, ob, wo, x2, g2, g3, tm):
    t, d = x2.shape
    ca = oa.shape[1]

    def body(oa_ref, ob_ref, w_ref, x_ref, g2_ref, g3_ref, mix_ref, x1_ref, h2_ref):
        mix = _dot(oa_ref[...], w_ref[0:ca, :]) + _dot(ob_ref[...], w_ref[ca:, :])
        mix_ref[...] = mix
        y2, _, _ = _rms_fwd(mix, g2_ref[...])
        x1 = x_ref[...] + y2
        x1_ref[...] = x1
        y3, _, _ = _rms_fwd(x1, g3_ref[...])
        h2_ref[...] = y3.astype(BF16)

    tile = lambda w: pl.BlockSpec((tm, w), lambda i: (i, 0))
    vec = pl.BlockSpec((1, d), lambda i: (0, 0))
    return pl.pallas_call(
        body,
        grid=(t // tm,),
        in_specs=[tile(ca), tile(ob.shape[1]), pl.BlockSpec(wo.shape, lambda i: (0, 0)), tile(d), vec, vec],
        out_specs=[tile(d), tile(d), tile(d)],
        out_shape=[SDS((t, d), F32), SDS((t, d), F32), SDS((t, d), BF16)],
        compiler_params=_cp("parallel"),
        name="mixout",
    )(oa, ob, wo, x2, g2, g3)


def _ffn_fwd(h2, wup_g, wdn, x1, target, g4, tm, jb):
    t, d = x1.shape
    nblk, _, tf = wup_g.shape
    ff = nblk * tf
    nt = t // tm
    nj = nblk // jb

    def body(h_ref, wu_ref, wd_ref, x1_ref, tg_ref, g_ref, u_ref, df_ref, dy_ref, dg_ref, loss_ref, acc_sc):
        i, j = pl.program_id(0), pl.program_id(1)

        @pl.when(j == 0)
        def _():
            acc_sc[...] = jnp.zeros_like(acc_sc)

        @pl.when((i == 0) & (j == 0))
        def _():
            dg_ref[...] = jnp.zeros_like(dg_ref)
            loss_ref[...] = jnp.zeros_like(loss_ref)

        h = h_ref[...]
        squares = []
        for s in range(jb):
            u = jnp.maximum(_dot(h, wu_ref[s]), 0.0)
            u_ref[:, s * tf:(s + 1) * tf] = u.astype(BF16)
            squares.append((u * u).astype(BF16))
        acc_sc[...] += _dot(jnp.concatenate(squares, axis=1), wd_ref[...])

        @pl.when(j == nj - 1)
        def _():
            g = g_ref[...]
            y4, n, r = _rms_fwd(acc_sc[...], g)
            e = (x1_ref[...] + y4) - tg_ref[...]
            loss_ref[...] += jnp.sum(e * e) * (0.5 / d)
            dy = e * (1.0 / d)
            dy_ref[...] = dy
            df, dgt = _rms_bwd(n, r, g, dy)
            df_ref[...] = df.astype(BF16)
            dg_ref[0:1, :] += jnp.sum(dgt, axis=0, keepdims=True)

    tile = pl.BlockSpec((tm, d), lambda i, j: (i, 0))
    return pl.pallas_call(
        body,
        grid=(nt, nj),
        in_specs=[tile,
                  pl.BlockSpec((jb, d, tf), lambda i, j: (j, 0, 0)),
                  pl.BlockSpec((jb * tf, d), lambda i, j: (j, 0)),
                  tile, tile,
                  pl.BlockSpec((1, d), lambda i, j: (0, 0))],
        out_specs=[pl.BlockSpec((tm, jb * tf), lambda i, j: (i, j)), tile, tile,
                   pl.BlockSpec((8, d), lambda i, j: (0, 0)),
                   pl.BlockSpec((8, 128), lambda i, j: (0, 0))],
        out_shape=[SDS((t, ff), BF16), SDS((t, d), BF16), SDS((t, d), F32), SDS((8, d), F32), SDS((8, 128), F32)],
        scratch_shapes=[pltpu.VMEM((tm, d), F32)],
        compiler_params=_cp("arbitrary", "arbitrary"),
        name="ffn_fwd",
    )(h2, wup_g, wdn, x1, target, g4)


def _ffn_bwd(df, u, wdn, wup_g, x1, dy, mix, g3, g2, tm, jb):
    t, d = x1.shape
    nblk, _, tf = wup_g.shape
    nt = t // tm
    nj = nblk // jb

    def body(df_ref, u_ref, wd_ref, wu_ref, x1_ref, dy_ref, mix_ref, g3_ref, g2_ref,
             dpre_ref, dx1_ref, dmix_ref, dg3_ref, dg2_ref, acc_sc):
        i, j = pl.program_id(0), pl.program_id(1)

        @pl.when(j == 0)
        def _():
            acc_sc[...] = jnp.zeros_like(acc_sc)

        @pl.when((i == 0) & (j == 0))
        def _():
            dg3_ref[...] = jnp.zeros_like(dg3_ref)
            dg2_ref[...] = jnp.zeros_like(dg2_ref)

        du2 = _dot_nt(df_ref[...], wd_ref[...])
        dpre = (2.0 * u_ref[...].astype(F32) * du2).astype(BF16)
        dpre_ref[...] = dpre
        dh = _dot_nt(dpre[:, 0:tf], wu_ref[0])
        for s in range(1, jb):
            dh = dh + _dot_nt(dpre[:, s * tf:(s + 1) * tf], wu_ref[s])
        acc_sc[...] += dh

        @pl.when(j == nj - 1)
        def _():
            g3, g2 = g3_ref[...], g2_ref[...]
            _, n3, r3 = _rms_fwd(x1_ref[...], g3)
            dx, dgt3 = _rms_bwd(n3, r3, g3, acc_sc[...])
            dx1 = dy_ref[...] + dx
            dx1_ref[...] = dx1
            dg3_ref[0:1, :] += jnp.sum(dgt3, axis=0, keepdims=True)
            _, n2, r2 = _rms_fwd(mix_ref[...], g2)
            dmix, dgt2 = _rms_bwd(n2, r2, g2, dx1)
            dmix_ref[...] = dmix.astype(BF16)
            dg2_ref[0:1, :] += jnp.sum(dgt2, axis=0, keepdims=True)

    tile = pl.BlockSpec((tm, d), lambda i, j: (i, 0))
    vec = pl.BlockSpec((1, d), lambda i, j: (0, 0))
    acc8 = pl.BlockSpec((8, d), lambda i, j: (0, 0))
    return pl.pallas_call(
        body,
        grid=(nt, nj),
        in_specs=[tile,
                  pl.BlockSpec((tm, jb * tf), lambda i, j: (i, j)),
                  pl.BlockSpec((jb * tf, d), lambda i, j: (j, 0)),
                  pl.BlockSpec((jb, d, tf), lambda i, j: (j, 0, 0)),
                  tile, tile, tile, vec, vec],
        out_specs=[pl.BlockSpec((tm, jb * tf), lambda i, j: (i, j)), tile, tile, acc8, acc8],
        out_shape=[SDS(u.shape, BF16), SDS((t, d), F32), SDS((t, d), BF16), SDS((8, d), F32), SDS((8, d), F32)],
        scratch_shapes=[pltpu.VMEM((tm, d), F32)],
        compiler_params=_cp("arbitrary", "arbitrary"),
        name="ffn_bwd",
    )(df, u, wdn, wup_g, x1, dy, mix, g3, g2)


def _wgrad_cols(a, b, nj, tt, name):
    t, m = a.shape
    n = b.shape[1]
    bn = n // nj

    def body(a_ref, b_ref, o_ref):
        @pl.when(pl.program_id(1) == 0)
        def _():
            o_ref[...] = jnp.zeros_like(o_ref)

        o_ref[0] += _dot_tn(a_ref[...], b_ref[...])

    return pl.pallas_call(
        body,
        grid=(nj, t // tt),
        in_specs=[pl.BlockSpec((tt, m), lambda j, k: (k, 0)),
                  pl.BlockSpec((tt, bn), lambda j, k: (k, j))],
        out_specs=pl.BlockSpec((1, m, bn), lambda j, k: (j, 0, 0)),
        out_shape=SDS((nj, m, bn), F32),
        compiler_params=_cp("parallel", "arbitrary"),
        name=name,
    )(a, b)


def _wgrad_cols_blocked(a, b3, tt, name):
    t, m = a.shape
    nj, _, bn = b3.shape

    def body(a_ref, b_ref, o_ref):
        @pl.when(pl.program_id(1) == 0)
        def _():
            o_ref[...] = jnp.zeros_like(o_ref)

        o_ref[0] += _dot_tn(a_ref[...], b_ref[0])

    return pl.pallas_call(
        body,
        grid=(nj, t // tt),
        in_specs=[pl.BlockSpec((tt, m), lambda j, k: (k, 0)),
                  pl.BlockSpec((1, tt, bn), lambda j, k: (j, k, 0))],
        out_specs=pl.BlockSpec((1, m, bn), lambda j, k: (j, 0, 0)),
        out_shape=SDS((nj, m, bn), F32),
        compiler_params=_cp("parallel", "arbitrary"),
        name=name,
    )(a, b3)


def _wgrad_rows(a_parts, b, nj, tt, square, name):
    t, n = b.shape
    widths = [p.shape[1] for p in a_parts]
    m = sum(widths)
    bm = m // nj
    per = [w // bm for w in widths]
    starts = [sum(per[:q]) for q in range(len(per))]
    np_ = len(a_parts)

    def body(*refs):
        a_refs, b_ref, o_ref = refs[:np_], refs[np_], refs[np_ + 1]
        j = pl.program_id(0)

        @pl.when(pl.program_id(1) == 0)
        def _():
            o_ref[...] = jnp.zeros_like(o_ref)

        for q in range(np_):
            @pl.when((j >= starts[q]) & (j < starts[q] + per[q]))
            def _(q=q):
                a = a_refs[q][...]
                if square:
                    af = a.astype(F32)
                    a = (af * af).astype(BF16)
                o_ref[0] += _dot_tn(a, b_ref[...])

    def a_spec(q):
        return pl.BlockSpec((tt, bm), lambda j, k: (k, jnp.clip(j - starts[q], 0, per[q] - 1)))

    return pl.pallas_call(
        body,
        grid=(nj, t // tt),
        in_specs=[a_spec(q) for q in range(np_)] + [pl.BlockSpec((tt, n), lambda j, k: (k, 0))],
        out_specs=pl.BlockSpec((1, bm, n), lambda j, k: (j, 0, 0)),
        out_shape=SDS((nj, bm, n), F32),
        compiler_params=_cp("parallel", "arbitrary"),
        name=name,
    )(*a_parts, b)


def _attn_out_bwd(dmix, wo, ca, tm):
    t, d = dmix.shape
    cb = wo.shape[0] - ca

    def body(dm_ref, w_ref, da_ref, db_ref):
        dm = dm_ref[...]
        da_ref[...] = _dot_nt(dm, w_ref[0:ca, :]).astype(BF16)
        db_ref[...] = _dot_nt(dm, w_ref[ca:, :]).astype(BF16)

    return pl.pallas_call(
        body,
        grid=(t // tm,),
        in_specs=[pl.BlockSpec((tm, d), lambda i: (i, 0)), pl.BlockSpec(wo.shape, lambda i: (0, 0))],
        out_specs=[pl.BlockSpec((tm, ca), lambda i: (i, 0)), pl.BlockSpec((tm, cb), lambda i: (i, 0))],
        out_shape=[SDS((t, ca), BF16), SDS((t, cb), BF16)],
        compiler_params=_cp("parallel"),
        name="attn_out_bwd",
    )(dmix, wo)


def _stack_heads(ref, rows):
    return jnp.concatenate([ref[:, h * HEAD_DIM:(h + 1) * HEAD_DIM] for h in range(GROUP)], axis=0)


def _attn_a_bwd(qa, ka, kat, va, do, o, lse, tq, tk, grads):
    bl, ha, s_len, _ = qa.shape
    kv = ka.shape[1]
    nq, nk = s_len // tq, s_len // tk
    assert nk % 2 == 0
    r = GROUP * tq
    ng = len(grads)

    def body(q_ref, k_ref, kt_ref, v_ref, do_ref, o_ref, l_ref, *rest):
        grad_refs, (dq_ref, dk_ref, dv_ref), parts = rest[:ng], rest[ng:ng + 3], rest[ng + 3:2 * ng + 3]
        st_sc, dp_sc, dkt_sc, dvt_sc, send_sems, recv_sems, local_sems = rest[2 * ng + 3:]
        i = pl.program_id(2)
        step_id = (pl.program_id(0) * kv + pl.program_id(1)) * nq + i
        start, wait = _direct_exchange("scatter", grad_refs, parts, send_sems, recv_sems, local_sems)
        pl.when(step_id == 0)(start)

        q = q_ref[0].reshape(r, HEAD_DIM)
        do2 = _stack_heads(do_ref, tq)
        qt = q.astype(F32).T
        dot32 = do2.astype(F32).T
        ot32 = _stack_heads(o_ref, tq).astype(F32).T
        drow = jnp.sum(dot32 * ot32, axis=0, keepdims=True)
        qt, dot = qt.astype(BF16), dot32.astype(BF16)
        lrow = l_ref[0, 0, 0, 0:1, :]

        @pl.when(i == 0)
        def _():
            dkt_sc[...] = jnp.zeros_like(dkt_sc)
            dvt_sc[...] = jnp.zeros_like(dvt_sc)

        def chunk(c):
            return pl.ds(pl.multiple_of(c * tk, tk), tk)

        def scores(c, slot):
            st_sc[slot] = _dot_nt(k_ref[0, 0, chunk(c), :], q)
            dp_sc[slot] = _dot_nt(v_ref[0, 0, chunk(c), :], do2)

        def fold(slot, c, dqt):
            pt = jnp.exp(st_sc[slot] - lrow)
            dsb = (pt * (dp_sc[slot] - drow)).astype(BF16)
            dvt_sc[:, chunk(c)] += _dot_nt(dot, pt.astype(BF16))
            dkt_sc[:, chunk(c)] += _dot_nt(qt, dsb)
            return dqt + _dot(kt_ref[0, 0, :, chunk(c)], dsb)

        scores(0, 0)

        def step(c2, dqt):
            c = 2 * c2
            scores(c + 1, 1)
            dqt = fold(0, c, dqt)
            scores(jnp.minimum(c + 2, nk - 1), 0)
            return fold(1, c + 1, dqt)

        dqt = lax.fori_loop(0, nk // 2, step, jnp.zeros((HEAD_DIM, r), F32))
        dq_ref[0] = dqt.T.reshape(GROUP, tq, HEAD_DIM)

        @pl.when(i == nq - 1)
        def _():
            dk_ref[0, 0] = dkt_sc[...].T
            dv_ref[0, 0] = dvt_sc[...].T

        pl.when(step_id == bl * kv * nq - 1)(wait)

    kvspec = pl.BlockSpec((1, 1, s_len, HEAD_DIM), lambda b, g, i: (b, g, 0, 0))
    qspec = pl.BlockSpec((1, GROUP, tq, HEAD_DIM), lambda b, g, i: (b, g, i, 0))
    tok = pl.BlockSpec((tq, GROUP * HEAD_DIM), lambda b, g, i: (b * nq + i, g))
    anyspec = pl.BlockSpec(memory_space=pl.ANY)
    res = pl.pallas_call(
        body,
        grid=(bl, kv, nq),
        in_specs=[qspec, kvspec, pl.BlockSpec((1, 1, HEAD_DIM, s_len), lambda b, g, i: (b, g, 0, 0)), kvspec,
                  tok, tok, pl.BlockSpec((1, 1, 1, 8, r), lambda b, g, i: (b, g, i, 0, 0))] + [anyspec] * ng,
        out_specs=[qspec, kvspec, kvspec] + [anyspec] * ng,
        out_shape=[SDS(qa.shape, F32), SDS(ka.shape, F32), SDS(va.shape, F32)]
        + [SDS(g.shape, g.dtype) for g in grads],
        scratch_shapes=[pltpu.VMEM((2, tk, r), F32), pltpu.VMEM((2, tk, r), F32),
                        pltpu.VMEM((HEAD_DIM, s_len), F32), pltpu.VMEM((HEAD_DIM, s_len), F32)]
        + _exchange_scratch(ng),
        compiler_params=_cp("arbitrary", "arbitrary", "arbitrary"),
        name="attn_a_bwd",
    )(qa, ka, kat, va, do, o, lse, *grads)
    return res[0], res[1], res[2], res[3:]


def _attn_b_bwd(qb, kb, kbt, vb, do, o, lse, bias_t, sink, s_len):
    bl, hb, _, _ = qb.shape
    kv, sp = kb.shape[1], kb.shape[2]
    nb = s_len // BLOCK
    nbs = min(QB_PER_STEP, nb)
    r = GROUP * BLOCK

    def body(q_ref, k_ref, kt_ref, v_ref, do_ref, o_ref, l_ref, bt_ref, sink_ref,
             dq_ref, dk_ref, dv_ref, dsum_ref, dsink_ref, dkt_sc, dvt_sc):
        g, b, ns = pl.program_id(0), pl.program_id(1), pl.program_id(2)
        sink_row = _sink_row(sink_ref, g)

        @pl.when(ns == 0)
        def _():
            dkt_sc[...] = jnp.zeros_like(dkt_sc)
            dvt_sc[...] = jnp.zeros_like(dvt_sc)

        @pl.when((b == 0) & (ns == 0))
        def _():
            dsum_ref[...] = jnp.zeros_like(dsum_ref)
            dsink_ref[...] = jnp.zeros_like(dsink_ref)

        dsum = jnp.zeros((SPAN, r), F32)
        dsink = jnp.zeros((1, r), F32)
        for j in range(nbs):
            n = ns * nbs + j
            span = pl.ds(pl.multiple_of(n * BLOCK, BLOCK), SPAN)
            rows = slice(j * BLOCK, (j + 1) * BLOCK)
            q = q_ref[0, :, rows, :].reshape(r, HEAD_DIM)
            do2 = jnp.concatenate([do_ref[rows, h * HEAD_DIM:(h + 1) * HEAD_DIM] for h in range(GROUP)], axis=0)
            o2 = jnp.concatenate([o_ref[rows, h * HEAD_DIM:(h + 1) * HEAD_DIM] for h in range(GROUP)], axis=0)
            dot32 = do2.astype(F32).T
            drow = jnp.sum(dot32 * o2.astype(F32).T, axis=0, keepdims=True)
            qt, dot = q.astype(F32).T.astype(BF16), dot32.astype(BF16)
            lrow = l_ref[0, 0, j, 0:1, :]
            st = _dot_nt(k_ref[0, 0, span, :], q) + bt_ref[_bias_variant(n, nb), 0]
            pt = jnp.exp(st - lrow)
            dst = pt * (_dot_nt(v_ref[0, 0, span, :], do2) - drow)
            dsum = dsum + dst
            dsink = dsink - jnp.exp(sink_row - lrow) * drow
            dsb = dst.astype(BF16)
            dvt_sc[:, span] += _dot_nt(dot, pt.astype(BF16))
            dkt_sc[:, span] += _dot_nt(qt, dsb)
            dq_ref[0, :, rows, :] = _dot(kt_ref[0, 0, :, span], dsb).T.reshape(GROUP, BLOCK, HEAD_DIM)
        dsum_ref[0] += dsum
        dsink_ref[0, 0:1, :] += dsink

        @pl.when(ns == nb // nbs - 1)
        def _():
            dk_ref[0, 0] = dkt_sc[:, BLOCK:BLOCK + s_len].T
            dv_ref[0, 0] = dvt_sc[:, BLOCK:BLOCK + s_len].T

    kvspec = pl.BlockSpec((1, 1, sp, HEAD_DIM), lambda g, b, n: (b, g, 0, 0))
    kvout = pl.BlockSpec((1, 1, s_len, HEAD_DIM), lambda g, b, n: (b, g, 0, 0))
    qspec = pl.BlockSpec((1, GROUP, nbs * BLOCK, HEAD_DIM), lambda g, b, n: (b, g, n, 0))
    tok = pl.BlockSpec((nbs * BLOCK, GROUP * HEAD_DIM), lambda g, b, n: (b * (nb // nbs) + n, g))
    return pl.pallas_call(
        body,
        grid=(kv, bl, nb // nbs),
        in_specs=[qspec, kvspec, pl.BlockSpec((1, 1, HEAD_DIM, sp), lambda g, b, n: (b, g, 0, 0)), kvspec, tok, tok,
                  pl.BlockSpec((1, 1, nbs, 8, r), lambda g, b, n: (b, g, n, 0, 0)),
                  pl.BlockSpec((3, 1, SPAN, r), lambda g, b, n: (0, g, 0, 0)),
                  pl.BlockSpec(memory_space=pltpu.SMEM)],
        out_specs=[qspec, kvout, kvout,
                   pl.BlockSpec((1, SPAN, r), lambda g, b, n: (g, 0, 0)),
                   pl.BlockSpec((1, 8, r), lambda g, b, n: (g, 0, 0))],
        out_shape=[SDS(qb.shape, F32), SDS((bl, kv, s_len, HEAD_DIM), F32), SDS((bl, kv, s_len, HEAD_DIM), F32),
                   SDS((kv, SPAN, r), F32), SDS((kv, 8, r), F32)],
        scratch_shapes=[pltpu.VMEM((HEAD_DIM, sp), F32), pltpu.VMEM((HEAD_DIM, sp), F32)],
        compiler_params=_cp("arbitrary", "arbitrary", "arbitrary"),
        name="attn_b_bwd",
    )(qb, kb, kbt, vb, do, o, lse, bias_t, sink)


def _bias_reduce(dsum, dsink, bucket_t4):
    kv, _, r = dsum.shape

    def body(ds_ref, dk_ref, bk_ref, rel_ref, sink_ref):
        lane = lax.broadcasted_iota(jnp.int32, (N_BUCKETS, 128), 1)
        lane8 = lax.broadcasted_iota(jnp.int32, (8, 128), 1)
        bk = bk_ref[...]
        for g in range(kv):
            ds = ds_ref[g]
            rowi = lax.broadcasted_iota(jnp.int32, (N_BUCKETS, r), 0)
            red = jnp.zeros((N_BUCKETS, r), F32)
            for b in range(N_BUCKETS):
                red = jnp.where(rowi == b, jnp.sum(jnp.where(bk == b, ds, 0.0), axis=0, keepdims=True), red)
            out = jnp.zeros((N_BUCKETS, 128), F32)
            so = jnp.zeros((8, 128), F32)
            for h in range(GROUP):
                col = jnp.sum(red[:, h * BLOCK:(h + 1) * BLOCK], axis=1, keepdims=True)
                out = jnp.where(lane == h, col, out)
                sc = jnp.sum(dk_ref[g][:, h * BLOCK:(h + 1) * BLOCK], axis=1, keepdims=True)
                so = jnp.where(lane8 == h, sc, so)
            rel_ref[g] = out
            sink_ref[g] = so

    vm = pl.BlockSpec(memory_space=pltpu.VMEM)
    return pl.pallas_call(
        body,
        in_specs=[vm, vm, vm],
        out_specs=[vm, vm],
        out_shape=[SDS((kv, N_BUCKETS, 128), F32), SDS((kv, 8, 128), F32)],
        name="bias_reduce",
    )(dsum, dsink, bucket_t4)


def _dqkprep(dqa, dka, dva, dqb, dkb, dvb, proj, cos, sin_signed, gq, gk, s_len, ts):
    nd, t, pb = proj.shape
    bl, ha = dqa.shape[0], dqa.shape[1]
    kva, hb, kvb = dka.shape[1], dqb.shape[1], dkb.shape[1]
    hpb = pb // HEAD_DIM
    ns = s_len // ts

    def body(dqa_ref, dka_ref, dva_ref, dqb_ref, dkb_ref, dvb_ref, p_ref, cos_ref, sin_ref, gq_ref, gk_ref,
             dp_ref, dgq_ref, dgk_ref):
        b, i = pl.program_id(0), pl.program_id(1)
        cs, sn = cos_ref[...], sin_ref[...]
        lane = lax.broadcasted_iota(jnp.int32, (ts, HEAD_DIM), 1)
        first = (lane % 32) < 16

        @pl.when((b == 0) & (i == 0))
        def _():
            dgq_ref[...] = jnp.zeros_like(dgq_ref)
            dgk_ref[...] = jnp.zeros_like(dgk_ref)

        def put(hh, val):
            dp_ref[hh // hpb, :, pl.ds((hh % hpb) * HEAD_DIM, HEAD_DIM)] = val.astype(BF16)

        def unrope_norm(d_rot, hh, g, dg_ref):
            ds = d_rot * sn
            dn = d_rot * cs + jnp.where(first, jnp.roll(ds, -16, axis=1), jnp.roll(ds, 16, axis=1))
            _, n, r = _rms_fwd(_head_slice(p_ref, hh, hpb), g)
            dx, dgt = _rms_bwd(n, r, g, dn)
            dg_ref[0:1, :] += jnp.sum(dgt, axis=0, keepdims=True)
            put(hh, dx)

        for h in range(ha):
            unrope_norm(dqa_ref[0, h] * SCALE, h, gq_ref[...], dgq_ref)
        for h in range(kva):
            unrope_norm(dka_ref[0, h], ha + h, gk_ref[...], dgk_ref)
            put(ha + kva + h, dva_ref[0, h])
        base = ha + 2 * kva
        for h in range(hb):
            put(base + h, dqb_ref[0, h] * SCALE)
        for h in range(kvb):
            put(base + hb + h, dkb_ref[0, h])
            put(base + hb + kvb + h, dvb_ref[0, h])

    def hm(nh):
        return pl.BlockSpec((1, nh, ts, HEAD_DIM), lambda b, i: (b, 0, i, 0))

    vec = pl.BlockSpec((1, HEAD_DIM), lambda b, i: (0, 0))
    tab = pl.BlockSpec((ts, HEAD_DIM), lambda b, i: (i, 0))
    acc = pl.BlockSpec((8, HEAD_DIM), lambda b, i: (0, 0))
    pspec = pl.BlockSpec((nd, ts, pb), lambda b, i: (0, b * ns + i, 0))
    return pl.pallas_call(
        body,
        grid=(bl, ns),
        in_specs=[hm(ha), hm(kva), hm(kva), hm(hb), hm(kvb), hm(kvb), pspec, tab, tab, vec, vec],
        out_specs=[pspec, acc, acc],
        out_shape=[SDS((nd, t, pb), BF16), SDS((8, HEAD_DIM), F32), SDS((8, HEAD_DIM), F32)],
        compiler_params=_cp("arbitrary", "arbitrary"),
        name="dqkprep",
    )(dqa, dka, dva, dqb, dkb, dvb, proj, cos, sin_signed, gq, gk)


def _dx_final(dproj, win_g, x2, dx1, g1, tm):
    t, d = x2.shape
    nd, _, pb = win_g.shape

    def body(dp_ref, w_ref, x_ref, dx1_ref, g_ref, dx_ref, dg_ref):
        @pl.when(pl.program_id(0) == 0)
        def _():
            dg_ref[...] = jnp.zeros_like(dg_ref)

        dh = _dot_nt(dp_ref[0], w_ref[0])
        for j in range(1, nd):
            dh = dh + _dot_nt(dp_ref[j], w_ref[j])
        g = g_ref[...]
        _, n, r = _rms_fwd(x_ref[...], g)
        dx, dgt = _rms_bwd(n, r, g, dh)
        dx_ref[...] = dx1_ref[...] + dx
        dg_ref[0:1, :] += jnp.sum(dgt, axis=0, keepdims=True)

    tile = pl.BlockSpec((tm, d), lambda i: (i, 0))
    return pl.pallas_call(
        body,
        grid=(t // tm,),
        in_specs=[pl.BlockSpec((nd, tm, pb), lambda i: (0, i, 0)),
                  pl.BlockSpec((nd, d, pb), lambda i: (0, 0, 0)),
                  tile, tile, pl.BlockSpec((1, d), lambda i: (0, 0))],
        out_specs=[tile, pl.BlockSpec((8, d), lambda i: (0, 0))],
        out_shape=[SDS((t, d), F32), SDS((8, d), F32)],
        compiler_params=_cp("arbitrary"),
        name="dx_final",
    )(dproj, win_g, x2, dx1, g1)


def _adamw_math(w, g, m, v):
    m = ADAM_B1 * m + (1.0 - ADAM_B1) * g
    v = ADAM_B2 * v + (1.0 - ADAM_B2) * (g * g)
    m_hat = m / (1.0 - ADAM_B1 ** ADAM_STEP)
    v_hat = v / (1.0 - ADAM_B2 ** ADAM_STEP)
    delta = -ADAM_LR * (m_hat / (jnp.sqrt(v_hat) + ADAM_EPS) + ADAM_WD * w)
    return delta, m, v


def _adamw_sum(parts, w, m, v, tr, name):
    rows, cols = w.shape

    def body(p_ref, w_ref, m_ref, v_ref, g_ref, d_ref, nm_ref, nv_ref):
        g = p_ref[0]
        for s in range(1, N_DEV):
            g = g + p_ref[s]
        g_ref[...] = g
        d_ref[...], nm_ref[...], nv_ref[...] = _adamw_math(w_ref[...], g, m_ref[...], v_ref[...])

    tr = min(tr, rows)
    tile = pl.BlockSpec((tr, cols), lambda i: (i, 0))
    return pl.pallas_call(
        body,
        grid=(rows // tr,),
        in_specs=[pl.BlockSpec((N_DEV, tr, cols), lambda i: (0, i, 0)), tile, tile, tile],
        out_specs=[tile] * 4,
        out_shape=[SDS((rows, cols), F32)] * 4,
        compiler_params=_cp("parallel"),
        name=name,
    )(parts, w, m, v)


def _adamw_small(g, w, m, v):
    def body(g_ref, w_ref, m_ref, v_ref, d_ref, nm_ref, nv_ref):
        d_ref[...], nm_ref[...], nv_ref[...] = _adamw_math(w_ref[...], g_ref[...], m_ref[...], v_ref[...])

    vm = pl.BlockSpec(memory_space=pltpu.VMEM)
    return pl.pallas_call(
        body,
        in_specs=[vm] * 4,
        out_specs=[vm] * 3,
        out_shape=[SDS(w.shape, F32)] * 3,
        name="adamw_small",
    )(g, w, m, v)


SMALL_ROWS = 8
SMALL_COLS = 1024


def _pack_small(g1, g2, g3, g4, gq, gk, sink, rel):
    row4 = jnp.concatenate([gq.reshape(-1), gk.reshape(-1), sink.reshape(-1)])
    row4 = jnp.pad(row4, (0, SMALL_COLS - row4.shape[0]))
    row5 = jnp.pad(rel.reshape(-1), (0, SMALL_COLS - rel.size))
    zero = jnp.zeros((SMALL_COLS,), F32)
    return jnp.stack([g1.reshape(-1), g2.reshape(-1), g3.reshape(-1), g4.reshape(-1), row4, row5, zero, zero])


def _unpack_small(p, hb):
    hd = HEAD_DIM
    return (p[0:1], p[1:2], p[2:3], p[3:4], p[4:5, 0:hd], p[4:5, hd:2 * hd], p[4:5, 2 * hd:2 * hd + hb],
            p[5, 0:N_BUCKETS * hb].reshape(N_BUCKETS, hb))


def _local_step(x, loss_target, win_g, wo_s, wup_s, wdn_s, g_pre_mix, g_post_mix, q_norm_a, k_norm_a, sink_b,
                rel_bias, g_pre_ffn, g_post_ffn):
    bl, s_len, d = x.shape
    t = bl * s_len
    nh = d // HEAD_DIM
    ha = nh // 2
    kva = ha // GROUP
    hb = nh - ha
    kvb = hb // GROUP
    tm = 512
    tw = min(2048, t)
    ts = min(512, s_len)
    tq, tk = BLOCK, min(512, s_len // 2)

    x2 = x.reshape(t, d)
    tg2 = loss_target.reshape(t, d)
    cos, sin_signed = _rope_tables(s_len)
    a = jnp.arange(BLOCK, dtype=jnp.int32)
    c = jnp.arange(SPAN, dtype=jnp.int32)
    bucket_t = _t5_bucket(c[:, None] - BLOCK - a[None, :])
    bucket_t4 = jnp.tile(bucket_t, (1, GROUP))

    h1, proj = _inproj(x2, g_pre_mix, win_g, tm)
    qa, ka, kat, va, vat, qb, kb, kbt, vb, vbt = _qkprep(
        proj, cos, sin_signed, q_norm_a, k_norm_a, bl, s_len, ha, kva, hb, kvb, ts)
    bias_t = _bias_build(bucket_t, rel_bias, hb)
    oa, lse_a, (wo_g, wup_g, wdn_g) = _attn_a_fwd(qa, ka, vat, tq, tk, [wo_s, wup_s, wdn_s])
    wo = wo_g.reshape(-1, d)
    wdn = wdn_g.reshape(-1, d)
    ob, lse_b = _attn_b_fwd(qb, kb, vbt, bias_t, sink_b, s_len)
    mix, x1, h2 = _mixout(oa, ob, wo, x2, g_post_mix, g_pre_ffn, tm)
    u, df, dy, dg4, loss8 = _ffn_fwd(h2, wup_g, wdn, x1, tg2, g_post_ffn, tm, FFN_BLOCKS_PER_STEP)

    dpre, dx1, dmix, dg3, dg2 = _ffn_bwd(df, u, wdn, wup_g, x1, dy, mix, g_pre_ffn, g_post_mix, tm,
                                         FFN_BLOCKS_PER_STEP)
    gw_dn = _wgrad_rows([u], df, N_DEV, tw, True, "wgrad_down")
    gw_up = _wgrad_cols(h2, dpre, N_DEV, tw, "wgrad_up")
    gw_o = _wgrad_rows([oa, ob], dmix, N_DEV, tw, False, "wgrad_o")
    doa, dob = _attn_out_bwd(dmix, wo, oa.shape[1], tm)
    dqa, dka, dva, (p_o, p_up, p_dn) = _attn_a_bwd(qa, ka, kat, va, doa, oa, lse_a, tq, tk, [gw_o, gw_up, gw_dn])
    dqb, dkb, dvb, dsum, dsink = _attn_b_bwd(qb, kb, kbt, vb, dob, ob, lse_b, bias_t, sink_b, s_len)
    drel_g, dsink_g = _bias_reduce(dsum, dsink, bucket_t4)
    dproj, dgq, dgk = _dqkprep(dqa, dka, dva, dqb, dkb, dvb, proj, cos, sin_signed, q_norm_a, k_norm_a, s_len, ts)
    gw_in = _wgrad_cols_blocked(h1, dproj, tw, "wgrad_in")
    grad_x, dg1 = _dx_final(dproj, win_g, x2, dx1, g_pre_mix, tm)

    drel = jnp.transpose(drel_g[:, :, 0:GROUP], (1, 0, 2)).reshape(N_BUCKETS, hb)
    dsink_v = dsink_g[:, 0, 0:GROUP].reshape(1, hb)
    small = _pack_small(dg1[0], dg2[0], dg3[0], dg4[0], dgq[0], dgk[0], dsink_v, drel)
    small = small.at[6, 0].set(loss8[0, 0])
    return grad_x.reshape(bl, s_len, d), gw_in, p_o, p_up, p_dn, small


def kernel(x, w_in, w_o, g_pre_mix, g_post_mix, q_norm_a, k_norm_a, sink_b, rel_bias, g_pre_ffn, w_ffn_up, w_ffn_down, g_post_ffn, loss_target, m_w_in, m_w_o, m_g_pre_mix, m_g_post_mix, m_q_norm_a, m_k_norm_a, m_sink_b, m_rel_bias, m_g_pre_ffn, m_w_ffn_up, m_w_ffn_down, m_g_post_ffn, v_w_in, v_w_o, v_g_pre_mix, v_g_post_mix, v_q_norm_a, v_k_norm_a, v_sink_b, v_rel_bias, v_g_pre_ffn, v_w_ffn_up, v_w_ffn_down, v_g_post_ffn):
    hb = sink_b.shape[1]
    d = x.shape[-1]
    (win_g,) = _weight_gather([w_in[0].astype(BF16)])

    grad_x, gw_in, p_o, p_up, p_dn, small = _local_step(
        x, loss_target, win_g, w_o[0].astype(BF16), w_ffn_up[0].astype(BF16), w_ffn_down[0].astype(BF16),
        g_pre_mix, g_post_mix, q_norm_a, k_norm_a, sink_b, rel_bias, g_pre_ffn, g_post_ffn)

    (p_in,) = _grad_exchange([gw_in])
    small = _small_allreduce(small)

    g_in, d_in, nm_in, nv_in = _adamw_sum(p_in, w_in[0], m_w_in[0], v_w_in[0], 256, "adamw_in")
    g_o, d_o, nm_o, nv_o = _adamw_sum(p_o, w_o[0], m_w_o[0], v_w_o[0], 128, "adamw_o")
    g_up, d_up, nm_up, nv_up = _adamw_sum(p_up, w_ffn_up[0], m_w_ffn_up[0], v_w_ffn_up[0], 256, "adamw_up")
    g_dn, d_dn, nm_dn, nv_dn = _adamw_sum(p_dn, w_ffn_down[0], m_w_ffn_down[0], v_w_ffn_down[0], 256, "adamw_down")

    pack = lambda *a: _pack_small(*a)
    w_s = pack(g_pre_mix, g_post_mix, g_pre_ffn, g_post_ffn, q_norm_a, k_norm_a, sink_b, rel_bias)
    m_s = pack(m_g_pre_mix, m_g_post_mix, m_g_pre_ffn, m_g_post_ffn, m_q_norm_a, m_k_norm_a, m_sink_b, m_rel_bias)
    v_s = pack(v_g_pre_mix, v_g_post_mix, v_g_pre_ffn, v_g_post_ffn, v_q_norm_a, v_k_norm_a, v_sink_b, v_rel_bias)
    d_s, nm_s, nv_s = _adamw_small(small, w_s, m_s, v_s)

    loss = small[6, 0]

    def outs(big_in, big_o, sm, big_up, big_dn):
        s1, s2, s3, s4, sq, sk, ss, sr = _unpack_small(sm, hb)
        return [big_in[None], big_o[None], s1, s2, sq, sk, ss, sr, s3, big_up[None], big_dn[None], s4]

    return (loss, grad_x,
            *outs(g_in, g_o, small, g_up, g_dn),
            *outs(d_in, d_o, d_s, d_up, d_dn),
            *outs(nm_in, nm_o, nm_s, nm_up, nm_dn),
            *outs(nv_in, nv_o, nv_s, nv_up, nv_dn))
```

```python
import functools

import jax
import jax.numpy as jnp
import numpy as np
from jax import lax
from jax.experimental import pallas as pl
from jax.experimental.pallas import tpu as pltpu

F32 = jnp.float32
BF16 = jnp.bfloat16
SDS = jax.ShapeDtypeStruct

N_DEV = 8
HEAD_DIM = 64
GROUP = 4
BLOCK = 128
SPAN = 3 * BLOCK
GRID_W = 64
N_BUCKETS = 32
MAX_DISTANCE = 128
ROPE_THETA = 10000.0
EPS = 1e-6
NEG_INF = -1e30
SCALE = HEAD_DIM ** -0.5
VT_PAD = 16

ADAM_LR = 0.001
ADAM_B1 = 0.9
ADAM_B2 = 0.999
ADAM_EPS = 1e-08
ADAM_WD = 0.01
ADAM_STEP = 10

VMEM_LIMIT = 56 * 1024 * 1024
MESH = pl.DeviceIdType.MESH


def _cp(*sem):
    return pltpu.CompilerParams(dimension_semantics=sem, vmem_limit_bytes=VMEM_LIMIT)


def _dot(a, b):
    return jnp.dot(a, b, preferred_element_type=F32)


def _dot_nt(a, b):
    return lax.dot_general(a, b, (((1,), (1,)), ((), ())), preferred_element_type=F32)


def _dot_tn(a, b):
    return lax.dot_general(a, b, (((0,), (0,)), ((), ())), preferred_element_type=F32)


def _rms_fwd(x, g):
    r = lax.rsqrt(jnp.mean(x * x, axis=-1, keepdims=True) + EPS)
    n = x * r
    return n * g, n, r


def _rms_bwd(n, r, g, dy):
    gd = g * dy
    dx = r * (gd - n * jnp.mean(n * gd, axis=-1, keepdims=True))
    return dx, dy * n


def _rope_tables(s_len):
    rows = s_len // GRID_W
    row = np.repeat(np.arange(rows, dtype=np.int32), GRID_W)
    col = np.tile(np.arange(GRID_W, dtype=np.int32), rows)
    nf = HEAD_DIM // 4
    freqs = np.float32(ROPE_THETA) ** (-np.arange(nf, dtype=np.float32) / np.float32(nf))
    ang_r = row.astype(np.float32)[:, None] * freqs[None, :]
    ang_c = col.astype(np.float32)[:, None] * freqs[None, :]
    cr, sr, cc, sc = np.cos(ang_r), np.sin(ang_r), np.cos(ang_c), np.sin(ang_c)
    cos = np.concatenate([cr, cr, cc, cc], axis=-1).astype(np.float32)
    sin_signed = np.concatenate([-sr, sr, -sc, sc], axis=-1).astype(np.float32)
    return jnp.asarray(cos), jnp.asarray(sin_signed)


def _t5_bucket(rel):
    nb = N_BUCKETS // 2
    ret = (rel > 0).astype(jnp.int32) * nb
    n = jnp.abs(rel)
    max_exact = nb // 2
    nf = jnp.maximum(n, 1).astype(F32)
    large = max_exact + (jnp.log(nf / max_exact) / np.float32(np.log(MAX_DISTANCE / max_exact))
                         * (nb - max_exact)).astype(jnp.int32)
    large = jnp.minimum(large, nb - 1)
    return ret + jnp.where(n < max_exact, n, large)


def _mesh_pos():
    return lax.axis_index("x"), lax.axis_index("y"), lax.axis_index("c")


def _lin(p):
    return 4 * p[0] + 2 * p[1] + p[2]


def _weight_gather(shards):
    n = len(shards)

    def body(*refs):
        xs, outs = refs[:n], refs[n:2 * n]
        send_sems, recv_sems, local_sems = refs[2 * n:]
        x, y, c = _mesh_pos()
        me, sibling = (x, y, c), (x, y, 1 - c)
        chips = [(1 - x, y), (x, 1 - y), (1 - x, 1 - y)]

        def copy(a, k, block, to, src=None):
            slot = outs[a].at[_lin(block)]
            return pltpu.make_async_remote_copy(
                src_ref=slot if src is None else src, dst_ref=slot,
                send_sem=send_sems.at[a, k], recv_sem=recv_sems.at[a, k],
                device_id=to, device_id_type=MESH)

        started = []
        for a in range(n):
            mine = pltpu.make_async_copy(xs[a], outs[a].at[_lin(me)], local_sems.at[a])
            mine.start()
            started.append(mine)
        sends = []
        for a in range(n):
            first = [copy(a, 0, me, sibling, src=xs[a])]
            first += [copy(a, 1 + j, me, (*chip, c), src=xs[a]) for j, chip in enumerate(chips)]
            for cp in first:
                cp.start()
            sends += first
        for a in range(n):
            for j, chip in enumerate(chips):
                copy(a, 1 + j, (*chip, c), me).wait_recv()
                fwd = copy(a, 4 + j, (*chip, c), sibling)
                fwd.start()
                sends.append(fwd)
        for a in range(n):
            copy(a, 0, sibling, me).wait_recv()
            for j, chip in enumerate(chips):
                copy(a, 4 + j, (*chip, 1 - c), me).wait_recv()
        for cp in sends:
            cp.wait_send()
        for mine in started:
            mine.wait()

    anyspec = pl.BlockSpec(memory_space=pl.ANY)
    return pl.pallas_call(
        body,
        out_shape=[SDS((N_DEV,) + s.shape, s.dtype) for s in shards],
        in_specs=[anyspec] * n,
        out_specs=[anyspec] * n,
        scratch_shapes=[pltpu.SemaphoreType.DMA((n, 7)), pltpu.SemaphoreType.DMA((n, 7)),
                        pltpu.SemaphoreType.DMA((n,))],
        name="weight_gather",
    )(*shards)


def _direct_exchange(kind, ins, outs, send_sems, recv_sems, local_sems):
    x, y, c = _mesh_pos()
    me = (x, y, c)
    peers = [(x, y, 1 - c), (1 - x, y, c), (x, 1 - y, c), (1 - x, 1 - y, c),
             (1 - x, y, 1 - c), (x, 1 - y, 1 - c), (1 - x, 1 - y, 1 - c)]

    def src(a, to):
        return ins[a] if kind == "gather" else ins[a].at[_lin(to)]

    def remote(a, k, to, frm):
        return pltpu.make_async_remote_copy(
            src_ref=src(a, to), dst_ref=outs[a].at[_lin(frm)],
            send_sem=send_sems.at[a, k], recv_sem=recv_sems.at[a, k],
            device_id=to, device_id_type=MESH)

    n = len(ins)
    sends = [remote(a, k, p, me) for a in range(n) for k, p in enumerate(peers)]
    arrivals = [remote(a, k, p, p) for a in range(n) for k, p in enumerate(peers)]
    local = [pltpu.make_async_copy(src(a, me), outs[a].at[_lin(me)], local_sems.at[a]) for a in range(n)]

    def start():
        for cp in local + sends:
            cp.start()

    def wait():
        for cp in arrivals:
            cp.wait_recv()
        for cp in sends:
            cp.wait_send()
        for cp in local:
            cp.wait()

    return start, wait


def _exchange_scratch(n):
    return [pltpu.SemaphoreType.DMA((n, 7)), pltpu.SemaphoreType.DMA((n, 7)), pltpu.SemaphoreType.DMA((n,))]


def _small_allreduce(v):
    rows, cols = v.shape

    def body(v_ref, out_ref, land_ref, send_sems, recv_sems):
        x, y, c = _mesh_pos()
        me = (x, y, c)
        peers = [(x, y, 1 - c), (1 - x, y, c), (x, 1 - y, c), (1 - x, 1 - y, c),
                 (1 - x, y, 1 - c), (x, 1 - y, 1 - c), (1 - x, 1 - y, 1 - c)]

        def copy(k, to, frm):
            return pltpu.make_async_remote_copy(
                src_ref=v_ref, dst_ref=land_ref.at[_lin(frm)],
                send_sem=send_sems.at[k], recv_sem=recv_sems.at[k],
                device_id=to, device_id_type=MESH)

        sends = [copy(k, p, me) for k, p in enumerate(peers)]
        for cp in sends:
            cp.start()
        land_ref[_lin(me)] = v_ref[...]
        for k, p in enumerate(peers):
            copy(k, p, p).wait_recv()
        for cp in sends:
            cp.wait_send()
        acc = land_ref[0]
        for s in range(1, N_DEV):
            acc = acc + land_ref[s]
        out_ref[...] = acc

    vm = pl.BlockSpec(memory_space=pltpu.VMEM)
    return pl.pallas_call(
        body,
        out_shape=SDS((rows, cols), F32),
        in_specs=[vm],
        out_specs=vm,
        scratch_shapes=[pltpu.VMEM((N_DEV, rows, cols), F32),
                        pltpu.SemaphoreType.DMA((7,)), pltpu.SemaphoreType.DMA((7,))],
        name="small_allreduce",
    )(v)


def _inproj(x2, g1, win_g, tm):
    t, d = x2.shape
    nd, _, pb = win_g.shape

    def body(x_ref, g_ref, w_ref, h_ref, p_ref):
        y, _, _ = _rms_fwd(x_ref[...], g_ref[...])
        h = y.astype(BF16)
        h_ref[...] = h
        for j in range(nd):
            p_ref[j] = _dot(h, w_ref[j])

    return pl.pallas_call(
        body,
        grid=(t // tm,),
        in_specs=[pl.BlockSpec((tm, d), lambda i: (i, 0)),
                  pl.BlockSpec((1, d), lambda i: (0, 0)),
                  pl.BlockSpec((nd, d, pb), lambda i: (0, 0, 0))],
        out_specs=[pl.BlockSpec((tm, d), lambda i: (i, 0)),
                   pl.BlockSpec((nd, tm, pb), lambda i: (0, i, 0))],
        out_shape=[SDS((t, d), BF16), SDS((nd, t, pb), F32)],
        compiler_params=_cp("parallel"),
        name="inproj",
    )(x2, g1, win_g)


def _head_slice(p_ref, hh, hpb):
    return p_ref[hh // hpb, :, pl.ds((hh % hpb) * HEAD_DIM, HEAD_DIM)]


def _rope(x, cos, sin_signed, first, transpose=False):
    def partner(v):
        return jnp.where(first, jnp.roll(v, -16, axis=1), jnp.roll(v, 16, axis=1))

    if transpose:
        return x * cos + partner(x * sin_signed)
    return x * cos + partner(x) * sin_signed


def _qkprep(proj, cos, sin_signed, gq, gk, bl, s_len, ha, kva, hb, kvb, ts):
    nd, t, pb = proj.shape
    hpb = pb // HEAD_DIM
    ns = s_len // ts
    sp = s_len + 2 * BLOCK

    def body(p_ref, cos_ref, sin_ref, gq_ref, gk_ref, qa_ref, ka_ref, kat_ref, va_ref, vat_ref, qb_ref, kb_ref,
             kbt_ref, vb_ref, vbt_ref):
        i = pl.program_id(1)
        cs, sn = cos_ref[...], sin_ref[...]
        lane = lax.broadcasted_iota(jnp.int32, (ts, HEAD_DIM), 1)
        first = (lane % 32) < 16
        ones_row = (lax.broadcasted_iota(jnp.int32, (VT_PAD, ts), 0) == 0).astype(BF16)

        def normrope(xh, g):
            y, _, _ = _rms_fwd(xh, g)
            return _rope(y, cs, sn, first)

        def transposed(xb):
            return xb.astype(F32).T.astype(BF16)

        for h in range(ha):
            qa_ref[0, h] = (normrope(_head_slice(p_ref, h, hpb), gq_ref[...]) * SCALE).astype(BF16)
        for h in range(kva):
            kh = normrope(_head_slice(p_ref, ha + h, hpb), gk_ref[...]).astype(BF16)
            ka_ref[0, h] = kh
            kat_ref[0, h] = transposed(kh)
            vh = _head_slice(p_ref, ha + kva + h, hpb).astype(BF16)
            va_ref[0, h] = vh
            vat_ref[0, h, 0:HEAD_DIM, :] = transposed(vh)
            vat_ref[0, h, HEAD_DIM:HEAD_DIM + VT_PAD, :] = ones_row
        base = ha + 2 * kva
        for h in range(hb):
            qb_ref[0, h] = (_head_slice(p_ref, base + h, hpb) * SCALE).astype(BF16)

        @pl.when(i == 0)
        def _():
            zeros = jnp.zeros((kvb, BLOCK, HEAD_DIM), BF16)
            zeros_t = jnp.zeros((kvb, HEAD_DIM + VT_PAD, BLOCK), BF16)
            for ref in (kb_ref, vb_ref):
                ref[0, :, 0:BLOCK, :] = zeros
                ref[0, :, sp - BLOCK:sp, :] = zeros
            kbt_ref[0, :, :, 0:BLOCK] = zeros_t[:, 0:HEAD_DIM]
            kbt_ref[0, :, :, sp - BLOCK:sp] = zeros_t[:, 0:HEAD_DIM]
            vbt_ref[0, :, :, 0:BLOCK] = zeros_t
            vbt_ref[0, :, :, sp - BLOCK:sp] = zeros_t

        row0 = pl.multiple_of(BLOCK + i * ts, BLOCK)
        for h in range(kvb):
            kh = _head_slice(p_ref, base + hb + h, hpb).astype(BF16)
            vh = _head_slice(p_ref, base + hb + kvb + h, hpb).astype(BF16)
            kb_ref[0, h, pl.ds(row0, ts), :] = kh
            vb_ref[0, h, pl.ds(row0, ts), :] = vh
            kbt_ref[0, h, :, pl.ds(row0, ts)] = transposed(kh)
            vbt_ref[0, h, 0:HEAD_DIM, pl.ds(row0, ts)] = transposed(vh)
            vbt_ref[0, h, HEAD_DIM:HEAD_DIM + VT_PAD, pl.ds(row0, ts)] = ones_row

    def hm(nh):
        return pl.BlockSpec((1, nh, ts, HEAD_DIM), lambda b, i: (b, 0, i, 0))

    def padded(nh):
        return pl.BlockSpec((1, nh, sp, HEAD_DIM), lambda b, i: (b, 0, 0, 0))

    def padded_t(nh, rows):
        return pl.BlockSpec((1, nh, rows, sp), lambda b, i: (b, 0, 0, 0))

    return pl.pallas_call(
        body,
        grid=(bl, ns),
        in_specs=[pl.BlockSpec((nd, ts, pb), lambda b, i: (0, b * ns + i, 0)),
                  pl.BlockSpec((ts, HEAD_DIM), lambda b, i: (i, 0)),
                  pl.BlockSpec((ts, HEAD_DIM), lambda b, i: (i, 0)),
                  pl.BlockSpec((1, HEAD_DIM), lambda b, i: (0, 0)),
                  pl.BlockSpec((1, HEAD_DIM), lambda b, i: (0, 0))],
        out_specs=[hm(ha), hm(kva), pl.BlockSpec((1, kva, HEAD_DIM, ts), lambda b, i: (b, 0, 0, i)), hm(kva),
                   pl.BlockSpec((1, kva, HEAD_DIM + VT_PAD, ts), lambda b, i: (b, 0, 0, i)),
                   hm(hb), padded(kvb), padded_t(kvb, HEAD_DIM), padded(kvb), padded_t(kvb, HEAD_DIM + VT_PAD)],
        out_shape=[SDS((bl, ha, s_len, HEAD_DIM), BF16), SDS((bl, kva, s_len, HEAD_DIM), BF16),
                   SDS((bl, kva, HEAD_DIM, s_len), BF16),
                   SDS((bl, kva, s_len, HEAD_DIM), BF16), SDS((bl, kva, HEAD_DIM + VT_PAD, s_len), BF16),
                   SDS((bl, hb, s_len, HEAD_DIM), BF16),
                   SDS((bl, kvb, sp, HEAD_DIM), BF16), SDS((bl, kvb, HEAD_DIM, sp), BF16),
                   SDS((bl, kvb, sp, HEAD_DIM), BF16), SDS((bl, kvb, HEAD_DIM + VT_PAD, sp), BF16)],
        compiler_params=_cp("parallel", "arbitrary"),
        name="qkprep",
    )(proj, cos, sin_signed, gq, gk)


def _bias_build(bucket_t, rel_bias, hb):
    kvb = hb // GROUP

    def body(bkt_ref, tbl_ref, out_ref):
        bkt = bkt_ref[...]
        ci = lax.broadcasted_iota(jnp.int32, (SPAN, BLOCK), 0)
        qi = lax.broadcasted_iota(jnp.int32, (SPAN, BLOCK), 1)
        band = jnp.abs(ci - BLOCK - qi) <= BLOCK
        masks = (band, band & (ci >= BLOCK), band & (ci < 2 * BLOCK))
        for h in range(hb):
            acct = jnp.zeros((SPAN, BLOCK), F32)
            for b in range(N_BUCKETS):
                acct = jnp.where(bkt == b, tbl_ref[b, h], acct)
            lanes = slice((h % GROUP) * BLOCK, (h % GROUP + 1) * BLOCK)
            for var, mask in enumerate(masks):
                out_ref[var, h // GROUP, :, lanes] = jnp.where(mask, acct, NEG_INF)

    vm = pl.BlockSpec(memory_space=pltpu.VMEM)
    return pl.pallas_call(
        body,
        in_specs=[vm, pl.BlockSpec(memory_space=pltpu.SMEM)],
        out_specs=vm,
        out_shape=SDS((3, kvb, SPAN, GROUP * BLOCK), F32),
        name="bias_build",
    )(bucket_t, rel_bias)


def _attn_a_fwd(qa, ka, vat, tq, tk, shards):
    bl, ha, s_len, _ = qa.shape
    kv = ka.shape[1]
    va_rows = vat.shape[2]
    nq, nk = s_len // tq, s_len // tk
    assert nk % 2 == 0
    r = GROUP * tq
    ns = len(shards)

    def body(q_ref, k_ref, v_ref, *rest):
        shard_refs, (o_ref, l_ref), gathered = rest[:ns], rest[ns:ns + 2], rest[ns + 2:2 * ns + 2]
        st_sc, send_sems, recv_sems, local_sems = rest[2 * ns + 2:]
        step_id = (pl.program_id(0) * kv + pl.program_id(1)) * nq + pl.program_id(2)
        start, wait = _direct_exchange("gather", shard_refs, gathered, send_sems, recv_sems, local_sems)
        pl.when(step_id == 0)(start)

        q = q_ref[0].reshape(r, HEAD_DIM)

        def scores(c):
            return _dot_nt(k_ref[0, 0, pl.ds(pl.multiple_of(c * tk, tk), tk), :], q)

        def fold(st, c, carry):
            m_old, acc = carry
            m_new = jnp.maximum(m_old, jnp.max(st, axis=0, keepdims=True))
            pt = jnp.exp(st - m_new).astype(BF16)
            vt = v_ref[0, 0, :, pl.ds(pl.multiple_of(c * tk, tk), tk)]
            return m_new, jnp.exp(m_old - m_new) * acc + _dot(vt, pt)

        st_sc[0] = scores(0)

        def step(c2, carry):
            c = 2 * c2
            st_sc[1] = scores(c + 1)
            carry = fold(st_sc[0], c, carry)
            st_sc[0] = scores(jnp.minimum(c + 2, nk - 1))
            return fold(st_sc[1], c + 1, carry)

        m, acc = lax.fori_loop(0, nk // 2, step,
                               (jnp.full((1, r), -jnp.inf, F32), jnp.zeros((va_rows, r), F32)))
        l = acc[HEAD_DIM:HEAD_DIM + 1, :]
        o = (acc[0:HEAD_DIM, :] / l).T
        for h in range(GROUP):
            o_ref[:, h * HEAD_DIM:(h + 1) * HEAD_DIM] = o[h * tq:(h + 1) * tq].astype(BF16)
        l_ref[0, 0, 0] = jnp.broadcast_to(m + jnp.log(l), (8, r))
        pl.when(step_id == bl * kv * nq - 1)(wait)

    anyspec = pl.BlockSpec(memory_space=pl.ANY)
    res = pl.pallas_call(
        body,
        grid=(bl, kv, nq),
        in_specs=[pl.BlockSpec((1, GROUP, tq, HEAD_DIM), lambda b, g, i: (b, g, i, 0)),
                  pl.BlockSpec((1, 1, s_len, HEAD_DIM), lambda b, g, i: (b, g, 0, 0)),
                  pl.BlockSpec((1, 1, va_rows, s_len), lambda b, g, i: (b, g, 0, 0))] + [anyspec] * ns,
        out_specs=[pl.BlockSpec((tq, GROUP * HEAD_DIM), lambda b, g, i: (b * nq + i, g)),
                   pl.BlockSpec((1, 1, 1, 8, r), lambda b, g, i: (b, g, i, 0, 0))] + [anyspec] * ns,
        out_shape=[SDS((bl * s_len, ha * HEAD_DIM), BF16), SDS((bl, kv, nq, 8, r), F32)]
        + [SDS((N_DEV,) + s.shape, s.dtype) for s in shards],
        scratch_shapes=[pltpu.VMEM((2, tk, r), F32)] + _exchange_scratch(ns),
        compiler_params=_cp("arbitrary", "arbitrary", "arbitrary"),
        name="attn_a_fwd",
    )(qa, ka, vat, *shards)
    return res[0], res[1], res[2:]


FFN_BLOCKS_PER_STEP = 4
QB_PER_STEP = 4


def _bias_variant(n, nb):
    return jnp.where(n == 0, 1, jnp.where(n == nb - 1, 2, 0))


def _sink_row(sink_ref, g):
    return jnp.concatenate([jnp.full((1, BLOCK), sink_ref[0, g * GROUP + h], F32) for h in range(GROUP)], axis=1)


def _attn_b_fwd(qb, kb, vbt, bias_t, sink, s_len):
    bl, hb, _, _ = qb.shape
    kv = kb.shape[1]
    sp = kb.shape[2]
    vt_rows = vbt.shape[2]
    nb = s_len // BLOCK
    nbs = min(QB_PER_STEP, nb)
    r = GROUP * BLOCK

    def body(q_ref, k_ref, vt_ref, bt_ref, sink_ref, o_ref, l_ref):
        g, n0 = pl.program_id(1), pl.program_id(2) * nbs
        sink_row = _sink_row(sink_ref, g)
        for j in range(nbs):
            n = n0 + j
            span = pl.ds(pl.multiple_of(n * BLOCK, BLOCK), SPAN)
            q = q_ref[0, :, j * BLOCK:(j + 1) * BLOCK, :].reshape(r, HEAD_DIM)
            st = _dot_nt(k_ref[0, 0, span, :], q) + bt_ref[_bias_variant(n, nb), 0]
            m = jnp.maximum(jnp.max(st, axis=0, keepdims=True), sink_row)
            acc = _dot(vt_ref[0, 0, :, span], jnp.exp(st - m).astype(BF16))
            l = acc[HEAD_DIM:HEAD_DIM + 1, :] + jnp.exp(sink_row - m)
            o = (acc[0:HEAD_DIM, :] / l).T
            for h in range(GROUP):
                o_ref[j * BLOCK:(j + 1) * BLOCK, h * HEAD_DIM:(h + 1) * HEAD_DIM] = (
                    o[h * BLOCK:(h + 1) * BLOCK].astype(BF16))
            l_ref[0, 0, j] = jnp.broadcast_to(m + jnp.log(l), (8, r))

    return pl.pallas_call(
        body,
        grid=(bl, kv, nb // nbs),
        in_specs=[pl.BlockSpec((1, GROUP, nbs * BLOCK, HEAD_DIM), lambda b, g, n: (b, g, n, 0)),
                  pl.BlockSpec((1, 1, sp, HEAD_DIM), lambda b, g, n: (b, g, 0, 0)),
                  pl.BlockSpec((1, 1, vt_rows, sp), lambda b, g, n: (b, g, 0, 0)),
                  pl.BlockSpec((3, 1, SPAN, r), lambda b, g, n: (0, g, 0, 0)),
                  pl.BlockSpec(memory_space=pltpu.SMEM)],
        out_specs=[pl.BlockSpec((nbs * BLOCK, GROUP * HEAD_DIM), lambda b, g, n: (b * (nb // nbs) + n, g)),
                   pl.BlockSpec((1, 1, nbs, 8, r), lambda b, g, n: (b, g, n, 0, 0))],
        out_shape=[SDS((bl * s_len, hb * HEAD_DIM), BF16), SDS((bl, kv, nb, 8, r), F32)],
        compiler_params=_cp("parallel", "parallel", "arbitrary"),
        name="attn_b_fwd",
    )(qb, kb, vbt, bias_t, sink)


def _mixout(oa, ob, wo, x2, g2, g3, tm):
    t, d = x2.shape
    ca = oa.shape[1]

    def body(oa_ref, ob_ref, w_ref, x_ref, g2_ref, g3_ref, mix_ref, x1_ref, h2_ref):
        mix = _dot(oa_ref[...], w_ref[0:ca, :]) + _dot(ob_ref[...], w_ref[ca:, :])
        mix_ref[...] = mix
        y2, _, _ = _rms_fwd(mix, g2_ref[...])
        x1 = x_ref[...] + y2
        x1_ref[...] = x1
        y3, _, _ = _rms_fwd(x1, g3_ref[...])
        h2_ref[...] = y3.astype(BF16)

    tile = lambda w: pl.BlockSpec((tm, w), lambda i: (i, 0))
    vec = pl.BlockSpec((1, d), lambda i: (0, 0))
    return pl.pallas_call(
        body,
        grid=(t // tm,),
        in_specs=[tile(ca), tile(ob.shape[1]), pl.BlockSpec(wo.shape, lambda i: (0, 0)), tile(d), vec, vec],
        out_specs=[tile(d), tile(d), tile(d)],
        out_shape=[SDS((t, d), F32), SDS((t, d), F32), SDS((t, d), BF16)],
        compiler_params=_cp("parallel"),
        name="mixout",
    )(oa, ob, wo, x2, g2, g3)


def _ffn_fwd(h2, wup_g, wdn, x1, target, g4, tm, jb):
    t, d = x1.shape
    nblk, _, tf = wup_g.shape
    ff = nblk * tf
    nt = t // tm
    nj = nblk // jb

    def body(h_ref, wu_ref, wd_ref, x1_ref, tg_ref, g_ref, u_ref, df_ref, dy_ref, dg_ref, loss_ref, acc_sc):
        i, j = pl.program_id(0), pl.program_id(1)

        @pl.when(j == 0)
        def _():
            acc_sc[...] = jnp.zeros_like(acc_sc)

        @pl.when((i == 0) & (j == 0))
        def _():
            dg_ref[...] = jnp.zeros_like(dg_ref)
            loss_ref[...] = jnp.zeros_like(loss_ref)

        h = h_ref[...]
        squares = []
        for s in range(jb):
            u = jnp.maximum(_dot(h, wu_ref[s]), 0.0)
            u_ref[:, s * tf:(s + 1) * tf] = u.astype(BF16)
            squares.append((u * u).astype(BF16))
        acc_sc[...] += _dot(jnp.concatenate(squares, axis=1), wd_ref[...])

        @pl.when(j == nj - 1)
        def _():
            g = g_ref[...]
            y4, n, r = _rms_fwd(acc_sc[...], g)
            e = (x1_ref[...] + y4) - tg_ref[...]
            loss_ref[...] += jnp.sum(e * e) * (0.5 / d)
            dy = e * (1.0 / d)
            dy_ref[...] = dy
            df, dgt = _rms_bwd(n, r, g, dy)
            df_ref[...] = df.astype(BF16)
            dg_ref[0:1, :] += jnp.sum(dgt, axis=0, keepdims=True)

    tile = pl.BlockSpec((tm, d), lambda i, j: (i, 0))
    return pl.pallas_call(
        body,
        grid=(nt, nj),
        in_specs=[tile,
                  pl.BlockSpec((jb, d, tf), lambda i, j: (j, 0, 0)),
                  pl.BlockSpec((jb * tf, d), lambda i, j: (j, 0)),
                  tile, tile,
                  pl.BlockSpec((1, d), lambda i, j: (0, 0))],
        out_specs=[pl.BlockSpec((tm, jb * tf), lambda i, j: (i, j)), tile, tile,
                   pl.BlockSpec((8, d), lambda i, j: (0, 0)),
                   pl.BlockSpec((8, 128), lambda i, j: (0, 0))],
        out_shape=[SDS((t, ff), BF16), SDS((t, d), BF16), SDS((t, d), F32), SDS((8, d), F32), SDS((8, 128), F32)],
        scratch_shapes=[pltpu.VMEM((tm, d), F32)],
        compiler_params=_cp("arbitrary", "arbitrary"),
        name="ffn_fwd",
    )(h2, wup_g, wdn, x1, target, g4)


def _ffn_bwd(df, u, wdn, wup_g, x1, dy, mix, g3, g2, tm, jb):
    t, d = x1.shape
    nblk, _, tf = wup_g.shape
    nt = t // tm
    nj = nblk // jb

    def body(df_ref, u_ref, wd_ref, wu_ref, x1_ref, dy_ref, mix_ref, g3_ref, g2_ref,
             dpre_ref, dx1_ref, dmix_ref, dg3_ref, dg2_ref, acc_sc):
        i, j = pl.program_id(0), pl.program_id(1)

        @pl.when(j == 0)
        def _():
            acc_sc[...] = jnp.zeros_like(acc_sc)

        @pl.when((i == 0) & (j == 0))
        def _():
            dg3_ref[...] = jnp.zeros_like(dg3_ref)
            dg2_ref[...] = jnp.zeros_like(dg2_ref)

        du2 = _dot_nt(df_ref[...], wd_ref[...])
        dpre = (2.0 * u_ref[...].astype(F32) * du2).astype(BF16)
        dpre_ref[...] = dpre
        dh = _dot_nt(dpre[:, 0:tf], wu_ref[0])
        for s in range(1, jb):
            dh = dh + _dot_nt(dpre[:, s * tf:(s + 1) * tf], wu_ref[s])
        acc_sc[...] += dh

        @pl.when(j == nj - 1)
        def _():
            g3, g2 = g3_ref[...], g2_ref[...]
            _, n3, r3 = _rms_fwd(x1_ref[...], g3)
            dx, dgt3 = _rms_bwd(n3, r3, g3, acc_sc[...])
            dx1 = dy_ref[...] + dx
            dx1_ref[...] = dx1
            dg3_ref[0:1, :] += jnp.sum(dgt3, axis=0, keepdims=True)
            _, n2, r2 = _rms_fwd(mix_ref[...], g2)
            dmix, dgt2 = _rms_bwd(n2, r2, g2, dx1)
            dmix_ref[...] = dmix.astype(BF16)
            dg2_ref[0:1, :] += jnp.sum(dgt2, axis=0, keepdims=True)

    tile = pl.BlockSpec((tm, d), lambda i, j: (i, 0))
    vec = pl.BlockSpec((1, d), lambda i, j: (0, 0))
    acc8 = pl.BlockSpec((8, d), lambda i, j: (0, 0))
    return pl.pallas_call(
        body,
        grid=(nt, nj),
        in_specs=[tile,
                  pl.BlockSpec((tm, jb * tf), lambda i, j: (i, j)),
                  pl.BlockSpec((jb * tf, d), lambda i, j: (j, 0)),
                  pl.BlockSpec((jb, d, tf), lambda i, j: (j, 0, 0)),
                  tile, tile, tile, vec, vec],
        out_specs=[pl.BlockSpec((tm, jb * tf), lambda i, j: (i, j)), tile, tile, acc8, acc8],
        out_shape=[SDS(u.shape, BF16), SDS((t, d), F32), SDS((t, d), BF16), SDS((8, d), F32), SDS((8, d), F32)],
        scratch_shapes=[pltpu.VMEM((tm, d), F32)],
        compiler_params=_cp("arbitrary", "arbitrary"),
        name="ffn_bwd",
    )(df, u, wdn, wup_g, x1, dy, mix, g3, g2)


def _wgrad_cols(a, b, nj, tt, name):
    t, m = a.shape
    n = b.shape[1]
    bn = n // nj

    def body(a_ref, b_ref, o_ref):
        @pl.when(pl.program_id(1) == 0)
        def _():
            o_ref[...] = jnp.zeros_like(o_ref)

        o_ref[0] += _dot_tn(a_ref[...], b_ref[...])

    return pl.pallas_call(
        body,
        grid=(nj, t // tt),
        in_specs=[pl.BlockSpec((tt, m), lambda j, k: (k, 0)),
                  pl.BlockSpec((tt, bn), lambda j, k: (k, j))],
        out_specs=pl.BlockSpec((1, m, bn), lambda j, k: (j, 0, 0)),
        out_shape=SDS((nj, m, bn), F32),
        compiler_params=_cp("parallel", "arbitrary"),
        name=name,
    )(a, b)


def _wgrad_cols_blocked(a, b3, tt, name):
    t, m = a.shape
    nj, _, bn = b3.shape

    def body(a_ref, b_ref, o_ref):
        @pl.when(pl.program_id(1) == 0)
        def _():
            o_ref[...] = jnp.zeros_like(o_ref)

        o_ref[0] += _dot_tn(a_ref[...], b_ref[0])

    return pl.pallas_call(
        body,
        grid=(nj, t // tt),
        in_specs=[pl.BlockSpec((tt, m), lambda j, k: (k, 0)),
                  pl.BlockSpec((1, tt, bn), lambda j, k: (j, k, 0))],
        out_specs=pl.BlockSpec((1, m, bn), lambda j, k: (j, 0, 0)),
        out_shape=SDS((nj, m, bn), F32),
        compiler_params=_cp("parallel", "arbitrary"),
        name=name,
    )(a, b3)


def _wgrad_rows(a_parts, b, nj, tt, square, name):
    t, n = b.shape
    widths = [p.shape[1] for p in a_parts]
    m = sum(widths)
    bm = m // nj
    per = [w // bm for w in widths]
    starts = [sum(per[:q]) for q in range(len(per))]
    np_ = len(a_parts)

    def body(*refs):
        a_refs, b_ref, o_ref = refs[:np_], refs[np_], refs[np_ + 1]
        j = pl.program_id(0)

        @pl.when(pl.program_id(1) == 0)
        def _():
            o_ref[...] = jnp.zeros_like(o_ref)

        for q in range(np_):
            @pl.when((j >= starts[q]) & (j < starts[q] + per[q]))
            def _(q=q):
                a = a_refs[q][...]
                if square:
                    af = a.astype(F32)
                    a = (af * af).astype(BF16)
                o_ref[0] += _dot_tn(a, b_ref[...])

    def a_spec(q):
        return pl.BlockSpec((tt, bm), lambda j, k: (k, jnp.clip(j - starts[q], 0, per[q] - 1)))

    return pl.pallas_call(
        body,
        grid=(nj, t // tt),
        in_specs=[a_spec(q) for q in range(np_)] + [pl.BlockSpec((tt, n), lambda j, k: (k, 0))],
        out_specs=pl.BlockSpec((1, bm, n), lambda j, k: (j, 0, 0)),
        out_shape=SDS((nj, bm, n), F32),
        compiler_params=_cp("parallel", "arbitrary"),
        name=name,
    )(*a_parts, b)


def _attn_out_bwd(dmix, wo, ca, tm):
    t, d = dmix.shape
    cb = wo.shape[0] - ca

    def body(dm_ref, w_ref, da_ref, db_ref):
        dm = dm_ref[...]
        da_ref[...] = _dot_nt(dm, w_ref[0:ca, :]).astype(BF16)
        db_ref[...] = _dot_nt(dm, w_ref[ca:, :]).astype(BF16)

    return pl.pallas_call(
        body,
        grid=(t // tm,),
        in_specs=[pl.BlockSpec((tm, d), lambda i: (i, 0)), pl.BlockSpec(wo.shape, lambda i: (0, 0))],
        out_specs=[pl.BlockSpec((tm, ca), lambda i: (i, 0)), pl.BlockSpec((tm, cb), lambda i: (i, 0))],
        out_shape=[SDS((t, ca), BF16), SDS((t, cb), BF16)],
        compiler_params=_cp("parallel"),
        name="attn_out_bwd",
    )(dmix, wo)


def _stack_heads(ref, rows):
    return jnp.concatenate([ref[:, h * HEAD_DIM:(h + 1) * HEAD_DIM] for h in range(GROUP)], axis=0)


def _attn_a_bwd(qa, ka, kat, va, do, o, lse, tq, tk, grads):
    bl, ha, s_len, _ = qa.shape
    kv = ka.shape[1]
    nq, nk = s_len // tq, s_len // tk
    assert nk % 2 == 0
    r = GROUP * tq
    ng = len(grads)

    def body(q_ref, k_ref, kt_ref, v_ref, do_ref, o_ref, l_ref, *rest):
        grad_refs, (dq_ref, dk_ref, dv_ref), parts = rest[:ng], rest[ng:ng + 3], rest[ng + 3:2 * ng + 3]
        st_sc, dp_sc, dkt_sc, dvt_sc, send_sems, recv_sems, local_sems = rest[2 * ng + 3:]
        i = pl.program_id(2)
        step_id = (pl.program_id(0) * kv + pl.program_id(1)) * nq + i
        start, wait = _direct_exchange("scatter", grad_refs, parts, send_sems, recv_sems, local_sems)
        pl.when(step_id == 0)(start)

        q = q_ref[0].reshape(r, HEAD_DIM)
        do2 = _stack_heads(do_ref, tq)
        qt = q.astype(F32).T
        dot32 = do2.astype(F32).T
        ot32 = _stack_heads(o_ref, tq).astype(F32).T
        drow = jnp.sum(dot32 * ot32, axis=0, keepdims=True)
        qt, dot = qt.astype(BF16), dot32.astype(BF16)
        lrow = l_ref[0, 0, 0, 0:1, :]

        @pl.when(i == 0)
        def _():
            dkt_sc[...] = jnp.zeros_like(dkt_sc)
            dvt_sc[...] = jnp.zeros_like(dvt_sc)

        def chunk(c):
            return pl.ds(pl.multiple_of(c * tk, tk), tk)

        def scores(c, slot):
            st_sc[slot] = _dot_nt(k_ref[0, 0, chunk(c), :], q)
            dp_sc[slot] = _dot_nt(v_ref[0, 0, chunk(c), :], do2)

        def fold(slot, c, dqt):
            pt = jnp.exp(st_sc[slot] - lrow)
            dsb = (pt * (dp_sc[slot] - drow)).astype(BF16)
            dvt_sc[:, chunk(c)] += _dot_nt(dot, pt.astype(BF16))
            dkt_sc[:, chunk(c)] += _dot_nt(qt, dsb)
            return dqt + _dot(kt_ref[0, 0, :, chunk(c)], dsb)

        scores(0, 0)

        def step(c2, dqt):
            c = 2 * c2
            scores(c + 1, 1)
            dqt = fold(0, c, dqt)
            scores(jnp.minimum(c + 2, nk - 1), 0)
            return fold(1, c + 1, dqt)

        dqt = lax.fori_loop(0, nk // 2, step, jnp.zeros((HEAD_DIM, r), F32))
        dq_ref[0] = dqt.T.reshape(GROUP, tq, HEAD_DIM)

        @pl.when(i == nq - 1)
        def _():
            dk_ref[0, 0] = dkt_sc[...].T
            dv_ref[0, 0] = dvt_sc[...].T

        pl.when(step_id == bl * kv * nq - 1)(wait)

    kvspec = pl.BlockSpec((1, 1, s_len, HEAD_DIM), lambda b, g, i: (b, g, 0, 0))
    qspec = pl.BlockSpec((1, GROUP, tq, HEAD_DIM), lambda b, g, i: (b, g, i, 0))
    tok = pl.BlockSpec((tq, GROUP * HEAD_DIM), lambda b, g, i: (b * nq + i, g))
    anyspec = pl.BlockSpec(memory_space=pl.ANY)
    res = pl.pallas_call(
        body,
        grid=(bl, kv, nq),
        in_specs=[qspec, kvspec, pl.BlockSpec((1, 1, HEAD_DIM, s_len), lambda b, g, i: (b, g, 0, 0)), kvspec,
                  tok, tok, pl.BlockSpec((1, 1, 1, 8, r), lambda b, g, i: (b, g, i, 0, 0))] + [anyspec] * ng,
        out_specs=[qspec, kvspec, kvspec] + [anyspec] * ng,
        out_shape=[SDS(qa.shape, F32), SDS(ka.shape, F32), SDS(va.shape, F32)]
        + [SDS(g.shape, g.dtype) for g in grads],
        scratch_shapes=[pltpu.VMEM((2, tk, r), F32), pltpu.VMEM((2, tk, r), F32),
                        pltpu.VMEM((HEAD_DIM, s_len), F32), pltpu.VMEM((HEAD_DIM, s_len), F32)]
        + _exchange_scratch(ng),
        compiler_params=_cp("arbitrary", "arbitrary", "arbitrary"),
        name="attn_a_bwd",
    )(qa, ka, kat, va, do, o, lse, *grads)
    return res[0], res[1], res[2], res[3:]


def _attn_b_bwd(qb, kb, kbt, vb, do, o, lse, bias_t, sink, s_len):
    bl, hb, _, _ = qb.shape
    kv, sp = kb.shape[1], kb.shape[2]
    nb = s_len // BLOCK
    nbs = min(QB_PER_STEP, nb)
    r = GROUP * BLOCK

    def body(q_ref, k_ref, kt_ref, v_ref, do_ref, o_ref, l_ref, bt_ref, sink_ref,
             dq_ref, dk_ref, dv_ref, dsum_ref, dsink_ref, dkt_sc, dvt_sc):
        g, b, ns = pl.program_id(0), pl.program_id(1), pl.program_id(2)
        sink_row = _sink_row(sink_ref, g)

        @pl.when(ns == 0)
        def _():
            dkt_sc[...] = jnp.zeros_like(dkt_sc)
            dvt_sc[...] = jnp.zeros_like(dvt_sc)

        @pl.when((b == 0) & (ns == 0))
        def _():
            dsum_ref[...] = jnp.zeros_like(dsum_ref)
            dsink_ref[...] = jnp.zeros_like(dsink_ref)

        dsum = jnp.zeros((SPAN, r), F32)
        dsink = jnp.zeros((1, r), F32)
        for j in range(nbs):
            n = ns * nbs + j
            span = pl.ds(pl.multiple_of(n * BLOCK, BLOCK), SPAN)
            rows = slice(j * BLOCK, (j + 1) * BLOCK)
            q = q_ref[0, :, rows, :].reshape(r, HEAD_DIM)
            do2 = jnp.concatenate([do_ref[rows, h * HEAD_DIM:(h + 1) * HEAD_DIM] for h in range(GROUP)], axis=0)
            o2 = jnp.concatenate([o_ref[rows, h * HEAD_DIM:(h + 1) * HEAD_DIM] for h in range(GROUP)], axis=0)
            dot32 = do2.astype(F32).T
            drow = jnp.sum(dot32 * o2.astype(F32).T, axis=0, keepdims=True)
            qt, dot = q.astype(F32).T.astype(BF16), dot32.astype(BF16)
            lrow = l_ref[0, 0, j, 0:1, :]
            st = _dot_nt(k_ref[0, 0, span, :], q) + bt_ref[_bias_variant(n, nb), 0]
            pt = jnp.exp(st - lrow)
            dst = pt * (_dot_nt(v_ref[0, 0, span, :], do2) - drow)
            dsum = dsum + dst
            dsink = dsink - jnp.exp(sink_row - lrow) * drow
            dsb = dst.astype(BF16)
            dvt_sc[:, span] += _dot_nt(dot, pt.astype(BF16))
            dkt_sc[:, span] += _dot_nt(qt, dsb)
            dq_ref[0, :, rows, :] = _dot(kt_ref[0, 0, :, span], dsb).T.reshape(GROUP, BLOCK, HEAD_DIM)
        dsum_ref[0] += dsum
        dsink_ref[0, 0:1, :] += dsink

        @pl.when(ns == nb // nbs - 1)
        def _():
            dk_ref[0, 0] = dkt_sc[:, BLOCK:BLOCK + s_len].T
            dv_ref[0, 0] = dvt_sc[:, BLOCK:BLOCK + s_len].T

    kvspec = pl.BlockSpec((1, 1, sp, HEAD_DIM), lambda g, b, n: (b, g, 0, 0))
    kvout = pl.BlockSpec((1, 1, s_len, HEAD_DIM), lambda g, b, n: (b, g, 0, 0))
    qspec = pl.BlockSpec((1, GROUP, nbs * BLOCK, HEAD_DIM), lambda g, b, n: (b, g, n, 0))
    tok = pl.BlockSpec((nbs * BLOCK, GROUP * HEAD_DIM), lambda g, b, n: (b * (nb // nbs) + n, g))
    return pl.pallas_call(
        body,
        grid=(kv, bl, nb // nbs),
        in_specs=[qspec, kvspec, pl.BlockSpec((1, 1, HEAD_DIM, sp), lambda g, b, n: (b, g, 0, 0)), kvspec, tok, tok,
                  pl.BlockSpec((1, 1, nbs, 8, r), lambda g, b, n: (b, g, n, 0, 0)),
                  pl.BlockSpec((3, 1, SPAN, r), lambda g, b, n: (0, g, 0, 0)),
                  pl.BlockSpec(memory_space=pltpu.SMEM)],
        out_specs=[qspec, kvout, kvout,
                   pl.BlockSpec((1, SPAN, r), lambda g, b, n: (g, 0, 0)),
                   pl.BlockSpec((1, 8, r), lambda g, b, n: (g, 0, 0))],
        out_shape=[SDS(qb.shape, F32), SDS((bl, kv, s_len, HEAD_DIM), F32), SDS((bl, kv, s_len, HEAD_DIM), F32),
                   SDS((kv, SPAN, r), F32), SDS((kv, 8, r), F32)],
        scratch_shapes=[pltpu.VMEM((HEAD_DIM, sp), F32), pltpu.VMEM((HEAD_DIM, sp), F32)],
        compiler_params=_cp("arbitrary", "arbitrary", "arbitrary"),
        name="attn_b_bwd",
    )(qb, kb, kbt, vb, do, o, lse, bias_t, sink)


def _bias_reduce(dsum, dsink, bucket_t4):
    kv, _, r = dsum.shape

    def body(ds_ref, dk_ref, bk_ref, rel_ref, sink_ref):
        lane = lax.broadcasted_iota(jnp.int32, (N_BUCKETS, 128), 1)
        lane8 = lax.broadcasted_iota(jnp.int32, (8, 128), 1)
        bk = bk_ref[...]
        for g in range(kv):
            ds = ds_ref[g]
            rowi = lax.broadcasted_iota(jnp.int32, (N_BUCKETS, r), 0)
            red = jnp.zeros((N_BUCKETS, r), F32)
            for b in range(N_BUCKETS):
                red = jnp.where(rowi == b, jnp.sum(jnp.where(bk == b, ds, 0.0), axis=0, keepdims=True), red)
            out = jnp.zeros((N_BUCKETS, 128), F32)
            so = jnp.zeros((8, 128), F32)
            for h in range(GROUP):
                col = jnp.sum(red[:, h * BLOCK:(h + 1) * BLOCK], axis=1, keepdims=True)
                out = jnp.where(lane == h, col, out)
                sc = jnp.sum(dk_ref[g][:, h * BLOCK:(h + 1) * BLOCK], axis=1, keepdims=True)
                so = jnp.where(lane8 == h, sc, so)
            rel_ref[g] = out
            sink_ref[g] = so

    vm = pl.BlockSpec(memory_space=pltpu.VMEM)
    return pl.pallas_call(
        body,
        in_specs=[vm, vm, vm],
        out_specs=[vm, vm],
        out_shape=[SDS((kv, N_BUCKETS, 128), F32), SDS((kv, 8, 128), F32)],
        name="bias_reduce",
    )(dsum, dsink, bucket_t4)


def _dqkprep(dqa, dka, dva, dqb, dkb, dvb, proj, cos, sin_signed, gq, gk, s_len, ts):
    nd, t, pb = proj.shape
    bl, ha = dqa.shape[0], dqa.shape[1]
    kva, hb, kvb = dka.shape[1], dqb.shape[1], dkb.shape[1]
    hpb = pb // HEAD_DIM
    ns = s_len // ts

    def body(dqa_ref, dka_ref, dva_ref, dqb_ref, dkb_ref, dvb_ref, p_ref, cos_ref, sin_ref, gq_ref, gk_ref,
             dp_ref, dgq_ref, dgk_ref):
        b, i = pl.program_id(0), pl.program_id(1)
        cs, sn = cos_ref[...], sin_ref[...]
        lane = lax.broadcasted_iota(jnp.int32, (ts, HEAD_DIM), 1)
        first = (lane % 32) < 16

        @pl.when((b == 0) & (i == 0))
        def _():
            dgq_ref[...] = jnp.zeros_like(dgq_ref)
            dgk_ref[...] = jnp.zeros_like(dgk_ref)

        def put(hh, val):
            dp_ref[hh // hpb, :, pl.ds((hh % hpb) * HEAD_DIM, HEAD_DIM)] = val.astype(BF16)

        def unrope_norm(d_rot, hh, g, dg_ref):
            dn = _rope(d_rot, cs, sn, first, transpose=True)
            _, n, r = _rms_fwd(_head_slice(p_ref, hh, hpb), g)
            dx, dgt = _rms_bwd(n, r, g, dn)
            dg_ref[0:1, :] += jnp.sum(dgt, axis=0, keepdims=True)
            put(hh, dx)

        for h in range(ha):
            unrope_norm(dqa_ref[0, h] * SCALE, h, gq_ref[...], dgq_ref)
        for h in range(kva):
            unrope_norm(dka_ref[0, h], ha + h, gk_ref[...], dgk_ref)
            put(ha + kva + h, dva_ref[0, h])
        base = ha + 2 * kva
        for h in range(hb):
            put(base + h, dqb_ref[0, h] * SCALE)
        for h in range(kvb):
            put(base + hb + h, dkb_ref[0, h])
            put(base + hb + kvb + h, dvb_ref[0, h])

    def hm(nh):
        return pl.BlockSpec((1, nh, ts, HEAD_DIM), lambda b, i: (b, 0, i, 0))

    vec = pl.BlockSpec((1, HEAD_DIM), lambda b, i: (0, 0))
    tab = pl.BlockSpec((ts, HEAD_DIM), lambda b, i: (i, 0))
    acc = pl.BlockSpec((8, HEAD_DIM), lambda b, i: (0, 0))
    pspec = pl.BlockSpec((nd, ts, pb), lambda b, i: (0, b * ns + i, 0))
    return pl.pallas_call(
        body,
        grid=(bl, ns),
        in_specs=[hm(ha), hm(kva), hm(kva), hm(hb), hm(kvb), hm(kvb), pspec, tab, tab, vec, vec],
        out_specs=[pspec, acc, acc],
        out_shape=[SDS((nd, t, pb), BF16), SDS((8, HEAD_DIM), F32), SDS((8, HEAD_DIM), F32)],
        compiler_params=_cp("arbitrary", "arbitrary"),
        name="dqkprep",
    )(dqa, dka, dva, dqb, dkb, dvb, proj, cos, sin_signed, gq, gk)


def _dx_final(dproj, win_g, x2, dx1, g1, tm, grads):
    t, d = x2.shape
    nd, _, pb = win_g.shape
    ng = len(grads)
    nsteps = t // tm

    def body(dp_ref, w_ref, x_ref, dx1_ref, g_ref, *rest):
        grad_refs, (dx_ref, dg_ref), parts = rest[:ng], rest[ng:ng + 2], rest[ng + 2:2 * ng + 2]
        start, wait = _direct_exchange("scatter", grad_refs, parts, *rest[2 * ng + 2:])

        @pl.when(pl.program_id(0) == 0)
        def _():
            start()
            dg_ref[...] = jnp.zeros_like(dg_ref)

        dh = _dot_nt(dp_ref[0], w_ref[0])
        for j in range(1, nd):
            dh = dh + _dot_nt(dp_ref[j], w_ref[j])
        g = g_ref[...]
        _, n, r = _rms_fwd(x_ref[...], g)
        dx, dgt = _rms_bwd(n, r, g, dh)
        dx_ref[...] = dx1_ref[...] + dx
        dg_ref[0:1, :] += jnp.sum(dgt, axis=0, keepdims=True)
        pl.when(pl.program_id(0) == nsteps - 1)(wait)

    tile = pl.BlockSpec((tm, d), lambda i: (i, 0))
    anyspec = pl.BlockSpec(memory_space=pl.ANY)
    res = pl.pallas_call(
        body,
        grid=(nsteps,),
        in_specs=[pl.BlockSpec((nd, tm, pb), lambda i: (0, i, 0)),
                  pl.BlockSpec((nd, d, pb), lambda i: (0, 0, 0)),
                  tile, tile, pl.BlockSpec((1, d), lambda i: (0, 0))] + [anyspec] * ng,
        out_specs=[tile, pl.BlockSpec((8, d), lambda i: (0, 0))] + [anyspec] * ng,
        out_shape=[SDS((t, d), F32), SDS((8, d), F32)] + [SDS(g.shape, g.dtype) for g in grads],
        scratch_shapes=_exchange_scratch(ng),
        compiler_params=_cp("arbitrary"),
        name="dx_final",
    )(dproj, win_g, x2, dx1, g1, *grads)
    return res[0], res[1], res[2:]


def _adamw_math(w, g, m, v):
    m = ADAM_B1 * m + (1.0 - ADAM_B1) * g
    v = ADAM_B2 * v + (1.0 - ADAM_B2) * (g * g)
    m_hat = m / (1.0 - ADAM_B1 ** ADAM_STEP)
    v_hat = v / (1.0 - ADAM_B2 ** ADAM_STEP)
    delta = -ADAM_LR * (m_hat / (jnp.sqrt(v_hat) + ADAM_EPS) + ADAM_WD * w)
    return delta, m, v


def _adamw_sum(parts, w, m, v, tr, name):
    rows, cols = w.shape

    def body(p_ref, w_ref, m_ref, v_ref, g_ref, d_ref, nm_ref, nv_ref):
        g = p_ref[0]
        for s in range(1, N_DEV):
            g = g + p_ref[s]
        g_ref[...] = g
        d_ref[...], nm_ref[...], nv_ref[...] = _adamw_math(w_ref[...], g, m_ref[...], v_ref[...])

    tr = min(tr, rows)
    tile = pl.BlockSpec((tr, cols), lambda i: (i, 0))
    return pl.pallas_call(
        body,
        grid=(rows // tr,),
        in_specs=[pl.BlockSpec((N_DEV, tr, cols), lambda i: (0, i, 0)), tile, tile, tile],
        out_specs=[tile] * 4,
        out_shape=[SDS((rows, cols), F32)] * 4,
        compiler_params=_cp("parallel"),
        name=name,
    )(parts, w, m, v)


def _adamw_small(g, w, m, v):
    def body(g_ref, w_ref, m_ref, v_ref, d_ref, nm_ref, nv_ref):
        d_ref[...], nm_ref[...], nv_ref[...] = _adamw_math(w_ref[...], g_ref[...], m_ref[...], v_ref[...])

    vm = pl.BlockSpec(memory_space=pltpu.VMEM)
    return pl.pallas_call(
        body,
        in_specs=[vm] * 4,
        out_specs=[vm] * 3,
        out_shape=[SDS(w.shape, F32)] * 3,
        name="adamw_small",
    )(g, w, m, v)


SMALL_ROWS = 8
SMALL_COLS = 1024


def _pack_small(g1, g2, g3, g4, gq, gk, sink, rel):
    row4 = jnp.concatenate([gq.reshape(-1), gk.reshape(-1), sink.reshape(-1)])
    row4 = jnp.pad(row4, (0, SMALL_COLS - row4.shape[0]))
    row5 = jnp.pad(rel.reshape(-1), (0, SMALL_COLS - rel.size))
    zero = jnp.zeros((SMALL_COLS,), F32)
    return jnp.stack([g1.reshape(-1), g2.reshape(-1), g3.reshape(-1), g4.reshape(-1), row4, row5, zero, zero])


def _unpack_small(p, hb):
    hd = HEAD_DIM
    return (p[0:1], p[1:2], p[2:3], p[3:4], p[4:5, 0:hd], p[4:5, hd:2 * hd], p[4:5, 2 * hd:2 * hd + hb],
            p[5, 0:N_BUCKETS * hb].reshape(N_BUCKETS, hb))


def _local_step(x, loss_target, win_g, wo_s, wup_s, wdn_s, g_pre_mix, g_post_mix, q_norm_a, k_norm_a, sink_b,
                rel_bias, g_pre_ffn, g_post_ffn):
    bl, s_len, d = x.shape
    t = bl * s_len
    nh = d // HEAD_DIM
    ha = nh // 2
    kva = ha // GROUP
    hb = nh - ha
    kvb = hb // GROUP
    tm = 512
    tw = min(4096, t)
    ts = min(512, s_len)
    tq, tk = BLOCK, min(512, s_len // 2)

    x2 = x.reshape(t, d)
    tg2 = loss_target.reshape(t, d)
    cos, sin_signed = _rope_tables(s_len)
    a = jnp.arange(BLOCK, dtype=jnp.int32)
    c = jnp.arange(SPAN, dtype=jnp.int32)
    bucket_t = _t5_bucket(c[:, None] - BLOCK - a[None, :])
    bucket_t4 = jnp.tile(bucket_t, (1, GROUP))

    h1, proj = _inproj(x2, g_pre_mix, win_g, tm)
    qa, ka, kat, va, vat, qb, kb, kbt, vb, vbt = _qkprep(
        proj, cos, sin_signed, q_norm_a, k_norm_a, bl, s_len, ha, kva, hb, kvb, ts)
    bias_t = _bias_build(bucket_t, rel_bias, hb)
    oa, lse_a, (wo_g, wup_g, wdn_g) = _attn_a_fwd(qa, ka, vat, tq, tk, [wo_s, wup_s, wdn_s])
    wo = wo_g.reshape(-1, d)
    wdn = wdn_g.reshape(-1, d)
    ob, lse_b = _attn_b_fwd(qb, kb, vbt, bias_t, sink_b, s_len)
    mix, x1, h2 = _mixout(oa, ob, wo, x2, g_post_mix, g_pre_ffn, tm)
    u, df, dy, dg4, loss8 = _ffn_fwd(h2, wup_g, wdn, x1, tg2, g_post_ffn, tm, FFN_BLOCKS_PER_STEP)

    dpre, dx1, dmix, dg3, dg2 = _ffn_bwd(df, u, wdn, wup_g, x1, dy, mix, g_pre_ffn, g_post_mix, tm,
                                         FFN_BLOCKS_PER_STEP)
    gw_dn = _wgrad_rows([u], df, N_DEV, tw, True, "wgrad_down")
    gw_up = _wgrad_cols(h2, dpre, N_DEV, tw, "wgrad_up")
    gw_o = _wgrad_rows([oa, ob], dmix, N_DEV, tw, False, "wgrad_o")
    doa, dob = _attn_out_bwd(dmix, wo, oa.shape[1], tm)
    dqa, dka, dva, (p_o, p_up, p_dn) = _attn_a_bwd(qa, ka, kat, va, doa, oa, lse_a, tq, tk, [gw_o, gw_up, gw_dn])
    dqb, dkb, dvb, dsum, dsink = _attn_b_bwd(qb, kb, kbt, vb, dob, ob, lse_b, bias_t, sink_b, s_len)
    drel_g, dsink_g = _bias_reduce(dsum, dsink, bucket_t4)
    dproj, dgq, dgk = _dqkprep(dqa, dka, dva, dqb, dkb, dvb, proj, cos, sin_signed, q_norm_a, k_norm_a, s_len, ts)
    gw_in = _wgrad_cols_blocked(h1, dproj, tw, "wgrad_in")
    grad_x, dg1, (p_in,) = _dx_final(dproj, win_g, x2, dx1, g_pre_mix, tm, [gw_in])

    drel = jnp.transpose(drel_g[:, :, 0:GROUP], (1, 0, 2)).reshape(N_BUCKETS, hb)
    dsink_v = dsink_g[:, 0, 0:GROUP].reshape(1, hb)
    small = _pack_small(dg1[0], dg2[0], dg3[0], dg4[0], dgq[0], dgk[0], dsink_v, drel)
    small = small.at[6, 0].set(loss8[0, 0])
    return grad_x.reshape(bl, s_len, d), p_in, p_o, p_up, p_dn, small


def kernel(x, w_in, w_o, g_pre_mix, g_post_mix, q_norm_a, k_norm_a, sink_b, rel_bias, g_pre_ffn, w_ffn_up, w_ffn_down, g_post_ffn, loss_target, m_w_in, m_w_o, m_g_pre_mix, m_g_post_mix, m_q_norm_a, m_k_norm_a, m_sink_b, m_rel_bias, m_g_pre_ffn, m_w_ffn_up, m_w_ffn_down, m_g_post_ffn, v_w_in, v_w_o, v_g_pre_mix, v_g_post_mix, v_q_norm_a, v_k_norm_a, v_sink_b, v_rel_bias, v_g_pre_ffn, v_w_ffn_up, v_w_ffn_down, v_g_post_ffn):
    hb = sink_b.shape[1]
    (win_g,) = _weight_gather([w_in[0].astype(BF16)])

    grad_x, p_in, p_o, p_up, p_dn, small = _local_step(
        x, loss_target, win_g, w_o[0].astype(BF16), w_ffn_up[0].astype(BF16), w_ffn_down[0].astype(BF16),
        g_pre_mix, g_post_mix, q_norm_a, k_norm_a, sink_b, rel_bias, g_pre_ffn, g_post_ffn)

    small = _small_allreduce(small)

    g_in, d_in, nm_in, nv_in = _adamw_sum(p_in, w_in[0], m_w_in[0], v_w_in[0], 256, "adamw_in")
    g_o, d_o, nm_o, nv_o = _adamw_sum(p_o, w_o[0], m_w_o[0], v_w_o[0], 128, "adamw_o")
    g_up, d_up, nm_up, nv_up = _adamw_sum(p_up, w_ffn_up[0], m_w_ffn_up[0], v_w_ffn_up[0], 256, "adamw_up")
    g_dn, d_dn, nm_dn, nv_dn = _adamw_sum(p_dn, w_ffn_down[0], m_w_ffn_down[0], v_w_ffn_down[0], 256, "adamw_down")

    pack = lambda *a: _pack_small(*a)
    w_s = pack(g_pre_mix, g_post_mix, g_pre_ffn, g_post_ffn, q_norm_a, k_norm_a, sink_b, rel_bias)
    m_s = pack(m_g_pre_mix, m_g_post_mix, m_g_pre_ffn, m_g_post_ffn, m_q_norm_a, m_k_norm_a, m_sink_b, m_rel_bias)
    v_s = pack(v_g_pre_mix, v_g_post_mix, v_g_pre_ffn, v_g_post_ffn, v_q_norm_a, v_k_norm_a, v_sink_b, v_rel_bias)
    d_s, nm_s, nv_s = _adamw_small(small, w_s, m_s, v_s)

    loss = small[6, 0]

    def outs(big_in, big_o, sm, big_up, big_dn):
        s1, s2, s3, s4, sq, sk, ss, sr = _unpack_small(sm, hb)
        return [big_in[None], big_o[None], s1, s2, sq, sk, ss, sr, s3, big_up[None], big_dn[None], s4]

    return (loss, grad_x,
            *outs(g_in, g_o, small, g_up, g_dn),
            *outs(d_in, d_o, d_s, d_up, d_dn),
            *outs(nm_in, nm_o, nm_s, nm_up, nm_dn),
            *outs(nv_in, nv_o, nv_s, nv_up, nv_dn))
```

```python
import functools

import jax
import jax.numpy as jnp
import numpy as np
from jax import lax
from jax.experimental import pallas as pl
from jax.experimental.pallas import tpu as pltpu

F32 = jnp.float32
BF16 = jnp.bfloat16
SDS = jax.ShapeDtypeStruct

N_DEV = 8
HEAD_DIM = 64
GROUP = 4
BLOCK = 128
SPAN = 3 * BLOCK
GRID_W = 64
N_BUCKETS = 32
MAX_DISTANCE = 128
ROPE_THETA = 10000.0
EPS = 1e-6
NEG_INF = -1e30
SCALE = HEAD_DIM ** -0.5
VT_PAD = 16

ADAM_LR = 0.001
ADAM_B1 = 0.9
ADAM_B2 = 0.999
ADAM_EPS = 1e-08
ADAM_WD = 0.01
ADAM_STEP = 10

VMEM_LIMIT = 56 * 1024 * 1024
MESH = pl.DeviceIdType.MESH


def _cp(*sem):
    return pltpu.CompilerParams(dimension_semantics=sem, vmem_limit_bytes=VMEM_LIMIT)


def _dot(a, b):
    return jnp.dot(a, b, preferred_element_type=F32)


def _dot_nt(a, b):
    return lax.dot_general(a, b, (((1,), (1,)), ((), ())), preferred_element_type=F32)


def _dot_tn(a, b):
    return lax.dot_general(a, b, (((0,), (0,)), ((), ())), preferred_element_type=F32)


def _rms_fwd(x, g):
    r = lax.rsqrt(jnp.mean(x * x, axis=-1, keepdims=True) + EPS)
    n = x * r
    return n * g, n, r


def _rms_bwd(n, r, g, dy):
    gd = g * dy
    dx = r * (gd - n * jnp.mean(n * gd, axis=-1, keepdims=True))
    return dx, dy * n


def _rope_tables(s_len):
    rows = s_len // GRID_W
    row = np.repeat(np.arange(rows, dtype=np.int32), GRID_W)
    col = np.tile(np.arange(GRID_W, dtype=np.int32), rows)
    nf = HEAD_DIM // 4
    freqs = np.float32(ROPE_THETA) ** (-np.arange(nf, dtype=np.float32) / np.float32(nf))
    ang_r = row.astype(np.float32)[:, None] * freqs[None, :]
    ang_c = col.astype(np.float32)[:, None] * freqs[None, :]
    cr, sr, cc, sc = np.cos(ang_r), np.sin(ang_r), np.cos(ang_c), np.sin(ang_c)
    cos = np.concatenate([cr, cr, cc, cc], axis=-1).astype(np.float32)
    sin_signed = np.concatenate([-sr, sr, -sc, sc], axis=-1).astype(np.float32)
    return jnp.asarray(cos), jnp.asarray(sin_signed)


def _t5_bucket(rel):
    nb = N_BUCKETS // 2
    ret = (rel > 0).astype(jnp.int32) * nb
    n = jnp.abs(rel)
    max_exact = nb // 2
    nf = jnp.maximum(n, 1).astype(F32)
    large = max_exact + (jnp.log(nf / max_exact) / np.float32(np.log(MAX_DISTANCE / max_exact))
                         * (nb - max_exact)).astype(jnp.int32)
    large = jnp.minimum(large, nb - 1)
    return ret + jnp.where(n < max_exact, n, large)


def _mesh_pos():
    return lax.axis_index("x"), lax.axis_index("y"), lax.axis_index("c")


def _lin(p):
    return 4 * p[0] + 2 * p[1] + p[2]


def _weight_gather(shards):
    n = len(shards)

    def body(*refs):
        xs, outs = refs[:n], refs[n:2 * n]
        send_sems, recv_sems, local_sems = refs[2 * n:]
        x, y, c = _mesh_pos()
        me, sibling = (x, y, c), (x, y, 1 - c)
        chips = [(1 - x, y), (x, 1 - y), (1 - x, 1 - y)]

        def copy(a, k, block, to, src=None):
            slot = outs[a].at[_lin(block)]
            return pltpu.make_async_remote_copy(
                src_ref=slot if src is None else src, dst_ref=slot,
                send_sem=send_sems.at[a, k], recv_sem=recv_sems.at[a, k],
                device_id=to, device_id_type=MESH)

        started = []
        for a in range(n):
            mine = pltpu.make_async_copy(xs[a], outs[a].at[_lin(me)], local_sems.at[a])
            mine.start()
            started.append(mine)
        sends = []
        for a in range(n):
            first = [copy(a, 0, me, sibling, src=xs[a])]
            first += [copy(a, 1 + j, me, (*chip, c), src=xs[a]) for j, chip in enumerate(chips)]
            for cp in first:
                cp.start()
            sends += first
        for a in range(n):
            for j, chip in enumerate(chips):
                copy(a, 1 + j, (*chip, c), me).wait_recv()
                fwd = copy(a, 4 + j, (*chip, c), sibling)
                fwd.start()
                sends.append(fwd)
        for a in range(n):
            copy(a, 0, sibling, me).wait_recv()
            for j, chip in enumerate(chips):
                copy(a, 4 + j, (*chip, 1 - c), me).wait_recv()
        for cp in sends:
            cp.wait_send()
        for mine in started:
            mine.wait()

    anyspec = pl.BlockSpec(memory_space=pl.ANY)
    return pl.pallas_call(
        body,
        out_shape=[SDS((N_DEV,) + s.shape, s.dtype) for s in shards],
        in_specs=[anyspec] * n,
        out_specs=[anyspec] * n,
        scratch_shapes=[pltpu.SemaphoreType.DMA((n, 7)), pltpu.SemaphoreType.DMA((n, 7)),
                        pltpu.SemaphoreType.DMA((n,))],
        name="weight_gather",
    )(*shards)


def _direct_exchange(kind, ins, outs, send_sems, recv_sems, local_sems):
    x, y, c = _mesh_pos()
    me = (x, y, c)
    peers = [(x, y, 1 - c), (1 - x, y, c), (x, 1 - y, c), (1 - x, 1 - y, c),
             (1 - x, y, 1 - c), (x, 1 - y, 1 - c), (1 - x, 1 - y, 1 - c)]

    def src(a, to):
        return ins[a] if kind == "gather" else ins[a].at[_lin(to)]

    def remote(a, k, to, frm):
        return pltpu.make_async_remote_copy(
            src_ref=src(a, to), dst_ref=outs[a].at[_lin(frm)],
            send_sem=send_sems.at[a, k], recv_sem=recv_sems.at[a, k],
            device_id=to, device_id_type=MESH)

    n = len(ins)
    sends = [remote(a, k, p, me) for a in range(n) for k, p in enumerate(peers)]
    arrivals = [remote(a, k, p, p) for a in range(n) for k, p in enumerate(peers)]
    local = [pltpu.make_async_copy(src(a, me), outs[a].at[_lin(me)], local_sems.at[a]) for a in range(n)]

    def start():
        for cp in local + sends:
            cp.start()

    def wait():
        for cp in arrivals:
            cp.wait_recv()
        for cp in sends:
            cp.wait_send()
        for cp in local:
            cp.wait()

    return start, wait


def _exchange_scratch(n):
    return [pltpu.SemaphoreType.DMA((n, 7)), pltpu.SemaphoreType.DMA((n, 7)), pltpu.SemaphoreType.DMA((n,))]


def _small_allreduce(v):
    rows, cols = v.shape

    def body(v_ref, out_ref, land_ref, send_sems, recv_sems):
        x, y, c = _mesh_pos()
        me = (x, y, c)
        peers = [(x, y, 1 - c), (1 - x, y, c), (x, 1 - y, c), (1 - x, 1 - y, c),
                 (1 - x, y, 1 - c), (x, 1 - y, 1 - c), (1 - x, 1 - y, 1 - c)]

        def copy(k, to, frm):
            return pltpu.make_async_remote_copy(
                src_ref=v_ref, dst_ref=land_ref.at[_lin(frm)],
                send_sem=send_sems.at[k], recv_sem=recv_sems.at[k],
                device_id=to, device_id_type=MESH)

        sends = [copy(k, p, me) for k, p in enumerate(peers)]
        for cp in sends:
            cp.start()
        land_ref[_lin(me)] = v_ref[...]
        for k, p in enumerate(peers):
            copy(k, p, p).wait_recv()
        for cp in sends:
            cp.wait_send()
        acc = land_ref[0]
        for s in range(1, N_DEV):
            acc = acc + land_ref[s]
        out_ref[...] = acc

    vm = pl.BlockSpec(memory_space=pltpu.VMEM)
    return pl.pallas_call(
        body,
        out_shape=SDS((rows, cols), F32),
        in_specs=[vm],
        out_specs=vm,
        scratch_shapes=[pltpu.VMEM((N_DEV, rows, cols), F32),
                        pltpu.SemaphoreType.DMA((7,)), pltpu.SemaphoreType.DMA((7,))],
        name="small_allreduce",
    )(v)


def _inproj(x2, g1, win_g, tm):
    t, d = x2.shape
    nd, _, pb = win_g.shape

    def body(x_ref, g_ref, w_ref, h_ref, p_ref):
        y, _, _ = _rms_fwd(x_ref[...], g_ref[...])
        h = y.astype(BF16)
        h_ref[...] = h
        for j in range(nd):
            p_ref[j] = _dot(h, w_ref[j])

    return pl.pallas_call(
        body,
        grid=(t // tm,),
        in_specs=[pl.BlockSpec((tm, d), lambda i: (i, 0)),
                  pl.BlockSpec((1, d), lambda i: (0, 0)),
                  pl.BlockSpec((nd, d, pb), lambda i: (0, 0, 0))],
        out_specs=[pl.BlockSpec((tm, d), lambda i: (i, 0)),
                   pl.BlockSpec((nd, tm, pb), lambda i: (0, i, 0))],
        out_shape=[SDS((t, d), BF16), SDS((nd, t, pb), F32)],
        compiler_params=_cp("parallel"),
        name="inproj",
    )(x2, g1, win_g)


def _head_slice(p_ref, hh, hpb):
    return p_ref[hh // hpb, :, pl.ds((hh % hpb) * HEAD_DIM, HEAD_DIM)]


def _rope(x, cos, sin_signed, first, transpose=False):
    def partner(v):
        return jnp.where(first, jnp.roll(v, -16, axis=1), jnp.roll(v, 16, axis=1))

    if transpose:
        return x * cos + partner(x * sin_signed)
    return x * cos + partner(x) * sin_signed


def _qkprep(proj, cos, sin_signed, gq, gk, bl, s_len, ha, kva, hb, kvb, ts):
    nd, t, pb = proj.shape
    hpb = pb // HEAD_DIM
    ns = s_len // ts
    sp = s_len + 2 * BLOCK

    def body(p_ref, cos_ref, sin_ref, gq_ref, gk_ref, qa_ref, ka_ref, kat_ref, va_ref, vat_ref, qb_ref, kb_ref,
             kbt_ref, vb_ref, vbt_ref):
        i = pl.program_id(1)
        cs, sn = cos_ref[...], sin_ref[...]
        lane = lax.broadcasted_iota(jnp.int32, (ts, HEAD_DIM), 1)
        first = (lane % 32) < 16
        ones_row = (lax.broadcasted_iota(jnp.int32, (VT_PAD, ts), 0) == 0).astype(BF16)

        def normrope(xh, g):
            y, _, _ = _rms_fwd(xh, g)
            return _rope(y, cs, sn, first)

        def transposed(xb):
            return xb.astype(F32).T.astype(BF16)

        for h in range(ha):
            qa_ref[0, h] = (normrope(_head_slice(p_ref, h, hpb), gq_ref[...]) * SCALE).astype(BF16)
        for h in range(kva):
            kh = normrope(_head_slice(p_ref, ha + h, hpb), gk_ref[...]).astype(BF16)
            ka_ref[0, h] = kh
            kat_ref[0, h] = transposed(kh)
            vh = _head_slice(p_ref, ha + kva + h, hpb).astype(BF16)
            va_ref[0, h] = vh
            vat_ref[0, h, 0:HEAD_DIM, :] = transposed(vh)
            vat_ref[0, h, HEAD_DIM:HEAD_DIM + VT_PAD, :] = ones_row
        base = ha + 2 * kva
        for h in range(hb):
            qb_ref[0, h] = (_head_slice(p_ref, base + h, hpb) * SCALE).astype(BF16)

        @pl.when(i == 0)
        def _():
            zeros = jnp.zeros((kvb, BLOCK, HEAD_DIM), BF16)
            zeros_t = jnp.zeros((kvb, HEAD_DIM + VT_PAD, BLOCK), BF16)
            for ref in (kb_ref, vb_ref):
                ref[0, :, 0:BLOCK, :] = zeros
                ref[0, :, sp - BLOCK:sp, :] = zeros
            kbt_ref[0, :, :, 0:BLOCK] = zeros_t[:, 0:HEAD_DIM]
            kbt_ref[0, :, :, sp - BLOCK:sp] = zeros_t[:, 0:HEAD_DIM]
            vbt_ref[0, :, :, 0:BLOCK] = zeros_t
            vbt_ref[0, :, :, sp - BLOCK:sp] = zeros_t

        row0 = pl.multiple_of(BLOCK + i * ts, BLOCK)
        for h in range(kvb):
            kh = _head_slice(p_ref, base + hb + h, hpb).astype(BF16)
            vh = _head_slice(p_ref, base + hb + kvb + h, hpb).astype(BF16)
            kb_ref[0, h, pl.ds(row0, ts), :] = kh
            vb_ref[0, h, pl.ds(row0, ts), :] = vh
            kbt_ref[0, h, :, pl.ds(row0, ts)] = transposed(kh)
            vbt_ref[0, h, 0:HEAD_DIM, pl.ds(row0, ts)] = transposed(vh)
            vbt_ref[0, h, HEAD_DIM:HEAD_DIM + VT_PAD, pl.ds(row0, ts)] = ones_row

    def hm(nh):
        return pl.BlockSpec((1, nh, ts, HEAD_DIM), lambda b, i: (b, 0, i, 0))

    def padded(nh):
        return pl.BlockSpec((1, nh, sp, HEAD_DIM), lambda b, i: (b, 0, 0, 0))

    def padded_t(nh, rows):
        return pl.BlockSpec((1, nh, rows, sp), lambda b, i: (b, 0, 0, 0))

    return pl.pallas_call(
        body,
        grid=(bl, ns),
        in_specs=[pl.BlockSpec((nd, ts, pb), lambda b, i: (0, b * ns + i, 0)),
                  pl.BlockSpec((ts, HEAD_DIM), lambda b, i: (i, 0)),
                  pl.BlockSpec((ts, HEAD_DIM), lambda b, i: (i, 0)),
                  pl.BlockSpec((1, HEAD_DIM), lambda b, i: (0, 0)),
                  pl.BlockSpec((1, HEAD_DIM), lambda b, i: (0, 0))],
        out_specs=[hm(ha), hm(kva), pl.BlockSpec((1, kva, HEAD_DIM, ts), lambda b, i: (b, 0, 0, i)), hm(kva),
                   pl.BlockSpec((1, kva, HEAD_DIM + VT_PAD, ts), lambda b, i: (b, 0, 0, i)),
                   hm(hb), padded(kvb), padded_t(kvb, HEAD_DIM), padded(kvb), padded_t(kvb, HEAD_DIM + VT_PAD)],
        out_shape=[SDS((bl, ha, s_len, HEAD_DIM), BF16), SDS((bl, kva, s_len, HEAD_DIM), BF16),
                   SDS((bl, kva, HEAD_DIM, s_len), BF16),
                   SDS((bl, kva, s_len, HEAD_DIM), BF16), SDS((bl, kva, HEAD_DIM + VT_PAD, s_len), BF16),
                   SDS((bl, hb, s_len, HEAD_DIM), BF16),
                   SDS((bl, kvb, sp, HEAD_DIM), BF16), SDS((bl, kvb, HEAD_DIM, sp), BF16),
                   SDS((bl, kvb, sp, HEAD_DIM), BF16), SDS((bl, kvb, HEAD_DIM + VT_PAD, sp), BF16)],
        compiler_params=_cp("parallel", "arbitrary"),
        name="qkprep",
    )(proj, cos, sin_signed, gq, gk)


def _bias_build(bucket_t, rel_bias, hb):
    kvb = hb // GROUP

    def body(bkt_ref, tbl_ref, out_ref):
        bkt = bkt_ref[...]
        ci = lax.broadcasted_iota(jnp.int32, (SPAN, BLOCK), 0)
        qi = lax.broadcasted_iota(jnp.int32, (SPAN, BLOCK), 1)
        band = jnp.abs(ci - BLOCK - qi) <= BLOCK
        masks = (band, band & (ci >= BLOCK), band & (ci < 2 * BLOCK))
        for h in range(hb):
            acct = jnp.zeros((SPAN, BLOCK), F32)
            for b in range(N_BUCKETS):
                acct = jnp.where(bkt == b, tbl_ref[b, h], acct)
            lanes = slice((h % GROUP) * BLOCK, (h % GROUP + 1) * BLOCK)
            for var, mask in enumerate(masks):
                out_ref[var, h // GROUP, :, lanes] = jnp.where(mask, acct, NEG_INF)

    vm = pl.BlockSpec(memory_space=pltpu.VMEM)
    return pl.pallas_call(
        body,
        in_specs=[vm, pl.BlockSpec(memory_space=pltpu.SMEM)],
        out_specs=vm,
        out_shape=SDS((3, kvb, SPAN, GROUP * BLOCK), F32),
        name="bias_build",
    )(bucket_t, rel_bias)


def _attn_a_fwd(qa, ka, vat, tq, tk, shards):
    bl, ha, s_len, _ = qa.shape
    kv = ka.shape[1]
    va_rows = vat.shape[2]
    nq, nk = s_len // tq, s_len // tk
    assert nk % 2 == 0
    r = GROUP * tq
    ns = len(shards)

    def body(q_ref, k_ref, v_ref, *rest):
        shard_refs, (o_ref, l_ref), gathered = rest[:ns], rest[ns:ns + 2], rest[ns + 2:2 * ns + 2]
        st_sc, send_sems, recv_sems, local_sems = rest[2 * ns + 2:]
        step_id = (pl.program_id(0) * kv + pl.program_id(1)) * nq + pl.program_id(2)
        start, wait = _direct_exchange("gather", shard_refs, gathered, send_sems, recv_sems, local_sems)
        pl.when(step_id == 0)(start)

        q = q_ref[0].reshape(r, HEAD_DIM)

        def scores(c):
            return _dot_nt(k_ref[0, 0, pl.ds(pl.multiple_of(c * tk, tk), tk), :], q)

        def fold(st, c, carry):
            m_old, acc = carry
            m_new = jnp.maximum(m_old, jnp.max(st, axis=0, keepdims=True))
            pt = jnp.exp(st - m_new).astype(BF16)
            vt = v_ref[0, 0, :, pl.ds(pl.multiple_of(c * tk, tk), tk)]
            return m_new, jnp.exp(m_old - m_new) * acc + _dot(vt, pt)

        st_sc[0] = scores(0)

        def step(c2, carry):
            c = 2 * c2
            st_sc[1] = scores(c + 1)
            carry = fold(st_sc[0], c, carry)
            st_sc[0] = scores(jnp.minimum(c + 2, nk - 1))
            return fold(st_sc[1], c + 1, carry)

        m, acc = lax.fori_loop(0, nk // 2, step,
                               (jnp.full((1, r), -jnp.inf, F32), jnp.zeros((va_rows, r), F32)))
        l = acc[HEAD_DIM:HEAD_DIM + 1, :]
        o = (acc[0:HEAD_DIM, :] / l).T
        for h in range(GROUP):
            o_ref[:, h * HEAD_DIM:(h + 1) * HEAD_DIM] = o[h * tq:(h + 1) * tq].astype(BF16)
        l_ref[0, 0, 0] = jnp.broadcast_to(m + jnp.log(l), (8, r))
        pl.when(step_id == bl * kv * nq - 1)(wait)

    anyspec = pl.BlockSpec(memory_space=pl.ANY)
    res = pl.pallas_call(
        body,
        grid=(bl, kv, nq),
        in_specs=[pl.BlockSpec((1, GROUP, tq, HEAD_DIM), lambda b, g, i: (b, g, i, 0)),
                  pl.BlockSpec((1, 1, s_len, HEAD_DIM), lambda b, g, i: (b, g, 0, 0)),
                  pl.BlockSpec((1, 1, va_rows, s_len), lambda b, g, i: (b, g, 0, 0))] + [anyspec] * ns,
        out_specs=[pl.BlockSpec((tq, GROUP * HEAD_DIM), lambda b, g, i: (b * nq + i, g)),
                   pl.BlockSpec((1, 1, 1, 8, r), lambda b, g, i: (b, g, i, 0, 0))] + [anyspec] * ns,
        out_shape=[SDS((bl * s_len, ha * HEAD_DIM), BF16), SDS((bl, kv, nq, 8, r), F32)]
        + [SDS((N_DEV,) + s.shape, s.dtype) for s in shards],
        scratch_shapes=[pltpu.VMEM((2, tk, r), F32)] + _exchange_scratch(ns),
        compiler_params=_cp("arbitrary", "arbitrary", "arbitrary"),
        name="attn_a_fwd",
    )(qa, ka, vat, *shards)
    return res[0], res[1], res[2:]


FFN_BLOCKS_PER_STEP = 4
QB_PER_STEP = 4


def _bias_variant(n, nb):
    return jnp.where(n == 0, 1, jnp.where(n == nb - 1, 2, 0))


def _sink_row(sink_ref, g):
    return jnp.concatenate([jnp.full((1, BLOCK), sink_ref[0, g * GROUP + h], F32) for h in range(GROUP)], axis=1)


def _attn_b_fwd(qb, kb, vbt, bias_t, sink, s_len):
    bl, hb, _, _ = qb.shape
    kv = kb.shape[1]
    sp = kb.shape[2]
    vt_rows = vbt.shape[2]
    nb = s_len // BLOCK
    nbs = min(QB_PER_STEP, nb)
    r = GROUP * BLOCK

    def body(q_ref, k_ref, vt_ref, bt_ref, sink_ref, o_ref, l_ref):
        g, n0 = pl.program_id(1), pl.program_id(2) * nbs
        sink_row = _sink_row(sink_ref, g)
        for j in range(nbs):
            n = n0 + j
            span = pl.ds(pl.multiple_of(n * BLOCK, BLOCK), SPAN)
            q = q_ref[0, :, j * BLOCK:(j + 1) * BLOCK, :].reshape(r, HEAD_DIM)
            st = _dot_nt(k_ref[0, 0, span, :], q) + bt_ref[_bias_variant(n, nb), 0]
            m = jnp.maximum(jnp.max(st, axis=0, keepdims=True), sink_row)
            acc = _dot(vt_ref[0, 0, :, span], jnp.exp(st - m).astype(BF16))
            l = acc[HEAD_DIM:HEAD_DIM + 1, :] + jnp.exp(sink_row - m)
            o = (acc[0:HEAD_DIM, :] / l).T
            for h in range(GROUP):
                o_ref[j * BLOCK:(j + 1) * BLOCK, h * HEAD_DIM:(h + 1) * HEAD_DIM] = (
                    o[h * BLOCK:(h + 1) * BLOCK].astype(BF16))
            l_ref[0, 0, j] = jnp.broadcast_to(m + jnp.log(l), (8, r))

    return pl.pallas_call(
        body,
        grid=(bl, kv, nb // nbs),
        in_specs=[pl.BlockSpec((1, GROUP, nbs * BLOCK, HEAD_DIM), lambda b, g, n: (b, g, n, 0)),
                  pl.BlockSpec((1, 1, sp, HEAD_DIM), lambda b, g, n: (b, g, 0, 0)),
                  pl.BlockSpec((1, 1, vt_rows, sp), lambda b, g, n: (b, g, 0, 0)),
                  pl.BlockSpec((3, 1, SPAN, r), lambda b, g, n: (0, g, 0, 0)),
                  pl.BlockSpec(memory_space=pltpu.SMEM)],
        out_specs=[pl.BlockSpec((nbs * BLOCK, GROUP * HEAD_DIM), lambda b, g, n: (b * (nb // nbs) + n, g)),
                   pl.BlockSpec((1, 1, nbs, 8, r), lambda b, g, n: (b, g, n, 0, 0))],
        out_shape=[SDS((bl * s_len, hb * HEAD_DIM), BF16), SDS((bl, kv, nb, 8, r), F32)],
        compiler_params=_cp("parallel", "parallel", "arbitrary"),
        name="attn_b_fwd",
    )(qb, kb, vbt, bias_t, sink)


def _mixout(oa, ob, wo, x2, g2, g3, tm):
    t, d = x2.shape
    ca = oa.shape[1]

    def body(oa_ref, ob_ref, w_ref, x_ref, g2_ref, g3_ref, mix_ref, x1_ref, h2_ref):
        mix = _dot(oa_ref[...], w_ref[0:ca, :]) + _dot(ob_ref[...], w_ref[ca:, :])
        mix_ref[...] = mix
        y2, _, _ = _rms_fwd(mix, g2_ref[...])
        x1 = x_ref[...] + y2
        x1_ref[...] = x1
        y3, _, _ = _rms_fwd(x1, g3_ref[...])
        h2_ref[...] = y3.astype(BF16)

    tile = lambda w: pl.BlockSpec((tm, w), lambda i: (i, 0))
    vec = pl.BlockSpec((1, d), lambda i: (0, 0))
    return pl.pallas_call(
        body,
        grid=(t // tm,),
        in_specs=[tile(ca), tile(ob.shape[1]), pl.BlockSpec(wo.shape, lambda i: (0, 0)), tile(d), vec, vec],
        out_specs=[tile(d), tile(d), tile(d)],
        out_shape=[SDS((t, d), F32), SDS((t, d), F32), SDS((t, d), BF16)],
        compiler_params=_cp("parallel"),
        name="mixout",
    )(oa, ob, wo, x2, g2, g3)


def _ffn_fwd(h2, wup_g, wdn, x1, target, g4, tm, jb):
    t, d = x1.shape
    nblk, _, tf = wup_g.shape
    ff = nblk * tf
    nt = t // tm
    nj = nblk // jb

    def body(h_ref, wu_ref, wd_ref, x1_ref, tg_ref, g_ref, u_ref, df_ref, dy_ref, dg_ref, loss_ref, acc_sc):
        i, j = pl.program_id(0), pl.program_id(1)

        @pl.when(j == 0)
        def _():
            acc_sc[...] = jnp.zeros_like(acc_sc)

        @pl.when((i == 0) & (j == 0))
        def _():
            dg_ref[...] = jnp.zeros_like(dg_ref)
            loss_ref[...] = jnp.zeros_like(loss_ref)

        h = h_ref[...]
        squares = []
        for s in range(jb):
            u = jnp.maximum(_dot(h, wu_ref[s]), 0.0)
            u_ref[:, s * tf:(s + 1) * tf] = u.astype(BF16)
            squares.append((u * u).astype(BF16))
        acc_sc[...] += _dot(jnp.concatenate(squares, axis=1), wd_ref[...])

        @pl.when(j == nj - 1)
        def _():
            g = g_ref[...]
            y4, n, r = _rms_fwd(acc_sc[...], g)
            e = (x1_ref[...] + y4) - tg_ref[...]
            loss_ref[...] += jnp.sum(e * e) * (0.5 / d)
            dy = e * (1.0 / d)
            dy_ref[...] = dy
            df, dgt = _rms_bwd(n, r, g, dy)
            df_ref[...] = df.astype(BF16)
            dg_ref[0:1, :] += jnp.sum(dgt, axis=0, keepdims=True)

    tile = pl.BlockSpec((tm, d), lambda i, j: (i, 0))
    return pl.pallas_call(
        body,
        grid=(nt, nj),
        in_specs=[tile,
                  pl.BlockSpec((jb, d, tf), lambda i, j: (j, 0, 0)),
                  pl.BlockSpec((jb * tf, d), lambda i, j: (j, 0)),
                  tile, tile,
                  pl.BlockSpec((1, d), lambda i, j: (0, 0))],
        out_specs=[pl.BlockSpec((tm, jb * tf), lambda i, j: (i, j)), tile, tile,
                   pl.BlockSpec((8, d), lambda i, j: (0, 0)),
                   pl.BlockSpec((8, 128), lambda i, j: (0, 0))],
        out_shape=[SDS((t, ff), BF16), SDS((t, d), BF16), SDS((t, d), F32), SDS((8, d), F32), SDS((8, 128), F32)],
        scratch_shapes=[pltpu.VMEM((tm, d), F32)],
        compiler_params=_cp("arbitrary", "arbitrary"),
        name="ffn_fwd",
    )(h2, wup_g, wdn, x1, target, g4)


def _ffn_bwd(df, u, wdn, wup_g, x1, dy, mix, g3, g2, tm, jb):
    t, d = x1.shape
    nblk, _, tf = wup_g.shape
    nt = t // tm
    nj = nblk // jb

    def body(df_ref, u_ref, wd_ref, wu_ref, x1_ref, dy_ref, mix_ref, g3_ref, g2_ref,
             dpre_ref, dx1_ref, dmix_ref, dg3_ref, dg2_ref, acc_sc):
        i, j = pl.program_id(0), pl.program_id(1)

        @pl.when(j == 0)
        def _():
            acc_sc[...] = jnp.zeros_like(acc_sc)

        @pl.when((i == 0) & (j == 0))
        def _():
            dg3_ref[...] = jnp.zeros_like(dg3_ref)
            dg2_ref[...] = jnp.zeros_like(dg2_ref)

        du2 = _dot_nt(df_ref[...], wd_ref[...])
        dpre = (2.0 * u_ref[...].astype(F32) * du2).astype(BF16)
        dpre_ref[...] = dpre
        dh = _dot_nt(dpre[:, 0:tf], wu_ref[0])
        for s in range(1, jb):
            dh = dh + _dot_nt(dpre[:, s * tf:(s + 1) * tf], wu_ref[s])
        acc_sc[...] += dh

        @pl.when(j == nj - 1)
        def _():
            g3, g2 = g3_ref[...], g2_ref[...]
            _, n3, r3 = _rms_fwd(x1_ref[...], g3)
            dx, dgt3 = _rms_bwd(n3, r3, g3, acc_sc[...])
            dx1 = dy_ref[...] + dx
            dx1_ref[...] = dx1
            dg3_ref[0:1, :] += jnp.sum(dgt3, axis=0, keepdims=True)
            _, n2, r2 = _rms_fwd(mix_ref[...], g2)
            dmix, dgt2 = _rms_bwd(n2, r2, g2, dx1)
            dmix_ref[...] = dmix.astype(BF16)
            dg2_ref[0:1, :] += jnp.sum(dgt2, axis=0, keepdims=True)

    tile = pl.BlockSpec((tm, d), lambda i, j: (i, 0))
    vec = pl.BlockSpec((1, d), lambda i, j: (0, 0))
    acc8 = pl.BlockSpec((8, d), lambda i, j: (0, 0))
    return pl.pallas_call(
        body,
        grid=(nt, nj),
        in_specs=[tile,
                  pl.BlockSpec((tm, jb * tf), lambda i, j: (i, j)),
                  pl.BlockSpec((jb * tf, d), lambda i, j: (j, 0)),
                  pl.BlockSpec((jb, d, tf), lambda i, j: (j, 0, 0)),
                  tile, tile, tile, vec, vec],
        out_specs=[pl.BlockSpec((tm, jb * tf), lambda i, j: (i, j)), tile, tile, acc8, acc8],
        out_shape=[SDS(u.shape, BF16), SDS((t, d), F32), SDS((t, d), BF16), SDS((8, d), F32), SDS((8, d), F32)],
        scratch_shapes=[pltpu.VMEM((tm, d), F32)],
        compiler_params=_cp("arbitrary", "arbitrary"),
        name="ffn_bwd",
    )(df, u, wdn, wup_g, x1, dy, mix, g3, g2)


def _wgrad_cols(a, b, nj, tt, name):
    t, m = a.shape
    n = b.shape[1]
    bn = n // nj

    def body(a_ref, b_ref, o_ref):
        @pl.when(pl.program_id(1) == 0)
        def _():
            o_ref[...] = jnp.zeros_like(o_ref)

        o_ref[0] += _dot_tn(a_ref[...], b_ref[...])

    return pl.pallas_call(
        body,
        grid=(nj, t // tt),
        in_specs=[pl.BlockSpec((tt, m), lambda j, k: (k, 0)),
                  pl.BlockSpec((tt, bn), lambda j, k: (k, j))],
        out_specs=pl.BlockSpec((1, m, bn), lambda j, k: (j, 0, 0)),
        out_shape=SDS((nj, m, bn), F32),
        compiler_params=_cp("parallel", "arbitrary"),
        name=name,
    )(a, b)


def _wgrad_cols_blocked(a, b3, tt, name):
    t, m = a.shape
    nj, _, bn = b3.shape

    def body(a_ref, b_ref, o_ref):
        @pl.when(pl.program_id(1) == 0)
        def _():
            o_ref[...] = jnp.zeros_like(o_ref)

        o_ref[0] += _dot_tn(a_ref[...], b_ref[0])

    return pl.pallas_call(
        body,
        grid=(nj, t // tt),
        in_specs=[pl.BlockSpec((tt, m), lambda j, k: (k, 0)),
                  pl.BlockSpec((1, tt, bn), lambda j, k: (j, k, 0))],
        out_specs=pl.BlockSpec((1, m, bn), lambda j, k: (j, 0, 0)),
        out_shape=SDS((nj, m, bn), F32),
        compiler_params=_cp("parallel", "arbitrary"),
        name=name,
    )(a, b3)


def _wgrad_rows(a_parts, b, nj, tt, square, name):
    t, n = b.shape
    widths = [p.shape[1] for p in a_parts]
    m = sum(widths)
    bm = m // nj
    per = [w // bm for w in widths]
    starts = [sum(per[:q]) for q in range(len(per))]
    np_ = len(a_parts)

    def body(*refs):
        a_refs, b_ref, o_ref = refs[:np_], refs[np_], refs[np_ + 1]
        j = pl.program_id(0)

        @pl.when(pl.program_id(1) == 0)
        def _():
            o_ref[...] = jnp.zeros_like(o_ref)

        for q in range(np_):
            @pl.when((j >= starts[q]) & (j < starts[q] + per[q]))
            def _(q=q):
                a = a_refs[q][...]
                if square:
                    af = a.astype(F32)
                    a = (af * af).astype(BF16)
                o_ref[0] += _dot_tn(a, b_ref[...])

    def a_spec(q):
        return pl.BlockSpec((tt, bm), lambda j, k: (k, jnp.clip(j - starts[q], 0, per[q] - 1)))

    return pl.pallas_call(
        body,
        grid=(nj, t // tt),
        in_specs=[a_spec(q) for q in range(np_)] + [pl.BlockSpec((tt, n), lambda j, k: (k, 0))],
        out_specs=pl.BlockSpec((1, bm, n), lambda j, k: (j, 0, 0)),
        out_shape=SDS((nj, bm, n), F32),
        compiler_params=_cp("parallel", "arbitrary"),
        name=name,
    )(*a_parts, b)


def _attn_out_bwd(dmix, wo, ca, tm):
    t, d = dmix.shape
    cb = wo.shape[0] - ca

    def body(dm_ref, w_ref, da_ref, db_ref):
        dm = dm_ref[...]
        da_ref[...] = _dot_nt(dm, w_ref[0:ca, :]).astype(BF16)
        db_ref[...] = _dot_nt(dm, w_ref[ca:, :]).astype(BF16)

    return pl.pallas_call(
        body,
        grid=(t // tm,),
        in_specs=[pl.BlockSpec((tm, d), lambda i: (i, 0)), pl.BlockSpec(wo.shape, lambda i: (0, 0))],
        out_specs=[pl.BlockSpec((tm, ca), lambda i: (i, 0)), pl.BlockSpec((tm, cb), lambda i: (i, 0))],
        out_shape=[SDS((t, ca), BF16), SDS((t, cb), BF16)],
        compiler_params=_cp("parallel"),
        name="attn_out_bwd",
    )(dmix, wo)


def _stack_heads(ref, rows):
    return jnp.concatenate([ref[:, h * HEAD_DIM:(h + 1) * HEAD_DIM] for h in range(GROUP)], axis=0)


def _attn_a_bwd(qa, ka, kat, va, do, o, lse, tq, tk, grads):
    bl, ha, s_len, _ = qa.shape
    kv = ka.shape[1]
    nq, nk = s_len // tq, s_len // tk
    assert nk % 2 == 0
    r = GROUP * tq
    ng = len(grads)

    def body(q_ref, k_ref, kt_ref, v_ref, do_ref, o_ref, l_ref, *rest):
        grad_refs, (dq_ref, dk_ref, dv_ref), parts = rest[:ng], rest[ng:ng + 3], rest[ng + 3:2 * ng + 3]
        st_sc, dp_sc, dkt_sc, dvt_sc, send_sems, recv_sems, local_sems = rest[2 * ng + 3:]
        i = pl.program_id(2)
        step_id = (pl.program_id(0) * kv + pl.program_id(1)) * nq + i
        start, wait = _direct_exchange("scatter", grad_refs, parts, send_sems, recv_sems, local_sems)
        pl.when(step_id == 0)(start)

        q = q_ref[0].reshape(r, HEAD_DIM)
        do2 = _stack_heads(do_ref, tq)
        qt = q.astype(F32).T
        dot32 = do2.astype(F32).T
        ot32 = _stack_heads(o_ref, tq).astype(F32).T
        drow = jnp.sum(dot32 * ot32, axis=0, keepdims=True)
        qt, dot = qt.astype(BF16), dot32.astype(BF16)
        lrow = l_ref[0, 0, 0, 0:1, :]

        @pl.when(i == 0)
        def _():
            dkt_sc[...] = jnp.zeros_like(dkt_sc)
            dvt_sc[...] = jnp.zeros_like(dvt_sc)

        def chunk(c):
            return pl.ds(pl.multiple_of(c * tk, tk), tk)

        def scores(c, slot):
            st_sc[slot] = _dot_nt(k_ref[0, 0, chunk(c), :], q)
            dp_sc[slot] = _dot_nt(v_ref[0, 0, chunk(c), :], do2)

        def fold(slot, c, dqt):
            pt = jnp.exp(st_sc[slot] - lrow)
            dsb = (pt * (dp_sc[slot] - drow)).astype(BF16)
            dvt_sc[:, chunk(c)] += _dot_nt(dot, pt.astype(BF16))
            dkt_sc[:, chunk(c)] += _dot_nt(qt, dsb)
            return dqt + _dot(kt_ref[0, 0, :, chunk(c)], dsb)

        scores(0, 0)

        def step(c2, dqt):
            c = 2 * c2
            scores(c + 1, 1)
            dqt = fold(0, c, dqt)
            scores(jnp.minimum(c + 2, nk - 1), 0)
            return fold(1, c + 1, dqt)

        dqt = lax.fori_loop(0, nk // 2, step, jnp.zeros((HEAD_DIM, r), F32))
        dq_ref[0] = dqt.T.reshape(GROUP, tq, HEAD_DIM)

        @pl.when(i == nq - 1)
        def _():
            dk_ref[0, 0] = dkt_sc[...].T
            dv_ref[0, 0] = dvt_sc[...].T

        pl.when(step_id == bl * kv * nq - 1)(wait)

    kvspec = pl.BlockSpec((1, 1, s_len, HEAD_DIM), lambda b, g, i: (b, g, 0, 0))
    qspec = pl.BlockSpec((1, GROUP, tq, HEAD_DIM), lambda b, g, i: (b, g, i, 0))
    tok = pl.BlockSpec((tq, GROUP * HEAD_DIM), lambda b, g, i: (b * nq + i, g))
    anyspec = pl.BlockSpec(memory_space=pl.ANY)
    res = pl.pallas_call(
        body,
        grid=(bl, kv, nq),
        in_specs=[qspec, kvspec, pl.BlockSpec((1, 1, HEAD_DIM, s_len), lambda b, g, i: (b, g, 0, 0)), kvspec,
                  tok, tok, pl.BlockSpec((1, 1, 1, 8, r), lambda b, g, i: (b, g, i, 0, 0))] + [anyspec] * ng,
        out_specs=[qspec, kvspec, kvspec] + [anyspec] * ng,
        out_shape=[SDS(qa.shape, F32), SDS(ka.shape, F32), SDS(va.shape, F32)]
        + [SDS(g.shape, g.dtype) for g in grads],
        scratch_shapes=[pltpu.VMEM((2, tk, r), F32), pltpu.VMEM((2, tk, r), F32),
                        pltpu.VMEM((HEAD_DIM, s_len), F32), pltpu.VMEM((HEAD_DIM, s_len), F32)]
        + _exchange_scratch(ng),
        compiler_params=_cp("arbitrary", "arbitrary", "arbitrary"),
        name="attn_a_bwd",
    )(qa, ka, kat, va, do, o, lse, *grads)
    return res[0], res[1], res[2], res[3:]


def _attn_b_bwd(qb, kb, kbt, vb, do, o, lse, bias_t, sink, s_len):
    bl, hb, _, _ = qb.shape
    kv, sp = kb.shape[1], kb.shape[2]
    nb = s_len // BLOCK
    nbs = min(QB_PER_STEP, nb)
    r = GROUP * BLOCK

    def body(q_ref, k_ref, kt_ref, v_ref, do_ref, o_ref, l_ref, bt_ref, sink_ref,
             dq_ref, dk_ref, dv_ref, dsum_ref, dsink_ref, dkt_sc, dvt_sc):
        g, b, ns = pl.program_id(0), pl.program_id(1), pl.program_id(2)
        sink_row = _sink_row(sink_ref, g)

        @pl.when(ns == 0)
        def _():
            dkt_sc[...] = jnp.zeros_like(dkt_sc)
            dvt_sc[...] = jnp.zeros_like(dvt_sc)

        @pl.when((b == 0) & (ns == 0))
        def _():
            dsum_ref[...] = jnp.zeros_like(dsum_ref)
            dsink_ref[...] = jnp.zeros_like(dsink_ref)

        dsum = jnp.zeros((SPAN, r), F32)
        dsink = jnp.zeros((1, r), F32)
        for j in range(nbs):
            n = ns * nbs + j
            span = pl.ds(pl.multiple_of(n * BLOCK, BLOCK), SPAN)
            rows = slice(j * BLOCK, (j + 1) * BLOCK)
            q = q_ref[0, :, rows, :].reshape(r, HEAD_DIM)
            do2 = jnp.concatenate([do_ref[rows, h * HEAD_DIM:(h + 1) * HEAD_DIM] for h in range(GROUP)], axis=0)
            o2 = jnp.concatenate([o_ref[rows, h * HEAD_DIM:(h + 1) * HEAD_DIM] for h in range(GROUP)], axis=0)
            dot32 = do2.astype(F32).T
            drow = jnp.sum(dot32 * o2.astype(F32).T, axis=0, keepdims=True)
            qt, dot = q.astype(F32).T.astype(BF16), dot32.astype(BF16)
            lrow = l_ref[0, 0, j, 0:1, :]
            st = _dot_nt(k_ref[0, 0, span, :], q) + bt_ref[_bias_variant(n, nb), 0]
            pt = jnp.exp(st - lrow)
            dst = pt * (_dot_nt(v_ref[0, 0, span, :], do2) - drow)
            dsum = dsum + dst
            dsink = dsink - jnp.exp(sink_row - lrow) * drow
            dsb = dst.astype(BF16)
            dvt_sc[:, span] += _dot_nt(dot, pt.astype(BF16))
            dkt_sc[:, span] += _dot_nt(qt, dsb)
            dq_ref[0, :, rows, :] = _dot(kt_ref[0, 0, :, span], dsb).T.reshape(GROUP, BLOCK, HEAD_DIM)
        dsum_ref[0] += dsum
        dsink_ref[0, 0:1, :] += dsink

        @pl.when(ns == nb // nbs - 1)
        def _():
            dk_ref[0, 0] = dkt_sc[:, BLOCK:BLOCK + s_len].T
            dv_ref[0, 0] = dvt_sc[:, BLOCK:BLOCK + s_len].T

    kvspec = pl.BlockSpec((1, 1, sp, HEAD_DIM), lambda g, b, n: (b, g, 0, 0))
    kvout = pl.BlockSpec((1, 1, s_len, HEAD_DIM), lambda g, b, n: (b, g, 0, 0))
    qspec = pl.BlockSpec((1, GROUP, nbs * BLOCK, HEAD_DIM), lambda g, b, n: (b, g, n, 0))
    tok = pl.BlockSpec((nbs * BLOCK, GROUP * HEAD_DIM), lambda g, b, n: (b * (nb // nbs) + n, g))
    return pl.pallas_call(
        body,
        grid=(kv, bl, nb // nbs),
        in_specs=[qspec, kvspec, pl.BlockSpec((1, 1, HEAD_DIM, sp), lambda g, b, n: (b, g, 0, 0)), kvspec, tok, tok,
                  pl.BlockSpec((1, 1, nbs, 8, r), lambda g, b, n: (b, g, n, 0, 0)),
                  pl.BlockSpec((3, 1, SPAN, r), lambda g, b, n: (0, g, 0, 0)),
                  pl.BlockSpec(memory_space=pltpu.SMEM)],
        out_specs=[qspec, kvout, kvout,
                   pl.BlockSpec((1, SPAN, r), lambda g, b, n: (g, 0, 0)),
                   pl.BlockSpec((1, 8, r), lambda g, b, n: (g, 0, 0))],
        out_shape=[SDS(qb.shape, F32), SDS((bl, kv, s_len, HEAD_DIM), F32), SDS((bl, kv, s_len, HEAD_DIM), F32),
                   SDS((kv, SPAN, r), F32), SDS((kv, 8, r), F32)],
        scratch_shapes=[pltpu.VMEM((HEAD_DIM, sp), F32), pltpu.VMEM((HEAD_DIM, sp), F32)],
        compiler_params=_cp("arbitrary", "arbitrary", "arbitrary"),
        name="attn_b_bwd",
    )(qb, kb, kbt, vb, do, o, lse, bias_t, sink)


def _bias_reduce(dsum, dsink, bucket_t4):
    kv, _, r = dsum.shape

    def body(ds_ref, dk_ref, bk_ref, rel_ref, sink_ref):
        lane = lax.broadcasted_iota(jnp.int32, (N_BUCKETS, 128), 1)
        lane8 = lax.broadcasted_iota(jnp.int32, (8, 128), 1)
        bk = bk_ref[...]
        for g in range(kv):
            ds = ds_ref[g]
            rowi = lax.broadcasted_iota(jnp.int32, (N_BUCKETS, r), 0)
            red = jnp.zeros((N_BUCKETS, r), F32)
            for b in range(N_BUCKETS):
                red = jnp.where(rowi == b, jnp.sum(jnp.where(bk == b, ds, 0.0), axis=0, keepdims=True), red)
            out = jnp.zeros((N_BUCKETS, 128), F32)
            so = jnp.zeros((8, 128), F32)
            for h in range(GROUP):
                col = jnp.sum(red[:, h * BLOCK:(h + 1) * BLOCK], axis=1, keepdims=True)
                out = jnp.where(lane == h, col, out)
                sc = jnp.sum(dk_ref[g][:, h * BLOCK:(h + 1) * BLOCK], axis=1, keepdims=True)
                so = jnp.where(lane8 == h, sc, so)
            rel_ref[g] = out
            sink_ref[g] = so

    vm = pl.BlockSpec(memory_space=pltpu.VMEM)
    return pl.pallas_call(
        body,
        in_specs=[vm, vm, vm],
        out_specs=[vm, vm],
        out_shape=[SDS((kv, N_BUCKETS, 128), F32), SDS((kv, 8, 128), F32)],
        name="bias_reduce",
    )(dsum, dsink, bucket_t4)


def _dqkprep(dqa, dka, dva, dqb, dkb, dvb, proj, cos, sin_signed, gq, gk, s_len, ts):
    nd, t, pb = proj.shape
    bl, ha = dqa.shape[0], dqa.shape[1]
    kva, hb, kvb = dka.shape[1], dqb.shape[1], dkb.shape[1]
    hpb = pb // HEAD_DIM
    ns = s_len // ts

    def body(dqa_ref, dka_ref, dva_ref, dqb_ref, dkb_ref, dvb_ref, p_ref, cos_ref, sin_ref, gq_ref, gk_ref,
             dp_ref, dgq_ref, dgk_ref):
        b, i = pl.program_id(0), pl.program_id(1)
        cs, sn = cos_ref[...], sin_ref[...]
        lane = lax.broadcasted_iota(jnp.int32, (ts, HEAD_DIM), 1)
        first = (lane % 32) < 16

        @pl.when((b == 0) & (i == 0))
        def _():
            dgq_ref[...] = jnp.zeros_like(dgq_ref)
            dgk_ref[...] = jnp.zeros_like(dgk_ref)

        def put(hh, val):
            dp_ref[hh // hpb, :, pl.ds((hh % hpb) * HEAD_DIM, HEAD_DIM)] = val.astype(BF16)

        def unrope_norm(d_rot, hh, g, dg_ref):
            dn = _rope(d_rot, cs, sn, first, transpose=True)
            _, n, r = _rms_fwd(_head_slice(p_ref, hh, hpb), g)
            dx, dgt = _rms_bwd(n, r, g, dn)
            dg_ref[0:1, :] += jnp.sum(dgt, axis=0, keepdims=True)
            put(hh, dx)

        for h in range(ha):
            unrope_norm(dqa_ref[0, h] * SCALE, h, gq_ref[...], dgq_ref)
        for h in range(kva):
            unrope_norm(dka_ref[0, h], ha + h, gk_ref[...], dgk_ref)
            put(ha + kva + h, dva_ref[0, h])
        base = ha + 2 * kva
        for h in range(hb):
            put(base + h, dqb_ref[0, h] * SCALE)
        for h in range(kvb):
            put(base + hb + h, dkb_ref[0, h])
            put(base + hb + kvb + h, dvb_ref[0, h])

    def hm(nh):
        return pl.BlockSpec((1, nh, ts, HEAD_DIM), lambda b, i: (b, 0, i, 0))

    vec = pl.BlockSpec((1, HEAD_DIM), lambda b, i: (0, 0))
    tab = pl.BlockSpec((ts, HEAD_DIM), lambda b, i: (i, 0))
    acc = pl.BlockSpec((8, HEAD_DIM), lambda b, i: (0, 0))
    pspec = pl.BlockSpec((nd, ts, pb), lambda b, i: (0, b * ns + i, 0))
    return pl.pallas_call(
        body,
        grid=(bl, ns),
        in_specs=[hm(ha), hm(kva), hm(kva), hm(hb), hm(kvb), hm(kvb), pspec, tab, tab, vec, vec],
        out_specs=[pspec, acc, acc],
        out_shape=[SDS((nd, t, pb), BF16), SDS((8, HEAD_DIM), F32), SDS((8, HEAD_DIM), F32)],
        compiler_params=_cp("arbitrary", "arbitrary"),
        name="dqkprep",
    )(dqa, dka, dva, dqb, dkb, dvb, proj, cos, sin_signed, gq, gk)


def _dx_final(dproj, win_g, x2, dx1, g1, tm, grads):
    t, d = x2.shape
    nd, _, pb = win_g.shape
    ng = len(grads)
    nsteps = t // tm

    def body(dp_ref, w_ref, x_ref, dx1_ref, g_ref, *rest):
        grad_refs, (dx_ref, dg_ref), parts = rest[:ng], rest[ng:ng + 2], rest[ng + 2:2 * ng + 2]
        start, wait = _direct_exchange("scatter", grad_refs, parts, *rest[2 * ng + 2:])

        @pl.when(pl.program_id(0) == 0)
        def _():
            start()
            dg_ref[...] = jnp.zeros_like(dg_ref)

        dh = _dot_nt(dp_ref[0], w_ref[0])
        for j in range(1, nd):
            dh = dh + _dot_nt(dp_ref[j], w_ref[j])
        g = g_ref[...]
        _, n, r = _rms_fwd(x_ref[...], g)
        dx, dgt = _rms_bwd(n, r, g, dh)
        dx_ref[...] = dx1_ref[...] + dx
        dg_ref[0:1, :] += jnp.sum(dgt, axis=0, keepdims=True)
        pl.when(pl.program_id(0) == nsteps - 1)(wait)

    tile = pl.BlockSpec((tm, d), lambda i: (i, 0))
    anyspec = pl.BlockSpec(memory_space=pl.ANY)
    res = pl.pallas_call(
        body,
        grid=(nsteps,),
        in_specs=[pl.BlockSpec((nd, tm, pb), lambda i: (0, i, 0)),
                  pl.BlockSpec((nd, d, pb), lambda i: (0, 0, 0)),
                  tile, tile, pl.BlockSpec((1, d), lambda i: (0, 0))] + [anyspec] * ng,
        out_specs=[tile, pl.BlockSpec((8, d), lambda i: (0, 0))] + [anyspec] * ng,
        out_shape=[SDS((t, d), F32), SDS((8, d), F32)] + [SDS(g.shape, g.dtype) for g in grads],
        scratch_shapes=_exchange_scratch(ng),
        compiler_params=_cp("arbitrary"),
        name="dx_final",
    )(dproj, win_g, x2, dx1, g1, *grads)
    return res[0], res[1], res[2:]


def _adamw_math(w, g, m, v):
    m = ADAM_B1 * m + (1.0 - ADAM_B1) * g
    v = ADAM_B2 * v + (1.0 - ADAM_B2) * (g * g)
    m_hat = m / (1.0 - ADAM_B1 ** ADAM_STEP)
    v_hat = v / (1.0 - ADAM_B2 ** ADAM_STEP)
    delta = -ADAM_LR * (m_hat / (jnp.sqrt(v_hat) + ADAM_EPS) + ADAM_WD * w)
    return delta, m, v


def _adamw_sum(parts, w, m, v, tr, name):
    rows, cols = w.shape

    def body(p_ref, w_ref, m_ref, v_ref, g_ref, d_ref, nm_ref, nv_ref):
        g = p_ref[0]
        for s in range(1, N_DEV):
            g = g + p_ref[s]
        g_ref[...] = g
        d_ref[...], nm_ref[...], nv_ref[...] = _adamw_math(w_ref[...], g, m_ref[...], v_ref[...])

    tr = min(tr, rows)
    tile = pl.BlockSpec((tr, cols), lambda i: (i, 0))
    return pl.pallas_call(
        body,
        grid=(rows // tr,),
        in_specs=[pl.BlockSpec((N_DEV, tr, cols), lambda i: (0, i, 0)), tile, tile, tile],
        out_specs=[tile] * 4,
        out_shape=[SDS((rows, cols), F32)] * 4,
        compiler_params=_cp("parallel"),
        name=name,
    )(parts, w, m, v)


def _adamw_small(g, w, m, v):
    def body(g_ref, w_ref, m_ref, v_ref, d_ref, nm_ref, nv_ref):
        d_ref[...], nm_ref[...], nv_ref[...] = _adamw_math(w_ref[...], g_ref[...], m_ref[...], v_ref[...])

    vm = pl.BlockSpec(memory_space=pltpu.VMEM)
    return pl.pallas_call(
        body,
        in_specs=[vm] * 4,
        out_specs=[vm] * 3,
        out_shape=[SDS(w.shape, F32)] * 3,
        name="adamw_small",
    )(g, w, m, v)


SMALL_ROWS = 8
SMALL_COLS = 1024


def _pack_small(g1, g2, g3, g4, gq, gk, sink, rel):
    row4 = jnp.concatenate([gq.reshape(-1), gk.reshape(-1), sink.reshape(-1)])
    row4 = jnp.pad(row4, (0, SMALL_COLS - row4.shape[0]))
    row5 = jnp.pad(rel.reshape(-1), (0, SMALL_COLS - rel.size))
    zero = jnp.zeros((SMALL_COLS,), F32)
    return jnp.stack([g1.reshape(-1), g2.reshape(-1), g3.reshape(-1), g4.reshape(-1), row4, row5, zero, zero])


def _unpack_small(p, hb):
    hd = HEAD_DIM
    return (p[0:1], p[1:2], p[2:3], p[3:4], p[4:5, 0:hd], p[4:5, hd:2 * hd], p[4:5, 2 * hd:2 * hd + hb],
            p[5, 0:N_BUCKETS * hb].reshape(N_BUCKETS, hb))


def _local_step(x, loss_target, win_g, wo_s, wup_s, wdn_s, g_pre_mix, g_post_mix, q_norm_a, k_norm_a, sink_b,
                rel_bias, g_pre_ffn, g_post_ffn):
    bl, s_len, d = x.shape
    t = bl * s_len
    nh = d // HEAD_DIM
    ha = nh // 2
    kva = ha // GROUP
    hb = nh - ha
    kvb = hb // GROUP
    tm = 512
    tw = min(4096, t)
    ts = min(512, s_len)
    tq, tk = 2 * BLOCK, min(512, s_len // 2)

    x2 = x.reshape(t, d)
    tg2 = loss_target.reshape(t, d)
    cos, sin_signed = _rope_tables(s_len)
    a = jnp.arange(BLOCK, dtype=jnp.int32)
    c = jnp.arange(SPAN, dtype=jnp.int32)
    bucket_t = _t5_bucket(c[:, None] - BLOCK - a[None, :])
    bucket_t4 = jnp.tile(bucket_t, (1, GROUP))

    h1, proj = _inproj(x2, g_pre_mix, win_g, tm)
    qa, ka, kat, va, vat, qb, kb, kbt, vb, vbt = _qkprep(
        proj, cos, sin_signed, q_norm_a, k_norm_a, bl, s_len, ha, kva, hb, kvb, ts)
    bias_t = _bias_build(bucket_t, rel_bias, hb)
    oa, lse_a, (wo_g, wup_g, wdn_g) = _attn_a_fwd(qa, ka, vat, tq, tk, [wo_s, wup_s, wdn_s])
    wo = wo_g.reshape(-1, d)
    wdn = wdn_g.reshape(-1, d)
    ob, lse_b = _attn_b_fwd(qb, kb, vbt, bias_t, sink_b, s_len)
    mix, x1, h2 = _mixout(oa, ob, wo, x2, g_post_mix, g_pre_ffn, tm)
    u, df, dy, dg4, loss8 = _ffn_fwd(h2, wup_g, wdn, x1, tg2, g_post_ffn, tm, FFN_BLOCKS_PER_STEP)

    dpre, dx1, dmix, dg3, dg2 = _ffn_bwd(df, u, wdn, wup_g, x1, dy, mix, g_pre_ffn, g_post_mix, tm,
                                         FFN_BLOCKS_PER_STEP)
    gw_dn = _wgrad_rows([u], df, N_DEV, tw, True, "wgrad_down")
    gw_up = _wgrad_cols(h2, dpre, N_DEV, tw, "wgrad_up")
    gw_o = _wgrad_rows([oa, ob], dmix, N_DEV, tw, False, "wgrad_o")
    doa, dob = _attn_out_bwd(dmix, wo, oa.shape[1], tm)
    dqa, dka, dva, (p_o, p_up, p_dn) = _attn_a_bwd(qa, ka, kat, va, doa, oa, lse_a, tq, tk, [gw_o, gw_up, gw_dn])
    dqb, dkb, dvb, dsum, dsink = _attn_b_bwd(qb, kb, kbt, vb, dob, ob, lse_b, bias_t, sink_b, s_len)
    drel_g, dsink_g = _bias_reduce(dsum, dsink, bucket_t4)
    dproj, dgq, dgk = _dqkprep(dqa, dka, dva, dqb, dkb, dvb, proj, cos, sin_signed, q_norm_a, k_norm_a, s_len, ts)
    gw_in = _wgrad_cols_blocked(h1, dproj, tw, "wgrad_in")
    grad_x, dg1, (p_in,) = _dx_final(dproj, win_g, x2, dx1, g_pre_mix, tm, [gw_in])

    drel = jnp.transpose(drel_g[:, :, 0:GROUP], (1, 0, 2)).reshape(N_BUCKETS, hb)
    dsink_v = dsink_g[:, 0, 0:GROUP].reshape(1, hb)
    small = _pack_small(dg1[0], dg2[0], dg3[0], dg4[0], dgq[0], dgk[0], dsink_v, drel)
    small = small.at[6, 0].set(loss8[0, 0])
    return grad_x.reshape(bl, s_len, d), p_in, p_o, p_up, p_dn, small


def kernel(x, w_in, w_o, g_pre_mix, g_post_mix, q_norm_a, k_norm_a, sink_b, rel_bias, g_pre_ffn, w_ffn_up, w_ffn_down, g_post_ffn, loss_target, m_w_in, m_w_o, m_g_pre_mix, m_g_post_mix, m_q_norm_a, m_k_norm_a, m_sink_b, m_rel_bias, m_g_pre_ffn, m_w_ffn_up, m_w_ffn_down, m_g_post_ffn, v_w_in, v_w_o, v_g_pre_mix, v_g_post_mix, v_q_norm_a, v_k_norm_a, v_sink_b, v_rel_bias, v_g_pre_ffn, v_w_ffn_up, v_w_ffn_down, v_g_post_ffn):
    hb = sink_b.shape[1]
    (win_g,) = _weight_gather([w_in[0].astype(BF16)])

    grad_x, p_in, p_o, p_up, p_dn, small = _local_step(
        x, loss_target, win_g, w_o[0].astype(BF16), w_ffn_up[0].astype(BF16), w_ffn_down[0].astype(BF16),
        g_pre_mix, g_post_mix, q_norm_a, k_norm_a, sink_b, rel_bias, g_pre_ffn, g_post_ffn)

    small = _small_allreduce(small)

    g_in, d_in, nm_in, nv_in = _adamw_sum(p_in, w_in[0], m_w_in[0], v_w_in[0], 256, "adamw_in")
    g_o, d_o, nm_o, nv_o = _adamw_sum(p_o, w_o[0], m_w_o[0], v_w_o[0], 128, "adamw_o")
    g_up, d_up, nm_up, nv_up = _adamw_sum(p_up, w_ffn_up[0], m_w_ffn_up[0], v_w_ffn_up[0], 256, "adamw_up")
    g_dn, d_dn, nm_dn, nv_dn = _adamw_sum(p_dn, w_ffn_down[0], m_w_ffn_down[0], v_w_ffn_down[0], 256, "adamw_down")

    pack = lambda *a: _pack_small(*a)
    w_s = pack(g_pre_mix, g_post_mix, g_pre_ffn, g_post_ffn, q_norm_a, k_norm_a, sink_b, rel_bias)
    m_s = pack(m_g_pre_mix, m_g_post_mix, m_g_pre_ffn, m_g_post_ffn, m_q_norm_a, m_k_norm_a, m_sink_b, m_rel_bias)
    v_s = pack(v_g_pre_mix, v_g_post_mix, v_g_pre_ffn, v_g_post_ffn, v_q_norm_a, v_k_norm_a, v_sink_b, v_rel_bias)
    d_s, nm_s, nv_s = _adamw_small(small, w_s, m_s, v_s)

    loss = small[6, 0]

    def outs(big_in, big_o, sm, big_up, big_dn):
        s1, s2, s3, s4, sq, sk, ss, sr = _unpack_small(sm, hb)
        return [big_in[None], big_o[None], s1, s2, sq, sk, ss, sr, s3, big_up[None], big_dn[None], s4]

    return (loss, grad_x,
            *outs(g_in, g_o, small, g_up, g_dn),
            *outs(d_in, d_o, d_s, d_up, d_dn),
            *outs(nm_in, nm_o, nm_s, nm_up, nm_dn),
            *outs(nv_in, nv_o, nv_s, nv_up, nv_dn))
```

```python
import functools

import jax
import jax.numpy as jnp
import numpy as np
from jax import lax
from jax.experimental import pallas as pl
from jax.experimental.pallas import tpu as pltpu

F32 = jnp.float32
BF16 = jnp.bfloat16
SDS = jax.ShapeDtypeStruct

N_DEV = 8
HEAD_DIM = 64
GROUP = 4
BLOCK = 128
SPAN = 3 * BLOCK
GRID_W = 64
N_BUCKETS = 32
MAX_DISTANCE = 128
ROPE_THETA = 10000.0
EPS = 1e-6
NEG_INF = -1e30
SCALE = HEAD_DIM ** -0.5
VT_PAD = 16

ADAM_LR = 0.001
ADAM_B1 = 0.9
ADAM_B2 = 0.999
ADAM_EPS = 1e-08
ADAM_WD = 0.01
ADAM_STEP = 10

VMEM_LIMIT = 56 * 1024 * 1024
MESH = pl.DeviceIdType.MESH


def _cp(*sem):
    return pltpu.CompilerParams(dimension_semantics=sem, vmem_limit_bytes=VMEM_LIMIT)


def _dot(a, b):
    return jnp.dot(a, b, preferred_element_type=F32)


def _dot_nt(a, b):
    return lax.dot_general(a, b, (((1,), (1,)), ((), ())), preferred_element_type=F32)


def _dot_tn(a, b):
    return lax.dot_general(a, b, (((0,), (0,)), ((), ())), preferred_element_type=F32)


def _rms_fwd(x, g):
    r = lax.rsqrt(jnp.mean(x * x, axis=-1, keepdims=True) + EPS)
    n = x * r
    return n * g, n, r


def _rms_bwd(n, r, g, dy):
    gd = g * dy
    dx = r * (gd - n * jnp.mean(n * gd, axis=-1, keepdims=True))
    return dx, dy * n


def _rope_tables(s_len):
    rows = s_len // GRID_W
    row = np.repeat(np.arange(rows, dtype=np.int32), GRID_W)
    col = np.tile(np.arange(GRID_W, dtype=np.int32), rows)
    nf = HEAD_DIM // 4
    freqs = np.float32(ROPE_THETA) ** (-np.arange(nf, dtype=np.float32) / np.float32(nf))
    ang_r = row.astype(np.float32)[:, None] * freqs[None, :]
    ang_c = col.astype(np.float32)[:, None] * freqs[None, :]
    cr, sr, cc, sc = np.cos(ang_r), np.sin(ang_r), np.cos(ang_c), np.sin(ang_c)
    cos = np.concatenate([cr, cr, cc, cc], axis=-1).astype(np.float32)
    sin_signed = np.concatenate([-sr, sr, -sc, sc], axis=-1).astype(np.float32)
    return jnp.asarray(cos), jnp.asarray(sin_signed)


def _t5_bucket(rel):
    nb = N_BUCKETS // 2
    ret = (rel > 0).astype(jnp.int32) * nb
    n = jnp.abs(rel)
    max_exact = nb // 2
    nf = jnp.maximum(n, 1).astype(F32)
    large = max_exact + (jnp.log(nf / max_exact) / np.float32(np.log(MAX_DISTANCE / max_exact))
                         * (nb - max_exact)).astype(jnp.int32)
    large = jnp.minimum(large, nb - 1)
    return ret + jnp.where(n < max_exact, n, large)


def _mesh_pos():
    return lax.axis_index("x"), lax.axis_index("y"), lax.axis_index("c")


def _lin(p):
    return 4 * p[0] + 2 * p[1] + p[2]


def _weight_gather(shards):
    n = len(shards)

    def body(*refs):
        xs, outs = refs[:n], refs[n:2 * n]
        send_sems, recv_sems, local_sems = refs[2 * n:]
        x, y, c = _mesh_pos()
        me, sibling = (x, y, c), (x, y, 1 - c)
        chips = [(1 - x, y), (x, 1 - y), (1 - x, 1 - y)]

        def copy(a, k, block, to, src=None):
            slot = outs[a].at[_lin(block)]
            return pltpu.make_async_remote_copy(
                src_ref=slot if src is None else src, dst_ref=slot,
                send_sem=send_sems.at[a, k], recv_sem=recv_sems.at[a, k],
                device_id=to, device_id_type=MESH)

        started = []
        for a in range(n):
            mine = pltpu.make_async_copy(xs[a], outs[a].at[_lin(me)], local_sems.at[a])
            mine.start()
            started.append(mine)
        sends = []
        for a in range(n):
            first = [copy(a, 0, me, sibling, src=xs[a])]
            first += [copy(a, 1 + j, me, (*chip, c), src=xs[a]) for j, chip in enumerate(chips)]
            for cp in first:
                cp.start()
            sends += first
        for a in range(n):
            for j, chip in enumerate(chips):
                copy(a, 1 + j, (*chip, c), me).wait_recv()
                fwd = copy(a, 4 + j, (*chip, c), sibling)
                fwd.start()
                sends.append(fwd)
        for a in range(n):
            copy(a, 0, sibling, me).wait_recv()
            for j, chip in enumerate(chips):
                copy(a, 4 + j, (*chip, 1 - c), me).wait_recv()
        for cp in sends:
            cp.wait_send()
        for mine in started:
            mine.wait()

    anyspec = pl.BlockSpec(memory_space=pl.ANY)
    return pl.pallas_call(
        body,
        out_shape=[SDS((N_DEV,) + s.shape, s.dtype) for s in shards],
        in_specs=[anyspec] * n,
        out_specs=[anyspec] * n,
        scratch_shapes=[pltpu.SemaphoreType.DMA((n, 7)), pltpu.SemaphoreType.DMA((n, 7)),
                        pltpu.SemaphoreType.DMA((n,))],
        name="weight_gather",
    )(*shards)


def _direct_exchange(kind, ins, outs, send_sems, recv_sems, local_sems):
    x, y, c = _mesh_pos()
    me = (x, y, c)
    peers = [(x, y, 1 - c), (1 - x, y, c), (x, 1 - y, c), (1 - x, 1 - y, c),
             (1 - x, y, 1 - c), (x, 1 - y, 1 - c), (1 - x, 1 - y, 1 - c)]

    def src(a, to):
        return ins[a] if kind == "gather" else ins[a].at[_lin(to)]

    def remote(a, k, to, frm):
        return pltpu.make_async_remote_copy(
            src_ref=src(a, to), dst_ref=outs[a].at[_lin(frm)],
            send_sem=send_sems.at[a, k], recv_sem=recv_sems.at[a, k],
            device_id=to, device_id_type=MESH)

    n = len(ins)
    sends = [remote(a, k, p, me) for a in range(n) for k, p in enumerate(peers)]
    arrivals = [remote(a, k, p, p) for a in range(n) for k, p in enumerate(peers)]
    local = [pltpu.make_async_copy(src(a, me), outs[a].at[_lin(me)], local_sems.at[a]) for a in range(n)]

    def start():
        for cp in local + sends:
            cp.start()

    def wait():
        for cp in arrivals:
            cp.wait_recv()
        for cp in sends:
            cp.wait_send()
        for cp in local:
            cp.wait()

    return start, wait


def _exchange_scratch(n):
    return [pltpu.SemaphoreType.DMA((n, 7)), pltpu.SemaphoreType.DMA((n, 7)), pltpu.SemaphoreType.DMA((n,))]


def _small_allreduce(v):
    rows, cols = v.shape

    def body(v_ref, out_ref, land_ref, send_sems, recv_sems):
        x, y, c = _mesh_pos()
        me = (x, y, c)
        peers = [(x, y, 1 - c), (1 - x, y, c), (x, 1 - y, c), (1 - x, 1 - y, c),
                 (1 - x, y, 1 - c), (x, 1 - y, 1 - c), (1 - x, 1 - y, 1 - c)]

        def copy(k, to, frm):
            return pltpu.make_async_remote_copy(
                src_ref=v_ref, dst_ref=land_ref.at[_lin(frm)],
                send_sem=send_sems.at[k], recv_sem=recv_sems.at[k],
                device_id=to, device_id_type=MESH)

        sends = [copy(k, p, me) for k, p in enumerate(peers)]
        for cp in sends:
            cp.start()
        land_ref[_lin(me)] = v_ref[...]
        for k, p in enumerate(peers):
            copy(k, p, p).wait_recv()
        for cp in sends:
            cp.wait_send()
        acc = land_ref[0]
        for s in range(1, N_DEV):
            acc = acc + land_ref[s]
        out_ref[...] = acc

    vm = pl.BlockSpec(memory_space=pltpu.VMEM)
    return pl.pallas_call(
        body,
        out_shape=SDS((rows, cols), F32),
        in_specs=[vm],
        out_specs=vm,
        scratch_shapes=[pltpu.VMEM((N_DEV, rows, cols), F32),
                        pltpu.SemaphoreType.DMA((7,)), pltpu.SemaphoreType.DMA((7,))],
        name="small_allreduce",
    )(v)


def _inproj(x2, g1, win_g, tm):
    t, d = x2.shape
    nd, _, pb = win_g.shape

    def body(x_ref, g_ref, w_ref, h_ref, p_ref):
        y, _, _ = _rms_fwd(x_ref[...], g_ref[...])
        h = y.astype(BF16)
        h_ref[...] = h
        for j in range(nd):
            p_ref[j] = _dot(h, w_ref[j])

    return pl.pallas_call(
        body,
        grid=(t // tm,),
        in_specs=[pl.BlockSpec((tm, d), lambda i: (i, 0)),
                  pl.BlockSpec((1, d), lambda i: (0, 0)),
                  pl.BlockSpec((nd, d, pb), lambda i: (0, 0, 0))],
        out_specs=[pl.BlockSpec((tm, d), lambda i: (i, 0)),
                   pl.BlockSpec((nd, tm, pb), lambda i: (0, i, 0))],
        out_shape=[SDS((t, d), BF16), SDS((nd, t, pb), F32)],
        compiler_params=_cp("parallel"),
        name="inproj",
    )(x2, g1, win_g)


def _head_slice(p_ref, hh, hpb):
    return p_ref[hh // hpb, :, pl.ds((hh % hpb) * HEAD_DIM, HEAD_DIM)]


def _rope(x, cos, sin_signed, first, transpose=False):
    def partner(v):
        return jnp.where(first, jnp.roll(v, -16, axis=1), jnp.roll(v, 16, axis=1))

    if transpose:
        return x * cos + partner(x * sin_signed)
    return x * cos + partner(x) * sin_signed


def _qkprep(proj, cos, sin_signed, gq, gk, bl, s_len, ha, kva, hb, kvb, ts):
    nd, t, pb = proj.shape
    hpb = pb // HEAD_DIM
    ns = s_len // ts
    sp = s_len + 2 * BLOCK

    def body(p_ref, cos_ref, sin_ref, gq_ref, gk_ref, qa_ref, ka_ref, kat_ref, va_ref, vat_ref, qb_ref, kb_ref,
             kbt_ref, vb_ref, vbt_ref):
        i = pl.program_id(1)
        cs, sn = cos_ref[...], sin_ref[...]
        lane = lax.broadcasted_iota(jnp.int32, (ts, HEAD_DIM), 1)
        first = (lane % 32) < 16
        ones_row = (lax.broadcasted_iota(jnp.int32, (VT_PAD, ts), 0) == 0).astype(BF16)

        def normrope(xh, g):
            y, _, _ = _rms_fwd(xh, g)
            return _rope(y, cs, sn, first)

        def transposed(xb):
            return xb.astype(F32).T.astype(BF16)

        for h in range(ha):
            qa_ref[0, h] = (normrope(_head_slice(p_ref, h, hpb), gq_ref[...]) * SCALE).astype(BF16)
        for h in range(kva):
            kh = normrope(_head_slice(p_ref, ha + h, hpb), gk_ref[...]).astype(BF16)
            ka_ref[0, h] = kh
            kat_ref[0, h] = transposed(kh)
            vh = _head_slice(p_ref, ha + kva + h, hpb).astype(BF16)
            va_ref[0, h] = vh
            vat_ref[0, h, 0:HEAD_DIM, :] = transposed(vh)
            vat_ref[0, h, HEAD_DIM:HEAD_DIM + VT_PAD, :] = ones_row
        base = ha + 2 * kva
        for h in range(hb):
            qb_ref[0, h] = (_head_slice(p_ref, base + h, hpb) * SCALE).astype(BF16)

        @pl.when(i == 0)
        def _():
            zeros = jnp.zeros((kvb, BLOCK, HEAD_DIM), BF16)
            zeros_t = jnp.zeros((kvb, HEAD_DIM + VT_PAD, BLOCK), BF16)
            for ref in (kb_ref, vb_ref):
                ref[0, :, 0:BLOCK, :] = zeros
                ref[0, :, sp - BLOCK:sp, :] = zeros
            kbt_ref[0, :, :, 0:BLOCK] = zeros_t[:, 0:HEAD_DIM]
            kbt_ref[0, :, :, sp - BLOCK:sp] = zeros_t[:, 0:HEAD_DIM]
            vbt_ref[0, :, :, 0:BLOCK] = zeros_t
            vbt_ref[0, :, :, sp - BLOCK:sp] = zeros_t

        row0 = pl.multiple_of(BLOCK + i * ts, BLOCK)
        for h in range(kvb):
            kh = _head_slice(p_ref, base + hb + h, hpb).astype(BF16)
            vh = _head_slice(p_ref, base + hb + kvb + h, hpb).astype(BF16)
            kb_ref[0, h, pl.ds(row0, ts), :] = kh
            vb_ref[0, h, pl.ds(row0, ts), :] = vh
            kbt_ref[0, h, :, pl.ds(row0, ts)] = transposed(kh)
            vbt_ref[0, h, 0:HEAD_DIM, pl.ds(row0, ts)] = transposed(vh)
            vbt_ref[0, h, HEAD_DIM:HEAD_DIM + VT_PAD, pl.ds(row0, ts)] = ones_row

    def hm(nh):
        return pl.BlockSpec((1, nh, ts, HEAD_DIM), lambda b, i: (b, 0, i, 0))

    def padded(nh):
        return pl.BlockSpec((1, nh, sp, HEAD_DIM), lambda b, i: (b, 0, 0, 0))

    def padded_t(nh, rows):
        return pl.BlockSpec((1, nh, rows, sp), lambda b, i: (b, 0, 0, 0))

    return pl.pallas_call(
        body,
        grid=(bl, ns),
        in_specs=[pl.BlockSpec((nd, ts, pb), lambda b, i: (0, b * ns + i, 0)),
                  pl.BlockSpec((ts, HEAD_DIM), lambda b, i: (i, 0)),
                  pl.BlockSpec((ts, HEAD_DIM), lambda b, i: (i, 0)),
                  pl.BlockSpec((1, HEAD_DIM), lambda b, i: (0, 0)),
                  pl.BlockSpec((1, HEAD_DIM), lambda b, i: (0, 0))],
        out_specs=[hm(ha), hm(kva), pl.BlockSpec((1, kva, HEAD_DIM, ts), lambda b, i: (b, 0, 0, i)), hm(kva),
                   pl.BlockSpec((1, kva, HEAD_DIM + VT_PAD, ts), lambda b, i: (b, 0, 0, i)),
                   hm(hb), padded(kvb), padded_t(kvb, HEAD_DIM), padded(kvb), padded_t(kvb, HEAD_DIM + VT_PAD)],
        out_shape=[SDS((bl, ha, s_len, HEAD_DIM), BF16), SDS((bl, kva, s_len, HEAD_DIM), BF16),
                   SDS((bl, kva, HEAD_DIM, s_len), BF16),
                   SDS((bl, kva, s_len, HEAD_DIM), BF16), SDS((bl, kva, HEAD_DIM + VT_PAD, s_len), BF16),
                   SDS((bl, hb, s_len, HEAD_DIM), BF16),
                   SDS((bl, kvb, sp, HEAD_DIM), BF16), SDS((bl, kvb, HEAD_DIM, sp), BF16),
                   SDS((bl, kvb, sp, HEAD_DIM), BF16), SDS((bl, kvb, HEAD_DIM + VT_PAD, sp), BF16)],
        compiler_params=_cp("parallel", "arbitrary"),
        name="qkprep",
    )(proj, cos, sin_signed, gq, gk)


def _bias_build(bucket_t, rel_bias, hb):
    kvb = hb // GROUP

    def body(bkt_ref, tbl_ref, out_ref):
        bkt = bkt_ref[...]
        ci = lax.broadcasted_iota(jnp.int32, (SPAN, BLOCK), 0)
        qi = lax.broadcasted_iota(jnp.int32, (SPAN, BLOCK), 1)
        band = jnp.abs(ci - BLOCK - qi) <= BLOCK
        masks = (band, band & (ci >= BLOCK), band & (ci < 2 * BLOCK))
        for h in range(hb):
            acct = jnp.zeros((SPAN, BLOCK), F32)
            for b in range(N_BUCKETS):
                acct = jnp.where(bkt == b, tbl_ref[b, h], acct)
            lanes = slice((h % GROUP) * BLOCK, (h % GROUP + 1) * BLOCK)
            for var, mask in enumerate(masks):
                out_ref[var, h // GROUP, :, lanes] = jnp.where(mask, acct, NEG_INF)

    vm = pl.BlockSpec(memory_space=pltpu.VMEM)
    return pl.pallas_call(
        body,
        in_specs=[vm, pl.BlockSpec(memory_space=pltpu.SMEM)],
        out_specs=vm,
        out_shape=SDS((3, kvb, SPAN, GROUP * BLOCK), F32),
        name="bias_build",
    )(bucket_t, rel_bias)


def _attn_a_fwd(qa, ka, vat, tq, tk, shards):
    bl, ha, s_len, _ = qa.shape
    kv = ka.shape[1]
    va_rows = vat.shape[2]
    nq, nk = s_len // tq, s_len // tk
    assert nk % 2 == 0
    r = GROUP * tq
    ns = len(shards)

    def body(q_ref, k_ref, v_ref, *rest):
        shard_refs, (o_ref, l_ref), gathered = rest[:ns], rest[ns:ns + 2], rest[ns + 2:2 * ns + 2]
        st_sc, send_sems, recv_sems, local_sems = rest[2 * ns + 2:]
        step_id = (pl.program_id(0) * kv + pl.program_id(1)) * nq + pl.program_id(2)
        start, wait = _direct_exchange("gather", shard_refs, gathered, send_sems, recv_sems, local_sems)
        pl.when(step_id == 0)(start)

        q = q_ref[0].reshape(r, HEAD_DIM)

        def scores(c):
            return _dot_nt(k_ref[0, 0, pl.ds(pl.multiple_of(c * tk, tk), tk), :], q)

        def fold(st, c, carry):
            m_old, acc = carry
            m_new = jnp.maximum(m_old, jnp.max(st, axis=0, keepdims=True))
            pt = jnp.exp(st - m_new).astype(BF16)
            vt = v_ref[0, 0, :, pl.ds(pl.multiple_of(c * tk, tk), tk)]
            return m_new, jnp.exp(m_old - m_new) * acc + _dot(vt, pt)

        st_sc[0] = scores(0)

        def step(c2, carry):
            c = 2 * c2
            st_sc[1] = scores(c + 1)
            carry = fold(st_sc[0], c, carry)
            st_sc[0] = scores(jnp.minimum(c + 2, nk - 1))
            return fold(st_sc[1], c + 1, carry)

        m, acc = lax.fori_loop(0, nk // 2, step,
                               (jnp.full((1, r), -jnp.inf, F32), jnp.zeros((va_rows, r), F32)))
        l = acc[HEAD_DIM:HEAD_DIM + 1, :]
        o = (acc[0:HEAD_DIM, :] / l).T
        for h in range(GROUP):
            o_ref[:, h * HEAD_DIM:(h + 1) * HEAD_DIM] = o[h * tq:(h + 1) * tq].astype(BF16)
        l_ref[0, 0, 0] = jnp.broadcast_to(m + jnp.log(l), (8, r))
        pl.when(step_id == bl * kv * nq - 1)(wait)

    anyspec = pl.BlockSpec(memory_space=pl.ANY)
    res = pl.pallas_call(
        body,
        grid=(bl, kv, nq),
        in_specs=[pl.BlockSpec((1, GROUP, tq, HEAD_DIM), lambda b, g, i: (b, g, i, 0)),
                  pl.BlockSpec((1, 1, s_len, HEAD_DIM), lambda b, g, i: (b, g, 0, 0)),
                  pl.BlockSpec((1, 1, va_rows, s_len), lambda b, g, i: (b, g, 0, 0))] + [anyspec] * ns,
        out_specs=[pl.BlockSpec((tq, GROUP * HEAD_DIM), lambda b, g, i: (b * nq + i, g)),
                   pl.BlockSpec((1, 1, 1, 8, r), lambda b, g, i: (b, g, i, 0, 0))] + [anyspec] * ns,
        out_shape=[SDS((bl * s_len, ha * HEAD_DIM), BF16), SDS((bl, kv, nq, 8, r), F32)]
        + [SDS((N_DEV,) + s.shape, s.dtype) for s in shards],
        scratch_shapes=[pltpu.VMEM((2, tk, r), F32)] + _exchange_scratch(ns),
        compiler_params=_cp("arbitrary", "arbitrary", "arbitrary"),
        name="attn_a_fwd",
    )(qa, ka, vat, *shards)
    return res[0], res[1], res[2:]


FFN_BLOCKS_PER_STEP = 4
QB_PER_STEP = 8


def _bias_variant(n, nb):
    return jnp.where(n == 0, 1, jnp.where(n == nb - 1, 2, 0))


def _sink_row(sink_ref, g):
    return jnp.concatenate([jnp.full((1, BLOCK), sink_ref[0, g * GROUP + h], F32) for h in range(GROUP)], axis=1)


def _attn_b_fwd(qb, kb, vbt, bias_t, sink, s_len):
    bl, hb, _, _ = qb.shape
    kv = kb.shape[1]
    sp = kb.shape[2]
    vt_rows = vbt.shape[2]
    nb = s_len // BLOCK
    nbs = min(QB_PER_STEP, nb)
    r = GROUP * BLOCK

    def body(q_ref, k_ref, vt_ref, bt_ref, sink_ref, o_ref, l_ref):
        g, n0 = pl.program_id(1), pl.program_id(2) * nbs
        sink_row = _sink_row(sink_ref, g)
        for j in range(nbs):
            n = n0 + j
            span = pl.ds(pl.multiple_of(n * BLOCK, BLOCK), SPAN)
            q = q_ref[0, :, j * BLOCK:(j + 1) * BLOCK, :].reshape(r, HEAD_DIM)
            st = _dot_nt(k_ref[0, 0, span, :], q) + bt_ref[_bias_variant(n, nb), 0]
            m = jnp.maximum(jnp.max(st, axis=0, keepdims=True), sink_row)
            acc = _dot(vt_ref[0, 0, :, span], jnp.exp(st - m).astype(BF16))
            l = acc[HEAD_DIM:HEAD_DIM + 1, :] + jnp.exp(sink_row - m)
            o = (acc[0:HEAD_DIM, :] / l).T
            for h in range(GROUP):
                o_ref[j * BLOCK:(j + 1) * BLOCK, h * HEAD_DIM:(h + 1) * HEAD_DIM] = (
                    o[h * BLOCK:(h + 1) * BLOCK].astype(BF16))
            l_ref[0, 0, j] = jnp.broadcast_to(m + jnp.log(l), (8, r))

    return pl.pallas_call(
        body,
        grid=(bl, kv, nb // nbs),
        in_specs=[pl.BlockSpec((1, GROUP, nbs * BLOCK, HEAD_DIM), lambda b, g, n: (b, g, n, 0)),
                  pl.BlockSpec((1, 1, sp, HEAD_DIM), lambda b, g, n: (b, g, 0, 0)),
                  pl.BlockSpec((1, 1, vt_rows, sp), lambda b, g, n: (b, g, 0, 0)),
                  pl.BlockSpec((3, 1, SPAN, r), lambda b, g, n: (0, g, 0, 0)),
                  pl.BlockSpec(memory_space=pltpu.SMEM)],
        out_specs=[pl.BlockSpec((nbs * BLOCK, GROUP * HEAD_DIM), lambda b, g, n: (b * (nb // nbs) + n, g)),
                   pl.BlockSpec((1, 1, nbs, 8, r), lambda b, g, n: (b, g, n, 0, 0))],
        out_shape=[SDS((bl * s_len, hb * HEAD_DIM), BF16), SDS((bl, kv, nb, 8, r), F32)],
        compiler_params=_cp("parallel", "parallel", "arbitrary"),
        name="attn_b_fwd",
    )(qb, kb, vbt, bias_t, sink)


def _mixout(oa, ob, wo, x2, g2, g3, tm):
    t, d = x2.shape
    ca = oa.shape[1]

    def body(oa_ref, ob_ref, w_ref, x_ref, g2_ref, g3_ref, mix_ref, x1_ref, h2_ref):
        mix = _dot(oa_ref[...], w_ref[0:ca, :]) + _dot(ob_ref[...], w_ref[ca:, :])
        mix_ref[...] = mix
        y2, _, _ = _rms_fwd(mix, g2_ref[...])
        x1 = x_ref[...] + y2
        x1_ref[...] = x1
        y3, _, _ = _rms_fwd(x1, g3_ref[...])
        h2_ref[...] = y3.astype(BF16)

    tile = lambda w: pl.BlockSpec((tm, w), lambda i: (i, 0))
    vec = pl.BlockSpec((1, d), lambda i: (0, 0))
    return pl.pallas_call(
        body,
        grid=(t // tm,),
        in_specs=[tile(ca), tile(ob.shape[1]), pl.BlockSpec(wo.shape, lambda i: (0, 0)), tile(d), vec, vec],
        out_specs=[tile(d), tile(d), tile(d)],
        out_shape=[SDS((t, d), F32), SDS((t, d), F32), SDS((t, d), BF16)],
        compiler_params=_cp("parallel"),
        name="mixout",
    )(oa, ob, wo, x2, g2, g3)


def _ffn_fwd(h2, wup_g, wdn, x1, target, g4, tm, jb):
    t, d = x1.shape
    nblk, _, tf = wup_g.shape
    ff = nblk * tf
    nt = t // tm
    nj = nblk // jb

    def body(h_ref, wu_ref, wd_ref, x1_ref, tg_ref, g_ref, u_ref, df_ref, dy_ref, dg_ref, loss_ref, acc_sc):
        i, j = pl.program_id(0), pl.program_id(1)

        @pl.when(j == 0)
        def _():
            acc_sc[...] = jnp.zeros_like(acc_sc)

        @pl.when((i == 0) & (j == 0))
        def _():
            dg_ref[...] = jnp.zeros_like(dg_ref)
            loss_ref[...] = jnp.zeros_like(loss_ref)

        h = h_ref[...]
        squares = []
        for s in range(jb):
            u = jnp.maximum(_dot(h, wu_ref[s]), 0.0)
            u_ref[:, s * tf:(s + 1) * tf] = u.astype(BF16)
            squares.append((u * u).astype(BF16))
        acc_sc[...] += _dot(jnp.concatenate(squares, axis=1), wd_ref[...])

        @pl.when(j == nj - 1)
        def _():
            g = g_ref[...]
            y4, n, r = _rms_fwd(acc_sc[...], g)
            e = (x1_ref[...] + y4) - tg_ref[...]
            loss_ref[...] += jnp.sum(e * e) * (0.5 / d)
            dy = e * (1.0 / d)
            dy_ref[...] = dy
            df, dgt = _rms_bwd(n, r, g, dy)
            df_ref[...] = df.astype(BF16)
            dg_ref[0:1, :] += jnp.sum(dgt, axis=0, keepdims=True)

    tile = pl.BlockSpec((tm, d), lambda i, j: (i, 0))
    return pl.pallas_call(
        body,
        grid=(nt, nj),
        in_specs=[tile,
                  pl.BlockSpec((jb, d, tf), lambda i, j: (j, 0, 0)),
                  pl.BlockSpec((jb * tf, d), lambda i, j: (j, 0)),
                  tile, tile,
                  pl.BlockSpec((1, d), lambda i, j: (0, 0))],
        out_specs=[pl.BlockSpec((tm, jb * tf), lambda i, j: (i, j)), tile, tile,
                   pl.BlockSpec((8, d), lambda i, j: (0, 0)),
                   pl.BlockSpec((8, 128), lambda i, j: (0, 0))],
        out_shape=[SDS((t, ff), BF16), SDS((t, d), BF16), SDS((t, d), F32), SDS((8, d), F32), SDS((8, 128), F32)],
        scratch_shapes=[pltpu.VMEM((tm, d), F32)],
        compiler_params=_cp("arbitrary", "arbitrary"),
        name="ffn_fwd",
    )(h2, wup_g, wdn, x1, target, g4)


def _ffn_bwd(df, u, wdn, wup_g, x1, dy, mix, g3, g2, tm, jb):
    t, d = x1.shape
    nblk, _, tf = wup_g.shape
    nt = t // tm
    nj = nblk // jb

    def body(df_ref, u_ref, wd_ref, wu_ref, x1_ref, dy_ref, mix_ref, g3_ref, g2_ref,
             dpre_ref, dx1_ref, dmix_ref, dg3_ref, dg2_ref, acc_sc):
        i, j = pl.program_id(0), pl.program_id(1)

        @pl.when(j == 0)
        def _():
            acc_sc[...] = jnp.zeros_like(acc_sc)

        @pl.when((i == 0) & (j == 0))
        def _():
            dg3_ref[...] = jnp.zeros_like(dg3_ref)
            dg2_ref[...] = jnp.zeros_like(dg2_ref)

        du2 = _dot_nt(df_ref[...], wd_ref[...])
        dpre = (2.0 * u_ref[...].astype(F32) * du2).astype(BF16)
        dpre_ref[...] = dpre
        dh = _dot_nt(dpre[:, 0:tf], wu_ref[0])
        for s in range(1, jb):
            dh = dh + _dot_nt(dpre[:, s * tf:(s + 1) * tf], wu_ref[s])
        acc_sc[...] += dh

        @pl.when(j == nj - 1)
        def _():
            g3, g2 = g3_ref[...], g2_ref[...]
            _, n3, r3 = _rms_fwd(x1_ref[...], g3)
            dx, dgt3 = _rms_bwd(n3, r3, g3, acc_sc[...])
            dx1 = dy_ref[...] + dx
            dx1_ref[...] = dx1
            dg3_ref[0:1, :] += jnp.sum(dgt3, axis=0, keepdims=True)
            _, n2, r2 = _rms_fwd(mix_ref[...], g2)
            dmix, dgt2 = _rms_bwd(n2, r2, g2, dx1)
            dmix_ref[...] = dmix.astype(BF16)
            dg2_ref[0:1, :] += jnp.sum(dgt2, axis=0, keepdims=True)

    tile = pl.BlockSpec((tm, d), lambda i, j: (i, 0))
    vec = pl.BlockSpec((1, d), lambda i, j: (0, 0))
    acc8 = pl.BlockSpec((8, d), lambda i, j: (0, 0))
    return pl.pallas_call(
        body,
        grid=(nt, nj),
        in_specs=[tile,
                  pl.BlockSpec((tm, jb * tf), lambda i, j: (i, j)),
                  pl.BlockSpec((jb * tf, d), lambda i, j: (j, 0)),
                  pl.BlockSpec((jb, d, tf), lambda i, j: (j, 0, 0)),
                  tile, tile, tile, vec, vec],
        out_specs=[pl.BlockSpec((tm, jb * tf), lambda i, j: (i, j)), tile, tile, acc8, acc8],
        out_shape=[SDS(u.shape, BF16), SDS((t, d), F32), SDS((t, d), BF16), SDS((8, d), F32), SDS((8, d), F32)],
        scratch_shapes=[pltpu.VMEM((tm, d), F32)],
        compiler_params=_cp("arbitrary", "arbitrary"),
        name="ffn_bwd",
    )(df, u, wdn, wup_g, x1, dy, mix, g3, g2)


def _wgrad(a, b, a_spec, b_spec, out_block, out_shape, nj, nk, name, prep_a=None, prep_b=None):
    acc_shape = out_block[1:]

    def body(a_ref, b_ref, o_ref, acc_sc):
        k = pl.program_id(1)
        av = a_ref[...] if prep_a is None else prep_a(a_ref)
        bv = b_ref[...] if prep_b is None else prep_b(b_ref)
        part = _dot_tn(av, bv)

        @pl.when(k == 0)
        def _():
            acc_sc[...] = part

        @pl.when(k > 0)
        def _():
            acc_sc[...] += part

        @pl.when(k == nk - 1)
        def _():
            o_ref[0] = acc_sc[...].astype(BF16)

    return pl.pallas_call(
        body,
        grid=(nj, nk),
        in_specs=[a_spec, b_spec],
        out_specs=pl.BlockSpec(out_block, lambda j, k: (j, 0, 0)),
        out_shape=SDS(out_shape, BF16),
        scratch_shapes=[pltpu.VMEM(acc_shape, F32)],
        compiler_params=_cp("parallel", "arbitrary"),
        name=name,
    )(a, b)


def _wgrad_cols(a, b, nj, tt, name):
    t, m = a.shape
    bn = b.shape[1] // nj
    return _wgrad(a, b, pl.BlockSpec((tt, m), lambda j, k: (k, 0)), pl.BlockSpec((tt, bn), lambda j, k: (k, j)),
                  (1, m, bn), (nj, m, bn), nj, t // tt, name)


def _wgrad_cols_blocked(a, b3, tt, name):
    t, m = a.shape
    nj, _, bn = b3.shape
    return _wgrad(a, b3, pl.BlockSpec((tt, m), lambda j, k: (k, 0)),
                  pl.BlockSpec((1, tt, bn), lambda j, k: (j, k, 0)),
                  (1, m, bn), (nj, m, bn), nj, t // tt, name, prep_b=lambda r: r[0])


def _wgrad_rows_squared(a, b, nj, tt, name):
    t, n = b.shape
    bm = a.shape[1] // nj

    def square(a_ref):
        af = a_ref[...].astype(F32)
        return (af * af).astype(BF16)

    return _wgrad(a, b, pl.BlockSpec((tt, bm), lambda j, k: (k, j)), pl.BlockSpec((tt, n), lambda j, k: (k, 0)),
                  (1, bm, n), (nj, bm, n), nj, t // tt, name, prep_a=square)


def _wgrad_o(oa, ob, dmix, nj, tt):
    t, n = dmix.shape
    ca, cb = oa.shape[1], ob.shape[1]
    m = ca + cb
    nk = t // tt

    def body(oa_ref, ob_ref, b_ref, o_ref, acc_sc):
        k = pl.program_id(0)
        part = _dot_tn(jnp.concatenate([oa_ref[...], ob_ref[...]], axis=1), b_ref[...])

        @pl.when(k == 0)
        def _():
            acc_sc[...] = part

        @pl.when(k > 0)
        def _():
            acc_sc[...] += part

        @pl.when(k == nk - 1)
        def _():
            o_ref[...] = acc_sc[...].reshape(nj, m // nj, n).astype(BF16)

    return pl.pallas_call(
        body,
        grid=(nk,),
        in_specs=[pl.BlockSpec((tt, ca), lambda k: (k, 0)), pl.BlockSpec((tt, cb), lambda k: (k, 0)),
                  pl.BlockSpec((tt, n), lambda k: (k, 0))],
        out_specs=pl.BlockSpec((nj, m // nj, n), lambda k: (0, 0, 0)),
        out_shape=SDS((nj, m // nj, n), BF16),
        scratch_shapes=[pltpu.VMEM((m, n), F32)],
        compiler_params=_cp("arbitrary"),
        name="wgrad_o",
    )(oa, ob, dmix)


def _attn_out_bwd(dmix, wo, ca, tm):
    t, d = dmix.shape
    cb = wo.shape[0] - ca

    def body(dm_ref, w_ref, da_ref, db_ref):
        dm = dm_ref[...]
        da_ref[...] = _dot_nt(dm, w_ref[0:ca, :]).astype(BF16)
        db_ref[...] = _dot_nt(dm, w_ref[ca:, :]).astype(BF16)

    return pl.pallas_call(
        body,
        grid=(t // tm,),
        in_specs=[pl.BlockSpec((tm, d), lambda i: (i, 0)), pl.BlockSpec(wo.shape, lambda i: (0, 0))],
        out_specs=[pl.BlockSpec((tm, ca), lambda i: (i, 0)), pl.BlockSpec((tm, cb), lambda i: (i, 0))],
        out_shape=[SDS((t, ca), BF16), SDS((t, cb), BF16)],
        compiler_params=_cp("parallel"),
        name="attn_out_bwd",
    )(dmix, wo)


def _stack_heads(ref, rows):
    return jnp.concatenate([ref[:, h * HEAD_DIM:(h + 1) * HEAD_DIM] for h in range(GROUP)], axis=0)


def _attn_a_bwd(qa, ka, kat, va, do, o, lse, tq, tk, grads):
    bl, ha, s_len, _ = qa.shape
    kv = ka.shape[1]
    nq, nk = s_len // tq, s_len // tk
    assert nk % 2 == 0
    r = GROUP * tq
    ng = len(grads)

    def body(q_ref, k_ref, kt_ref, v_ref, do_ref, o_ref, l_ref, *rest):
        grad_refs, (dq_ref, dk_ref, dv_ref), parts = rest[:ng], rest[ng:ng + 3], rest[ng + 3:2 * ng + 3]
        st_sc, dp_sc, dkt_sc, dvt_sc, send_sems, recv_sems, local_sems = rest[2 * ng + 3:]
        i = pl.program_id(2)
        step_id = (pl.program_id(0) * kv + pl.program_id(1)) * nq + i
        start, wait = _direct_exchange("scatter", grad_refs, parts, send_sems, recv_sems, local_sems)
        pl.when(step_id == 0)(start)

        q = q_ref[0].reshape(r, HEAD_DIM)
        do2 = _stack_heads(do_ref, tq)
        qt = q.astype(F32).T
        dot32 = do2.astype(F32).T
        ot32 = _stack_heads(o_ref, tq).astype(F32).T
        drow = jnp.sum(dot32 * ot32, axis=0, keepdims=True)
        qt, dot = qt.astype(BF16), dot32.astype(BF16)
        lrow = l_ref[0, 0, 0, 0:1, :]

        @pl.when(i == 0)
        def _():
            dkt_sc[...] = jnp.zeros_like(dkt_sc)
            dvt_sc[...] = jnp.zeros_like(dvt_sc)

        def chunk(c):
            return pl.ds(pl.multiple_of(c * tk, tk), tk)

        def scores(c, slot):
            st_sc[slot] = _dot_nt(k_ref[0, 0, chunk(c), :], q)
            dp_sc[slot] = _dot_nt(v_ref[0, 0, chunk(c), :], do2)

        def fold(slot, c, dqt):
            pt = jnp.exp(st_sc[slot] - lrow)
            dsb = (pt * (dp_sc[slot] - drow)).astype(BF16)
            dvt_sc[:, chunk(c)] += _dot_nt(dot, pt.astype(BF16))
            dkt_sc[:, chunk(c)] += _dot_nt(qt, dsb)
            return dqt + _dot(kt_ref[0, 0, :, chunk(c)], dsb)

        scores(0, 0)

        def step(c2, dqt):
            c = 2 * c2
            scores(c + 1, 1)
            dqt = fold(0, c, dqt)
            scores(jnp.minimum(c + 2, nk - 1), 0)
            return fold(1, c + 1, dqt)

        dqt = lax.fori_loop(0, nk // 2, step, jnp.zeros((HEAD_DIM, r), F32))
        dq_ref[0] = dqt.T.reshape(GROUP, tq, HEAD_DIM)

        @pl.when(i == nq - 1)
        def _():
            dk_ref[0, 0] = dkt_sc[...].T
            dv_ref[0, 0] = dvt_sc[...].T

        pl.when(step_id == bl * kv * nq - 1)(wait)

    kvspec = pl.BlockSpec((1, 1, s_len, HEAD_DIM), lambda b, g, i: (b, g, 0, 0))
    qspec = pl.BlockSpec((1, GROUP, tq, HEAD_DIM), lambda b, g, i: (b, g, i, 0))
    tok = pl.BlockSpec((tq, GROUP * HEAD_DIM), lambda b, g, i: (b * nq + i, g))
    anyspec = pl.BlockSpec(memory_space=pl.ANY)
    res = pl.pallas_call(
        body,
        grid=(bl, kv, nq),
        in_specs=[qspec, kvspec, pl.BlockSpec((1, 1, HEAD_DIM, s_len), lambda b, g, i: (b, g, 0, 0)), kvspec,
                  tok, tok, pl.BlockSpec((1, 1, 1, 8, r), lambda b, g, i: (b, g, i, 0, 0))] + [anyspec] * ng,
        out_specs=[qspec, kvspec, kvspec] + [anyspec] * ng,
        out_shape=[SDS(qa.shape, F32), SDS(ka.shape, F32), SDS(va.shape, F32)]
        + [SDS(g.shape, g.dtype) for g in grads],
        scratch_shapes=[pltpu.VMEM((2, tk, r), F32), pltpu.VMEM((2, tk, r), F32),
                        pltpu.VMEM((HEAD_DIM, s_len), F32), pltpu.VMEM((HEAD_DIM, s_len), F32)]
        + _exchange_scratch(ng),
        compiler_params=_cp("arbitrary", "arbitrary", "arbitrary"),
        name="attn_a_bwd",
    )(qa, ka, kat, va, do, o, lse, *grads)
    return res[0], res[1], res[2], res[3:]


def _attn_b_bwd(qb, kb, kbt, vb, do, o, lse, bias_t, sink, s_len):
    bl, hb, _, _ = qb.shape
    kv, sp = kb.shape[1], kb.shape[2]
    nb = s_len // BLOCK
    nbs = min(QB_PER_STEP, nb)
    r = GROUP * BLOCK

    def body(q_ref, k_ref, kt_ref, v_ref, do_ref, o_ref, l_ref, bt_ref, sink_ref,
             dq_ref, dk_ref, dv_ref, dsum_ref, dsink_ref, dkt_sc, dvt_sc):
        g, b, ns = pl.program_id(0), pl.program_id(1), pl.program_id(2)
        sink_row = _sink_row(sink_ref, g)

        @pl.when(ns == 0)
        def _():
            dkt_sc[...] = jnp.zeros_like(dkt_sc)
            dvt_sc[...] = jnp.zeros_like(dvt_sc)

        @pl.when((b == 0) & (ns == 0))
        def _():
            dsum_ref[...] = jnp.zeros_like(dsum_ref)
            dsink_ref[...] = jnp.zeros_like(dsink_ref)

        dsum = jnp.zeros((SPAN, r), F32)
        dsink = jnp.zeros((1, r), F32)
        for j in range(nbs):
            n = ns * nbs + j
            span = pl.ds(pl.multiple_of(n * BLOCK, BLOCK), SPAN)
            rows = slice(j * BLOCK, (j + 1) * BLOCK)
            q = q_ref[0, :, rows, :].reshape(r, HEAD_DIM)
            do2 = jnp.concatenate([do_ref[rows, h * HEAD_DIM:(h + 1) * HEAD_DIM] for h in range(GROUP)], axis=0)
            o2 = jnp.concatenate([o_ref[rows, h * HEAD_DIM:(h + 1) * HEAD_DIM] for h in range(GROUP)], axis=0)
            dot32 = do2.astype(F32).T
            drow = jnp.sum(dot32 * o2.astype(F32).T, axis=0, keepdims=True)
            qt, dot = q.astype(F32).T.astype(BF16), dot32.astype(BF16)
            lrow = l_ref[0, 0, j, 0:1, :]
            st = _dot_nt(k_ref[0, 0, span, :], q) + bt_ref[_bias_variant(n, nb), 0]
            pt = jnp.exp(st - lrow)
            dst = pt * (_dot_nt(v_ref[0, 0, span, :], do2) - drow)
            dsum = dsum + dst
            dsink = dsink - jnp.exp(sink_row - lrow) * drow
            dsb = dst.astype(BF16)
            dvt_sc[:, span] += _dot_nt(dot, pt.astype(BF16))
            dkt_sc[:, span] += _dot_nt(qt, dsb)
            dq_ref[0, :, rows, :] = _dot(kt_ref[0, 0, :, span], dsb).T.reshape(GROUP, BLOCK, HEAD_DIM)
        dsum_ref[0] += dsum
        dsink_ref[0, 0:1, :] += dsink

        @pl.when(ns == nb // nbs - 1)
        def _():
            dk_ref[0, 0] = dkt_sc[:, BLOCK:BLOCK + s_len].T
            dv_ref[0, 0] = dvt_sc[:, BLOCK:BLOCK + s_len].T

    kvspec = pl.BlockSpec((1, 1, sp, HEAD_DIM), lambda g, b, n: (b, g, 0, 0))
    kvout = pl.BlockSpec((1, 1, s_len, HEAD_DIM), lambda g, b, n: (b, g, 0, 0))
    qspec = pl.BlockSpec((1, GROUP, nbs * BLOCK, HEAD_DIM), lambda g, b, n: (b, g, n, 0))
    tok = pl.BlockSpec((nbs * BLOCK, GROUP * HEAD_DIM), lambda g, b, n: (b * (nb // nbs) + n, g))
    return pl.pallas_call(
        body,
        grid=(kv, bl, nb // nbs),
        in_specs=[qspec, kvspec, pl.BlockSpec((1, 1, HEAD_DIM, sp), lambda g, b, n: (b, g, 0, 0)), kvspec, tok, tok,
                  pl.BlockSpec((1, 1, nbs, 8, r), lambda g, b, n: (b, g, n, 0, 0)),
                  pl.BlockSpec((3, 1, SPAN, r), lambda g, b, n: (0, g, 0, 0)),
                  pl.BlockSpec(memory_space=pltpu.SMEM)],
        out_specs=[qspec, kvout, kvout,
                   pl.BlockSpec((1, SPAN, r), lambda g, b, n: (g, 0, 0)),
                   pl.BlockSpec((1, 8, r), lambda g, b, n: (g, 0, 0))],
        out_shape=[SDS(qb.shape, F32), SDS((bl, kv, s_len, HEAD_DIM), F32), SDS((bl, kv, s_len, HEAD_DIM), F32),
                   SDS((kv, SPAN, r), F32), SDS((kv, 8, r), F32)],
        scratch_shapes=[pltpu.VMEM((HEAD_DIM, sp), F32), pltpu.VMEM((HEAD_DIM, sp), F32)],
        compiler_params=_cp("arbitrary", "arbitrary", "arbitrary"),
        name="attn_b_bwd",
    )(qb, kb, kbt, vb, do, o, lse, bias_t, sink)


def _bias_reduce(dsum, dsink, bucket_t4):
    kv, _, r = dsum.shape

    def body(ds_ref, dk_ref, bk_ref, rel_ref, sink_ref):
        lane = lax.broadcasted_iota(jnp.int32, (N_BUCKETS, 128), 1)
        lane8 = lax.broadcasted_iota(jnp.int32, (8, 128), 1)
        bk = bk_ref[...]
        for g in range(kv):
            ds = ds_ref[g]
            rowi = lax.broadcasted_iota(jnp.int32, (N_BUCKETS, r), 0)
            red = jnp.zeros((N_BUCKETS, r), F32)
            for b in range(N_BUCKETS):
                red = jnp.where(rowi == b, jnp.sum(jnp.where(bk == b, ds, 0.0), axis=0, keepdims=True), red)
            out = jnp.zeros((N_BUCKETS, 128), F32)
            so = jnp.zeros((8, 128), F32)
            for h in range(GROUP):
                col = jnp.sum(red[:, h * BLOCK:(h + 1) * BLOCK], axis=1, keepdims=True)
                out = jnp.where(lane == h, col, out)
                sc = jnp.sum(dk_ref[g][:, h * BLOCK:(h + 1) * BLOCK], axis=1, keepdims=True)
                so = jnp.where(lane8 == h, sc, so)
            rel_ref[g] = out
            sink_ref[g] = so

    vm = pl.BlockSpec(memory_space=pltpu.VMEM)
    return pl.pallas_call(
        body,
        in_specs=[vm, vm, vm],
        out_specs=[vm, vm],
        out_shape=[SDS((kv, N_BUCKETS, 128), F32), SDS((kv, 8, 128), F32)],
        name="bias_reduce",
    )(dsum, dsink, bucket_t4)


def _dqkprep(dqa, dka, dva, dqb, dkb, dvb, proj, cos, sin_signed, gq, gk, s_len, ts):
    nd, t, pb = proj.shape
    bl, ha = dqa.shape[0], dqa.shape[1]
    kva, hb, kvb = dka.shape[1], dqb.shape[1], dkb.shape[1]
    hpb = pb // HEAD_DIM
    ns = s_len // ts

    def body(dqa_ref, dka_ref, dva_ref, dqb_ref, dkb_ref, dvb_ref, p_ref, cos_ref, sin_ref, gq_ref, gk_ref,
             dp_ref, dgq_ref, dgk_ref):
        b, i = pl.program_id(0), pl.program_id(1)
        cs, sn = cos_ref[...], sin_ref[...]
        lane = lax.broadcasted_iota(jnp.int32, (ts, HEAD_DIM), 1)
        first = (lane % 32) < 16

        @pl.when((b == 0) & (i == 0))
        def _():
            dgq_ref[...] = jnp.zeros_like(dgq_ref)
            dgk_ref[...] = jnp.zeros_like(dgk_ref)

        def put(hh, val):
            dp_ref[hh // hpb, :, pl.ds((hh % hpb) * HEAD_DIM, HEAD_DIM)] = val.astype(BF16)

        def unrope_norm(d_rot, hh, g, dg_ref):
            dn = _rope(d_rot, cs, sn, first, transpose=True)
            _, n, r = _rms_fwd(_head_slice(p_ref, hh, hpb), g)
            dx, dgt = _rms_bwd(n, r, g, dn)
            dg_ref[0:1, :] += jnp.sum(dgt, axis=0, keepdims=True)
            put(hh, dx)

        for h in range(ha):
            unrope_norm(dqa_ref[0, h] * SCALE, h, gq_ref[...], dgq_ref)
        for h in range(kva):
            unrope_norm(dka_ref[0, h], ha + h, gk_ref[...], dgk_ref)
            put(ha + kva + h, dva_ref[0, h])
        base = ha + 2 * kva
        for h in range(hb):
            put(base + h, dqb_ref[0, h] * SCALE)
        for h in range(kvb):
            put(base + hb + h, dkb_ref[0, h])
            put(base + hb + kvb + h, dvb_ref[0, h])

    def hm(nh):
        return pl.BlockSpec((1, nh, ts, HEAD_DIM), lambda b, i: (b, 0, i, 0))

    vec = pl.BlockSpec((1, HEAD_DIM), lambda b, i: (0, 0))
    tab = pl.BlockSpec((ts, HEAD_DIM), lambda b, i: (i, 0))
    acc = pl.BlockSpec((8, HEAD_DIM), lambda b, i: (0, 0))
    pspec = pl.BlockSpec((nd, ts, pb), lambda b, i: (0, b * ns + i, 0))
    return pl.pallas_call(
        body,
        grid=(bl, ns),
        in_specs=[hm(ha), hm(kva), hm(kva), hm(hb), hm(kvb), hm(kvb), pspec, tab, tab, vec, vec],
        out_specs=[pspec, acc, acc],
        out_shape=[SDS((nd, t, pb), BF16), SDS((8, HEAD_DIM), F32), SDS((8, HEAD_DIM), F32)],
        compiler_params=_cp("arbitrary", "arbitrary"),
        name="dqkprep",
    )(dqa, dka, dva, dqb, dkb, dvb, proj, cos, sin_signed, gq, gk)


def _dx_final(dproj, win_g, x2, dx1, g1, tm, grads):
    t, d = x2.shape
    nd, _, pb = win_g.shape
    ng = len(grads)
    nsteps = t // tm

    def body(dp_ref, w_ref, x_ref, dx1_ref, g_ref, *rest):
        grad_refs, (dx_ref, dg_ref), parts = rest[:ng], rest[ng:ng + 2], rest[ng + 2:2 * ng + 2]
        start, wait = _direct_exchange("scatter", grad_refs, parts, *rest[2 * ng + 2:])

        @pl.when(pl.program_id(0) == 0)
        def _():
            start()
            dg_ref[...] = jnp.zeros_like(dg_ref)

        dh = _dot_nt(dp_ref[0], w_ref[0])
        for j in range(1, nd):
            dh = dh + _dot_nt(dp_ref[j], w_ref[j])
        g = g_ref[...]
        _, n, r = _rms_fwd(x_ref[...], g)
        dx, dgt = _rms_bwd(n, r, g, dh)
        dx_ref[...] = dx1_ref[...] + dx
        dg_ref[0:1, :] += jnp.sum(dgt, axis=0, keepdims=True)
        pl.when(pl.program_id(0) == nsteps - 1)(wait)

    tile = pl.BlockSpec((tm, d), lambda i: (i, 0))
    anyspec = pl.BlockSpec(memory_space=pl.ANY)
    res = pl.pallas_call(
        body,
        grid=(nsteps,),
        in_specs=[pl.BlockSpec((nd, tm, pb), lambda i: (0, i, 0)),
                  pl.BlockSpec((nd, d, pb), lambda i: (0, 0, 0)),
                  tile, tile, pl.BlockSpec((1, d), lambda i: (0, 0))] + [anyspec] * ng,
        out_specs=[tile, pl.BlockSpec((8, d), lambda i: (0, 0))] + [anyspec] * ng,
        out_shape=[SDS((t, d), F32), SDS((8, d), F32)] + [SDS(g.shape, g.dtype) for g in grads],
        scratch_shapes=_exchange_scratch(ng),
        compiler_params=_cp("arbitrary"),
        name="dx_final",
    )(dproj, win_g, x2, dx1, g1, *grads)
    return res[0], res[1], res[2:]


def _adamw_math(w, g, m, v):
    m = ADAM_B1 * m + (1.0 - ADAM_B1) * g
    v = ADAM_B2 * v + (1.0 - ADAM_B2) * (g * g)
    m_hat = m / (1.0 - ADAM_B1 ** ADAM_STEP)
    v_hat = v / (1.0 - ADAM_B2 ** ADAM_STEP)
    delta = -ADAM_LR * (m_hat / (jnp.sqrt(v_hat) + ADAM_EPS) + ADAM_WD * w)
    return delta, m, v


def _adamw_sum(parts, w, m, v, tr, name):
    rows, cols = w.shape

    def body(p_ref, w_ref, m_ref, v_ref, g_ref, d_ref, nm_ref, nv_ref):
        g = p_ref[0].astype(F32)
        for s in range(1, N_DEV):
            g = g + p_ref[s].astype(F32)
        g_ref[...] = g
        d_ref[...], nm_ref[...], nv_ref[...] = _adamw_math(w_ref[...], g, m_ref[...], v_ref[...])

    tr = min(tr, rows)
    tile = pl.BlockSpec((tr, cols), lambda i: (i, 0))
    return pl.pallas_call(
        body,
        grid=(rows // tr,),
        in_specs=[pl.BlockSpec((N_DEV, tr, cols), lambda i: (0, i, 0)), tile, tile, tile],
        out_specs=[tile] * 4,
        out_shape=[SDS((rows, cols), F32)] * 4,
        compiler_params=_cp("parallel"),
        name=name,
    )(parts, w, m, v)


def _adamw_small(g, w, m, v):
    def body(g_ref, w_ref, m_ref, v_ref, d_ref, nm_ref, nv_ref):
        d_ref[...], nm_ref[...], nv_ref[...] = _adamw_math(w_ref[...], g_ref[...], m_ref[...], v_ref[...])

    vm = pl.BlockSpec(memory_space=pltpu.VMEM)
    return pl.pallas_call(
        body,
        in_specs=[vm] * 4,
        out_specs=[vm] * 3,
        out_shape=[SDS(w.shape, F32)] * 3,
        name="adamw_small",
    )(g, w, m, v)


SMALL_ROWS = 8
SMALL_COLS = 1024


def _pack_small(g1, g2, g3, g4, gq, gk, sink, rel):
    row4 = jnp.concatenate([gq.reshape(-1), gk.reshape(-1), sink.reshape(-1)])
    row4 = jnp.pad(row4, (0, SMALL_COLS - row4.shape[0]))
    row5 = jnp.pad(rel.reshape(-1), (0, SMALL_COLS - rel.size))
    zero = jnp.zeros((SMALL_COLS,), F32)
    return jnp.stack([g1.reshape(-1), g2.reshape(-1), g3.reshape(-1), g4.reshape(-1), row4, row5, zero, zero])


def _unpack_small(p, hb):
    hd = HEAD_DIM
    return (p[0:1], p[1:2], p[2:3], p[3:4], p[4:5, 0:hd], p[4:5, hd:2 * hd], p[4:5, 2 * hd:2 * hd + hb],
            p[5, 0:N_BUCKETS * hb].reshape(N_BUCKETS, hb))


def _local_step(x, loss_target, win_g, wo_s, wup_s, wdn_s, g_pre_mix, g_post_mix, q_norm_a, k_norm_a, sink_b,
                rel_bias, g_pre_ffn, g_post_ffn):
    bl, s_len, d = x.shape
    t = bl * s_len
    nh = d // HEAD_DIM
    ha = nh // 2
    kva = ha // GROUP
    hb = nh - ha
    kvb = hb // GROUP
    tm = 512
    tw = min(4096, t)
    ts = min(512, s_len)
    tq, tk = 2 * BLOCK, min(512, s_len // 2)

    x2 = x.reshape(t, d)
    tg2 = loss_target.reshape(t, d)
    cos, sin_signed = _rope_tables(s_len)
    a = jnp.arange(BLOCK, dtype=jnp.int32)
    c = jnp.arange(SPAN, dtype=jnp.int32)
    bucket_t = _t5_bucket(c[:, None] - BLOCK - a[None, :])
    bucket_t4 = jnp.tile(bucket_t, (1, GROUP))

    h1, proj = _inproj(x2, g_pre_mix, win_g, tm)
    qa, ka, kat, va, vat, qb, kb, kbt, vb, vbt = _qkprep(
        proj, cos, sin_signed, q_norm_a, k_norm_a, bl, s_len, ha, kva, hb, kvb, ts)
    bias_t = _bias_build(bucket_t, rel_bias, hb)
    oa, lse_a, (wo_g, wup_g, wdn_g) = _attn_a_fwd(qa, ka, vat, tq, tk, [wo_s, wup_s, wdn_s])
    wo = wo_g.reshape(-1, d)
    wdn = wdn_g.reshape(-1, d)
    ob, lse_b = _attn_b_fwd(qb, kb, vbt, bias_t, sink_b, s_len)
    mix, x1, h2 = _mixout(oa, ob, wo, x2, g_post_mix, g_pre_ffn, tm)
    u, df, dy, dg4, loss8 = _ffn_fwd(h2, wup_g, wdn, x1, tg2, g_post_ffn, tm, FFN_BLOCKS_PER_STEP)

    dpre, dx1, dmix, dg3, dg2 = _ffn_bwd(df, u, wdn, wup_g, x1, dy, mix, g_pre_ffn, g_post_mix, tm,
                                         FFN_BLOCKS_PER_STEP)
    gw_dn = _wgrad_rows_squared(u, df, N_DEV, tw, "wgrad_down")
    gw_up = _wgrad_cols(h2, dpre, N_DEV, tw, "wgrad_up")
    gw_o = _wgrad_o(oa, ob, dmix, N_DEV, min(2048, t))
    doa, dob = _attn_out_bwd(dmix, wo, oa.shape[1], tm)
    dqa, dka, dva, (p_o, p_up, p_dn) = _attn_a_bwd(qa, ka, kat, va, doa, oa, lse_a, tq, tk, [gw_o, gw_up, gw_dn])
    dqb, dkb, dvb, dsum, dsink = _attn_b_bwd(qb, kb, kbt, vb, dob, ob, lse_b, bias_t, sink_b, s_len)
    drel_g, dsink_g = _bias_reduce(dsum, dsink, bucket_t4)
    dproj, dgq, dgk = _dqkprep(dqa, dka, dva, dqb, dkb, dvb, proj, cos, sin_signed, q_norm_a, k_norm_a, s_len, ts)
    gw_in = _wgrad_cols_blocked(h1, dproj, tw, "wgrad_in")
    grad_x, dg1, (p_in,) = _dx_final(dproj, win_g, x2, dx1, g_pre_mix, tm, [gw_in])

    drel = jnp.transpose(drel_g[:, :, 0:GROUP], (1, 0, 2)).reshape(N_BUCKETS, hb)
    dsink_v = dsink_g[:, 0, 0:GROUP].reshape(1, hb)
    small = _pack_small(dg1[0], dg2[0], dg3[0], dg4[0], dgq[0], dgk[0], dsink_v, drel)
    small = small.at[6, 0].set(loss8[0, 0])
    return grad_x.reshape(bl, s_len, d), p_in, p_o, p_up, p_dn, small


def kernel(x, w_in, w_o, g_pre_mix, g_post_mix, q_norm_a, k_norm_a, sink_b, rel_bias, g_pre_ffn, w_ffn_up, w_ffn_down, g_post_ffn, loss_target, m_w_in, m_w_o, m_g_pre_mix, m_g_post_mix, m_q_norm_a, m_k_norm_a, m_sink_b, m_rel_bias, m_g_pre_ffn, m_w_ffn_up, m_w_ffn_down, m_g_post_ffn, v_w_in, v_w_o, v_g_pre_mix, v_g_post_mix, v_q_norm_a, v_k_norm_a, v_sink_b, v_rel_bias, v_g_pre_ffn, v_w_ffn_up, v_w_ffn_down, v_g_post_ffn):
    hb = sink_b.shape[1]
    (win_g,) = _weight_gather([w_in[0].astype(BF16)])

    grad_x, p_in, p_o, p_up, p_dn, small = _local_step(
        x, loss_target, win_g, w_o[0].astype(BF16), w_ffn_up[0].astype(BF16), w_ffn_down[0].astype(BF16),
        g_pre_mix, g_post_mix, q_norm_a, k_norm_a, sink_b, rel_bias, g_pre_ffn, g_post_ffn)

    small = _small_allreduce(small)

    g_in, d_in, nm_in, nv_in = _adamw_sum(p_in, w_in[0], m_w_in[0], v_w_in[0], 256, "adamw_in")
    g_o, d_o, nm_o, nv_o = _adamw_sum(p_o, w_o[0], m_w_o[0], v_w_o[0], 128, "adamw_o")
    g_up, d_up, nm_up, nv_up = _adamw_sum(p_up, w_ffn_up[0], m_w_ffn_up[0], v_w_ffn_up[0], 256, "adamw_up")
    g_dn, d_dn, nm_dn, nv_dn = _adamw_sum(p_dn, w_ffn_down[0], m_w_ffn_down[0], v_w_ffn_down[0], 256, "adamw_down")

    pack = lambda *a: _pack_small(*a)
    w_s = pack(g_pre_mix, g_post_mix, g_pre_ffn, g_post_ffn, q_norm_a, k_norm_a, sink_b, rel_bias)
    m_s = pack(m_g_pre_mix, m_g_post_mix, m_g_pre_ffn, m_g_post_ffn, m_q_norm_a, m_k_norm_a, m_sink_b, m_rel_bias)
    v_s = pack(v_g_pre_mix, v_g_post_mix, v_g_pre_ffn, v_g_post_ffn, v_q_norm_a, v_k_norm_a, v_sink_b, v_rel_bias)
    d_s, nm_s, nv_s = _adamw_small(small, w_s, m_s, v_s)

    loss = small[6, 0]

    def outs(big_in, big_o, sm, big_up, big_dn):
        s1, s2, s3, s4, sq, sk, ss, sr = _unpack_small(sm, hb)
        return [big_in[None], big_o[None], s1, s2, sq, sk, ss, sr, s3, big_up[None], big_dn[None], s4]

    return (loss, grad_x,
            *outs(g_in, g_o, small, g_up, g_dn),
            *outs(d_in, d_o, d_s, d_up, d_dn),
            *outs(nm_in, nm_o, nm_s, nm_up, nm_dn),
            *outs(nv_in, nv_o, nv_s, nv_up, nv_dn))
```

```python
import functools

import jax
import jax.numpy as jnp
import numpy as np
from jax import lax
from jax.experimental import pallas as pl
from jax.experimental.pallas import tpu as pltpu

F32 = jnp.float32
BF16 = jnp.bfloat16
SDS = jax.ShapeDtypeStruct

N_DEV = 8
HEAD_DIM = 64
GROUP = 4
BLOCK = 128
SPAN = 3 * BLOCK
GRID_W = 64
N_BUCKETS = 32
MAX_DISTANCE = 128
ROPE_THETA = 10000.0
EPS = 1e-6
NEG_INF = -1e30
SCALE = HEAD_DIM ** -0.5
VT_PAD = 16

ADAM_LR = 0.001
ADAM_B1 = 0.9
ADAM_B2 = 0.999
ADAM_EPS = 1e-08
ADAM_WD = 0.01
ADAM_STEP = 10

VMEM_LIMIT = 56 * 1024 * 1024
MESH = pl.DeviceIdType.MESH


def _cp(*sem):
    return pltpu.CompilerParams(dimension_semantics=sem, vmem_limit_bytes=VMEM_LIMIT)


def _dot(a, b):
    return jnp.dot(a, b, preferred_element_type=F32)


def _dot_nt(a, b):
    return lax.dot_general(a, b, (((1,), (1,)), ((), ())), preferred_element_type=F32)


def _dot_tn(a, b):
    return lax.dot_general(a, b, (((0,), (0,)), ((), ())), preferred_element_type=F32)


def _rms_fwd(x, g):
    r = lax.rsqrt(jnp.mean(x * x, axis=-1, keepdims=True) + EPS)
    n = x * r
    return n * g, n, r


def _rms_bwd(n, r, g, dy):
    gd = g * dy
    dx = r * (gd - n * jnp.mean(n * gd, axis=-1, keepdims=True))
    return dx, dy * n


def _rope_tables(s_len):
    rows = s_len // GRID_W
    row = np.repeat(np.arange(rows, dtype=np.int32), GRID_W)
    col = np.tile(np.arange(GRID_W, dtype=np.int32), rows)
    nf = HEAD_DIM // 4
    freqs = np.float32(ROPE_THETA) ** (-np.arange(nf, dtype=np.float32) / np.float32(nf))
    ang_r = row.astype(np.float32)[:, None] * freqs[None, :]
    ang_c = col.astype(np.float32)[:, None] * freqs[None, :]
    cr, sr, cc, sc = np.cos(ang_r), np.sin(ang_r), np.cos(ang_c), np.sin(ang_c)
    cos = np.concatenate([cr, cr, cc, cc], axis=-1).astype(np.float32)
    sin_signed = np.concatenate([-sr, sr, -sc, sc], axis=-1).astype(np.float32)
    return jnp.asarray(cos), jnp.asarray(sin_signed)


def _t5_bucket(rel):
    nb = N_BUCKETS // 2
    ret = (rel > 0).astype(jnp.int32) * nb
    n = jnp.abs(rel)
    max_exact = nb // 2
    nf = jnp.maximum(n, 1).astype(F32)
    large = max_exact + (jnp.log(nf / max_exact) / np.float32(np.log(MAX_DISTANCE / max_exact))
                         * (nb - max_exact)).astype(jnp.int32)
    large = jnp.minimum(large, nb - 1)
    return ret + jnp.where(n < max_exact, n, large)


def _mesh_pos():
    return lax.axis_index("x"), lax.axis_index("y"), lax.axis_index("c")


def _lin(p):
    return 4 * p[0] + 2 * p[1] + p[2]


def _weight_gather(shards):
    n = len(shards)

    def body(*refs):
        xs, outs = refs[:n], refs[n:2 * n]
        send_sems, recv_sems, local_sems = refs[2 * n:]
        x, y, c = _mesh_pos()
        me, sibling = (x, y, c), (x, y, 1 - c)
        chips = [(1 - x, y), (x, 1 - y), (1 - x, 1 - y)]

        def copy(a, k, block, to, src=None):
            slot = outs[a].at[_lin(block)]
            return pltpu.make_async_remote_copy(
                src_ref=slot if src is None else src, dst_ref=slot,
                send_sem=send_sems.at[a, k], recv_sem=recv_sems.at[a, k],
                device_id=to, device_id_type=MESH)

        started = []
        for a in range(n):
            mine = pltpu.make_async_copy(xs[a], outs[a].at[_lin(me)], local_sems.at[a])
            mine.start()
            started.append(mine)
        sends = []
        for a in range(n):
            first = [copy(a, 0, me, sibling, src=xs[a])]
            first += [copy(a, 1 + j, me, (*chip, c), src=xs[a]) for j, chip in enumerate(chips)]
            for cp in first:
                cp.start()
            sends += first
        for a in range(n):
            for j, chip in enumerate(chips):
                copy(a, 1 + j, (*chip, c), me).wait_recv()
                fwd = copy(a, 4 + j, (*chip, c), sibling)
                fwd.start()
                sends.append(fwd)
        for a in range(n):
            copy(a, 0, sibling, me).wait_recv()
            for j, chip in enumerate(chips):
                copy(a, 4 + j, (*chip, 1 - c), me).wait_recv()
        for cp in sends:
            cp.wait_send()
        for mine in started:
            mine.wait()

    anyspec = pl.BlockSpec(memory_space=pl.ANY)
    return pl.pallas_call(
        body,
        out_shape=[SDS((N_DEV,) + s.shape, s.dtype) for s in shards],
        in_specs=[anyspec] * n,
        out_specs=[anyspec] * n,
        scratch_shapes=[pltpu.SemaphoreType.DMA((n, 7)), pltpu.SemaphoreType.DMA((n, 7)),
                        pltpu.SemaphoreType.DMA((n,))],
        name="weight_gather",
    )(*shards)


def _direct_exchange(kind, ins, outs, send_sems, recv_sems, local_sems):
    x, y, c = _mesh_pos()
    me = (x, y, c)
    peers = [(x, y, 1 - c), (1 - x, y, c), (x, 1 - y, c), (1 - x, 1 - y, c),
             (1 - x, y, 1 - c), (x, 1 - y, 1 - c), (1 - x, 1 - y, 1 - c)]

    def src(a, to):
        return ins[a] if kind == "gather" else ins[a].at[_lin(to)]

    def remote(a, k, to, frm):
        return pltpu.make_async_remote_copy(
            src_ref=src(a, to), dst_ref=outs[a].at[_lin(frm)],
            send_sem=send_sems.at[a, k], recv_sem=recv_sems.at[a, k],
            device_id=to, device_id_type=MESH)

    n = len(ins)
    sends = [remote(a, k, p, me) for a in range(n) for k, p in enumerate(peers)]
    arrivals = [remote(a, k, p, p) for a in range(n) for k, p in enumerate(peers)]
    local = [pltpu.make_async_copy(src(a, me), outs[a].at[_lin(me)], local_sems.at[a]) for a in range(n)]

    def start():
        for cp in local + sends:
            cp.start()

    def wait():
        for cp in arrivals:
            cp.wait_recv()
        for cp in sends:
            cp.wait_send()
        for cp in local:
            cp.wait()

    return start, wait


def _exchange_scratch(n):
    return [pltpu.SemaphoreType.DMA((n, 7)), pltpu.SemaphoreType.DMA((n, 7)), pltpu.SemaphoreType.DMA((n,))]


SMALL_LANES = 128


def _small_allreduce(dg_rows, dgq, dgk, dsink_g, drel_g, loss8):
    d = dg_rows[0].shape[1]
    kv = dsink_g.shape[0]

    def body(g1_ref, g2_ref, g3_ref, g4_ref, gq_ref, gk_ref, sk_ref, rl_ref, ls_ref, vec_ref, rel_ref,
             vbuf, rbuf, vland, rland, send_sems, recv_sems):
        x, y, c = _mesh_pos()
        me = (x, y, c)
        peers = [(x, y, 1 - c), (1 - x, y, c), (x, 1 - y, c), (1 - x, 1 - y, c),
                 (1 - x, y, 1 - c), (x, 1 - y, 1 - c), (1 - x, 1 - y, 1 - c)]
        vbuf[...] = jnp.zeros_like(vbuf)
        rbuf[...] = jnp.zeros_like(rbuf)
        for row, ref in enumerate((g1_ref, g2_ref, g3_ref, g4_ref)):
            vbuf[row:row + 1, :] = ref[0:1, :]
        vbuf[4:5, 0:HEAD_DIM] = gq_ref[0:1, :]
        vbuf[4:5, SMALL_LANES:SMALL_LANES + HEAD_DIM] = gk_ref[0:1, :]
        for g in range(kv):
            vbuf[4:5, 2 * SMALL_LANES + g * GROUP:2 * SMALL_LANES + (g + 1) * GROUP] = sk_ref[g, 0:1, 0:GROUP]
            rbuf[:, g * GROUP:(g + 1) * GROUP] = rl_ref[g, :, 0:GROUP]
        vbuf[4:5, 3 * SMALL_LANES:3 * SMALL_LANES + 1] = ls_ref[0:1, 0:1]

        def copies(k, to, frm):
            return [pltpu.make_async_remote_copy(
                src_ref=buf, dst_ref=land.at[_lin(frm)], send_sem=send_sems.at[a, k], recv_sem=recv_sems.at[a, k],
                device_id=to, device_id_type=MESH) for a, (buf, land) in enumerate(((vbuf, vland), (rbuf, rland)))]

        sends = [cp for k, p in enumerate(peers) for cp in copies(k, p, me)]
        for cp in sends:
            cp.start()
        vland[_lin(me)] = vbuf[...]
        rland[_lin(me)] = rbuf[...]
        for k, p in enumerate(peers):
            for cp in copies(k, p, p):
                cp.wait_recv()
        for cp in sends:
            cp.wait_send()
        vacc, racc = vland[0], rland[0]
        for s in range(1, N_DEV):
            vacc, racc = vacc + vland[s], racc + rland[s]
        vec_ref[...] = vacc
        rel_ref[...] = racc

    vm = pl.BlockSpec(memory_space=pltpu.VMEM)
    return pl.pallas_call(
        body,
        out_shape=[SDS((8, d), F32), SDS((N_BUCKETS, 128), F32)],
        in_specs=[vm] * 9,
        out_specs=[vm, vm],
        scratch_shapes=[pltpu.VMEM((8, d), F32), pltpu.VMEM((N_BUCKETS, 128), F32),
                        pltpu.VMEM((N_DEV, 8, d), F32), pltpu.VMEM((N_DEV, N_BUCKETS, 128), F32),
                        pltpu.SemaphoreType.DMA((2, 7)), pltpu.SemaphoreType.DMA((2, 7))],
        name="small_allreduce",
    )(*dg_rows, dgq, dgk, dsink_g, drel_g, loss8)


def _inproj(x2, g1, win_g, tm):
    t, d = x2.shape
    nd, _, pb = win_g.shape

    def body(x_ref, g_ref, w_ref, h_ref, p_ref):
        y, _, _ = _rms_fwd(x_ref[...], g_ref[...])
        h = y.astype(BF16)
        h_ref[...] = h
        for j in range(nd):
            p_ref[j] = _dot(h, w_ref[j])

    return pl.pallas_call(
        body,
        grid=(t // tm,),
        in_specs=[pl.BlockSpec((tm, d), lambda i: (i, 0)),
                  pl.BlockSpec((1, d), lambda i: (0, 0)),
                  pl.BlockSpec((nd, d, pb), lambda i: (0, 0, 0))],
        out_specs=[pl.BlockSpec((tm, d), lambda i: (i, 0)),
                   pl.BlockSpec((nd, tm, pb), lambda i: (0, i, 0))],
        out_shape=[SDS((t, d), BF16), SDS((nd, t, pb), F32)],
        compiler_params=_cp("parallel"),
        name="inproj",
    )(x2, g1, win_g)


def _head_slice(p_ref, hh, hpb):
    return p_ref[hh // hpb, :, pl.ds((hh % hpb) * HEAD_DIM, HEAD_DIM)]


def _rope(x, cos, sin_signed, first, transpose=False):
    def partner(v):
        return jnp.where(first, jnp.roll(v, -16, axis=1), jnp.roll(v, 16, axis=1))

    if transpose:
        return x * cos + partner(x * sin_signed)
    return x * cos + partner(x) * sin_signed


def _qkprep(proj, cos, sin_signed, gq, gk, bl, s_len, ha, kva, hb, kvb, ts):
    nd, t, pb = proj.shape
    hpb = pb // HEAD_DIM
    ns = s_len // ts
    sp = s_len + 2 * BLOCK

    def body(p_ref, cos_ref, sin_ref, gq_ref, gk_ref, qa_ref, ka_ref, kat_ref, va_ref, vat_ref, qb_ref, kb_ref,
             kbt_ref, vb_ref, vbt_ref):
        i = pl.program_id(1)
        cs, sn = cos_ref[...], sin_ref[...]
        lane = lax.broadcasted_iota(jnp.int32, (ts, HEAD_DIM), 1)
        first = (lane % 32) < 16
        ones_row = (lax.broadcasted_iota(jnp.int32, (VT_PAD, ts), 0) == 0).astype(BF16)

        def normrope(xh, g):
            y, _, _ = _rms_fwd(xh, g)
            return _rope(y, cs, sn, first)

        def transposed(xb):
            return xb.astype(F32).T.astype(BF16)

        for h in range(ha):
            qa_ref[0, h] = (normrope(_head_slice(p_ref, h, hpb), gq_ref[...]) * SCALE).astype(BF16)
        for h in range(kva):
            kh = normrope(_head_slice(p_ref, ha + h, hpb), gk_ref[...]).astype(BF16)
            ka_ref[0, h] = kh
            kat_ref[0, h] = transposed(kh)
            vh = _head_slice(p_ref, ha + kva + h, hpb).astype(BF16)
            va_ref[0, h] = vh
            vat_ref[0, h, 0:HEAD_DIM, :] = transposed(vh)
            vat_ref[0, h, HEAD_DIM:HEAD_DIM + VT_PAD, :] = ones_row
        base = ha + 2 * kva
        for h in range(hb):
            qb_ref[0, h] = (_head_slice(p_ref, base + h, hpb) * SCALE).astype(BF16)

        @pl.when(i == 0)
        def _():
            zeros = jnp.zeros((kvb, BLOCK, HEAD_DIM), BF16)
            zeros_t = jnp.zeros((kvb, HEAD_DIM + VT_PAD, BLOCK), BF16)
            for ref in (kb_ref, vb_ref):
                ref[0, :, 0:BLOCK, :] = zeros
                ref[0, :, sp - BLOCK:sp, :] = zeros
            kbt_ref[0, :, :, 0:BLOCK] = zeros_t[:, 0:HEAD_DIM]
            kbt_ref[0, :, :, sp - BLOCK:sp] = zeros_t[:, 0:HEAD_DIM]
            vbt_ref[0, :, :, 0:BLOCK] = zeros_t
            vbt_ref[0, :, :, sp - BLOCK:sp] = zeros_t

        row0 = pl.multiple_of(BLOCK + i * ts, BLOCK)
        for h in range(kvb):
            kh = _head_slice(p_ref, base + hb + h, hpb).astype(BF16)
            vh = _head_slice(p_ref, base + hb + kvb + h, hpb).astype(BF16)
            kb_ref[0, h, pl.ds(row0, ts), :] = kh
            vb_ref[0, h, pl.ds(row0, ts), :] = vh
            kbt_ref[0, h, :, pl.ds(row0, ts)] = transposed(kh)
            vbt_ref[0, h, 0:HEAD_DIM, pl.ds(row0, ts)] = transposed(vh)
            vbt_ref[0, h, HEAD_DIM:HEAD_DIM + VT_PAD, pl.ds(row0, ts)] = ones_row

    def hm(nh):
        return pl.BlockSpec((1, nh, ts, HEAD_DIM), lambda b, i: (b, 0, i, 0))

    def padded(nh):
        return pl.BlockSpec((1, nh, sp, HEAD_DIM), lambda b, i: (b, 0, 0, 0))

    def padded_t(nh, rows):
        return pl.BlockSpec((1, nh, rows, sp), lambda b, i: (b, 0, 0, 0))

    return pl.pallas_call(
        body,
        grid=(bl, ns),
        in_specs=[pl.BlockSpec((nd, ts, pb), lambda b, i: (0, b * ns + i, 0)),
                  pl.BlockSpec((ts, HEAD_DIM), lambda b, i: (i, 0)),
                  pl.BlockSpec((ts, HEAD_DIM), lambda b, i: (i, 0)),
                  pl.BlockSpec((1, HEAD_DIM), lambda b, i: (0, 0)),
                  pl.BlockSpec((1, HEAD_DIM), lambda b, i: (0, 0))],
        out_specs=[hm(ha), hm(kva), pl.BlockSpec((1, kva, HEAD_DIM, ts), lambda b, i: (b, 0, 0, i)), hm(kva),
                   pl.BlockSpec((1, kva, HEAD_DIM + VT_PAD, ts), lambda b, i: (b, 0, 0, i)),
                   hm(hb), padded(kvb), padded_t(kvb, HEAD_DIM), padded(kvb), padded_t(kvb, HEAD_DIM + VT_PAD)],
        out_shape=[SDS((bl, ha, s_len, HEAD_DIM), BF16), SDS((bl, kva, s_len, HEAD_DIM), BF16),
                   SDS((bl, kva, HEAD_DIM, s_len), BF16),
                   SDS((bl, kva, s_len, HEAD_DIM), BF16), SDS((bl, kva, HEAD_DIM + VT_PAD, s_len), BF16),
                   SDS((bl, hb, s_len, HEAD_DIM), BF16),
                   SDS((bl, kvb, sp, HEAD_DIM), BF16), SDS((bl, kvb, HEAD_DIM, sp), BF16),
                   SDS((bl, kvb, sp, HEAD_DIM), BF16), SDS((bl, kvb, HEAD_DIM + VT_PAD, sp), BF16)],
        compiler_params=_cp("parallel", "arbitrary"),
        name="qkprep",
    )(proj, cos, sin_signed, gq, gk)


def _bias_build(bucket_t, rel_bias, hb):
    kvb = hb // GROUP

    def body(bkt_ref, tbl_ref, out_ref):
        bkt = bkt_ref[...]
        ci = lax.broadcasted_iota(jnp.int32, (SPAN, BLOCK), 0)
        qi = lax.broadcasted_iota(jnp.int32, (SPAN, BLOCK), 1)
        band = jnp.abs(ci - BLOCK - qi) <= BLOCK
        masks = (band, band & (ci >= BLOCK), band & (ci < 2 * BLOCK))
        for h in range(hb):
            acct = jnp.zeros((SPAN, BLOCK), F32)
            for b in range(N_BUCKETS):
                acct = jnp.where(bkt == b, tbl_ref[b, h], acct)
            lanes = slice((h % GROUP) * BLOCK, (h % GROUP + 1) * BLOCK)
            for var, mask in enumerate(masks):
                out_ref[var, h // GROUP, :, lanes] = jnp.where(mask, acct, NEG_INF)

    vm = pl.BlockSpec(memory_space=pltpu.VMEM)
    return pl.pallas_call(
        body,
        in_specs=[vm, pl.BlockSpec(memory_space=pltpu.SMEM)],
        out_specs=vm,
        out_shape=SDS((3, kvb, SPAN, GROUP * BLOCK), F32),
        name="bias_build",
    )(bucket_t, rel_bias)


def _attn_a_fwd(qa, ka, vat, tq, tk, shards):
    bl, ha, s_len, _ = qa.shape
    kv = ka.shape[1]
    va_rows = vat.shape[2]
    nq, nk = s_len // tq, s_len // tk
    assert nk % 2 == 0
    r = GROUP * tq
    ns = len(shards)

    def body(q_ref, k_ref, v_ref, *rest):
        shard_refs, (o_ref, l_ref), gathered = rest[:ns], rest[ns:ns + 2], rest[ns + 2:2 * ns + 2]
        st_sc, send_sems, recv_sems, local_sems = rest[2 * ns + 2:]
        step_id = (pl.program_id(0) * kv + pl.program_id(1)) * nq + pl.program_id(2)
        start, wait = _direct_exchange("gather", shard_refs, gathered, send_sems, recv_sems, local_sems)
        pl.when(step_id == 0)(start)

        q = q_ref[0].reshape(r, HEAD_DIM)

        def scores(c):
            return _dot_nt(k_ref[0, 0, pl.ds(pl.multiple_of(c * tk, tk), tk), :], q)

        def fold(st, c, carry):
            m_old, acc = carry
            m_new = jnp.maximum(m_old, jnp.max(st, axis=0, keepdims=True))
            pt = jnp.exp(st - m_new).astype(BF16)
            vt = v_ref[0, 0, :, pl.ds(pl.multiple_of(c * tk, tk), tk)]
            return m_new, jnp.exp(m_old - m_new) * acc + _dot(vt, pt)

        st_sc[0] = scores(0)

        def step(c2, carry):
            c = 2 * c2
            st_sc[1] = scores(c + 1)
            carry = fold(st_sc[0], c, carry)
            st_sc[0] = scores(jnp.minimum(c + 2, nk - 1))
            return fold(st_sc[1], c + 1, carry)

        m, acc = lax.fori_loop(0, nk // 2, step,
                               (jnp.full((1, r), -jnp.inf, F32), jnp.zeros((va_rows, r), F32)))
        l = acc[HEAD_DIM:HEAD_DIM + 1, :]
        o = (acc[0:HEAD_DIM, :] / l).T
        for h in range(GROUP):
            o_ref[:, h * HEAD_DIM:(h + 1) * HEAD_DIM] = o[h * tq:(h + 1) * tq].astype(BF16)
        l_ref[0, 0, 0] = jnp.broadcast_to(m + jnp.log(l), (8, r))
        pl.when(step_id == bl * kv * nq - 1)(wait)

    anyspec = pl.BlockSpec(memory_space=pl.ANY)
    res = pl.pallas_call(
        body,
        grid=(bl, kv, nq),
        in_specs=[pl.BlockSpec((1, GROUP, tq, HEAD_DIM), lambda b, g, i: (b, g, i, 0)),
                  pl.BlockSpec((1, 1, s_len, HEAD_DIM), lambda b, g, i: (b, g, 0, 0)),
                  pl.BlockSpec((1, 1, va_rows, s_len), lambda b, g, i: (b, g, 0, 0))] + [anyspec] * ns,
        out_specs=[pl.BlockSpec((tq, GROUP * HEAD_DIM), lambda b, g, i: (b * nq + i, g)),
                   pl.BlockSpec((1, 1, 1, 8, r), lambda b, g, i: (b, g, i, 0, 0))] + [anyspec] * ns,
        out_shape=[SDS((bl * s_len, ha * HEAD_DIM), BF16), SDS((bl, kv, nq, 8, r), F32)]
        + [SDS((N_DEV,) + s.shape, s.dtype) for s in shards],
        scratch_shapes=[pltpu.VMEM((2, tk, r), F32)] + _exchange_scratch(ns),
        compiler_params=_cp("arbitrary", "arbitrary", "arbitrary"),
        name="attn_a_fwd",
    )(qa, ka, vat, *shards)
    return res[0], res[1], res[2:]


FFN_BLOCKS_PER_STEP = 4
QB_PER_STEP = 8


def _bias_variant(n, nb):
    return jnp.where(n == 0, 1, jnp.where(n == nb - 1, 2, 0))


def _sink_row(sink_ref, g):
    return jnp.concatenate([jnp.full((1, BLOCK), sink_ref[0, g * GROUP + h], F32) for h in range(GROUP)], axis=1)


def _attn_b_fwd(qb, kb, vbt, bias_t, sink, s_len):
    bl, hb, _, _ = qb.shape
    kv = kb.shape[1]
    sp = kb.shape[2]
    vt_rows = vbt.shape[2]
    nb = s_len // BLOCK
    nbs = min(QB_PER_STEP, nb)
    r = GROUP * BLOCK

    def body(q_ref, k_ref, vt_ref, bt_ref, sink_ref, o_ref, l_ref):
        g, n0 = pl.program_id(1), pl.program_id(2) * nbs
        sink_row = _sink_row(sink_ref, g)
        for j in range(nbs):
            n = n0 + j
            span = pl.ds(pl.multiple_of(n * BLOCK, BLOCK), SPAN)
            q = q_ref[0, :, j * BLOCK:(j + 1) * BLOCK, :].reshape(r, HEAD_DIM)
            st = _dot_nt(k_ref[0, 0, span, :], q) + bt_ref[_bias_variant(n, nb), 0]
            m = jnp.maximum(jnp.max(st, axis=0, keepdims=True), sink_row)
            acc = _dot(vt_ref[0, 0, :, span], jnp.exp(st - m).astype(BF16))
            l = acc[HEAD_DIM:HEAD_DIM + 1, :] + jnp.exp(sink_row - m)
            o = (acc[0:HEAD_DIM, :] / l).T
            for h in range(GROUP):
                o_ref[j * BLOCK:(j + 1) * BLOCK, h * HEAD_DIM:(h + 1) * HEAD_DIM] = (
                    o[h * BLOCK:(h + 1) * BLOCK].astype(BF16))
            l_ref[0, 0, j] = jnp.broadcast_to(m + jnp.log(l), (8, r))

    return pl.pallas_call(
        body,
        grid=(bl, kv, nb // nbs),
        in_specs=[pl.BlockSpec((1, GROUP, nbs * BLOCK, HEAD_DIM), lambda b, g, n: (b, g, n, 0)),
                  pl.BlockSpec((1, 1, sp, HEAD_DIM), lambda b, g, n: (b, g, 0, 0)),
                  pl.BlockSpec((1, 1, vt_rows, sp), lambda b, g, n: (b, g, 0, 0)),
                  pl.BlockSpec((3, 1, SPAN, r), lambda b, g, n: (0, g, 0, 0)),
                  pl.BlockSpec(memory_space=pltpu.SMEM)],
        out_specs=[pl.BlockSpec((nbs * BLOCK, GROUP * HEAD_DIM), lambda b, g, n: (b * (nb // nbs) + n, g)),
                   pl.BlockSpec((1, 1, nbs, 8, r), lambda b, g, n: (b, g, n, 0, 0))],
        out_shape=[SDS((bl * s_len, hb * HEAD_DIM), BF16), SDS((bl, kv, nb, 8, r), F32)],
        compiler_params=_cp("parallel", "parallel", "arbitrary"),
        name="attn_b_fwd",
    )(qb, kb, vbt, bias_t, sink)


def _mixout(oa, ob, wo, x2, g2, g3, tm):
    t, d = x2.shape
    ca = oa.shape[1]

    def body(oa_ref, ob_ref, w_ref, x_ref, g2_ref, g3_ref, mix_ref, x1_ref, h2_ref):
        mix = _dot(oa_ref[...], w_ref[0:ca, :]) + _dot(ob_ref[...], w_ref[ca:, :])
        mix_ref[...] = mix
        y2, _, _ = _rms_fwd(mix, g2_ref[...])
        x1 = x_ref[...] + y2
        x1_ref[...] = x1
        y3, _, _ = _rms_fwd(x1, g3_ref[...])
        h2_ref[...] = y3.astype(BF16)

    tile = lambda w: pl.BlockSpec((tm, w), lambda i: (i, 0))
    vec = pl.BlockSpec((1, d), lambda i: (0, 0))
    return pl.pallas_call(
        body,
        grid=(t // tm,),
        in_specs=[tile(ca), tile(ob.shape[1]), pl.BlockSpec(wo.shape, lambda i: (0, 0)), tile(d), vec, vec],
        out_specs=[tile(d), tile(d), tile(d)],
        out_shape=[SDS((t, d), F32), SDS((t, d), F32), SDS((t, d), BF16)],
        compiler_params=_cp("parallel"),
        name="mixout",
    )(oa, ob, wo, x2, g2, g3)


def _ffn_fwd(h2, wup_g, wdn, x1, target, g4, tm, jb):
    t, d = x1.shape
    nblk, _, tf = wup_g.shape
    ff = nblk * tf
    nt = t // tm
    nj = nblk // jb

    def body(h_ref, wu_ref, wd_ref, x1_ref, tg_ref, g_ref, u_ref, df_ref, dy_ref, dg_ref, loss_ref, acc_sc):
        i, j = pl.program_id(0), pl.program_id(1)

        @pl.when(j == 0)
        def _():
            acc_sc[...] = jnp.zeros_like(acc_sc)

        @pl.when((i == 0) & (j == 0))
        def _():
            dg_ref[...] = jnp.zeros_like(dg_ref)
            loss_ref[...] = jnp.zeros_like(loss_ref)

        h = h_ref[...]
        squares = []
        for s in range(jb):
            u = jnp.maximum(_dot(h, wu_ref[s]), 0.0)
            u_ref[:, s * tf:(s + 1) * tf] = u.astype(BF16)
            squares.append((u * u).astype(BF16))
        acc_sc[...] += _dot(jnp.concatenate(squares, axis=1), wd_ref[...])

        @pl.when(j == nj - 1)
        def _():
            g = g_ref[...]
            y4, n, r = _rms_fwd(acc_sc[...], g)
            e = (x1_ref[...] + y4) - tg_ref[...]
            loss_ref[...] += jnp.sum(e * e) * (0.5 / d)
            dy = e * (1.0 / d)
            dy_ref[...] = dy
            df, dgt = _rms_bwd(n, r, g, dy)
            df_ref[...] = df.astype(BF16)
            dg_ref[0:1, :] += jnp.sum(dgt, axis=0, keepdims=True)

    tile = pl.BlockSpec((tm, d), lambda i, j: (i, 0))
    return pl.pallas_call(
        body,
        grid=(nt, nj),
        in_specs=[tile,
                  pl.BlockSpec((jb, d, tf), lambda i, j: (j, 0, 0)),
                  pl.BlockSpec((jb * tf, d), lambda i, j: (j, 0)),
                  tile, tile,
                  pl.BlockSpec((1, d), lambda i, j: (0, 0))],
        out_specs=[pl.BlockSpec((tm, jb * tf), lambda i, j: (i, j)), tile, tile,
                   pl.BlockSpec((8, d), lambda i, j: (0, 0)),
                   pl.BlockSpec((8, 128), lambda i, j: (0, 0))],
        out_shape=[SDS((t, ff), BF16), SDS((t, d), BF16), SDS((t, d), F32), SDS((8, d), F32), SDS((8, 128), F32)],
        scratch_shapes=[pltpu.VMEM((tm, d), F32)],
        compiler_params=_cp("arbitrary", "arbitrary"),
        name="ffn_fwd",
    )(h2, wup_g, wdn, x1, target, g4)


def _ffn_bwd(df, u, wdn, wup_g, x1, dy, mix, g3, g2, tm, jb):
    t, d = x1.shape
    nblk, _, tf = wup_g.shape
    nt = t // tm
    nj = nblk // jb

    def body(df_ref, u_ref, wd_ref, wu_ref, x1_ref, dy_ref, mix_ref, g3_ref, g2_ref,
             dpre_ref, dx1_ref, dmix_ref, dg3_ref, dg2_ref, acc_sc):
        i, j = pl.program_id(0), pl.program_id(1)

        @pl.when(j == 0)
        def _():
            acc_sc[...] = jnp.zeros_like(acc_sc)

        @pl.when((i == 0) & (j == 0))
        def _():
            dg3_ref[...] = jnp.zeros_like(dg3_ref)
            dg2_ref[...] = jnp.zeros_like(dg2_ref)

        du2 = _dot_nt(df_ref[...], wd_ref[...])
        dpre = (2.0 * u_ref[...].astype(F32) * du2).astype(BF16)
        dpre_ref[...] = dpre
        dh = _dot_nt(dpre[:, 0:tf], wu_ref[0])
        for s in range(1, jb):
            dh = dh + _dot_nt(dpre[:, s * tf:(s + 1) * tf], wu_ref[s])
        acc_sc[...] += dh

        @pl.when(j == nj - 1)
        def _():
            g3, g2 = g3_ref[...], g2_ref[...]
            _, n3, r3 = _rms_fwd(x1_ref[...], g3)
            dx, dgt3 = _rms_bwd(n3, r3, g3, acc_sc[...])
            dx1 = dy_ref[...] + dx
            dx1_ref[...] = dx1
            dg3_ref[0:1, :] += jnp.sum(dgt3, axis=0, keepdims=True)
            _, n2, r2 = _rms_fwd(mix_ref[...], g2)
            dmix, dgt2 = _rms_bwd(n2, r2, g2, dx1)
            dmix_ref[...] = dmix.astype(BF16)
            dg2_ref[0:1, :] += jnp.sum(dgt2, axis=0, keepdims=True)

    tile = pl.BlockSpec((tm, d), lambda i, j: (i, 0))
    vec = pl.BlockSpec((1, d), lambda i, j: (0, 0))
    acc8 = pl.BlockSpec((8, d), lambda i, j: (0, 0))
    return pl.pallas_call(
        body,
        grid=(nt, nj),
        in_specs=[tile,
                  pl.BlockSpec((tm, jb * tf), lambda i, j: (i, j)),
                  pl.BlockSpec((jb * tf, d), lambda i, j: (j, 0)),
                  pl.BlockSpec((jb, d, tf), lambda i, j: (j, 0, 0)),
                  tile, tile, tile, vec, vec],
        out_specs=[pl.BlockSpec((tm, jb * tf), lambda i, j: (i, j)), tile, tile, acc8, acc8],
        out_shape=[SDS(u.shape, BF16), SDS((t, d), F32), SDS((t, d), BF16), SDS((8, d), F32), SDS((8, d), F32)],
        scratch_shapes=[pltpu.VMEM((tm, d), F32)],
        compiler_params=_cp("arbitrary", "arbitrary"),
        name="ffn_bwd",
    )(df, u, wdn, wup_g, x1, dy, mix, g3, g2)


def _wgrad(a, b, a_spec, b_spec, out_block, out_shape, nj, nk, name, prep_a=None, prep_b=None):
    acc_shape = out_block[1:]

    def body(a_ref, b_ref, o_ref, acc_sc):
        k = pl.program_id(1)
        av = a_ref[...] if prep_a is None else prep_a(a_ref)
        bv = b_ref[...] if prep_b is None else prep_b(b_ref)
        part = _dot_tn(av, bv)

        @pl.when(k == 0)
        def _():
            acc_sc[...] = part

        @pl.when(k > 0)
        def _():
            acc_sc[...] += part

        @pl.when(k == nk - 1)
        def _():
            o_ref[0] = acc_sc[...].astype(BF16)

    return pl.pallas_call(
        body,
        grid=(nj, nk),
        in_specs=[a_spec, b_spec],
        out_specs=pl.BlockSpec(out_block, lambda j, k: (j, 0, 0)),
        out_shape=SDS(out_shape, BF16),
        scratch_shapes=[pltpu.VMEM(acc_shape, F32)],
        compiler_params=_cp("parallel", "arbitrary"),
        name=name,
    )(a, b)


def _wgrad_cols(a, b, nj, tt, name):
    t, m = a.shape
    bn = b.shape[1] // nj
    return _wgrad(a, b, pl.BlockSpec((tt, m), lambda j, k: (k, 0)), pl.BlockSpec((tt, bn), lambda j, k: (k, j)),
                  (1, m, bn), (nj, m, bn), nj, t // tt, name)


def _wgrad_cols_blocked(a, b3, tt, name):
    t, m = a.shape
    nj, _, bn = b3.shape
    return _wgrad(a, b3, pl.BlockSpec((tt, m), lambda j, k: (k, 0)),
                  pl.BlockSpec((1, tt, bn), lambda j, k: (j, k, 0)),
                  (1, m, bn), (nj, m, bn), nj, t // tt, name, prep_b=lambda r: r[0])


def _wgrad_rows_squared(a, b, nj, tt, name):
    t, n = b.shape
    bm = a.shape[1] // nj

    def square(a_ref):
        af = a_ref[...].astype(F32)
        return (af * af).astype(BF16)

    return _wgrad(a, b, pl.BlockSpec((tt, bm), lambda j, k: (k, j)), pl.BlockSpec((tt, n), lambda j, k: (k, 0)),
                  (1, bm, n), (nj, bm, n), nj, t // tt, name, prep_a=square)


def _wgrad_o(oa, ob, dmix, nj, tt):
    t, n = dmix.shape
    ca, cb = oa.shape[1], ob.shape[1]
    m = ca + cb
    nk = t // tt

    def body(oa_ref, ob_ref, b_ref, o_ref, acc_sc):
        k = pl.program_id(0)
        part = _dot_tn(jnp.concatenate([oa_ref[...], ob_ref[...]], axis=1), b_ref[...])

        @pl.when(k == 0)
        def _():
            acc_sc[...] = part

        @pl.when(k > 0)
        def _():
            acc_sc[...] += part

        @pl.when(k == nk - 1)
        def _():
            o_ref[...] = acc_sc[...].reshape(nj, m // nj, n).astype(BF16)

    return pl.pallas_call(
        body,
        grid=(nk,),
        in_specs=[pl.BlockSpec((tt, ca), lambda k: (k, 0)), pl.BlockSpec((tt, cb), lambda k: (k, 0)),
                  pl.BlockSpec((tt, n), lambda k: (k, 0))],
        out_specs=pl.BlockSpec((nj, m // nj, n), lambda k: (0, 0, 0)),
        out_shape=SDS((nj, m // nj, n), BF16),
        scratch_shapes=[pltpu.VMEM((m, n), F32)],
        compiler_params=_cp("arbitrary"),
        name="wgrad_o",
    )(oa, ob, dmix)


def _attn_out_bwd(dmix, wo, ca, tm):
    t, d = dmix.shape
    cb = wo.shape[0] - ca

    def body(dm_ref, w_ref, da_ref, db_ref):
        dm = dm_ref[...]
        da_ref[...] = _dot_nt(dm, w_ref[0:ca, :]).astype(BF16)
        db_ref[...] = _dot_nt(dm, w_ref[ca:, :]).astype(BF16)

    return pl.pallas_call(
        body,
        grid=(t // tm,),
        in_specs=[pl.BlockSpec((tm, d), lambda i: (i, 0)), pl.BlockSpec(wo.shape, lambda i: (0, 0))],
        out_specs=[pl.BlockSpec((tm, ca), lambda i: (i, 0)), pl.BlockSpec((tm, cb), lambda i: (i, 0))],
        out_shape=[SDS((t, ca), BF16), SDS((t, cb), BF16)],
        compiler_params=_cp("parallel"),
        name="attn_out_bwd",
    )(dmix, wo)


def _stack_heads(ref, rows):
    return jnp.concatenate([ref[:, h * HEAD_DIM:(h + 1) * HEAD_DIM] for h in range(GROUP)], axis=0)


def _attn_a_bwd(qa, ka, kat, va, do, o, lse, tq, tk, grads):
    bl, ha, s_len, _ = qa.shape
    kv = ka.shape[1]
    nq, nk = s_len // tq, s_len // tk
    assert nk % 2 == 0
    r = GROUP * tq
    ng = len(grads)

    def body(q_ref, k_ref, kt_ref, v_ref, do_ref, o_ref, l_ref, *rest):
        grad_refs, (dq_ref, dk_ref, dv_ref), parts = rest[:ng], rest[ng:ng + 3], rest[ng + 3:2 * ng + 3]
        st_sc, dp_sc, dkt_sc, dvt_sc, send_sems, recv_sems, local_sems = rest[2 * ng + 3:]
        i = pl.program_id(2)
        step_id = (pl.program_id(0) * kv + pl.program_id(1)) * nq + i
        start, wait = _direct_exchange("scatter", grad_refs, parts, send_sems, recv_sems, local_sems)
        pl.when(step_id == 0)(start)

        q = q_ref[0].reshape(r, HEAD_DIM)
        do2 = _stack_heads(do_ref, tq)
        qt = q.astype(F32).T
        dot32 = do2.astype(F32).T
        ot32 = _stack_heads(o_ref, tq).astype(F32).T
        drow = jnp.sum(dot32 * ot32, axis=0, keepdims=True)
        qt, dot = qt.astype(BF16), dot32.astype(BF16)
        lrow = l_ref[0, 0, 0, 0:1, :]

        @pl.when(i == 0)
        def _():
            dkt_sc[...] = jnp.zeros_like(dkt_sc)
            dvt_sc[...] = jnp.zeros_like(dvt_sc)

        def chunk(c):
            return pl.ds(pl.multiple_of(c * tk, tk), tk)

        def scores(c, slot):
            st_sc[slot] = _dot_nt(k_ref[0, 0, chunk(c), :], q)
            dp_sc[slot] = _dot_nt(v_ref[0, 0, chunk(c), :], do2)

        def fold(slot, c, dqt):
            pt = jnp.exp(st_sc[slot] - lrow)
            dsb = (pt * (dp_sc[slot] - drow)).astype(BF16)
            dvt_sc[:, chunk(c)] += _dot_nt(dot, pt.astype(BF16))
            dkt_sc[:, chunk(c)] += _dot_nt(qt, dsb)
            return dqt + _dot(kt_ref[0, 0, :, chunk(c)], dsb)

        scores(0, 0)

        def step(c2, dqt):
            c = 2 * c2
            scores(c + 1, 1)
            dqt = fold(0, c, dqt)
            scores(jnp.minimum(c + 2, nk - 1), 0)
            return fold(1, c + 1, dqt)

        dqt = lax.fori_loop(0, nk // 2, step, jnp.zeros((HEAD_DIM, r), F32))
        dq_ref[0] = dqt.T.reshape(GROUP, tq, HEAD_DIM)

        @pl.when(i == nq - 1)
        def _():
            dk_ref[0, 0] = dkt_sc[...].T
            dv_ref[0, 0] = dvt_sc[...].T

        pl.when(step_id == bl * kv * nq - 1)(wait)

    kvspec = pl.BlockSpec((1, 1, s_len, HEAD_DIM), lambda b, g, i: (b, g, 0, 0))
    qspec = pl.BlockSpec((1, GROUP, tq, HEAD_DIM), lambda b, g, i: (b, g, i, 0))
    tok = pl.BlockSpec((tq, GROUP * HEAD_DIM), lambda b, g, i: (b * nq + i, g))
    anyspec = pl.BlockSpec(memory_space=pl.ANY)
    res = pl.pallas_call(
        body,
        grid=(bl, kv, nq),
        in_specs=[qspec, kvspec, pl.BlockSpec((1, 1, HEAD_DIM, s_len), lambda b, g, i: (b, g, 0, 0)), kvspec,
                  tok, tok, pl.BlockSpec((1, 1, 1, 8, r), lambda b, g, i: (b, g, i, 0, 0))] + [anyspec] * ng,
        out_specs=[qspec, kvspec, kvspec] + [anyspec] * ng,
        out_shape=[SDS(qa.shape, F32), SDS(ka.shape, F32), SDS(va.shape, F32)]
        + [SDS(g.shape, g.dtype) for g in grads],
        scratch_shapes=[pltpu.VMEM((2, tk, r), F32), pltpu.VMEM((2, tk, r), F32),
                        pltpu.VMEM((HEAD_DIM, s_len), F32), pltpu.VMEM((HEAD_DIM, s_len), F32)]
        + _exchange_scratch(ng),
        compiler_params=_cp("arbitrary", "arbitrary", "arbitrary"),
        name="attn_a_bwd",
    )(qa, ka, kat, va, do, o, lse, *grads)
    return res[0], res[1], res[2], res[3:]


def _attn_b_bwd(qb, kb, kbt, vb, do, o, lse, bias_t, sink, s_len):
    bl, hb, _, _ = qb.shape
    kv, sp = kb.shape[1], kb.shape[2]
    nb = s_len // BLOCK
    nbs = min(QB_PER_STEP, nb)
    r = GROUP * BLOCK

    def body(q_ref, k_ref, kt_ref, v_ref, do_ref, o_ref, l_ref, bt_ref, sink_ref,
             dq_ref, dk_ref, dv_ref, dsum_ref, dsink_ref, dkt_sc, dvt_sc):
        g, b, ns = pl.program_id(0), pl.program_id(1), pl.program_id(2)
        sink_row = _sink_row(sink_ref, g)

        @pl.when(ns == 0)
        def _():
            dkt_sc[...] = jnp.zeros_like(dkt_sc)
            dvt_sc[...] = jnp.zeros_like(dvt_sc)

        @pl.when((b == 0) & (ns == 0))
        def _():
            dsum_ref[...] = jnp.zeros_like(dsum_ref)
            dsink_ref[...] = jnp.zeros_like(dsink_ref)

        dsum = jnp.zeros((SPAN, r), F32)
        dsink = jnp.zeros((1, r), F32)
        for j in range(nbs):
            n = ns * nbs + j
            span = pl.ds(pl.multiple_of(n * BLOCK, BLOCK), SPAN)
            rows = slice(j * BLOCK, (j + 1) * BLOCK)
            q = q_ref[0, :, rows, :].reshape(r, HEAD_DIM)
            do2 = jnp.concatenate([do_ref[rows, h * HEAD_DIM:(h + 1) * HEAD_DIM] for h in range(GROUP)], axis=0)
            o2 = jnp.concatenate([o_ref[rows, h * HEAD_DIM:(h + 1) * HEAD_DIM] for h in range(GROUP)], axis=0)
            dot32 = do2.astype(F32).T
            drow = jnp.sum(dot32 * o2.astype(F32).T, axis=0, keepdims=True)
            qt, dot = q.astype(F32).T.astype(BF16), dot32.astype(BF16)
            lrow = l_ref[0, 0, j, 0:1, :]
            st = _dot_nt(k_ref[0, 0, span, :], q) + bt_ref[_bias_variant(n, nb), 0]
            pt = jnp.exp(st - lrow)
            dst = pt * (_dot_nt(v_ref[0, 0, span, :], do2) - drow)
            dsum = dsum + dst
            dsink = dsink - jnp.exp(sink_row - lrow) * drow
            dsb = dst.astype(BF16)
            dvt_sc[:, span] += _dot_nt(dot, pt.astype(BF16))
            dkt_sc[:, span] += _dot_nt(qt, dsb)
            dq_ref[0, :, rows, :] = _dot(kt_ref[0, 0, :, span], dsb).T.reshape(GROUP, BLOCK, HEAD_DIM)
        dsum_ref[0] += dsum
        dsink_ref[0, 0:1, :] += dsink

        @pl.when(ns == nb // nbs - 1)
        def _():
            dk_ref[0, 0] = dkt_sc[:, BLOCK:BLOCK + s_len].T
            dv_ref[0, 0] = dvt_sc[:, BLOCK:BLOCK + s_len].T

    kvspec = pl.BlockSpec((1, 1, sp, HEAD_DIM), lambda g, b, n: (b, g, 0, 0))
    kvout = pl.BlockSpec((1, 1, s_len, HEAD_DIM), lambda g, b, n: (b, g, 0, 0))
    qspec = pl.BlockSpec((1, GROUP, nbs * BLOCK, HEAD_DIM), lambda g, b, n: (b, g, n, 0))
    tok = pl.BlockSpec((nbs * BLOCK, GROUP * HEAD_DIM), lambda g, b, n: (b * (nb // nbs) + n, g))
    return pl.pallas_call(
        body,
        grid=(kv, bl, nb // nbs),
        in_specs=[qspec, kvspec, pl.BlockSpec((1, 1, HEAD_DIM, sp), lambda g, b, n: (b, g, 0, 0)), kvspec, tok, tok,
                  pl.BlockSpec((1, 1, nbs, 8, r), lambda g, b, n: (b, g, n, 0, 0)),
                  pl.BlockSpec((3, 1, SPAN, r), lambda g, b, n: (0, g, 0, 0)),
                  pl.BlockSpec(memory_space=pltpu.SMEM)],
        out_specs=[qspec, kvout, kvout,
                   pl.BlockSpec((1, SPAN, r), lambda g, b, n: (g, 0, 0)),
                   pl.BlockSpec((1, 8, r), lambda g, b, n: (g, 0, 0))],
        out_shape=[SDS(qb.shape, F32), SDS((bl, kv, s_len, HEAD_DIM), F32), SDS((bl, kv, s_len, HEAD_DIM), F32),
                   SDS((kv, SPAN, r), F32), SDS((kv, 8, r), F32)],
        scratch_shapes=[pltpu.VMEM((HEAD_DIM, sp), F32), pltpu.VMEM((HEAD_DIM, sp), F32)],
        compiler_params=_cp("arbitrary", "arbitrary", "arbitrary"),
        name="attn_b_bwd",
    )(qb, kb, kbt, vb, do, o, lse, bias_t, sink)


def _bias_reduce(dsum, dsink, bucket_t4):
    kv, _, r = dsum.shape

    def body(ds_ref, dk_ref, bk_ref, rel_ref, sink_ref):
        lane = lax.broadcasted_iota(jnp.int32, (N_BUCKETS, 128), 1)
        lane8 = lax.broadcasted_iota(jnp.int32, (8, 128), 1)
        bk = bk_ref[...]
        for g in range(kv):
            ds = ds_ref[g]
            rowi = lax.broadcasted_iota(jnp.int32, (N_BUCKETS, r), 0)
            red = jnp.zeros((N_BUCKETS, r), F32)
            for b in range(N_BUCKETS):
                red = jnp.where(rowi == b, jnp.sum(jnp.where(bk == b, ds, 0.0), axis=0, keepdims=True), red)
            out = jnp.zeros((N_BUCKETS, 128), F32)
            so = jnp.zeros((8, 128), F32)
            for h in range(GROUP):
                col = jnp.sum(red[:, h * BLOCK:(h + 1) * BLOCK], axis=1, keepdims=True)
                out = jnp.where(lane == h, col, out)
                sc = jnp.sum(dk_ref[g][:, h * BLOCK:(h + 1) * BLOCK], axis=1, keepdims=True)
                so = jnp.where(lane8 == h, sc, so)
            rel_ref[g] = out
            sink_ref[g] = so

    vm = pl.BlockSpec(memory_space=pltpu.VMEM)
    return pl.pallas_call(
        body,
        in_specs=[vm, vm, vm],
        out_specs=[vm, vm],
        out_shape=[SDS((kv, N_BUCKETS, 128), F32), SDS((kv, 8, 128), F32)],
        name="bias_reduce",
    )(dsum, dsink, bucket_t4)


def _dqkprep(dqa, dka, dva, dqb, dkb, dvb, proj, cos, sin_signed, gq, gk, s_len, ts):
    nd, t, pb = proj.shape
    bl, ha = dqa.shape[0], dqa.shape[1]
    kva, hb, kvb = dka.shape[1], dqb.shape[1], dkb.shape[1]
    hpb = pb // HEAD_DIM
    ns = s_len // ts

    def body(dqa_ref, dka_ref, dva_ref, dqb_ref, dkb_ref, dvb_ref, p_ref, cos_ref, sin_ref, gq_ref, gk_ref,
             dp_ref, dgq_ref, dgk_ref):
        b, i = pl.program_id(0), pl.program_id(1)
        cs, sn = cos_ref[...], sin_ref[...]
        lane = lax.broadcasted_iota(jnp.int32, (ts, HEAD_DIM), 1)
        first = (lane % 32) < 16

        @pl.when((b == 0) & (i == 0))
        def _():
            dgq_ref[...] = jnp.zeros_like(dgq_ref)
            dgk_ref[...] = jnp.zeros_like(dgk_ref)

        def put(hh, val):
            dp_ref[hh // hpb, :, pl.ds((hh % hpb) * HEAD_DIM, HEAD_DIM)] = val.astype(BF16)

        def unrope_norm(d_rot, hh, g, dg_ref):
            dn = _rope(d_rot, cs, sn, first, transpose=True)
            _, n, r = _rms_fwd(_head_slice(p_ref, hh, hpb), g)
            dx, dgt = _rms_bwd(n, r, g, dn)
            dg_ref[0:1, :] += jnp.sum(dgt, axis=0, keepdims=True)
            put(hh, dx)

        for h in range(ha):
            unrope_norm(dqa_ref[0, h] * SCALE, h, gq_ref[...], dgq_ref)
        for h in range(kva):
            unrope_norm(dka_ref[0, h], ha + h, gk_ref[...], dgk_ref)
            put(ha + kva + h, dva_ref[0, h])
        base = ha + 2 * kva
        for h in range(hb):
            put(base + h, dqb_ref[0, h] * SCALE)
        for h in range(kvb):
            put(base + hb + h, dkb_ref[0, h])
            put(base + hb + kvb + h, dvb_ref[0, h])

    def hm(nh):
        return pl.BlockSpec((1, nh, ts, HEAD_DIM), lambda b, i: (b, 0, i, 0))

    vec = pl.BlockSpec((1, HEAD_DIM), lambda b, i: (0, 0))
    tab = pl.BlockSpec((ts, HEAD_DIM), lambda b, i: (i, 0))
    acc = pl.BlockSpec((8, HEAD_DIM), lambda b, i: (0, 0))
    pspec = pl.BlockSpec((nd, ts, pb), lambda b, i: (0, b * ns + i, 0))
    return pl.pallas_call(
        body,
        grid=(bl, ns),
        in_specs=[hm(ha), hm(kva), hm(kva), hm(hb), hm(kvb), hm(kvb), pspec, tab, tab, vec, vec],
        out_specs=[pspec, acc, acc],
        out_shape=[SDS((nd, t, pb), BF16), SDS((8, HEAD_DIM), F32), SDS((8, HEAD_DIM), F32)],
        compiler_params=_cp("arbitrary", "arbitrary"),
        name="dqkprep",
    )(dqa, dka, dva, dqb, dkb, dvb, proj, cos, sin_signed, gq, gk)


def _dx_final(dproj, win_g, x2, dx1, g1, tm, grads):
    t, d = x2.shape
    nd, _, pb = win_g.shape
    ng = len(grads)
    nsteps = t // tm

    def body(dp_ref, w_ref, x_ref, dx1_ref, g_ref, *rest):
        grad_refs, (dx_ref, dg_ref), parts = rest[:ng], rest[ng:ng + 2], rest[ng + 2:2 * ng + 2]
        start, wait = _direct_exchange("scatter", grad_refs, parts, *rest[2 * ng + 2:])

        @pl.when(pl.program_id(0) == 0)
        def _():
            start()
            dg_ref[...] = jnp.zeros_like(dg_ref)

        dh = _dot_nt(dp_ref[0], w_ref[0])
        for j in range(1, nd):
            dh = dh + _dot_nt(dp_ref[j], w_ref[j])
        g = g_ref[...]
        _, n, r = _rms_fwd(x_ref[...], g)
        dx, dgt = _rms_bwd(n, r, g, dh)
        dx_ref[...] = dx1_ref[...] + dx
        dg_ref[0:1, :] += jnp.sum(dgt, axis=0, keepdims=True)
        pl.when(pl.program_id(0) == nsteps - 1)(wait)

    tile = pl.BlockSpec((tm, d), lambda i: (i, 0))
    anyspec = pl.BlockSpec(memory_space=pl.ANY)
    res = pl.pallas_call(
        body,
        grid=(nsteps,),
        in_specs=[pl.BlockSpec((nd, tm, pb), lambda i: (0, i, 0)),
                  pl.BlockSpec((nd, d, pb), lambda i: (0, 0, 0)),
                  tile, tile, pl.BlockSpec((1, d), lambda i: (0, 0))] + [anyspec] * ng,
        out_specs=[tile, pl.BlockSpec((8, d), lambda i: (0, 0))] + [anyspec] * ng,
        out_shape=[SDS((t, d), F32), SDS((8, d), F32)] + [SDS(g.shape, g.dtype) for g in grads],
        scratch_shapes=_exchange_scratch(ng),
        compiler_params=_cp("arbitrary"),
        name="dx_final",
    )(dproj, win_g, x2, dx1, g1, *grads)
    return res[0], res[1], res[2:]


def _adamw_math(w, g, m, v):
    m = ADAM_B1 * m + (1.0 - ADAM_B1) * g
    v = ADAM_B2 * v + (1.0 - ADAM_B2) * (g * g)
    m_hat = m / (1.0 - ADAM_B1 ** ADAM_STEP)
    v_hat = v / (1.0 - ADAM_B2 ** ADAM_STEP)
    delta = -ADAM_LR * (m_hat / (jnp.sqrt(v_hat) + ADAM_EPS) + ADAM_WD * w)
    return delta, m, v


def _adamw_sum(parts, w, m, v, tr, name):
    rows, cols = w.shape

    def body(p_ref, w_ref, m_ref, v_ref, g_ref, d_ref, nm_ref, nv_ref):
        g = p_ref[0].astype(F32)
        for s in range(1, N_DEV):
            g = g + p_ref[s].astype(F32)
        g_ref[...] = g
        d_ref[...], nm_ref[...], nv_ref[...] = _adamw_math(w_ref[...], g, m_ref[...], v_ref[...])

    tr = min(tr, rows)
    tile = pl.BlockSpec((tr, cols), lambda i: (i, 0))
    return pl.pallas_call(
        body,
        grid=(rows // tr,),
        in_specs=[pl.BlockSpec((N_DEV, tr, cols), lambda i: (0, i, 0)), tile, tile, tile],
        out_specs=[tile] * 4,
        out_shape=[SDS((rows, cols), F32)] * 4,
        compiler_params=_cp("parallel"),
        name=name,
    )(parts, w, m, v)


def _adamw_small(vec, rel, ws, ms, vs):
    hb = ws[6].shape[1]
    n = len(ws)

    def body(vec_ref, rel_ref, *rest):
        w_refs, m_refs, v_refs = rest[:n], rest[n:2 * n], rest[2 * n:3 * n]
        loss_ref, outs = rest[3 * n], rest[3 * n + 1:]
        grads = [vec_ref[0:1, :], vec_ref[1:2, :], vec_ref[2:3, :], vec_ref[3:4, :],
                 vec_ref[4:5, 0:HEAD_DIM], vec_ref[4:5, SMALL_LANES:SMALL_LANES + HEAD_DIM],
                 vec_ref[4:5, 2 * SMALL_LANES:2 * SMALL_LANES + hb], rel_ref[:, 0:hb]]
        loss_ref[...] = vec_ref[4:5, 3 * SMALL_LANES:3 * SMALL_LANES + 1]
        for p, g in enumerate(grads):
            g_ref, d_ref, nm_ref, nv_ref = outs[4 * p:4 * p + 4]
            g_ref[...] = g
            d_ref[...], nm_ref[...], nv_ref[...] = _adamw_math(w_refs[p][...], g, m_refs[p][...], v_refs[p][...])

    vm = pl.BlockSpec(memory_space=pltpu.VMEM)
    res = pl.pallas_call(
        body,
        in_specs=[vm] * (2 + 3 * n),
        out_specs=[vm] * (1 + 4 * n),
        out_shape=[SDS((1, 1), F32)] + [SDS(w.shape, F32) for w in ws for _ in range(4)],
        name="adamw_small",
    )(vec, rel, *ws, *ms, *vs)
    return res[0], [res[1 + 4 * p:5 + 4 * p] for p in range(n)]


def _local_step(x, loss_target, win_g, wo_s, wup_s, wdn_s, g_pre_mix, g_post_mix, q_norm_a, k_norm_a, sink_b,
                rel_bias, g_pre_ffn, g_post_ffn):
    bl, s_len, d = x.shape
    t = bl * s_len
    nh = d // HEAD_DIM
    ha = nh // 2
    kva = ha // GROUP
    hb = nh - ha
    kvb = hb // GROUP
    tm = 512
    tw = min(4096, t)
    ts = min(512, s_len)
    tq, tk = 2 * BLOCK, min(512, s_len // 2)

    x2 = x.reshape(t, d)
    tg2 = loss_target.reshape(t, d)
    cos, sin_signed = _rope_tables(s_len)
    a = jnp.arange(BLOCK, dtype=jnp.int32)
    c = jnp.arange(SPAN, dtype=jnp.int32)
    bucket_t = _t5_bucket(c[:, None] - BLOCK - a[None, :])
    bucket_t4 = jnp.tile(bucket_t, (1, GROUP))

    h1, proj = _inproj(x2, g_pre_mix, win_g, tm)
    qa, ka, kat, va, vat, qb, kb, kbt, vb, vbt = _qkprep(
        proj, cos, sin_signed, q_norm_a, k_norm_a, bl, s_len, ha, kva, hb, kvb, ts)
    bias_t = _bias_build(bucket_t, rel_bias, hb)
    oa, lse_a, (wo_g, wup_g, wdn_g) = _attn_a_fwd(qa, ka, vat, tq, tk, [wo_s, wup_s, wdn_s])
    wo = wo_g.reshape(-1, d)
    wdn = wdn_g.reshape(-1, d)
    ob, lse_b = _attn_b_fwd(qb, kb, vbt, bias_t, sink_b, s_len)
    mix, x1, h2 = _mixout(oa, ob, wo, x2, g_post_mix, g_pre_ffn, tm)
    u, df, dy, dg4, loss8 = _ffn_fwd(h2, wup_g, wdn, x1, tg2, g_post_ffn, tm, FFN_BLOCKS_PER_STEP)

    dpre, dx1, dmix, dg3, dg2 = _ffn_bwd(df, u, wdn, wup_g, x1, dy, mix, g_pre_ffn, g_post_mix, tm,
                                         FFN_BLOCKS_PER_STEP)
    gw_dn = _wgrad_rows_squared(u, df, N_DEV, tw, "wgrad_down")
    gw_up = _wgrad_cols(h2, dpre, N_DEV, tw, "wgrad_up")
    gw_o = _wgrad_o(oa, ob, dmix, N_DEV, min(2048, t))
    doa, dob = _attn_out_bwd(dmix, wo, oa.shape[1], tm)
    dqa, dka, dva, (p_o, p_up, p_dn) = _attn_a_bwd(qa, ka, kat, va, doa, oa, lse_a, tq, tk, [gw_o, gw_up, gw_dn])
    dqb, dkb, dvb, dsum, dsink = _attn_b_bwd(qb, kb, kbt, vb, dob, ob, lse_b, bias_t, sink_b, s_len)
    drel_g, dsink_g = _bias_reduce(dsum, dsink, bucket_t4)
    dproj, dgq, dgk = _dqkprep(dqa, dka, dva, dqb, dkb, dvb, proj, cos, sin_signed, q_norm_a, k_norm_a, s_len, ts)
    gw_in = _wgrad_cols_blocked(h1, dproj, tw, "wgrad_in")
    grad_x, dg1, (p_in,) = _dx_final(dproj, win_g, x2, dx1, g_pre_mix, tm, [gw_in])

    vec, rel = _small_allreduce([dg1, dg2, dg3, dg4], dgq, dgk, dsink_g, drel_g, loss8)
    return grad_x.reshape(bl, s_len, d), p_in, p_o, p_up, p_dn, vec, rel


def kernel(x, w_in, w_o, g_pre_mix, g_post_mix, q_norm_a, k_norm_a, sink_b, rel_bias, g_pre_ffn, w_ffn_up, w_ffn_down, g_post_ffn, loss_target, m_w_in, m_w_o, m_g_pre_mix, m_g_post_mix, m_q_norm_a, m_k_norm_a, m_sink_b, m_rel_bias, m_g_pre_ffn, m_w_ffn_up, m_w_ffn_down, m_g_post_ffn, v_w_in, v_w_o, v_g_pre_mix, v_g_post_mix, v_q_norm_a, v_k_norm_a, v_sink_b, v_rel_bias, v_g_pre_ffn, v_w_ffn_up, v_w_ffn_down, v_g_post_ffn):
    (win_g,) = _weight_gather([w_in[0].astype(BF16)])

    grad_x, p_in, p_o, p_up, p_dn, vec, rel = _local_step(
        x, loss_target, win_g, w_o[0].astype(BF16), w_ffn_up[0].astype(BF16), w_ffn_down[0].astype(BF16),
        g_pre_mix, g_post_mix, q_norm_a, k_norm_a, sink_b, rel_bias, g_pre_ffn, g_post_ffn)

    big = {
        "w_in": _adamw_sum(p_in, w_in[0], m_w_in[0], v_w_in[0], 256, "adamw_in"),
        "w_o": _adamw_sum(p_o, w_o[0], m_w_o[0], v_w_o[0], 128, "adamw_o"),
        "w_up": _adamw_sum(p_up, w_ffn_up[0], m_w_ffn_up[0], v_w_ffn_up[0], 256, "adamw_up"),
        "w_dn": _adamw_sum(p_dn, w_ffn_down[0], m_w_ffn_down[0], v_w_ffn_down[0], 256, "adamw_down"),
    }
    loss, small = _adamw_small(
        vec, rel,
        [g_pre_mix, g_post_mix, g_pre_ffn, g_post_ffn, q_norm_a, k_norm_a, sink_b, rel_bias],
        [m_g_pre_mix, m_g_post_mix, m_g_pre_ffn, m_g_post_ffn, m_q_norm_a, m_k_norm_a, m_sink_b, m_rel_bias],
        [v_g_pre_mix, v_g_post_mix, v_g_pre_ffn, v_g_post_ffn, v_q_norm_a, v_k_norm_a, v_sink_b, v_rel_bias])
    s_pre_mix, s_post_mix, s_pre_ffn, s_post_ffn, s_qn, s_kn, s_sink, s_rel = small

    def outs(kind):
        return [big["w_in"][kind][None], big["w_o"][kind][None], s_pre_mix[kind], s_post_mix[kind], s_qn[kind],
                s_kn[kind], s_sink[kind], s_rel[kind], s_pre_ffn[kind], big["w_up"][kind][None],
                big["w_dn"][kind][None], s_post_ffn[kind]]

    return (loss.reshape(()), grad_x, *outs(0), *outs(1), *outs(2), *outs(3))
```

```python
import functools

import jax
import jax.numpy as jnp
import numpy as np
from jax import lax
from jax.experimental import pallas as pl
from jax.experimental.pallas import tpu as pltpu

F32 = jnp.float32
BF16 = jnp.bfloat16
SDS = jax.ShapeDtypeStruct

N_DEV = 8
HEAD_DIM = 64
GROUP = 4
BLOCK = 128
SPAN = 3 * BLOCK
GRID_W = 64
N_BUCKETS = 32
MAX_DISTANCE = 128
ROPE_THETA = 10000.0
EPS = 1e-6
NEG_INF = -1e30
SCALE = HEAD_DIM ** -0.5
VT_PAD = 16

ADAM_LR = 0.001
ADAM_B1 = 0.9
ADAM_B2 = 0.999
ADAM_EPS = 1e-08
ADAM_WD = 0.01
ADAM_STEP = 10

VMEM_LIMIT = 56 * 1024 * 1024
MESH = pl.DeviceIdType.MESH


def _cp(*sem):
    return pltpu.CompilerParams(dimension_semantics=sem, vmem_limit_bytes=VMEM_LIMIT)


def _dot(a, b):
    return jnp.dot(a, b, preferred_element_type=F32)


def _dot_nt(a, b):
    return lax.dot_general(a, b, (((1,), (1,)), ((), ())), preferred_element_type=F32)


def _dot_tn(a, b):
    return lax.dot_general(a, b, (((0,), (0,)), ((), ())), preferred_element_type=F32)


def _rms_fwd(x, g):
    r = lax.rsqrt(jnp.mean(x * x, axis=-1, keepdims=True) + EPS)
    n = x * r
    return n * g, n, r


def _rms_bwd(n, r, g, dy):
    gd = g * dy
    dx = r * (gd - n * jnp.mean(n * gd, axis=-1, keepdims=True))
    return dx, dy * n


def _rope_tables(s_len):
    rows = s_len // GRID_W
    row = np.repeat(np.arange(rows, dtype=np.int32), GRID_W)
    col = np.tile(np.arange(GRID_W, dtype=np.int32), rows)
    nf = HEAD_DIM // 4
    freqs = np.float32(ROPE_THETA) ** (-np.arange(nf, dtype=np.float32) / np.float32(nf))
    ang_r = row.astype(np.float32)[:, None] * freqs[None, :]
    ang_c = col.astype(np.float32)[:, None] * freqs[None, :]
    cr, sr, cc, sc = np.cos(ang_r), np.sin(ang_r), np.cos(ang_c), np.sin(ang_c)
    cos = np.concatenate([cr, cr, cc, cc] * 2, axis=-1).astype(np.float32)
    sin_signed = np.concatenate([-sr, sr, -sc, sc] * 2, axis=-1).astype(np.float32)
    return jnp.asarray(cos), jnp.asarray(sin_signed)


def _t5_bucket(rel):
    nb = N_BUCKETS // 2
    ret = (rel > 0).astype(jnp.int32) * nb
    n = jnp.abs(rel)
    max_exact = nb // 2
    nf = jnp.maximum(n, 1).astype(F32)
    large = max_exact + (jnp.log(nf / max_exact) / np.float32(np.log(MAX_DISTANCE / max_exact))
                         * (nb - max_exact)).astype(jnp.int32)
    large = jnp.minimum(large, nb - 1)
    return ret + jnp.where(n < max_exact, n, large)


def _mesh_pos():
    return lax.axis_index("x"), lax.axis_index("y"), lax.axis_index("c")


def _lin(p):
    return 4 * p[0] + 2 * p[1] + p[2]


def _weight_gather(shards):
    n = len(shards)

    def body(*refs):
        xs, outs = refs[:n], refs[n:2 * n]
        send_sems, recv_sems, local_sems = refs[2 * n:]
        x, y, c = _mesh_pos()
        me, sibling = (x, y, c), (x, y, 1 - c)
        chips = [(1 - x, y), (x, 1 - y), (1 - x, 1 - y)]

        def copy(a, k, block, to, src=None):
            slot = outs[a].at[_lin(block)]
            return pltpu.make_async_remote_copy(
                src_ref=slot if src is None else src, dst_ref=slot,
                send_sem=send_sems.at[a, k], recv_sem=recv_sems.at[a, k],
                device_id=to, device_id_type=MESH)

        started = []
        for a in range(n):
            mine = pltpu.make_async_copy(xs[a], outs[a].at[_lin(me)], local_sems.at[a])
            mine.start()
            started.append(mine)
        sends = []
        for a in range(n):
            first = [copy(a, 0, me, sibling, src=xs[a])]
            first += [copy(a, 1 + j, me, (*chip, c), src=xs[a]) for j, chip in enumerate(chips)]
            for cp in first:
                cp.start()
            sends += first
        for a in range(n):
            for j, chip in enumerate(chips):
                copy(a, 1 + j, (*chip, c), me).wait_recv()
                fwd = copy(a, 4 + j, (*chip, c), sibling)
                fwd.start()
                sends.append(fwd)
        for a in range(n):
            copy(a, 0, sibling, me).wait_recv()
            for j, chip in enumerate(chips):
                copy(a, 4 + j, (*chip, 1 - c), me).wait_recv()
        for cp in sends:
            cp.wait_send()
        for mine in started:
            mine.wait()

    anyspec = pl.BlockSpec(memory_space=pl.ANY)
    return pl.pallas_call(
        body,
        out_shape=[SDS((N_DEV,) + s.shape, s.dtype) for s in shards],
        in_specs=[anyspec] * n,
        out_specs=[anyspec] * n,
        scratch_shapes=[pltpu.SemaphoreType.DMA((n, 7)), pltpu.SemaphoreType.DMA((n, 7)),
                        pltpu.SemaphoreType.DMA((n,))],
        name="weight_gather",
    )(*shards)


def _direct_exchange(kind, ins, outs, send_sems, recv_sems, local_sems):
    x, y, c = _mesh_pos()
    me = (x, y, c)
    peers = [(x, y, 1 - c), (1 - x, y, c), (x, 1 - y, c), (1 - x, 1 - y, c),
             (1 - x, y, 1 - c), (x, 1 - y, 1 - c), (1 - x, 1 - y, 1 - c)]

    def src(a, to):
        return ins[a] if kind == "gather" else ins[a].at[_lin(to)]

    def remote(a, k, to, frm):
        return pltpu.make_async_remote_copy(
            src_ref=src(a, to), dst_ref=outs[a].at[_lin(frm)],
            send_sem=send_sems.at[a, k], recv_sem=recv_sems.at[a, k],
            device_id=to, device_id_type=MESH)

    n = len(ins)
    sends = [remote(a, k, p, me) for a in range(n) for k, p in enumerate(peers)]
    arrivals = [remote(a, k, p, p) for a in range(n) for k, p in enumerate(peers)]
    local = [pltpu.make_async_copy(src(a, me), outs[a].at[_lin(me)], local_sems.at[a]) for a in range(n)]

    def start():
        for cp in local + sends:
            cp.start()

    def wait():
        for cp in arrivals:
            cp.wait_recv()
        for cp in sends:
            cp.wait_send()
        for cp in local:
            cp.wait()

    return start, wait


def _exchange_scratch(n):
    return [pltpu.SemaphoreType.DMA((n, 7)), pltpu.SemaphoreType.DMA((n, 7)), pltpu.SemaphoreType.DMA((n,))]


SMALL_LANES = 128


def _small_allreduce(dg_rows, dgq, dgk, dsink_g, drel_g, loss8):
    d = dg_rows[0].shape[1]
    kv = dsink_g.shape[0]

    def body(g1_ref, g2_ref, g3_ref, g4_ref, gq_ref, gk_ref, sk_ref, rl_ref, ls_ref, vec_ref, rel_ref,
             vbuf, rbuf, vland, rland, send_sems, recv_sems):
        x, y, c = _mesh_pos()
        me = (x, y, c)
        peers = [(x, y, 1 - c), (1 - x, y, c), (x, 1 - y, c), (1 - x, 1 - y, c),
                 (1 - x, y, 1 - c), (x, 1 - y, 1 - c), (1 - x, 1 - y, 1 - c)]
        vbuf[...] = jnp.zeros_like(vbuf)
        rbuf[...] = jnp.zeros_like(rbuf)
        for row, ref in enumerate((g1_ref, g2_ref, g3_ref, g4_ref)):
            vbuf[row:row + 1, :] = ref[0:1, :]
        vbuf[4:5, 0:HEAD_DIM] = gq_ref[0:1, 0:HEAD_DIM] + gq_ref[0:1, HEAD_DIM:PAIR]
        vbuf[4:5, SMALL_LANES:SMALL_LANES + HEAD_DIM] = gk_ref[0:1, 0:HEAD_DIM] + gk_ref[0:1, HEAD_DIM:PAIR]
        for g in range(kv):
            vbuf[4:5, 2 * SMALL_LANES + g * GROUP:2 * SMALL_LANES + (g + 1) * GROUP] = sk_ref[g, 0:1, 0:GROUP]
            rbuf[:, g * GROUP:(g + 1) * GROUP] = rl_ref[g, :, 0:GROUP]
        vbuf[4:5, 3 * SMALL_LANES:3 * SMALL_LANES + 1] = ls_ref[0:1, 0:1]

        def copies(k, to, frm):
            return [pltpu.make_async_remote_copy(
                src_ref=buf, dst_ref=land.at[_lin(frm)], send_sem=send_sems.at[a, k], recv_sem=recv_sems.at[a, k],
                device_id=to, device_id_type=MESH) for a, (buf, land) in enumerate(((vbuf, vland), (rbuf, rland)))]

        sends = [cp for k, p in enumerate(peers) for cp in copies(k, p, me)]
        for cp in sends:
            cp.start()
        vland[_lin(me)] = vbuf[...]
        rland[_lin(me)] = rbuf[...]
        for k, p in enumerate(peers):
            for cp in copies(k, p, p):
                cp.wait_recv()
        for cp in sends:
            cp.wait_send()
        vacc, racc = vland[0], rland[0]
        for s in range(1, N_DEV):
            vacc, racc = vacc + vland[s], racc + rland[s]
        vec_ref[...] = vacc
        rel_ref[...] = racc

    vm = pl.BlockSpec(memory_space=pltpu.VMEM)
    return pl.pallas_call(
        body,
        out_shape=[SDS((8, d), F32), SDS((N_BUCKETS, 128), F32)],
        in_specs=[vm] * 9,
        out_specs=[vm, vm],
        scratch_shapes=[pltpu.VMEM((8, d), F32), pltpu.VMEM((N_BUCKETS, 128), F32),
                        pltpu.VMEM((N_DEV, 8, d), F32), pltpu.VMEM((N_DEV, N_BUCKETS, 128), F32),
                        pltpu.SemaphoreType.DMA((2, 7)), pltpu.SemaphoreType.DMA((2, 7))],
        name="small_allreduce",
    )(*dg_rows, dgq, dgk, dsink_g, drel_g, loss8)


def _inproj(x2, g1, w, tm):
    t, d = x2.shape
    p = w.shape[1]

    def body(x_ref, g_ref, w_ref, h_ref, p_ref):
        y, _, _ = _rms_fwd(x_ref[...], g_ref[...])
        h = y.astype(BF16)
        h_ref[...] = h
        p_ref[...] = _dot(h, w_ref[...])

    return pl.pallas_call(
        body,
        grid=(t // tm,),
        in_specs=[pl.BlockSpec((tm, d), lambda i: (i, 0)),
                  pl.BlockSpec((1, d), lambda i: (0, 0)),
                  pl.BlockSpec((d, p), lambda i: (0, 0))],
        out_specs=[pl.BlockSpec((tm, d), lambda i: (i, 0)),
                   pl.BlockSpec((tm, p), lambda i: (i, 0))],
        out_shape=[SDS((t, d), BF16), SDS((t, p), F32)],
        compiler_params=_cp("parallel"),
        name="inproj",
    )(x2, g1, w)


PAIR = 2 * HEAD_DIM


def _pair_masks(ts):
    lane = lax.broadcasted_iota(jnp.int32, (ts, PAIR), 1)
    return lane < HEAD_DIM, (lane % 32) < 16


def _pair_mean(v, low):
    lo = jnp.sum(jnp.where(low, v, 0.0), axis=1, keepdims=True)
    hi = jnp.sum(jnp.where(low, 0.0, v), axis=1, keepdims=True)
    return jnp.where(low, lo, hi) * (1.0 / HEAD_DIM)


def _pair_partner(v, first):
    return jnp.where(first, pltpu.roll(v, PAIR - 16, 1), pltpu.roll(v, 16, 1))


def _qkprep(proj, cos, sin_signed, gq, gk, bl, s_len, ha, kva, hb, kvb, ts):
    t, p_cols = proj.shape
    assert ha % 2 == 0 and kva % 2 == 0 and hb % 2 == 0 and kvb % 2 == 0
    ns = s_len // ts
    sp = s_len + 2 * BLOCK

    def body(p_ref, cos_ref, sin_ref, gq_ref, gk_ref, qa_ref, ka_ref, kat_ref, va_ref, vat_ref, qb_ref, kb_ref,
             kbt_ref, vb_ref, vbt_ref):
        i = pl.program_id(1)
        cs, sn = cos_ref[...], sin_ref[...]
        low, first = _pair_masks(ts)
        ones_row = (lax.broadcasted_iota(jnp.int32, (VT_PAD, ts), 0) == 0).astype(BF16)
        heads = (slice(0, HEAD_DIM), slice(HEAD_DIM, PAIR))

        def pair(p):
            return p_ref[:, p * PAIR:(p + 1) * PAIR]

        def normrope(x, g):
            y = x * lax.rsqrt(_pair_mean(x * x, low) + EPS) * g
            return y * cs + _pair_partner(y, first) * sn

        def transposed(xb):
            return xb.astype(F32).T.astype(BF16)

        for p in range(ha // 2):
            q = (normrope(pair(p), gq_ref[...]) * SCALE).astype(BF16)
            for e, lanes in enumerate(heads):
                qa_ref[0, 2 * p + e] = q[:, lanes]
        base = ha // 2
        for p in range(kva // 2):
            k = normrope(pair(base + p), gk_ref[...]).astype(BF16)
            v = pair(base + kva // 2 + p).astype(BF16)
            kt, vt = transposed(k), transposed(v)
            for e, lanes in enumerate(heads):
                ka_ref[0, 2 * p + e] = k[:, lanes]
                va_ref[0, 2 * p + e] = v[:, lanes]
                kat_ref[0, 2 * p + e] = kt[lanes, :]
                vat_ref[0, 2 * p + e, 0:HEAD_DIM, :] = vt[lanes, :]
                vat_ref[0, 2 * p + e, HEAD_DIM:HEAD_DIM + VT_PAD, :] = ones_row
        base += kva
        for p in range(hb // 2):
            q = (pair(base + p) * SCALE).astype(BF16)
            for e, lanes in enumerate(heads):
                qb_ref[0, 2 * p + e] = q[:, lanes]
        base += hb // 2

        @pl.when(i == 0)
        def _():
            zeros = jnp.zeros((kvb, BLOCK, HEAD_DIM), BF16)
            zeros_t = jnp.zeros((kvb, HEAD_DIM + VT_PAD, BLOCK), BF16)
            for ref in (kb_ref, vb_ref):
                ref[0, :, 0:BLOCK, :] = zeros
                ref[0, :, sp - BLOCK:sp, :] = zeros
            kbt_ref[0, :, :, 0:BLOCK] = zeros_t[:, 0:HEAD_DIM]
            kbt_ref[0, :, :, sp - BLOCK:sp] = zeros_t[:, 0:HEAD_DIM]
            vbt_ref[0, :, :, 0:BLOCK] = zeros_t
            vbt_ref[0, :, :, sp - BLOCK:sp] = zeros_t

        rows = pl.ds(pl.multiple_of(BLOCK + i * ts, BLOCK), ts)
        for p in range(kvb // 2):
            k = pair(base + p).astype(BF16)
            v = pair(base + kvb // 2 + p).astype(BF16)
            kt, vt = transposed(k), transposed(v)
            for e, lanes in enumerate(heads):
                kb_ref[0, 2 * p + e, rows, :] = k[:, lanes]
                vb_ref[0, 2 * p + e, rows, :] = v[:, lanes]
                kbt_ref[0, 2 * p + e, :, rows] = kt[lanes, :]
                vbt_ref[0, 2 * p + e, 0:HEAD_DIM, rows] = vt[lanes, :]
                vbt_ref[0, 2 * p + e, HEAD_DIM:HEAD_DIM + VT_PAD, rows] = ones_row

    def hm(nh):
        return pl.BlockSpec((1, nh, ts, HEAD_DIM), lambda b, i: (b, 0, i, 0))

    def padded(nh):
        return pl.BlockSpec((1, nh, sp, HEAD_DIM), lambda b, i: (b, 0, 0, 0))

    def padded_t(nh, rows):
        return pl.BlockSpec((1, nh, rows, sp), lambda b, i: (b, 0, 0, 0))

    return pl.pallas_call(
        body,
        grid=(bl, ns),
        in_specs=[pl.BlockSpec((ts, p_cols), lambda b, i: (b * ns + i, 0)),
                  pl.BlockSpec((ts, PAIR), lambda b, i: (i, 0)),
                  pl.BlockSpec((ts, PAIR), lambda b, i: (i, 0)),
                  pl.BlockSpec((1, PAIR), lambda b, i: (0, 0)),
                  pl.BlockSpec((1, PAIR), lambda b, i: (0, 0))],
        out_specs=[hm(ha), hm(kva), pl.BlockSpec((1, kva, HEAD_DIM, ts), lambda b, i: (b, 0, 0, i)), hm(kva),
                   pl.BlockSpec((1, kva, HEAD_DIM + VT_PAD, ts), lambda b, i: (b, 0, 0, i)),
                   hm(hb), padded(kvb), padded_t(kvb, HEAD_DIM), padded(kvb), padded_t(kvb, HEAD_DIM + VT_PAD)],
        out_shape=[SDS((bl, ha, s_len, HEAD_DIM), BF16), SDS((bl, kva, s_len, HEAD_DIM), BF16),
                   SDS((bl, kva, HEAD_DIM, s_len), BF16),
                   SDS((bl, kva, s_len, HEAD_DIM), BF16), SDS((bl, kva, HEAD_DIM + VT_PAD, s_len), BF16),
                   SDS((bl, hb, s_len, HEAD_DIM), BF16),
                   SDS((bl, kvb, sp, HEAD_DIM), BF16), SDS((bl, kvb, HEAD_DIM, sp), BF16),
                   SDS((bl, kvb, sp, HEAD_DIM), BF16), SDS((bl, kvb, HEAD_DIM + VT_PAD, sp), BF16)],
        compiler_params=_cp("parallel", "arbitrary"),
        name="qkprep",
    )(proj, cos, sin_signed, gq, gk)


def _bias_build(bucket_t, rel_bias, hb):
    kvb = hb // GROUP

    def body(bkt_ref, tbl_ref, out_ref):
        bkt = bkt_ref[...]
        ci = lax.broadcasted_iota(jnp.int32, (SPAN, BLOCK), 0)
        qi = lax.broadcasted_iota(jnp.int32, (SPAN, BLOCK), 1)
        band = jnp.abs(ci - BLOCK - qi) <= BLOCK
        masks = (band, band & (ci >= BLOCK), band & (ci < 2 * BLOCK))
        for h in range(hb):
            acct = jnp.zeros((SPAN, BLOCK), F32)
            for b in range(N_BUCKETS):
                acct = jnp.where(bkt == b, tbl_ref[b, h], acct)
            lanes = slice((h % GROUP) * BLOCK, (h % GROUP + 1) * BLOCK)
            for var, mask in enumerate(masks):
                out_ref[var, h // GROUP, :, lanes] = jnp.where(mask, acct, NEG_INF)

    vm = pl.BlockSpec(memory_space=pltpu.VMEM)
    return pl.pallas_call(
        body,
        in_specs=[vm, pl.BlockSpec(memory_space=pltpu.SMEM)],
        out_specs=vm,
        out_shape=SDS((3, kvb, SPAN, GROUP * BLOCK), F32),
        name="bias_build",
    )(bucket_t, rel_bias)


def _attn_a_fwd(qa, ka, vat, tq, tk, shards):
    bl, ha, s_len, _ = qa.shape
    kv = ka.shape[1]
    va_rows = vat.shape[2]
    nq, nk = s_len // tq, s_len // tk
    assert nk % 2 == 0
    r = GROUP * tq
    ns = len(shards)

    def body(q_ref, k_ref, v_ref, *rest):
        shard_refs, (o_ref, l_ref), gathered = rest[:ns], rest[ns:ns + 2], rest[ns + 2:2 * ns + 2]
        st_sc, send_sems, recv_sems, local_sems = rest[2 * ns + 2:]
        step_id = (pl.program_id(0) * kv + pl.program_id(1)) * nq + pl.program_id(2)
        start, wait = _direct_exchange("gather", shard_refs, gathered, send_sems, recv_sems, local_sems)
        pl.when(step_id == 0)(start)

        q = q_ref[0].reshape(r, HEAD_DIM)

        def scores(c):
            return _dot_nt(k_ref[0, 0, pl.ds(pl.multiple_of(c * tk, tk), tk), :], q)

        def fold(st, c, carry):
            m_old, acc = carry
            m_new = jnp.maximum(m_old, jnp.max(st, axis=0, keepdims=True))
            pt = jnp.exp(st - m_new).astype(BF16)
            vt = v_ref[0, 0, :, pl.ds(pl.multiple_of(c * tk, tk), tk)]
            return m_new, jnp.exp(m_old - m_new) * acc + _dot(vt, pt)

        st_sc[0] = scores(0)

        def step(c2, carry):
            c = 2 * c2
            st_sc[1] = scores(c + 1)
            carry = fold(st_sc[0], c, carry)
            st_sc[0] = scores(jnp.minimum(c + 2, nk - 1))
            return fold(st_sc[1], c + 1, carry)

        m, acc = lax.fori_loop(0, nk // 2, step,
                               (jnp.full((1, r), -jnp.inf, F32), jnp.zeros((va_rows, r), F32)))
        l = acc[HEAD_DIM:HEAD_DIM + 1, :]
        o = (acc[0:HEAD_DIM, :] / l).T
        for h in range(GROUP):
            o_ref[:, h * HEAD_DIM:(h + 1) * HEAD_DIM] = o[h * tq:(h + 1) * tq].astype(BF16)
        l_ref[0, 0, 0] = jnp.broadcast_to(m + jnp.log(l), (8, r))
        pl.when(step_id == bl * kv * nq - 1)(wait)

    anyspec = pl.BlockSpec(memory_space=pl.ANY)
    res = pl.pallas_call(
        body,
        grid=(bl, kv, nq),
        in_specs=[pl.BlockSpec((1, GROUP, tq, HEAD_DIM), lambda b, g, i: (b, g, i, 0)),
                  pl.BlockSpec((1, 1, s_len, HEAD_DIM), lambda b, g, i: (b, g, 0, 0)),
                  pl.BlockSpec((1, 1, va_rows, s_len), lambda b, g, i: (b, g, 0, 0))] + [anyspec] * ns,
        out_specs=[pl.BlockSpec((tq, GROUP * HEAD_DIM), lambda b, g, i: (b * nq + i, g)),
                   pl.BlockSpec((1, 1, 1, 8, r), lambda b, g, i: (b, g, i, 0, 0))] + [anyspec] * ns,
        out_shape=[SDS((bl * s_len, ha * HEAD_DIM), BF16), SDS((bl, kv, nq, 8, r), F32)]
        + [SDS((N_DEV,) + s.shape, s.dtype) for s in shards],
        scratch_shapes=[pltpu.VMEM((2, tk, r), F32)] + _exchange_scratch(ns),
        compiler_params=_cp("arbitrary", "arbitrary", "arbitrary"),
        name="attn_a_fwd",
    )(qa, ka, vat, *shards)
    return res[0], res[1], res[2:]


FFN_BLOCKS_PER_STEP = 4
QB_PER_STEP = 8


def _bias_variant(n, nb):
    return jnp.where(n == 0, 1, jnp.where(n == nb - 1, 2, 0))


def _sink_row(sink_ref, g):
    return jnp.concatenate([jnp.full((1, BLOCK), sink_ref[0, g * GROUP + h], F32) for h in range(GROUP)], axis=1)


def _attn_b_fwd(qb, kb, vbt, bias_t, sink, s_len):
    bl, hb, _, _ = qb.shape
    kv = kb.shape[1]
    sp = kb.shape[2]
    vt_rows = vbt.shape[2]
    nb = s_len // BLOCK
    nbs = min(QB_PER_STEP, nb)
    r = GROUP * BLOCK

    def body(q_ref, k_ref, vt_ref, bt_ref, sink_ref, o_ref, l_ref):
        g, n0 = pl.program_id(1), pl.program_id(2) * nbs
        sink_row = _sink_row(sink_ref, g)
        for j in range(nbs):
            n = n0 + j
            span = pl.ds(pl.multiple_of(n * BLOCK, BLOCK), SPAN)
            q = q_ref[0, :, j * BLOCK:(j + 1) * BLOCK, :].reshape(r, HEAD_DIM)
            st = _dot_nt(k_ref[0, 0, span, :], q) + bt_ref[_bias_variant(n, nb), 0]
            m = jnp.maximum(jnp.max(st, axis=0, keepdims=True), sink_row)
            acc = _dot(vt_ref[0, 0, :, span], jnp.exp(st - m).astype(BF16))
            l = acc[HEAD_DIM:HEAD_DIM + 1, :] + jnp.exp(sink_row - m)
            o = (acc[0:HEAD_DIM, :] / l).T
            for h in range(GROUP):
                o_ref[j * BLOCK:(j + 1) * BLOCK, h * HEAD_DIM:(h + 1) * HEAD_DIM] = (
                    o[h * BLOCK:(h + 1) * BLOCK].astype(BF16))
            l_ref[0, 0, j] = jnp.broadcast_to(m + jnp.log(l), (8, r))

    return pl.pallas_call(
        body,
        grid=(bl, kv, nb // nbs),
        in_specs=[pl.BlockSpec((1, GROUP, nbs * BLOCK, HEAD_DIM), lambda b, g, n: (b, g, n, 0)),
                  pl.BlockSpec((1, 1, sp, HEAD_DIM), lambda b, g, n: (b, g, 0, 0)),
                  pl.BlockSpec((1, 1, vt_rows, sp), lambda b, g, n: (b, g, 0, 0)),
                  pl.BlockSpec((3, 1, SPAN, r), lambda b, g, n: (0, g, 0, 0)),
                  pl.BlockSpec(memory_space=pltpu.SMEM)],
        out_specs=[pl.BlockSpec((nbs * BLOCK, GROUP * HEAD_DIM), lambda b, g, n: (b * (nb // nbs) + n, g)),
                   pl.BlockSpec((1, 1, nbs, 8, r), lambda b, g, n: (b, g, n, 0, 0))],
        out_shape=[SDS((bl * s_len, hb * HEAD_DIM), BF16), SDS((bl, kv, nb, 8, r), F32)],
        compiler_params=_cp("parallel", "parallel", "arbitrary"),
        name="attn_b_fwd",
    )(qb, kb, vbt, bias_t, sink)


def _mixout(oa, ob, wo, x2, g2, g3, tm):
    t, d = x2.shape
    ca = oa.shape[1]

    def body(oa_ref, ob_ref, w_ref, x_ref, g2_ref, g3_ref, mix_ref, x1_ref, h2_ref):
        mix = _dot(oa_ref[...], w_ref[0:ca, :]) + _dot(ob_ref[...], w_ref[ca:, :])
        mix_ref[...] = mix
        y2, _, _ = _rms_fwd(mix, g2_ref[...])
        x1 = x_ref[...] + y2
        x1_ref[...] = x1
        y3, _, _ = _rms_fwd(x1, g3_ref[...])
        h2_ref[...] = y3.astype(BF16)

    tile = lambda w: pl.BlockSpec((tm, w), lambda i: (i, 0))
    vec = pl.BlockSpec((1, d), lambda i: (0, 0))
    return pl.pallas_call(
        body,
        grid=(t // tm,),
        in_specs=[tile(ca), tile(ob.shape[1]), pl.BlockSpec(wo.shape, lambda i: (0, 0)), tile(d), vec, vec],
        out_specs=[tile(d), tile(d), tile(d)],
        out_shape=[SDS((t, d), F32), SDS((t, d), F32), SDS((t, d), BF16)],
        compiler_params=_cp("parallel"),
        name="mixout",
    )(oa, ob, wo, x2, g2, g3)


def _ffn_fwd(h2, wup_g, wdn, x1, target, g4, tm, jb):
    t, d = x1.shape
    nblk, _, tf = wup_g.shape
    ff = nblk * tf
    nt = t // tm
    nj = nblk // jb

    def body(h_ref, wu_ref, wd_ref, x1_ref, tg_ref, g_ref, u_ref, df_ref, dy_ref, dg_ref, loss_ref, acc_sc):
        i, j = pl.program_id(0), pl.program_id(1)

        @pl.when(j == 0)
        def _():
            acc_sc[...] = jnp.zeros_like(acc_sc)

        @pl.when((i == 0) & (j == 0))
        def _():
            dg_ref[...] = jnp.zeros_like(dg_ref)
            loss_ref[...] = jnp.zeros_like(loss_ref)

        h = h_ref[...]
        squares = []
        for s in range(jb):
            u = jnp.maximum(_dot(h, wu_ref[s]), 0.0)
            u_ref[:, s * tf:(s + 1) * tf] = u.astype(BF16)
            squares.append((u * u).astype(BF16))
        acc_sc[...] += _dot(jnp.concatenate(squares, axis=1), wd_ref[...])

        @pl.when(j == nj - 1)
        def _():
            g = g_ref[...]
            y4, n, r = _rms_fwd(acc_sc[...], g)
            e = (x1_ref[...] + y4) - tg_ref[...]
            loss_ref[...] += jnp.sum(e * e) * (0.5 / d)
            dy = e * (1.0 / d)
            dy_ref[...] = dy
            df, dgt = _rms_bwd(n, r, g, dy)
            df_ref[...] = df.astype(BF16)
            dg_ref[0:1, :] += jnp.sum(dgt, axis=0, keepdims=True)

    tile = pl.BlockSpec((tm, d), lambda i, j: (i, 0))
    return pl.pallas_call(
        body,
        grid=(nt, nj),
        in_specs=[tile,
                  pl.BlockSpec((jb, d, tf), lambda i, j: (j, 0, 0)),
                  pl.BlockSpec((jb * tf, d), lambda i, j: (j, 0)),
                  tile, tile,
                  pl.BlockSpec((1, d), lambda i, j: (0, 0))],
        out_specs=[pl.BlockSpec((tm, jb * tf), lambda i, j: (i, j)), tile, tile,
                   pl.BlockSpec((8, d), lambda i, j: (0, 0)),
                   pl.BlockSpec((8, 128), lambda i, j: (0, 0))],
        out_shape=[SDS((t, ff), BF16), SDS((t, d), BF16), SDS((t, d), F32), SDS((8, d), F32), SDS((8, 128), F32)],
        scratch_shapes=[pltpu.VMEM((tm, d), F32)],
        compiler_params=_cp("arbitrary", "arbitrary"),
        name="ffn_fwd",
    )(h2, wup_g, wdn, x1, target, g4)


def _ffn_bwd(df, u, wdn, wup_g, x1, dy, mix, g3, g2, tm, jb):
    t, d = x1.shape
    nblk, _, tf = wup_g.shape
    nt = t // tm
    nj = nblk // jb

    def body(df_ref, u_ref, wd_ref, wu_ref, x1_ref, dy_ref, mix_ref, g3_ref, g2_ref,
             dpre_ref, dx1_ref, dmix_ref, dg3_ref, dg2_ref, acc_sc):
        i, j = pl.program_id(0), pl.program_id(1)

        @pl.when(j == 0)
        def _():
            acc_sc[...] = jnp.zeros_like(acc_sc)

        @pl.when((i == 0) & (j == 0))
        def _():
            dg3_ref[...] = jnp.zeros_like(dg3_ref)
            dg2_ref[...] = jnp.zeros_like(dg2_ref)

        du2 = _dot_nt(df_ref[...], wd_ref[...])
        dpre = (2.0 * u_ref[...].astype(F32) * du2).astype(BF16)
        dpre_ref[...] = dpre
        dh = _dot_nt(dpre[:, 0:tf], wu_ref[0])
        for s in range(1, jb):
            dh = dh + _dot_nt(dpre[:, s * tf:(s + 1) * tf], wu_ref[s])
        acc_sc[...] += dh

        @pl.when(j == nj - 1)
        def _():
            g3, g2 = g3_ref[...], g2_ref[...]
            _, n3, r3 = _rms_fwd(x1_ref[...], g3)
            dx, dgt3 = _rms_bwd(n3, r3, g3, acc_sc[...])
            dx1 = dy_ref[...] + dx
            dx1_ref[...] = dx1
            dg3_ref[0:1, :] += jnp.sum(dgt3, axis=0, keepdims=True)
            _, n2, r2 = _rms_fwd(mix_ref[...], g2)
            dmix, dgt2 = _rms_bwd(n2, r2, g2, dx1)
            dmix_ref[...] = dmix.astype(BF16)
            dg2_ref[0:1, :] += jnp.sum(dgt2, axis=0, keepdims=True)

    tile = pl.BlockSpec((tm, d), lambda i, j: (i, 0))
    vec = pl.BlockSpec((1, d), lambda i, j: (0, 0))
    acc8 = pl.BlockSpec((8, d), lambda i, j: (0, 0))
    return pl.pallas_call(
        body,
        grid=(nt, nj),
        in_specs=[tile,
                  pl.BlockSpec((tm, jb * tf), lambda i, j: (i, j)),
                  pl.BlockSpec((jb * tf, d), lambda i, j: (j, 0)),
                  pl.BlockSpec((jb, d, tf), lambda i, j: (j, 0, 0)),
                  tile, tile, tile, vec, vec],
        out_specs=[pl.BlockSpec((tm, jb * tf), lambda i, j: (i, j)), tile, tile, acc8, acc8],
        out_shape=[SDS(u.shape, BF16), SDS((t, d), F32), SDS((t, d), BF16), SDS((8, d), F32), SDS((8, d), F32)],
        scratch_shapes=[pltpu.VMEM((tm, d), F32)],
        compiler_params=_cp("arbitrary", "arbitrary"),
        name="ffn_bwd",
    )(df, u, wdn, wup_g, x1, dy, mix, g3, g2)


def _wgrad(a, b, a_spec, b_spec, out_block, out_shape, nj, nk, name, prep_a=None, prep_b=None):
    acc_shape = out_block[1:]

    def body(a_ref, b_ref, o_ref, acc_sc):
        k = pl.program_id(1)
        av = a_ref[...] if prep_a is None else prep_a(a_ref)
        bv = b_ref[...] if prep_b is None else prep_b(b_ref)
        part = _dot_tn(av, bv)

        @pl.when(k == 0)
        def _():
            acc_sc[...] = part

        @pl.when(k > 0)
        def _():
            acc_sc[...] += part

        @pl.when(k == nk - 1)
        def _():
            o_ref[0] = acc_sc[...].astype(BF16)

    return pl.pallas_call(
        body,
        grid=(nj, nk),
        in_specs=[a_spec, b_spec],
        out_specs=pl.BlockSpec(out_block, lambda j, k: (j, 0, 0)),
        out_shape=SDS(out_shape, BF16),
        scratch_shapes=[pltpu.VMEM(acc_shape, F32)],
        compiler_params=_cp("parallel", "arbitrary"),
        name=name,
    )(a, b)


def _wgrad_cols(a, b, nj, tt, name):
    t, m = a.shape
    bn = b.shape[1] // nj
    return _wgrad(a, b, pl.BlockSpec((tt, m), lambda j, k: (k, 0)), pl.BlockSpec((tt, bn), lambda j, k: (k, j)),
                  (1, m, bn), (nj, m, bn), nj, t // tt, name)


def _wgrad_rows_squared(a, b, nj, tt, name):
    t, n = b.shape
    bm = a.shape[1] // nj

    def square(a_ref):
        af = a_ref[...].astype(F32)
        return (af * af).astype(BF16)

    return _wgrad(a, b, pl.BlockSpec((tt, bm), lambda j, k: (k, j)), pl.BlockSpec((tt, n), lambda j, k: (k, 0)),
                  (1, bm, n), (nj, bm, n), nj, t // tt, name, prep_a=square)


def _wgrad_o(oa, ob, dmix, nj, tt):
    t, n = dmix.shape
    ca, cb = oa.shape[1], ob.shape[1]
    m = ca + cb
    nk = t // tt

    def body(oa_ref, ob_ref, b_ref, o_ref, acc_sc):
        k = pl.program_id(0)
        part = _dot_tn(jnp.concatenate([oa_ref[...], ob_ref[...]], axis=1), b_ref[...])

        @pl.when(k == 0)
        def _():
            acc_sc[...] = part

        @pl.when(k > 0)
        def _():
            acc_sc[...] += part

        @pl.when(k == nk - 1)
        def _():
            o_ref[...] = acc_sc[...].reshape(nj, m // nj, n).astype(BF16)

    return pl.pallas_call(
        body,
        grid=(nk,),
        in_specs=[pl.BlockSpec((tt, ca), lambda k: (k, 0)), pl.BlockSpec((tt, cb), lambda k: (k, 0)),
                  pl.BlockSpec((tt, n), lambda k: (k, 0))],
        out_specs=pl.BlockSpec((nj, m // nj, n), lambda k: (0, 0, 0)),
        out_shape=SDS((nj, m // nj, n), BF16),
        scratch_shapes=[pltpu.VMEM((m, n), F32)],
        compiler_params=_cp("arbitrary"),
        name="wgrad_o",
    )(oa, ob, dmix)


def _attn_out_bwd(dmix, wo, ca, tm):
    t, d = dmix.shape
    cb = wo.shape[0] - ca

    def body(dm_ref, w_ref, da_ref, db_ref):
        dm = dm_ref[...]
        da_ref[...] = _dot_nt(dm, w_ref[0:ca, :]).astype(BF16)
        db_ref[...] = _dot_nt(dm, w_ref[ca:, :]).astype(BF16)

    return pl.pallas_call(
        body,
        grid=(t // tm,),
        in_specs=[pl.BlockSpec((tm, d), lambda i: (i, 0)), pl.BlockSpec(wo.shape, lambda i: (0, 0))],
        out_specs=[pl.BlockSpec((tm, ca), lambda i: (i, 0)), pl.BlockSpec((tm, cb), lambda i: (i, 0))],
        out_shape=[SDS((t, ca), BF16), SDS((t, cb), BF16)],
        compiler_params=_cp("parallel"),
        name="attn_out_bwd",
    )(dmix, wo)


def _stack_heads(ref, rows):
    return jnp.concatenate([ref[:, h * HEAD_DIM:(h + 1) * HEAD_DIM] for h in range(GROUP)], axis=0)


def _attn_a_bwd(qa, ka, kat, va, do, o, lse, tq, tk, grads):
    bl, ha, s_len, _ = qa.shape
    kv = ka.shape[1]
    nq, nk = s_len // tq, s_len // tk
    assert nk % 2 == 0
    r = GROUP * tq
    ng = len(grads)

    def body(q_ref, k_ref, kt_ref, v_ref, do_ref, o_ref, l_ref, *rest):
        grad_refs, (dq_ref, dk_ref, dv_ref), parts = rest[:ng], rest[ng:ng + 3], rest[ng + 3:2 * ng + 3]
        st_sc, dp_sc, dkt_sc, dvt_sc, send_sems, recv_sems, local_sems = rest[2 * ng + 3:]
        i = pl.program_id(2)
        step_id = (pl.program_id(0) * kv + pl.program_id(1)) * nq + i
        start, wait = _direct_exchange("scatter", grad_refs, parts, send_sems, recv_sems, local_sems)
        pl.when(step_id == 0)(start)

        q = q_ref[0].reshape(r, HEAD_DIM)
        do2 = _stack_heads(do_ref, tq)
        qt = q.astype(F32).T
        dot32 = do2.astype(F32).T
        ot32 = _stack_heads(o_ref, tq).astype(F32).T
        drow = jnp.sum(dot32 * ot32, axis=0, keepdims=True)
        qt, dot = qt.astype(BF16), dot32.astype(BF16)
        lrow = l_ref[0, 0, 0, 0:1, :]

        @pl.when(i == 0)
        def _():
            dkt_sc[...] = jnp.zeros_like(dkt_sc)
            dvt_sc[...] = jnp.zeros_like(dvt_sc)

        def chunk(c):
            return pl.ds(pl.multiple_of(c * tk, tk), tk)

        def scores(c, slot):
            st_sc[slot] = _dot_nt(k_ref[0, 0, chunk(c), :], q)
            dp_sc[slot] = _dot_nt(v_ref[0, 0, chunk(c), :], do2)

        def fold(slot, c, dqt):
            pt = jnp.exp(st_sc[slot] - lrow)
            dsb = (pt * (dp_sc[slot] - drow)).astype(BF16)
            dvt_sc[:, chunk(c)] += _dot_nt(dot, pt.astype(BF16))
            dkt_sc[:, chunk(c)] += _dot_nt(qt, dsb)
            return dqt + _dot(kt_ref[0, 0, :, chunk(c)], dsb)

        scores(0, 0)

        def step(c2, dqt):
            c = 2 * c2
            scores(c + 1, 1)
            dqt = fold(0, c, dqt)
            scores(jnp.minimum(c + 2, nk - 1), 0)
            return fold(1, c + 1, dqt)

        dqt = lax.fori_loop(0, nk // 2, step, jnp.zeros((HEAD_DIM, r), F32))
        dq_ref[0] = dqt.T.reshape(GROUP, tq, HEAD_DIM)

        @pl.when(i == nq - 1)
        def _():
            dk_ref[0, 0] = dkt_sc[...].T
            dv_ref[0, 0] = dvt_sc[...].T

        pl.when(step_id == bl * kv * nq - 1)(wait)

    kvspec = pl.BlockSpec((1, 1, s_len, HEAD_DIM), lambda b, g, i: (b, g, 0, 0))
    qspec = pl.BlockSpec((1, GROUP, tq, HEAD_DIM), lambda b, g, i: (b, g, i, 0))
    tok = pl.BlockSpec((tq, GROUP * HEAD_DIM), lambda b, g, i: (b * nq + i, g))
    anyspec = pl.BlockSpec(memory_space=pl.ANY)
    res = pl.pallas_call(
        body,
        grid=(bl, kv, nq),
        in_specs=[qspec, kvspec, pl.BlockSpec((1, 1, HEAD_DIM, s_len), lambda b, g, i: (b, g, 0, 0)), kvspec,
                  tok, tok, pl.BlockSpec((1, 1, 1, 8, r), lambda b, g, i: (b, g, i, 0, 0))] + [anyspec] * ng,
        out_specs=[qspec, kvspec, kvspec] + [anyspec] * ng,
        out_shape=[SDS(qa.shape, F32), SDS(ka.shape, F32), SDS(va.shape, F32)]
        + [SDS(g.shape, g.dtype) for g in grads],
        scratch_shapes=[pltpu.VMEM((2, tk, r), F32), pltpu.VMEM((2, tk, r), F32),
                        pltpu.VMEM((HEAD_DIM, s_len), F32), pltpu.VMEM((HEAD_DIM, s_len), F32)]
        + _exchange_scratch(ng),
        compiler_params=_cp("arbitrary", "arbitrary", "arbitrary"),
        name="attn_a_bwd",
    )(qa, ka, kat, va, do, o, lse, *grads)
    return res[0], res[1], res[2], res[3:]


def _attn_b_bwd(qb, kb, kbt, vb, do, o, lse, bias_t, sink, s_len):
    bl, hb, _, _ = qb.shape
    kv, sp = kb.shape[1], kb.shape[2]
    nb = s_len // BLOCK
    nbs = min(QB_PER_STEP, nb)
    r = GROUP * BLOCK

    def body(q_ref, k_ref, kt_ref, v_ref, do_ref, o_ref, l_ref, bt_ref, sink_ref,
             dq_ref, dk_ref, dv_ref, dsum_ref, dsink_ref, dkt_sc, dvt_sc):
        g, b, ns = pl.program_id(0), pl.program_id(1), pl.program_id(2)
        sink_row = _sink_row(sink_ref, g)

        @pl.when(ns == 0)
        def _():
            dkt_sc[...] = jnp.zeros_like(dkt_sc)
            dvt_sc[...] = jnp.zeros_like(dvt_sc)

        @pl.when((b == 0) & (ns == 0))
        def _():
            dsum_ref[...] = jnp.zeros_like(dsum_ref)
            dsink_ref[...] = jnp.zeros_like(dsink_ref)

        dsum = jnp.zeros((SPAN, r), F32)
        dsink = jnp.zeros((1, r), F32)
        for j in range(nbs):
            n = ns * nbs + j
            span = pl.ds(pl.multiple_of(n * BLOCK, BLOCK), SPAN)
            rows = slice(j * BLOCK, (j + 1) * BLOCK)
            q = q_ref[0, :, rows, :].reshape(r, HEAD_DIM)
            do2 = jnp.concatenate([do_ref[rows, h * HEAD_DIM:(h + 1) * HEAD_DIM] for h in range(GROUP)], axis=0)
            o2 = jnp.concatenate([o_ref[rows, h * HEAD_DIM:(h + 1) * HEAD_DIM] for h in range(GROUP)], axis=0)
            dot32 = do2.astype(F32).T
            drow = jnp.sum(dot32 * o2.astype(F32).T, axis=0, keepdims=True)
            qt, dot = q.astype(F32).T.astype(BF16), dot32.astype(BF16)
            lrow = l_ref[0, 0, j, 0:1, :]
            st = _dot_nt(k_ref[0, 0, span, :], q) + bt_ref[_bias_variant(n, nb), 0]
            pt = jnp.exp(st - lrow)
            dst = pt * (_dot_nt(v_ref[0, 0, span, :], do2) - drow)
            dsum = dsum + dst
            dsink = dsink - jnp.exp(sink_row - lrow) * drow
            dsb = dst.astype(BF16)
            dvt_sc[:, span] += _dot_nt(dot, pt.astype(BF16))
            dkt_sc[:, span] += _dot_nt(qt, dsb)
            dq_ref[0, :, rows, :] = _dot(kt_ref[0, 0, :, span], dsb).T.reshape(GROUP, BLOCK, HEAD_DIM)
        dsum_ref[0] += dsum
        dsink_ref[0, 0:1, :] += dsink

        @pl.when(ns == nb // nbs - 1)
        def _():
            dk_ref[0, 0] = dkt_sc[:, BLOCK:BLOCK + s_len].T
            dv_ref[0, 0] = dvt_sc[:, BLOCK:BLOCK + s_len].T

    kvspec = pl.BlockSpec((1, 1, sp, HEAD_DIM), lambda g, b, n: (b, g, 0, 0))
    kvout = pl.BlockSpec((1, 1, s_len, HEAD_DIM), lambda g, b, n: (b, g, 0, 0))
    qspec = pl.BlockSpec((1, GROUP, nbs * BLOCK, HEAD_DIM), lambda g, b, n: (b, g, n, 0))
    tok = pl.BlockSpec((nbs * BLOCK, GROUP * HEAD_DIM), lambda g, b, n: (b * (nb // nbs) + n, g))
    return pl.pallas_call(
        body,
        grid=(kv, bl, nb // nbs),
        in_specs=[qspec, kvspec, pl.BlockSpec((1, 1, HEAD_DIM, sp), lambda g, b, n: (b, g, 0, 0)), kvspec, tok, tok,
                  pl.BlockSpec((1, 1, nbs, 8, r), lambda g, b, n: (b, g, n, 0, 0)),
                  pl.BlockSpec((3, 1, SPAN, r), lambda g, b, n: (0, g, 0, 0)),
                  pl.BlockSpec(memory_space=pltpu.SMEM)],
        out_specs=[qspec, kvout, kvout,
                   pl.BlockSpec((1, SPAN, r), lambda g, b, n: (g, 0, 0)),
                   pl.BlockSpec((1, 8, r), lambda g, b, n: (g, 0, 0))],
        out_shape=[SDS(qb.shape, F32), SDS((bl, kv, s_len, HEAD_DIM), F32), SDS((bl, kv, s_len, HEAD_DIM), F32),
                   SDS((kv, SPAN, r), F32), SDS((kv, 8, r), F32)],
        scratch_shapes=[pltpu.VMEM((HEAD_DIM, sp), F32), pltpu.VMEM((HEAD_DIM, sp), F32)],
        compiler_params=_cp("arbitrary", "arbitrary", "arbitrary"),
        name="attn_b_bwd",
    )(qb, kb, kbt, vb, do, o, lse, bias_t, sink)


def _bias_reduce(dsum, dsink, bucket_t4):
    kv, _, r = dsum.shape

    def body(ds_ref, dk_ref, bk_ref, rel_ref, sink_ref):
        lane = lax.broadcasted_iota(jnp.int32, (N_BUCKETS, 128), 1)
        lane8 = lax.broadcasted_iota(jnp.int32, (8, 128), 1)
        bk = bk_ref[...]
        for g in range(kv):
            ds = ds_ref[g]
            rowi = lax.broadcasted_iota(jnp.int32, (N_BUCKETS, r), 0)
            red = jnp.zeros((N_BUCKETS, r), F32)
            for b in range(N_BUCKETS):
                red = jnp.where(rowi == b, jnp.sum(jnp.where(bk == b, ds, 0.0), axis=0, keepdims=True), red)
            out = jnp.zeros((N_BUCKETS, 128), F32)
            so = jnp.zeros((8, 128), F32)
            for h in range(GROUP):
                col = jnp.sum(red[:, h * BLOCK:(h + 1) * BLOCK], axis=1, keepdims=True)
                out = jnp.where(lane == h, col, out)
                sc = jnp.sum(dk_ref[g][:, h * BLOCK:(h + 1) * BLOCK], axis=1, keepdims=True)
                so = jnp.where(lane8 == h, sc, so)
            rel_ref[g] = out
            sink_ref[g] = so

    vm = pl.BlockSpec(memory_space=pltpu.VMEM)
    return pl.pallas_call(
        body,
        in_specs=[vm, vm, vm],
        out_specs=[vm, vm],
        out_shape=[SDS((kv, N_BUCKETS, 128), F32), SDS((kv, 8, 128), F32)],
        name="bias_reduce",
    )(dsum, dsink, bucket_t4)


def _dqkprep(dqa, dka, dva, dqb, dkb, dvb, proj, cos, sin_signed, gq, gk, s_len, ts):
    t, p_cols = proj.shape
    bl, ha = dqa.shape[0], dqa.shape[1]
    kva, hb, kvb = dka.shape[1], dqb.shape[1], dkb.shape[1]
    ns = s_len // ts

    def body(dqa_ref, dka_ref, dva_ref, dqb_ref, dkb_ref, dvb_ref, p_ref, cos_ref, sin_ref, gq_ref, gk_ref,
             dp_ref, dgq_ref, dgk_ref):
        b, i = pl.program_id(0), pl.program_id(1)
        cs, sn = cos_ref[...], sin_ref[...]
        low, first = _pair_masks(ts)

        @pl.when((b == 0) & (i == 0))
        def _():
            dgq_ref[...] = jnp.zeros_like(dgq_ref)
            dgk_ref[...] = jnp.zeros_like(dgk_ref)

        def grad_pair(ref, p):
            return jnp.concatenate([ref[0, 2 * p], ref[0, 2 * p + 1]], axis=1)

        def put(p, val):
            dp_ref[:, p * PAIR:(p + 1) * PAIR] = val.astype(BF16)

        def unrope_norm(d_rot, p, g, dg_ref):
            dn = d_rot * cs + _pair_partner(d_rot * sn, first)
            xp = p_ref[:, p * PAIR:(p + 1) * PAIR]
            r = lax.rsqrt(_pair_mean(xp * xp, low) + EPS)
            n = xp * r
            gd = g * dn
            dg_ref[0:1, :] += jnp.sum(dn * n, axis=0, keepdims=True)
            put(p, r * (gd - n * _pair_mean(n * gd, low)))

        for p in range(ha // 2):
            unrope_norm(grad_pair(dqa_ref, p) * SCALE, p, gq_ref[...], dgq_ref)
        base = ha // 2
        for p in range(kva // 2):
            unrope_norm(grad_pair(dka_ref, p), base + p, gk_ref[...], dgk_ref)
            put(base + kva // 2 + p, grad_pair(dva_ref, p))
        base += kva
        for p in range(hb // 2):
            put(base + p, grad_pair(dqb_ref, p) * SCALE)
        base += hb // 2
        for p in range(kvb // 2):
            put(base + p, grad_pair(dkb_ref, p))
            put(base + kvb // 2 + p, grad_pair(dvb_ref, p))

    def hm(nh):
        return pl.BlockSpec((1, nh, ts, HEAD_DIM), lambda b, i: (b, 0, i, 0))

    vec = pl.BlockSpec((1, PAIR), lambda b, i: (0, 0))
    tab = pl.BlockSpec((ts, PAIR), lambda b, i: (i, 0))
    acc = pl.BlockSpec((8, PAIR), lambda b, i: (0, 0))
    pspec = pl.BlockSpec((ts, p_cols), lambda b, i: (b * ns + i, 0))
    return pl.pallas_call(
        body,
        grid=(bl, ns),
        in_specs=[hm(ha), hm(kva), hm(kva), hm(hb), hm(kvb), hm(kvb), pspec, tab, tab, vec, vec],
        out_specs=[pspec, acc, acc],
        out_shape=[SDS((t, p_cols), BF16), SDS((8, PAIR), F32), SDS((8, PAIR), F32)],
        compiler_params=_cp("arbitrary", "arbitrary"),
        name="dqkprep",
    )(dqa, dka, dva, dqb, dkb, dvb, proj, cos, sin_signed, gq, gk)


def _dx_final(dproj, w, x2, dx1, g1, tm, grads):
    t, d = x2.shape
    p_cols = w.shape[1]
    ng = len(grads)
    nsteps = t // tm

    def body(dp_ref, w_ref, x_ref, dx1_ref, g_ref, *rest):
        grad_refs, (dx_ref, dg_ref), parts = rest[:ng], rest[ng:ng + 2], rest[ng + 2:2 * ng + 2]
        start, wait = _direct_exchange("scatter", grad_refs, parts, *rest[2 * ng + 2:])

        @pl.when(pl.program_id(0) == 0)
        def _():
            start()
            dg_ref[...] = jnp.zeros_like(dg_ref)

        dh = _dot_nt(dp_ref[...], w_ref[...])
        g = g_ref[...]
        _, n, r = _rms_fwd(x_ref[...], g)
        dx, dgt = _rms_bwd(n, r, g, dh)
        dx_ref[...] = dx1_ref[...] + dx
        dg_ref[0:1, :] += jnp.sum(dgt, axis=0, keepdims=True)
        pl.when(pl.program_id(0) == nsteps - 1)(wait)

    tile = pl.BlockSpec((tm, d), lambda i: (i, 0))
    anyspec = pl.BlockSpec(memory_space=pl.ANY)
    res = pl.pallas_call(
        body,
        grid=(nsteps,),
        in_specs=[pl.BlockSpec((tm, p_cols), lambda i: (i, 0)),
                  pl.BlockSpec((d, p_cols), lambda i: (0, 0)),
                  tile, tile, pl.BlockSpec((1, d), lambda i: (0, 0))] + [anyspec] * ng,
        out_specs=[tile, pl.BlockSpec((8, d), lambda i: (0, 0))] + [anyspec] * ng,
        out_shape=[SDS((t, d), F32), SDS((8, d), F32)] + [SDS(g.shape, g.dtype) for g in grads],
        scratch_shapes=_exchange_scratch(ng),
        compiler_params=_cp("arbitrary"),
        name="dx_final",
    )(dproj, w, x2, dx1, g1, *grads)
    return res[0], res[1], res[2:]


def _adamw_math(w, g, m, v):
    m = ADAM_B1 * m + (1.0 - ADAM_B1) * g
    v = ADAM_B2 * v + (1.0 - ADAM_B2) * (g * g)
    m_hat = m / (1.0 - ADAM_B1 ** ADAM_STEP)
    v_hat = v / (1.0 - ADAM_B2 ** ADAM_STEP)
    delta = -ADAM_LR * (m_hat / (jnp.sqrt(v_hat) + ADAM_EPS) + ADAM_WD * w)
    return delta, m, v


def _adamw_sum(parts, w, m, v, tr, name):
    rows, cols = w.shape

    def body(p_ref, w_ref, m_ref, v_ref, g_ref, d_ref, nm_ref, nv_ref):
        g = p_ref[0].astype(F32)
        for s in range(1, N_DEV):
            g = g + p_ref[s].astype(F32)
        g_ref[...] = g
        d_ref[...], nm_ref[...], nv_ref[...] = _adamw_math(w_ref[...], g, m_ref[...], v_ref[...])

    tr = min(tr, rows)
    tile = pl.BlockSpec((tr, cols), lambda i: (i, 0))
    return pl.pallas_call(
        body,
        grid=(rows // tr,),
        in_specs=[pl.BlockSpec((N_DEV, tr, cols), lambda i: (0, i, 0)), tile, tile, tile],
        out_specs=[tile] * 4,
        out_shape=[SDS((rows, cols), F32)] * 4,
        compiler_params=_cp("parallel"),
        name=name,
    )(parts, w, m, v)


def _adamw_small(vec, rel, ws, ms, vs):
    hb = ws[6].shape[1]
    n = len(ws)

    def body(vec_ref, rel_ref, *rest):
        w_refs, m_refs, v_refs = rest[:n], rest[n:2 * n], rest[2 * n:3 * n]
        loss_ref, outs = rest[3 * n], rest[3 * n + 1:]
        grads = [vec_ref[0:1, :], vec_ref[1:2, :], vec_ref[2:3, :], vec_ref[3:4, :],
                 vec_ref[4:5, 0:HEAD_DIM], vec_ref[4:5, SMALL_LANES:SMALL_LANES + HEAD_DIM],
                 vec_ref[4:5, 2 * SMALL_LANES:2 * SMALL_LANES + hb], rel_ref[:, 0:hb]]
        loss_ref[...] = vec_ref[4:5, 3 * SMALL_LANES:3 * SMALL_LANES + 1]
        for p, g in enumerate(grads):
            g_ref, d_ref, nm_ref, nv_ref = outs[4 * p:4 * p + 4]
            g_ref[...] = g
            d_ref[...], nm_ref[...], nv_ref[...] = _adamw_math(w_refs[p][...], g, m_refs[p][...], v_refs[p][...])

    vm = pl.BlockSpec(memory_space=pltpu.VMEM)
    res = pl.pallas_call(
        body,
        in_specs=[vm] * (2 + 3 * n),
        out_specs=[vm] * (1 + 4 * n),
        out_shape=[SDS((1, 1), F32)] + [SDS(w.shape, F32) for w in ws for _ in range(4)],
        name="adamw_small",
    )(vec, rel, *ws, *ms, *vs)
    return res[0], [res[1 + 4 * p:5 + 4 * p] for p in range(n)]


def _local_step(x, loss_target, win_g, wo_s, wup_s, wdn_s, g_pre_mix, g_post_mix, q_norm_a, k_norm_a, sink_b,
                rel_bias, g_pre_ffn, g_post_ffn):
    bl, s_len, d = x.shape
    t = bl * s_len
    nh = d // HEAD_DIM
    ha = nh // 2
    kva = ha // GROUP
    hb = nh - ha
    kvb = hb // GROUP
    tm = 512
    tw = min(4096, t)
    ts = min(512, s_len)
    tq, tk = 2 * BLOCK, min(512, s_len // 2)

    x2 = x.reshape(t, d)
    tg2 = loss_target.reshape(t, d)
    cos, sin_signed = _rope_tables(s_len)
    gq2, gk2 = jnp.tile(q_norm_a, (1, 2)), jnp.tile(k_norm_a, (1, 2))
    a = jnp.arange(BLOCK, dtype=jnp.int32)
    c = jnp.arange(SPAN, dtype=jnp.int32)
    bucket_t = _t5_bucket(c[:, None] - BLOCK - a[None, :])
    bucket_t4 = jnp.tile(bucket_t, (1, GROUP))
    w_in_full = jnp.transpose(win_g, (1, 0, 2)).reshape(d, -1)
    p_cols = w_in_full.shape[1]

    h1, proj = _inproj(x2, g_pre_mix, w_in_full, tm)
    qa, ka, kat, va, vat, qb, kb, kbt, vb, vbt = _qkprep(
        proj, cos, sin_signed, gq2, gk2, bl, s_len, ha, kva, hb, kvb, ts)
    bias_t = _bias_build(bucket_t, rel_bias, hb)
    oa, lse_a, (wo_g, wup_g, wdn_g) = _attn_a_fwd(qa, ka, vat, tq, tk, [wo_s, wup_s, wdn_s])
    wo = wo_g.reshape(-1, d)
    wdn = wdn_g.reshape(-1, d)
    ob, lse_b = _attn_b_fwd(qb, kb, vbt, bias_t, sink_b, s_len)
    mix, x1, h2 = _mixout(oa, ob, wo, x2, g_post_mix, g_pre_ffn, tm)
    u, df, dy, dg4, loss8 = _ffn_fwd(h2, wup_g, wdn, x1, tg2, g_post_ffn, tm, FFN_BLOCKS_PER_STEP)

    dpre, dx1, dmix, dg3, dg2 = _ffn_bwd(df, u, wdn, wup_g, x1, dy, mix, g_pre_ffn, g_post_mix, tm,
                                         FFN_BLOCKS_PER_STEP)
    gw_dn = _wgrad_rows_squared(u, df, N_DEV, tw, "wgrad_down")
    gw_up = _wgrad_cols(h2, dpre, N_DEV, tw, "wgrad_up")
    gw_o = _wgrad_o(oa, ob, dmix, N_DEV, min(2048, t))
    doa, dob = _attn_out_bwd(dmix, wo, oa.shape[1], tm)
    dqa, dka, dva, (p_o, p_up, p_dn) = _attn_a_bwd(qa, ka, kat, va, doa, oa, lse_a, tq, tk, [gw_o, gw_up, gw_dn])
    dqb, dkb, dvb, dsum, dsink = _attn_b_bwd(qb, kb, kbt, vb, dob, ob, lse_b, bias_t, sink_b, s_len)
    drel_g, dsink_g = _bias_reduce(dsum, dsink, bucket_t4)
    dproj, dgq, dgk = _dqkprep(dqa, dka, dva, dqb, dkb, dvb, proj, cos, sin_signed, gq2, gk2, s_len, ts)
    gw_in = _wgrad_cols(h1, dproj, p_cols // 256, tw, "wgrad_in")
    gw_in = jnp.transpose(jnp.transpose(gw_in, (1, 0, 2)).reshape(d, N_DEV, -1), (1, 0, 2))
    grad_x, dg1, (p_in,) = _dx_final(dproj, w_in_full, x2, dx1, g_pre_mix, tm, [gw_in])

    vec, rel = _small_allreduce([dg1, dg2, dg3, dg4], dgq, dgk, dsink_g, drel_g, loss8)
    return grad_x.reshape(bl, s_len, d), p_in, p_o, p_up, p_dn, vec, rel


def kernel(x, w_in, w_o, g_pre_mix, g_post_mix, q_norm_a, k_norm_a, sink_b, rel_bias, g_pre_ffn, w_ffn_up, w_ffn_down, g_post_ffn, loss_target, m_w_in, m_w_o, m_g_pre_mix, m_g_post_mix, m_q_norm_a, m_k_norm_a, m_sink_b, m_rel_bias, m_g_pre_ffn, m_w_ffn_up, m_w_ffn_down, m_g_post_ffn, v_w_in, v_w_o, v_g_pre_mix, v_g_post_mix, v_q_norm_a, v_k_norm_a, v_sink_b, v_rel_bias, v_g_pre_ffn, v_w_ffn_up, v_w_ffn_down, v_g_post_ffn):
    (win_g,) = _weight_gather([w_in[0].astype(BF16)])

    grad_x, p_in, p_o, p_up, p_dn, vec, rel = _local_step(
        x, loss_target, win_g, w_o[0].astype(BF16), w_ffn_up[0].astype(BF16), w_ffn_down[0].astype(BF16),
        g_pre_mix, g_post_mix, q_norm_a, k_norm_a, sink_b, rel_bias, g_pre_ffn, g_post_ffn)

    big = {
        "w_in": _adamw_sum(p_in, w_in[0], m_w_in[0], v_w_in[0], 256, "adamw_in"),
        "w_o": _adamw_sum(p_o, w_o[0], m_w_o[0], v_w_o[0], 128, "adamw_o"),
        "w_up": _adamw_sum(p_up, w_ffn_up[0], m_w_ffn_up[0], v_w_ffn_up[0], 256, "adamw_up"),
        "w_dn": _adamw_sum(p_dn, w_ffn_down[0], m_w_ffn_down[0], v_w_ffn_down[0], 256, "adamw_down"),
    }
    loss, small = _adamw_small(
        vec, rel,
        [g_pre_mix, g_post_mix, g_pre_ffn, g_post_ffn, q_norm_a, k_norm_a, sink_b, rel_bias],
        [m_g_pre_mix, m_g_post_mix, m_g_pre_ffn, m_g_post_ffn, m_q_norm_a, m_k_norm_a, m_sink_b, m_rel_bias],
        [v_g_pre_mix, v_g_post_mix, v_g_pre_ffn, v_g_post_ffn, v_q_norm_a, v_k_norm_a, v_sink_b, v_rel_bias])
    s_pre_mix, s_post_mix, s_pre_ffn, s_post_ffn, s_qn, s_kn, s_sink, s_rel = small

    def outs(kind):
        return [big["w_in"][kind][None], big["w_o"][kind][None], s_pre_mix[kind], s_post_mix[kind], s_qn[kind],
                s_kn[kind], s_sink[kind], s_rel[kind], s_pre_ffn[kind], big["w_up"][kind][None],
                big["w_dn"][kind][None], s_post_ffn[kind]]

    return (loss.reshape(()), grad_x, *outs(0), *outs(1), *outs(2), *outs(3))
```

```python
import functools

import jax
import jax.numpy as jnp
import numpy as np
from jax import lax
from jax.experimental import pallas as pl
from jax.experimental.pallas import tpu as pltpu

F32 = jnp.float32
BF16 = jnp.bfloat16
SDS = jax.ShapeDtypeStruct

N_DEV = 8
HEAD_DIM = 64
GROUP = 4
BLOCK = 128
SPAN = 3 * BLOCK
GRID_W = 64
N_BUCKETS = 32
MAX_DISTANCE = 128
ROPE_THETA = 10000.0
EPS = 1e-6
NEG_INF = -1e30
SCALE = HEAD_DIM ** -0.5
VT_PAD = 16

ADAM_LR = 0.001
ADAM_B1 = 0.9
ADAM_B2 = 0.999
ADAM_EPS = 1e-08
ADAM_WD = 0.01
ADAM_STEP = 10

VMEM_LIMIT = 56 * 1024 * 1024
MESH = pl.DeviceIdType.MESH


def _cp(*sem):
    return pltpu.CompilerParams(dimension_semantics=sem, vmem_limit_bytes=VMEM_LIMIT)


def _dot(a, b):
    return jnp.dot(a, b, preferred_element_type=F32)


def _dot_nt(a, b):
    return lax.dot_general(a, b, (((1,), (1,)), ((), ())), preferred_element_type=F32)


def _dot_tn(a, b):
    return lax.dot_general(a, b, (((0,), (0,)), ((), ())), preferred_element_type=F32)


def _rms_fwd(x, g):
    r = lax.rsqrt(jnp.mean(x * x, axis=-1, keepdims=True) + EPS)
    n = x * r
    return n * g, n, r


def _rms_bwd(n, r, g, dy):
    gd = g * dy
    dx = r * (gd - n * jnp.mean(n * gd, axis=-1, keepdims=True))
    return dx, dy * n


def _rope_tables(s_len):
    rows = s_len // GRID_W
    row = np.repeat(np.arange(rows, dtype=np.int32), GRID_W)
    col = np.tile(np.arange(GRID_W, dtype=np.int32), rows)
    nf = HEAD_DIM // 4
    freqs = np.float32(ROPE_THETA) ** (-np.arange(nf, dtype=np.float32) / np.float32(nf))
    ang_r = row.astype(np.float32)[:, None] * freqs[None, :]
    ang_c = col.astype(np.float32)[:, None] * freqs[None, :]
    cr, sr, cc, sc = np.cos(ang_r), np.sin(ang_r), np.cos(ang_c), np.sin(ang_c)
    cos = np.concatenate([cr, cr, cc, cc] * 2, axis=-1).astype(np.float32)
    sin_signed = np.concatenate([-sr, sr, -sc, sc] * 2, axis=-1).astype(np.float32)
    return jnp.asarray(cos), jnp.asarray(sin_signed)


def _t5_bucket(rel):
    nb = N_BUCKETS // 2
    ret = (rel > 0).astype(jnp.int32) * nb
    n = jnp.abs(rel)
    max_exact = nb // 2
    nf = jnp.maximum(n, 1).astype(F32)
    large = max_exact + (jnp.log(nf / max_exact) / np.float32(np.log(MAX_DISTANCE / max_exact))
                         * (nb - max_exact)).astype(jnp.int32)
    large = jnp.minimum(large, nb - 1)
    return ret + jnp.where(n < max_exact, n, large)


def _mesh_pos():
    return lax.axis_index("x"), lax.axis_index("y"), lax.axis_index("c")


def _lin(p):
    return 4 * p[0] + 2 * p[1] + p[2]


def _weight_gather(shards):
    n = len(shards)

    def body(*refs):
        xs, outs = refs[:n], refs[n:2 * n]
        send_sems, recv_sems, local_sems = refs[2 * n:]
        x, y, c = _mesh_pos()
        me, sibling = (x, y, c), (x, y, 1 - c)
        chips = [(1 - x, y), (x, 1 - y), (1 - x, 1 - y)]

        def copy(a, k, block, to, src=None):
            slot = outs[a].at[_lin(block)]
            return pltpu.make_async_remote_copy(
                src_ref=slot if src is None else src, dst_ref=slot,
                send_sem=send_sems.at[a, k], recv_sem=recv_sems.at[a, k],
                device_id=to, device_id_type=MESH)

        started = []
        for a in range(n):
            mine = pltpu.make_async_copy(xs[a], outs[a].at[_lin(me)], local_sems.at[a])
            mine.start()
            started.append(mine)
        sends = []
        for a in range(n):
            first = [copy(a, 0, me, sibling, src=xs[a])]
            first += [copy(a, 1 + j, me, (*chip, c), src=xs[a]) for j, chip in enumerate(chips)]
            for cp in first:
                cp.start()
            sends += first
        for a in range(n):
            for j, chip in enumerate(chips):
                copy(a, 1 + j, (*chip, c), me).wait_recv()
                fwd = copy(a, 4 + j, (*chip, c), sibling)
                fwd.start()
                sends.append(fwd)
        for a in range(n):
            copy(a, 0, sibling, me).wait_recv()
            for j, chip in enumerate(chips):
                copy(a, 4 + j, (*chip, 1 - c), me).wait_recv()
        for cp in sends:
            cp.wait_send()
        for mine in started:
            mine.wait()

    anyspec = pl.BlockSpec(memory_space=pl.ANY)
    return pl.pallas_call(
        body,
        out_shape=[SDS((N_DEV,) + s.shape, s.dtype) for s in shards],
        in_specs=[anyspec] * n,
        out_specs=[anyspec] * n,
        scratch_shapes=[pltpu.SemaphoreType.DMA((n, 7)), pltpu.SemaphoreType.DMA((n, 7)),
                        pltpu.SemaphoreType.DMA((n,))],
        name="weight_gather",
    )(*shards)


def _direct_exchange(kind, ins, outs, send_sems, recv_sems, local_sems):
    x, y, c = _mesh_pos()
    me = (x, y, c)
    peers = [(x, y, 1 - c), (1 - x, y, c), (x, 1 - y, c), (1 - x, 1 - y, c),
             (1 - x, y, 1 - c), (x, 1 - y, 1 - c), (1 - x, 1 - y, 1 - c)]

    def src(a, to):
        return ins[a] if kind == "gather" else ins[a].at[_lin(to)]

    def remote(a, k, to, frm):
        return pltpu.make_async_remote_copy(
            src_ref=src(a, to), dst_ref=outs[a].at[_lin(frm)],
            send_sem=send_sems.at[a, k], recv_sem=recv_sems.at[a, k],
            device_id=to, device_id_type=MESH)

    n = len(ins)
    sends = [remote(a, k, p, me) for a in range(n) for k, p in enumerate(peers)]
    arrivals = [remote(a, k, p, p) for a in range(n) for k, p in enumerate(peers)]
    local = [pltpu.make_async_copy(src(a, me), outs[a].at[_lin(me)], local_sems.at[a]) for a in range(n)]

    def start():
        for cp in local + sends:
            cp.start()

    def wait():
        for cp in arrivals:
            cp.wait_recv()
        for cp in sends:
            cp.wait_send()
        for cp in local:
            cp.wait()

    return start, wait


def _exchange_scratch(n):
    return [pltpu.SemaphoreType.DMA((n, 7)), pltpu.SemaphoreType.DMA((n, 7)), pltpu.SemaphoreType.DMA((n,))]


SMALL_LANES = 128


def _small_allreduce(dg_rows, dgq, dgk, dsink_g, drel_g, loss8):
    d = dg_rows[0].shape[1]
    kv = dsink_g.shape[0]

    def body(g1_ref, g2_ref, g3_ref, g4_ref, gq_ref, gk_ref, sk_ref, rl_ref, ls_ref, vec_ref, rel_ref,
             vbuf, rbuf, vland, rland, send_sems, recv_sems):
        x, y, c = _mesh_pos()
        me = (x, y, c)
        peers = [(x, y, 1 - c), (1 - x, y, c), (x, 1 - y, c), (1 - x, 1 - y, c),
                 (1 - x, y, 1 - c), (x, 1 - y, 1 - c), (1 - x, 1 - y, 1 - c)]
        vbuf[...] = jnp.zeros_like(vbuf)
        rbuf[...] = jnp.zeros_like(rbuf)
        for row, ref in enumerate((g1_ref, g2_ref, g3_ref, g4_ref)):
            vbuf[row:row + 1, :] = ref[0:1, :]
        vbuf[4:5, 0:HEAD_DIM] = gq_ref[0:1, 0:HEAD_DIM] + gq_ref[0:1, HEAD_DIM:PAIR]
        vbuf[4:5, SMALL_LANES:SMALL_LANES + HEAD_DIM] = gk_ref[0:1, 0:HEAD_DIM] + gk_ref[0:1, HEAD_DIM:PAIR]
        for g in range(kv):
            vbuf[4:5, 2 * SMALL_LANES + g * GROUP:2 * SMALL_LANES + (g + 1) * GROUP] = sk_ref[g, 0:1, 0:GROUP]
            rbuf[:, g * GROUP:(g + 1) * GROUP] = rl_ref[g, :, 0:GROUP]
        vbuf[4:5, 3 * SMALL_LANES:3 * SMALL_LANES + 1] = ls_ref[0:1, 0:1]

        def copies(k, to, frm):
            return [pltpu.make_async_remote_copy(
                src_ref=buf, dst_ref=land.at[_lin(frm)], send_sem=send_sems.at[a, k], recv_sem=recv_sems.at[a, k],
                device_id=to, device_id_type=MESH) for a, (buf, land) in enumerate(((vbuf, vland), (rbuf, rland)))]

        sends = [cp for k, p in enumerate(peers) for cp in copies(k, p, me)]
        for cp in sends:
            cp.start()
        vland[_lin(me)] = vbuf[...]
        rland[_lin(me)] = rbuf[...]
        for k, p in enumerate(peers):
            for cp in copies(k, p, p):
                cp.wait_recv()
        for cp in sends:
            cp.wait_send()
        vacc, racc = vland[0], rland[0]
        for s in range(1, N_DEV):
            vacc, racc = vacc + vland[s], racc + rland[s]
        vec_ref[...] = vacc
        rel_ref[...] = racc

    vm = pl.BlockSpec(memory_space=pltpu.VMEM)
    return pl.pallas_call(
        body,
        out_shape=[SDS((8, d), F32), SDS((N_BUCKETS, 128), F32)],
        in_specs=[vm] * 9,
        out_specs=[vm, vm],
        scratch_shapes=[pltpu.VMEM((8, d), F32), pltpu.VMEM((N_BUCKETS, 128), F32),
                        pltpu.VMEM((N_DEV, 8, d), F32), pltpu.VMEM((N_DEV, N_BUCKETS, 128), F32),
                        pltpu.SemaphoreType.DMA((2, 7)), pltpu.SemaphoreType.DMA((2, 7))],
        name="small_allreduce",
    )(*dg_rows, dgq, dgk, dsink_g, drel_g, loss8)


def _inproj(x2, g1, w, tm):
    t, d = x2.shape
    p = w.shape[1]

    def body(x_ref, g_ref, w_ref, h_ref, p_ref):
        y, _, _ = _rms_fwd(x_ref[...], g_ref[...])
        h = y.astype(BF16)
        h_ref[...] = h
        p_ref[...] = _dot(h, w_ref[...])

    return pl.pallas_call(
        body,
        grid=(t // tm,),
        in_specs=[pl.BlockSpec((tm, d), lambda i: (i, 0)),
                  pl.BlockSpec((1, d), lambda i: (0, 0)),
                  pl.BlockSpec((d, p), lambda i: (0, 0))],
        out_specs=[pl.BlockSpec((tm, d), lambda i: (i, 0)),
                   pl.BlockSpec((tm, p), lambda i: (i, 0))],
        out_shape=[SDS((t, d), BF16), SDS((t, p), F32)],
        compiler_params=_cp("parallel"),
        name="inproj",
    )(x2, g1, w)


PAIR = 2 * HEAD_DIM


def _pair_masks(ts):
    lane = lax.broadcasted_iota(jnp.int32, (ts, PAIR), 1)
    return lane < HEAD_DIM, (lane % 32) < 16


def _pair_mean(v, low):
    lo = jnp.sum(jnp.where(low, v, 0.0), axis=1, keepdims=True)
    hi = jnp.sum(jnp.where(low, 0.0, v), axis=1, keepdims=True)
    return jnp.where(low, lo, hi) * (1.0 / HEAD_DIM)


def _pair_partner(v, first):
    return jnp.where(first, pltpu.roll(v, PAIR - 16, 1), pltpu.roll(v, 16, 1))


def _qkprep(proj, cos, sin_signed, gq, gk, bl, s_len, ha, kva, hb, kvb, ts):
    t, p_cols = proj.shape
    assert ha % 2 == 0 and kva % 2 == 0 and hb % 2 == 0 and kvb % 2 == 0
    ns = s_len // ts
    sp = s_len + 2 * BLOCK

    def body(p_ref, cos_ref, sin_ref, gq_ref, gk_ref, qa_ref, ka_ref, kat_ref, va_ref, vat_ref, qb_ref, kb_ref,
             kbt_ref, vb_ref, vbt_ref):
        i = pl.program_id(1)
        cs, sn = cos_ref[...], sin_ref[...]
        low, first = _pair_masks(ts)
        ones_row = (lax.broadcasted_iota(jnp.int32, (VT_PAD, ts), 0) == 0).astype(BF16)
        heads = (slice(0, HEAD_DIM), slice(HEAD_DIM, PAIR))

        def pair(p):
            return p_ref[:, p * PAIR:(p + 1) * PAIR]

        def normrope(x, g):
            y = x * lax.rsqrt(_pair_mean(x * x, low) + EPS) * g
            return y * cs + _pair_partner(y, first) * sn

        def transposed(xb):
            return xb.astype(F32).T.astype(BF16)

        for p in range(ha // 2):
            q = (normrope(pair(p), gq_ref[...]) * SCALE).astype(BF16)
            for e, lanes in enumerate(heads):
                qa_ref[0, 2 * p + e] = q[:, lanes]
        base = ha // 2
        for p in range(kva // 2):
            k = normrope(pair(base + p), gk_ref[...]).astype(BF16)
            v = pair(base + kva // 2 + p).astype(BF16)
            kt, vt = transposed(k), transposed(v)
            for e, lanes in enumerate(heads):
                ka_ref[0, 2 * p + e] = k[:, lanes]
                va_ref[0, 2 * p + e] = v[:, lanes]
                kat_ref[0, 2 * p + e] = kt[lanes, :]
                vat_ref[0, 2 * p + e, 0:HEAD_DIM, :] = vt[lanes, :]
                vat_ref[0, 2 * p + e, HEAD_DIM:HEAD_DIM + VT_PAD, :] = ones_row
        base += kva
        for p in range(hb // 2):
            q = (pair(base + p) * SCALE).astype(BF16)
            for e, lanes in enumerate(heads):
                qb_ref[0, 2 * p + e] = q[:, lanes]
        base += hb // 2

        @pl.when(i == 0)
        def _():
            zeros = jnp.zeros((kvb, BLOCK, HEAD_DIM), BF16)
            zeros_t = jnp.zeros((kvb, HEAD_DIM + VT_PAD, BLOCK), BF16)
            for ref in (kb_ref, vb_ref):
                ref[0, :, 0:BLOCK, :] = zeros
                ref[0, :, sp - BLOCK:sp, :] = zeros
            kbt_ref[0, :, :, 0:BLOCK] = zeros_t[:, 0:HEAD_DIM]
            kbt_ref[0, :, :, sp - BLOCK:sp] = zeros_t[:, 0:HEAD_DIM]
            vbt_ref[0, :, :, 0:BLOCK] = zeros_t
            vbt_ref[0, :, :, sp - BLOCK:sp] = zeros_t

        rows = pl.ds(pl.multiple_of(BLOCK + i * ts, BLOCK), ts)
        for p in range(kvb // 2):
            k = pair(base + p).astype(BF16)
            v = pair(base + kvb // 2 + p).astype(BF16)
            kt, vt = transposed(k), transposed(v)
            for e, lanes in enumerate(heads):
                kb_ref[0, 2 * p + e, rows, :] = k[:, lanes]
                vb_ref[0, 2 * p + e, rows, :] = v[:, lanes]
                kbt_ref[0, 2 * p + e, :, rows] = kt[lanes, :]
                vbt_ref[0, 2 * p + e, 0:HEAD_DIM, rows] = vt[lanes, :]
                vbt_ref[0, 2 * p + e, HEAD_DIM:HEAD_DIM + VT_PAD, rows] = ones_row

    def hm(nh):
        return pl.BlockSpec((1, nh, ts, HEAD_DIM), lambda b, i: (b, 0, i, 0))

    def padded(nh):
        return pl.BlockSpec((1, nh, sp, HEAD_DIM), lambda b, i: (b, 0, 0, 0))

    def padded_t(nh, rows):
        return pl.BlockSpec((1, nh, rows, sp), lambda b, i: (b, 0, 0, 0))

    return pl.pallas_call(
        body,
        grid=(bl, ns),
        in_specs=[pl.BlockSpec((ts, p_cols), lambda b, i: (b * ns + i, 0)),
                  pl.BlockSpec((ts, PAIR), lambda b, i: (i, 0)),
                  pl.BlockSpec((ts, PAIR), lambda b, i: (i, 0)),
                  pl.BlockSpec((1, PAIR), lambda b, i: (0, 0)),
                  pl.BlockSpec((1, PAIR), lambda b, i: (0, 0))],
        out_specs=[hm(ha), hm(kva), pl.BlockSpec((1, kva, HEAD_DIM, ts), lambda b, i: (b, 0, 0, i)), hm(kva),
                   pl.BlockSpec((1, kva, HEAD_DIM + VT_PAD, ts), lambda b, i: (b, 0, 0, i)),
                   hm(hb), padded(kvb), padded_t(kvb, HEAD_DIM), padded(kvb), padded_t(kvb, HEAD_DIM + VT_PAD)],
        out_shape=[SDS((bl, ha, s_len, HEAD_DIM), BF16), SDS((bl, kva, s_len, HEAD_DIM), BF16),
                   SDS((bl, kva, HEAD_DIM, s_len), BF16),
                   SDS((bl, kva, s_len, HEAD_DIM), BF16), SDS((bl, kva, HEAD_DIM + VT_PAD, s_len), BF16),
                   SDS((bl, hb, s_len, HEAD_DIM), BF16),
                   SDS((bl, kvb, sp, HEAD_DIM), BF16), SDS((bl, kvb, HEAD_DIM, sp), BF16),
                   SDS((bl, kvb, sp, HEAD_DIM), BF16), SDS((bl, kvb, HEAD_DIM + VT_PAD, sp), BF16)],
        compiler_params=_cp("parallel", "arbitrary"),
        name="qkprep",
    )(proj, cos, sin_signed, gq, gk)


def _bias_build(bucket_t, rel_bias, hb):
    kvb = hb // GROUP

    def body(bkt_ref, tbl_ref, out_ref):
        bkt = bkt_ref[...]
        ci = lax.broadcasted_iota(jnp.int32, (SPAN, BLOCK), 0)
        qi = lax.broadcasted_iota(jnp.int32, (SPAN, BLOCK), 1)
        band = jnp.abs(ci - BLOCK - qi) <= BLOCK
        masks = (band, band & (ci >= BLOCK), band & (ci < 2 * BLOCK))
        for h in range(hb):
            acct = jnp.zeros((SPAN, BLOCK), F32)
            for b in range(N_BUCKETS):
                acct = jnp.where(bkt == b, tbl_ref[b, h], acct)
            lanes = slice((h % GROUP) * BLOCK, (h % GROUP + 1) * BLOCK)
            for var, mask in enumerate(masks):
                out_ref[var, h // GROUP, :, lanes] = jnp.where(mask, acct, NEG_INF)

    vm = pl.BlockSpec(memory_space=pltpu.VMEM)
    return pl.pallas_call(
        body,
        in_specs=[vm, pl.BlockSpec(memory_space=pltpu.SMEM)],
        out_specs=vm,
        out_shape=SDS((3, kvb, SPAN, GROUP * BLOCK), F32),
        name="bias_build",
    )(bucket_t, rel_bias)


def _attn_a_fwd(qa, ka, vat, tq, tk, shards):
    bl, ha, s_len, _ = qa.shape
    kv = ka.shape[1]
    va_rows = vat.shape[2]
    nq, nk = s_len // tq, s_len // tk
    assert nk % 2 == 0
    r = GROUP * tq
    ns = len(shards)

    def body(q_ref, qn_ref, k_ref, v_ref, *rest):
        shard_refs, (o_ref, l_ref), gathered = rest[:ns], rest[ns:ns + 2], rest[ns + 2:2 * ns + 2]
        st_sc, send_sems, recv_sems, local_sems = rest[2 * ns + 2:]
        i = pl.program_id(2)
        step_id = (pl.program_id(0) * kv + pl.program_id(1)) * nq + i
        start, wait = _direct_exchange("gather", shard_refs, gathered, send_sems, recv_sems, local_sems)
        pl.when(step_id == 0)(start)

        q = q_ref[0].reshape(r, HEAD_DIM)

        def scores(c, qv):
            return _dot_nt(k_ref[0, 0, pl.ds(pl.multiple_of(c * tk, tk), tk), :], qv)

        def fold(st, c, carry):
            m_old, acc = carry
            m_new = jnp.maximum(m_old, jnp.max(st, axis=0, keepdims=True))
            pt = jnp.exp(st - m_new).astype(BF16)
            vt = v_ref[0, 0, :, pl.ds(pl.multiple_of(c * tk, tk), tk)]
            return m_new, jnp.exp(m_old - m_new) * acc + _dot(vt, pt)

        @pl.when(i == 0)
        def _():
            st_sc[0] = scores(0, q)

        def step(c2, carry):
            c = 2 * c2
            st_sc[1] = scores(c + 1, q)
            carry = fold(st_sc[0], c, carry)
            st_sc[0] = scores(c + 2, q)
            return fold(st_sc[1], c + 1, carry)

        carry = lax.fori_loop(0, nk // 2 - 1, step,
                              (jnp.full((1, r), -jnp.inf, F32), jnp.zeros((va_rows, r), F32)))
        st_sc[1] = scores(nk - 1, q)
        carry = fold(st_sc[0], nk - 2, carry)
        st_sc[0] = scores(0, qn_ref[0].reshape(r, HEAD_DIM))
        m, acc = fold(st_sc[1], nk - 1, carry)
        l = acc[HEAD_DIM:HEAD_DIM + 1, :]
        o = (acc[0:HEAD_DIM, :] / l).T
        for h in range(GROUP):
            o_ref[:, h * HEAD_DIM:(h + 1) * HEAD_DIM] = o[h * tq:(h + 1) * tq].astype(BF16)
        l_ref[0, 0, 0] = jnp.broadcast_to(m + jnp.log(l), (8, r))
        pl.when(step_id == bl * kv * nq - 1)(wait)

    anyspec = pl.BlockSpec(memory_space=pl.ANY)
    res = pl.pallas_call(
        body,
        grid=(bl, kv, nq),
        in_specs=[pl.BlockSpec((1, GROUP, tq, HEAD_DIM), lambda b, g, i: (b, g, i, 0)),
                  pl.BlockSpec((1, GROUP, tq, HEAD_DIM), lambda b, g, i: (b, g, jnp.minimum(i + 1, nq - 1), 0)),
                  pl.BlockSpec((1, 1, s_len, HEAD_DIM), lambda b, g, i: (b, g, 0, 0)),
                  pl.BlockSpec((1, 1, va_rows, s_len), lambda b, g, i: (b, g, 0, 0))] + [anyspec] * ns,
        out_specs=[pl.BlockSpec((tq, GROUP * HEAD_DIM), lambda b, g, i: (b * nq + i, g)),
                   pl.BlockSpec((1, 1, 1, 8, r), lambda b, g, i: (b, g, i, 0, 0))] + [anyspec] * ns,
        out_shape=[SDS((bl * s_len, ha * HEAD_DIM), BF16), SDS((bl, kv, nq, 8, r), F32)]
        + [SDS((N_DEV,) + s.shape, s.dtype) for s in shards],
        scratch_shapes=[pltpu.VMEM((2, tk, r), F32)] + _exchange_scratch(ns),
        compiler_params=_cp("arbitrary", "arbitrary", "arbitrary"),
        name="attn_a_fwd",
    )(qa, qa, ka, vat, *shards)
    return res[0], res[1], res[2:]


FFN_FWD_BLOCKS_PER_STEP = 8
FFN_BWD_BLOCKS_PER_STEP = 4
QB_PER_STEP = 8


def _bias_variant(n, nb):
    return jnp.where(n == 0, 1, jnp.where(n == nb - 1, 2, 0))


def _sink_row(sink_ref, g):
    return jnp.concatenate([jnp.full((1, BLOCK), sink_ref[0, g * GROUP + h], F32) for h in range(GROUP)], axis=1)


def _attn_b_fwd(qb, kb, vbt, bias_t, sink, s_len):
    bl, hb, _, _ = qb.shape
    kv = kb.shape[1]
    sp = kb.shape[2]
    vt_rows = vbt.shape[2]
    nb = s_len // BLOCK
    nbs = min(QB_PER_STEP, nb)
    r = GROUP * BLOCK

    def body(q_ref, k_ref, vt_ref, bt_ref, sink_ref, o_ref, l_ref, st_sc, pb_sc):
        g, n0 = pl.program_id(1), pl.program_id(2) * nbs
        sink_row = _sink_row(sink_ref, g)

        def span(j):
            return pl.ds(pl.multiple_of((n0 + j) * BLOCK, BLOCK), SPAN)

        for j in range(nbs):
            q = q_ref[0, :, j * BLOCK:(j + 1) * BLOCK, :].reshape(r, HEAD_DIM)
            st_sc[j] = _dot_nt(k_ref[0, 0, span(j), :], q) + bt_ref[_bias_variant(n0 + j, nb), 0]
        maxes = []
        for j in range(nbs):
            st = st_sc[j]
            m = jnp.maximum(jnp.max(st, axis=0, keepdims=True), sink_row)
            pb_sc[j] = jnp.exp(st - m).astype(BF16)
            maxes.append(m)
        for j in range(nbs):
            m = maxes[j]
            acc = _dot(vt_ref[0, 0, :, span(j)], pb_sc[j])
            l = acc[HEAD_DIM:HEAD_DIM + 1, :] + jnp.exp(sink_row - m)
            o = (acc[0:HEAD_DIM, :] / l).T
            for h in range(GROUP):
                o_ref[j * BLOCK:(j + 1) * BLOCK, h * HEAD_DIM:(h + 1) * HEAD_DIM] = (
                    o[h * BLOCK:(h + 1) * BLOCK].astype(BF16))
            l_ref[0, 0, j] = jnp.broadcast_to(m + jnp.log(l), (8, r))

    return pl.pallas_call(
        body,
        grid=(bl, kv, nb // nbs),
        in_specs=[pl.BlockSpec((1, GROUP, nbs * BLOCK, HEAD_DIM), lambda b, g, n: (b, g, n, 0)),
                  pl.BlockSpec((1, 1, sp, HEAD_DIM), lambda b, g, n: (b, g, 0, 0)),
                  pl.BlockSpec((1, 1, vt_rows, sp), lambda b, g, n: (b, g, 0, 0)),
                  pl.BlockSpec((3, 1, SPAN, r), lambda b, g, n: (0, g, 0, 0)),
                  pl.BlockSpec(memory_space=pltpu.SMEM)],
        out_specs=[pl.BlockSpec((nbs * BLOCK, GROUP * HEAD_DIM), lambda b, g, n: (b * (nb // nbs) + n, g)),
                   pl.BlockSpec((1, 1, nbs, 8, r), lambda b, g, n: (b, g, n, 0, 0))],
        out_shape=[SDS((bl * s_len, hb * HEAD_DIM), BF16), SDS((bl, kv, nb, 8, r), F32)],
        scratch_shapes=[pltpu.VMEM((nbs, SPAN, r), F32), pltpu.VMEM((nbs, SPAN, r), BF16)],
        compiler_params=_cp("parallel", "parallel", "arbitrary"),
        name="attn_b_fwd",
    )(qb, kb, vbt, bias_t, sink)


def _mixout(oa, ob, wo, x2, g2, g3, tm):
    t, d = x2.shape
    ca = oa.shape[1]

    def body(oa_ref, ob_ref, w_ref, x_ref, g2_ref, g3_ref, mix_ref, x1_ref, h2_ref):
        mix = _dot(oa_ref[...], w_ref[0:ca, :]) + _dot(ob_ref[...], w_ref[ca:, :])
        mix_ref[...] = mix
        y2, _, _ = _rms_fwd(mix, g2_ref[...])
        x1 = x_ref[...] + y2
        x1_ref[...] = x1
        y3, _, _ = _rms_fwd(x1, g3_ref[...])
        h2_ref[...] = y3.astype(BF16)

    tile = lambda w: pl.BlockSpec((tm, w), lambda i: (i, 0))
    vec = pl.BlockSpec((1, d), lambda i: (0, 0))
    return pl.pallas_call(
        body,
        grid=(t // tm,),
        in_specs=[tile(ca), tile(ob.shape[1]), pl.BlockSpec(wo.shape, lambda i: (0, 0)), tile(d), vec, vec],
        out_specs=[tile(d), tile(d), tile(d)],
        out_shape=[SDS((t, d), F32), SDS((t, d), F32), SDS((t, d), BF16)],
        compiler_params=_cp("parallel"),
        name="mixout",
    )(oa, ob, wo, x2, g2, g3)


def _ffn_fwd(h2, wup_g, wdn, x1, target, g4, tm, jb):
    t, d = x1.shape
    nblk, _, tf = wup_g.shape
    ff = nblk * tf
    nt = t // tm
    nj = nblk // jb

    def body(h_ref, wu_ref, wd_ref, x1_ref, tg_ref, g_ref, u_ref, df_ref, dy_ref, dg_ref, loss_ref, acc_sc):
        i, j = pl.program_id(0), pl.program_id(1)

        @pl.when(j == 0)
        def _():
            acc_sc[...] = jnp.zeros_like(acc_sc)

        @pl.when((i == 0) & (j == 0))
        def _():
            dg_ref[...] = jnp.zeros_like(dg_ref)
            loss_ref[...] = jnp.zeros_like(loss_ref)

        h = h_ref[...]
        squares = []
        for s in range(jb):
            u = jnp.maximum(_dot(h, wu_ref[s]), 0.0)
            u_ref[:, s * tf:(s + 1) * tf] = u.astype(BF16)
            squares.append((u * u).astype(BF16))
        acc_sc[...] += _dot(jnp.concatenate(squares, axis=1), wd_ref[...])

        @pl.when(j == nj - 1)
        def _():
            g = g_ref[...]
            y4, n, r = _rms_fwd(acc_sc[...], g)
            e = (x1_ref[...] + y4) - tg_ref[...]
            loss_ref[...] += jnp.sum(e * e) * (0.5 / d)
            dy = e * (1.0 / d)
            dy_ref[...] = dy
            df, dgt = _rms_bwd(n, r, g, dy)
            df_ref[...] = df.astype(BF16)
            dg_ref[0:1, :] += jnp.sum(dgt, axis=0, keepdims=True)

    tile = pl.BlockSpec((tm, d), lambda i, j: (i, 0))
    return pl.pallas_call(
        body,
        grid=(nt, nj),
        in_specs=[tile,
                  pl.BlockSpec((jb, d, tf), lambda i, j: (j, 0, 0)),
                  pl.BlockSpec((jb * tf, d), lambda i, j: (j, 0)),
                  tile, tile,
                  pl.BlockSpec((1, d), lambda i, j: (0, 0))],
        out_specs=[pl.BlockSpec((tm, jb * tf), lambda i, j: (i, j)), tile, tile,
                   pl.BlockSpec((8, d), lambda i, j: (0, 0)),
                   pl.BlockSpec((8, 128), lambda i, j: (0, 0))],
        out_shape=[SDS((t, ff), BF16), SDS((t, d), BF16), SDS((t, d), F32), SDS((8, d), F32), SDS((8, 128), F32)],
        scratch_shapes=[pltpu.VMEM((tm, d), F32)],
        compiler_params=_cp("arbitrary", "arbitrary"),
        name="ffn_fwd",
    )(h2, wup_g, wdn, x1, target, g4)


def _ffn_bwd(df, u, wdn, wup_g, x1, dy, mix, g3, g2, tm, jb):
    t, d = x1.shape
    nblk, _, tf = wup_g.shape
    nt = t // tm
    nj = nblk // jb

    def body(df_ref, u_ref, wd_ref, wu_ref, x1_ref, dy_ref, mix_ref, g3_ref, g2_ref,
             dpre_ref, dx1_ref, dmix_ref, dg3_ref, dg2_ref, acc_sc):
        i, j = pl.program_id(0), pl.program_id(1)

        @pl.when(j == 0)
        def _():
            acc_sc[...] = jnp.zeros_like(acc_sc)

        @pl.when((i == 0) & (j == 0))
        def _():
            dg3_ref[...] = jnp.zeros_like(dg3_ref)
            dg2_ref[...] = jnp.zeros_like(dg2_ref)

        du2 = _dot_nt(df_ref[...], wd_ref[...])
        dpre = (2.0 * u_ref[...].astype(F32) * du2).astype(BF16)
        dpre_ref[...] = dpre
        dh = _dot_nt(dpre[:, 0:tf], wu_ref[0])
        for s in range(1, jb):
            dh = dh + _dot_nt(dpre[:, s * tf:(s + 1) * tf], wu_ref[s])
        acc_sc[...] += dh

        @pl.when(j == nj - 1)
        def _():
            g3, g2 = g3_ref[...], g2_ref[...]
            _, n3, r3 = _rms_fwd(x1_ref[...], g3)
            dx, dgt3 = _rms_bwd(n3, r3, g3, acc_sc[...])
            dx1 = dy_ref[...] + dx
            dx1_ref[...] = dx1
            dg3_ref[0:1, :] += jnp.sum(dgt3, axis=0, keepdims=True)
            _, n2, r2 = _rms_fwd(mix_ref[...], g2)
            dmix, dgt2 = _rms_bwd(n2, r2, g2, dx1)
            dmix_ref[...] = dmix.astype(BF16)
            dg2_ref[0:1, :] += jnp.sum(dgt2, axis=0, keepdims=True)

    tile = pl.BlockSpec((tm, d), lambda i, j: (i, 0))
    vec = pl.BlockSpec((1, d), lambda i, j: (0, 0))
    acc8 = pl.BlockSpec((8, d), lambda i, j: (0, 0))
    return pl.pallas_call(
        body,
        grid=(nt, nj),
        in_specs=[tile,
                  pl.BlockSpec((tm, jb * tf), lambda i, j: (i, j)),
                  pl.BlockSpec((jb * tf, d), lambda i, j: (j, 0)),
                  pl.BlockSpec((jb, d, tf), lambda i, j: (j, 0, 0)),
                  tile, tile, tile, vec, vec],
        out_specs=[pl.BlockSpec((tm, jb * tf), lambda i, j: (i, j)), tile, tile, acc8, acc8],
        out_shape=[SDS(u.shape, BF16), SDS((t, d), F32), SDS((t, d), BF16), SDS((8, d), F32), SDS((8, d), F32)],
        scratch_shapes=[pltpu.VMEM((tm, d), F32)],
        compiler_params=_cp("arbitrary", "arbitrary"),
        name="ffn_bwd",
    )(df, u, wdn, wup_g, x1, dy, mix, g3, g2)


def _wgrad(a, b, a_spec, b_spec, out_block, out_shape, nj, nk, name, prep_a=None, prep_b=None):
    acc_shape = out_block[1:]

    def body(a_ref, b_ref, o_ref, acc_sc):
        k = pl.program_id(1)
        av = a_ref[...] if prep_a is None else prep_a(a_ref)
        bv = b_ref[...] if prep_b is None else prep_b(b_ref)
        part = _dot_tn(av, bv)

        @pl.when(k == 0)
        def _():
            acc_sc[...] = part

        @pl.when(k > 0)
        def _():
            acc_sc[...] += part

        @pl.when(k == nk - 1)
        def _():
            o_ref[0] = acc_sc[...].astype(BF16)

    return pl.pallas_call(
        body,
        grid=(nj, nk),
        in_specs=[a_spec, b_spec],
        out_specs=pl.BlockSpec(out_block, lambda j, k: (j, 0, 0)),
        out_shape=SDS(out_shape, BF16),
        scratch_shapes=[pltpu.VMEM(acc_shape, F32)],
        compiler_params=_cp("parallel", "arbitrary"),
        name=name,
    )(a, b)


def _wgrad_cols(a, b, nj, tt, name):
    t, m = a.shape
    bn = b.shape[1] // nj
    return _wgrad(a, b, pl.BlockSpec((tt, m), lambda j, k: (k, 0)), pl.BlockSpec((tt, bn), lambda j, k: (k, j)),
                  (1, m, bn), (nj, m, bn), nj, t // tt, name)


def _wgrad_rows_squared(a, b, nj, tt, name):
    t, n = b.shape
    bm = a.shape[1] // nj

    def square(a_ref):
        af = a_ref[...].astype(F32)
        return (af * af).astype(BF16)

    return _wgrad(a, b, pl.BlockSpec((tt, bm), lambda j, k: (k, j)), pl.BlockSpec((tt, n), lambda j, k: (k, 0)),
                  (1, bm, n), (nj, bm, n), nj, t // tt, name, prep_a=square)


def _wgrad_o(oa, ob, dmix, nj, tt):
    t, n = dmix.shape
    ca, cb = oa.shape[1], ob.shape[1]
    m = ca + cb
    nk = t // tt

    def body(oa_ref, ob_ref, b_ref, o_ref, acc_sc):
        k = pl.program_id(0)
        part = _dot_tn(jnp.concatenate([oa_ref[...], ob_ref[...]], axis=1), b_ref[...])

        @pl.when(k == 0)
        def _():
            acc_sc[...] = part

        @pl.when(k > 0)
        def _():
            acc_sc[...] += part

        @pl.when(k == nk - 1)
        def _():
            o_ref[...] = acc_sc[...].reshape(nj, m // nj, n).astype(BF16)

    return pl.pallas_call(
        body,
        grid=(nk,),
        in_specs=[pl.BlockSpec((tt, ca), lambda k: (k, 0)), pl.BlockSpec((tt, cb), lambda k: (k, 0)),
                  pl.BlockSpec((tt, n), lambda k: (k, 0))],
        out_specs=pl.BlockSpec((nj, m // nj, n), lambda k: (0, 0, 0)),
        out_shape=SDS((nj, m // nj, n), BF16),
        scratch_shapes=[pltpu.VMEM((m, n), F32)],
        compiler_params=_cp("arbitrary"),
        name="wgrad_o",
    )(oa, ob, dmix)


def _attn_out_bwd(dmix, wo, ca, tm):
    t, d = dmix.shape
    cb = wo.shape[0] - ca

    def body(dm_ref, w_ref, da_ref, db_ref):
        dm = dm_ref[...]
        da_ref[...] = _dot_nt(dm, w_ref[0:ca, :]).astype(BF16)
        db_ref[...] = _dot_nt(dm, w_ref[ca:, :]).astype(BF16)

    return pl.pallas_call(
        body,
        grid=(t // tm,),
        in_specs=[pl.BlockSpec((tm, d), lambda i: (i, 0)), pl.BlockSpec(wo.shape, lambda i: (0, 0))],
        out_specs=[pl.BlockSpec((tm, ca), lambda i: (i, 0)), pl.BlockSpec((tm, cb), lambda i: (i, 0))],
        out_shape=[SDS((t, ca), BF16), SDS((t, cb), BF16)],
        compiler_params=_cp("parallel"),
        name="attn_out_bwd",
    )(dmix, wo)


def _stack_heads(ref, rows):
    return jnp.concatenate([ref[:, h * HEAD_DIM:(h + 1) * HEAD_DIM] for h in range(GROUP)], axis=0)


def _attn_a_bwd(qa, ka, kat, va, do, o, lse, tq, tk, grads):
    bl, ha, s_len, _ = qa.shape
    kv = ka.shape[1]
    nq, nk = s_len // tq, s_len // tk
    assert nk % 2 == 0
    r = GROUP * tq
    ng = len(grads)

    def body(q_ref, qn_ref, k_ref, kt_ref, v_ref, do_ref, don_ref, o_ref, l_ref, *rest):
        grad_refs, (dq_ref, dk_ref, dv_ref), parts = rest[:ng], rest[ng:ng + 3], rest[ng + 3:2 * ng + 3]
        st_sc, dp_sc, dkt_sc, dvt_sc, send_sems, recv_sems, local_sems = rest[2 * ng + 3:]
        i = pl.program_id(2)
        step_id = (pl.program_id(0) * kv + pl.program_id(1)) * nq + i
        start, wait = _direct_exchange("scatter", grad_refs, parts, send_sems, recv_sems, local_sems)
        pl.when(step_id == 0)(start)

        q = q_ref[0].reshape(r, HEAD_DIM)
        do2 = _stack_heads(do_ref, tq)
        qt = q.astype(F32).T
        dot32 = do2.astype(F32).T
        ot32 = _stack_heads(o_ref, tq).astype(F32).T
        drow = jnp.sum(dot32 * ot32, axis=0, keepdims=True)
        qt, dot = qt.astype(BF16), dot32.astype(BF16)
        lrow = l_ref[0, 0, 0, 0:1, :]

        @pl.when(i == 0)
        def _():
            dkt_sc[...] = jnp.zeros_like(dkt_sc)
            dvt_sc[...] = jnp.zeros_like(dvt_sc)

        def chunk(c):
            return pl.ds(pl.multiple_of(c * tk, tk), tk)

        def scores(c, slot, qv=q, dov=do2):
            st_sc[slot] = _dot_nt(k_ref[0, 0, chunk(c), :], qv)
            dp_sc[slot] = _dot_nt(v_ref[0, 0, chunk(c), :], dov)

        def fold(slot, c, dqt):
            pt = jnp.exp(st_sc[slot] - lrow)
            dsb = (pt * (dp_sc[slot] - drow)).astype(BF16)
            dvt_sc[:, chunk(c)] += _dot_nt(dot, pt.astype(BF16))
            dkt_sc[:, chunk(c)] += _dot_nt(qt, dsb)
            return dqt + _dot(kt_ref[0, 0, :, chunk(c)], dsb)

        @pl.when(i == 0)
        def _():
            scores(0, 0)

        def step(c2, dqt):
            c = 2 * c2
            scores(c + 1, 1)
            dqt = fold(0, c, dqt)
            scores(c + 2, 0)
            return fold(1, c + 1, dqt)

        dqt = lax.fori_loop(0, nk // 2 - 1, step, jnp.zeros((HEAD_DIM, r), F32))
        scores(nk - 1, 1)
        dqt = fold(0, nk - 2, dqt)
        scores(0, 0, qn_ref[0].reshape(r, HEAD_DIM), _stack_heads(don_ref, tq))
        dqt = fold(1, nk - 1, dqt)
        dq_ref[0] = dqt.T.reshape(GROUP, tq, HEAD_DIM)

        @pl.when(i == nq - 1)
        def _():
            dk_ref[0, 0] = dkt_sc[...].T
            dv_ref[0, 0] = dvt_sc[...].T

        pl.when(step_id == bl * kv * nq - 1)(wait)

    kvspec = pl.BlockSpec((1, 1, s_len, HEAD_DIM), lambda b, g, i: (b, g, 0, 0))
    qspec = pl.BlockSpec((1, GROUP, tq, HEAD_DIM), lambda b, g, i: (b, g, i, 0))
    tok = pl.BlockSpec((tq, GROUP * HEAD_DIM), lambda b, g, i: (b * nq + i, g))
    qnext = pl.BlockSpec((1, GROUP, tq, HEAD_DIM), lambda b, g, i: (b, g, jnp.minimum(i + 1, nq - 1), 0))
    toknext = pl.BlockSpec((tq, GROUP * HEAD_DIM), lambda b, g, i: (b * nq + jnp.minimum(i + 1, nq - 1), g))
    anyspec = pl.BlockSpec(memory_space=pl.ANY)
    res = pl.pallas_call(
        body,
        grid=(bl, kv, nq),
        in_specs=[qspec, qnext, kvspec, pl.BlockSpec((1, 1, HEAD_DIM, s_len), lambda b, g, i: (b, g, 0, 0)), kvspec,
                  tok, toknext, tok, pl.BlockSpec((1, 1, 1, 8, r), lambda b, g, i: (b, g, i, 0, 0))] + [anyspec] * ng,
        out_specs=[qspec, kvspec, kvspec] + [anyspec] * ng,
        out_shape=[SDS(qa.shape, F32), SDS(ka.shape, F32), SDS(va.shape, F32)]
        + [SDS(g.shape, g.dtype) for g in grads],
        scratch_shapes=[pltpu.VMEM((2, tk, r), F32), pltpu.VMEM((2, tk, r), F32),
                        pltpu.VMEM((HEAD_DIM, s_len), F32), pltpu.VMEM((HEAD_DIM, s_len), F32)]
        + _exchange_scratch(ng),
        compiler_params=_cp("arbitrary", "arbitrary", "arbitrary"),
        name="attn_a_bwd",
    )(qa, qa, ka, kat, va, do, do, o, lse, *grads)
    return res[0], res[1], res[2], res[3:]


def _attn_b_bwd(qb, kb, kbt, vb, do, o, lse, bias_t, sink, s_len):
    bl, hb, _, _ = qb.shape
    kv, sp = kb.shape[1], kb.shape[2]
    nb = s_len // BLOCK
    nbs = min(QB_PER_STEP, nb)
    r = GROUP * BLOCK

    def body(q_ref, k_ref, kt_ref, v_ref, do_ref, o_ref, l_ref, bt_ref, sink_ref,
             dq_ref, dk_ref, dv_ref, dsum_ref, dsink_ref, dkt_sc, dvt_sc):
        g, b, ns = pl.program_id(0), pl.program_id(1), pl.program_id(2)
        sink_row = _sink_row(sink_ref, g)

        @pl.when(ns == 0)
        def _():
            dkt_sc[...] = jnp.zeros_like(dkt_sc)
            dvt_sc[...] = jnp.zeros_like(dvt_sc)

        @pl.when((b == 0) & (ns == 0))
        def _():
            dsum_ref[...] = jnp.zeros_like(dsum_ref)
            dsink_ref[...] = jnp.zeros_like(dsink_ref)

        dsum = jnp.zeros((SPAN, r), F32)
        dsink = jnp.zeros((1, r), F32)
        for j in range(nbs):
            n = ns * nbs + j
            span = pl.ds(pl.multiple_of(n * BLOCK, BLOCK), SPAN)
            rows = slice(j * BLOCK, (j + 1) * BLOCK)
            q = q_ref[0, :, rows, :].reshape(r, HEAD_DIM)
            do2 = jnp.concatenate([do_ref[rows, h * HEAD_DIM:(h + 1) * HEAD_DIM] for h in range(GROUP)], axis=0)
            o2 = jnp.concatenate([o_ref[rows, h * HEAD_DIM:(h + 1) * HEAD_DIM] for h in range(GROUP)], axis=0)
            dot32 = do2.astype(F32).T
            drow = jnp.sum(dot32 * o2.astype(F32).T, axis=0, keepdims=True)
            qt, dot = q.astype(F32).T.astype(BF16), dot32.astype(BF16)
            lrow = l_ref[0, 0, j, 0:1, :]
            st = _dot_nt(k_ref[0, 0, span, :], q) + bt_ref[_bias_variant(n, nb), 0]
            pt = jnp.exp(st - lrow)
            dst = pt * (_dot_nt(v_ref[0, 0, span, :], do2) - drow)
            dsum = dsum + dst
            dsink = dsink - jnp.exp(sink_row - lrow) * drow
            dsb = dst.astype(BF16)
            dvt_sc[:, span] += _dot_nt(dot, pt.astype(BF16))
            dkt_sc[:, span] += _dot_nt(qt, dsb)
            dq_ref[0, :, rows, :] = _dot(kt_ref[0, 0, :, span], dsb).T.reshape(GROUP, BLOCK, HEAD_DIM)
        dsum_ref[0] += dsum
        dsink_ref[0, 0:1, :] += dsink

        @pl.when(ns == nb // nbs - 1)
        def _():
            dk_ref[0, 0] = dkt_sc[:, BLOCK:BLOCK + s_len].T
            dv_ref[0, 0] = dvt_sc[:, BLOCK:BLOCK + s_len].T

    kvspec = pl.BlockSpec((1, 1, sp, HEAD_DIM), lambda g, b, n: (b, g, 0, 0))
    kvout = pl.BlockSpec((1, 1, s_len, HEAD_DIM), lambda g, b, n: (b, g, 0, 0))
    qspec = pl.BlockSpec((1, GROUP, nbs * BLOCK, HEAD_DIM), lambda g, b, n: (b, g, n, 0))
    tok = pl.BlockSpec((nbs * BLOCK, GROUP * HEAD_DIM), lambda g, b, n: (b * (nb // nbs) + n, g))
    return pl.pallas_call(
        body,
        grid=(kv, bl, nb // nbs),
        in_specs=[qspec, kvspec, pl.BlockSpec((1, 1, HEAD_DIM, sp), lambda g, b, n: (b, g, 0, 0)), kvspec, tok, tok,
                  pl.BlockSpec((1, 1, nbs, 8, r), lambda g, b, n: (b, g, n, 0, 0)),
                  pl.BlockSpec((3, 1, SPAN, r), lambda g, b, n: (0, g, 0, 0)),
                  pl.BlockSpec(memory_space=pltpu.SMEM)],
        out_specs=[qspec, kvout, kvout,
                   pl.BlockSpec((1, SPAN, r), lambda g, b, n: (g, 0, 0)),
                   pl.BlockSpec((1, 8, r), lambda g, b, n: (g, 0, 0))],
        out_shape=[SDS(qb.shape, F32), SDS((bl, kv, s_len, HEAD_DIM), F32), SDS((bl, kv, s_len, HEAD_DIM), F32),
                   SDS((kv, SPAN, r), F32), SDS((kv, 8, r), F32)],
        scratch_shapes=[pltpu.VMEM((HEAD_DIM, sp), F32), pltpu.VMEM((HEAD_DIM, sp), F32)],
        compiler_params=_cp("arbitrary", "arbitrary", "arbitrary"),
        name="attn_b_bwd",
    )(qb, kb, kbt, vb, do, o, lse, bias_t, sink)


def _bias_reduce(dsum, dsink, bucket_t4):
    kv, _, r = dsum.shape

    def body(ds_ref, dk_ref, bk_ref, rel_ref, sink_ref):
        lane = lax.broadcasted_iota(jnp.int32, (N_BUCKETS, 128), 1)
        lane8 = lax.broadcasted_iota(jnp.int32, (8, 128), 1)
        bk = bk_ref[...]
        for g in range(kv):
            ds = ds_ref[g]
            rowi = lax.broadcasted_iota(jnp.int32, (N_BUCKETS, r), 0)
            red = jnp.zeros((N_BUCKETS, r), F32)
            for b in range(N_BUCKETS):
                red = jnp.where(rowi == b, jnp.sum(jnp.where(bk == b, ds, 0.0), axis=0, keepdims=True), red)
            out = jnp.zeros((N_BUCKETS, 128), F32)
            so = jnp.zeros((8, 128), F32)
            for h in range(GROUP):
                col = jnp.sum(red[:, h * BLOCK:(h + 1) * BLOCK], axis=1, keepdims=True)
                out = jnp.where(lane == h, col, out)
                sc = jnp.sum(dk_ref[g][:, h * BLOCK:(h + 1) * BLOCK], axis=1, keepdims=True)
                so = jnp.where(lane8 == h, sc, so)
            rel_ref[g] = out
            sink_ref[g] = so

    vm = pl.BlockSpec(memory_space=pltpu.VMEM)
    return pl.pallas_call(
        body,
        in_specs=[vm, vm, vm],
        out_specs=[vm, vm],
        out_shape=[SDS((kv, N_BUCKETS, 128), F32), SDS((kv, 8, 128), F32)],
        name="bias_reduce",
    )(dsum, dsink, bucket_t4)


def _dqkprep(dqa, dka, dva, dqb, dkb, dvb, proj, cos, sin_signed, gq, gk, s_len, ts):
    t, p_cols = proj.shape
    bl, ha = dqa.shape[0], dqa.shape[1]
    kva, hb, kvb = dka.shape[1], dqb.shape[1], dkb.shape[1]
    ns = s_len // ts

    def body(dqa_ref, dka_ref, dva_ref, dqb_ref, dkb_ref, dvb_ref, p_ref, cos_ref, sin_ref, gq_ref, gk_ref,
             dp_ref, dgq_ref, dgk_ref):
        b, i = pl.program_id(0), pl.program_id(1)
        cs, sn = cos_ref[...], sin_ref[...]
        low, first = _pair_masks(ts)

        @pl.when((b == 0) & (i == 0))
        def _():
            dgq_ref[...] = jnp.zeros_like(dgq_ref)
            dgk_ref[...] = jnp.zeros_like(dgk_ref)

        def grad_pair(ref, p):
            return jnp.concatenate([ref[0, 2 * p], ref[0, 2 * p + 1]], axis=1)

        def put(p, val):
            dp_ref[:, p * PAIR:(p + 1) * PAIR] = val.astype(BF16)

        def unrope_norm(d_rot, p, g, dg_ref):
            dn = d_rot * cs + _pair_partner(d_rot * sn, first)
            xp = p_ref[:, p * PAIR:(p + 1) * PAIR]
            r = lax.rsqrt(_pair_mean(xp * xp, low) + EPS)
            n = xp * r
            gd = g * dn
            dg_ref[0:1, :] += jnp.sum(dn * n, axis=0, keepdims=True)
            put(p, r * (gd - n * _pair_mean(n * gd, low)))

        for p in range(ha // 2):
            unrope_norm(grad_pair(dqa_ref, p) * SCALE, p, gq_ref[...], dgq_ref)
        base = ha // 2
        for p in range(kva // 2):
            unrope_norm(grad_pair(dka_ref, p), base + p, gk_ref[...], dgk_ref)
            put(base + kva // 2 + p, grad_pair(dva_ref, p))
        base += kva
        for p in range(hb // 2):
            put(base + p, grad_pair(dqb_ref, p) * SCALE)
        base += hb // 2
        for p in range(kvb // 2):
            put(base + p, grad_pair(dkb_ref, p))
            put(base + kvb // 2 + p, grad_pair(dvb_ref, p))

    def hm(nh):
        return pl.BlockSpec((1, nh, ts, HEAD_DIM), lambda b, i: (b, 0, i, 0))

    vec = pl.BlockSpec((1, PAIR), lambda b, i: (0, 0))
    tab = pl.BlockSpec((ts, PAIR), lambda b, i: (i, 0))
    acc = pl.BlockSpec((8, PAIR), lambda b, i: (0, 0))
    pspec = pl.BlockSpec((ts, p_cols), lambda b, i: (b * ns + i, 0))
    return pl.pallas_call(
        body,
        grid=(bl, ns),
        in_specs=[hm(ha), hm(kva), hm(kva), hm(hb), hm(kvb), hm(kvb), pspec, tab, tab, vec, vec],
        out_specs=[pspec, acc, acc],
        out_shape=[SDS((t, p_cols), BF16), SDS((8, PAIR), F32), SDS((8, PAIR), F32)],
        compiler_params=_cp("arbitrary", "arbitrary"),
        name="dqkprep",
    )(dqa, dka, dva, dqb, dkb, dvb, proj, cos, sin_signed, gq, gk)


def _dx_final(dproj, w, x2, dx1, g1, tm, grads):
    t, d = x2.shape
    p_cols = w.shape[1]
    ng = len(grads)
    nsteps = t // tm

    def body(dp_ref, w_ref, x_ref, dx1_ref, g_ref, *rest):
        grad_refs, (dx_ref, dg_ref), parts = rest[:ng], rest[ng:ng + 2], rest[ng + 2:2 * ng + 2]
        start, wait = _direct_exchange("scatter", grad_refs, parts, *rest[2 * ng + 2:])

        @pl.when(pl.program_id(0) == 0)
        def _():
            start()
            dg_ref[...] = jnp.zeros_like(dg_ref)

        dh = _dot_nt(dp_ref[...], w_ref[...])
        g = g_ref[...]
        _, n, r = _rms_fwd(x_ref[...], g)
        dx, dgt = _rms_bwd(n, r, g, dh)
        dx_ref[...] = dx1_ref[...] + dx
        dg_ref[0:1, :] += jnp.sum(dgt, axis=0, keepdims=True)
        pl.when(pl.program_id(0) == nsteps - 1)(wait)

    tile = pl.BlockSpec((tm, d), lambda i: (i, 0))
    anyspec = pl.BlockSpec(memory_space=pl.ANY)
    res = pl.pallas_call(
        body,
        grid=(nsteps,),
        in_specs=[pl.BlockSpec((tm, p_cols), lambda i: (i, 0)),
                  pl.BlockSpec((d, p_cols), lambda i: (0, 0)),
                  tile, tile, pl.BlockSpec((1, d), lambda i: (0, 0))] + [anyspec] * ng,
        out_specs=[tile, pl.BlockSpec((8, d), lambda i: (0, 0))] + [anyspec] * ng,
        out_shape=[SDS((t, d), F32), SDS((8, d), F32)] + [SDS(g.shape, g.dtype) for g in grads],
        scratch_shapes=_exchange_scratch(ng),
        compiler_params=_cp("arbitrary"),
        name="dx_final",
    )(dproj, w, x2, dx1, g1, *grads)
    return res[0], res[1], res[2:]


def _adamw_math(w, g, m, v):
    m = ADAM_B1 * m + (1.0 - ADAM_B1) * g
    v = ADAM_B2 * v + (1.0 - ADAM_B2) * (g * g)
    m_hat = m / (1.0 - ADAM_B1 ** ADAM_STEP)
    v_hat = v / (1.0 - ADAM_B2 ** ADAM_STEP)
    delta = -ADAM_LR * (m_hat / (jnp.sqrt(v_hat) + ADAM_EPS) + ADAM_WD * w)
    return delta, m, v


def _adamw_sum(parts, w, m, v, tr, name):
    rows, cols = w.shape

    def body(p_ref, w_ref, m_ref, v_ref, g_ref, d_ref, nm_ref, nv_ref):
        g = p_ref[0].astype(F32)
        for s in range(1, N_DEV):
            g = g + p_ref[s].astype(F32)
        g_ref[...] = g
        d_ref[...], nm_ref[...], nv_ref[...] = _adamw_math(w_ref[...], g, m_ref[...], v_ref[...])

    tr = min(tr, rows)
    tile = pl.BlockSpec((tr, cols), lambda i: (i, 0))
    return pl.pallas_call(
        body,
        grid=(rows // tr,),
        in_specs=[pl.BlockSpec((N_DEV, tr, cols), lambda i: (0, i, 0)), tile, tile, tile],
        out_specs=[tile] * 4,
        out_shape=[SDS((rows, cols), F32)] * 4,
        compiler_params=_cp("parallel"),
        name=name,
    )(parts, w, m, v)


def _adamw_small(vec, rel, ws, ms, vs):
    hb = ws[6].shape[1]
    n = len(ws)

    def body(vec_ref, rel_ref, *rest):
        w_refs, m_refs, v_refs = rest[:n], rest[n:2 * n], rest[2 * n:3 * n]
        loss_ref, outs = rest[3 * n], rest[3 * n + 1:]
        grads = [vec_ref[0:1, :], vec_ref[1:2, :], vec_ref[2:3, :], vec_ref[3:4, :],
                 vec_ref[4:5, 0:HEAD_DIM], vec_ref[4:5, SMALL_LANES:SMALL_LANES + HEAD_DIM],
                 vec_ref[4:5, 2 * SMALL_LANES:2 * SMALL_LANES + hb], rel_ref[:, 0:hb]]
        loss_ref[...] = vec_ref[4:5, 3 * SMALL_LANES:3 * SMALL_LANES + 1]
        for p, g in enumerate(grads):
            g_ref, d_ref, nm_ref, nv_ref = outs[4 * p:4 * p + 4]
            g_ref[...] = g
            d_ref[...], nm_ref[...], nv_ref[...] = _adamw_math(w_refs[p][...], g, m_refs[p][...], v_refs[p][...])

    vm = pl.BlockSpec(memory_space=pltpu.VMEM)
    res = pl.pallas_call(
        body,
        in_specs=[vm] * (2 + 3 * n),
        out_specs=[vm] * (1 + 4 * n),
        out_shape=[SDS((1, 1), F32)] + [SDS(w.shape, F32) for w in ws for _ in range(4)],
        name="adamw_small",
    )(vec, rel, *ws, *ms, *vs)
    return res[0], [res[1 + 4 * p:5 + 4 * p] for p in range(n)]


def _local_step(x, loss_target, win_g, wo_s, wup_s, wdn_s, g_pre_mix, g_post_mix, q_norm_a, k_norm_a, sink_b,
                rel_bias, g_pre_ffn, g_post_ffn):
    bl, s_len, d = x.shape
    t = bl * s_len
    nh = d // HEAD_DIM
    ha = nh // 2
    kva = ha // GROUP
    hb = nh - ha
    kvb = hb // GROUP
    tm = 512
    tw = min(4096, t)
    ts = min(512, s_len)
    tq, tk = 2 * BLOCK, min(512, s_len // 2)

    x2 = x.reshape(t, d)
    tg2 = loss_target.reshape(t, d)
    cos, sin_signed = _rope_tables(s_len)
    gq2, gk2 = jnp.tile(q_norm_a, (1, 2)), jnp.tile(k_norm_a, (1, 2))
    a = jnp.arange(BLOCK, dtype=jnp.int32)
    c = jnp.arange(SPAN, dtype=jnp.int32)
    bucket_t = _t5_bucket(c[:, None] - BLOCK - a[None, :])
    bucket_t4 = jnp.tile(bucket_t, (1, GROUP))
    w_in_full = jnp.transpose(win_g, (1, 0, 2)).reshape(d, -1)
    p_cols = w_in_full.shape[1]

    h1, proj = _inproj(x2, g_pre_mix, w_in_full, tm)
    qa, ka, kat, va, vat, qb, kb, kbt, vb, vbt = _qkprep(
        proj, cos, sin_signed, gq2, gk2, bl, s_len, ha, kva, hb, kvb, ts)
    bias_t = _bias_build(bucket_t, rel_bias, hb)
    oa, lse_a, (wo_g, wup_g, wdn_g) = _attn_a_fwd(qa, ka, vat, tq, tk, [wo_s, wup_s, wdn_s])
    wo = wo_g.reshape(-1, d)
    wdn = wdn_g.reshape(-1, d)
    ob, lse_b = _attn_b_fwd(qb, kb, vbt, bias_t, sink_b, s_len)
    mix, x1, h2 = _mixout(oa, ob, wo, x2, g_post_mix, g_pre_ffn, tm)
    u, df, dy, dg4, loss8 = _ffn_fwd(h2, wup_g, wdn, x1, tg2, g_post_ffn, tm, FFN_FWD_BLOCKS_PER_STEP)

    dpre, dx1, dmix, dg3, dg2 = _ffn_bwd(df, u, wdn, wup_g, x1, dy, mix, g_pre_ffn, g_post_mix, tm,
                                         FFN_BWD_BLOCKS_PER_STEP)
    gw_dn = _wgrad_rows_squared(u, df, N_DEV, tw, "wgrad_down")
    gw_up = _wgrad_cols(h2, dpre, N_DEV, tw, "wgrad_up")
    gw_o = _wgrad_o(oa, ob, dmix, N_DEV, min(2048, t))
    doa, dob = _attn_out_bwd(dmix, wo, oa.shape[1], tm)
    dqa, dka, dva, (p_o, p_up, p_dn) = _attn_a_bwd(qa, ka, kat, va, doa, oa, lse_a, tq, tk, [gw_o, gw_up, gw_dn])
    dqb, dkb, dvb, dsum, dsink = _attn_b_bwd(qb, kb, kbt, vb, dob, ob, lse_b, bias_t, sink_b, s_len)
    drel_g, dsink_g = _bias_reduce(dsum, dsink, bucket_t4)
    dproj, dgq, dgk = _dqkprep(dqa, dka, dva, dqb, dkb, dvb, proj, cos, sin_signed, gq2, gk2, s_len, ts)
    gw_in = _wgrad_cols(h1, dproj, p_cols // 256, tw, "wgrad_in")
    gw_in = jnp.transpose(jnp.transpose(gw_in, (1, 0, 2)).reshape(d, N_DEV, -1), (1, 0, 2))
    grad_x, dg1, (p_in,) = _dx_final(dproj, w_in_full, x2, dx1, g_pre_mix, tm, [gw_in])

    vec, rel = _small_allreduce([dg1, dg2, dg3, dg4], dgq, dgk, dsink_g, drel_g, loss8)
    return grad_x.reshape(bl, s_len, d), p_in, p_o, p_up, p_dn, vec, rel


def kernel(x, w_in, w_o, g_pre_mix, g_post_mix, q_norm_a, k_norm_a, sink_b, rel_bias, g_pre_ffn, w_ffn_up, w_ffn_down, g_post_ffn, loss_target, m_w_in, m_w_o, m_g_pre_mix, m_g_post_mix, m_q_norm_a, m_k_norm_a, m_sink_b, m_rel_bias, m_g_pre_ffn, m_w_ffn_up, m_w_ffn_down, m_g_post_ffn, v_w_in, v_w_o, v_g_pre_mix, v_g_post_mix, v_q_norm_a, v_k_norm_a, v_sink_b, v_rel_bias, v_g_pre_ffn, v_w_ffn_up, v_w_ffn_down, v_g_post_ffn):
    (win_g,) = _weight_gather([w_in[0].astype(BF16)])

    grad_x, p_in, p_o, p_up, p_dn, vec, rel = _local_step(
        x, loss_target, win_g, w_o[0].astype(BF16), w_ffn_up[0].astype(BF16), w_ffn_down[0].astype(BF16),
        g_pre_mix, g_post_mix, q_norm_a, k_norm_a, sink_b, rel_bias, g_pre_ffn, g_post_ffn)

    big = {
        "w_in": _adamw_sum(p_in, w_in[0], m_w_in[0], v_w_in[0], 256, "adamw_in"),
        "w_o": _adamw_sum(p_o, w_o[0], m_w_o[0], v_w_o[0], 128, "adamw_o"),
        "w_up": _adamw_sum(p_up, w_ffn_up[0], m_w_ffn_up[0], v_w_ffn_up[0], 256, "adamw_up"),
        "w_dn": _adamw_sum(p_dn, w_ffn_down[0], m_w_ffn_down[0], v_w_ffn_down[0], 256, "adamw_down"),
    }
    loss, small = _adamw_small(
        vec, rel,
        [g_pre_mix, g_post_mix, g_pre_ffn, g_post_ffn, q_norm_a, k_norm_a, sink_b, rel_bias],
        [m_g_pre_mix, m_g_post_mix, m_g_pre_ffn, m_g_post_ffn, m_q_norm_a, m_k_norm_a, m_sink_b, m_rel_bias],
        [v_g_pre_mix, v_g_post_mix, v_g_pre_ffn, v_g_post_ffn, v_q_norm_a, v_k_norm_a, v_sink_b, v_rel_bias])
    s_pre_mix, s_post_mix, s_pre_ffn, s_post_ffn, s_qn, s_kn, s_sink, s_rel = small

    def outs(kind):
        return [big["w_in"][kind][None], big["w_o"][kind][None], s_pre_mix[kind], s_post_mix[kind], s_qn[kind],
                s_kn[kind], s_sink[kind], s_rel[kind], s_pre_ffn[kind], big["w_up"][kind][None],
                big["w_dn"][kind][None], s_post_ffn[kind]]

    return (loss.reshape(()), grad_x, *outs(0), *outs(1), *outs(2), *outs(3))
```

```python
import functools

import jax
import jax.numpy as jnp
import numpy as np
from jax import lax
from jax.experimental import pallas as pl
from jax.experimental.pallas import tpu as pltpu

F32 = jnp.float32
BF16 = jnp.bfloat16
SDS = jax.ShapeDtypeStruct

N_DEV = 8
HEAD_DIM = 64
GROUP = 4
BLOCK = 128
SPAN = 3 * BLOCK
GRID_W = 64
N_BUCKETS = 32
MAX_DISTANCE = 128
ROPE_THETA = 10000.0
EPS = 1e-6
NEG_INF = -1e30
SCALE = HEAD_DIM ** -0.5
VT_PAD = 16

ADAM_LR = 0.001
ADAM_B1 = 0.9
ADAM_B2 = 0.999
ADAM_EPS = 1e-08
ADAM_WD = 0.01
ADAM_STEP = 10

VMEM_LIMIT = 56 * 1024 * 1024
MESH = pl.DeviceIdType.MESH


def _cp(*sem):
    return pltpu.CompilerParams(dimension_semantics=sem, vmem_limit_bytes=VMEM_LIMIT)


def _dot(a, b):
    return jnp.dot(a, b, preferred_element_type=F32)


def _dot_nt(a, b):
    return lax.dot_general(a, b, (((1,), (1,)), ((), ())), preferred_element_type=F32)


def _dot_tn(a, b):
    return lax.dot_general(a, b, (((0,), (0,)), ((), ())), preferred_element_type=F32)


def _rms_fwd(x, g):
    r = lax.rsqrt(jnp.mean(x * x, axis=-1, keepdims=True) + EPS)
    n = x * r
    return n * g, n, r


def _rms_bwd(n, r, g, dy):
    gd = g * dy
    dx = r * (gd - n * jnp.mean(n * gd, axis=-1, keepdims=True))
    return dx, dy * n


def _rope_tables(s_len):
    rows = s_len // GRID_W
    row = np.repeat(np.arange(rows, dtype=np.int32), GRID_W)
    col = np.tile(np.arange(GRID_W, dtype=np.int32), rows)
    nf = HEAD_DIM // 4
    freqs = np.float32(ROPE_THETA) ** (-np.arange(nf, dtype=np.float32) / np.float32(nf))
    ang_r = row.astype(np.float32)[:, None] * freqs[None, :]
    ang_c = col.astype(np.float32)[:, None] * freqs[None, :]
    cr, sr, cc, sc = np.cos(ang_r), np.sin(ang_r), np.cos(ang_c), np.sin(ang_c)
    cos = np.concatenate([cr, cr, cc, cc] * 2, axis=-1).astype(np.float32)
    sin_signed = np.concatenate([-sr, sr, -sc, sc] * 2, axis=-1).astype(np.float32)
    return jnp.asarray(cos), jnp.asarray(sin_signed)


def _t5_bucket(rel):
    nb = N_BUCKETS // 2
    ret = (rel > 0).astype(jnp.int32) * nb
    n = jnp.abs(rel)
    max_exact = nb // 2
    nf = jnp.maximum(n, 1).astype(F32)
    large = max_exact + (jnp.log(nf / max_exact) / np.float32(np.log(MAX_DISTANCE / max_exact))
                         * (nb - max_exact)).astype(jnp.int32)
    large = jnp.minimum(large, nb - 1)
    return ret + jnp.where(n < max_exact, n, large)


def _mesh_pos():
    return lax.axis_index("x"), lax.axis_index("y"), lax.axis_index("c")


def _lin(p):
    return 4 * p[0] + 2 * p[1] + p[2]


def _weight_gather(shards):
    n = len(shards)

    def body(*refs):
        xs, outs = refs[:n], refs[n:2 * n]
        send_sems, recv_sems, local_sems = refs[2 * n:]
        x, y, c = _mesh_pos()
        me, sibling = (x, y, c), (x, y, 1 - c)
        chips = [(1 - x, y), (x, 1 - y), (1 - x, 1 - y)]

        def copy(a, k, block, to, src=None):
            slot = outs[a].at[_lin(block)]
            return pltpu.make_async_remote_copy(
                src_ref=slot if src is None else src, dst_ref=slot,
                send_sem=send_sems.at[a, k], recv_sem=recv_sems.at[a, k],
                device_id=to, device_id_type=MESH)

        started = []
        for a in range(n):
            mine = pltpu.make_async_copy(xs[a], outs[a].at[_lin(me)], local_sems.at[a])
            mine.start()
            started.append(mine)
        sends = []
        for a in range(n):
            first = [copy(a, 0, me, sibling, src=xs[a])]
            first += [copy(a, 1 + j, me, (*chip, c), src=xs[a]) for j, chip in enumerate(chips)]
            for cp in first:
                cp.start()
            sends += first
        for a in range(n):
            for j, chip in enumerate(chips):
                copy(a, 1 + j, (*chip, c), me).wait_recv()
                fwd = copy(a, 4 + j, (*chip, c), sibling)
                fwd.start()
                sends.append(fwd)
        for a in range(n):
            copy(a, 0, sibling, me).wait_recv()
            for j, chip in enumerate(chips):
                copy(a, 4 + j, (*chip, 1 - c), me).wait_recv()
        for cp in sends:
            cp.wait_send()
        for mine in started:
            mine.wait()

    anyspec = pl.BlockSpec(memory_space=pl.ANY)
    return pl.pallas_call(
        body,
        out_shape=[SDS((N_DEV,) + s.shape, s.dtype) for s in shards],
        in_specs=[anyspec] * n,
        out_specs=[anyspec] * n,
        scratch_shapes=[pltpu.SemaphoreType.DMA((n, 7)), pltpu.SemaphoreType.DMA((n, 7)),
                        pltpu.SemaphoreType.DMA((n,))],
        name="weight_gather",
    )(*shards)


def _direct_exchange(kind, ins, outs, send_sems, recv_sems, local_sems):
    x, y, c = _mesh_pos()
    me = (x, y, c)
    peers = [(x, y, 1 - c), (1 - x, y, c), (x, 1 - y, c), (1 - x, 1 - y, c),
             (1 - x, y, 1 - c), (x, 1 - y, 1 - c), (1 - x, 1 - y, 1 - c)]

    def src(a, to):
        return ins[a] if kind == "gather" else ins[a].at[_lin(to)]

    def remote(a, k, to, frm):
        return pltpu.make_async_remote_copy(
            src_ref=src(a, to), dst_ref=outs[a].at[_lin(frm)],
            send_sem=send_sems.at[a, k], recv_sem=recv_sems.at[a, k],
            device_id=to, device_id_type=MESH)

    n = len(ins)
    sends = [remote(a, k, p, me) for a in range(n) for k, p in enumerate(peers)]
    arrivals = [remote(a, k, p, p) for a in range(n) for k, p in enumerate(peers)]
    local = [pltpu.make_async_copy(src(a, me), outs[a].at[_lin(me)], local_sems.at[a]) for a in range(n)]

    def start():
        for cp in local + sends:
            cp.start()

    def wait():
        for cp in arrivals:
            cp.wait_recv()
        for cp in sends:
            cp.wait_send()
        for cp in local:
            cp.wait()

    return start, wait


def _exchange_scratch(n):
    return [pltpu.SemaphoreType.DMA((n, 7)), pltpu.SemaphoreType.DMA((n, 7)), pltpu.SemaphoreType.DMA((n,))]


SMALL_LANES = 128


def _small_allreduce(dg_rows, dgq, dgk, dsink_g, drel_g, loss8):
    d = dg_rows[0].shape[1]
    kv = dsink_g.shape[0]

    def body(g1_ref, g2_ref, g3_ref, g4_ref, gq_ref, gk_ref, sk_ref, rl_ref, ls_ref, vec_ref, rel_ref,
             vbuf, rbuf, vland, rland, send_sems, recv_sems):
        x, y, c = _mesh_pos()
        me = (x, y, c)
        peers = [(x, y, 1 - c), (1 - x, y, c), (x, 1 - y, c), (1 - x, 1 - y, c),
                 (1 - x, y, 1 - c), (x, 1 - y, 1 - c), (1 - x, 1 - y, 1 - c)]
        vbuf[...] = jnp.zeros_like(vbuf)
        rbuf[...] = jnp.zeros_like(rbuf)
        for row, ref in enumerate((g1_ref, g2_ref, g3_ref, g4_ref)):
            vbuf[row:row + 1, :] = ref[0:1, :]
        vbuf[4:5, 0:HEAD_DIM] = gq_ref[0:1, 0:HEAD_DIM] + gq_ref[0:1, HEAD_DIM:PAIR]
        vbuf[4:5, SMALL_LANES:SMALL_LANES + HEAD_DIM] = gk_ref[0:1, 0:HEAD_DIM] + gk_ref[0:1, HEAD_DIM:PAIR]
        for g in range(kv):
            vbuf[4:5, 2 * SMALL_LANES + g * GROUP:2 * SMALL_LANES + (g + 1) * GROUP] = sk_ref[g, 0:1, 0:GROUP]
            rbuf[:, g * GROUP:(g + 1) * GROUP] = rl_ref[g, :, 0:GROUP]
        vbuf[4:5, 3 * SMALL_LANES:3 * SMALL_LANES + 1] = ls_ref[0:1, 0:1]

        def copies(k, to, frm):
            return [pltpu.make_async_remote_copy(
                src_ref=buf, dst_ref=land.at[_lin(frm)], send_sem=send_sems.at[a, k], recv_sem=recv_sems.at[a, k],
                device_id=to, device_id_type=MESH) for a, (buf, land) in enumerate(((vbuf, vland), (rbuf, rland)))]

        sends = [cp for k, p in enumerate(peers) for cp in copies(k, p, me)]
        for cp in sends:
            cp.start()
        vland[_lin(me)] = vbuf[...]
        rland[_lin(me)] = rbuf[...]
        for k, p in enumerate(peers):
            for cp in copies(k, p, p):
                cp.wait_recv()
        for cp in sends:
            cp.wait_send()
        vacc, racc = vland[0], rland[0]
        for s in range(1, N_DEV):
            vacc, racc = vacc + vland[s], racc + rland[s]
        vec_ref[...] = vacc
        rel_ref[...] = racc

    vm = pl.BlockSpec(memory_space=pltpu.VMEM)
    return pl.pallas_call(
        body,
        out_shape=[SDS((8, d), F32), SDS((N_BUCKETS, 128), F32)],
        in_specs=[vm] * 9,
        out_specs=[vm, vm],
        scratch_shapes=[pltpu.VMEM((8, d), F32), pltpu.VMEM((N_BUCKETS, 128), F32),
                        pltpu.VMEM((N_DEV, 8, d), F32), pltpu.VMEM((N_DEV, N_BUCKETS, 128), F32),
                        pltpu.SemaphoreType.DMA((2, 7)), pltpu.SemaphoreType.DMA((2, 7))],
        name="small_allreduce",
    )(*dg_rows, dgq, dgk, dsink_g, drel_g, loss8)


def _inproj(x2, g1, w, tm):
    t, d = x2.shape
    p = w.shape[1]

    def body(x_ref, g_ref, w_ref, h_ref, p_ref):
        y, _, _ = _rms_fwd(x_ref[...], g_ref[...])
        h = y.astype(BF16)
        h_ref[...] = h
        p_ref[...] = _dot(h, w_ref[...])

    return pl.pallas_call(
        body,
        grid=(t // tm,),
        in_specs=[pl.BlockSpec((tm, d), lambda i: (i, 0)),
                  pl.BlockSpec((1, d), lambda i: (0, 0)),
                  pl.BlockSpec((d, p), lambda i: (0, 0))],
        out_specs=[pl.BlockSpec((tm, d), lambda i: (i, 0)),
                   pl.BlockSpec((tm, p), lambda i: (i, 0))],
        out_shape=[SDS((t, d), BF16), SDS((t, p), F32)],
        compiler_params=_cp("parallel"),
        name="inproj",
    )(x2, g1, w)


PAIR = 2 * HEAD_DIM


def _pair_masks(ts):
    lane = lax.broadcasted_iota(jnp.int32, (ts, PAIR), 1)
    return lane < HEAD_DIM, (lane % 32) < 16


def _pair_mean(v, low):
    lo = jnp.sum(jnp.where(low, v, 0.0), axis=1, keepdims=True)
    hi = jnp.sum(jnp.where(low, 0.0, v), axis=1, keepdims=True)
    return jnp.where(low, lo, hi) * (1.0 / HEAD_DIM)


def _pair_partner(v, first):
    return jnp.where(first, pltpu.roll(v, PAIR - 16, 1), pltpu.roll(v, 16, 1))


def _qkprep(proj, cos, sin_signed, gq, gk, bl, s_len, ha, kva, hb, kvb, ts):
    t, p_cols = proj.shape
    assert ha % 2 == 0 and kva % 2 == 0 and hb % 2 == 0 and kvb % 2 == 0
    ns = s_len // ts
    sp = s_len + 2 * BLOCK

    def body(p_ref, cos_ref, sin_ref, gq_ref, gk_ref, qa_ref, ka_ref, kat_ref, va_ref, vat_ref, qb_ref, kb_ref,
             kbt_ref, vb_ref, vbt_ref):
        i = pl.program_id(1)
        cs, sn = cos_ref[...], sin_ref[...]
        low, first = _pair_masks(ts)
        ones_row = (lax.broadcasted_iota(jnp.int32, (VT_PAD, ts), 0) == 0).astype(BF16)
        heads = (slice(0, HEAD_DIM), slice(HEAD_DIM, PAIR))

        def pair(p):
            return p_ref[:, p * PAIR:(p + 1) * PAIR]

        def normrope(x, g):
            y = x * lax.rsqrt(_pair_mean(x * x, low) + EPS) * g
            return y * cs + _pair_partner(y, first) * sn

        def transposed(xb):
            return xb.astype(F32).T.astype(BF16)

        for p in range(ha // 2):
            q = (normrope(pair(p), gq_ref[...]) * SCALE).astype(BF16)
            for e, lanes in enumerate(heads):
                qa_ref[0, 2 * p + e] = q[:, lanes]
        base = ha // 2
        for p in range(kva // 2):
            k = normrope(pair(base + p), gk_ref[...]).astype(BF16)
            v = pair(base + kva // 2 + p).astype(BF16)
            kt, vt = transposed(k), transposed(v)
            for e, lanes in enumerate(heads):
                ka_ref[0, 2 * p + e] = k[:, lanes]
                va_ref[0, 2 * p + e] = v[:, lanes]
                kat_ref[0, 2 * p + e] = kt[lanes, :]
                vat_ref[0, 2 * p + e, 0:HEAD_DIM, :] = vt[lanes, :]
                vat_ref[0, 2 * p + e, HEAD_DIM:HEAD_DIM + VT_PAD, :] = ones_row
        base += kva
        for p in range(hb // 2):
            q = (pair(base + p) * SCALE).astype(BF16)
            for e, lanes in enumerate(heads):
                qb_ref[0, 2 * p + e] = q[:, lanes]
        base += hb // 2

        @pl.when(i == 0)
        def _():
            zeros = jnp.zeros((kvb, BLOCK, HEAD_DIM), BF16)
            zeros_t = jnp.zeros((kvb, HEAD_DIM + VT_PAD, BLOCK), BF16)
            for ref in (kb_ref, vb_ref):
                ref[0, :, 0:BLOCK, :] = zeros
                ref[0, :, sp - BLOCK:sp, :] = zeros
            kbt_ref[0, :, :, 0:BLOCK] = zeros_t[:, 0:HEAD_DIM]
            kbt_ref[0, :, :, sp - BLOCK:sp] = zeros_t[:, 0:HEAD_DIM]
            vbt_ref[0, :, :, 0:BLOCK] = zeros_t
            vbt_ref[0, :, :, sp - BLOCK:sp] = zeros_t

        rows = pl.ds(pl.multiple_of(BLOCK + i * ts, BLOCK), ts)
        for p in range(kvb // 2):
            k = pair(base + p).astype(BF16)
            v = pair(base + kvb // 2 + p).astype(BF16)
            kt, vt = transposed(k), transposed(v)
            for e, lanes in enumerate(heads):
                kb_ref[0, 2 * p + e, rows, :] = k[:, lanes]
                vb_ref[0, 2 * p + e, rows, :] = v[:, lanes]
                kbt_ref[0, 2 * p + e, :, rows] = kt[lanes, :]
                vbt_ref[0, 2 * p + e, 0:HEAD_DIM, rows] = vt[lanes, :]
                vbt_ref[0, 2 * p + e, HEAD_DIM:HEAD_DIM + VT_PAD, rows] = ones_row

    def hm(nh):
        return pl.BlockSpec((1, nh, ts, HEAD_DIM), lambda b, i: (b, 0, i, 0))

    def padded(nh):
        return pl.BlockSpec((1, nh, sp, HEAD_DIM), lambda b, i: (b, 0, 0, 0))

    def padded_t(nh, rows):
        return pl.BlockSpec((1, nh, rows, sp), lambda b, i: (b, 0, 0, 0))

    return pl.pallas_call(
        body,
        grid=(bl, ns),
        in_specs=[pl.BlockSpec((ts, p_cols), lambda b, i: (b * ns + i, 0)),
                  pl.BlockSpec((ts, PAIR), lambda b, i: (i, 0)),
                  pl.BlockSpec((ts, PAIR), lambda b, i: (i, 0)),
                  pl.BlockSpec((1, PAIR), lambda b, i: (0, 0)),
                  pl.BlockSpec((1, PAIR), lambda b, i: (0, 0))],
        out_specs=[hm(ha), hm(kva), pl.BlockSpec((1, kva, HEAD_DIM, ts), lambda b, i: (b, 0, 0, i)), hm(kva),
                   pl.BlockSpec((1, kva, HEAD_DIM + VT_PAD, ts), lambda b, i: (b, 0, 0, i)),
                   hm(hb), padded(kvb), padded_t(kvb, HEAD_DIM), padded(kvb), padded_t(kvb, HEAD_DIM + VT_PAD)],
        out_shape=[SDS((bl, ha, s_len, HEAD_DIM), BF16), SDS((bl, kva, s_len, HEAD_DIM), BF16),
                   SDS((bl, kva, HEAD_DIM, s_len), BF16),
                   SDS((bl, kva, s_len, HEAD_DIM), BF16), SDS((bl, kva, HEAD_DIM + VT_PAD, s_len), BF16),
                   SDS((bl, hb, s_len, HEAD_DIM), BF16),
                   SDS((bl, kvb, sp, HEAD_DIM), BF16), SDS((bl, kvb, HEAD_DIM, sp), BF16),
                   SDS((bl, kvb, sp, HEAD_DIM), BF16), SDS((bl, kvb, HEAD_DIM + VT_PAD, sp), BF16)],
        compiler_params=_cp("parallel", "arbitrary"),
        name="qkprep",
    )(proj, cos, sin_signed, gq, gk)


def _bias_build(bucket_t, rel_bias, hb):
    kvb = hb // GROUP

    def body(bkt_ref, tbl_ref, out_ref):
        bkt = bkt_ref[...]
        ci = lax.broadcasted_iota(jnp.int32, (SPAN, BLOCK), 0)
        qi = lax.broadcasted_iota(jnp.int32, (SPAN, BLOCK), 1)
        band = jnp.abs(ci - BLOCK - qi) <= BLOCK
        masks = (band, band & (ci >= BLOCK), band & (ci < 2 * BLOCK))
        for h in range(hb):
            acct = jnp.zeros((SPAN, BLOCK), F32)
            for b in range(N_BUCKETS):
                acct = jnp.where(bkt == b, tbl_ref[b, h], acct)
            lanes = slice((h % GROUP) * BLOCK, (h % GROUP + 1) * BLOCK)
            for var, mask in enumerate(masks):
                out_ref[var, h // GROUP, :, lanes] = jnp.where(mask, acct, NEG_INF)

    vm = pl.BlockSpec(memory_space=pltpu.VMEM)
    return pl.pallas_call(
        body,
        in_specs=[vm, pl.BlockSpec(memory_space=pltpu.SMEM)],
        out_specs=vm,
        out_shape=SDS((3, kvb, SPAN, GROUP * BLOCK), F32),
        name="bias_build",
    )(bucket_t, rel_bias)


def _attn_a_fwd(qa, ka, vat, tq, tk, shards):
    bl, ha, s_len, _ = qa.shape
    kv = ka.shape[1]
    va_rows = vat.shape[2]
    nq, nk = s_len // tq, s_len // tk
    assert nk % 2 == 0
    r = GROUP * tq
    ns = len(shards)

    def body(q_ref, qn_ref, k_ref, v_ref, *rest):
        shard_refs, (o_ref, l_ref), gathered = rest[:ns], rest[ns:ns + 2], rest[ns + 2:2 * ns + 2]
        st_sc, send_sems, recv_sems, local_sems = rest[2 * ns + 2:]
        i = pl.program_id(2)
        step_id = (pl.program_id(0) * kv + pl.program_id(1)) * nq + i
        start, wait = _direct_exchange("gather", shard_refs, gathered, send_sems, recv_sems, local_sems)
        pl.when(step_id == 0)(start)

        q = q_ref[0].reshape(r, HEAD_DIM)

        def scores(c, qv):
            return _dot_nt(k_ref[0, 0, pl.ds(pl.multiple_of(c * tk, tk), tk), :], qv)

        def fold(st, c, carry):
            m_old, acc = carry
            m_new = jnp.maximum(m_old, jnp.max(st, axis=0, keepdims=True))
            pt = jnp.exp(st - m_new).astype(BF16)
            vt = v_ref[0, 0, :, pl.ds(pl.multiple_of(c * tk, tk), tk)]
            return m_new, jnp.exp(m_old - m_new) * acc + _dot(vt, pt)

        @pl.when(i == 0)
        def _():
            st_sc[0] = scores(0, q)

        def step(c2, carry):
            c = 2 * c2
            st_sc[1] = scores(c + 1, q)
            carry = fold(st_sc[0], c, carry)
            st_sc[0] = scores(c + 2, q)
            return fold(st_sc[1], c + 1, carry)

        carry = (jnp.full((1, r), -jnp.inf, F32), jnp.zeros((va_rows, r), F32))
        for c2 in range(nk // 2 - 1):
            carry = step(c2, carry)
        st_sc[1] = scores(nk - 1, q)
        carry = fold(st_sc[0], nk - 2, carry)
        st_sc[0] = scores(0, qn_ref[0].reshape(r, HEAD_DIM))
        m, acc = fold(st_sc[1], nk - 1, carry)
        l = acc[HEAD_DIM:HEAD_DIM + 1, :]
        o = (acc[0:HEAD_DIM, :] / l).T
        for h in range(GROUP):
            o_ref[:, h * HEAD_DIM:(h + 1) * HEAD_DIM] = o[h * tq:(h + 1) * tq].astype(BF16)
        l_ref[0, 0, 0] = jnp.broadcast_to(m + jnp.log(l), (8, r))
        pl.when(step_id == bl * kv * nq - 1)(wait)

    anyspec = pl.BlockSpec(memory_space=pl.ANY)
    res = pl.pallas_call(
        body,
        grid=(bl, kv, nq),
        in_specs=[pl.BlockSpec((1, GROUP, tq, HEAD_DIM), lambda b, g, i: (b, g, i, 0)),
                  pl.BlockSpec((1, GROUP, tq, HEAD_DIM), lambda b, g, i: (b, g, jnp.minimum(i + 1, nq - 1), 0)),
                  pl.BlockSpec((1, 1, s_len, HEAD_DIM), lambda b, g, i: (b, g, 0, 0)),
                  pl.BlockSpec((1, 1, va_rows, s_len), lambda b, g, i: (b, g, 0, 0))] + [anyspec] * ns,
        out_specs=[pl.BlockSpec((tq, GROUP * HEAD_DIM), lambda b, g, i: (b * nq + i, g)),
                   pl.BlockSpec((1, 1, 1, 8, r), lambda b, g, i: (b, g, i, 0, 0))] + [anyspec] * ns,
        out_shape=[SDS((bl * s_len, ha * HEAD_DIM), BF16), SDS((bl, kv, nq, 8, r), F32)]
        + [SDS((N_DEV,) + s.shape, s.dtype) for s in shards],
        scratch_shapes=[pltpu.VMEM((2, tk, r), F32)] + _exchange_scratch(ns),
        compiler_params=_cp("arbitrary", "arbitrary", "arbitrary"),
        name="attn_a_fwd",
    )(qa, qa, ka, vat, *shards)
    return res[0], res[1], res[2:]


FFN_FWD_BLOCKS_PER_STEP = 8
FFN_BWD_BLOCKS_PER_STEP = 4
QB_PER_STEP = 8


def _bias_variant(n, nb):
    return jnp.where(n == 0, 1, jnp.where(n == nb - 1, 2, 0))


def _sink_row(sink_ref, g):
    return jnp.concatenate([jnp.full((1, BLOCK), sink_ref[0, g * GROUP + h], F32) for h in range(GROUP)], axis=1)


def _attn_b_fwd(qb, kb, vbt, bias_t, sink, s_len):
    bl, hb, _, _ = qb.shape
    kv = kb.shape[1]
    sp = kb.shape[2]
    vt_rows = vbt.shape[2]
    nb = s_len // BLOCK
    nbs = min(QB_PER_STEP, nb)
    r = GROUP * BLOCK

    def body(q_ref, k_ref, vt_ref, bt_ref, sink_ref, o_ref, l_ref, st_sc, pb_sc):
        g, n0 = pl.program_id(1), pl.program_id(2) * nbs
        sink_row = _sink_row(sink_ref, g)

        def span(j):
            return pl.ds(pl.multiple_of((n0 + j) * BLOCK, BLOCK), SPAN)

        for j in range(nbs):
            q = q_ref[0, :, j * BLOCK:(j + 1) * BLOCK, :].reshape(r, HEAD_DIM)
            st_sc[j] = _dot_nt(k_ref[0, 0, span(j), :], q) + bt_ref[_bias_variant(n0 + j, nb), 0]
        maxes = []
        for j in range(nbs):
            st = st_sc[j]
            m = jnp.maximum(jnp.max(st, axis=0, keepdims=True), sink_row)
            pb_sc[j] = jnp.exp(st - m).astype(BF16)
            maxes.append(m)
        for j in range(nbs):
            m = maxes[j]
            acc = _dot(vt_ref[0, 0, :, span(j)], pb_sc[j])
            l = acc[HEAD_DIM:HEAD_DIM + 1, :] + jnp.exp(sink_row - m)
            o = (acc[0:HEAD_DIM, :] / l).T
            for h in range(GROUP):
                o_ref[j * BLOCK:(j + 1) * BLOCK, h * HEAD_DIM:(h + 1) * HEAD_DIM] = (
                    o[h * BLOCK:(h + 1) * BLOCK].astype(BF16))
            l_ref[0, 0, j] = jnp.broadcast_to(m + jnp.log(l), (8, r))

    return pl.pallas_call(
        body,
        grid=(bl, kv, nb // nbs),
        in_specs=[pl.BlockSpec((1, GROUP, nbs * BLOCK, HEAD_DIM), lambda b, g, n: (b, g, n, 0)),
                  pl.BlockSpec((1, 1, sp, HEAD_DIM), lambda b, g, n: (b, g, 0, 0)),
                  pl.BlockSpec((1, 1, vt_rows, sp), lambda b, g, n: (b, g, 0, 0)),
                  pl.BlockSpec((3, 1, SPAN, r), lambda b, g, n: (0, g, 0, 0)),
                  pl.BlockSpec(memory_space=pltpu.SMEM)],
        out_specs=[pl.BlockSpec((nbs * BLOCK, GROUP * HEAD_DIM), lambda b, g, n: (b * (nb // nbs) + n, g)),
                   pl.BlockSpec((1, 1, nbs, 8, r), lambda b, g, n: (b, g, n, 0, 0))],
        out_shape=[SDS((bl * s_len, hb * HEAD_DIM), BF16), SDS((bl, kv, nb, 8, r), F32)],
        scratch_shapes=[pltpu.VMEM((nbs, SPAN, r), F32), pltpu.VMEM((nbs, SPAN, r), BF16)],
        compiler_params=_cp("parallel", "parallel", "arbitrary"),
        name="attn_b_fwd",
    )(qb, kb, vbt, bias_t, sink)


def _mixout(oa, ob, wo, x2, g2, g3, tm):
    t, d = x2.shape
    ca = oa.shape[1]

    def body(oa_ref, ob_ref, w_ref, x_ref, g2_ref, g3_ref, mix_ref, x1_ref, h2_ref):
        mix = _dot(oa_ref[...], w_ref[0:ca, :]) + _dot(ob_ref[...], w_ref[ca:, :])
        mix_ref[...] = mix
        y2, _, _ = _rms_fwd(mix, g2_ref[...])
        x1 = x_ref[...] + y2
        x1_ref[...] = x1
        y3, _, _ = _rms_fwd(x1, g3_ref[...])
        h2_ref[...] = y3.astype(BF16)

    tile = lambda w: pl.BlockSpec((tm, w), lambda i: (i, 0))
    vec = pl.BlockSpec((1, d), lambda i: (0, 0))
    return pl.pallas_call(
        body,
        grid=(t // tm,),
        in_specs=[tile(ca), tile(ob.shape[1]), pl.BlockSpec(wo.shape, lambda i: (0, 0)), tile(d), vec, vec],
        out_specs=[tile(d), tile(d), tile(d)],
        out_shape=[SDS((t, d), F32), SDS((t, d), F32), SDS((t, d), BF16)],
        compiler_params=_cp("parallel"),
        name="mixout",
    )(oa, ob, wo, x2, g2, g3)


def _ffn_fwd(h2, wup_g, wdn, x1, target, g4, tm, jb):
    t, d = x1.shape
    nblk, _, tf = wup_g.shape
    ff = nblk * tf
    nt = t // tm
    nj = nblk // jb

    def body(h_ref, wu_ref, wd_ref, x1_ref, tg_ref, g_ref, u_ref, df_ref, dy_ref, dg_ref, loss_ref, acc_sc):
        i, j = pl.program_id(0), pl.program_id(1)

        @pl.when(j == 0)
        def _():
            acc_sc[...] = jnp.zeros_like(acc_sc)

        @pl.when((i == 0) & (j == 0))
        def _():
            dg_ref[...] = jnp.zeros_like(dg_ref)
            loss_ref[...] = jnp.zeros_like(loss_ref)

        h = h_ref[...]
        squares = []
        for s in range(jb):
            u = jnp.maximum(_dot(h, wu_ref[s]), 0.0)
            u_ref[:, s * tf:(s + 1) * tf] = u.astype(BF16)
            squares.append((u * u).astype(BF16))
        acc_sc[...] += _dot(jnp.concatenate(squares, axis=1), wd_ref[...])

        @pl.when(j == nj - 1)
        def _():
            g = g_ref[...]
            y4, n, r = _rms_fwd(acc_sc[...], g)
            e = (x1_ref[...] + y4) - tg_ref[...]
            loss_ref[...] += jnp.sum(e * e) * (0.5 / d)
            dy = e * (1.0 / d)
            dy_ref[...] = dy
            df, dgt = _rms_bwd(n, r, g, dy)
            df_ref[...] = df.astype(BF16)
            dg_ref[0:1, :] += jnp.sum(dgt, axis=0, keepdims=True)

    tile = pl.BlockSpec((tm, d), lambda i, j: (i, 0))
    return pl.pallas_call(
        body,
        grid=(nt, nj),
        in_specs=[tile,
                  pl.BlockSpec((jb, d, tf), lambda i, j: (j, 0, 0)),
                  pl.BlockSpec((jb * tf, d), lambda i, j: (j, 0)),
                  tile, tile,
                  pl.BlockSpec((1, d), lambda i, j: (0, 0))],
        out_specs=[pl.BlockSpec((tm, jb * tf), lambda i, j: (i, j)), tile, tile,
                   pl.BlockSpec((8, d), lambda i, j: (0, 0)),
                   pl.BlockSpec((8, 128), lambda i, j: (0, 0))],
        out_shape=[SDS((t, ff), BF16), SDS((t, d), BF16), SDS((t, d), F32), SDS((8, d), F32), SDS((8, 128), F32)],
        scratch_shapes=[pltpu.VMEM((tm, d), F32)],
        compiler_params=_cp("arbitrary", "arbitrary"),
        name="ffn_fwd",
    )(h2, wup_g, wdn, x1, target, g4)


def _ffn_bwd(df, u, wdn, wup_g, x1, dy, mix, g3, g2, tm, jb):
    t, d = x1.shape
    nblk, _, tf = wup_g.shape
    nt = t // tm
    nj = nblk // jb

    def body(df_ref, u_ref, wd_ref, wu_ref, x1_ref, dy_ref, mix_ref, g3_ref, g2_ref,
             dpre_ref, dx1_ref, dmix_ref, dg3_ref, dg2_ref, acc_sc):
        i, j = pl.program_id(0), pl.program_id(1)

        @pl.when(j == 0)
        def _():
            acc_sc[...] = jnp.zeros_like(acc_sc)

        @pl.when((i == 0) & (j == 0))
        def _():
            dg3_ref[...] = jnp.zeros_like(dg3_ref)
            dg2_ref[...] = jnp.zeros_like(dg2_ref)

        du2 = _dot_nt(df_ref[...], wd_ref[...])
        dpre = (2.0 * u_ref[...].astype(F32) * du2).astype(BF16)
        dpre_ref[...] = dpre
        dh = _dot_nt(dpre[:, 0:tf], wu_ref[0])
        for s in range(1, jb):
            dh = dh + _dot_nt(dpre[:, s * tf:(s + 1) * tf], wu_ref[s])
        acc_sc[...] += dh

        @pl.when(j == nj - 1)
        def _():
            g3, g2 = g3_ref[...], g2_ref[...]
            _, n3, r3 = _rms_fwd(x1_ref[...], g3)
            dx, dgt3 = _rms_bwd(n3, r3, g3, acc_sc[...])
            dx1 = dy_ref[...] + dx
            dx1_ref[...] = dx1
            dg3_ref[0:1, :] += jnp.sum(dgt3, axis=0, keepdims=True)
            _, n2, r2 = _rms_fwd(mix_ref[...], g2)
            dmix, dgt2 = _rms_bwd(n2, r2, g2, dx1)
            dmix_ref[...] = dmix.astype(BF16)
            dg2_ref[0:1, :] += jnp.sum(dgt2, axis=0, keepdims=True)

    tile = pl.BlockSpec((tm, d), lambda i, j: (i, 0))
    vec = pl.BlockSpec((1, d), lambda i, j: (0, 0))
    acc8 = pl.BlockSpec((8, d), lambda i, j: (0, 0))
    return pl.pallas_call(
        body,
        grid=(nt, nj),
        in_specs=[tile,
                  pl.BlockSpec((tm, jb * tf), lambda i, j: (i, j)),
                  pl.BlockSpec((jb * tf, d), lambda i, j: (j, 0)),
                  pl.BlockSpec((jb, d, tf), lambda i, j: (j, 0, 0)),
                  tile, tile, tile, vec, vec],
        out_specs=[pl.BlockSpec((tm, jb * tf), lambda i, j: (i, j)), tile, tile, acc8, acc8],
        out_shape=[SDS(u.shape, BF16), SDS((t, d), F32), SDS((t, d), BF16), SDS((8, d), F32), SDS((8, d), F32)],
        scratch_shapes=[pltpu.VMEM((tm, d), F32)],
        compiler_params=_cp("arbitrary", "arbitrary"),
        name="ffn_bwd",
    )(df, u, wdn, wup_g, x1, dy, mix, g3, g2)


def _wgrad(a, b, a_spec, b_spec, out_block, out_shape, nj, nk, name, prep_a=None, prep_b=None):
    acc_shape = out_block[1:]

    def body(a_ref, b_ref, o_ref, acc_sc):
        k = pl.program_id(1)
        av = a_ref[...] if prep_a is None else prep_a(a_ref)
        bv = b_ref[...] if prep_b is None else prep_b(b_ref)
        part = _dot_tn(av, bv)

        @pl.when(k == 0)
        def _():
            acc_sc[...] = part

        @pl.when(k > 0)
        def _():
            acc_sc[...] += part

        @pl.when(k == nk - 1)
        def _():
            o_ref[0] = acc_sc[...].astype(BF16)

    return pl.pallas_call(
        body,
        grid=(nj, nk),
        in_specs=[a_spec, b_spec],
        out_specs=pl.BlockSpec(out_block, lambda j, k: (j, 0, 0)),
        out_shape=SDS(out_shape, BF16),
        scratch_shapes=[pltpu.VMEM(acc_shape, F32)],
        compiler_params=_cp("parallel", "arbitrary"),
        name=name,
    )(a, b)


def _wgrad_cols(a, b, nj, tt, name):
    t, m = a.shape
    bn = b.shape[1] // nj
    return _wgrad(a, b, pl.BlockSpec((tt, m), lambda j, k: (k, 0)), pl.BlockSpec((tt, bn), lambda j, k: (k, j)),
                  (1, m, bn), (nj, m, bn), nj, t // tt, name)


def _wgrad_rows_squared(a, b, nj, tt, name):
    t, n = b.shape
    bm = a.shape[1] // nj

    def square(a_ref):
        af = a_ref[...].astype(F32)
        return (af * af).astype(BF16)

    return _wgrad(a, b, pl.BlockSpec((tt, bm), lambda j, k: (k, j)), pl.BlockSpec((tt, n), lambda j, k: (k, 0)),
                  (1, bm, n), (nj, bm, n), nj, t // tt, name, prep_a=square)


def _wgrad_o(oa, ob, dmix, nj, tt):
    t, n = dmix.shape
    ca, cb = oa.shape[1], ob.shape[1]
    m = ca + cb
    nk = t // tt

    def body(oa_ref, ob_ref, b_ref, o_ref, acc_sc):
        k = pl.program_id(0)
        part = _dot_tn(jnp.concatenate([oa_ref[...], ob_ref[...]], axis=1), b_ref[...])

        @pl.when(k == 0)
        def _():
            acc_sc[...] = part

        @pl.when(k > 0)
        def _():
            acc_sc[...] += part

        @pl.when(k == nk - 1)
        def _():
            o_ref[...] = acc_sc[...].reshape(nj, m // nj, n).astype(BF16)

    return pl.pallas_call(
        body,
        grid=(nk,),
        in_specs=[pl.BlockSpec((tt, ca), lambda k: (k, 0)), pl.BlockSpec((tt, cb), lambda k: (k, 0)),
                  pl.BlockSpec((tt, n), lambda k: (k, 0))],
        out_specs=pl.BlockSpec((nj, m // nj, n), lambda k: (0, 0, 0)),
        out_shape=SDS((nj, m // nj, n), BF16),
        scratch_shapes=[pltpu.VMEM((m, n), F32)],
        compiler_params=_cp("arbitrary"),
        name="wgrad_o",
    )(oa, ob, dmix)


def _attn_out_bwd(dmix, wo, ca, tm):
    t, d = dmix.shape
    cb = wo.shape[0] - ca

    def body(dm_ref, w_ref, da_ref, db_ref):
        dm = dm_ref[...]
        da_ref[...] = _dot_nt(dm, w_ref[0:ca, :]).astype(BF16)
        db_ref[...] = _dot_nt(dm, w_ref[ca:, :]).astype(BF16)

    return pl.pallas_call(
        body,
        grid=(t // tm,),
        in_specs=[pl.BlockSpec((tm, d), lambda i: (i, 0)), pl.BlockSpec(wo.shape, lambda i: (0, 0))],
        out_specs=[pl.BlockSpec((tm, ca), lambda i: (i, 0)), pl.BlockSpec((tm, cb), lambda i: (i, 0))],
        out_shape=[SDS((t, ca), BF16), SDS((t, cb), BF16)],
        compiler_params=_cp("parallel"),
        name="attn_out_bwd",
    )(dmix, wo)


def _stack_heads(ref, rows):
    return jnp.concatenate([ref[:, h * HEAD_DIM:(h + 1) * HEAD_DIM] for h in range(GROUP)], axis=0)


def _attn_a_bwd(qa, ka, kat, va, do, o, lse, tq, tk, grads):
    bl, ha, s_len, _ = qa.shape
    kv = ka.shape[1]
    nq, nk = s_len // tq, s_len // tk
    assert nk % 2 == 0
    r = GROUP * tq
    ng = len(grads)

    def body(q_ref, qn_ref, k_ref, kt_ref, v_ref, do_ref, don_ref, o_ref, l_ref, *rest):
        grad_refs, (dq_ref, dk_ref, dv_ref), parts = rest[:ng], rest[ng:ng + 3], rest[ng + 3:2 * ng + 3]
        st_sc, dp_sc, dkt_sc, dvt_sc, send_sems, recv_sems, local_sems = rest[2 * ng + 3:]
        i = pl.program_id(2)
        step_id = (pl.program_id(0) * kv + pl.program_id(1)) * nq + i
        start, wait = _direct_exchange("scatter", grad_refs, parts, send_sems, recv_sems, local_sems)
        pl.when(step_id == 0)(start)

        q = q_ref[0].reshape(r, HEAD_DIM)
        do2 = _stack_heads(do_ref, tq)
        qt = q.astype(F32).T
        dot32 = do2.astype(F32).T
        ot32 = _stack_heads(o_ref, tq).astype(F32).T
        drow = jnp.sum(dot32 * ot32, axis=0, keepdims=True)
        qt, dot = qt.astype(BF16), dot32.astype(BF16)
        lrow = l_ref[0, 0, 0, 0:1, :]

        @pl.when(i == 0)
        def _():
            dkt_sc[...] = jnp.zeros_like(dkt_sc)
            dvt_sc[...] = jnp.zeros_like(dvt_sc)

        def chunk(c):
            return pl.ds(pl.multiple_of(c * tk, tk), tk)

        def scores(c, slot, qv=q, dov=do2):
            st_sc[slot] = _dot_nt(k_ref[0, 0, chunk(c), :], qv)
            dp_sc[slot] = _dot_nt(v_ref[0, 0, chunk(c), :], dov)

        def fold(slot, c, dqt):
            pt = jnp.exp(st_sc[slot] - lrow)
            dsb = (pt * (dp_sc[slot] - drow)).astype(BF16)
            dvt_sc[:, chunk(c)] += _dot_nt(dot, pt.astype(BF16))
            dkt_sc[:, chunk(c)] += _dot_nt(qt, dsb)
            return dqt + _dot(kt_ref[0, 0, :, chunk(c)], dsb)

        @pl.when(i == 0)
        def _():
            scores(0, 0)

        def step(c2, dqt):
            c = 2 * c2
            scores(c + 1, 1)
            dqt = fold(0, c, dqt)
            scores(c + 2, 0)
            return fold(1, c + 1, dqt)

        dqt = jnp.zeros((HEAD_DIM, r), F32)
        for c2 in range(nk // 2 - 1):
            dqt = step(c2, dqt)
        scores(nk - 1, 1)
        dqt = fold(0, nk - 2, dqt)
        scores(0, 0, qn_ref[0].reshape(r, HEAD_DIM), _stack_heads(don_ref, tq))
        dqt = fold(1, nk - 1, dqt)
        dq_ref[0] = dqt.T.reshape(GROUP, tq, HEAD_DIM)

        @pl.when(i == nq - 1)
        def _():
            dk_ref[0, 0] = dkt_sc[...].T
            dv_ref[0, 0] = dvt_sc[...].T

        pl.when(step_id == bl * kv * nq - 1)(wait)

    kvspec = pl.BlockSpec((1, 1, s_len, HEAD_DIM), lambda b, g, i: (b, g, 0, 0))
    qspec = pl.BlockSpec((1, GROUP, tq, HEAD_DIM), lambda b, g, i: (b, g, i, 0))
    tok = pl.BlockSpec((tq, GROUP * HEAD_DIM), lambda b, g, i: (b * nq + i, g))
    qnext = pl.BlockSpec((1, GROUP, tq, HEAD_DIM), lambda b, g, i: (b, g, jnp.minimum(i + 1, nq - 1), 0))
    toknext = pl.BlockSpec((tq, GROUP * HEAD_DIM), lambda b, g, i: (b * nq + jnp.minimum(i + 1, nq - 1), g))
    anyspec = pl.BlockSpec(memory_space=pl.ANY)
    res = pl.pallas_call(
        body,
        grid=(bl, kv, nq),
        in_specs=[qspec, qnext, kvspec, pl.BlockSpec((1, 1, HEAD_DIM, s_len), lambda b, g, i: (b, g, 0, 0)), kvspec,
                  tok, toknext, tok, pl.BlockSpec((1, 1, 1, 8, r), lambda b, g, i: (b, g, i, 0, 0))] + [anyspec] * ng,
        out_specs=[qspec, kvspec, kvspec] + [anyspec] * ng,
        out_shape=[SDS(qa.shape, F32), SDS(ka.shape, F32), SDS(va.shape, F32)]
        + [SDS(g.shape, g.dtype) for g in grads],
        scratch_shapes=[pltpu.VMEM((2, tk, r), F32), pltpu.VMEM((2, tk, r), F32),
                        pltpu.VMEM((HEAD_DIM, s_len), F32), pltpu.VMEM((HEAD_DIM, s_len), F32)]
        + _exchange_scratch(ng),
        compiler_params=_cp("arbitrary", "arbitrary", "arbitrary"),
        name="attn_a_bwd",
    )(qa, qa, ka, kat, va, do, do, o, lse, *grads)
    return res[0], res[1], res[2], res[3:]


def _attn_b_bwd(qb, kb, kbt, vb, do, o, lse, bias_t, sink, s_len):
    bl, hb, _, _ = qb.shape
    kv, sp = kb.shape[1], kb.shape[2]
    nb = s_len // BLOCK
    nbs = min(QB_PER_STEP, nb)
    r = GROUP * BLOCK

    def body(q_ref, k_ref, kt_ref, v_ref, do_ref, o_ref, l_ref, bt_ref, sink_ref,
             dq_ref, dk_ref, dv_ref, dsum_ref, dsink_ref, dkt_sc, dvt_sc):
        g, b, ns = pl.program_id(0), pl.program_id(1), pl.program_id(2)
        sink_row = _sink_row(sink_ref, g)

        @pl.when(ns == 0)
        def _():
            dkt_sc[...] = jnp.zeros_like(dkt_sc)
            dvt_sc[...] = jnp.zeros_like(dvt_sc)

        @pl.when((b == 0) & (ns == 0))
        def _():
            dsum_ref[...] = jnp.zeros_like(dsum_ref)
            dsink_ref[...] = jnp.zeros_like(dsink_ref)

        dsum = jnp.zeros((SPAN, r), F32)
        dsink = jnp.zeros((1, r), F32)
        for j in range(nbs):
            n = ns * nbs + j
            span = pl.ds(pl.multiple_of(n * BLOCK, BLOCK), SPAN)
            rows = slice(j * BLOCK, (j + 1) * BLOCK)
            q = q_ref[0, :, rows, :].reshape(r, HEAD_DIM)
            do2 = jnp.concatenate([do_ref[rows, h * HEAD_DIM:(h + 1) * HEAD_DIM] for h in range(GROUP)], axis=0)
            o2 = jnp.concatenate([o_ref[rows, h * HEAD_DIM:(h + 1) * HEAD_DIM] for h in range(GROUP)], axis=0)
            dot32 = do2.astype(F32).T
            drow = jnp.sum(dot32 * o2.astype(F32).T, axis=0, keepdims=True)
            qt, dot = q.astype(F32).T.astype(BF16), dot32.astype(BF16)
            lrow = l_ref[0, 0, j, 0:1, :]
            st = _dot_nt(k_ref[0, 0, span, :], q) + bt_ref[_bias_variant(n, nb), 0]
            pt = jnp.exp(st - lrow)
            dst = pt * (_dot_nt(v_ref[0, 0, span, :], do2) - drow)
            dsum = dsum + dst
            dsink = dsink - jnp.exp(sink_row - lrow) * drow
            dsb = dst.astype(BF16)
            dvt_sc[:, span] += _dot_nt(dot, pt.astype(BF16))
            dkt_sc[:, span] += _dot_nt(qt, dsb)
            dq_ref[0, :, rows, :] = _dot(kt_ref[0, 0, :, span], dsb).T.reshape(GROUP, BLOCK, HEAD_DIM)
        dsum_ref[0] += dsum
        dsink_ref[0, 0:1, :] += dsink

        @pl.when(ns == nb // nbs - 1)
        def _():
            dk_ref[0, 0] = dkt_sc[:, BLOCK:BLOCK + s_len].T
            dv_ref[0, 0] = dvt_sc[:, BLOCK:BLOCK + s_len].T

    kvspec = pl.BlockSpec((1, 1, sp, HEAD_DIM), lambda g, b, n: (b, g, 0, 0))
    kvout = pl.BlockSpec((1, 1, s_len, HEAD_DIM), lambda g, b, n: (b, g, 0, 0))
    qspec = pl.BlockSpec((1, GROUP, nbs * BLOCK, HEAD_DIM), lambda g, b, n: (b, g, n, 0))
    tok = pl.BlockSpec((nbs * BLOCK, GROUP * HEAD_DIM), lambda g, b, n: (b * (nb // nbs) + n, g))
    return pl.pallas_call(
        body,
        grid=(kv, bl, nb // nbs),
        in_specs=[qspec, kvspec, pl.BlockSpec((1, 1, HEAD_DIM, sp), lambda g, b, n: (b, g, 0, 0)), kvspec, tok, tok,
                  pl.BlockSpec((1, 1, nbs, 8, r), lambda g, b, n: (b, g, n, 0, 0)),
                  pl.BlockSpec((3, 1, SPAN, r), lambda g, b, n: (0, g, 0, 0)),
                  pl.BlockSpec(memory_space=pltpu.SMEM)],
        out_specs=[qspec, kvout, kvout,
                   pl.BlockSpec((1, SPAN, r), lambda g, b, n: (g, 0, 0)),
                   pl.BlockSpec((1, 8, r), lambda g, b, n: (g, 0, 0))],
        out_shape=[SDS(qb.shape, F32), SDS((bl, kv, s_len, HEAD_DIM), F32), SDS((bl, kv, s_len, HEAD_DIM), F32),
                   SDS((kv, SPAN, r), F32), SDS((kv, 8, r), F32)],
        scratch_shapes=[pltpu.VMEM((HEAD_DIM, sp), F32), pltpu.VMEM((HEAD_DIM, sp), F32)],
        compiler_params=_cp("arbitrary", "arbitrary", "arbitrary"),
        name="attn_b_bwd",
    )(qb, kb, kbt, vb, do, o, lse, bias_t, sink)


def _bias_reduce(dsum, dsink, bucket_t4):
    kv, _, r = dsum.shape

    def body(ds_ref, dk_ref, bk_ref, rel_ref, sink_ref):
        lane = lax.broadcasted_iota(jnp.int32, (N_BUCKETS, 128), 1)
        lane8 = lax.broadcasted_iota(jnp.int32, (8, 128), 1)
        bk = bk_ref[...]
        for g in range(kv):
            ds = ds_ref[g]
            rowi = lax.broadcasted_iota(jnp.int32, (N_BUCKETS, r), 0)
            red = jnp.zeros((N_BUCKETS, r), F32)
            for b in range(N_BUCKETS):
                red = jnp.where(rowi == b, jnp.sum(jnp.where(bk == b, ds, 0.0), axis=0, keepdims=True), red)
            out = jnp.zeros((N_BUCKETS, 128), F32)
            so = jnp.zeros((8, 128), F32)
            for h in range(GROUP):
                col = jnp.sum(red[:, h * BLOCK:(h + 1) * BLOCK], axis=1, keepdims=True)
                out = jnp.where(lane == h, col, out)
                sc = jnp.sum(dk_ref[g][:, h * BLOCK:(h + 1) * BLOCK], axis=1, keepdims=True)
                so = jnp.where(lane8 == h, sc, so)
            rel_ref[g] = out
            sink_ref[g] = so

    vm = pl.BlockSpec(memory_space=pltpu.VMEM)
    return pl.pallas_call(
        body,
        in_specs=[vm, vm, vm],
        out_specs=[vm, vm],
        out_shape=[SDS((kv, N_BUCKETS, 128), F32), SDS((kv, 8, 128), F32)],
        name="bias_reduce",
    )(dsum, dsink, bucket_t4)


def _dqkprep(dqa, dka, dva, dqb, dkb, dvb, proj, cos, sin_signed, gq, gk, s_len, ts):
    t, p_cols = proj.shape
    bl, ha = dqa.shape[0], dqa.shape[1]
    kva, hb, kvb = dka.shape[1], dqb.shape[1], dkb.shape[1]
    ns = s_len // ts

    def body(dqa_ref, dka_ref, dva_ref, dqb_ref, dkb_ref, dvb_ref, p_ref, cos_ref, sin_ref, gq_ref, gk_ref,
             dp_ref, dgq_ref, dgk_ref):
        b, i = pl.program_id(0), pl.program_id(1)
        cs, sn = cos_ref[...], sin_ref[...]
        low, first = _pair_masks(ts)

        @pl.when((b == 0) & (i == 0))
        def _():
            dgq_ref[...] = jnp.zeros_like(dgq_ref)
            dgk_ref[...] = jnp.zeros_like(dgk_ref)

        def grad_pair(ref, p):
            return jnp.concatenate([ref[0, 2 * p], ref[0, 2 * p + 1]], axis=1)

        def put(p, val):
            dp_ref[:, p * PAIR:(p + 1) * PAIR] = val.astype(BF16)

        def unrope_norm(d_rot, p, g, dg_ref):
            dn = d_rot * cs + _pair_partner(d_rot * sn, first)
            xp = p_ref[:, p * PAIR:(p + 1) * PAIR]
            r = lax.rsqrt(_pair_mean(xp * xp, low) + EPS)
            n = xp * r
            gd = g * dn
            dg_ref[0:1, :] += jnp.sum(dn * n, axis=0, keepdims=True)
            put(p, r * (gd - n * _pair_mean(n * gd, low)))

        for p in range(ha // 2):
            unrope_norm(grad_pair(dqa_ref, p) * SCALE, p, gq_ref[...], dgq_ref)
        base = ha // 2
        for p in range(kva // 2):
            unrope_norm(grad_pair(dka_ref, p), base + p, gk_ref[...], dgk_ref)
            put(base + kva // 2 + p, grad_pair(dva_ref, p))
        base += kva
        for p in range(hb // 2):
            put(base + p, grad_pair(dqb_ref, p) * SCALE)
        base += hb // 2
        for p in range(kvb // 2):
            put(base + p, grad_pair(dkb_ref, p))
            put(base + kvb // 2 + p, grad_pair(dvb_ref, p))

    def hm(nh):
        return pl.BlockSpec((1, nh, ts, HEAD_DIM), lambda b, i: (b, 0, i, 0))

    vec = pl.BlockSpec((1, PAIR), lambda b, i: (0, 0))
    tab = pl.BlockSpec((ts, PAIR), lambda b, i: (i, 0))
    acc = pl.BlockSpec((8, PAIR), lambda b, i: (0, 0))
    pspec = pl.BlockSpec((ts, p_cols), lambda b, i: (b * ns + i, 0))
    return pl.pallas_call(
        body,
        grid=(bl, ns),
        in_specs=[hm(ha), hm(kva), hm(kva), hm(hb), hm(kvb), hm(kvb), pspec, tab, tab, vec, vec],
        out_specs=[pspec, acc, acc],
        out_shape=[SDS((t, p_cols), BF16), SDS((8, PAIR), F32), SDS((8, PAIR), F32)],
        compiler_params=_cp("arbitrary", "arbitrary"),
        name="dqkprep",
    )(dqa, dka, dva, dqb, dkb, dvb, proj, cos, sin_signed, gq, gk)


def _dx_final(dproj, w, x2, dx1, g1, tm, grads):
    t, d = x2.shape
    p_cols = w.shape[1]
    ng = len(grads)
    nsteps = t // tm

    def body(dp_ref, w_ref, x_ref, dx1_ref, g_ref, *rest):
        grad_refs, (dx_ref, dg_ref), parts = rest[:ng], rest[ng:ng + 2], rest[ng + 2:2 * ng + 2]
        start, wait = _direct_exchange("scatter", grad_refs, parts, *rest[2 * ng + 2:])

        @pl.when(pl.program_id(0) == 0)
        def _():
            start()
            dg_ref[...] = jnp.zeros_like(dg_ref)

        dh = _dot_nt(dp_ref[...], w_ref[...])
        g = g_ref[...]
        _, n, r = _rms_fwd(x_ref[...], g)
        dx, dgt = _rms_bwd(n, r, g, dh)
        dx_ref[...] = dx1_ref[...] + dx
        dg_ref[0:1, :] += jnp.sum(dgt, axis=0, keepdims=True)
        pl.when(pl.program_id(0) == nsteps - 1)(wait)

    tile = pl.BlockSpec((tm, d), lambda i: (i, 0))
    anyspec = pl.BlockSpec(memory_space=pl.ANY)
    res = pl.pallas_call(
        body,
        grid=(nsteps,),
        in_specs=[pl.BlockSpec((tm, p_cols), lambda i: (i, 0)),
                  pl.BlockSpec((d, p_cols), lambda i: (0, 0)),
                  tile, tile, pl.BlockSpec((1, d), lambda i: (0, 0))] + [anyspec] * ng,
        out_specs=[tile, pl.BlockSpec((8, d), lambda i: (0, 0))] + [anyspec] * ng,
        out_shape=[SDS((t, d), F32), SDS((8, d), F32)] + [SDS(g.shape, g.dtype) for g in grads],
        scratch_shapes=_exchange_scratch(ng),
        compiler_params=_cp("arbitrary"),
        name="dx_final",
    )(dproj, w, x2, dx1, g1, *grads)
    return res[0], res[1], res[2:]


def _adamw_math(w, g, m, v):
    m = ADAM_B1 * m + (1.0 - ADAM_B1) * g
    v = ADAM_B2 * v + (1.0 - ADAM_B2) * (g * g)
    m_hat = m / (1.0 - ADAM_B1 ** ADAM_STEP)
    v_hat = v / (1.0 - ADAM_B2 ** ADAM_STEP)
    delta = -ADAM_LR * (m_hat / (jnp.sqrt(v_hat) + ADAM_EPS) + ADAM_WD * w)
    return delta, m, v


def _adamw_sum(parts, w, m, v, tr, name):
    rows, cols = w.shape

    def body(p_ref, w_ref, m_ref, v_ref, g_ref, d_ref, nm_ref, nv_ref):
        g = p_ref[0].astype(F32)
        for s in range(1, N_DEV):
            g = g + p_ref[s].astype(F32)
        g_ref[...] = g
        d_ref[...], nm_ref[...], nv_ref[...] = _adamw_math(w_ref[...], g, m_ref[...], v_ref[...])

    tr = min(tr, rows)
    tile = pl.BlockSpec((tr, cols), lambda i: (i, 0))
    return pl.pallas_call(
        body,
        grid=(rows // tr,),
        in_specs=[pl.BlockSpec((N_DEV, tr, cols), lambda i: (0, i, 0)), tile, tile, tile],
        out_specs=[tile] * 4,
        out_shape=[SDS((rows, cols), F32)] * 4,
        compiler_params=_cp("parallel"),
        name=name,
    )(parts, w, m, v)


def _adamw_small(vec, rel, ws, ms, vs):
    hb = ws[6].shape[1]
    n = len(ws)

    def body(vec_ref, rel_ref, *rest):
        w_refs, m_refs, v_refs = rest[:n], rest[n:2 * n], rest[2 * n:3 * n]
        loss_ref, outs = rest[3 * n], rest[3 * n + 1:]
        grads = [vec_ref[0:1, :], vec_ref[1:2, :], vec_ref[2:3, :], vec_ref[3:4, :],
                 vec_ref[4:5, 0:HEAD_DIM], vec_ref[4:5, SMALL_LANES:SMALL_LANES + HEAD_DIM],
                 vec_ref[4:5, 2 * SMALL_LANES:2 * SMALL_LANES + hb], rel_ref[:, 0:hb]]
        loss_ref[...] = vec_ref[4:5, 3 * SMALL_LANES:3 * SMALL_LANES + 1]
        for p, g in enumerate(grads):
            g_ref, d_ref, nm_ref, nv_ref = outs[4 * p:4 * p + 4]
            g_ref[...] = g
            d_ref[...], nm_ref[...], nv_ref[...] = _adamw_math(w_refs[p][...], g, m_refs[p][...], v_refs[p][...])

    vm = pl.BlockSpec(memory_space=pltpu.VMEM)
    res = pl.pallas_call(
        body,
        in_specs=[vm] * (2 + 3 * n),
        out_specs=[vm] * (1 + 4 * n),
        out_shape=[SDS((1, 1), F32)] + [SDS(w.shape, F32) for w in ws for _ in range(4)],
        name="adamw_small",
    )(vec, rel, *ws, *ms, *vs)
    return res[0], [res[1 + 4 * p:5 + 4 * p] for p in range(n)]


def _local_step(x, loss_target, win_g, wo_s, wup_s, wdn_s, g_pre_mix, g_post_mix, q_norm_a, k_norm_a, sink_b,
                rel_bias, g_pre_ffn, g_post_ffn):
    bl, s_len, d = x.shape
    t = bl * s_len
    nh = d // HEAD_DIM
    ha = nh // 2
    kva = ha // GROUP
    hb = nh - ha
    kvb = hb // GROUP
    tm = 512
    tw = min(4096, t)
    ts = min(512, s_len)
    tq, tk = 2 * BLOCK, min(512, s_len // 2)

    x2 = x.reshape(t, d)
    tg2 = loss_target.reshape(t, d)
    cos, sin_signed = _rope_tables(s_len)
    gq2, gk2 = jnp.tile(q_norm_a, (1, 2)), jnp.tile(k_norm_a, (1, 2))
    a = jnp.arange(BLOCK, dtype=jnp.int32)
    c = jnp.arange(SPAN, dtype=jnp.int32)
    bucket_t = _t5_bucket(c[:, None] - BLOCK - a[None, :])
    bucket_t4 = jnp.tile(bucket_t, (1, GROUP))
    w_in_full = jnp.transpose(win_g, (1, 0, 2)).reshape(d, -1)
    p_cols = w_in_full.shape[1]

    h1, proj = _inproj(x2, g_pre_mix, w_in_full, tm)
    qa, ka, kat, va, vat, qb, kb, kbt, vb, vbt = _qkprep(
        proj, cos, sin_signed, gq2, gk2, bl, s_len, ha, kva, hb, kvb, ts)
    bias_t = _bias_build(bucket_t, rel_bias, hb)
    oa, lse_a, (wo_g, wup_g, wdn_g) = _attn_a_fwd(qa, ka, vat, tq, tk, [wo_s, wup_s, wdn_s])
    wo = wo_g.reshape(-1, d)
    wdn = wdn_g.reshape(-1, d)
    ob, lse_b = _attn_b_fwd(qb, kb, vbt, bias_t, sink_b, s_len)
    mix, x1, h2 = _mixout(oa, ob, wo, x2, g_post_mix, g_pre_ffn, tm)
    u, df, dy, dg4, loss8 = _ffn_fwd(h2, wup_g, wdn, x1, tg2, g_post_ffn, tm, FFN_FWD_BLOCKS_PER_STEP)

    dpre, dx1, dmix, dg3, dg2 = _ffn_bwd(df, u, wdn, wup_g, x1, dy, mix, g_pre_ffn, g_post_mix, tm,
                                         FFN_BWD_BLOCKS_PER_STEP)
    gw_dn = _wgrad_rows_squared(u, df, N_DEV, tw, "wgrad_down")
    gw_up = _wgrad_cols(h2, dpre, N_DEV, tw, "wgrad_up")
    gw_o = _wgrad_o(oa, ob, dmix, N_DEV, min(2048, t))
    doa, dob = _attn_out_bwd(dmix, wo, oa.shape[1], tm)
    dqa, dka, dva, (p_o, p_up, p_dn) = _attn_a_bwd(qa, ka, kat, va, doa, oa, lse_a, tq, tk, [gw_o, gw_up, gw_dn])
    dqb, dkb, dvb, dsum, dsink = _attn_b_bwd(qb, kb, kbt, vb, dob, ob, lse_b, bias_t, sink_b, s_len)
    drel_g, dsink_g = _bias_reduce(dsum, dsink, bucket_t4)
    dproj, dgq, dgk = _dqkprep(dqa, dka, dva, dqb, dkb, dvb, proj, cos, sin_signed, gq2, gk2, s_len, ts)
    gw_in = _wgrad_cols(h1, dproj, p_cols // 256, tw, "wgrad_in")
    gw_in = jnp.transpose(jnp.transpose(gw_in, (1, 0, 2)).reshape(d, N_DEV, -1), (1, 0, 2))
    grad_x, dg1, (p_in,) = _dx_final(dproj, w_in_full, x2, dx1, g_pre_mix, tm, [gw_in])

    vec, rel = _small_allreduce([dg1, dg2, dg3, dg4], dgq, dgk, dsink_g, drel_g, loss8)
    return grad_x.reshape(bl, s_len, d), p_in, p_o, p_up, p_dn, vec, rel


def kernel(x, w_in, w_o, g_pre_mix, g_post_mix, q_norm_a, k_norm_a, sink_b, rel_bias, g_pre_ffn, w_ffn_up, w_ffn_down, g_post_ffn, loss_target, m_w_in, m_w_o, m_g_pre_mix, m_g_post_mix, m_q_norm_a, m_k_norm_a, m_sink_b, m_rel_bias, m_g_pre_ffn, m_w_ffn_up, m_w_ffn_down, m_g_post_ffn, v_w_in, v_w_o, v_g_pre_mix, v_g_post_mix, v_q_norm_a, v_k_norm_a, v_sink_b, v_rel_bias, v_g_pre_ffn, v_w_ffn_up, v_w_ffn_down, v_g_post_ffn):
    (win_g,) = _weight_gather([w_in[0].astype(BF16)])

    grad_x, p_in, p_o, p_up, p_dn, vec, rel = _local_step(
        x, loss_target, win_g, w_o[0].astype(BF16), w_ffn_up[0].astype(BF16), w_ffn_down[0].astype(BF16),
        g_pre_mix, g_post_mix, q_norm_a, k_norm_a, sink_b, rel_bias, g_pre_ffn, g_post_ffn)

    big = {
        "w_in": _adamw_sum(p_in, w_in[0], m_w_in[0], v_w_in[0], 256, "adamw_in"),
        "w_o": _adamw_sum(p_o, w_o[0], m_w_o[0], v_w_o[0], 128, "adamw_o"),
        "w_up": _adamw_sum(p_up, w_ffn_up[0], m_w_ffn_up[0], v_w_ffn_up[0], 256, "adamw_up"),
        "w_dn": _adamw_sum(p_dn, w_ffn_down[0], m_w_ffn_down[0], v_w_ffn_down[0], 256, "adamw_down"),
    }
    loss, small = _adamw_small(
        vec, rel,
        [g_pre_mix, g_post_mix, g_pre_ffn, g_post_ffn, q_norm_a, k_norm_a, sink_b, rel_bias],
        [m_g_pre_mix, m_g_post_mix, m_g_pre_ffn, m_g_post_ffn, m_q_norm_a, m_k_norm_a, m_sink_b, m_rel_bias],
        [v_g_pre_mix, v_g_post_mix, v_g_pre_ffn, v_g_post_ffn, v_q_norm_a, v_k_norm_a, v_sink_b, v_rel_bias])
    s_pre_mix, s_post_mix, s_pre_ffn, s_post_ffn, s_qn, s_kn, s_sink, s_rel = small

    def outs(kind):
        return [big["w_in"][kind][None], big["w_o"][kind][None], s_pre_mix[kind], s_post_mix[kind], s_qn[kind],
                s_kn[kind], s_sink[kind], s_rel[kind], s_pre_ffn[kind], big["w_up"][kind][None],
                big["w_dn"][kind][None], s_post_ffn[kind]]

    return (loss.reshape(()), grad_x, *outs(0), *outs(1), *outs(2), *outs(3))
```

```python
import functools

import jax
import jax.numpy as jnp
import numpy as np
from jax import lax
from jax.experimental import pallas as pl
from jax.experimental.pallas import tpu as pltpu

F32 = jnp.float32
BF16 = jnp.bfloat16
SDS = jax.ShapeDtypeStruct

N_DEV = 8
HEAD_DIM = 64
GROUP = 4
BLOCK = 128
SPAN = 3 * BLOCK
GRID_W = 64
N_BUCKETS = 32
MAX_DISTANCE = 128
ROPE_THETA = 10000.0
EPS = 1e-6
NEG_INF = -1e30
SCALE = HEAD_DIM ** -0.5
VT_PAD = 16

ADAM_LR = 0.001
ADAM_B1 = 0.9
ADAM_B2 = 0.999
ADAM_EPS = 1e-08
ADAM_WD = 0.01
ADAM_STEP = 10

VMEM_LIMIT = 56 * 1024 * 1024
MESH = pl.DeviceIdType.MESH


def _cp(*sem):
    return pltpu.CompilerParams(dimension_semantics=sem, vmem_limit_bytes=VMEM_LIMIT)


def _dot(a, b):
    return jnp.dot(a, b, preferred_element_type=F32)


def _dot_nt(a, b):
    return lax.dot_general(a, b, (((1,), (1,)), ((), ())), preferred_element_type=F32)


def _dot_tn(a, b):
    return lax.dot_general(a, b, (((0,), (0,)), ((), ())), preferred_element_type=F32)


def _rms_fwd(x, g):
    r = lax.rsqrt(jnp.mean(x * x, axis=-1, keepdims=True) + EPS)
    n = x * r
    return n * g, n, r


def _rms_bwd(n, r, g, dy):
    gd = g * dy
    dx = r * (gd - n * jnp.mean(n * gd, axis=-1, keepdims=True))
    return dx, dy * n


def _rope_tables(s_len):
    rows = s_len // GRID_W
    row = np.repeat(np.arange(rows, dtype=np.int32), GRID_W)
    col = np.tile(np.arange(GRID_W, dtype=np.int32), rows)
    nf = HEAD_DIM // 4
    freqs = np.float32(ROPE_THETA) ** (-np.arange(nf, dtype=np.float32) / np.float32(nf))
    ang_r = row.astype(np.float32)[:, None] * freqs[None, :]
    ang_c = col.astype(np.float32)[:, None] * freqs[None, :]
    cr, sr, cc, sc = np.cos(ang_r), np.sin(ang_r), np.cos(ang_c), np.sin(ang_c)
    cos = np.concatenate([cr, cr, cc, cc] * 2, axis=-1).astype(np.float32)
    sin_signed = np.concatenate([-sr, sr, -sc, sc] * 2, axis=-1).astype(np.float32)
    return jnp.asarray(cos), jnp.asarray(sin_signed)


def _t5_bucket(rel):
    nb = N_BUCKETS // 2
    ret = (rel > 0).astype(jnp.int32) * nb
    n = jnp.abs(rel)
    max_exact = nb // 2
    nf = jnp.maximum(n, 1).astype(F32)
    large = max_exact + (jnp.log(nf / max_exact) / np.float32(np.log(MAX_DISTANCE / max_exact))
                         * (nb - max_exact)).astype(jnp.int32)
    large = jnp.minimum(large, nb - 1)
    return ret + jnp.where(n < max_exact, n, large)


def _mesh_pos():
    return lax.axis_index("x"), lax.axis_index("y"), lax.axis_index("c")


def _lin(p):
    return 4 * p[0] + 2 * p[1] + p[2]


def _weight_gather(shards):
    n = len(shards)

    def body(*refs):
        xs, outs = refs[:n], refs[n:2 * n]
        send_sems, recv_sems, local_sems = refs[2 * n:]
        x, y, c = _mesh_pos()
        me, sibling = (x, y, c), (x, y, 1 - c)
        chips = [(1 - x, y), (x, 1 - y), (1 - x, 1 - y)]

        def copy(a, k, block, to, src=None):
            slot = outs[a].at[_lin(block)]
            return pltpu.make_async_remote_copy(
                src_ref=slot if src is None else src, dst_ref=slot,
                send_sem=send_sems.at[a, k], recv_sem=recv_sems.at[a, k],
                device_id=to, device_id_type=MESH)

        started = []
        for a in range(n):
            mine = pltpu.make_async_copy(xs[a], outs[a].at[_lin(me)], local_sems.at[a])
            mine.start()
            started.append(mine)
        sends = []
        for a in range(n):
            first = [copy(a, 0, me, sibling, src=xs[a])]
            first += [copy(a, 1 + j, me, (*chip, c), src=xs[a]) for j, chip in enumerate(chips)]
            for cp in first:
                cp.start()
            sends += first
        for a in range(n):
            for j, chip in enumerate(chips):
                copy(a, 1 + j, (*chip, c), me).wait_recv()
                fwd = copy(a, 4 + j, (*chip, c), sibling)
                fwd.start()
                sends.append(fwd)
        for a in range(n):
            copy(a, 0, sibling, me).wait_recv()
            for j, chip in enumerate(chips):
                copy(a, 4 + j, (*chip, 1 - c), me).wait_recv()
        for cp in sends:
            cp.wait_send()
        for mine in started:
            mine.wait()

    anyspec = pl.BlockSpec(memory_space=pl.ANY)
    return pl.pallas_call(
        body,
        out_shape=[SDS((N_DEV,) + s.shape, s.dtype) for s in shards],
        in_specs=[anyspec] * n,
        out_specs=[anyspec] * n,
        scratch_shapes=[pltpu.SemaphoreType.DMA((n, 7)), pltpu.SemaphoreType.DMA((n, 7)),
                        pltpu.SemaphoreType.DMA((n,))],
        name="weight_gather",
    )(*shards)


def _direct_exchange(kind, ins, outs, send_sems, recv_sems, local_sems):
    x, y, c = _mesh_pos()
    me = (x, y, c)
    peers = [(x, y, 1 - c), (1 - x, y, c), (x, 1 - y, c), (1 - x, 1 - y, c),
             (1 - x, y, 1 - c), (x, 1 - y, 1 - c), (1 - x, 1 - y, 1 - c)]

    def src(a, to):
        return ins[a] if kind == "gather" else ins[a].at[_lin(to)]

    def remote(a, k, to, frm):
        return pltpu.make_async_remote_copy(
            src_ref=src(a, to), dst_ref=outs[a].at[_lin(frm)],
            send_sem=send_sems.at[a, k], recv_sem=recv_sems.at[a, k],
            device_id=to, device_id_type=MESH)

    n = len(ins)
    sends = [remote(a, k, p, me) for a in range(n) for k, p in enumerate(peers)]
    arrivals = [remote(a, k, p, p) for a in range(n) for k, p in enumerate(peers)]
    local = [pltpu.make_async_copy(src(a, me), outs[a].at[_lin(me)], local_sems.at[a]) for a in range(n)]

    def start():
        for cp in local + sends:
            cp.start()

    def wait():
        for cp in arrivals:
            cp.wait_recv()
        for cp in sends:
            cp.wait_send()
        for cp in local:
            cp.wait()

    return start, wait


def _exchange_scratch(n):
    return [pltpu.SemaphoreType.DMA((n, 7)), pltpu.SemaphoreType.DMA((n, 7)), pltpu.SemaphoreType.DMA((n,))]


SMALL_LANES = 128


def _small_allreduce(dg_rows, dgq, dgk, dsink_g, drel_g, loss8):
    d = dg_rows[0].shape[1]
    kv = dsink_g.shape[0]

    def body(g1_ref, g2_ref, g3_ref, g4_ref, gq_ref, gk_ref, sk_ref, rl_ref, ls_ref, vec_ref, rel_ref,
             vbuf, rbuf, vland, rland, send_sems, recv_sems):
        x, y, c = _mesh_pos()
        me = (x, y, c)
        peers = [(x, y, 1 - c), (1 - x, y, c), (x, 1 - y, c), (1 - x, 1 - y, c),
                 (1 - x, y, 1 - c), (x, 1 - y, 1 - c), (1 - x, 1 - y, 1 - c)]
        vbuf[...] = jnp.zeros_like(vbuf)
        rbuf[...] = jnp.zeros_like(rbuf)
        for row, ref in enumerate((g1_ref, g2_ref, g3_ref, g4_ref)):
            vbuf[row:row + 1, :] = ref[0:1, :]
        vbuf[4:5, 0:HEAD_DIM] = gq_ref[0:1, 0:HEAD_DIM] + gq_ref[0:1, HEAD_DIM:PAIR]
        vbuf[4:5, SMALL_LANES:SMALL_LANES + HEAD_DIM] = gk_ref[0:1, 0:HEAD_DIM] + gk_ref[0:1, HEAD_DIM:PAIR]
        for g in range(kv):
            vbuf[4:5, 2 * SMALL_LANES + g * GROUP:2 * SMALL_LANES + (g + 1) * GROUP] = sk_ref[g, 0:1, 0:GROUP]
            rbuf[:, g * GROUP:(g + 1) * GROUP] = rl_ref[g, :, 0:GROUP]
        vbuf[4:5, 3 * SMALL_LANES:3 * SMALL_LANES + 1] = ls_ref[0:1, 0:1]

        def copies(k, to, frm):
            return [pltpu.make_async_remote_copy(
                src_ref=buf, dst_ref=land.at[_lin(frm)], send_sem=send_sems.at[a, k], recv_sem=recv_sems.at[a, k],
                device_id=to, device_id_type=MESH) for a, (buf, land) in enumerate(((vbuf, vland), (rbuf, rland)))]

        sends = [cp for k, p in enumerate(peers) for cp in copies(k, p, me)]
        for cp in sends:
            cp.start()
        vland[_lin(me)] = vbuf[...]
        rland[_lin(me)] = rbuf[...]
        for k, p in enumerate(peers):
            for cp in copies(k, p, p):
                cp.wait_recv()
        for cp in sends:
            cp.wait_send()
        vacc, racc = vland[0], rland[0]
        for s in range(1, N_DEV):
            vacc, racc = vacc + vland[s], racc + rland[s]
        vec_ref[...] = vacc
        rel_ref[...] = racc

    vm = pl.BlockSpec(memory_space=pltpu.VMEM)
    return pl.pallas_call(
        body,
        out_shape=[SDS((8, d), F32), SDS((N_BUCKETS, 128), F32)],
        in_specs=[vm] * 9,
        out_specs=[vm, vm],
        scratch_shapes=[pltpu.VMEM((8, d), F32), pltpu.VMEM((N_BUCKETS, 128), F32),
                        pltpu.VMEM((N_DEV, 8, d), F32), pltpu.VMEM((N_DEV, N_BUCKETS, 128), F32),
                        pltpu.SemaphoreType.DMA((2, 7)), pltpu.SemaphoreType.DMA((2, 7))],
        name="small_allreduce",
    )(*dg_rows, dgq, dgk, dsink_g, drel_g, loss8)


def _inproj(x2, g1, w_t, tm):
    t, d = x2.shape
    p = w_t.shape[0]

    def body(x_ref, g_ref, w_ref, h_ref, p_ref):
        y, _, _ = _rms_fwd(x_ref[...], g_ref[...])
        h = y.astype(BF16)
        h_ref[...] = h
        p_ref[...] = _dot_nt(h, w_ref[...])

    return pl.pallas_call(
        body,
        grid=(t // tm,),
        in_specs=[pl.BlockSpec((tm, d), lambda i: (i, 0)),
                  pl.BlockSpec((1, d), lambda i: (0, 0)),
                  pl.BlockSpec((p, d), lambda i: (0, 0))],
        out_specs=[pl.BlockSpec((tm, d), lambda i: (i, 0)),
                   pl.BlockSpec((tm, p), lambda i: (i, 0))],
        out_shape=[SDS((t, d), BF16), SDS((t, p), F32)],
        compiler_params=_cp("parallel"),
        name="inproj",
    )(x2, g1, w_t)


PAIR = 2 * HEAD_DIM


def _pair_masks(ts):
    lane = lax.broadcasted_iota(jnp.int32, (ts, PAIR), 1)
    return lane < HEAD_DIM, (lane % 32) < 16


def _pair_mean(v, low):
    lo = jnp.sum(jnp.where(low, v, 0.0), axis=1, keepdims=True)
    hi = jnp.sum(jnp.where(low, 0.0, v), axis=1, keepdims=True)
    return jnp.where(low, lo, hi) * (1.0 / HEAD_DIM)


def _pair_partner(v, first):
    return jnp.where(first, pltpu.roll(v, PAIR - 16, 1), pltpu.roll(v, 16, 1))


def _qkprep(proj, cos, sin_signed, gq, gk, bl, s_len, ha, kva, hb, kvb, ts):
    t, p_cols = proj.shape
    assert ha % 2 == 0 and kva % 2 == 0 and hb % 2 == 0 and kvb % 2 == 0
    ns = s_len // ts
    sp = s_len + 2 * BLOCK

    def body(p_ref, cos_ref, sin_ref, gq_ref, gk_ref, qa_ref, ka_ref, kat_ref, va_ref, vat_ref, qb_ref, kb_ref,
             kbt_ref, vb_ref, vbt_ref):
        i = pl.program_id(1)
        cs, sn = cos_ref[...], sin_ref[...]
        low, first = _pair_masks(ts)
        ones_row = (lax.broadcasted_iota(jnp.int32, (VT_PAD, ts), 0) == 0).astype(BF16)
        heads = (slice(0, HEAD_DIM), slice(HEAD_DIM, PAIR))

        def pair(p):
            return p_ref[:, p * PAIR:(p + 1) * PAIR]

        def normrope(x, g):
            y = x * lax.rsqrt(_pair_mean(x * x, low) + EPS) * g
            return y * cs + _pair_partner(y, first) * sn

        def transposed(xb):
            return xb.astype(F32).T.astype(BF16)

        for p in range(ha // 2):
            q = (normrope(pair(p), gq_ref[...]) * SCALE).astype(BF16)
            for e, lanes in enumerate(heads):
                qa_ref[0, 2 * p + e] = q[:, lanes]
        base = ha // 2
        for p in range(kva // 2):
            k = normrope(pair(base + p), gk_ref[...]).astype(BF16)
            v = pair(base + kva // 2 + p).astype(BF16)
            kt, vt = transposed(k), transposed(v)
            for e, lanes in enumerate(heads):
                ka_ref[0, 2 * p + e] = k[:, lanes]
                va_ref[0, 2 * p + e] = v[:, lanes]
                kat_ref[0, 2 * p + e] = kt[lanes, :]
                vat_ref[0, 2 * p + e, 0:HEAD_DIM, :] = vt[lanes, :]
                vat_ref[0, 2 * p + e, HEAD_DIM:HEAD_DIM + VT_PAD, :] = ones_row
        base += kva
        for p in range(hb // 2):
            q = (pair(base + p) * SCALE).astype(BF16)
            for e, lanes in enumerate(heads):
                qb_ref[0, 2 * p + e] = q[:, lanes]
        base += hb // 2

        @pl.when(i == 0)
        def _():
            zeros = jnp.zeros((kvb, BLOCK, HEAD_DIM), BF16)
            zeros_t = jnp.zeros((kvb, HEAD_DIM + VT_PAD, BLOCK), BF16)
            for ref in (kb_ref, vb_ref):
                ref[0, :, 0:BLOCK, :] = zeros
                ref[0, :, sp - BLOCK:sp, :] = zeros
            kbt_ref[0, :, :, 0:BLOCK] = zeros_t[:, 0:HEAD_DIM]
            kbt_ref[0, :, :, sp - BLOCK:sp] = zeros_t[:, 0:HEAD_DIM]
            vbt_ref[0, :, :, 0:BLOCK] = zeros_t
            vbt_ref[0, :, :, sp - BLOCK:sp] = zeros_t

        rows = pl.ds(pl.multiple_of(BLOCK + i * ts, BLOCK), ts)
        for p in range(kvb // 2):
            k = pair(base + p).astype(BF16)
            v = pair(base + kvb // 2 + p).astype(BF16)
            kt, vt = transposed(k), transposed(v)
            for e, lanes in enumerate(heads):
                kb_ref[0, 2 * p + e, rows, :] = k[:, lanes]
                vb_ref[0, 2 * p + e, rows, :] = v[:, lanes]
                kbt_ref[0, 2 * p + e, :, rows] = kt[lanes, :]
                vbt_ref[0, 2 * p + e, 0:HEAD_DIM, rows] = vt[lanes, :]
                vbt_ref[0, 2 * p + e, HEAD_DIM:HEAD_DIM + VT_PAD, rows] = ones_row

    def hm(nh):
        return pl.BlockSpec((1, nh, ts, HEAD_DIM), lambda b, i: (b, 0, i, 0))

    def padded(nh):
        return pl.BlockSpec((1, nh, sp, HEAD_DIM), lambda b, i: (b, 0, 0, 0))

    def padded_t(nh, rows):
        return pl.BlockSpec((1, nh, rows, sp), lambda b, i: (b, 0, 0, 0))

    return pl.pallas_call(
        body,
        grid=(bl, ns),
        in_specs=[pl.BlockSpec((ts, p_cols), lambda b, i: (b * ns + i, 0)),
                  pl.BlockSpec((ts, PAIR), lambda b, i: (i, 0)),
                  pl.BlockSpec((ts, PAIR), lambda b, i: (i, 0)),
                  pl.BlockSpec((1, PAIR), lambda b, i: (0, 0)),
                  pl.BlockSpec((1, PAIR), lambda b, i: (0, 0))],
        out_specs=[hm(ha), hm(kva), pl.BlockSpec((1, kva, HEAD_DIM, ts), lambda b, i: (b, 0, 0, i)), hm(kva),
                   pl.BlockSpec((1, kva, HEAD_DIM + VT_PAD, ts), lambda b, i: (b, 0, 0, i)),
                   hm(hb), padded(kvb), padded_t(kvb, HEAD_DIM), padded(kvb), padded_t(kvb, HEAD_DIM + VT_PAD)],
        out_shape=[SDS((bl, ha, s_len, HEAD_DIM), BF16), SDS((bl, kva, s_len, HEAD_DIM), BF16),
                   SDS((bl, kva, HEAD_DIM, s_len), BF16),
                   SDS((bl, kva, s_len, HEAD_DIM), BF16), SDS((bl, kva, HEAD_DIM + VT_PAD, s_len), BF16),
                   SDS((bl, hb, s_len, HEAD_DIM), BF16),
                   SDS((bl, kvb, sp, HEAD_DIM), BF16), SDS((bl, kvb, HEAD_DIM, sp), BF16),
                   SDS((bl, kvb, sp, HEAD_DIM), BF16), SDS((bl, kvb, HEAD_DIM + VT_PAD, sp), BF16)],
        compiler_params=_cp("parallel", "arbitrary"),
        name="qkprep",
    )(proj, cos, sin_signed, gq, gk)


def _bias_build(bucket_t, rel_bias, hb):
    kvb = hb // GROUP

    def body(bkt_ref, tbl_ref, out_ref):
        bkt = bkt_ref[...]
        ci = lax.broadcasted_iota(jnp.int32, (SPAN, BLOCK), 0)
        qi = lax.broadcasted_iota(jnp.int32, (SPAN, BLOCK), 1)
        band = jnp.abs(ci - BLOCK - qi) <= BLOCK
        masks = (band, band & (ci >= BLOCK), band & (ci < 2 * BLOCK))
        for h in range(hb):
            acct = jnp.zeros((SPAN, BLOCK), F32)
            for b in range(N_BUCKETS):
                acct = jnp.where(bkt == b, tbl_ref[b, h], acct)
            lanes = slice((h % GROUP) * BLOCK, (h % GROUP + 1) * BLOCK)
            for var, mask in enumerate(masks):
                out_ref[var, h // GROUP, :, lanes] = jnp.where(mask, acct, NEG_INF)

    vm = pl.BlockSpec(memory_space=pltpu.VMEM)
    return pl.pallas_call(
        body,
        in_specs=[vm, pl.BlockSpec(memory_space=pltpu.SMEM)],
        out_specs=vm,
        out_shape=SDS((3, kvb, SPAN, GROUP * BLOCK), F32),
        name="bias_build",
    )(bucket_t, rel_bias)


def _attn_a_fwd(qa, ka, vat, tq, tk, shards):
    bl, ha, s_len, _ = qa.shape
    kv = ka.shape[1]
    va_rows = vat.shape[2]
    nq, nk = s_len // tq, s_len // tk
    assert nk % 2 == 0
    r = GROUP * tq
    ns = len(shards)

    def body(q_ref, qn_ref, k_ref, v_ref, *rest):
        shard_refs, (o_ref, l_ref), gathered = rest[:ns], rest[ns:ns + 2], rest[ns + 2:2 * ns + 2]
        st_sc, send_sems, recv_sems, local_sems = rest[2 * ns + 2:]
        i = pl.program_id(2)
        step_id = (pl.program_id(0) * kv + pl.program_id(1)) * nq + i
        start, wait = _direct_exchange("gather", shard_refs, gathered, send_sems, recv_sems, local_sems)
        pl.when(step_id == 0)(start)

        q = q_ref[0].reshape(r, HEAD_DIM)

        def scores(c, qv):
            return _dot_nt(k_ref[0, 0, pl.ds(pl.multiple_of(c * tk, tk), tk), :], qv)

        def fold(st, c, carry):
            m_old, acc = carry
            m_new = jnp.maximum(m_old, jnp.max(st, axis=0, keepdims=True))
            pt = jnp.exp(st - m_new).astype(BF16)
            vt = v_ref[0, 0, :, pl.ds(pl.multiple_of(c * tk, tk), tk)]
            return m_new, jnp.exp(m_old - m_new) * acc + _dot(vt, pt)

        @pl.when(i == 0)
        def _():
            st_sc[0] = scores(0, q)

        def step(c2, carry):
            c = 2 * c2
            st_sc[1] = scores(c + 1, q)
            carry = fold(st_sc[0], c, carry)
            st_sc[0] = scores(c + 2, q)
            return fold(st_sc[1], c + 1, carry)

        carry = (jnp.full((1, r), -jnp.inf, F32), jnp.zeros((va_rows, r), F32))
        for c2 in range(nk // 2 - 1):
            carry = step(c2, carry)
        st_sc[1] = scores(nk - 1, q)
        carry = fold(st_sc[0], nk - 2, carry)
        st_sc[0] = scores(0, qn_ref[0].reshape(r, HEAD_DIM))
        m, acc = fold(st_sc[1], nk - 1, carry)
        l = acc[HEAD_DIM:HEAD_DIM + 1, :]
        o = (acc[0:HEAD_DIM, :] / l).T
        for h in range(GROUP):
            o_ref[:, h * HEAD_DIM:(h + 1) * HEAD_DIM] = o[h * tq:(h + 1) * tq].astype(BF16)
        l_ref[0, 0, 0] = jnp.broadcast_to(m + jnp.log(l), (8, r))
        pl.when(step_id == bl * kv * nq - 1)(wait)

    anyspec = pl.BlockSpec(memory_space=pl.ANY)
    res = pl.pallas_call(
        body,
        grid=(bl, kv, nq),
        in_specs=[pl.BlockSpec((1, GROUP, tq, HEAD_DIM), lambda b, g, i: (b, g, i, 0)),
                  pl.BlockSpec((1, GROUP, tq, HEAD_DIM), lambda b, g, i: (b, g, jnp.minimum(i + 1, nq - 1), 0)),
                  pl.BlockSpec((1, 1, s_len, HEAD_DIM), lambda b, g, i: (b, g, 0, 0)),
                  pl.BlockSpec((1, 1, va_rows, s_len), lambda b, g, i: (b, g, 0, 0))] + [anyspec] * ns,
        out_specs=[pl.BlockSpec((tq, GROUP * HEAD_DIM), lambda b, g, i: (b * nq + i, g)),
                   pl.BlockSpec((1, 1, 1, 8, r), lambda b, g, i: (b, g, i, 0, 0))] + [anyspec] * ns,
        out_shape=[SDS((bl * s_len, ha * HEAD_DIM), BF16), SDS((bl, kv, nq, 8, r), F32)]
        + [SDS((N_DEV,) + s.shape, s.dtype) for s in shards],
        scratch_shapes=[pltpu.VMEM((2, tk, r), F32)] + _exchange_scratch(ns),
        compiler_params=_cp("arbitrary", "arbitrary", "arbitrary"),
        name="attn_a_fwd",
    )(qa, qa, ka, vat, *shards)
    return res[0], res[1], res[2:]


FFN_FWD_BLOCKS_PER_STEP = 8
FFN_BWD_BLOCKS_PER_STEP = 4
QB_PER_STEP = 8


def _bias_variant(n, nb):
    return jnp.where(n == 0, 1, jnp.where(n == nb - 1, 2, 0))


def _sink_row(sink_ref, g):
    return jnp.concatenate([jnp.full((1, BLOCK), sink_ref[0, g * GROUP + h], F32) for h in range(GROUP)], axis=1)


def _attn_b_fwd(qb, kb, vbt, bias_t, sink, s_len):
    bl, hb, _, _ = qb.shape
    kv = kb.shape[1]
    sp = kb.shape[2]
    vt_rows = vbt.shape[2]
    nb = s_len // BLOCK
    nbs = min(QB_PER_STEP, nb)
    r = GROUP * BLOCK

    def body(q_ref, k_ref, vt_ref, bt_ref, sink_ref, o_ref, l_ref, st_sc, pb_sc):
        g, n0 = pl.program_id(1), pl.program_id(2) * nbs
        sink_row = _sink_row(sink_ref, g)

        def span(j):
            return pl.ds(pl.multiple_of((n0 + j) * BLOCK, BLOCK), SPAN)

        for j in range(nbs):
            q = q_ref[0, :, j * BLOCK:(j + 1) * BLOCK, :].reshape(r, HEAD_DIM)
            st_sc[j] = _dot_nt(k_ref[0, 0, span(j), :], q) + bt_ref[_bias_variant(n0 + j, nb), 0]
        maxes = []
        for j in range(nbs):
            st = st_sc[j]
            m = jnp.maximum(jnp.max(st, axis=0, keepdims=True), sink_row)
            pb_sc[j] = jnp.exp(st - m).astype(BF16)
            maxes.append(m)
        for j in range(nbs):
            m = maxes[j]
            acc = _dot(vt_ref[0, 0, :, span(j)], pb_sc[j])
            l = acc[HEAD_DIM:HEAD_DIM + 1, :] + jnp.exp(sink_row - m)
            o = (acc[0:HEAD_DIM, :] / l).T
            for h in range(GROUP):
                o_ref[j * BLOCK:(j + 1) * BLOCK, h * HEAD_DIM:(h + 1) * HEAD_DIM] = (
                    o[h * BLOCK:(h + 1) * BLOCK].astype(BF16))
            l_ref[0, 0, j] = jnp.broadcast_to(m + jnp.log(l), (8, r))

    return pl.pallas_call(
        body,
        grid=(bl, kv, nb // nbs),
        in_specs=[pl.BlockSpec((1, GROUP, nbs * BLOCK, HEAD_DIM), lambda b, g, n: (b, g, n, 0)),
                  pl.BlockSpec((1, 1, sp, HEAD_DIM), lambda b, g, n: (b, g, 0, 0)),
                  pl.BlockSpec((1, 1, vt_rows, sp), lambda b, g, n: (b, g, 0, 0)),
                  pl.BlockSpec((3, 1, SPAN, r), lambda b, g, n: (0, g, 0, 0)),
                  pl.BlockSpec(memory_space=pltpu.SMEM)],
        out_specs=[pl.BlockSpec((nbs * BLOCK, GROUP * HEAD_DIM), lambda b, g, n: (b * (nb // nbs) + n, g)),
                   pl.BlockSpec((1, 1, nbs, 8, r), lambda b, g, n: (b, g, n, 0, 0))],
        out_shape=[SDS((bl * s_len, hb * HEAD_DIM), BF16), SDS((bl, kv, nb, 8, r), F32)],
        scratch_shapes=[pltpu.VMEM((nbs, SPAN, r), F32), pltpu.VMEM((nbs, SPAN, r), BF16)],
        compiler_params=_cp("parallel", "parallel", "arbitrary"),
        name="attn_b_fwd",
    )(qb, kb, vbt, bias_t, sink)


def _mixout(oa, ob, wo, x2, g2, g3, tm):
    t, d = x2.shape
    ca = oa.shape[1]

    def body(oa_ref, ob_ref, w_ref, x_ref, g2_ref, g3_ref, mix_ref, x1_ref, h2_ref):
        mix = _dot(oa_ref[...], w_ref[0:ca, :]) + _dot(ob_ref[...], w_ref[ca:, :])
        mix_ref[...] = mix
        y2, _, _ = _rms_fwd(mix, g2_ref[...])
        x1 = x_ref[...] + y2
        x1_ref[...] = x1
        y3, _, _ = _rms_fwd(x1, g3_ref[...])
        h2_ref[...] = y3.astype(BF16)

    tile = lambda w: pl.BlockSpec((tm, w), lambda i: (i, 0))
    vec = pl.BlockSpec((1, d), lambda i: (0, 0))
    return pl.pallas_call(
        body,
        grid=(t // tm,),
        in_specs=[tile(ca), tile(ob.shape[1]), pl.BlockSpec(wo.shape, lambda i: (0, 0)), tile(d), vec, vec],
        out_specs=[tile(d), tile(d), tile(d)],
        out_shape=[SDS((t, d), F32), SDS((t, d), F32), SDS((t, d), BF16)],
        compiler_params=_cp("parallel"),
        name="mixout",
    )(oa, ob, wo, x2, g2, g3)


def _ffn_fwd(h2, wup_g, wdn, x1, target, g4, tm, jb):
    t, d = x1.shape
    nblk, _, tf = wup_g.shape
    ff = nblk * tf
    nt = t // tm
    nj = nblk // jb

    def body(h_ref, wu_ref, wd_ref, x1_ref, tg_ref, g_ref, u_ref, df_ref, dy_ref, dg_ref, loss_ref, acc_sc):
        i, j = pl.program_id(0), pl.program_id(1)

        @pl.when(j == 0)
        def _():
            acc_sc[...] = jnp.zeros_like(acc_sc)

        @pl.when((i == 0) & (j == 0))
        def _():
            dg_ref[...] = jnp.zeros_like(dg_ref)
            loss_ref[...] = jnp.zeros_like(loss_ref)

        h = h_ref[...]
        squares = []
        for s in range(jb):
            u = jnp.maximum(_dot(h, wu_ref[s]), 0.0)
            u_ref[:, s * tf:(s + 1) * tf] = u.astype(BF16)
            squares.append((u * u).astype(BF16))
        acc_sc[...] += _dot(jnp.concatenate(squares, axis=1), wd_ref[...])

        @pl.when(j == nj - 1)
        def _():
            g = g_ref[...]
            y4, n, r = _rms_fwd(acc_sc[...], g)
            e = (x1_ref[...] + y4) - tg_ref[...]
            loss_ref[...] += jnp.sum(e * e) * (0.5 / d)
            dy = e * (1.0 / d)
            dy_ref[...] = dy
            df, dgt = _rms_bwd(n, r, g, dy)
            df_ref[...] = df.astype(BF16)
            dg_ref[0:1, :] += jnp.sum(dgt, axis=0, keepdims=True)

    tile = pl.BlockSpec((tm, d), lambda i, j: (i, 0))
    return pl.pallas_call(
        body,
        grid=(nt, nj),
        in_specs=[tile,
                  pl.BlockSpec((jb, d, tf), lambda i, j: (j, 0, 0)),
                  pl.BlockSpec((jb * tf, d), lambda i, j: (j, 0)),
                  tile, tile,
                  pl.BlockSpec((1, d), lambda i, j: (0, 0))],
        out_specs=[pl.BlockSpec((tm, jb * tf), lambda i, j: (i, j)), tile, tile,
                   pl.BlockSpec((8, d), lambda i, j: (0, 0)),
                   pl.BlockSpec((8, 128), lambda i, j: (0, 0))],
        out_shape=[SDS((t, ff), BF16), SDS((t, d), BF16), SDS((t, d), F32), SDS((8, d), F32), SDS((8, 128), F32)],
        scratch_shapes=[pltpu.VMEM((tm, d), F32)],
        compiler_params=_cp("arbitrary", "arbitrary"),
        name="ffn_fwd",
    )(h2, wup_g, wdn, x1, target, g4)


def _ffn_bwd(df, u, wdn, wup_g, x1, dy, mix, g3, g2, tm, jb):
    t, d = x1.shape
    nblk, _, tf = wup_g.shape
    nt = t // tm
    nj = nblk // jb

    def body(df_ref, u_ref, wd_ref, wu_ref, x1_ref, dy_ref, mix_ref, g3_ref, g2_ref,
             dpre_ref, dx1_ref, dmix_ref, dg3_ref, dg2_ref, acc_sc):
        i, j = pl.program_id(0), pl.program_id(1)

        @pl.when(j == 0)
        def _():
            acc_sc[...] = jnp.zeros_like(acc_sc)

        @pl.when((i == 0) & (j == 0))
        def _():
            dg3_ref[...] = jnp.zeros_like(dg3_ref)
            dg2_ref[...] = jnp.zeros_like(dg2_ref)

        du2 = _dot_nt(df_ref[...], wd_ref[...])
        dpre = (2.0 * u_ref[...].astype(F32) * du2).astype(BF16)
        dpre_ref[...] = dpre
        dh = _dot_nt(dpre[:, 0:tf], wu_ref[0])
        for s in range(1, jb):
            dh = dh + _dot_nt(dpre[:, s * tf:(s + 1) * tf], wu_ref[s])
        acc_sc[...] += dh

        @pl.when(j == nj - 1)
        def _():
            g3, g2 = g3_ref[...], g2_ref[...]
            _, n3, r3 = _rms_fwd(x1_ref[...], g3)
            dx, dgt3 = _rms_bwd(n3, r3, g3, acc_sc[...])
            dx1 = dy_ref[...] + dx
            dx1_ref[...] = dx1
            dg3_ref[0:1, :] += jnp.sum(dgt3, axis=0, keepdims=True)
            _, n2, r2 = _rms_fwd(mix_ref[...], g2)
            dmix, dgt2 = _rms_bwd(n2, r2, g2, dx1)
            dmix_ref[...] = dmix.astype(BF16)
            dg2_ref[0:1, :] += jnp.sum(dgt2, axis=0, keepdims=True)

    tile = pl.BlockSpec((tm, d), lambda i, j: (i, 0))
    vec = pl.BlockSpec((1, d), lambda i, j: (0, 0))
    acc8 = pl.BlockSpec((8, d), lambda i, j: (0, 0))
    return pl.pallas_call(
        body,
        grid=(nt, nj),
        in_specs=[tile,
                  pl.BlockSpec((tm, jb * tf), lambda i, j: (i, j)),
                  pl.BlockSpec((jb * tf, d), lambda i, j: (j, 0)),
                  pl.BlockSpec((jb, d, tf), lambda i, j: (j, 0, 0)),
                  tile, tile, tile, vec, vec],
        out_specs=[pl.BlockSpec((tm, jb * tf), lambda i, j: (i, j)), tile, tile, acc8, acc8],
        out_shape=[SDS(u.shape, BF16), SDS((t, d), F32), SDS((t, d), BF16), SDS((8, d), F32), SDS((8, d), F32)],
        scratch_shapes=[pltpu.VMEM((tm, d), F32)],
        compiler_params=_cp("arbitrary", "arbitrary"),
        name="ffn_bwd",
    )(df, u, wdn, wup_g, x1, dy, mix, g3, g2)


def _wgrad(a, b, a_spec, b_spec, out_block, out_shape, nj, nk, name, prep_a=None, prep_b=None):
    acc_shape = out_block[1:]

    def body(a_ref, b_ref, o_ref, acc_sc):
        k = pl.program_id(1)
        av = a_ref[...] if prep_a is None else prep_a(a_ref)
        bv = b_ref[...] if prep_b is None else prep_b(b_ref)
        part = _dot_tn(av, bv)

        @pl.when(k == 0)
        def _():
            acc_sc[...] = part

        @pl.when(k > 0)
        def _():
            acc_sc[...] += part

        @pl.when(k == nk - 1)
        def _():
            o_ref[0] = acc_sc[...].astype(BF16)

    return pl.pallas_call(
        body,
        grid=(nj, nk),
        in_specs=[a_spec, b_spec],
        out_specs=pl.BlockSpec(out_block, lambda j, k: (j, 0, 0)),
        out_shape=SDS(out_shape, BF16),
        scratch_shapes=[pltpu.VMEM(acc_shape, F32)],
        compiler_params=_cp("parallel", "arbitrary"),
        name=name,
    )(a, b)


def _wgrad_cols(a, b, nj, tt, name):
    t, m = a.shape
    bn = b.shape[1] // nj
    return _wgrad(a, b, pl.BlockSpec((tt, m), lambda j, k: (k, 0)), pl.BlockSpec((tt, bn), lambda j, k: (k, j)),
                  (1, m, bn), (nj, m, bn), nj, t // tt, name)


def _wgrad_rows(a, b, nj, tt, name, square=False):
    t, n = b.shape
    bm = a.shape[1] // nj

    def squared(a_ref):
        af = a_ref[...].astype(F32)
        return (af * af).astype(BF16)

    return _wgrad(a, b, pl.BlockSpec((tt, bm), lambda j, k: (k, j)), pl.BlockSpec((tt, n), lambda j, k: (k, 0)),
                  (1, bm, n), (nj, bm, n), nj, t // tt, name, prep_a=squared if square else None)


def _wgrad_o(oa, ob, dmix, nj, tt):
    t, n = dmix.shape
    ca, cb = oa.shape[1], ob.shape[1]
    m = ca + cb
    nk = t // tt

    def body(oa_ref, ob_ref, b_ref, o_ref, acc_sc):
        k = pl.program_id(0)
        part = _dot_tn(jnp.concatenate([oa_ref[...], ob_ref[...]], axis=1), b_ref[...])

        @pl.when(k == 0)
        def _():
            acc_sc[...] = part

        @pl.when(k > 0)
        def _():
            acc_sc[...] += part

        @pl.when(k == nk - 1)
        def _():
            o_ref[...] = acc_sc[...].reshape(nj, m // nj, n).astype(BF16)

    return pl.pallas_call(
        body,
        grid=(nk,),
        in_specs=[pl.BlockSpec((tt, ca), lambda k: (k, 0)), pl.BlockSpec((tt, cb), lambda k: (k, 0)),
                  pl.BlockSpec((tt, n), lambda k: (k, 0))],
        out_specs=pl.BlockSpec((nj, m // nj, n), lambda k: (0, 0, 0)),
        out_shape=SDS((nj, m // nj, n), BF16),
        scratch_shapes=[pltpu.VMEM((m, n), F32)],
        compiler_params=_cp("arbitrary"),
        name="wgrad_o",
    )(oa, ob, dmix)


def _attn_out_bwd(dmix, wo, ca, tm):
    t, d = dmix.shape
    cb = wo.shape[0] - ca

    def body(dm_ref, w_ref, da_ref, db_ref):
        dm = dm_ref[...]
        da_ref[...] = _dot_nt(dm, w_ref[0:ca, :]).astype(BF16)
        db_ref[...] = _dot_nt(dm, w_ref[ca:, :]).astype(BF16)

    return pl.pallas_call(
        body,
        grid=(t // tm,),
        in_specs=[pl.BlockSpec((tm, d), lambda i: (i, 0)), pl.BlockSpec(wo.shape, lambda i: (0, 0))],
        out_specs=[pl.BlockSpec((tm, ca), lambda i: (i, 0)), pl.BlockSpec((tm, cb), lambda i: (i, 0))],
        out_shape=[SDS((t, ca), BF16), SDS((t, cb), BF16)],
        compiler_params=_cp("parallel"),
        name="attn_out_bwd",
    )(dmix, wo)


def _stack_heads(ref, rows):
    return jnp.concatenate([ref[:, h * HEAD_DIM:(h + 1) * HEAD_DIM] for h in range(GROUP)], axis=0)


def _attn_a_bwd(qa, ka, kat, va, do, o, lse, tq, tk, grads):
    bl, ha, s_len, _ = qa.shape
    kv = ka.shape[1]
    nq, nk = s_len // tq, s_len // tk
    assert nk % 2 == 0
    r = GROUP * tq
    ng = len(grads)

    def body(q_ref, qn_ref, k_ref, kt_ref, v_ref, do_ref, don_ref, o_ref, l_ref, *rest):
        grad_refs, (dq_ref, dk_ref, dv_ref), parts = rest[:ng], rest[ng:ng + 3], rest[ng + 3:2 * ng + 3]
        st_sc, dp_sc, dkt_sc, dvt_sc, send_sems, recv_sems, local_sems = rest[2 * ng + 3:]
        i = pl.program_id(2)
        step_id = (pl.program_id(0) * kv + pl.program_id(1)) * nq + i
        start, wait = _direct_exchange("scatter", grad_refs, parts, send_sems, recv_sems, local_sems)
        pl.when(step_id == 0)(start)

        q = q_ref[0].reshape(r, HEAD_DIM)
        do2 = _stack_heads(do_ref, tq)
        qt = q.astype(F32).T
        dot32 = do2.astype(F32).T
        ot32 = _stack_heads(o_ref, tq).astype(F32).T
        drow = jnp.sum(dot32 * ot32, axis=0, keepdims=True)
        qt, dot = qt.astype(BF16), dot32.astype(BF16)
        lrow = l_ref[0, 0, 0, 0:1, :]

        @pl.when(i == 0)
        def _():
            dkt_sc[...] = jnp.zeros_like(dkt_sc)
            dvt_sc[...] = jnp.zeros_like(dvt_sc)

        def chunk(c):
            return pl.ds(pl.multiple_of(c * tk, tk), tk)

        def scores(c, slot, qv=q, dov=do2):
            st_sc[slot] = _dot_nt(k_ref[0, 0, chunk(c), :], qv)
            dp_sc[slot] = _dot_nt(v_ref[0, 0, chunk(c), :], dov)

        def fold(slot, c, dqt):
            pt = jnp.exp(st_sc[slot] - lrow)
            dsb = (pt * (dp_sc[slot] - drow)).astype(BF16)
            dvt_sc[:, chunk(c)] += _dot_nt(dot, pt.astype(BF16))
            dkt_sc[:, chunk(c)] += _dot_nt(qt, dsb)
            return dqt + _dot(kt_ref[0, 0, :, chunk(c)], dsb)

        @pl.when(i == 0)
        def _():
            scores(0, 0)

        def step(c2, dqt):
            c = 2 * c2
            scores(c + 1, 1)
            dqt = fold(0, c, dqt)
            scores(c + 2, 0)
            return fold(1, c + 1, dqt)

        dqt = jnp.zeros((HEAD_DIM, r), F32)
        for c2 in range(nk // 2 - 1):
            dqt = step(c2, dqt)
        scores(nk - 1, 1)
        dqt = fold(0, nk - 2, dqt)
        scores(0, 0, qn_ref[0].reshape(r, HEAD_DIM), _stack_heads(don_ref, tq))
        dqt = fold(1, nk - 1, dqt)
        dq_ref[0] = dqt.T.reshape(GROUP, tq, HEAD_DIM)

        @pl.when(i == nq - 1)
        def _():
            dk_ref[0, 0] = dkt_sc[...].T
            dv_ref[0, 0] = dvt_sc[...].T

        pl.when(step_id == bl * kv * nq - 1)(wait)

    kvspec = pl.BlockSpec((1, 1, s_len, HEAD_DIM), lambda b, g, i: (b, g, 0, 0))
    qspec = pl.BlockSpec((1, GROUP, tq, HEAD_DIM), lambda b, g, i: (b, g, i, 0))
    tok = pl.BlockSpec((tq, GROUP * HEAD_DIM), lambda b, g, i: (b * nq + i, g))
    qnext = pl.BlockSpec((1, GROUP, tq, HEAD_DIM), lambda b, g, i: (b, g, jnp.minimum(i + 1, nq - 1), 0))
    toknext = pl.BlockSpec((tq, GROUP * HEAD_DIM), lambda b, g, i: (b * nq + jnp.minimum(i + 1, nq - 1), g))
    anyspec = pl.BlockSpec(memory_space=pl.ANY)
    res = pl.pallas_call(
        body,
        grid=(bl, kv, nq),
        in_specs=[qspec, qnext, kvspec, pl.BlockSpec((1, 1, HEAD_DIM, s_len), lambda b, g, i: (b, g, 0, 0)), kvspec,
                  tok, toknext, tok, pl.BlockSpec((1, 1, 1, 8, r), lambda b, g, i: (b, g, i, 0, 0))] + [anyspec] * ng,
        out_specs=[qspec, kvspec, kvspec] + [anyspec] * ng,
        out_shape=[SDS(qa.shape, F32), SDS(ka.shape, F32), SDS(va.shape, F32)]
        + [SDS(g.shape, g.dtype) for g in grads],
        scratch_shapes=[pltpu.VMEM((2, tk, r), F32), pltpu.VMEM((2, tk, r), F32),
                        pltpu.VMEM((HEAD_DIM, s_len), F32), pltpu.VMEM((HEAD_DIM, s_len), F32)]
        + _exchange_scratch(ng),
        compiler_params=_cp("arbitrary", "arbitrary", "arbitrary"),
        name="attn_a_bwd",
    )(qa, qa, ka, kat, va, do, do, o, lse, *grads)
    return res[0], res[1], res[2], res[3:]


def _attn_b_bwd(qb, kb, kbt, vb, do, o, lse, bias_t, sink, s_len):
    bl, hb, _, _ = qb.shape
    kv, sp = kb.shape[1], kb.shape[2]
    nb = s_len // BLOCK
    nbs = min(QB_PER_STEP, nb)
    r = GROUP * BLOCK

    def body(q_ref, k_ref, kt_ref, v_ref, do_ref, o_ref, l_ref, bt_ref, sink_ref,
             dq_ref, dk_ref, dv_ref, dsum_ref, dsink_ref, dkt_sc, dvt_sc):
        g, b, ns = pl.program_id(0), pl.program_id(1), pl.program_id(2)
        sink_row = _sink_row(sink_ref, g)

        @pl.when(ns == 0)
        def _():
            dkt_sc[...] = jnp.zeros_like(dkt_sc)
            dvt_sc[...] = jnp.zeros_like(dvt_sc)

        @pl.when((b == 0) & (ns == 0))
        def _():
            dsum_ref[...] = jnp.zeros_like(dsum_ref)
            dsink_ref[...] = jnp.zeros_like(dsink_ref)

        dsum = jnp.zeros((SPAN, r), F32)
        dsink = jnp.zeros((1, r), F32)
        for j in range(nbs):
            n = ns * nbs + j
            span = pl.ds(pl.multiple_of(n * BLOCK, BLOCK), SPAN)
            rows = slice(j * BLOCK, (j + 1) * BLOCK)
            q = q_ref[0, :, rows, :].reshape(r, HEAD_DIM)
            do2 = jnp.concatenate([do_ref[rows, h * HEAD_DIM:(h + 1) * HEAD_DIM] for h in range(GROUP)], axis=0)
            o2 = jnp.concatenate([o_ref[rows, h * HEAD_DIM:(h + 1) * HEAD_DIM] for h in range(GROUP)], axis=0)
            dot32 = do2.astype(F32).T
            drow = jnp.sum(dot32 * o2.astype(F32).T, axis=0, keepdims=True)
            qt, dot = q.astype(F32).T.astype(BF16), dot32.astype(BF16)
            lrow = l_ref[0, 0, j, 0:1, :]
            st = _dot_nt(k_ref[0, 0, span, :], q) + bt_ref[_bias_variant(n, nb), 0]
            pt = jnp.exp(st - lrow)
            dst = pt * (_dot_nt(v_ref[0, 0, span, :], do2) - drow)
            dsum = dsum + dst
            dsink = dsink - jnp.exp(sink_row - lrow) * drow
            dsb = dst.astype(BF16)
            dvt_sc[:, span] += _dot_nt(dot, pt.astype(BF16))
            dkt_sc[:, span] += _dot_nt(qt, dsb)
            dq_ref[0, :, rows, :] = _dot(kt_ref[0, 0, :, span], dsb).T.reshape(GROUP, BLOCK, HEAD_DIM)
        dsum_ref[0] += dsum
        dsink_ref[0, 0:1, :] += dsink

        @pl.when(ns == nb // nbs - 1)
        def _():
            dk_ref[0, 0] = dkt_sc[:, BLOCK:BLOCK + s_len].T
            dv_ref[0, 0] = dvt_sc[:, BLOCK:BLOCK + s_len].T

    kvspec = pl.BlockSpec((1, 1, sp, HEAD_DIM), lambda g, b, n: (b, g, 0, 0))
    kvout = pl.BlockSpec((1, 1, s_len, HEAD_DIM), lambda g, b, n: (b, g, 0, 0))
    qspec = pl.BlockSpec((1, GROUP, nbs * BLOCK, HEAD_DIM), lambda g, b, n: (b, g, n, 0))
    tok = pl.BlockSpec((nbs * BLOCK, GROUP * HEAD_DIM), lambda g, b, n: (b * (nb // nbs) + n, g))
    return pl.pallas_call(
        body,
        grid=(kv, bl, nb // nbs),
        in_specs=[qspec, kvspec, pl.BlockSpec((1, 1, HEAD_DIM, sp), lambda g, b, n: (b, g, 0, 0)), kvspec, tok, tok,
                  pl.BlockSpec((1, 1, nbs, 8, r), lambda g, b, n: (b, g, n, 0, 0)),
                  pl.BlockSpec((3, 1, SPAN, r), lambda g, b, n: (0, g, 0, 0)),
                  pl.BlockSpec(memory_space=pltpu.SMEM)],
        out_specs=[qspec, kvout, kvout,
                   pl.BlockSpec((1, SPAN, r), lambda g, b, n: (g, 0, 0)),
                   pl.BlockSpec((1, 8, r), lambda g, b, n: (g, 0, 0))],
        out_shape=[SDS(qb.shape, F32), SDS((bl, kv, s_len, HEAD_DIM), F32), SDS((bl, kv, s_len, HEAD_DIM), F32),
                   SDS((kv, SPAN, r), F32), SDS((kv, 8, r), F32)],
        scratch_shapes=[pltpu.VMEM((HEAD_DIM, sp), F32), pltpu.VMEM((HEAD_DIM, sp), F32)],
        compiler_params=_cp("arbitrary", "arbitrary", "arbitrary"),
        name="attn_b_bwd",
    )(qb, kb, kbt, vb, do, o, lse, bias_t, sink)


def _bias_reduce(dsum, dsink, bucket_t4):
    kv, _, r = dsum.shape

    def body(ds_ref, dk_ref, bk_ref, rel_ref, sink_ref):
        lane = lax.broadcasted_iota(jnp.int32, (N_BUCKETS, 128), 1)
        lane8 = lax.broadcasted_iota(jnp.int32, (8, 128), 1)
        bk = bk_ref[...]
        for g in range(kv):
            ds = ds_ref[g]
            rowi = lax.broadcasted_iota(jnp.int32, (N_BUCKETS, r), 0)
            red = jnp.zeros((N_BUCKETS, r), F32)
            for b in range(N_BUCKETS):
                red = jnp.where(rowi == b, jnp.sum(jnp.where(bk == b, ds, 0.0), axis=0, keepdims=True), red)
            out = jnp.zeros((N_BUCKETS, 128), F32)
            so = jnp.zeros((8, 128), F32)
            for h in range(GROUP):
                col = jnp.sum(red[:, h * BLOCK:(h + 1) * BLOCK], axis=1, keepdims=True)
                out = jnp.where(lane == h, col, out)
                sc = jnp.sum(dk_ref[g][:, h * BLOCK:(h + 1) * BLOCK], axis=1, keepdims=True)
                so = jnp.where(lane8 == h, sc, so)
            rel_ref[g] = out
            sink_ref[g] = so

    vm = pl.BlockSpec(memory_space=pltpu.VMEM)
    return pl.pallas_call(
        body,
        in_specs=[vm, vm, vm],
        out_specs=[vm, vm],
        out_shape=[SDS((kv, N_BUCKETS, 128), F32), SDS((kv, 8, 128), F32)],
        name="bias_reduce",
    )(dsum, dsink, bucket_t4)


def _dqkprep(dqa, dka, dva, dqb, dkb, dvb, proj, cos, sin_signed, gq, gk, s_len, ts):
    t, p_cols = proj.shape
    bl, ha = dqa.shape[0], dqa.shape[1]
    kva, hb, kvb = dka.shape[1], dqb.shape[1], dkb.shape[1]
    ns = s_len // ts

    def body(dqa_ref, dka_ref, dva_ref, dqb_ref, dkb_ref, dvb_ref, p_ref, cos_ref, sin_ref, gq_ref, gk_ref,
             dp_ref, dgq_ref, dgk_ref):
        b, i = pl.program_id(0), pl.program_id(1)
        cs, sn = cos_ref[...], sin_ref[...]
        low, first = _pair_masks(ts)

        @pl.when((b == 0) & (i == 0))
        def _():
            dgq_ref[...] = jnp.zeros_like(dgq_ref)
            dgk_ref[...] = jnp.zeros_like(dgk_ref)

        def grad_pair(ref, p):
            return jnp.concatenate([ref[0, 2 * p], ref[0, 2 * p + 1]], axis=1)

        def put(p, val):
            dp_ref[:, p * PAIR:(p + 1) * PAIR] = val.astype(BF16)

        def unrope_norm(d_rot, p, g, dg_ref):
            dn = d_rot * cs + _pair_partner(d_rot * sn, first)
            xp = p_ref[:, p * PAIR:(p + 1) * PAIR]
            r = lax.rsqrt(_pair_mean(xp * xp, low) + EPS)
            n = xp * r
            gd = g * dn
            dg_ref[0:1, :] += jnp.sum(dn * n, axis=0, keepdims=True)
            put(p, r * (gd - n * _pair_mean(n * gd, low)))

        for p in range(ha // 2):
            unrope_norm(grad_pair(dqa_ref, p) * SCALE, p, gq_ref[...], dgq_ref)
        base = ha // 2
        for p in range(kva // 2):
            unrope_norm(grad_pair(dka_ref, p), base + p, gk_ref[...], dgk_ref)
            put(base + kva // 2 + p, grad_pair(dva_ref, p))
        base += kva
        for p in range(hb // 2):
            put(base + p, grad_pair(dqb_ref, p) * SCALE)
        base += hb // 2
        for p in range(kvb // 2):
            put(base + p, grad_pair(dkb_ref, p))
            put(base + kvb // 2 + p, grad_pair(dvb_ref, p))

    def hm(nh):
        return pl.BlockSpec((1, nh, ts, HEAD_DIM), lambda b, i: (b, 0, i, 0))

    vec = pl.BlockSpec((1, PAIR), lambda b, i: (0, 0))
    tab = pl.BlockSpec((ts, PAIR), lambda b, i: (i, 0))
    acc = pl.BlockSpec((8, PAIR), lambda b, i: (0, 0))
    pspec = pl.BlockSpec((ts, p_cols), lambda b, i: (b * ns + i, 0))
    return pl.pallas_call(
        body,
        grid=(bl, ns),
        in_specs=[hm(ha), hm(kva), hm(kva), hm(hb), hm(kvb), hm(kvb), pspec, tab, tab, vec, vec],
        out_specs=[pspec, acc, acc],
        out_shape=[SDS((t, p_cols), BF16), SDS((8, PAIR), F32), SDS((8, PAIR), F32)],
        compiler_params=_cp("arbitrary", "arbitrary"),
        name="dqkprep",
    )(dqa, dka, dva, dqb, dkb, dvb, proj, cos, sin_signed, gq, gk)


def _dx_final(dproj, w_t, x2, dx1, g1, tm, grads):
    t, d = x2.shape
    p_cols = w_t.shape[0]
    ng = len(grads)
    nsteps = t // tm

    def body(dp_ref, w_ref, x_ref, dx1_ref, g_ref, *rest):
        grad_refs, (dx_ref, dg_ref), parts = rest[:ng], rest[ng:ng + 2], rest[ng + 2:2 * ng + 2]
        start, wait = _direct_exchange("scatter", grad_refs, parts, *rest[2 * ng + 2:])

        @pl.when(pl.program_id(0) == 0)
        def _():
            start()
            dg_ref[...] = jnp.zeros_like(dg_ref)

        dh = _dot(dp_ref[...], w_ref[...])
        g = g_ref[...]
        _, n, r = _rms_fwd(x_ref[...], g)
        dx, dgt = _rms_bwd(n, r, g, dh)
        dx_ref[...] = dx1_ref[...] + dx
        dg_ref[0:1, :] += jnp.sum(dgt, axis=0, keepdims=True)
        pl.when(pl.program_id(0) == nsteps - 1)(wait)

    tile = pl.BlockSpec((tm, d), lambda i: (i, 0))
    anyspec = pl.BlockSpec(memory_space=pl.ANY)
    res = pl.pallas_call(
        body,
        grid=(nsteps,),
        in_specs=[pl.BlockSpec((tm, p_cols), lambda i: (i, 0)),
                  pl.BlockSpec((p_cols, d), lambda i: (0, 0)),
                  tile, tile, pl.BlockSpec((1, d), lambda i: (0, 0))] + [anyspec] * ng,
        out_specs=[tile, pl.BlockSpec((8, d), lambda i: (0, 0))] + [anyspec] * ng,
        out_shape=[SDS((t, d), F32), SDS((8, d), F32)] + [SDS(g.shape, g.dtype) for g in grads],
        scratch_shapes=_exchange_scratch(ng),
        compiler_params=_cp("arbitrary"),
        name="dx_final",
    )(dproj, w_t, x2, dx1, g1, *grads)
    return res[0], res[1], res[2:]


def _adamw_math(w, g, m, v):
    m = ADAM_B1 * m + (1.0 - ADAM_B1) * g
    v = ADAM_B2 * v + (1.0 - ADAM_B2) * (g * g)
    m_hat = m / (1.0 - ADAM_B1 ** ADAM_STEP)
    v_hat = v / (1.0 - ADAM_B2 ** ADAM_STEP)
    delta = -ADAM_LR * (m_hat / (jnp.sqrt(v_hat) + ADAM_EPS) + ADAM_WD * w)
    return delta, m, v


def _adamw_sum(parts, w, m, v, tr, name):
    rows, cols = w.shape

    def body(p_ref, w_ref, m_ref, v_ref, g_ref, d_ref, nm_ref, nv_ref):
        g = p_ref[0].astype(F32)
        for s in range(1, N_DEV):
            g = g + p_ref[s].astype(F32)
        g_ref[...] = g
        d_ref[...], nm_ref[...], nv_ref[...] = _adamw_math(w_ref[...], g, m_ref[...], v_ref[...])

    tr = min(tr, rows)
    tile = pl.BlockSpec((tr, cols), lambda i: (i, 0))
    return pl.pallas_call(
        body,
        grid=(rows // tr,),
        in_specs=[pl.BlockSpec((N_DEV, tr, cols), lambda i: (0, i, 0)), tile, tile, tile],
        out_specs=[tile] * 4,
        out_shape=[SDS((rows, cols), F32)] * 4,
        compiler_params=_cp("parallel"),
        name=name,
    )(parts, w, m, v)


def _adamw_small(vec, rel, ws, ms, vs):
    hb = ws[6].shape[1]
    n = len(ws)

    def body(vec_ref, rel_ref, *rest):
        w_refs, m_refs, v_refs = rest[:n], rest[n:2 * n], rest[2 * n:3 * n]
        loss_ref, outs = rest[3 * n], rest[3 * n + 1:]
        grads = [vec_ref[0:1, :], vec_ref[1:2, :], vec_ref[2:3, :], vec_ref[3:4, :],
                 vec_ref[4:5, 0:HEAD_DIM], vec_ref[4:5, SMALL_LANES:SMALL_LANES + HEAD_DIM],
                 vec_ref[4:5, 2 * SMALL_LANES:2 * SMALL_LANES + hb], rel_ref[:, 0:hb]]
        loss_ref[...] = vec_ref[4:5, 3 * SMALL_LANES:3 * SMALL_LANES + 1]
        for p, g in enumerate(grads):
            g_ref, d_ref, nm_ref, nv_ref = outs[4 * p:4 * p + 4]
            g_ref[...] = g
            d_ref[...], nm_ref[...], nv_ref[...] = _adamw_math(w_refs[p][...], g, m_refs[p][...], v_refs[p][...])

    vm = pl.BlockSpec(memory_space=pltpu.VMEM)
    res = pl.pallas_call(
        body,
        in_specs=[vm] * (2 + 3 * n),
        out_specs=[vm] * (1 + 4 * n),
        out_shape=[SDS((1, 1), F32)] + [SDS(w.shape, F32) for w in ws for _ in range(4)],
        name="adamw_small",
    )(vec, rel, *ws, *ms, *vs)
    return res[0], [res[1 + 4 * p:5 + 4 * p] for p in range(n)]


def _local_step(x, loss_target, win_g, wo_s, wup_s, wdn_s, g_pre_mix, g_post_mix, q_norm_a, k_norm_a, sink_b,
                rel_bias, g_pre_ffn, g_post_ffn):
    bl, s_len, d = x.shape
    t = bl * s_len
    nh = d // HEAD_DIM
    ha = nh // 2
    kva = ha // GROUP
    hb = nh - ha
    kvb = hb // GROUP
    tm = 512
    tw = min(4096, t)
    ts = min(512, s_len)
    tq, tk = 2 * BLOCK, min(512, s_len // 2)

    x2 = x.reshape(t, d)
    tg2 = loss_target.reshape(t, d)
    cos, sin_signed = _rope_tables(s_len)
    gq2, gk2 = jnp.tile(q_norm_a, (1, 2)), jnp.tile(k_norm_a, (1, 2))
    a = jnp.arange(BLOCK, dtype=jnp.int32)
    c = jnp.arange(SPAN, dtype=jnp.int32)
    bucket_t = _t5_bucket(c[:, None] - BLOCK - a[None, :])
    bucket_t4 = jnp.tile(bucket_t, (1, GROUP))
    w_in_t = win_g.reshape(-1, d)
    p_cols = w_in_t.shape[0]

    h1, proj = _inproj(x2, g_pre_mix, w_in_t, tm)
    qa, ka, kat, va, vat, qb, kb, kbt, vb, vbt = _qkprep(
        proj, cos, sin_signed, gq2, gk2, bl, s_len, ha, kva, hb, kvb, ts)
    bias_t = _bias_build(bucket_t, rel_bias, hb)
    oa, lse_a, (wo_g, wup_g, wdn_g) = _attn_a_fwd(qa, ka, vat, tq, tk, [wo_s, wup_s, wdn_s])
    wo = wo_g.reshape(-1, d)
    wdn = wdn_g.reshape(-1, d)
    ob, lse_b = _attn_b_fwd(qb, kb, vbt, bias_t, sink_b, s_len)
    mix, x1, h2 = _mixout(oa, ob, wo, x2, g_post_mix, g_pre_ffn, tm)
    u, df, dy, dg4, loss8 = _ffn_fwd(h2, wup_g, wdn, x1, tg2, g_post_ffn, tm, FFN_FWD_BLOCKS_PER_STEP)

    dpre, dx1, dmix, dg3, dg2 = _ffn_bwd(df, u, wdn, wup_g, x1, dy, mix, g_pre_ffn, g_post_mix, tm,
                                         FFN_BWD_BLOCKS_PER_STEP)
    gw_dn = _wgrad_rows(u, df, N_DEV, tw, "wgrad_down", square=True)
    gw_up = _wgrad_cols(h2, dpre, N_DEV, tw, "wgrad_up")
    gw_o = _wgrad_o(oa, ob, dmix, N_DEV, min(2048, t))
    doa, dob = _attn_out_bwd(dmix, wo, oa.shape[1], tm)
    dqa, dka, dva, (p_o, p_up, p_dn) = _attn_a_bwd(qa, ka, kat, va, doa, oa, lse_a, tq, tk, [gw_o, gw_up, gw_dn])
    dqb, dkb, dvb, dsum, dsink = _attn_b_bwd(qb, kb, kbt, vb, dob, ob, lse_b, bias_t, sink_b, s_len)
    drel_g, dsink_g = _bias_reduce(dsum, dsink, bucket_t4)
    dproj, dgq, dgk = _dqkprep(dqa, dka, dva, dqb, dkb, dvb, proj, cos, sin_signed, gq2, gk2, s_len, ts)
    gw_in_t = _wgrad_rows(dproj, h1, p_cols // 256, tw, "wgrad_in").reshape(N_DEV, -1, d)
    grad_x, dg1, (p_in,) = _dx_final(dproj, w_in_t, x2, dx1, g_pre_mix, tm, [gw_in_t])

    vec, rel = _small_allreduce([dg1, dg2, dg3, dg4], dgq, dgk, dsink_g, drel_g, loss8)
    return grad_x.reshape(bl, s_len, d), p_in, p_o, p_up, p_dn, vec, rel


def kernel(x, w_in, w_o, g_pre_mix, g_post_mix, q_norm_a, k_norm_a, sink_b, rel_bias, g_pre_ffn, w_ffn_up, w_ffn_down, g_post_ffn, loss_target, m_w_in, m_w_o, m_g_pre_mix, m_g_post_mix, m_q_norm_a, m_k_norm_a, m_sink_b, m_rel_bias, m_g_pre_ffn, m_w_ffn_up, m_w_ffn_down, m_g_post_ffn, v_w_in, v_w_o, v_g_pre_mix, v_g_post_mix, v_q_norm_a, v_k_norm_a, v_sink_b, v_rel_bias, v_g_pre_ffn, v_w_ffn_up, v_w_ffn_down, v_g_post_ffn):
    w_in_t = w_in[0].T
    (win_g,) = _weight_gather([w_in_t.astype(BF16)])

    grad_x, p_in, p_o, p_up, p_dn, vec, rel = _local_step(
        x, loss_target, win_g, w_o[0].astype(BF16), w_ffn_up[0].astype(BF16), w_ffn_down[0].astype(BF16),
        g_pre_mix, g_post_mix, q_norm_a, k_norm_a, sink_b, rel_bias, g_pre_ffn, g_post_ffn)

    big = {
        "w_in": [a.T for a in _adamw_sum(p_in, w_in_t, m_w_in[0].T, v_w_in[0].T, 192, "adamw_in")],
        "w_o": _adamw_sum(p_o, w_o[0], m_w_o[0], v_w_o[0], 128, "adamw_o"),
        "w_up": _adamw_sum(p_up, w_ffn_up[0], m_w_ffn_up[0], v_w_ffn_up[0], 256, "adamw_up"),
        "w_dn": _adamw_sum(p_dn, w_ffn_down[0], m_w_ffn_down[0], v_w_ffn_down[0], 256, "adamw_down"),
    }
    loss, small = _adamw_small(
        vec, rel,
        [g_pre_mix, g_post_mix, g_pre_ffn, g_post_ffn, q_norm_a, k_norm_a, sink_b, rel_bias],
        [m_g_pre_mix, m_g_post_mix, m_g_pre_ffn, m_g_post_ffn, m_q_norm_a, m_k_norm_a, m_sink_b, m_rel_bias],
        [v_g_pre_mix, v_g_post_mix, v_g_pre_ffn, v_g_post_ffn, v_q_norm_a, v_k_norm_a, v_sink_b, v_rel_bias])
    s_pre_mix, s_post_mix, s_pre_ffn, s_post_ffn, s_qn, s_kn, s_sink, s_rel = small

    def outs(kind):
        return [big["w_in"][kind][None], big["w_o"][kind][None], s_pre_mix[kind], s_post_mix[kind], s_qn[kind],
                s_kn[kind], s_sink[kind], s_rel[kind], s_pre_ffn[kind], big["w_up"][kind][None],
                big["w_dn"][kind][None], s_post_ffn[kind]]

    return (loss.reshape(()), grad_x, *outs(0), *outs(1), *outs(2), *outs(3))
```

```python
import jax
import jax.numpy as jnp
import numpy as np
from jax import lax
from jax.experimental import pallas as pl
from jax.experimental.pallas import tpu as pltpu

F32 = jnp.float32
BF16 = jnp.bfloat16
SDS = jax.ShapeDtypeStruct

N_DEV = 8
HEAD_DIM = 64
GROUP = 4
BLOCK = 128
SPAN = 3 * BLOCK
GRID_W = 64
N_BUCKETS = 32
MAX_DISTANCE = 128
ROPE_THETA = 10000.0
EPS = 1e-6
NEG_INF = -1e30
SCALE = HEAD_DIM ** -0.5
VT_PAD = 16

ADAM_LR = 0.001
ADAM_B1 = 0.9
ADAM_B2 = 0.999
ADAM_EPS = 1e-08
ADAM_WD = 0.01
ADAM_STEP = 10

VMEM_LIMIT = 56 * 1024 * 1024
MESH = pl.DeviceIdType.MESH


def _cp(*sem):
    return pltpu.CompilerParams(dimension_semantics=sem, vmem_limit_bytes=VMEM_LIMIT)


def _dot(a, b):
    return jnp.dot(a, b, preferred_element_type=F32)


def _dot_nt(a, b):
    return lax.dot_general(a, b, (((1,), (1,)), ((), ())), preferred_element_type=F32)


def _dot_tn(a, b):
    return lax.dot_general(a, b, (((0,), (0,)), ((), ())), preferred_element_type=F32)


def _rms_fwd(x, g):
    r = lax.rsqrt(jnp.mean(x * x, axis=-1, keepdims=True) + EPS)
    n = x * r
    return n * g, n, r


def _rms_bwd(n, r, g, dy):
    gd = g * dy
    dx = r * (gd - n * jnp.mean(n * gd, axis=-1, keepdims=True))
    return dx, dy * n


def _rope_tables(s_len):
    rows = s_len // GRID_W
    row = np.repeat(np.arange(rows, dtype=np.int32), GRID_W)
    col = np.tile(np.arange(GRID_W, dtype=np.int32), rows)
    nf = HEAD_DIM // 4
    freqs = np.float32(ROPE_THETA) ** (-np.arange(nf, dtype=np.float32) / np.float32(nf))
    ang_r = row.astype(np.float32)[:, None] * freqs[None, :]
    ang_c = col.astype(np.float32)[:, None] * freqs[None, :]
    cr, sr, cc, sc = np.cos(ang_r), np.sin(ang_r), np.cos(ang_c), np.sin(ang_c)
    cos = np.concatenate([cr, cr, cc, cc] * 2, axis=-1).astype(np.float32)
    sin_signed = np.concatenate([-sr, sr, -sc, sc] * 2, axis=-1).astype(np.float32)
    return jnp.asarray(cos), jnp.asarray(sin_signed)


def _t5_bucket(rel):
    nb = N_BUCKETS // 2
    ret = (rel > 0).astype(jnp.int32) * nb
    n = jnp.abs(rel)
    max_exact = nb // 2
    nf = jnp.maximum(n, 1).astype(F32)
    large = max_exact + (jnp.log(nf / max_exact) / np.float32(np.log(MAX_DISTANCE / max_exact))
                         * (nb - max_exact)).astype(jnp.int32)
    large = jnp.minimum(large, nb - 1)
    return ret + jnp.where(n < max_exact, n, large)


def _mesh_pos():
    return lax.axis_index("x"), lax.axis_index("y"), lax.axis_index("c")


def _lin(p):
    return 4 * p[0] + 2 * p[1] + p[2]


def _weight_gather(shards):
    n = len(shards)

    def body(*refs):
        xs, outs = refs[:n], refs[n:2 * n]
        send_sems, recv_sems, local_sems = refs[2 * n:]
        x, y, c = _mesh_pos()
        me, sibling = (x, y, c), (x, y, 1 - c)
        chips = [(1 - x, y), (x, 1 - y), (1 - x, 1 - y)]

        def copy(a, k, block, to, src=None):
            slot = outs[a].at[_lin(block)]
            return pltpu.make_async_remote_copy(
                src_ref=slot if src is None else src, dst_ref=slot,
                send_sem=send_sems.at[a, k], recv_sem=recv_sems.at[a, k],
                device_id=to, device_id_type=MESH)

        started = []
        for a in range(n):
            mine = pltpu.make_async_copy(xs[a], outs[a].at[_lin(me)], local_sems.at[a])
            mine.start()
            started.append(mine)
        sends = []
        for a in range(n):
            first = [copy(a, 0, me, sibling, src=xs[a])]
            first += [copy(a, 1 + j, me, (*chip, c), src=xs[a]) for j, chip in enumerate(chips)]
            for cp in first:
                cp.start()
            sends += first
        for a in range(n):
            for j, chip in enumerate(chips):
                copy(a, 1 + j, (*chip, c), me).wait_recv()
                fwd = copy(a, 4 + j, (*chip, c), sibling)
                fwd.start()
                sends.append(fwd)
        for a in range(n):
            copy(a, 0, sibling, me).wait_recv()
            for j, chip in enumerate(chips):
                copy(a, 4 + j, (*chip, 1 - c), me).wait_recv()
        for cp in sends:
            cp.wait_send()
        for mine in started:
            mine.wait()

    anyspec = pl.BlockSpec(memory_space=pl.ANY)
    return pl.pallas_call(
        body,
        out_shape=[SDS((N_DEV,) + s.shape, s.dtype) for s in shards],
        in_specs=[anyspec] * n,
        out_specs=[anyspec] * n,
        scratch_shapes=[pltpu.SemaphoreType.DMA((n, 7)), pltpu.SemaphoreType.DMA((n, 7)),
                        pltpu.SemaphoreType.DMA((n,))],
        name="weight_gather",
    )(*shards)


def _direct_exchange(kind, ins, outs, send_sems, recv_sems, local_sems):
    x, y, c = _mesh_pos()
    me = (x, y, c)
    peers = [(x, y, 1 - c), (1 - x, y, c), (x, 1 - y, c), (1 - x, 1 - y, c),
             (1 - x, y, 1 - c), (x, 1 - y, 1 - c), (1 - x, 1 - y, 1 - c)]

    def src(a, to):
        return ins[a] if kind == "gather" else ins[a].at[_lin(to)]

    def remote(a, k, to, frm):
        return pltpu.make_async_remote_copy(
            src_ref=src(a, to), dst_ref=outs[a].at[_lin(frm)],
            send_sem=send_sems.at[a, k], recv_sem=recv_sems.at[a, k],
            device_id=to, device_id_type=MESH)

    n = len(ins)
    sends = [remote(a, k, p, me) for a in range(n) for k, p in enumerate(peers)]
    arrivals = [remote(a, k, p, p) for a in range(n) for k, p in enumerate(peers)]
    local = [pltpu.make_async_copy(src(a, me), outs[a].at[_lin(me)], local_sems.at[a]) for a in range(n)]

    def start():
        for cp in local + sends:
            cp.start()

    def wait():
        for cp in arrivals:
            cp.wait_recv()
        for cp in sends:
            cp.wait_send()
        for cp in local:
            cp.wait()

    return start, wait


def _exchange_scratch(n):
    return [pltpu.SemaphoreType.DMA((n, 7)), pltpu.SemaphoreType.DMA((n, 7)), pltpu.SemaphoreType.DMA((n,))]


SMALL_LANES = 128


def _small_allreduce(dg_rows, dgq, dgk, dsink_g, drel_g, loss8):
    d = dg_rows[0].shape[1]
    kv = dsink_g.shape[0]

    def body(g1_ref, g2_ref, g3_ref, g4_ref, gq_ref, gk_ref, sk_ref, rl_ref, ls_ref, vec_ref, rel_ref,
             vbuf, rbuf, vland, rland, send_sems, recv_sems):
        x, y, c = _mesh_pos()
        me = (x, y, c)
        peers = [(x, y, 1 - c), (1 - x, y, c), (x, 1 - y, c), (1 - x, 1 - y, c),
                 (1 - x, y, 1 - c), (x, 1 - y, 1 - c), (1 - x, 1 - y, 1 - c)]
        vbuf[...] = jnp.zeros_like(vbuf)
        rbuf[...] = jnp.zeros_like(rbuf)
        for row, ref in enumerate((g1_ref, g2_ref, g3_ref, g4_ref)):
            vbuf[row:row + 1, :] = ref[0:1, :]
        vbuf[4:5, 0:HEAD_DIM] = gq_ref[0:1, 0:HEAD_DIM] + gq_ref[0:1, HEAD_DIM:PAIR]
        vbuf[4:5, SMALL_LANES:SMALL_LANES + HEAD_DIM] = gk_ref[0:1, 0:HEAD_DIM] + gk_ref[0:1, HEAD_DIM:PAIR]
        for g in range(kv):
            vbuf[4:5, 2 * SMALL_LANES + g * GROUP:2 * SMALL_LANES + (g + 1) * GROUP] = sk_ref[g, 0:1, 0:GROUP]
            rbuf[:, g * GROUP:(g + 1) * GROUP] = rl_ref[g, :, 0:GROUP]
        vbuf[4:5, 3 * SMALL_LANES:3 * SMALL_LANES + 1] = ls_ref[0:1, 0:1]

        def copies(k, to, frm):
            return [pltpu.make_async_remote_copy(
                src_ref=buf, dst_ref=land.at[_lin(frm)], send_sem=send_sems.at[a, k], recv_sem=recv_sems.at[a, k],
                device_id=to, device_id_type=MESH) for a, (buf, land) in enumerate(((vbuf, vland), (rbuf, rland)))]

        sends = [cp for k, p in enumerate(peers) for cp in copies(k, p, me)]
        for cp in sends:
            cp.start()
        vland[_lin(me)] = vbuf[...]
        rland[_lin(me)] = rbuf[...]
        for k, p in enumerate(peers):
            for cp in copies(k, p, p):
                cp.wait_recv()
        for cp in sends:
            cp.wait_send()
        vacc, racc = vland[0], rland[0]
        for s in range(1, N_DEV):
            vacc, racc = vacc + vland[s], racc + rland[s]
        vec_ref[...] = vacc
        rel_ref[...] = racc

    vm = pl.BlockSpec(memory_space=pltpu.VMEM)
    return pl.pallas_call(
        body,
        out_shape=[SDS((8, d), F32), SDS((N_BUCKETS, 128), F32)],
        in_specs=[vm] * 9,
        out_specs=[vm, vm],
        scratch_shapes=[pltpu.VMEM((8, d), F32), pltpu.VMEM((N_BUCKETS, 128), F32),
                        pltpu.VMEM((N_DEV, 8, d), F32), pltpu.VMEM((N_DEV, N_BUCKETS, 128), F32),
                        pltpu.SemaphoreType.DMA((2, 7)), pltpu.SemaphoreType.DMA((2, 7))],
        name="small_allreduce",
    )(*dg_rows, dgq, dgk, dsink_g, drel_g, loss8)


def _inproj(x2, g1, w_t, tm):
    t, d = x2.shape
    p = w_t.shape[0]

    def body(x_ref, g_ref, w_ref, h_ref, p_ref):
        y, _, _ = _rms_fwd(x_ref[...], g_ref[...])
        h = y.astype(BF16)
        h_ref[...] = h
        p_ref[...] = _dot_nt(h, w_ref[...])

    return pl.pallas_call(
        body,
        grid=(t // tm,),
        in_specs=[pl.BlockSpec((tm, d), lambda i: (i, 0)),
                  pl.BlockSpec((1, d), lambda i: (0, 0)),
                  pl.BlockSpec((p, d), lambda i: (0, 0))],
        out_specs=[pl.BlockSpec((tm, d), lambda i: (i, 0)),
                   pl.BlockSpec((tm, p), lambda i: (i, 0))],
        out_shape=[SDS((t, d), BF16), SDS((t, p), F32)],
        compiler_params=_cp("parallel"),
        name="inproj",
    )(x2, g1, w_t)


PAIR = 2 * HEAD_DIM


def _pair_masks(ts):
    lane = lax.broadcasted_iota(jnp.int32, (ts, PAIR), 1)
    return lane < HEAD_DIM, (lane % 32) < 16


def _pair_mean(v, low):
    lo = jnp.sum(jnp.where(low, v, 0.0), axis=1, keepdims=True)
    hi = jnp.sum(jnp.where(low, 0.0, v), axis=1, keepdims=True)
    return jnp.where(low, lo, hi) * (1.0 / HEAD_DIM)


def _pair_partner(v, first):
    return jnp.where(first, pltpu.roll(v, PAIR - 16, 1), pltpu.roll(v, 16, 1))


def _qkprep(proj, cos, sin_signed, gq, gk, bl, s_len, ha, kva, hb, kvb, ts):
    t, p_cols = proj.shape
    assert ha % 2 == 0 and kva % 2 == 0 and hb % 2 == 0 and kvb % 2 == 0
    ns = s_len // ts
    sp = s_len + 2 * BLOCK

    def body(p_ref, cos_ref, sin_ref, gq_ref, gk_ref, qa_ref, ka_ref, kat_ref, va_ref, vat_ref, qb_ref, kb_ref,
             kbt_ref, vb_ref, vbt_ref):
        i = pl.program_id(1)
        cs, sn = cos_ref[...], sin_ref[...]
        low, first = _pair_masks(ts)
        ones_row = (lax.broadcasted_iota(jnp.int32, (VT_PAD, ts), 0) == 0).astype(BF16)
        heads = (slice(0, HEAD_DIM), slice(HEAD_DIM, PAIR))

        def pair(p):
            return p_ref[:, p * PAIR:(p + 1) * PAIR]

        def normrope(x, g):
            y = x * lax.rsqrt(_pair_mean(x * x, low) + EPS) * g
            return y * cs + _pair_partner(y, first) * sn

        def transposed(xb):
            return xb.astype(F32).T.astype(BF16)

        for p in range(ha // 2):
            q = (normrope(pair(p), gq_ref[...]) * SCALE).astype(BF16)
            for e, lanes in enumerate(heads):
                qa_ref[0, 2 * p + e] = q[:, lanes]
        base = ha // 2
        for p in range(kva // 2):
            k = normrope(pair(base + p), gk_ref[...]).astype(BF16)
            v = pair(base + kva // 2 + p).astype(BF16)
            kt, vt = transposed(k), transposed(v)
            for e, lanes in enumerate(heads):
                ka_ref[0, 2 * p + e] = k[:, lanes]
                va_ref[0, 2 * p + e] = v[:, lanes]
                kat_ref[0, 2 * p + e] = kt[lanes, :]
                vat_ref[0, 2 * p + e, 0:HEAD_DIM, :] = vt[lanes, :]
                vat_ref[0, 2 * p + e, HEAD_DIM:HEAD_DIM + VT_PAD, :] = ones_row
        base += kva
        for p in range(hb // 2):
            q = (pair(base + p) * SCALE).astype(BF16)
            for e, lanes in enumerate(heads):
                qb_ref[0, 2 * p + e] = q[:, lanes]
        base += hb // 2

        @pl.when(i == 0)
        def _():
            zeros = jnp.zeros((kvb, BLOCK, HEAD_DIM), BF16)
            zeros_t = jnp.zeros((kvb, HEAD_DIM + VT_PAD, BLOCK), BF16)
            for ref in (kb_ref, vb_ref):
                ref[0, :, 0:BLOCK, :] = zeros
                ref[0, :, sp - BLOCK:sp, :] = zeros
            kbt_ref[0, :, :, 0:BLOCK] = zeros_t[:, 0:HEAD_DIM]
            kbt_ref[0, :, :, sp - BLOCK:sp] = zeros_t[:, 0:HEAD_DIM]
            vbt_ref[0, :, :, 0:BLOCK] = zeros_t
            vbt_ref[0, :, :, sp - BLOCK:sp] = zeros_t

        rows = pl.ds(pl.multiple_of(BLOCK + i * ts, BLOCK), ts)
        for p in range(kvb // 2):
            k = pair(base + p).astype(BF16)
            v = pair(base + kvb // 2 + p).astype(BF16)
            kt, vt = transposed(k), transposed(v)
            for e, lanes in enumerate(heads):
                kb_ref[0, 2 * p + e, rows, :] = k[:, lanes]
                vb_ref[0, 2 * p + e, rows, :] = v[:, lanes]
                kbt_ref[0, 2 * p + e, :, rows] = kt[lanes, :]
                vbt_ref[0, 2 * p + e, 0:HEAD_DIM, rows] = vt[lanes, :]
                vbt_ref[0, 2 * p + e, HEAD_DIM:HEAD_DIM + VT_PAD, rows] = ones_row

    def hm(nh):
        return pl.BlockSpec((1, nh, ts, HEAD_DIM), lambda b, i: (b, 0, i, 0))

    def padded(nh):
        return pl.BlockSpec((1, nh, sp, HEAD_DIM), lambda b, i: (b, 0, 0, 0))

    def padded_t(nh, rows):
        return pl.BlockSpec((1, nh, rows, sp), lambda b, i: (b, 0, 0, 0))

    return pl.pallas_call(
        body,
        grid=(bl, ns),
        in_specs=[pl.BlockSpec((ts, p_cols), lambda b, i: (b * ns + i, 0)),
                  pl.BlockSpec((ts, PAIR), lambda b, i: (i, 0)),
                  pl.BlockSpec((ts, PAIR), lambda b, i: (i, 0)),
                  pl.BlockSpec((1, PAIR), lambda b, i: (0, 0)),
                  pl.BlockSpec((1, PAIR), lambda b, i: (0, 0))],
        out_specs=[hm(ha), hm(kva), pl.BlockSpec((1, kva, HEAD_DIM, ts), lambda b, i: (b, 0, 0, i)), hm(kva),
                   pl.BlockSpec((1, kva, HEAD_DIM + VT_PAD, ts), lambda b, i: (b, 0, 0, i)),
                   hm(hb), padded(kvb), padded_t(kvb, HEAD_DIM), padded(kvb), padded_t(kvb, HEAD_DIM + VT_PAD)],
        out_shape=[SDS((bl, ha, s_len, HEAD_DIM), BF16), SDS((bl, kva, s_len, HEAD_DIM), BF16),
                   SDS((bl, kva, HEAD_DIM, s_len), BF16),
                   SDS((bl, kva, s_len, HEAD_DIM), BF16), SDS((bl, kva, HEAD_DIM + VT_PAD, s_len), BF16),
                   SDS((bl, hb, s_len, HEAD_DIM), BF16),
                   SDS((bl, kvb, sp, HEAD_DIM), BF16), SDS((bl, kvb, HEAD_DIM, sp), BF16),
                   SDS((bl, kvb, sp, HEAD_DIM), BF16), SDS((bl, kvb, HEAD_DIM + VT_PAD, sp), BF16)],
        compiler_params=_cp("parallel", "arbitrary"),
        name="qkprep",
    )(proj, cos, sin_signed, gq, gk)


def _bias_build(bucket_t, rel_bias, hb):
    kvb = hb // GROUP

    def body(bkt_ref, tbl_ref, out_ref):
        bkt = bkt_ref[...]
        ci = lax.broadcasted_iota(jnp.int32, (SPAN, BLOCK), 0)
        qi = lax.broadcasted_iota(jnp.int32, (SPAN, BLOCK), 1)
        band = jnp.abs(ci - BLOCK - qi) <= BLOCK
        masks = (band, band & (ci >= BLOCK), band & (ci < 2 * BLOCK))
        for h in range(hb):
            acct = jnp.zeros((SPAN, BLOCK), F32)
            for b in range(N_BUCKETS):
                acct = jnp.where(bkt == b, tbl_ref[b, h], acct)
            lanes = slice((h % GROUP) * BLOCK, (h % GROUP + 1) * BLOCK)
            for var, mask in enumerate(masks):
                out_ref[var, h // GROUP, :, lanes] = jnp.where(mask, acct, NEG_INF)

    vm = pl.BlockSpec(memory_space=pltpu.VMEM)
    return pl.pallas_call(
        body,
        in_specs=[vm, pl.BlockSpec(memory_space=pltpu.SMEM)],
        out_specs=vm,
        out_shape=SDS((3, kvb, SPAN, GROUP * BLOCK), F32),
        name="bias_build",
    )(bucket_t, rel_bias)


def _attn_a_fwd(qa, ka, vat, tq, tk, shards):
    bl, ha, s_len, _ = qa.shape
    kv = ka.shape[1]
    va_rows = vat.shape[2]
    nq, nk = s_len // tq, s_len // tk
    assert nk % 2 == 0
    r = GROUP * tq
    ns = len(shards)

    def body(q_ref, qn_ref, k_ref, v_ref, *rest):
        shard_refs, (o_ref, l_ref), gathered = rest[:ns], rest[ns:ns + 2], rest[ns + 2:2 * ns + 2]
        st_sc, send_sems, recv_sems, local_sems = rest[2 * ns + 2:]
        i = pl.program_id(2)
        step_id = (pl.program_id(0) * kv + pl.program_id(1)) * nq + i
        start, wait = _direct_exchange("gather", shard_refs, gathered, send_sems, recv_sems, local_sems)
        pl.when(step_id == 0)(start)

        q = q_ref[0].reshape(r, HEAD_DIM)

        def scores(c, qv):
            return _dot_nt(k_ref[0, 0, pl.ds(pl.multiple_of(c * tk, tk), tk), :], qv)

        def fold(st, c, carry):
            m_old, acc = carry
            m_new = jnp.maximum(m_old, jnp.max(st, axis=0, keepdims=True))
            pt = jnp.exp(st - m_new).astype(BF16)
            vt = v_ref[0, 0, :, pl.ds(pl.multiple_of(c * tk, tk), tk)]
            return m_new, jnp.exp(m_old - m_new) * acc + _dot(vt, pt)

        @pl.when(i == 0)
        def _():
            st_sc[0] = scores(0, q)

        def step(c2, carry):
            c = 2 * c2
            st_sc[1] = scores(c + 1, q)
            carry = fold(st_sc[0], c, carry)
            st_sc[0] = scores(c + 2, q)
            return fold(st_sc[1], c + 1, carry)

        carry = (jnp.full((1, r), -jnp.inf, F32), jnp.zeros((va_rows, r), F32))
        for c2 in range(nk // 2 - 1):
            carry = step(c2, carry)
        st_sc[1] = scores(nk - 1, q)
        carry = fold(st_sc[0], nk - 2, carry)
        st_sc[0] = scores(0, qn_ref[0].reshape(r, HEAD_DIM))
        m, acc = fold(st_sc[1], nk - 1, carry)
        l = acc[HEAD_DIM:HEAD_DIM + 1, :]
        o = (acc[0:HEAD_DIM, :] / l).T
        for h in range(GROUP):
            o_ref[:, h * HEAD_DIM:(h + 1) * HEAD_DIM] = o[h * tq:(h + 1) * tq].astype(BF16)
        l_ref[0, 0, 0] = jnp.broadcast_to(m + jnp.log(l), (8, r))
        pl.when(step_id == bl * kv * nq - 1)(wait)

    anyspec = pl.BlockSpec(memory_space=pl.ANY)
    res = pl.pallas_call(
        body,
        grid=(bl, kv, nq),
        in_specs=[pl.BlockSpec((1, GROUP, tq, HEAD_DIM), lambda b, g, i: (b, g, i, 0)),
                  pl.BlockSpec((1, GROUP, tq, HEAD_DIM), lambda b, g, i: (b, g, jnp.minimum(i + 1, nq - 1), 0)),
                  pl.BlockSpec((1, 1, s_len, HEAD_DIM), lambda b, g, i: (b, g, 0, 0)),
                  pl.BlockSpec((1, 1, va_rows, s_len), lambda b, g, i: (b, g, 0, 0))] + [anyspec] * ns,
        out_specs=[pl.BlockSpec((tq, GROUP * HEAD_DIM), lambda b, g, i: (b * nq + i, g)),
                   pl.BlockSpec((1, 1, 1, 8, r), lambda b, g, i: (b, g, i, 0, 0))] + [anyspec] * ns,
        out_shape=[SDS((bl * s_len, ha * HEAD_DIM), BF16), SDS((bl, kv, nq, 8, r), F32)]
        + [SDS((N_DEV,) + s.shape, s.dtype) for s in shards],
        scratch_shapes=[pltpu.VMEM((2, tk, r), F32)] + _exchange_scratch(ns),
        compiler_params=_cp("arbitrary", "arbitrary", "arbitrary"),
        name="attn_a_fwd",
    )(qa, qa, ka, vat, *shards)
    return res[0], res[1], res[2:]


FFN_BLOCKS_PER_STEP = 8
FFN_BWD_TOKENS = 256
QB_PER_STEP = 16


def _bias_variant(n, nb):
    return jnp.where(n == 0, 1, jnp.where(n == nb - 1, 2, 0))


def _sink_row(sink_ref, g):
    return jnp.concatenate([jnp.full((1, BLOCK), sink_ref[0, g * GROUP + h], F32) for h in range(GROUP)], axis=1)


def _attn_b_fwd(qb, kb, vbt, bias_t, sink, s_len):
    bl, hb, _, _ = qb.shape
    kv = kb.shape[1]
    sp = kb.shape[2]
    vt_rows = vbt.shape[2]
    nb = s_len // BLOCK
    nbs = min(QB_PER_STEP, nb)
    r = GROUP * BLOCK

    def body(q_ref, k_ref, vt_ref, bt_ref, sink_ref, o_ref, l_ref, st_sc, pb_sc):
        g, n0 = pl.program_id(1), pl.program_id(2) * nbs
        sink_row = _sink_row(sink_ref, g)

        def span(j):
            return pl.ds(pl.multiple_of((n0 + j) * BLOCK, BLOCK), SPAN)

        for j in range(nbs):
            q = q_ref[0, :, j * BLOCK:(j + 1) * BLOCK, :].reshape(r, HEAD_DIM)
            st_sc[j] = _dot_nt(k_ref[0, 0, span(j), :], q) + bt_ref[_bias_variant(n0 + j, nb), 0]
        maxes = []
        for j in range(nbs):
            st = st_sc[j]
            m = jnp.maximum(jnp.max(st, axis=0, keepdims=True), sink_row)
            pb_sc[j] = jnp.exp(st - m).astype(BF16)
            maxes.append(m)
        for j in range(nbs):
            m = maxes[j]
            acc = _dot(vt_ref[0, 0, :, span(j)], pb_sc[j])
            l = acc[HEAD_DIM:HEAD_DIM + 1, :] + jnp.exp(sink_row - m)
            o = (acc[0:HEAD_DIM, :] / l).T
            for h in range(GROUP):
                o_ref[j * BLOCK:(j + 1) * BLOCK, h * HEAD_DIM:(h + 1) * HEAD_DIM] = (
                    o[h * BLOCK:(h + 1) * BLOCK].astype(BF16))
            l_ref[0, 0, j] = jnp.broadcast_to(m + jnp.log(l), (8, r))

    return pl.pallas_call(
        body,
        grid=(bl, kv, nb // nbs),
        in_specs=[pl.BlockSpec((1, GROUP, nbs * BLOCK, HEAD_DIM), lambda b, g, n: (b, g, n, 0)),
                  pl.BlockSpec((1, 1, sp, HEAD_DIM), lambda b, g, n: (b, g, 0, 0)),
                  pl.BlockSpec((1, 1, vt_rows, sp), lambda b, g, n: (b, g, 0, 0)),
                  pl.BlockSpec((3, 1, SPAN, r), lambda b, g, n: (0, g, 0, 0)),
                  pl.BlockSpec(memory_space=pltpu.SMEM)],
        out_specs=[pl.BlockSpec((nbs * BLOCK, GROUP * HEAD_DIM), lambda b, g, n: (b * (nb // nbs) + n, g)),
                   pl.BlockSpec((1, 1, nbs, 8, r), lambda b, g, n: (b, g, n, 0, 0))],
        out_shape=[SDS((bl * s_len, hb * HEAD_DIM), BF16), SDS((bl, kv, nb, 8, r), F32)],
        scratch_shapes=[pltpu.VMEM((nbs, SPAN, r), F32), pltpu.VMEM((nbs, SPAN, r), BF16)],
        compiler_params=_cp("parallel", "parallel", "arbitrary"),
        name="attn_b_fwd",
    )(qb, kb, vbt, bias_t, sink)


def _mixout(oa, ob, wo, x2, g2, g3, tm):
    t, d = x2.shape
    ca = oa.shape[1]

    def body(oa_ref, ob_ref, w_ref, x_ref, g2_ref, g3_ref, mix_ref, x1_ref, h2_ref):
        mix = _dot(oa_ref[...], w_ref[0:ca, :]) + _dot(ob_ref[...], w_ref[ca:, :])
        mix_ref[...] = mix
        y2, _, _ = _rms_fwd(mix, g2_ref[...])
        x1 = x_ref[...] + y2
        x1_ref[...] = x1
        y3, _, _ = _rms_fwd(x1, g3_ref[...])
        h2_ref[...] = y3.astype(BF16)

    tile = lambda w: pl.BlockSpec((tm, w), lambda i: (i, 0))
    vec = pl.BlockSpec((1, d), lambda i: (0, 0))
    return pl.pallas_call(
        body,
        grid=(t // tm,),
        in_specs=[tile(ca), tile(ob.shape[1]), pl.BlockSpec(wo.shape, lambda i: (0, 0)), tile(d), vec, vec],
        out_specs=[tile(d), tile(d), tile(d)],
        out_shape=[SDS((t, d), F32), SDS((t, d), F32), SDS((t, d), BF16)],
        compiler_params=_cp("parallel"),
        name="mixout",
    )(oa, ob, wo, x2, g2, g3)


def _ffn_fwd(h2, wup_g, wdn, x1, target, g4, tm, jb):
    t, d = x1.shape
    nblk, _, tf = wup_g.shape
    ff = nblk * tf
    nt = t // tm
    nj = nblk // jb

    def body(h_ref, wu_ref, wd_ref, x1_ref, tg_ref, g_ref, u_ref, df_ref, dy_ref, dg_ref, loss_ref, acc_sc):
        i, j = pl.program_id(0), pl.program_id(1)

        @pl.when(j == 0)
        def _():
            acc_sc[...] = jnp.zeros_like(acc_sc)

        @pl.when((i == 0) & (j == 0))
        def _():
            dg_ref[...] = jnp.zeros_like(dg_ref)
            loss_ref[...] = jnp.zeros_like(loss_ref)

        h = h_ref[...]
        squares = []
        for s in range(jb):
            u = jnp.maximum(_dot(h, wu_ref[s]), 0.0)
            u_ref[:, s * tf:(s + 1) * tf] = u.astype(BF16)
            squares.append((u * u).astype(BF16))
        acc_sc[...] += _dot(jnp.concatenate(squares, axis=1), wd_ref[...])

        @pl.when(j == nj - 1)
        def _():
            g = g_ref[...]
            y4, n, r = _rms_fwd(acc_sc[...], g)
            e = (x1_ref[...] + y4) - tg_ref[...]
            loss_ref[...] += jnp.sum(e * e) * (0.5 / d)
            dy = e * (1.0 / d)
            dy_ref[...] = dy
            df, dgt = _rms_bwd(n, r, g, dy)
            df_ref[...] = df.astype(BF16)
            dg_ref[0:1, :] += jnp.sum(dgt, axis=0, keepdims=True)

    tile = pl.BlockSpec((tm, d), lambda i, j: (i, 0))
    return pl.pallas_call(
        body,
        grid=(nt, nj),
        in_specs=[tile,
                  pl.BlockSpec((jb, d, tf), lambda i, j: (j, 0, 0)),
                  pl.BlockSpec((jb * tf, d), lambda i, j: (j, 0)),
                  tile, tile,
                  pl.BlockSpec((1, d), lambda i, j: (0, 0))],
        out_specs=[pl.BlockSpec((tm, jb * tf), lambda i, j: (i, j)), tile, tile,
                   pl.BlockSpec((8, d), lambda i, j: (0, 0)),
                   pl.BlockSpec((8, 128), lambda i, j: (0, 0))],
        out_shape=[SDS((t, ff), BF16), SDS((t, d), BF16), SDS((t, d), F32), SDS((8, d), F32), SDS((8, 128), F32)],
        scratch_shapes=[pltpu.VMEM((tm, d), F32)],
        compiler_params=_cp("arbitrary", "arbitrary"),
        name="ffn_fwd",
    )(h2, wup_g, wdn, x1, target, g4)


def _ffn_bwd(df, u, wdn, wup_g, x1, dy, mix, g3, g2, tm, jb):
    t, d = x1.shape
    nblk, _, tf = wup_g.shape
    nt = t // tm
    nj = nblk // jb

    def body(df_ref, u_ref, wd_ref, wu_ref, x1_ref, dy_ref, mix_ref, g3_ref, g2_ref,
             dpre_ref, dx1_ref, dmix_ref, dg3_ref, dg2_ref, acc_sc):
        i, j = pl.program_id(0), pl.program_id(1)

        @pl.when(j == 0)
        def _():
            acc_sc[...] = jnp.zeros_like(acc_sc)

        @pl.when((i == 0) & (j == 0))
        def _():
            dg3_ref[...] = jnp.zeros_like(dg3_ref)
            dg2_ref[...] = jnp.zeros_like(dg2_ref)

        du2 = _dot_nt(df_ref[...], wd_ref[...])
        dpre = (2.0 * u_ref[...].astype(F32) * du2).astype(BF16)
        dpre_ref[...] = dpre
        dh = _dot_nt(dpre[:, 0:tf], wu_ref[0])
        for s in range(1, jb):
            dh = dh + _dot_nt(dpre[:, s * tf:(s + 1) * tf], wu_ref[s])
        acc_sc[...] += dh

        @pl.when(j == nj - 1)
        def _():
            g3, g2 = g3_ref[...], g2_ref[...]
            _, n3, r3 = _rms_fwd(x1_ref[...], g3)
            dx, dgt3 = _rms_bwd(n3, r3, g3, acc_sc[...])
            dx1 = dy_ref[...] + dx
            dx1_ref[...] = dx1
            dg3_ref[0:1, :] += jnp.sum(dgt3, axis=0, keepdims=True)
            _, n2, r2 = _rms_fwd(mix_ref[...], g2)
            dmix, dgt2 = _rms_bwd(n2, r2, g2, dx1)
            dmix_ref[...] = dmix.astype(BF16)
            dg2_ref[0:1, :] += jnp.sum(dgt2, axis=0, keepdims=True)

    tile = pl.BlockSpec((tm, d), lambda i, j: (i, 0))
    vec = pl.BlockSpec((1, d), lambda i, j: (0, 0))
    acc8 = pl.BlockSpec((8, d), lambda i, j: (0, 0))
    return pl.pallas_call(
        body,
        grid=(nt, nj),
        in_specs=[tile,
                  pl.BlockSpec((tm, jb * tf), lambda i, j: (i, j)),
                  pl.BlockSpec((jb * tf, d), lambda i, j: (j, 0)),
                  pl.BlockSpec((jb, d, tf), lambda i, j: (j, 0, 0)),
                  tile, tile, tile, vec, vec],
        out_specs=[pl.BlockSpec((tm, jb * tf), lambda i, j: (i, j)), tile, tile, acc8, acc8],
        out_shape=[SDS(u.shape, BF16), SDS((t, d), F32), SDS((t, d), BF16), SDS((8, d), F32), SDS((8, d), F32)],
        scratch_shapes=[pltpu.VMEM((tm, d), F32)],
        compiler_params=_cp("arbitrary", "arbitrary"),
        name="ffn_bwd",
    )(df, u, wdn, wup_g, x1, dy, mix, g3, g2)


def _wgrad(a, b, a_spec, b_spec, out_block, out_shape, nj, nk, name, prep_a=None, prep_b=None):
    acc_shape = out_block[1:]

    def body(a_ref, b_ref, o_ref, acc_sc):
        k = pl.program_id(1)
        av = a_ref[...] if prep_a is None else prep_a(a_ref)
        bv = b_ref[...] if prep_b is None else prep_b(b_ref)
        part = _dot_tn(av, bv)

        @pl.when(k == 0)
        def _():
            acc_sc[...] = part

        @pl.when(k > 0)
        def _():
            acc_sc[...] += part

        @pl.when(k == nk - 1)
        def _():
            o_ref[0] = acc_sc[...].astype(BF16)

    return pl.pallas_call(
        body,
        grid=(nj, nk),
        in_specs=[a_spec, b_spec],
        out_specs=pl.BlockSpec(out_block, lambda j, k: (j, 0, 0)),
        out_shape=SDS(out_shape, BF16),
        scratch_shapes=[pltpu.VMEM(acc_shape, F32)],
        compiler_params=_cp("parallel", "arbitrary"),
        name=name,
    )(a, b)


def _wgrad_cols(a, b, nj, tt, name):
    t, m = a.shape
    bn = b.shape[1] // nj
    return _wgrad(a, b, pl.BlockSpec((tt, m), lambda j, k: (k, 0)), pl.BlockSpec((tt, bn), lambda j, k: (k, j)),
                  (1, m, bn), (nj, m, bn), nj, t // tt, name)


def _wgrad_rows(a, b, nj, tt, name, square=False):
    t, n = b.shape
    bm = a.shape[1] // nj

    def squared(a_ref):
        af = a_ref[...].astype(F32)
        return (af * af).astype(BF16)

    return _wgrad(a, b, pl.BlockSpec((tt, bm), lambda j, k: (k, j)), pl.BlockSpec((tt, n), lambda j, k: (k, 0)),
                  (1, bm, n), (nj, bm, n), nj, t // tt, name, prep_a=squared if square else None)


def _wgrad_o(oa, ob, dmix, nj, tt):
    t, n = dmix.shape
    ca, cb = oa.shape[1], ob.shape[1]
    m = ca + cb
    nk = t // tt

    def body(oa_ref, ob_ref, b_ref, o_ref, acc_sc):
        k = pl.program_id(0)
        part = _dot_tn(jnp.concatenate([oa_ref[...], ob_ref[...]], axis=1), b_ref[...])

        @pl.when(k == 0)
        def _():
            acc_sc[...] = part

        @pl.when(k > 0)
        def _():
            acc_sc[...] += part

        @pl.when(k == nk - 1)
        def _():
            o_ref[...] = acc_sc[...].reshape(nj, m // nj, n).astype(BF16)

    return pl.pallas_call(
        body,
        grid=(nk,),
        in_specs=[pl.BlockSpec((tt, ca), lambda k: (k, 0)), pl.BlockSpec((tt, cb), lambda k: (k, 0)),
                  pl.BlockSpec((tt, n), lambda k: (k, 0))],
        out_specs=pl.BlockSpec((nj, m // nj, n), lambda k: (0, 0, 0)),
        out_shape=SDS((nj, m // nj, n), BF16),
        scratch_shapes=[pltpu.VMEM((m, n), F32)],
        compiler_params=_cp("arbitrary"),
        name="wgrad_o",
    )(oa, ob, dmix)


def _attn_out_bwd(dmix, wo, ca, tm):
    t, d = dmix.shape
    cb = wo.shape[0] - ca

    def body(dm_ref, w_ref, da_ref, db_ref):
        dm = dm_ref[...]
        da_ref[...] = _dot_nt(dm, w_ref[0:ca, :]).astype(BF16)
        db_ref[...] = _dot_nt(dm, w_ref[ca:, :]).astype(BF16)

    return pl.pallas_call(
        body,
        grid=(t // tm,),
        in_specs=[pl.BlockSpec((tm, d), lambda i: (i, 0)), pl.BlockSpec(wo.shape, lambda i: (0, 0))],
        out_specs=[pl.BlockSpec((tm, ca), lambda i: (i, 0)), pl.BlockSpec((tm, cb), lambda i: (i, 0))],
        out_shape=[SDS((t, ca), BF16), SDS((t, cb), BF16)],
        compiler_params=_cp("parallel"),
        name="attn_out_bwd",
    )(dmix, wo)


def _stack_heads(ref):
    return jnp.concatenate([ref[:, h * HEAD_DIM:(h + 1) * HEAD_DIM] for h in range(GROUP)], axis=0)


def _attn_a_bwd(qa, ka, kat, va, do, o, lse, tq, tk, grads):
    bl, ha, s_len, _ = qa.shape
    kv = ka.shape[1]
    nq, nk = s_len // tq, s_len // tk
    assert nk % 2 == 0
    r = GROUP * tq
    ng = len(grads)

    def body(q_ref, qn_ref, k_ref, kt_ref, v_ref, do_ref, don_ref, o_ref, l_ref, *rest):
        grad_refs, (dq_ref, dk_ref, dv_ref), parts = rest[:ng], rest[ng:ng + 3], rest[ng + 3:2 * ng + 3]
        st_sc, dp_sc, dkt_sc, dvt_sc, send_sems, recv_sems, local_sems = rest[2 * ng + 3:]
        i = pl.program_id(2)
        step_id = (pl.program_id(0) * kv + pl.program_id(1)) * nq + i
        start, wait = _direct_exchange("scatter", grad_refs, parts, send_sems, recv_sems, local_sems)
        pl.when(step_id == 0)(start)

        q = q_ref[0].reshape(r, HEAD_DIM)
        do2 = _stack_heads(do_ref)
        qt = q.astype(F32).T
        dot32 = do2.astype(F32).T
        ot32 = _stack_heads(o_ref).astype(F32).T
        drow = jnp.sum(dot32 * ot32, axis=0, keepdims=True)
        qt, dot = qt.astype(BF16), dot32.astype(BF16)
        lrow = l_ref[0, 0, 0, 0:1, :]

        @pl.when(i == 0)
        def _():
            dkt_sc[...] = jnp.zeros_like(dkt_sc)
            dvt_sc[...] = jnp.zeros_like(dvt_sc)

        def chunk(c):
            return pl.ds(pl.multiple_of(c * tk, tk), tk)

        def scores(c, slot, qv=q, dov=do2):
            st_sc[slot] = _dot_nt(k_ref[0, 0, chunk(c), :], qv)
            dp_sc[slot] = _dot_nt(v_ref[0, 0, chunk(c), :], dov)

        def fold(slot, c, dqt):
            pt = jnp.exp(st_sc[slot] - lrow)
            dsb = (pt * (dp_sc[slot] - drow)).astype(BF16)
            dvt_sc[:, chunk(c)] += _dot_nt(dot, pt.astype(BF16))
            dkt_sc[:, chunk(c)] += _dot_nt(qt, dsb)
            return dqt + _dot(kt_ref[0, 0, :, chunk(c)], dsb)

        @pl.when(i == 0)
        def _():
            scores(0, 0)

        def step(c2, dqt):
            c = 2 * c2
            scores(c + 1, 1)
            dqt = fold(0, c, dqt)
            scores(c + 2, 0)
            return fold(1, c + 1, dqt)

        dqt = jnp.zeros((HEAD_DIM, r), F32)
        for c2 in range(nk // 2 - 1):
            dqt = step(c2, dqt)
        scores(nk - 1, 1)
        dqt = fold(0, nk - 2, dqt)
        scores(0, 0, qn_ref[0].reshape(r, HEAD_DIM), _stack_heads(don_ref))
        dqt = fold(1, nk - 1, dqt)
        dq_ref[0] = dqt.T.reshape(GROUP, tq, HEAD_DIM)

        @pl.when(i == nq - 1)
        def _():
            dk_ref[0, 0] = dkt_sc[...].T
            dv_ref[0, 0] = dvt_sc[...].T

        pl.when(step_id == bl * kv * nq - 1)(wait)

    kvspec = pl.BlockSpec((1, 1, s_len, HEAD_DIM), lambda b, g, i: (b, g, 0, 0))
    qspec = pl.BlockSpec((1, GROUP, tq, HEAD_DIM), lambda b, g, i: (b, g, i, 0))
    tok = pl.BlockSpec((tq, GROUP * HEAD_DIM), lambda b, g, i: (b * nq + i, g))
    qnext = pl.BlockSpec((1, GROUP, tq, HEAD_DIM), lambda b, g, i: (b, g, jnp.minimum(i + 1, nq - 1), 0))
    toknext = pl.BlockSpec((tq, GROUP * HEAD_DIM), lambda b, g, i: (b * nq + jnp.minimum(i + 1, nq - 1), g))
    anyspec = pl.BlockSpec(memory_space=pl.ANY)
    res = pl.pallas_call(
        body,
        grid=(bl, kv, nq),
        in_specs=[qspec, qnext, kvspec, pl.BlockSpec((1, 1, HEAD_DIM, s_len), lambda b, g, i: (b, g, 0, 0)), kvspec,
                  tok, toknext, tok, pl.BlockSpec((1, 1, 1, 8, r), lambda b, g, i: (b, g, i, 0, 0))] + [anyspec] * ng,
        out_specs=[qspec, kvspec, kvspec] + [anyspec] * ng,
        out_shape=[SDS(qa.shape, F32), SDS(ka.shape, F32), SDS(va.shape, F32)]
        + [SDS(g.shape, g.dtype) for g in grads],
        scratch_shapes=[pltpu.VMEM((2, tk, r), F32), pltpu.VMEM((2, tk, r), F32),
                        pltpu.VMEM((HEAD_DIM, s_len), F32), pltpu.VMEM((HEAD_DIM, s_len), F32)]
        + _exchange_scratch(ng),
        compiler_params=_cp("arbitrary", "arbitrary", "arbitrary"),
        name="attn_a_bwd",
    )(qa, qa, ka, kat, va, do, do, o, lse, *grads)
    return res[0], res[1], res[2], res[3:]


def _attn_b_bwd(qb, kb, kbt, vb, do, o, lse, bias_t, sink, s_len):
    bl, hb, _, _ = qb.shape
    kv, sp = kb.shape[1], kb.shape[2]
    nb = s_len // BLOCK
    nbs = min(QB_PER_STEP, nb)
    r = GROUP * BLOCK

    def body(q_ref, k_ref, kt_ref, v_ref, do_ref, o_ref, l_ref, bt_ref, sink_ref,
             dq_ref, dk_ref, dv_ref, dsum_ref, dsink_ref, dkt_sc, dvt_sc):
        g, b, ns = pl.program_id(0), pl.program_id(1), pl.program_id(2)
        sink_row = _sink_row(sink_ref, g)

        @pl.when(ns == 0)
        def _():
            dkt_sc[...] = jnp.zeros_like(dkt_sc)
            dvt_sc[...] = jnp.zeros_like(dvt_sc)

        @pl.when((b == 0) & (ns == 0))
        def _():
            dsum_ref[...] = jnp.zeros_like(dsum_ref)
            dsink_ref[...] = jnp.zeros_like(dsink_ref)

        dsum = jnp.zeros((SPAN, r), F32)
        dsink = jnp.zeros((1, r), F32)
        for j in range(nbs):
            n = ns * nbs + j
            span = pl.ds(pl.multiple_of(n * BLOCK, BLOCK), SPAN)
            rows = slice(j * BLOCK, (j + 1) * BLOCK)
            q = q_ref[0, :, rows, :].reshape(r, HEAD_DIM)
            do2 = jnp.concatenate([do_ref[rows, h * HEAD_DIM:(h + 1) * HEAD_DIM] for h in range(GROUP)], axis=0)
            o2 = jnp.concatenate([o_ref[rows, h * HEAD_DIM:(h + 1) * HEAD_DIM] for h in range(GROUP)], axis=0)
            dot32 = do2.astype(F32).T
            drow = jnp.sum(dot32 * o2.astype(F32).T, axis=0, keepdims=True)
            qt, dot = q.astype(F32).T.astype(BF16), dot32.astype(BF16)
            lrow = l_ref[0, 0, j, 0:1, :]
            st = _dot_nt(k_ref[0, 0, span, :], q) + bt_ref[_bias_variant(n, nb), 0]
            pt = jnp.exp(st - lrow)
            dst = pt * (_dot_nt(v_ref[0, 0, span, :], do2) - drow)
            dsum = dsum + dst
            dsink = dsink - jnp.exp(sink_row - lrow) * drow
            dsb = dst.astype(BF16)
            dvt_sc[:, span] += _dot_nt(dot, pt.astype(BF16))
            dkt_sc[:, span] += _dot_nt(qt, dsb)
            dq_ref[0, :, rows, :] = _dot(kt_ref[0, 0, :, span], dsb).T.reshape(GROUP, BLOCK, HEAD_DIM)
        dsum_ref[0] += dsum
        dsink_ref[0, 0:1, :] += dsink

        @pl.when(ns == nb // nbs - 1)
        def _():
            dk_ref[0, 0] = dkt_sc[:, BLOCK:BLOCK + s_len].T
            dv_ref[0, 0] = dvt_sc[:, BLOCK:BLOCK + s_len].T

    kvspec = pl.BlockSpec((1, 1, sp, HEAD_DIM), lambda g, b, n: (b, g, 0, 0))
    kvout = pl.BlockSpec((1, 1, s_len, HEAD_DIM), lambda g, b, n: (b, g, 0, 0))
    qspec = pl.BlockSpec((1, GROUP, nbs * BLOCK, HEAD_DIM), lambda g, b, n: (b, g, n, 0))
    tok = pl.BlockSpec((nbs * BLOCK, GROUP * HEAD_DIM), lambda g, b, n: (b * (nb // nbs) + n, g))
    return pl.pallas_call(
        body,
        grid=(kv, bl, nb // nbs),
        in_specs=[qspec, kvspec, pl.BlockSpec((1, 1, HEAD_DIM, sp), lambda g, b, n: (b, g, 0, 0)), kvspec, tok, tok,
                  pl.BlockSpec((1, 1, nbs, 8, r), lambda g, b, n: (b, g, n, 0, 0)),
                  pl.BlockSpec((3, 1, SPAN, r), lambda g, b, n: (0, g, 0, 0)),
                  pl.BlockSpec(memory_space=pltpu.SMEM)],
        out_specs=[qspec, kvout, kvout,
                   pl.BlockSpec((1, SPAN, r), lambda g, b, n: (g, 0, 0)),
                   pl.BlockSpec((1, 8, r), lambda g, b, n: (g, 0, 0))],
        out_shape=[SDS(qb.shape, F32), SDS((bl, kv, s_len, HEAD_DIM), F32), SDS((bl, kv, s_len, HEAD_DIM), F32),
                   SDS((kv, SPAN, r), F32), SDS((kv, 8, r), F32)],
        scratch_shapes=[pltpu.VMEM((HEAD_DIM, sp), F32), pltpu.VMEM((HEAD_DIM, sp), F32)],
        compiler_params=_cp("arbitrary", "arbitrary", "arbitrary"),
        name="attn_b_bwd",
    )(qb, kb, kbt, vb, do, o, lse, bias_t, sink)


def _bias_reduce(dsum, dsink, bucket_t4):
    kv, _, r = dsum.shape

    def body(ds_ref, dk_ref, bk_ref, rel_ref, sink_ref):
        lane = lax.broadcasted_iota(jnp.int32, (N_BUCKETS, 128), 1)
        lane8 = lax.broadcasted_iota(jnp.int32, (8, 128), 1)
        bk = bk_ref[...]
        for g in range(kv):
            ds = ds_ref[g]
            rowi = lax.broadcasted_iota(jnp.int32, (N_BUCKETS, r), 0)
            red = jnp.zeros((N_BUCKETS, r), F32)
            for b in range(N_BUCKETS):
                red = jnp.where(rowi == b, jnp.sum(jnp.where(bk == b, ds, 0.0), axis=0, keepdims=True), red)
            out = jnp.zeros((N_BUCKETS, 128), F32)
            so = jnp.zeros((8, 128), F32)
            for h in range(GROUP):
                col = jnp.sum(red[:, h * BLOCK:(h + 1) * BLOCK], axis=1, keepdims=True)
                out = jnp.where(lane == h, col, out)
                sc = jnp.sum(dk_ref[g][:, h * BLOCK:(h + 1) * BLOCK], axis=1, keepdims=True)
                so = jnp.where(lane8 == h, sc, so)
            rel_ref[g] = out
            sink_ref[g] = so

    vm = pl.BlockSpec(memory_space=pltpu.VMEM)
    return pl.pallas_call(
        body,
        in_specs=[vm, vm, vm],
        out_specs=[vm, vm],
        out_shape=[SDS((kv, N_BUCKETS, 128), F32), SDS((kv, 8, 128), F32)],
        name="bias_reduce",
    )(dsum, dsink, bucket_t4)


def _dqkprep(dqa, dka, dva, dqb, dkb, dvb, proj, cos, sin_signed, gq, gk, s_len, ts):
    t, p_cols = proj.shape
    bl, ha = dqa.shape[0], dqa.shape[1]
    kva, hb, kvb = dka.shape[1], dqb.shape[1], dkb.shape[1]
    ns = s_len // ts

    def body(dqa_ref, dka_ref, dva_ref, dqb_ref, dkb_ref, dvb_ref, p_ref, cos_ref, sin_ref, gq_ref, gk_ref,
             dp_ref, dgq_ref, dgk_ref):
        b, i = pl.program_id(0), pl.program_id(1)
        cs, sn = cos_ref[...], sin_ref[...]
        low, first = _pair_masks(ts)

        @pl.when((b == 0) & (i == 0))
        def _():
            dgq_ref[...] = jnp.zeros_like(dgq_ref)
            dgk_ref[...] = jnp.zeros_like(dgk_ref)

        def grad_pair(ref, p):
            return jnp.concatenate([ref[0, 2 * p], ref[0, 2 * p + 1]], axis=1)

        def put(p, val):
            dp_ref[:, p * PAIR:(p + 1) * PAIR] = val.astype(BF16)

        def unrope_norm(d_rot, p, g, dg_ref):
            dn = d_rot * cs + _pair_partner(d_rot * sn, first)
            xp = p_ref[:, p * PAIR:(p + 1) * PAIR]
            r = lax.rsqrt(_pair_mean(xp * xp, low) + EPS)
            n = xp * r
            gd = g * dn
            dg_ref[0:1, :] += jnp.sum(dn * n, axis=0, keepdims=True)
            put(p, r * (gd - n * _pair_mean(n * gd, low)))

        for p in range(ha // 2):
            unrope_norm(grad_pair(dqa_ref, p) * SCALE, p, gq_ref[...], dgq_ref)
        base = ha // 2
        for p in range(kva // 2):
            unrope_norm(grad_pair(dka_ref, p), base + p, gk_ref[...], dgk_ref)
            put(base + kva // 2 + p, grad_pair(dva_ref, p))
        base += kva
        for p in range(hb // 2):
            put(base + p, grad_pair(dqb_ref, p) * SCALE)
        base += hb // 2
        for p in range(kvb // 2):
            put(base + p, grad_pair(dkb_ref, p))
            put(base + kvb // 2 + p, grad_pair(dvb_ref, p))

    def hm(nh):
        return pl.BlockSpec((1, nh, ts, HEAD_DIM), lambda b, i: (b, 0, i, 0))

    vec = pl.BlockSpec((1, PAIR), lambda b, i: (0, 0))
    tab = pl.BlockSpec((ts, PAIR), lambda b, i: (i, 0))
    acc = pl.BlockSpec((8, PAIR), lambda b, i: (0, 0))
    pspec = pl.BlockSpec((ts, p_cols), lambda b, i: (b * ns + i, 0))
    return pl.pallas_call(
        body,
        grid=(bl, ns),
        in_specs=[hm(ha), hm(kva), hm(kva), hm(hb), hm(kvb), hm(kvb), pspec, tab, tab, vec, vec],
        out_specs=[pspec, acc, acc],
        out_shape=[SDS((t, p_cols), BF16), SDS((8, PAIR), F32), SDS((8, PAIR), F32)],
        compiler_params=_cp("arbitrary", "arbitrary"),
        name="dqkprep",
    )(dqa, dka, dva, dqb, dkb, dvb, proj, cos, sin_signed, gq, gk)


def _dx_final(dproj, w_t, x2, dx1, g1, tm, grads):
    t, d = x2.shape
    p_cols = w_t.shape[0]
    ng = len(grads)
    nsteps = t // tm

    def body(dp_ref, w_ref, x_ref, dx1_ref, g_ref, *rest):
        grad_refs, (dx_ref, dg_ref), parts = rest[:ng], rest[ng:ng + 2], rest[ng + 2:2 * ng + 2]
        start, wait = _direct_exchange("scatter", grad_refs, parts, *rest[2 * ng + 2:])

        @pl.when(pl.program_id(0) == 0)
        def _():
            start()
            dg_ref[...] = jnp.zeros_like(dg_ref)

        dh = _dot(dp_ref[...], w_ref[...])
        g = g_ref[...]
        _, n, r = _rms_fwd(x_ref[...], g)
        dx, dgt = _rms_bwd(n, r, g, dh)
        dx_ref[...] = dx1_ref[...] + dx
        dg_ref[0:1, :] += jnp.sum(dgt, axis=0, keepdims=True)
        pl.when(pl.program_id(0) == nsteps - 1)(wait)

    tile = pl.BlockSpec((tm, d), lambda i: (i, 0))
    anyspec = pl.BlockSpec(memory_space=pl.ANY)
    res = pl.pallas_call(
        body,
        grid=(nsteps,),
        in_specs=[pl.BlockSpec((tm, p_cols), lambda i: (i, 0)),
                  pl.BlockSpec((p_cols, d), lambda i: (0, 0)),
                  tile, tile, pl.BlockSpec((1, d), lambda i: (0, 0))] + [anyspec] * ng,
        out_specs=[tile, pl.BlockSpec((8, d), lambda i: (0, 0))] + [anyspec] * ng,
        out_shape=[SDS((t, d), F32), SDS((8, d), F32)] + [SDS(g.shape, g.dtype) for g in grads],
        scratch_shapes=_exchange_scratch(ng),
        compiler_params=_cp("arbitrary"),
        name="dx_final",
    )(dproj, w_t, x2, dx1, g1, *grads)
    return res[0], res[1], res[2:]


def _adamw_math(w, g, m, v):
    m = ADAM_B1 * m + (1.0 - ADAM_B1) * g
    v = ADAM_B2 * v + (1.0 - ADAM_B2) * (g * g)
    m_hat = m / (1.0 - ADAM_B1 ** ADAM_STEP)
    v_hat = v / (1.0 - ADAM_B2 ** ADAM_STEP)
    delta = -ADAM_LR * (m_hat / (jnp.sqrt(v_hat) + ADAM_EPS) + ADAM_WD * w)
    return delta, m, v


def _adamw_sum(parts, w, m, v, tr, name):
    rows, cols = w.shape

    def body(p_ref, w_ref, m_ref, v_ref, g_ref, d_ref, nm_ref, nv_ref):
        g = p_ref[0].astype(F32)
        for s in range(1, N_DEV):
            g = g + p_ref[s].astype(F32)
        g_ref[...] = g
        d_ref[...], nm_ref[...], nv_ref[...] = _adamw_math(w_ref[...], g, m_ref[...], v_ref[...])

    tr = min(tr, rows)
    tile = pl.BlockSpec((tr, cols), lambda i: (i, 0))
    return pl.pallas_call(
        body,
        grid=(rows // tr,),
        in_specs=[pl.BlockSpec((N_DEV, tr, cols), lambda i: (0, i, 0)), tile, tile, tile],
        out_specs=[tile] * 4,
        out_shape=[SDS((rows, cols), F32)] * 4,
        compiler_params=_cp("parallel"),
        name=name,
    )(parts, w, m, v)


def _adamw_small(vec, rel, ws, ms, vs):
    hb = ws[6].shape[1]
    n = len(ws)

    def body(vec_ref, rel_ref, *rest):
        w_refs, m_refs, v_refs = rest[:n], rest[n:2 * n], rest[2 * n:3 * n]
        loss_ref, outs = rest[3 * n], rest[3 * n + 1:]
        grads = [vec_ref[0:1, :], vec_ref[1:2, :], vec_ref[2:3, :], vec_ref[3:4, :],
                 vec_ref[4:5, 0:HEAD_DIM], vec_ref[4:5, SMALL_LANES:SMALL_LANES + HEAD_DIM],
                 vec_ref[4:5, 2 * SMALL_LANES:2 * SMALL_LANES + hb], rel_ref[:, 0:hb]]
        loss_ref[...] = vec_ref[4:5, 3 * SMALL_LANES:3 * SMALL_LANES + 1]
        for p, g in enumerate(grads):
            g_ref, d_ref, nm_ref, nv_ref = outs[4 * p:4 * p + 4]
            g_ref[...] = g
            d_ref[...], nm_ref[...], nv_ref[...] = _adamw_math(w_refs[p][...], g, m_refs[p][...], v_refs[p][...])

    vm = pl.BlockSpec(memory_space=pltpu.VMEM)
    res = pl.pallas_call(
        body,
        in_specs=[vm] * (2 + 3 * n),
        out_specs=[vm] * (1 + 4 * n),
        out_shape=[SDS((1, 1), F32)] + [SDS(w.shape, F32) for w in ws for _ in range(4)],
        name="adamw_small",
    )(vec, rel, *ws, *ms, *vs)
    return res[0], [res[1 + 4 * p:5 + 4 * p] for p in range(n)]


def _local_step(x, loss_target, win_g, wo_s, wup_s, wdn_s, g_pre_mix, g_post_mix, q_norm_a, k_norm_a, sink_b,
                rel_bias, g_pre_ffn, g_post_ffn):
    bl, s_len, d = x.shape
    t = bl * s_len
    nh = d // HEAD_DIM
    ha = nh // 2
    kva = ha // GROUP
    hb = nh - ha
    kvb = hb // GROUP
    tm = 512
    tw = min(4096, t)
    ts = min(512, s_len)
    tq, tk = 2 * BLOCK, min(512, s_len // 2)

    x2 = x.reshape(t, d)
    tg2 = loss_target.reshape(t, d)
    cos, sin_signed = _rope_tables(s_len)
    gq2, gk2 = jnp.tile(q_norm_a, (1, 2)), jnp.tile(k_norm_a, (1, 2))
    a = jnp.arange(BLOCK, dtype=jnp.int32)
    c = jnp.arange(SPAN, dtype=jnp.int32)
    bucket_t = _t5_bucket(c[:, None] - BLOCK - a[None, :])
    bucket_t4 = jnp.tile(bucket_t, (1, GROUP))
    w_in_t = win_g.reshape(-1, d)
    p_cols = w_in_t.shape[0]

    h1, proj = _inproj(x2, g_pre_mix, w_in_t, tm)
    qa, ka, kat, va, vat, qb, kb, kbt, vb, vbt = _qkprep(
        proj, cos, sin_signed, gq2, gk2, bl, s_len, ha, kva, hb, kvb, ts)
    bias_t = _bias_build(bucket_t, rel_bias, hb)
    oa, lse_a, (wo_g, wup_g, wdn_g) = _attn_a_fwd(qa, ka, vat, tq, tk, [wo_s, wup_s, wdn_s])
    wo = wo_g.reshape(-1, d)
    wdn = wdn_g.reshape(-1, d)
    ob, lse_b = _attn_b_fwd(qb, kb, vbt, bias_t, sink_b, s_len)
    mix, x1, h2 = _mixout(oa, ob, wo, x2, g_post_mix, g_pre_ffn, tm)
    u, df, dy, dg4, loss8 = _ffn_fwd(h2, wup_g, wdn, x1, tg2, g_post_ffn, tm, FFN_BLOCKS_PER_STEP)

    dpre, dx1, dmix, dg3, dg2 = _ffn_bwd(df, u, wdn, wup_g, x1, dy, mix, g_pre_ffn, g_post_mix, FFN_BWD_TOKENS,
                                         FFN_BLOCKS_PER_STEP)
    gw_dn = _wgrad_rows(u, df, N_DEV, tw, "wgrad_down", square=True)
    gw_up = _wgrad_cols(h2, dpre, N_DEV, tw, "wgrad_up")
    gw_o = _wgrad_o(oa, ob, dmix, N_DEV, min(2048, t))
    doa, dob = _attn_out_bwd(dmix, wo, oa.shape[1], tm)
    dqa, dka, dva, (p_o, p_up, p_dn) = _attn_a_bwd(qa, ka, kat, va, doa, oa, lse_a, tq, tk, [gw_o, gw_up, gw_dn])
    dqb, dkb, dvb, dsum, dsink = _attn_b_bwd(qb, kb, kbt, vb, dob, ob, lse_b, bias_t, sink_b, s_len)
    drel_g, dsink_g = _bias_reduce(dsum, dsink, bucket_t4)
    dproj, dgq, dgk = _dqkprep(dqa, dka, dva, dqb, dkb, dvb, proj, cos, sin_signed, gq2, gk2, s_len, ts)
    gw_in_t = _wgrad_rows(dproj, h1, p_cols // 256, tw, "wgrad_in").reshape(N_DEV, -1, d)
    grad_x, dg1, (p_in,) = _dx_final(dproj, w_in_t, x2, dx1, g_pre_mix, tm, [gw_in_t])

    vec, rel = _small_allreduce([dg1, dg2, dg3, dg4], dgq, dgk, dsink_g, drel_g, loss8)
    return grad_x.reshape(bl, s_len, d), p_in, p_o, p_up, p_dn, vec, rel


def kernel(x, w_in, w_o, g_pre_mix, g_post_mix, q_norm_a, k_norm_a, sink_b, rel_bias, g_pre_ffn, w_ffn_up, w_ffn_down, g_post_ffn, loss_target, m_w_in, m_w_o, m_g_pre_mix, m_g_post_mix, m_q_norm_a, m_k_norm_a, m_sink_b, m_rel_bias, m_g_pre_ffn, m_w_ffn_up, m_w_ffn_down, m_g_post_ffn, v_w_in, v_w_o, v_g_pre_mix, v_g_post_mix, v_q_norm_a, v_k_norm_a, v_sink_b, v_rel_bias, v_g_pre_ffn, v_w_ffn_up, v_w_ffn_down, v_g_post_ffn):
    w_in_t = w_in[0].T
    (win_g,) = _weight_gather([w_in_t.astype(BF16)])

    grad_x, p_in, p_o, p_up, p_dn, vec, rel = _local_step(
        x, loss_target, win_g, w_o[0].astype(BF16), w_ffn_up[0].astype(BF16), w_ffn_down[0].astype(BF16),
        g_pre_mix, g_post_mix, q_norm_a, k_norm_a, sink_b, rel_bias, g_pre_ffn, g_post_ffn)

    big = {
        "w_in": [a.T for a in _adamw_sum(p_in, w_in_t, m_w_in[0].T, v_w_in[0].T, 192, "adamw_in")],
        "w_o": _adamw_sum(p_o, w_o[0], m_w_o[0], v_w_o[0], 128, "adamw_o"),
        "w_up": _adamw_sum(p_up, w_ffn_up[0], m_w_ffn_up[0], v_w_ffn_up[0], 256, "adamw_up"),
        "w_dn": _adamw_sum(p_dn, w_ffn_down[0], m_w_ffn_down[0], v_w_ffn_down[0], 256, "adamw_down"),
    }
    loss, small = _adamw_small(
        vec, rel,
        [g_pre_mix, g_post_mix, g_pre_ffn, g_post_ffn, q_norm_a, k_norm_a, sink_b, rel_bias],
        [m_g_pre_mix, m_g_post_mix, m_g_pre_ffn, m_g_post_ffn, m_q_norm_a, m_k_norm_a, m_sink_b, m_rel_bias],
        [v_g_pre_mix, v_g_post_mix, v_g_pre_ffn, v_g_post_ffn, v_q_norm_a, v_k_norm_a, v_sink_b, v_rel_bias])
    s_pre_mix, s_post_mix, s_pre_ffn, s_post_ffn, s_qn, s_kn, s_sink, s_rel = small

    def outs(kind):
        return [big["w_in"][kind][None], big["w_o"][kind][None], s_pre_mix[kind], s_post_mix[kind], s_qn[kind],
                s_kn[kind], s_sink[kind], s_rel[kind], s_pre_ffn[kind], big["w_up"][kind][None],
                big["w_dn"][kind][None], s_post_ffn[kind]]

    return (loss.reshape(()), grad_x, *outs(0), *outs(1), *outs(2), *outs(3))
```

```python
import jax
import jax.numpy as jnp
import numpy as np
from jax import lax
from jax.experimental import pallas as pl
from jax.experimental.pallas import tpu as pltpu

F32 = jnp.float32
BF16 = jnp.bfloat16
SDS = jax.ShapeDtypeStruct

N_DEV = 8
HEAD_DIM = 64
GROUP = 4
BLOCK = 128
SPAN = 3 * BLOCK
GRID_W = 64
N_BUCKETS = 32
MAX_DISTANCE = 128
ROPE_THETA = 10000.0
EPS = 1e-6
NEG_INF = -1e30
SCALE = HEAD_DIM ** -0.5
VT_PAD = 16

ADAM_LR = 0.001
ADAM_B1 = 0.9
ADAM_B2 = 0.999
ADAM_EPS = 1e-08
ADAM_WD = 0.01
ADAM_STEP = 10

VMEM_LIMIT = 56 * 1024 * 1024
MESH = pl.DeviceIdType.MESH


def _cp(*sem):
    return pltpu.CompilerParams(dimension_semantics=sem, vmem_limit_bytes=VMEM_LIMIT)


def _dot(a, b):
    return jnp.dot(a, b, preferred_element_type=F32)


def _dot_nt(a, b):
    return lax.dot_general(a, b, (((1,), (1,)), ((), ())), preferred_element_type=F32)


def _dot_tn(a, b):
    return lax.dot_general(a, b, (((0,), (0,)), ((), ())), preferred_element_type=F32)


def _rms_fwd(x, g):
    r = lax.rsqrt(jnp.mean(x * x, axis=-1, keepdims=True) + EPS)
    n = x * r
    return n * g, n, r


def _rms_bwd(n, r, g, dy):
    gd = g * dy
    dx = r * (gd - n * jnp.mean(n * gd, axis=-1, keepdims=True))
    return dx, dy * n


def _rope_tables(s_len):
    rows = s_len // GRID_W
    row = np.repeat(np.arange(rows, dtype=np.int32), GRID_W)
    col = np.tile(np.arange(GRID_W, dtype=np.int32), rows)
    nf = HEAD_DIM // 4
    freqs = np.float32(ROPE_THETA) ** (-np.arange(nf, dtype=np.float32) / np.float32(nf))
    ang_r = row.astype(np.float32)[:, None] * freqs[None, :]
    ang_c = col.astype(np.float32)[:, None] * freqs[None, :]
    cr, sr, cc, sc = np.cos(ang_r), np.sin(ang_r), np.cos(ang_c), np.sin(ang_c)
    cos = np.concatenate([cr, cr, cc, cc] * 2, axis=-1).astype(np.float32)
    sin_signed = np.concatenate([-sr, sr, -sc, sc] * 2, axis=-1).astype(np.float32)
    return jnp.asarray(cos), jnp.asarray(sin_signed)


def _t5_bucket(rel):
    nb = N_BUCKETS // 2
    ret = (rel > 0).astype(jnp.int32) * nb
    n = jnp.abs(rel)
    max_exact = nb // 2
    nf = jnp.maximum(n, 1).astype(F32)
    large = max_exact + (jnp.log(nf / max_exact) / np.float32(np.log(MAX_DISTANCE / max_exact))
                         * (nb - max_exact)).astype(jnp.int32)
    large = jnp.minimum(large, nb - 1)
    return ret + jnp.where(n < max_exact, n, large)


def _mesh_pos():
    return lax.axis_index("x"), lax.axis_index("y"), lax.axis_index("c")


def _lin(p):
    return 4 * p[0] + 2 * p[1] + p[2]


def _weight_gather(shards):
    n = len(shards)

    def body(*refs):
        xs, outs = refs[:n], refs[n:2 * n]
        send_sems, recv_sems, local_sems = refs[2 * n:]
        x, y, c = _mesh_pos()
        me, sibling = (x, y, c), (x, y, 1 - c)
        chips = [(1 - x, y), (x, 1 - y), (1 - x, 1 - y)]

        def copy(a, k, block, to, src=None):
            slot = outs[a].at[_lin(block)]
            return pltpu.make_async_remote_copy(
                src_ref=slot if src is None else src, dst_ref=slot,
                send_sem=send_sems.at[a, k], recv_sem=recv_sems.at[a, k],
                device_id=to, device_id_type=MESH)

        started = []
        for a in range(n):
            mine = pltpu.make_async_copy(xs[a], outs[a].at[_lin(me)], local_sems.at[a])
            mine.start()
            started.append(mine)
        sends = []
        for a in range(n):
            first = [copy(a, 0, me, sibling, src=xs[a])]
            first += [copy(a, 1 + j, me, (*chip, c), src=xs[a]) for j, chip in enumerate(chips)]
            for cp in first:
                cp.start()
            sends += first
        for a in range(n):
            for j, chip in enumerate(chips):
                copy(a, 1 + j, (*chip, c), me).wait_recv()
                fwd = copy(a, 4 + j, (*chip, c), sibling)
                fwd.start()
                sends.append(fwd)
        for a in range(n):
            copy(a, 0, sibling, me).wait_recv()
            for j, chip in enumerate(chips):
                copy(a, 4 + j, (*chip, 1 - c), me).wait_recv()
        for cp in sends:
            cp.wait_send()
        for mine in started:
            mine.wait()

    anyspec = pl.BlockSpec(memory_space=pl.ANY)
    return pl.pallas_call(
        body,
        out_shape=[SDS((N_DEV,) + s.shape, s.dtype) for s in shards],
        in_specs=[anyspec] * n,
        out_specs=[anyspec] * n,
        scratch_shapes=[pltpu.SemaphoreType.DMA((n, 7)), pltpu.SemaphoreType.DMA((n, 7)),
                        pltpu.SemaphoreType.DMA((n,))],
        name="weight_gather",
    )(*shards)


def _direct_exchange(kind, ins, outs, send_sems, recv_sems, local_sems):
    x, y, c = _mesh_pos()
    me = (x, y, c)
    peers = [(x, y, 1 - c), (1 - x, y, c), (x, 1 - y, c), (1 - x, 1 - y, c),
             (1 - x, y, 1 - c), (x, 1 - y, 1 - c), (1 - x, 1 - y, 1 - c)]

    def src(a, to):
        return ins[a] if kind == "gather" else ins[a].at[_lin(to)]

    def remote(a, k, to, frm):
        return pltpu.make_async_remote_copy(
            src_ref=src(a, to), dst_ref=outs[a].at[_lin(frm)],
            send_sem=send_sems.at[a, k], recv_sem=recv_sems.at[a, k],
            device_id=to, device_id_type=MESH)

    n = len(ins)
    sends = [remote(a, k, p, me) for a in range(n) for k, p in enumerate(peers)]
    arrivals = [remote(a, k, p, p) for a in range(n) for k, p in enumerate(peers)]
    local = [pltpu.make_async_copy(src(a, me), outs[a].at[_lin(me)], local_sems.at[a]) for a in range(n)]

    def start():
        for cp in local + sends:
            cp.start()

    def wait():
        for cp in arrivals:
            cp.wait_recv()
        for cp in sends:
            cp.wait_send()
        for cp in local:
            cp.wait()

    return start, wait


def _exchange_scratch(n):
    return [pltpu.SemaphoreType.DMA((n, 7)), pltpu.SemaphoreType.DMA((n, 7)), pltpu.SemaphoreType.DMA((n,))]


SMALL_LANES = 128


def _small_allreduce(dg_rows, dgq, dgk, dsink_g, drel_g, loss8):
    d = dg_rows[0].shape[1]
    kv = dsink_g.shape[0]

    def body(g1_ref, g2_ref, g3_ref, g4_ref, gq_ref, gk_ref, sk_ref, rl_ref, ls_ref, vec_ref, rel_ref,
             vbuf, rbuf, vland, rland, send_sems, recv_sems):
        x, y, c = _mesh_pos()
        me = (x, y, c)
        peers = [(x, y, 1 - c), (1 - x, y, c), (x, 1 - y, c), (1 - x, 1 - y, c),
                 (1 - x, y, 1 - c), (x, 1 - y, 1 - c), (1 - x, 1 - y, 1 - c)]
        vbuf[...] = jnp.zeros_like(vbuf)
        rbuf[...] = jnp.zeros_like(rbuf)
        for row, ref in enumerate((g1_ref, g2_ref, g3_ref, g4_ref)):
            vbuf[row:row + 1, :] = ref[0:1, :]
        vbuf[4:5, 0:HEAD_DIM] = gq_ref[0:1, 0:HEAD_DIM] + gq_ref[0:1, HEAD_DIM:PAIR]
        vbuf[4:5, SMALL_LANES:SMALL_LANES + HEAD_DIM] = gk_ref[0:1, 0:HEAD_DIM] + gk_ref[0:1, HEAD_DIM:PAIR]
        for g in range(kv):
            vbuf[4:5, 2 * SMALL_LANES + g * GROUP:2 * SMALL_LANES + (g + 1) * GROUP] = sk_ref[g, 0:1, 0:GROUP]
            rbuf[:, g * GROUP:(g + 1) * GROUP] = rl_ref[g, :, 0:GROUP]
        vbuf[4:5, 3 * SMALL_LANES:3 * SMALL_LANES + 1] = ls_ref[0:1, 0:1]

        def copies(k, to, frm):
            return [pltpu.make_async_remote_copy(
                src_ref=buf, dst_ref=land.at[_lin(frm)], send_sem=send_sems.at[a, k], recv_sem=recv_sems.at[a, k],
                device_id=to, device_id_type=MESH) for a, (buf, land) in enumerate(((vbuf, vland), (rbuf, rland)))]

        sends = [cp for k, p in enumerate(peers) for cp in copies(k, p, me)]
        for cp in sends:
            cp.start()
        vland[_lin(me)] = vbuf[...]
        rland[_lin(me)] = rbuf[...]
        for k, p in enumerate(peers):
            for cp in copies(k, p, p):
                cp.wait_recv()
        for cp in sends:
            cp.wait_send()
        vacc, racc = vland[0], rland[0]
        for s in range(1, N_DEV):
            vacc, racc = vacc + vland[s], racc + rland[s]
        vec_ref[...] = vacc
        rel_ref[...] = racc

    vm = pl.BlockSpec(memory_space=pltpu.VMEM)
    return pl.pallas_call(
        body,
        out_shape=[SDS((8, d), F32), SDS((N_BUCKETS, 128), F32)],
        in_specs=[vm] * 9,
        out_specs=[vm, vm],
        scratch_shapes=[pltpu.VMEM((8, d), F32), pltpu.VMEM((N_BUCKETS, 128), F32),
                        pltpu.VMEM((N_DEV, 8, d), F32), pltpu.VMEM((N_DEV, N_BUCKETS, 128), F32),
                        pltpu.SemaphoreType.DMA((2, 7)), pltpu.SemaphoreType.DMA((2, 7))],
        name="small_allreduce",
    )(*dg_rows, dgq, dgk, dsink_g, drel_g, loss8)


def _inproj(x2, g1, w_t, tm):
    t, d = x2.shape
    p = w_t.shape[0]

    def body(x_ref, g_ref, w_ref, h_ref, p_ref):
        y, _, _ = _rms_fwd(x_ref[...], g_ref[...])
        h = y.astype(BF16)
        h_ref[...] = h
        p_ref[...] = _dot_nt(h, w_ref[...])

    return pl.pallas_call(
        body,
        grid=(t // tm,),
        in_specs=[pl.BlockSpec((tm, d), lambda i: (i, 0)),
                  pl.BlockSpec((1, d), lambda i: (0, 0)),
                  pl.BlockSpec((p, d), lambda i: (0, 0))],
        out_specs=[pl.BlockSpec((tm, d), lambda i: (i, 0)),
                   pl.BlockSpec((tm, p), lambda i: (i, 0))],
        out_shape=[SDS((t, d), BF16), SDS((t, p), F32)],
        compiler_params=_cp("parallel"),
        name="inproj",
    )(x2, g1, w_t)


PAIR = 2 * HEAD_DIM


def _pair_masks(ts):
    lane = lax.broadcasted_iota(jnp.int32, (ts, PAIR), 1)
    return lane < HEAD_DIM, (lane % 32) < 16


def _pair_mean(v, low):
    del low
    r = lax.broadcasted_iota(jnp.int32, (PAIR, PAIR), 0) // HEAD_DIM
    c = lax.broadcasted_iota(jnp.int32, (PAIR, PAIR), 1) // HEAD_DIM
    same_head = (r == c).astype(BF16)
    hi = v.astype(BF16)
    lo = (v - hi.astype(F32)).astype(BF16)
    return (_dot(hi, same_head) + _dot(lo, same_head)) * (1.0 / HEAD_DIM)


def _pair_partner(v, first):
    return jnp.where(first, pltpu.roll(v, PAIR - 16, 1), pltpu.roll(v, 16, 1))


def _qkprep(proj, cos, sin_signed, gq, gk, bl, s_len, ha, kva, hb, kvb, ts):
    t, p_cols = proj.shape
    assert ha % 2 == 0 and kva % 2 == 0 and hb % 2 == 0 and kvb % 2 == 0
    ns = s_len // ts
    sp = s_len + 2 * BLOCK

    def body(p_ref, cos_ref, sin_ref, gq_ref, gk_ref, qa_ref, ka_ref, kat_ref, va_ref, vat_ref, qb_ref, kb_ref,
             kbt_ref, vb_ref, vbt_ref):
        i = pl.program_id(1)
        cs, sn = cos_ref[...], sin_ref[...]
        low, first = _pair_masks(ts)
        ones_row = (lax.broadcasted_iota(jnp.int32, (VT_PAD, ts), 0) == 0).astype(BF16)
        heads = (slice(0, HEAD_DIM), slice(HEAD_DIM, PAIR))

        def pair(p):
            return p_ref[:, p * PAIR:(p + 1) * PAIR]

        def normrope(x, g):
            y = x * lax.rsqrt(_pair_mean(x * x, low) + EPS) * g
            return y * cs + _pair_partner(y, first) * sn

        eye = (lax.broadcasted_iota(jnp.int32, (PAIR, PAIR), 0)
               == lax.broadcasted_iota(jnp.int32, (PAIR, PAIR), 1)).astype(BF16)

        def transposed(xb):
            return _dot_nt(eye, xb).astype(BF16)

        for p in range(ha // 2):
            q = (normrope(pair(p), gq_ref[...]) * SCALE).astype(BF16)
            for e, lanes in enumerate(heads):
                qa_ref[0, 2 * p + e] = q[:, lanes]
        base = ha // 2
        for p in range(kva // 2):
            k = normrope(pair(base + p), gk_ref[...]).astype(BF16)
            v = pair(base + kva // 2 + p).astype(BF16)
            kt, vt = transposed(k), transposed(v)
            for e, lanes in enumerate(heads):
                ka_ref[0, 2 * p + e] = k[:, lanes]
                va_ref[0, 2 * p + e] = v[:, lanes]
                kat_ref[0, 2 * p + e] = kt[lanes, :]
                vat_ref[0, 2 * p + e, 0:HEAD_DIM, :] = vt[lanes, :]
                vat_ref[0, 2 * p + e, HEAD_DIM:HEAD_DIM + VT_PAD, :] = ones_row
        base += kva
        for p in range(hb // 2):
            q = (pair(base + p) * SCALE).astype(BF16)
            for e, lanes in enumerate(heads):
                qb_ref[0, 2 * p + e] = q[:, lanes]
        base += hb // 2

        @pl.when(i == 0)
        def _():
            zeros = jnp.zeros((kvb, BLOCK, HEAD_DIM), BF16)
            zeros_t = jnp.zeros((kvb, HEAD_DIM + VT_PAD, BLOCK), BF16)
            for ref in (kb_ref, vb_ref):
                ref[0, :, 0:BLOCK, :] = zeros
                ref[0, :, sp - BLOCK:sp, :] = zeros
            kbt_ref[0, :, :, 0:BLOCK] = zeros_t[:, 0:HEAD_DIM]
            kbt_ref[0, :, :, sp - BLOCK:sp] = zeros_t[:, 0:HEAD_DIM]
            vbt_ref[0, :, :, 0:BLOCK] = zeros_t
            vbt_ref[0, :, :, sp - BLOCK:sp] = zeros_t

        rows = pl.ds(pl.multiple_of(BLOCK + i * ts, BLOCK), ts)
        for p in range(kvb // 2):
            k = pair(base + p).astype(BF16)
            v = pair(base + kvb // 2 + p).astype(BF16)
            kt, vt = transposed(k), transposed(v)
            for e, lanes in enumerate(heads):
                kb_ref[0, 2 * p + e, rows, :] = k[:, lanes]
                vb_ref[0, 2 * p + e, rows, :] = v[:, lanes]
                kbt_ref[0, 2 * p + e, :, rows] = kt[lanes, :]
                vbt_ref[0, 2 * p + e, 0:HEAD_DIM, rows] = vt[lanes, :]
                vbt_ref[0, 2 * p + e, HEAD_DIM:HEAD_DIM + VT_PAD, rows] = ones_row

    def hm(nh):
        return pl.BlockSpec((1, nh, ts, HEAD_DIM), lambda b, i: (b, 0, i, 0))

    def padded(nh):
        return pl.BlockSpec((1, nh, sp, HEAD_DIM), lambda b, i: (b, 0, 0, 0))

    def padded_t(nh, rows):
        return pl.BlockSpec((1, nh, rows, sp), lambda b, i: (b, 0, 0, 0))

    return pl.pallas_call(
        body,
        grid=(bl, ns),
        in_specs=[pl.BlockSpec((ts, p_cols), lambda b, i: (b * ns + i, 0)),
                  pl.BlockSpec((ts, PAIR), lambda b, i: (i, 0)),
                  pl.BlockSpec((ts, PAIR), lambda b, i: (i, 0)),
                  pl.BlockSpec((1, PAIR), lambda b, i: (0, 0)),
                  pl.BlockSpec((1, PAIR), lambda b, i: (0, 0))],
        out_specs=[hm(ha), hm(kva), pl.BlockSpec((1, kva, HEAD_DIM, ts), lambda b, i: (b, 0, 0, i)), hm(kva),
                   pl.BlockSpec((1, kva, HEAD_DIM + VT_PAD, ts), lambda b, i: (b, 0, 0, i)),
                   hm(hb), padded(kvb), padded_t(kvb, HEAD_DIM), padded(kvb), padded_t(kvb, HEAD_DIM + VT_PAD)],
        out_shape=[SDS((bl, ha, s_len, HEAD_DIM), BF16), SDS((bl, kva, s_len, HEAD_DIM), BF16),
                   SDS((bl, kva, HEAD_DIM, s_len), BF16),
                   SDS((bl, kva, s_len, HEAD_DIM), BF16), SDS((bl, kva, HEAD_DIM + VT_PAD, s_len), BF16),
                   SDS((bl, hb, s_len, HEAD_DIM), BF16),
                   SDS((bl, kvb, sp, HEAD_DIM), BF16), SDS((bl, kvb, HEAD_DIM, sp), BF16),
                   SDS((bl, kvb, sp, HEAD_DIM), BF16), SDS((bl, kvb, HEAD_DIM + VT_PAD, sp), BF16)],
        compiler_params=_cp("parallel", "arbitrary"),
        name="qkprep",
    )(proj, cos, sin_signed, gq, gk)


def _bias_build(bucket_t, rel_bias, hb):
    kvb = hb // GROUP

    def body(bkt_ref, tbl_ref, out_ref):
        bkt = bkt_ref[...]
        ci = lax.broadcasted_iota(jnp.int32, (SPAN, BLOCK), 0)
        qi = lax.broadcasted_iota(jnp.int32, (SPAN, BLOCK), 1)
        band = jnp.abs(ci - BLOCK - qi) <= BLOCK
        masks = (band, band & (ci >= BLOCK), band & (ci < 2 * BLOCK))
        for h in range(hb):
            acct = jnp.zeros((SPAN, BLOCK), F32)
            for b in range(N_BUCKETS):
                acct = jnp.where(bkt == b, tbl_ref[b, h], acct)
            lanes = slice((h % GROUP) * BLOCK, (h % GROUP + 1) * BLOCK)
            for var, mask in enumerate(masks):
                out_ref[var, h // GROUP, :, lanes] = jnp.where(mask, acct, NEG_INF)

    vm = pl.BlockSpec(memory_space=pltpu.VMEM)
    return pl.pallas_call(
        body,
        in_specs=[vm, pl.BlockSpec(memory_space=pltpu.SMEM)],
        out_specs=vm,
        out_shape=SDS((3, kvb, SPAN, GROUP * BLOCK), F32),
        name="bias_build",
    )(bucket_t, rel_bias)


def _attn_a_fwd(qa, ka, vat, tq, tk, shards):
    bl, ha, s_len, _ = qa.shape
    kv = ka.shape[1]
    va_rows = vat.shape[2]
    nq, nk = s_len // tq, s_len // tk
    assert nk % 2 == 0
    r = GROUP * tq
    ns = len(shards)

    def body(q_ref, qn_ref, k_ref, v_ref, *rest):
        shard_refs, (o_ref, l_ref), gathered = rest[:ns], rest[ns:ns + 2], rest[ns + 2:2 * ns + 2]
        st_sc, send_sems, recv_sems, local_sems = rest[2 * ns + 2:]
        i = pl.program_id(2)
        step_id = (pl.program_id(0) * kv + pl.program_id(1)) * nq + i
        start, wait = _direct_exchange("gather", shard_refs, gathered, send_sems, recv_sems, local_sems)
        pl.when(step_id == 0)(start)

        q = q_ref[0].reshape(r, HEAD_DIM)

        def scores(c, qv):
            return _dot_nt(k_ref[0, 0, pl.ds(pl.multiple_of(c * tk, tk), tk), :], qv)

        def fold(st, c, carry):
            m_old, acc = carry
            m_new = jnp.maximum(m_old, jnp.max(st, axis=0, keepdims=True))
            pt = jnp.exp(st - m_new).astype(BF16)
            vt = v_ref[0, 0, :, pl.ds(pl.multiple_of(c * tk, tk), tk)]
            return m_new, jnp.exp(m_old - m_new) * acc + _dot(vt, pt)

        @pl.when(i == 0)
        def _():
            st_sc[0] = scores(0, q)

        def step(c2, carry):
            c = 2 * c2
            st_sc[1] = scores(c + 1, q)
            carry = fold(st_sc[0], c, carry)
            st_sc[0] = scores(c + 2, q)
            return fold(st_sc[1], c + 1, carry)

        carry = (jnp.full((1, r), -jnp.inf, F32), jnp.zeros((va_rows, r), F32))
        for c2 in range(nk // 2 - 1):
            carry = step(c2, carry)
        st_sc[1] = scores(nk - 1, q)
        carry = fold(st_sc[0], nk - 2, carry)
        st_sc[0] = scores(0, qn_ref[0].reshape(r, HEAD_DIM))
        m, acc = fold(st_sc[1], nk - 1, carry)
        l = acc[HEAD_DIM:HEAD_DIM + 1, :]
        o = (acc[0:HEAD_DIM, :] / l).T
        for h in range(GROUP):
            o_ref[:, h * HEAD_DIM:(h + 1) * HEAD_DIM] = o[h * tq:(h + 1) * tq].astype(BF16)
        l_ref[0, 0, 0] = jnp.broadcast_to(m + jnp.log(l), (8, r))
        pl.when(step_id == bl * kv * nq - 1)(wait)

    anyspec = pl.BlockSpec(memory_space=pl.ANY)
    res = pl.pallas_call(
        body,
        grid=(bl, kv, nq),
        in_specs=[pl.BlockSpec((1, GROUP, tq, HEAD_DIM), lambda b, g, i: (b, g, i, 0)),
                  pl.BlockSpec((1, GROUP, tq, HEAD_DIM), lambda b, g, i: (b, g, jnp.minimum(i + 1, nq - 1), 0)),
                  pl.BlockSpec((1, 1, s_len, HEAD_DIM), lambda b, g, i: (b, g, 0, 0)),
                  pl.BlockSpec((1, 1, va_rows, s_len), lambda b, g, i: (b, g, 0, 0))] + [anyspec] * ns,
        out_specs=[pl.BlockSpec((tq, GROUP * HEAD_DIM), lambda b, g, i: (b * nq + i, g)),
                   pl.BlockSpec((1, 1, 1, 8, r), lambda b, g, i: (b, g, i, 0, 0))] + [anyspec] * ns,
        out_shape=[SDS((bl * s_len, ha * HEAD_DIM), BF16), SDS((bl, kv, nq, 8, r), F32)]
        + [SDS((N_DEV,) + s.shape, s.dtype) for s in shards],
        scratch_shapes=[pltpu.VMEM((2, tk, r), F32)] + _exchange_scratch(ns),
        compiler_params=_cp("arbitrary", "arbitrary", "arbitrary"),
        name="attn_a_fwd",
    )(qa, qa, ka, vat, *shards)
    return res[0], res[1], res[2:]


FFN_BLOCKS_PER_STEP = 8
FFN_BWD_TOKENS = 256
QB_PER_STEP = 16


def _bias_variant(n, nb):
    return jnp.where(n == 0, 1, jnp.where(n == nb - 1, 2, 0))


def _sink_row(sink_ref, g):
    return jnp.concatenate([jnp.full((1, BLOCK), sink_ref[0, g * GROUP + h], F32) for h in range(GROUP)], axis=1)


def _attn_b_fwd(qb, kb, vbt, bias_t, sink, s_len):
    bl, hb, _, _ = qb.shape
    kv = kb.shape[1]
    sp = kb.shape[2]
    vt_rows = vbt.shape[2]
    nb = s_len // BLOCK
    nbs = min(QB_PER_STEP, nb)
    r = GROUP * BLOCK

    def body(q_ref, k_ref, vt_ref, bt_ref, sink_ref, o_ref, l_ref, st_sc, pb_sc):
        g, n0 = pl.program_id(1), pl.program_id(2) * nbs
        sink_row = _sink_row(sink_ref, g)

        def span(j):
            return pl.ds(pl.multiple_of((n0 + j) * BLOCK, BLOCK), SPAN)

        for j in range(nbs):
            q = q_ref[0, :, j * BLOCK:(j + 1) * BLOCK, :].reshape(r, HEAD_DIM)
            st_sc[j] = _dot_nt(k_ref[0, 0, span(j), :], q) + bt_ref[_bias_variant(n0 + j, nb), 0]
        maxes = []
        for j in range(nbs):
            st = st_sc[j]
            m = jnp.maximum(jnp.max(st, axis=0, keepdims=True), sink_row)
            pb_sc[j] = jnp.exp(st - m).astype(BF16)
            maxes.append(m)
        for j in range(nbs):
            m = maxes[j]
            acc = _dot(vt_ref[0, 0, :, span(j)], pb_sc[j])
            l = acc[HEAD_DIM:HEAD_DIM + 1, :] + jnp.exp(sink_row - m)
            o = (acc[0:HEAD_DIM, :] / l).T
            for h in range(GROUP):
                o_ref[j * BLOCK:(j + 1) * BLOCK, h * HEAD_DIM:(h + 1) * HEAD_DIM] = (
                    o[h * BLOCK:(h + 1) * BLOCK].astype(BF16))
            l_ref[0, 0, j] = jnp.broadcast_to(m + jnp.log(l), (8, r))

    return pl.pallas_call(
        body,
        grid=(bl, kv, nb // nbs),
        in_specs=[pl.BlockSpec((1, GROUP, nbs * BLOCK, HEAD_DIM), lambda b, g, n: (b, g, n, 0)),
                  pl.BlockSpec((1, 1, sp, HEAD_DIM), lambda b, g, n: (b, g, 0, 0)),
                  pl.BlockSpec((1, 1, vt_rows, sp), lambda b, g, n: (b, g, 0, 0)),
                  pl.BlockSpec((3, 1, SPAN, r), lambda b, g, n: (0, g, 0, 0)),
                  pl.BlockSpec(memory_space=pltpu.SMEM)],
        out_specs=[pl.BlockSpec((nbs * BLOCK, GROUP * HEAD_DIM), lambda b, g, n: (b * (nb // nbs) + n, g)),
                   pl.BlockSpec((1, 1, nbs, 8, r), lambda b, g, n: (b, g, n, 0, 0))],
        out_shape=[SDS((bl * s_len, hb * HEAD_DIM), BF16), SDS((bl, kv, nb, 8, r), F32)],
        scratch_shapes=[pltpu.VMEM((nbs, SPAN, r), F32), pltpu.VMEM((nbs, SPAN, r), BF16)],
        compiler_params=_cp("parallel", "parallel", "arbitrary"),
        name="attn_b_fwd",
    )(qb, kb, vbt, bias_t, sink)


def _mixout(oa, ob, wo, x2, g2, g3, tm):
    t, d = x2.shape
    ca = oa.shape[1]

    def body(oa_ref, ob_ref, w_ref, x_ref, g2_ref, g3_ref, mix_ref, x1_ref, h2_ref):
        mix = _dot(oa_ref[...], w_ref[0:ca, :]) + _dot(ob_ref[...], w_ref[ca:, :])
        mix_ref[...] = mix
        y2, _, _ = _rms_fwd(mix, g2_ref[...])
        x1 = x_ref[...] + y2
        x1_ref[...] = x1
        y3, _, _ = _rms_fwd(x1, g3_ref[...])
        h2_ref[...] = y3.astype(BF16)

    tile = lambda w: pl.BlockSpec((tm, w), lambda i: (i, 0))
    vec = pl.BlockSpec((1, d), lambda i: (0, 0))
    return pl.pallas_call(
        body,
        grid=(t // tm,),
        in_specs=[tile(ca), tile(ob.shape[1]), pl.BlockSpec(wo.shape, lambda i: (0, 0)), tile(d), vec, vec],
        out_specs=[tile(d), tile(d), tile(d)],
        out_shape=[SDS((t, d), F32), SDS((t, d), F32), SDS((t, d), BF16)],
        compiler_params=_cp("parallel"),
        name="mixout",
    )(oa, ob, wo, x2, g2, g3)


def _ffn_fwd(h2, wup_g, wdn, x1, target, g4, tm, jb):
    t, d = x1.shape
    nblk, _, tf = wup_g.shape
    ff = nblk * tf
    nt = t // tm
    nj = nblk // jb

    def body(h_ref, wu_ref, wd_ref, x1_ref, tg_ref, g_ref, u_ref, df_ref, dy_ref, dg_ref, loss_ref, acc_sc):
        i, j = pl.program_id(0), pl.program_id(1)

        @pl.when(j == 0)
        def _():
            acc_sc[...] = jnp.zeros_like(acc_sc)

        @pl.when((i == 0) & (j == 0))
        def _():
            dg_ref[...] = jnp.zeros_like(dg_ref)
            loss_ref[...] = jnp.zeros_like(loss_ref)

        h = h_ref[...]
        squares = []
        for s in range(jb):
            u = jnp.maximum(_dot(h, wu_ref[s]), 0.0)
            u_ref[:, s * tf:(s + 1) * tf] = u.astype(BF16)
            squares.append((u * u).astype(BF16))
        acc_sc[...] += _dot(jnp.concatenate(squares, axis=1), wd_ref[...])

        @pl.when(j == nj - 1)
        def _():
            g = g_ref[...]
            y4, n, r = _rms_fwd(acc_sc[...], g)
            e = (x1_ref[...] + y4) - tg_ref[...]
            loss_ref[...] += jnp.sum(e * e) * (0.5 / d)
            dy = e * (1.0 / d)
            dy_ref[...] = dy
            df, dgt = _rms_bwd(n, r, g, dy)
            df_ref[...] = df.astype(BF16)
            dg_ref[0:1, :] += jnp.sum(dgt, axis=0, keepdims=True)

    tile = pl.BlockSpec((tm, d), lambda i, j: (i, 0))
    return pl.pallas_call(
        body,
        grid=(nt, nj),
        in_specs=[tile,
                  pl.BlockSpec((jb, d, tf), lambda i, j: (j, 0, 0)),
                  pl.BlockSpec((jb * tf, d), lambda i, j: (j, 0)),
                  tile, tile,
                  pl.BlockSpec((1, d), lambda i, j: (0, 0))],
        out_specs=[pl.BlockSpec((tm, jb * tf), lambda i, j: (i, j)), tile, tile,
                   pl.BlockSpec((8, d), lambda i, j: (0, 0)),
                   pl.BlockSpec((8, 128), lambda i, j: (0, 0))],
        out_shape=[SDS((t, ff), BF16), SDS((t, d), BF16), SDS((t, d), F32), SDS((8, d), F32), SDS((8, 128), F32)],
        scratch_shapes=[pltpu.VMEM((tm, d), F32)],
        compiler_params=_cp("arbitrary", "arbitrary"),
        name="ffn_fwd",
    )(h2, wup_g, wdn, x1, target, g4)


def _ffn_bwd(df, u, wdn, wup_g, x1, dy, mix, g3, g2, tm, jb):
    t, d = x1.shape
    nblk, _, tf = wup_g.shape
    nt = t // tm
    nj = nblk // jb

    def body(df_ref, u_ref, wd_ref, wu_ref, x1_ref, dy_ref, mix_ref, g3_ref, g2_ref,
             dpre_ref, dx1_ref, dmix_ref, dg3_ref, dg2_ref, acc_sc):
        i, j = pl.program_id(0), pl.program_id(1)

        @pl.when(j == 0)
        def _():
            acc_sc[...] = jnp.zeros_like(acc_sc)

        @pl.when((i == 0) & (j == 0))
        def _():
            dg3_ref[...] = jnp.zeros_like(dg3_ref)
            dg2_ref[...] = jnp.zeros_like(dg2_ref)

        du2 = _dot_nt(df_ref[...], wd_ref[...])
        dpre = (2.0 * u_ref[...].astype(F32) * du2).astype(BF16)
        dpre_ref[...] = dpre
        dh = _dot_nt(dpre[:, 0:tf], wu_ref[0])
        for s in range(1, jb):
            dh = dh + _dot_nt(dpre[:, s * tf:(s + 1) * tf], wu_ref[s])
        acc_sc[...] += dh

        @pl.when(j == nj - 1)
        def _():
            g3, g2 = g3_ref[...], g2_ref[...]
            _, n3, r3 = _rms_fwd(x1_ref[...], g3)
            dx, dgt3 = _rms_bwd(n3, r3, g3, acc_sc[...])
            dx1 = dy_ref[...] + dx
            dx1_ref[...] = dx1
            dg3_ref[0:1, :] += jnp.sum(dgt3, axis=0, keepdims=True)
            _, n2, r2 = _rms_fwd(mix_ref[...], g2)
            dmix, dgt2 = _rms_bwd(n2, r2, g2, dx1)
            dmix_ref[...] = dmix.astype(BF16)
            dg2_ref[0:1, :] += jnp.sum(dgt2, axis=0, keepdims=True)

    tile = pl.BlockSpec((tm, d), lambda i, j: (i, 0))
    vec = pl.BlockSpec((1, d), lambda i, j: (0, 0))
    acc8 = pl.BlockSpec((8, d), lambda i, j: (0, 0))
    return pl.pallas_call(
        body,
        grid=(nt, nj),
        in_specs=[tile,
                  pl.BlockSpec((tm, jb * tf), lambda i, j: (i, j)),
                  pl.BlockSpec((jb * tf, d), lambda i, j: (j, 0)),
                  pl.BlockSpec((jb, d, tf), lambda i, j: (j, 0, 0)),
                  tile, tile, tile, vec, vec],
        out_specs=[pl.BlockSpec((tm, jb * tf), lambda i, j: (i, j)), tile, tile, acc8, acc8],
        out_shape=[SDS(u.shape, BF16), SDS((t, d), F32), SDS((t, d), BF16), SDS((8, d), F32), SDS((8, d), F32)],
        scratch_shapes=[pltpu.VMEM((tm, d), F32)],
        compiler_params=_cp("arbitrary", "arbitrary"),
        name="ffn_bwd",
    )(df, u, wdn, wup_g, x1, dy, mix, g3, g2)


def _wgrad(a, b, a_spec, b_spec, out_block, out_shape, nj, nk, name, prep_a=None, prep_b=None):
    acc_shape = out_block[1:]

    def body(a_ref, b_ref, o_ref, acc_sc):
        k = pl.program_id(1)
        av = a_ref[...] if prep_a is None else prep_a(a_ref)
        bv = b_ref[...] if prep_b is None else prep_b(b_ref)
        part = _dot_tn(av, bv)

        @pl.when(k == 0)
        def _():
            acc_sc[...] = part

        @pl.when(k > 0)
        def _():
            acc_sc[...] += part

        @pl.when(k == nk - 1)
        def _():
            o_ref[0] = acc_sc[...].astype(BF16)

    return pl.pallas_call(
        body,
        grid=(nj, nk),
        in_specs=[a_spec, b_spec],
        out_specs=pl.BlockSpec(out_block, lambda j, k: (j, 0, 0)),
        out_shape=SDS(out_shape, BF16),
        scratch_shapes=[pltpu.VMEM(acc_shape, F32)],
        compiler_params=_cp("parallel", "arbitrary"),
        name=name,
    )(a, b)


def _wgrad_cols(a, b, nj, tt, name):
    t, m = a.shape
    bn = b.shape[1] // nj
    return _wgrad(a, b, pl.BlockSpec((tt, m), lambda j, k: (k, 0)), pl.BlockSpec((tt, bn), lambda j, k: (k, j)),
                  (1, m, bn), (nj, m, bn), nj, t // tt, name)


def _wgrad_rows(a, b, nj, tt, name, square=False):
    t, n = b.shape
    bm = a.shape[1] // nj

    def squared(a_ref):
        af = a_ref[...].astype(F32)
        return (af * af).astype(BF16)

    return _wgrad(a, b, pl.BlockSpec((tt, bm), lambda j, k: (k, j)), pl.BlockSpec((tt, n), lambda j, k: (k, 0)),
                  (1, bm, n), (nj, bm, n), nj, t // tt, name, prep_a=squared if square else None)


def _wgrad_o(oa, ob, dmix, nj, tt):
    t, n = dmix.shape
    ca, cb = oa.shape[1], ob.shape[1]
    m = ca + cb
    nk = t // tt

    def body(oa_ref, ob_ref, b_ref, o_ref, acc_sc):
        k = pl.program_id(0)
        part = _dot_tn(jnp.concatenate([oa_ref[...], ob_ref[...]], axis=1), b_ref[...])

        @pl.when(k == 0)
        def _():
            acc_sc[...] = part

        @pl.when(k > 0)
        def _():
            acc_sc[...] += part

        @pl.when(k == nk - 1)
        def _():
            o_ref[...] = acc_sc[...].reshape(nj, m // nj, n).astype(BF16)

    return pl.pallas_call(
        body,
        grid=(nk,),
        in_specs=[pl.BlockSpec((tt, ca), lambda k: (k, 0)), pl.BlockSpec((tt, cb), lambda k: (k, 0)),
                  pl.BlockSpec((tt, n), lambda k: (k, 0))],
        out_specs=pl.BlockSpec((nj, m // nj, n), lambda k: (0, 0, 0)),
        out_shape=SDS((nj, m // nj, n), BF16),
        scratch_shapes=[pltpu.VMEM((m, n), F32)],
        compiler_params=_cp("arbitrary"),
        name="wgrad_o",
    )(oa, ob, dmix)


def _attn_out_bwd(dmix, wo, ca, tm):
    t, d = dmix.shape
    cb = wo.shape[0] - ca

    def body(dm_ref, w_ref, da_ref, db_ref):
        dm = dm_ref[...]
        da_ref[...] = _dot_nt(dm, w_ref[0:ca, :]).astype(BF16)
        db_ref[...] = _dot_nt(dm, w_ref[ca:, :]).astype(BF16)

    return pl.pallas_call(
        body,
        grid=(t // tm,),
        in_specs=[pl.BlockSpec((tm, d), lambda i: (i, 0)), pl.BlockSpec(wo.shape, lambda i: (0, 0))],
        out_specs=[pl.BlockSpec((tm, ca), lambda i: (i, 0)), pl.BlockSpec((tm, cb), lambda i: (i, 0))],
        out_shape=[SDS((t, ca), BF16), SDS((t, cb), BF16)],
        compiler_params=_cp("parallel"),
        name="attn_out_bwd",
    )(dmix, wo)


def _stack_heads(ref):
    return jnp.concatenate([ref[:, h * HEAD_DIM:(h + 1) * HEAD_DIM] for h in range(GROUP)], axis=0)


def _attn_a_bwd(qa, ka, kat, va, do, o, lse, tq, tk, grads):
    bl, ha, s_len, _ = qa.shape
    kv = ka.shape[1]
    nq, nk = s_len // tq, s_len // tk
    assert nk % 2 == 0
    r = GROUP * tq
    ng = len(grads)

    def body(q_ref, qn_ref, k_ref, kt_ref, v_ref, do_ref, don_ref, o_ref, l_ref, *rest):
        grad_refs, (dq_ref, dk_ref, dv_ref), parts = rest[:ng], rest[ng:ng + 3], rest[ng + 3:2 * ng + 3]
        st_sc, dp_sc, dkt_sc, dvt_sc, send_sems, recv_sems, local_sems = rest[2 * ng + 3:]
        i = pl.program_id(2)
        step_id = (pl.program_id(0) * kv + pl.program_id(1)) * nq + i
        start, wait = _direct_exchange("scatter", grad_refs, parts, send_sems, recv_sems, local_sems)
        pl.when(step_id == 0)(start)

        q = q_ref[0].reshape(r, HEAD_DIM)
        do2 = _stack_heads(do_ref)
        qt = q.astype(F32).T
        dot32 = do2.astype(F32).T
        ot32 = _stack_heads(o_ref).astype(F32).T
        drow = jnp.sum(dot32 * ot32, axis=0, keepdims=True)
        qt, dot = qt.astype(BF16), dot32.astype(BF16)
        lrow = l_ref[0, 0, 0, 0:1, :]

        @pl.when(i == 0)
        def _():
            dkt_sc[...] = jnp.zeros_like(dkt_sc)
            dvt_sc[...] = jnp.zeros_like(dvt_sc)

        def chunk(c):
            return pl.ds(pl.multiple_of(c * tk, tk), tk)

        def scores(c, slot, qv=q, dov=do2):
            st_sc[slot] = _dot_nt(k_ref[0, 0, chunk(c), :], qv)
            dp_sc[slot] = _dot_nt(v_ref[0, 0, chunk(c), :], dov)

        def fold(slot, c, dqt):
            pt = jnp.exp(st_sc[slot] - lrow)
            dsb = (pt * (dp_sc[slot] - drow)).astype(BF16)
            dvt_sc[:, chunk(c)] += _dot_nt(dot, pt.astype(BF16))
            dkt_sc[:, chunk(c)] += _dot_nt(qt, dsb)
            return dqt + _dot(kt_ref[0, 0, :, chunk(c)], dsb)

        @pl.when(i == 0)
        def _():
            scores(0, 0)

        def step(c2, dqt):
            c = 2 * c2
            scores(c + 1, 1)
            dqt = fold(0, c, dqt)
            scores(c + 2, 0)
            return fold(1, c + 1, dqt)

        dqt = jnp.zeros((HEAD_DIM, r), F32)
        for c2 in range(nk // 2 - 1):
            dqt = step(c2, dqt)
        scores(nk - 1, 1)
        dqt = fold(0, nk - 2, dqt)
        scores(0, 0, qn_ref[0].reshape(r, HEAD_DIM), _stack_heads(don_ref))
        dqt = fold(1, nk - 1, dqt)
        dq_ref[0] = dqt.T.reshape(GROUP, tq, HEAD_DIM)

        @pl.when(i == nq - 1)
        def _():
            dk_ref[0, 0] = dkt_sc[...].T
            dv_ref[0, 0] = dvt_sc[...].T

        pl.when(step_id == bl * kv * nq - 1)(wait)

    kvspec = pl.BlockSpec((1, 1, s_len, HEAD_DIM), lambda b, g, i: (b, g, 0, 0))
    qspec = pl.BlockSpec((1, GROUP, tq, HEAD_DIM), lambda b, g, i: (b, g, i, 0))
    tok = pl.BlockSpec((tq, GROUP * HEAD_DIM), lambda b, g, i: (b * nq + i, g))
    qnext = pl.BlockSpec((1, GROUP, tq, HEAD_DIM), lambda b, g, i: (b, g, jnp.minimum(i + 1, nq - 1), 0))
    toknext = pl.BlockSpec((tq, GROUP * HEAD_DIM), lambda b, g, i: (b * nq + jnp.minimum(i + 1, nq - 1), g))
    anyspec = pl.BlockSpec(memory_space=pl.ANY)
    res = pl.pallas_call(
        body,
        grid=(bl, kv, nq),
        in_specs=[qspec, qnext, kvspec, pl.BlockSpec((1, 1, HEAD_DIM, s_len), lambda b, g, i: (b, g, 0, 0)), kvspec,
                  tok, toknext, tok, pl.BlockSpec((1, 1, 1, 8, r), lambda b, g, i: (b, g, i, 0, 0))] + [anyspec] * ng,
        out_specs=[qspec, kvspec, kvspec] + [anyspec] * ng,
        out_shape=[SDS(qa.shape, F32), SDS(ka.shape, F32), SDS(va.shape, F32)]
        + [SDS(g.shape, g.dtype) for g in grads],
        scratch_shapes=[pltpu.VMEM((2, tk, r), F32), pltpu.VMEM((2, tk, r), F32),
                        pltpu.VMEM((HEAD_DIM, s_len), F32), pltpu.VMEM((HEAD_DIM, s_len), F32)]
        + _exchange_scratch(ng),
        compiler_params=_cp("arbitrary", "arbitrary", "arbitrary"),
        name="attn_a_bwd",
    )(qa, qa, ka, kat, va, do, do, o, lse, *grads)
    return res[0], res[1], res[2], res[3:]


def _attn_b_bwd(qb, kb, kbt, vb, do, o, lse, bias_t, sink, s_len):
    bl, hb, _, _ = qb.shape
    kv, sp = kb.shape[1], kb.shape[2]
    nb = s_len // BLOCK
    nbs = min(QB_PER_STEP, nb)
    r = GROUP * BLOCK

    def body(q_ref, k_ref, kt_ref, v_ref, do_ref, o_ref, l_ref, bt_ref, sink_ref,
             dq_ref, dk_ref, dv_ref, dsum_ref, dsink_ref, dkt_sc, dvt_sc):
        g, b, ns = pl.program_id(0), pl.program_id(1), pl.program_id(2)
        sink_row = _sink_row(sink_ref, g)

        @pl.when(ns == 0)
        def _():
            dkt_sc[...] = jnp.zeros_like(dkt_sc)
            dvt_sc[...] = jnp.zeros_like(dvt_sc)

        @pl.when((b == 0) & (ns == 0))
        def _():
            dsum_ref[...] = jnp.zeros_like(dsum_ref)
            dsink_ref[...] = jnp.zeros_like(dsink_ref)

        dsum = jnp.zeros((SPAN, r), F32)
        dsink = jnp.zeros((1, r), F32)
        for j in range(nbs):
            n = ns * nbs + j
            span = pl.ds(pl.multiple_of(n * BLOCK, BLOCK), SPAN)
            rows = slice(j * BLOCK, (j + 1) * BLOCK)
            q = q_ref[0, :, rows, :].reshape(r, HEAD_DIM)
            do2 = jnp.concatenate([do_ref[rows, h * HEAD_DIM:(h + 1) * HEAD_DIM] for h in range(GROUP)], axis=0)
            o2 = jnp.concatenate([o_ref[rows, h * HEAD_DIM:(h + 1) * HEAD_DIM] for h in range(GROUP)], axis=0)
            dot32 = do2.astype(F32).T
            drow = jnp.sum(dot32 * o2.astype(F32).T, axis=0, keepdims=True)
            qt, dot = q.astype(F32).T.astype(BF16), dot32.astype(BF16)
            lrow = l_ref[0, 0, j, 0:1, :]
            st = _dot_nt(k_ref[0, 0, span, :], q) + bt_ref[_bias_variant(n, nb), 0]
            pt = jnp.exp(st - lrow)
            dst = pt * (_dot_nt(v_ref[0, 0, span, :], do2) - drow)
            dsum = dsum + dst
            dsink = dsink - jnp.exp(sink_row - lrow) * drow
            dsb = dst.astype(BF16)
            dvt_sc[:, span] += _dot_nt(dot, pt.astype(BF16))
            dkt_sc[:, span] += _dot_nt(qt, dsb)
            dq_ref[0, :, rows, :] = _dot(kt_ref[0, 0, :, span], dsb).T.reshape(GROUP, BLOCK, HEAD_DIM)
        dsum_ref[0] += dsum
        dsink_ref[0, 0:1, :] += dsink

        @pl.when(ns == nb // nbs - 1)
        def _():
            dk_ref[0, 0] = dkt_sc[:, BLOCK:BLOCK + s_len].T
            dv_ref[0, 0] = dvt_sc[:, BLOCK:BLOCK + s_len].T

    kvspec = pl.BlockSpec((1, 1, sp, HEAD_DIM), lambda g, b, n: (b, g, 0, 0))
    kvout = pl.BlockSpec((1, 1, s_len, HEAD_DIM), lambda g, b, n: (b, g, 0, 0))
    qspec = pl.BlockSpec((1, GROUP, nbs * BLOCK, HEAD_DIM), lambda g, b, n: (b, g, n, 0))
    tok = pl.BlockSpec((nbs * BLOCK, GROUP * HEAD_DIM), lambda g, b, n: (b * (nb // nbs) + n, g))
    return pl.pallas_call(
        body,
        grid=(kv, bl, nb // nbs),
        in_specs=[qspec, kvspec, pl.BlockSpec((1, 1, HEAD_DIM, sp), lambda g, b, n: (b, g, 0, 0)), kvspec, tok, tok,
                  pl.BlockSpec((1, 1, nbs, 8, r), lambda g, b, n: (b, g, n, 0, 0)),
                  pl.BlockSpec((3, 1, SPAN, r), lambda g, b, n: (0, g, 0, 0)),
                  pl.BlockSpec(memory_space=pltpu.SMEM)],
        out_specs=[qspec, kvout, kvout,
                   pl.BlockSpec((1, SPAN, r), lambda g, b, n: (g, 0, 0)),
                   pl.BlockSpec((1, 8, r), lambda g, b, n: (g, 0, 0))],
        out_shape=[SDS(qb.shape, F32), SDS((bl, kv, s_len, HEAD_DIM), F32), SDS((bl, kv, s_len, HEAD_DIM), F32),
                   SDS((kv, SPAN, r), F32), SDS((kv, 8, r), F32)],
        scratch_shapes=[pltpu.VMEM((HEAD_DIM, sp), F32), pltpu.VMEM((HEAD_DIM, sp), F32)],
        compiler_params=_cp("arbitrary", "arbitrary", "arbitrary"),
        name="attn_b_bwd",
    )(qb, kb, kbt, vb, do, o, lse, bias_t, sink)


def _bias_reduce(dsum, dsink, bucket_t4):
    kv, _, r = dsum.shape

    def body(ds_ref, dk_ref, bk_ref, rel_ref, sink_ref):
        lane = lax.broadcasted_iota(jnp.int32, (N_BUCKETS, 128), 1)
        lane8 = lax.broadcasted_iota(jnp.int32, (8, 128), 1)
        bk = bk_ref[...]
        for g in range(kv):
            ds = ds_ref[g]
            rowi = lax.broadcasted_iota(jnp.int32, (N_BUCKETS, r), 0)
            red = jnp.zeros((N_BUCKETS, r), F32)
            for b in range(N_BUCKETS):
                red = jnp.where(rowi == b, jnp.sum(jnp.where(bk == b, ds, 0.0), axis=0, keepdims=True), red)
            out = jnp.zeros((N_BUCKETS, 128), F32)
            so = jnp.zeros((8, 128), F32)
            for h in range(GROUP):
                col = jnp.sum(red[:, h * BLOCK:(h + 1) * BLOCK], axis=1, keepdims=True)
                out = jnp.where(lane == h, col, out)
                sc = jnp.sum(dk_ref[g][:, h * BLOCK:(h + 1) * BLOCK], axis=1, keepdims=True)
                so = jnp.where(lane8 == h, sc, so)
            rel_ref[g] = out
            sink_ref[g] = so

    vm = pl.BlockSpec(memory_space=pltpu.VMEM)
    return pl.pallas_call(
        body,
        in_specs=[vm, vm, vm],
        out_specs=[vm, vm],
        out_shape=[SDS((kv, N_BUCKETS, 128), F32), SDS((kv, 8, 128), F32)],
        name="bias_reduce",
    )(dsum, dsink, bucket_t4)


def _dqkprep(dqa, dka, dva, dqb, dkb, dvb, proj, cos, sin_signed, gq, gk, s_len, ts):
    t, p_cols = proj.shape
    bl, ha = dqa.shape[0], dqa.shape[1]
    kva, hb, kvb = dka.shape[1], dqb.shape[1], dkb.shape[1]
    ns = s_len // ts

    def body(dqa_ref, dka_ref, dva_ref, dqb_ref, dkb_ref, dvb_ref, p_ref, cos_ref, sin_ref, gq_ref, gk_ref,
             dp_ref, dgq_ref, dgk_ref):
        b, i = pl.program_id(0), pl.program_id(1)
        cs, sn = cos_ref[...], sin_ref[...]
        low, first = _pair_masks(ts)

        @pl.when((b == 0) & (i == 0))
        def _():
            dgq_ref[...] = jnp.zeros_like(dgq_ref)
            dgk_ref[...] = jnp.zeros_like(dgk_ref)

        def grad_pair(ref, p):
            return jnp.concatenate([ref[0, 2 * p], ref[0, 2 * p + 1]], axis=1)

        def put(p, val):
            dp_ref[:, p * PAIR:(p + 1) * PAIR] = val.astype(BF16)

        def unrope_norm(d_rot, p, g, dg_ref):
            dn = d_rot * cs + _pair_partner(d_rot * sn, first)
            xp = p_ref[:, p * PAIR:(p + 1) * PAIR]
            r = lax.rsqrt(_pair_mean(xp * xp, low) + EPS)
            n = xp * r
            gd = g * dn
            dg_ref[0:1, :] += jnp.sum(dn * n, axis=0, keepdims=True)
            put(p, r * (gd - n * _pair_mean(n * gd, low)))

        for p in range(ha // 2):
            unrope_norm(grad_pair(dqa_ref, p) * SCALE, p, gq_ref[...], dgq_ref)
        base = ha // 2
        for p in range(kva // 2):
            unrope_norm(grad_pair(dka_ref, p), base + p, gk_ref[...], dgk_ref)
            put(base + kva // 2 + p, grad_pair(dva_ref, p))
        base += kva
        for p in range(hb // 2):
            put(base + p, grad_pair(dqb_ref, p) * SCALE)
        base += hb // 2
        for p in range(kvb // 2):
            put(base + p, grad_pair(dkb_ref, p))
            put(base + kvb // 2 + p, grad_pair(dvb_ref, p))

    def hm(nh):
        return pl.BlockSpec((1, nh, ts, HEAD_DIM), lambda b, i: (b, 0, i, 0))

    vec = pl.BlockSpec((1, PAIR), lambda b, i: (0, 0))
    tab = pl.BlockSpec((ts, PAIR), lambda b, i: (i, 0))
    acc = pl.BlockSpec((8, PAIR), lambda b, i: (0, 0))
    pspec = pl.BlockSpec((ts, p_cols), lambda b, i: (b * ns + i, 0))
    return pl.pallas_call(
        body,
        grid=(bl, ns),
        in_specs=[hm(ha), hm(kva), hm(kva), hm(hb), hm(kvb), hm(kvb), pspec, tab, tab, vec, vec],
        out_specs=[pspec, acc, acc],
        out_shape=[SDS((t, p_cols), BF16), SDS((8, PAIR), F32), SDS((8, PAIR), F32)],
        compiler_params=_cp("arbitrary", "arbitrary"),
        name="dqkprep",
    )(dqa, dka, dva, dqb, dkb, dvb, proj, cos, sin_signed, gq, gk)


def _dx_final(dproj, w_t, x2, dx1, g1, tm, grads):
    t, d = x2.shape
    p_cols = w_t.shape[0]
    ng = len(grads)
    nsteps = t // tm

    def body(dp_ref, w_ref, x_ref, dx1_ref, g_ref, *rest):
        grad_refs, (dx_ref, dg_ref), parts = rest[:ng], rest[ng:ng + 2], rest[ng + 2:2 * ng + 2]
        start, wait = _direct_exchange("scatter", grad_refs, parts, *rest[2 * ng + 2:])

        @pl.when(pl.program_id(0) == 0)
        def _():
            start()
            dg_ref[...] = jnp.zeros_like(dg_ref)

        dh = _dot(dp_ref[...], w_ref[...])
        g = g_ref[...]
        _, n, r = _rms_fwd(x_ref[...], g)
        dx, dgt = _rms_bwd(n, r, g, dh)
        dx_ref[...] = dx1_ref[...] + dx
        dg_ref[0:1, :] += jnp.sum(dgt, axis=0, keepdims=True)
        pl.when(pl.program_id(0) == nsteps - 1)(wait)

    tile = pl.BlockSpec((tm, d), lambda i: (i, 0))
    anyspec = pl.BlockSpec(memory_space=pl.ANY)
    res = pl.pallas_call(
        body,
        grid=(nsteps,),
        in_specs=[pl.BlockSpec((tm, p_cols), lambda i: (i, 0)),
                  pl.BlockSpec((p_cols, d), lambda i: (0, 0)),
                  tile, tile, pl.BlockSpec((1, d), lambda i: (0, 0))] + [anyspec] * ng,
        out_specs=[tile, pl.BlockSpec((8, d), lambda i: (0, 0))] + [anyspec] * ng,
        out_shape=[SDS((t, d), F32), SDS((8, d), F32)] + [SDS(g.shape, g.dtype) for g in grads],
        scratch_shapes=_exchange_scratch(ng),
        compiler_params=_cp("arbitrary"),
        name="dx_final",
    )(dproj, w_t, x2, dx1, g1, *grads)
    return res[0], res[1], res[2:]


def _adamw_math(w, g, m, v):
    m = ADAM_B1 * m + (1.0 - ADAM_B1) * g
    v = ADAM_B2 * v + (1.0 - ADAM_B2) * (g * g)
    m_hat = m / (1.0 - ADAM_B1 ** ADAM_STEP)
    v_hat = v / (1.0 - ADAM_B2 ** ADAM_STEP)
    delta = -ADAM_LR * (m_hat / (jnp.sqrt(v_hat) + ADAM_EPS) + ADAM_WD * w)
    return delta, m, v


def _adamw_sum(parts, w, m, v, tr, name):
    rows, cols = w.shape

    def body(p_ref, w_ref, m_ref, v_ref, g_ref, d_ref, nm_ref, nv_ref):
        g = p_ref[0].astype(F32)
        for s in range(1, N_DEV):
            g = g + p_ref[s].astype(F32)
        g_ref[...] = g
        d_ref[...], nm_ref[...], nv_ref[...] = _adamw_math(w_ref[...], g, m_ref[...], v_ref[...])

    tr = min(tr, rows)
    tile = pl.BlockSpec((tr, cols), lambda i: (i, 0))
    return pl.pallas_call(
        body,
        grid=(rows // tr,),
        in_specs=[pl.BlockSpec((N_DEV, tr, cols), lambda i: (0, i, 0)), tile, tile, tile],
        out_specs=[tile] * 4,
        out_shape=[SDS((rows, cols), F32)] * 4,
        compiler_params=_cp("parallel"),
        name=name,
    )(parts, w, m, v)


def _adamw_small(vec, rel, ws, ms, vs):
    hb = ws[6].shape[1]
    n = len(ws)

    def body(vec_ref, rel_ref, *rest):
        w_refs, m_refs, v_refs = rest[:n], rest[n:2 * n], rest[2 * n:3 * n]
        loss_ref, outs = rest[3 * n], rest[3 * n + 1:]
        grads = [vec_ref[0:1, :], vec_ref[1:2, :], vec_ref[2:3, :], vec_ref[3:4, :],
                 vec_ref[4:5, 0:HEAD_DIM], vec_ref[4:5, SMALL_LANES:SMALL_LANES + HEAD_DIM],
                 vec_ref[4:5, 2 * SMALL_LANES:2 * SMALL_LANES + hb], rel_ref[:, 0:hb]]
        loss_ref[...] = vec_ref[4:5, 3 * SMALL_LANES:3 * SMALL_LANES + 1]
        for p, g in enumerate(grads):
            g_ref, d_ref, nm_ref, nv_ref = outs[4 * p:4 * p + 4]
            g_ref[...] = g
            d_ref[...], nm_ref[...], nv_ref[...] = _adamw_math(w_refs[p][...], g, m_refs[p][...], v_refs[p][...])

    vm = pl.BlockSpec(memory_space=pltpu.VMEM)
    res = pl.pallas_call(
        body,
        in_specs=[vm] * (2 + 3 * n),
        out_specs=[vm] * (1 + 4 * n),
        out_shape=[SDS((1, 1), F32)] + [SDS(w.shape, F32) for w in ws for _ in range(4)],
        name="adamw_small",
    )(vec, rel, *ws, *ms, *vs)
    return res[0], [res[1 + 4 * p:5 + 4 * p] for p in range(n)]


def _local_step(x, loss_target, win_g, wo_s, wup_s, wdn_s, g_pre_mix, g_post_mix, q_norm_a, k_norm_a, sink_b,
                rel_bias, g_pre_ffn, g_post_ffn):
    bl, s_len, d = x.shape
    t = bl * s_len
    nh = d // HEAD_DIM
    ha = nh // 2
    kva = ha // GROUP
    hb = nh - ha
    kvb = hb // GROUP
    tm = 512
    tw = min(4096, t)
    ts = min(512, s_len)
    tq, tk = 2 * BLOCK, min(512, s_len // 2)

    x2 = x.reshape(t, d)
    tg2 = loss_target.reshape(t, d)
    cos, sin_signed = _rope_tables(s_len)
    gq2, gk2 = jnp.tile(q_norm_a, (1, 2)), jnp.tile(k_norm_a, (1, 2))
    a = jnp.arange(BLOCK, dtype=jnp.int32)
    c = jnp.arange(SPAN, dtype=jnp.int32)
    bucket_t = _t5_bucket(c[:, None] - BLOCK - a[None, :])
    bucket_t4 = jnp.tile(bucket_t, (1, GROUP))
    w_in_t = win_g.reshape(-1, d)
    p_cols = w_in_t.shape[0]

    h1, proj = _inproj(x2, g_pre_mix, w_in_t, tm)
    qa, ka, kat, va, vat, qb, kb, kbt, vb, vbt = _qkprep(
        proj, cos, sin_signed, gq2, gk2, bl, s_len, ha, kva, hb, kvb, ts)
    bias_t = _bias_build(bucket_t, rel_bias, hb)
    oa, lse_a, (wo_g, wup_g, wdn_g) = _attn_a_fwd(qa, ka, vat, tq, tk, [wo_s, wup_s, wdn_s])
    wo = wo_g.reshape(-1, d)
    wdn = wdn_g.reshape(-1, d)
    ob, lse_b = _attn_b_fwd(qb, kb, vbt, bias_t, sink_b, s_len)
    mix, x1, h2 = _mixout(oa, ob, wo, x2, g_post_mix, g_pre_ffn, tm)
    u, df, dy, dg4, loss8 = _ffn_fwd(h2, wup_g, wdn, x1, tg2, g_post_ffn, tm, FFN_BLOCKS_PER_STEP)

    dpre, dx1, dmix, dg3, dg2 = _ffn_bwd(df, u, wdn, wup_g, x1, dy, mix, g_pre_ffn, g_post_mix, FFN_BWD_TOKENS,
                                         FFN_BLOCKS_PER_STEP)
    gw_dn = _wgrad_rows(u, df, N_DEV, tw, "wgrad_down", square=True)
    gw_up = _wgrad_cols(h2, dpre, N_DEV, tw, "wgrad_up")
    gw_o = _wgrad_o(oa, ob, dmix, N_DEV, min(2048, t))
    doa, dob = _attn_out_bwd(dmix, wo, oa.shape[1], tm)
    dqa, dka, dva, (p_o, p_up, p_dn) = _attn_a_bwd(qa, ka, kat, va, doa, oa, lse_a, tq, tk, [gw_o, gw_up, gw_dn])
    dqb, dkb, dvb, dsum, dsink = _attn_b_bwd(qb, kb, kbt, vb, dob, ob, lse_b, bias_t, sink_b, s_len)
    drel_g, dsink_g = _bias_reduce(dsum, dsink, bucket_t4)
    dproj, dgq, dgk = _dqkprep(dqa, dka, dva, dqb, dkb, dvb, proj, cos, sin_signed, gq2, gk2, s_len, ts)
    gw_in_t = _wgrad_rows(dproj, h1, p_cols // 256, tw, "wgrad_in").reshape(N_DEV, -1, d)
    grad_x, dg1, (p_in,) = _dx_final(dproj, w_in_t, x2, dx1, g_pre_mix, tm, [gw_in_t])

    vec, rel = _small_allreduce([dg1, dg2, dg3, dg4], dgq, dgk, dsink_g, drel_g, loss8)
    return grad_x.reshape(bl, s_len, d), p_in, p_o, p_up, p_dn, vec, rel


def kernel(x, w_in, w_o, g_pre_mix, g_post_mix, q_norm_a, k_norm_a, sink_b, rel_bias, g_pre_ffn, w_ffn_up, w_ffn_down, g_post_ffn, loss_target, m_w_in, m_w_o, m_g_pre_mix, m_g_post_mix, m_q_norm_a, m_k_norm_a, m_sink_b, m_rel_bias, m_g_pre_ffn, m_w_ffn_up, m_w_ffn_down, m_g_post_ffn, v_w_in, v_w_o, v_g_pre_mix, v_g_post_mix, v_q_norm_a, v_k_norm_a, v_sink_b, v_rel_bias, v_g_pre_ffn, v_w_ffn_up, v_w_ffn_down, v_g_post_ffn):
    w_in_t = w_in[0].T
    (win_g,) = _weight_gather([w_in_t.astype(BF16)])

    grad_x, p_in, p_o, p_up, p_dn, vec, rel = _local_step(
        x, loss_target, win_g, w_o[0].astype(BF16), w_ffn_up[0].astype(BF16), w_ffn_down[0].astype(BF16),
        g_pre_mix, g_post_mix, q_norm_a, k_norm_a, sink_b, rel_bias, g_pre_ffn, g_post_ffn)

    big = {
        "w_in": [a.T for a in _adamw_sum(p_in, w_in_t, m_w_in[0].T, v_w_in[0].T, 192, "adamw_in")],
        "w_o": _adamw_sum(p_o, w_o[0], m_w_o[0], v_w_o[0], 128, "adamw_o"),
        "w_up": _adamw_sum(p_up, w_ffn_up[0], m_w_ffn_up[0], v_w_ffn_up[0], 256, "adamw_up"),
        "w_dn": _adamw_sum(p_dn, w_ffn_down[0], m_w_ffn_down[0], v_w_ffn_down[0], 256, "adamw_down"),
    }
    loss, small = _adamw_small(
        vec, rel,
        [g_pre_mix, g_post_mix, g_pre_ffn, g_post_ffn, q_norm_a, k_norm_a, sink_b, rel_bias],
        [m_g_pre_mix, m_g_post_mix, m_g_pre_ffn, m_g_post_ffn, m_q_norm_a, m_k_norm_a, m_sink_b, m_rel_bias],
        [v_g_pre_mix, v_g_post_mix, v_g_pre_ffn, v_g_post_ffn, v_q_norm_a, v_k_norm_a, v_sink_b, v_rel_bias])
    s_pre_mix, s_post_mix, s_pre_ffn, s_post_ffn, s_qn, s_kn, s_sink, s_rel = small

    def outs(kind):
        return [big["w_in"][kind][None], big["w_o"][kind][None], s_pre_mix[kind], s_post_mix[kind], s_qn[kind],
                s_kn[kind], s_sink[kind], s_rel[kind], s_pre_ffn[kind], big["w_up"][kind][None],
                big["w_dn"][kind][None], s_post_ffn[kind]]

    return (loss.reshape(()), grad_x, *outs(0), *outs(1), *outs(2), *outs(3))
```

```python
import jax
import jax.numpy as jnp
import numpy as np
from jax import lax
from jax.experimental import pallas as pl
from jax.experimental.pallas import tpu as pltpu

F32 = jnp.float32
BF16 = jnp.bfloat16
SDS = jax.ShapeDtypeStruct

N_DEV = 8
HEAD_DIM = 64
GROUP = 4
BLOCK = 128
SPAN = 3 * BLOCK
GRID_W = 64
N_BUCKETS = 32
MAX_DISTANCE = 128
ROPE_THETA = 10000.0
EPS = 1e-6
NEG_INF = -1e30
SCALE = HEAD_DIM ** -0.5
VT_PAD = 16

ADAM_LR = 0.001
ADAM_B1 = 0.9
ADAM_B2 = 0.999
ADAM_EPS = 1e-08
ADAM_WD = 0.01
ADAM_STEP = 10

VMEM_LIMIT = 56 * 1024 * 1024
MESH = pl.DeviceIdType.MESH


def _cp(*sem):
    return pltpu.CompilerParams(dimension_semantics=sem, vmem_limit_bytes=VMEM_LIMIT)


def _dot(a, b):
    return jnp.dot(a, b, preferred_element_type=F32)


def _dot_nt(a, b):
    return lax.dot_general(a, b, (((1,), (1,)), ((), ())), preferred_element_type=F32)


def _dot_tn(a, b):
    return lax.dot_general(a, b, (((0,), (0,)), ((), ())), preferred_element_type=F32)


def _rms_fwd(x, g):
    r = lax.rsqrt(jnp.mean(x * x, axis=-1, keepdims=True) + EPS)
    n = x * r
    return n * g, n, r


def _rms_bwd(n, r, g, dy):
    gd = g * dy
    dx = r * (gd - n * jnp.mean(n * gd, axis=-1, keepdims=True))
    return dx, dy * n


def _rope_tables(s_len):
    rows = s_len // GRID_W
    row = np.repeat(np.arange(rows, dtype=np.int32), GRID_W)
    col = np.tile(np.arange(GRID_W, dtype=np.int32), rows)
    nf = HEAD_DIM // 4
    freqs = np.float32(ROPE_THETA) ** (-np.arange(nf, dtype=np.float32) / np.float32(nf))
    ang_r = row.astype(np.float32)[:, None] * freqs[None, :]
    ang_c = col.astype(np.float32)[:, None] * freqs[None, :]
    cr, sr, cc, sc = np.cos(ang_r), np.sin(ang_r), np.cos(ang_c), np.sin(ang_c)
    cos = np.concatenate([cr, cr, cc, cc] * 2, axis=-1).astype(np.float32)
    sin_signed = np.concatenate([-sr, sr, -sc, sc] * 2, axis=-1).astype(np.float32)
    return jnp.asarray(cos), jnp.asarray(sin_signed)


def _t5_bucket(rel):
    nb = N_BUCKETS // 2
    ret = (rel > 0).astype(jnp.int32) * nb
    n = jnp.abs(rel)
    max_exact = nb // 2
    nf = jnp.maximum(n, 1).astype(F32)
    large = max_exact + (jnp.log(nf / max_exact) / np.float32(np.log(MAX_DISTANCE / max_exact))
                         * (nb - max_exact)).astype(jnp.int32)
    large = jnp.minimum(large, nb - 1)
    return ret + jnp.where(n < max_exact, n, large)


def _mesh_pos():
    return lax.axis_index("x"), lax.axis_index("y"), lax.axis_index("c")


def _lin(p):
    return 4 * p[0] + 2 * p[1] + p[2]


def _weight_gather(shards):
    n = len(shards)

    def body(*refs):
        xs, outs = refs[:n], refs[n:2 * n]
        send_sems, recv_sems, local_sems = refs[2 * n:]
        x, y, c = _mesh_pos()
        me, sibling = (x, y, c), (x, y, 1 - c)
        chips = [(1 - x, y), (x, 1 - y), (1 - x, 1 - y)]

        def copy(a, k, block, to, src=None):
            slot = outs[a].at[_lin(block)]
            return pltpu.make_async_remote_copy(
                src_ref=slot if src is None else src, dst_ref=slot,
                send_sem=send_sems.at[a, k], recv_sem=recv_sems.at[a, k],
                device_id=to, device_id_type=MESH)

        started = []
        for a in range(n):
            mine = pltpu.make_async_copy(xs[a], outs[a].at[_lin(me)], local_sems.at[a])
            mine.start()
            started.append(mine)
        sends = []
        for a in range(n):
            first = [copy(a, 0, me, sibling, src=xs[a])]
            first += [copy(a, 1 + j, me, (*chip, c), src=xs[a]) for j, chip in enumerate(chips)]
            for cp in first:
                cp.start()
            sends += first
        for a in range(n):
            for j, chip in enumerate(chips):
                copy(a, 1 + j, (*chip, c), me).wait_recv()
                fwd = copy(a, 4 + j, (*chip, c), sibling)
                fwd.start()
                sends.append(fwd)
        for a in range(n):
            copy(a, 0, sibling, me).wait_recv()
            for j, chip in enumerate(chips):
                copy(a, 4 + j, (*chip, 1 - c), me).wait_recv()
        for cp in sends:
            cp.wait_send()
        for mine in started:
            mine.wait()

    anyspec = pl.BlockSpec(memory_space=pl.ANY)
    return pl.pallas_call(
        body,
        out_shape=[SDS((N_DEV,) + s.shape, s.dtype) for s in shards],
        in_specs=[anyspec] * n,
        out_specs=[anyspec] * n,
        scratch_shapes=[pltpu.SemaphoreType.DMA((n, 7)), pltpu.SemaphoreType.DMA((n, 7)),
                        pltpu.SemaphoreType.DMA((n,))],
        name="weight_gather",
    )(*shards)


def _direct_exchange(kind, ins, outs, send_sems, recv_sems, local_sems):
    x, y, c = _mesh_pos()
    me = (x, y, c)
    peers = [(x, y, 1 - c), (1 - x, y, c), (x, 1 - y, c), (1 - x, 1 - y, c),
             (1 - x, y, 1 - c), (x, 1 - y, 1 - c), (1 - x, 1 - y, 1 - c)]

    def src(a, to):
        return ins[a] if kind == "gather" else ins[a].at[_lin(to)]

    def remote(a, k, to, frm):
        return pltpu.make_async_remote_copy(
            src_ref=src(a, to), dst_ref=outs[a].at[_lin(frm)],
            send_sem=send_sems.at[a, k], recv_sem=recv_sems.at[a, k],
            device_id=to, device_id_type=MESH)

    n = len(ins)
    sends = [remote(a, k, p, me) for a in range(n) for k, p in enumerate(peers)]
    arrivals = [remote(a, k, p, p) for a in range(n) for k, p in enumerate(peers)]
    local = [pltpu.make_async_copy(src(a, me), outs[a].at[_lin(me)], local_sems.at[a]) for a in range(n)]

    def start():
        for cp in local + sends:
            cp.start()

    def wait():
        for cp in arrivals:
            cp.wait_recv()
        for cp in sends:
            cp.wait_send()
        for cp in local:
            cp.wait()

    return start, wait


def _exchange_scratch(n):
    return [pltpu.SemaphoreType.DMA((n, 7)), pltpu.SemaphoreType.DMA((n, 7)), pltpu.SemaphoreType.DMA((n,))]


SMALL_LANES = 128


def _small_allreduce(dg_rows, dgq, dgk, dsink_g, drel_g, loss8):
    d = dg_rows[0].shape[1]
    kv = dsink_g.shape[0]

    def body(g1_ref, g2_ref, g3_ref, g4_ref, gq_ref, gk_ref, sk_ref, rl_ref, ls_ref, vec_ref, rel_ref,
             vbuf, rbuf, vland, rland, send_sems, recv_sems):
        x, y, c = _mesh_pos()
        me = (x, y, c)
        peers = [(x, y, 1 - c), (1 - x, y, c), (x, 1 - y, c), (1 - x, 1 - y, c),
                 (1 - x, y, 1 - c), (x, 1 - y, 1 - c), (1 - x, 1 - y, 1 - c)]
        vbuf[...] = jnp.zeros_like(vbuf)
        rbuf[...] = jnp.zeros_like(rbuf)
        for row, ref in enumerate((g1_ref, g2_ref, g3_ref, g4_ref)):
            vbuf[row:row + 1, :] = ref[0:1, :]
        vbuf[4:5, 0:HEAD_DIM] = gq_ref[0:1, 0:HEAD_DIM] + gq_ref[0:1, HEAD_DIM:PAIR]
        vbuf[4:5, SMALL_LANES:SMALL_LANES + HEAD_DIM] = gk_ref[0:1, 0:HEAD_DIM] + gk_ref[0:1, HEAD_DIM:PAIR]
        for g in range(kv):
            vbuf[4:5, 2 * SMALL_LANES + g * GROUP:2 * SMALL_LANES + (g + 1) * GROUP] = sk_ref[g, 0:1, 0:GROUP]
            rbuf[:, g * GROUP:(g + 1) * GROUP] = rl_ref[g, :, 0:GROUP]
        vbuf[4:5, 3 * SMALL_LANES:3 * SMALL_LANES + 1] = ls_ref[0:1, 0:1]

        def copies(k, to, frm):
            return [pltpu.make_async_remote_copy(
                src_ref=buf, dst_ref=land.at[_lin(frm)], send_sem=send_sems.at[a, k], recv_sem=recv_sems.at[a, k],
                device_id=to, device_id_type=MESH) for a, (buf, land) in enumerate(((vbuf, vland), (rbuf, rland)))]

        sends = [cp for k, p in enumerate(peers) for cp in copies(k, p, me)]
        for cp in sends:
            cp.start()
        vland[_lin(me)] = vbuf[...]
        rland[_lin(me)] = rbuf[...]
        for k, p in enumerate(peers):
            for cp in copies(k, p, p):
                cp.wait_recv()
        for cp in sends:
            cp.wait_send()
        vacc, racc = vland[0], rland[0]
        for s in range(1, N_DEV):
            vacc, racc = vacc + vland[s], racc + rland[s]
        vec_ref[...] = vacc
        rel_ref[...] = racc

    vm = pl.BlockSpec(memory_space=pltpu.VMEM)
    return pl.pallas_call(
        body,
        out_shape=[SDS((8, d), F32), SDS((N_BUCKETS, 128), F32)],
        in_specs=[vm] * 9,
        out_specs=[vm, vm],
        scratch_shapes=[pltpu.VMEM((8, d), F32), pltpu.VMEM((N_BUCKETS, 128), F32),
                        pltpu.VMEM((N_DEV, 8, d), F32), pltpu.VMEM((N_DEV, N_BUCKETS, 128), F32),
                        pltpu.SemaphoreType.DMA((2, 7)), pltpu.SemaphoreType.DMA((2, 7))],
        name="small_allreduce",
    )(*dg_rows, dgq, dgk, dsink_g, drel_g, loss8)


def _inproj(x2, g1, w_t, tm):
    t, d = x2.shape
    p = w_t.shape[0]

    def body(x_ref, g_ref, w_ref, h_ref, p_ref):
        y, _, _ = _rms_fwd(x_ref[...], g_ref[...])
        h = y.astype(BF16)
        h_ref[...] = h
        p_ref[...] = _dot_nt(h, w_ref[...])

    return pl.pallas_call(
        body,
        grid=(t // tm,),
        in_specs=[pl.BlockSpec((tm, d), lambda i: (i, 0)),
                  pl.BlockSpec((1, d), lambda i: (0, 0)),
                  pl.BlockSpec((p, d), lambda i: (0, 0))],
        out_specs=[pl.BlockSpec((tm, d), lambda i: (i, 0)),
                   pl.BlockSpec((tm, p), lambda i: (i, 0))],
        out_shape=[SDS((t, d), BF16), SDS((t, p), F32)],
        compiler_params=_cp("parallel"),
        name="inproj",
    )(x2, g1, w_t)


PAIR = 2 * HEAD_DIM


def _pair_masks(ts):
    lane = lax.broadcasted_iota(jnp.int32, (ts, PAIR), 1)
    return lane < HEAD_DIM, (lane % 32) < 16


def _pair_mean(v, low):
    del low
    r = lax.broadcasted_iota(jnp.int32, (PAIR, PAIR), 0) // HEAD_DIM
    c = lax.broadcasted_iota(jnp.int32, (PAIR, PAIR), 1) // HEAD_DIM
    same_head = (r == c).astype(BF16)
    hi = v.astype(BF16)
    lo = (v - hi.astype(F32)).astype(BF16)
    return (_dot(hi, same_head) + _dot(lo, same_head)) * (1.0 / HEAD_DIM)


def _pair_partner(v, first):
    return jnp.where(first, pltpu.roll(v, PAIR - 16, 1), pltpu.roll(v, 16, 1))


def _qkprep(proj, cos, sin_signed, gq, gk, bl, s_len, ha, kva, hb, kvb, ts):
    t, p_cols = proj.shape
    assert ha % 2 == 0 and kva % 2 == 0 and hb % 2 == 0 and kvb % 2 == 0
    ns = s_len // ts
    sp = s_len + 2 * BLOCK

    def body(p_ref, cos_ref, sin_ref, gq_ref, gk_ref, qa_ref, ka_ref, kat_ref, va_ref, vat_ref, qb_ref, kb_ref,
             kbt_ref, vb_ref, vbt_ref):
        i = pl.program_id(1)
        cs, sn = cos_ref[...], sin_ref[...]
        low, first = _pair_masks(ts)
        ones_row = (lax.broadcasted_iota(jnp.int32, (VT_PAD, ts), 0) == 0).astype(BF16)
        heads = (slice(0, HEAD_DIM), slice(HEAD_DIM, PAIR))

        def pair(p):
            return p_ref[:, p * PAIR:(p + 1) * PAIR]

        def normrope(x, g):
            y = x * lax.rsqrt(_pair_mean(x * x, low) + EPS) * g
            return y * cs + _pair_partner(y, first) * sn

        eye = (lax.broadcasted_iota(jnp.int32, (PAIR, PAIR), 0)
               == lax.broadcasted_iota(jnp.int32, (PAIR, PAIR), 1)).astype(BF16)

        def transposed(xb):
            return _dot_nt(eye, xb).astype(BF16)

        for p in range(ha // 2):
            qa_ref[:, p * PAIR:(p + 1) * PAIR] = (normrope(pair(p), gq_ref[...]) * SCALE).astype(BF16)
        base = ha // 2
        for p in range(kva // 2):
            k = normrope(pair(base + p), gk_ref[...]).astype(BF16)
            v = pair(base + kva // 2 + p).astype(BF16)
            kt, vt = transposed(k), transposed(v)
            for e, lanes in enumerate(heads):
                ka_ref[0, 2 * p + e] = k[:, lanes]
                va_ref[0, 2 * p + e] = v[:, lanes]
                kat_ref[0, 2 * p + e] = kt[lanes, :]
                vat_ref[0, 2 * p + e, 0:HEAD_DIM, :] = vt[lanes, :]
                vat_ref[0, 2 * p + e, HEAD_DIM:HEAD_DIM + VT_PAD, :] = ones_row
        base += kva
        for p in range(hb // 2):
            qb_ref[:, p * PAIR:(p + 1) * PAIR] = (pair(base + p) * SCALE).astype(BF16)
        base += hb // 2

        @pl.when(i == 0)
        def _():
            zeros = jnp.zeros((kvb, BLOCK, HEAD_DIM), BF16)
            zeros_t = jnp.zeros((kvb, HEAD_DIM + VT_PAD, BLOCK), BF16)
            for ref in (kb_ref, vb_ref):
                ref[0, :, 0:BLOCK, :] = zeros
                ref[0, :, sp - BLOCK:sp, :] = zeros
            kbt_ref[0, :, :, 0:BLOCK] = zeros_t[:, 0:HEAD_DIM]
            kbt_ref[0, :, :, sp - BLOCK:sp] = zeros_t[:, 0:HEAD_DIM]
            vbt_ref[0, :, :, 0:BLOCK] = zeros_t
            vbt_ref[0, :, :, sp - BLOCK:sp] = zeros_t

        rows = pl.ds(pl.multiple_of(BLOCK + i * ts, BLOCK), ts)
        for p in range(kvb // 2):
            k = pair(base + p).astype(BF16)
            v = pair(base + kvb // 2 + p).astype(BF16)
            kt, vt = transposed(k), transposed(v)
            for e, lanes in enumerate(heads):
                kb_ref[0, 2 * p + e, rows, :] = k[:, lanes]
                vb_ref[0, 2 * p + e, rows, :] = v[:, lanes]
                kbt_ref[0, 2 * p + e, :, rows] = kt[lanes, :]
                vbt_ref[0, 2 * p + e, 0:HEAD_DIM, rows] = vt[lanes, :]
                vbt_ref[0, 2 * p + e, HEAD_DIM:HEAD_DIM + VT_PAD, rows] = ones_row

    def hm(nh):
        return pl.BlockSpec((1, nh, ts, HEAD_DIM), lambda b, i: (b, 0, i, 0))

    def tokmajor(nh):
        return pl.BlockSpec((ts, nh * HEAD_DIM), lambda b, i: (b * ns + i, 0))

    def padded(nh):
        return pl.BlockSpec((1, nh, sp, HEAD_DIM), lambda b, i: (b, 0, 0, 0))

    def padded_t(nh, rows):
        return pl.BlockSpec((1, nh, rows, sp), lambda b, i: (b, 0, 0, 0))

    return pl.pallas_call(
        body,
        grid=(bl, ns),
        in_specs=[pl.BlockSpec((ts, p_cols), lambda b, i: (b * ns + i, 0)),
                  pl.BlockSpec((ts, PAIR), lambda b, i: (i, 0)),
                  pl.BlockSpec((ts, PAIR), lambda b, i: (i, 0)),
                  pl.BlockSpec((1, PAIR), lambda b, i: (0, 0)),
                  pl.BlockSpec((1, PAIR), lambda b, i: (0, 0))],
        out_specs=[tokmajor(ha), hm(kva), pl.BlockSpec((1, kva, HEAD_DIM, ts), lambda b, i: (b, 0, 0, i)), hm(kva),
                   pl.BlockSpec((1, kva, HEAD_DIM + VT_PAD, ts), lambda b, i: (b, 0, 0, i)),
                   tokmajor(hb), padded(kvb), padded_t(kvb, HEAD_DIM), padded(kvb),
                   padded_t(kvb, HEAD_DIM + VT_PAD)],
        out_shape=[SDS((t, ha * HEAD_DIM), BF16), SDS((bl, kva, s_len, HEAD_DIM), BF16),
                   SDS((bl, kva, HEAD_DIM, s_len), BF16),
                   SDS((bl, kva, s_len, HEAD_DIM), BF16), SDS((bl, kva, HEAD_DIM + VT_PAD, s_len), BF16),
                   SDS((t, hb * HEAD_DIM), BF16),
                   SDS((bl, kvb, sp, HEAD_DIM), BF16), SDS((bl, kvb, HEAD_DIM, sp), BF16),
                   SDS((bl, kvb, sp, HEAD_DIM), BF16), SDS((bl, kvb, HEAD_DIM + VT_PAD, sp), BF16)],
        compiler_params=_cp("parallel", "arbitrary"),
        name="qkprep",
    )(proj, cos, sin_signed, gq, gk)


def _bias_build(bucket_t, rel_bias, hb):
    kvb = hb // GROUP

    def body(bkt_ref, tbl_ref, out_ref):
        bkt = bkt_ref[...]
        ci = lax.broadcasted_iota(jnp.int32, (SPAN, BLOCK), 0)
        qi = lax.broadcasted_iota(jnp.int32, (SPAN, BLOCK), 1)
        band = jnp.abs(ci - BLOCK - qi) <= BLOCK
        masks = (band, band & (ci >= BLOCK), band & (ci < 2 * BLOCK))
        for h in range(hb):
            acct = jnp.zeros((SPAN, BLOCK), F32)
            for b in range(N_BUCKETS):
                acct = jnp.where(bkt == b, tbl_ref[b, h], acct)
            lanes = slice((h % GROUP) * BLOCK, (h % GROUP + 1) * BLOCK)
            for var, mask in enumerate(masks):
                out_ref[var, h // GROUP, :, lanes] = jnp.where(mask, acct, NEG_INF)

    vm = pl.BlockSpec(memory_space=pltpu.VMEM)
    return pl.pallas_call(
        body,
        in_specs=[vm, pl.BlockSpec(memory_space=pltpu.SMEM)],
        out_specs=vm,
        out_shape=SDS((3, kvb, SPAN, GROUP * BLOCK), F32),
        name="bias_build",
    )(bucket_t, rel_bias)


def _attn_a_fwd(qa, ka, vat, tq, tk, shards):
    bl, kv, s_len, _ = ka.shape
    ha = qa.shape[1] // HEAD_DIM
    va_rows = vat.shape[2]
    nq, nk = s_len // tq, s_len // tk
    assert nk % 2 == 0
    r = GROUP * tq
    ns = len(shards)

    def body(q_ref, qn_ref, k_ref, v_ref, *rest):
        shard_refs, (o_ref, l_ref), gathered = rest[:ns], rest[ns:ns + 2], rest[ns + 2:2 * ns + 2]
        st_sc, send_sems, recv_sems, local_sems = rest[2 * ns + 2:]
        i = pl.program_id(2)
        step_id = (pl.program_id(0) * kv + pl.program_id(1)) * nq + i
        start, wait = _direct_exchange("gather", shard_refs, gathered, send_sems, recv_sems, local_sems)
        pl.when(step_id == 0)(start)

        q = _stack_heads(q_ref)

        def scores(c, qv):
            return _dot_nt(k_ref[0, 0, pl.ds(pl.multiple_of(c * tk, tk), tk), :], qv)

        def fold(st, c, carry):
            m_old, acc = carry
            m_new = jnp.maximum(m_old, jnp.max(st, axis=0, keepdims=True))
            pt = jnp.exp(st - m_new).astype(BF16)
            vt = v_ref[0, 0, :, pl.ds(pl.multiple_of(c * tk, tk), tk)]
            return m_new, jnp.exp(m_old - m_new) * acc + _dot(vt, pt)

        @pl.when(i == 0)
        def _():
            st_sc[0] = scores(0, q)

        def step(c2, carry):
            c = 2 * c2
            st_sc[1] = scores(c + 1, q)
            carry = fold(st_sc[0], c, carry)
            st_sc[0] = scores(c + 2, q)
            return fold(st_sc[1], c + 1, carry)

        carry = (jnp.full((1, r), -jnp.inf, F32), jnp.zeros((va_rows, r), F32))
        for c2 in range(nk // 2 - 1):
            carry = step(c2, carry)
        st_sc[1] = scores(nk - 1, q)
        carry = fold(st_sc[0], nk - 2, carry)
        st_sc[0] = scores(0, _stack_heads(qn_ref))
        m, acc = fold(st_sc[1], nk - 1, carry)
        l = acc[HEAD_DIM:HEAD_DIM + 1, :]
        o = (acc[0:HEAD_DIM, :] / l).T
        for h in range(GROUP):
            o_ref[:, h * HEAD_DIM:(h + 1) * HEAD_DIM] = o[h * tq:(h + 1) * tq].astype(BF16)
        l_ref[0, 0, 0] = jnp.broadcast_to(m + jnp.log(l), (8, r))
        pl.when(step_id == bl * kv * nq - 1)(wait)

    anyspec = pl.BlockSpec(memory_space=pl.ANY)
    res = pl.pallas_call(
        body,
        grid=(bl, kv, nq),
        in_specs=[pl.BlockSpec((tq, GROUP * HEAD_DIM), lambda b, g, i: (b * nq + i, g)),
                  pl.BlockSpec((tq, GROUP * HEAD_DIM), lambda b, g, i: (b * nq + jnp.minimum(i + 1, nq - 1), g)),
                  pl.BlockSpec((1, 1, s_len, HEAD_DIM), lambda b, g, i: (b, g, 0, 0)),
                  pl.BlockSpec((1, 1, va_rows, s_len), lambda b, g, i: (b, g, 0, 0))] + [anyspec] * ns,
        out_specs=[pl.BlockSpec((tq, GROUP * HEAD_DIM), lambda b, g, i: (b * nq + i, g)),
                   pl.BlockSpec((1, 1, 1, 8, r), lambda b, g, i: (b, g, i, 0, 0))] + [anyspec] * ns,
        out_shape=[SDS((bl * s_len, ha * HEAD_DIM), BF16), SDS((bl, kv, nq, 8, r), F32)]
        + [SDS((N_DEV,) + s.shape, s.dtype) for s in shards],
        scratch_shapes=[pltpu.VMEM((2, tk, r), F32)] + _exchange_scratch(ns),
        compiler_params=_cp("arbitrary", "arbitrary", "arbitrary"),
        name="attn_a_fwd",
    )(qa, qa, ka, vat, *shards)
    return res[0], res[1], res[2:]


FFN_BLOCKS_PER_STEP = 8
FFN_BWD_TOKENS = 256
QB_PER_STEP = 16


def _bias_variant(n, nb):
    return jnp.where(n == 0, 1, jnp.where(n == nb - 1, 2, 0))


def _sink_row(sink_ref, g):
    return jnp.concatenate([jnp.full((1, BLOCK), sink_ref[0, g * GROUP + h], F32) for h in range(GROUP)], axis=1)


def _attn_b_fwd(qb, kb, vbt, bias_t, sink, s_len):
    bl, kv, sp, _ = kb.shape
    hb = qb.shape[1] // HEAD_DIM
    vt_rows = vbt.shape[2]
    nb = s_len // BLOCK
    nbs = min(QB_PER_STEP, nb)
    r = GROUP * BLOCK

    def body(q_ref, k_ref, vt_ref, bt_ref, sink_ref, o_ref, l_ref, st_sc, pb_sc):
        g, n0 = pl.program_id(1), pl.program_id(2) * nbs
        sink_row = _sink_row(sink_ref, g)

        def span(j):
            return pl.ds(pl.multiple_of((n0 + j) * BLOCK, BLOCK), SPAN)

        for j in range(nbs):
            q = _stack_heads(q_ref.at[j * BLOCK:(j + 1) * BLOCK, :])
            st_sc[j] = _dot_nt(k_ref[0, 0, span(j), :], q) + bt_ref[_bias_variant(n0 + j, nb), 0]
        maxes = []
        for j in range(nbs):
            st = st_sc[j]
            m = jnp.maximum(jnp.max(st, axis=0, keepdims=True), sink_row)
            pb_sc[j] = jnp.exp(st - m).astype(BF16)
            maxes.append(m)
        for j in range(nbs):
            m = maxes[j]
            acc = _dot(vt_ref[0, 0, :, span(j)], pb_sc[j])
            l = acc[HEAD_DIM:HEAD_DIM + 1, :] + jnp.exp(sink_row - m)
            o = (acc[0:HEAD_DIM, :] / l).T
            for h in range(GROUP):
                o_ref[j * BLOCK:(j + 1) * BLOCK, h * HEAD_DIM:(h + 1) * HEAD_DIM] = (
                    o[h * BLOCK:(h + 1) * BLOCK].astype(BF16))
            l_ref[0, 0, j] = jnp.broadcast_to(m + jnp.log(l), (8, r))

    return pl.pallas_call(
        body,
        grid=(bl, kv, nb // nbs),
        in_specs=[pl.BlockSpec((nbs * BLOCK, GROUP * HEAD_DIM), lambda b, g, n: (b * (nb // nbs) + n, g)),
                  pl.BlockSpec((1, 1, sp, HEAD_DIM), lambda b, g, n: (b, g, 0, 0)),
                  pl.BlockSpec((1, 1, vt_rows, sp), lambda b, g, n: (b, g, 0, 0)),
                  pl.BlockSpec((3, 1, SPAN, r), lambda b, g, n: (0, g, 0, 0)),
                  pl.BlockSpec(memory_space=pltpu.SMEM)],
        out_specs=[pl.BlockSpec((nbs * BLOCK, GROUP * HEAD_DIM), lambda b, g, n: (b * (nb // nbs) + n, g)),
                   pl.BlockSpec((1, 1, nbs, 8, r), lambda b, g, n: (b, g, n, 0, 0))],
        out_shape=[SDS((bl * s_len, hb * HEAD_DIM), BF16), SDS((bl, kv, nb, 8, r), F32)],
        scratch_shapes=[pltpu.VMEM((nbs, SPAN, r), F32), pltpu.VMEM((nbs, SPAN, r), BF16)],
        compiler_params=_cp("parallel", "parallel", "arbitrary"),
        name="attn_b_fwd",
    )(qb, kb, vbt, bias_t, sink)


def _mixout(oa, ob, wo, x2, g2, g3, tm):
    t, d = x2.shape
    ca = oa.shape[1]

    def body(oa_ref, ob_ref, w_ref, x_ref, g2_ref, g3_ref, mix_ref, x1_ref, h2_ref):
        mix = _dot(oa_ref[...], w_ref[0:ca, :]) + _dot(ob_ref[...], w_ref[ca:, :])
        mix_ref[...] = mix
        y2, _, _ = _rms_fwd(mix, g2_ref[...])
        x1 = x_ref[...] + y2
        x1_ref[...] = x1
        y3, _, _ = _rms_fwd(x1, g3_ref[...])
        h2_ref[...] = y3.astype(BF16)

    tile = lambda w: pl.BlockSpec((tm, w), lambda i: (i, 0))
    vec = pl.BlockSpec((1, d), lambda i: (0, 0))
    return pl.pallas_call(
        body,
        grid=(t // tm,),
        in_specs=[tile(ca), tile(ob.shape[1]), pl.BlockSpec(wo.shape, lambda i: (0, 0)), tile(d), vec, vec],
        out_specs=[tile(d), tile(d), tile(d)],
        out_shape=[SDS((t, d), F32), SDS((t, d), F32), SDS((t, d), BF16)],
        compiler_params=_cp("parallel"),
        name="mixout",
    )(oa, ob, wo, x2, g2, g3)


def _ffn_fwd(h2, wup_g, wdn, x1, target, g4, tm, jb):
    t, d = x1.shape
    nblk, _, tf = wup_g.shape
    ff = nblk * tf
    nt = t // tm
    nj = nblk // jb

    def body(h_ref, wu_ref, wd_ref, x1_ref, tg_ref, g_ref, u_ref, df_ref, dy_ref, dg_ref, loss_ref, acc_sc):
        i, j = pl.program_id(0), pl.program_id(1)

        @pl.when(j == 0)
        def _():
            acc_sc[...] = jnp.zeros_like(acc_sc)

        @pl.when((i == 0) & (j == 0))
        def _():
            dg_ref[...] = jnp.zeros_like(dg_ref)
            loss_ref[...] = jnp.zeros_like(loss_ref)

        h = h_ref[...]
        squares = []
        for s in range(jb):
            u = jnp.maximum(_dot(h, wu_ref[s]), 0.0)
            u_ref[:, s * tf:(s + 1) * tf] = u.astype(BF16)
            squares.append((u * u).astype(BF16))
        acc_sc[...] += _dot(jnp.concatenate(squares, axis=1), wd_ref[...])

        @pl.when(j == nj - 1)
        def _():
            g = g_ref[...]
            y4, n, r = _rms_fwd(acc_sc[...], g)
            e = (x1_ref[...] + y4) - tg_ref[...]
            loss_ref[...] += jnp.sum(e * e) * (0.5 / d)
            dy = e * (1.0 / d)
            dy_ref[...] = dy
            df, dgt = _rms_bwd(n, r, g, dy)
            df_ref[...] = df.astype(BF16)
            dg_ref[0:1, :] += jnp.sum(dgt, axis=0, keepdims=True)

    tile = pl.BlockSpec((tm, d), lambda i, j: (i, 0))
    return pl.pallas_call(
        body,
        grid=(nt, nj),
        in_specs=[tile,
                  pl.BlockSpec((jb, d, tf), lambda i, j: (j, 0, 0)),
                  pl.BlockSpec((jb * tf, d), lambda i, j: (j, 0)),
                  tile, tile,
                  pl.BlockSpec((1, d), lambda i, j: (0, 0))],
        out_specs=[pl.BlockSpec((tm, jb * tf), lambda i, j: (i, j)), tile, tile,
                   pl.BlockSpec((8, d), lambda i, j: (0, 0)),
                   pl.BlockSpec((8, 128), lambda i, j: (0, 0))],
        out_shape=[SDS((t, ff), BF16), SDS((t, d), BF16), SDS((t, d), F32), SDS((8, d), F32), SDS((8, 128), F32)],
        scratch_shapes=[pltpu.VMEM((tm, d), F32)],
        compiler_params=_cp("arbitrary", "arbitrary"),
        name="ffn_fwd",
    )(h2, wup_g, wdn, x1, target, g4)


def _ffn_bwd(df, u, wdn, wup_g, x1, dy, mix, g3, g2, tm, jb):
    t, d = x1.shape
    nblk, _, tf = wup_g.shape
    nt = t // tm
    nj = nblk // jb

    def body(df_ref, u_ref, wd_ref, wu_ref, x1_ref, dy_ref, mix_ref, g3_ref, g2_ref,
             dpre_ref, dx1_ref, dmix_ref, dg3_ref, dg2_ref, acc_sc):
        i, j = pl.program_id(0), pl.program_id(1)

        @pl.when(j == 0)
        def _():
            acc_sc[...] = jnp.zeros_like(acc_sc)

        @pl.when((i == 0) & (j == 0))
        def _():
            dg3_ref[...] = jnp.zeros_like(dg3_ref)
            dg2_ref[...] = jnp.zeros_like(dg2_ref)

        du2 = _dot_nt(df_ref[...], wd_ref[...])
        dpre = (2.0 * u_ref[...].astype(F32) * du2).astype(BF16)
        dpre_ref[...] = dpre
        dh = _dot_nt(dpre[:, 0:tf], wu_ref[0])
        for s in range(1, jb):
            dh = dh + _dot_nt(dpre[:, s * tf:(s + 1) * tf], wu_ref[s])
        acc_sc[...] += dh

        @pl.when(j == nj - 1)
        def _():
            g3, g2 = g3_ref[...], g2_ref[...]
            _, n3, r3 = _rms_fwd(x1_ref[...], g3)
            dx, dgt3 = _rms_bwd(n3, r3, g3, acc_sc[...])
            dx1 = dy_ref[...] + dx
            dx1_ref[...] = dx1
            dg3_ref[0:1, :] += jnp.sum(dgt3, axis=0, keepdims=True)
            _, n2, r2 = _rms_fwd(mix_ref[...], g2)
            dmix, dgt2 = _rms_bwd(n2, r2, g2, dx1)
            dmix_ref[...] = dmix.astype(BF16)
            dg2_ref[0:1, :] += jnp.sum(dgt2, axis=0, keepdims=True)

    tile = pl.BlockSpec((tm, d), lambda i, j: (i, 0))
    vec = pl.BlockSpec((1, d), lambda i, j: (0, 0))
    acc8 = pl.BlockSpec((8, d), lambda i, j: (0, 0))
    return pl.pallas_call(
        body,
        grid=(nt, nj),
        in_specs=[tile,
                  pl.BlockSpec((tm, jb * tf), lambda i, j: (i, j)),
                  pl.BlockSpec((jb * tf, d), lambda i, j: (j, 0)),
                  pl.BlockSpec((jb, d, tf), lambda i, j: (j, 0, 0)),
                  tile, tile, tile, vec, vec],
        out_specs=[pl.BlockSpec((tm, jb * tf), lambda i, j: (i, j)), tile, tile, acc8, acc8],
        out_shape=[SDS(u.shape, BF16), SDS((t, d), F32), SDS((t, d), BF16), SDS((8, d), F32), SDS((8, d), F32)],
        scratch_shapes=[pltpu.VMEM((tm, d), F32)],
        compiler_params=_cp("arbitrary", "arbitrary"),
        name="ffn_bwd",
    )(df, u, wdn, wup_g, x1, dy, mix, g3, g2)


def _wgrad(a, b, a_spec, b_spec, out_block, out_shape, nj, nk, name, prep_a=None, prep_b=None):
    acc_shape = out_block[1:]

    def body(a_ref, b_ref, o_ref, acc_sc):
        k = pl.program_id(1)
        av = a_ref[...] if prep_a is None else prep_a(a_ref)
        bv = b_ref[...] if prep_b is None else prep_b(b_ref)
        part = _dot_tn(av, bv)

        @pl.when(k == 0)
        def _():
            acc_sc[...] = part

        @pl.when(k > 0)
        def _():
            acc_sc[...] += part

        @pl.when(k == nk - 1)
        def _():
            o_ref[0] = acc_sc[...].astype(BF16)

    return pl.pallas_call(
        body,
        grid=(nj, nk),
        in_specs=[a_spec, b_spec],
        out_specs=pl.BlockSpec(out_block, lambda j, k: (j, 0, 0)),
        out_shape=SDS(out_shape, BF16),
        scratch_shapes=[pltpu.VMEM(acc_shape, F32)],
        compiler_params=_cp("parallel", "arbitrary"),
        name=name,
    )(a, b)


def _wgrad_cols(a, b, nj, tt, name):
    t, m = a.shape
    bn = b.shape[1] // nj
    return _wgrad(a, b, pl.BlockSpec((tt, m), lambda j, k: (k, 0)), pl.BlockSpec((tt, bn), lambda j, k: (k, j)),
                  (1, m, bn), (nj, m, bn), nj, t // tt, name)


def _wgrad_rows(a, b, nj, tt, name, square=False):
    t, n = b.shape
    bm = a.shape[1] // nj

    def squared(a_ref):
        af = a_ref[...].astype(F32)
        return (af * af).astype(BF16)

    return _wgrad(a, b, pl.BlockSpec((tt, bm), lambda j, k: (k, j)), pl.BlockSpec((tt, n), lambda j, k: (k, 0)),
                  (1, bm, n), (nj, bm, n), nj, t // tt, name, prep_a=squared if square else None)


def _wgrad_o(oa, ob, dmix, nj, tt):
    t, n = dmix.shape
    ca, cb = oa.shape[1], ob.shape[1]
    m = ca + cb
    nk = t // tt

    def body(oa_ref, ob_ref, b_ref, o_ref, acc_sc):
        k = pl.program_id(0)
        part = _dot_tn(jnp.concatenate([oa_ref[...], ob_ref[...]], axis=1), b_ref[...])

        @pl.when(k == 0)
        def _():
            acc_sc[...] = part

        @pl.when(k > 0)
        def _():
            acc_sc[...] += part

        @pl.when(k == nk - 1)
        def _():
            o_ref[...] = acc_sc[...].reshape(nj, m // nj, n).astype(BF16)

    return pl.pallas_call(
        body,
        grid=(nk,),
        in_specs=[pl.BlockSpec((tt, ca), lambda k: (k, 0)), pl.BlockSpec((tt, cb), lambda k: (k, 0)),
                  pl.BlockSpec((tt, n), lambda k: (k, 0))],
        out_specs=pl.BlockSpec((nj, m // nj, n), lambda k: (0, 0, 0)),
        out_shape=SDS((nj, m // nj, n), BF16),
        scratch_shapes=[pltpu.VMEM((m, n), F32)],
        compiler_params=_cp("arbitrary"),
        name="wgrad_o",
    )(oa, ob, dmix)


def _attn_out_bwd(dmix, wo, ca, tm):
    t, d = dmix.shape
    cb = wo.shape[0] - ca

    def body(dm_ref, w_ref, da_ref, db_ref):
        dm = dm_ref[...]
        da_ref[...] = _dot_nt(dm, w_ref[0:ca, :]).astype(BF16)
        db_ref[...] = _dot_nt(dm, w_ref[ca:, :]).astype(BF16)

    return pl.pallas_call(
        body,
        grid=(t // tm,),
        in_specs=[pl.BlockSpec((tm, d), lambda i: (i, 0)), pl.BlockSpec(wo.shape, lambda i: (0, 0))],
        out_specs=[pl.BlockSpec((tm, ca), lambda i: (i, 0)), pl.BlockSpec((tm, cb), lambda i: (i, 0))],
        out_shape=[SDS((t, ca), BF16), SDS((t, cb), BF16)],
        compiler_params=_cp("parallel"),
        name="attn_out_bwd",
    )(dmix, wo)


def _stack_heads(ref):
    return jnp.concatenate([ref[:, h * HEAD_DIM:(h + 1) * HEAD_DIM] for h in range(GROUP)], axis=0)


def _attn_a_bwd(qa, ka, kat, va, do, o, lse, tq, tk, grads):
    bl, kv, s_len, _ = ka.shape
    nq, nk = s_len // tq, s_len // tk
    assert nk % 2 == 0
    r = GROUP * tq
    ng = len(grads)

    def body(q_ref, qn_ref, k_ref, kt_ref, v_ref, do_ref, don_ref, o_ref, l_ref, *rest):
        grad_refs, (dq_ref, dk_ref, dv_ref), parts = rest[:ng], rest[ng:ng + 3], rest[ng + 3:2 * ng + 3]
        st_sc, dp_sc, dkt_sc, dvt_sc, send_sems, recv_sems, local_sems = rest[2 * ng + 3:]
        i = pl.program_id(2)
        step_id = (pl.program_id(0) * kv + pl.program_id(1)) * nq + i
        start, wait = _direct_exchange("scatter", grad_refs, parts, send_sems, recv_sems, local_sems)
        pl.when(step_id == 0)(start)

        q = _stack_heads(q_ref)
        do2 = _stack_heads(do_ref)
        qt = q.astype(F32).T
        dot32 = do2.astype(F32).T
        ot32 = _stack_heads(o_ref).astype(F32).T
        drow = jnp.sum(dot32 * ot32, axis=0, keepdims=True)
        qt, dot = qt.astype(BF16), dot32.astype(BF16)
        lrow = l_ref[0, 0, 0, 0:1, :]

        @pl.when(i == 0)
        def _():
            dkt_sc[...] = jnp.zeros_like(dkt_sc)
            dvt_sc[...] = jnp.zeros_like(dvt_sc)

        def chunk(c):
            return pl.ds(pl.multiple_of(c * tk, tk), tk)

        def scores(c, slot, qv=q, dov=do2):
            st_sc[slot] = _dot_nt(k_ref[0, 0, chunk(c), :], qv)
            dp_sc[slot] = _dot_nt(v_ref[0, 0, chunk(c), :], dov)

        def fold(slot, c, dqt):
            pt = jnp.exp(st_sc[slot] - lrow)
            dsb = (pt * (dp_sc[slot] - drow)).astype(BF16)
            dvt_sc[:, chunk(c)] += _dot_nt(dot, pt.astype(BF16))
            dkt_sc[:, chunk(c)] += _dot_nt(qt, dsb)
            return dqt + _dot(kt_ref[0, 0, :, chunk(c)], dsb)

        @pl.when(i == 0)
        def _():
            scores(0, 0)

        def step(c2, dqt):
            c = 2 * c2
            scores(c + 1, 1)
            dqt = fold(0, c, dqt)
            scores(c + 2, 0)
            return fold(1, c + 1, dqt)

        dqt = jnp.zeros((HEAD_DIM, r), F32)
        for c2 in range(nk // 2 - 1):
            dqt = step(c2, dqt)
        scores(nk - 1, 1)
        dqt = fold(0, nk - 2, dqt)
        scores(0, 0, _stack_heads(qn_ref), _stack_heads(don_ref))
        dq = fold(1, nk - 1, dqt).T
        for h in range(GROUP):
            dq_ref[:, h * HEAD_DIM:(h + 1) * HEAD_DIM] = dq[h * tq:(h + 1) * tq]

        @pl.when(i == nq - 1)
        def _():
            dk_ref[0, 0] = dkt_sc[...].T
            dv_ref[0, 0] = dvt_sc[...].T

        pl.when(step_id == bl * kv * nq - 1)(wait)

    kvspec = pl.BlockSpec((1, 1, s_len, HEAD_DIM), lambda b, g, i: (b, g, 0, 0))
    tok = pl.BlockSpec((tq, GROUP * HEAD_DIM), lambda b, g, i: (b * nq + i, g))
    toknext = pl.BlockSpec((tq, GROUP * HEAD_DIM), lambda b, g, i: (b * nq + jnp.minimum(i + 1, nq - 1), g))
    anyspec = pl.BlockSpec(memory_space=pl.ANY)
    res = pl.pallas_call(
        body,
        grid=(bl, kv, nq),
        in_specs=[tok, toknext, kvspec, pl.BlockSpec((1, 1, HEAD_DIM, s_len), lambda b, g, i: (b, g, 0, 0)), kvspec,
                  tok, toknext, tok, pl.BlockSpec((1, 1, 1, 8, r), lambda b, g, i: (b, g, i, 0, 0))] + [anyspec] * ng,
        out_specs=[tok, kvspec, kvspec] + [anyspec] * ng,
        out_shape=[SDS(qa.shape, F32), SDS(ka.shape, F32), SDS(va.shape, F32)]
        + [SDS(g.shape, g.dtype) for g in grads],
        scratch_shapes=[pltpu.VMEM((2, tk, r), F32), pltpu.VMEM((2, tk, r), F32),
                        pltpu.VMEM((HEAD_DIM, s_len), F32), pltpu.VMEM((HEAD_DIM, s_len), F32)]
        + _exchange_scratch(ng),
        compiler_params=_cp("arbitrary", "arbitrary", "arbitrary"),
        name="attn_a_bwd",
    )(qa, qa, ka, kat, va, do, do, o, lse, *grads)
    return res[0], res[1], res[2], res[3:]


def _attn_b_bwd(qb, kb, kbt, vb, do, o, lse, bias_t, sink, s_len):
    bl, kv, sp, _ = kb.shape
    nb = s_len // BLOCK
    nbs = min(QB_PER_STEP, nb)
    r = GROUP * BLOCK

    def body(q_ref, k_ref, kt_ref, v_ref, do_ref, o_ref, l_ref, bt_ref, sink_ref,
             dq_ref, dk_ref, dv_ref, dsum_ref, dsink_ref, dkt_sc, dvt_sc):
        g, b, ns = pl.program_id(0), pl.program_id(1), pl.program_id(2)
        sink_row = _sink_row(sink_ref, g)

        @pl.when(ns == 0)
        def _():
            dkt_sc[...] = jnp.zeros_like(dkt_sc)
            dvt_sc[...] = jnp.zeros_like(dvt_sc)

        @pl.when((b == 0) & (ns == 0))
        def _():
            dsum_ref[...] = jnp.zeros_like(dsum_ref)
            dsink_ref[...] = jnp.zeros_like(dsink_ref)

        dsum = jnp.zeros((SPAN, r), F32)
        dsink = jnp.zeros((1, r), F32)
        for j in range(nbs):
            n = ns * nbs + j
            span = pl.ds(pl.multiple_of(n * BLOCK, BLOCK), SPAN)
            rows = slice(j * BLOCK, (j + 1) * BLOCK)
            q = _stack_heads(q_ref.at[rows, :])
            do2 = _stack_heads(do_ref.at[rows, :])
            o2 = _stack_heads(o_ref.at[rows, :])
            dot32 = do2.astype(F32).T
            drow = jnp.sum(dot32 * o2.astype(F32).T, axis=0, keepdims=True)
            qt, dot = q.astype(F32).T.astype(BF16), dot32.astype(BF16)
            lrow = l_ref[0, 0, j, 0:1, :]
            st = _dot_nt(k_ref[0, 0, span, :], q) + bt_ref[_bias_variant(n, nb), 0]
            pt = jnp.exp(st - lrow)
            dst = pt * (_dot_nt(v_ref[0, 0, span, :], do2) - drow)
            dsum = dsum + dst
            dsink = dsink - jnp.exp(sink_row - lrow) * drow
            dsb = dst.astype(BF16)
            dvt_sc[:, span] += _dot_nt(dot, pt.astype(BF16))
            dkt_sc[:, span] += _dot_nt(qt, dsb)
            dq = _dot(kt_ref[0, 0, :, span], dsb).T
            for h in range(GROUP):
                dq_ref[rows, h * HEAD_DIM:(h + 1) * HEAD_DIM] = dq[h * BLOCK:(h + 1) * BLOCK]
        dsum_ref[0] += dsum
        dsink_ref[0, 0:1, :] += dsink

        @pl.when(ns == nb // nbs - 1)
        def _():
            dk_ref[0, 0] = dkt_sc[:, BLOCK:BLOCK + s_len].T
            dv_ref[0, 0] = dvt_sc[:, BLOCK:BLOCK + s_len].T

    kvspec = pl.BlockSpec((1, 1, sp, HEAD_DIM), lambda g, b, n: (b, g, 0, 0))
    kvout = pl.BlockSpec((1, 1, s_len, HEAD_DIM), lambda g, b, n: (b, g, 0, 0))
    tok = pl.BlockSpec((nbs * BLOCK, GROUP * HEAD_DIM), lambda g, b, n: (b * (nb // nbs) + n, g))
    return pl.pallas_call(
        body,
        grid=(kv, bl, nb // nbs),
        in_specs=[tok, kvspec, pl.BlockSpec((1, 1, HEAD_DIM, sp), lambda g, b, n: (b, g, 0, 0)), kvspec, tok, tok,
                  pl.BlockSpec((1, 1, nbs, 8, r), lambda g, b, n: (b, g, n, 0, 0)),
                  pl.BlockSpec((3, 1, SPAN, r), lambda g, b, n: (0, g, 0, 0)),
                  pl.BlockSpec(memory_space=pltpu.SMEM)],
        out_specs=[tok, kvout, kvout,
                   pl.BlockSpec((1, SPAN, r), lambda g, b, n: (g, 0, 0)),
                   pl.BlockSpec((1, 8, r), lambda g, b, n: (g, 0, 0))],
        out_shape=[SDS(qb.shape, F32), SDS((bl, kv, s_len, HEAD_DIM), F32), SDS((bl, kv, s_len, HEAD_DIM), F32),
                   SDS((kv, SPAN, r), F32), SDS((kv, 8, r), F32)],
        scratch_shapes=[pltpu.VMEM((HEAD_DIM, sp), F32), pltpu.VMEM((HEAD_DIM, sp), F32)],
        compiler_params=_cp("arbitrary", "arbitrary", "arbitrary"),
        name="attn_b_bwd",
    )(qb, kb, kbt, vb, do, o, lse, bias_t, sink)


def _bias_reduce(dsum, dsink, bucket_t4):
    kv, _, r = dsum.shape

    def body(ds_ref, dk_ref, bk_ref, rel_ref, sink_ref):
        lane = lax.broadcasted_iota(jnp.int32, (N_BUCKETS, 128), 1)
        lane8 = lax.broadcasted_iota(jnp.int32, (8, 128), 1)
        bk = bk_ref[...]
        for g in range(kv):
            ds = ds_ref[g]
            rowi = lax.broadcasted_iota(jnp.int32, (N_BUCKETS, r), 0)
            red = jnp.zeros((N_BUCKETS, r), F32)
            for b in range(N_BUCKETS):
                red = jnp.where(rowi == b, jnp.sum(jnp.where(bk == b, ds, 0.0), axis=0, keepdims=True), red)
            out = jnp.zeros((N_BUCKETS, 128), F32)
            so = jnp.zeros((8, 128), F32)
            for h in range(GROUP):
                col = jnp.sum(red[:, h * BLOCK:(h + 1) * BLOCK], axis=1, keepdims=True)
                out = jnp.where(lane == h, col, out)
                sc = jnp.sum(dk_ref[g][:, h * BLOCK:(h + 1) * BLOCK], axis=1, keepdims=True)
                so = jnp.where(lane8 == h, sc, so)
            rel_ref[g] = out
            sink_ref[g] = so

    vm = pl.BlockSpec(memory_space=pltpu.VMEM)
    return pl.pallas_call(
        body,
        in_specs=[vm, vm, vm],
        out_specs=[vm, vm],
        out_shape=[SDS((kv, N_BUCKETS, 128), F32), SDS((kv, 8, 128), F32)],
        name="bias_reduce",
    )(dsum, dsink, bucket_t4)


def _dqkprep(dqa, dka, dva, dqb, dkb, dvb, proj, cos, sin_signed, gq, gk, s_len, ts):
    t, p_cols = proj.shape
    bl, kva, kvb = dka.shape[0], dka.shape[1], dkb.shape[1]
    ha, hb = dqa.shape[1] // HEAD_DIM, dqb.shape[1] // HEAD_DIM
    ns = s_len // ts

    def body(dqa_ref, dka_ref, dva_ref, dqb_ref, dkb_ref, dvb_ref, p_ref, cos_ref, sin_ref, gq_ref, gk_ref,
             dp_ref, dgq_ref, dgk_ref):
        b, i = pl.program_id(0), pl.program_id(1)
        cs, sn = cos_ref[...], sin_ref[...]
        low, first = _pair_masks(ts)

        @pl.when((b == 0) & (i == 0))
        def _():
            dgq_ref[...] = jnp.zeros_like(dgq_ref)
            dgk_ref[...] = jnp.zeros_like(dgk_ref)

        def grad_pair(ref, p):
            return jnp.concatenate([ref[0, 2 * p], ref[0, 2 * p + 1]], axis=1)

        def put(p, val):
            dp_ref[:, p * PAIR:(p + 1) * PAIR] = val.astype(BF16)

        def unrope_norm(d_rot, p, g, dg_ref):
            dn = d_rot * cs + _pair_partner(d_rot * sn, first)
            xp = p_ref[:, p * PAIR:(p + 1) * PAIR]
            r = lax.rsqrt(_pair_mean(xp * xp, low) + EPS)
            n = xp * r
            gd = g * dn
            dg_ref[0:1, :] += jnp.sum(dn * n, axis=0, keepdims=True)
            put(p, r * (gd - n * _pair_mean(n * gd, low)))

        for p in range(ha // 2):
            unrope_norm(dqa_ref[:, p * PAIR:(p + 1) * PAIR] * SCALE, p, gq_ref[...], dgq_ref)
        base = ha // 2
        for p in range(kva // 2):
            unrope_norm(grad_pair(dka_ref, p), base + p, gk_ref[...], dgk_ref)
            put(base + kva // 2 + p, grad_pair(dva_ref, p))
        base += kva
        for p in range(hb // 2):
            put(base + p, dqb_ref[:, p * PAIR:(p + 1) * PAIR] * SCALE)
        base += hb // 2
        for p in range(kvb // 2):
            put(base + p, grad_pair(dkb_ref, p))
            put(base + kvb // 2 + p, grad_pair(dvb_ref, p))

    def hm(nh):
        return pl.BlockSpec((1, nh, ts, HEAD_DIM), lambda b, i: (b, 0, i, 0))

    def tokmajor(nh):
        return pl.BlockSpec((ts, nh * HEAD_DIM), lambda b, i: (b * ns + i, 0))

    vec = pl.BlockSpec((1, PAIR), lambda b, i: (0, 0))
    tab = pl.BlockSpec((ts, PAIR), lambda b, i: (i, 0))
    acc = pl.BlockSpec((8, PAIR), lambda b, i: (0, 0))
    pspec = pl.BlockSpec((ts, p_cols), lambda b, i: (b * ns + i, 0))
    return pl.pallas_call(
        body,
        grid=(bl, ns),
        in_specs=[tokmajor(ha), hm(kva), hm(kva), tokmajor(hb), hm(kvb), hm(kvb), pspec, tab, tab, vec, vec],
        out_specs=[pspec, acc, acc],
        out_shape=[SDS((t, p_cols), BF16), SDS((8, PAIR), F32), SDS((8, PAIR), F32)],
        compiler_params=_cp("arbitrary", "arbitrary"),
        name="dqkprep",
    )(dqa, dka, dva, dqb, dkb, dvb, proj, cos, sin_signed, gq, gk)


def _dx_final(dproj, w_t, x2, dx1, g1, tm, grads):
    t, d = x2.shape
    p_cols = w_t.shape[0]
    ng = len(grads)
    nsteps = t // tm

    def body(dp_ref, w_ref, x_ref, dx1_ref, g_ref, *rest):
        grad_refs, (dx_ref, dg_ref), parts = rest[:ng], rest[ng:ng + 2], rest[ng + 2:2 * ng + 2]
        start, wait = _direct_exchange("scatter", grad_refs, parts, *rest[2 * ng + 2:])

        @pl.when(pl.program_id(0) == 0)
        def _():
            start()
            dg_ref[...] = jnp.zeros_like(dg_ref)

        dh = _dot(dp_ref[...], w_ref[...])
        g = g_ref[...]
        _, n, r = _rms_fwd(x_ref[...], g)
        dx, dgt = _rms_bwd(n, r, g, dh)
        dx_ref[...] = dx1_ref[...] + dx
        dg_ref[0:1, :] += jnp.sum(dgt, axis=0, keepdims=True)
        pl.when(pl.program_id(0) == nsteps - 1)(wait)

    tile = pl.BlockSpec((tm, d), lambda i: (i, 0))
    anyspec = pl.BlockSpec(memory_space=pl.ANY)
    res = pl.pallas_call(
        body,
        grid=(nsteps,),
        in_specs=[pl.BlockSpec((tm, p_cols), lambda i: (i, 0)),
                  pl.BlockSpec((p_cols, d), lambda i: (0, 0)),
                  tile, tile, pl.BlockSpec((1, d), lambda i: (0, 0))] + [anyspec] * ng,
        out_specs=[tile, pl.BlockSpec((8, d), lambda i: (0, 0))] + [anyspec] * ng,
        out_shape=[SDS((t, d), F32), SDS((8, d), F32)] + [SDS(g.shape, g.dtype) for g in grads],
        scratch_shapes=_exchange_scratch(ng),
        compiler_params=_cp("arbitrary"),
        name="dx_final",
    )(dproj, w_t, x2, dx1, g1, *grads)
    return res[0], res[1], res[2:]


def _adamw_math(w, g, m, v):
    m = ADAM_B1 * m + (1.0 - ADAM_B1) * g
    v = ADAM_B2 * v + (1.0 - ADAM_B2) * (g * g)
    m_hat = m / (1.0 - ADAM_B1 ** ADAM_STEP)
    v_hat = v / (1.0 - ADAM_B2 ** ADAM_STEP)
    delta = -ADAM_LR * (m_hat / (jnp.sqrt(v_hat) + ADAM_EPS) + ADAM_WD * w)
    return delta, m, v


def _adamw_sum(parts, w, m, v, tr, name):
    rows, cols = w.shape

    def body(p_ref, w_ref, m_ref, v_ref, g_ref, d_ref, nm_ref, nv_ref):
        g = p_ref[0].astype(F32)
        for s in range(1, N_DEV):
            g = g + p_ref[s].astype(F32)
        g_ref[...] = g
        d_ref[...], nm_ref[...], nv_ref[...] = _adamw_math(w_ref[...], g, m_ref[...], v_ref[...])

    tr = min(tr, rows)
    tile = pl.BlockSpec((tr, cols), lambda i: (i, 0))
    return pl.pallas_call(
        body,
        grid=(rows // tr,),
        in_specs=[pl.BlockSpec((N_DEV, tr, cols), lambda i: (0, i, 0)), tile, tile, tile],
        out_specs=[tile] * 4,
        out_shape=[SDS((rows, cols), F32)] * 4,
        compiler_params=_cp("parallel"),
        name=name,
    )(parts, w, m, v)


def _adamw_small(vec, rel, ws, ms, vs):
    hb = ws[6].shape[1]
    n = len(ws)

    def body(vec_ref, rel_ref, *rest):
        w_refs, m_refs, v_refs = rest[:n], rest[n:2 * n], rest[2 * n:3 * n]
        loss_ref, outs = rest[3 * n], rest[3 * n + 1:]
        grads = [vec_ref[0:1, :], vec_ref[1:2, :], vec_ref[2:3, :], vec_ref[3:4, :],
                 vec_ref[4:5, 0:HEAD_DIM], vec_ref[4:5, SMALL_LANES:SMALL_LANES + HEAD_DIM],
                 vec_ref[4:5, 2 * SMALL_LANES:2 * SMALL_LANES + hb], rel_ref[:, 0:hb]]
        loss_ref[...] = vec_ref[4:5, 3 * SMALL_LANES:3 * SMALL_LANES + 1]
        for p, g in enumerate(grads):
            g_ref, d_ref, nm_ref, nv_ref = outs[4 * p:4 * p + 4]
            g_ref[...] = g
            d_ref[...], nm_ref[...], nv_ref[...] = _adamw_math(w_refs[p][...], g, m_refs[p][...], v_refs[p][...])

    vm = pl.BlockSpec(memory_space=pltpu.VMEM)
    res = pl.pallas_call(
        body,
        in_specs=[vm] * (2 + 3 * n),
        out_specs=[vm] * (1 + 4 * n),
        out_shape=[SDS((1, 1), F32)] + [SDS(w.shape, F32) for w in ws for _ in range(4)],
        name="adamw_small",
    )(vec, rel, *ws, *ms, *vs)
    return res[0], [res[1 + 4 * p:5 + 4 * p] for p in range(n)]


def _local_step(x, loss_target, win_g, wo_s, wup_s, wdn_s, g_pre_mix, g_post_mix, q_norm_a, k_norm_a, sink_b,
                rel_bias, g_pre_ffn, g_post_ffn):
    bl, s_len, d = x.shape
    t = bl * s_len
    nh = d // HEAD_DIM
    ha = nh // 2
    kva = ha // GROUP
    hb = nh - ha
    kvb = hb // GROUP
    tm = 512
    tw = min(4096, t)
    ts = min(512, s_len)
    tq, tk = 2 * BLOCK, min(512, s_len // 2)

    x2 = x.reshape(t, d)
    tg2 = loss_target.reshape(t, d)
    cos, sin_signed = _rope_tables(s_len)
    gq2, gk2 = jnp.tile(q_norm_a, (1, 2)), jnp.tile(k_norm_a, (1, 2))
    a = jnp.arange(BLOCK, dtype=jnp.int32)
    c = jnp.arange(SPAN, dtype=jnp.int32)
    bucket_t = _t5_bucket(c[:, None] - BLOCK - a[None, :])
    bucket_t4 = jnp.tile(bucket_t, (1, GROUP))
    w_in_t = win_g.reshape(-1, d)
    p_cols = w_in_t.shape[0]

    h1, proj = _inproj(x2, g_pre_mix, w_in_t, tm)
    qa, ka, kat, va, vat, qb, kb, kbt, vb, vbt = _qkprep(
        proj, cos, sin_signed, gq2, gk2, bl, s_len, ha, kva, hb, kvb, ts)
    bias_t = _bias_build(bucket_t, rel_bias, hb)
    oa, lse_a, (wo_g, wup_g, wdn_g) = _attn_a_fwd(qa, ka, vat, tq, tk, [wo_s, wup_s, wdn_s])
    wo = wo_g.reshape(-1, d)
    wdn = wdn_g.reshape(-1, d)
    ob, lse_b = _attn_b_fwd(qb, kb, vbt, bias_t, sink_b, s_len)
    mix, x1, h2 = _mixout(oa, ob, wo, x2, g_post_mix, g_pre_ffn, tm)
    u, df, dy, dg4, loss8 = _ffn_fwd(h2, wup_g, wdn, x1, tg2, g_post_ffn, tm, FFN_BLOCKS_PER_STEP)

    dpre, dx1, dmix, dg3, dg2 = _ffn_bwd(df, u, wdn, wup_g, x1, dy, mix, g_pre_ffn, g_post_mix, FFN_BWD_TOKENS,
                                         FFN_BLOCKS_PER_STEP)
    gw_dn = _wgrad_rows(u, df, N_DEV, tw, "wgrad_down", square=True)
    gw_up = _wgrad_cols(h2, dpre, N_DEV, tw, "wgrad_up")
    gw_o = _wgrad_o(oa, ob, dmix, N_DEV, min(2048, t))
    doa, dob = _attn_out_bwd(dmix, wo, oa.shape[1], tm)
    dqa, dka, dva, (p_o, p_up, p_dn) = _attn_a_bwd(qa, ka, kat, va, doa, oa, lse_a, tq, tk, [gw_o, gw_up, gw_dn])
    dqb, dkb, dvb, dsum, dsink = _attn_b_bwd(qb, kb, kbt, vb, dob, ob, lse_b, bias_t, sink_b, s_len)
    drel_g, dsink_g = _bias_reduce(dsum, dsink, bucket_t4)
    dproj, dgq, dgk = _dqkprep(dqa, dka, dva, dqb, dkb, dvb, proj, cos, sin_signed, gq2, gk2, s_len, ts)
    gw_in_t = _wgrad_rows(dproj, h1, p_cols // 256, tw, "wgrad_in").reshape(N_DEV, -1, d)
    grad_x, dg1, (p_in,) = _dx_final(dproj, w_in_t, x2, dx1, g_pre_mix, tm, [gw_in_t])

    vec, rel = _small_allreduce([dg1, dg2, dg3, dg4], dgq, dgk, dsink_g, drel_g, loss8)
    return grad_x.reshape(bl, s_len, d), p_in, p_o, p_up, p_dn, vec, rel


def kernel(x, w_in, w_o, g_pre_mix, g_post_mix, q_norm_a, k_norm_a, sink_b, rel_bias, g_pre_ffn, w_ffn_up, w_ffn_down, g_post_ffn, loss_target, m_w_in, m_w_o, m_g_pre_mix, m_g_post_mix, m_q_norm_a, m_k_norm_a, m_sink_b, m_rel_bias, m_g_pre_ffn, m_w_ffn_up, m_w_ffn_down, m_g_post_ffn, v_w_in, v_w_o, v_g_pre_mix, v_g_post_mix, v_q_norm_a, v_k_norm_a, v_sink_b, v_rel_bias, v_g_pre_ffn, v_w_ffn_up, v_w_ffn_down, v_g_post_ffn):
    w_in_t = w_in[0].T
    (win_g,) = _weight_gather([w_in_t.astype(BF16)])

    grad_x, p_in, p_o, p_up, p_dn, vec, rel = _local_step(
        x, loss_target, win_g, w_o[0].astype(BF16), w_ffn_up[0].astype(BF16), w_ffn_down[0].astype(BF16),
        g_pre_mix, g_post_mix, q_norm_a, k_norm_a, sink_b, rel_bias, g_pre_ffn, g_post_ffn)

    big = {
        "w_in": [a.T for a in _adamw_sum(p_in, w_in_t, m_w_in[0].T, v_w_in[0].T, 192, "adamw_in")],
        "w_o": _adamw_sum(p_o, w_o[0], m_w_o[0], v_w_o[0], 128, "adamw_o"),
        "w_up": _adamw_sum(p_up, w_ffn_up[0], m_w_ffn_up[0], v_w_ffn_up[0], 256, "adamw_up"),
        "w_dn": _adamw_sum(p_dn, w_ffn_down[0], m_w_ffn_down[0], v_w_ffn_down[0], 256, "adamw_down"),
    }
    loss, small = _adamw_small(
        vec, rel,
        [g_pre_mix, g_post_mix, g_pre_ffn, g_post_ffn, q_norm_a, k_norm_a, sink_b, rel_bias],
        [m_g_pre_mix, m_g_post_mix, m_g_pre_ffn, m_g_post_ffn, m_q_norm_a, m_k_norm_a, m_sink_b, m_rel_bias],
        [v_g_pre_mix, v_g_post_mix, v_g_pre_ffn, v_g_post_ffn, v_q_norm_a, v_k_norm_a, v_sink_b, v_rel_bias])
    s_pre_mix, s_post_mix, s_pre_ffn, s_post_ffn, s_qn, s_kn, s_sink, s_rel = small

    def outs(kind):
        return [big["w_in"][kind][None], big["w_o"][kind][None], s_pre_mix[kind], s_post_mix[kind], s_qn[kind],
                s_kn[kind], s_sink[kind], s_rel[kind], s_pre_ffn[kind], big["w_up"][kind][None],
                big["w_dn"][kind][None], s_post_ffn[kind]]

    return (loss.reshape(()), grad_x, *outs(0), *outs(1), *outs(2), *outs(3))
```

```python
import jax
import jax.numpy as jnp
import numpy as np
from jax import lax
from jax.experimental import pallas as pl
from jax.experimental.pallas import tpu as pltpu

F32 = jnp.float32
BF16 = jnp.bfloat16
SDS = jax.ShapeDtypeStruct

N_DEV = 8
HEAD_DIM = 64
GROUP = 4
BLOCK = 128
SPAN = 3 * BLOCK
GRID_W = 64
N_BUCKETS = 32
MAX_DISTANCE = 128
ROPE_THETA = 10000.0
EPS = 1e-6
NEG_INF = -1e30
SCALE = HEAD_DIM ** -0.5
VT_PAD = 16

ADAM_LR = 0.001
ADAM_B1 = 0.9
ADAM_B2 = 0.999
ADAM_EPS = 1e-08
ADAM_WD = 0.01
ADAM_STEP = 10

VMEM_LIMIT = 56 * 1024 * 1024
MESH = pl.DeviceIdType.MESH


def _cp(*sem):
    return pltpu.CompilerParams(dimension_semantics=sem, vmem_limit_bytes=VMEM_LIMIT)


def _dot(a, b):
    return jnp.dot(a, b, preferred_element_type=F32)


def _dot_nt(a, b):
    return lax.dot_general(a, b, (((1,), (1,)), ((), ())), preferred_element_type=F32)


def _dot_tn(a, b):
    return lax.dot_general(a, b, (((0,), (0,)), ((), ())), preferred_element_type=F32)


def _rms_fwd(x, g):
    r = lax.rsqrt(jnp.mean(x * x, axis=-1, keepdims=True) + EPS)
    n = x * r
    return n * g, n, r


def _rms_bwd(n, r, g, dy):
    gd = g * dy
    dx = r * (gd - n * jnp.mean(n * gd, axis=-1, keepdims=True))
    return dx, dy * n


def _rope_tables(s_len):
    rows = s_len // GRID_W
    row = np.repeat(np.arange(rows, dtype=np.int32), GRID_W)
    col = np.tile(np.arange(GRID_W, dtype=np.int32), rows)
    nf = HEAD_DIM // 4
    freqs = np.float32(ROPE_THETA) ** (-np.arange(nf, dtype=np.float32) / np.float32(nf))
    ang_r = row.astype(np.float32)[:, None] * freqs[None, :]
    ang_c = col.astype(np.float32)[:, None] * freqs[None, :]
    cr, sr, cc, sc = np.cos(ang_r), np.sin(ang_r), np.cos(ang_c), np.sin(ang_c)
    cos = np.concatenate([cr, cr, cc, cc] * 2, axis=-1).astype(np.float32)
    sin_signed = np.concatenate([-sr, sr, -sc, sc] * 2, axis=-1).astype(np.float32)
    return jnp.asarray(cos), jnp.asarray(sin_signed)


def _t5_bucket(rel):
    nb = N_BUCKETS // 2
    ret = (rel > 0).astype(jnp.int32) * nb
    n = jnp.abs(rel)
    max_exact = nb // 2
    nf = jnp.maximum(n, 1).astype(F32)
    large = max_exact + (jnp.log(nf / max_exact) / np.float32(np.log(MAX_DISTANCE / max_exact))
                         * (nb - max_exact)).astype(jnp.int32)
    large = jnp.minimum(large, nb - 1)
    return ret + jnp.where(n < max_exact, n, large)


def _mesh_pos():
    return lax.axis_index("x"), lax.axis_index("y"), lax.axis_index("c")


def _lin(p):
    return 4 * p[0] + 2 * p[1] + p[2]


def _weight_gather(shards):
    n = len(shards)

    def body(*refs):
        xs, outs = refs[:n], refs[n:2 * n]
        send_sems, recv_sems, local_sems = refs[2 * n:]
        x, y, c = _mesh_pos()
        me, sibling = (x, y, c), (x, y, 1 - c)
        chips = [(1 - x, y), (x, 1 - y), (1 - x, 1 - y)]

        def copy(a, k, block, to, src=None):
            slot = outs[a].at[_lin(block)]
            return pltpu.make_async_remote_copy(
                src_ref=slot if src is None else src, dst_ref=slot,
                send_sem=send_sems.at[a, k], recv_sem=recv_sems.at[a, k],
                device_id=to, device_id_type=MESH)

        started = []
        for a in range(n):
            mine = pltpu.make_async_copy(xs[a], outs[a].at[_lin(me)], local_sems.at[a])
            mine.start()
            started.append(mine)
        sends = []
        for a in range(n):
            first = [copy(a, 0, me, sibling, src=xs[a])]
            first += [copy(a, 1 + j, me, (*chip, c), src=xs[a]) for j, chip in enumerate(chips)]
            for cp in first:
                cp.start()
            sends += first
        for a in range(n):
            for j, chip in enumerate(chips):
                copy(a, 1 + j, (*chip, c), me).wait_recv()
                fwd = copy(a, 4 + j, (*chip, c), sibling)
                fwd.start()
                sends.append(fwd)
        for a in range(n):
            copy(a, 0, sibling, me).wait_recv()
            for j, chip in enumerate(chips):
                copy(a, 4 + j, (*chip, 1 - c), me).wait_recv()
        for cp in sends:
            cp.wait_send()
        for mine in started:
            mine.wait()

    anyspec = pl.BlockSpec(memory_space=pl.ANY)
    return pl.pallas_call(
        body,
        out_shape=[SDS((N_DEV,) + s.shape, s.dtype) for s in shards],
        in_specs=[anyspec] * n,
        out_specs=[anyspec] * n,
        scratch_shapes=[pltpu.SemaphoreType.DMA((n, 7)), pltpu.SemaphoreType.DMA((n, 7)),
                        pltpu.SemaphoreType.DMA((n,))],
        name="weight_gather",
    )(*shards)


def _direct_exchange(kind, ins, outs, send_sems, recv_sems, local_sems):
    x, y, c = _mesh_pos()
    me = (x, y, c)
    peers = [(x, y, 1 - c), (1 - x, y, c), (x, 1 - y, c), (1 - x, 1 - y, c),
             (1 - x, y, 1 - c), (x, 1 - y, 1 - c), (1 - x, 1 - y, 1 - c)]

    def src(a, to):
        return ins[a] if kind == "gather" else ins[a].at[_lin(to)]

    def remote(a, k, to, frm):
        return pltpu.make_async_remote_copy(
            src_ref=src(a, to), dst_ref=outs[a].at[_lin(frm)],
            send_sem=send_sems.at[a, k], recv_sem=recv_sems.at[a, k],
            device_id=to, device_id_type=MESH)

    n = len(ins)
    sends = [remote(a, k, p, me) for a in range(n) for k, p in enumerate(peers)]
    arrivals = [remote(a, k, p, p) for a in range(n) for k, p in enumerate(peers)]
    local = [pltpu.make_async_copy(src(a, me), outs[a].at[_lin(me)], local_sems.at[a]) for a in range(n)]

    def start():
        for cp in local + sends:
            cp.start()

    def wait():
        for cp in arrivals:
            cp.wait_recv()
        for cp in sends:
            cp.wait_send()
        for cp in local:
            cp.wait()

    return start, wait


def _exchange_scratch(n):
    return [pltpu.SemaphoreType.DMA((n, 7)), pltpu.SemaphoreType.DMA((n, 7)), pltpu.SemaphoreType.DMA((n,))]


SMALL_LANES = 128


def _small_allreduce(dg_rows, dgq, dgk, dsink_g, drel_g, loss8):
    d = dg_rows[0].shape[1]
    kv = dsink_g.shape[0]

    def body(g1_ref, g2_ref, g3_ref, g4_ref, gq_ref, gk_ref, sk_ref, rl_ref, ls_ref, vec_ref, rel_ref,
             vbuf, rbuf, vland, rland, send_sems, recv_sems):
        x, y, c = _mesh_pos()
        me = (x, y, c)
        peers = [(x, y, 1 - c), (1 - x, y, c), (x, 1 - y, c), (1 - x, 1 - y, c),
                 (1 - x, y, 1 - c), (x, 1 - y, 1 - c), (1 - x, 1 - y, 1 - c)]
        vbuf[...] = jnp.zeros_like(vbuf)
        rbuf[...] = jnp.zeros_like(rbuf)
        for row, ref in enumerate((g1_ref, g2_ref, g3_ref, g4_ref)):
            vbuf[row:row + 1, :] = ref[0:1, :]
        vbuf[4:5, 0:HEAD_DIM] = gq_ref[0:1, 0:HEAD_DIM] + gq_ref[0:1, HEAD_DIM:PAIR]
        vbuf[4:5, SMALL_LANES:SMALL_LANES + HEAD_DIM] = gk_ref[0:1, 0:HEAD_DIM] + gk_ref[0:1, HEAD_DIM:PAIR]
        for g in range(kv):
            vbuf[4:5, 2 * SMALL_LANES + g * GROUP:2 * SMALL_LANES + (g + 1) * GROUP] = sk_ref[g, 0:1, 0:GROUP]
            rbuf[:, g * GROUP:(g + 1) * GROUP] = rl_ref[g, :, 0:GROUP]
        vbuf[4:5, 3 * SMALL_LANES:3 * SMALL_LANES + 1] = ls_ref[0:1, 0:1]

        def copies(k, to, frm):
            return [pltpu.make_async_remote_copy(
                src_ref=buf, dst_ref=land.at[_lin(frm)], send_sem=send_sems.at[a, k], recv_sem=recv_sems.at[a, k],
                device_id=to, device_id_type=MESH) for a, (buf, land) in enumerate(((vbuf, vland), (rbuf, rland)))]

        sends = [cp for k, p in enumerate(peers) for cp in copies(k, p, me)]
        for cp in sends:
            cp.start()
        vland[_lin(me)] = vbuf[...]
        rland[_lin(me)] = rbuf[...]
        for k, p in enumerate(peers):
            for cp in copies(k, p, p):
                cp.wait_recv()
        for cp in sends:
            cp.wait_send()
        vacc, racc = vland[0], rland[0]
        for s in range(1, N_DEV):
            vacc, racc = vacc + vland[s], racc + rland[s]
        vec_ref[...] = vacc
        rel_ref[...] = racc

    vm = pl.BlockSpec(memory_space=pltpu.VMEM)
    return pl.pallas_call(
        body,
        out_shape=[SDS((8, d), F32), SDS((N_BUCKETS, 128), F32)],
        in_specs=[vm] * 9,
        out_specs=[vm, vm],
        scratch_shapes=[pltpu.VMEM((8, d), F32), pltpu.VMEM((N_BUCKETS, 128), F32),
                        pltpu.VMEM((N_DEV, 8, d), F32), pltpu.VMEM((N_DEV, N_BUCKETS, 128), F32),
                        pltpu.SemaphoreType.DMA((2, 7)), pltpu.SemaphoreType.DMA((2, 7))],
        name="small_allreduce",
    )(*dg_rows, dgq, dgk, dsink_g, drel_g, loss8)


def _inproj(x2, g1, w_t, tm):
    t, d = x2.shape
    p = w_t.shape[0]

    def body(x_ref, g_ref, w_ref, h_ref, p_ref):
        y, _, _ = _rms_fwd(x_ref[...], g_ref[...])
        h = y.astype(BF16)
        h_ref[...] = h
        p_ref[...] = _dot_nt(h, w_ref[...])

    return pl.pallas_call(
        body,
        grid=(t // tm,),
        in_specs=[pl.BlockSpec((tm, d), lambda i: (i, 0)),
                  pl.BlockSpec((1, d), lambda i: (0, 0)),
                  pl.BlockSpec((p, d), lambda i: (0, 0))],
        out_specs=[pl.BlockSpec((tm, d), lambda i: (i, 0)),
                   pl.BlockSpec((tm, p), lambda i: (i, 0))],
        out_shape=[SDS((t, d), BF16), SDS((t, p), F32)],
        compiler_params=_cp("parallel"),
        name="inproj",
    )(x2, g1, w_t)


PAIR = 2 * HEAD_DIM


def _pair_masks(ts):
    lane = lax.broadcasted_iota(jnp.int32, (ts, PAIR), 1)
    return lane < HEAD_DIM, (lane % 32) < 16


def _pair_mean(v, low):
    del low
    r = lax.broadcasted_iota(jnp.int32, (PAIR, PAIR), 0) // HEAD_DIM
    c = lax.broadcasted_iota(jnp.int32, (PAIR, PAIR), 1) // HEAD_DIM
    same_head = (r == c).astype(BF16)
    hi = v.astype(BF16)
    lo = (v - hi.astype(F32)).astype(BF16)
    return (_dot(hi, same_head) + _dot(lo, same_head)) * (1.0 / HEAD_DIM)


def _pair_partner(v, first):
    return jnp.where(first, pltpu.roll(v, PAIR - 16, 1), pltpu.roll(v, 16, 1))


def _qkprep(proj, cos, sin_signed, gq, gk, bl, s_len, ha, kva, hb, kvb, ts):
    t, p_cols = proj.shape
    assert ha % 2 == 0 and kva % 2 == 0 and hb % 2 == 0 and kvb % 2 == 0
    ns = s_len // ts
    sp = s_len + 2 * BLOCK

    def body(p_ref, cos_ref, sin_ref, gq_ref, gk_ref, qa_ref, ka_ref, kat_ref, va_ref, vat_ref, qb_ref, kb_ref,
             kbt_ref, vb_ref, vbt_ref):
        i = pl.program_id(1)
        cs, sn = cos_ref[...], sin_ref[...]
        low, first = _pair_masks(ts)
        ones_row = (lax.broadcasted_iota(jnp.int32, (VT_PAD, ts), 0) == 0).astype(BF16)
        heads = (slice(0, HEAD_DIM), slice(HEAD_DIM, PAIR))

        def pair(p):
            return p_ref[:, p * PAIR:(p + 1) * PAIR]

        def normrope(x, g):
            y = x * lax.rsqrt(_pair_mean(x * x, low) + EPS) * g
            return y * cs + _pair_partner(y, first) * sn

        eye = (lax.broadcasted_iota(jnp.int32, (PAIR, PAIR), 0)
               == lax.broadcasted_iota(jnp.int32, (PAIR, PAIR), 1)).astype(BF16)

        def transposed(xb):
            return _dot_nt(eye, xb).astype(BF16)

        for p in range(ha // 2):
            qa_ref[:, p * PAIR:(p + 1) * PAIR] = (normrope(pair(p), gq_ref[...]) * SCALE).astype(BF16)
        base = ha // 2
        for p in range(kva // 2):
            k = normrope(pair(base + p), gk_ref[...]).astype(BF16)
            v = pair(base + kva // 2 + p).astype(BF16)
            kt, vt = transposed(k), transposed(v)
            for e, lanes in enumerate(heads):
                ka_ref[0, 2 * p + e] = k[:, lanes]
                va_ref[0, 2 * p + e] = v[:, lanes]
                kat_ref[0, 2 * p + e] = kt[lanes, :]
                vat_ref[0, 2 * p + e, 0:HEAD_DIM, :] = vt[lanes, :]
                vat_ref[0, 2 * p + e, HEAD_DIM:HEAD_DIM + VT_PAD, :] = ones_row
        base += kva
        for p in range(hb // 2):
            qb_ref[:, p * PAIR:(p + 1) * PAIR] = (pair(base + p) * SCALE).astype(BF16)
        base += hb // 2

        @pl.when(i == 0)
        def _():
            zeros = jnp.zeros((kvb, BLOCK, HEAD_DIM), BF16)
            zeros_t = jnp.zeros((kvb, HEAD_DIM + VT_PAD, BLOCK), BF16)
            for ref in (kb_ref, vb_ref):
                ref[0, :, 0:BLOCK, :] = zeros
                ref[0, :, sp - BLOCK:sp, :] = zeros
            kbt_ref[0, :, :, 0:BLOCK] = zeros_t[:, 0:HEAD_DIM]
            kbt_ref[0, :, :, sp - BLOCK:sp] = zeros_t[:, 0:HEAD_DIM]
            vbt_ref[0, :, :, 0:BLOCK] = zeros_t
            vbt_ref[0, :, :, sp - BLOCK:sp] = zeros_t

        rows = pl.ds(pl.multiple_of(BLOCK + i * ts, BLOCK), ts)
        for p in range(kvb // 2):
            k = pair(base + p).astype(BF16)
            v = pair(base + kvb // 2 + p).astype(BF16)
            kt, vt = transposed(k), transposed(v)
            for e, lanes in enumerate(heads):
                kb_ref[0, 2 * p + e, rows, :] = k[:, lanes]
                vb_ref[0, 2 * p + e, rows, :] = v[:, lanes]
                kbt_ref[0, 2 * p + e, :, rows] = kt[lanes, :]
                vbt_ref[0, 2 * p + e, 0:HEAD_DIM, rows] = vt[lanes, :]
                vbt_ref[0, 2 * p + e, HEAD_DIM:HEAD_DIM + VT_PAD, rows] = ones_row

    def hm(nh):
        return pl.BlockSpec((1, nh, ts, HEAD_DIM), lambda b, i: (b, 0, i, 0))

    def tokmajor(nh):
        return pl.BlockSpec((ts, nh * HEAD_DIM), lambda b, i: (b * ns + i, 0))

    def padded(nh):
        return pl.BlockSpec((1, nh, sp, HEAD_DIM), lambda b, i: (b, 0, 0, 0))

    def padded_t(nh, rows):
        return pl.BlockSpec((1, nh, rows, sp), lambda b, i: (b, 0, 0, 0))

    return pl.pallas_call(
        body,
        grid=(bl, ns),
        in_specs=[pl.BlockSpec((ts, p_cols), lambda b, i: (b * ns + i, 0)),
                  pl.BlockSpec((ts, PAIR), lambda b, i: (i, 0)),
                  pl.BlockSpec((ts, PAIR), lambda b, i: (i, 0)),
                  pl.BlockSpec((1, PAIR), lambda b, i: (0, 0)),
                  pl.BlockSpec((1, PAIR), lambda b, i: (0, 0))],
        out_specs=[tokmajor(ha), hm(kva), pl.BlockSpec((1, kva, HEAD_DIM, ts), lambda b, i: (b, 0, 0, i)), hm(kva),
                   pl.BlockSpec((1, kva, HEAD_DIM + VT_PAD, ts), lambda b, i: (b, 0, 0, i)),
                   tokmajor(hb), padded(kvb), padded_t(kvb, HEAD_DIM), padded(kvb),
                   padded_t(kvb, HEAD_DIM + VT_PAD)],
        out_shape=[SDS((t, ha * HEAD_DIM), BF16), SDS((bl, kva, s_len, HEAD_DIM), BF16),
                   SDS((bl, kva, HEAD_DIM, s_len), BF16),
                   SDS((bl, kva, s_len, HEAD_DIM), BF16), SDS((bl, kva, HEAD_DIM + VT_PAD, s_len), BF16),
                   SDS((t, hb * HEAD_DIM), BF16),
                   SDS((bl, kvb, sp, HEAD_DIM), BF16), SDS((bl, kvb, HEAD_DIM, sp), BF16),
                   SDS((bl, kvb, sp, HEAD_DIM), BF16), SDS((bl, kvb, HEAD_DIM + VT_PAD, sp), BF16)],
        compiler_params=_cp("parallel", "arbitrary"),
        name="qkprep",
    )(proj, cos, sin_signed, gq, gk)


def _bias_build(bucket_t, rel_bias, hb):
    kvb = hb // GROUP

    def body(bkt_ref, tbl_ref, out_ref):
        bkt = bkt_ref[...]
        ci = lax.broadcasted_iota(jnp.int32, (SPAN, BLOCK), 0)
        qi = lax.broadcasted_iota(jnp.int32, (SPAN, BLOCK), 1)
        band = jnp.abs(ci - BLOCK - qi) <= BLOCK
        masks = (band, band & (ci >= BLOCK), band & (ci < 2 * BLOCK))
        for h in range(hb):
            acct = jnp.zeros((SPAN, BLOCK), F32)
            for b in range(N_BUCKETS):
                acct = jnp.where(bkt == b, tbl_ref[b, h], acct)
            lanes = slice((h % GROUP) * BLOCK, (h % GROUP + 1) * BLOCK)
            for var, mask in enumerate(masks):
                out_ref[var, h // GROUP, :, lanes] = jnp.where(mask, acct, NEG_INF)

    vm = pl.BlockSpec(memory_space=pltpu.VMEM)
    return pl.pallas_call(
        body,
        in_specs=[vm, pl.BlockSpec(memory_space=pltpu.SMEM)],
        out_specs=vm,
        out_shape=SDS((3, kvb, SPAN, GROUP * BLOCK), F32),
        name="bias_build",
    )(bucket_t, rel_bias)


def _attn_a_fwd(qa, ka, vat, tq, tk, shards):
    bl, kv, s_len, _ = ka.shape
    ha = qa.shape[1] // HEAD_DIM
    va_rows = vat.shape[2]
    nq, nk = s_len // tq, s_len // tk
    assert nk % 2 == 0
    r = GROUP * tq
    ns = len(shards)

    def body(q_ref, qn_ref, k_ref, v_ref, *rest):
        shard_refs, (o_ref, l_ref), gathered = rest[:ns], rest[ns:ns + 2], rest[ns + 2:2 * ns + 2]
        st_sc, send_sems, recv_sems, local_sems = rest[2 * ns + 2:]
        i = pl.program_id(2)
        step_id = (pl.program_id(0) * kv + pl.program_id(1)) * nq + i
        start, wait = _direct_exchange("gather", shard_refs, gathered, send_sems, recv_sems, local_sems)
        pl.when(step_id == 0)(start)

        q = _heads_t(q_ref[...]).astype(BF16)

        def scores(c, qv):
            return _dot(k_ref[0, 0, pl.ds(pl.multiple_of(c * tk, tk), tk), :], qv)

        def fold(st, c, carry):
            m_old, acc = carry
            m_new = jnp.maximum(m_old, jnp.max(st, axis=0, keepdims=True))
            pt = jnp.exp(st - m_new).astype(BF16)
            vt = v_ref[0, 0, :, pl.ds(pl.multiple_of(c * tk, tk), tk)]
            return m_new, jnp.exp(m_old - m_new) * acc + _dot(vt, pt)

        @pl.when(i == 0)
        def _():
            st_sc[0] = scores(0, q)

        def step(c2, carry):
            c = 2 * c2
            st_sc[1] = scores(c + 1, q)
            carry = fold(st_sc[0], c, carry)
            st_sc[0] = scores(c + 2, q)
            return fold(st_sc[1], c + 1, carry)

        carry = (jnp.full((1, r), -jnp.inf, F32), jnp.zeros((va_rows, r), F32))
        for c2 in range(nk // 2 - 1):
            carry = step(c2, carry)
        st_sc[1] = scores(nk - 1, q)
        carry = fold(st_sc[0], nk - 2, carry)
        st_sc[0] = scores(0, _heads_t(qn_ref[...]).astype(BF16))
        m, acc = fold(st_sc[1], nk - 1, carry)
        l = acc[HEAD_DIM:HEAD_DIM + 1, :]
        o_ref[...] = _heads_t_inv(acc[0:HEAD_DIM, :] / l).astype(BF16)
        l_ref[0, 0, 0] = jnp.broadcast_to(m + jnp.log(l), (8, r))
        pl.when(step_id == bl * kv * nq - 1)(wait)

    anyspec = pl.BlockSpec(memory_space=pl.ANY)
    res = pl.pallas_call(
        body,
        grid=(bl, kv, nq),
        in_specs=[pl.BlockSpec((tq, GROUP * HEAD_DIM), lambda b, g, i: (b * nq + i, g)),
                  pl.BlockSpec((tq, GROUP * HEAD_DIM), lambda b, g, i: (b * nq + jnp.minimum(i + 1, nq - 1), g)),
                  pl.BlockSpec((1, 1, s_len, HEAD_DIM), lambda b, g, i: (b, g, 0, 0)),
                  pl.BlockSpec((1, 1, va_rows, s_len), lambda b, g, i: (b, g, 0, 0))] + [anyspec] * ns,
        out_specs=[pl.BlockSpec((tq, GROUP * HEAD_DIM), lambda b, g, i: (b * nq + i, g)),
                   pl.BlockSpec((1, 1, 1, 8, r), lambda b, g, i: (b, g, i, 0, 0))] + [anyspec] * ns,
        out_shape=[SDS((bl * s_len, ha * HEAD_DIM), BF16), SDS((bl, kv, nq, 8, r), F32)]
        + [SDS((N_DEV,) + s.shape, s.dtype) for s in shards],
        scratch_shapes=[pltpu.VMEM((2, tk, r), F32)] + _exchange_scratch(ns),
        compiler_params=_cp("arbitrary", "arbitrary", "arbitrary"),
        name="attn_a_fwd",
    )(qa, qa, ka, vat, *shards)
    return res[0], res[1], res[2:]


FFN_BLOCKS_PER_STEP = 8
FFN_BWD_TOKENS = 256
QB_PER_STEP = 16


def _bias_variant(n, nb):
    return jnp.where(n == 0, 1, jnp.where(n == nb - 1, 2, 0))


def _sink_row(sink_ref, g):
    return jnp.concatenate([jnp.full((1, BLOCK), sink_ref[0, g * GROUP + h], F32) for h in range(GROUP)], axis=1)


def _attn_b_fwd(qb, kb, vbt, bias_t, sink, s_len):
    bl, kv, sp, _ = kb.shape
    hb = qb.shape[1] // HEAD_DIM
    vt_rows = vbt.shape[2]
    nb = s_len // BLOCK
    nbs = min(QB_PER_STEP, nb)
    r = GROUP * BLOCK

    def body(q_ref, k_ref, vt_ref, bt_ref, sink_ref, o_ref, l_ref, st_sc, pb_sc):
        g, n0 = pl.program_id(1), pl.program_id(2) * nbs
        sink_row = _sink_row(sink_ref, g)

        def span(j):
            return pl.ds(pl.multiple_of((n0 + j) * BLOCK, BLOCK), SPAN)

        for j in range(nbs):
            qt = _heads_t(q_ref[j * BLOCK:(j + 1) * BLOCK, :]).astype(BF16)
            st_sc[j] = _dot(k_ref[0, 0, span(j), :], qt) + bt_ref[_bias_variant(n0 + j, nb), 0]
        maxes = []
        for j in range(nbs):
            st = st_sc[j]
            m = jnp.maximum(jnp.max(st, axis=0, keepdims=True), sink_row)
            pb_sc[j] = jnp.exp(st - m).astype(BF16)
            maxes.append(m)
        for j in range(nbs):
            m = maxes[j]
            acc = _dot(vt_ref[0, 0, :, span(j)], pb_sc[j])
            l = acc[HEAD_DIM:HEAD_DIM + 1, :] + jnp.exp(sink_row - m)
            o_ref[j * BLOCK:(j + 1) * BLOCK, :] = _heads_t_inv(acc[0:HEAD_DIM, :] / l).astype(BF16)
            l_ref[0, 0, j] = jnp.broadcast_to(m + jnp.log(l), (8, r))

    return pl.pallas_call(
        body,
        grid=(bl, kv, nb // nbs),
        in_specs=[pl.BlockSpec((nbs * BLOCK, GROUP * HEAD_DIM), lambda b, g, n: (b * (nb // nbs) + n, g)),
                  pl.BlockSpec((1, 1, sp, HEAD_DIM), lambda b, g, n: (b, g, 0, 0)),
                  pl.BlockSpec((1, 1, vt_rows, sp), lambda b, g, n: (b, g, 0, 0)),
                  pl.BlockSpec((3, 1, SPAN, r), lambda b, g, n: (0, g, 0, 0)),
                  pl.BlockSpec(memory_space=pltpu.SMEM)],
        out_specs=[pl.BlockSpec((nbs * BLOCK, GROUP * HEAD_DIM), lambda b, g, n: (b * (nb // nbs) + n, g)),
                   pl.BlockSpec((1, 1, nbs, 8, r), lambda b, g, n: (b, g, n, 0, 0))],
        out_shape=[SDS((bl * s_len, hb * HEAD_DIM), BF16), SDS((bl, kv, nb, 8, r), F32)],
        scratch_shapes=[pltpu.VMEM((nbs, SPAN, r), F32), pltpu.VMEM((nbs, SPAN, r), BF16)],
        compiler_params=_cp("parallel", "parallel", "arbitrary"),
        name="attn_b_fwd",
    )(qb, kb, vbt, bias_t, sink)


def _mixout(oa, ob, wo, x2, g2, g3, tm):
    t, d = x2.shape
    ca = oa.shape[1]

    def body(oa_ref, ob_ref, w_ref, x_ref, g2_ref, g3_ref, mix_ref, x1_ref, h2_ref):
        mix = _dot(oa_ref[...], w_ref[0:ca, :]) + _dot(ob_ref[...], w_ref[ca:, :])
        mix_ref[...] = mix
        y2, _, _ = _rms_fwd(mix, g2_ref[...])
        x1 = x_ref[...] + y2
        x1_ref[...] = x1
        y3, _, _ = _rms_fwd(x1, g3_ref[...])
        h2_ref[...] = y3.astype(BF16)

    tile = lambda w: pl.BlockSpec((tm, w), lambda i: (i, 0))
    vec = pl.BlockSpec((1, d), lambda i: (0, 0))
    return pl.pallas_call(
        body,
        grid=(t // tm,),
        in_specs=[tile(ca), tile(ob.shape[1]), pl.BlockSpec(wo.shape, lambda i: (0, 0)), tile(d), vec, vec],
        out_specs=[tile(d), tile(d), tile(d)],
        out_shape=[SDS((t, d), F32), SDS((t, d), F32), SDS((t, d), BF16)],
        compiler_params=_cp("parallel"),
        name="mixout",
    )(oa, ob, wo, x2, g2, g3)


def _ffn_fwd(h2, wup_g, wdn, x1, target, g4, tm, jb):
    t, d = x1.shape
    nblk, _, tf = wup_g.shape
    ff = nblk * tf
    nt = t // tm
    nj = nblk // jb

    def body(h_ref, wu_ref, wd_ref, x1_ref, tg_ref, g_ref, u_ref, df_ref, dy_ref, dg_ref, loss_ref, acc_sc):
        i, j = pl.program_id(0), pl.program_id(1)

        @pl.when(j == 0)
        def _():
            acc_sc[...] = jnp.zeros_like(acc_sc)

        @pl.when((i == 0) & (j == 0))
        def _():
            dg_ref[...] = jnp.zeros_like(dg_ref)
            loss_ref[...] = jnp.zeros_like(loss_ref)

        h = h_ref[...]
        squares = []
        for s in range(jb):
            u = jnp.maximum(_dot(h, wu_ref[s]), 0.0)
            u_ref[:, s * tf:(s + 1) * tf] = u.astype(BF16)
            squares.append((u * u).astype(BF16))
        acc_sc[...] += _dot(jnp.concatenate(squares, axis=1), wd_ref[...])

        @pl.when(j == nj - 1)
        def _():
            g = g_ref[...]
            y4, n, r = _rms_fwd(acc_sc[...], g)
            e = (x1_ref[...] + y4) - tg_ref[...]
            loss_ref[...] += jnp.sum(e * e) * (0.5 / d)
            dy = e * (1.0 / d)
            dy_ref[...] = dy
            df, dgt = _rms_bwd(n, r, g, dy)
            df_ref[...] = df.astype(BF16)
            dg_ref[0:1, :] += jnp.sum(dgt, axis=0, keepdims=True)

    tile = pl.BlockSpec((tm, d), lambda i, j: (i, 0))
    return pl.pallas_call(
        body,
        grid=(nt, nj),
        in_specs=[tile,
                  pl.BlockSpec((jb, d, tf), lambda i, j: (j, 0, 0)),
                  pl.BlockSpec((jb * tf, d), lambda i, j: (j, 0)),
                  tile, tile,
                  pl.BlockSpec((1, d), lambda i, j: (0, 0))],
        out_specs=[pl.BlockSpec((tm, jb * tf), lambda i, j: (i, j)), tile, tile,
                   pl.BlockSpec((8, d), lambda i, j: (0, 0)),
                   pl.BlockSpec((8, 128), lambda i, j: (0, 0))],
        out_shape=[SDS((t, ff), BF16), SDS((t, d), BF16), SDS((t, d), F32), SDS((8, d), F32), SDS((8, 128), F32)],
        scratch_shapes=[pltpu.VMEM((tm, d), F32)],
        compiler_params=_cp("arbitrary", "arbitrary"),
        name="ffn_fwd",
    )(h2, wup_g, wdn, x1, target, g4)


def _ffn_bwd(df, u, wdn, wup_g, x1, dy, mix, g3, g2, tm, jb):
    t, d = x1.shape
    nblk, _, tf = wup_g.shape
    nt = t // tm
    nj = nblk // jb

    def body(df_ref, u_ref, wd_ref, wu_ref, x1_ref, dy_ref, mix_ref, g3_ref, g2_ref,
             dpre_ref, dx1_ref, dmix_ref, dg3_ref, dg2_ref, acc_sc):
        i, j = pl.program_id(0), pl.program_id(1)

        @pl.when(j == 0)
        def _():
            acc_sc[...] = jnp.zeros_like(acc_sc)

        @pl.when((i == 0) & (j == 0))
        def _():
            dg3_ref[...] = jnp.zeros_like(dg3_ref)
            dg2_ref[...] = jnp.zeros_like(dg2_ref)

        du2 = _dot_nt(df_ref[...], wd_ref[...])
        dpre = (2.0 * u_ref[...].astype(F32) * du2).astype(BF16)
        dpre_ref[...] = dpre
        dh = _dot_nt(dpre[:, 0:tf], wu_ref[0])
        for s in range(1, jb):
            dh = dh + _dot_nt(dpre[:, s * tf:(s + 1) * tf], wu_ref[s])
        acc_sc[...] += dh

        @pl.when(j == nj - 1)
        def _():
            g3, g2 = g3_ref[...], g2_ref[...]
            _, n3, r3 = _rms_fwd(x1_ref[...], g3)
            dx, dgt3 = _rms_bwd(n3, r3, g3, acc_sc[...])
            dx1 = dy_ref[...] + dx
            dx1_ref[...] = dx1
            dg3_ref[0:1, :] += jnp.sum(dgt3, axis=0, keepdims=True)
            _, n2, r2 = _rms_fwd(mix_ref[...], g2)
            dmix, dgt2 = _rms_bwd(n2, r2, g2, dx1)
            dmix_ref[...] = dmix.astype(BF16)
            dg2_ref[0:1, :] += jnp.sum(dgt2, axis=0, keepdims=True)

    tile = pl.BlockSpec((tm, d), lambda i, j: (i, 0))
    vec = pl.BlockSpec((1, d), lambda i, j: (0, 0))
    acc8 = pl.BlockSpec((8, d), lambda i, j: (0, 0))
    return pl.pallas_call(
        body,
        grid=(nt, nj),
        in_specs=[tile,
                  pl.BlockSpec((tm, jb * tf), lambda i, j: (i, j)),
                  pl.BlockSpec((jb * tf, d), lambda i, j: (j, 0)),
                  pl.BlockSpec((jb, d, tf), lambda i, j: (j, 0, 0)),
                  tile, tile, tile, vec, vec],
        out_specs=[pl.BlockSpec((tm, jb * tf), lambda i, j: (i, j)), tile, tile, acc8, acc8],
        out_shape=[SDS(u.shape, BF16), SDS((t, d), F32), SDS((t, d), BF16), SDS((8, d), F32), SDS((8, d), F32)],
        scratch_shapes=[pltpu.VMEM((tm, d), F32)],
        compiler_params=_cp("arbitrary", "arbitrary"),
        name="ffn_bwd",
    )(df, u, wdn, wup_g, x1, dy, mix, g3, g2)


def _wgrad(a, b, a_spec, b_spec, out_block, out_shape, nj, nk, name, prep_a=None, prep_b=None):
    acc_shape = out_block[1:]

    def body(a_ref, b_ref, o_ref, acc_sc):
        k = pl.program_id(1)
        av = a_ref[...] if prep_a is None else prep_a(a_ref)
        bv = b_ref[...] if prep_b is None else prep_b(b_ref)
        part = _dot_tn(av, bv)

        @pl.when(k == 0)
        def _():
            acc_sc[...] = part

        @pl.when(k > 0)
        def _():
            acc_sc[...] += part

        @pl.when(k == nk - 1)
        def _():
            o_ref[0] = acc_sc[...].astype(BF16)

    return pl.pallas_call(
        body,
        grid=(nj, nk),
        in_specs=[a_spec, b_spec],
        out_specs=pl.BlockSpec(out_block, lambda j, k: (j, 0, 0)),
        out_shape=SDS(out_shape, BF16),
        scratch_shapes=[pltpu.VMEM(acc_shape, F32)],
        compiler_params=_cp("parallel", "arbitrary"),
        name=name,
    )(a, b)


def _wgrad_cols(a, b, nj, tt, name):
    t, m = a.shape
    bn = b.shape[1] // nj
    return _wgrad(a, b, pl.BlockSpec((tt, m), lambda j, k: (k, 0)), pl.BlockSpec((tt, bn), lambda j, k: (k, j)),
                  (1, m, bn), (nj, m, bn), nj, t // tt, name)


def _wgrad_rows(a, b, nj, tt, name, square=False):
    t, n = b.shape
    bm = a.shape[1] // nj

    def squared(a_ref):
        af = a_ref[...].astype(F32)
        return (af * af).astype(BF16)

    return _wgrad(a, b, pl.BlockSpec((tt, bm), lambda j, k: (k, j)), pl.BlockSpec((tt, n), lambda j, k: (k, 0)),
                  (1, bm, n), (nj, bm, n), nj, t // tt, name, prep_a=squared if square else None)


def _wgrad_o(oa, ob, dmix, nj, tt):
    t, n = dmix.shape
    ca, cb = oa.shape[1], ob.shape[1]
    m = ca + cb
    nk = t // tt

    def body(oa_ref, ob_ref, b_ref, o_ref, acc_sc):
        k = pl.program_id(0)
        part = _dot_tn(jnp.concatenate([oa_ref[...], ob_ref[...]], axis=1), b_ref[...])

        @pl.when(k == 0)
        def _():
            acc_sc[...] = part

        @pl.when(k > 0)
        def _():
            acc_sc[...] += part

        @pl.when(k == nk - 1)
        def _():
            o_ref[...] = acc_sc[...].reshape(nj, m // nj, n).astype(BF16)

    return pl.pallas_call(
        body,
        grid=(nk,),
        in_specs=[pl.BlockSpec((tt, ca), lambda k: (k, 0)), pl.BlockSpec((tt, cb), lambda k: (k, 0)),
                  pl.BlockSpec((tt, n), lambda k: (k, 0))],
        out_specs=pl.BlockSpec((nj, m // nj, n), lambda k: (0, 0, 0)),
        out_shape=SDS((nj, m // nj, n), BF16),
        scratch_shapes=[pltpu.VMEM((m, n), F32)],
        compiler_params=_cp("arbitrary"),
        name="wgrad_o",
    )(oa, ob, dmix)


def _attn_out_bwd(dmix, wo, ca, tm):
    t, d = dmix.shape
    cb = wo.shape[0] - ca

    def body(dm_ref, w_ref, da_ref, db_ref):
        dm = dm_ref[...]
        da_ref[...] = _dot_nt(dm, w_ref[0:ca, :]).astype(BF16)
        db_ref[...] = _dot_nt(dm, w_ref[ca:, :]).astype(BF16)

    return pl.pallas_call(
        body,
        grid=(t // tm,),
        in_specs=[pl.BlockSpec((tm, d), lambda i: (i, 0)), pl.BlockSpec(wo.shape, lambda i: (0, 0))],
        out_specs=[pl.BlockSpec((tm, ca), lambda i: (i, 0)), pl.BlockSpec((tm, cb), lambda i: (i, 0))],
        out_shape=[SDS((t, ca), BF16), SDS((t, cb), BF16)],
        compiler_params=_cp("parallel"),
        name="attn_out_bwd",
    )(dmix, wo)


def _heads_t(x):
    xt = x.astype(F32).T
    return jnp.concatenate([xt[h * HEAD_DIM:(h + 1) * HEAD_DIM, :] for h in range(GROUP)], axis=1)


def _heads_t_inv(yt):
    n = yt.shape[1] // GROUP
    return jnp.concatenate([yt[:, h * n:(h + 1) * n] for h in range(GROUP)], axis=0).T


def _attn_a_bwd(qa, ka, kat, va, do, o, lse, tq, tk, grads):
    bl, kv, s_len, _ = ka.shape
    nq, nk = s_len // tq, s_len // tk
    assert nk % 2 == 0
    r = GROUP * tq
    ng = len(grads)

    def body(q_ref, qn_ref, k_ref, kt_ref, v_ref, do_ref, don_ref, o_ref, l_ref, *rest):
        grad_refs, (dq_ref, dk_ref, dv_ref), parts = rest[:ng], rest[ng:ng + 3], rest[ng + 3:2 * ng + 3]
        st_sc, dp_sc, dkt_sc, dvt_sc, send_sems, recv_sems, local_sems = rest[2 * ng + 3:]
        i = pl.program_id(2)
        step_id = (pl.program_id(0) * kv + pl.program_id(1)) * nq + i
        start, wait = _direct_exchange("scatter", grad_refs, parts, send_sems, recv_sems, local_sems)
        pl.when(step_id == 0)(start)

        dot32 = _heads_t(do_ref[...])
        drow = jnp.sum(dot32 * _heads_t(o_ref[...]), axis=0, keepdims=True)
        qt, dot = _heads_t(q_ref[...]).astype(BF16), dot32.astype(BF16)
        lrow = l_ref[0, 0, 0, 0:1, :]

        @pl.when(i == 0)
        def _():
            dkt_sc[...] = jnp.zeros_like(dkt_sc)
            dvt_sc[...] = jnp.zeros_like(dvt_sc)

        def chunk(c):
            return pl.ds(pl.multiple_of(c * tk, tk), tk)

        def scores(c, slot, qv=qt, dov=dot):
            st_sc[slot] = _dot(k_ref[0, 0, chunk(c), :], qv)
            dp_sc[slot] = _dot(v_ref[0, 0, chunk(c), :], dov)

        def fold(slot, c, dqt):
            pt = jnp.exp(st_sc[slot] - lrow)
            dsb = (pt * (dp_sc[slot] - drow)).astype(BF16)
            dvt_sc[:, chunk(c)] += _dot_nt(dot, pt.astype(BF16))
            dkt_sc[:, chunk(c)] += _dot_nt(qt, dsb)
            return dqt + _dot(kt_ref[0, 0, :, chunk(c)], dsb)

        @pl.when(i == 0)
        def _():
            scores(0, 0)

        def step(c2, dqt):
            c = 2 * c2
            scores(c + 1, 1)
            dqt = fold(0, c, dqt)
            scores(c + 2, 0)
            return fold(1, c + 1, dqt)

        dqt = jnp.zeros((HEAD_DIM, r), F32)
        for c2 in range(nk // 2 - 1):
            dqt = step(c2, dqt)
        scores(nk - 1, 1)
        dqt = fold(0, nk - 2, dqt)
        scores(0, 0, _heads_t(qn_ref[...]).astype(BF16), _heads_t(don_ref[...]).astype(BF16))
        dq_ref[...] = _heads_t_inv(fold(1, nk - 1, dqt))

        @pl.when(i == nq - 1)
        def _():
            dk_ref[0, 0] = dkt_sc[...].T
            dv_ref[0, 0] = dvt_sc[...].T

        pl.when(step_id == bl * kv * nq - 1)(wait)

    kvspec = pl.BlockSpec((1, 1, s_len, HEAD_DIM), lambda b, g, i: (b, g, 0, 0))
    tok = pl.BlockSpec((tq, GROUP * HEAD_DIM), lambda b, g, i: (b * nq + i, g))
    toknext = pl.BlockSpec((tq, GROUP * HEAD_DIM), lambda b, g, i: (b * nq + jnp.minimum(i + 1, nq - 1), g))
    anyspec = pl.BlockSpec(memory_space=pl.ANY)
    res = pl.pallas_call(
        body,
        grid=(bl, kv, nq),
        in_specs=[tok, toknext, kvspec, pl.BlockSpec((1, 1, HEAD_DIM, s_len), lambda b, g, i: (b, g, 0, 0)), kvspec,
                  tok, toknext, tok, pl.BlockSpec((1, 1, 1, 8, r), lambda b, g, i: (b, g, i, 0, 0))] + [anyspec] * ng,
        out_specs=[tok, kvspec, kvspec] + [anyspec] * ng,
        out_shape=[SDS(qa.shape, F32), SDS(ka.shape, F32), SDS(va.shape, F32)]
        + [SDS(g.shape, g.dtype) for g in grads],
        scratch_shapes=[pltpu.VMEM((2, tk, r), F32), pltpu.VMEM((2, tk, r), F32),
                        pltpu.VMEM((HEAD_DIM, s_len), F32), pltpu.VMEM((HEAD_DIM, s_len), F32)]
        + _exchange_scratch(ng),
        compiler_params=_cp("arbitrary", "arbitrary", "arbitrary"),
        name="attn_a_bwd",
    )(qa, qa, ka, kat, va, do, do, o, lse, *grads)
    return res[0], res[1], res[2], res[3:]


def _attn_b_bwd(qb, kb, kbt, vb, do, o, lse, bias_t, sink, s_len):
    bl, kv, sp, _ = kb.shape
    nb = s_len // BLOCK
    nbs = min(QB_PER_STEP, nb)
    r = GROUP * BLOCK

    def body(q_ref, k_ref, kt_ref, v_ref, do_ref, o_ref, l_ref, bt_ref, sink_ref,
             dq_ref, dk_ref, dv_ref, dsum_ref, dsink_ref, dkt_sc, dvt_sc):
        g, b, ns = pl.program_id(0), pl.program_id(1), pl.program_id(2)
        sink_row = _sink_row(sink_ref, g)

        @pl.when(ns == 0)
        def _():
            dkt_sc[...] = jnp.zeros_like(dkt_sc)
            dvt_sc[...] = jnp.zeros_like(dvt_sc)

        @pl.when((b == 0) & (ns == 0))
        def _():
            dsum_ref[...] = jnp.zeros_like(dsum_ref)
            dsink_ref[...] = jnp.zeros_like(dsink_ref)

        dsum = jnp.zeros((SPAN, r), F32)
        dsink = jnp.zeros((1, r), F32)
        for j in range(nbs):
            n = ns * nbs + j
            span = pl.ds(pl.multiple_of(n * BLOCK, BLOCK), SPAN)
            rows = slice(j * BLOCK, (j + 1) * BLOCK)
            dot32 = _heads_t(do_ref[rows, :])
            drow = jnp.sum(dot32 * _heads_t(o_ref[rows, :]), axis=0, keepdims=True)
            qt, dot = _heads_t(q_ref[rows, :]).astype(BF16), dot32.astype(BF16)
            lrow = l_ref[0, 0, j, 0:1, :]
            st = _dot(k_ref[0, 0, span, :], qt) + bt_ref[_bias_variant(n, nb), 0]
            pt = jnp.exp(st - lrow)
            dst = pt * (_dot(v_ref[0, 0, span, :], dot) - drow)
            dsum = dsum + dst
            dsink = dsink - jnp.exp(sink_row - lrow) * drow
            dsb = dst.astype(BF16)
            dvt_sc[:, span] += _dot_nt(dot, pt.astype(BF16))
            dkt_sc[:, span] += _dot_nt(qt, dsb)
            dq_ref[rows, :] = _heads_t_inv(_dot(kt_ref[0, 0, :, span], dsb))
        dsum_ref[0] += dsum
        dsink_ref[0, 0:1, :] += dsink

        @pl.when(ns == nb // nbs - 1)
        def _():
            dk_ref[0, 0] = dkt_sc[:, BLOCK:BLOCK + s_len].T
            dv_ref[0, 0] = dvt_sc[:, BLOCK:BLOCK + s_len].T

    kvspec = pl.BlockSpec((1, 1, sp, HEAD_DIM), lambda g, b, n: (b, g, 0, 0))
    kvout = pl.BlockSpec((1, 1, s_len, HEAD_DIM), lambda g, b, n: (b, g, 0, 0))
    tok = pl.BlockSpec((nbs * BLOCK, GROUP * HEAD_DIM), lambda g, b, n: (b * (nb // nbs) + n, g))
    return pl.pallas_call(
        body,
        grid=(kv, bl, nb // nbs),
        in_specs=[tok, kvspec, pl.BlockSpec((1, 1, HEAD_DIM, sp), lambda g, b, n: (b, g, 0, 0)), kvspec, tok, tok,
                  pl.BlockSpec((1, 1, nbs, 8, r), lambda g, b, n: (b, g, n, 0, 0)),
                  pl.BlockSpec((3, 1, SPAN, r), lambda g, b, n: (0, g, 0, 0)),
                  pl.BlockSpec(memory_space=pltpu.SMEM)],
        out_specs=[tok, kvout, kvout,
                   pl.BlockSpec((1, SPAN, r), lambda g, b, n: (g, 0, 0)),
                   pl.BlockSpec((1, 8, r), lambda g, b, n: (g, 0, 0))],
        out_shape=[SDS(qb.shape, F32), SDS((bl, kv, s_len, HEAD_DIM), F32), SDS((bl, kv, s_len, HEAD_DIM), F32),
                   SDS((kv, SPAN, r), F32), SDS((kv, 8, r), F32)],
        scratch_shapes=[pltpu.VMEM((HEAD_DIM, sp), F32), pltpu.VMEM((HEAD_DIM, sp), F32)],
        compiler_params=_cp("arbitrary", "arbitrary", "arbitrary"),
        name="attn_b_bwd",
    )(qb, kb, kbt, vb, do, o, lse, bias_t, sink)


def _bias_reduce(dsum, dsink, bucket_t4):
    kv, _, r = dsum.shape

    def body(ds_ref, dk_ref, bk_ref, rel_ref, sink_ref):
        lane = lax.broadcasted_iota(jnp.int32, (N_BUCKETS, 128), 1)
        lane8 = lax.broadcasted_iota(jnp.int32, (8, 128), 1)
        bk = bk_ref[...]
        for g in range(kv):
            ds = ds_ref[g]
            rowi = lax.broadcasted_iota(jnp.int32, (N_BUCKETS, r), 0)
            red = jnp.zeros((N_BUCKETS, r), F32)
            for b in range(N_BUCKETS):
                red = jnp.where(rowi == b, jnp.sum(jnp.where(bk == b, ds, 0.0), axis=0, keepdims=True), red)
            out = jnp.zeros((N_BUCKETS, 128), F32)
            so = jnp.zeros((8, 128), F32)
            for h in range(GROUP):
                col = jnp.sum(red[:, h * BLOCK:(h + 1) * BLOCK], axis=1, keepdims=True)
                out = jnp.where(lane == h, col, out)
                sc = jnp.sum(dk_ref[g][:, h * BLOCK:(h + 1) * BLOCK], axis=1, keepdims=True)
                so = jnp.where(lane8 == h, sc, so)
            rel_ref[g] = out
            sink_ref[g] = so

    vm = pl.BlockSpec(memory_space=pltpu.VMEM)
    return pl.pallas_call(
        body,
        in_specs=[vm, vm, vm],
        out_specs=[vm, vm],
        out_shape=[SDS((kv, N_BUCKETS, 128), F32), SDS((kv, 8, 128), F32)],
        name="bias_reduce",
    )(dsum, dsink, bucket_t4)


def _dqkprep(dqa, dka, dva, dqb, dkb, dvb, proj, cos, sin_signed, gq, gk, s_len, ts):
    t, p_cols = proj.shape
    bl, kva, kvb = dka.shape[0], dka.shape[1], dkb.shape[1]
    ha, hb = dqa.shape[1] // HEAD_DIM, dqb.shape[1] // HEAD_DIM
    ns = s_len // ts

    def body(dqa_ref, dka_ref, dva_ref, dqb_ref, dkb_ref, dvb_ref, p_ref, cos_ref, sin_ref, gq_ref, gk_ref,
             dp_ref, dgq_ref, dgk_ref):
        b, i = pl.program_id(0), pl.program_id(1)
        cs, sn = cos_ref[...], sin_ref[...]
        low, first = _pair_masks(ts)

        @pl.when((b == 0) & (i == 0))
        def _():
            dgq_ref[...] = jnp.zeros_like(dgq_ref)
            dgk_ref[...] = jnp.zeros_like(dgk_ref)

        def grad_pair(ref, p):
            return jnp.concatenate([ref[0, 2 * p], ref[0, 2 * p + 1]], axis=1)

        def put(p, val):
            dp_ref[:, p * PAIR:(p + 1) * PAIR] = val.astype(BF16)

        def unrope_norm(d_rot, p, g, dg_ref):
            dn = d_rot * cs + _pair_partner(d_rot * sn, first)
            xp = p_ref[:, p * PAIR:(p + 1) * PAIR]
            r = lax.rsqrt(_pair_mean(xp * xp, low) + EPS)
            n = xp * r
            gd = g * dn
            dg_ref[0:1, :] += jnp.sum(dn * n, axis=0, keepdims=True)
            put(p, r * (gd - n * _pair_mean(n * gd, low)))

        for p in range(ha // 2):
            unrope_norm(dqa_ref[:, p * PAIR:(p + 1) * PAIR] * SCALE, p, gq_ref[...], dgq_ref)
        base = ha // 2
        for p in range(kva // 2):
            unrope_norm(grad_pair(dka_ref, p), base + p, gk_ref[...], dgk_ref)
            put(base + kva // 2 + p, grad_pair(dva_ref, p))
        base += kva
        for p in range(hb // 2):
            put(base + p, dqb_ref[:, p * PAIR:(p + 1) * PAIR] * SCALE)
        base += hb // 2
        for p in range(kvb // 2):
            put(base + p, grad_pair(dkb_ref, p))
            put(base + kvb // 2 + p, grad_pair(dvb_ref, p))

    def hm(nh):
        return pl.BlockSpec((1, nh, ts, HEAD_DIM), lambda b, i: (b, 0, i, 0))

    def tokmajor(nh):
        return pl.BlockSpec((ts, nh * HEAD_DIM), lambda b, i: (b * ns + i, 0))

    vec = pl.BlockSpec((1, PAIR), lambda b, i: (0, 0))
    tab = pl.BlockSpec((ts, PAIR), lambda b, i: (i, 0))
    acc = pl.BlockSpec((8, PAIR), lambda b, i: (0, 0))
    pspec = pl.BlockSpec((ts, p_cols), lambda b, i: (b * ns + i, 0))
    return pl.pallas_call(
        body,
        grid=(bl, ns),
        in_specs=[tokmajor(ha), hm(kva), hm(kva), tokmajor(hb), hm(kvb), hm(kvb), pspec, tab, tab, vec, vec],
        out_specs=[pspec, acc, acc],
        out_shape=[SDS((t, p_cols), BF16), SDS((8, PAIR), F32), SDS((8, PAIR), F32)],
        compiler_params=_cp("arbitrary", "arbitrary"),
        name="dqkprep",
    )(dqa, dka, dva, dqb, dkb, dvb, proj, cos, sin_signed, gq, gk)


def _dx_final(dproj, w_t, x2, dx1, g1, tm, grads):
    t, d = x2.shape
    p_cols = w_t.shape[0]
    ng = len(grads)
    nsteps = t // tm

    def body(dp_ref, w_ref, x_ref, dx1_ref, g_ref, *rest):
        grad_refs, (dx_ref, dg_ref), parts = rest[:ng], rest[ng:ng + 2], rest[ng + 2:2 * ng + 2]
        start, wait = _direct_exchange("scatter", grad_refs, parts, *rest[2 * ng + 2:])

        @pl.when(pl.program_id(0) == 0)
        def _():
            start()
            dg_ref[...] = jnp.zeros_like(dg_ref)

        dh = _dot(dp_ref[...], w_ref[...])
        g = g_ref[...]
        _, n, r = _rms_fwd(x_ref[...], g)
        dx, dgt = _rms_bwd(n, r, g, dh)
        dx_ref[...] = dx1_ref[...] + dx
        dg_ref[0:1, :] += jnp.sum(dgt, axis=0, keepdims=True)
        pl.when(pl.program_id(0) == nsteps - 1)(wait)

    tile = pl.BlockSpec((tm, d), lambda i: (i, 0))
    anyspec = pl.BlockSpec(memory_space=pl.ANY)
    res = pl.pallas_call(
        body,
        grid=(nsteps,),
        in_specs=[pl.BlockSpec((tm, p_cols), lambda i: (i, 0)),
                  pl.BlockSpec((p_cols, d), lambda i: (0, 0)),
                  tile, tile, pl.BlockSpec((1, d), lambda i: (0, 0))] + [anyspec] * ng,
        out_specs=[tile, pl.BlockSpec((8, d), lambda i: (0, 0))] + [anyspec] * ng,
        out_shape=[SDS((t, d), F32), SDS((8, d), F32)] + [SDS(g.shape, g.dtype) for g in grads],
        scratch_shapes=_exchange_scratch(ng),
        compiler_params=_cp("arbitrary"),
        name="dx_final",
    )(dproj, w_t, x2, dx1, g1, *grads)
    return res[0], res[1], res[2:]


def _adamw_math(w, g, m, v):
    m = ADAM_B1 * m + (1.0 - ADAM_B1) * g
    v = ADAM_B2 * v + (1.0 - ADAM_B2) * (g * g)
    m_hat = m / (1.0 - ADAM_B1 ** ADAM_STEP)
    v_hat = v / (1.0 - ADAM_B2 ** ADAM_STEP)
    delta = -ADAM_LR * (m_hat / (jnp.sqrt(v_hat) + ADAM_EPS) + ADAM_WD * w)
    return delta, m, v


def _adamw_sum(parts, w, m, v, tr, name):
    rows, cols = w.shape

    def body(p_ref, w_ref, m_ref, v_ref, g_ref, d_ref, nm_ref, nv_ref):
        g = p_ref[0].astype(F32)
        for s in range(1, N_DEV):
            g = g + p_ref[s].astype(F32)
        g_ref[...] = g
        d_ref[...], nm_ref[...], nv_ref[...] = _adamw_math(w_ref[...], g, m_ref[...], v_ref[...])

    tr = min(tr, rows)
    tile = pl.BlockSpec((tr, cols), lambda i: (i, 0))
    return pl.pallas_call(
        body,
        grid=(rows // tr,),
        in_specs=[pl.BlockSpec((N_DEV, tr, cols), lambda i: (0, i, 0)), tile, tile, tile],
        out_specs=[tile] * 4,
        out_shape=[SDS((rows, cols), F32)] * 4,
        compiler_params=_cp("parallel"),
        name=name,
    )(parts, w, m, v)


def _adamw_small(vec, rel, ws, ms, vs):
    hb = ws[6].shape[1]
    n = len(ws)

    def body(vec_ref, rel_ref, *rest):
        w_refs, m_refs, v_refs = rest[:n], rest[n:2 * n], rest[2 * n:3 * n]
        loss_ref, outs = rest[3 * n], rest[3 * n + 1:]
        grads = [vec_ref[0:1, :], vec_ref[1:2, :], vec_ref[2:3, :], vec_ref[3:4, :],
                 vec_ref[4:5, 0:HEAD_DIM], vec_ref[4:5, SMALL_LANES:SMALL_LANES + HEAD_DIM],
                 vec_ref[4:5, 2 * SMALL_LANES:2 * SMALL_LANES + hb], rel_ref[:, 0:hb]]
        loss_ref[...] = vec_ref[4:5, 3 * SMALL_LANES:3 * SMALL_LANES + 1]
        for p, g in enumerate(grads):
            g_ref, d_ref, nm_ref, nv_ref = outs[4 * p:4 * p + 4]
            g_ref[...] = g
            d_ref[...], nm_ref[...], nv_ref[...] = _adamw_math(w_refs[p][...], g, m_refs[p][...], v_refs[p][...])

    vm = pl.BlockSpec(memory_space=pltpu.VMEM)
    res = pl.pallas_call(
        body,
        in_specs=[vm] * (2 + 3 * n),
        out_specs=[vm] * (1 + 4 * n),
        out_shape=[SDS((1, 1), F32)] + [SDS(w.shape, F32) for w in ws for _ in range(4)],
        name="adamw_small",
    )(vec, rel, *ws, *ms, *vs)
    return res[0], [res[1 + 4 * p:5 + 4 * p] for p in range(n)]


def _local_step(x, loss_target, win_g, wo_s, wup_s, wdn_s, g_pre_mix, g_post_mix, q_norm_a, k_norm_a, sink_b,
                rel_bias, g_pre_ffn, g_post_ffn):
    bl, s_len, d = x.shape
    t = bl * s_len
    nh = d // HEAD_DIM
    ha = nh // 2
    kva = ha // GROUP
    hb = nh - ha
    kvb = hb // GROUP
    tm = 512
    tw = min(4096, t)
    ts = min(512, s_len)
    tq, tk = 2 * BLOCK, min(512, s_len // 2)

    x2 = x.reshape(t, d)
    tg2 = loss_target.reshape(t, d)
    cos, sin_signed = _rope_tables(s_len)
    gq2, gk2 = jnp.tile(q_norm_a, (1, 2)), jnp.tile(k_norm_a, (1, 2))
    a = jnp.arange(BLOCK, dtype=jnp.int32)
    c = jnp.arange(SPAN, dtype=jnp.int32)
    bucket_t = _t5_bucket(c[:, None] - BLOCK - a[None, :])
    bucket_t4 = jnp.tile(bucket_t, (1, GROUP))
    w_in_t = win_g.reshape(-1, d)
    p_cols = w_in_t.shape[0]

    h1, proj = _inproj(x2, g_pre_mix, w_in_t, tm)
    qa, ka, kat, va, vat, qb, kb, kbt, vb, vbt = _qkprep(
        proj, cos, sin_signed, gq2, gk2, bl, s_len, ha, kva, hb, kvb, ts)
    bias_t = _bias_build(bucket_t, rel_bias, hb)
    oa, lse_a, (wo_g, wup_g, wdn_g) = _attn_a_fwd(qa, ka, vat, tq, tk, [wo_s, wup_s, wdn_s])
    wo = wo_g.reshape(-1, d)
    wdn = wdn_g.reshape(-1, d)
    ob, lse_b = _attn_b_fwd(qb, kb, vbt, bias_t, sink_b, s_len)
    mix, x1, h2 = _mixout(oa, ob, wo, x2, g_post_mix, g_pre_ffn, tm)
    u, df, dy, dg4, loss8 = _ffn_fwd(h2, wup_g, wdn, x1, tg2, g_post_ffn, tm, FFN_BLOCKS_PER_STEP)

    dpre, dx1, dmix, dg3, dg2 = _ffn_bwd(df, u, wdn, wup_g, x1, dy, mix, g_pre_ffn, g_post_mix, FFN_BWD_TOKENS,
                                         FFN_BLOCKS_PER_STEP)
    gw_dn = _wgrad_rows(u, df, N_DEV, tw, "wgrad_down", square=True)
    gw_up = _wgrad_cols(h2, dpre, N_DEV, tw, "wgrad_up")
    gw_o = _wgrad_o(oa, ob, dmix, N_DEV, min(2048, t))
    doa, dob = _attn_out_bwd(dmix, wo, oa.shape[1], tm)
    dqa, dka, dva, (p_o, p_up, p_dn) = _attn_a_bwd(qa, ka, kat, va, doa, oa, lse_a, tq, tk, [gw_o, gw_up, gw_dn])
    dqb, dkb, dvb, dsum, dsink = _attn_b_bwd(qb, kb, kbt, vb, dob, ob, lse_b, bias_t, sink_b, s_len)
    drel_g, dsink_g = _bias_reduce(dsum, dsink, bucket_t4)
    dproj, dgq, dgk = _dqkprep(dqa, dka, dva, dqb, dkb, dvb, proj, cos, sin_signed, gq2, gk2, s_len, ts)
    gw_in_t = _wgrad_rows(dproj, h1, p_cols // 256, tw, "wgrad_in").reshape(N_DEV, -1, d)
    grad_x, dg1, (p_in,) = _dx_final(dproj, w_in_t, x2, dx1, g_pre_mix, tm, [gw_in_t])

    vec, rel = _small_allreduce([dg1, dg2, dg3, dg4], dgq, dgk, dsink_g, drel_g, loss8)
    return grad_x.reshape(bl, s_len, d), p_in, p_o, p_up, p_dn, vec, rel


def kernel(x, w_in, w_o, g_pre_mix, g_post_mix, q_norm_a, k_norm_a, sink_b, rel_bias, g_pre_ffn, w_ffn_up, w_ffn_down, g_post_ffn, loss_target, m_w_in, m_w_o, m_g_pre_mix, m_g_post_mix, m_q_norm_a, m_k_norm_a, m_sink_b, m_rel_bias, m_g_pre_ffn, m_w_ffn_up, m_w_ffn_down, m_g_post_ffn, v_w_in, v_w_o, v_g_pre_mix, v_g_post_mix, v_q_norm_a, v_k_norm_a, v_sink_b, v_rel_bias, v_g_pre_ffn, v_w_ffn_up, v_w_ffn_down, v_g_post_ffn):
    w_in_t = w_in[0].T
    (win_g,) = _weight_gather([w_in_t.astype(BF16)])

    grad_x, p_in, p_o, p_up, p_dn, vec, rel = _local_step(
        x, loss_target, win_g, w_o[0].astype(BF16), w_ffn_up[0].astype(BF16), w_ffn_down[0].astype(BF16),
        g_pre_mix, g_post_mix, q_norm_a, k_norm_a, sink_b, rel_bias, g_pre_ffn, g_post_ffn)

    big = {
        "w_in": [a.T for a in _adamw_sum(p_in, w_in_t, m_w_in[0].T, v_w_in[0].T, 192, "adamw_in")],
        "w_o": _adamw_sum(p_o, w_o[0], m_w_o[0], v_w_o[0], 128, "adamw_o"),
        "w_up": _adamw_sum(p_up, w_ffn_up[0], m_w_ffn_up[0], v_w_ffn_up[0], 256, "adamw_up"),
        "w_dn": _adamw_sum(p_dn, w_ffn_down[0], m_w_ffn_down[0], v_w_ffn_down[0], 256, "adamw_down"),
    }
    loss, small = _adamw_small(
        vec, rel,
        [g_pre_mix, g_post_mix, g_pre_ffn, g_post_ffn, q_norm_a, k_norm_a, sink_b, rel_bias],
        [m_g_pre_mix, m_g_post_mix, m_g_pre_ffn, m_g_post_ffn, m_q_norm_a, m_k_norm_a, m_sink_b, m_rel_bias],
        [v_g_pre_mix, v_g_post_mix, v_g_pre_ffn, v_g_post_ffn, v_q_norm_a, v_k_norm_a, v_sink_b, v_rel_bias])
    s_pre_mix, s_post_mix, s_pre_ffn, s_post_ffn, s_qn, s_kn, s_sink, s_rel = small

    def outs(kind):
        return [big["w_in"][kind][None], big["w_o"][kind][None], s_pre_mix[kind], s_post_mix[kind], s_qn[kind],
                s_kn[kind], s_sink[kind], s_rel[kind], s_pre_ffn[kind], big["w_up"][kind][None],
                big["w_dn"][kind][None], s_post_ffn[kind]]

    return (loss.reshape(()), grad_x, *outs(0), *outs(1), *outs(2), *outs(3))
```

```python
import jax
import jax.numpy as jnp
import numpy as np
from jax import lax
from jax.experimental import pallas as pl
from jax.experimental.pallas import tpu as pltpu

F32 = jnp.float32
BF16 = jnp.bfloat16
SDS = jax.ShapeDtypeStruct

N_DEV = 8
HEAD_DIM = 64
GROUP = 4
BLOCK = 128
SPAN = 3 * BLOCK
GRID_W = 64
N_BUCKETS = 32
MAX_DISTANCE = 128
ROPE_THETA = 10000.0
EPS = 1e-6
NEG_INF = -1e30
SCALE = HEAD_DIM ** -0.5
VT_PAD = 16

ADAM_LR = 0.001
ADAM_B1 = 0.9
ADAM_B2 = 0.999
ADAM_EPS = 1e-08
ADAM_WD = 0.01
ADAM_STEP = 10

VMEM_LIMIT = 56 * 1024 * 1024
MESH = pl.DeviceIdType.MESH


def _cp(*sem):
    return pltpu.CompilerParams(dimension_semantics=sem, vmem_limit_bytes=VMEM_LIMIT)


def _dot(a, b):
    return jnp.dot(a, b, preferred_element_type=F32)


def _dot_nt(a, b):
    return lax.dot_general(a, b, (((1,), (1,)), ((), ())), preferred_element_type=F32)


def _dot_tn(a, b):
    return lax.dot_general(a, b, (((0,), (0,)), ((), ())), preferred_element_type=F32)


def _rms_fwd(x, g):
    r = lax.rsqrt(jnp.mean(x * x, axis=-1, keepdims=True) + EPS)
    n = x * r
    return n * g, n, r


def _rms_bwd(n, r, g, dy):
    gd = g * dy
    dx = r * (gd - n * jnp.mean(n * gd, axis=-1, keepdims=True))
    return dx, dy * n


def _rope_tables(s_len):
    rows = s_len // GRID_W
    row = np.repeat(np.arange(rows, dtype=np.int32), GRID_W)
    col = np.tile(np.arange(GRID_W, dtype=np.int32), rows)
    nf = HEAD_DIM // 4
    freqs = np.float32(ROPE_THETA) ** (-np.arange(nf, dtype=np.float32) / np.float32(nf))
    ang_r = row.astype(np.float32)[:, None] * freqs[None, :]
    ang_c = col.astype(np.float32)[:, None] * freqs[None, :]
    cr, sr, cc, sc = np.cos(ang_r), np.sin(ang_r), np.cos(ang_c), np.sin(ang_c)
    cos = np.concatenate([cr, cr, cc, cc] * 2, axis=-1).astype(np.float32)
    sin_signed = np.concatenate([-sr, sr, -sc, sc] * 2, axis=-1).astype(np.float32)
    return jnp.asarray(cos), jnp.asarray(sin_signed)


def _t5_bucket(rel):
    nb = N_BUCKETS // 2
    ret = (rel > 0).astype(jnp.int32) * nb
    n = jnp.abs(rel)
    max_exact = nb // 2
    nf = jnp.maximum(n, 1).astype(F32)
    large = max_exact + (jnp.log(nf / max_exact) / np.float32(np.log(MAX_DISTANCE / max_exact))
                         * (nb - max_exact)).astype(jnp.int32)
    large = jnp.minimum(large, nb - 1)
    return ret + jnp.where(n < max_exact, n, large)


def _mesh_pos():
    return lax.axis_index("x"), lax.axis_index("y"), lax.axis_index("c")


def _lin(p):
    return 4 * p[0] + 2 * p[1] + p[2]


def _bias_tables(bkt_ref, tbl_ref, out_ref, hb):
    bkt = bkt_ref[...]
    ci = lax.broadcasted_iota(jnp.int32, (SPAN, BLOCK), 0)
    qi = lax.broadcasted_iota(jnp.int32, (SPAN, BLOCK), 1)
    band = jnp.abs(ci - BLOCK - qi) <= BLOCK
    masks = (band, band & (ci >= BLOCK), band & (ci < 2 * BLOCK))
    for h in range(hb):
        acct = jnp.zeros((SPAN, BLOCK), F32)
        for b in range(N_BUCKETS):
            acct = jnp.where(bkt == b, tbl_ref[b, h], acct)
        lanes = slice((h % GROUP) * BLOCK, (h % GROUP + 1) * BLOCK)
        for var, mask in enumerate(masks):
            out_ref[var, h // GROUP, :, lanes] = jnp.where(mask, acct, NEG_INF)


def _weight_gather(shards, bucket_t, rel_bias):
    n = len(shards)
    hb = rel_bias.shape[1]

    def body(*refs):
        xs, (bkt_ref, tbl_ref), outs, bias_ref = refs[:n], refs[n:n + 2], refs[n + 2:2 * n + 2], refs[2 * n + 2]
        send_sems, recv_sems, local_sems = refs[2 * n + 3:]
        x, y, c = _mesh_pos()
        me, sibling = (x, y, c), (x, y, 1 - c)
        chips = [(1 - x, y), (x, 1 - y), (1 - x, 1 - y)]

        def copy(a, k, block, to, src=None):
            slot = outs[a].at[_lin(block)]
            return pltpu.make_async_remote_copy(
                src_ref=slot if src is None else src, dst_ref=slot,
                send_sem=send_sems.at[a, k], recv_sem=recv_sems.at[a, k],
                device_id=to, device_id_type=MESH)

        started = []
        for a in range(n):
            mine = pltpu.make_async_copy(xs[a], outs[a].at[_lin(me)], local_sems.at[a])
            mine.start()
            started.append(mine)
        sends = []
        for a in range(n):
            first = [copy(a, 0, me, sibling, src=xs[a])]
            first += [copy(a, 1 + j, me, (*chip, c), src=xs[a]) for j, chip in enumerate(chips)]
            for cp in first:
                cp.start()
            sends += first
        _bias_tables(bkt_ref, tbl_ref, bias_ref, hb)
        for a in range(n):
            for j, chip in enumerate(chips):
                copy(a, 1 + j, (*chip, c), me).wait_recv()
                fwd = copy(a, 4 + j, (*chip, c), sibling)
                fwd.start()
                sends.append(fwd)
        for a in range(n):
            copy(a, 0, sibling, me).wait_recv()
            for j, chip in enumerate(chips):
                copy(a, 4 + j, (*chip, 1 - c), me).wait_recv()
        for cp in sends:
            cp.wait_send()
        for mine in started:
            mine.wait()

    anyspec = pl.BlockSpec(memory_space=pl.ANY)
    vm = pl.BlockSpec(memory_space=pltpu.VMEM)
    res = pl.pallas_call(
        body,
        out_shape=[SDS((N_DEV,) + s.shape, s.dtype) for s in shards]
        + [SDS((3, hb // GROUP, SPAN, GROUP * BLOCK), F32)],
        in_specs=[anyspec] * n + [vm, pl.BlockSpec(memory_space=pltpu.SMEM)],
        out_specs=[anyspec] * n + [vm],
        scratch_shapes=[pltpu.SemaphoreType.DMA((n, 7)), pltpu.SemaphoreType.DMA((n, 7)),
                        pltpu.SemaphoreType.DMA((n,))],
        name="weight_gather",
    )(*shards, bucket_t, rel_bias)
    return res[:n], res[n]


def _direct_exchange(kind, ins, outs, send_sems, recv_sems, local_sems):
    x, y, c = _mesh_pos()
    me = (x, y, c)
    peers = [(x, y, 1 - c), (1 - x, y, c), (x, 1 - y, c), (1 - x, 1 - y, c),
             (1 - x, y, 1 - c), (x, 1 - y, 1 - c), (1 - x, 1 - y, 1 - c)]

    def src(a, to):
        return ins[a] if kind == "gather" else ins[a].at[_lin(to)]

    def remote(a, k, to, frm):
        return pltpu.make_async_remote_copy(
            src_ref=src(a, to), dst_ref=outs[a].at[_lin(frm)],
            send_sem=send_sems.at[a, k], recv_sem=recv_sems.at[a, k],
            device_id=to, device_id_type=MESH)

    n = len(ins)
    sends = [remote(a, k, p, me) for a in range(n) for k, p in enumerate(peers)]
    arrivals = [remote(a, k, p, p) for a in range(n) for k, p in enumerate(peers)]
    local = [pltpu.make_async_copy(src(a, me), outs[a].at[_lin(me)], local_sems.at[a]) for a in range(n)]

    def start():
        for cp in local + sends:
            cp.start()

    def wait():
        for cp in arrivals:
            cp.wait_recv()
        for cp in sends:
            cp.wait_send()
        for cp in local:
            cp.wait()

    return start, wait


def _exchange_scratch(n):
    return [pltpu.SemaphoreType.DMA((n, 7)), pltpu.SemaphoreType.DMA((n, 7)), pltpu.SemaphoreType.DMA((n,))]


SMALL_LANES = 128


def _small_allreduce(dg_rows, dgq, dgk, dsink_g, drel_g, loss8):
    d = dg_rows[0].shape[1]
    kv = dsink_g.shape[0]

    def body(g1_ref, g2_ref, g3_ref, g4_ref, gq_ref, gk_ref, sk_ref, rl_ref, ls_ref, vec_ref, rel_ref,
             vbuf, rbuf, vland, rland, send_sems, recv_sems):
        x, y, c = _mesh_pos()
        me = (x, y, c)
        peers = [(x, y, 1 - c), (1 - x, y, c), (x, 1 - y, c), (1 - x, 1 - y, c),
                 (1 - x, y, 1 - c), (x, 1 - y, 1 - c), (1 - x, 1 - y, 1 - c)]
        vbuf[...] = jnp.zeros_like(vbuf)
        rbuf[...] = jnp.zeros_like(rbuf)
        for row, ref in enumerate((g1_ref, g2_ref, g3_ref, g4_ref)):
            vbuf[row:row + 1, :] = ref[0:1, :]
        vbuf[4:5, 0:HEAD_DIM] = gq_ref[0:1, 0:HEAD_DIM] + gq_ref[0:1, HEAD_DIM:PAIR]
        vbuf[4:5, SMALL_LANES:SMALL_LANES + HEAD_DIM] = gk_ref[0:1, 0:HEAD_DIM] + gk_ref[0:1, HEAD_DIM:PAIR]
        for g in range(kv):
            vbuf[4:5, 2 * SMALL_LANES + g * GROUP:2 * SMALL_LANES + (g + 1) * GROUP] = sk_ref[g, 0:1, 0:GROUP]
            rbuf[:, g * GROUP:(g + 1) * GROUP] = rl_ref[g, :, 0:GROUP]
        vbuf[4:5, 3 * SMALL_LANES:3 * SMALL_LANES + 1] = ls_ref[0:1, 0:1]

        def copies(k, to, frm):
            return [pltpu.make_async_remote_copy(
                src_ref=buf, dst_ref=land.at[_lin(frm)], send_sem=send_sems.at[a, k], recv_sem=recv_sems.at[a, k],
                device_id=to, device_id_type=MESH) for a, (buf, land) in enumerate(((vbuf, vland), (rbuf, rland)))]

        sends = [cp for k, p in enumerate(peers) for cp in copies(k, p, me)]
        for cp in sends:
            cp.start()
        vland[_lin(me)] = vbuf[...]
        rland[_lin(me)] = rbuf[...]
        for k, p in enumerate(peers):
            for cp in copies(k, p, p):
                cp.wait_recv()
        for cp in sends:
            cp.wait_send()
        vacc, racc = vland[0], rland[0]
        for s in range(1, N_DEV):
            vacc, racc = vacc + vland[s], racc + rland[s]
        vec_ref[...] = vacc
        rel_ref[...] = racc

    vm = pl.BlockSpec(memory_space=pltpu.VMEM)
    return pl.pallas_call(
        body,
        out_shape=[SDS((8, d), F32), SDS((N_BUCKETS, 128), F32)],
        in_specs=[vm] * 9,
        out_specs=[vm, vm],
        scratch_shapes=[pltpu.VMEM((8, d), F32), pltpu.VMEM((N_BUCKETS, 128), F32),
                        pltpu.VMEM((N_DEV, 8, d), F32), pltpu.VMEM((N_DEV, N_BUCKETS, 128), F32),
                        pltpu.SemaphoreType.DMA((2, 7)), pltpu.SemaphoreType.DMA((2, 7))],
        name="small_allreduce",
    )(*dg_rows, dgq, dgk, dsink_g, drel_g, loss8)


def _inproj(x2, g1, w_t, tm):
    t, d = x2.shape
    p = w_t.shape[0]

    def body(x_ref, g_ref, w_ref, h_ref, p_ref):
        y, _, _ = _rms_fwd(x_ref[...], g_ref[...])
        h = y.astype(BF16)
        h_ref[...] = h
        p_ref[...] = _dot_nt(h, w_ref[...])

    return pl.pallas_call(
        body,
        grid=(t // tm,),
        in_specs=[pl.BlockSpec((tm, d), lambda i: (i, 0)),
                  pl.BlockSpec((1, d), lambda i: (0, 0)),
                  pl.BlockSpec((p, d), lambda i: (0, 0))],
        out_specs=[pl.BlockSpec((tm, d), lambda i: (i, 0)),
                   pl.BlockSpec((tm, p), lambda i: (i, 0))],
        out_shape=[SDS((t, d), BF16), SDS((t, p), F32)],
        compiler_params=_cp("parallel"),
        name="inproj",
    )(x2, g1, w_t)


PAIR = 2 * HEAD_DIM


def _pair_masks(ts):
    lane = lax.broadcasted_iota(jnp.int32, (ts, PAIR), 1)
    return lane < HEAD_DIM, (lane % 32) < 16


def _pair_mean(v, low):
    del low
    r = lax.broadcasted_iota(jnp.int32, (PAIR, PAIR), 0) // HEAD_DIM
    c = lax.broadcasted_iota(jnp.int32, (PAIR, PAIR), 1) // HEAD_DIM
    same_head = (r == c).astype(BF16)
    hi = v.astype(BF16)
    lo = (v - hi.astype(F32)).astype(BF16)
    return (_dot(hi, same_head) + _dot(lo, same_head)) * (1.0 / HEAD_DIM)


def _pair_partner(v, first):
    return jnp.where(first, pltpu.roll(v, PAIR - 16, 1), pltpu.roll(v, 16, 1))


def _qkprep(proj, cos, sin_signed, gq, gk, bl, s_len, ha, kva, hb, kvb, ts):
    t, p_cols = proj.shape
    assert ha % 2 == 0 and kva % 2 == 0 and hb % 2 == 0 and kvb % 2 == 0
    ns = s_len // ts
    sp = s_len + 2 * BLOCK

    def body(p_ref, cos_ref, sin_ref, gq_ref, gk_ref, qa_ref, ka_ref, kat_ref, va_ref, vat_ref, qb_ref, kb_ref,
             kbt_ref, vb_ref, vbt_ref):
        i = pl.program_id(1)
        cs, sn = cos_ref[...], sin_ref[...]
        low, first = _pair_masks(ts)
        ones_row = (lax.broadcasted_iota(jnp.int32, (VT_PAD, ts), 0) == 0).astype(BF16)
        heads = (slice(0, HEAD_DIM), slice(HEAD_DIM, PAIR))

        def pair(p):
            return p_ref[:, p * PAIR:(p + 1) * PAIR]

        def normrope(x, g):
            y = x * lax.rsqrt(_pair_mean(x * x, low) + EPS) * g
            return y * cs + _pair_partner(y, first) * sn

        eye = (lax.broadcasted_iota(jnp.int32, (PAIR, PAIR), 0)
               == lax.broadcasted_iota(jnp.int32, (PAIR, PAIR), 1)).astype(BF16)

        def transposed(xb):
            return _dot_nt(eye, xb).astype(BF16)

        for p in range(ha // 2):
            qa_ref[:, p * PAIR:(p + 1) * PAIR] = (normrope(pair(p), gq_ref[...]) * SCALE).astype(BF16)
        base = ha // 2
        for p in range(kva // 2):
            k = normrope(pair(base + p), gk_ref[...]).astype(BF16)
            v = pair(base + kva // 2 + p).astype(BF16)
            kt, vt = transposed(k), transposed(v)
            for e, lanes in enumerate(heads):
                ka_ref[0, 2 * p + e] = k[:, lanes]
                va_ref[0, 2 * p + e] = v[:, lanes]
                kat_ref[0, 2 * p + e] = kt[lanes, :]
                vat_ref[0, 2 * p + e, 0:HEAD_DIM, :] = vt[lanes, :]
                vat_ref[0, 2 * p + e, HEAD_DIM:HEAD_DIM + VT_PAD, :] = ones_row
        base += kva
        for p in range(hb // 2):
            qb_ref[:, p * PAIR:(p + 1) * PAIR] = (pair(base + p) * SCALE).astype(BF16)
        base += hb // 2

        @pl.when(i == 0)
        def _():
            zeros = jnp.zeros((kvb, BLOCK, HEAD_DIM), BF16)
            zeros_t = jnp.zeros((kvb, HEAD_DIM + VT_PAD, BLOCK), BF16)
            for ref in (kb_ref, vb_ref):
                ref[0, :, 0:BLOCK, :] = zeros
                ref[0, :, sp - BLOCK:sp, :] = zeros
            kbt_ref[0, :, :, 0:BLOCK] = zeros_t[:, 0:HEAD_DIM]
            kbt_ref[0, :, :, sp - BLOCK:sp] = zeros_t[:, 0:HEAD_DIM]
            vbt_ref[0, :, :, 0:BLOCK] = zeros_t
            vbt_ref[0, :, :, sp - BLOCK:sp] = zeros_t

        rows = pl.ds(pl.multiple_of(BLOCK + i * ts, BLOCK), ts)
        for p in range(kvb // 2):
            k = pair(base + p).astype(BF16)
            v = pair(base + kvb // 2 + p).astype(BF16)
            kt, vt = transposed(k), transposed(v)
            for e, lanes in enumerate(heads):
                kb_ref[0, 2 * p + e, rows, :] = k[:, lanes]
                vb_ref[0, 2 * p + e, rows, :] = v[:, lanes]
                kbt_ref[0, 2 * p + e, :, rows] = kt[lanes, :]
                vbt_ref[0, 2 * p + e, 0:HEAD_DIM, rows] = vt[lanes, :]
                vbt_ref[0, 2 * p + e, HEAD_DIM:HEAD_DIM + VT_PAD, rows] = ones_row

    def hm(nh):
        return pl.BlockSpec((1, nh, ts, HEAD_DIM), lambda b, i: (b, 0, i, 0))

    def tokmajor(nh):
        return pl.BlockSpec((ts, nh * HEAD_DIM), lambda b, i: (b * ns + i, 0))

    def padded(nh):
        return pl.BlockSpec((1, nh, sp, HEAD_DIM), lambda b, i: (b, 0, 0, 0))

    def padded_t(nh, rows):
        return pl.BlockSpec((1, nh, rows, sp), lambda b, i: (b, 0, 0, 0))

    return pl.pallas_call(
        body,
        grid=(bl, ns),
        in_specs=[pl.BlockSpec((ts, p_cols), lambda b, i: (b * ns + i, 0)),
                  pl.BlockSpec((ts, PAIR), lambda b, i: (i, 0)),
                  pl.BlockSpec((ts, PAIR), lambda b, i: (i, 0)),
                  pl.BlockSpec((1, PAIR), lambda b, i: (0, 0)),
                  pl.BlockSpec((1, PAIR), lambda b, i: (0, 0))],
        out_specs=[tokmajor(ha), hm(kva), pl.BlockSpec((1, kva, HEAD_DIM, ts), lambda b, i: (b, 0, 0, i)), hm(kva),
                   pl.BlockSpec((1, kva, HEAD_DIM + VT_PAD, ts), lambda b, i: (b, 0, 0, i)),
                   tokmajor(hb), padded(kvb), padded_t(kvb, HEAD_DIM), padded(kvb),
                   padded_t(kvb, HEAD_DIM + VT_PAD)],
        out_shape=[SDS((t, ha * HEAD_DIM), BF16), SDS((bl, kva, s_len, HEAD_DIM), BF16),
                   SDS((bl, kva, HEAD_DIM, s_len), BF16),
                   SDS((bl, kva, s_len, HEAD_DIM), BF16), SDS((bl, kva, HEAD_DIM + VT_PAD, s_len), BF16),
                   SDS((t, hb * HEAD_DIM), BF16),
                   SDS((bl, kvb, sp, HEAD_DIM), BF16), SDS((bl, kvb, HEAD_DIM, sp), BF16),
                   SDS((bl, kvb, sp, HEAD_DIM), BF16), SDS((bl, kvb, HEAD_DIM + VT_PAD, sp), BF16)],
        compiler_params=_cp("parallel", "arbitrary"),
        name="qkprep",
    )(proj, cos, sin_signed, gq, gk)


def _attn_a_fwd(qa, ka, vat, tq, tk, shards):
    bl, kv, s_len, _ = ka.shape
    ha = qa.shape[1] // HEAD_DIM
    va_rows = vat.shape[2]
    nq, nk = s_len // tq, s_len // tk
    assert nk % 2 == 0
    r = GROUP * tq
    ns = len(shards)

    def body(q_ref, qn_ref, k_ref, v_ref, *rest):
        shard_refs, (o_ref, l_ref), gathered = rest[:ns], rest[ns:ns + 2], rest[ns + 2:2 * ns + 2]
        st_sc, send_sems, recv_sems, local_sems = rest[2 * ns + 2:]
        i = pl.program_id(2)
        step_id = (pl.program_id(0) * kv + pl.program_id(1)) * nq + i
        start, wait = _direct_exchange("gather", shard_refs, gathered, send_sems, recv_sems, local_sems)
        pl.when(step_id == 0)(start)

        q = _heads_t(q_ref[...]).astype(BF16)

        def scores(c, qv):
            return _dot(k_ref[0, 0, pl.ds(pl.multiple_of(c * tk, tk), tk), :], qv)

        def fold(st, c, carry):
            m_old, acc = carry
            m_new = jnp.maximum(m_old, jnp.max(st, axis=0, keepdims=True))
            pt = jnp.exp(st - m_new).astype(BF16)
            vt = v_ref[0, 0, :, pl.ds(pl.multiple_of(c * tk, tk), tk)]
            return m_new, jnp.exp(m_old - m_new) * acc + _dot(vt, pt)

        @pl.when(i == 0)
        def _():
            st_sc[0] = scores(0, q)

        def step(c2, carry):
            c = 2 * c2
            st_sc[1] = scores(c + 1, q)
            carry = fold(st_sc[0], c, carry)
            st_sc[0] = scores(c + 2, q)
            return fold(st_sc[1], c + 1, carry)

        carry = (jnp.full((1, r), -jnp.inf, F32), jnp.zeros((va_rows, r), F32))
        for c2 in range(nk // 2 - 1):
            carry = step(c2, carry)
        st_sc[1] = scores(nk - 1, q)
        carry = fold(st_sc[0], nk - 2, carry)
        st_sc[0] = scores(0, _heads_t(qn_ref[...]).astype(BF16))
        m, acc = fold(st_sc[1], nk - 1, carry)
        l = acc[HEAD_DIM:HEAD_DIM + 1, :]
        o_ref[...] = _heads_t_inv(acc[0:HEAD_DIM, :] / l).astype(BF16)
        l_ref[0, 0, 0] = jnp.broadcast_to(m + jnp.log(l), (8, r))
        pl.when(step_id == bl * kv * nq - 1)(wait)

    anyspec = pl.BlockSpec(memory_space=pl.ANY)
    res = pl.pallas_call(
        body,
        grid=(bl, kv, nq),
        in_specs=[pl.BlockSpec((tq, GROUP * HEAD_DIM), lambda b, g, i: (b * nq + i, g)),
                  pl.BlockSpec((tq, GROUP * HEAD_DIM), lambda b, g, i: (b * nq + jnp.minimum(i + 1, nq - 1), g)),
                  pl.BlockSpec((1, 1, s_len, HEAD_DIM), lambda b, g, i: (b, g, 0, 0)),
                  pl.BlockSpec((1, 1, va_rows, s_len), lambda b, g, i: (b, g, 0, 0))] + [anyspec] * ns,
        out_specs=[pl.BlockSpec((tq, GROUP * HEAD_DIM), lambda b, g, i: (b * nq + i, g)),
                   pl.BlockSpec((1, 1, 1, 8, r), lambda b, g, i: (b, g, i, 0, 0))] + [anyspec] * ns,
        out_shape=[SDS((bl * s_len, ha * HEAD_DIM), BF16), SDS((bl, kv, nq, 8, r), F32)]
        + [SDS((N_DEV,) + s.shape, s.dtype) for s in shards],
        scratch_shapes=[pltpu.VMEM((2, tk, r), F32)] + _exchange_scratch(ns),
        compiler_params=_cp("arbitrary", "arbitrary", "arbitrary"),
        name="attn_a_fwd",
    )(qa, qa, ka, vat, *shards)
    return res[0], res[1], res[2:]


FFN_BLOCKS_PER_STEP = 8
FFN_BWD_TOKENS = 256
QB_PER_STEP = 16


def _bias_variant(n, nb):
    return jnp.where(n == 0, 1, jnp.where(n == nb - 1, 2, 0))


def _sink_row(sink_ref, g):
    return jnp.concatenate([jnp.full((1, BLOCK), sink_ref[0, g * GROUP + h], F32) for h in range(GROUP)], axis=1)


def _attn_b_fwd(qb, kb, vbt, bias_t, sink, s_len):
    bl, kv, sp, _ = kb.shape
    hb = qb.shape[1] // HEAD_DIM
    vt_rows = vbt.shape[2]
    nb = s_len // BLOCK
    nbs = min(QB_PER_STEP, nb)
    r = GROUP * BLOCK

    def body(q_ref, k_ref, vt_ref, bt_ref, sink_ref, o_ref, l_ref, st_sc, pb_sc):
        g, n0 = pl.program_id(1), pl.program_id(2) * nbs
        sink_row = _sink_row(sink_ref, g)

        def span(j):
            return pl.ds(pl.multiple_of((n0 + j) * BLOCK, BLOCK), SPAN)

        for j in range(nbs):
            qt = _heads_t(q_ref[j * BLOCK:(j + 1) * BLOCK, :]).astype(BF16)
            st_sc[j] = _dot(k_ref[0, 0, span(j), :], qt) + bt_ref[_bias_variant(n0 + j, nb), 0]
        maxes = []
        for j in range(nbs):
            st = st_sc[j]
            m = jnp.maximum(jnp.max(st, axis=0, keepdims=True), sink_row)
            pb_sc[j] = jnp.exp(st - m).astype(BF16)
            maxes.append(m)
        for j in range(nbs):
            m = maxes[j]
            acc = _dot(vt_ref[0, 0, :, span(j)], pb_sc[j])
            l = acc[HEAD_DIM:HEAD_DIM + 1, :] + jnp.exp(sink_row - m)
            o_ref[j * BLOCK:(j + 1) * BLOCK, :] = _heads_t_inv(acc[0:HEAD_DIM, :] / l).astype(BF16)
            l_ref[0, 0, j] = jnp.broadcast_to(m + jnp.log(l), (8, r))

    return pl.pallas_call(
        body,
        grid=(bl, kv, nb // nbs),
        in_specs=[pl.BlockSpec((nbs * BLOCK, GROUP * HEAD_DIM), lambda b, g, n: (b * (nb // nbs) + n, g)),
                  pl.BlockSpec((1, 1, sp, HEAD_DIM), lambda b, g, n: (b, g, 0, 0)),
                  pl.BlockSpec((1, 1, vt_rows, sp), lambda b, g, n: (b, g, 0, 0)),
                  pl.BlockSpec((3, 1, SPAN, r), lambda b, g, n: (0, g, 0, 0)),
                  pl.BlockSpec(memory_space=pltpu.SMEM)],
        out_specs=[pl.BlockSpec((nbs * BLOCK, GROUP * HEAD_DIM), lambda b, g, n: (b * (nb // nbs) + n, g)),
                   pl.BlockSpec((1, 1, nbs, 8, r), lambda b, g, n: (b, g, n, 0, 0))],
        out_shape=[SDS((bl * s_len, hb * HEAD_DIM), BF16), SDS((bl, kv, nb, 8, r), F32)],
        scratch_shapes=[pltpu.VMEM((nbs, SPAN, r), F32), pltpu.VMEM((nbs, SPAN, r), BF16)],
        compiler_params=_cp("parallel", "parallel", "arbitrary"),
        name="attn_b_fwd",
    )(qb, kb, vbt, bias_t, sink)


def _mixout(oa, ob, wo, x2, g2, g3, tm):
    t, d = x2.shape
    ca = oa.shape[1]

    def body(oa_ref, ob_ref, w_ref, x_ref, g2_ref, g3_ref, mix_ref, x1_ref, h2_ref):
        mix = _dot(oa_ref[...], w_ref[0:ca, :]) + _dot(ob_ref[...], w_ref[ca:, :])
        mix_ref[...] = mix
        y2, _, _ = _rms_fwd(mix, g2_ref[...])
        x1 = x_ref[...] + y2
        x1_ref[...] = x1
        y3, _, _ = _rms_fwd(x1, g3_ref[...])
        h2_ref[...] = y3.astype(BF16)

    tile = lambda w: pl.BlockSpec((tm, w), lambda i: (i, 0))
    vec = pl.BlockSpec((1, d), lambda i: (0, 0))
    return pl.pallas_call(
        body,
        grid=(t // tm,),
        in_specs=[tile(ca), tile(ob.shape[1]), pl.BlockSpec(wo.shape, lambda i: (0, 0)), tile(d), vec, vec],
        out_specs=[tile(d), tile(d), tile(d)],
        out_shape=[SDS((t, d), F32), SDS((t, d), F32), SDS((t, d), BF16)],
        compiler_params=_cp("parallel"),
        name="mixout",
    )(oa, ob, wo, x2, g2, g3)


def _ffn_fwd(h2, wup_g, wdn, x1, target, g4, tm, jb):
    t, d = x1.shape
    nblk, _, tf = wup_g.shape
    ff = nblk * tf
    nt = t // tm
    nj = nblk // jb

    def body(h_ref, wu_ref, wd_ref, x1_ref, tg_ref, g_ref, u_ref, df_ref, dy_ref, dg_ref, loss_ref, acc_sc):
        i, j = pl.program_id(0), pl.program_id(1)

        @pl.when(j == 0)
        def _():
            acc_sc[...] = jnp.zeros_like(acc_sc)

        @pl.when((i == 0) & (j == 0))
        def _():
            dg_ref[...] = jnp.zeros_like(dg_ref)
            loss_ref[...] = jnp.zeros_like(loss_ref)

        h = h_ref[...]
        squares = []
        for s in range(jb):
            u = jnp.maximum(_dot(h, wu_ref[s]), 0.0)
            u_ref[:, s * tf:(s + 1) * tf] = u.astype(BF16)
            squares.append((u * u).astype(BF16))
        acc_sc[...] += _dot(jnp.concatenate(squares, axis=1), wd_ref[...])

        @pl.when(j == nj - 1)
        def _():
            g = g_ref[...]
            y4, n, r = _rms_fwd(acc_sc[...], g)
            e = (x1_ref[...] + y4) - tg_ref[...]
            loss_ref[...] += jnp.sum(e * e) * (0.5 / d)
            dy = e * (1.0 / d)
            dy_ref[...] = dy
            df, dgt = _rms_bwd(n, r, g, dy)
            df_ref[...] = df.astype(BF16)
            dg_ref[0:1, :] += jnp.sum(dgt, axis=0, keepdims=True)

    tile = pl.BlockSpec((tm, d), lambda i, j: (i, 0))
    return pl.pallas_call(
        body,
        grid=(nt, nj),
        in_specs=[tile,
                  pl.BlockSpec((jb, d, tf), lambda i, j: (j, 0, 0)),
                  pl.BlockSpec((jb * tf, d), lambda i, j: (j, 0)),
                  tile, tile,
                  pl.BlockSpec((1, d), lambda i, j: (0, 0))],
        out_specs=[pl.BlockSpec((tm, jb * tf), lambda i, j: (i, j)), tile, tile,
                   pl.BlockSpec((8, d), lambda i, j: (0, 0)),
                   pl.BlockSpec((8, 128), lambda i, j: (0, 0))],
        out_shape=[SDS((t, ff), BF16), SDS((t, d), BF16), SDS((t, d), F32), SDS((8, d), F32), SDS((8, 128), F32)],
        scratch_shapes=[pltpu.VMEM((tm, d), F32)],
        compiler_params=_cp("arbitrary", "arbitrary"),
        name="ffn_fwd",
    )(h2, wup_g, wdn, x1, target, g4)


def _ffn_bwd(df, u, wdn, wup_g, x1, dy, mix, g3, g2, tm, jb):
    t, d = x1.shape
    nblk, _, tf = wup_g.shape
    nt = t // tm
    nj = nblk // jb

    def body(df_ref, u_ref, wd_ref, wu_ref, x1_ref, dy_ref, mix_ref, g3_ref, g2_ref,
             dpre_ref, dx1_ref, dmix_ref, dg3_ref, dg2_ref, acc_sc):
        i, j = pl.program_id(0), pl.program_id(1)

        @pl.when(j == 0)
        def _():
            acc_sc[...] = jnp.zeros_like(acc_sc)

        @pl.when((i == 0) & (j == 0))
        def _():
            dg3_ref[...] = jnp.zeros_like(dg3_ref)
            dg2_ref[...] = jnp.zeros_like(dg2_ref)

        du2 = _dot_nt(df_ref[...], wd_ref[...])
        dpre = (2.0 * u_ref[...].astype(F32) * du2).astype(BF16)
        dpre_ref[...] = dpre
        dh = _dot_nt(dpre[:, 0:tf], wu_ref[0])
        for s in range(1, jb):
            dh = dh + _dot_nt(dpre[:, s * tf:(s + 1) * tf], wu_ref[s])
        acc_sc[...] += dh

        @pl.when(j == nj - 1)
        def _():
            g3, g2 = g3_ref[...], g2_ref[...]
            _, n3, r3 = _rms_fwd(x1_ref[...], g3)
            dx, dgt3 = _rms_bwd(n3, r3, g3, acc_sc[...])
            dx1 = dy_ref[...] + dx
            dx1_ref[...] = dx1
            dg3_ref[0:1, :] += jnp.sum(dgt3, axis=0, keepdims=True)
            _, n2, r2 = _rms_fwd(mix_ref[...], g2)
            dmix, dgt2 = _rms_bwd(n2, r2, g2, dx1)
            dmix_ref[...] = dmix.astype(BF16)
            dg2_ref[0:1, :] += jnp.sum(dgt2, axis=0, keepdims=True)

    tile = pl.BlockSpec((tm, d), lambda i, j: (i, 0))
    vec = pl.BlockSpec((1, d), lambda i, j: (0, 0))
    acc8 = pl.BlockSpec((8, d), lambda i, j: (0, 0))
    return pl.pallas_call(
        body,
        grid=(nt, nj),
        in_specs=[tile,
                  pl.BlockSpec((tm, jb * tf), lambda i, j: (i, j)),
                  pl.BlockSpec((jb * tf, d), lambda i, j: (j, 0)),
                  pl.BlockSpec((jb, d, tf), lambda i, j: (j, 0, 0)),
                  tile, tile, tile, vec, vec],
        out_specs=[pl.BlockSpec((tm, jb * tf), lambda i, j: (i, j)), tile, tile, acc8, acc8],
        out_shape=[SDS(u.shape, BF16), SDS((t, d), F32), SDS((t, d), BF16), SDS((8, d), F32), SDS((8, d), F32)],
        scratch_shapes=[pltpu.VMEM((tm, d), F32)],
        compiler_params=_cp("arbitrary", "arbitrary"),
        name="ffn_bwd",
    )(df, u, wdn, wup_g, x1, dy, mix, g3, g2)


def _wgrad(a, b, a_spec, b_spec, out_block, out_shape, nj, nk, name, prep_a=None, prep_b=None):
    acc_shape = out_block[1:]

    def body(a_ref, b_ref, o_ref, acc_sc):
        k = pl.program_id(1)
        av = a_ref[...] if prep_a is None else prep_a(a_ref)
        bv = b_ref[...] if prep_b is None else prep_b(b_ref)
        part = _dot_tn(av, bv)

        @pl.when(k == 0)
        def _():
            acc_sc[...] = part

        @pl.when(k > 0)
        def _():
            acc_sc[...] += part

        @pl.when(k == nk - 1)
        def _():
            o_ref[0] = acc_sc[...].astype(BF16)

    return pl.pallas_call(
        body,
        grid=(nj, nk),
        in_specs=[a_spec, b_spec],
        out_specs=pl.BlockSpec(out_block, lambda j, k: (j, 0, 0)),
        out_shape=SDS(out_shape, BF16),
        scratch_shapes=[pltpu.VMEM(acc_shape, F32)],
        compiler_params=_cp("parallel", "arbitrary"),
        name=name,
    )(a, b)


def _wgrad_cols(a, b, nj, tt, name):
    t, m = a.shape
    bn = b.shape[1] // nj
    return _wgrad(a, b, pl.BlockSpec((tt, m), lambda j, k: (k, 0)), pl.BlockSpec((tt, bn), lambda j, k: (k, j)),
                  (1, m, bn), (nj, m, bn), nj, t // tt, name)


def _wgrad_rows(a, b, nj, tt, name, square=False):
    t, n = b.shape
    bm = a.shape[1] // nj

    def squared(a_ref):
        af = a_ref[...].astype(F32)
        return (af * af).astype(BF16)

    return _wgrad(a, b, pl.BlockSpec((tt, bm), lambda j, k: (k, j)), pl.BlockSpec((tt, n), lambda j, k: (k, 0)),
                  (1, bm, n), (nj, bm, n), nj, t // tt, name, prep_a=squared if square else None)


def _wgrad_o(oa, ob, dmix, nj, tt):
    t, n = dmix.shape
    ca, cb = oa.shape[1], ob.shape[1]
    m = ca + cb
    nk = t // tt

    def body(oa_ref, ob_ref, b_ref, o_ref, acc_sc):
        k = pl.program_id(0)
        part = _dot_tn(jnp.concatenate([oa_ref[...], ob_ref[...]], axis=1), b_ref[...])

        @pl.when(k == 0)
        def _():
            acc_sc[...] = part

        @pl.when(k > 0)
        def _():
            acc_sc[...] += part

        @pl.when(k == nk - 1)
        def _():
            o_ref[...] = acc_sc[...].reshape(nj, m // nj, n).astype(BF16)

    return pl.pallas_call(
        body,
        grid=(nk,),
        in_specs=[pl.BlockSpec((tt, ca), lambda k: (k, 0)), pl.BlockSpec((tt, cb), lambda k: (k, 0)),
                  pl.BlockSpec((tt, n), lambda k: (k, 0))],
        out_specs=pl.BlockSpec((nj, m // nj, n), lambda k: (0, 0, 0)),
        out_shape=SDS((nj, m // nj, n), BF16),
        scratch_shapes=[pltpu.VMEM((m, n), F32)],
        compiler_params=_cp("arbitrary"),
        name="wgrad_o",
    )(oa, ob, dmix)


def _attn_out_bwd(dmix, wo, ca, tm):
    t, d = dmix.shape
    cb = wo.shape[0] - ca

    def body(dm_ref, w_ref, da_ref, db_ref):
        dm = dm_ref[...]
        da_ref[...] = _dot_nt(dm, w_ref[0:ca, :]).astype(BF16)
        db_ref[...] = _dot_nt(dm, w_ref[ca:, :]).astype(BF16)

    return pl.pallas_call(
        body,
        grid=(t // tm,),
        in_specs=[pl.BlockSpec((tm, d), lambda i: (i, 0)), pl.BlockSpec(wo.shape, lambda i: (0, 0))],
        out_specs=[pl.BlockSpec((tm, ca), lambda i: (i, 0)), pl.BlockSpec((tm, cb), lambda i: (i, 0))],
        out_shape=[SDS((t, ca), BF16), SDS((t, cb), BF16)],
        compiler_params=_cp("parallel"),
        name="attn_out_bwd",
    )(dmix, wo)


def _heads_t(x):
    xt = x.astype(F32).T
    return jnp.concatenate([xt[h * HEAD_DIM:(h + 1) * HEAD_DIM, :] for h in range(GROUP)], axis=1)


def _heads_t_inv(yt):
    n = yt.shape[1] // GROUP
    return jnp.concatenate([yt[:, h * n:(h + 1) * n] for h in range(GROUP)], axis=0).T


def _attn_a_bwd(qa, ka, kat, va, do, o, lse, tq, tk, grads):
    bl, kv, s_len, _ = ka.shape
    nq, nk = s_len // tq, s_len // tk
    assert nk % 2 == 0
    r = GROUP * tq
    ng = len(grads)

    def body(q_ref, qn_ref, k_ref, kt_ref, v_ref, do_ref, don_ref, o_ref, l_ref, *rest):
        grad_refs, (dq_ref, dk_ref, dv_ref), parts = rest[:ng], rest[ng:ng + 3], rest[ng + 3:2 * ng + 3]
        st_sc, dp_sc, dkt_sc, dvt_sc, send_sems, recv_sems, local_sems = rest[2 * ng + 3:]
        i = pl.program_id(2)
        step_id = (pl.program_id(0) * kv + pl.program_id(1)) * nq + i
        start, wait = _direct_exchange("scatter", grad_refs, parts, send_sems, recv_sems, local_sems)
        pl.when(step_id == 0)(start)

        dot32 = _heads_t(do_ref[...])
        drow = jnp.sum(dot32 * _heads_t(o_ref[...]), axis=0, keepdims=True)
        qt, dot = _heads_t(q_ref[...]).astype(BF16), dot32.astype(BF16)
        lrow = l_ref[0, 0, 0, 0:1, :]

        @pl.when(i == 0)
        def _():
            dkt_sc[...] = jnp.zeros_like(dkt_sc)
            dvt_sc[...] = jnp.zeros_like(dvt_sc)

        def chunk(c):
            return pl.ds(pl.multiple_of(c * tk, tk), tk)

        def scores(c, slot, qv=qt, dov=dot):
            st_sc[slot] = _dot(k_ref[0, 0, chunk(c), :], qv)
            dp_sc[slot] = _dot(v_ref[0, 0, chunk(c), :], dov)

        def fold(slot, c, dqt):
            pt = jnp.exp(st_sc[slot] - lrow)
            dsb = (pt * (dp_sc[slot] - drow)).astype(BF16)
            dvt_sc[:, chunk(c)] += _dot_nt(dot, pt.astype(BF16))
            dkt_sc[:, chunk(c)] += _dot_nt(qt, dsb)
            return dqt + _dot(kt_ref[0, 0, :, chunk(c)], dsb)

        @pl.when(i == 0)
        def _():
            scores(0, 0)

        def step(c2, dqt):
            c = 2 * c2
            scores(c + 1, 1)
            dqt = fold(0, c, dqt)
            scores(c + 2, 0)
            return fold(1, c + 1, dqt)

        dqt = jnp.zeros((HEAD_DIM, r), F32)
        for c2 in range(nk // 2 - 1):
            dqt = step(c2, dqt)
        scores(nk - 1, 1)
        dqt = fold(0, nk - 2, dqt)
        scores(0, 0, _heads_t(qn_ref[...]).astype(BF16), _heads_t(don_ref[...]).astype(BF16))
        dq_ref[...] = _heads_t_inv(fold(1, nk - 1, dqt))

        @pl.when(i == nq - 1)
        def _():
            dk_ref[0, 0] = dkt_sc[...].T
            dv_ref[0, 0] = dvt_sc[...].T

        pl.when(step_id == bl * kv * nq - 1)(wait)

    kvspec = pl.BlockSpec((1, 1, s_len, HEAD_DIM), lambda b, g, i: (b, g, 0, 0))
    tok = pl.BlockSpec((tq, GROUP * HEAD_DIM), lambda b, g, i: (b * nq + i, g))
    toknext = pl.BlockSpec((tq, GROUP * HEAD_DIM), lambda b, g, i: (b * nq + jnp.minimum(i + 1, nq - 1), g))
    anyspec = pl.BlockSpec(memory_space=pl.ANY)
    res = pl.pallas_call(
        body,
        grid=(bl, kv, nq),
        in_specs=[tok, toknext, kvspec, pl.BlockSpec((1, 1, HEAD_DIM, s_len), lambda b, g, i: (b, g, 0, 0)), kvspec,
                  tok, toknext, tok, pl.BlockSpec((1, 1, 1, 8, r), lambda b, g, i: (b, g, i, 0, 0))] + [anyspec] * ng,
        out_specs=[tok, kvspec, kvspec] + [anyspec] * ng,
        out_shape=[SDS(qa.shape, F32), SDS(ka.shape, F32), SDS(va.shape, F32)]
        + [SDS(g.shape, g.dtype) for g in grads],
        scratch_shapes=[pltpu.VMEM((2, tk, r), F32), pltpu.VMEM((2, tk, r), F32),
                        pltpu.VMEM((HEAD_DIM, s_len), F32), pltpu.VMEM((HEAD_DIM, s_len), F32)]
        + _exchange_scratch(ng),
        compiler_params=_cp("arbitrary", "arbitrary", "arbitrary"),
        name="attn_a_bwd",
    )(qa, qa, ka, kat, va, do, do, o, lse, *grads)
    return res[0], res[1], res[2], res[3:]


def _attn_b_bwd(qb, kb, kbt, vb, do, o, lse, bias_t, sink, s_len):
    bl, kv, sp, _ = kb.shape
    nb = s_len // BLOCK
    nbs = min(QB_PER_STEP, nb)
    r = GROUP * BLOCK

    def body(q_ref, k_ref, kt_ref, v_ref, do_ref, o_ref, l_ref, bt_ref, sink_ref,
             dq_ref, dk_ref, dv_ref, dsum_ref, dsink_ref, dkt_sc, dvt_sc):
        g, b, ns = pl.program_id(0), pl.program_id(1), pl.program_id(2)
        sink_row = _sink_row(sink_ref, g)

        @pl.when(ns == 0)
        def _():
            dkt_sc[...] = jnp.zeros_like(dkt_sc)
            dvt_sc[...] = jnp.zeros_like(dvt_sc)

        @pl.when((b == 0) & (ns == 0))
        def _():
            dsum_ref[...] = jnp.zeros_like(dsum_ref)
            dsink_ref[...] = jnp.zeros_like(dsink_ref)

        dsum = jnp.zeros((SPAN, r), F32)
        dsink = jnp.zeros((1, r), F32)
        for j in range(nbs):
            n = ns * nbs + j
            span = pl.ds(pl.multiple_of(n * BLOCK, BLOCK), SPAN)
            rows = slice(j * BLOCK, (j + 1) * BLOCK)
            dot32 = _heads_t(do_ref[rows, :])
            drow = jnp.sum(dot32 * _heads_t(o_ref[rows, :]), axis=0, keepdims=True)
            qt, dot = _heads_t(q_ref[rows, :]).astype(BF16), dot32.astype(BF16)
            lrow = l_ref[0, 0, j, 0:1, :]
            st = _dot(k_ref[0, 0, span, :], qt) + bt_ref[_bias_variant(n, nb), 0]
            pt = jnp.exp(st - lrow)
            dst = pt * (_dot(v_ref[0, 0, span, :], dot) - drow)
            dsum = dsum + dst
            dsink = dsink - jnp.exp(sink_row - lrow) * drow
            dsb = dst.astype(BF16)
            dvt_sc[:, span] += _dot_nt(dot, pt.astype(BF16))
            dkt_sc[:, span] += _dot_nt(qt, dsb)
            dq_ref[rows, :] = _heads_t_inv(_dot(kt_ref[0, 0, :, span], dsb))
        dsum_ref[0] += dsum
        dsink_ref[0, 0:1, :] += dsink

        @pl.when(ns == nb // nbs - 1)
        def _():
            dk_ref[0, 0] = dkt_sc[:, BLOCK:BLOCK + s_len].T
            dv_ref[0, 0] = dvt_sc[:, BLOCK:BLOCK + s_len].T

    kvspec = pl.BlockSpec((1, 1, sp, HEAD_DIM), lambda g, b, n: (b, g, 0, 0))
    kvout = pl.BlockSpec((1, 1, s_len, HEAD_DIM), lambda g, b, n: (b, g, 0, 0))
    tok = pl.BlockSpec((nbs * BLOCK, GROUP * HEAD_DIM), lambda g, b, n: (b * (nb // nbs) + n, g))
    return pl.pallas_call(
        body,
        grid=(kv, bl, nb // nbs),
        in_specs=[tok, kvspec, pl.BlockSpec((1, 1, HEAD_DIM, sp), lambda g, b, n: (b, g, 0, 0)), kvspec, tok, tok,
                  pl.BlockSpec((1, 1, nbs, 8, r), lambda g, b, n: (b, g, n, 0, 0)),
                  pl.BlockSpec((3, 1, SPAN, r), lambda g, b, n: (0, g, 0, 0)),
                  pl.BlockSpec(memory_space=pltpu.SMEM)],
        out_specs=[tok, kvout, kvout,
                   pl.BlockSpec((1, SPAN, r), lambda g, b, n: (g, 0, 0)),
                   pl.BlockSpec((1, 8, r), lambda g, b, n: (g, 0, 0))],
        out_shape=[SDS(qb.shape, F32), SDS((bl, kv, s_len, HEAD_DIM), F32), SDS((bl, kv, s_len, HEAD_DIM), F32),
                   SDS((kv, SPAN, r), F32), SDS((kv, 8, r), F32)],
        scratch_shapes=[pltpu.VMEM((HEAD_DIM, sp), F32), pltpu.VMEM((HEAD_DIM, sp), F32)],
        compiler_params=_cp("arbitrary", "arbitrary", "arbitrary"),
        name="attn_b_bwd",
    )(qb, kb, kbt, vb, do, o, lse, bias_t, sink)


def _bias_reduce(dsum, dsink, bucket_t4):
    kv, _, r = dsum.shape

    def body(ds_ref, dk_ref, bk_ref, rel_ref, sink_ref):
        lane = lax.broadcasted_iota(jnp.int32, (N_BUCKETS, 128), 1)
        lane8 = lax.broadcasted_iota(jnp.int32, (8, 128), 1)
        bk = bk_ref[...]
        for g in range(kv):
            ds = ds_ref[g]
            rowi = lax.broadcasted_iota(jnp.int32, (N_BUCKETS, r), 0)
            red = jnp.zeros((N_BUCKETS, r), F32)
            for b in range(N_BUCKETS):
                red = jnp.where(rowi == b, jnp.sum(jnp.where(bk == b, ds, 0.0), axis=0, keepdims=True), red)
            out = jnp.zeros((N_BUCKETS, 128), F32)
            so = jnp.zeros((8, 128), F32)
            for h in range(GROUP):
                col = jnp.sum(red[:, h * BLOCK:(h + 1) * BLOCK], axis=1, keepdims=True)
                out = jnp.where(lane == h, col, out)
                sc = jnp.sum(dk_ref[g][:, h * BLOCK:(h + 1) * BLOCK], axis=1, keepdims=True)
                so = jnp.where(lane8 == h, sc, so)
            rel_ref[g] = out
            sink_ref[g] = so

    vm = pl.BlockSpec(memory_space=pltpu.VMEM)
    return pl.pallas_call(
        body,
        in_specs=[vm, vm, vm],
        out_specs=[vm, vm],
        out_shape=[SDS((kv, N_BUCKETS, 128), F32), SDS((kv, 8, 128), F32)],
        name="bias_reduce",
    )(dsum, dsink, bucket_t4)


def _dqkprep(dqa, dka, dva, dqb, dkb, dvb, proj, cos, sin_signed, gq, gk, s_len, ts):
    t, p_cols = proj.shape
    bl, kva, kvb = dka.shape[0], dka.shape[1], dkb.shape[1]
    ha, hb = dqa.shape[1] // HEAD_DIM, dqb.shape[1] // HEAD_DIM
    ns = s_len // ts

    def body(dqa_ref, dka_ref, dva_ref, dqb_ref, dkb_ref, dvb_ref, p_ref, cos_ref, sin_ref, gq_ref, gk_ref,
             dp_ref, dgq_ref, dgk_ref):
        b, i = pl.program_id(0), pl.program_id(1)
        cs, sn = cos_ref[...], sin_ref[...]
        low, first = _pair_masks(ts)

        @pl.when((b == 0) & (i == 0))
        def _():
            dgq_ref[...] = jnp.zeros_like(dgq_ref)
            dgk_ref[...] = jnp.zeros_like(dgk_ref)

        def grad_pair(ref, p):
            return jnp.concatenate([ref[0, 2 * p], ref[0, 2 * p + 1]], axis=1)

        def put(p, val):
            dp_ref[:, p * PAIR:(p + 1) * PAIR] = val.astype(BF16)

        def unrope_norm(d_rot, p, g, dg_ref):
            dn = d_rot * cs + _pair_partner(d_rot * sn, first)
            xp = p_ref[:, p * PAIR:(p + 1) * PAIR]
            r = lax.rsqrt(_pair_mean(xp * xp, low) + EPS)
            n = xp * r
            gd = g * dn
            dg_ref[0:1, :] += jnp.sum(dn * n, axis=0, keepdims=True)
            put(p, r * (gd - n * _pair_mean(n * gd, low)))

        for p in range(ha // 2):
            unrope_norm(dqa_ref[:, p * PAIR:(p + 1) * PAIR] * SCALE, p, gq_ref[...], dgq_ref)
        base = ha // 2
        for p in range(kva // 2):
            unrope_norm(grad_pair(dka_ref, p), base + p, gk_ref[...], dgk_ref)
            put(base + kva // 2 + p, grad_pair(dva_ref, p))
        base += kva
        for p in range(hb // 2):
            put(base + p, dqb_ref[:, p * PAIR:(p + 1) * PAIR] * SCALE)
        base += hb // 2
        for p in range(kvb // 2):
            put(base + p, grad_pair(dkb_ref, p))
            put(base + kvb // 2 + p, grad_pair(dvb_ref, p))

    def hm(nh):
        return pl.BlockSpec((1, nh, ts, HEAD_DIM), lambda b, i: (b, 0, i, 0))

    def tokmajor(nh):
        return pl.BlockSpec((ts, nh * HEAD_DIM), lambda b, i: (b * ns + i, 0))

    vec = pl.BlockSpec((1, PAIR), lambda b, i: (0, 0))
    tab = pl.BlockSpec((ts, PAIR), lambda b, i: (i, 0))
    acc = pl.BlockSpec((8, PAIR), lambda b, i: (0, 0))
    pspec = pl.BlockSpec((ts, p_cols), lambda b, i: (b * ns + i, 0))
    return pl.pallas_call(
        body,
        grid=(bl, ns),
        in_specs=[tokmajor(ha), hm(kva), hm(kva), tokmajor(hb), hm(kvb), hm(kvb), pspec, tab, tab, vec, vec],
        out_specs=[pspec, acc, acc],
        out_shape=[SDS((t, p_cols), BF16), SDS((8, PAIR), F32), SDS((8, PAIR), F32)],
        compiler_params=_cp("arbitrary", "arbitrary"),
        name="dqkprep",
    )(dqa, dka, dva, dqb, dkb, dvb, proj, cos, sin_signed, gq, gk)


def _dx_final(dproj, w_t, x2, dx1, g1, tm, grads):
    t, d = x2.shape
    p_cols = w_t.shape[0]
    ng = len(grads)
    nsteps = t // tm

    def body(dp_ref, w_ref, x_ref, dx1_ref, g_ref, *rest):
        grad_refs, (dx_ref, dg_ref), parts = rest[:ng], rest[ng:ng + 2], rest[ng + 2:2 * ng + 2]
        start, wait = _direct_exchange("scatter", grad_refs, parts, *rest[2 * ng + 2:])

        @pl.when(pl.program_id(0) == 0)
        def _():
            start()
            dg_ref[...] = jnp.zeros_like(dg_ref)

        dh = _dot(dp_ref[...], w_ref[...])
        g = g_ref[...]
        _, n, r = _rms_fwd(x_ref[...], g)
        dx, dgt = _rms_bwd(n, r, g, dh)
        dx_ref[...] = dx1_ref[...] + dx
        dg_ref[0:1, :] += jnp.sum(dgt, axis=0, keepdims=True)
        pl.when(pl.program_id(0) == nsteps - 1)(wait)

    tile = pl.BlockSpec((tm, d), lambda i: (i, 0))
    anyspec = pl.BlockSpec(memory_space=pl.ANY)
    res = pl.pallas_call(
        body,
        grid=(nsteps,),
        in_specs=[pl.BlockSpec((tm, p_cols), lambda i: (i, 0)),
                  pl.BlockSpec((p_cols, d), lambda i: (0, 0)),
                  tile, tile, pl.BlockSpec((1, d), lambda i: (0, 0))] + [anyspec] * ng,
        out_specs=[tile, pl.BlockSpec((8, d), lambda i: (0, 0))] + [anyspec] * ng,
        out_shape=[SDS((t, d), F32), SDS((8, d), F32)] + [SDS(g.shape, g.dtype) for g in grads],
        scratch_shapes=_exchange_scratch(ng),
        compiler_params=_cp("arbitrary"),
        name="dx_final",
    )(dproj, w_t, x2, dx1, g1, *grads)
    return res[0], res[1], res[2:]


def _adamw_math(w, g, m, v):
    m = ADAM_B1 * m + (1.0 - ADAM_B1) * g
    v = ADAM_B2 * v + (1.0 - ADAM_B2) * (g * g)
    m_hat = m / (1.0 - ADAM_B1 ** ADAM_STEP)
    v_hat = v / (1.0 - ADAM_B2 ** ADAM_STEP)
    delta = -ADAM_LR * (m_hat / (jnp.sqrt(v_hat) + ADAM_EPS) + ADAM_WD * w)
    return delta, m, v


def _adamw_sum(parts, w, m, v, tr, name):
    rows, cols = w.shape

    def body(p_ref, w_ref, m_ref, v_ref, g_ref, d_ref, nm_ref, nv_ref):
        g = p_ref[0].astype(F32)
        for s in range(1, N_DEV):
            g = g + p_ref[s].astype(F32)
        g_ref[...] = g
        d_ref[...], nm_ref[...], nv_ref[...] = _adamw_math(w_ref[...], g, m_ref[...], v_ref[...])

    tr = min(tr, rows)
    tile = pl.BlockSpec((tr, cols), lambda i: (i, 0))
    return pl.pallas_call(
        body,
        grid=(rows // tr,),
        in_specs=[pl.BlockSpec((N_DEV, tr, cols), lambda i: (0, i, 0)), tile, tile, tile],
        out_specs=[tile] * 4,
        out_shape=[SDS((rows, cols), F32)] * 4,
        compiler_params=_cp("parallel"),
        name=name,
    )(parts, w, m, v)


def _adamw_small(vec, rel, ws, ms, vs):
    hb = ws[6].shape[1]
    n = len(ws)

    def body(vec_ref, rel_ref, *rest):
        w_refs, m_refs, v_refs = rest[:n], rest[n:2 * n], rest[2 * n:3 * n]
        loss_ref, outs = rest[3 * n], rest[3 * n + 1:]
        grads = [vec_ref[0:1, :], vec_ref[1:2, :], vec_ref[2:3, :], vec_ref[3:4, :],
                 vec_ref[4:5, 0:HEAD_DIM], vec_ref[4:5, SMALL_LANES:SMALL_LANES + HEAD_DIM],
                 vec_ref[4:5, 2 * SMALL_LANES:2 * SMALL_LANES + hb], rel_ref[:, 0:hb]]
        loss_ref[...] = vec_ref[4:5, 3 * SMALL_LANES:3 * SMALL_LANES + 1]
        for p, g in enumerate(grads):
            g_ref, d_ref, nm_ref, nv_ref = outs[4 * p:4 * p + 4]
            g_ref[...] = g
            d_ref[...], nm_ref[...], nv_ref[...] = _adamw_math(w_refs[p][...], g, m_refs[p][...], v_refs[p][...])

    vm = pl.BlockSpec(memory_space=pltpu.VMEM)
    res = pl.pallas_call(
        body,
        in_specs=[vm] * (2 + 3 * n),
        out_specs=[vm] * (1 + 4 * n),
        out_shape=[SDS((1, 1), F32)] + [SDS(w.shape, F32) for w in ws for _ in range(4)],
        name="adamw_small",
    )(vec, rel, *ws, *ms, *vs)
    return res[0], [res[1 + 4 * p:5 + 4 * p] for p in range(n)]


def _local_step(x, loss_target, win_s, wo_s, wup_s, wdn_s, g_pre_mix, g_post_mix, q_norm_a, k_norm_a, sink_b,
                rel_bias, g_pre_ffn, g_post_ffn):
    bl, s_len, d = x.shape
    t = bl * s_len
    nh = d // HEAD_DIM
    ha = nh // 2
    kva = ha // GROUP
    hb = nh - ha
    kvb = hb // GROUP
    tm = 512
    tp = min(1024, t)
    tw = min(4096, t)
    ts = min(512, s_len)
    tq, tk = 2 * BLOCK, min(512, s_len // 2)

    x2 = x.reshape(t, d)
    tg2 = loss_target.reshape(t, d)
    cos, sin_signed = _rope_tables(s_len)
    gq2, gk2 = jnp.tile(q_norm_a, (1, 2)), jnp.tile(k_norm_a, (1, 2))
    a = jnp.arange(BLOCK, dtype=jnp.int32)
    c = jnp.arange(SPAN, dtype=jnp.int32)
    bucket_t = _t5_bucket(c[:, None] - BLOCK - a[None, :])
    bucket_t4 = jnp.tile(bucket_t, (1, GROUP))
    (win_g,), bias_t = _weight_gather([win_s], bucket_t, rel_bias)
    w_in_t = win_g.reshape(-1, d)
    p_cols = w_in_t.shape[0]

    h1, proj = _inproj(x2, g_pre_mix, w_in_t, tp)
    qa, ka, kat, va, vat, qb, kb, kbt, vb, vbt = _qkprep(
        proj, cos, sin_signed, gq2, gk2, bl, s_len, ha, kva, hb, kvb, ts)
    oa, lse_a, (wo_g, wup_g, wdn_g) = _attn_a_fwd(qa, ka, vat, tq, tk, [wo_s, wup_s, wdn_s])
    wo = wo_g.reshape(-1, d)
    wdn = wdn_g.reshape(-1, d)
    ob, lse_b = _attn_b_fwd(qb, kb, vbt, bias_t, sink_b, s_len)
    mix, x1, h2 = _mixout(oa, ob, wo, x2, g_post_mix, g_pre_ffn, tp)
    u, df, dy, dg4, loss8 = _ffn_fwd(h2, wup_g, wdn, x1, tg2, g_post_ffn, tm, FFN_BLOCKS_PER_STEP)

    dpre, dx1, dmix, dg3, dg2 = _ffn_bwd(df, u, wdn, wup_g, x1, dy, mix, g_pre_ffn, g_post_mix, FFN_BWD_TOKENS,
                                         FFN_BLOCKS_PER_STEP)
    gw_dn = _wgrad_rows(u, df, N_DEV, tw, "wgrad_down", square=True)
    gw_up = _wgrad_cols(h2, dpre, N_DEV, tw, "wgrad_up")
    gw_o = _wgrad_o(oa, ob, dmix, N_DEV, min(2048, t))
    doa, dob = _attn_out_bwd(dmix, wo, oa.shape[1], tp)
    dqa, dka, dva, (p_o, p_up, p_dn) = _attn_a_bwd(qa, ka, kat, va, doa, oa, lse_a, tq, tk, [gw_o, gw_up, gw_dn])
    dqb, dkb, dvb, dsum, dsink = _attn_b_bwd(qb, kb, kbt, vb, dob, ob, lse_b, bias_t, sink_b, s_len)
    drel_g, dsink_g = _bias_reduce(dsum, dsink, bucket_t4)
    dproj, dgq, dgk = _dqkprep(dqa, dka, dva, dqb, dkb, dvb, proj, cos, sin_signed, gq2, gk2, s_len, ts)
    gw_in_t = _wgrad_rows(dproj, h1, p_cols // 256, tw, "wgrad_in").reshape(N_DEV, -1, d)
    grad_x, dg1, (p_in,) = _dx_final(dproj, w_in_t, x2, dx1, g_pre_mix, tp, [gw_in_t])

    vec, rel = _small_allreduce([dg1, dg2, dg3, dg4], dgq, dgk, dsink_g, drel_g, loss8)
    return grad_x.reshape(bl, s_len, d), p_in, p_o, p_up, p_dn, vec, rel


def kernel(x, w_in, w_o, g_pre_mix, g_post_mix, q_norm_a, k_norm_a, sink_b, rel_bias, g_pre_ffn, w_ffn_up, w_ffn_down, g_post_ffn, loss_target, m_w_in, m_w_o, m_g_pre_mix, m_g_post_mix, m_q_norm_a, m_k_norm_a, m_sink_b, m_rel_bias, m_g_pre_ffn, m_w_ffn_up, m_w_ffn_down, m_g_post_ffn, v_w_in, v_w_o, v_g_pre_mix, v_g_post_mix, v_q_norm_a, v_k_norm_a, v_sink_b, v_rel_bias, v_g_pre_ffn, v_w_ffn_up, v_w_ffn_down, v_g_post_ffn):
    w_in_t = w_in[0].T

    grad_x, p_in, p_o, p_up, p_dn, vec, rel = _local_step(
        x, loss_target, w_in_t.astype(BF16), w_o[0].astype(BF16), w_ffn_up[0].astype(BF16), w_ffn_down[0].astype(BF16),
        g_pre_mix, g_post_mix, q_norm_a, k_norm_a, sink_b, rel_bias, g_pre_ffn, g_post_ffn)

    big = {
        "w_in": [a.T for a in _adamw_sum(p_in, w_in_t, m_w_in[0].T, v_w_in[0].T, 192, "adamw_in")],
        "w_o": _adamw_sum(p_o, w_o[0], m_w_o[0], v_w_o[0], 128, "adamw_o"),
        "w_up": _adamw_sum(p_up, w_ffn_up[0], m_w_ffn_up[0], v_w_ffn_up[0], 256, "adamw_up"),
        "w_dn": _adamw_sum(p_dn, w_ffn_down[0], m_w_ffn_down[0], v_w_ffn_down[0], 256, "adamw_down"),
    }
    loss, small = _adamw_small(
        vec, rel,
        [g_pre_mix, g_post_mix, g_pre_ffn, g_post_ffn, q_norm_a, k_norm_a, sink_b, rel_bias],
        [m_g_pre_mix, m_g_post_mix, m_g_pre_ffn, m_g_post_ffn, m_q_norm_a, m_k_norm_a, m_sink_b, m_rel_bias],
        [v_g_pre_mix, v_g_post_mix, v_g_pre_ffn, v_g_post_ffn, v_q_norm_a, v_k_norm_a, v_sink_b, v_rel_bias])
    s_pre_mix, s_post_mix, s_pre_ffn, s_post_ffn, s_qn, s_kn, s_sink, s_rel = small

    def outs(kind):
        return [big["w_in"][kind][None], big["w_o"][kind][None], s_pre_mix[kind], s_post_mix[kind], s_qn[kind],
                s_kn[kind], s_sink[kind], s_rel[kind], s_pre_ffn[kind], big["w_up"][kind][None],
                big["w_dn"][kind][None], s_post_ffn[kind]]

    return (loss.reshape(()), grad_x, *outs(0), *outs(1), *outs(2), *outs(3))
```

```python
import jax
import jax.numpy as jnp
import numpy as np
from jax import lax
from jax.experimental import pallas as pl
from jax.experimental.pallas import tpu as pltpu

F32 = jnp.float32
BF16 = jnp.bfloat16
SDS = jax.ShapeDtypeStruct

N_DEV = 8
HEAD_DIM = 64
GROUP = 4
BLOCK = 128
SPAN = 3 * BLOCK
GRID_W = 64
N_BUCKETS = 32
MAX_DISTANCE = 128
ROPE_THETA = 10000.0
EPS = 1e-6
NEG_INF = -1e30
SCALE = HEAD_DIM ** -0.5
VT_PAD = 16

ADAM_LR = 0.001
ADAM_B1 = 0.9
ADAM_B2 = 0.999
ADAM_EPS = 1e-08
ADAM_WD = 0.01
ADAM_STEP = 10

VMEM_LIMIT = 56 * 1024 * 1024
MESH = pl.DeviceIdType.MESH


def _cp(*sem):
    return pltpu.CompilerParams(dimension_semantics=sem, vmem_limit_bytes=VMEM_LIMIT)


def _dot(a, b):
    return jnp.dot(a, b, preferred_element_type=F32)


def _dot_nt(a, b):
    return lax.dot_general(a, b, (((1,), (1,)), ((), ())), preferred_element_type=F32)


def _dot_tn(a, b):
    return lax.dot_general(a, b, (((0,), (0,)), ((), ())), preferred_element_type=F32)


def _rms_fwd(x, g):
    r = lax.rsqrt(jnp.mean(x * x, axis=-1, keepdims=True) + EPS)
    n = x * r
    return n * g, n, r


def _rms_bwd(n, r, g, dy):
    gd = g * dy
    dx = r * (gd - n * jnp.mean(n * gd, axis=-1, keepdims=True))
    return dx, dy * n


def _rope_tables(s_len):
    rows = s_len // GRID_W
    row = np.repeat(np.arange(rows, dtype=np.int32), GRID_W)
    col = np.tile(np.arange(GRID_W, dtype=np.int32), rows)
    nf = HEAD_DIM // 4
    freqs = np.float32(ROPE_THETA) ** (-np.arange(nf, dtype=np.float32) / np.float32(nf))
    ang_r = row.astype(np.float32)[:, None] * freqs[None, :]
    ang_c = col.astype(np.float32)[:, None] * freqs[None, :]
    cr, sr, cc, sc = np.cos(ang_r), np.sin(ang_r), np.cos(ang_c), np.sin(ang_c)
    cos = np.concatenate([cr, cr, cc, cc] * 2, axis=-1).astype(np.float32)
    sin_signed = np.concatenate([-sr, sr, -sc, sc] * 2, axis=-1).astype(np.float32)
    return jnp.asarray(cos), jnp.asarray(sin_signed)


def _t5_bucket(rel):
    nb = N_BUCKETS // 2
    ret = (rel > 0).astype(jnp.int32) * nb
    n = jnp.abs(rel)
    max_exact = nb // 2
    nf = jnp.maximum(n, 1).astype(F32)
    large = max_exact + (jnp.log(nf / max_exact) / np.float32(np.log(MAX_DISTANCE / max_exact))
                         * (nb - max_exact)).astype(jnp.int32)
    large = jnp.minimum(large, nb - 1)
    return ret + jnp.where(n < max_exact, n, large)


def _mesh_pos():
    return lax.axis_index("x"), lax.axis_index("y"), lax.axis_index("c")


def _lin(p):
    return 4 * p[0] + 2 * p[1] + p[2]


def _bias_tables(bkt_ref, tbl_ref, out_ref, hb):
    bkt = bkt_ref[...]
    ci = lax.broadcasted_iota(jnp.int32, (SPAN, BLOCK), 0)
    qi = lax.broadcasted_iota(jnp.int32, (SPAN, BLOCK), 1)
    band = jnp.abs(ci - BLOCK - qi) <= BLOCK
    masks = (band, band & (ci >= BLOCK), band & (ci < 2 * BLOCK))
    for h in range(hb):
        acct = jnp.zeros((SPAN, BLOCK), F32)
        for b in range(N_BUCKETS):
            acct = jnp.where(bkt == b, tbl_ref[b, h], acct)
        lanes = slice((h % GROUP) * BLOCK, (h % GROUP + 1) * BLOCK)
        for var, mask in enumerate(masks):
            out_ref[var, h // GROUP, :, lanes] = jnp.where(mask, acct, NEG_INF)


def _weight_gather(shards, bucket_t, rel_bias):
    n = len(shards)
    hb = rel_bias.shape[1]

    def body(*refs):
        xs, (bkt_ref, tbl_ref), outs, bias_ref = refs[:n], refs[n:n + 2], refs[n + 2:2 * n + 2], refs[2 * n + 2]
        send_sems, recv_sems, local_sems = refs[2 * n + 3:]
        x, y, c = _mesh_pos()
        me, sibling = (x, y, c), (x, y, 1 - c)
        chips = [(1 - x, y), (x, 1 - y), (1 - x, 1 - y)]

        def copy(a, k, block, to, src=None):
            slot = outs[a].at[_lin(block)]
            return pltpu.make_async_remote_copy(
                src_ref=slot if src is None else src, dst_ref=slot,
                send_sem=send_sems.at[a, k], recv_sem=recv_sems.at[a, k],
                device_id=to, device_id_type=MESH)

        started = []
        for a in range(n):
            mine = pltpu.make_async_copy(xs[a], outs[a].at[_lin(me)], local_sems.at[a])
            mine.start()
            started.append(mine)
        sends = []
        for a in range(n):
            first = [copy(a, 0, me, sibling, src=xs[a])]
            first += [copy(a, 1 + j, me, (*chip, c), src=xs[a]) for j, chip in enumerate(chips)]
            for cp in first:
                cp.start()
            sends += first
        _bias_tables(bkt_ref, tbl_ref, bias_ref, hb)
        for a in range(n):
            for j, chip in enumerate(chips):
                copy(a, 1 + j, (*chip, c), me).wait_recv()
                fwd = copy(a, 4 + j, (*chip, c), sibling)
                fwd.start()
                sends.append(fwd)
        for a in range(n):
            copy(a, 0, sibling, me).wait_recv()
            for j, chip in enumerate(chips):
                copy(a, 4 + j, (*chip, 1 - c), me).wait_recv()
        for cp in sends:
            cp.wait_send()
        for mine in started:
            mine.wait()

    anyspec = pl.BlockSpec(memory_space=pl.ANY)
    vm = pl.BlockSpec(memory_space=pltpu.VMEM)
    res = pl.pallas_call(
        body,
        out_shape=[SDS((N_DEV,) + s.shape, s.dtype) for s in shards]
        + [SDS((3, hb // GROUP, SPAN, GROUP * BLOCK), F32)],
        in_specs=[anyspec] * n + [vm, pl.BlockSpec(memory_space=pltpu.SMEM)],
        out_specs=[anyspec] * n + [vm],
        scratch_shapes=[pltpu.SemaphoreType.DMA((n, 7)), pltpu.SemaphoreType.DMA((n, 7)),
                        pltpu.SemaphoreType.DMA((n,))],
        name="weight_gather",
    )(*shards, bucket_t, rel_bias)
    return res[:n], res[n]


def _direct_exchange(kind, ins, outs, send_sems, recv_sems, local_sems):
    x, y, c = _mesh_pos()
    me = (x, y, c)
    peers = [(x, y, 1 - c), (1 - x, y, c), (x, 1 - y, c), (1 - x, 1 - y, c),
             (1 - x, y, 1 - c), (x, 1 - y, 1 - c), (1 - x, 1 - y, 1 - c)]

    def src(a, to):
        return ins[a] if kind == "gather" else ins[a].at[_lin(to)]

    def remote(a, k, to, frm):
        return pltpu.make_async_remote_copy(
            src_ref=src(a, to), dst_ref=outs[a].at[_lin(frm)],
            send_sem=send_sems.at[a, k], recv_sem=recv_sems.at[a, k],
            device_id=to, device_id_type=MESH)

    n = len(ins)
    sends = [remote(a, k, p, me) for a in range(n) for k, p in enumerate(peers)]
    arrivals = [remote(a, k, p, p) for a in range(n) for k, p in enumerate(peers)]
    local = [pltpu.make_async_copy(src(a, me), outs[a].at[_lin(me)], local_sems.at[a]) for a in range(n)]

    def start():
        for cp in local + sends:
            cp.start()

    def wait():
        for cp in arrivals:
            cp.wait_recv()
        for cp in sends:
            cp.wait_send()
        for cp in local:
            cp.wait()

    return start, wait


def _exchange_scratch(n):
    return [pltpu.SemaphoreType.DMA((n, 7)), pltpu.SemaphoreType.DMA((n, 7)), pltpu.SemaphoreType.DMA((n,))]


SMALL_LANES = 128


def _small_allreduce(dg_rows, dgq, dgk, dsink_g, drel_g, loss8):
    d = dg_rows[0].shape[1]
    kv = dsink_g.shape[0]

    def body(g1_ref, g2_ref, g3_ref, g4_ref, gq_ref, gk_ref, sk_ref, rl_ref, ls_ref, vec_ref, rel_ref,
             vbuf, rbuf, vland, rland, send_sems, recv_sems):
        x, y, c = _mesh_pos()
        me = (x, y, c)
        peers = [(x, y, 1 - c), (1 - x, y, c), (x, 1 - y, c), (1 - x, 1 - y, c),
                 (1 - x, y, 1 - c), (x, 1 - y, 1 - c), (1 - x, 1 - y, 1 - c)]
        vbuf[...] = jnp.zeros_like(vbuf)
        rbuf[...] = jnp.zeros_like(rbuf)
        for row, ref in enumerate((g1_ref, g2_ref, g3_ref, g4_ref)):
            vbuf[row:row + 1, :] = ref[0:1, :]
        vbuf[4:5, 0:HEAD_DIM] = gq_ref[0:1, 0:HEAD_DIM] + gq_ref[0:1, HEAD_DIM:PAIR]
        vbuf[4:5, SMALL_LANES:SMALL_LANES + HEAD_DIM] = gk_ref[0:1, 0:HEAD_DIM] + gk_ref[0:1, HEAD_DIM:PAIR]
        for g in range(kv):
            vbuf[4:5, 2 * SMALL_LANES + g * GROUP:2 * SMALL_LANES + (g + 1) * GROUP] = sk_ref[g, 0:1, 0:GROUP]
            rbuf[:, g * GROUP:(g + 1) * GROUP] = rl_ref[g, :, 0:GROUP]
        vbuf[4:5, 3 * SMALL_LANES:3 * SMALL_LANES + 1] = ls_ref[0:1, 0:1]

        def copies(k, to, frm):
            return [pltpu.make_async_remote_copy(
                src_ref=buf, dst_ref=land.at[_lin(frm)], send_sem=send_sems.at[a, k], recv_sem=recv_sems.at[a, k],
                device_id=to, device_id_type=MESH) for a, (buf, land) in enumerate(((vbuf, vland), (rbuf, rland)))]

        sends = [cp for k, p in enumerate(peers) for cp in copies(k, p, me)]
        for cp in sends:
            cp.start()
        vland[_lin(me)] = vbuf[...]
        rland[_lin(me)] = rbuf[...]
        for k, p in enumerate(peers):
            for cp in copies(k, p, p):
                cp.wait_recv()
        for cp in sends:
            cp.wait_send()
        vacc, racc = vland[0], rland[0]
        for s in range(1, N_DEV):
            vacc, racc = vacc + vland[s], racc + rland[s]
        vec_ref[...] = vacc
        rel_ref[...] = racc

    vm = pl.BlockSpec(memory_space=pltpu.VMEM)
    return pl.pallas_call(
        body,
        out_shape=[SDS((8, d), F32), SDS((N_BUCKETS, 128), F32)],
        in_specs=[vm] * 9,
        out_specs=[vm, vm],
        scratch_shapes=[pltpu.VMEM((8, d), F32), pltpu.VMEM((N_BUCKETS, 128), F32),
                        pltpu.VMEM((N_DEV, 8, d), F32), pltpu.VMEM((N_DEV, N_BUCKETS, 128), F32),
                        pltpu.SemaphoreType.DMA((2, 7)), pltpu.SemaphoreType.DMA((2, 7))],
        name="small_allreduce",
    )(*dg_rows, dgq, dgk, dsink_g, drel_g, loss8)


def _inproj(x2, g1, w_t, tm):
    t, d = x2.shape
    p = w_t.shape[0]

    def body(x_ref, g_ref, w_ref, h_ref, p_ref):
        y, _, _ = _rms_fwd(x_ref[...], g_ref[...])
        h = y.astype(BF16)
        h_ref[...] = h
        p_ref[...] = _dot_nt(h, w_ref[...])

    return pl.pallas_call(
        body,
        grid=(t // tm,),
        in_specs=[pl.BlockSpec((tm, d), lambda i: (i, 0)),
                  pl.BlockSpec((1, d), lambda i: (0, 0)),
                  pl.BlockSpec((p, d), lambda i: (0, 0))],
        out_specs=[pl.BlockSpec((tm, d), lambda i: (i, 0)),
                   pl.BlockSpec((tm, p), lambda i: (i, 0))],
        out_shape=[SDS((t, d), BF16), SDS((t, p), F32)],
        compiler_params=_cp("parallel"),
        name="inproj",
    )(x2, g1, w_t)


PAIR = 2 * HEAD_DIM


def _pair_masks(ts):
    lane = lax.broadcasted_iota(jnp.int32, (ts, PAIR), 1)
    return lane < HEAD_DIM, (lane % 32) < 16


def _pair_mean(v, low):
    del low
    r = lax.broadcasted_iota(jnp.int32, (PAIR, PAIR), 0) // HEAD_DIM
    c = lax.broadcasted_iota(jnp.int32, (PAIR, PAIR), 1) // HEAD_DIM
    same_head = (r == c).astype(BF16)
    hi = v.astype(BF16)
    lo = (v - hi.astype(F32)).astype(BF16)
    return (_dot(hi, same_head) + _dot(lo, same_head)) * (1.0 / HEAD_DIM)


def _pair_partner(v, first):
    return jnp.where(first, pltpu.roll(v, PAIR - 16, 1), pltpu.roll(v, 16, 1))


def _qkprep(proj, cos, sin_signed, gq, gk, bl, s_len, ha, kva, hb, kvb, ts):
    t, p_cols = proj.shape
    assert ha % 2 == 0 and kva % 2 == 0 and hb % 2 == 0 and kvb % 2 == 0
    ns = s_len // ts
    sp = s_len + 2 * BLOCK

    def body(p_ref, cos_ref, sin_ref, gq_ref, gk_ref, qa_ref, ka_ref, kat_ref, va_ref, vat_ref, qb_ref, kb_ref,
             kbt_ref, vb_ref, vbt_ref):
        i = pl.program_id(1)
        cs, sn = cos_ref[...], sin_ref[...]
        low, first = _pair_masks(ts)
        ones_row = (lax.broadcasted_iota(jnp.int32, (VT_PAD, ts), 0) == 0).astype(BF16)
        heads = (slice(0, HEAD_DIM), slice(HEAD_DIM, PAIR))

        def pair(p):
            return p_ref[:, p * PAIR:(p + 1) * PAIR]

        def normrope(x, g):
            y = x * lax.rsqrt(_pair_mean(x * x, low) + EPS) * g
            return y * cs + _pair_partner(y, first) * sn

        eye = (lax.broadcasted_iota(jnp.int32, (PAIR, PAIR), 0)
               == lax.broadcasted_iota(jnp.int32, (PAIR, PAIR), 1)).astype(BF16)

        def transposed(xb):
            return _dot_nt(eye, xb).astype(BF16)

        for p in range(ha // 2):
            qa_ref[:, p * PAIR:(p + 1) * PAIR] = (normrope(pair(p), gq_ref[...]) * SCALE).astype(BF16)
        base = ha // 2
        for p in range(kva // 2):
            k = normrope(pair(base + p), gk_ref[...]).astype(BF16)
            v = pair(base + kva // 2 + p).astype(BF16)
            kt, vt = transposed(k), transposed(v)
            for e, lanes in enumerate(heads):
                ka_ref[0, 2 * p + e] = k[:, lanes]
                va_ref[0, 2 * p + e] = v[:, lanes]
                kat_ref[0, 2 * p + e] = kt[lanes, :]
                vat_ref[0, 2 * p + e, 0:HEAD_DIM, :] = vt[lanes, :]
                vat_ref[0, 2 * p + e, HEAD_DIM:HEAD_DIM + VT_PAD, :] = ones_row
        base += kva
        for p in range(hb // 2):
            qb_ref[:, p * PAIR:(p + 1) * PAIR] = (pair(base + p) * SCALE).astype(BF16)
        base += hb // 2

        @pl.when(i == 0)
        def _():
            zeros = jnp.zeros((kvb, BLOCK, HEAD_DIM), BF16)
            zeros_t = jnp.zeros((kvb, HEAD_DIM + VT_PAD, BLOCK), BF16)
            for ref in (kb_ref, vb_ref):
                ref[0, :, 0:BLOCK, :] = zeros
                ref[0, :, sp - BLOCK:sp, :] = zeros
            kbt_ref[0, :, :, 0:BLOCK] = zeros_t[:, 0:HEAD_DIM]
            kbt_ref[0, :, :, sp - BLOCK:sp] = zeros_t[:, 0:HEAD_DIM]
            vbt_ref[0, :, :, 0:BLOCK] = zeros_t
            vbt_ref[0, :, :, sp - BLOCK:sp] = zeros_t

        rows = pl.ds(pl.multiple_of(BLOCK + i * ts, BLOCK), ts)
        for p in range(kvb // 2):
            k = pair(base + p).astype(BF16)
            v = pair(base + kvb // 2 + p).astype(BF16)
            kt, vt = transposed(k), transposed(v)
            for e, lanes in enumerate(heads):
                kb_ref[0, 2 * p + e, rows, :] = k[:, lanes]
                vb_ref[0, 2 * p + e, rows, :] = v[:, lanes]
                kbt_ref[0, 2 * p + e, :, rows] = kt[lanes, :]
                vbt_ref[0, 2 * p + e, 0:HEAD_DIM, rows] = vt[lanes, :]
                vbt_ref[0, 2 * p + e, HEAD_DIM:HEAD_DIM + VT_PAD, rows] = ones_row

    def hm(nh):
        return pl.BlockSpec((1, nh, ts, HEAD_DIM), lambda b, i: (b, 0, i, 0))

    def tokmajor(nh):
        return pl.BlockSpec((ts, nh * HEAD_DIM), lambda b, i: (b * ns + i, 0))

    def padded(nh):
        return pl.BlockSpec((1, nh, sp, HEAD_DIM), lambda b, i: (b, 0, 0, 0))

    def padded_t(nh, rows):
        return pl.BlockSpec((1, nh, rows, sp), lambda b, i: (b, 0, 0, 0))

    return pl.pallas_call(
        body,
        grid=(bl, ns),
        in_specs=[pl.BlockSpec((ts, p_cols), lambda b, i: (b * ns + i, 0)),
                  pl.BlockSpec((ts, PAIR), lambda b, i: (i, 0)),
                  pl.BlockSpec((ts, PAIR), lambda b, i: (i, 0)),
                  pl.BlockSpec((1, PAIR), lambda b, i: (0, 0)),
                  pl.BlockSpec((1, PAIR), lambda b, i: (0, 0))],
        out_specs=[tokmajor(ha), hm(kva), pl.BlockSpec((1, kva, HEAD_DIM, ts), lambda b, i: (b, 0, 0, i)), hm(kva),
                   pl.BlockSpec((1, kva, HEAD_DIM + VT_PAD, ts), lambda b, i: (b, 0, 0, i)),
                   tokmajor(hb), padded(kvb), padded_t(kvb, HEAD_DIM), padded(kvb),
                   padded_t(kvb, HEAD_DIM + VT_PAD)],
        out_shape=[SDS((t, ha * HEAD_DIM), BF16), SDS((bl, kva, s_len, HEAD_DIM), BF16),
                   SDS((bl, kva, HEAD_DIM, s_len), BF16),
                   SDS((bl, kva, s_len, HEAD_DIM), BF16), SDS((bl, kva, HEAD_DIM + VT_PAD, s_len), BF16),
                   SDS((t, hb * HEAD_DIM), BF16),
                   SDS((bl, kvb, sp, HEAD_DIM), BF16), SDS((bl, kvb, HEAD_DIM, sp), BF16),
                   SDS((bl, kvb, sp, HEAD_DIM), BF16), SDS((bl, kvb, HEAD_DIM + VT_PAD, sp), BF16)],
        compiler_params=_cp("parallel", "arbitrary"),
        name="qkprep",
    )(proj, cos, sin_signed, gq, gk)


def _attn_a_fwd(qa, ka, vat, tq, tk, shards):
    bl, kv, s_len, _ = ka.shape
    ha = qa.shape[1] // HEAD_DIM
    va_rows = vat.shape[2]
    nq, nk = s_len // tq, s_len // tk
    assert nk % 2 == 0
    r = GROUP * tq
    ns = len(shards)

    def body(q_ref, qn_ref, k_ref, v_ref, *rest):
        shard_refs, (o_ref, l_ref), gathered = rest[:ns], rest[ns:ns + 2], rest[ns + 2:2 * ns + 2]
        st_sc, send_sems, recv_sems, local_sems = rest[2 * ns + 2:]
        i = pl.program_id(2)
        step_id = (pl.program_id(0) * kv + pl.program_id(1)) * nq + i
        start, wait = _direct_exchange("gather", shard_refs, gathered, send_sems, recv_sems, local_sems)
        pl.when(step_id == 0)(start)

        q = _heads_t(q_ref[...]).astype(BF16)

        def scores(c, qv):
            return _dot(k_ref[0, 0, pl.ds(pl.multiple_of(c * tk, tk), tk), :], qv)

        def fold(st, c, carry):
            m_old, acc = carry
            m_new = jnp.maximum(m_old, jnp.max(st, axis=0, keepdims=True))
            pt = jnp.exp(st - m_new).astype(BF16)
            vt = v_ref[0, 0, :, pl.ds(pl.multiple_of(c * tk, tk), tk)]
            return m_new, jnp.exp(m_old - m_new) * acc + _dot(vt, pt)

        @pl.when(i == 0)
        def _():
            st_sc[0] = scores(0, q)

        def step(c2, carry):
            c = 2 * c2
            st_sc[1] = scores(c + 1, q)
            carry = fold(st_sc[0], c, carry)
            st_sc[0] = scores(c + 2, q)
            return fold(st_sc[1], c + 1, carry)

        carry = (jnp.full((1, r), -jnp.inf, F32), jnp.zeros((va_rows, r), F32))
        for c2 in range(nk // 2 - 1):
            carry = step(c2, carry)
        st_sc[1] = scores(nk - 1, q)
        carry = fold(st_sc[0], nk - 2, carry)
        st_sc[0] = scores(0, _heads_t(qn_ref[...]).astype(BF16))
        m, acc = fold(st_sc[1], nk - 1, carry)
        l = acc[HEAD_DIM:HEAD_DIM + 1, :]
        o_ref[...] = _heads_t_inv(acc[0:HEAD_DIM, :] / l).astype(BF16)
        l_ref[0, 0, 0] = jnp.broadcast_to(m + jnp.log(l), (8, r))
        pl.when(step_id == bl * kv * nq - 1)(wait)

    anyspec = pl.BlockSpec(memory_space=pl.ANY)
    res = pl.pallas_call(
        body,
        grid=(bl, kv, nq),
        in_specs=[pl.BlockSpec((tq, GROUP * HEAD_DIM), lambda b, g, i: (b * nq + i, g)),
                  pl.BlockSpec((tq, GROUP * HEAD_DIM), lambda b, g, i: (b * nq + jnp.minimum(i + 1, nq - 1), g)),
                  pl.BlockSpec((1, 1, s_len, HEAD_DIM), lambda b, g, i: (b, g, 0, 0)),
                  pl.BlockSpec((1, 1, va_rows, s_len), lambda b, g, i: (b, g, 0, 0))] + [anyspec] * ns,
        out_specs=[pl.BlockSpec((tq, GROUP * HEAD_DIM), lambda b, g, i: (b * nq + i, g)),
                   pl.BlockSpec((1, 1, 1, 8, r), lambda b, g, i: (b, g, i, 0, 0))] + [anyspec] * ns,
        out_shape=[SDS((bl * s_len, ha * HEAD_DIM), BF16), SDS((bl, kv, nq, 8, r), F32)]
        + [SDS((N_DEV,) + s.shape, s.dtype) for s in shards],
        scratch_shapes=[pltpu.VMEM((2, tk, r), F32)] + _exchange_scratch(ns),
        compiler_params=_cp("arbitrary", "arbitrary", "arbitrary"),
        name="attn_a_fwd",
    )(qa, qa, ka, vat, *shards)
    return res[0], res[1], res[2:]


FFN_BLOCKS_PER_STEP = 8
FFN_BWD_TOKENS = 256
QB_PER_STEP = 16


def _bias_variant(n, nb):
    return jnp.where(n == 0, 1, jnp.where(n == nb - 1, 2, 0))


def _sink_row(sink_ref, g):
    return jnp.concatenate([jnp.full((1, BLOCK), sink_ref[0, g * GROUP + h], F32) for h in range(GROUP)], axis=1)


def _attn_b_fwd(qb, kb, vbt, bias_t, sink, s_len):
    bl, kv, sp, _ = kb.shape
    hb = qb.shape[1] // HEAD_DIM
    vt_rows = vbt.shape[2]
    nb = s_len // BLOCK
    nbs = min(QB_PER_STEP, nb)
    r = GROUP * BLOCK

    def body(q_ref, k_ref, vt_ref, bt_ref, sink_ref, o_ref, l_ref, st_sc, pb_sc):
        g, n0 = pl.program_id(1), pl.program_id(2) * nbs
        sink_row = _sink_row(sink_ref, g)

        def span(j):
            return pl.ds(pl.multiple_of((n0 + j) * BLOCK, BLOCK), SPAN)

        for j in range(nbs):
            qt = _heads_t(q_ref[j * BLOCK:(j + 1) * BLOCK, :]).astype(BF16)
            st_sc[j] = _dot(k_ref[0, 0, span(j), :], qt) + bt_ref[_bias_variant(n0 + j, nb), 0]
        maxes = []
        for j in range(nbs):
            st = st_sc[j]
            m = jnp.maximum(jnp.max(st, axis=0, keepdims=True), sink_row)
            pb_sc[j] = jnp.exp(st - m).astype(BF16)
            maxes.append(m)
        for j in range(nbs):
            m = maxes[j]
            acc = _dot(vt_ref[0, 0, :, span(j)], pb_sc[j])
            l = acc[HEAD_DIM:HEAD_DIM + 1, :] + jnp.exp(sink_row - m)
            o_ref[j * BLOCK:(j + 1) * BLOCK, :] = _heads_t_inv(acc[0:HEAD_DIM, :] / l).astype(BF16)
            l_ref[0, 0, j] = jnp.broadcast_to(m + jnp.log(l), (8, r))

    return pl.pallas_call(
        body,
        grid=(bl, kv, nb // nbs),
        in_specs=[pl.BlockSpec((nbs * BLOCK, GROUP * HEAD_DIM), lambda b, g, n: (b * (nb // nbs) + n, g)),
                  pl.BlockSpec((1, 1, sp, HEAD_DIM), lambda b, g, n: (b, g, 0, 0)),
                  pl.BlockSpec((1, 1, vt_rows, sp), lambda b, g, n: (b, g, 0, 0)),
                  pl.BlockSpec((3, 1, SPAN, r), lambda b, g, n: (0, g, 0, 0)),
                  pl.BlockSpec(memory_space=pltpu.SMEM)],
        out_specs=[pl.BlockSpec((nbs * BLOCK, GROUP * HEAD_DIM), lambda b, g, n: (b * (nb // nbs) + n, g)),
                   pl.BlockSpec((1, 1, nbs, 8, r), lambda b, g, n: (b, g, n, 0, 0))],
        out_shape=[SDS((bl * s_len, hb * HEAD_DIM), BF16), SDS((bl, kv, nb, 8, r), F32)],
        scratch_shapes=[pltpu.VMEM((nbs, SPAN, r), F32), pltpu.VMEM((nbs, SPAN, r), BF16)],
        compiler_params=_cp("parallel", "parallel", "arbitrary"),
        name="attn_b_fwd",
    )(qb, kb, vbt, bias_t, sink)


def _mixout(oa, ob, wo, x2, g2, g3, tm):
    t, d = x2.shape
    ca = oa.shape[1]

    def body(oa_ref, ob_ref, w_ref, x_ref, g2_ref, g3_ref, mix_ref, x1_ref, h2_ref):
        mix = _dot(oa_ref[...], w_ref[0:ca, :]) + _dot(ob_ref[...], w_ref[ca:, :])
        mix_ref[...] = mix
        y2, _, _ = _rms_fwd(mix, g2_ref[...])
        x1 = x_ref[...] + y2
        x1_ref[...] = x1
        y3, _, _ = _rms_fwd(x1, g3_ref[...])
        h2_ref[...] = y3.astype(BF16)

    tile = lambda w: pl.BlockSpec((tm, w), lambda i: (i, 0))
    vec = pl.BlockSpec((1, d), lambda i: (0, 0))
    return pl.pallas_call(
        body,
        grid=(t // tm,),
        in_specs=[tile(ca), tile(ob.shape[1]), pl.BlockSpec(wo.shape, lambda i: (0, 0)), tile(d), vec, vec],
        out_specs=[tile(d), tile(d), tile(d)],
        out_shape=[SDS((t, d), F32), SDS((t, d), F32), SDS((t, d), BF16)],
        compiler_params=_cp("parallel"),
        name="mixout",
    )(oa, ob, wo, x2, g2, g3)


def _ffn_fwd(h2, wup_g, wdn, x1, target, g4, tm, jb):
    t, d = x1.shape
    nblk, _, tf = wup_g.shape
    ff = nblk * tf
    nt = t // tm
    nj = nblk // jb

    def body(h_ref, wu_ref, wd_ref, x1_ref, tg_ref, g_ref, u_ref, df_ref, dy_ref, dg_ref, loss_ref, acc_sc):
        i, j = pl.program_id(0), pl.program_id(1)

        @pl.when(j == 0)
        def _():
            acc_sc[...] = jnp.zeros_like(acc_sc)

        @pl.when((i == 0) & (j == 0))
        def _():
            dg_ref[...] = jnp.zeros_like(dg_ref)
            loss_ref[...] = jnp.zeros_like(loss_ref)

        h = h_ref[...]
        squares = []
        for s in range(jb):
            u = jnp.maximum(_dot(h, wu_ref[s]), 0.0)
            u_ref[:, s * tf:(s + 1) * tf] = u.astype(BF16)
            squares.append((u * u).astype(BF16))
        acc_sc[...] += _dot(jnp.concatenate(squares, axis=1), wd_ref[...])

        @pl.when(j == nj - 1)
        def _():
            g = g_ref[...]
            y4, n, r = _rms_fwd(acc_sc[...], g)
            e = (x1_ref[...] + y4) - tg_ref[...]
            loss_ref[...] += jnp.sum(e * e) * (0.5 / d)
            dy = e * (1.0 / d)
            dy_ref[...] = dy
            df, dgt = _rms_bwd(n, r, g, dy)
            df_ref[...] = df.astype(BF16)
            dg_ref[0:1, :] += jnp.sum(dgt, axis=0, keepdims=True)

    tile = pl.BlockSpec((tm, d), lambda i, j: (i, 0))
    return pl.pallas_call(
        body,
        grid=(nt, nj),
        in_specs=[tile,
                  pl.BlockSpec((jb, d, tf), lambda i, j: (j, 0, 0)),
                  pl.BlockSpec((jb * tf, d), lambda i, j: (j, 0)),
                  tile, tile,
                  pl.BlockSpec((1, d), lambda i, j: (0, 0))],
        out_specs=[pl.BlockSpec((tm, jb * tf), lambda i, j: (i, j)), tile, tile,
                   pl.BlockSpec((8, d), lambda i, j: (0, 0)),
                   pl.BlockSpec((8, 128), lambda i, j: (0, 0))],
        out_shape=[SDS((t, ff), BF16), SDS((t, d), BF16), SDS((t, d), F32), SDS((8, d), F32), SDS((8, 128), F32)],
        scratch_shapes=[pltpu.VMEM((tm, d), F32)],
        compiler_params=_cp("arbitrary", "arbitrary"),
        name="ffn_fwd",
    )(h2, wup_g, wdn, x1, target, g4)


def _ffn_bwd(df, u, wdn, wup_g, x1, dy, mix, g3, g2, tm, jb):
    t, d = x1.shape
    nblk, _, tf = wup_g.shape
    nt = t // tm
    nj = nblk // jb

    def body(df_ref, u_ref, wd_ref, wu_ref, x1_ref, dy_ref, mix_ref, g3_ref, g2_ref,
             dpre_ref, dx1_ref, dmix_ref, dg3_ref, dg2_ref, acc_sc):
        i, j = pl.program_id(0), pl.program_id(1)

        @pl.when(j == 0)
        def _():
            acc_sc[...] = jnp.zeros_like(acc_sc)

        @pl.when((i == 0) & (j == 0))
        def _():
            dg3_ref[...] = jnp.zeros_like(dg3_ref)
            dg2_ref[...] = jnp.zeros_like(dg2_ref)

        du2 = _dot_nt(df_ref[...], wd_ref[...])
        dpre = (2.0 * u_ref[...].astype(F32) * du2).astype(BF16)
        dpre_ref[...] = dpre
        dh = _dot_nt(dpre[:, 0:tf], wu_ref[0])
        for s in range(1, jb):
            dh = dh + _dot_nt(dpre[:, s * tf:(s + 1) * tf], wu_ref[s])
        acc_sc[...] += dh

        @pl.when(j == nj - 1)
        def _():
            g3, g2 = g3_ref[...], g2_ref[...]
            _, n3, r3 = _rms_fwd(x1_ref[...], g3)
            dx, dgt3 = _rms_bwd(n3, r3, g3, acc_sc[...])
            dx1 = dy_ref[...] + dx
            dx1_ref[...] = dx1
            dg3_ref[0:1, :] += jnp.sum(dgt3, axis=0, keepdims=True)
            _, n2, r2 = _rms_fwd(mix_ref[...], g2)
            dmix, dgt2 = _rms_bwd(n2, r2, g2, dx1)
            dmix_ref[...] = dmix.astype(BF16)
            dg2_ref[0:1, :] += jnp.sum(dgt2, axis=0, keepdims=True)

    tile = pl.BlockSpec((tm, d), lambda i, j: (i, 0))
    vec = pl.BlockSpec((1, d), lambda i, j: (0, 0))
    acc8 = pl.BlockSpec((8, d), lambda i, j: (0, 0))
    return pl.pallas_call(
        body,
        grid=(nt, nj),
        in_specs=[tile,
                  pl.BlockSpec((tm, jb * tf), lambda i, j: (i, j)),
                  pl.BlockSpec((jb * tf, d), lambda i, j: (j, 0)),
                  pl.BlockSpec((jb, d, tf), lambda i, j: (j, 0, 0)),
                  tile, tile, tile, vec, vec],
        out_specs=[pl.BlockSpec((tm, jb * tf), lambda i, j: (i, j)), tile, tile, acc8, acc8],
        out_shape=[SDS(u.shape, BF16), SDS((t, d), F32), SDS((t, d), BF16), SDS((8, d), F32), SDS((8, d), F32)],
        scratch_shapes=[pltpu.VMEM((tm, d), F32)],
        compiler_params=_cp("arbitrary", "arbitrary"),
        name="ffn_bwd",
    )(df, u, wdn, wup_g, x1, dy, mix, g3, g2)


def _wgrad(a, b, a_spec, b_spec, out_block, out_shape, nj, nk, name, prep_a=None, prep_b=None):
    acc_shape = out_block[1:]

    def body(a_ref, b_ref, o_ref, acc_sc):
        k = pl.program_id(1)
        av = a_ref[...] if prep_a is None else prep_a(a_ref)
        bv = b_ref[...] if prep_b is None else prep_b(b_ref)
        part = _dot_tn(av, bv)

        @pl.when(k == 0)
        def _():
            acc_sc[...] = part

        @pl.when(k > 0)
        def _():
            acc_sc[...] += part

        @pl.when(k == nk - 1)
        def _():
            o_ref[0] = acc_sc[...].astype(BF16)

    return pl.pallas_call(
        body,
        grid=(nj, nk),
        in_specs=[a_spec, b_spec],
        out_specs=pl.BlockSpec(out_block, lambda j, k: (j, 0, 0)),
        out_shape=SDS(out_shape, BF16),
        scratch_shapes=[pltpu.VMEM(acc_shape, F32)],
        compiler_params=_cp("parallel", "arbitrary"),
        name=name,
    )(a, b)


def _wgrad_cols(a, b, nj, tt, name):
    t, m = a.shape
    bn = b.shape[1] // nj
    return _wgrad(a, b, pl.BlockSpec((tt, m), lambda j, k: (k, 0)), pl.BlockSpec((tt, bn), lambda j, k: (k, j)),
                  (1, m, bn), (nj, m, bn), nj, t // tt, name)


def _wgrad_rows(a, b, nj, tt, name, square=False):
    t, n = b.shape
    bm = a.shape[1] // nj

    def squared(a_ref):
        af = a_ref[...].astype(F32)
        return (af * af).astype(BF16)

    return _wgrad(a, b, pl.BlockSpec((tt, bm), lambda j, k: (k, j)), pl.BlockSpec((tt, n), lambda j, k: (k, 0)),
                  (1, bm, n), (nj, bm, n), nj, t // tt, name, prep_a=squared if square else None)


def _wgrad_o(oa, ob, dmix, nj, tt):
    t, n = dmix.shape
    ca, cb = oa.shape[1], ob.shape[1]
    m = ca + cb
    nk = t // tt

    def body(oa_ref, ob_ref, b_ref, o_ref, acc_sc):
        k = pl.program_id(0)
        part = _dot_tn(jnp.concatenate([oa_ref[...], ob_ref[...]], axis=1), b_ref[...])

        @pl.when(k == 0)
        def _():
            acc_sc[...] = part

        @pl.when(k > 0)
        def _():
            acc_sc[...] += part

        @pl.when(k == nk - 1)
        def _():
            o_ref[...] = acc_sc[...].reshape(nj, m // nj, n).astype(BF16)

    return pl.pallas_call(
        body,
        grid=(nk,),
        in_specs=[pl.BlockSpec((tt, ca), lambda k: (k, 0)), pl.BlockSpec((tt, cb), lambda k: (k, 0)),
                  pl.BlockSpec((tt, n), lambda k: (k, 0))],
        out_specs=pl.BlockSpec((nj, m // nj, n), lambda k: (0, 0, 0)),
        out_shape=SDS((nj, m // nj, n), BF16),
        scratch_shapes=[pltpu.VMEM((m, n), F32)],
        compiler_params=_cp("arbitrary"),
        name="wgrad_o",
    )(oa, ob, dmix)


def _attn_out_bwd(dmix, wo, ca, tm):
    t, d = dmix.shape
    cb = wo.shape[0] - ca

    def body(dm_ref, w_ref, da_ref, db_ref):
        dm = dm_ref[...]
        da_ref[...] = _dot_nt(dm, w_ref[0:ca, :]).astype(BF16)
        db_ref[...] = _dot_nt(dm, w_ref[ca:, :]).astype(BF16)

    return pl.pallas_call(
        body,
        grid=(t // tm,),
        in_specs=[pl.BlockSpec((tm, d), lambda i: (i, 0)), pl.BlockSpec(wo.shape, lambda i: (0, 0))],
        out_specs=[pl.BlockSpec((tm, ca), lambda i: (i, 0)), pl.BlockSpec((tm, cb), lambda i: (i, 0))],
        out_shape=[SDS((t, ca), BF16), SDS((t, cb), BF16)],
        compiler_params=_cp("parallel"),
        name="attn_out_bwd",
    )(dmix, wo)


def _heads_t(x):
    xt = x.astype(F32).T
    return jnp.concatenate([xt[h * HEAD_DIM:(h + 1) * HEAD_DIM, :] for h in range(GROUP)], axis=1)


def _heads_t_inv(yt):
    n = yt.shape[1] // GROUP
    return jnp.concatenate([yt[:, h * n:(h + 1) * n] for h in range(GROUP)], axis=0).T


def _attn_a_bwd(qa, ka, kat, va, do, o, lse, tq, tk, grads):
    bl, kv, s_len, _ = ka.shape
    nq, nk = s_len // tq, s_len // tk
    assert nk % 2 == 0
    r = GROUP * tq
    ng = len(grads)

    def body(q_ref, qn_ref, k_ref, kt_ref, v_ref, do_ref, don_ref, o_ref, l_ref, *rest):
        grad_refs, (dq_ref, dk_ref, dv_ref), parts = rest[:ng], rest[ng:ng + 3], rest[ng + 3:2 * ng + 3]
        st_sc, dp_sc, dkt_sc, dvt_sc, send_sems, recv_sems, local_sems = rest[2 * ng + 3:]
        i = pl.program_id(2)
        step_id = (pl.program_id(0) * kv + pl.program_id(1)) * nq + i
        start, wait = _direct_exchange("scatter", grad_refs, parts, send_sems, recv_sems, local_sems)
        pl.when(step_id == 0)(start)

        dot32 = _heads_t(do_ref[...])
        drow = jnp.sum(dot32 * _heads_t(o_ref[...]), axis=0, keepdims=True)
        qt, dot = _heads_t(q_ref[...]).astype(BF16), dot32.astype(BF16)
        lrow = l_ref[0, 0, 0, 0:1, :]

        @pl.when(i == 0)
        def _():
            dkt_sc[...] = jnp.zeros_like(dkt_sc)
            dvt_sc[...] = jnp.zeros_like(dvt_sc)

        def chunk(c):
            return pl.ds(pl.multiple_of(c * tk, tk), tk)

        def scores(c, slot, qv=qt, dov=dot):
            st_sc[slot] = _dot(k_ref[0, 0, chunk(c), :], qv)
            dp_sc[slot] = _dot(v_ref[0, 0, chunk(c), :], dov)

        def fold(slot, c, dqt):
            pt = jnp.exp(st_sc[slot] - lrow)
            dsb = (pt * (dp_sc[slot] - drow)).astype(BF16)
            dvt_sc[:, chunk(c)] += _dot_nt(dot, pt.astype(BF16))
            dkt_sc[:, chunk(c)] += _dot_nt(qt, dsb)
            return dqt + _dot(kt_ref[0, 0, :, chunk(c)], dsb)

        @pl.when(i == 0)
        def _():
            scores(0, 0)

        def step(c2, dqt):
            c = 2 * c2
            scores(c + 1, 1)
            dqt = fold(0, c, dqt)
            scores(c + 2, 0)
            return fold(1, c + 1, dqt)

        dqt = jnp.zeros((HEAD_DIM, r), F32)
        for c2 in range(nk // 2 - 1):
            dqt = step(c2, dqt)
        scores(nk - 1, 1)
        dqt = fold(0, nk - 2, dqt)
        scores(0, 0, _heads_t(qn_ref[...]).astype(BF16), _heads_t(don_ref[...]).astype(BF16))
        dq_ref[...] = _heads_t_inv(fold(1, nk - 1, dqt))

        @pl.when(i == nq - 1)
        def _():
            dk_ref[0, 0] = dkt_sc[...].T
            dv_ref[0, 0] = dvt_sc[...].T

        pl.when(step_id == bl * kv * nq - 1)(wait)

    kvspec = pl.BlockSpec((1, 1, s_len, HEAD_DIM), lambda b, g, i: (b, g, 0, 0))
    tok = pl.BlockSpec((tq, GROUP * HEAD_DIM), lambda b, g, i: (b * nq + i, g))
    toknext = pl.BlockSpec((tq, GROUP * HEAD_DIM), lambda b, g, i: (b * nq + jnp.minimum(i + 1, nq - 1), g))
    anyspec = pl.BlockSpec(memory_space=pl.ANY)
    res = pl.pallas_call(
        body,
        grid=(bl, kv, nq),
        in_specs=[tok, toknext, kvspec, pl.BlockSpec((1, 1, HEAD_DIM, s_len), lambda b, g, i: (b, g, 0, 0)), kvspec,
                  tok, toknext, tok, pl.BlockSpec((1, 1, 1, 8, r), lambda b, g, i: (b, g, i, 0, 0))] + [anyspec] * ng,
        out_specs=[tok, kvspec, kvspec] + [anyspec] * ng,
        out_shape=[SDS(qa.shape, F32), SDS(ka.shape, F32), SDS(va.shape, F32)]
        + [SDS(g.shape, g.dtype) for g in grads],
        scratch_shapes=[pltpu.VMEM((2, tk, r), F32), pltpu.VMEM((2, tk, r), F32),
                        pltpu.VMEM((HEAD_DIM, s_len), F32), pltpu.VMEM((HEAD_DIM, s_len), F32)]
        + _exchange_scratch(ng),
        compiler_params=_cp("arbitrary", "arbitrary", "arbitrary"),
        name="attn_a_bwd",
    )(qa, qa, ka, kat, va, do, do, o, lse, *grads)
    return res[0], res[1], res[2], res[3:]


def _attn_b_bwd(qb, kb, kbt, vb, do, o, lse, bias_t, sink, s_len):
    bl, kv, sp, _ = kb.shape
    nb = s_len // BLOCK
    nbs = min(QB_PER_STEP, nb)
    r = GROUP * BLOCK

    def body(q_ref, k_ref, kt_ref, v_ref, do_ref, o_ref, l_ref, bt_ref, sink_ref,
             dq_ref, dk_ref, dv_ref, dsum_ref, dsink_ref, dkt_sc, dvt_sc):
        g, b, ns = pl.program_id(0), pl.program_id(1), pl.program_id(2)
        sink_row = _sink_row(sink_ref, g)

        @pl.when(ns == 0)
        def _():
            dkt_sc[...] = jnp.zeros_like(dkt_sc)
            dvt_sc[...] = jnp.zeros_like(dvt_sc)

        @pl.when((b == 0) & (ns == 0))
        def _():
            dsum_ref[...] = jnp.zeros_like(dsum_ref)
            dsink_ref[...] = jnp.zeros_like(dsink_ref)

        dsum = jnp.zeros((SPAN, r), F32)
        dsink = jnp.zeros((1, r), F32)
        for j in range(nbs):
            n = ns * nbs + j
            span = pl.ds(pl.multiple_of(n * BLOCK, BLOCK), SPAN)
            rows = slice(j * BLOCK, (j + 1) * BLOCK)
            dot32 = _heads_t(do_ref[rows, :])
            drow = jnp.sum(dot32 * _heads_t(o_ref[rows, :]), axis=0, keepdims=True)
            qt, dot = _heads_t(q_ref[rows, :]).astype(BF16), dot32.astype(BF16)
            lrow = l_ref[0, 0, j, 0:1, :]
            st = _dot(k_ref[0, 0, span, :], qt) + bt_ref[_bias_variant(n, nb), 0]
            pt = jnp.exp(st - lrow)
            dst = pt * (_dot(v_ref[0, 0, span, :], dot) - drow)
            dsum = dsum + dst
            dsink = dsink - jnp.exp(sink_row - lrow) * drow
            dsb = dst.astype(BF16)
            dvt_sc[:, span] += _dot_nt(dot, pt.astype(BF16))
            dkt_sc[:, span] += _dot_nt(qt, dsb)
            dq_ref[rows, :] = _heads_t_inv(_dot(kt_ref[0, 0, :, span], dsb))
        dsum_ref[0] += dsum
        dsink_ref[0, 0:1, :] += dsink

        @pl.when(ns == nb // nbs - 1)
        def _():
            dk_ref[0, 0] = dkt_sc[:, BLOCK:BLOCK + s_len].T
            dv_ref[0, 0] = dvt_sc[:, BLOCK:BLOCK + s_len].T

    kvspec = pl.BlockSpec((1, 1, sp, HEAD_DIM), lambda g, b, n: (b, g, 0, 0))
    kvout = pl.BlockSpec((1, 1, s_len, HEAD_DIM), lambda g, b, n: (b, g, 0, 0))
    tok = pl.BlockSpec((nbs * BLOCK, GROUP * HEAD_DIM), lambda g, b, n: (b * (nb // nbs) + n, g))
    return pl.pallas_call(
        body,
        grid=(kv, bl, nb // nbs),
        in_specs=[tok, kvspec, pl.BlockSpec((1, 1, HEAD_DIM, sp), lambda g, b, n: (b, g, 0, 0)), kvspec, tok, tok,
                  pl.BlockSpec((1, 1, nbs, 8, r), lambda g, b, n: (b, g, n, 0, 0)),
                  pl.BlockSpec((3, 1, SPAN, r), lambda g, b, n: (0, g, 0, 0)),
                  pl.BlockSpec(memory_space=pltpu.SMEM)],
        out_specs=[tok, kvout, kvout,
                   pl.BlockSpec((1, SPAN, r), lambda g, b, n: (g, 0, 0)),
                   pl.BlockSpec((1, 8, r), lambda g, b, n: (g, 0, 0))],
        out_shape=[SDS(qb.shape, F32), SDS((bl, kv, s_len, HEAD_DIM), F32), SDS((bl, kv, s_len, HEAD_DIM), F32),
                   SDS((kv, SPAN, r), F32), SDS((kv, 8, r), F32)],
        scratch_shapes=[pltpu.VMEM((HEAD_DIM, sp), F32), pltpu.VMEM((HEAD_DIM, sp), F32)],
        compiler_params=_cp("arbitrary", "arbitrary", "arbitrary"),
        name="attn_b_bwd",
    )(qb, kb, kbt, vb, do, o, lse, bias_t, sink)


def _bias_reduce(dsum, dsink, bucket_t4):
    kv, _, r = dsum.shape

    def body(ds_ref, dk_ref, bk_ref, rel_ref, sink_ref):
        lane = lax.broadcasted_iota(jnp.int32, (N_BUCKETS, 128), 1)
        lane8 = lax.broadcasted_iota(jnp.int32, (8, 128), 1)
        bk = bk_ref[...]
        for g in range(kv):
            ds = ds_ref[g]
            rowi = lax.broadcasted_iota(jnp.int32, (N_BUCKETS, r), 0)
            red = jnp.zeros((N_BUCKETS, r), F32)
            for b in range(N_BUCKETS):
                red = jnp.where(rowi == b, jnp.sum(jnp.where(bk == b, ds, 0.0), axis=0, keepdims=True), red)
            out = jnp.zeros((N_BUCKETS, 128), F32)
            so = jnp.zeros((8, 128), F32)
            for h in range(GROUP):
                col = jnp.sum(red[:, h * BLOCK:(h + 1) * BLOCK], axis=1, keepdims=True)
                out = jnp.where(lane == h, col, out)
                sc = jnp.sum(dk_ref[g][:, h * BLOCK:(h + 1) * BLOCK], axis=1, keepdims=True)
                so = jnp.where(lane8 == h, sc, so)
            rel_ref[g] = out
            sink_ref[g] = so

    vm = pl.BlockSpec(memory_space=pltpu.VMEM)
    return pl.pallas_call(
        body,
        in_specs=[vm, vm, vm],
        out_specs=[vm, vm],
        out_shape=[SDS((kv, N_BUCKETS, 128), F32), SDS((kv, 8, 128), F32)],
        name="bias_reduce",
    )(dsum, dsink, bucket_t4)


def _dqkprep(dqa, dka, dva, dqb, dkb, dvb, proj, h1, cos, sin_signed, gq, gk, s_len, ts):
    t, p_cols = proj.shape
    d = h1.shape[1]
    bl, kva, kvb = dka.shape[0], dka.shape[1], dkb.shape[1]
    ha, hb = dqa.shape[1] // HEAD_DIM, dqb.shape[1] // HEAD_DIM
    ns = s_len // ts

    def body(dqa_ref, dka_ref, dva_ref, dqb_ref, dkb_ref, dvb_ref, p_ref, h1_ref, h1p_ref, cos_ref, sin_ref,
             gq_ref, gk_ref, dp_ref, dgq_ref, dgk_ref, gw_ref, gw_sc, dpp_sc):
        b, i = pl.program_id(0), pl.program_id(1)
        cs, sn = cos_ref[...], sin_ref[...]
        low, first = _pair_masks(ts)

        @pl.when((b == 0) & (i == 0))
        def _():
            dgq_ref[...] = jnp.zeros_like(dgq_ref)
            dgk_ref[...] = jnp.zeros_like(dgk_ref)
            gw_sc[...] = jnp.zeros_like(gw_sc)
            dpp_sc[...] = jnp.zeros_like(dpp_sc)

        gw_sc[...] += _dot_tn(dpp_sc[...], h1p_ref[...])

        def grad_pair(ref, p):
            return jnp.concatenate([ref[0, 2 * p], ref[0, 2 * p + 1]], axis=1)

        def put(p, val):
            dp_ref[:, p * PAIR:(p + 1) * PAIR] = val.astype(BF16)

        def unrope_norm(d_rot, p, g, dg_ref):
            dn = d_rot * cs + _pair_partner(d_rot * sn, first)
            xp = p_ref[:, p * PAIR:(p + 1) * PAIR]
            r = lax.rsqrt(_pair_mean(xp * xp, low) + EPS)
            n = xp * r
            gd = g * dn
            dg_ref[0:1, :] += jnp.sum(dn * n, axis=0, keepdims=True)
            put(p, r * (gd - n * _pair_mean(n * gd, low)))

        for p in range(ha // 2):
            unrope_norm(dqa_ref[:, p * PAIR:(p + 1) * PAIR] * SCALE, p, gq_ref[...], dgq_ref)
        base = ha // 2
        for p in range(kva // 2):
            unrope_norm(grad_pair(dka_ref, p), base + p, gk_ref[...], dgk_ref)
            put(base + kva // 2 + p, grad_pair(dva_ref, p))
        base += kva
        for p in range(hb // 2):
            put(base + p, dqb_ref[:, p * PAIR:(p + 1) * PAIR] * SCALE)
        base += hb // 2
        for p in range(kvb // 2):
            put(base + p, grad_pair(dkb_ref, p))
            put(base + kvb // 2 + p, grad_pair(dvb_ref, p))

        dpp_sc[...] = dp_ref[...]

        @pl.when((b == bl - 1) & (i == ns - 1))
        def _():
            gw_ref[...] = (gw_sc[...] + _dot_tn(dp_ref[...], h1_ref[...])).astype(BF16)

    def hm(nh):
        return pl.BlockSpec((1, nh, ts, HEAD_DIM), lambda b, i: (b, 0, i, 0))

    def tokmajor(nh):
        return pl.BlockSpec((ts, nh * HEAD_DIM), lambda b, i: (b * ns + i, 0))

    vec = pl.BlockSpec((1, PAIR), lambda b, i: (0, 0))
    tab = pl.BlockSpec((ts, PAIR), lambda b, i: (i, 0))
    acc = pl.BlockSpec((8, PAIR), lambda b, i: (0, 0))
    pspec = pl.BlockSpec((ts, p_cols), lambda b, i: (b * ns + i, 0))
    return pl.pallas_call(
        body,
        grid=(bl, ns),
        in_specs=[tokmajor(ha), hm(kva), hm(kva), tokmajor(hb), hm(kvb), hm(kvb), pspec,
                  pl.BlockSpec((ts, d), lambda b, i: (b * ns + i, 0)),
                  pl.BlockSpec((ts, d), lambda b, i: (jnp.maximum(b * ns + i - 1, 0), 0)), tab, tab, vec, vec],
        out_specs=[pspec, acc, acc, pl.BlockSpec((p_cols, d), lambda b, i: (0, 0))],
        out_shape=[SDS((t, p_cols), BF16), SDS((8, PAIR), F32), SDS((8, PAIR), F32), SDS((p_cols, d), BF16)],
        scratch_shapes=[pltpu.VMEM((p_cols, d), F32), pltpu.VMEM((ts, p_cols), BF16)],
        compiler_params=_cp("arbitrary", "arbitrary"),
        name="dqkprep",
    )(dqa, dka, dva, dqb, dkb, dvb, proj, h1, h1, cos, sin_signed, gq, gk)


def _dx_final(dproj, w_t, x2, dx1, g1, tm, grads):
    t, d = x2.shape
    p_cols = w_t.shape[0]
    ng = len(grads)
    nsteps = t // tm

    def body(dp_ref, w_ref, x_ref, dx1_ref, g_ref, *rest):
        grad_refs, (dx_ref, dg_ref), parts = rest[:ng], rest[ng:ng + 2], rest[ng + 2:2 * ng + 2]
        start, wait = _direct_exchange("scatter", grad_refs, parts, *rest[2 * ng + 2:])

        @pl.when(pl.program_id(0) == 0)
        def _():
            start()
            dg_ref[...] = jnp.zeros_like(dg_ref)

        dh = _dot(dp_ref[...], w_ref[...])
        g = g_ref[...]
        _, n, r = _rms_fwd(x_ref[...], g)
        dx, dgt = _rms_bwd(n, r, g, dh)
        dx_ref[...] = dx1_ref[...] + dx
        dg_ref[0:1, :] += jnp.sum(dgt, axis=0, keepdims=True)
        pl.when(pl.program_id(0) == nsteps - 1)(wait)

    tile = pl.BlockSpec((tm, d), lambda i: (i, 0))
    anyspec = pl.BlockSpec(memory_space=pl.ANY)
    res = pl.pallas_call(
        body,
        grid=(nsteps,),
        in_specs=[pl.BlockSpec((tm, p_cols), lambda i: (i, 0)),
                  pl.BlockSpec((p_cols, d), lambda i: (0, 0)),
                  tile, tile, pl.BlockSpec((1, d), lambda i: (0, 0))] + [anyspec] * ng,
        out_specs=[tile, pl.BlockSpec((8, d), lambda i: (0, 0))] + [anyspec] * ng,
        out_shape=[SDS((t, d), F32), SDS((8, d), F32)] + [SDS(g.shape, g.dtype) for g in grads],
        scratch_shapes=_exchange_scratch(ng),
        compiler_params=_cp("arbitrary"),
        name="dx_final",
    )(dproj, w_t, x2, dx1, g1, *grads)
    return res[0], res[1], res[2:]


def _adamw_math(w, g, m, v):
    m = ADAM_B1 * m + (1.0 - ADAM_B1) * g
    v = ADAM_B2 * v + (1.0 - ADAM_B2) * (g * g)
    m_hat = m / (1.0 - ADAM_B1 ** ADAM_STEP)
    v_hat = v / (1.0 - ADAM_B2 ** ADAM_STEP)
    delta = -ADAM_LR * (m_hat / (jnp.sqrt(v_hat) + ADAM_EPS) + ADAM_WD * w)
    return delta, m, v


def _adamw_sum(parts, w, m, v, tr, name):
    rows, cols = w.shape

    def body(p_ref, w_ref, m_ref, v_ref, g_ref, d_ref, nm_ref, nv_ref):
        g = p_ref[0].astype(F32)
        for s in range(1, N_DEV):
            g = g + p_ref[s].astype(F32)
        g_ref[...] = g
        d_ref[...], nm_ref[...], nv_ref[...] = _adamw_math(w_ref[...], g, m_ref[...], v_ref[...])

    tr = min(tr, rows)
    tile = pl.BlockSpec((tr, cols), lambda i: (i, 0))
    return pl.pallas_call(
        body,
        grid=(rows // tr,),
        in_specs=[pl.BlockSpec((N_DEV, tr, cols), lambda i: (0, i, 0)), tile, tile, tile],
        out_specs=[tile] * 4,
        out_shape=[SDS((rows, cols), F32)] * 4,
        compiler_params=_cp("parallel"),
        name=name,
    )(parts, w, m, v)


def _adamw_small(vec, rel, ws, ms, vs):
    hb = ws[6].shape[1]
    n = len(ws)

    def body(vec_ref, rel_ref, *rest):
        w_refs, m_refs, v_refs = rest[:n], rest[n:2 * n], rest[2 * n:3 * n]
        loss_ref, outs = rest[3 * n], rest[3 * n + 1:]
        grads = [vec_ref[0:1, :], vec_ref[1:2, :], vec_ref[2:3, :], vec_ref[3:4, :],
                 vec_ref[4:5, 0:HEAD_DIM], vec_ref[4:5, SMALL_LANES:SMALL_LANES + HEAD_DIM],
                 vec_ref[4:5, 2 * SMALL_LANES:2 * SMALL_LANES + hb], rel_ref[:, 0:hb]]
        loss_ref[...] = vec_ref[4:5, 3 * SMALL_LANES:3 * SMALL_LANES + 1]
        for p, g in enumerate(grads):
            g_ref, d_ref, nm_ref, nv_ref = outs[4 * p:4 * p + 4]
            g_ref[...] = g
            d_ref[...], nm_ref[...], nv_ref[...] = _adamw_math(w_refs[p][...], g, m_refs[p][...], v_refs[p][...])

    vm = pl.BlockSpec(memory_space=pltpu.VMEM)
    res = pl.pallas_call(
        body,
        in_specs=[vm] * (2 + 3 * n),
        out_specs=[vm] * (1 + 4 * n),
        out_shape=[SDS((1, 1), F32)] + [SDS(w.shape, F32) for w in ws for _ in range(4)],
        name="adamw_small",
    )(vec, rel, *ws, *ms, *vs)
    return res[0], [res[1 + 4 * p:5 + 4 * p] for p in range(n)]


def _local_step(x, loss_target, win_s, wo_s, wup_s, wdn_s, g_pre_mix, g_post_mix, q_norm_a, k_norm_a, sink_b,
                rel_bias, g_pre_ffn, g_post_ffn):
    bl, s_len, d = x.shape
    t = bl * s_len
    nh = d // HEAD_DIM
    ha = nh // 2
    kva = ha // GROUP
    hb = nh - ha
    kvb = hb // GROUP
    tm = 512
    tp = min(1024, t)
    tw = min(4096, t)
    ts = min(512, s_len)
    tq, tk = 2 * BLOCK, min(512, s_len // 2)

    x2 = x.reshape(t, d)
    tg2 = loss_target.reshape(t, d)
    cos, sin_signed = _rope_tables(s_len)
    gq2, gk2 = jnp.tile(q_norm_a, (1, 2)), jnp.tile(k_norm_a, (1, 2))
    a = jnp.arange(BLOCK, dtype=jnp.int32)
    c = jnp.arange(SPAN, dtype=jnp.int32)
    bucket_t = _t5_bucket(c[:, None] - BLOCK - a[None, :])
    bucket_t4 = jnp.tile(bucket_t, (1, GROUP))
    (win_g,), bias_t = _weight_gather([win_s], bucket_t, rel_bias)
    w_in_t = win_g.reshape(-1, d)
    p_cols = w_in_t.shape[0]

    h1, proj = _inproj(x2, g_pre_mix, w_in_t, tp)
    qa, ka, kat, va, vat, qb, kb, kbt, vb, vbt = _qkprep(
        proj, cos, sin_signed, gq2, gk2, bl, s_len, ha, kva, hb, kvb, ts)
    oa, lse_a, (wo_g, wup_g, wdn_g) = _attn_a_fwd(qa, ka, vat, tq, tk, [wo_s, wup_s, wdn_s])
    wo = wo_g.reshape(-1, d)
    wdn = wdn_g.reshape(-1, d)
    ob, lse_b = _attn_b_fwd(qb, kb, vbt, bias_t, sink_b, s_len)
    mix, x1, h2 = _mixout(oa, ob, wo, x2, g_post_mix, g_pre_ffn, tp)
    u, df, dy, dg4, loss8 = _ffn_fwd(h2, wup_g, wdn, x1, tg2, g_post_ffn, tm, FFN_BLOCKS_PER_STEP)

    dpre, dx1, dmix, dg3, dg2 = _ffn_bwd(df, u, wdn, wup_g, x1, dy, mix, g_pre_ffn, g_post_mix, FFN_BWD_TOKENS,
                                         FFN_BLOCKS_PER_STEP)
    gw_dn = _wgrad_rows(u, df, N_DEV, tw, "wgrad_down", square=True)
    gw_up = _wgrad_cols(h2, dpre, N_DEV, tw, "wgrad_up")
    gw_o = _wgrad_o(oa, ob, dmix, N_DEV, min(2048, t))
    doa, dob = _attn_out_bwd(dmix, wo, oa.shape[1], tp)
    dqa, dka, dva, (p_o, p_up, p_dn) = _attn_a_bwd(qa, ka, kat, va, doa, oa, lse_a, tq, tk, [gw_o, gw_up, gw_dn])
    dqb, dkb, dvb, dsum, dsink = _attn_b_bwd(qb, kb, kbt, vb, dob, ob, lse_b, bias_t, sink_b, s_len)
    drel_g, dsink_g = _bias_reduce(dsum, dsink, bucket_t4)
    dproj, dgq, dgk, gw_in_t = _dqkprep(dqa, dka, dva, dqb, dkb, dvb, proj, h1, cos, sin_signed, gq2, gk2, s_len, ts)
    gw_in_t = gw_in_t.reshape(N_DEV, -1, d)
    grad_x, dg1, (p_in,) = _dx_final(dproj, w_in_t, x2, dx1, g_pre_mix, tp, [gw_in_t])

    vec, rel = _small_allreduce([dg1, dg2, dg3, dg4], dgq, dgk, dsink_g, drel_g, loss8)
    return grad_x.reshape(bl, s_len, d), p_in, p_o, p_up, p_dn, vec, rel


def kernel(x, w_in, w_o, g_pre_mix, g_post_mix, q_norm_a, k_norm_a, sink_b, rel_bias, g_pre_ffn, w_ffn_up, w_ffn_down, g_post_ffn, loss_target, m_w_in, m_w_o, m_g_pre_mix, m_g_post_mix, m_q_norm_a, m_k_norm_a, m_sink_b, m_rel_bias, m_g_pre_ffn, m_w_ffn_up, m_w_ffn_down, m_g_post_ffn, v_w_in, v_w_o, v_g_pre_mix, v_g_post_mix, v_q_norm_a, v_k_norm_a, v_sink_b, v_rel_bias, v_g_pre_ffn, v_w_ffn_up, v_w_ffn_down, v_g_post_ffn):
    w_in_t = w_in[0].T

    grad_x, p_in, p_o, p_up, p_dn, vec, rel = _local_step(
        x, loss_target, w_in_t.astype(BF16), w_o[0].astype(BF16), w_ffn_up[0].astype(BF16), w_ffn_down[0].astype(BF16),
        g_pre_mix, g_post_mix, q_norm_a, k_norm_a, sink_b, rel_bias, g_pre_ffn, g_post_ffn)

    big = {
        "w_in": [a.T for a in _adamw_sum(p_in, w_in_t, m_w_in[0].T, v_w_in[0].T, 192, "adamw_in")],
        "w_o": _adamw_sum(p_o, w_o[0], m_w_o[0], v_w_o[0], 128, "adamw_o"),
        "w_up": _adamw_sum(p_up, w_ffn_up[0], m_w_ffn_up[0], v_w_ffn_up[0], 256, "adamw_up"),
        "w_dn": _adamw_sum(p_dn, w_ffn_down[0], m_w_ffn_down[0], v_w_ffn_down[0], 256, "adamw_down"),
    }
    loss, small = _adamw_small(
        vec, rel,
        [g_pre_mix, g_post_mix, g_pre_ffn, g_post_ffn, q_norm_a, k_norm_a, sink_b, rel_bias],
        [m_g_pre_mix, m_g_post_mix, m_g_pre_ffn, m_g_post_ffn, m_q_norm_a, m_k_norm_a, m_sink_b, m_rel_bias],
        [v_g_pre_mix, v_g_post_mix, v_g_pre_ffn, v_g_post_ffn, v_q_norm_a, v_k_norm_a, v_sink_b, v_rel_bias])
    s_pre_mix, s_post_mix, s_pre_ffn, s_post_ffn, s_qn, s_kn, s_sink, s_rel = small

    def outs(kind):
        return [big["w_in"][kind][None], big["w_o"][kind][None], s_pre_mix[kind], s_post_mix[kind], s_qn[kind],
                s_kn[kind], s_sink[kind], s_rel[kind], s_pre_ffn[kind], big["w_up"][kind][None],
                big["w_dn"][kind][None], s_post_ffn[kind]]

    return (loss.reshape(()), grad_x, *outs(0), *outs(1), *outs(2), *outs(3))
```

```python
import functools

import jax
import jax.numpy as jnp
import numpy as np
from jax import lax
from jax.experimental import pallas as pl
from jax.experimental.pallas import tpu as pltpu

F32 = jnp.float32
BF16 = jnp.bfloat16
SDS = jax.ShapeDtypeStruct

N_DEV = 8
HEAD_DIM = 64
GROUP = 4
BLOCK = 128
SPAN = 3 * BLOCK
GRID_W = 64
N_BUCKETS = 32
MAX_DISTANCE = 128
ROPE_THETA = 10000.0
EPS = 1e-6
NEG_INF = -1e30
SCALE = HEAD_DIM ** -0.5
VT_PAD = 16

ADAM_LR = 0.001
ADAM_B1 = 0.9
ADAM_B2 = 0.999
ADAM_EPS = 1e-08
ADAM_WD = 0.01
ADAM_STEP = 10

VMEM_LIMIT = 56 * 1024 * 1024
MESH = pl.DeviceIdType.MESH


def _cp(*sem):
    return pltpu.CompilerParams(dimension_semantics=sem, vmem_limit_bytes=VMEM_LIMIT)


def _dot(a, b):
    return jnp.dot(a, b, preferred_element_type=F32)


def _dot_nt(a, b):
    return lax.dot_general(a, b, (((1,), (1,)), ((), ())), preferred_element_type=F32)


def _dot_tn(a, b):
    return lax.dot_general(a, b, (((0,), (0,)), ((), ())), preferred_element_type=F32)


def _rms_fwd(x, g):
    r = lax.rsqrt(jnp.mean(x * x, axis=-1, keepdims=True) + EPS)
    n = x * r
    return n * g, n, r


def _rms_bwd(n, r, g, dy):
    gd = g * dy
    dx = r * (gd - n * jnp.mean(n * gd, axis=-1, keepdims=True))
    return dx, dy * n


def _rope_tables(s_len):
    rows = s_len // GRID_W
    row = np.repeat(np.arange(rows, dtype=np.int32), GRID_W)
    col = np.tile(np.arange(GRID_W, dtype=np.int32), rows)
    nf = HEAD_DIM // 4
    freqs = np.float32(ROPE_THETA) ** (-np.arange(nf, dtype=np.float32) / np.float32(nf))
    ang_r = row.astype(np.float32)[:, None] * freqs[None, :]
    ang_c = col.astype(np.float32)[:, None] * freqs[None, :]
    cr, sr, cc, sc = np.cos(ang_r), np.sin(ang_r), np.cos(ang_c), np.sin(ang_c)
    cos = np.concatenate([cr, cr, cc, cc] * 2, axis=-1).astype(np.float32)
    sin_signed = np.concatenate([-sr, sr, -sc, sc] * 2, axis=-1).astype(np.float32)
    return jnp.asarray(cos), jnp.asarray(sin_signed)


def _t5_bucket(rel):
    nb = N_BUCKETS // 2
    ret = (rel > 0).astype(jnp.int32) * nb
    n = jnp.abs(rel)
    max_exact = nb // 2
    nf = jnp.maximum(n, 1).astype(F32)
    large = max_exact + (jnp.log(nf / max_exact) / np.float32(np.log(MAX_DISTANCE / max_exact))
                         * (nb - max_exact)).astype(jnp.int32)
    large = jnp.minimum(large, nb - 1)
    return ret + jnp.where(n < max_exact, n, large)


def _mesh_pos():
    return lax.axis_index("x"), lax.axis_index("y"), lax.axis_index("c")


def _lin(p):
    return 4 * p[0] + 2 * p[1] + p[2]


def _bias_tables(bkt_ref, tbl_ref, out_ref, hb):
    bkt = bkt_ref[...]
    ci = lax.broadcasted_iota(jnp.int32, (SPAN, BLOCK), 0)
    qi = lax.broadcasted_iota(jnp.int32, (SPAN, BLOCK), 1)
    band = jnp.abs(ci - BLOCK - qi) <= BLOCK
    masks = (band, band & (ci >= BLOCK), band & (ci < 2 * BLOCK))
    for h in range(hb):
        acct = jnp.zeros((SPAN, BLOCK), F32)
        for b in range(N_BUCKETS):
            acct = jnp.where(bkt == b, tbl_ref[b, h], acct)
        lanes = slice((h % GROUP) * BLOCK, (h % GROUP + 1) * BLOCK)
        for var, mask in enumerate(masks):
            out_ref[var, h // GROUP, :, lanes] = jnp.where(mask, acct, NEG_INF)


def _weight_gather(shards, bucket_t, rel_bias):
    n = len(shards)
    hb = rel_bias.shape[1]

    def body(*refs):
        xs, (bkt_ref, tbl_ref), outs, bias_ref = refs[:n], refs[n:n + 2], refs[n + 2:2 * n + 2], refs[2 * n + 2]
        send_sems, recv_sems, local_sems = refs[2 * n + 3:]
        x, y, c = _mesh_pos()
        me, sibling = (x, y, c), (x, y, 1 - c)
        chips = [(1 - x, y), (x, 1 - y), (1 - x, 1 - y)]

        def copy(a, k, block, to, src=None):
            slot = outs[a].at[_lin(block)]
            return pltpu.make_async_remote_copy(
                src_ref=slot if src is None else src, dst_ref=slot,
                send_sem=send_sems.at[a, k], recv_sem=recv_sems.at[a, k],
                device_id=to, device_id_type=MESH)

        started = []
        for a in range(n):
            mine = pltpu.make_async_copy(xs[a], outs[a].at[_lin(me)], local_sems.at[a])
            mine.start()
            started.append(mine)
        sends = []
        for a in range(n):
            first = [copy(a, 0, me, sibling, src=xs[a])]
            first += [copy(a, 1 + j, me, (*chip, c), src=xs[a]) for j, chip in enumerate(chips)]
            for cp in first:
                cp.start()
            sends += first
        _bias_tables(bkt_ref, tbl_ref, bias_ref, hb)
        for a in range(n):
            for j, chip in enumerate(chips):
                copy(a, 1 + j, (*chip, c), me).wait_recv()
                fwd = copy(a, 4 + j, (*chip, c), sibling)
                fwd.start()
                sends.append(fwd)
        for a in range(n):
            copy(a, 0, sibling, me).wait_recv()
            for j, chip in enumerate(chips):
                copy(a, 4 + j, (*chip, 1 - c), me).wait_recv()
        for cp in sends:
            cp.wait_send()
        for mine in started:
            mine.wait()

    anyspec = pl.BlockSpec(memory_space=pl.ANY)
    vm = pl.BlockSpec(memory_space=pltpu.VMEM)
    res = pl.pallas_call(
        body,
        out_shape=[SDS((N_DEV,) + s.shape, s.dtype) for s in shards]
        + [SDS((3, hb // GROUP, SPAN, GROUP * BLOCK), F32)],
        in_specs=[anyspec] * n + [vm, pl.BlockSpec(memory_space=pltpu.SMEM)],
        out_specs=[anyspec] * n + [vm],
        scratch_shapes=[pltpu.SemaphoreType.DMA((n, 7)), pltpu.SemaphoreType.DMA((n, 7)),
                        pltpu.SemaphoreType.DMA((n,))],
        name="weight_gather",
    )(*shards, bucket_t, rel_bias)
    return res[:n], res[n]


def _direct_exchange(kind, ins, outs, send_sems, recv_sems, local_sems):
    x, y, c = _mesh_pos()
    me = (x, y, c)
    peers = [(x, y, 1 - c), (1 - x, y, c), (x, 1 - y, c), (1 - x, 1 - y, c),
             (1 - x, y, 1 - c), (x, 1 - y, 1 - c), (1 - x, 1 - y, 1 - c)]

    def src(a, to):
        return ins[a] if kind == "gather" else ins[a].at[_lin(to)]

    def remote(a, k, to, frm):
        return pltpu.make_async_remote_copy(
            src_ref=src(a, to), dst_ref=outs[a].at[_lin(frm)],
            send_sem=send_sems.at[a, k], recv_sem=recv_sems.at[a, k],
            device_id=to, device_id_type=MESH)

    n = len(ins)
    sends = [remote(a, k, p, me) for a in range(n) for k, p in enumerate(peers)]
    arrivals = [remote(a, k, p, p) for a in range(n) for k, p in enumerate(peers)]
    local = [pltpu.make_async_copy(src(a, me), outs[a].at[_lin(me)], local_sems.at[a]) for a in range(n)]

    def start():
        for cp in local + sends:
            cp.start()

    def wait():
        for cp in arrivals:
            cp.wait_recv()
        for cp in sends:
            cp.wait_send()
        for cp in local:
            cp.wait()

    return start, wait


def _exchange_scratch(n):
    return [pltpu.SemaphoreType.DMA((n, 7)), pltpu.SemaphoreType.DMA((n, 7)), pltpu.SemaphoreType.DMA((n,))]


SMALL_LANES = 128


def _small_allreduce(dg_rows, dgq, dgk, dsink_g, drel_g, loss8):
    d = dg_rows[0].shape[1]
    kv = dsink_g.shape[0]

    def body(g1_ref, g2_ref, g3_ref, g4_ref, gq_ref, gk_ref, sk_ref, rl_ref, ls_ref, vec_ref, rel_ref,
             vbuf, rbuf, vland, rland, send_sems, recv_sems):
        x, y, c = _mesh_pos()
        me = (x, y, c)
        peers = [(x, y, 1 - c), (1 - x, y, c), (x, 1 - y, c), (1 - x, 1 - y, c),
                 (1 - x, y, 1 - c), (x, 1 - y, 1 - c), (1 - x, 1 - y, 1 - c)]
        vbuf[...] = jnp.zeros_like(vbuf)
        rbuf[...] = jnp.zeros_like(rbuf)
        for row, ref in enumerate((g1_ref, g2_ref, g3_ref, g4_ref)):
            vbuf[row:row + 1, :] = ref[0:1, :]
        vbuf[4:5, 0:HEAD_DIM] = gq_ref[0:1, 0:HEAD_DIM] + gq_ref[0:1, HEAD_DIM:PAIR]
        vbuf[4:5, SMALL_LANES:SMALL_LANES + HEAD_DIM] = gk_ref[0:1, 0:HEAD_DIM] + gk_ref[0:1, HEAD_DIM:PAIR]
        for g in range(kv):
            vbuf[4:5, 2 * SMALL_LANES + g * GROUP:2 * SMALL_LANES + (g + 1) * GROUP] = sk_ref[g, 0:1, 0:GROUP]
            rbuf[:, g * GROUP:(g + 1) * GROUP] = rl_ref[g, :, 0:GROUP]
        vbuf[4:5, 3 * SMALL_LANES:3 * SMALL_LANES + 1] = ls_ref[0:1, 0:1]

        def copies(k, to, frm):
            return [pltpu.make_async_remote_copy(
                src_ref=buf, dst_ref=land.at[_lin(frm)], send_sem=send_sems.at[a, k], recv_sem=recv_sems.at[a, k],
                device_id=to, device_id_type=MESH) for a, (buf, land) in enumerate(((vbuf, vland), (rbuf, rland)))]

        sends = [cp for k, p in enumerate(peers) for cp in copies(k, p, me)]
        for cp in sends:
            cp.start()
        vland[_lin(me)] = vbuf[...]
        rland[_lin(me)] = rbuf[...]
        for k, p in enumerate(peers):
            for cp in copies(k, p, p):
                cp.wait_recv()
        for cp in sends:
            cp.wait_send()
        vacc, racc = vland[0], rland[0]
        for s in range(1, N_DEV):
            vacc, racc = vacc + vland[s], racc + rland[s]
        vec_ref[...] = vacc
        rel_ref[...] = racc

    vm = pl.BlockSpec(memory_space=pltpu.VMEM)
    return pl.pallas_call(
        body,
        out_shape=[SDS((8, d), F32), SDS((N_BUCKETS, 128), F32)],
        in_specs=[vm] * 9,
        out_specs=[vm, vm],
        scratch_shapes=[pltpu.VMEM((8, d), F32), pltpu.VMEM((N_BUCKETS, 128), F32),
                        pltpu.VMEM((N_DEV, 8, d), F32), pltpu.VMEM((N_DEV, N_BUCKETS, 128), F32),
                        pltpu.SemaphoreType.DMA((2, 7)), pltpu.SemaphoreType.DMA((2, 7))],
        name="small_allreduce",
    )(*dg_rows, dgq, dgk, dsink_g, drel_g, loss8)


PAIR = 2 * HEAD_DIM


def _pair_masks(ts):
    lane = lax.broadcasted_iota(jnp.int32, (ts, PAIR), 1)
    return lane < HEAD_DIM, (lane % 32) < 16


def _pair_mean(v, low):
    del low
    r = lax.broadcasted_iota(jnp.int32, (PAIR, PAIR), 0) // HEAD_DIM
    c = lax.broadcasted_iota(jnp.int32, (PAIR, PAIR), 1) // HEAD_DIM
    same_head = (r == c).astype(BF16)
    hi = v.astype(BF16)
    lo = (v - hi.astype(F32)).astype(BF16)
    return (_dot(hi, same_head) + _dot(lo, same_head)) * (1.0 / HEAD_DIM)


def _pair_partner(v, first):
    return jnp.where(first, pltpu.roll(v, PAIR - 16, 1), pltpu.roll(v, 16, 1))


def _inproj_qkprep(x2, g1, w_t, cos, sin_signed, gq, gk, bl, s_len, ha, kva, hb, kvb, ts):
    t, d = x2.shape
    p_cols = w_t.shape[0]
    assert ha % 2 == 0 and kva % 2 == 0 and hb % 2 == 0 and kvb % 2 == 0
    ns = s_len // ts
    nt = bl * ns
    sp = s_len + 2 * BLOCK

    def body(*refs):
        kb_ref, kbt_ref, vb_ref, vbt_ref, p_even, p_odd = refs[-6:]
        s = pl.program_id(0)
        i = lax.rem(jnp.maximum(s - 1, 0), ns)

        @pl.when(s == 0)
        def _():
            p_odd[...] = jnp.zeros_like(p_odd)

        @pl.when(i == 0)
        def _():
            zeros = jnp.zeros((kvb, BLOCK, HEAD_DIM), BF16)
            zeros_t = jnp.zeros((kvb, HEAD_DIM + VT_PAD, BLOCK), BF16)
            for ref in (kb_ref, vb_ref):
                ref[0, :, 0:BLOCK, :] = zeros
                ref[0, :, sp - BLOCK:sp, :] = zeros
            kbt_ref[0, :, :, 0:BLOCK] = zeros_t[:, 0:HEAD_DIM]
            kbt_ref[0, :, :, sp - BLOCK:sp] = zeros_t[:, 0:HEAD_DIM]
            vbt_ref[0, :, :, 0:BLOCK] = zeros_t
            vbt_ref[0, :, :, sp - BLOCK:sp] = zeros_t

        even = lax.rem(s, 2) == 0
        pl.when(even)(functools.partial(tile_work, p_even, p_odd, i, *refs[:-2]))
        pl.when(jnp.logical_not(even))(functools.partial(tile_work, p_odd, p_even, i, *refs[:-2]))

    def tile_work(p_new, p_ref, i, x_ref, g1_ref, w_ref, cos_ref, sin_ref, gq_ref, gk_ref, h_ref, po_ref, qa_ref,
                  ka_ref, kat_ref, va_ref, vat_ref, qb_ref, kb_ref, kbt_ref, vb_ref, vbt_ref):
        y, _, _ = _rms_fwd(x_ref[...], g1_ref[...])
        h = y.astype(BF16)
        h_ref[...] = h
        n_parts = 6
        pw = p_cols // n_parts

        def project(c):
            p_new[:, c * pw:(c + 1) * pw] = _dot_nt(h, w_ref[c * pw:(c + 1) * pw, :])

        cs, sn = cos_ref[...], sin_ref[...]
        low, first = _pair_masks(ts)
        ones_row = (lax.broadcasted_iota(jnp.int32, (VT_PAD, ts), 0) == 0).astype(BF16)
        heads = (slice(0, HEAD_DIM), slice(HEAD_DIM, PAIR))

        def pair(p):
            v = p_ref[:, p * PAIR:(p + 1) * PAIR]
            po_ref[:, p * PAIR:(p + 1) * PAIR] = v
            return v

        def normrope(x, g):
            y = x * lax.rsqrt(_pair_mean(x * x, low) + EPS) * g
            return y * cs + _pair_partner(y, first) * sn

        eye = (lax.broadcasted_iota(jnp.int32, (PAIR, PAIR), 0)
               == lax.broadcasted_iota(jnp.int32, (PAIR, PAIR), 1)).astype(BF16)

        def transposed(xb):
            return _dot_nt(eye, xb).astype(BF16)

        def prep_qa(p):
            qa_ref[:, p * PAIR:(p + 1) * PAIR] = (normrope(pair(p), gq_ref[...]) * SCALE).astype(BF16)

        def prep_kva(p):
            base = ha // 2
            k = normrope(pair(base + p), gk_ref[...]).astype(BF16)
            v = pair(base + kva // 2 + p).astype(BF16)
            kt, vt = transposed(k), transposed(v)
            for e, lanes in enumerate(heads):
                ka_ref[0, 2 * p + e] = k[:, lanes]
                va_ref[0, 2 * p + e] = v[:, lanes]
                kat_ref[0, 2 * p + e] = kt[lanes, :]
                vat_ref[0, 2 * p + e, 0:HEAD_DIM, :] = vt[lanes, :]
                vat_ref[0, 2 * p + e, HEAD_DIM:HEAD_DIM + VT_PAD, :] = ones_row

        def prep_qb(p):
            base = ha // 2 + kva
            qb_ref[:, p * PAIR:(p + 1) * PAIR] = (pair(base + p) * SCALE).astype(BF16)

        rows = pl.ds(pl.multiple_of(BLOCK + i * ts, BLOCK), ts)

        def prep_kvb(p):
            base = ha // 2 + kva + hb // 2
            k = pair(base + p).astype(BF16)
            v = pair(base + kvb // 2 + p).astype(BF16)
            kt, vt = transposed(k), transposed(v)
            for e, lanes in enumerate(heads):
                kb_ref[0, 2 * p + e, rows, :] = k[:, lanes]
                vb_ref[0, 2 * p + e, rows, :] = v[:, lanes]
                kbt_ref[0, 2 * p + e, :, rows] = kt[lanes, :]
                vbt_ref[0, 2 * p + e, 0:HEAD_DIM, rows] = vt[lanes, :]
                vbt_ref[0, 2 * p + e, HEAD_DIM:HEAD_DIM + VT_PAD, rows] = ones_row

        work = ([functools.partial(prep_qa, p) for p in range(ha // 2)]
                + [functools.partial(prep_kva, p) for p in range(kva // 2)]
                + [functools.partial(prep_qb, p) for p in range(hb // 2)]
                + [functools.partial(prep_kvb, p) for p in range(kvb // 2)])
        per_part = -(-len(work) // n_parts)
        for c in range(n_parts):
            for item in work[c * per_part:(c + 1) * per_part]:
                item()
            project(c)

    def cur(s):
        return jnp.minimum(s, nt - 1)

    def prev(s):
        return jnp.maximum(s - 1, 0) // ns, lax.rem(jnp.maximum(s - 1, 0), ns)

    def hm(nh):
        return pl.BlockSpec((1, nh, ts, HEAD_DIM), lambda s: (prev(s)[0], 0, prev(s)[1], 0))

    def hm_t(nh, rows):
        return pl.BlockSpec((1, nh, rows, ts), lambda s: (prev(s)[0], 0, 0, prev(s)[1]))

    def tokmajor(nh):
        return pl.BlockSpec((ts, nh * HEAD_DIM), lambda s: (jnp.maximum(s - 1, 0), 0))

    def padded(nh):
        return pl.BlockSpec((1, nh, sp, HEAD_DIM), lambda s: (prev(s)[0], 0, 0, 0))

    def padded_t(nh, rows):
        return pl.BlockSpec((1, nh, rows, sp), lambda s: (prev(s)[0], 0, 0, 0))

    tab = pl.BlockSpec((ts, PAIR), lambda s: (prev(s)[1], 0))
    vec = pl.BlockSpec((1, PAIR), lambda s: (0, 0))
    return pl.pallas_call(
        body,
        grid=(nt + 1,),
        in_specs=[pl.BlockSpec((ts, d), lambda s: (cur(s), 0)),
                  pl.BlockSpec((1, d), lambda s: (0, 0)),
                  pl.BlockSpec((p_cols, d), lambda s: (0, 0)),
                  tab, tab, vec, vec],
        out_specs=[pl.BlockSpec((ts, d), lambda s: (cur(s), 0)), tokmajor(p_cols // HEAD_DIM),
                   tokmajor(ha), hm(kva), hm_t(kva, HEAD_DIM), hm(kva), hm_t(kva, HEAD_DIM + VT_PAD),
                   tokmajor(hb), padded(kvb), padded_t(kvb, HEAD_DIM), padded(kvb),
                   padded_t(kvb, HEAD_DIM + VT_PAD)],
        out_shape=[SDS((t, d), BF16), SDS((t, p_cols), F32),
                   SDS((t, ha * HEAD_DIM), BF16), SDS((bl, kva, s_len, HEAD_DIM), BF16),
                   SDS((bl, kva, HEAD_DIM, s_len), BF16),
                   SDS((bl, kva, s_len, HEAD_DIM), BF16), SDS((bl, kva, HEAD_DIM + VT_PAD, s_len), BF16),
                   SDS((t, hb * HEAD_DIM), BF16),
                   SDS((bl, kvb, sp, HEAD_DIM), BF16), SDS((bl, kvb, HEAD_DIM, sp), BF16),
                   SDS((bl, kvb, sp, HEAD_DIM), BF16), SDS((bl, kvb, HEAD_DIM + VT_PAD, sp), BF16)],
        scratch_shapes=[pltpu.VMEM((ts, p_cols), F32)] * 2,
        compiler_params=_cp("arbitrary"),
        name="inproj_qkprep",
    )(x2, g1, w_t, cos, sin_signed, gq, gk)


def _attn_a_fwd(qa, ka, vat, tq, tk, shards):
    bl, kv, s_len, _ = ka.shape
    ha = qa.shape[1] // HEAD_DIM
    va_rows = vat.shape[2]
    nq, nk = s_len // tq, s_len // tk
    assert nk % 2 == 0
    r = GROUP * tq
    ns = len(shards)

    def body(q_ref, qn_ref, k_ref, v_ref, *rest):
        shard_refs, (o_ref, l_ref), gathered = rest[:ns], rest[ns:ns + 2], rest[ns + 2:2 * ns + 2]
        st_sc, send_sems, recv_sems, local_sems = rest[2 * ns + 2:]
        i = pl.program_id(2)
        step_id = (pl.program_id(0) * kv + pl.program_id(1)) * nq + i
        start, wait = _direct_exchange("gather", shard_refs, gathered, send_sems, recv_sems, local_sems)
        pl.when(step_id == 0)(start)

        q = _heads_t(q_ref[...]).astype(BF16)

        def scores(c, qv):
            return _dot(k_ref[0, 0, pl.ds(pl.multiple_of(c * tk, tk), tk), :], qv)

        def fold(st, c, carry):
            m_old, acc = carry
            m_new = jnp.maximum(m_old, jnp.max(st, axis=0, keepdims=True))
            pt = jnp.exp(st - m_new).astype(BF16)
            vt = v_ref[0, 0, :, pl.ds(pl.multiple_of(c * tk, tk), tk)]
            return m_new, jnp.exp(m_old - m_new) * acc + _dot(vt, pt)

        @pl.when(i == 0)
        def _():
            st_sc[0] = scores(0, q)

        def step(c2, carry):
            c = 2 * c2
            st_sc[1] = scores(c + 1, q)
            carry = fold(st_sc[0], c, carry)
            st_sc[0] = scores(c + 2, q)
            return fold(st_sc[1], c + 1, carry)

        carry = (jnp.full((1, r), -jnp.inf, F32), jnp.zeros((va_rows, r), F32))
        for c2 in range(nk // 2 - 1):
            carry = step(c2, carry)
        st_sc[1] = scores(nk - 1, q)
        carry = fold(st_sc[0], nk - 2, carry)
        st_sc[0] = scores(0, _heads_t(qn_ref[...]).astype(BF16))
        m, acc = fold(st_sc[1], nk - 1, carry)
        l = acc[HEAD_DIM:HEAD_DIM + 1, :]
        o_ref[...] = _heads_t_inv(acc[0:HEAD_DIM, :] / l).astype(BF16)
        l_ref[0, 0, 0] = jnp.broadcast_to(m + jnp.log(l), (8, r))
        pl.when(step_id == bl * kv * nq - 1)(wait)

    anyspec = pl.BlockSpec(memory_space=pl.ANY)
    res = pl.pallas_call(
        body,
        grid=(bl, kv, nq),
        in_specs=[pl.BlockSpec((tq, GROUP * HEAD_DIM), lambda b, g, i: (b * nq + i, g)),
                  pl.BlockSpec((tq, GROUP * HEAD_DIM), lambda b, g, i: (b * nq + jnp.minimum(i + 1, nq - 1), g)),
                  pl.BlockSpec((1, 1, s_len, HEAD_DIM), lambda b, g, i: (b, g, 0, 0)),
                  pl.BlockSpec((1, 1, va_rows, s_len), lambda b, g, i: (b, g, 0, 0))] + [anyspec] * ns,
        out_specs=[pl.BlockSpec((tq, GROUP * HEAD_DIM), lambda b, g, i: (b * nq + i, g)),
                   pl.BlockSpec((1, 1, 1, 8, r), lambda b, g, i: (b, g, i, 0, 0))] + [anyspec] * ns,
        out_shape=[SDS((bl * s_len, ha * HEAD_DIM), BF16), SDS((bl, kv, nq, 8, r), F32)]
        + [SDS((N_DEV,) + s.shape, s.dtype) for s in shards],
        scratch_shapes=[pltpu.VMEM((2, tk, r), F32)] + _exchange_scratch(ns),
        compiler_params=_cp("arbitrary", "arbitrary", "arbitrary"),
        name="attn_a_fwd",
    )(qa, qa, ka, vat, *shards)
    return res[0], res[1], res[2:]


FFN_BLOCKS_PER_STEP = 8
FFN_BWD_TOKENS = 256
QB_PER_STEP = 16


def _bias_variant(n, nb):
    return jnp.where(n == 0, 1, jnp.where(n == nb - 1, 2, 0))


def _sink_row(sink_ref, g):
    return jnp.concatenate([jnp.full((1, BLOCK), sink_ref[0, g * GROUP + h], F32) for h in range(GROUP)], axis=1)


def _attn_b_fwd(qb, kb, vbt, bias_t, sink, s_len):
    bl, kv, sp, _ = kb.shape
    hb = qb.shape[1] // HEAD_DIM
    vt_rows = vbt.shape[2]
    nb = s_len // BLOCK
    nbs = min(QB_PER_STEP, nb)
    r = GROUP * BLOCK

    def body(q_ref, k_ref, vt_ref, bt_ref, sink_ref, o_ref, l_ref, st_sc, pb_sc):
        g, n0 = pl.program_id(1), pl.program_id(2) * nbs
        sink_row = _sink_row(sink_ref, g)

        def span(j):
            return pl.ds(pl.multiple_of((n0 + j) * BLOCK, BLOCK), SPAN)

        for j in range(nbs):
            qt = _heads_t(q_ref[j * BLOCK:(j + 1) * BLOCK, :]).astype(BF16)
            st_sc[j] = _dot(k_ref[0, 0, span(j), :], qt) + bt_ref[_bias_variant(n0 + j, nb), 0]
        maxes = []
        for j in range(nbs):
            st = st_sc[j]
            m = jnp.maximum(jnp.max(st, axis=0, keepdims=True), sink_row)
            pb_sc[j] = jnp.exp(st - m).astype(BF16)
            maxes.append(m)
        for j in range(nbs):
            m = maxes[j]
            acc = _dot(vt_ref[0, 0, :, span(j)], pb_sc[j])
            l = acc[HEAD_DIM:HEAD_DIM + 1, :] + jnp.exp(sink_row - m)
            o_ref[j * BLOCK:(j + 1) * BLOCK, :] = _heads_t_inv(acc[0:HEAD_DIM, :] / l).astype(BF16)
            l_ref[0, 0, j] = jnp.broadcast_to(m + jnp.log(l), (8, r))

    return pl.pallas_call(
        body,
        grid=(bl, kv, nb // nbs),
        in_specs=[pl.BlockSpec((nbs * BLOCK, GROUP * HEAD_DIM), lambda b, g, n: (b * (nb // nbs) + n, g)),
                  pl.BlockSpec((1, 1, sp, HEAD_DIM), lambda b, g, n: (b, g, 0, 0)),
                  pl.BlockSpec((1, 1, vt_rows, sp), lambda b, g, n: (b, g, 0, 0)),
                  pl.BlockSpec((3, 1, SPAN, r), lambda b, g, n: (0, g, 0, 0)),
                  pl.BlockSpec(memory_space=pltpu.SMEM)],
        out_specs=[pl.BlockSpec((nbs * BLOCK, GROUP * HEAD_DIM), lambda b, g, n: (b * (nb // nbs) + n, g)),
                   pl.BlockSpec((1, 1, nbs, 8, r), lambda b, g, n: (b, g, n, 0, 0))],
        out_shape=[SDS((bl * s_len, hb * HEAD_DIM), BF16), SDS((bl, kv, nb, 8, r), F32)],
        scratch_shapes=[pltpu.VMEM((nbs, SPAN, r), F32), pltpu.VMEM((nbs, SPAN, r), BF16)],
        compiler_params=_cp("parallel", "parallel", "arbitrary"),
        name="attn_b_fwd",
    )(qb, kb, vbt, bias_t, sink)


def _mixout(oa, ob, wo, x2, g2, g3, tm):
    t, d = x2.shape
    ca = oa.shape[1]

    def body(oa_ref, ob_ref, w_ref, x_ref, g2_ref, g3_ref, mix_ref, x1_ref, h2_ref):
        mix = _dot(oa_ref[...], w_ref[0:ca, :]) + _dot(ob_ref[...], w_ref[ca:, :])
        mix_ref[...] = mix
        y2, _, _ = _rms_fwd(mix, g2_ref[...])
        x1 = x_ref[...] + y2
        x1_ref[...] = x1
        y3, _, _ = _rms_fwd(x1, g3_ref[...])
        h2_ref[...] = y3.astype(BF16)

    tile = lambda w: pl.BlockSpec((tm, w), lambda i: (i, 0))
    vec = pl.BlockSpec((1, d), lambda i: (0, 0))
    return pl.pallas_call(
        body,
        grid=(t // tm,),
        in_specs=[tile(ca), tile(ob.shape[1]), pl.BlockSpec(wo.shape, lambda i: (0, 0)), tile(d), vec, vec],
        out_specs=[tile(d), tile(d), tile(d)],
        out_shape=[SDS((t, d), F32), SDS((t, d), F32), SDS((t, d), BF16)],
        compiler_params=_cp("parallel"),
        name="mixout",
    )(oa, ob, wo, x2, g2, g3)


def _ffn_fwd(h2, wup_g, wdn, x1, target, g4, tm, jb):
    t, d = x1.shape
    nblk, _, tf = wup_g.shape
    ff = nblk * tf
    nt = t // tm
    nj = nblk // jb

    def body(h_ref, wu_ref, wd_ref, x1_ref, tg_ref, g_ref, u_ref, df_ref, dy_ref, dg_ref, loss_ref, acc_sc):
        i, j = pl.program_id(0), pl.program_id(1)

        @pl.when(j == 0)
        def _():
            acc_sc[...] = jnp.zeros_like(acc_sc)

        @pl.when((i == 0) & (j == 0))
        def _():
            dg_ref[...] = jnp.zeros_like(dg_ref)
            loss_ref[...] = jnp.zeros_like(loss_ref)

        h = h_ref[...]
        squares = []
        for s in range(jb):
            u = jnp.maximum(_dot(h, wu_ref[s]), 0.0)
            u_ref[:, s * tf:(s + 1) * tf] = u.astype(BF16)
            squares.append((u * u).astype(BF16))
        acc_sc[...] += _dot(jnp.concatenate(squares, axis=1), wd_ref[...])

        @pl.when(j == nj - 1)
        def _():
            g = g_ref[...]
            y4, n, r = _rms_fwd(acc_sc[...], g)
            e = (x1_ref[...] + y4) - tg_ref[...]
            loss_ref[...] += jnp.sum(e * e) * (0.5 / d)
            dy = e * (1.0 / d)
            dy_ref[...] = dy
            df, dgt = _rms_bwd(n, r, g, dy)
            df_ref[...] = df.astype(BF16)
            dg_ref[0:1, :] += jnp.sum(dgt, axis=0, keepdims=True)

    tile = pl.BlockSpec((tm, d), lambda i, j: (i, 0))
    return pl.pallas_call(
        body,
        grid=(nt, nj),
        in_specs=[tile,
                  pl.BlockSpec((jb, d, tf), lambda i, j: (j, 0, 0)),
                  pl.BlockSpec((jb * tf, d), lambda i, j: (j, 0)),
                  tile, tile,
                  pl.BlockSpec((1, d), lambda i, j: (0, 0))],
        out_specs=[pl.BlockSpec((tm, jb * tf), lambda i, j: (i, j)), tile, tile,
                   pl.BlockSpec((8, d), lambda i, j: (0, 0)),
                   pl.BlockSpec((8, 128), lambda i, j: (0, 0))],
        out_shape=[SDS((t, ff), BF16), SDS((t, d), BF16), SDS((t, d), F32), SDS((8, d), F32), SDS((8, 128), F32)],
        scratch_shapes=[pltpu.VMEM((tm, d), F32)],
        compiler_params=_cp("arbitrary", "arbitrary"),
        name="ffn_fwd",
    )(h2, wup_g, wdn, x1, target, g4)


def _ffn_bwd(df, u, wdn, wup_g, x1, dy, mix, g3, g2, tm, jb):
    t, d = x1.shape
    nblk, _, tf = wup_g.shape
    nt = t // tm
    nj = nblk // jb

    def body(df_ref, u_ref, wd_ref, wu_ref, x1_ref, dy_ref, mix_ref, g3_ref, g2_ref,
             dpre_ref, dx1_ref, dmix_ref, dg3_ref, dg2_ref, acc_sc):
        i, j = pl.program_id(0), pl.program_id(1)

        @pl.when(j == 0)
        def _():
            acc_sc[...] = jnp.zeros_like(acc_sc)

        @pl.when((i == 0) & (j == 0))
        def _():
            dg3_ref[...] = jnp.zeros_like(dg3_ref)
            dg2_ref[...] = jnp.zeros_like(dg2_ref)

        du2 = _dot_nt(df_ref[...], wd_ref[...])
        dpre = (2.0 * u_ref[...].astype(F32) * du2).astype(BF16)
        dpre_ref[...] = dpre
        dh = _dot_nt(dpre[:, 0:tf], wu_ref[0])
        for s in range(1, jb):
            dh = dh + _dot_nt(dpre[:, s * tf:(s + 1) * tf], wu_ref[s])
        acc_sc[...] += dh

        @pl.when(j == nj - 1)
        def _():
            g3, g2 = g3_ref[...], g2_ref[...]
            _, n3, r3 = _rms_fwd(x1_ref[...], g3)
            dx, dgt3 = _rms_bwd(n3, r3, g3, acc_sc[...])
            dx1 = dy_ref[...] + dx
            dx1_ref[...] = dx1
            dg3_ref[0:1, :] += jnp.sum(dgt3, axis=0, keepdims=True)
            _, n2, r2 = _rms_fwd(mix_ref[...], g2)
            dmix, dgt2 = _rms_bwd(n2, r2, g2, dx1)
            dmix_ref[...] = dmix.astype(BF16)
            dg2_ref[0:1, :] += jnp.sum(dgt2, axis=0, keepdims=True)

    tile = pl.BlockSpec((tm, d), lambda i, j: (i, 0))
    vec = pl.BlockSpec((1, d), lambda i, j: (0, 0))
    acc8 = pl.BlockSpec((8, d), lambda i, j: (0, 0))
    return pl.pallas_call(
        body,
        grid=(nt, nj),
        in_specs=[tile,
                  pl.BlockSpec((tm, jb * tf), lambda i, j: (i, j)),
                  pl.BlockSpec((jb * tf, d), lambda i, j: (j, 0)),
                  pl.BlockSpec((jb, d, tf), lambda i, j: (j, 0, 0)),
                  tile, tile, tile, vec, vec],
        out_specs=[pl.BlockSpec((tm, jb * tf), lambda i, j: (i, j)), tile, tile, acc8, acc8],
        out_shape=[SDS(u.shape, BF16), SDS((t, d), F32), SDS((t, d), BF16), SDS((8, d), F32), SDS((8, d), F32)],
        scratch_shapes=[pltpu.VMEM((tm, d), F32)],
        compiler_params=_cp("arbitrary", "arbitrary"),
        name="ffn_bwd",
    )(df, u, wdn, wup_g, x1, dy, mix, g3, g2)


def _wgrad(a, b, a_spec, b_spec, out_block, out_shape, nj, nk, name, prep_a=None, prep_b=None):
    acc_shape = out_block[1:]

    def body(a_ref, b_ref, o_ref, acc_sc):
        k = pl.program_id(1)
        av = a_ref[...] if prep_a is None else prep_a(a_ref)
        bv = b_ref[...] if prep_b is None else prep_b(b_ref)
        part = _dot_tn(av, bv)

        @pl.when(k == 0)
        def _():
            acc_sc[...] = part

        @pl.when(k > 0)
        def _():
            acc_sc[...] += part

        @pl.when(k == nk - 1)
        def _():
            o_ref[0] = acc_sc[...].astype(BF16)

    return pl.pallas_call(
        body,
        grid=(nj, nk),
        in_specs=[a_spec, b_spec],
        out_specs=pl.BlockSpec(out_block, lambda j, k: (j, 0, 0)),
        out_shape=SDS(out_shape, BF16),
        scratch_shapes=[pltpu.VMEM(acc_shape, F32)],
        compiler_params=_cp("parallel", "arbitrary"),
        name=name,
    )(a, b)


def _wgrad_cols(a, b, nj, tt, name):
    t, m = a.shape
    bn = b.shape[1] // nj
    return _wgrad(a, b, pl.BlockSpec((tt, m), lambda j, k: (k, 0)), pl.BlockSpec((tt, bn), lambda j, k: (k, j)),
                  (1, m, bn), (nj, m, bn), nj, t // tt, name)


def _wgrad_rows(a, b, nj, tt, name, square=False):
    t, n = b.shape
    bm = a.shape[1] // nj

    def squared(a_ref):
        af = a_ref[...].astype(F32)
        return (af * af).astype(BF16)

    return _wgrad(a, b, pl.BlockSpec((tt, bm), lambda j, k: (k, j)), pl.BlockSpec((tt, n), lambda j, k: (k, 0)),
                  (1, bm, n), (nj, bm, n), nj, t // tt, name, prep_a=squared if square else None)


def _wgrad_o(oa, ob, dmix, nj, tt):
    t, n = dmix.shape
    ca, cb = oa.shape[1], ob.shape[1]
    m = ca + cb
    nk = t // tt

    def body(oa_ref, ob_ref, b_ref, o_ref, acc_sc):
        k = pl.program_id(0)
        part = _dot_tn(jnp.concatenate([oa_ref[...], ob_ref[...]], axis=1), b_ref[...])

        @pl.when(k == 0)
        def _():
            acc_sc[...] = part

        @pl.when(k > 0)
        def _():
            acc_sc[...] += part

        @pl.when(k == nk - 1)
        def _():
            o_ref[...] = acc_sc[...].reshape(nj, m // nj, n).astype(BF16)

    return pl.pallas_call(
        body,
        grid=(nk,),
        in_specs=[pl.BlockSpec((tt, ca), lambda k: (k, 0)), pl.BlockSpec((tt, cb), lambda k: (k, 0)),
                  pl.BlockSpec((tt, n), lambda k: (k, 0))],
        out_specs=pl.BlockSpec((nj, m // nj, n), lambda k: (0, 0, 0)),
        out_shape=SDS((nj, m // nj, n), BF16),
        scratch_shapes=[pltpu.VMEM((m, n), F32)],
        compiler_params=_cp("arbitrary"),
        name="wgrad_o",
    )(oa, ob, dmix)


def _attn_out_bwd(dmix, wo, ca, tm):
    t, d = dmix.shape
    cb = wo.shape[0] - ca

    def body(dm_ref, w_ref, da_ref, db_ref):
        dm = dm_ref[...]
        da_ref[...] = _dot_nt(dm, w_ref[0:ca, :]).astype(BF16)
        db_ref[...] = _dot_nt(dm, w_ref[ca:, :]).astype(BF16)

    return pl.pallas_call(
        body,
        grid=(t // tm,),
        in_specs=[pl.BlockSpec((tm, d), lambda i: (i, 0)), pl.BlockSpec(wo.shape, lambda i: (0, 0))],
        out_specs=[pl.BlockSpec((tm, ca), lambda i: (i, 0)), pl.BlockSpec((tm, cb), lambda i: (i, 0))],
        out_shape=[SDS((t, ca), BF16), SDS((t, cb), BF16)],
        compiler_params=_cp("parallel"),
        name="attn_out_bwd",
    )(dmix, wo)


def _heads_t(x):
    xt = x.astype(F32).T
    return jnp.concatenate([xt[h * HEAD_DIM:(h + 1) * HEAD_DIM, :] for h in range(GROUP)], axis=1)


def _heads_t_inv(yt):
    n = yt.shape[1] // GROUP
    return jnp.concatenate([yt[:, h * n:(h + 1) * n] for h in range(GROUP)], axis=0).T


def _attn_a_bwd(qa, ka, kat, va, do, o, lse, tq, tk, grads):
    bl, kv, s_len, _ = ka.shape
    nq, nk = s_len // tq, s_len // tk
    assert nk % 2 == 0
    r = GROUP * tq
    ng = len(grads)

    def body(q_ref, qn_ref, k_ref, kt_ref, v_ref, do_ref, don_ref, o_ref, l_ref, *rest):
        grad_refs, (dq_ref, dk_ref, dv_ref), parts = rest[:ng], rest[ng:ng + 3], rest[ng + 3:2 * ng + 3]
        st_sc, dp_sc, dkt_sc, dvt_sc, send_sems, recv_sems, local_sems = rest[2 * ng + 3:]
        i = pl.program_id(2)
        step_id = (pl.program_id(0) * kv + pl.program_id(1)) * nq + i
        start, wait = _direct_exchange("scatter", grad_refs, parts, send_sems, recv_sems, local_sems)
        pl.when(step_id == 0)(start)

        dot32 = _heads_t(do_ref[...])
        drow = jnp.sum(dot32 * _heads_t(o_ref[...]), axis=0, keepdims=True)
        qt, dot = _heads_t(q_ref[...]).astype(BF16), dot32.astype(BF16)
        lrow = l_ref[0, 0, 0, 0:1, :]

        @pl.when(i == 0)
        def _():
            dkt_sc[...] = jnp.zeros_like(dkt_sc)
            dvt_sc[...] = jnp.zeros_like(dvt_sc)

        def chunk(c):
            return pl.ds(pl.multiple_of(c * tk, tk), tk)

        def scores(c, slot, qv=qt, dov=dot):
            st_sc[slot] = _dot(k_ref[0, 0, chunk(c), :], qv)
            dp_sc[slot] = _dot(v_ref[0, 0, chunk(c), :], dov)

        def fold(slot, c, dqt):
            pt = jnp.exp(st_sc[slot] - lrow)
            dsb = (pt * (dp_sc[slot] - drow)).astype(BF16)
            dvt_sc[:, chunk(c)] += _dot_nt(dot, pt.astype(BF16))
            dkt_sc[:, chunk(c)] += _dot_nt(qt, dsb)
            return dqt + _dot(kt_ref[0, 0, :, chunk(c)], dsb)

        @pl.when(i == 0)
        def _():
            scores(0, 0)

        def step(c2, dqt):
            c = 2 * c2
            scores(c + 1, 1)
            dqt = fold(0, c, dqt)
            scores(c + 2, 0)
            return fold(1, c + 1, dqt)

        dqt = jnp.zeros((HEAD_DIM, r), F32)
        for c2 in range(nk // 2 - 1):
            dqt = step(c2, dqt)
        scores(nk - 1, 1)
        dqt = fold(0, nk - 2, dqt)
        scores(0, 0, _heads_t(qn_ref[...]).astype(BF16), _heads_t(don_ref[...]).astype(BF16))
        dq_ref[...] = _heads_t_inv(fold(1, nk - 1, dqt))

        @pl.when(i == nq - 1)
        def _():
            dk_ref[0, 0] = dkt_sc[...].T
            dv_ref[0, 0] = dvt_sc[...].T

        pl.when(step_id == bl * kv * nq - 1)(wait)

    kvspec = pl.BlockSpec((1, 1, s_len, HEAD_DIM), lambda b, g, i: (b, g, 0, 0))
    tok = pl.BlockSpec((tq, GROUP * HEAD_DIM), lambda b, g, i: (b * nq + i, g))
    toknext = pl.BlockSpec((tq, GROUP * HEAD_DIM), lambda b, g, i: (b * nq + jnp.minimum(i + 1, nq - 1), g))
    anyspec = pl.BlockSpec(memory_space=pl.ANY)
    res = pl.pallas_call(
        body,
        grid=(bl, kv, nq),
        in_specs=[tok, toknext, kvspec, pl.BlockSpec((1, 1, HEAD_DIM, s_len), lambda b, g, i: (b, g, 0, 0)), kvspec,
                  tok, toknext, tok, pl.BlockSpec((1, 1, 1, 8, r), lambda b, g, i: (b, g, i, 0, 0))] + [anyspec] * ng,
        out_specs=[tok, kvspec, kvspec] + [anyspec] * ng,
        out_shape=[SDS(qa.shape, F32), SDS(ka.shape, F32), SDS(va.shape, F32)]
        + [SDS(g.shape, g.dtype) for g in grads],
        scratch_shapes=[pltpu.VMEM((2, tk, r), F32), pltpu.VMEM((2, tk, r), F32),
                        pltpu.VMEM((HEAD_DIM, s_len), F32), pltpu.VMEM((HEAD_DIM, s_len), F32)]
        + _exchange_scratch(ng),
        compiler_params=_cp("arbitrary", "arbitrary", "arbitrary"),
        name="attn_a_bwd",
    )(qa, qa, ka, kat, va, do, do, o, lse, *grads)
    return res[0], res[1], res[2], res[3:]


def _attn_b_bwd(qb, kb, kbt, vb, do, o, lse, bias_t, sink, s_len):
    bl, kv, sp, _ = kb.shape
    nb = s_len // BLOCK
    nbs = min(QB_PER_STEP, nb)
    r = GROUP * BLOCK

    def body(q_ref, k_ref, kt_ref, v_ref, do_ref, o_ref, l_ref, bt_ref, sink_ref,
             dq_ref, dk_ref, dv_ref, dsum_ref, dsink_ref, dkt_sc, dvt_sc):
        g, b, ns = pl.program_id(0), pl.program_id(1), pl.program_id(2)
        sink_row = _sink_row(sink_ref, g)

        @pl.when(ns == 0)
        def _():
            dkt_sc[...] = jnp.zeros_like(dkt_sc)
            dvt_sc[...] = jnp.zeros_like(dvt_sc)

        @pl.when((b == 0) & (ns == 0))
        def _():
            dsum_ref[...] = jnp.zeros_like(dsum_ref)
            dsink_ref[...] = jnp.zeros_like(dsink_ref)

        dsum = jnp.zeros((SPAN, r), F32)
        dsink = jnp.zeros((1, r), F32)
        for j in range(nbs):
            n = ns * nbs + j
            span = pl.ds(pl.multiple_of(n * BLOCK, BLOCK), SPAN)
            rows = slice(j * BLOCK, (j + 1) * BLOCK)
            dot32 = _heads_t(do_ref[rows, :])
            drow = jnp.sum(dot32 * _heads_t(o_ref[rows, :]), axis=0, keepdims=True)
            qt, dot = _heads_t(q_ref[rows, :]).astype(BF16), dot32.astype(BF16)
            lrow = l_ref[0, 0, j, 0:1, :]
            st = _dot(k_ref[0, 0, span, :], qt) + bt_ref[_bias_variant(n, nb), 0]
            pt = jnp.exp(st - lrow)
            dst = pt * (_dot(v_ref[0, 0, span, :], dot) - drow)
            dsum = dsum + dst
            dsink = dsink - jnp.exp(sink_row - lrow) * drow
            dsb = dst.astype(BF16)
            dvt_sc[:, span] += _dot_nt(dot, pt.astype(BF16))
            dkt_sc[:, span] += _dot_nt(qt, dsb)
            dq_ref[rows, :] = _heads_t_inv(_dot(kt_ref[0, 0, :, span], dsb))
        dsum_ref[0] += dsum
        dsink_ref[0, 0:1, :] += dsink

        @pl.when(ns == nb // nbs - 1)
        def _():
            dk_ref[0, 0] = dkt_sc[:, BLOCK:BLOCK + s_len].T
            dv_ref[0, 0] = dvt_sc[:, BLOCK:BLOCK + s_len].T

    kvspec = pl.BlockSpec((1, 1, sp, HEAD_DIM), lambda g, b, n: (b, g, 0, 0))
    kvout = pl.BlockSpec((1, 1, s_len, HEAD_DIM), lambda g, b, n: (b, g, 0, 0))
    tok = pl.BlockSpec((nbs * BLOCK, GROUP * HEAD_DIM), lambda g, b, n: (b * (nb // nbs) + n, g))
    return pl.pallas_call(
        body,
        grid=(kv, bl, nb // nbs),
        in_specs=[tok, kvspec, pl.BlockSpec((1, 1, HEAD_DIM, sp), lambda g, b, n: (b, g, 0, 0)), kvspec, tok, tok,
                  pl.BlockSpec((1, 1, nbs, 8, r), lambda g, b, n: (b, g, n, 0, 0)),
                  pl.BlockSpec((3, 1, SPAN, r), lambda g, b, n: (0, g, 0, 0)),
                  pl.BlockSpec(memory_space=pltpu.SMEM)],
        out_specs=[tok, kvout, kvout,
                   pl.BlockSpec((1, SPAN, r), lambda g, b, n: (g, 0, 0)),
                   pl.BlockSpec((1, 8, r), lambda g, b, n: (g, 0, 0))],
        out_shape=[SDS(qb.shape, F32), SDS((bl, kv, s_len, HEAD_DIM), F32), SDS((bl, kv, s_len, HEAD_DIM), F32),
                   SDS((kv, SPAN, r), F32), SDS((kv, 8, r), F32)],
        scratch_shapes=[pltpu.VMEM((HEAD_DIM, sp), F32), pltpu.VMEM((HEAD_DIM, sp), F32)],
        compiler_params=_cp("arbitrary", "arbitrary", "arbitrary"),
        name="attn_b_bwd",
    )(qb, kb, kbt, vb, do, o, lse, bias_t, sink)


def _bias_reduce(dsum, dsink, bucket_t4):
    kv, _, r = dsum.shape

    def body(ds_ref, dk_ref, bk_ref, rel_ref, sink_ref):
        lane = lax.broadcasted_iota(jnp.int32, (N_BUCKETS, 128), 1)
        lane8 = lax.broadcasted_iota(jnp.int32, (8, 128), 1)
        bk = bk_ref[...]
        for g in range(kv):
            ds = ds_ref[g]
            rowi = lax.broadcasted_iota(jnp.int32, (N_BUCKETS, r), 0)
            red = jnp.zeros((N_BUCKETS, r), F32)
            for b in range(N_BUCKETS):
                red = jnp.where(rowi == b, jnp.sum(jnp.where(bk == b, ds, 0.0), axis=0, keepdims=True), red)
            out = jnp.zeros((N_BUCKETS, 128), F32)
            so = jnp.zeros((8, 128), F32)
            for h in range(GROUP):
                col = jnp.sum(red[:, h * BLOCK:(h + 1) * BLOCK], axis=1, keepdims=True)
                out = jnp.where(lane == h, col, out)
                sc = jnp.sum(dk_ref[g][:, h * BLOCK:(h + 1) * BLOCK], axis=1, keepdims=True)
                so = jnp.where(lane8 == h, sc, so)
            rel_ref[g] = out
            sink_ref[g] = so

    vm = pl.BlockSpec(memory_space=pltpu.VMEM)
    return pl.pallas_call(
        body,
        in_specs=[vm, vm, vm],
        out_specs=[vm, vm],
        out_shape=[SDS((kv, N_BUCKETS, 128), F32), SDS((kv, 8, 128), F32)],
        name="bias_reduce",
    )(dsum, dsink, bucket_t4)


def _dqkprep(dqa, dka, dva, dqb, dkb, dvb, proj, h1, cos, sin_signed, gq, gk, s_len, ts):
    t, p_cols = proj.shape
    d = h1.shape[1]
    bl, kva, kvb = dka.shape[0], dka.shape[1], dkb.shape[1]
    ha, hb = dqa.shape[1] // HEAD_DIM, dqb.shape[1] // HEAD_DIM
    ns = s_len // ts

    def body(dqa_ref, dka_ref, dva_ref, dqb_ref, dkb_ref, dvb_ref, p_ref, h1_ref, h1p_ref, cos_ref, sin_ref,
             gq_ref, gk_ref, dp_ref, dgq_ref, dgk_ref, gw_ref, gw_sc, dpp_sc):
        b, i = pl.program_id(0), pl.program_id(1)
        cs, sn = cos_ref[...], sin_ref[...]
        low, first = _pair_masks(ts)

        @pl.when((b == 0) & (i == 0))
        def _():
            dgq_ref[...] = jnp.zeros_like(dgq_ref)
            dgk_ref[...] = jnp.zeros_like(dgk_ref)
            gw_sc[...] = jnp.zeros_like(gw_sc)
            dpp_sc[...] = jnp.zeros_like(dpp_sc)

        gw_sc[...] += _dot_tn(dpp_sc[...], h1p_ref[...])

        def grad_pair(ref, p):
            return jnp.concatenate([ref[0, 2 * p], ref[0, 2 * p + 1]], axis=1)

        def put(p, val):
            dp_ref[:, p * PAIR:(p + 1) * PAIR] = val.astype(BF16)

        def unrope_norm(d_rot, p, g, dg_ref):
            dn = d_rot * cs + _pair_partner(d_rot * sn, first)
            xp = p_ref[:, p * PAIR:(p + 1) * PAIR]
            r = lax.rsqrt(_pair_mean(xp * xp, low) + EPS)
            n = xp * r
            gd = g * dn
            dg_ref[0:1, :] += jnp.sum(dn * n, axis=0, keepdims=True)
            put(p, r * (gd - n * _pair_mean(n * gd, low)))

        for p in range(ha // 2):
            unrope_norm(dqa_ref[:, p * PAIR:(p + 1) * PAIR] * SCALE, p, gq_ref[...], dgq_ref)
        base = ha // 2
        for p in range(kva // 2):
            unrope_norm(grad_pair(dka_ref, p), base + p, gk_ref[...], dgk_ref)
            put(base + kva // 2 + p, grad_pair(dva_ref, p))
        base += kva
        for p in range(hb // 2):
            put(base + p, dqb_ref[:, p * PAIR:(p + 1) * PAIR] * SCALE)
        base += hb // 2
        for p in range(kvb // 2):
            put(base + p, grad_pair(dkb_ref, p))
            put(base + kvb // 2 + p, grad_pair(dvb_ref, p))

        dpp_sc[...] = dp_ref[...]

        @pl.when((b == bl - 1) & (i == ns - 1))
        def _():
            gw_ref[...] = (gw_sc[...] + _dot_tn(dp_ref[...], h1_ref[...])).astype(BF16)

    def hm(nh):
        return pl.BlockSpec((1, nh, ts, HEAD_DIM), lambda b, i: (b, 0, i, 0))

    def tokmajor(nh):
        return pl.BlockSpec((ts, nh * HEAD_DIM), lambda b, i: (b * ns + i, 0))

    vec = pl.BlockSpec((1, PAIR), lambda b, i: (0, 0))
    tab = pl.BlockSpec((ts, PAIR), lambda b, i: (i, 0))
    acc = pl.BlockSpec((8, PAIR), lambda b, i: (0, 0))
    pspec = pl.BlockSpec((ts, p_cols), lambda b, i: (b * ns + i, 0))
    return pl.pallas_call(
        body,
        grid=(bl, ns),
        in_specs=[tokmajor(ha), hm(kva), hm(kva), tokmajor(hb), hm(kvb), hm(kvb), pspec,
                  pl.BlockSpec((ts, d), lambda b, i: (b * ns + i, 0)),
                  pl.BlockSpec((ts, d), lambda b, i: (jnp.maximum(b * ns + i - 1, 0), 0)), tab, tab, vec, vec],
        out_specs=[pspec, acc, acc, pl.BlockSpec((p_cols, d), lambda b, i: (0, 0))],
        out_shape=[SDS((t, p_cols), BF16), SDS((8, PAIR), F32), SDS((8, PAIR), F32), SDS((p_cols, d), BF16)],
        scratch_shapes=[pltpu.VMEM((p_cols, d), F32), pltpu.VMEM((ts, p_cols), BF16)],
        compiler_params=_cp("arbitrary", "arbitrary"),
        name="dqkprep",
    )(dqa, dka, dva, dqb, dkb, dvb, proj, h1, h1, cos, sin_signed, gq, gk)


def _dx_final(dproj, w_t, x2, dx1, g1, tm, grads):
    t, d = x2.shape
    p_cols = w_t.shape[0]
    ng = len(grads)
    nsteps = t // tm

    def body(dp_ref, w_ref, x_ref, dx1_ref, g_ref, *rest):
        grad_refs, (dx_ref, dg_ref), parts = rest[:ng], rest[ng:ng + 2], rest[ng + 2:2 * ng + 2]
        start, wait = _direct_exchange("scatter", grad_refs, parts, *rest[2 * ng + 2:])

        @pl.when(pl.program_id(0) == 0)
        def _():
            start()
            dg_ref[...] = jnp.zeros_like(dg_ref)

        dh = _dot(dp_ref[...], w_ref[...])
        g = g_ref[...]
        _, n, r = _rms_fwd(x_ref[...], g)
        dx, dgt = _rms_bwd(n, r, g, dh)
        dx_ref[...] = dx1_ref[...] + dx
        dg_ref[0:1, :] += jnp.sum(dgt, axis=0, keepdims=True)
        pl.when(pl.program_id(0) == nsteps - 1)(wait)

    tile = pl.BlockSpec((tm, d), lambda i: (i, 0))
    anyspec = pl.BlockSpec(memory_space=pl.ANY)
    res = pl.pallas_call(
        body,
        grid=(nsteps,),
        in_specs=[pl.BlockSpec((tm, p_cols), lambda i: (i, 0)),
                  pl.BlockSpec((p_cols, d), lambda i: (0, 0)),
                  tile, tile, pl.BlockSpec((1, d), lambda i: (0, 0))] + [anyspec] * ng,
        out_specs=[tile, pl.BlockSpec((8, d), lambda i: (0, 0))] + [anyspec] * ng,
        out_shape=[SDS((t, d), F32), SDS((8, d), F32)] + [SDS(g.shape, g.dtype) for g in grads],
        scratch_shapes=_exchange_scratch(ng),
        compiler_params=_cp("arbitrary"),
        name="dx_final",
    )(dproj, w_t, x2, dx1, g1, *grads)
    return res[0], res[1], res[2:]


def _adamw_math(w, g, m, v):
    m = ADAM_B1 * m + (1.0 - ADAM_B1) * g
    v = ADAM_B2 * v + (1.0 - ADAM_B2) * (g * g)
    m_hat = m / (1.0 - ADAM_B1 ** ADAM_STEP)
    v_hat = v / (1.0 - ADAM_B2 ** ADAM_STEP)
    delta = -ADAM_LR * (m_hat / (jnp.sqrt(v_hat) + ADAM_EPS) + ADAM_WD * w)
    return delta, m, v


def _adamw_sum(parts, w, m, v, tr, name):
    rows, cols = w.shape

    def body(p_ref, w_ref, m_ref, v_ref, g_ref, d_ref, nm_ref, nv_ref):
        g = p_ref[0].astype(F32)
        for s in range(1, N_DEV):
            g = g + p_ref[s].astype(F32)
        g_ref[...] = g
        d_ref[...], nm_ref[...], nv_ref[...] = _adamw_math(w_ref[...], g, m_ref[...], v_ref[...])

    tr = min(tr, rows)
    tile = pl.BlockSpec((tr, cols), lambda i: (i, 0))
    return pl.pallas_call(
        body,
        grid=(rows // tr,),
        in_specs=[pl.BlockSpec((N_DEV, tr, cols), lambda i: (0, i, 0)), tile, tile, tile],
        out_specs=[tile] * 4,
        out_shape=[SDS((rows, cols), F32)] * 4,
        compiler_params=_cp("parallel"),
        name=name,
    )(parts, w, m, v)


def _adamw_small(vec, rel, ws, ms, vs):
    hb = ws[6].shape[1]
    n = len(ws)

    def body(vec_ref, rel_ref, *rest):
        w_refs, m_refs, v_refs = rest[:n], rest[n:2 * n], rest[2 * n:3 * n]
        loss_ref, outs = rest[3 * n], rest[3 * n + 1:]
        grads = [vec_ref[0:1, :], vec_ref[1:2, :], vec_ref[2:3, :], vec_ref[3:4, :],
                 vec_ref[4:5, 0:HEAD_DIM], vec_ref[4:5, SMALL_LANES:SMALL_LANES + HEAD_DIM],
                 vec_ref[4:5, 2 * SMALL_LANES:2 * SMALL_LANES + hb], rel_ref[:, 0:hb]]
        loss_ref[...] = vec_ref[4:5, 3 * SMALL_LANES:3 * SMALL_LANES + 1]
        for p, g in enumerate(grads):
            g_ref, d_ref, nm_ref, nv_ref = outs[4 * p:4 * p + 4]
            g_ref[...] = g
            d_ref[...], nm_ref[...], nv_ref[...] = _adamw_math(w_refs[p][...], g, m_refs[p][...], v_refs[p][...])

    vm = pl.BlockSpec(memory_space=pltpu.VMEM)
    res = pl.pallas_call(
        body,
        in_specs=[vm] * (2 + 3 * n),
        out_specs=[vm] * (1 + 4 * n),
        out_shape=[SDS((1, 1), F32)] + [SDS(w.shape, F32) for w in ws for _ in range(4)],
        name="adamw_small",
    )(vec, rel, *ws, *ms, *vs)
    return res[0], [res[1 + 4 * p:5 + 4 * p] for p in range(n)]


def _local_step(x, loss_target, win_s, wo_s, wup_s, wdn_s, g_pre_mix, g_post_mix, q_norm_a, k_norm_a, sink_b,
                rel_bias, g_pre_ffn, g_post_ffn):
    bl, s_len, d = x.shape
    t = bl * s_len
    nh = d // HEAD_DIM
    ha = nh // 2
    kva = ha // GROUP
    hb = nh - ha
    kvb = hb // GROUP
    tm = 512
    tp = min(1024, t)
    tw = min(4096, t)
    ts = min(512, s_len)
    tq, tk = 2 * BLOCK, min(512, s_len // 2)

    x2 = x.reshape(t, d)
    tg2 = loss_target.reshape(t, d)
    cos, sin_signed = _rope_tables(s_len)
    gq2, gk2 = jnp.tile(q_norm_a, (1, 2)), jnp.tile(k_norm_a, (1, 2))
    a = jnp.arange(BLOCK, dtype=jnp.int32)
    c = jnp.arange(SPAN, dtype=jnp.int32)
    bucket_t = _t5_bucket(c[:, None] - BLOCK - a[None, :])
    bucket_t4 = jnp.tile(bucket_t, (1, GROUP))
    (win_g,), bias_t = _weight_gather([win_s], bucket_t, rel_bias)
    w_in_t = win_g.reshape(-1, d)
    p_cols = w_in_t.shape[0]

    h1, proj, qa, ka, kat, va, vat, qb, kb, kbt, vb, vbt = _inproj_qkprep(
        x2, g_pre_mix, w_in_t, cos, sin_signed, gq2, gk2, bl, s_len, ha, kva, hb, kvb, ts)
    oa, lse_a, (wo_g, wup_g, wdn_g) = _attn_a_fwd(qa, ka, vat, tq, tk, [wo_s, wup_s, wdn_s])
    wo = wo_g.reshape(-1, d)
    wdn = wdn_g.reshape(-1, d)
    ob, lse_b = _attn_b_fwd(qb, kb, vbt, bias_t, sink_b, s_len)
    mix, x1, h2 = _mixout(oa, ob, wo, x2, g_post_mix, g_pre_ffn, tp)
    u, df, dy, dg4, loss8 = _ffn_fwd(h2, wup_g, wdn, x1, tg2, g_post_ffn, tm, FFN_BLOCKS_PER_STEP)

    dpre, dx1, dmix, dg3, dg2 = _ffn_bwd(df, u, wdn, wup_g, x1, dy, mix, g_pre_ffn, g_post_mix, FFN_BWD_TOKENS,
                                         FFN_BLOCKS_PER_STEP)
    gw_dn = _wgrad_rows(u, df, N_DEV, tw, "wgrad_down", square=True)
    gw_up = _wgrad_cols(h2, dpre, N_DEV, tw, "wgrad_up")
    gw_o = _wgrad_o(oa, ob, dmix, N_DEV, min(2048, t))
    doa, dob = _attn_out_bwd(dmix, wo, oa.shape[1], tp)
    dqa, dka, dva, (p_o, p_up, p_dn) = _attn_a_bwd(qa, ka, kat, va, doa, oa, lse_a, tq, tk, [gw_o, gw_up, gw_dn])
    dqb, dkb, dvb, dsum, dsink = _attn_b_bwd(qb, kb, kbt, vb, dob, ob, lse_b, bias_t, sink_b, s_len)
    drel_g, dsink_g = _bias_reduce(dsum, dsink, bucket_t4)
    dproj, dgq, dgk, gw_in_t = _dqkprep(dqa, dka, dva, dqb, dkb, dvb, proj, h1, cos, sin_signed, gq2, gk2, s_len, ts)
    gw_in_t = gw_in_t.reshape(N_DEV, -1, d)
    grad_x, dg1, (p_in,) = _dx_final(dproj, w_in_t, x2, dx1, g_pre_mix, tp, [gw_in_t])

    vec, rel = _small_allreduce([dg1, dg2, dg3, dg4], dgq, dgk, dsink_g, drel_g, loss8)
    return grad_x.reshape(bl, s_len, d), p_in, p_o, p_up, p_dn, vec, rel


def kernel(x, w_in, w_o, g_pre_mix, g_post_mix, q_norm_a, k_norm_a, sink_b, rel_bias, g_pre_ffn, w_ffn_up, w_ffn_down, g_post_ffn, loss_target, m_w_in, m_w_o, m_g_pre_mix, m_g_post_mix, m_q_norm_a, m_k_norm_a, m_sink_b, m_rel_bias, m_g_pre_ffn, m_w_ffn_up, m_w_ffn_down, m_g_post_ffn, v_w_in, v_w_o, v_g_pre_mix, v_g_post_mix, v_q_norm_a, v_k_norm_a, v_sink_b, v_rel_bias, v_g_pre_ffn, v_w_ffn_up, v_w_ffn_down, v_g_post_ffn):
    w_in_t = w_in[0].T

    grad_x, p_in, p_o, p_up, p_dn, vec, rel = _local_step(
        x, loss_target, w_in_t.astype(BF16), w_o[0].astype(BF16), w_ffn_up[0].astype(BF16), w_ffn_down[0].astype(BF16),
        g_pre_mix, g_post_mix, q_norm_a, k_norm_a, sink_b, rel_bias, g_pre_ffn, g_post_ffn)

    big = {
        "w_in": [a.T for a in _adamw_sum(p_in, w_in_t, m_w_in[0].T, v_w_in[0].T, 192, "adamw_in")],
        "w_o": _adamw_sum(p_o, w_o[0], m_w_o[0], v_w_o[0], 128, "adamw_o"),
        "w_up": _adamw_sum(p_up, w_ffn_up[0], m_w_ffn_up[0], v_w_ffn_up[0], 256, "adamw_up"),
        "w_dn": _adamw_sum(p_dn, w_ffn_down[0], m_w_ffn_down[0], v_w_ffn_down[0], 256, "adamw_down"),
    }
    loss, small = _adamw_small(
        vec, rel,
        [g_pre_mix, g_post_mix, g_pre_ffn, g_post_ffn, q_norm_a, k_norm_a, sink_b, rel_bias],
        [m_g_pre_mix, m_g_post_mix, m_g_pre_ffn, m_g_post_ffn, m_q_norm_a, m_k_norm_a, m_sink_b, m_rel_bias],
        [v_g_pre_mix, v_g_post_mix, v_g_pre_ffn, v_g_post_ffn, v_q_norm_a, v_k_norm_a, v_sink_b, v_rel_bias])
    s_pre_mix, s_post_mix, s_pre_ffn, s_post_ffn, s_qn, s_kn, s_sink, s_rel = small

    def outs(kind):
        return [big["w_in"][kind][None], big["w_o"][kind][None], s_pre_mix[kind], s_post_mix[kind], s_qn[kind],
                s_kn[kind], s_sink[kind], s_rel[kind], s_pre_ffn[kind], big["w_up"][kind][None],
                big["w_dn"][kind][None], s_post_ffn[kind]]

    return (loss.reshape(()), grad_x, *outs(0), *outs(1), *outs(2), *outs(3))
```

```python
import functools

import jax
import jax.numpy as jnp
import numpy as np
from jax import lax
from jax.experimental import pallas as pl
from jax.experimental.pallas import tpu as pltpu

F32 = jnp.float32
BF16 = jnp.bfloat16
SDS = jax.ShapeDtypeStruct

N_DEV = 8
HEAD_DIM = 64
GROUP = 4
BLOCK = 128
SPAN = 3 * BLOCK
GRID_W = 64
N_BUCKETS = 32
MAX_DISTANCE = 128
ROPE_THETA = 10000.0
EPS = 1e-6
NEG_INF = -1e30
SCALE = HEAD_DIM ** -0.5
VT_PAD = 16

ADAM_LR = 0.001
ADAM_B1 = 0.9
ADAM_B2 = 0.999
ADAM_EPS = 1e-08
ADAM_WD = 0.01
ADAM_STEP = 10

VMEM_LIMIT = 56 * 1024 * 1024
MESH = pl.DeviceIdType.MESH


def _cp(*sem):
    return pltpu.CompilerParams(dimension_semantics=sem, vmem_limit_bytes=VMEM_LIMIT)


def _dot(a, b):
    return jnp.dot(a, b, preferred_element_type=F32)


def _dot_nt(a, b):
    return lax.dot_general(a, b, (((1,), (1,)), ((), ())), preferred_element_type=F32)


def _dot_tn(a, b):
    return lax.dot_general(a, b, (((0,), (0,)), ((), ())), preferred_element_type=F32)


def _rms_fwd(x, g):
    r = lax.rsqrt(jnp.mean(x * x, axis=-1, keepdims=True) + EPS)
    n = x * r
    return n * g, n, r


def _rms_bwd(n, r, g, dy):
    gd = g * dy
    dx = r * (gd - n * jnp.mean(n * gd, axis=-1, keepdims=True))
    return dx, dy * n


def _rope_tables(s_len):
    rows = s_len // GRID_W
    row = np.repeat(np.arange(rows, dtype=np.int32), GRID_W)
    col = np.tile(np.arange(GRID_W, dtype=np.int32), rows)
    nf = HEAD_DIM // 4
    freqs = np.float32(ROPE_THETA) ** (-np.arange(nf, dtype=np.float32) / np.float32(nf))
    ang_r = row.astype(np.float32)[:, None] * freqs[None, :]
    ang_c = col.astype(np.float32)[:, None] * freqs[None, :]
    cr, sr, cc, sc = np.cos(ang_r), np.sin(ang_r), np.cos(ang_c), np.sin(ang_c)
    cos = np.concatenate([cr, cr, cc, cc] * 2, axis=-1).astype(np.float32)
    sin_signed = np.concatenate([-sr, sr, -sc, sc] * 2, axis=-1).astype(np.float32)
    return jnp.asarray(cos), jnp.asarray(sin_signed)


def _t5_bucket(rel):
    nb = N_BUCKETS // 2
    ret = (rel > 0).astype(jnp.int32) * nb
    n = jnp.abs(rel)
    max_exact = nb // 2
    nf = jnp.maximum(n, 1).astype(F32)
    large = max_exact + (jnp.log(nf / max_exact) / np.float32(np.log(MAX_DISTANCE / max_exact))
                         * (nb - max_exact)).astype(jnp.int32)
    large = jnp.minimum(large, nb - 1)
    return ret + jnp.where(n < max_exact, n, large)


def _mesh_pos():
    return lax.axis_index("x"), lax.axis_index("y"), lax.axis_index("c")


def _lin(p):
    return 4 * p[0] + 2 * p[1] + p[2]


def _bias_tables(bkt_ref, tbl_ref, out_ref, hb):
    bkt = bkt_ref[...]
    ci = lax.broadcasted_iota(jnp.int32, (SPAN, BLOCK), 0)
    qi = lax.broadcasted_iota(jnp.int32, (SPAN, BLOCK), 1)
    band = jnp.abs(ci - BLOCK - qi) <= BLOCK
    masks = (band, band & (ci >= BLOCK), band & (ci < 2 * BLOCK))
    for h in range(hb):
        acct = jnp.zeros((SPAN, BLOCK), F32)
        for b in range(N_BUCKETS):
            acct = jnp.where(bkt == b, tbl_ref[b, h], acct)
        lanes = slice((h % GROUP) * BLOCK, (h % GROUP + 1) * BLOCK)
        for var, mask in enumerate(masks):
            out_ref[var, h // GROUP, :, lanes] = jnp.where(mask, acct, NEG_INF)


def _weight_gather(shards, bucket_t, rel_bias):
    n = len(shards)
    hb = rel_bias.shape[1]

    def body(*refs):
        xs, (bkt_ref, tbl_ref), outs, bias_ref = refs[:n], refs[n:n + 2], refs[n + 2:2 * n + 2], refs[2 * n + 2]
        send_sems, recv_sems, local_sems = refs[2 * n + 3:]
        x, y, c = _mesh_pos()
        me, sibling = (x, y, c), (x, y, 1 - c)
        chips = [(1 - x, y), (x, 1 - y), (1 - x, 1 - y)]

        def copy(a, k, block, to, src=None):
            slot = outs[a].at[_lin(block)]
            return pltpu.make_async_remote_copy(
                src_ref=slot if src is None else src, dst_ref=slot,
                send_sem=send_sems.at[a, k], recv_sem=recv_sems.at[a, k],
                device_id=to, device_id_type=MESH)

        started = []
        for a in range(n):
            mine = pltpu.make_async_copy(xs[a], outs[a].at[_lin(me)], local_sems.at[a])
            mine.start()
            started.append(mine)
        sends = []
        for a in range(n):
            first = [copy(a, 0, me, sibling, src=xs[a])]
            first += [copy(a, 1 + j, me, (*chip, c), src=xs[a]) for j, chip in enumerate(chips)]
            for cp in first:
                cp.start()
            sends += first
        _bias_tables(bkt_ref, tbl_ref, bias_ref, hb)
        for a in range(n):
            for j, chip in enumerate(chips):
                copy(a, 1 + j, (*chip, c), me).wait_recv()
                fwd = copy(a, 4 + j, (*chip, c), sibling)
                fwd.start()
                sends.append(fwd)
        for a in range(n):
            copy(a, 0, sibling, me).wait_recv()
            for j, chip in enumerate(chips):
                copy(a, 4 + j, (*chip, 1 - c), me).wait_recv()
        for cp in sends:
            cp.wait_send()
        for mine in started:
            mine.wait()

    anyspec = pl.BlockSpec(memory_space=pl.ANY)
    vm = pl.BlockSpec(memory_space=pltpu.VMEM)
    res = pl.pallas_call(
        body,
        out_shape=[SDS((N_DEV,) + s.shape, s.dtype) for s in shards]
        + [SDS((3, hb // GROUP, SPAN, GROUP * BLOCK), F32)],
        in_specs=[anyspec] * n + [vm, pl.BlockSpec(memory_space=pltpu.SMEM)],
        out_specs=[anyspec] * n + [vm],
        scratch_shapes=[pltpu.SemaphoreType.DMA((n, 7)), pltpu.SemaphoreType.DMA((n, 7)),
                        pltpu.SemaphoreType.DMA((n,))],
        name="weight_gather",
    )(*shards, bucket_t, rel_bias)
    return res[:n], res[n]


def _direct_exchange(kind, ins, outs, send_sems, recv_sems, local_sems):
    x, y, c = _mesh_pos()
    me = (x, y, c)
    peers = [(x, y, 1 - c), (1 - x, y, c), (x, 1 - y, c), (1 - x, 1 - y, c),
             (1 - x, y, 1 - c), (x, 1 - y, 1 - c), (1 - x, 1 - y, 1 - c)]

    def src(a, to):
        return ins[a] if kind == "gather" else ins[a].at[_lin(to)]

    def remote(a, k, to, frm):
        return pltpu.make_async_remote_copy(
            src_ref=src(a, to), dst_ref=outs[a].at[_lin(frm)],
            send_sem=send_sems.at[a, k], recv_sem=recv_sems.at[a, k],
            device_id=to, device_id_type=MESH)

    n = len(ins)
    sends = [remote(a, k, p, me) for a in range(n) for k, p in enumerate(peers)]
    arrivals = [remote(a, k, p, p) for a in range(n) for k, p in enumerate(peers)]
    local = [pltpu.make_async_copy(src(a, me), outs[a].at[_lin(me)], local_sems.at[a]) for a in range(n)]

    def start():
        for cp in local + sends:
            cp.start()

    def wait():
        for cp in arrivals:
            cp.wait_recv()
        for cp in sends:
            cp.wait_send()
        for cp in local:
            cp.wait()

    return start, wait


def _exchange_scratch(n):
    return [pltpu.SemaphoreType.DMA((n, 7)), pltpu.SemaphoreType.DMA((n, 7)), pltpu.SemaphoreType.DMA((n,))]


SMALL_LANES = 128


def _small_allreduce(dg_rows, dgq, dgk, dsink_g, drel_g, loss8):
    d = dg_rows[0].shape[1]
    kv = dsink_g.shape[0]

    def body(g1_ref, g2_ref, g3_ref, g4_ref, gq_ref, gk_ref, sk_ref, rl_ref, ls_ref, vec_ref, rel_ref,
             vbuf, rbuf, vland, rland, send_sems, recv_sems):
        x, y, c = _mesh_pos()
        me = (x, y, c)
        peers = [(x, y, 1 - c), (1 - x, y, c), (x, 1 - y, c), (1 - x, 1 - y, c),
                 (1 - x, y, 1 - c), (x, 1 - y, 1 - c), (1 - x, 1 - y, 1 - c)]
        vbuf[...] = jnp.zeros_like(vbuf)
        rbuf[...] = jnp.zeros_like(rbuf)
        for row, ref in enumerate((g1_ref, g2_ref, g3_ref, g4_ref)):
            vbuf[row:row + 1, :] = ref[0:1, :]
        vbuf[4:5, 0:HEAD_DIM] = gq_ref[0:1, 0:HEAD_DIM] + gq_ref[0:1, HEAD_DIM:PAIR]
        vbuf[4:5, SMALL_LANES:SMALL_LANES + HEAD_DIM] = gk_ref[0:1, 0:HEAD_DIM] + gk_ref[0:1, HEAD_DIM:PAIR]
        for g in range(kv):
            vbuf[4:5, 2 * SMALL_LANES + g * GROUP:2 * SMALL_LANES + (g + 1) * GROUP] = sk_ref[g, 0:1, 0:GROUP]
            rbuf[:, g * GROUP:(g + 1) * GROUP] = rl_ref[g, :, 0:GROUP]
        vbuf[4:5, 3 * SMALL_LANES:3 * SMALL_LANES + 1] = ls_ref[0:1, 0:1]

        def copies(k, to, frm):
            return [pltpu.make_async_remote_copy(
                src_ref=buf, dst_ref=land.at[_lin(frm)], send_sem=send_sems.at[a, k], recv_sem=recv_sems.at[a, k],
                device_id=to, device_id_type=MESH) for a, (buf, land) in enumerate(((vbuf, vland), (rbuf, rland)))]

        sends = [cp for k, p in enumerate(peers) for cp in copies(k, p, me)]
        for cp in sends:
            cp.start()
        vland[_lin(me)] = vbuf[...]
        rland[_lin(me)] = rbuf[...]
        for k, p in enumerate(peers):
            for cp in copies(k, p, p):
                cp.wait_recv()
        for cp in sends:
            cp.wait_send()
        vacc, racc = vland[0], rland[0]
        for s in range(1, N_DEV):
            vacc, racc = vacc + vland[s], racc + rland[s]
        vec_ref[...] = vacc
        rel_ref[...] = racc

    vm = pl.BlockSpec(memory_space=pltpu.VMEM)
    return pl.pallas_call(
        body,
        out_shape=[SDS((8, d), F32), SDS((N_BUCKETS, 128), F32)],
        in_specs=[vm] * 9,
        out_specs=[vm, vm],
        scratch_shapes=[pltpu.VMEM((8, d), F32), pltpu.VMEM((N_BUCKETS, 128), F32),
                        pltpu.VMEM((N_DEV, 8, d), F32), pltpu.VMEM((N_DEV, N_BUCKETS, 128), F32),
                        pltpu.SemaphoreType.DMA((2, 7)), pltpu.SemaphoreType.DMA((2, 7))],
        name="small_allreduce",
    )(*dg_rows, dgq, dgk, dsink_g, drel_g, loss8)


PAIR = 2 * HEAD_DIM


def _pair_masks(ts):
    lane = lax.broadcasted_iota(jnp.int32, (ts, PAIR), 1)
    return lane < HEAD_DIM, (lane % 32) < 16


def _pair_mean(v, low):
    del low
    r = lax.broadcasted_iota(jnp.int32, (PAIR, PAIR), 0) // HEAD_DIM
    c = lax.broadcasted_iota(jnp.int32, (PAIR, PAIR), 1) // HEAD_DIM
    same_head = (r == c).astype(BF16)
    hi = v.astype(BF16)
    lo = (v - hi.astype(F32)).astype(BF16)
    return (_dot(hi, same_head) + _dot(lo, same_head)) * (1.0 / HEAD_DIM)


def _pair_partner(v, first):
    return jnp.where(first, pltpu.roll(v, PAIR - 16, 1), pltpu.roll(v, 16, 1))


def _inproj_qkprep(x2, g1, w_t, cos, sin_signed, gq, gk, bl, s_len, ha, kva, hb, kvb, ts):
    t, d = x2.shape
    p_cols = w_t.shape[0]
    assert ha % 2 == 0 and kva % 2 == 0 and hb % 2 == 0 and kvb % 2 == 0
    ns = s_len // ts
    nt = bl * ns
    sp = s_len + 2 * BLOCK

    def body(*refs):
        kb_ref, kbt_ref, vb_ref, vbt_ref, p_even, p_odd = refs[-6:]
        s = pl.program_id(0)
        i = lax.rem(jnp.maximum(s - 1, 0), ns)

        @pl.when(s == 0)
        def _():
            p_odd[...] = jnp.zeros_like(p_odd)

        @pl.when(i == 0)
        def _():
            zeros = jnp.zeros((kvb, BLOCK, HEAD_DIM), BF16)
            zeros_t = jnp.zeros((kvb, HEAD_DIM + VT_PAD, BLOCK), BF16)
            for ref in (kb_ref, vb_ref):
                ref[0, :, 0:BLOCK, :] = zeros
                ref[0, :, sp - BLOCK:sp, :] = zeros
            kbt_ref[0, :, :, 0:BLOCK] = zeros_t[:, 0:HEAD_DIM]
            kbt_ref[0, :, :, sp - BLOCK:sp] = zeros_t[:, 0:HEAD_DIM]
            vbt_ref[0, :, :, 0:BLOCK] = zeros_t
            vbt_ref[0, :, :, sp - BLOCK:sp] = zeros_t

        even = lax.rem(s, 2) == 0
        pl.when(even)(functools.partial(tile_work, p_even, p_odd, i, *refs[:-2]))
        pl.when(jnp.logical_not(even))(functools.partial(tile_work, p_odd, p_even, i, *refs[:-2]))

    def tile_work(p_new, p_ref, i, x_ref, g1_ref, w_ref, cos_ref, sin_ref, gq_ref, gk_ref, h_ref, po_ref, qa_ref,
                  ka_ref, kat_ref, va_ref, vat_ref, qb_ref, kb_ref, kbt_ref, vb_ref, vbt_ref):
        y, _, _ = _rms_fwd(x_ref[...], g1_ref[...])
        h = y.astype(BF16)
        h_ref[...] = h
        n_parts = 6
        pw = p_cols // n_parts

        def project(c):
            p_new[:, c * pw:(c + 1) * pw] = _dot_nt(h, w_ref[c * pw:(c + 1) * pw, :])

        cs, sn = cos_ref[...], sin_ref[...]
        low, first = _pair_masks(ts)
        ones_row = (lax.broadcasted_iota(jnp.int32, (VT_PAD, ts), 0) == 0).astype(BF16)
        heads = (slice(0, HEAD_DIM), slice(HEAD_DIM, PAIR))

        def pair(p):
            v = p_ref[:, p * PAIR:(p + 1) * PAIR]
            po_ref[:, p * PAIR:(p + 1) * PAIR] = v
            return v

        def normrope(x, g):
            y = x * lax.rsqrt(_pair_mean(x * x, low) + EPS) * g
            return y * cs + _pair_partner(y, first) * sn

        eye = (lax.broadcasted_iota(jnp.int32, (PAIR, PAIR), 0)
               == lax.broadcasted_iota(jnp.int32, (PAIR, PAIR), 1)).astype(BF16)

        def transposed(xb):
            return _dot_nt(eye, xb).astype(BF16)

        def prep_qa(p):
            qa_ref[:, p * PAIR:(p + 1) * PAIR] = (normrope(pair(p), gq_ref[...]) * SCALE).astype(BF16)

        def prep_kva(p):
            base = ha // 2
            k = normrope(pair(base + p), gk_ref[...]).astype(BF16)
            v = pair(base + kva // 2 + p).astype(BF16)
            kt, vt = transposed(k), transposed(v)
            for e, lanes in enumerate(heads):
                ka_ref[0, 2 * p + e] = k[:, lanes]
                va_ref[0, 2 * p + e] = v[:, lanes]
                kat_ref[0, 2 * p + e] = kt[lanes, :]
                vat_ref[0, 2 * p + e, 0:HEAD_DIM, :] = vt[lanes, :]
                vat_ref[0, 2 * p + e, HEAD_DIM:HEAD_DIM + VT_PAD, :] = ones_row

        def prep_qb(p):
            base = ha // 2 + kva
            qb_ref[:, p * PAIR:(p + 1) * PAIR] = (pair(base + p) * SCALE).astype(BF16)

        rows = pl.ds(pl.multiple_of(BLOCK + i * ts, BLOCK), ts)

        def prep_kvb(p):
            base = ha // 2 + kva + hb // 2
            k = pair(base + p).astype(BF16)
            v = pair(base + kvb // 2 + p).astype(BF16)
            kt, vt = transposed(k), transposed(v)
            for e, lanes in enumerate(heads):
                kb_ref[0, 2 * p + e, rows, :] = k[:, lanes]
                vb_ref[0, 2 * p + e, rows, :] = v[:, lanes]
                kbt_ref[0, 2 * p + e, :, rows] = kt[lanes, :]
                vbt_ref[0, 2 * p + e, 0:HEAD_DIM, rows] = vt[lanes, :]
                vbt_ref[0, 2 * p + e, HEAD_DIM:HEAD_DIM + VT_PAD, rows] = ones_row

        work = ([functools.partial(prep_qa, p) for p in range(ha // 2)]
                + [functools.partial(prep_kva, p) for p in range(kva // 2)]
                + [functools.partial(prep_qb, p) for p in range(hb // 2)]
                + [functools.partial(prep_kvb, p) for p in range(kvb // 2)])
        per_part = -(-len(work) // n_parts)
        for c in range(n_parts):
            for item in work[c * per_part:(c + 1) * per_part]:
                item()
            project(c)

    def cur(s):
        return jnp.minimum(s, nt - 1)

    def prev(s):
        return jnp.maximum(s - 1, 0) // ns, lax.rem(jnp.maximum(s - 1, 0), ns)

    def hm(nh):
        return pl.BlockSpec((1, nh, ts, HEAD_DIM), lambda s: (prev(s)[0], 0, prev(s)[1], 0))

    def hm_t(nh, rows):
        return pl.BlockSpec((1, nh, rows, ts), lambda s: (prev(s)[0], 0, 0, prev(s)[1]))

    def tokmajor(nh):
        return pl.BlockSpec((ts, nh * HEAD_DIM), lambda s: (jnp.maximum(s - 1, 0), 0))

    def padded(nh):
        return pl.BlockSpec((1, nh, sp, HEAD_DIM), lambda s: (prev(s)[0], 0, 0, 0))

    def padded_t(nh, rows):
        return pl.BlockSpec((1, nh, rows, sp), lambda s: (prev(s)[0], 0, 0, 0))

    tab = pl.BlockSpec((ts, PAIR), lambda s: (prev(s)[1], 0))
    vec = pl.BlockSpec((1, PAIR), lambda s: (0, 0))
    return pl.pallas_call(
        body,
        grid=(nt + 1,),
        in_specs=[pl.BlockSpec((ts, d), lambda s: (cur(s), 0)),
                  pl.BlockSpec((1, d), lambda s: (0, 0)),
                  pl.BlockSpec((p_cols, d), lambda s: (0, 0)),
                  tab, tab, vec, vec],
        out_specs=[pl.BlockSpec((ts, d), lambda s: (cur(s), 0)), tokmajor(p_cols // HEAD_DIM),
                   tokmajor(ha), hm(kva), hm_t(kva, HEAD_DIM), hm(kva), hm_t(kva, HEAD_DIM + VT_PAD),
                   tokmajor(hb), padded(kvb), padded_t(kvb, HEAD_DIM), padded(kvb),
                   padded_t(kvb, HEAD_DIM + VT_PAD)],
        out_shape=[SDS((t, d), BF16), SDS((t, p_cols), F32),
                   SDS((t, ha * HEAD_DIM), BF16), SDS((bl, kva, s_len, HEAD_DIM), BF16),
                   SDS((bl, kva, HEAD_DIM, s_len), BF16),
                   SDS((bl, kva, s_len, HEAD_DIM), BF16), SDS((bl, kva, HEAD_DIM + VT_PAD, s_len), BF16),
                   SDS((t, hb * HEAD_DIM), BF16),
                   SDS((bl, kvb, sp, HEAD_DIM), BF16), SDS((bl, kvb, HEAD_DIM, sp), BF16),
                   SDS((bl, kvb, sp, HEAD_DIM), BF16), SDS((bl, kvb, HEAD_DIM + VT_PAD, sp), BF16)],
        scratch_shapes=[pltpu.VMEM((ts, p_cols), F32)] * 2,
        compiler_params=_cp("arbitrary"),
        name="inproj_qkprep",
    )(x2, g1, w_t, cos, sin_signed, gq, gk)


def _attn_a_fwd(qa, ka, vat, tq, tk, shards):
    bl, kv, s_len, _ = ka.shape
    ha = qa.shape[1] // HEAD_DIM
    va_rows = vat.shape[2]
    nq, nk = s_len // tq, s_len // tk
    assert nk % 2 == 0
    r = GROUP * tq
    ns = len(shards)

    def body(q_ref, qn_ref, k_ref, v_ref, *rest):
        shard_refs, (o_ref, l_ref), gathered = rest[:ns], rest[ns:ns + 2], rest[ns + 2:2 * ns + 2]
        st_sc, send_sems, recv_sems, local_sems = rest[2 * ns + 2:]
        i = pl.program_id(2)
        step_id = (pl.program_id(0) * kv + pl.program_id(1)) * nq + i
        start, wait = _direct_exchange("gather", shard_refs, gathered, send_sems, recv_sems, local_sems)
        pl.when(step_id == 0)(start)

        q = _heads_t(q_ref[...]).astype(BF16)

        def scores(c, qv):
            return _dot(k_ref[0, 0, pl.ds(pl.multiple_of(c * tk, tk), tk), :], qv)

        def fold(st, c, carry):
            m_old, acc = carry
            m_new = jnp.maximum(m_old, jnp.max(st, axis=0, keepdims=True))
            pt = jnp.exp(st - m_new).astype(BF16)
            vt = v_ref[0, 0, :, pl.ds(pl.multiple_of(c * tk, tk), tk)]
            return m_new, jnp.exp(m_old - m_new) * acc + _dot(vt, pt)

        @pl.when(i == 0)
        def _():
            st_sc[0] = scores(0, q)

        def step(c2, carry):
            c = 2 * c2
            st_sc[1] = scores(c + 1, q)
            carry = fold(st_sc[0], c, carry)
            st_sc[0] = scores(c + 2, q)
            return fold(st_sc[1], c + 1, carry)

        carry = (jnp.full((1, r), -jnp.inf, F32), jnp.zeros((va_rows, r), F32))
        for c2 in range(nk // 2 - 1):
            carry = step(c2, carry)
        st_sc[1] = scores(nk - 1, q)
        carry = fold(st_sc[0], nk - 2, carry)
        st_sc[0] = scores(0, _heads_t(qn_ref[...]).astype(BF16))
        m, acc = fold(st_sc[1], nk - 1, carry)
        l = acc[HEAD_DIM:HEAD_DIM + 1, :]
        o_ref[...] = _heads_t_inv(acc[0:HEAD_DIM, :] / l).astype(BF16)
        l_ref[0, 0, 0] = jnp.broadcast_to(m + jnp.log(l), (8, r))
        pl.when(step_id == bl * kv * nq - 1)(wait)

    anyspec = pl.BlockSpec(memory_space=pl.ANY)
    res = pl.pallas_call(
        body,
        grid=(bl, kv, nq),
        in_specs=[pl.BlockSpec((tq, GROUP * HEAD_DIM), lambda b, g, i: (b * nq + i, g)),
                  pl.BlockSpec((tq, GROUP * HEAD_DIM), lambda b, g, i: (b * nq + jnp.minimum(i + 1, nq - 1), g)),
                  pl.BlockSpec((1, 1, s_len, HEAD_DIM), lambda b, g, i: (b, g, 0, 0)),
                  pl.BlockSpec((1, 1, va_rows, s_len), lambda b, g, i: (b, g, 0, 0))] + [anyspec] * ns,
        out_specs=[pl.BlockSpec((tq, GROUP * HEAD_DIM), lambda b, g, i: (b * nq + i, g)),
                   pl.BlockSpec((1, 1, 1, 8, r), lambda b, g, i: (b, g, i, 0, 0))] + [anyspec] * ns,
        out_shape=[SDS((bl * s_len, ha * HEAD_DIM), BF16), SDS((bl, kv, nq, 8, r), F32)]
        + [SDS((N_DEV,) + s.shape, s.dtype) for s in shards],
        scratch_shapes=[pltpu.VMEM((2, tk, r), F32)] + _exchange_scratch(ns),
        compiler_params=_cp("arbitrary", "arbitrary", "arbitrary"),
        name="attn_a_fwd",
    )(qa, qa, ka, vat, *shards)
    return res[0], res[1], res[2:]


FFN_BWD_TOKENS = 256
QB_PER_STEP = 16


def _bias_variant(n, nb):
    return jnp.where(n == 0, 1, jnp.where(n == nb - 1, 2, 0))


def _sink_row(sink_ref, g):
    return jnp.concatenate([jnp.full((1, BLOCK), sink_ref[0, g * GROUP + h], F32) for h in range(GROUP)], axis=1)


def _attn_b_fwd(qb, kb, vbt, bias_t, sink, s_len):
    bl, kv, sp, _ = kb.shape
    hb = qb.shape[1] // HEAD_DIM
    vt_rows = vbt.shape[2]
    nb = s_len // BLOCK
    nbs = min(QB_PER_STEP, nb)
    r = GROUP * BLOCK

    def body(q_ref, k_ref, vt_ref, bt_ref, sink_ref, o_ref, l_ref, st_sc, pb_sc):
        g, n0 = pl.program_id(1), pl.program_id(2) * nbs
        sink_row = _sink_row(sink_ref, g)

        def span(j):
            return pl.ds(pl.multiple_of((n0 + j) * BLOCK, BLOCK), SPAN)

        for j in range(nbs):
            qt = _heads_t(q_ref[j * BLOCK:(j + 1) * BLOCK, :]).astype(BF16)
            st_sc[j] = _dot(k_ref[0, 0, span(j), :], qt) + bt_ref[_bias_variant(n0 + j, nb), 0]
        maxes = []
        for j in range(nbs):
            st = st_sc[j]
            m = jnp.maximum(jnp.max(st, axis=0, keepdims=True), sink_row)
            pb_sc[j] = jnp.exp(st - m).astype(BF16)
            maxes.append(m)
        for j in range(nbs):
            m = maxes[j]
            acc = _dot(vt_ref[0, 0, :, span(j)], pb_sc[j])
            l = acc[HEAD_DIM:HEAD_DIM + 1, :] + jnp.exp(sink_row - m)
            o_ref[j * BLOCK:(j + 1) * BLOCK, :] = _heads_t_inv(acc[0:HEAD_DIM, :] / l).astype(BF16)
            l_ref[0, 0, j] = jnp.broadcast_to(m + jnp.log(l), (8, r))

    return pl.pallas_call(
        body,
        grid=(bl, kv, nb // nbs),
        in_specs=[pl.BlockSpec((nbs * BLOCK, GROUP * HEAD_DIM), lambda b, g, n: (b * (nb // nbs) + n, g)),
                  pl.BlockSpec((1, 1, sp, HEAD_DIM), lambda b, g, n: (b, g, 0, 0)),
                  pl.BlockSpec((1, 1, vt_rows, sp), lambda b, g, n: (b, g, 0, 0)),
                  pl.BlockSpec((3, 1, SPAN, r), lambda b, g, n: (0, g, 0, 0)),
                  pl.BlockSpec(memory_space=pltpu.SMEM)],
        out_specs=[pl.BlockSpec((nbs * BLOCK, GROUP * HEAD_DIM), lambda b, g, n: (b * (nb // nbs) + n, g)),
                   pl.BlockSpec((1, 1, nbs, 8, r), lambda b, g, n: (b, g, n, 0, 0))],
        out_shape=[SDS((bl * s_len, hb * HEAD_DIM), BF16), SDS((bl, kv, nb, 8, r), F32)],
        scratch_shapes=[pltpu.VMEM((nbs, SPAN, r), F32), pltpu.VMEM((nbs, SPAN, r), BF16)],
        compiler_params=_cp("parallel", "parallel", "arbitrary"),
        name="attn_b_fwd",
    )(qb, kb, vbt, bias_t, sink)


def _mixout(oa, ob, wo, x2, g2, g3, tm):
    t, d = x2.shape
    ca = oa.shape[1]

    def body(oa_ref, ob_ref, w_ref, x_ref, g2_ref, g3_ref, mix_ref, x1_ref, h2_ref):
        mix = _dot(oa_ref[...], w_ref[0:ca, :]) + _dot(ob_ref[...], w_ref[ca:, :])
        mix_ref[...] = mix
        y2, _, _ = _rms_fwd(mix, g2_ref[...])
        x1 = x_ref[...] + y2
        x1_ref[...] = x1
        y3, _, _ = _rms_fwd(x1, g3_ref[...])
        h2_ref[...] = y3.astype(BF16)

    tile = lambda w: pl.BlockSpec((tm, w), lambda i: (i, 0))
    vec = pl.BlockSpec((1, d), lambda i: (0, 0))
    return pl.pallas_call(
        body,
        grid=(t // tm,),
        in_specs=[tile(ca), tile(ob.shape[1]), pl.BlockSpec(wo.shape, lambda i: (0, 0)), tile(d), vec, vec],
        out_specs=[tile(d), tile(d), tile(d)],
        out_shape=[SDS((t, d), F32), SDS((t, d), F32), SDS((t, d), BF16)],
        compiler_params=_cp("parallel"),
        name="mixout",
    )(oa, ob, wo, x2, g2, g3)


def _ffn_fwd(h2, wup_g, wdn, x1, target, g4, tm):
    t, d = x1.shape
    nblk, _, tf = wup_g.shape
    ff = nblk * tf
    nt = t // tm

    def body(h_ref, wu_ref, wd_ref, x1_ref, tg_ref, g_ref, u_ref, df_ref, dy_ref, dg_ref, loss_ref, f_sc):
        s = pl.program_id(0)

        @pl.when(s == 0)
        def _():
            f_sc[...] = jnp.zeros_like(f_sc)
            dg_ref[...] = jnp.zeros_like(dg_ref)
            loss_ref[...] = jnp.zeros_like(loss_ref)

        counted = (s > 0).astype(F32)

        def loss_part():
            g = g_ref[...]
            y4, n, r = _rms_fwd(f_sc[...], g)
            e = (x1_ref[...] + y4) - tg_ref[...]
            loss_ref[...] += jnp.sum(e * e) * (0.5 / d) * counted
            dy = e * (1.0 / d)
            dy_ref[...] = dy
            return n, r, g, dy

        def norm_bwd_part(n, r, g, dy):
            df, dgt = _rms_bwd(n, r, g, dy)
            df_ref[...] = df.astype(BF16)
            dg_ref[0:1, :] += jnp.sum(dgt, axis=0, keepdims=True) * counted

        @pl.when(s < nt)
        def _():
            h = h_ref[...]
            squares = []
            saved = None
            for c in range(nblk):
                u = jnp.maximum(_dot(h, wu_ref[c]), 0.0)
                u_ref[:, c * tf:(c + 1) * tf] = u.astype(BF16)
                squares.append((u * u).astype(BF16))
                if c == 0:
                    saved = loss_part()
                elif c == 1:
                    norm_bwd_part(*saved)
            f_sc[...] = _dot(jnp.concatenate(squares, axis=1), wd_ref[...])

        @pl.when(s == nt)
        def _():
            norm_bwd_part(*loss_part())

    cur = lambda s: (jnp.minimum(s, nt - 1), 0)
    prev = lambda s: (jnp.maximum(s - 1, 0), 0)
    return pl.pallas_call(
        body,
        grid=(nt + 1,),
        in_specs=[pl.BlockSpec((tm, d), cur),
                  pl.BlockSpec((nblk, d, tf), lambda s: (0, 0, 0)),
                  pl.BlockSpec((ff, d), lambda s: (0, 0)),
                  pl.BlockSpec((tm, d), prev), pl.BlockSpec((tm, d), prev),
                  pl.BlockSpec((1, d), lambda s: (0, 0))],
        out_specs=[pl.BlockSpec((tm, ff), cur), pl.BlockSpec((tm, d), prev), pl.BlockSpec((tm, d), prev),
                   pl.BlockSpec((8, d), lambda s: (0, 0)),
                   pl.BlockSpec((8, 128), lambda s: (0, 0))],
        out_shape=[SDS((t, ff), BF16), SDS((t, d), BF16), SDS((t, d), F32), SDS((8, d), F32), SDS((8, 128), F32)],
        scratch_shapes=[pltpu.VMEM((tm, d), F32)],
        compiler_params=_cp("arbitrary"),
        name="ffn_fwd",
    )(h2, wup_g, wdn, x1, target, g4)


def _ffn_bwd(df, u, wdn, wup_g, x1, dy, mix, g3, g2, tm):
    t, d = x1.shape
    nblk, _, tf = wup_g.shape
    ff = nblk * tf
    nt = t // tm

    def body(df_ref, u_ref, wd_ref, wu_ref, x1_ref, dy_ref, mix_ref, g3_ref, g2_ref,
             dpre_ref, dx1_ref, dmix_ref, dg3_ref, dg2_ref, dh_sc):
        s = pl.program_id(0)

        @pl.when(s == 0)
        def _():
            dh_sc[...] = jnp.zeros_like(dh_sc)
            dg3_ref[...] = jnp.zeros_like(dg3_ref)
            dg2_ref[...] = jnp.zeros_like(dg2_ref)

        counted = (s > 0).astype(F32)

        def residual_norm_part():
            g3 = g3_ref[...]
            _, n3, r3 = _rms_fwd(x1_ref[...], g3)
            dx, dgt3 = _rms_bwd(n3, r3, g3, dh_sc[...])
            dx1 = dy_ref[...] + dx
            dx1_ref[...] = dx1
            dg3_ref[0:1, :] += jnp.sum(dgt3, axis=0, keepdims=True) * counted

        def mix_norm_part():
            g2 = g2_ref[...]
            _, n2, r2 = _rms_fwd(mix_ref[...], g2)
            dmix, dgt2 = _rms_bwd(n2, r2, g2, dx1_ref[...])
            dmix_ref[...] = dmix.astype(BF16)
            dg2_ref[0:1, :] += jnp.sum(dgt2, axis=0, keepdims=True) * counted

        @pl.when(s < nt)
        def _():
            du2 = _dot_nt(df_ref[...], wd_ref[...])
            dpre = (2.0 * u_ref[...].astype(F32) * du2).astype(BF16)
            dpre_ref[...] = dpre
            dh = _dot_nt(dpre[:, 0:tf], wu_ref[0])
            for c in range(1, nblk):
                if c == 1:
                    residual_norm_part()
                elif c == 3:
                    mix_norm_part()
                dh = dh + _dot_nt(dpre[:, c * tf:(c + 1) * tf], wu_ref[c])
            dh_sc[...] = dh

        @pl.when(s == nt)
        def _():
            residual_norm_part()
            mix_norm_part()

    cur = lambda s: (jnp.minimum(s, nt - 1), 0)
    prev = lambda s: (jnp.maximum(s - 1, 0), 0)
    vec = pl.BlockSpec((1, d), lambda s: (0, 0))
    acc8 = pl.BlockSpec((8, d), lambda s: (0, 0))
    return pl.pallas_call(
        body,
        grid=(nt + 1,),
        in_specs=[pl.BlockSpec((tm, d), cur),
                  pl.BlockSpec((tm, ff), cur),
                  pl.BlockSpec((ff, d), lambda s: (0, 0)),
                  pl.BlockSpec((nblk, d, tf), lambda s: (0, 0, 0)),
                  pl.BlockSpec((tm, d), prev), pl.BlockSpec((tm, d), prev), pl.BlockSpec((tm, d), prev), vec, vec],
        out_specs=[pl.BlockSpec((tm, ff), cur), pl.BlockSpec((tm, d), prev), pl.BlockSpec((tm, d), prev),
                   acc8, acc8],
        out_shape=[SDS(u.shape, BF16), SDS((t, d), F32), SDS((t, d), BF16), SDS((8, d), F32), SDS((8, d), F32)],
        scratch_shapes=[pltpu.VMEM((tm, d), F32)],
        compiler_params=_cp("arbitrary"),
        name="ffn_bwd",
    )(df, u, wdn, wup_g, x1, dy, mix, g3, g2)


def _wgrad(a, b, a_spec, b_spec, out_block, out_shape, nj, nk, name, prep_a=None, prep_b=None):
    acc_shape = out_block[1:]

    def body(a_ref, b_ref, o_ref, acc_sc):
        k = pl.program_id(1)
        av = a_ref[...] if prep_a is None else prep_a(a_ref)
        bv = b_ref[...] if prep_b is None else prep_b(b_ref)
        part = _dot_tn(av, bv)

        @pl.when(k == 0)
        def _():
            acc_sc[...] = part

        @pl.when(k > 0)
        def _():
            acc_sc[...] += part

        @pl.when(k == nk - 1)
        def _():
            o_ref[0] = acc_sc[...].astype(BF16)

    return pl.pallas_call(
        body,
        grid=(nj, nk),
        in_specs=[a_spec, b_spec],
        out_specs=pl.BlockSpec(out_block, lambda j, k: (j, 0, 0)),
        out_shape=SDS(out_shape, BF16),
        scratch_shapes=[pltpu.VMEM(acc_shape, F32)],
        compiler_params=_cp("parallel", "arbitrary"),
        name=name,
    )(a, b)


def _wgrad_cols(a, b, nj, tt, name):
    t, m = a.shape
    bn = b.shape[1] // nj
    return _wgrad(a, b, pl.BlockSpec((tt, m), lambda j, k: (k, 0)), pl.BlockSpec((tt, bn), lambda j, k: (k, j)),
                  (1, m, bn), (nj, m, bn), nj, t // tt, name)


def _wgrad_rows(a, b, nj, tt, name, square=False):
    t, n = b.shape
    bm = a.shape[1] // nj

    def squared(a_ref):
        af = a_ref[...].astype(F32)
        return (af * af).astype(BF16)

    return _wgrad(a, b, pl.BlockSpec((tt, bm), lambda j, k: (k, j)), pl.BlockSpec((tt, n), lambda j, k: (k, 0)),
                  (1, bm, n), (nj, bm, n), nj, t // tt, name, prep_a=squared if square else None)


def _wgrad_o(oa, ob, dmix, nj, tt):
    t, n = dmix.shape
    ca, cb = oa.shape[1], ob.shape[1]
    m = ca + cb
    nk = t // tt

    def body(oa_ref, ob_ref, b_ref, o_ref, acc_sc):
        k = pl.program_id(0)
        part = _dot_tn(jnp.concatenate([oa_ref[...], ob_ref[...]], axis=1), b_ref[...])

        @pl.when(k == 0)
        def _():
            acc_sc[...] = part

        @pl.when(k > 0)
        def _():
            acc_sc[...] += part

        @pl.when(k == nk - 1)
        def _():
            o_ref[...] = acc_sc[...].reshape(nj, m // nj, n).astype(BF16)

    return pl.pallas_call(
        body,
        grid=(nk,),
        in_specs=[pl.BlockSpec((tt, ca), lambda k: (k, 0)), pl.BlockSpec((tt, cb), lambda k: (k, 0)),
                  pl.BlockSpec((tt, n), lambda k: (k, 0))],
        out_specs=pl.BlockSpec((nj, m // nj, n), lambda k: (0, 0, 0)),
        out_shape=SDS((nj, m // nj, n), BF16),
        scratch_shapes=[pltpu.VMEM((m, n), F32)],
        compiler_params=_cp("arbitrary"),
        name="wgrad_o",
    )(oa, ob, dmix)


def _attn_out_bwd(dmix, wo, ca, tm):
    t, d = dmix.shape
    cb = wo.shape[0] - ca

    def body(dm_ref, w_ref, da_ref, db_ref):
        dm = dm_ref[...]
        da_ref[...] = _dot_nt(dm, w_ref[0:ca, :]).astype(BF16)
        db_ref[...] = _dot_nt(dm, w_ref[ca:, :]).astype(BF16)

    return pl.pallas_call(
        body,
        grid=(t // tm,),
        in_specs=[pl.BlockSpec((tm, d), lambda i: (i, 0)), pl.BlockSpec(wo.shape, lambda i: (0, 0))],
        out_specs=[pl.BlockSpec((tm, ca), lambda i: (i, 0)), pl.BlockSpec((tm, cb), lambda i: (i, 0))],
        out_shape=[SDS((t, ca), BF16), SDS((t, cb), BF16)],
        compiler_params=_cp("parallel"),
        name="attn_out_bwd",
    )(dmix, wo)


def _heads_t(x):
    xt = x.astype(F32).T
    return jnp.concatenate([xt[h * HEAD_DIM:(h + 1) * HEAD_DIM, :] for h in range(GROUP)], axis=1)


def _heads_t_inv(yt):
    n = yt.shape[1] // GROUP
    return jnp.concatenate([yt[:, h * n:(h + 1) * n] for h in range(GROUP)], axis=0).T


def _attn_a_bwd(qa, ka, kat, va, do, o, lse, tq, tk, grads):
    bl, kv, s_len, _ = ka.shape
    nq, nk = s_len // tq, s_len // tk
    assert nk % 2 == 0
    r = GROUP * tq
    ng = len(grads)

    def body(q_ref, qn_ref, k_ref, kt_ref, v_ref, do_ref, don_ref, o_ref, l_ref, *rest):
        grad_refs, (dq_ref, dk_ref, dv_ref), parts = rest[:ng], rest[ng:ng + 3], rest[ng + 3:2 * ng + 3]
        st_sc, dp_sc, dkt_sc, dvt_sc, send_sems, recv_sems, local_sems = rest[2 * ng + 3:]
        i = pl.program_id(2)
        step_id = (pl.program_id(0) * kv + pl.program_id(1)) * nq + i
        start, wait = _direct_exchange("scatter", grad_refs, parts, send_sems, recv_sems, local_sems)
        pl.when(step_id == 0)(start)

        dot32 = _heads_t(do_ref[...])
        drow = jnp.sum(dot32 * _heads_t(o_ref[...]), axis=0, keepdims=True)
        qt, dot = _heads_t(q_ref[...]).astype(BF16), dot32.astype(BF16)
        lrow = l_ref[0, 0, 0, 0:1, :]

        @pl.when(i == 0)
        def _():
            dkt_sc[...] = jnp.zeros_like(dkt_sc)
            dvt_sc[...] = jnp.zeros_like(dvt_sc)

        def chunk(c):
            return pl.ds(pl.multiple_of(c * tk, tk), tk)

        def scores(c, slot, qv=qt, dov=dot):
            st_sc[slot] = _dot(k_ref[0, 0, chunk(c), :], qv)
            dp_sc[slot] = _dot(v_ref[0, 0, chunk(c), :], dov)

        def fold(slot, c, dqt):
            pt = jnp.exp(st_sc[slot] - lrow)
            dsb = (pt * (dp_sc[slot] - drow)).astype(BF16)
            dvt_sc[:, chunk(c)] += _dot_nt(dot, pt.astype(BF16))
            dkt_sc[:, chunk(c)] += _dot_nt(qt, dsb)
            return dqt + _dot(kt_ref[0, 0, :, chunk(c)], dsb)

        @pl.when(i == 0)
        def _():
            scores(0, 0)

        def step(c2, dqt):
            c = 2 * c2
            scores(c + 1, 1)
            dqt = fold(0, c, dqt)
            scores(c + 2, 0)
            return fold(1, c + 1, dqt)

        dqt = jnp.zeros((HEAD_DIM, r), F32)
        for c2 in range(nk // 2 - 1):
            dqt = step(c2, dqt)
        scores(nk - 1, 1)
        dqt = fold(0, nk - 2, dqt)
        scores(0, 0, _heads_t(qn_ref[...]).astype(BF16), _heads_t(don_ref[...]).astype(BF16))
        dq_ref[...] = _heads_t_inv(fold(1, nk - 1, dqt))

        @pl.when(i == nq - 1)
        def _():
            dk_ref[0, 0] = dkt_sc[...].T
            dv_ref[0, 0] = dvt_sc[...].T

        pl.when(step_id == bl * kv * nq - 1)(wait)

    kvspec = pl.BlockSpec((1, 1, s_len, HEAD_DIM), lambda b, g, i: (b, g, 0, 0))
    tok = pl.BlockSpec((tq, GROUP * HEAD_DIM), lambda b, g, i: (b * nq + i, g))
    toknext = pl.BlockSpec((tq, GROUP * HEAD_DIM), lambda b, g, i: (b * nq + jnp.minimum(i + 1, nq - 1), g))
    anyspec = pl.BlockSpec(memory_space=pl.ANY)
    res = pl.pallas_call(
        body,
        grid=(bl, kv, nq),
        in_specs=[tok, toknext, kvspec, pl.BlockSpec((1, 1, HEAD_DIM, s_len), lambda b, g, i: (b, g, 0, 0)), kvspec,
                  tok, toknext, tok, pl.BlockSpec((1, 1, 1, 8, r), lambda b, g, i: (b, g, i, 0, 0))] + [anyspec] * ng,
        out_specs=[tok, kvspec, kvspec] + [anyspec] * ng,
        out_shape=[SDS(qa.shape, F32), SDS(ka.shape, F32), SDS(va.shape, F32)]
        + [SDS(g.shape, g.dtype) for g in grads],
        scratch_shapes=[pltpu.VMEM((2, tk, r), F32), pltpu.VMEM((2, tk, r), F32),
                        pltpu.VMEM((HEAD_DIM, s_len), F32), pltpu.VMEM((HEAD_DIM, s_len), F32)]
        + _exchange_scratch(ng),
        compiler_params=_cp("arbitrary", "arbitrary", "arbitrary"),
        name="attn_a_bwd",
    )(qa, qa, ka, kat, va, do, do, o, lse, *grads)
    return res[0], res[1], res[2], res[3:]


def _attn_b_bwd(qb, kb, kbt, vb, do, o, lse, bias_t, sink, s_len):
    bl, kv, sp, _ = kb.shape
    nb = s_len // BLOCK
    nbs = min(QB_PER_STEP, nb)
    r = GROUP * BLOCK

    def body(q_ref, k_ref, kt_ref, v_ref, do_ref, o_ref, l_ref, bt_ref, sink_ref,
             dq_ref, dk_ref, dv_ref, dsum_ref, dsink_ref, dkt_sc, dvt_sc):
        g, b, ns = pl.program_id(0), pl.program_id(1), pl.program_id(2)
        sink_row = _sink_row(sink_ref, g)

        @pl.when(ns == 0)
        def _():
            dkt_sc[...] = jnp.zeros_like(dkt_sc)
            dvt_sc[...] = jnp.zeros_like(dvt_sc)

        @pl.when((b == 0) & (ns == 0))
        def _():
            dsum_ref[...] = jnp.zeros_like(dsum_ref)
            dsink_ref[...] = jnp.zeros_like(dsink_ref)

        dsum = jnp.zeros((SPAN, r), F32)
        dsink = jnp.zeros((1, r), F32)
        for j in range(nbs):
            n = ns * nbs + j
            span = pl.ds(pl.multiple_of(n * BLOCK, BLOCK), SPAN)
            rows = slice(j * BLOCK, (j + 1) * BLOCK)
            dot32 = _heads_t(do_ref[rows, :])
            drow = jnp.sum(dot32 * _heads_t(o_ref[rows, :]), axis=0, keepdims=True)
            qt, dot = _heads_t(q_ref[rows, :]).astype(BF16), dot32.astype(BF16)
            lrow = l_ref[0, 0, j, 0:1, :]
            st = _dot(k_ref[0, 0, span, :], qt) + bt_ref[_bias_variant(n, nb), 0]
            pt = jnp.exp(st - lrow)
            dst = pt * (_dot(v_ref[0, 0, span, :], dot) - drow)
            dsum = dsum + dst
            dsink = dsink - jnp.exp(sink_row - lrow) * drow
            dsb = dst.astype(BF16)
            dvt_sc[:, span] += _dot_nt(dot, pt.astype(BF16))
            dkt_sc[:, span] += _dot_nt(qt, dsb)
            dq_ref[rows, :] = _heads_t_inv(_dot(kt_ref[0, 0, :, span], dsb))
        dsum_ref[0] += dsum
        dsink_ref[0, 0:1, :] += dsink

        @pl.when(ns == nb // nbs - 1)
        def _():
            dk_ref[0, 0] = dkt_sc[:, BLOCK:BLOCK + s_len].T
            dv_ref[0, 0] = dvt_sc[:, BLOCK:BLOCK + s_len].T

    kvspec = pl.BlockSpec((1, 1, sp, HEAD_DIM), lambda g, b, n: (b, g, 0, 0))
    kvout = pl.BlockSpec((1, 1, s_len, HEAD_DIM), lambda g, b, n: (b, g, 0, 0))
    tok = pl.BlockSpec((nbs * BLOCK, GROUP * HEAD_DIM), lambda g, b, n: (b * (nb // nbs) + n, g))
    return pl.pallas_call(
        body,
        grid=(kv, bl, nb // nbs),
        in_specs=[tok, kvspec, pl.BlockSpec((1, 1, HEAD_DIM, sp), lambda g, b, n: (b, g, 0, 0)), kvspec, tok, tok,
                  pl.BlockSpec((1, 1, nbs, 8, r), lambda g, b, n: (b, g, n, 0, 0)),
                  pl.BlockSpec((3, 1, SPAN, r), lambda g, b, n: (0, g, 0, 0)),
                  pl.BlockSpec(memory_space=pltpu.SMEM)],
        out_specs=[tok, kvout, kvout,
                   pl.BlockSpec((1, SPAN, r), lambda g, b, n: (g, 0, 0)),
                   pl.BlockSpec((1, 8, r), lambda g, b, n: (g, 0, 0))],
        out_shape=[SDS(qb.shape, F32), SDS((bl, kv, s_len, HEAD_DIM), F32), SDS((bl, kv, s_len, HEAD_DIM), F32),
                   SDS((kv, SPAN, r), F32), SDS((kv, 8, r), F32)],
        scratch_shapes=[pltpu.VMEM((HEAD_DIM, sp), F32), pltpu.VMEM((HEAD_DIM, sp), F32)],
        compiler_params=_cp("arbitrary", "arbitrary", "arbitrary"),
        name="attn_b_bwd",
    )(qb, kb, kbt, vb, do, o, lse, bias_t, sink)


def _bias_reduce(dsum, dsink, bucket_t4):
    kv, _, r = dsum.shape

    def body(ds_ref, dk_ref, bk_ref, rel_ref, sink_ref):
        lane = lax.broadcasted_iota(jnp.int32, (N_BUCKETS, 128), 1)
        lane8 = lax.broadcasted_iota(jnp.int32, (8, 128), 1)
        bk = bk_ref[...]
        for g in range(kv):
            ds = ds_ref[g]
            rowi = lax.broadcasted_iota(jnp.int32, (N_BUCKETS, r), 0)
            red = jnp.zeros((N_BUCKETS, r), F32)
            for b in range(N_BUCKETS):
                red = jnp.where(rowi == b, jnp.sum(jnp.where(bk == b, ds, 0.0), axis=0, keepdims=True), red)
            out = jnp.zeros((N_BUCKETS, 128), F32)
            so = jnp.zeros((8, 128), F32)
            for h in range(GROUP):
                col = jnp.sum(red[:, h * BLOCK:(h + 1) * BLOCK], axis=1, keepdims=True)
                out = jnp.where(lane == h, col, out)
                sc = jnp.sum(dk_ref[g][:, h * BLOCK:(h + 1) * BLOCK], axis=1, keepdims=True)
                so = jnp.where(lane8 == h, sc, so)
            rel_ref[g] = out
            sink_ref[g] = so

    vm = pl.BlockSpec(memory_space=pltpu.VMEM)
    return pl.pallas_call(
        body,
        in_specs=[vm, vm, vm],
        out_specs=[vm, vm],
        out_shape=[SDS((kv, N_BUCKETS, 128), F32), SDS((kv, 8, 128), F32)],
        name="bias_reduce",
    )(dsum, dsink, bucket_t4)


def _dqkprep(dqa, dka, dva, dqb, dkb, dvb, proj, h1, cos, sin_signed, gq, gk, s_len, ts):
    t, p_cols = proj.shape
    d = h1.shape[1]
    bl, kva, kvb = dka.shape[0], dka.shape[1], dkb.shape[1]
    ha, hb = dqa.shape[1] // HEAD_DIM, dqb.shape[1] // HEAD_DIM
    ns = s_len // ts

    def body(dqa_ref, dka_ref, dva_ref, dqb_ref, dkb_ref, dvb_ref, p_ref, h1_ref, h1p_ref, cos_ref, sin_ref,
             gq_ref, gk_ref, dp_ref, dgq_ref, dgk_ref, gw_ref, gw_sc, dpp_sc):
        b, i = pl.program_id(0), pl.program_id(1)
        cs, sn = cos_ref[...], sin_ref[...]
        low, first = _pair_masks(ts)

        @pl.when((b == 0) & (i == 0))
        def _():
            dgq_ref[...] = jnp.zeros_like(dgq_ref)
            dgk_ref[...] = jnp.zeros_like(dgk_ref)
            gw_sc[...] = jnp.zeros_like(gw_sc)
            dpp_sc[...] = jnp.zeros_like(dpp_sc)

        gw_sc[...] += _dot_tn(dpp_sc[...], h1p_ref[...])

        def grad_pair(ref, p):
            return jnp.concatenate([ref[0, 2 * p], ref[0, 2 * p + 1]], axis=1)

        def put(p, val):
            dp_ref[:, p * PAIR:(p + 1) * PAIR] = val.astype(BF16)

        def unrope_norm(d_rot, p, g, dg_ref):
            dn = d_rot * cs + _pair_partner(d_rot * sn, first)
            xp = p_ref[:, p * PAIR:(p + 1) * PAIR]
            r = lax.rsqrt(_pair_mean(xp * xp, low) + EPS)
            n = xp * r
            gd = g * dn
            dg_ref[0:1, :] += jnp.sum(dn * n, axis=0, keepdims=True)
            put(p, r * (gd - n * _pair_mean(n * gd, low)))

        for p in range(ha // 2):
            unrope_norm(dqa_ref[:, p * PAIR:(p + 1) * PAIR] * SCALE, p, gq_ref[...], dgq_ref)
        base = ha // 2
        for p in range(kva // 2):
            unrope_norm(grad_pair(dka_ref, p), base + p, gk_ref[...], dgk_ref)
            put(base + kva // 2 + p, grad_pair(dva_ref, p))
        base += kva
        for p in range(hb // 2):
            put(base + p, dqb_ref[:, p * PAIR:(p + 1) * PAIR] * SCALE)
        base += hb // 2
        for p in range(kvb // 2):
            put(base + p, grad_pair(dkb_ref, p))
            put(base + kvb // 2 + p, grad_pair(dvb_ref, p))

        dpp_sc[...] = dp_ref[...]

        @pl.when((b == bl - 1) & (i == ns - 1))
        def _():
            gw_ref[...] = (gw_sc[...] + _dot_tn(dp_ref[...], h1_ref[...])).astype(BF16)

    def hm(nh):
        return pl.BlockSpec((1, nh, ts, HEAD_DIM), lambda b, i: (b, 0, i, 0))

    def tokmajor(nh):
        return pl.BlockSpec((ts, nh * HEAD_DIM), lambda b, i: (b * ns + i, 0))

    vec = pl.BlockSpec((1, PAIR), lambda b, i: (0, 0))
    tab = pl.BlockSpec((ts, PAIR), lambda b, i: (i, 0))
    acc = pl.BlockSpec((8, PAIR), lambda b, i: (0, 0))
    pspec = pl.BlockSpec((ts, p_cols), lambda b, i: (b * ns + i, 0))
    return pl.pallas_call(
        body,
        grid=(bl, ns),
        in_specs=[tokmajor(ha), hm(kva), hm(kva), tokmajor(hb), hm(kvb), hm(kvb), pspec,
                  pl.BlockSpec((ts, d), lambda b, i: (b * ns + i, 0)),
                  pl.BlockSpec((ts, d), lambda b, i: (jnp.maximum(b * ns + i - 1, 0), 0)), tab, tab, vec, vec],
        out_specs=[pspec, acc, acc, pl.BlockSpec((p_cols, d), lambda b, i: (0, 0))],
        out_shape=[SDS((t, p_cols), BF16), SDS((8, PAIR), F32), SDS((8, PAIR), F32), SDS((p_cols, d), BF16)],
        scratch_shapes=[pltpu.VMEM((p_cols, d), F32), pltpu.VMEM((ts, p_cols), BF16)],
        compiler_params=_cp("arbitrary", "arbitrary"),
        name="dqkprep",
    )(dqa, dka, dva, dqb, dkb, dvb, proj, h1, h1, cos, sin_signed, gq, gk)


def _dx_final(dproj, w_t, x2, dx1, g1, tm, grads):
    t, d = x2.shape
    p_cols = w_t.shape[0]
    ng = len(grads)
    nsteps = t // tm

    def body(dp_ref, w_ref, x_ref, dx1_ref, g_ref, *rest):
        grad_refs, (dx_ref, dg_ref), parts = rest[:ng], rest[ng:ng + 2], rest[ng + 2:2 * ng + 2]
        start, wait = _direct_exchange("scatter", grad_refs, parts, *rest[2 * ng + 2:])

        @pl.when(pl.program_id(0) == 0)
        def _():
            start()
            dg_ref[...] = jnp.zeros_like(dg_ref)

        dh = _dot(dp_ref[...], w_ref[...])
        g = g_ref[...]
        _, n, r = _rms_fwd(x_ref[...], g)
        dx, dgt = _rms_bwd(n, r, g, dh)
        dx_ref[...] = dx1_ref[...] + dx
        dg_ref[0:1, :] += jnp.sum(dgt, axis=0, keepdims=True)
        pl.when(pl.program_id(0) == nsteps - 1)(wait)

    tile = pl.BlockSpec((tm, d), lambda i: (i, 0))
    anyspec = pl.BlockSpec(memory_space=pl.ANY)
    res = pl.pallas_call(
        body,
        grid=(nsteps,),
        in_specs=[pl.BlockSpec((tm, p_cols), lambda i: (i, 0)),
                  pl.BlockSpec((p_cols, d), lambda i: (0, 0)),
                  tile, tile, pl.BlockSpec((1, d), lambda i: (0, 0))] + [anyspec] * ng,
        out_specs=[tile, pl.BlockSpec((8, d), lambda i: (0, 0))] + [anyspec] * ng,
        out_shape=[SDS((t, d), F32), SDS((8, d), F32)] + [SDS(g.shape, g.dtype) for g in grads],
        scratch_shapes=_exchange_scratch(ng),
        compiler_params=_cp("arbitrary"),
        name="dx_final",
    )(dproj, w_t, x2, dx1, g1, *grads)
    return res[0], res[1], res[2:]


def _adamw_math(w, g, m, v):
    m = ADAM_B1 * m + (1.0 - ADAM_B1) * g
    v = ADAM_B2 * v + (1.0 - ADAM_B2) * (g * g)
    m_hat = m / (1.0 - ADAM_B1 ** ADAM_STEP)
    v_hat = v / (1.0 - ADAM_B2 ** ADAM_STEP)
    delta = -ADAM_LR * (m_hat / (jnp.sqrt(v_hat) + ADAM_EPS) + ADAM_WD * w)
    return delta, m, v


def _adamw_sum(parts, w, m, v, tr, name):
    rows, cols = w.shape

    def body(p_ref, w_ref, m_ref, v_ref, g_ref, d_ref, nm_ref, nv_ref):
        g = p_ref[0].astype(F32)
        for s in range(1, N_DEV):
            g = g + p_ref[s].astype(F32)
        g_ref[...] = g
        d_ref[...], nm_ref[...], nv_ref[...] = _adamw_math(w_ref[...], g, m_ref[...], v_ref[...])

    tr = min(tr, rows)
    tile = pl.BlockSpec((tr, cols), lambda i: (i, 0))
    return pl.pallas_call(
        body,
        grid=(rows // tr,),
        in_specs=[pl.BlockSpec((N_DEV, tr, cols), lambda i: (0, i, 0)), tile, tile, tile],
        out_specs=[tile] * 4,
        out_shape=[SDS((rows, cols), F32)] * 4,
        compiler_params=_cp("parallel"),
        name=name,
    )(parts, w, m, v)


def _adamw_small(vec, rel, ws, ms, vs):
    hb = ws[6].shape[1]
    n = len(ws)

    def body(vec_ref, rel_ref, *rest):
        w_refs, m_refs, v_refs = rest[:n], rest[n:2 * n], rest[2 * n:3 * n]
        loss_ref, outs = rest[3 * n], rest[3 * n + 1:]
        grads = [vec_ref[0:1, :], vec_ref[1:2, :], vec_ref[2:3, :], vec_ref[3:4, :],
                 vec_ref[4:5, 0:HEAD_DIM], vec_ref[4:5, SMALL_LANES:SMALL_LANES + HEAD_DIM],
                 vec_ref[4:5, 2 * SMALL_LANES:2 * SMALL_LANES + hb], rel_ref[:, 0:hb]]
        loss_ref[...] = vec_ref[4:5, 3 * SMALL_LANES:3 * SMALL_LANES + 1]
        for p, g in enumerate(grads):
            g_ref, d_ref, nm_ref, nv_ref = outs[4 * p:4 * p + 4]
            g_ref[...] = g
            d_ref[...], nm_ref[...], nv_ref[...] = _adamw_math(w_refs[p][...], g, m_refs[p][...], v_refs[p][...])

    vm = pl.BlockSpec(memory_space=pltpu.VMEM)
    res = pl.pallas_call(
        body,
        in_specs=[vm] * (2 + 3 * n),
        out_specs=[vm] * (1 + 4 * n),
        out_shape=[SDS((1, 1), F32)] + [SDS(w.shape, F32) for w in ws for _ in range(4)],
        name="adamw_small",
    )(vec, rel, *ws, *ms, *vs)
    return res[0], [res[1 + 4 * p:5 + 4 * p] for p in range(n)]


def _local_step(x, loss_target, win_s, wo_s, wup_s, wdn_s, g_pre_mix, g_post_mix, q_norm_a, k_norm_a, sink_b,
                rel_bias, g_pre_ffn, g_post_ffn):
    bl, s_len, d = x.shape
    t = bl * s_len
    nh = d // HEAD_DIM
    ha = nh // 2
    kva = ha // GROUP
    hb = nh - ha
    kvb = hb // GROUP
    tm = 512
    tp = min(1024, t)
    tw = min(4096, t)
    ts = min(512, s_len)
    tq, tk = 2 * BLOCK, min(512, s_len // 2)

    x2 = x.reshape(t, d)
    tg2 = loss_target.reshape(t, d)
    cos, sin_signed = _rope_tables(s_len)
    gq2, gk2 = jnp.tile(q_norm_a, (1, 2)), jnp.tile(k_norm_a, (1, 2))
    a = jnp.arange(BLOCK, dtype=jnp.int32)
    c = jnp.arange(SPAN, dtype=jnp.int32)
    bucket_t = _t5_bucket(c[:, None] - BLOCK - a[None, :])
    bucket_t4 = jnp.tile(bucket_t, (1, GROUP))
    (win_g,), bias_t = _weight_gather([win_s], bucket_t, rel_bias)
    w_in_t = win_g.reshape(-1, d)
    p_cols = w_in_t.shape[0]

    h1, proj, qa, ka, kat, va, vat, qb, kb, kbt, vb, vbt = _inproj_qkprep(
        x2, g_pre_mix, w_in_t, cos, sin_signed, gq2, gk2, bl, s_len, ha, kva, hb, kvb, ts)
    oa, lse_a, (wo_g, wup_g, wdn_g) = _attn_a_fwd(qa, ka, vat, tq, tk, [wo_s, wup_s, wdn_s])
    wo = wo_g.reshape(-1, d)
    wdn = wdn_g.reshape(-1, d)
    ob, lse_b = _attn_b_fwd(qb, kb, vbt, bias_t, sink_b, s_len)
    mix, x1, h2 = _mixout(oa, ob, wo, x2, g_post_mix, g_pre_ffn, tp)
    u, df, dy, dg4, loss8 = _ffn_fwd(h2, wup_g, wdn, x1, tg2, g_post_ffn, tm)

    dpre, dx1, dmix, dg3, dg2 = _ffn_bwd(df, u, wdn, wup_g, x1, dy, mix, g_pre_ffn, g_post_mix, FFN_BWD_TOKENS)
    gw_dn = _wgrad_rows(u, df, N_DEV, tw, "wgrad_down", square=True)
    gw_up = _wgrad_cols(h2, dpre, N_DEV, tw, "wgrad_up")
    gw_o = _wgrad_o(oa, ob, dmix, N_DEV, min(2048, t))
    doa, dob = _attn_out_bwd(dmix, wo, oa.shape[1], tp)
    dqa, dka, dva, (p_o, p_up, p_dn) = _attn_a_bwd(qa, ka, kat, va, doa, oa, lse_a, tq, tk, [gw_o, gw_up, gw_dn])
    dqb, dkb, dvb, dsum, dsink = _attn_b_bwd(qb, kb, kbt, vb, dob, ob, lse_b, bias_t, sink_b, s_len)
    drel_g, dsink_g = _bias_reduce(dsum, dsink, bucket_t4)
    dproj, dgq, dgk, gw_in_t = _dqkprep(dqa, dka, dva, dqb, dkb, dvb, proj, h1, cos, sin_signed, gq2, gk2, s_len, ts)
    gw_in_t = gw_in_t.reshape(N_DEV, -1, d)
    grad_x, dg1, (p_in,) = _dx_final(dproj, w_in_t, x2, dx1, g_pre_mix, tp, [gw_in_t])

    vec, rel = _small_allreduce([dg1, dg2, dg3, dg4], dgq, dgk, dsink_g, drel_g, loss8)
    return grad_x.reshape(bl, s_len, d), p_in, p_o, p_up, p_dn, vec, rel


def kernel(x, w_in, w_o, g_pre_mix, g_post_mix, q_norm_a, k_norm_a, sink_b, rel_bias, g_pre_ffn, w_ffn_up, w_ffn_down, g_post_ffn, loss_target, m_w_in, m_w_o, m_g_pre_mix, m_g_post_mix, m_q_norm_a, m_k_norm_a, m_sink_b, m_rel_bias, m_g_pre_ffn, m_w_ffn_up, m_w_ffn_down, m_g_post_ffn, v_w_in, v_w_o, v_g_pre_mix, v_g_post_mix, v_q_norm_a, v_k_norm_a, v_sink_b, v_rel_bias, v_g_pre_ffn, v_w_ffn_up, v_w_ffn_down, v_g_post_ffn):
    w_in_t = w_in[0].T

    grad_x, p_in, p_o, p_up, p_dn, vec, rel = _local_step(
        x, loss_target, w_in_t.astype(BF16), w_o[0].astype(BF16), w_ffn_up[0].astype(BF16), w_ffn_down[0].astype(BF16),
        g_pre_mix, g_post_mix, q_norm_a, k_norm_a, sink_b, rel_bias, g_pre_ffn, g_post_ffn)

    big = {
        "w_in": [a.T for a in _adamw_sum(p_in, w_in_t, m_w_in[0].T, v_w_in[0].T, 192, "adamw_in")],
        "w_o": _adamw_sum(p_o, w_o[0], m_w_o[0], v_w_o[0], 128, "adamw_o"),
        "w_up": _adamw_sum(p_up, w_ffn_up[0], m_w_ffn_up[0], v_w_ffn_up[0], 256, "adamw_up"),
        "w_dn": _adamw_sum(p_dn, w_ffn_down[0], m_w_ffn_down[0], v_w_ffn_down[0], 256, "adamw_down"),
    }
    loss, small = _adamw_small(
        vec, rel,
        [g_pre_mix, g_post_mix, g_pre_ffn, g_post_ffn, q_norm_a, k_norm_a, sink_b, rel_bias],
        [m_g_pre_mix, m_g_post_mix, m_g_pre_ffn, m_g_post_ffn, m_q_norm_a, m_k_norm_a, m_sink_b, m_rel_bias],
        [v_g_pre_mix, v_g_post_mix, v_g_pre_ffn, v_g_post_ffn, v_q_norm_a, v_k_norm_a, v_sink_b, v_rel_bias])
    s_pre_mix, s_post_mix, s_pre_ffn, s_post_ffn, s_qn, s_kn, s_sink, s_rel = small

    def outs(kind):
        return [big["w_in"][kind][None], big["w_o"][kind][None], s_pre_mix[kind], s_post_mix[kind], s_qn[kind],
                s_kn[kind], s_sink[kind], s_rel[kind], s_pre_ffn[kind], big["w_up"][kind][None],
                big["w_dn"][kind][None], s_post_ffn[kind]]

    return (loss.reshape(()), grad_x, *outs(0), *outs(1), *outs(2), *outs(3))
```

```python
import functools

import jax
import jax.numpy as jnp
import numpy as np
from jax import lax
from jax.experimental import pallas as pl
from jax.experimental.pallas import tpu as pltpu

F32 = jnp.float32
BF16 = jnp.bfloat16
SDS = jax.ShapeDtypeStruct

N_DEV = 8
HEAD_DIM = 64
GROUP = 4
BLOCK = 128
SPAN = 3 * BLOCK
GRID_W = 64
N_BUCKETS = 32
MAX_DISTANCE = 128
ROPE_THETA = 10000.0
EPS = 1e-6
NEG_INF = -1e30
SCALE = HEAD_DIM ** -0.5
VT_PAD = 16

ADAM_LR = 0.001
ADAM_B1 = 0.9
ADAM_B2 = 0.999
ADAM_EPS = 1e-08
ADAM_WD = 0.01
ADAM_STEP = 10

VMEM_LIMIT = 56 * 1024 * 1024
MESH = pl.DeviceIdType.MESH


def _cp(*sem):
    return pltpu.CompilerParams(dimension_semantics=sem, vmem_limit_bytes=VMEM_LIMIT)


def _dot(a, b):
    return jnp.dot(a, b, preferred_element_type=F32)


def _dot_nt(a, b):
    return lax.dot_general(a, b, (((1,), (1,)), ((), ())), preferred_element_type=F32)


def _dot_tn(a, b):
    return lax.dot_general(a, b, (((0,), (0,)), ((), ())), preferred_element_type=F32)


def _rms_fwd(x, g):
    r = lax.rsqrt(jnp.mean(x * x, axis=-1, keepdims=True) + EPS)
    n = x * r
    return n * g, n, r


def _rms_bwd(n, r, g, dy):
    gd = g * dy
    dx = r * (gd - n * jnp.mean(n * gd, axis=-1, keepdims=True))
    return dx, dy * n


def _rope_tables(s_len):
    rows = s_len // GRID_W
    row = np.repeat(np.arange(rows, dtype=np.int32), GRID_W)
    col = np.tile(np.arange(GRID_W, dtype=np.int32), rows)
    nf = HEAD_DIM // 4
    freqs = np.float32(ROPE_THETA) ** (-np.arange(nf, dtype=np.float32) / np.float32(nf))
    ang_r = row.astype(np.float32)[:, None] * freqs[None, :]
    ang_c = col.astype(np.float32)[:, None] * freqs[None, :]
    cr, sr, cc, sc = np.cos(ang_r), np.sin(ang_r), np.cos(ang_c), np.sin(ang_c)
    cos = np.concatenate([cr, cr, cc, cc] * 2, axis=-1).astype(np.float32)
    sin_signed = np.concatenate([-sr, sr, -sc, sc] * 2, axis=-1).astype(np.float32)
    return jnp.asarray(cos), jnp.asarray(sin_signed)


def _t5_bucket(rel):
    nb = N_BUCKETS // 2
    ret = (rel > 0).astype(jnp.int32) * nb
    n = jnp.abs(rel)
    max_exact = nb // 2
    nf = jnp.maximum(n, 1).astype(F32)
    large = max_exact + (jnp.log(nf / max_exact) / np.float32(np.log(MAX_DISTANCE / max_exact))
                         * (nb - max_exact)).astype(jnp.int32)
    large = jnp.minimum(large, nb - 1)
    return ret + jnp.where(n < max_exact, n, large)


def _mesh_pos():
    return lax.axis_index("x"), lax.axis_index("y"), lax.axis_index("c")


def _lin(p):
    return 4 * p[0] + 2 * p[1] + p[2]


def _bias_tables(bkt_ref, tbl_ref, out_ref, hb):
    bkt = bkt_ref[...]
    ci = lax.broadcasted_iota(jnp.int32, (SPAN, BLOCK), 0)
    qi = lax.broadcasted_iota(jnp.int32, (SPAN, BLOCK), 1)
    band = jnp.abs(ci - BLOCK - qi) <= BLOCK
    masks = (band, band & (ci >= BLOCK), band & (ci < 2 * BLOCK))
    for h in range(hb):
        acct = jnp.zeros((SPAN, BLOCK), F32)
        for b in range(N_BUCKETS):
            acct = jnp.where(bkt == b, tbl_ref[b, h], acct)
        lanes = slice((h % GROUP) * BLOCK, (h % GROUP + 1) * BLOCK)
        for var, mask in enumerate(masks):
            out_ref[var, h // GROUP, :, lanes] = jnp.where(mask, acct, NEG_INF)


def _weight_gather(shards, bucket_t, rel_bias):
    n = len(shards)
    hb = rel_bias.shape[1]

    def body(*refs):
        xs, (bkt_ref, tbl_ref), outs, bias_ref = refs[:n], refs[n:n + 2], refs[n + 2:2 * n + 2], refs[2 * n + 2]
        send_sems, recv_sems, local_sems = refs[2 * n + 3:]
        x, y, c = _mesh_pos()
        me, sibling = (x, y, c), (x, y, 1 - c)
        chips = [(1 - x, y), (x, 1 - y), (1 - x, 1 - y)]

        def copy(a, k, block, to, src=None):
            slot = outs[a].at[_lin(block)]
            return pltpu.make_async_remote_copy(
                src_ref=slot if src is None else src, dst_ref=slot,
                send_sem=send_sems.at[a, k], recv_sem=recv_sems.at[a, k],
                device_id=to, device_id_type=MESH)

        started = []
        for a in range(n):
            mine = pltpu.make_async_copy(xs[a], outs[a].at[_lin(me)], local_sems.at[a])
            mine.start()
            started.append(mine)
        sends = []
        for a in range(n):
            first = [copy(a, 0, me, sibling, src=xs[a])]
            first += [copy(a, 1 + j, me, (*chip, c), src=xs[a]) for j, chip in enumerate(chips)]
            for cp in first:
                cp.start()
            sends += first
        _bias_tables(bkt_ref, tbl_ref, bias_ref, hb)
        for a in range(n):
            for j, chip in enumerate(chips):
                copy(a, 1 + j, (*chip, c), me).wait_recv()
                fwd = copy(a, 4 + j, (*chip, c), sibling)
                fwd.start()
                sends.append(fwd)
        for a in range(n):
            copy(a, 0, sibling, me).wait_recv()
            for j, chip in enumerate(chips):
                copy(a, 4 + j, (*chip, 1 - c), me).wait_recv()
        for cp in sends:
            cp.wait_send()
        for mine in started:
            mine.wait()

    anyspec = pl.BlockSpec(memory_space=pl.ANY)
    vm = pl.BlockSpec(memory_space=pltpu.VMEM)
    res = pl.pallas_call(
        body,
        out_shape=[SDS((N_DEV,) + s.shape, s.dtype) for s in shards]
        + [SDS((3, hb // GROUP, SPAN, GROUP * BLOCK), F32)],
        in_specs=[anyspec] * n + [vm, pl.BlockSpec(memory_space=pltpu.SMEM)],
        out_specs=[anyspec] * n + [vm],
        scratch_shapes=[pltpu.SemaphoreType.DMA((n, 7)), pltpu.SemaphoreType.DMA((n, 7)),
                        pltpu.SemaphoreType.DMA((n,))],
        name="weight_gather",
    )(*shards, bucket_t, rel_bias)
    return res[:n], res[n]


def _direct_exchange(kind, ins, outs, send_sems, recv_sems, local_sems):
    x, y, c = _mesh_pos()
    me = (x, y, c)
    peers = [(x, y, 1 - c), (1 - x, y, c), (x, 1 - y, c), (1 - x, 1 - y, c),
             (1 - x, y, 1 - c), (x, 1 - y, 1 - c), (1 - x, 1 - y, 1 - c)]

    def src(a, to):
        return ins[a] if kind == "gather" else ins[a].at[_lin(to)]

    def remote(a, k, to, frm):
        return pltpu.make_async_remote_copy(
            src_ref=src(a, to), dst_ref=outs[a].at[_lin(frm)],
            send_sem=send_sems.at[a, k], recv_sem=recv_sems.at[a, k],
            device_id=to, device_id_type=MESH)

    n = len(ins)
    sends = [remote(a, k, p, me) for a in range(n) for k, p in enumerate(peers)]
    arrivals = [remote(a, k, p, p) for a in range(n) for k, p in enumerate(peers)]
    local = [pltpu.make_async_copy(src(a, me), outs[a].at[_lin(me)], local_sems.at[a]) for a in range(n)]

    def start():
        for cp in local + sends:
            cp.start()

    def wait():
        for cp in arrivals:
            cp.wait_recv()
        for cp in sends:
            cp.wait_send()
        for cp in local:
            cp.wait()

    return start, wait


def _exchange_scratch(n):
    return [pltpu.SemaphoreType.DMA((n, 7)), pltpu.SemaphoreType.DMA((n, 7)), pltpu.SemaphoreType.DMA((n,))]


SMALL_LANES = 128


def _small_allreduce(dg_rows, dgq, dgk, dsink_g, drel_g, loss8):
    d = dg_rows[0].shape[1]
    kv = dsink_g.shape[0]

    def body(g1_ref, g2_ref, g3_ref, g4_ref, gq_ref, gk_ref, sk_ref, rl_ref, ls_ref, vec_ref, rel_ref,
             vbuf, rbuf, vland, rland, send_sems, recv_sems):
        x, y, c = _mesh_pos()
        me = (x, y, c)
        peers = [(x, y, 1 - c), (1 - x, y, c), (x, 1 - y, c), (1 - x, 1 - y, c),
                 (1 - x, y, 1 - c), (x, 1 - y, 1 - c), (1 - x, 1 - y, 1 - c)]
        vbuf[...] = jnp.zeros_like(vbuf)
        rbuf[...] = jnp.zeros_like(rbuf)
        for row, ref in enumerate((g1_ref, g2_ref, g3_ref, g4_ref)):
            vbuf[row:row + 1, :] = ref[0:1, :]
        vbuf[4:5, 0:HEAD_DIM] = gq_ref[0:1, 0:HEAD_DIM] + gq_ref[0:1, HEAD_DIM:PAIR]
        vbuf[4:5, SMALL_LANES:SMALL_LANES + HEAD_DIM] = gk_ref[0:1, 0:HEAD_DIM] + gk_ref[0:1, HEAD_DIM:PAIR]
        for g in range(kv):
            vbuf[4:5, 2 * SMALL_LANES + g * GROUP:2 * SMALL_LANES + (g + 1) * GROUP] = sk_ref[g, 0:1, 0:GROUP]
            rbuf[:, g * GROUP:(g + 1) * GROUP] = rl_ref[g, :, 0:GROUP]
        vbuf[4:5, 3 * SMALL_LANES:3 * SMALL_LANES + 1] = ls_ref[0:1, 0:1]

        def copies(k, to, frm):
            return [pltpu.make_async_remote_copy(
                src_ref=buf, dst_ref=land.at[_lin(frm)], send_sem=send_sems.at[a, k], recv_sem=recv_sems.at[a, k],
                device_id=to, device_id_type=MESH) for a, (buf, land) in enumerate(((vbuf, vland), (rbuf, rland)))]

        sends = [cp for k, p in enumerate(peers) for cp in copies(k, p, me)]
        for cp in sends:
            cp.start()
        vland[_lin(me)] = vbuf[...]
        rland[_lin(me)] = rbuf[...]
        for k, p in enumerate(peers):
            for cp in copies(k, p, p):
                cp.wait_recv()
        for cp in sends:
            cp.wait_send()
        vacc, racc = vland[0], rland[0]
        for s in range(1, N_DEV):
            vacc, racc = vacc + vland[s], racc + rland[s]
        vec_ref[...] = vacc
        rel_ref[...] = racc

    vm = pl.BlockSpec(memory_space=pltpu.VMEM)
    return pl.pallas_call(
        body,
        out_shape=[SDS((8, d), F32), SDS((N_BUCKETS, 128), F32)],
        in_specs=[vm] * 9,
        out_specs=[vm, vm],
        scratch_shapes=[pltpu.VMEM((8, d), F32), pltpu.VMEM((N_BUCKETS, 128), F32),
                        pltpu.VMEM((N_DEV, 8, d), F32), pltpu.VMEM((N_DEV, N_BUCKETS, 128), F32),
                        pltpu.SemaphoreType.DMA((2, 7)), pltpu.SemaphoreType.DMA((2, 7))],
        name="small_allreduce",
    )(*dg_rows, dgq, dgk, dsink_g, drel_g, loss8)


PAIR = 2 * HEAD_DIM


def _pair_masks(ts):
    lane = lax.broadcasted_iota(jnp.int32, (ts, PAIR), 1)
    return lane < HEAD_DIM, (lane % 32) < 16


def _pair_mean(v, low):
    del low
    r = lax.broadcasted_iota(jnp.int32, (PAIR, PAIR), 0) // HEAD_DIM
    c = lax.broadcasted_iota(jnp.int32, (PAIR, PAIR), 1) // HEAD_DIM
    same_head = (r == c).astype(BF16)
    hi = v.astype(BF16)
    lo = (v - hi.astype(F32)).astype(BF16)
    return (_dot(hi, same_head) + _dot(lo, same_head)) * (1.0 / HEAD_DIM)


def _pair_partner(v, first):
    return jnp.where(first, pltpu.roll(v, PAIR - 16, 1), pltpu.roll(v, 16, 1))


def _inproj_qkprep(x2, g1, w_t, cos, sin_signed, gq, gk, bl, s_len, ha, kva, hb, kvb, ts):
    t, d = x2.shape
    p_cols = w_t.shape[0]
    assert ha % 2 == 0 and kva % 2 == 0 and hb % 2 == 0 and kvb % 2 == 0
    ns = s_len // ts
    nt = bl * ns
    sp = s_len + 2 * BLOCK

    def body(*refs):
        kb_ref, kbt_ref, vb_ref, vbt_ref, p_even, p_odd = refs[-6:]
        s = pl.program_id(0)
        i = lax.rem(jnp.maximum(s - 1, 0), ns)

        @pl.when(s == 0)
        def _():
            p_odd[...] = jnp.zeros_like(p_odd)

        @pl.when(i == 0)
        def _():
            zeros = jnp.zeros((kvb, BLOCK, HEAD_DIM), BF16)
            zeros_t = jnp.zeros((kvb, HEAD_DIM + VT_PAD, BLOCK), BF16)
            for ref in (kb_ref, vb_ref):
                ref[0, :, 0:BLOCK, :] = zeros
                ref[0, :, sp - BLOCK:sp, :] = zeros
            kbt_ref[0, :, :, 0:BLOCK] = zeros_t[:, 0:HEAD_DIM]
            kbt_ref[0, :, :, sp - BLOCK:sp] = zeros_t[:, 0:HEAD_DIM]
            vbt_ref[0, :, :, 0:BLOCK] = zeros_t
            vbt_ref[0, :, :, sp - BLOCK:sp] = zeros_t

        even = lax.rem(s, 2) == 0
        pl.when(even)(functools.partial(tile_work, p_even, p_odd, i, *refs[:-2]))
        pl.when(jnp.logical_not(even))(functools.partial(tile_work, p_odd, p_even, i, *refs[:-2]))

    def tile_work(p_new, p_ref, i, x_ref, g1_ref, w_ref, cos_ref, sin_ref, gq_ref, gk_ref, h_ref, po_ref, qa_ref,
                  ka_ref, kat_ref, va_ref, vat_ref, qb_ref, kb_ref, kbt_ref, vb_ref, vbt_ref):
        y, _, _ = _rms_fwd(x_ref[...], g1_ref[...])
        h = y.astype(BF16)
        h_ref[...] = h
        n_parts = 6
        pw = p_cols // n_parts

        def project(c):
            p_new[:, c * pw:(c + 1) * pw] = _dot_nt(h, w_ref[c * pw:(c + 1) * pw, :])

        cs, sn = cos_ref[...], sin_ref[...]
        low, first = _pair_masks(ts)
        ones_row = (lax.broadcasted_iota(jnp.int32, (VT_PAD, ts), 0) == 0).astype(BF16)
        heads = (slice(0, HEAD_DIM), slice(HEAD_DIM, PAIR))

        def pair(p):
            v = p_ref[:, p * PAIR:(p + 1) * PAIR]
            po_ref[:, p * PAIR:(p + 1) * PAIR] = v
            return v

        def normrope(x, g):
            y = x * lax.rsqrt(_pair_mean(x * x, low) + EPS) * g
            return y * cs + _pair_partner(y, first) * sn

        eye = (lax.broadcasted_iota(jnp.int32, (PAIR, PAIR), 0)
               == lax.broadcasted_iota(jnp.int32, (PAIR, PAIR), 1)).astype(BF16)

        def transposed(xb):
            return _dot_nt(eye, xb).astype(BF16)

        def prep_qa(p):
            qa_ref[:, p * PAIR:(p + 1) * PAIR] = (normrope(pair(p), gq_ref[...]) * SCALE).astype(BF16)

        def prep_kva(p):
            base = ha // 2
            k = normrope(pair(base + p), gk_ref[...]).astype(BF16)
            v = pair(base + kva // 2 + p).astype(BF16)
            kt, vt = transposed(k), transposed(v)
            for e, lanes in enumerate(heads):
                ka_ref[0, 2 * p + e] = k[:, lanes]
                va_ref[0, 2 * p + e] = v[:, lanes]
                kat_ref[0, 2 * p + e] = kt[lanes, :]
                vat_ref[0, 2 * p + e, 0:HEAD_DIM, :] = vt[lanes, :]
                vat_ref[0, 2 * p + e, HEAD_DIM:HEAD_DIM + VT_PAD, :] = ones_row

        def prep_qb(p):
            base = ha // 2 + kva
            qb_ref[:, p * PAIR:(p + 1) * PAIR] = (pair(base + p) * SCALE).astype(BF16)

        rows = pl.ds(pl.multiple_of(BLOCK + i * ts, BLOCK), ts)

        def prep_kvb(p):
            base = ha // 2 + kva + hb // 2
            k = pair(base + p).astype(BF16)
            v = pair(base + kvb // 2 + p).astype(BF16)
            kt, vt = transposed(k), transposed(v)
            for e, lanes in enumerate(heads):
                kb_ref[0, 2 * p + e, rows, :] = k[:, lanes]
                vb_ref[0, 2 * p + e, rows, :] = v[:, lanes]
                kbt_ref[0, 2 * p + e, :, rows] = kt[lanes, :]
                vbt_ref[0, 2 * p + e, 0:HEAD_DIM, rows] = vt[lanes, :]
                vbt_ref[0, 2 * p + e, HEAD_DIM:HEAD_DIM + VT_PAD, rows] = ones_row

        work = ([functools.partial(prep_qa, p) for p in range(ha // 2)]
                + [functools.partial(prep_kva, p) for p in range(kva // 2)]
                + [functools.partial(prep_qb, p) for p in range(hb // 2)]
                + [functools.partial(prep_kvb, p) for p in range(kvb // 2)])
        per_part = -(-len(work) // n_parts)
        for c in range(n_parts):
            for item in work[c * per_part:(c + 1) * per_part]:
                item()
            project(c)

    def cur(s):
        return jnp.minimum(s, nt - 1)

    def prev(s):
        return jnp.maximum(s - 1, 0) // ns, lax.rem(jnp.maximum(s - 1, 0), ns)

    def hm(nh):
        return pl.BlockSpec((1, nh, ts, HEAD_DIM), lambda s: (prev(s)[0], 0, prev(s)[1], 0))

    def hm_t(nh, rows):
        return pl.BlockSpec((1, nh, rows, ts), lambda s: (prev(s)[0], 0, 0, prev(s)[1]))

    def tokmajor(nh):
        return pl.BlockSpec((ts, nh * HEAD_DIM), lambda s: (jnp.maximum(s - 1, 0), 0))

    def padded(nh):
        return pl.BlockSpec((1, nh, sp, HEAD_DIM), lambda s: (prev(s)[0], 0, 0, 0))

    def padded_t(nh, rows):
        return pl.BlockSpec((1, nh, rows, sp), lambda s: (prev(s)[0], 0, 0, 0))

    tab = pl.BlockSpec((ts, PAIR), lambda s: (prev(s)[1], 0))
    vec = pl.BlockSpec((1, PAIR), lambda s: (0, 0))
    return pl.pallas_call(
        body,
        grid=(nt + 1,),
        in_specs=[pl.BlockSpec((ts, d), lambda s: (cur(s), 0)),
                  pl.BlockSpec((1, d), lambda s: (0, 0)),
                  pl.BlockSpec((p_cols, d), lambda s: (0, 0)),
                  tab, tab, vec, vec],
        out_specs=[pl.BlockSpec((ts, d), lambda s: (cur(s), 0)), tokmajor(p_cols // HEAD_DIM),
                   tokmajor(ha), hm(kva), hm_t(kva, HEAD_DIM), hm(kva), hm_t(kva, HEAD_DIM + VT_PAD),
                   tokmajor(hb), padded(kvb), padded_t(kvb, HEAD_DIM), padded(kvb),
                   padded_t(kvb, HEAD_DIM + VT_PAD)],
        out_shape=[SDS((t, d), BF16), SDS((t, p_cols), F32),
                   SDS((t, ha * HEAD_DIM), BF16), SDS((bl, kva, s_len, HEAD_DIM), BF16),
                   SDS((bl, kva, HEAD_DIM, s_len), BF16),
                   SDS((bl, kva, s_len, HEAD_DIM), BF16), SDS((bl, kva, HEAD_DIM + VT_PAD, s_len), BF16),
                   SDS((t, hb * HEAD_DIM), BF16),
                   SDS((bl, kvb, sp, HEAD_DIM), BF16), SDS((bl, kvb, HEAD_DIM, sp), BF16),
                   SDS((bl, kvb, sp, HEAD_DIM), BF16), SDS((bl, kvb, HEAD_DIM + VT_PAD, sp), BF16)],
        scratch_shapes=[pltpu.VMEM((ts, p_cols), F32)] * 2,
        compiler_params=_cp("arbitrary"),
        name="inproj_qkprep",
    )(x2, g1, w_t, cos, sin_signed, gq, gk)


def _attn_a_fwd(qa, ka, vat, tq, tk, shards):
    bl, kv, s_len, _ = ka.shape
    ha = qa.shape[1] // HEAD_DIM
    va_rows = vat.shape[2]
    nq, nk = s_len // tq, s_len // tk
    assert nk % 2 == 0
    r = GROUP * tq
    ns = len(shards)

    def body(q_ref, qn_ref, k_ref, v_ref, *rest):
        shard_refs, (o_ref, l_ref), gathered = rest[:ns], rest[ns:ns + 2], rest[ns + 2:2 * ns + 2]
        st_sc, send_sems, recv_sems, local_sems = rest[2 * ns + 2:]
        i = pl.program_id(2)
        step_id = (pl.program_id(0) * kv + pl.program_id(1)) * nq + i
        start, wait = _direct_exchange("gather", shard_refs, gathered, send_sems, recv_sems, local_sems)
        pl.when(step_id == 0)(start)

        q = _heads_t(q_ref[...]).astype(BF16)

        def scores(c, qv):
            return _dot(k_ref[0, 0, pl.ds(pl.multiple_of(c * tk, tk), tk), :], qv)

        def fold(st, c, carry):
            m_old, acc = carry
            m_new = jnp.maximum(m_old, jnp.max(st, axis=0, keepdims=True))
            pt = jnp.exp(st - m_new).astype(BF16)
            vt = v_ref[0, 0, :, pl.ds(pl.multiple_of(c * tk, tk), tk)]
            return m_new, jnp.exp(m_old - m_new) * acc + _dot(vt, pt)

        @pl.when(i == 0)
        def _():
            st_sc[0] = scores(0, q)

        def step(c2, carry):
            c = 2 * c2
            st_sc[1] = scores(c + 1, q)
            carry = fold(st_sc[0], c, carry)
            st_sc[0] = scores(c + 2, q)
            return fold(st_sc[1], c + 1, carry)

        carry = (jnp.full((1, r), -jnp.inf, F32), jnp.zeros((va_rows, r), F32))
        for c2 in range(nk // 2 - 1):
            carry = step(c2, carry)
        st_sc[1] = scores(nk - 1, q)
        carry = fold(st_sc[0], nk - 2, carry)
        st_sc[0] = scores(0, _heads_t(qn_ref[...]).astype(BF16))
        m, acc = fold(st_sc[1], nk - 1, carry)
        l = acc[HEAD_DIM:HEAD_DIM + 1, :]
        o_ref[...] = _heads_t_inv(acc[0:HEAD_DIM, :] / l).astype(BF16)
        l_ref[0, 0, 0] = jnp.broadcast_to(m + jnp.log(l), (8, r))
        pl.when(step_id == bl * kv * nq - 1)(wait)

    anyspec = pl.BlockSpec(memory_space=pl.ANY)
    res = pl.pallas_call(
        body,
        grid=(bl, kv, nq),
        in_specs=[pl.BlockSpec((tq, GROUP * HEAD_DIM), lambda b, g, i: (b * nq + i, g)),
                  pl.BlockSpec((tq, GROUP * HEAD_DIM), lambda b, g, i: (b * nq + jnp.minimum(i + 1, nq - 1), g)),
                  pl.BlockSpec((1, 1, s_len, HEAD_DIM), lambda b, g, i: (b, g, 0, 0)),
                  pl.BlockSpec((1, 1, va_rows, s_len), lambda b, g, i: (b, g, 0, 0))] + [anyspec] * ns,
        out_specs=[pl.BlockSpec((tq, GROUP * HEAD_DIM), lambda b, g, i: (b * nq + i, g)),
                   pl.BlockSpec((1, 1, 1, 8, r), lambda b, g, i: (b, g, i, 0, 0))] + [anyspec] * ns,
        out_shape=[SDS((bl * s_len, ha * HEAD_DIM), BF16), SDS((bl, kv, nq, 8, r), F32)]
        + [SDS((N_DEV,) + s.shape, s.dtype) for s in shards],
        scratch_shapes=[pltpu.VMEM((2, tk, r), F32)] + _exchange_scratch(ns),
        compiler_params=_cp("arbitrary", "arbitrary", "arbitrary"),
        name="attn_a_fwd",
    )(qa, qa, ka, vat, *shards)
    return res[0], res[1], res[2:]


FFN_BWD_TOKENS = 256
QB_PER_STEP = 16


def _bias_variant(n, nb):
    return jnp.where(n == 0, 1, jnp.where(n == nb - 1, 2, 0))


def _sink_row(sink_ref, g):
    return jnp.concatenate([jnp.full((1, BLOCK), sink_ref[0, g * GROUP + h], F32) for h in range(GROUP)], axis=1)


def _attn_b_fwd(qb, kb, vbt, bias_t, sink, s_len):
    bl, kv, sp, _ = kb.shape
    hb = qb.shape[1] // HEAD_DIM
    vt_rows = vbt.shape[2]
    nb = s_len // BLOCK
    nbs = min(QB_PER_STEP, nb)
    r = GROUP * BLOCK

    def body(q_ref, k_ref, vt_ref, bt_ref, sink_ref, o_ref, l_ref, st_sc, pb_sc):
        g, n0 = pl.program_id(1), pl.program_id(2) * nbs
        sink_row = _sink_row(sink_ref, g)

        def span(j):
            return pl.ds(pl.multiple_of((n0 + j) * BLOCK, BLOCK), SPAN)

        for j in range(nbs):
            qt = _heads_t(q_ref[j * BLOCK:(j + 1) * BLOCK, :]).astype(BF16)
            st_sc[j] = _dot(k_ref[0, 0, span(j), :], qt) + bt_ref[_bias_variant(n0 + j, nb), 0]
        maxes = []
        for j in range(nbs):
            st = st_sc[j]
            m = jnp.maximum(jnp.max(st, axis=0, keepdims=True), sink_row)
            pb_sc[j] = jnp.exp(st - m).astype(BF16)
            maxes.append(m)
        for j in range(nbs):
            m = maxes[j]
            acc = _dot(vt_ref[0, 0, :, span(j)], pb_sc[j])
            l = acc[HEAD_DIM:HEAD_DIM + 1, :] + jnp.exp(sink_row - m)
            o_ref[j * BLOCK:(j + 1) * BLOCK, :] = _heads_t_inv(acc[0:HEAD_DIM, :] / l).astype(BF16)
            l_ref[0, 0, j] = jnp.broadcast_to(m + jnp.log(l), (8, r))

    return pl.pallas_call(
        body,
        grid=(bl, kv, nb // nbs),
        in_specs=[pl.BlockSpec((nbs * BLOCK, GROUP * HEAD_DIM), lambda b, g, n: (b * (nb // nbs) + n, g)),
                  pl.BlockSpec((1, 1, sp, HEAD_DIM), lambda b, g, n: (b, g, 0, 0)),
                  pl.BlockSpec((1, 1, vt_rows, sp), lambda b, g, n: (b, g, 0, 0)),
                  pl.BlockSpec((3, 1, SPAN, r), lambda b, g, n: (0, g, 0, 0)),
                  pl.BlockSpec(memory_space=pltpu.SMEM)],
        out_specs=[pl.BlockSpec((nbs * BLOCK, GROUP * HEAD_DIM), lambda b, g, n: (b * (nb // nbs) + n, g)),
                   pl.BlockSpec((1, 1, nbs, 8, r), lambda b, g, n: (b, g, n, 0, 0))],
        out_shape=[SDS((bl * s_len, hb * HEAD_DIM), BF16), SDS((bl, kv, nb, 8, r), F32)],
        scratch_shapes=[pltpu.VMEM((nbs, SPAN, r), F32), pltpu.VMEM((nbs, SPAN, r), BF16)],
        compiler_params=_cp("parallel", "parallel", "arbitrary"),
        name="attn_b_fwd",
    )(qb, kb, vbt, bias_t, sink)


def _mixout(oa, ob, wo, x2, g2, g3, tm):
    t, d = x2.shape
    ca = oa.shape[1]

    def body(oa_ref, ob_ref, w_ref, x_ref, g2_ref, g3_ref, mix_ref, x1_ref, h2_ref):
        mix = _dot(oa_ref[...], w_ref[0:ca, :]) + _dot(ob_ref[...], w_ref[ca:, :])
        mix_ref[...] = mix
        y2, _, _ = _rms_fwd(mix, g2_ref[...])
        x1 = x_ref[...] + y2
        x1_ref[...] = x1
        y3, _, _ = _rms_fwd(x1, g3_ref[...])
        h2_ref[...] = y3.astype(BF16)

    tile = lambda w: pl.BlockSpec((tm, w), lambda i: (i, 0))
    vec = pl.BlockSpec((1, d), lambda i: (0, 0))
    return pl.pallas_call(
        body,
        grid=(t // tm,),
        in_specs=[tile(ca), tile(ob.shape[1]), pl.BlockSpec(wo.shape, lambda i: (0, 0)), tile(d), vec, vec],
        out_specs=[tile(d), tile(d), tile(d)],
        out_shape=[SDS((t, d), F32), SDS((t, d), F32), SDS((t, d), BF16)],
        compiler_params=_cp("parallel"),
        name="mixout",
    )(oa, ob, wo, x2, g2, g3)


def _ffn_fwd(h2, wup_g, wdn, x1, target, g4, tm):
    t, d = x1.shape
    nblk, _, tf = wup_g.shape
    ff = nblk * tf
    nt = t // tm

    def body(h_ref, wu_ref, wd_ref, x1_ref, tg_ref, g_ref, u_ref, df_ref, dy_ref, dg_ref, loss_ref, f_sc):
        s = pl.program_id(0)

        @pl.when(s == 0)
        def _():
            f_sc[...] = jnp.zeros_like(f_sc)
            dg_ref[...] = jnp.zeros_like(dg_ref)
            loss_ref[...] = jnp.zeros_like(loss_ref)

        counted = (s > 0).astype(F32)

        def loss_part():
            g = g_ref[...]
            y4, n, r = _rms_fwd(f_sc[...], g)
            e = (x1_ref[...] + y4) - tg_ref[...]
            loss_ref[...] += jnp.sum(e * e) * (0.5 / d) * counted
            dy = e * (1.0 / d)
            dy_ref[...] = dy
            return n, r, g, dy

        def norm_bwd_part(n, r, g, dy):
            df, dgt = _rms_bwd(n, r, g, dy)
            df_ref[...] = df.astype(BF16)
            dg_ref[0:1, :] += jnp.sum(dgt, axis=0, keepdims=True) * counted

        @pl.when(s < nt)
        def _():
            h = h_ref[...]
            squares = []
            saved = None
            for c in range(nblk):
                u = jnp.maximum(_dot(h, wu_ref[c]), 0.0)
                u_ref[:, c * tf:(c + 1) * tf] = u.astype(BF16)
                squares.append((u * u).astype(BF16))
                if c == 0:
                    saved = loss_part()
                elif c == 1:
                    norm_bwd_part(*saved)
            f_sc[...] = _dot(jnp.concatenate(squares, axis=1), wd_ref[...])

        @pl.when(s == nt)
        def _():
            norm_bwd_part(*loss_part())

    cur = lambda s: (jnp.minimum(s, nt - 1), 0)
    prev = lambda s: (jnp.maximum(s - 1, 0), 0)
    return pl.pallas_call(
        body,
        grid=(nt + 1,),
        in_specs=[pl.BlockSpec((tm, d), cur),
                  pl.BlockSpec((nblk, d, tf), lambda s: (0, 0, 0)),
                  pl.BlockSpec((ff, d), lambda s: (0, 0)),
                  pl.BlockSpec((tm, d), prev), pl.BlockSpec((tm, d), prev),
                  pl.BlockSpec((1, d), lambda s: (0, 0))],
        out_specs=[pl.BlockSpec((tm, ff), cur), pl.BlockSpec((tm, d), prev), pl.BlockSpec((tm, d), prev),
                   pl.BlockSpec((8, d), lambda s: (0, 0)),
                   pl.BlockSpec((8, 128), lambda s: (0, 0))],
        out_shape=[SDS((t, ff), BF16), SDS((t, d), BF16), SDS((t, d), F32), SDS((8, d), F32), SDS((8, 128), F32)],
        scratch_shapes=[pltpu.VMEM((tm, d), F32)],
        compiler_params=_cp("arbitrary"),
        name="ffn_fwd",
    )(h2, wup_g, wdn, x1, target, g4)


def _ffn_bwd(df, u, wdn, wup_g, x1, dy, mix, g3, g2, tm):
    t, d = x1.shape
    nblk, _, tf = wup_g.shape
    ff = nblk * tf
    nt = t // tm

    def body(df_ref, u_ref, wd_ref, wu_ref, x1_ref, dy_ref, mix_ref, g3_ref, g2_ref,
             dpre_ref, dx1_ref, dmix_ref, dg3_ref, dg2_ref, dh_sc):
        s = pl.program_id(0)

        @pl.when(s == 0)
        def _():
            dh_sc[...] = jnp.zeros_like(dh_sc)
            dg3_ref[...] = jnp.zeros_like(dg3_ref)
            dg2_ref[...] = jnp.zeros_like(dg2_ref)

        counted = (s > 0).astype(F32)

        def residual_norm_part():
            g3 = g3_ref[...]
            _, n3, r3 = _rms_fwd(x1_ref[...], g3)
            dx, dgt3 = _rms_bwd(n3, r3, g3, dh_sc[...])
            dx1 = dy_ref[...] + dx
            dx1_ref[...] = dx1
            dg3_ref[0:1, :] += jnp.sum(dgt3, axis=0, keepdims=True) * counted

        def mix_norm_part():
            g2 = g2_ref[...]
            _, n2, r2 = _rms_fwd(mix_ref[...], g2)
            dmix, dgt2 = _rms_bwd(n2, r2, g2, dx1_ref[...])
            dmix_ref[...] = dmix.astype(BF16)
            dg2_ref[0:1, :] += jnp.sum(dgt2, axis=0, keepdims=True) * counted

        @pl.when(s < nt)
        def _():
            du2 = _dot_nt(df_ref[...], wd_ref[...])
            dpre = (2.0 * u_ref[...].astype(F32) * du2).astype(BF16)
            dpre_ref[...] = dpre
            dh = _dot_nt(dpre[:, 0:tf], wu_ref[0])
            for c in range(1, nblk):
                if c == 1:
                    residual_norm_part()
                elif c == 3:
                    mix_norm_part()
                dh = dh + _dot_nt(dpre[:, c * tf:(c + 1) * tf], wu_ref[c])
            dh_sc[...] = dh

        @pl.when(s == nt)
        def _():
            residual_norm_part()
            mix_norm_part()

    cur = lambda s: (jnp.minimum(s, nt - 1), 0)
    prev = lambda s: (jnp.maximum(s - 1, 0), 0)
    vec = pl.BlockSpec((1, d), lambda s: (0, 0))
    acc8 = pl.BlockSpec((8, d), lambda s: (0, 0))
    return pl.pallas_call(
        body,
        grid=(nt + 1,),
        in_specs=[pl.BlockSpec((tm, d), cur),
                  pl.BlockSpec((tm, ff), cur),
                  pl.BlockSpec((ff, d), lambda s: (0, 0)),
                  pl.BlockSpec((nblk, d, tf), lambda s: (0, 0, 0)),
                  pl.BlockSpec((tm, d), prev), pl.BlockSpec((tm, d), prev), pl.BlockSpec((tm, d), prev), vec, vec],
        out_specs=[pl.BlockSpec((tm, ff), cur), pl.BlockSpec((tm, d), prev), pl.BlockSpec((tm, d), prev),
                   acc8, acc8],
        out_shape=[SDS(u.shape, BF16), SDS((t, d), F32), SDS((t, d), BF16), SDS((8, d), F32), SDS((8, d), F32)],
        scratch_shapes=[pltpu.VMEM((tm, d), F32)],
        compiler_params=_cp("arbitrary"),
        name="ffn_bwd",
    )(df, u, wdn, wup_g, x1, dy, mix, g3, g2)


def _wgrad(a, b, a_spec, b_spec, out_block, out_shape, nj, nk, name, prep_a=None, prep_b=None):
    acc_shape = out_block[1:]

    def body(a_ref, b_ref, o_ref, acc_sc):
        k = pl.program_id(1)
        av = a_ref[...] if prep_a is None else prep_a(a_ref)
        bv = b_ref[...] if prep_b is None else prep_b(b_ref)
        part = _dot_tn(av, bv)

        @pl.when(k == 0)
        def _():
            acc_sc[...] = part

        @pl.when(k > 0)
        def _():
            acc_sc[...] += part

        @pl.when(k == nk - 1)
        def _():
            o_ref[0] = acc_sc[...].astype(BF16)

    return pl.pallas_call(
        body,
        grid=(nj, nk),
        in_specs=[a_spec, b_spec],
        out_specs=pl.BlockSpec(out_block, lambda j, k: (j, 0, 0)),
        out_shape=SDS(out_shape, BF16),
        scratch_shapes=[pltpu.VMEM(acc_shape, F32)],
        compiler_params=_cp("parallel", "arbitrary"),
        name=name,
    )(a, b)


def _wgrad_cols(a, b, nj, tt, name):
    t, m = a.shape
    bn = b.shape[1] // nj
    return _wgrad(a, b, pl.BlockSpec((tt, m), lambda j, k: (k, 0)), pl.BlockSpec((tt, bn), lambda j, k: (k, j)),
                  (1, m, bn), (nj, m, bn), nj, t // tt, name)


def _wgrad_rows(a, b, nj, tt, name, square=False):
    t, n = b.shape
    bm = a.shape[1] // nj

    def squared(a_ref):
        af = a_ref[...].astype(F32)
        return (af * af).astype(BF16)

    return _wgrad(a, b, pl.BlockSpec((tt, bm), lambda j, k: (k, j)), pl.BlockSpec((tt, n), lambda j, k: (k, 0)),
                  (1, bm, n), (nj, bm, n), nj, t // tt, name, prep_a=squared if square else None)


def _wgrad_o(oa, ob, dmix, nj, tt):
    t, n = dmix.shape
    ca, cb = oa.shape[1], ob.shape[1]
    m = ca + cb
    nk = t // tt

    def body(oa_ref, ob_ref, b_ref, o_ref, acc_sc):
        k = pl.program_id(0)
        part = _dot_tn(jnp.concatenate([oa_ref[...], ob_ref[...]], axis=1), b_ref[...])

        @pl.when(k == 0)
        def _():
            acc_sc[...] = part

        @pl.when(k > 0)
        def _():
            acc_sc[...] += part

        @pl.when(k == nk - 1)
        def _():
            o_ref[...] = acc_sc[...].reshape(nj, m // nj, n).astype(BF16)

    return pl.pallas_call(
        body,
        grid=(nk,),
        in_specs=[pl.BlockSpec((tt, ca), lambda k: (k, 0)), pl.BlockSpec((tt, cb), lambda k: (k, 0)),
                  pl.BlockSpec((tt, n), lambda k: (k, 0))],
        out_specs=pl.BlockSpec((nj, m // nj, n), lambda k: (0, 0, 0)),
        out_shape=SDS((nj, m // nj, n), BF16),
        scratch_shapes=[pltpu.VMEM((m, n), F32)],
        compiler_params=_cp("arbitrary"),
        name="wgrad_o",
    )(oa, ob, dmix)


def _attn_out_bwd(dmix, wo, ca, tm):
    t, d = dmix.shape
    cb = wo.shape[0] - ca

    def body(dm_ref, w_ref, da_ref, db_ref):
        dm = dm_ref[...]
        da_ref[...] = _dot_nt(dm, w_ref[0:ca, :]).astype(BF16)
        db_ref[...] = _dot_nt(dm, w_ref[ca:, :]).astype(BF16)

    return pl.pallas_call(
        body,
        grid=(t // tm,),
        in_specs=[pl.BlockSpec((tm, d), lambda i: (i, 0)), pl.BlockSpec(wo.shape, lambda i: (0, 0))],
        out_specs=[pl.BlockSpec((tm, ca), lambda i: (i, 0)), pl.BlockSpec((tm, cb), lambda i: (i, 0))],
        out_shape=[SDS((t, ca), BF16), SDS((t, cb), BF16)],
        compiler_params=_cp("parallel"),
        name="attn_out_bwd",
    )(dmix, wo)


def _heads_t(x):
    xt = x.astype(F32).T
    return jnp.concatenate([xt[h * HEAD_DIM:(h + 1) * HEAD_DIM, :] for h in range(GROUP)], axis=1)


def _heads_t_inv(yt):
    n = yt.shape[1] // GROUP
    return jnp.concatenate([yt[:, h * n:(h + 1) * n] for h in range(GROUP)], axis=0).T


def _attn_a_bwd(qa, ka, kat, va, do, o, lse, tq, tk, grads):
    bl, kv, s_len, _ = ka.shape
    nq, nk = s_len // tq, s_len // tk
    assert nk % 2 == 0
    r = GROUP * tq
    ng = len(grads)

    def body(q_ref, qn_ref, k_ref, kt_ref, v_ref, do_ref, don_ref, o_ref, l_ref, *rest):
        grad_refs, (dq_ref, dk_ref, dv_ref), parts = rest[:ng], rest[ng:ng + 3], rest[ng + 3:2 * ng + 3]
        st_sc, dp_sc, dkt_sc, dvt_sc, send_sems, recv_sems, local_sems = rest[2 * ng + 3:]
        i = pl.program_id(2)
        step_id = (pl.program_id(0) * kv + pl.program_id(1)) * nq + i
        start, wait = _direct_exchange("scatter", grad_refs, parts, send_sems, recv_sems, local_sems)
        pl.when(step_id == 0)(start)

        dot32 = _heads_t(do_ref[...])
        drow = jnp.sum(dot32 * _heads_t(o_ref[...]), axis=0, keepdims=True)
        qt, dot = _heads_t(q_ref[...]).astype(BF16), dot32.astype(BF16)
        lrow = l_ref[0, 0, 0, 0:1, :]

        @pl.when(i == 0)
        def _():
            dkt_sc[...] = jnp.zeros_like(dkt_sc)
            dvt_sc[...] = jnp.zeros_like(dvt_sc)

        def chunk(c):
            return pl.ds(pl.multiple_of(c * tk, tk), tk)

        def scores(c, slot, qv=qt, dov=dot):
            st_sc[slot] = _dot(k_ref[0, 0, chunk(c), :], qv)
            dp_sc[slot] = _dot(v_ref[0, 0, chunk(c), :], dov)

        def fold(slot, c, dqt):
            pt = jnp.exp(st_sc[slot] - lrow)
            dsb = (pt * (dp_sc[slot] - drow)).astype(BF16)
            dvt_sc[:, chunk(c)] += _dot_nt(dot, pt.astype(BF16))
            dkt_sc[:, chunk(c)] += _dot_nt(qt, dsb)
            return dqt + _dot(kt_ref[0, 0, :, chunk(c)], dsb)

        @pl.when(i == 0)
        def _():
            scores(0, 0)

        def step(c2, dqt):
            c = 2 * c2
            scores(c + 1, 1)
            dqt = fold(0, c, dqt)
            scores(c + 2, 0)
            return fold(1, c + 1, dqt)

        dqt = jnp.zeros((HEAD_DIM, r), F32)
        for c2 in range(nk // 2 - 1):
            dqt = step(c2, dqt)
        scores(nk - 1, 1)
        dqt = fold(0, nk - 2, dqt)
        scores(0, 0, _heads_t(qn_ref[...]).astype(BF16), _heads_t(don_ref[...]).astype(BF16))
        dq_ref[...] = _heads_t_inv(fold(1, nk - 1, dqt))

        @pl.when(i == nq - 1)
        def _():
            dk_ref[0, 0] = dkt_sc[...].T
            dv_ref[0, 0] = dvt_sc[...].T

        pl.when(step_id == bl * kv * nq - 1)(wait)

    kvspec = pl.BlockSpec((1, 1, s_len, HEAD_DIM), lambda b, g, i: (b, g, 0, 0))
    tok = pl.BlockSpec((tq, GROUP * HEAD_DIM), lambda b, g, i: (b * nq + i, g))
    toknext = pl.BlockSpec((tq, GROUP * HEAD_DIM), lambda b, g, i: (b * nq + jnp.minimum(i + 1, nq - 1), g))
    anyspec = pl.BlockSpec(memory_space=pl.ANY)
    res = pl.pallas_call(
        body,
        grid=(bl, kv, nq),
        in_specs=[tok, toknext, kvspec, pl.BlockSpec((1, 1, HEAD_DIM, s_len), lambda b, g, i: (b, g, 0, 0)), kvspec,
                  tok, toknext, tok, pl.BlockSpec((1, 1, 1, 8, r), lambda b, g, i: (b, g, i, 0, 0))] + [anyspec] * ng,
        out_specs=[tok, kvspec, kvspec] + [anyspec] * ng,
        out_shape=[SDS(qa.shape, F32), SDS(ka.shape, F32), SDS(va.shape, F32)]
        + [SDS(g.shape, g.dtype) for g in grads],
        scratch_shapes=[pltpu.VMEM((2, tk, r), F32), pltpu.VMEM((2, tk, r), F32),
                        pltpu.VMEM((HEAD_DIM, s_len), F32), pltpu.VMEM((HEAD_DIM, s_len), F32)]
        + _exchange_scratch(ng),
        compiler_params=_cp("arbitrary", "arbitrary", "arbitrary"),
        name="attn_a_bwd",
    )(qa, qa, ka, kat, va, do, do, o, lse, *grads)
    return res[0], res[1], res[2], res[3:]


def _attn_b_bwd(qb, kb, kbt, vb, do, o, lse, bias_t, sink, s_len):
    bl, kv, sp, _ = kb.shape
    nb = s_len // BLOCK
    nbs = min(QB_PER_STEP, nb)
    r = GROUP * BLOCK

    def body(q_ref, k_ref, kt_ref, v_ref, do_ref, o_ref, l_ref, bt_ref, sink_ref,
             dq_ref, dk_ref, dv_ref, dsum_ref, dsink_ref, dkt_sc, dvt_sc):
        g, b, ns = pl.program_id(0), pl.program_id(1), pl.program_id(2)
        sink_row = _sink_row(sink_ref, g)

        @pl.when(ns == 0)
        def _():
            dkt_sc[...] = jnp.zeros_like(dkt_sc)
            dvt_sc[...] = jnp.zeros_like(dvt_sc)

        @pl.when((b == 0) & (ns == 0))
        def _():
            dsum_ref[...] = jnp.zeros_like(dsum_ref)
            dsink_ref[...] = jnp.zeros_like(dsink_ref)

        dsum = jnp.zeros((SPAN, r), F32)
        dsink = jnp.zeros((1, r), F32)
        for j in range(nbs):
            n = ns * nbs + j
            span = pl.ds(pl.multiple_of(n * BLOCK, BLOCK), SPAN)
            rows = slice(j * BLOCK, (j + 1) * BLOCK)
            dot32 = _heads_t(do_ref[rows, :])
            drow = jnp.sum(dot32 * _heads_t(o_ref[rows, :]), axis=0, keepdims=True)
            qt, dot = _heads_t(q_ref[rows, :]).astype(BF16), dot32.astype(BF16)
            lrow = l_ref[0, 0, j, 0:1, :]
            st = _dot(k_ref[0, 0, span, :], qt) + bt_ref[_bias_variant(n, nb), 0]
            pt = jnp.exp(st - lrow)
            dst = pt * (_dot(v_ref[0, 0, span, :], dot) - drow)
            dsum = dsum + dst
            dsink = dsink - jnp.exp(sink_row - lrow) * drow
            dsb = dst.astype(BF16)
            dvt_sc[:, span] += _dot_nt(dot, pt.astype(BF16))
            dkt_sc[:, span] += _dot_nt(qt, dsb)
            dq_ref[rows, :] = _heads_t_inv(_dot(kt_ref[0, 0, :, span], dsb))
        dsum_ref[0] += dsum
        dsink_ref[0, 0:1, :] += dsink

        @pl.when(ns == nb // nbs - 1)
        def _():
            dk_ref[0, 0] = dkt_sc[:, BLOCK:BLOCK + s_len].T
            dv_ref[0, 0] = dvt_sc[:, BLOCK:BLOCK + s_len].T

    kvspec = pl.BlockSpec((1, 1, sp, HEAD_DIM), lambda g, b, n: (b, g, 0, 0))
    kvout = pl.BlockSpec((1, 1, s_len, HEAD_DIM), lambda g, b, n: (b, g, 0, 0))
    tok = pl.BlockSpec((nbs * BLOCK, GROUP * HEAD_DIM), lambda g, b, n: (b * (nb // nbs) + n, g))
    return pl.pallas_call(
        body,
        grid=(kv, bl, nb // nbs),
        in_specs=[tok, kvspec, pl.BlockSpec((1, 1, HEAD_DIM, sp), lambda g, b, n: (b, g, 0, 0)), kvspec, tok, tok,
                  pl.BlockSpec((1, 1, nbs, 8, r), lambda g, b, n: (b, g, n, 0, 0)),
                  pl.BlockSpec((3, 1, SPAN, r), lambda g, b, n: (0, g, 0, 0)),
                  pl.BlockSpec(memory_space=pltpu.SMEM)],
        out_specs=[tok, kvout, kvout,
                   pl.BlockSpec((1, SPAN, r), lambda g, b, n: (g, 0, 0)),
                   pl.BlockSpec((1, 8, r), lambda g, b, n: (g, 0, 0))],
        out_shape=[SDS(qb.shape, F32), SDS((bl, kv, s_len, HEAD_DIM), F32), SDS((bl, kv, s_len, HEAD_DIM), F32),
                   SDS((kv, SPAN, r), F32), SDS((kv, 8, r), F32)],
        scratch_shapes=[pltpu.VMEM((HEAD_DIM, sp), F32), pltpu.VMEM((HEAD_DIM, sp), F32)],
        compiler_params=_cp("arbitrary", "arbitrary", "arbitrary"),
        name="attn_b_bwd",
    )(qb, kb, kbt, vb, do, o, lse, bias_t, sink)


def _bias_reduce(dsum, dsink, bucket_t4):
    kv, _, r = dsum.shape

    def body(ds_ref, dk_ref, bk_ref, rel_ref, sink_ref):
        lane = lax.broadcasted_iota(jnp.int32, (N_BUCKETS, 128), 1)
        lane8 = lax.broadcasted_iota(jnp.int32, (8, 128), 1)
        bk = bk_ref[...]
        for g in range(kv):
            ds = ds_ref[g]
            rowi = lax.broadcasted_iota(jnp.int32, (N_BUCKETS, r), 0)
            red = jnp.zeros((N_BUCKETS, r), F32)
            for b in range(N_BUCKETS):
                red = jnp.where(rowi == b, jnp.sum(jnp.where(bk == b, ds, 0.0), axis=0, keepdims=True), red)
            out = jnp.zeros((N_BUCKETS, 128), F32)
            so = jnp.zeros((8, 128), F32)
            for h in range(GROUP):
                col = jnp.sum(red[:, h * BLOCK:(h + 1) * BLOCK], axis=1, keepdims=True)
                out = jnp.where(lane == h, col, out)
                sc = jnp.sum(dk_ref[g][:, h * BLOCK:(h + 1) * BLOCK], axis=1, keepdims=True)
                so = jnp.where(lane8 == h, sc, so)
            rel_ref[g] = out
            sink_ref[g] = so

    vm = pl.BlockSpec(memory_space=pltpu.VMEM)
    return pl.pallas_call(
        body,
        in_specs=[vm, vm, vm],
        out_specs=[vm, vm],
        out_shape=[SDS((kv, N_BUCKETS, 128), F32), SDS((kv, 8, 128), F32)],
        name="bias_reduce",
    )(dsum, dsink, bucket_t4)


def _dqkprep(dqa, dka, dva, dqb, dkb, dvb, proj, h1, cos, sin_signed, gq, gk, s_len, ts):
    t, p_cols = proj.shape
    d = h1.shape[1]
    bl, kva, kvb = dka.shape[0], dka.shape[1], dkb.shape[1]
    ha, hb = dqa.shape[1] // HEAD_DIM, dqb.shape[1] // HEAD_DIM
    ns = s_len // ts

    def body(dqa_ref, dka_ref, dva_ref, dqb_ref, dkb_ref, dvb_ref, p_ref, h1_ref, h1p_ref, cos_ref, sin_ref,
             gq_ref, gk_ref, dp_ref, dgq_ref, dgk_ref, gw_ref, gw_sc, dpp_sc):
        b, i = pl.program_id(0), pl.program_id(1)
        cs, sn = cos_ref[...], sin_ref[...]
        low, first = _pair_masks(ts)

        @pl.when((b == 0) & (i == 0))
        def _():
            dgq_ref[...] = jnp.zeros_like(dgq_ref)
            dgk_ref[...] = jnp.zeros_like(dgk_ref)
            gw_sc[...] = jnp.zeros_like(gw_sc)
            dpp_sc[...] = jnp.zeros_like(dpp_sc)

        n_parts = 6
        pw = p_cols // n_parts

        def wgrad_part(c):
            rows = slice(c * pw, (c + 1) * pw)
            gw_sc[rows, :] += _dot_tn(dpp_sc[:, rows], h1p_ref[...])

        def grad_pair(ref, p):
            return jnp.concatenate([ref[0, 2 * p], ref[0, 2 * p + 1]], axis=1)

        def put(p, val):
            dp_ref[:, p * PAIR:(p + 1) * PAIR] = val.astype(BF16)

        def unrope_norm(d_rot, p, g, dg_ref):
            dn = d_rot * cs + _pair_partner(d_rot * sn, first)
            xp = p_ref[:, p * PAIR:(p + 1) * PAIR]
            r = lax.rsqrt(_pair_mean(xp * xp, low) + EPS)
            n = xp * r
            gd = g * dn
            dg_ref[0:1, :] += jnp.sum(dn * n, axis=0, keepdims=True)
            put(p, r * (gd - n * _pair_mean(n * gd, low)))

        parts = iter(range(n_parts))

        def next_wgrad_part():
            c = next(parts, None)
            if c is not None:
                wgrad_part(c)

        for p in range(ha // 2):
            next_wgrad_part()
            unrope_norm(dqa_ref[:, p * PAIR:(p + 1) * PAIR] * SCALE, p, gq_ref[...], dgq_ref)
        base = ha // 2
        for p in range(kva // 2):
            next_wgrad_part()
            unrope_norm(grad_pair(dka_ref, p), base + p, gk_ref[...], dgk_ref)
            put(base + kva // 2 + p, grad_pair(dva_ref, p))
        base += kva
        for p in range(hb // 2):
            put(base + p, dqb_ref[:, p * PAIR:(p + 1) * PAIR] * SCALE)
        base += hb // 2
        for p in range(kvb // 2):
            put(base + p, grad_pair(dkb_ref, p))
            put(base + kvb // 2 + p, grad_pair(dvb_ref, p))
        for c in parts:
            wgrad_part(c)

        dpp_sc[...] = dp_ref[...]

        @pl.when((b == bl - 1) & (i == ns - 1))
        def _():
            gw_ref[...] = (gw_sc[...] + _dot_tn(dp_ref[...], h1_ref[...])).astype(BF16)

    def hm(nh):
        return pl.BlockSpec((1, nh, ts, HEAD_DIM), lambda b, i: (b, 0, i, 0))

    def tokmajor(nh):
        return pl.BlockSpec((ts, nh * HEAD_DIM), lambda b, i: (b * ns + i, 0))

    vec = pl.BlockSpec((1, PAIR), lambda b, i: (0, 0))
    tab = pl.BlockSpec((ts, PAIR), lambda b, i: (i, 0))
    acc = pl.BlockSpec((8, PAIR), lambda b, i: (0, 0))
    pspec = pl.BlockSpec((ts, p_cols), lambda b, i: (b * ns + i, 0))
    return pl.pallas_call(
        body,
        grid=(bl, ns),
        in_specs=[tokmajor(ha), hm(kva), hm(kva), tokmajor(hb), hm(kvb), hm(kvb), pspec,
                  pl.BlockSpec((ts, d), lambda b, i: (b * ns + i, 0)),
                  pl.BlockSpec((ts, d), lambda b, i: (jnp.maximum(b * ns + i - 1, 0), 0)), tab, tab, vec, vec],
        out_specs=[pspec, acc, acc, pl.BlockSpec((p_cols, d), lambda b, i: (0, 0))],
        out_shape=[SDS((t, p_cols), BF16), SDS((8, PAIR), F32), SDS((8, PAIR), F32), SDS((p_cols, d), BF16)],
        scratch_shapes=[pltpu.VMEM((p_cols, d), F32), pltpu.VMEM((ts, p_cols), BF16)],
        compiler_params=_cp("arbitrary", "arbitrary"),
        name="dqkprep",
    )(dqa, dka, dva, dqb, dkb, dvb, proj, h1, h1, cos, sin_signed, gq, gk)


def _dx_final(dproj, w_t, x2, dx1, g1, tm, grads):
    t, d = x2.shape
    p_cols = w_t.shape[0]
    ng = len(grads)
    nsteps = t // tm

    def body(dp_ref, w_ref, x_ref, dx1_ref, g_ref, *rest):
        grad_refs, (dx_ref, dg_ref), parts = rest[:ng], rest[ng:ng + 2], rest[ng + 2:2 * ng + 2]
        start, wait = _direct_exchange("scatter", grad_refs, parts, *rest[2 * ng + 2:])

        @pl.when(pl.program_id(0) == 0)
        def _():
            start()
            dg_ref[...] = jnp.zeros_like(dg_ref)

        dh = _dot(dp_ref[...], w_ref[...])
        g = g_ref[...]
        _, n, r = _rms_fwd(x_ref[...], g)
        dx, dgt = _rms_bwd(n, r, g, dh)
        dx_ref[...] = dx1_ref[...] + dx
        dg_ref[0:1, :] += jnp.sum(dgt, axis=0, keepdims=True)
        pl.when(pl.program_id(0) == nsteps - 1)(wait)

    tile = pl.BlockSpec((tm, d), lambda i: (i, 0))
    anyspec = pl.BlockSpec(memory_space=pl.ANY)
    res = pl.pallas_call(
        body,
        grid=(nsteps,),
        in_specs=[pl.BlockSpec((tm, p_cols), lambda i: (i, 0)),
                  pl.BlockSpec((p_cols, d), lambda i: (0, 0)),
                  tile, tile, pl.BlockSpec((1, d), lambda i: (0, 0))] + [anyspec] * ng,
        out_specs=[tile, pl.BlockSpec((8, d), lambda i: (0, 0))] + [anyspec] * ng,
        out_shape=[SDS((t, d), F32), SDS((8, d), F32)] + [SDS(g.shape, g.dtype) for g in grads],
        scratch_shapes=_exchange_scratch(ng),
        compiler_params=_cp("arbitrary"),
        name="dx_final",
    )(dproj, w_t, x2, dx1, g1, *grads)
    return res[0], res[1], res[2:]


def _adamw_math(w, g, m, v):
    m = ADAM_B1 * m + (1.0 - ADAM_B1) * g
    v = ADAM_B2 * v + (1.0 - ADAM_B2) * (g * g)
    m_hat = m / (1.0 - ADAM_B1 ** ADAM_STEP)
    v_hat = v / (1.0 - ADAM_B2 ** ADAM_STEP)
    delta = -ADAM_LR * (m_hat / (jnp.sqrt(v_hat) + ADAM_EPS) + ADAM_WD * w)
    return delta, m, v


def _adamw_sum(parts, ws, ms, vs, steps):
    nw = len(ws)

    def body(*refs):
        ins, outs = refs[:4 * nw], refs[4 * nw:]
        for k in range(nw):
            p_ref, w_ref, m_ref, v_ref = ins[4 * k:4 * k + 4]
            g_ref, d_ref, nm_ref, nv_ref = outs[4 * k:4 * k + 4]
            g = p_ref[0].astype(F32)
            for s in range(1, N_DEV):
                g = g + p_ref[s].astype(F32)
            g_ref[...] = g
            d_ref[...], nm_ref[...], nv_ref[...] = _adamw_math(w_ref[...], g, m_ref[...], v_ref[...])

    in_specs, out_specs, out_shape, args = [], [], [], []
    for p, w, m, v in zip(parts, ws, ms, vs):
        rows, cols = w.shape
        tile = pl.BlockSpec((rows // steps, cols), lambda i: (i, 0))
        in_specs += [pl.BlockSpec((N_DEV, rows // steps, cols), lambda i: (0, i, 0)), tile, tile, tile]
        out_specs += [tile] * 4
        out_shape += [SDS((rows, cols), F32)] * 4
        args += [p, w, m, v]
    res = pl.pallas_call(
        body,
        grid=(steps,),
        in_specs=in_specs,
        out_specs=out_specs,
        out_shape=out_shape,
        compiler_params=_cp("parallel"),
        name="adamw_weights",
    )(*args)
    return [res[4 * k:4 * k + 4] for k in range(nw)]


def _adamw_small(vec, rel, ws, ms, vs):
    hb = ws[6].shape[1]
    n = len(ws)

    def body(vec_ref, rel_ref, *rest):
        w_refs, m_refs, v_refs = rest[:n], rest[n:2 * n], rest[2 * n:3 * n]
        loss_ref, outs = rest[3 * n], rest[3 * n + 1:]
        grads = [vec_ref[0:1, :], vec_ref[1:2, :], vec_ref[2:3, :], vec_ref[3:4, :],
                 vec_ref[4:5, 0:HEAD_DIM], vec_ref[4:5, SMALL_LANES:SMALL_LANES + HEAD_DIM],
                 vec_ref[4:5, 2 * SMALL_LANES:2 * SMALL_LANES + hb], rel_ref[:, 0:hb]]
        loss_ref[...] = vec_ref[4:5, 3 * SMALL_LANES:3 * SMALL_LANES + 1]
        for p, g in enumerate(grads):
            g_ref, d_ref, nm_ref, nv_ref = outs[4 * p:4 * p + 4]
            g_ref[...] = g
            d_ref[...], nm_ref[...], nv_ref[...] = _adamw_math(w_refs[p][...], g, m_refs[p][...], v_refs[p][...])

    vm = pl.BlockSpec(memory_space=pltpu.VMEM)
    res = pl.pallas_call(
        body,
        in_specs=[vm] * (2 + 3 * n),
        out_specs=[vm] * (1 + 4 * n),
        out_shape=[SDS((1, 1), F32)] + [SDS(w.shape, F32) for w in ws for _ in range(4)],
        name="adamw_small",
    )(vec, rel, *ws, *ms, *vs)
    return res[0], [res[1 + 4 * p:5 + 4 * p] for p in range(n)]


def _local_step(x, loss_target, win_s, wo_s, wup_s, wdn_s, g_pre_mix, g_post_mix, q_norm_a, k_norm_a, sink_b,
                rel_bias, g_pre_ffn, g_post_ffn):
    bl, s_len, d = x.shape
    t = bl * s_len
    nh = d // HEAD_DIM
    ha = nh // 2
    kva = ha // GROUP
    hb = nh - ha
    kvb = hb // GROUP
    tm = 512
    tp = min(1024, t)
    tw = min(4096, t)
    ts = min(512, s_len)
    tq, tk = 2 * BLOCK, min(512, s_len // 2)

    x2 = x.reshape(t, d)
    tg2 = loss_target.reshape(t, d)
    cos, sin_signed = _rope_tables(s_len)
    gq2, gk2 = jnp.tile(q_norm_a, (1, 2)), jnp.tile(k_norm_a, (1, 2))
    a = jnp.arange(BLOCK, dtype=jnp.int32)
    c = jnp.arange(SPAN, dtype=jnp.int32)
    bucket_t = _t5_bucket(c[:, None] - BLOCK - a[None, :])
    bucket_t4 = jnp.tile(bucket_t, (1, GROUP))
    (win_g,), bias_t = _weight_gather([win_s], bucket_t, rel_bias)
    w_in_t = win_g.reshape(-1, d)
    p_cols = w_in_t.shape[0]

    h1, proj, qa, ka, kat, va, vat, qb, kb, kbt, vb, vbt = _inproj_qkprep(
        x2, g_pre_mix, w_in_t, cos, sin_signed, gq2, gk2, bl, s_len, ha, kva, hb, kvb, ts)
    oa, lse_a, (wo_g, wup_g, wdn_g) = _attn_a_fwd(qa, ka, vat, tq, tk, [wo_s, wup_s, wdn_s])
    wo = wo_g.reshape(-1, d)
    wdn = wdn_g.reshape(-1, d)
    ob, lse_b = _attn_b_fwd(qb, kb, vbt, bias_t, sink_b, s_len)
    mix, x1, h2 = _mixout(oa, ob, wo, x2, g_post_mix, g_pre_ffn, tp)
    u, df, dy, dg4, loss8 = _ffn_fwd(h2, wup_g, wdn, x1, tg2, g_post_ffn, tm)

    dpre, dx1, dmix, dg3, dg2 = _ffn_bwd(df, u, wdn, wup_g, x1, dy, mix, g_pre_ffn, g_post_mix, FFN_BWD_TOKENS)
    gw_dn = _wgrad_rows(u, df, N_DEV, tw, "wgrad_down", square=True)
    gw_up = _wgrad_cols(h2, dpre, N_DEV, tw, "wgrad_up")
    gw_o = _wgrad_o(oa, ob, dmix, N_DEV, min(2048, t))
    doa, dob = _attn_out_bwd(dmix, wo, oa.shape[1], tp)
    dqa, dka, dva, (p_o, p_up, p_dn) = _attn_a_bwd(qa, ka, kat, va, doa, oa, lse_a, tq, tk, [gw_o, gw_up, gw_dn])
    dqb, dkb, dvb, dsum, dsink = _attn_b_bwd(qb, kb, kbt, vb, dob, ob, lse_b, bias_t, sink_b, s_len)
    drel_g, dsink_g = _bias_reduce(dsum, dsink, bucket_t4)
    dproj, dgq, dgk, gw_in_t = _dqkprep(dqa, dka, dva, dqb, dkb, dvb, proj, h1, cos, sin_signed, gq2, gk2, s_len, ts)
    gw_in_t = gw_in_t.reshape(N_DEV, -1, d)
    grad_x, dg1, (p_in,) = _dx_final(dproj, w_in_t, x2, dx1, g_pre_mix, tp, [gw_in_t])

    vec, rel = _small_allreduce([dg1, dg2, dg3, dg4], dgq, dgk, dsink_g, drel_g, loss8)
    return grad_x.reshape(bl, s_len, d), p_in, p_o, p_up, p_dn, vec, rel


def kernel(x, w_in, w_o, g_pre_mix, g_post_mix, q_norm_a, k_norm_a, sink_b, rel_bias, g_pre_ffn, w_ffn_up, w_ffn_down, g_post_ffn, loss_target, m_w_in, m_w_o, m_g_pre_mix, m_g_post_mix, m_q_norm_a, m_k_norm_a, m_sink_b, m_rel_bias, m_g_pre_ffn, m_w_ffn_up, m_w_ffn_down, m_g_post_ffn, v_w_in, v_w_o, v_g_pre_mix, v_g_post_mix, v_q_norm_a, v_k_norm_a, v_sink_b, v_rel_bias, v_g_pre_ffn, v_w_ffn_up, v_w_ffn_down, v_g_post_ffn):
    w_in_t = w_in[0].T

    grad_x, p_in, p_o, p_up, p_dn, vec, rel = _local_step(
        x, loss_target, w_in_t.astype(BF16), w_o[0].astype(BF16), w_ffn_up[0].astype(BF16), w_ffn_down[0].astype(BF16),
        g_pre_mix, g_post_mix, q_norm_a, k_norm_a, sink_b, rel_bias, g_pre_ffn, g_post_ffn)

    r_in, r_o, r_up, r_dn = _adamw_sum(
        [p_in, p_o, p_up, p_dn],
        [w_in_t, w_o[0], w_ffn_up[0], w_ffn_down[0]],
        [m_w_in[0].T, m_w_o[0], m_w_ffn_up[0], m_w_ffn_down[0]],
        [v_w_in[0].T, v_w_o[0], v_w_ffn_up[0], v_w_ffn_down[0]], 4)
    big = {"w_in": [a.T for a in r_in], "w_o": r_o, "w_up": r_up, "w_dn": r_dn}
    loss, small = _adamw_small(
        vec, rel,
        [g_pre_mix, g_post_mix, g_pre_ffn, g_post_ffn, q_norm_a, k_norm_a, sink_b, rel_bias],
        [m_g_pre_mix, m_g_post_mix, m_g_pre_ffn, m_g_post_ffn, m_q_norm_a, m_k_norm_a, m_sink_b, m_rel_bias],
        [v_g_pre_mix, v_g_post_mix, v_g_pre_ffn, v_g_post_ffn, v_q_norm_a, v_k_norm_a, v_sink_b, v_rel_bias])
    s_pre_mix, s_post_mix, s_pre_ffn, s_post_ffn, s_qn, s_kn, s_sink, s_rel = small

    def outs(kind):
        return [big["w_in"][kind][None], big["w_o"][kind][None], s_pre_mix[kind], s_post_mix[kind], s_qn[kind],
                s_kn[kind], s_sink[kind], s_rel[kind], s_pre_ffn[kind], big["w_up"][kind][None],
                big["w_dn"][kind][None], s_post_ffn[kind]]

    return (loss.reshape(()), grad_x, *outs(0), *outs(1), *outs(2), *outs(3))
```

```python
import functools

import jax
import jax.numpy as jnp
import numpy as np
from jax import lax
from jax.experimental import pallas as pl
from jax.experimental.pallas import tpu as pltpu

F32 = jnp.float32
BF16 = jnp.bfloat16
SDS = jax.ShapeDtypeStruct

N_DEV = 8
HEAD_DIM = 64
GROUP = 4
BLOCK = 128
SPAN = 3 * BLOCK
GRID_W = 64
N_BUCKETS = 32
MAX_DISTANCE = 128
ROPE_THETA = 10000.0
EPS = 1e-6
NEG_INF = -1e30
SCALE = HEAD_DIM ** -0.5
VT_PAD = 16

ADAM_LR = 0.001
ADAM_B1 = 0.9
ADAM_B2 = 0.999
ADAM_EPS = 1e-08
ADAM_WD = 0.01
ADAM_STEP = 10

VMEM_LIMIT = 56 * 1024 * 1024
MESH = pl.DeviceIdType.MESH


def _cp(*sem):
    return pltpu.CompilerParams(dimension_semantics=sem, vmem_limit_bytes=VMEM_LIMIT)


def _dot(a, b):
    return jnp.dot(a, b, preferred_element_type=F32)


def _dot_nt(a, b):
    return lax.dot_general(a, b, (((1,), (1,)), ((), ())), preferred_element_type=F32)


def _dot_tn(a, b):
    return lax.dot_general(a, b, (((0,), (0,)), ((), ())), preferred_element_type=F32)


def _rms_fwd(x, g):
    r = lax.rsqrt(jnp.mean(x * x, axis=-1, keepdims=True) + EPS)
    n = x * r
    return n * g, n, r


def _rms_bwd(n, r, g, dy):
    gd = g * dy
    dx = r * (gd - n * jnp.mean(n * gd, axis=-1, keepdims=True))
    return dx, dy * n


def _rope_tables(s_len):
    rows = s_len // GRID_W
    row = np.repeat(np.arange(rows, dtype=np.int32), GRID_W)
    col = np.tile(np.arange(GRID_W, dtype=np.int32), rows)
    nf = HEAD_DIM // 4
    freqs = np.float32(ROPE_THETA) ** (-np.arange(nf, dtype=np.float32) / np.float32(nf))
    ang_r = row.astype(np.float32)[:, None] * freqs[None, :]
    ang_c = col.astype(np.float32)[:, None] * freqs[None, :]
    cr, sr, cc, sc = np.cos(ang_r), np.sin(ang_r), np.cos(ang_c), np.sin(ang_c)
    cos = np.concatenate([cr, cr, cc, cc] * 2, axis=-1).astype(np.float32)
    sin_signed = np.concatenate([-sr, sr, -sc, sc] * 2, axis=-1).astype(np.float32)
    return jnp.asarray(cos), jnp.asarray(sin_signed)


def _t5_bucket(rel):
    nb = N_BUCKETS // 2
    ret = (rel > 0).astype(jnp.int32) * nb
    n = jnp.abs(rel)
    max_exact = nb // 2
    nf = jnp.maximum(n, 1).astype(F32)
    large = max_exact + (jnp.log(nf / max_exact) / np.float32(np.log(MAX_DISTANCE / max_exact))
                         * (nb - max_exact)).astype(jnp.int32)
    large = jnp.minimum(large, nb - 1)
    return ret + jnp.where(n < max_exact, n, large)


def _mesh_pos():
    return lax.axis_index("x"), lax.axis_index("y"), lax.axis_index("c")


def _lin(p):
    return 4 * p[0] + 2 * p[1] + p[2]


def _bias_tables(bkt_ref, tbl_ref, out_ref, hb):
    bkt = bkt_ref[...]
    ci = lax.broadcasted_iota(jnp.int32, (SPAN, BLOCK), 0)
    qi = lax.broadcasted_iota(jnp.int32, (SPAN, BLOCK), 1)
    band = jnp.abs(ci - BLOCK - qi) <= BLOCK
    masks = (band, band & (ci >= BLOCK), band & (ci < 2 * BLOCK))
    for h in range(hb):
        acct = jnp.zeros((SPAN, BLOCK), F32)
        for b in range(N_BUCKETS):
            acct = jnp.where(bkt == b, tbl_ref[h, b], acct)
        lanes = slice((h % GROUP) * BLOCK, (h % GROUP + 1) * BLOCK)
        for var, mask in enumerate(masks):
            out_ref[var, h // GROUP, :, lanes] = jnp.where(mask, acct, NEG_INF)


def _weight_gather(shards, bucket_t, rel_bias_t):
    n = len(shards)
    hb = rel_bias_t.shape[0]

    def body(*refs):
        xs, (bkt_ref, tbl_ref), outs, bias_ref = refs[:n], refs[n:n + 2], refs[n + 2:2 * n + 2], refs[2 * n + 2]
        send_sems, recv_sems, local_sems = refs[2 * n + 3:]
        x, y, c = _mesh_pos()
        me, sibling = (x, y, c), (x, y, 1 - c)
        chips = [(1 - x, y), (x, 1 - y), (1 - x, 1 - y)]

        def copy(a, k, block, to, src=None):
            slot = outs[a].at[_lin(block)]
            return pltpu.make_async_remote_copy(
                src_ref=slot if src is None else src, dst_ref=slot,
                send_sem=send_sems.at[a, k], recv_sem=recv_sems.at[a, k],
                device_id=to, device_id_type=MESH)

        started = []
        for a in range(n):
            mine = pltpu.make_async_copy(xs[a], outs[a].at[_lin(me)], local_sems.at[a])
            mine.start()
            started.append(mine)
        sends = []
        for a in range(n):
            first = [copy(a, 0, me, sibling, src=xs[a])]
            first += [copy(a, 1 + j, me, (*chip, c), src=xs[a]) for j, chip in enumerate(chips)]
            for cp in first:
                cp.start()
            sends += first
        _bias_tables(bkt_ref, tbl_ref, bias_ref, hb)
        for a in range(n):
            for j, chip in enumerate(chips):
                copy(a, 1 + j, (*chip, c), me).wait_recv()
                fwd = copy(a, 4 + j, (*chip, c), sibling)
                fwd.start()
                sends.append(fwd)
        for a in range(n):
            copy(a, 0, sibling, me).wait_recv()
            for j, chip in enumerate(chips):
                copy(a, 4 + j, (*chip, 1 - c), me).wait_recv()
        for cp in sends:
            cp.wait_send()
        for mine in started:
            mine.wait()

    anyspec = pl.BlockSpec(memory_space=pl.ANY)
    vm = pl.BlockSpec(memory_space=pltpu.VMEM)
    res = pl.pallas_call(
        body,
        out_shape=[SDS((N_DEV,) + s.shape, s.dtype) for s in shards]
        + [SDS((3, hb // GROUP, SPAN, GROUP * BLOCK), F32)],
        in_specs=[anyspec] * n + [vm, pl.BlockSpec(memory_space=pltpu.SMEM)],
        out_specs=[anyspec] * n + [vm],
        scratch_shapes=[pltpu.SemaphoreType.DMA((n, 7)), pltpu.SemaphoreType.DMA((n, 7)),
                        pltpu.SemaphoreType.DMA((n,))],
        name="weight_gather",
    )(*shards, bucket_t, rel_bias_t)
    return res[:n], res[n]


def _direct_exchange(kind, ins, outs, send_sems, recv_sems, local_sems):
    x, y, c = _mesh_pos()
    me = (x, y, c)
    peers = [(x, y, 1 - c), (1 - x, y, c), (x, 1 - y, c), (1 - x, 1 - y, c),
             (1 - x, y, 1 - c), (x, 1 - y, 1 - c), (1 - x, 1 - y, 1 - c)]

    def src(a, to):
        return ins[a] if kind == "gather" else ins[a].at[_lin(to)]

    def remote(a, k, to, frm):
        return pltpu.make_async_remote_copy(
            src_ref=src(a, to), dst_ref=outs[a].at[_lin(frm)],
            send_sem=send_sems.at[a, k], recv_sem=recv_sems.at[a, k],
            device_id=to, device_id_type=MESH)

    n = len(ins)
    sends = [remote(a, k, p, me) for a in range(n) for k, p in enumerate(peers)]
    arrivals = [remote(a, k, p, p) for a in range(n) for k, p in enumerate(peers)]
    local = [pltpu.make_async_copy(src(a, me), outs[a].at[_lin(me)], local_sems.at[a]) for a in range(n)]

    def start():
        for cp in local + sends:
            cp.start()

    def wait():
        for cp in arrivals:
            cp.wait_recv()
        for cp in sends:
            cp.wait_send()
        for cp in local:
            cp.wait()

    return start, wait


def _exchange_scratch(n):
    return [pltpu.SemaphoreType.DMA((n, 7)), pltpu.SemaphoreType.DMA((n, 7)), pltpu.SemaphoreType.DMA((n,))]


SMALL_LANES = 128


def _small_allreduce(dg_rows, dgq, dgk, dsink_g, drel_g, loss8):
    d = dg_rows[0].shape[1]
    kv = dsink_g.shape[0]

    def body(g1_ref, g2_ref, g3_ref, g4_ref, gq_ref, gk_ref, sk_ref, rl_ref, ls_ref, vec_ref, rel_ref,
             vbuf, rbuf, vland, rland, send_sems, recv_sems):
        x, y, c = _mesh_pos()
        me = (x, y, c)
        peers = [(x, y, 1 - c), (1 - x, y, c), (x, 1 - y, c), (1 - x, 1 - y, c),
                 (1 - x, y, 1 - c), (x, 1 - y, 1 - c), (1 - x, 1 - y, 1 - c)]
        vbuf[...] = jnp.zeros_like(vbuf)
        rbuf[...] = jnp.zeros_like(rbuf)
        for row, ref in enumerate((g1_ref, g2_ref, g3_ref, g4_ref)):
            vbuf[row:row + 1, :] = ref[0:1, :]
        vbuf[4:5, 0:HEAD_DIM] = gq_ref[0:1, 0:HEAD_DIM] + gq_ref[0:1, HEAD_DIM:PAIR]
        vbuf[4:5, SMALL_LANES:SMALL_LANES + HEAD_DIM] = gk_ref[0:1, 0:HEAD_DIM] + gk_ref[0:1, HEAD_DIM:PAIR]
        for g in range(kv):
            vbuf[4:5, 2 * SMALL_LANES + g * GROUP:2 * SMALL_LANES + (g + 1) * GROUP] = sk_ref[g, 0:1, 0:GROUP]
            rbuf[:, g * GROUP:(g + 1) * GROUP] = rl_ref[g, :, 0:GROUP]
        vbuf[4:5, 3 * SMALL_LANES:3 * SMALL_LANES + 1] = ls_ref[0:1, 0:1]

        def copies(k, to, frm):
            return [pltpu.make_async_remote_copy(
                src_ref=buf, dst_ref=land.at[_lin(frm)], send_sem=send_sems.at[a, k], recv_sem=recv_sems.at[a, k],
                device_id=to, device_id_type=MESH) for a, (buf, land) in enumerate(((vbuf, vland), (rbuf, rland)))]

        sends = [cp for k, p in enumerate(peers) for cp in copies(k, p, me)]
        for cp in sends:
            cp.start()
        vland[_lin(me)] = vbuf[...]
        rland[_lin(me)] = rbuf[...]
        for k, p in enumerate(peers):
            for cp in copies(k, p, p):
                cp.wait_recv()
        for cp in sends:
            cp.wait_send()
        vacc, racc = vland[0], rland[0]
        for s in range(1, N_DEV):
            vacc, racc = vacc + vland[s], racc + rland[s]
        vec_ref[...] = vacc
        rel_ref[...] = racc

    vm = pl.BlockSpec(memory_space=pltpu.VMEM)
    return pl.pallas_call(
        body,
        out_shape=[SDS((8, d), F32), SDS((N_BUCKETS, 128), F32)],
        in_specs=[vm] * 9,
        out_specs=[vm, vm],
        scratch_shapes=[pltpu.VMEM((8, d), F32), pltpu.VMEM((N_BUCKETS, 128), F32),
                        pltpu.VMEM((N_DEV, 8, d), F32), pltpu.VMEM((N_DEV, N_BUCKETS, 128), F32),
                        pltpu.SemaphoreType.DMA((2, 7)), pltpu.SemaphoreType.DMA((2, 7))],
        name="small_allreduce",
    )(*dg_rows, dgq, dgk, dsink_g, drel_g, loss8)


PAIR = 2 * HEAD_DIM


def _pair_masks(ts):
    lane = lax.broadcasted_iota(jnp.int32, (ts, PAIR), 1)
    return lane < HEAD_DIM, (lane % 32) < 16


def _pair_mean(v, low):
    del low
    r = lax.broadcasted_iota(jnp.int32, (PAIR, PAIR), 0) // HEAD_DIM
    c = lax.broadcasted_iota(jnp.int32, (PAIR, PAIR), 1) // HEAD_DIM
    same_head = (r == c).astype(BF16)
    hi = v.astype(BF16)
    lo = (v - hi.astype(F32)).astype(BF16)
    return (_dot(hi, same_head) + _dot(lo, same_head)) * (1.0 / HEAD_DIM)


def _pair_partner(v, first):
    return jnp.where(first, pltpu.roll(v, PAIR - 16, 1), pltpu.roll(v, 16, 1))


def _inproj_qkprep(x2, g1, w_t, cos, sin_signed, gq, gk, bl, s_len, ha, kva, hb, kvb, ts):
    t, d = x2.shape
    p_cols = w_t.shape[0]
    assert ha % 2 == 0 and kva % 2 == 0 and hb % 2 == 0 and kvb % 2 == 0
    ns = s_len // ts
    nt = bl * ns
    sp = s_len + 2 * BLOCK

    def body(*refs):
        kb_ref, kbt_ref, vb_ref, vbt_ref, p_even, p_odd = refs[-6:]
        s = pl.program_id(0)
        i = lax.rem(jnp.maximum(s - 1, 0), ns)

        @pl.when(s == 0)
        def _():
            p_odd[...] = jnp.zeros_like(p_odd)

        @pl.when(i == 0)
        def _():
            zeros = jnp.zeros((kvb, BLOCK, HEAD_DIM), BF16)
            zeros_t = jnp.zeros((kvb, HEAD_DIM + VT_PAD, BLOCK), BF16)
            for ref in (kb_ref, vb_ref):
                ref[0, :, 0:BLOCK, :] = zeros
                ref[0, :, sp - BLOCK:sp, :] = zeros
            kbt_ref[0, :, :, 0:BLOCK] = zeros_t[:, 0:HEAD_DIM]
            kbt_ref[0, :, :, sp - BLOCK:sp] = zeros_t[:, 0:HEAD_DIM]
            vbt_ref[0, :, :, 0:BLOCK] = zeros_t
            vbt_ref[0, :, :, sp - BLOCK:sp] = zeros_t

        even = lax.rem(s, 2) == 0
        pl.when(even)(functools.partial(tile_work, p_even, p_odd, i, *refs[:-2]))
        pl.when(jnp.logical_not(even))(functools.partial(tile_work, p_odd, p_even, i, *refs[:-2]))

    def tile_work(p_new, p_ref, i, x_ref, g1_ref, w_ref, cos_ref, sin_ref, gq_ref, gk_ref, h_ref, po_ref, qa_ref,
                  ka_ref, kat_ref, va_ref, vat_ref, qb_ref, kb_ref, kbt_ref, vb_ref, vbt_ref):
        y, _, _ = _rms_fwd(x_ref[...], g1_ref[...])
        h = y.astype(BF16)
        h_ref[...] = h
        n_parts = 6
        pw = p_cols // n_parts

        def project(c):
            p_new[:, c * pw:(c + 1) * pw] = _dot_nt(h, w_ref[c * pw:(c + 1) * pw, :])

        cs, sn = cos_ref[...], sin_ref[...]
        low, first = _pair_masks(ts)
        ones_row = (lax.broadcasted_iota(jnp.int32, (VT_PAD, ts), 0) == 0).astype(BF16)
        heads = (slice(0, HEAD_DIM), slice(HEAD_DIM, PAIR))

        def pair(p):
            v = p_ref[:, p * PAIR:(p + 1) * PAIR]
            po_ref[:, p * PAIR:(p + 1) * PAIR] = v
            return v

        def normrope(x, g):
            y = x * lax.rsqrt(_pair_mean(x * x, low) + EPS) * g
            return y * cs + _pair_partner(y, first) * sn

        eye = (lax.broadcasted_iota(jnp.int32, (PAIR, PAIR), 0)
               == lax.broadcasted_iota(jnp.int32, (PAIR, PAIR), 1)).astype(BF16)

        def transposed(xb):
            return _dot_nt(eye, xb).astype(BF16)

        def prep_qa(p):
            qa_ref[:, p * PAIR:(p + 1) * PAIR] = (normrope(pair(p), gq_ref[...]) * SCALE).astype(BF16)

        def prep_kva(p):
            base = ha // 2
            k = normrope(pair(base + p), gk_ref[...]).astype(BF16)
            v = pair(base + kva // 2 + p).astype(BF16)
            kt, vt = transposed(k), transposed(v)
            for e, lanes in enumerate(heads):
                ka_ref[0, 2 * p + e] = k[:, lanes]
                va_ref[0, 2 * p + e] = v[:, lanes]
                kat_ref[0, 2 * p + e] = kt[lanes, :]
                vat_ref[0, 2 * p + e, 0:HEAD_DIM, :] = vt[lanes, :]
                vat_ref[0, 2 * p + e, HEAD_DIM:HEAD_DIM + VT_PAD, :] = ones_row

        def prep_qb(p):
            base = ha // 2 + kva
            qb_ref[:, p * PAIR:(p + 1) * PAIR] = (pair(base + p) * SCALE).astype(BF16)

        rows = pl.ds(pl.multiple_of(BLOCK + i * ts, BLOCK), ts)

        def prep_kvb(p):
            base = ha // 2 + kva + hb // 2
            k = pair(base + p).astype(BF16)
            v = pair(base + kvb // 2 + p).astype(BF16)
            kt, vt = transposed(k), transposed(v)
            for e, lanes in enumerate(heads):
                kb_ref[0, 2 * p + e, rows, :] = k[:, lanes]
                vb_ref[0, 2 * p + e, rows, :] = v[:, lanes]
                kbt_ref[0, 2 * p + e, :, rows] = kt[lanes, :]
                vbt_ref[0, 2 * p + e, 0:HEAD_DIM, rows] = vt[lanes, :]
                vbt_ref[0, 2 * p + e, HEAD_DIM:HEAD_DIM + VT_PAD, rows] = ones_row

        work = ([functools.partial(prep_qa, p) for p in range(ha // 2)]
                + [functools.partial(prep_kva, p) for p in range(kva // 2)]
                + [functools.partial(prep_qb, p) for p in range(hb // 2)]
                + [functools.partial(prep_kvb, p) for p in range(kvb // 2)])
        per_part = -(-len(work) // n_parts)
        for c in range(n_parts):
            for item in work[c * per_part:(c + 1) * per_part]:
                item()
            project(c)

    def cur(s):
        return jnp.minimum(s, nt - 1)

    def prev(s):
        return jnp.maximum(s - 1, 0) // ns, lax.rem(jnp.maximum(s - 1, 0), ns)

    def hm(nh):
        return pl.BlockSpec((1, nh, ts, HEAD_DIM), lambda s: (prev(s)[0], 0, prev(s)[1], 0))

    def hm_t(nh, rows):
        return pl.BlockSpec((1, nh, rows, ts), lambda s: (prev(s)[0], 0, 0, prev(s)[1]))

    def tokmajor(nh):
        return pl.BlockSpec((ts, nh * HEAD_DIM), lambda s: (jnp.maximum(s - 1, 0), 0))

    def padded(nh):
        return pl.BlockSpec((1, nh, sp, HEAD_DIM), lambda s: (prev(s)[0], 0, 0, 0))

    def padded_t(nh, rows):
        return pl.BlockSpec((1, nh, rows, sp), lambda s: (prev(s)[0], 0, 0, 0))

    tab = pl.BlockSpec((ts, PAIR), lambda s: (prev(s)[1], 0))
    vec = pl.BlockSpec((1, PAIR), lambda s: (0, 0))
    return pl.pallas_call(
        body,
        grid=(nt + 1,),
        in_specs=[pl.BlockSpec((ts, d), lambda s: (cur(s), 0)),
                  pl.BlockSpec((1, d), lambda s: (0, 0)),
                  pl.BlockSpec((p_cols, d), lambda s: (0, 0)),
                  tab, tab, vec, vec],
        out_specs=[pl.BlockSpec((ts, d), lambda s: (cur(s), 0)), tokmajor(p_cols // HEAD_DIM),
                   tokmajor(ha), hm(kva), hm_t(kva, HEAD_DIM), hm(kva), hm_t(kva, HEAD_DIM + VT_PAD),
                   tokmajor(hb), padded(kvb), padded_t(kvb, HEAD_DIM), padded(kvb),
                   padded_t(kvb, HEAD_DIM + VT_PAD)],
        out_shape=[SDS((t, d), BF16), SDS((t, p_cols), F32),
                   SDS((t, ha * HEAD_DIM), BF16), SDS((bl, kva, s_len, HEAD_DIM), BF16),
                   SDS((bl, kva, HEAD_DIM, s_len), BF16),
                   SDS((bl, kva, s_len, HEAD_DIM), BF16), SDS((bl, kva, HEAD_DIM + VT_PAD, s_len), BF16),
                   SDS((t, hb * HEAD_DIM), BF16),
                   SDS((bl, kvb, sp, HEAD_DIM), BF16), SDS((bl, kvb, HEAD_DIM, sp), BF16),
                   SDS((bl, kvb, sp, HEAD_DIM), BF16), SDS((bl, kvb, HEAD_DIM + VT_PAD, sp), BF16)],
        scratch_shapes=[pltpu.VMEM((ts, p_cols), F32)] * 2,
        compiler_params=_cp("arbitrary"),
        name="inproj_qkprep",
    )(x2, g1, w_t, cos, sin_signed, gq, gk)


def _attn_a_fwd(qa, ka, vat, tq, tk, shards):
    bl, kv, s_len, _ = ka.shape
    ha = qa.shape[1] // HEAD_DIM
    va_rows = vat.shape[2]
    nq, nk = s_len // tq, s_len // tk
    assert nk % 2 == 0
    r = GROUP * tq
    ns = len(shards)

    def body(q_ref, qn_ref, k_ref, v_ref, *rest):
        shard_refs, (o_ref, l_ref), gathered = rest[:ns], rest[ns:ns + 2], rest[ns + 2:2 * ns + 2]
        st_sc, send_sems, recv_sems, local_sems = rest[2 * ns + 2:]
        i = pl.program_id(2)
        step_id = (pl.program_id(0) * kv + pl.program_id(1)) * nq + i
        start, wait = _direct_exchange("gather", shard_refs, gathered, send_sems, recv_sems, local_sems)
        pl.when(step_id == 0)(start)

        q = _heads_t(q_ref[...]).astype(BF16)

        def scores(c, qv):
            return _dot(k_ref[0, 0, pl.ds(pl.multiple_of(c * tk, tk), tk), :], qv)

        def fold(st, c, carry):
            m_old, acc = carry
            m_new = jnp.maximum(m_old, jnp.max(st, axis=0, keepdims=True))
            pt = jnp.exp(st - m_new).astype(BF16)
            vt = v_ref[0, 0, :, pl.ds(pl.multiple_of(c * tk, tk), tk)]
            return m_new, jnp.exp(m_old - m_new) * acc + _dot(vt, pt)

        @pl.when(i == 0)
        def _():
            st_sc[0] = scores(0, q)

        def step(c2, carry):
            c = 2 * c2
            st_sc[1] = scores(c + 1, q)
            carry = fold(st_sc[0], c, carry)
            st_sc[0] = scores(c + 2, q)
            return fold(st_sc[1], c + 1, carry)

        carry = (jnp.full((1, r), -jnp.inf, F32), jnp.zeros((va_rows, r), F32))
        for c2 in range(nk // 2 - 1):
            carry = step(c2, carry)
        st_sc[1] = scores(nk - 1, q)
        carry = fold(st_sc[0], nk - 2, carry)
        st_sc[0] = scores(0, _heads_t(qn_ref[...]).astype(BF16))
        m, acc = fold(st_sc[1], nk - 1, carry)
        l = acc[HEAD_DIM:HEAD_DIM + 1, :]
        o_ref[...] = _heads_t_inv(acc[0:HEAD_DIM, :] / l).astype(BF16)
        l_ref[0, 0, 0] = jnp.broadcast_to(m + jnp.log(l), (8, r))
        pl.when(step_id == bl * kv * nq - 1)(wait)

    anyspec = pl.BlockSpec(memory_space=pl.ANY)
    res = pl.pallas_call(
        body,
        grid=(bl, kv, nq),
        in_specs=[pl.BlockSpec((tq, GROUP * HEAD_DIM), lambda b, g, i: (b * nq + i, g)),
                  pl.BlockSpec((tq, GROUP * HEAD_DIM), lambda b, g, i: (b * nq + jnp.minimum(i + 1, nq - 1), g)),
                  pl.BlockSpec((1, 1, s_len, HEAD_DIM), lambda b, g, i: (b, g, 0, 0)),
                  pl.BlockSpec((1, 1, va_rows, s_len), lambda b, g, i: (b, g, 0, 0))] + [anyspec] * ns,
        out_specs=[pl.BlockSpec((tq, GROUP * HEAD_DIM), lambda b, g, i: (b * nq + i, g)),
                   pl.BlockSpec((1, 1, 1, 8, r), lambda b, g, i: (b, g, i, 0, 0))] + [anyspec] * ns,
        out_shape=[SDS((bl * s_len, ha * HEAD_DIM), BF16), SDS((bl, kv, nq, 8, r), F32)]
        + [SDS((N_DEV,) + s.shape, s.dtype) for s in shards],
        scratch_shapes=[pltpu.VMEM((2, tk, r), F32)] + _exchange_scratch(ns),
        compiler_params=_cp("arbitrary", "arbitrary", "arbitrary"),
        name="attn_a_fwd",
    )(qa, qa, ka, vat, *shards)
    return res[0], res[1], res[2:]


FFN_BWD_TOKENS = 256
QB_PER_STEP = 16


def _bias_variant(n, nb):
    return jnp.where(n == 0, 1, jnp.where(n == nb - 1, 2, 0))


def _sink_row(sink_ref, g):
    return jnp.concatenate([jnp.full((1, BLOCK), sink_ref[0, g * GROUP + h], F32) for h in range(GROUP)], axis=1)


def _attn_b_fwd(qb, kb, vbt, bias_t, sink, s_len):
    bl, kv, sp, _ = kb.shape
    hb = qb.shape[1] // HEAD_DIM
    vt_rows = vbt.shape[2]
    nb = s_len // BLOCK
    nbs = min(QB_PER_STEP, nb)
    r = GROUP * BLOCK

    def body(q_ref, k_ref, vt_ref, bt_ref, sink_ref, o_ref, l_ref, st_sc, pb_sc):
        g, n0 = pl.program_id(1), pl.program_id(2) * nbs
        sink_row = _sink_row(sink_ref, g)

        def span(j):
            return pl.ds(pl.multiple_of((n0 + j) * BLOCK, BLOCK), SPAN)

        for j in range(nbs):
            qt = _heads_t(q_ref[j * BLOCK:(j + 1) * BLOCK, :]).astype(BF16)
            st_sc[j] = _dot(k_ref[0, 0, span(j), :], qt) + bt_ref[_bias_variant(n0 + j, nb), 0]
        maxes = []
        for j in range(nbs):
            st = st_sc[j]
            m = jnp.maximum(jnp.max(st, axis=0, keepdims=True), sink_row)
            pb_sc[j] = jnp.exp(st - m).astype(BF16)
            maxes.append(m)
        for j in range(nbs):
            m = maxes[j]
            acc = _dot(vt_ref[0, 0, :, span(j)], pb_sc[j])
            l = acc[HEAD_DIM:HEAD_DIM + 1, :] + jnp.exp(sink_row - m)
            o_ref[j * BLOCK:(j + 1) * BLOCK, :] = _heads_t_inv(acc[0:HEAD_DIM, :] / l).astype(BF16)
            l_ref[0, 0, j] = jnp.broadcast_to(m + jnp.log(l), (8, r))

    return pl.pallas_call(
        body,
        grid=(bl, kv, nb // nbs),
        in_specs=[pl.BlockSpec((nbs * BLOCK, GROUP * HEAD_DIM), lambda b, g, n: (b * (nb // nbs) + n, g)),
                  pl.BlockSpec((1, 1, sp, HEAD_DIM), lambda b, g, n: (b, g, 0, 0)),
                  pl.BlockSpec((1, 1, vt_rows, sp), lambda b, g, n: (b, g, 0, 0)),
                  pl.BlockSpec((3, 1, SPAN, r), lambda b, g, n: (0, g, 0, 0)),
                  pl.BlockSpec(memory_space=pltpu.SMEM)],
        out_specs=[pl.BlockSpec((nbs * BLOCK, GROUP * HEAD_DIM), lambda b, g, n: (b * (nb // nbs) + n, g)),
                   pl.BlockSpec((1, 1, nbs, 8, r), lambda b, g, n: (b, g, n, 0, 0))],
        out_shape=[SDS((bl * s_len, hb * HEAD_DIM), BF16), SDS((bl, kv, nb, 8, r), F32)],
        scratch_shapes=[pltpu.VMEM((nbs, SPAN, r), F32), pltpu.VMEM((nbs, SPAN, r), BF16)],
        compiler_params=_cp("parallel", "parallel", "arbitrary"),
        name="attn_b_fwd",
    )(qb, kb, vbt, bias_t, sink)


def _mixout(oa, ob, wo, x2, g2, g3, tm):
    t, d = x2.shape
    ca = oa.shape[1]

    def body(oa_ref, ob_ref, w_ref, x_ref, g2_ref, g3_ref, mix_ref, x1_ref, h2_ref):
        mix = _dot(oa_ref[...], w_ref[0:ca, :]) + _dot(ob_ref[...], w_ref[ca:, :])
        mix_ref[...] = mix
        y2, _, _ = _rms_fwd(mix, g2_ref[...])
        x1 = x_ref[...] + y2
        x1_ref[...] = x1
        y3, _, _ = _rms_fwd(x1, g3_ref[...])
        h2_ref[...] = y3.astype(BF16)

    tile = lambda w: pl.BlockSpec((tm, w), lambda i: (i, 0))
    vec = pl.BlockSpec((1, d), lambda i: (0, 0))
    return pl.pallas_call(
        body,
        grid=(t // tm,),
        in_specs=[tile(ca), tile(ob.shape[1]), pl.BlockSpec(wo.shape, lambda i: (0, 0)), tile(d), vec, vec],
        out_specs=[tile(d), tile(d), tile(d)],
        out_shape=[SDS((t, d), F32), SDS((t, d), F32), SDS((t, d), BF16)],
        compiler_params=_cp("parallel"),
        name="mixout",
    )(oa, ob, wo, x2, g2, g3)


def _ffn_fwd(h2, wup_g, wdn, x1, target, g4, tm):
    t, d = x1.shape
    nblk, _, tf = wup_g.shape
    ff = nblk * tf
    nt = t // tm

    def body(h_ref, wu_ref, wd_ref, x1_ref, tg_ref, g_ref, u_ref, df_ref, dy_ref, dg_ref, loss_ref, f_sc):
        s = pl.program_id(0)

        @pl.when(s == 0)
        def _():
            f_sc[...] = jnp.zeros_like(f_sc)
            dg_ref[...] = jnp.zeros_like(dg_ref)
            loss_ref[...] = jnp.zeros_like(loss_ref)

        counted = (s > 0).astype(F32)

        def loss_part():
            g = g_ref[...]
            y4, n, r = _rms_fwd(f_sc[...], g)
            e = (x1_ref[...] + y4) - tg_ref[...]
            loss_ref[...] += jnp.sum(e * e) * (0.5 / d) * counted
            dy = e * (1.0 / d)
            dy_ref[...] = dy
            return n, r, g, dy

        def norm_bwd_part(n, r, g, dy):
            df, dgt = _rms_bwd(n, r, g, dy)
            df_ref[...] = df.astype(BF16)
            dg_ref[0:1, :] += jnp.sum(dgt, axis=0, keepdims=True) * counted

        @pl.when(s < nt)
        def _():
            h = h_ref[...]
            squares = []
            saved = None
            for c in range(nblk):
                u = jnp.maximum(_dot(h, wu_ref[c]), 0.0)
                u_ref[:, c * tf:(c + 1) * tf] = u.astype(BF16)
                squares.append((u * u).astype(BF16))
                if c == 0:
                    saved = loss_part()
                elif c == 1:
                    norm_bwd_part(*saved)
            f_sc[...] = _dot(jnp.concatenate(squares, axis=1), wd_ref[...])

        @pl.when(s == nt)
        def _():
            norm_bwd_part(*loss_part())

    cur = lambda s: (jnp.minimum(s, nt - 1), 0)
    prev = lambda s: (jnp.maximum(s - 1, 0), 0)
    return pl.pallas_call(
        body,
        grid=(nt + 1,),
        in_specs=[pl.BlockSpec((tm, d), cur),
                  pl.BlockSpec((nblk, d, tf), lambda s: (0, 0, 0)),
                  pl.BlockSpec((ff, d), lambda s: (0, 0)),
                  pl.BlockSpec((tm, d), prev), pl.BlockSpec((tm, d), prev),
                  pl.BlockSpec((1, d), lambda s: (0, 0))],
        out_specs=[pl.BlockSpec((tm, ff), cur), pl.BlockSpec((tm, d), prev), pl.BlockSpec((tm, d), prev),
                   pl.BlockSpec((8, d), lambda s: (0, 0)),
                   pl.BlockSpec((8, 128), lambda s: (0, 0))],
        out_shape=[SDS((t, ff), BF16), SDS((t, d), BF16), SDS((t, d), F32), SDS((8, d), F32), SDS((8, 128), F32)],
        scratch_shapes=[pltpu.VMEM((tm, d), F32)],
        compiler_params=_cp("arbitrary"),
        name="ffn_fwd",
    )(h2, wup_g, wdn, x1, target, g4)


def _ffn_bwd(df, u, wdn, wup_g, x1, dy, mix, g3, g2, tm):
    t, d = x1.shape
    nblk, _, tf = wup_g.shape
    ff = nblk * tf
    nt = t // tm

    def body(df_ref, u_ref, wd_ref, wu_ref, x1_ref, dy_ref, mix_ref, g3_ref, g2_ref,
             dpre_ref, dx1_ref, dmix_ref, dg3_ref, dg2_ref, dh_sc):
        s = pl.program_id(0)

        @pl.when(s == 0)
        def _():
            dh_sc[...] = jnp.zeros_like(dh_sc)
            dg3_ref[...] = jnp.zeros_like(dg3_ref)
            dg2_ref[...] = jnp.zeros_like(dg2_ref)

        counted = (s > 0).astype(F32)

        def residual_norm_part():
            g3 = g3_ref[...]
            _, n3, r3 = _rms_fwd(x1_ref[...], g3)
            dx, dgt3 = _rms_bwd(n3, r3, g3, dh_sc[...])
            dx1 = dy_ref[...] + dx
            dx1_ref[...] = dx1
            dg3_ref[0:1, :] += jnp.sum(dgt3, axis=0, keepdims=True) * counted

        def mix_norm_part():
            g2 = g2_ref[...]
            _, n2, r2 = _rms_fwd(mix_ref[...], g2)
            dmix, dgt2 = _rms_bwd(n2, r2, g2, dx1_ref[...])
            dmix_ref[...] = dmix.astype(BF16)
            dg2_ref[0:1, :] += jnp.sum(dgt2, axis=0, keepdims=True) * counted

        @pl.when(s < nt)
        def _():
            du2 = _dot_nt(df_ref[...], wd_ref[...])
            dpre = (2.0 * u_ref[...].astype(F32) * du2).astype(BF16)
            dpre_ref[...] = dpre
            dh = _dot_nt(dpre[:, 0:tf], wu_ref[0])
            for c in range(1, nblk):
                if c == 1:
                    residual_norm_part()
                elif c == 3:
                    mix_norm_part()
                dh = dh + _dot_nt(dpre[:, c * tf:(c + 1) * tf], wu_ref[c])
            dh_sc[...] = dh

        @pl.when(s == nt)
        def _():
            residual_norm_part()
            mix_norm_part()

    cur = lambda s: (jnp.minimum(s, nt - 1), 0)
    prev = lambda s: (jnp.maximum(s - 1, 0), 0)
    vec = pl.BlockSpec((1, d), lambda s: (0, 0))
    acc8 = pl.BlockSpec((8, d), lambda s: (0, 0))
    return pl.pallas_call(
        body,
        grid=(nt + 1,),
        in_specs=[pl.BlockSpec((tm, d), cur),
                  pl.BlockSpec((tm, ff), cur),
                  pl.BlockSpec((ff, d), lambda s: (0, 0)),
                  pl.BlockSpec((nblk, d, tf), lambda s: (0, 0, 0)),
                  pl.BlockSpec((tm, d), prev), pl.BlockSpec((tm, d), prev), pl.BlockSpec((tm, d), prev), vec, vec],
        out_specs=[pl.BlockSpec((tm, ff), cur), pl.BlockSpec((tm, d), prev), pl.BlockSpec((tm, d), prev),
                   acc8, acc8],
        out_shape=[SDS(u.shape, BF16), SDS((t, d), F32), SDS((t, d), BF16), SDS((8, d), F32), SDS((8, d), F32)],
        scratch_shapes=[pltpu.VMEM((tm, d), F32)],
        compiler_params=_cp("arbitrary"),
        name="ffn_bwd",
    )(df, u, wdn, wup_g, x1, dy, mix, g3, g2)


def _wgrad(a, b, a_spec, b_spec, out_block, out_shape, nj, nk, name, prep_a=None, prep_b=None):
    acc_shape = out_block[1:]

    def body(a_ref, b_ref, o_ref, acc_sc):
        k = pl.program_id(1)
        av = a_ref[...] if prep_a is None else prep_a(a_ref)
        bv = b_ref[...] if prep_b is None else prep_b(b_ref)
        part = _dot_tn(av, bv)

        @pl.when(k == 0)
        def _():
            acc_sc[...] = part

        @pl.when(k > 0)
        def _():
            acc_sc[...] += part

        @pl.when(k == nk - 1)
        def _():
            o_ref[0] = acc_sc[...].astype(BF16)

    return pl.pallas_call(
        body,
        grid=(nj, nk),
        in_specs=[a_spec, b_spec],
        out_specs=pl.BlockSpec(out_block, lambda j, k: (j, 0, 0)),
        out_shape=SDS(out_shape, BF16),
        scratch_shapes=[pltpu.VMEM(acc_shape, F32)],
        compiler_params=_cp("parallel", "arbitrary"),
        name=name,
    )(a, b)


def _wgrad_cols(a, b, nj, tt, name):
    t, m = a.shape
    bn = b.shape[1] // nj
    return _wgrad(a, b, pl.BlockSpec((tt, m), lambda j, k: (k, 0)), pl.BlockSpec((tt, bn), lambda j, k: (k, j)),
                  (1, m, bn), (nj, m, bn), nj, t // tt, name)


def _wgrad_rows(a, b, nj, tt, name, square=False):
    t, n = b.shape
    bm = a.shape[1] // nj

    def squared(a_ref):
        af = a_ref[...].astype(F32)
        return (af * af).astype(BF16)

    return _wgrad(a, b, pl.BlockSpec((tt, bm), lambda j, k: (k, j)), pl.BlockSpec((tt, n), lambda j, k: (k, 0)),
                  (1, bm, n), (nj, bm, n), nj, t // tt, name, prep_a=squared if square else None)


def _wgrad_o(oa, ob, dmix, nj, tt):
    t, n = dmix.shape
    ca, cb = oa.shape[1], ob.shape[1]
    m = ca + cb
    nk = t // tt

    def body(oa_ref, ob_ref, b_ref, o_ref, acc_sc):
        k = pl.program_id(0)
        part = _dot_tn(jnp.concatenate([oa_ref[...], ob_ref[...]], axis=1), b_ref[...])

        @pl.when(k == 0)
        def _():
            acc_sc[...] = part

        @pl.when(k > 0)
        def _():
            acc_sc[...] += part

        @pl.when(k == nk - 1)
        def _():
            o_ref[...] = acc_sc[...].reshape(nj, m // nj, n).astype(BF16)

    return pl.pallas_call(
        body,
        grid=(nk,),
        in_specs=[pl.BlockSpec((tt, ca), lambda k: (k, 0)), pl.BlockSpec((tt, cb), lambda k: (k, 0)),
                  pl.BlockSpec((tt, n), lambda k: (k, 0))],
        out_specs=pl.BlockSpec((nj, m // nj, n), lambda k: (0, 0, 0)),
        out_shape=SDS((nj, m // nj, n), BF16),
        scratch_shapes=[pltpu.VMEM((m, n), F32)],
        compiler_params=_cp("arbitrary"),
        name="wgrad_o",
    )(oa, ob, dmix)


def _attn_out_bwd(dmix, wo, ca, tm):
    t, d = dmix.shape
    cb = wo.shape[0] - ca

    def body(dm_ref, w_ref, da_ref, db_ref):
        dm = dm_ref[...]
        da_ref[...] = _dot_nt(dm, w_ref[0:ca, :]).astype(BF16)
        db_ref[...] = _dot_nt(dm, w_ref[ca:, :]).astype(BF16)

    return pl.pallas_call(
        body,
        grid=(t // tm,),
        in_specs=[pl.BlockSpec((tm, d), lambda i: (i, 0)), pl.BlockSpec(wo.shape, lambda i: (0, 0))],
        out_specs=[pl.BlockSpec((tm, ca), lambda i: (i, 0)), pl.BlockSpec((tm, cb), lambda i: (i, 0))],
        out_shape=[SDS((t, ca), BF16), SDS((t, cb), BF16)],
        compiler_params=_cp("parallel"),
        name="attn_out_bwd",
    )(dmix, wo)


def _heads_t(x):
    xt = x.astype(F32).T
    return jnp.concatenate([xt[h * HEAD_DIM:(h + 1) * HEAD_DIM, :] for h in range(GROUP)], axis=1)


def _heads_t_inv(yt):
    n = yt.shape[1] // GROUP
    return jnp.concatenate([yt[:, h * n:(h + 1) * n] for h in range(GROUP)], axis=0).T


def _attn_a_bwd(qa, ka, kat, va, do, o, lse, tq, tk, grads):
    bl, kv, s_len, _ = ka.shape
    nq, nk = s_len // tq, s_len // tk
    assert nk % 2 == 0
    r = GROUP * tq
    ng = len(grads)

    def body(q_ref, qn_ref, k_ref, kt_ref, v_ref, do_ref, don_ref, o_ref, l_ref, *rest):
        grad_refs, (dq_ref, dk_ref, dv_ref), parts = rest[:ng], rest[ng:ng + 3], rest[ng + 3:2 * ng + 3]
        st_sc, dp_sc, dkt_sc, dvt_sc, send_sems, recv_sems, local_sems = rest[2 * ng + 3:]
        i = pl.program_id(2)
        step_id = (pl.program_id(0) * kv + pl.program_id(1)) * nq + i
        start, wait = _direct_exchange("scatter", grad_refs, parts, send_sems, recv_sems, local_sems)
        pl.when(step_id == 0)(start)

        dot32 = _heads_t(do_ref[...])
        drow = jnp.sum(dot32 * _heads_t(o_ref[...]), axis=0, keepdims=True)
        qt, dot = _heads_t(q_ref[...]).astype(BF16), dot32.astype(BF16)
        lrow = l_ref[0, 0, 0, 0:1, :]

        @pl.when(i == 0)
        def _():
            dkt_sc[...] = jnp.zeros_like(dkt_sc)
            dvt_sc[...] = jnp.zeros_like(dvt_sc)

        def chunk(c):
            return pl.ds(pl.multiple_of(c * tk, tk), tk)

        def scores(c, slot, qv=qt, dov=dot):
            st_sc[slot] = _dot(k_ref[0, 0, chunk(c), :], qv)
            dp_sc[slot] = _dot(v_ref[0, 0, chunk(c), :], dov)

        def fold(slot, c, dqt):
            pt = jnp.exp(st_sc[slot] - lrow)
            dsb = (pt * (dp_sc[slot] - drow)).astype(BF16)
            dvt_sc[:, chunk(c)] += _dot_nt(dot, pt.astype(BF16))
            dkt_sc[:, chunk(c)] += _dot_nt(qt, dsb)
            return dqt + _dot(kt_ref[0, 0, :, chunk(c)], dsb)

        @pl.when(i == 0)
        def _():
            scores(0, 0)

        def step(c2, dqt):
            c = 2 * c2
            scores(c + 1, 1)
            dqt = fold(0, c, dqt)
            scores(c + 2, 0)
            return fold(1, c + 1, dqt)

        dqt = jnp.zeros((HEAD_DIM, r), F32)
        for c2 in range(nk // 2 - 1):
            dqt = step(c2, dqt)
        scores(nk - 1, 1)
        dqt = fold(0, nk - 2, dqt)
        scores(0, 0, _heads_t(qn_ref[...]).astype(BF16), _heads_t(don_ref[...]).astype(BF16))
        dq_ref[...] = _heads_t_inv(fold(1, nk - 1, dqt))

        @pl.when(i == nq - 1)
        def _():
            dk_ref[0, 0] = dkt_sc[...].T
            dv_ref[0, 0] = dvt_sc[...].T

        pl.when(step_id == bl * kv * nq - 1)(wait)

    kvspec = pl.BlockSpec((1, 1, s_len, HEAD_DIM), lambda b, g, i: (b, g, 0, 0))
    tok = pl.BlockSpec((tq, GROUP * HEAD_DIM), lambda b, g, i: (b * nq + i, g))
    toknext = pl.BlockSpec((tq, GROUP * HEAD_DIM), lambda b, g, i: (b * nq + jnp.minimum(i + 1, nq - 1), g))
    anyspec = pl.BlockSpec(memory_space=pl.ANY)
    res = pl.pallas_call(
        body,
        grid=(bl, kv, nq),
        in_specs=[tok, toknext, kvspec, pl.BlockSpec((1, 1, HEAD_DIM, s_len), lambda b, g, i: (b, g, 0, 0)), kvspec,
                  tok, toknext, tok, pl.BlockSpec((1, 1, 1, 8, r), lambda b, g, i: (b, g, i, 0, 0))] + [anyspec] * ng,
        out_specs=[tok, kvspec, kvspec] + [anyspec] * ng,
        out_shape=[SDS(qa.shape, F32), SDS(ka.shape, F32), SDS(va.shape, F32)]
        + [SDS(g.shape, g.dtype) for g in grads],
        scratch_shapes=[pltpu.VMEM((2, tk, r), F32), pltpu.VMEM((2, tk, r), F32),
                        pltpu.VMEM((HEAD_DIM, s_len), F32), pltpu.VMEM((HEAD_DIM, s_len), F32)]
        + _exchange_scratch(ng),
        compiler_params=_cp("arbitrary", "arbitrary", "arbitrary"),
        name="attn_a_bwd",
    )(qa, qa, ka, kat, va, do, do, o, lse, *grads)
    return res[0], res[1], res[2], res[3:]


def _attn_b_bwd(qb, kb, kbt, vb, do, o, lse, bias_t, sink, s_len):
    bl, kv, sp, _ = kb.shape
    nb = s_len // BLOCK
    nbs = min(QB_PER_STEP, nb)
    r = GROUP * BLOCK

    def body(q_ref, k_ref, kt_ref, v_ref, do_ref, o_ref, l_ref, bt_ref, sink_ref,
             dq_ref, dk_ref, dv_ref, dsum_ref, dsink_ref, dkt_sc, dvt_sc):
        g, b, ns = pl.program_id(0), pl.program_id(1), pl.program_id(2)
        sink_row = _sink_row(sink_ref, g)

        @pl.when(ns == 0)
        def _():
            dkt_sc[...] = jnp.zeros_like(dkt_sc)
            dvt_sc[...] = jnp.zeros_like(dvt_sc)

        @pl.when((b == 0) & (ns == 0))
        def _():
            dsum_ref[...] = jnp.zeros_like(dsum_ref)
            dsink_ref[...] = jnp.zeros_like(dsink_ref)

        dsum = jnp.zeros((SPAN, r), F32)
        dsink = jnp.zeros((1, r), F32)
        for j in range(nbs):
            n = ns * nbs + j
            span = pl.ds(pl.multiple_of(n * BLOCK, BLOCK), SPAN)
            rows = slice(j * BLOCK, (j + 1) * BLOCK)
            dot32 = _heads_t(do_ref[rows, :])
            drow = jnp.sum(dot32 * _heads_t(o_ref[rows, :]), axis=0, keepdims=True)
            qt, dot = _heads_t(q_ref[rows, :]).astype(BF16), dot32.astype(BF16)
            lrow = l_ref[0, 0, j, 0:1, :]
            st = _dot(k_ref[0, 0, span, :], qt) + bt_ref[_bias_variant(n, nb), 0]
            pt = jnp.exp(st - lrow)
            dst = pt * (_dot(v_ref[0, 0, span, :], dot) - drow)
            dsum = dsum + dst
            dsink = dsink - jnp.exp(sink_row - lrow) * drow
            dsb = dst.astype(BF16)
            dvt_sc[:, span] += _dot_nt(dot, pt.astype(BF16))
            dkt_sc[:, span] += _dot_nt(qt, dsb)
            dq_ref[rows, :] = _heads_t_inv(_dot(kt_ref[0, 0, :, span], dsb))
        dsum_ref[0] += dsum
        dsink_ref[0, 0:1, :] += dsink

        @pl.when(ns == nb // nbs - 1)
        def _():
            dk_ref[0, 0] = dkt_sc[:, BLOCK:BLOCK + s_len].T
            dv_ref[0, 0] = dvt_sc[:, BLOCK:BLOCK + s_len].T

    kvspec = pl.BlockSpec((1, 1, sp, HEAD_DIM), lambda g, b, n: (b, g, 0, 0))
    kvout = pl.BlockSpec((1, 1, s_len, HEAD_DIM), lambda g, b, n: (b, g, 0, 0))
    tok = pl.BlockSpec((nbs * BLOCK, GROUP * HEAD_DIM), lambda g, b, n: (b * (nb // nbs) + n, g))
    return pl.pallas_call(
        body,
        grid=(kv, bl, nb // nbs),
        in_specs=[tok, kvspec, pl.BlockSpec((1, 1, HEAD_DIM, sp), lambda g, b, n: (b, g, 0, 0)), kvspec, tok, tok,
                  pl.BlockSpec((1, 1, nbs, 8, r), lambda g, b, n: (b, g, n, 0, 0)),
                  pl.BlockSpec((3, 1, SPAN, r), lambda g, b, n: (0, g, 0, 0)),
                  pl.BlockSpec(memory_space=pltpu.SMEM)],
        out_specs=[tok, kvout, kvout,
                   pl.BlockSpec((1, SPAN, r), lambda g, b, n: (g, 0, 0)),
                   pl.BlockSpec((1, 8, r), lambda g, b, n: (g, 0, 0))],
        out_shape=[SDS(qb.shape, F32), SDS((bl, kv, s_len, HEAD_DIM), F32), SDS((bl, kv, s_len, HEAD_DIM), F32),
                   SDS((kv, SPAN, r), F32), SDS((kv, 8, r), F32)],
        scratch_shapes=[pltpu.VMEM((HEAD_DIM, sp), F32), pltpu.VMEM((HEAD_DIM, sp), F32)],
        compiler_params=_cp("arbitrary", "arbitrary", "arbitrary"),
        name="attn_b_bwd",
    )(qb, kb, kbt, vb, do, o, lse, bias_t, sink)


def _bias_reduce(dsum, dsink, bucket_t4):
    kv, _, r = dsum.shape

    def body(ds_ref, dk_ref, bk_ref, rel_ref, sink_ref):
        lane = lax.broadcasted_iota(jnp.int32, (N_BUCKETS, 128), 1)
        lane8 = lax.broadcasted_iota(jnp.int32, (8, 128), 1)
        bk = bk_ref[...]
        for g in range(kv):
            ds = ds_ref[g]
            rowi = lax.broadcasted_iota(jnp.int32, (N_BUCKETS, r), 0)
            red = jnp.zeros((N_BUCKETS, r), F32)
            for b in range(N_BUCKETS):
                red = jnp.where(rowi == b, jnp.sum(jnp.where(bk == b, ds, 0.0), axis=0, keepdims=True), red)
            out = jnp.zeros((N_BUCKETS, 128), F32)
            so = jnp.zeros((8, 128), F32)
            for h in range(GROUP):
                col = jnp.sum(red[:, h * BLOCK:(h + 1) * BLOCK], axis=1, keepdims=True)
                out = jnp.where(lane == h, col, out)
                sc = jnp.sum(dk_ref[g][:, h * BLOCK:(h + 1) * BLOCK], axis=1, keepdims=True)
                so = jnp.where(lane8 == h, sc, so)
            rel_ref[g] = out
            sink_ref[g] = so

    vm = pl.BlockSpec(memory_space=pltpu.VMEM)
    return pl.pallas_call(
        body,
        in_specs=[vm, vm, vm],
        out_specs=[vm, vm],
        out_shape=[SDS((kv, N_BUCKETS, 128), F32), SDS((kv, 8, 128), F32)],
        name="bias_reduce",
    )(dsum, dsink, bucket_t4)


def _dqkprep(dqa, dka, dva, dqb, dkb, dvb, proj, h1, cos, sin_signed, gq, gk, s_len, ts):
    t, p_cols = proj.shape
    d = h1.shape[1]
    bl, kva, kvb = dka.shape[0], dka.shape[1], dkb.shape[1]
    ha, hb = dqa.shape[1] // HEAD_DIM, dqb.shape[1] // HEAD_DIM
    ns = s_len // ts

    def body(dqa_ref, dka_ref, dva_ref, dqb_ref, dkb_ref, dvb_ref, p_ref, h1_ref, h1p_ref, cos_ref, sin_ref,
             gq_ref, gk_ref, dp_ref, dgq_ref, dgk_ref, gw_ref, gw_sc, dpp_sc):
        b, i = pl.program_id(0), pl.program_id(1)
        cs, sn = cos_ref[...], sin_ref[...]
        low, first = _pair_masks(ts)

        @pl.when((b == 0) & (i == 0))
        def _():
            dgq_ref[...] = jnp.zeros_like(dgq_ref)
            dgk_ref[...] = jnp.zeros_like(dgk_ref)
            gw_sc[...] = jnp.zeros_like(gw_sc)
            dpp_sc[...] = jnp.zeros_like(dpp_sc)

        n_parts = 6
        pw = p_cols // n_parts

        def wgrad_part(c):
            rows = slice(c * pw, (c + 1) * pw)
            gw_sc[rows, :] += _dot_tn(dpp_sc[:, rows], h1p_ref[...])

        def grad_pair(ref, p):
            return jnp.concatenate([ref[0, 2 * p], ref[0, 2 * p + 1]], axis=1)

        def put(p, val):
            dp_ref[:, p * PAIR:(p + 1) * PAIR] = val.astype(BF16)

        def unrope_norm(d_rot, p, g, dg_ref):
            dn = d_rot * cs + _pair_partner(d_rot * sn, first)
            xp = p_ref[:, p * PAIR:(p + 1) * PAIR]
            r = lax.rsqrt(_pair_mean(xp * xp, low) + EPS)
            n = xp * r
            gd = g * dn
            dg_ref[0:1, :] += jnp.sum(dn * n, axis=0, keepdims=True)
            put(p, r * (gd - n * _pair_mean(n * gd, low)))

        parts = iter(range(n_parts))

        def next_wgrad_part():
            c = next(parts, None)
            if c is not None:
                wgrad_part(c)

        for p in range(ha // 2):
            next_wgrad_part()
            unrope_norm(dqa_ref[:, p * PAIR:(p + 1) * PAIR] * SCALE, p, gq_ref[...], dgq_ref)
        base = ha // 2
        for p in range(kva // 2):
            next_wgrad_part()
            unrope_norm(grad_pair(dka_ref, p), base + p, gk_ref[...], dgk_ref)
            put(base + kva // 2 + p, grad_pair(dva_ref, p))
        base += kva
        for p in range(hb // 2):
            put(base + p, dqb_ref[:, p * PAIR:(p + 1) * PAIR] * SCALE)
        base += hb // 2
        for p in range(kvb // 2):
            put(base + p, grad_pair(dkb_ref, p))
            put(base + kvb // 2 + p, grad_pair(dvb_ref, p))
        for c in parts:
            wgrad_part(c)

        dpp_sc[...] = dp_ref[...]

        @pl.when((b == bl - 1) & (i == ns - 1))
        def _():
            gw_ref[...] = (gw_sc[...] + _dot_tn(dp_ref[...], h1_ref[...])).astype(BF16)

    def hm(nh):
        return pl.BlockSpec((1, nh, ts, HEAD_DIM), lambda b, i: (b, 0, i, 0))

    def tokmajor(nh):
        return pl.BlockSpec((ts, nh * HEAD_DIM), lambda b, i: (b * ns + i, 0))

    vec = pl.BlockSpec((1, PAIR), lambda b, i: (0, 0))
    tab = pl.BlockSpec((ts, PAIR), lambda b, i: (i, 0))
    acc = pl.BlockSpec((8, PAIR), lambda b, i: (0, 0))
    pspec = pl.BlockSpec((ts, p_cols), lambda b, i: (b * ns + i, 0))
    return pl.pallas_call(
        body,
        grid=(bl, ns),
        in_specs=[tokmajor(ha), hm(kva), hm(kva), tokmajor(hb), hm(kvb), hm(kvb), pspec,
                  pl.BlockSpec((ts, d), lambda b, i: (b * ns + i, 0)),
                  pl.BlockSpec((ts, d), lambda b, i: (jnp.maximum(b * ns + i - 1, 0), 0)), tab, tab, vec, vec],
        out_specs=[pspec, acc, acc, pl.BlockSpec((p_cols, d), lambda b, i: (0, 0))],
        out_shape=[SDS((t, p_cols), BF16), SDS((8, PAIR), F32), SDS((8, PAIR), F32), SDS((p_cols, d), BF16)],
        scratch_shapes=[pltpu.VMEM((p_cols, d), F32), pltpu.VMEM((ts, p_cols), BF16)],
        compiler_params=_cp("arbitrary", "arbitrary"),
        name="dqkprep",
    )(dqa, dka, dva, dqb, dkb, dvb, proj, h1, h1, cos, sin_signed, gq, gk)


def _dx_final(dproj, w_t, x2, dx1, g1, tm, grads):
    t, d = x2.shape
    p_cols = w_t.shape[0]
    ng = len(grads)
    nsteps = t // tm

    def body(dp_ref, w_ref, x_ref, dx1_ref, g_ref, *rest):
        grad_refs, (dx_ref, dg_ref), parts = rest[:ng], rest[ng:ng + 2], rest[ng + 2:2 * ng + 2]
        start, wait = _direct_exchange("scatter", grad_refs, parts, *rest[2 * ng + 2:])

        @pl.when(pl.program_id(0) == 0)
        def _():
            start()
            dg_ref[...] = jnp.zeros_like(dg_ref)

        dh = _dot(dp_ref[...], w_ref[...])
        g = g_ref[...]
        _, n, r = _rms_fwd(x_ref[...], g)
        dx, dgt = _rms_bwd(n, r, g, dh)
        dx_ref[...] = dx1_ref[...] + dx
        dg_ref[0:1, :] += jnp.sum(dgt, axis=0, keepdims=True)
        pl.when(pl.program_id(0) == nsteps - 1)(wait)

    tile = pl.BlockSpec((tm, d), lambda i: (i, 0))
    anyspec = pl.BlockSpec(memory_space=pl.ANY)
    res = pl.pallas_call(
        body,
        grid=(nsteps,),
        in_specs=[pl.BlockSpec((tm, p_cols), lambda i: (i, 0)),
                  pl.BlockSpec((p_cols, d), lambda i: (0, 0)),
                  tile, tile, pl.BlockSpec((1, d), lambda i: (0, 0))] + [anyspec] * ng,
        out_specs=[tile, pl.BlockSpec((8, d), lambda i: (0, 0))] + [anyspec] * ng,
        out_shape=[SDS((t, d), F32), SDS((8, d), F32)] + [SDS(g.shape, g.dtype) for g in grads],
        scratch_shapes=_exchange_scratch(ng),
        compiler_params=_cp("arbitrary"),
        name="dx_final",
    )(dproj, w_t, x2, dx1, g1, *grads)
    return res[0], res[1], res[2:]


def _adamw_math(w, g, m, v):
    m = ADAM_B1 * m + (1.0 - ADAM_B1) * g
    v = ADAM_B2 * v + (1.0 - ADAM_B2) * (g * g)
    m_hat = m / (1.0 - ADAM_B1 ** ADAM_STEP)
    v_hat = v / (1.0 - ADAM_B2 ** ADAM_STEP)
    delta = -ADAM_LR * (m_hat / (jnp.sqrt(v_hat) + ADAM_EPS) + ADAM_WD * w)
    return delta, m, v


def _adamw_sum(parts, ws, ms, vs, steps):
    nw = len(ws)

    def body(*refs):
        ins, outs = refs[:4 * nw], refs[4 * nw:]
        for k in range(nw):
            p_ref, w_ref, m_ref, v_ref = ins[4 * k:4 * k + 4]
            g_ref, d_ref, nm_ref, nv_ref = outs[4 * k:4 * k + 4]
            g = p_ref[0].astype(F32)
            for s in range(1, N_DEV):
                g = g + p_ref[s].astype(F32)
            g_ref[...] = g
            d_ref[...], nm_ref[...], nv_ref[...] = _adamw_math(w_ref[...], g, m_ref[...], v_ref[...])

    in_specs, out_specs, out_shape, args = [], [], [], []
    for p, w, m, v in zip(parts, ws, ms, vs):
        rows, cols = w.shape
        tile = pl.BlockSpec((rows // steps, cols), lambda i: (i, 0))
        in_specs += [pl.BlockSpec((N_DEV, rows // steps, cols), lambda i: (0, i, 0)), tile, tile, tile]
        out_specs += [tile] * 4
        out_shape += [SDS((rows, cols), F32)] * 4
        args += [p, w, m, v]
    res = pl.pallas_call(
        body,
        grid=(steps,),
        in_specs=in_specs,
        out_specs=out_specs,
        out_shape=out_shape,
        compiler_params=_cp("parallel"),
        name="adamw_weights",
    )(*args)
    return [res[4 * k:4 * k + 4] for k in range(nw)]


def _adamw_small(vec, rel, ws, ms, vs):
    hb = ws[6].shape[1]
    n = len(ws)

    def body(vec_ref, rel_ref, *rest):
        w_refs, m_refs, v_refs = rest[:n], rest[n:2 * n], rest[2 * n:3 * n]
        loss_ref, outs = rest[3 * n], rest[3 * n + 1:]
        grads = [vec_ref[0:1, :], vec_ref[1:2, :], vec_ref[2:3, :], vec_ref[3:4, :],
                 vec_ref[4:5, 0:HEAD_DIM], vec_ref[4:5, SMALL_LANES:SMALL_LANES + HEAD_DIM],
                 vec_ref[4:5, 2 * SMALL_LANES:2 * SMALL_LANES + hb], rel_ref[...].T[0:hb, :]]
        loss_ref[...] = vec_ref[4:5, 3 * SMALL_LANES:3 * SMALL_LANES + 1]
        for p, g in enumerate(grads):
            g_ref, d_ref, nm_ref, nv_ref = outs[4 * p:4 * p + 4]
            g_ref[...] = g
            d_ref[...], nm_ref[...], nv_ref[...] = _adamw_math(w_refs[p][...], g, m_refs[p][...], v_refs[p][...])

    vm = pl.BlockSpec(memory_space=pltpu.VMEM)
    res = pl.pallas_call(
        body,
        in_specs=[vm] * (2 + 3 * n),
        out_specs=[vm] * (1 + 4 * n),
        out_shape=[SDS((1, 1), F32)] + [SDS(w.shape, F32) for w in ws for _ in range(4)],
        name="adamw_small",
    )(vec, rel, *ws, *ms, *vs)
    return res[0], [res[1 + 4 * p:5 + 4 * p] for p in range(n)]


def _local_step(x, loss_target, win_s, wo_s, wup_s, wdn_s, g_pre_mix, g_post_mix, q_norm_a, k_norm_a, sink_b,
                rel_bias_t, g_pre_ffn, g_post_ffn):
    bl, s_len, d = x.shape
    t = bl * s_len
    nh = d // HEAD_DIM
    ha = nh // 2
    kva = ha // GROUP
    hb = nh - ha
    kvb = hb // GROUP
    tm = 512
    tp = min(1024, t)
    tw = min(4096, t)
    ts = min(512, s_len)
    tq, tk = 2 * BLOCK, min(512, s_len // 2)

    x2 = x.reshape(t, d)
    tg2 = loss_target.reshape(t, d)
    cos, sin_signed = _rope_tables(s_len)
    gq2, gk2 = jnp.tile(q_norm_a, (1, 2)), jnp.tile(k_norm_a, (1, 2))
    a = jnp.arange(BLOCK, dtype=jnp.int32)
    c = jnp.arange(SPAN, dtype=jnp.int32)
    bucket_t = _t5_bucket(c[:, None] - BLOCK - a[None, :])
    bucket_t4 = jnp.tile(bucket_t, (1, GROUP))
    (win_g,), bias_t = _weight_gather([win_s], bucket_t, rel_bias_t)
    w_in_t = win_g.reshape(-1, d)
    p_cols = w_in_t.shape[0]

    h1, proj, qa, ka, kat, va, vat, qb, kb, kbt, vb, vbt = _inproj_qkprep(
        x2, g_pre_mix, w_in_t, cos, sin_signed, gq2, gk2, bl, s_len, ha, kva, hb, kvb, ts)
    oa, lse_a, (wo_g, wup_g, wdn_g) = _attn_a_fwd(qa, ka, vat, tq, tk, [wo_s, wup_s, wdn_s])
    wo = wo_g.reshape(-1, d)
    wdn = wdn_g.reshape(-1, d)
    ob, lse_b = _attn_b_fwd(qb, kb, vbt, bias_t, sink_b, s_len)
    mix, x1, h2 = _mixout(oa, ob, wo, x2, g_post_mix, g_pre_ffn, tp)
    u, df, dy, dg4, loss8 = _ffn_fwd(h2, wup_g, wdn, x1, tg2, g_post_ffn, tm)

    dpre, dx1, dmix, dg3, dg2 = _ffn_bwd(df, u, wdn, wup_g, x1, dy, mix, g_pre_ffn, g_post_mix, FFN_BWD_TOKENS)
    gw_dn = _wgrad_rows(u, df, N_DEV, tw, "wgrad_down", square=True)
    gw_up = _wgrad_cols(h2, dpre, N_DEV, tw, "wgrad_up")
    gw_o = _wgrad_o(oa, ob, dmix, N_DEV, min(2048, t))
    doa, dob = _attn_out_bwd(dmix, wo, oa.shape[1], tp)
    dqa, dka, dva, (p_o, p_up, p_dn) = _attn_a_bwd(qa, ka, kat, va, doa, oa, lse_a, tq, tk, [gw_o, gw_up, gw_dn])
    dqb, dkb, dvb, dsum, dsink = _attn_b_bwd(qb, kb, kbt, vb, dob, ob, lse_b, bias_t, sink_b, s_len)
    drel_g, dsink_g = _bias_reduce(dsum, dsink, bucket_t4)
    dproj, dgq, dgk, gw_in_t = _dqkprep(dqa, dka, dva, dqb, dkb, dvb, proj, h1, cos, sin_signed, gq2, gk2, s_len, ts)
    gw_in_t = gw_in_t.reshape(N_DEV, -1, d)
    grad_x, dg1, (p_in,) = _dx_final(dproj, w_in_t, x2, dx1, g_pre_mix, tp, [gw_in_t])

    vec, rel = _small_allreduce([dg1, dg2, dg3, dg4], dgq, dgk, dsink_g, drel_g, loss8)
    return grad_x.reshape(bl, s_len, d), p_in, p_o, p_up, p_dn, vec, rel


def kernel(x, w_in, w_o, g_pre_mix, g_post_mix, q_norm_a, k_norm_a, sink_b, rel_bias, g_pre_ffn, w_ffn_up, w_ffn_down, g_post_ffn, loss_target, m_w_in, m_w_o, m_g_pre_mix, m_g_post_mix, m_q_norm_a, m_k_norm_a, m_sink_b, m_rel_bias, m_g_pre_ffn, m_w_ffn_up, m_w_ffn_down, m_g_post_ffn, v_w_in, v_w_o, v_g_pre_mix, v_g_post_mix, v_q_norm_a, v_k_norm_a, v_sink_b, v_rel_bias, v_g_pre_ffn, v_w_ffn_up, v_w_ffn_down, v_g_post_ffn):
    w_in_t = w_in[0].T
    rel_bias_t = rel_bias.T

    grad_x, p_in, p_o, p_up, p_dn, vec, rel = _local_step(
        x, loss_target, w_in_t.astype(BF16), w_o[0].astype(BF16), w_ffn_up[0].astype(BF16), w_ffn_down[0].astype(BF16),
        g_pre_mix, g_post_mix, q_norm_a, k_norm_a, sink_b, rel_bias_t, g_pre_ffn, g_post_ffn)

    r_in, r_o, r_up, r_dn = _adamw_sum(
        [p_in, p_o, p_up, p_dn],
        [w_in_t, w_o[0], w_ffn_up[0], w_ffn_down[0]],
        [m_w_in[0].T, m_w_o[0], m_w_ffn_up[0], m_w_ffn_down[0]],
        [v_w_in[0].T, v_w_o[0], v_w_ffn_up[0], v_w_ffn_down[0]], 4)
    big = {"w_in": [a.T for a in r_in], "w_o": r_o, "w_up": r_up, "w_dn": r_dn}
    loss, small = _adamw_small(
        vec, rel,
        [g_pre_mix, g_post_mix, g_pre_ffn, g_post_ffn, q_norm_a, k_norm_a, sink_b, rel_bias_t],
        [m_g_pre_mix, m_g_post_mix, m_g_pre_ffn, m_g_post_ffn, m_q_norm_a, m_k_norm_a, m_sink_b, m_rel_bias.T],
        [v_g_pre_mix, v_g_post_mix, v_g_pre_ffn, v_g_post_ffn, v_q_norm_a, v_k_norm_a, v_sink_b, v_rel_bias.T])
    s_pre_mix, s_post_mix, s_pre_ffn, s_post_ffn, s_qn, s_kn, s_sink, s_rel_t = small
    s_rel = [a.T for a in s_rel_t]

    def outs(kind):
        return [big["w_in"][kind][None], big["w_o"][kind][None], s_pre_mix[kind], s_post_mix[kind], s_qn[kind],
                s_kn[kind], s_sink[kind], s_rel[kind], s_pre_ffn[kind], big["w_up"][kind][None],
                big["w_dn"][kind][None], s_post_ffn[kind]]

    return (loss.reshape(()), grad_x, *outs(0), *outs(1), *outs(2), *outs(3))
```

```python
import functools

import jax
import jax.numpy as jnp
import numpy as np
from jax import lax
from jax.experimental import pallas as pl
from jax.experimental.pallas import tpu as pltpu

F32 = jnp.float32
BF16 = jnp.bfloat16
SDS = jax.ShapeDtypeStruct

N_DEV = 8
HEAD_DIM = 64
GROUP = 4
BLOCK = 128
SPAN = 3 * BLOCK
GRID_W = 64
N_BUCKETS = 32
MAX_DISTANCE = 128
ROPE_THETA = 10000.0
EPS = 1e-6
NEG_INF = -1e30
SCALE = HEAD_DIM ** -0.5
VT_PAD = 16

ADAM_LR = 0.001
ADAM_B1 = 0.9
ADAM_B2 = 0.999
ADAM_EPS = 1e-08
ADAM_WD = 0.01
ADAM_STEP = 10

VMEM_LIMIT = 56 * 1024 * 1024
MESH = pl.DeviceIdType.MESH


def _cp(*sem):
    return pltpu.CompilerParams(dimension_semantics=sem, vmem_limit_bytes=VMEM_LIMIT)


def _dot(a, b):
    return jnp.dot(a, b, preferred_element_type=F32)


def _dot_nt(a, b):
    return lax.dot_general(a, b, (((1,), (1,)), ((), ())), preferred_element_type=F32)


def _dot_tn(a, b):
    return lax.dot_general(a, b, (((0,), (0,)), ((), ())), preferred_element_type=F32)


def _rms_fwd(x, g):
    r = lax.rsqrt(jnp.mean(x * x, axis=-1, keepdims=True) + EPS)
    n = x * r
    return n * g, n, r


def _rms_bwd(n, r, g, dy):
    gd = g * dy
    dx = r * (gd - n * jnp.mean(n * gd, axis=-1, keepdims=True))
    return dx, dy * n


def _rope_tables(s_len):
    rows = s_len // GRID_W
    row = np.repeat(np.arange(rows, dtype=np.int32), GRID_W)
    col = np.tile(np.arange(GRID_W, dtype=np.int32), rows)
    nf = HEAD_DIM // 4
    freqs = np.float32(ROPE_THETA) ** (-np.arange(nf, dtype=np.float32) / np.float32(nf))
    ang_r = row.astype(np.float32)[:, None] * freqs[None, :]
    ang_c = col.astype(np.float32)[:, None] * freqs[None, :]
    cr, sr, cc, sc = np.cos(ang_r), np.sin(ang_r), np.cos(ang_c), np.sin(ang_c)
    cos = np.concatenate([cr, cr, cc, cc] * 2, axis=-1).astype(np.float32)
    sin_signed = np.concatenate([-sr, sr, -sc, sc] * 2, axis=-1).astype(np.float32)
    return jnp.asarray(cos), jnp.asarray(sin_signed)


def _t5_bucket(rel):
    nb = N_BUCKETS // 2
    ret = (rel > 0).astype(jnp.int32) * nb
    n = jnp.abs(rel)
    max_exact = nb // 2
    nf = jnp.maximum(n, 1).astype(F32)
    large = max_exact + (jnp.log(nf / max_exact) / np.float32(np.log(MAX_DISTANCE / max_exact))
                         * (nb - max_exact)).astype(jnp.int32)
    large = jnp.minimum(large, nb - 1)
    return ret + jnp.where(n < max_exact, n, large)


def _mesh_pos():
    return lax.axis_index("x"), lax.axis_index("y"), lax.axis_index("c")


def _lin(p):
    return 4 * p[0] + 2 * p[1] + p[2]


def _bias_tables(bkt_ref, tbl_ref, out_ref, hb):
    bkt = bkt_ref[...]
    ci = lax.broadcasted_iota(jnp.int32, (SPAN, BLOCK), 0)
    qi = lax.broadcasted_iota(jnp.int32, (SPAN, BLOCK), 1)
    band = jnp.abs(ci - BLOCK - qi) <= BLOCK
    masks = (band, band & (ci >= BLOCK), band & (ci < 2 * BLOCK))
    for h in range(hb):
        acct = jnp.zeros((SPAN, BLOCK), F32)
        for b in range(N_BUCKETS):
            acct = jnp.where(bkt == b, tbl_ref[h, b], acct)
        lanes = slice((h % GROUP) * BLOCK, (h % GROUP + 1) * BLOCK)
        for var, mask in enumerate(masks):
            out_ref[var, h // GROUP, :, lanes] = jnp.where(mask, acct, NEG_INF)


def _weight_gather(shards, bucket_t, rel_bias_t):
    n = len(shards)
    hb = rel_bias_t.shape[0]

    def body(*refs):
        xs, (bkt_ref, tbl_ref), outs, bias_ref = refs[:n], refs[n:n + 2], refs[n + 2:2 * n + 2], refs[2 * n + 2]
        send_sems, recv_sems, local_sems = refs[2 * n + 3:]
        x, y, c = _mesh_pos()
        me, sibling = (x, y, c), (x, y, 1 - c)
        chips = [(1 - x, y), (x, 1 - y), (1 - x, 1 - y)]

        def copy(a, k, block, to, src=None):
            slot = outs[a].at[_lin(block)]
            return pltpu.make_async_remote_copy(
                src_ref=slot if src is None else src, dst_ref=slot,
                send_sem=send_sems.at[a, k], recv_sem=recv_sems.at[a, k],
                device_id=to, device_id_type=MESH)

        started = []
        for a in range(n):
            mine = pltpu.make_async_copy(xs[a], outs[a].at[_lin(me)], local_sems.at[a])
            mine.start()
            started.append(mine)
        sends = []
        for a in range(n):
            first = [copy(a, 0, me, sibling, src=xs[a])]
            first += [copy(a, 1 + j, me, (*chip, c), src=xs[a]) for j, chip in enumerate(chips)]
            for cp in first:
                cp.start()
            sends += first
        _bias_tables(bkt_ref, tbl_ref, bias_ref, hb)
        for a in range(n):
            for j, chip in enumerate(chips):
                copy(a, 1 + j, (*chip, c), me).wait_recv()
                fwd = copy(a, 4 + j, (*chip, c), sibling)
                fwd.start()
                sends.append(fwd)
        for a in range(n):
            copy(a, 0, sibling, me).wait_recv()
            for j, chip in enumerate(chips):
                copy(a, 4 + j, (*chip, 1 - c), me).wait_recv()
        for cp in sends:
            cp.wait_send()
        for mine in started:
            mine.wait()

    anyspec = pl.BlockSpec(memory_space=pl.ANY)
    vm = pl.BlockSpec(memory_space=pltpu.VMEM)
    res = pl.pallas_call(
        body,
        out_shape=[SDS((N_DEV,) + s.shape, s.dtype) for s in shards]
        + [SDS((3, hb // GROUP, SPAN, GROUP * BLOCK), F32)],
        in_specs=[anyspec] * n + [vm, pl.BlockSpec(memory_space=pltpu.SMEM)],
        out_specs=[anyspec] * n + [vm],
        scratch_shapes=[pltpu.SemaphoreType.DMA((n, 7)), pltpu.SemaphoreType.DMA((n, 7)),
                        pltpu.SemaphoreType.DMA((n,))],
        name="weight_gather",
    )(*shards, bucket_t, rel_bias_t)
    return res[:n], res[n]


def _direct_exchange(kind, ins, outs, send_sems, recv_sems, local_sems):
    x, y, c = _mesh_pos()
    me = (x, y, c)
    peers = [(x, y, 1 - c), (1 - x, y, c), (x, 1 - y, c), (1 - x, 1 - y, c),
             (1 - x, y, 1 - c), (x, 1 - y, 1 - c), (1 - x, 1 - y, 1 - c)]

    def src(a, to):
        return ins[a] if kind == "gather" else ins[a].at[_lin(to)]

    def remote(a, k, to, frm):
        return pltpu.make_async_remote_copy(
            src_ref=src(a, to), dst_ref=outs[a].at[_lin(frm)],
            send_sem=send_sems.at[a, k], recv_sem=recv_sems.at[a, k],
            device_id=to, device_id_type=MESH)

    n = len(ins)
    sends = [remote(a, k, p, me) for a in range(n) for k, p in enumerate(peers)]
    arrivals = [remote(a, k, p, p) for a in range(n) for k, p in enumerate(peers)]
    local = [pltpu.make_async_copy(src(a, me), outs[a].at[_lin(me)], local_sems.at[a]) for a in range(n)]

    def start():
        for cp in local + sends:
            cp.start()

    def wait():
        for cp in arrivals:
            cp.wait_recv()
        for cp in sends:
            cp.wait_send()
        for cp in local:
            cp.wait()

    return start, wait


def _exchange_scratch(n):
    return [pltpu.SemaphoreType.DMA((n, 7)), pltpu.SemaphoreType.DMA((n, 7)), pltpu.SemaphoreType.DMA((n,))]


SMALL_LANES = 128


def _small_allreduce(g1_ref, g2_ref, g3_ref, g4_ref, gq_ref, gk_ref, sk_ref, rl_ref, ls_ref, vec_ref, rel_ref,
                     vbuf, rbuf, vland, rland, send_sems, recv_sems):
    kv = sk_ref.shape[0]
    x, y, c = _mesh_pos()
    me = (x, y, c)
    peers = [(x, y, 1 - c), (1 - x, y, c), (x, 1 - y, c), (1 - x, 1 - y, c),
             (1 - x, y, 1 - c), (x, 1 - y, 1 - c), (1 - x, 1 - y, 1 - c)]

    def copies(k, to, frm):
        return [pltpu.make_async_remote_copy(
            src_ref=buf, dst_ref=land.at[_lin(frm)], send_sem=send_sems.at[a, k], recv_sem=recv_sems.at[a, k],
            device_id=to, device_id_type=MESH) for a, (buf, land) in enumerate(((vbuf, vland), (rbuf, rland)))]

    sends = [cp for k, p in enumerate(peers) for cp in copies(k, p, me)]

    def start():
        vbuf[...] = jnp.zeros_like(vbuf)
        rbuf[...] = jnp.zeros_like(rbuf)
        for row, ref in enumerate((g1_ref, g2_ref, g3_ref, g4_ref)):
            vbuf[row:row + 1, :] = ref[0:1, :]
        vbuf[4:5, 0:HEAD_DIM] = gq_ref[0:1, 0:HEAD_DIM] + gq_ref[0:1, HEAD_DIM:PAIR]
        vbuf[4:5, SMALL_LANES:SMALL_LANES + HEAD_DIM] = gk_ref[0:1, 0:HEAD_DIM] + gk_ref[0:1, HEAD_DIM:PAIR]
        for g in range(kv):
            vbuf[4:5, 2 * SMALL_LANES + g * GROUP:2 * SMALL_LANES + (g + 1) * GROUP] = sk_ref[g, 0:1, 0:GROUP]
            rbuf[:, g * GROUP:(g + 1) * GROUP] = rl_ref[g, :, 0:GROUP]
        vbuf[4:5, 3 * SMALL_LANES:3 * SMALL_LANES + 1] = ls_ref[0:1, 0:1]
        for cp in sends:
            cp.start()
        vland[_lin(me)] = vbuf[...]
        rland[_lin(me)] = rbuf[...]

    def finish():
        for k, p in enumerate(peers):
            for cp in copies(k, p, p):
                cp.wait_recv()
        for cp in sends:
            cp.wait_send()
        vacc, racc = vland[0], rland[0]
        for s in range(1, N_DEV):
            vacc, racc = vacc + vland[s], racc + rland[s]
        vec_ref[...] = vacc
        rel_ref[...] = racc

    return start, finish


def _small_allreduce_scratch(d):
    return [pltpu.VMEM((8, d), F32), pltpu.VMEM((N_BUCKETS, 128), F32),
            pltpu.VMEM((N_DEV, 8, d), F32), pltpu.VMEM((N_DEV, N_BUCKETS, 128), F32),
            pltpu.SemaphoreType.DMA((2, 7)), pltpu.SemaphoreType.DMA((2, 7))]


PAIR = 2 * HEAD_DIM


def _pair_masks(ts):
    lane = lax.broadcasted_iota(jnp.int32, (ts, PAIR), 1)
    return lane < HEAD_DIM, (lane % 32) < 16


def _pair_mean(v, low):
    del low
    r = lax.broadcasted_iota(jnp.int32, (PAIR, PAIR), 0) // HEAD_DIM
    c = lax.broadcasted_iota(jnp.int32, (PAIR, PAIR), 1) // HEAD_DIM
    same_head = (r == c).astype(BF16)
    hi = v.astype(BF16)
    lo = (v - hi.astype(F32)).astype(BF16)
    return (_dot(hi, same_head) + _dot(lo, same_head)) * (1.0 / HEAD_DIM)


def _pair_partner(v, first):
    return jnp.where(first, pltpu.roll(v, PAIR - 16, 1), pltpu.roll(v, 16, 1))


def _inproj_qkprep(x2, g1, w_t, cos, sin_signed, gq, gk, bl, s_len, ha, kva, hb, kvb, ts):
    t, d = x2.shape
    p_cols = w_t.shape[0]
    assert ha % 2 == 0 and kva % 2 == 0 and hb % 2 == 0 and kvb % 2 == 0
    ns = s_len // ts
    nt = bl * ns
    sp = s_len + 2 * BLOCK

    def body(*refs):
        kb_ref, kbt_ref, vb_ref, vbt_ref, p_even, p_odd = refs[-6:]
        s = pl.program_id(0)
        i = lax.rem(jnp.maximum(s - 1, 0), ns)

        @pl.when(s == 0)
        def _():
            p_odd[...] = jnp.zeros_like(p_odd)

        @pl.when(i == 0)
        def _():
            zeros = jnp.zeros((kvb, BLOCK, HEAD_DIM), BF16)
            zeros_t = jnp.zeros((kvb, HEAD_DIM + VT_PAD, BLOCK), BF16)
            for ref in (kb_ref, vb_ref):
                ref[0, :, 0:BLOCK, :] = zeros
                ref[0, :, sp - BLOCK:sp, :] = zeros
            kbt_ref[0, :, :, 0:BLOCK] = zeros_t[:, 0:HEAD_DIM]
            kbt_ref[0, :, :, sp - BLOCK:sp] = zeros_t[:, 0:HEAD_DIM]
            vbt_ref[0, :, :, 0:BLOCK] = zeros_t
            vbt_ref[0, :, :, sp - BLOCK:sp] = zeros_t

        even = lax.rem(s, 2) == 0
        pl.when(even)(functools.partial(tile_work, p_even, p_odd, i, *refs[:-2]))
        pl.when(jnp.logical_not(even))(functools.partial(tile_work, p_odd, p_even, i, *refs[:-2]))

    def tile_work(p_new, p_ref, i, x_ref, g1_ref, w_ref, cos_ref, sin_ref, gq_ref, gk_ref, h_ref, po_ref, qa_ref,
                  ka_ref, kat_ref, va_ref, vat_ref, qb_ref, kb_ref, kbt_ref, vb_ref, vbt_ref):
        y, _, _ = _rms_fwd(x_ref[...], g1_ref[...])
        h = y.astype(BF16)
        h_ref[...] = h
        n_parts = 6
        pw = p_cols // n_parts

        def project(c):
            p_new[:, c * pw:(c + 1) * pw] = _dot_nt(h, w_ref[c * pw:(c + 1) * pw, :])

        cs, sn = cos_ref[...], sin_ref[...]
        low, first = _pair_masks(ts)
        ones_row = (lax.broadcasted_iota(jnp.int32, (VT_PAD, ts), 0) == 0).astype(BF16)
        heads = (slice(0, HEAD_DIM), slice(HEAD_DIM, PAIR))

        def pair(p):
            v = p_ref[:, p * PAIR:(p + 1) * PAIR]
            po_ref[:, p * PAIR:(p + 1) * PAIR] = v
            return v

        def normrope(x, g):
            y = x * lax.rsqrt(_pair_mean(x * x, low) + EPS) * g
            return y * cs + _pair_partner(y, first) * sn

        eye = (lax.broadcasted_iota(jnp.int32, (PAIR, PAIR), 0)
               == lax.broadcasted_iota(jnp.int32, (PAIR, PAIR), 1)).astype(BF16)

        def transposed(xb):
            return _dot_nt(eye, xb).astype(BF16)

        def prep_qa(p):
            qa_ref[:, p * PAIR:(p + 1) * PAIR] = (normrope(pair(p), gq_ref[...]) * SCALE).astype(BF16)

        def prep_kva(p):
            base = ha // 2
            k = normrope(pair(base + p), gk_ref[...]).astype(BF16)
            v = pair(base + kva // 2 + p).astype(BF16)
            kt, vt = transposed(k), transposed(v)
            for e, lanes in enumerate(heads):
                ka_ref[0, 2 * p + e] = k[:, lanes]
                va_ref[0, 2 * p + e] = v[:, lanes]
                kat_ref[0, 2 * p + e] = kt[lanes, :]
                vat_ref[0, 2 * p + e, 0:HEAD_DIM, :] = vt[lanes, :]
                vat_ref[0, 2 * p + e, HEAD_DIM:HEAD_DIM + VT_PAD, :] = ones_row

        def prep_qb(p):
            base = ha // 2 + kva
            qb_ref[:, p * PAIR:(p + 1) * PAIR] = (pair(base + p) * SCALE).astype(BF16)

        rows = pl.ds(pl.multiple_of(BLOCK + i * ts, BLOCK), ts)

        def prep_kvb(p):
            base = ha // 2 + kva + hb // 2
            k = pair(base + p).astype(BF16)
            v = pair(base + kvb // 2 + p).astype(BF16)
            kt, vt = transposed(k), transposed(v)
            for e, lanes in enumerate(heads):
                kb_ref[0, 2 * p + e, rows, :] = k[:, lanes]
                vb_ref[0, 2 * p + e, rows, :] = v[:, lanes]
                kbt_ref[0, 2 * p + e, :, rows] = kt[lanes, :]
                vbt_ref[0, 2 * p + e, 0:HEAD_DIM, rows] = vt[lanes, :]
                vbt_ref[0, 2 * p + e, HEAD_DIM:HEAD_DIM + VT_PAD, rows] = ones_row

        work = ([functools.partial(prep_qa, p) for p in range(ha // 2)]
                + [functools.partial(prep_kva, p) for p in range(kva // 2)]
                + [functools.partial(prep_qb, p) for p in range(hb // 2)]
                + [functools.partial(prep_kvb, p) for p in range(kvb // 2)])
        per_part = -(-len(work) // n_parts)
        for c in range(n_parts):
            for item in work[c * per_part:(c + 1) * per_part]:
                item()
            project(c)

    def cur(s):
        return jnp.minimum(s, nt - 1)

    def prev(s):
        return jnp.maximum(s - 1, 0) // ns, lax.rem(jnp.maximum(s - 1, 0), ns)

    def hm(nh):
        return pl.BlockSpec((1, nh, ts, HEAD_DIM), lambda s: (prev(s)[0], 0, prev(s)[1], 0))

    def hm_t(nh, rows):
        return pl.BlockSpec((1, nh, rows, ts), lambda s: (prev(s)[0], 0, 0, prev(s)[1]))

    def tokmajor(nh):
        return pl.BlockSpec((ts, nh * HEAD_DIM), lambda s: (jnp.maximum(s - 1, 0), 0))

    def padded(nh):
        return pl.BlockSpec((1, nh, sp, HEAD_DIM), lambda s: (prev(s)[0], 0, 0, 0))

    def padded_t(nh, rows):
        return pl.BlockSpec((1, nh, rows, sp), lambda s: (prev(s)[0], 0, 0, 0))

    tab = pl.BlockSpec((ts, PAIR), lambda s: (prev(s)[1], 0))
    vec = pl.BlockSpec((1, PAIR), lambda s: (0, 0))
    return pl.pallas_call(
        body,
        grid=(nt + 1,),
        in_specs=[pl.BlockSpec((ts, d), lambda s: (cur(s), 0)),
                  pl.BlockSpec((1, d), lambda s: (0, 0)),
                  pl.BlockSpec((p_cols, d), lambda s: (0, 0)),
                  tab, tab, vec, vec],
        out_specs=[pl.BlockSpec((ts, d), lambda s: (cur(s), 0)), tokmajor(p_cols // HEAD_DIM),
                   tokmajor(ha), hm(kva), hm_t(kva, HEAD_DIM), hm(kva), hm_t(kva, HEAD_DIM + VT_PAD),
                   tokmajor(hb), padded(kvb), padded_t(kvb, HEAD_DIM), padded(kvb),
                   padded_t(kvb, HEAD_DIM + VT_PAD)],
        out_shape=[SDS((t, d), BF16), SDS((t, p_cols), F32),
                   SDS((t, ha * HEAD_DIM), BF16), SDS((bl, kva, s_len, HEAD_DIM), BF16),
                   SDS((bl, kva, HEAD_DIM, s_len), BF16),
                   SDS((bl, kva, s_len, HEAD_DIM), BF16), SDS((bl, kva, HEAD_DIM + VT_PAD, s_len), BF16),
                   SDS((t, hb * HEAD_DIM), BF16),
                   SDS((bl, kvb, sp, HEAD_DIM), BF16), SDS((bl, kvb, HEAD_DIM, sp), BF16),
                   SDS((bl, kvb, sp, HEAD_DIM), BF16), SDS((bl, kvb, HEAD_DIM + VT_PAD, sp), BF16)],
        scratch_shapes=[pltpu.VMEM((ts, p_cols), F32)] * 2,
        compiler_params=_cp("arbitrary"),
        name="inproj_qkprep",
    )(x2, g1, w_t, cos, sin_signed, gq, gk)


def _attn_a_fwd(qa, ka, vat, tq, tk, shards):
    bl, kv, s_len, _ = ka.shape
    ha = qa.shape[1] // HEAD_DIM
    va_rows = vat.shape[2]
    nq, nk = s_len // tq, s_len // tk
    assert nk % 2 == 0
    r = GROUP * tq
    ns = len(shards)

    def body(q_ref, qn_ref, k_ref, v_ref, *rest):
        shard_refs, (o_ref, l_ref), gathered = rest[:ns], rest[ns:ns + 2], rest[ns + 2:2 * ns + 2]
        st_sc, send_sems, recv_sems, local_sems = rest[2 * ns + 2:]
        i = pl.program_id(2)
        step_id = (pl.program_id(0) * kv + pl.program_id(1)) * nq + i
        start, wait = _direct_exchange("gather", shard_refs, gathered, send_sems, recv_sems, local_sems)
        pl.when(step_id == 0)(start)

        q = _heads_t(q_ref[...]).astype(BF16)

        def scores(c, qv):
            return _dot(k_ref[0, 0, pl.ds(pl.multiple_of(c * tk, tk), tk), :], qv)

        def fold(st, c, carry):
            m_old, acc = carry
            m_new = jnp.maximum(m_old, jnp.max(st, axis=0, keepdims=True))
            pt = jnp.exp(st - m_new).astype(BF16)
            vt = v_ref[0, 0, :, pl.ds(pl.multiple_of(c * tk, tk), tk)]
            return m_new, jnp.exp(m_old - m_new) * acc + _dot(vt, pt)

        @pl.when(i == 0)
        def _():
            st_sc[0] = scores(0, q)

        def step(c2, carry):
            c = 2 * c2
            st_sc[1] = scores(c + 1, q)
            carry = fold(st_sc[0], c, carry)
            st_sc[0] = scores(c + 2, q)
            return fold(st_sc[1], c + 1, carry)

        carry = (jnp.full((1, r), -jnp.inf, F32), jnp.zeros((va_rows, r), F32))
        for c2 in range(nk // 2 - 1):
            carry = step(c2, carry)
        st_sc[1] = scores(nk - 1, q)
        carry = fold(st_sc[0], nk - 2, carry)
        st_sc[0] = scores(0, _heads_t(qn_ref[...]).astype(BF16))
        m, acc = fold(st_sc[1], nk - 1, carry)
        l = acc[HEAD_DIM:HEAD_DIM + 1, :]
        o_ref[...] = _heads_t_inv(acc[0:HEAD_DIM, :] / l).astype(BF16)
        l_ref[0, 0, 0] = jnp.broadcast_to(m + jnp.log(l), (8, r))
        pl.when(step_id == bl * kv * nq - 1)(wait)

    anyspec = pl.BlockSpec(memory_space=pl.ANY)
    res = pl.pallas_call(
        body,
        grid=(bl, kv, nq),
        in_specs=[pl.BlockSpec((tq, GROUP * HEAD_DIM), lambda b, g, i: (b * nq + i, g)),
                  pl.BlockSpec((tq, GROUP * HEAD_DIM), lambda b, g, i: (b * nq + jnp.minimum(i + 1, nq - 1), g)),
                  pl.BlockSpec((1, 1, s_len, HEAD_DIM), lambda b, g, i: (b, g, 0, 0)),
                  pl.BlockSpec((1, 1, va_rows, s_len), lambda b, g, i: (b, g, 0, 0))] + [anyspec] * ns,
        out_specs=[pl.BlockSpec((tq, GROUP * HEAD_DIM), lambda b, g, i: (b * nq + i, g)),
                   pl.BlockSpec((1, 1, 1, 8, r), lambda b, g, i: (b, g, i, 0, 0))] + [anyspec] * ns,
        out_shape=[SDS((bl * s_len, ha * HEAD_DIM), BF16), SDS((bl, kv, nq, 8, r), F32)]
        + [SDS((N_DEV,) + s.shape, s.dtype) for s in shards],
        scratch_shapes=[pltpu.VMEM((2, tk, r), F32)] + _exchange_scratch(ns),
        compiler_params=_cp("arbitrary", "arbitrary", "arbitrary"),
        name="attn_a_fwd",
    )(qa, qa, ka, vat, *shards)
    return res[0], res[1], res[2:]


FFN_BWD_TOKENS = 256
QB_PER_STEP = 16


def _bias_variant(n, nb):
    return jnp.where(n == 0, 1, jnp.where(n == nb - 1, 2, 0))


def _sink_row(sink_ref, g):
    return jnp.concatenate([jnp.full((1, BLOCK), sink_ref[0, g * GROUP + h], F32) for h in range(GROUP)], axis=1)


def _attn_b_fwd(qb, kb, vbt, bias_t, sink, s_len):
    bl, kv, sp, _ = kb.shape
    hb = qb.shape[1] // HEAD_DIM
    vt_rows = vbt.shape[2]
    nb = s_len // BLOCK
    nbs = min(QB_PER_STEP, nb)
    r = GROUP * BLOCK

    def body(q_ref, k_ref, vt_ref, bt_ref, sink_ref, o_ref, l_ref, st_sc, pb_sc):
        g, n0 = pl.program_id(1), pl.program_id(2) * nbs
        sink_row = _sink_row(sink_ref, g)

        def span(j):
            return pl.ds(pl.multiple_of((n0 + j) * BLOCK, BLOCK), SPAN)

        for j in range(nbs):
            qt = _heads_t(q_ref[j * BLOCK:(j + 1) * BLOCK, :]).astype(BF16)
            st_sc[j] = _dot(k_ref[0, 0, span(j), :], qt) + bt_ref[_bias_variant(n0 + j, nb), 0]
        maxes = []
        for j in range(nbs):
            st = st_sc[j]
            m = jnp.maximum(jnp.max(st, axis=0, keepdims=True), sink_row)
            pb_sc[j] = jnp.exp(st - m).astype(BF16)
            maxes.append(m)
        for j in range(nbs):
            m = maxes[j]
            acc = _dot(vt_ref[0, 0, :, span(j)], pb_sc[j])
            l = acc[HEAD_DIM:HEAD_DIM + 1, :] + jnp.exp(sink_row - m)
            o_ref[j * BLOCK:(j + 1) * BLOCK, :] = _heads_t_inv(acc[0:HEAD_DIM, :] / l).astype(BF16)
            l_ref[0, 0, j] = jnp.broadcast_to(m + jnp.log(l), (8, r))

    return pl.pallas_call(
        body,
        grid=(bl, kv, nb // nbs),
        in_specs=[pl.BlockSpec((nbs * BLOCK, GROUP * HEAD_DIM), lambda b, g, n: (b * (nb // nbs) + n, g)),
                  pl.BlockSpec((1, 1, sp, HEAD_DIM), lambda b, g, n: (b, g, 0, 0)),
                  pl.BlockSpec((1, 1, vt_rows, sp), lambda b, g, n: (b, g, 0, 0)),
                  pl.BlockSpec((3, 1, SPAN, r), lambda b, g, n: (0, g, 0, 0)),
                  pl.BlockSpec(memory_space=pltpu.SMEM)],
        out_specs=[pl.BlockSpec((nbs * BLOCK, GROUP * HEAD_DIM), lambda b, g, n: (b * (nb // nbs) + n, g)),
                   pl.BlockSpec((1, 1, nbs, 8, r), lambda b, g, n: (b, g, n, 0, 0))],
        out_shape=[SDS((bl * s_len, hb * HEAD_DIM), BF16), SDS((bl, kv, nb, 8, r), F32)],
        scratch_shapes=[pltpu.VMEM((nbs, SPAN, r), F32), pltpu.VMEM((nbs, SPAN, r), BF16)],
        compiler_params=_cp("parallel", "parallel", "arbitrary"),
        name="attn_b_fwd",
    )(qb, kb, vbt, bias_t, sink)


def _mixout(oa, ob, wo, x2, g2, g3, tm):
    t, d = x2.shape
    ca = oa.shape[1]

    def body(oa_ref, ob_ref, w_ref, x_ref, g2_ref, g3_ref, mix_ref, x1_ref, h2_ref):
        mix = _dot(oa_ref[...], w_ref[0:ca, :]) + _dot(ob_ref[...], w_ref[ca:, :])
        mix_ref[...] = mix
        y2, _, _ = _rms_fwd(mix, g2_ref[...])
        x1 = x_ref[...] + y2
        x1_ref[...] = x1
        y3, _, _ = _rms_fwd(x1, g3_ref[...])
        h2_ref[...] = y3.astype(BF16)

    tile = lambda w: pl.BlockSpec((tm, w), lambda i: (i, 0))
    vec = pl.BlockSpec((1, d), lambda i: (0, 0))
    return pl.pallas_call(
        body,
        grid=(t // tm,),
        in_specs=[tile(ca), tile(ob.shape[1]), pl.BlockSpec(wo.shape, lambda i: (0, 0)), tile(d), vec, vec],
        out_specs=[tile(d), tile(d), tile(d)],
        out_shape=[SDS((t, d), F32), SDS((t, d), F32), SDS((t, d), BF16)],
        compiler_params=_cp("parallel"),
        name="mixout",
    )(oa, ob, wo, x2, g2, g3)


def _ffn_fwd(h2, wup_g, wdn, x1, target, g4, tm):
    t, d = x1.shape
    nblk, _, tf = wup_g.shape
    ff = nblk * tf
    nt = t // tm

    def body(h_ref, wu_ref, wd_ref, x1_ref, tg_ref, g_ref, u_ref, df_ref, dy_ref, dg_ref, loss_ref, f_sc):
        s = pl.program_id(0)

        @pl.when(s == 0)
        def _():
            f_sc[...] = jnp.zeros_like(f_sc)
            dg_ref[...] = jnp.zeros_like(dg_ref)
            loss_ref[...] = jnp.zeros_like(loss_ref)

        counted = (s > 0).astype(F32)

        def loss_part():
            g = g_ref[...]
            y4, n, r = _rms_fwd(f_sc[...], g)
            e = (x1_ref[...] + y4) - tg_ref[...]
            loss_ref[...] += jnp.sum(e * e) * (0.5 / d) * counted
            dy = e * (1.0 / d)
            dy_ref[...] = dy
            return n, r, g, dy

        def norm_bwd_part(n, r, g, dy):
            df, dgt = _rms_bwd(n, r, g, dy)
            df_ref[...] = df.astype(BF16)
            dg_ref[0:1, :] += jnp.sum(dgt, axis=0, keepdims=True) * counted

        @pl.when(s < nt)
        def _():
            h = h_ref[...]
            squares = []
            saved = None
            for c in range(nblk):
                u = jnp.maximum(_dot(h, wu_ref[c]), 0.0)
                u_ref[:, c * tf:(c + 1) * tf] = u.astype(BF16)
                squares.append((u * u).astype(BF16))
                if c == 0:
                    saved = loss_part()
                elif c == 1:
                    norm_bwd_part(*saved)
            f_sc[...] = _dot(jnp.concatenate(squares, axis=1), wd_ref[...])

        @pl.when(s == nt)
        def _():
            norm_bwd_part(*loss_part())

    cur = lambda s: (jnp.minimum(s, nt - 1), 0)
    prev = lambda s: (jnp.maximum(s - 1, 0), 0)
    return pl.pallas_call(
        body,
        grid=(nt + 1,),
        in_specs=[pl.BlockSpec((tm, d), cur),
                  pl.BlockSpec((nblk, d, tf), lambda s: (0, 0, 0)),
                  pl.BlockSpec((ff, d), lambda s: (0, 0)),
                  pl.BlockSpec((tm, d), prev), pl.BlockSpec((tm, d), prev),
                  pl.BlockSpec((1, d), lambda s: (0, 0))],
        out_specs=[pl.BlockSpec((tm, ff), cur), pl.BlockSpec((tm, d), prev), pl.BlockSpec((tm, d), prev),
                   pl.BlockSpec((8, d), lambda s: (0, 0)),
                   pl.BlockSpec((8, 128), lambda s: (0, 0))],
        out_shape=[SDS((t, ff), BF16), SDS((t, d), BF16), SDS((t, d), F32), SDS((8, d), F32), SDS((8, 128), F32)],
        scratch_shapes=[pltpu.VMEM((tm, d), F32)],
        compiler_params=_cp("arbitrary"),
        name="ffn_fwd",
    )(h2, wup_g, wdn, x1, target, g4)


def _ffn_bwd(df, u, wdn, wup_g, x1, dy, mix, g3, g2, tm):
    t, d = x1.shape
    nblk, _, tf = wup_g.shape
    ff = nblk * tf
    nt = t // tm

    def body(df_ref, u_ref, wd_ref, wu_ref, x1_ref, dy_ref, mix_ref, g3_ref, g2_ref,
             dpre_ref, dx1_ref, dmix_ref, dg3_ref, dg2_ref, dh_sc):
        s = pl.program_id(0)

        @pl.when(s == 0)
        def _():
            dh_sc[...] = jnp.zeros_like(dh_sc)
            dg3_ref[...] = jnp.zeros_like(dg3_ref)
            dg2_ref[...] = jnp.zeros_like(dg2_ref)

        counted = (s > 0).astype(F32)

        def residual_norm_part():
            g3 = g3_ref[...]
            _, n3, r3 = _rms_fwd(x1_ref[...], g3)
            dx, dgt3 = _rms_bwd(n3, r3, g3, dh_sc[...])
            dx1 = dy_ref[...] + dx
            dx1_ref[...] = dx1
            dg3_ref[0:1, :] += jnp.sum(dgt3, axis=0, keepdims=True) * counted

        def mix_norm_part():
            g2 = g2_ref[...]
            _, n2, r2 = _rms_fwd(mix_ref[...], g2)
            dmix, dgt2 = _rms_bwd(n2, r2, g2, dx1_ref[...])
            dmix_ref[...] = dmix.astype(BF16)
            dg2_ref[0:1, :] += jnp.sum(dgt2, axis=0, keepdims=True) * counted

        @pl.when(s < nt)
        def _():
            du2 = _dot_nt(df_ref[...], wd_ref[...])
            dpre = (2.0 * u_ref[...].astype(F32) * du2).astype(BF16)
            dpre_ref[...] = dpre
            dh = _dot_nt(dpre[:, 0:tf], wu_ref[0])
            for c in range(1, nblk):
                if c == 1:
                    residual_norm_part()
                elif c == 3:
                    mix_norm_part()
                dh = dh + _dot_nt(dpre[:, c * tf:(c + 1) * tf], wu_ref[c])
            dh_sc[...] = dh

        @pl.when(s == nt)
        def _():
            residual_norm_part()
            mix_norm_part()

    cur = lambda s: (jnp.minimum(s, nt - 1), 0)
    prev = lambda s: (jnp.maximum(s - 1, 0), 0)
    vec = pl.BlockSpec((1, d), lambda s: (0, 0))
    acc8 = pl.BlockSpec((8, d), lambda s: (0, 0))
    return pl.pallas_call(
        body,
        grid=(nt + 1,),
        in_specs=[pl.BlockSpec((tm, d), cur),
                  pl.BlockSpec((tm, ff), cur),
                  pl.BlockSpec((ff, d), lambda s: (0, 0)),
                  pl.BlockSpec((nblk, d, tf), lambda s: (0, 0, 0)),
                  pl.BlockSpec((tm, d), prev), pl.BlockSpec((tm, d), prev), pl.BlockSpec((tm, d), prev), vec, vec],
        out_specs=[pl.BlockSpec((tm, ff), cur), pl.BlockSpec((tm, d), prev), pl.BlockSpec((tm, d), prev),
                   acc8, acc8],
        out_shape=[SDS(u.shape, BF16), SDS((t, d), F32), SDS((t, d), BF16), SDS((8, d), F32), SDS((8, d), F32)],
        scratch_shapes=[pltpu.VMEM((tm, d), F32)],
        compiler_params=_cp("arbitrary"),
        name="ffn_bwd",
    )(df, u, wdn, wup_g, x1, dy, mix, g3, g2)


def _wgrad(a, b, a_spec, b_spec, out_block, out_shape, nj, nk, name, prep_a=None, prep_b=None):
    acc_shape = out_block[1:]

    def body(a_ref, b_ref, o_ref, acc_sc):
        k = pl.program_id(1)
        av = a_ref[...] if prep_a is None else prep_a(a_ref)
        bv = b_ref[...] if prep_b is None else prep_b(b_ref)
        part = _dot_tn(av, bv)

        @pl.when(k == 0)
        def _():
            acc_sc[...] = part

        @pl.when(k > 0)
        def _():
            acc_sc[...] += part

        @pl.when(k == nk - 1)
        def _():
            o_ref[0] = acc_sc[...].astype(BF16)

    return pl.pallas_call(
        body,
        grid=(nj, nk),
        in_specs=[a_spec, b_spec],
        out_specs=pl.BlockSpec(out_block, lambda j, k: (j, 0, 0)),
        out_shape=SDS(out_shape, BF16),
        scratch_shapes=[pltpu.VMEM(acc_shape, F32)],
        compiler_params=_cp("parallel", "arbitrary"),
        name=name,
    )(a, b)


def _wgrad_cols(a, b, nj, tt, name):
    t, m = a.shape
    bn = b.shape[1] // nj
    return _wgrad(a, b, pl.BlockSpec((tt, m), lambda j, k: (k, 0)), pl.BlockSpec((tt, bn), lambda j, k: (k, j)),
                  (1, m, bn), (nj, m, bn), nj, t // tt, name)


def _wgrad_rows(a, b, nj, tt, name, square=False):
    t, n = b.shape
    bm = a.shape[1] // nj

    def squared(a_ref):
        af = a_ref[...].astype(F32)
        return (af * af).astype(BF16)

    return _wgrad(a, b, pl.BlockSpec((tt, bm), lambda j, k: (k, j)), pl.BlockSpec((tt, n), lambda j, k: (k, 0)),
                  (1, bm, n), (nj, bm, n), nj, t // tt, name, prep_a=squared if square else None)


def _wgrad_o(oa, ob, dmix, nj, tt):
    t, n = dmix.shape
    ca, cb = oa.shape[1], ob.shape[1]
    m = ca + cb
    nk = t // tt

    def body(oa_ref, ob_ref, b_ref, o_ref, acc_sc):
        k = pl.program_id(0)
        part = _dot_tn(jnp.concatenate([oa_ref[...], ob_ref[...]], axis=1), b_ref[...])

        @pl.when(k == 0)
        def _():
            acc_sc[...] = part

        @pl.when(k > 0)
        def _():
            acc_sc[...] += part

        @pl.when(k == nk - 1)
        def _():
            o_ref[...] = acc_sc[...].reshape(nj, m // nj, n).astype(BF16)

    return pl.pallas_call(
        body,
        grid=(nk,),
        in_specs=[pl.BlockSpec((tt, ca), lambda k: (k, 0)), pl.BlockSpec((tt, cb), lambda k: (k, 0)),
                  pl.BlockSpec((tt, n), lambda k: (k, 0))],
        out_specs=pl.BlockSpec((nj, m // nj, n), lambda k: (0, 0, 0)),
        out_shape=SDS((nj, m // nj, n), BF16),
        scratch_shapes=[pltpu.VMEM((m, n), F32)],
        compiler_params=_cp("arbitrary"),
        name="wgrad_o",
    )(oa, ob, dmix)


def _attn_out_bwd(dmix, wo, ca, tm):
    t, d = dmix.shape
    cb = wo.shape[0] - ca

    def body(dm_ref, w_ref, da_ref, db_ref):
        dm = dm_ref[...]
        da_ref[...] = _dot_nt(dm, w_ref[0:ca, :]).astype(BF16)
        db_ref[...] = _dot_nt(dm, w_ref[ca:, :]).astype(BF16)

    return pl.pallas_call(
        body,
        grid=(t // tm,),
        in_specs=[pl.BlockSpec((tm, d), lambda i: (i, 0)), pl.BlockSpec(wo.shape, lambda i: (0, 0))],
        out_specs=[pl.BlockSpec((tm, ca), lambda i: (i, 0)), pl.BlockSpec((tm, cb), lambda i: (i, 0))],
        out_shape=[SDS((t, ca), BF16), SDS((t, cb), BF16)],
        compiler_params=_cp("parallel"),
        name="attn_out_bwd",
    )(dmix, wo)


def _heads_t(x):
    xt = x.astype(F32).T
    return jnp.concatenate([xt[h * HEAD_DIM:(h + 1) * HEAD_DIM, :] for h in range(GROUP)], axis=1)


def _heads_t_inv(yt):
    n = yt.shape[1] // GROUP
    return jnp.concatenate([yt[:, h * n:(h + 1) * n] for h in range(GROUP)], axis=0).T


def _attn_a_bwd(qa, ka, kat, va, do, o, lse, tq, tk, grads):
    bl, kv, s_len, _ = ka.shape
    nq, nk = s_len // tq, s_len // tk
    assert nk % 2 == 0
    r = GROUP * tq
    ng = len(grads)

    def body(q_ref, qn_ref, k_ref, kt_ref, v_ref, do_ref, don_ref, o_ref, l_ref, *rest):
        grad_refs, (dq_ref, dk_ref, dv_ref), parts = rest[:ng], rest[ng:ng + 3], rest[ng + 3:2 * ng + 3]
        st_sc, dp_sc, dkt_sc, dvt_sc, send_sems, recv_sems, local_sems = rest[2 * ng + 3:]
        i = pl.program_id(2)
        step_id = (pl.program_id(0) * kv + pl.program_id(1)) * nq + i
        start, wait = _direct_exchange("scatter", grad_refs, parts, send_sems, recv_sems, local_sems)
        pl.when(step_id == 0)(start)

        dot32 = _heads_t(do_ref[...])
        drow = jnp.sum(dot32 * _heads_t(o_ref[...]), axis=0, keepdims=True)
        qt, dot = _heads_t(q_ref[...]).astype(BF16), dot32.astype(BF16)
        lrow = l_ref[0, 0, 0, 0:1, :]

        @pl.when(i == 0)
        def _():
            dkt_sc[...] = jnp.zeros_like(dkt_sc)
            dvt_sc[...] = jnp.zeros_like(dvt_sc)

        def chunk(c):
            return pl.ds(pl.multiple_of(c * tk, tk), tk)

        def scores(c, slot, qv=qt, dov=dot):
            st_sc[slot] = _dot(k_ref[0, 0, chunk(c), :], qv)
            dp_sc[slot] = _dot(v_ref[0, 0, chunk(c), :], dov)

        def fold(slot, c, dqt):
            pt = jnp.exp(st_sc[slot] - lrow)
            dsb = (pt * (dp_sc[slot] - drow)).astype(BF16)
            dvt_sc[:, chunk(c)] += _dot_nt(dot, pt.astype(BF16))
            dkt_sc[:, chunk(c)] += _dot_nt(qt, dsb)
            return dqt + _dot(kt_ref[0, 0, :, chunk(c)], dsb)

        @pl.when(i == 0)
        def _():
            scores(0, 0)

        def step(c2, dqt):
            c = 2 * c2
            scores(c + 1, 1)
            dqt = fold(0, c, dqt)
            scores(c + 2, 0)
            return fold(1, c + 1, dqt)

        dqt = jnp.zeros((HEAD_DIM, r), F32)
        for c2 in range(nk // 2 - 1):
            dqt = step(c2, dqt)
        scores(nk - 1, 1)
        dqt = fold(0, nk - 2, dqt)
        scores(0, 0, _heads_t(qn_ref[...]).astype(BF16), _heads_t(don_ref[...]).astype(BF16))
        dq_ref[...] = _heads_t_inv(fold(1, nk - 1, dqt))

        @pl.when(i == nq - 1)
        def _():
            dk_ref[0, 0] = dkt_sc[...].T
            dv_ref[0, 0] = dvt_sc[...].T

        pl.when(step_id == bl * kv * nq - 1)(wait)

    kvspec = pl.BlockSpec((1, 1, s_len, HEAD_DIM), lambda b, g, i: (b, g, 0, 0))
    tok = pl.BlockSpec((tq, GROUP * HEAD_DIM), lambda b, g, i: (b * nq + i, g))
    toknext = pl.BlockSpec((tq, GROUP * HEAD_DIM), lambda b, g, i: (b * nq + jnp.minimum(i + 1, nq - 1), g))
    anyspec = pl.BlockSpec(memory_space=pl.ANY)
    res = pl.pallas_call(
        body,
        grid=(bl, kv, nq),
        in_specs=[tok, toknext, kvspec, pl.BlockSpec((1, 1, HEAD_DIM, s_len), lambda b, g, i: (b, g, 0, 0)), kvspec,
                  tok, toknext, tok, pl.BlockSpec((1, 1, 1, 8, r), lambda b, g, i: (b, g, i, 0, 0))] + [anyspec] * ng,
        out_specs=[tok, kvspec, kvspec] + [anyspec] * ng,
        out_shape=[SDS(qa.shape, F32), SDS(ka.shape, F32), SDS(va.shape, F32)]
        + [SDS(g.shape, g.dtype) for g in grads],
        scratch_shapes=[pltpu.VMEM((2, tk, r), F32), pltpu.VMEM((2, tk, r), F32),
                        pltpu.VMEM((HEAD_DIM, s_len), F32), pltpu.VMEM((HEAD_DIM, s_len), F32)]
        + _exchange_scratch(ng),
        compiler_params=_cp("arbitrary", "arbitrary", "arbitrary"),
        name="attn_a_bwd",
    )(qa, qa, ka, kat, va, do, do, o, lse, *grads)
    return res[0], res[1], res[2], res[3:]


def _attn_b_bwd(qb, kb, kbt, vb, do, o, lse, bias_t, sink, s_len):
    bl, kv, sp, _ = kb.shape
    nb = s_len // BLOCK
    nbs = min(QB_PER_STEP, nb)
    r = GROUP * BLOCK

    def body(q_ref, k_ref, kt_ref, v_ref, do_ref, o_ref, l_ref, bt_ref, sink_ref,
             dq_ref, dk_ref, dv_ref, dsum_ref, dsink_ref, dkt_sc, dvt_sc):
        g, b, ns = pl.program_id(0), pl.program_id(1), pl.program_id(2)
        sink_row = _sink_row(sink_ref, g)

        @pl.when(ns == 0)
        def _():
            dkt_sc[...] = jnp.zeros_like(dkt_sc)
            dvt_sc[...] = jnp.zeros_like(dvt_sc)

        @pl.when((b == 0) & (ns == 0))
        def _():
            dsum_ref[...] = jnp.zeros_like(dsum_ref)
            dsink_ref[...] = jnp.zeros_like(dsink_ref)

        dsum = jnp.zeros((SPAN, r), F32)
        dsink = jnp.zeros((1, r), F32)
        for j in range(nbs):
            n = ns * nbs + j
            span = pl.ds(pl.multiple_of(n * BLOCK, BLOCK), SPAN)
            rows = slice(j * BLOCK, (j + 1) * BLOCK)
            dot32 = _heads_t(do_ref[rows, :])
            drow = jnp.sum(dot32 * _heads_t(o_ref[rows, :]), axis=0, keepdims=True)
            qt, dot = _heads_t(q_ref[rows, :]).astype(BF16), dot32.astype(BF16)
            lrow = l_ref[0, 0, j, 0:1, :]
            st = _dot(k_ref[0, 0, span, :], qt) + bt_ref[_bias_variant(n, nb), 0]
            pt = jnp.exp(st - lrow)
            dst = pt * (_dot(v_ref[0, 0, span, :], dot) - drow)
            dsum = dsum + dst
            dsink = dsink - jnp.exp(sink_row - lrow) * drow
            dsb = dst.astype(BF16)
            dvt_sc[:, span] += _dot_nt(dot, pt.astype(BF16))
            dkt_sc[:, span] += _dot_nt(qt, dsb)
            dq_ref[rows, :] = _heads_t_inv(_dot(kt_ref[0, 0, :, span], dsb))
        dsum_ref[0] += dsum
        dsink_ref[0, 0:1, :] += dsink

        @pl.when(ns == nb // nbs - 1)
        def _():
            dk_ref[0, 0] = dkt_sc[:, BLOCK:BLOCK + s_len].T
            dv_ref[0, 0] = dvt_sc[:, BLOCK:BLOCK + s_len].T

    kvspec = pl.BlockSpec((1, 1, sp, HEAD_DIM), lambda g, b, n: (b, g, 0, 0))
    kvout = pl.BlockSpec((1, 1, s_len, HEAD_DIM), lambda g, b, n: (b, g, 0, 0))
    tok = pl.BlockSpec((nbs * BLOCK, GROUP * HEAD_DIM), lambda g, b, n: (b * (nb // nbs) + n, g))
    return pl.pallas_call(
        body,
        grid=(kv, bl, nb // nbs),
        in_specs=[tok, kvspec, pl.BlockSpec((1, 1, HEAD_DIM, sp), lambda g, b, n: (b, g, 0, 0)), kvspec, tok, tok,
                  pl.BlockSpec((1, 1, nbs, 8, r), lambda g, b, n: (b, g, n, 0, 0)),
                  pl.BlockSpec((3, 1, SPAN, r), lambda g, b, n: (0, g, 0, 0)),
                  pl.BlockSpec(memory_space=pltpu.SMEM)],
        out_specs=[tok, kvout, kvout,
                   pl.BlockSpec((1, SPAN, r), lambda g, b, n: (g, 0, 0)),
                   pl.BlockSpec((1, 8, r), lambda g, b, n: (g, 0, 0))],
        out_shape=[SDS(qb.shape, F32), SDS((bl, kv, s_len, HEAD_DIM), F32), SDS((bl, kv, s_len, HEAD_DIM), F32),
                   SDS((kv, SPAN, r), F32), SDS((kv, 8, r), F32)],
        scratch_shapes=[pltpu.VMEM((HEAD_DIM, sp), F32), pltpu.VMEM((HEAD_DIM, sp), F32)],
        compiler_params=_cp("arbitrary", "arbitrary", "arbitrary"),
        name="attn_b_bwd",
    )(qb, kb, kbt, vb, do, o, lse, bias_t, sink)


def _bias_reduce(dsum, dsink, bucket_t4):
    kv, _, r = dsum.shape

    def body(ds_ref, dk_ref, bk_ref, rel_ref, sink_ref):
        lane = lax.broadcasted_iota(jnp.int32, (N_BUCKETS, 128), 1)
        lane8 = lax.broadcasted_iota(jnp.int32, (8, 128), 1)
        bk = bk_ref[...]
        for g in range(kv):
            ds = ds_ref[g]
            rowi = lax.broadcasted_iota(jnp.int32, (N_BUCKETS, r), 0)
            red = jnp.zeros((N_BUCKETS, r), F32)
            for b in range(N_BUCKETS):
                red = jnp.where(rowi == b, jnp.sum(jnp.where(bk == b, ds, 0.0), axis=0, keepdims=True), red)
            out = jnp.zeros((N_BUCKETS, 128), F32)
            so = jnp.zeros((8, 128), F32)
            for h in range(GROUP):
                col = jnp.sum(red[:, h * BLOCK:(h + 1) * BLOCK], axis=1, keepdims=True)
                out = jnp.where(lane == h, col, out)
                sc = jnp.sum(dk_ref[g][:, h * BLOCK:(h + 1) * BLOCK], axis=1, keepdims=True)
                so = jnp.where(lane8 == h, sc, so)
            rel_ref[g] = out
            sink_ref[g] = so

    vm = pl.BlockSpec(memory_space=pltpu.VMEM)
    return pl.pallas_call(
        body,
        in_specs=[vm, vm, vm],
        out_specs=[vm, vm],
        out_shape=[SDS((kv, N_BUCKETS, 128), F32), SDS((kv, 8, 128), F32)],
        name="bias_reduce",
    )(dsum, dsink, bucket_t4)


def _dqkprep(dqa, dka, dva, dqb, dkb, dvb, proj, h1, cos, sin_signed, gq, gk, s_len, ts):
    t, p_cols = proj.shape
    d = h1.shape[1]
    bl, kva, kvb = dka.shape[0], dka.shape[1], dkb.shape[1]
    ha, hb = dqa.shape[1] // HEAD_DIM, dqb.shape[1] // HEAD_DIM
    ns = s_len // ts

    def body(dqa_ref, dka_ref, dva_ref, dqb_ref, dkb_ref, dvb_ref, p_ref, h1_ref, h1p_ref, cos_ref, sin_ref,
             gq_ref, gk_ref, dp_ref, dgq_ref, dgk_ref, gw_ref, gw_sc, dpp_sc):
        b, i = pl.program_id(0), pl.program_id(1)
        cs, sn = cos_ref[...], sin_ref[...]
        low, first = _pair_masks(ts)

        @pl.when((b == 0) & (i == 0))
        def _():
            dgq_ref[...] = jnp.zeros_like(dgq_ref)
            dgk_ref[...] = jnp.zeros_like(dgk_ref)
            gw_sc[...] = jnp.zeros_like(gw_sc)
            dpp_sc[...] = jnp.zeros_like(dpp_sc)

        n_parts = 6
        pw = p_cols // n_parts

        def wgrad_part(c):
            rows = slice(c * pw, (c + 1) * pw)
            gw_sc[rows, :] += _dot_tn(dpp_sc[:, rows], h1p_ref[...])

        def grad_pair(ref, p):
            return jnp.concatenate([ref[0, 2 * p], ref[0, 2 * p + 1]], axis=1)

        def put(p, val):
            dp_ref[:, p * PAIR:(p + 1) * PAIR] = val.astype(BF16)

        def unrope_norm(d_rot, p, g, dg_ref):
            dn = d_rot * cs + _pair_partner(d_rot * sn, first)
            xp = p_ref[:, p * PAIR:(p + 1) * PAIR]
            r = lax.rsqrt(_pair_mean(xp * xp, low) + EPS)
            n = xp * r
            gd = g * dn
            dg_ref[0:1, :] += jnp.sum(dn * n, axis=0, keepdims=True)
            put(p, r * (gd - n * _pair_mean(n * gd, low)))

        parts = iter(range(n_parts))

        def next_wgrad_part():
            c = next(parts, None)
            if c is not None:
                wgrad_part(c)

        for p in range(ha // 2):
            next_wgrad_part()
            unrope_norm(dqa_ref[:, p * PAIR:(p + 1) * PAIR] * SCALE, p, gq_ref[...], dgq_ref)
        base = ha // 2
        for p in range(kva // 2):
            next_wgrad_part()
            unrope_norm(grad_pair(dka_ref, p), base + p, gk_ref[...], dgk_ref)
            put(base + kva // 2 + p, grad_pair(dva_ref, p))
        base += kva
        for p in range(hb // 2):
            put(base + p, dqb_ref[:, p * PAIR:(p + 1) * PAIR] * SCALE)
        base += hb // 2
        for p in range(kvb // 2):
            put(base + p, grad_pair(dkb_ref, p))
            put(base + kvb // 2 + p, grad_pair(dvb_ref, p))
        for c in parts:
            wgrad_part(c)

        dpp_sc[...] = dp_ref[...]

        @pl.when((b == bl - 1) & (i == ns - 1))
        def _():
            gw_ref[...] = (gw_sc[...] + _dot_tn(dp_ref[...], h1_ref[...])).astype(BF16)

    def hm(nh):
        return pl.BlockSpec((1, nh, ts, HEAD_DIM), lambda b, i: (b, 0, i, 0))

    def tokmajor(nh):
        return pl.BlockSpec((ts, nh * HEAD_DIM), lambda b, i: (b * ns + i, 0))

    vec = pl.BlockSpec((1, PAIR), lambda b, i: (0, 0))
    tab = pl.BlockSpec((ts, PAIR), lambda b, i: (i, 0))
    acc = pl.BlockSpec((8, PAIR), lambda b, i: (0, 0))
    pspec = pl.BlockSpec((ts, p_cols), lambda b, i: (b * ns + i, 0))
    return pl.pallas_call(
        body,
        grid=(bl, ns),
        in_specs=[tokmajor(ha), hm(kva), hm(kva), tokmajor(hb), hm(kvb), hm(kvb), pspec,
                  pl.BlockSpec((ts, d), lambda b, i: (b * ns + i, 0)),
                  pl.BlockSpec((ts, d), lambda b, i: (jnp.maximum(b * ns + i - 1, 0), 0)), tab, tab, vec, vec],
        out_specs=[pspec, acc, acc, pl.BlockSpec((p_cols, d), lambda b, i: (0, 0))],
        out_shape=[SDS((t, p_cols), BF16), SDS((8, PAIR), F32), SDS((8, PAIR), F32), SDS((p_cols, d), BF16)],
        scratch_shapes=[pltpu.VMEM((p_cols, d), F32), pltpu.VMEM((ts, p_cols), BF16)],
        compiler_params=_cp("arbitrary", "arbitrary"),
        name="dqkprep",
    )(dqa, dka, dva, dqb, dkb, dvb, proj, h1, h1, cos, sin_signed, gq, gk)


def _dx_final(dproj, w_t, x2, dx1, g1, tm, grads, small):
    t, d = x2.shape
    p_cols = w_t.shape[0]
    ng, nsm = len(grads), len(small)
    nsteps = t // tm

    def body(dp_ref, w_ref, x_ref, dx1_ref, g_ref, *rest):
        small_refs, grad_refs = rest[:nsm], rest[nsm:nsm + ng]
        dx_ref, vec_ref, rel_ref = rest[nsm + ng:nsm + ng + 3]
        parts = rest[nsm + ng + 3:nsm + 2 * ng + 3]
        sems, dg_sc, small_scratch = rest[nsm + 2 * ng + 3:nsm + 2 * ng + 6], rest[nsm + 2 * ng + 6], rest[nsm + 2 * ng + 7:]
        start, wait = _direct_exchange("scatter", grad_refs, parts, *sems)
        start_small, finish_small = _small_allreduce(dg_sc, *small_refs, vec_ref, rel_ref, *small_scratch)

        @pl.when(pl.program_id(0) == 0)
        def _():
            start()
            dg_sc[...] = jnp.zeros_like(dg_sc)

        dh = _dot(dp_ref[...], w_ref[...])
        g = g_ref[...]
        _, n, r = _rms_fwd(x_ref[...], g)
        dx, dgt = _rms_bwd(n, r, g, dh)
        dx_ref[...] = dx1_ref[...] + dx
        dg_sc[0:1, :] += jnp.sum(dgt, axis=0, keepdims=True)

        @pl.when(pl.program_id(0) == nsteps - 1)
        def _():
            start_small()
            wait()
            finish_small()

    tile = pl.BlockSpec((tm, d), lambda i: (i, 0))
    anyspec = pl.BlockSpec(memory_space=pl.ANY)

    def whole(a):
        return pl.BlockSpec(a.shape, lambda i: (0,) * a.ndim)

    vec_shape, rel_shape = SDS((8, d), F32), SDS((N_BUCKETS, 128), F32)
    res = pl.pallas_call(
        body,
        grid=(nsteps,),
        in_specs=[pl.BlockSpec((tm, p_cols), lambda i: (i, 0)),
                  pl.BlockSpec((p_cols, d), lambda i: (0, 0)),
                  tile, tile, pl.BlockSpec((1, d), lambda i: (0, 0))] + [whole(a) for a in small] + [anyspec] * ng,
        out_specs=[tile, whole(vec_shape), whole(rel_shape)] + [anyspec] * ng,
        out_shape=[SDS((t, d), F32), vec_shape, rel_shape] + [SDS(g.shape, g.dtype) for g in grads],
        scratch_shapes=_exchange_scratch(ng) + [pltpu.VMEM((8, d), F32)] + _small_allreduce_scratch(d),
        compiler_params=_cp("arbitrary"),
        name="dx_final",
    )(dproj, w_t, x2, dx1, g1, *small, *grads)
    return res[0], res[1], res[2], res[3:]


def _adamw_math(w, g, m, v):
    m = ADAM_B1 * m + (1.0 - ADAM_B1) * g
    v = ADAM_B2 * v + (1.0 - ADAM_B2) * (g * g)
    m_hat = m / (1.0 - ADAM_B1 ** ADAM_STEP)
    v_hat = v / (1.0 - ADAM_B2 ** ADAM_STEP)
    delta = -ADAM_LR * (m_hat / (jnp.sqrt(v_hat) + ADAM_EPS) + ADAM_WD * w)
    return delta, m, v


def _adamw_sum(parts, ws, ms, vs, steps):
    nw = len(ws)

    def body(*refs):
        ins, outs = refs[:4 * nw], refs[4 * nw:]
        for k in range(nw):
            p_ref, w_ref, m_ref, v_ref = ins[4 * k:4 * k + 4]
            g_ref, d_ref, nm_ref, nv_ref = outs[4 * k:4 * k + 4]
            g = p_ref[0].astype(F32)
            for s in range(1, N_DEV):
                g = g + p_ref[s].astype(F32)
            g_ref[...] = g
            d_ref[...], nm_ref[...], nv_ref[...] = _adamw_math(w_ref[...], g, m_ref[...], v_ref[...])

    in_specs, out_specs, out_shape, args = [], [], [], []
    for p, w, m, v in zip(parts, ws, ms, vs):
        rows, cols = w.shape
        tile = pl.BlockSpec((rows // steps, cols), lambda i: (i, 0))
        in_specs += [pl.BlockSpec((N_DEV, rows // steps, cols), lambda i: (0, i, 0)), tile, tile, tile]
        out_specs += [tile] * 4
        out_shape += [SDS((rows, cols), F32)] * 4
        args += [p, w, m, v]
    res = pl.pallas_call(
        body,
        grid=(steps,),
        in_specs=in_specs,
        out_specs=out_specs,
        out_shape=out_shape,
        compiler_params=_cp("parallel"),
        name="adamw_weights",
    )(*args)
    return [res[4 * k:4 * k + 4] for k in range(nw)]


def _adamw_small(vec, rel, ws, ms, vs):
    hb = ws[6].shape[1]
    n = len(ws)

    def body(vec_ref, rel_ref, *rest):
        w_refs, m_refs, v_refs = rest[:n], rest[n:2 * n], rest[2 * n:3 * n]
        loss_ref, outs = rest[3 * n], rest[3 * n + 1:]
        grads = [vec_ref[0:1, :], vec_ref[1:2, :], vec_ref[2:3, :], vec_ref[3:4, :],
                 vec_ref[4:5, 0:HEAD_DIM], vec_ref[4:5, SMALL_LANES:SMALL_LANES + HEAD_DIM],
                 vec_ref[4:5, 2 * SMALL_LANES:2 * SMALL_LANES + hb], rel_ref[...].T[0:hb, :]]
        loss_ref[...] = vec_ref[4:5, 3 * SMALL_LANES:3 * SMALL_LANES + 1]
        for p, g in enumerate(grads):
            g_ref, d_ref, nm_ref, nv_ref = outs[4 * p:4 * p + 4]
            g_ref[...] = g
            d_ref[...], nm_ref[...], nv_ref[...] = _adamw_math(w_refs[p][...], g, m_refs[p][...], v_refs[p][...])

    vm = pl.BlockSpec(memory_space=pltpu.VMEM)
    res = pl.pallas_call(
        body,
        in_specs=[vm] * (2 + 3 * n),
        out_specs=[vm] * (1 + 4 * n),
        out_shape=[SDS((1, 1), F32)] + [SDS(w.shape, F32) for w in ws for _ in range(4)],
        name="adamw_small",
    )(vec, rel, *ws, *ms, *vs)
    return res[0], [res[1 + 4 * p:5 + 4 * p] for p in range(n)]


def _local_step(x, loss_target, win_s, wo_s, wup_s, wdn_s, g_pre_mix, g_post_mix, q_norm_a, k_norm_a, sink_b,
                rel_bias_t, g_pre_ffn, g_post_ffn):
    bl, s_len, d = x.shape
    t = bl * s_len
    nh = d // HEAD_DIM
    ha = nh // 2
    kva = ha // GROUP
    hb = nh - ha
    kvb = hb // GROUP
    tm = 512
    tp = min(1024, t)
    tw = min(4096, t)
    ts = min(512, s_len)
    tq, tk = 2 * BLOCK, min(512, s_len // 2)

    x2 = x.reshape(t, d)
    tg2 = loss_target.reshape(t, d)
    cos, sin_signed = _rope_tables(s_len)
    gq2, gk2 = jnp.tile(q_norm_a, (1, 2)), jnp.tile(k_norm_a, (1, 2))
    a = jnp.arange(BLOCK, dtype=jnp.int32)
    c = jnp.arange(SPAN, dtype=jnp.int32)
    bucket_t = _t5_bucket(c[:, None] - BLOCK - a[None, :])
    bucket_t4 = jnp.tile(bucket_t, (1, GROUP))
    (win_g,), bias_t = _weight_gather([win_s], bucket_t, rel_bias_t)
    w_in_t = win_g.reshape(-1, d)
    p_cols = w_in_t.shape[0]

    h1, proj, qa, ka, kat, va, vat, qb, kb, kbt, vb, vbt = _inproj_qkprep(
        x2, g_pre_mix, w_in_t, cos, sin_signed, gq2, gk2, bl, s_len, ha, kva, hb, kvb, ts)
    oa, lse_a, (wo_g, wup_g, wdn_g) = _attn_a_fwd(qa, ka, vat, tq, tk, [wo_s, wup_s, wdn_s])
    wo = wo_g.reshape(-1, d)
    wdn = wdn_g.reshape(-1, d)
    ob, lse_b = _attn_b_fwd(qb, kb, vbt, bias_t, sink_b, s_len)
    mix, x1, h2 = _mixout(oa, ob, wo, x2, g_post_mix, g_pre_ffn, tp)
    u, df, dy, dg4, loss8 = _ffn_fwd(h2, wup_g, wdn, x1, tg2, g_post_ffn, tm)

    dpre, dx1, dmix, dg3, dg2 = _ffn_bwd(df, u, wdn, wup_g, x1, dy, mix, g_pre_ffn, g_post_mix, FFN_BWD_TOKENS)
    gw_dn = _wgrad_rows(u, df, N_DEV, tw, "wgrad_down", square=True)
    gw_up = _wgrad_cols(h2, dpre, N_DEV, tw, "wgrad_up")
    gw_o = _wgrad_o(oa, ob, dmix, N_DEV, min(2048, t))
    doa, dob = _attn_out_bwd(dmix, wo, oa.shape[1], tp)
    dqa, dka, dva, (p_o, p_up, p_dn) = _attn_a_bwd(qa, ka, kat, va, doa, oa, lse_a, tq, tk, [gw_o, gw_up, gw_dn])
    dqb, dkb, dvb, dsum, dsink = _attn_b_bwd(qb, kb, kbt, vb, dob, ob, lse_b, bias_t, sink_b, s_len)
    drel_g, dsink_g = _bias_reduce(dsum, dsink, bucket_t4)
    dproj, dgq, dgk, gw_in_t = _dqkprep(dqa, dka, dva, dqb, dkb, dvb, proj, h1, cos, sin_signed, gq2, gk2, s_len, ts)
    gw_in_t = gw_in_t.reshape(N_DEV, -1, d)
    grad_x, vec, rel, (p_in,) = _dx_final(dproj, w_in_t, x2, dx1, g_pre_mix, tp, [gw_in_t],
                                          [dg2, dg3, dg4, dgq, dgk, dsink_g, drel_g, loss8])
    return grad_x.reshape(bl, s_len, d), p_in, p_o, p_up, p_dn, vec, rel


def kernel(x, w_in, w_o, g_pre_mix, g_post_mix, q_norm_a, k_norm_a, sink_b, rel_bias, g_pre_ffn, w_ffn_up, w_ffn_down, g_post_ffn, loss_target, m_w_in, m_w_o, m_g_pre_mix, m_g_post_mix, m_q_norm_a, m_k_norm_a, m_sink_b, m_rel_bias, m_g_pre_ffn, m_w_ffn_up, m_w_ffn_down, m_g_post_ffn, v_w_in, v_w_o, v_g_pre_mix, v_g_post_mix, v_q_norm_a, v_k_norm_a, v_sink_b, v_rel_bias, v_g_pre_ffn, v_w_ffn_up, v_w_ffn_down, v_g_post_ffn):
    w_in_t = w_in[0].T
    rel_bias_t = rel_bias.T

    grad_x, p_in, p_o, p_up, p_dn, vec, rel = _local_step(
        x, loss_target, w_in_t.astype(BF16), w_o[0].astype(BF16), w_ffn_up[0].astype(BF16), w_ffn_down[0].astype(BF16),
        g_pre_mix, g_post_mix, q_norm_a, k_norm_a, sink_b, rel_bias_t, g_pre_ffn, g_post_ffn)

    r_in, r_o, r_up, r_dn = _adamw_sum(
        [p_in, p_o, p_up, p_dn],
        [w_in_t, w_o[0], w_ffn_up[0], w_ffn_down[0]],
        [m_w_in[0].T, m_w_o[0], m_w_ffn_up[0], m_w_ffn_down[0]],
        [v_w_in[0].T, v_w_o[0], v_w_ffn_up[0], v_w_ffn_down[0]], 4)
    big = {"w_in": [a.T for a in r_in], "w_o": r_o, "w_up": r_up, "w_dn": r_dn}
    loss, small = _adamw_small(
        vec, rel,
        [g_pre_mix, g_post_mix, g_pre_ffn, g_post_ffn, q_norm_a, k_norm_a, sink_b, rel_bias_t],
        [m_g_pre_mix, m_g_post_mix, m_g_pre_ffn, m_g_post_ffn, m_q_norm_a, m_k_norm_a, m_sink_b, m_rel_bias.T],
        [v_g_pre_mix, v_g_post_mix, v_g_pre_ffn, v_g_post_ffn, v_q_norm_a, v_k_norm_a, v_sink_b, v_rel_bias.T])
    s_pre_mix, s_post_mix, s_pre_ffn, s_post_ffn, s_qn, s_kn, s_sink, s_rel_t = small
    s_rel = [a.T for a in s_rel_t]

    def outs(kind):
        return [big["w_in"][kind][None], big["w_o"][kind][None], s_pre_mix[kind], s_post_mix[kind], s_qn[kind],
                s_kn[kind], s_sink[kind], s_rel[kind], s_pre_ffn[kind], big["w_up"][kind][None],
                big["w_dn"][kind][None], s_post_ffn[kind]]

    return (loss.reshape(()), grad_x, *outs(0), *outs(1), *outs(2), *outs(3))
```

```python
import functools

import jax
import jax.numpy as jnp
import numpy as np
from jax import lax
from jax.experimental import pallas as pl
from jax.experimental.pallas import tpu as pltpu

F32 = jnp.float32
BF16 = jnp.bfloat16
SDS = jax.ShapeDtypeStruct

N_DEV = 8
HEAD_DIM = 64
GROUP = 4
BLOCK = 128
SPAN = 3 * BLOCK
GRID_W = 64
N_BUCKETS = 32
MAX_DISTANCE = 128
ROPE_THETA = 10000.0
EPS = 1e-6
NEG_INF = -1e30
SCALE = HEAD_DIM ** -0.5
VT_PAD = 16

ADAM_LR = 0.001
ADAM_B1 = 0.9
ADAM_B2 = 0.999
ADAM_EPS = 1e-08
ADAM_WD = 0.01
ADAM_STEP = 10

VMEM_LIMIT = 56 * 1024 * 1024
MESH = pl.DeviceIdType.MESH


def _cp(*sem):
    return pltpu.CompilerParams(dimension_semantics=sem, vmem_limit_bytes=VMEM_LIMIT)


def _dot(a, b):
    return jnp.dot(a, b, preferred_element_type=F32)


def _dot_nt(a, b):
    return lax.dot_general(a, b, (((1,), (1,)), ((), ())), preferred_element_type=F32)


def _dot_tn(a, b):
    return lax.dot_general(a, b, (((0,), (0,)), ((), ())), preferred_element_type=F32)


def _rms_fwd(x, g):
    r = lax.rsqrt(jnp.mean(x * x, axis=-1, keepdims=True) + EPS)
    n = x * r
    return n * g, n, r


def _rms_bwd(n, r, g, dy):
    gd = g * dy
    dx = r * (gd - n * jnp.mean(n * gd, axis=-1, keepdims=True))
    return dx, dy * n


def _rope_tables(s_len):
    rows = s_len // GRID_W
    row = np.repeat(np.arange(rows, dtype=np.int32), GRID_W)
    col = np.tile(np.arange(GRID_W, dtype=np.int32), rows)
    nf = HEAD_DIM // 4
    freqs = np.float32(ROPE_THETA) ** (-np.arange(nf, dtype=np.float32) / np.float32(nf))
    ang_r = row.astype(np.float32)[:, None] * freqs[None, :]
    ang_c = col.astype(np.float32)[:, None] * freqs[None, :]
    cr, sr, cc, sc = np.cos(ang_r), np.sin(ang_r), np.cos(ang_c), np.sin(ang_c)
    cos = np.concatenate([cr, cr, cc, cc] * 2, axis=-1).astype(np.float32)
    sin_signed = np.concatenate([-sr, sr, -sc, sc] * 2, axis=-1).astype(np.float32)
    return jnp.asarray(cos), jnp.asarray(sin_signed)


def _t5_bucket(rel):
    nb = N_BUCKETS // 2
    ret = (rel > 0).astype(jnp.int32) * nb
    n = jnp.abs(rel)
    max_exact = nb // 2
    nf = jnp.maximum(n, 1).astype(F32)
    large = max_exact + (jnp.log(nf / max_exact) / np.float32(np.log(MAX_DISTANCE / max_exact))
                         * (nb - max_exact)).astype(jnp.int32)
    large = jnp.minimum(large, nb - 1)
    return ret + jnp.where(n < max_exact, n, large)


def _mesh_pos():
    return lax.axis_index("x"), lax.axis_index("y"), lax.axis_index("c")


def _lin(p):
    return 4 * p[0] + 2 * p[1] + p[2]


def _bias_tables(bkt_ref, tbl_ref, out_ref, hb):
    bkt = bkt_ref[...]
    ci = lax.broadcasted_iota(jnp.int32, (SPAN, BLOCK), 0)
    qi = lax.broadcasted_iota(jnp.int32, (SPAN, BLOCK), 1)
    band = jnp.abs(ci - BLOCK - qi) <= BLOCK
    masks = (band, band & (ci >= BLOCK), band & (ci < 2 * BLOCK))
    for h in range(hb):
        acct = jnp.zeros((SPAN, BLOCK), F32)
        for b in range(N_BUCKETS):
            acct = jnp.where(bkt == b, tbl_ref[h, b], acct)
        lanes = slice((h % GROUP) * BLOCK, (h % GROUP + 1) * BLOCK)
        for var, mask in enumerate(masks):
            out_ref[var, h // GROUP, :, lanes] = jnp.where(mask, acct, NEG_INF)


def _weight_gather(shards, bucket_t, rel_bias_t):
    n = len(shards)
    hb = rel_bias_t.shape[0]

    def body(*refs):
        xs, (bkt_ref, tbl_ref), outs, bias_ref = refs[:n], refs[n:n + 2], refs[n + 2:2 * n + 2], refs[2 * n + 2]
        send_sems, recv_sems, local_sems = refs[2 * n + 3:]
        x, y, c = _mesh_pos()
        me, sibling = (x, y, c), (x, y, 1 - c)
        chips = [(1 - x, y), (x, 1 - y), (1 - x, 1 - y)]

        def copy(a, k, block, to, src=None):
            slot = outs[a].at[_lin(block)]
            return pltpu.make_async_remote_copy(
                src_ref=slot if src is None else src, dst_ref=slot,
                send_sem=send_sems.at[a, k], recv_sem=recv_sems.at[a, k],
                device_id=to, device_id_type=MESH)

        started = []
        for a in range(n):
            mine = pltpu.make_async_copy(xs[a], outs[a].at[_lin(me)], local_sems.at[a])
            mine.start()
            started.append(mine)
        sends = []
        for a in range(n):
            first = [copy(a, 0, me, sibling, src=xs[a])]
            first += [copy(a, 1 + j, me, (*chip, c), src=xs[a]) for j, chip in enumerate(chips)]
            for cp in first:
                cp.start()
            sends += first
        _bias_tables(bkt_ref, tbl_ref, bias_ref, hb)
        for a in range(n):
            for j, chip in enumerate(chips):
                copy(a, 1 + j, (*chip, c), me).wait_recv()
                fwd = copy(a, 4 + j, (*chip, c), sibling)
                fwd.start()
                sends.append(fwd)
        for a in range(n):
            copy(a, 0, sibling, me).wait_recv()
            for j, chip in enumerate(chips):
                copy(a, 4 + j, (*chip, 1 - c), me).wait_recv()
        for cp in sends:
            cp.wait_send()
        for mine in started:
            mine.wait()

    anyspec = pl.BlockSpec(memory_space=pl.ANY)
    vm = pl.BlockSpec(memory_space=pltpu.VMEM)
    res = pl.pallas_call(
        body,
        out_shape=[SDS((N_DEV,) + s.shape, s.dtype) for s in shards]
        + [SDS((3, hb // GROUP, SPAN, GROUP * BLOCK), F32)],
        in_specs=[anyspec] * n + [vm, pl.BlockSpec(memory_space=pltpu.SMEM)],
        out_specs=[anyspec] * n + [vm],
        scratch_shapes=[pltpu.SemaphoreType.DMA((n, 7)), pltpu.SemaphoreType.DMA((n, 7)),
                        pltpu.SemaphoreType.DMA((n,))],
        name="weight_gather",
    )(*shards, bucket_t, rel_bias_t)
    return res[:n], res[n]


def _direct_exchange(kind, ins, outs, send_sems, recv_sems, local_sems):
    x, y, c = _mesh_pos()
    me = (x, y, c)
    peers = [(x, y, 1 - c), (1 - x, y, c), (x, 1 - y, c), (1 - x, 1 - y, c),
             (1 - x, y, 1 - c), (x, 1 - y, 1 - c), (1 - x, 1 - y, 1 - c)]

    def src(a, to):
        return ins[a] if kind == "gather" else ins[a].at[_lin(to)]

    def remote(a, k, to, frm):
        return pltpu.make_async_remote_copy(
            src_ref=src(a, to), dst_ref=outs[a].at[_lin(frm)],
            send_sem=send_sems.at[a, k], recv_sem=recv_sems.at[a, k],
            device_id=to, device_id_type=MESH)

    n = len(ins)
    sends = [remote(a, k, p, me) for a in range(n) for k, p in enumerate(peers)]
    arrivals = [remote(a, k, p, p) for a in range(n) for k, p in enumerate(peers)]
    local = [pltpu.make_async_copy(src(a, me), outs[a].at[_lin(me)], local_sems.at[a]) for a in range(n)]

    def start():
        for cp in local + sends:
            cp.start()

    def wait():
        for cp in arrivals:
            cp.wait_recv()
        for cp in sends:
            cp.wait_send()
        for cp in local:
            cp.wait()

    return start, wait


def _exchange_scratch(n):
    return [pltpu.SemaphoreType.DMA((n, 7)), pltpu.SemaphoreType.DMA((n, 7)), pltpu.SemaphoreType.DMA((n,))]


SMALL_LANES = 128


def _small_allreduce(g1_ref, g2_ref, g3_ref, g4_ref, gq_ref, gk_ref, sk_ref, rl_ref, ls_ref, vec_ref, rel_ref,
                     vbuf, rbuf, vland, rland, send_sems, recv_sems):
    kv = sk_ref.shape[0]
    x, y, c = _mesh_pos()
    me = (x, y, c)
    peers = [(x, y, 1 - c), (1 - x, y, c), (x, 1 - y, c), (1 - x, 1 - y, c),
             (1 - x, y, 1 - c), (x, 1 - y, 1 - c), (1 - x, 1 - y, 1 - c)]

    def copies(k, to, frm):
        return [pltpu.make_async_remote_copy(
            src_ref=buf, dst_ref=land.at[_lin(frm)], send_sem=send_sems.at[a, k], recv_sem=recv_sems.at[a, k],
            device_id=to, device_id_type=MESH) for a, (buf, land) in enumerate(((vbuf, vland), (rbuf, rland)))]

    sends = [cp for k, p in enumerate(peers) for cp in copies(k, p, me)]

    def start():
        vbuf[...] = jnp.zeros_like(vbuf)
        rbuf[...] = jnp.zeros_like(rbuf)
        for row, ref in enumerate((g1_ref, g2_ref, g3_ref, g4_ref)):
            vbuf[row:row + 1, :] = ref[0:1, :]
        vbuf[4:5, 0:HEAD_DIM] = gq_ref[0:1, 0:HEAD_DIM] + gq_ref[0:1, HEAD_DIM:PAIR]
        vbuf[4:5, SMALL_LANES:SMALL_LANES + HEAD_DIM] = gk_ref[0:1, 0:HEAD_DIM] + gk_ref[0:1, HEAD_DIM:PAIR]
        for g in range(kv):
            vbuf[4:5, 2 * SMALL_LANES + g * GROUP:2 * SMALL_LANES + (g + 1) * GROUP] = sk_ref[g, 0:1, 0:GROUP]
            rbuf[:, g * GROUP:(g + 1) * GROUP] = rl_ref[g, :, 0:GROUP]
        vbuf[4:5, 3 * SMALL_LANES:3 * SMALL_LANES + 1] = ls_ref[0:1, 0:1]
        for cp in sends:
            cp.start()
        vland[_lin(me)] = vbuf[...]
        rland[_lin(me)] = rbuf[...]

    def finish():
        for k, p in enumerate(peers):
            for cp in copies(k, p, p):
                cp.wait_recv()
        for cp in sends:
            cp.wait_send()
        vacc, racc = vland[0], rland[0]
        for s in range(1, N_DEV):
            vacc, racc = vacc + vland[s], racc + rland[s]
        vec_ref[...] = vacc
        rel_ref[...] = racc

    return start, finish


def _small_allreduce_scratch(d):
    return [pltpu.VMEM((8, d), F32), pltpu.VMEM((N_BUCKETS, 128), F32),
            pltpu.VMEM((N_DEV, 8, d), F32), pltpu.VMEM((N_DEV, N_BUCKETS, 128), F32),
            pltpu.SemaphoreType.DMA((2, 7)), pltpu.SemaphoreType.DMA((2, 7))]


PAIR = 2 * HEAD_DIM


def _pair_masks(ts):
    lane = lax.broadcasted_iota(jnp.int32, (ts, PAIR), 1)
    return lane < HEAD_DIM, (lane % 32) < 16


def _pair_mean(v, low):
    del low
    r = lax.broadcasted_iota(jnp.int32, (PAIR, PAIR), 0) // HEAD_DIM
    c = lax.broadcasted_iota(jnp.int32, (PAIR, PAIR), 1) // HEAD_DIM
    same_head = (r == c).astype(BF16)
    hi = v.astype(BF16)
    lo = (v - hi.astype(F32)).astype(BF16)
    return (_dot(hi, same_head) + _dot(lo, same_head)) * (1.0 / HEAD_DIM)


def _pair_partner(v, first):
    return jnp.where(first, pltpu.roll(v, PAIR - 16, 1), pltpu.roll(v, 16, 1))


def _inproj_qkprep(x2, g1, w_t, cos, sin_signed, gq, gk, bl, s_len, ha, kva, hb, kvb, ts):
    t, d = x2.shape
    p_cols = w_t.shape[0]
    assert ha % 2 == 0 and kva % 2 == 0 and hb % 2 == 0 and kvb % 2 == 0
    ns = s_len // ts
    nt = bl * ns
    sp = s_len + 2 * BLOCK

    def body(*refs):
        kb_ref, kbt_ref, vb_ref, vbt_ref, p_even, p_odd = refs[-6:]
        s = pl.program_id(0)
        i = lax.rem(jnp.maximum(s - 1, 0), ns)

        @pl.when(s == 0)
        def _():
            p_odd[...] = jnp.zeros_like(p_odd)

        @pl.when(i == 0)
        def _():
            zeros = jnp.zeros((kvb, BLOCK, HEAD_DIM), BF16)
            zeros_t = jnp.zeros((kvb, HEAD_DIM + VT_PAD, BLOCK), BF16)
            for ref in (kb_ref, vb_ref):
                ref[0, :, 0:BLOCK, :] = zeros
                ref[0, :, sp - BLOCK:sp, :] = zeros
            kbt_ref[0, :, :, 0:BLOCK] = zeros_t[:, 0:HEAD_DIM]
            kbt_ref[0, :, :, sp - BLOCK:sp] = zeros_t[:, 0:HEAD_DIM]
            vbt_ref[0, :, :, 0:BLOCK] = zeros_t
            vbt_ref[0, :, :, sp - BLOCK:sp] = zeros_t

        even = lax.rem(s, 2) == 0
        pl.when(even)(functools.partial(tile_work, p_even, p_odd, i, *refs[:-2]))
        pl.when(jnp.logical_not(even))(functools.partial(tile_work, p_odd, p_even, i, *refs[:-2]))

    def tile_work(p_new, p_ref, i, x_ref, g1_ref, w_ref, cos_ref, sin_ref, gq_ref, gk_ref, h_ref, po_ref, qa_ref,
                  ka_ref, kat_ref, va_ref, vat_ref, qb_ref, kb_ref, kbt_ref, vb_ref, vbt_ref):
        y, _, _ = _rms_fwd(x_ref[...], g1_ref[...])
        h = y.astype(BF16)
        h_ref[...] = h
        n_parts = 6
        pw = p_cols // n_parts

        def project(c):
            p_new[:, c * pw:(c + 1) * pw] = _dot_nt(h, w_ref[c * pw:(c + 1) * pw, :])

        cs, sn = cos_ref[...], sin_ref[...]
        low, first = _pair_masks(ts)
        ones_row = (lax.broadcasted_iota(jnp.int32, (VT_PAD, ts), 0) == 0).astype(BF16)
        heads = (slice(0, HEAD_DIM), slice(HEAD_DIM, PAIR))

        def pair(p):
            v = p_ref[:, p * PAIR:(p + 1) * PAIR]
            po_ref[:, p * PAIR:(p + 1) * PAIR] = v
            return v

        def normrope(x, g):
            y = x * lax.rsqrt(_pair_mean(x * x, low) + EPS) * g
            return y * cs + _pair_partner(y, first) * sn

        eye = (lax.broadcasted_iota(jnp.int32, (PAIR, PAIR), 0)
               == lax.broadcasted_iota(jnp.int32, (PAIR, PAIR), 1)).astype(BF16)

        def transposed(xb):
            return _dot_nt(eye, xb).astype(BF16)

        def prep_qa(p):
            qa_ref[:, p * PAIR:(p + 1) * PAIR] = (normrope(pair(p), gq_ref[...]) * SCALE).astype(BF16)

        def prep_kva(p):
            base = ha // 2
            k = normrope(pair(base + p), gk_ref[...]).astype(BF16)
            v = pair(base + kva // 2 + p).astype(BF16)
            kt, vt = transposed(k), transposed(v)
            for e, lanes in enumerate(heads):
                ka_ref[0, 2 * p + e] = k[:, lanes]
                va_ref[0, 2 * p + e] = v[:, lanes]
                kat_ref[0, 2 * p + e] = kt[lanes, :]
                vat_ref[0, 2 * p + e, 0:HEAD_DIM, :] = vt[lanes, :]
                vat_ref[0, 2 * p + e, HEAD_DIM:HEAD_DIM + VT_PAD, :] = ones_row

        def prep_qb(p):
            base = ha // 2 + kva
            qb_ref[:, p * PAIR:(p + 1) * PAIR] = (pair(base + p) * SCALE).astype(BF16)

        rows = pl.ds(pl.multiple_of(BLOCK + i * ts, BLOCK), ts)

        def prep_kvb(p):
            base = ha // 2 + kva + hb // 2
            k = pair(base + p).astype(BF16)
            v = pair(base + kvb // 2 + p).astype(BF16)
            kt, vt = transposed(k), transposed(v)
            for e, lanes in enumerate(heads):
                kb_ref[0, 2 * p + e, rows, :] = k[:, lanes]
                vb_ref[0, 2 * p + e, rows, :] = v[:, lanes]
                kbt_ref[0, 2 * p + e, :, rows] = kt[lanes, :]
                vbt_ref[0, 2 * p + e, 0:HEAD_DIM, rows] = vt[lanes, :]
                vbt_ref[0, 2 * p + e, HEAD_DIM:HEAD_DIM + VT_PAD, rows] = ones_row

        work = ([functools.partial(prep_qa, p) for p in range(ha // 2)]
                + [functools.partial(prep_kva, p) for p in range(kva // 2)]
                + [functools.partial(prep_qb, p) for p in range(hb // 2)]
                + [functools.partial(prep_kvb, p) for p in range(kvb // 2)])
        per_part = -(-len(work) // n_parts)
        for c in range(n_parts):
            for item in work[c * per_part:(c + 1) * per_part]:
                item()
            project(c)

    def cur(s):
        return jnp.minimum(s, nt - 1)

    def prev(s):
        return jnp.maximum(s - 1, 0) // ns, lax.rem(jnp.maximum(s - 1, 0), ns)

    def hm(nh):
        return pl.BlockSpec((1, nh, ts, HEAD_DIM), lambda s: (prev(s)[0], 0, prev(s)[1], 0))

    def hm_t(nh, rows):
        return pl.BlockSpec((1, nh, rows, ts), lambda s: (prev(s)[0], 0, 0, prev(s)[1]))

    def tokmajor(nh):
        return pl.BlockSpec((ts, nh * HEAD_DIM), lambda s: (jnp.maximum(s - 1, 0), 0))

    def padded(nh):
        return pl.BlockSpec((1, nh, sp, HEAD_DIM), lambda s: (prev(s)[0], 0, 0, 0))

    def padded_t(nh, rows):
        return pl.BlockSpec((1, nh, rows, sp), lambda s: (prev(s)[0], 0, 0, 0))

    tab = pl.BlockSpec((ts, PAIR), lambda s: (prev(s)[1], 0))
    vec = pl.BlockSpec((1, PAIR), lambda s: (0, 0))
    return pl.pallas_call(
        body,
        grid=(nt + 1,),
        in_specs=[pl.BlockSpec((ts, d), lambda s: (cur(s), 0)),
                  pl.BlockSpec((1, d), lambda s: (0, 0)),
                  pl.BlockSpec((p_cols, d), lambda s: (0, 0)),
                  tab, tab, vec, vec],
        out_specs=[pl.BlockSpec((ts, d), lambda s: (cur(s), 0)), tokmajor(p_cols // HEAD_DIM),
                   tokmajor(ha), hm(kva), hm_t(kva, HEAD_DIM), hm(kva), hm_t(kva, HEAD_DIM + VT_PAD),
                   tokmajor(hb), padded(kvb), padded_t(kvb, HEAD_DIM), padded(kvb),
                   padded_t(kvb, HEAD_DIM + VT_PAD)],
        out_shape=[SDS((t, d), BF16), SDS((t, p_cols), F32),
                   SDS((t, ha * HEAD_DIM), BF16), SDS((bl, kva, s_len, HEAD_DIM), BF16),
                   SDS((bl, kva, HEAD_DIM, s_len), BF16),
                   SDS((bl, kva, s_len, HEAD_DIM), BF16), SDS((bl, kva, HEAD_DIM + VT_PAD, s_len), BF16),
                   SDS((t, hb * HEAD_DIM), BF16),
                   SDS((bl, kvb, sp, HEAD_DIM), BF16), SDS((bl, kvb, HEAD_DIM, sp), BF16),
                   SDS((bl, kvb, sp, HEAD_DIM), BF16), SDS((bl, kvb, HEAD_DIM + VT_PAD, sp), BF16)],
        scratch_shapes=[pltpu.VMEM((ts, p_cols), F32)] * 2,
        compiler_params=_cp("arbitrary"),
        name="inproj_qkprep",
    )(x2, g1, w_t, cos, sin_signed, gq, gk)


def _attn_a_fwd(qa, ka, vat, tq, tk, shards):
    bl, kv, s_len, _ = ka.shape
    ha = qa.shape[1] // HEAD_DIM
    va_rows = vat.shape[2]
    nq, nk = s_len // tq, s_len // tk
    assert nk % 2 == 0
    r = GROUP * tq
    ns = len(shards)

    def body(q_ref, qn_ref, k_ref, v_ref, *rest):
        shard_refs, (o_ref, l_ref), gathered = rest[:ns], rest[ns:ns + 2], rest[ns + 2:2 * ns + 2]
        st_sc, send_sems, recv_sems, local_sems = rest[2 * ns + 2:]
        i = pl.program_id(2)
        step_id = (pl.program_id(0) * kv + pl.program_id(1)) * nq + i
        start, wait = _direct_exchange("gather", shard_refs, gathered, send_sems, recv_sems, local_sems)
        pl.when(step_id == 0)(start)

        q = _heads_t(q_ref[...]).astype(BF16)

        def scores(c, qv):
            return _dot(k_ref[0, 0, pl.ds(pl.multiple_of(c * tk, tk), tk), :], qv)

        def fold(st, c, carry):
            m_old, acc = carry
            m_new = jnp.maximum(m_old, jnp.max(st, axis=0, keepdims=True))
            pt = jnp.exp(st - m_new).astype(BF16)
            vt = v_ref[0, 0, :, pl.ds(pl.multiple_of(c * tk, tk), tk)]
            return m_new, jnp.exp(m_old - m_new) * acc + _dot(vt, pt)

        @pl.when(i == 0)
        def _():
            st_sc[0] = scores(0, q)

        def step(c2, carry):
            c = 2 * c2
            st_sc[1] = scores(c + 1, q)
            carry = fold(st_sc[0], c, carry)
            st_sc[0] = scores(c + 2, q)
            return fold(st_sc[1], c + 1, carry)

        carry = (jnp.full((1, r), -jnp.inf, F32), jnp.zeros((va_rows, r), F32))
        for c2 in range(nk // 2 - 1):
            carry = step(c2, carry)
        st_sc[1] = scores(nk - 1, q)
        carry = fold(st_sc[0], nk - 2, carry)
        st_sc[0] = scores(0, _heads_t(qn_ref[...]).astype(BF16))
        m, acc = fold(st_sc[1], nk - 1, carry)
        l = acc[HEAD_DIM:HEAD_DIM + 1, :]
        o_ref[...] = _heads_t_inv(acc[0:HEAD_DIM, :] / l).astype(BF16)
        l_ref[0, 0, 0] = jnp.broadcast_to(m + jnp.log(l), (8, r))
        pl.when(step_id == bl * kv * nq - 1)(wait)

    anyspec = pl.BlockSpec(memory_space=pl.ANY)
    res = pl.pallas_call(
        body,
        grid=(bl, kv, nq),
        in_specs=[pl.BlockSpec((tq, GROUP * HEAD_DIM), lambda b, g, i: (b * nq + i, g)),
                  pl.BlockSpec((tq, GROUP * HEAD_DIM), lambda b, g, i: (b * nq + jnp.minimum(i + 1, nq - 1), g)),
                  pl.BlockSpec((1, 1, s_len, HEAD_DIM), lambda b, g, i: (b, g, 0, 0)),
                  pl.BlockSpec((1, 1, va_rows, s_len), lambda b, g, i: (b, g, 0, 0))] + [anyspec] * ns,
        out_specs=[pl.BlockSpec((tq, GROUP * HEAD_DIM), lambda b, g, i: (b * nq + i, g)),
                   pl.BlockSpec((1, 1, 1, 8, r), lambda b, g, i: (b, g, i, 0, 0))] + [anyspec] * ns,
        out_shape=[SDS((bl * s_len, ha * HEAD_DIM), BF16), SDS((bl, kv, nq, 8, r), F32)]
        + [SDS((N_DEV,) + s.shape, s.dtype) for s in shards],
        scratch_shapes=[pltpu.VMEM((2, tk, r), F32)] + _exchange_scratch(ns),
        compiler_params=_cp("arbitrary", "arbitrary", "arbitrary"),
        name="attn_a_fwd",
    )(qa, qa, ka, vat, *shards)
    return res[0], res[1], res[2:]


FFN_BWD_TOKENS = 256
QB_PER_STEP = 16


def _bias_variant(n, nb):
    return jnp.where(n == 0, 1, jnp.where(n == nb - 1, 2, 0))


def _sink_row(sink_ref, g):
    return jnp.concatenate([jnp.full((1, BLOCK), sink_ref[0, g * GROUP + h], F32) for h in range(GROUP)], axis=1)


def _attn_b_fwd(qb, kb, vbt, bias_t, sink, s_len):
    bl, kv, sp, _ = kb.shape
    hb = qb.shape[1] // HEAD_DIM
    vt_rows = vbt.shape[2]
    nb = s_len // BLOCK
    nbs = min(QB_PER_STEP, nb)
    r = GROUP * BLOCK

    def body(q_ref, k_ref, vt_ref, bt_ref, sink_ref, o_ref, l_ref, st_sc, pb_sc):
        g, n0 = pl.program_id(1), pl.program_id(2) * nbs
        sink_row = _sink_row(sink_ref, g)

        def span(j):
            return pl.ds(pl.multiple_of((n0 + j) * BLOCK, BLOCK), SPAN)

        for j in range(nbs):
            qt = _heads_t(q_ref[j * BLOCK:(j + 1) * BLOCK, :]).astype(BF16)
            st_sc[j] = _dot(k_ref[0, 0, span(j), :], qt) + bt_ref[_bias_variant(n0 + j, nb), 0]
        maxes = []
        for j in range(nbs):
            st = st_sc[j]
            m = jnp.maximum(jnp.max(st, axis=0, keepdims=True), sink_row)
            pb_sc[j] = jnp.exp(st - m).astype(BF16)
            maxes.append(m)
        for j in range(nbs):
            m = maxes[j]
            acc = _dot(vt_ref[0, 0, :, span(j)], pb_sc[j])
            l = acc[HEAD_DIM:HEAD_DIM + 1, :] + jnp.exp(sink_row - m)
            o_ref[j * BLOCK:(j + 1) * BLOCK, :] = _heads_t_inv(acc[0:HEAD_DIM, :] / l).astype(BF16)
            l_ref[0, 0, j] = jnp.broadcast_to(m + jnp.log(l), (8, r))

    return pl.pallas_call(
        body,
        grid=(bl, kv, nb // nbs),
        in_specs=[pl.BlockSpec((nbs * BLOCK, GROUP * HEAD_DIM), lambda b, g, n: (b * (nb // nbs) + n, g)),
                  pl.BlockSpec((1, 1, sp, HEAD_DIM), lambda b, g, n: (b, g, 0, 0)),
                  pl.BlockSpec((1, 1, vt_rows, sp), lambda b, g, n: (b, g, 0, 0)),
                  pl.BlockSpec((3, 1, SPAN, r), lambda b, g, n: (0, g, 0, 0)),
                  pl.BlockSpec(memory_space=pltpu.SMEM)],
        out_specs=[pl.BlockSpec((nbs * BLOCK, GROUP * HEAD_DIM), lambda b, g, n: (b * (nb // nbs) + n, g)),
                   pl.BlockSpec((1, 1, nbs, 8, r), lambda b, g, n: (b, g, n, 0, 0))],
        out_shape=[SDS((bl * s_len, hb * HEAD_DIM), BF16), SDS((bl, kv, nb, 8, r), F32)],
        scratch_shapes=[pltpu.VMEM((nbs, SPAN, r), F32), pltpu.VMEM((nbs, SPAN, r), BF16)],
        compiler_params=_cp("parallel", "parallel", "arbitrary"),
        name="attn_b_fwd",
    )(qb, kb, vbt, bias_t, sink)


def _mixout(oa, ob, wo, x2, g2, g3, tm):
    t, d = x2.shape
    ca = oa.shape[1]

    def body(oa_ref, ob_ref, w_ref, x_ref, g2_ref, g3_ref, mix_ref, x1_ref, h2_ref):
        mix = _dot(oa_ref[...], w_ref[0:ca, :]) + _dot(ob_ref[...], w_ref[ca:, :])
        mix_ref[...] = mix
        y2, _, _ = _rms_fwd(mix, g2_ref[...])
        x1 = x_ref[...] + y2
        x1_ref[...] = x1
        y3, _, _ = _rms_fwd(x1, g3_ref[...])
        h2_ref[...] = y3.astype(BF16)

    tile = lambda w: pl.BlockSpec((tm, w), lambda i: (i, 0))
    vec = pl.BlockSpec((1, d), lambda i: (0, 0))
    return pl.pallas_call(
        body,
        grid=(t // tm,),
        in_specs=[tile(ca), tile(ob.shape[1]), pl.BlockSpec(wo.shape, lambda i: (0, 0)), tile(d), vec, vec],
        out_specs=[tile(d), tile(d), tile(d)],
        out_shape=[SDS((t, d), F32), SDS((t, d), F32), SDS((t, d), BF16)],
        compiler_params=_cp("parallel"),
        name="mixout",
    )(oa, ob, wo, x2, g2, g3)


def _ffn_fwd(h2, wup_g, wdn, x1, target, g4, tm):
    t, d = x1.shape
    nblk, _, tf = wup_g.shape
    ff = nblk * tf
    nt = t // tm

    def body(h_ref, wu_ref, wd_ref, x1_ref, tg_ref, g_ref, u_ref, df_ref, dy_ref, dg_ref, loss_ref, f_sc):
        s = pl.program_id(0)

        @pl.when(s == 0)
        def _():
            f_sc[...] = jnp.zeros_like(f_sc)
            dg_ref[...] = jnp.zeros_like(dg_ref)
            loss_ref[...] = jnp.zeros_like(loss_ref)

        counted = (s > 0).astype(F32)

        def loss_part():
            g = g_ref[...]
            y4, n, r = _rms_fwd(f_sc[...], g)
            e = (x1_ref[...] + y4) - tg_ref[...]
            loss_ref[...] += jnp.sum(e * e) * (0.5 / d) * counted
            dy = e * (1.0 / d)
            dy_ref[...] = dy
            return n, r, g, dy

        def norm_bwd_part(n, r, g, dy):
            df, dgt = _rms_bwd(n, r, g, dy)
            df_ref[...] = df.astype(BF16)
            dg_ref[0:1, :] += jnp.sum(dgt, axis=0, keepdims=True) * counted

        @pl.when(s < nt)
        def _():
            h = h_ref[...]
            squares = []
            saved = None
            for c in range(nblk):
                u = jnp.maximum(_dot(h, wu_ref[c]), 0.0)
                u_ref[:, c * tf:(c + 1) * tf] = u.astype(BF16)
                squares.append((u * u).astype(BF16))
                if c == 0:
                    saved = loss_part()
                elif c == 1:
                    norm_bwd_part(*saved)
            f_sc[...] = _dot(jnp.concatenate(squares, axis=1), wd_ref[...])

        @pl.when(s == nt)
        def _():
            norm_bwd_part(*loss_part())

    cur = lambda s: (jnp.minimum(s, nt - 1), 0)
    prev = lambda s: (jnp.maximum(s - 1, 0), 0)
    return pl.pallas_call(
        body,
        grid=(nt + 1,),
        in_specs=[pl.BlockSpec((tm, d), cur),
                  pl.BlockSpec((nblk, d, tf), lambda s: (0, 0, 0)),
                  pl.BlockSpec((ff, d), lambda s: (0, 0)),
                  pl.BlockSpec((tm, d), prev), pl.BlockSpec((tm, d), prev),
                  pl.BlockSpec((1, d), lambda s: (0, 0))],
        out_specs=[pl.BlockSpec((tm, ff), cur), pl.BlockSpec((tm, d), prev), pl.BlockSpec((tm, d), prev),
                   pl.BlockSpec((8, d), lambda s: (0, 0)),
                   pl.BlockSpec((8, 128), lambda s: (0, 0))],
        out_shape=[SDS((t, ff), BF16), SDS((t, d), BF16), SDS((t, d), F32), SDS((8, d), F32), SDS((8, 128), F32)],
        scratch_shapes=[pltpu.VMEM((tm, d), F32)],
        compiler_params=_cp("arbitrary"),
        name="ffn_fwd",
    )(h2, wup_g, wdn, x1, target, g4)


def _ffn_bwd(df, u, wdn, wup_g, x1, dy, mix, g3, g2, tm):
    t, d = x1.shape
    nblk, _, tf = wup_g.shape
    ff = nblk * tf
    nt = t // tm

    def body(df_ref, u_ref, wd_ref, wu_ref, x1_ref, dy_ref, mix_ref, g3_ref, g2_ref,
             dpre_ref, dx1_ref, dmix_ref, dg3_ref, dg2_ref, dh_sc):
        s = pl.program_id(0)

        @pl.when(s == 0)
        def _():
            dh_sc[...] = jnp.zeros_like(dh_sc)
            dg3_ref[...] = jnp.zeros_like(dg3_ref)
            dg2_ref[...] = jnp.zeros_like(dg2_ref)

        counted = (s > 0).astype(F32)

        def residual_norm_part():
            g3 = g3_ref[...]
            _, n3, r3 = _rms_fwd(x1_ref[...], g3)
            dx, dgt3 = _rms_bwd(n3, r3, g3, dh_sc[...])
            dx1 = dy_ref[...] + dx
            dx1_ref[...] = dx1
            dg3_ref[0:1, :] += jnp.sum(dgt3, axis=0, keepdims=True) * counted

        def mix_norm_part():
            g2 = g2_ref[...]
            _, n2, r2 = _rms_fwd(mix_ref[...], g2)
            dmix, dgt2 = _rms_bwd(n2, r2, g2, dx1_ref[...])
            dmix_ref[...] = dmix.astype(BF16)
            dg2_ref[0:1, :] += jnp.sum(dgt2, axis=0, keepdims=True) * counted

        @pl.when(s < nt)
        def _():
            du2 = _dot_nt(df_ref[...], wd_ref[...])
            dpre = (2.0 * u_ref[...].astype(F32) * du2).astype(BF16)
            dpre_ref[...] = dpre
            dh = _dot_nt(dpre[:, 0:tf], wu_ref[0])
            for c in range(1, nblk):
                if c == 1:
                    residual_norm_part()
                elif c == 3:
                    mix_norm_part()
                dh = dh + _dot_nt(dpre[:, c * tf:(c + 1) * tf], wu_ref[c])
            dh_sc[...] = dh

        @pl.when(s == nt)
        def _():
            residual_norm_part()
            mix_norm_part()

    cur = lambda s: (jnp.minimum(s, nt - 1), 0)
    prev = lambda s: (jnp.maximum(s - 1, 0), 0)
    vec = pl.BlockSpec((1, d), lambda s: (0, 0))
    acc8 = pl.BlockSpec((8, d), lambda s: (0, 0))
    return pl.pallas_call(
        body,
        grid=(nt + 1,),
        in_specs=[pl.BlockSpec((tm, d), cur),
                  pl.BlockSpec((tm, ff), cur),
                  pl.BlockSpec((ff, d), lambda s: (0, 0)),
                  pl.BlockSpec((nblk, d, tf), lambda s: (0, 0, 0)),
                  pl.BlockSpec((tm, d), prev), pl.BlockSpec((tm, d), prev), pl.BlockSpec((tm, d), prev), vec, vec],
        out_specs=[pl.BlockSpec((tm, ff), cur), pl.BlockSpec((tm, d), prev), pl.BlockSpec((tm, d), prev),
                   acc8, acc8],
        out_shape=[SDS(u.shape, BF16), SDS((t, d), F32), SDS((t, d), BF16), SDS((8, d), F32), SDS((8, d), F32)],
        scratch_shapes=[pltpu.VMEM((tm, d), F32)],
        compiler_params=_cp("arbitrary"),
        name="ffn_bwd",
    )(df, u, wdn, wup_g, x1, dy, mix, g3, g2)


def _wgrad(a, b, a_spec, b_spec, out_block, out_shape, nj, nk, name, prep_a=None, prep_b=None):
    acc_shape = out_block[1:]

    def body(a_ref, b_ref, o_ref, acc_sc):
        k = pl.program_id(1)
        av = a_ref[...] if prep_a is None else prep_a(a_ref)
        bv = b_ref[...] if prep_b is None else prep_b(b_ref)
        part = _dot_tn(av, bv)

        @pl.when(k == 0)
        def _():
            acc_sc[...] = part

        @pl.when(k > 0)
        def _():
            acc_sc[...] += part

        @pl.when(k == nk - 1)
        def _():
            o_ref[0] = acc_sc[...].astype(BF16)

    return pl.pallas_call(
        body,
        grid=(nj, nk),
        in_specs=[a_spec, b_spec],
        out_specs=pl.BlockSpec(out_block, lambda j, k: (j, 0, 0)),
        out_shape=SDS(out_shape, BF16),
        scratch_shapes=[pltpu.VMEM(acc_shape, F32)],
        compiler_params=_cp("parallel", "arbitrary"),
        name=name,
    )(a, b)


def _wgrad_cols(a, b, nj, tt, name):
    t, m = a.shape
    bn = b.shape[1] // nj
    return _wgrad(a, b, pl.BlockSpec((tt, m), lambda j, k: (k, 0)), pl.BlockSpec((tt, bn), lambda j, k: (k, j)),
                  (1, m, bn), (nj, m, bn), nj, t // tt, name)


def _wgrad_rows(a, b, nj, tt, name, square=False):
    t, n = b.shape
    bm = a.shape[1] // nj

    def squared(a_ref):
        af = a_ref[...].astype(F32)
        return (af * af).astype(BF16)

    return _wgrad(a, b, pl.BlockSpec((tt, bm), lambda j, k: (k, j)), pl.BlockSpec((tt, n), lambda j, k: (k, 0)),
                  (1, bm, n), (nj, bm, n), nj, t // tt, name, prep_a=squared if square else None)


def _wgrad_o(oa, ob, dmix, nj, tt):
    t, n = dmix.shape
    ca, cb = oa.shape[1], ob.shape[1]
    m = ca + cb
    nk = t // tt

    def body(oa_ref, ob_ref, b_ref, o_ref, acc_sc):
        k = pl.program_id(0)
        part = _dot_tn(jnp.concatenate([oa_ref[...], ob_ref[...]], axis=1), b_ref[...])

        @pl.when(k == 0)
        def _():
            acc_sc[...] = part

        @pl.when(k > 0)
        def _():
            acc_sc[...] += part

        @pl.when(k == nk - 1)
        def _():
            o_ref[...] = acc_sc[...].reshape(nj, m // nj, n).astype(BF16)

    return pl.pallas_call(
        body,
        grid=(nk,),
        in_specs=[pl.BlockSpec((tt, ca), lambda k: (k, 0)), pl.BlockSpec((tt, cb), lambda k: (k, 0)),
                  pl.BlockSpec((tt, n), lambda k: (k, 0))],
        out_specs=pl.BlockSpec((nj, m // nj, n), lambda k: (0, 0, 0)),
        out_shape=SDS((nj, m // nj, n), BF16),
        scratch_shapes=[pltpu.VMEM((m, n), F32)],
        compiler_params=_cp("arbitrary"),
        name="wgrad_o",
    )(oa, ob, dmix)


def _attn_out_bwd(dmix, wo, ca, tm):
    t, d = dmix.shape
    cb = wo.shape[0] - ca

    def body(dm_ref, w_ref, da_ref, db_ref):
        dm = dm_ref[...]
        da_ref[...] = _dot_nt(dm, w_ref[0:ca, :]).astype(BF16)
        db_ref[...] = _dot_nt(dm, w_ref[ca:, :]).astype(BF16)

    return pl.pallas_call(
        body,
        grid=(t // tm,),
        in_specs=[pl.BlockSpec((tm, d), lambda i: (i, 0)), pl.BlockSpec(wo.shape, lambda i: (0, 0))],
        out_specs=[pl.BlockSpec((tm, ca), lambda i: (i, 0)), pl.BlockSpec((tm, cb), lambda i: (i, 0))],
        out_shape=[SDS((t, ca), BF16), SDS((t, cb), BF16)],
        compiler_params=_cp("parallel"),
        name="attn_out_bwd",
    )(dmix, wo)


def _heads_t(x):
    xt = x.astype(F32).T
    return jnp.concatenate([xt[h * HEAD_DIM:(h + 1) * HEAD_DIM, :] for h in range(GROUP)], axis=1)


def _heads_t_inv(yt):
    n = yt.shape[1] // GROUP
    return jnp.concatenate([yt[:, h * n:(h + 1) * n] for h in range(GROUP)], axis=0).T


def _attn_a_bwd(qa, ka, kat, va, do, o, lse, tq, tk, grads):
    bl, kv, s_len, _ = ka.shape
    nq, nk = s_len // tq, s_len // tk
    assert nk % 2 == 0
    r = GROUP * tq
    ng = len(grads)

    def body(q_ref, qn_ref, k_ref, kt_ref, v_ref, do_ref, don_ref, o_ref, l_ref, *rest):
        grad_refs, (dq_ref, dk_ref, dv_ref), parts = rest[:ng], rest[ng:ng + 3], rest[ng + 3:2 * ng + 3]
        st_sc, dp_sc, dkt_sc, dvt_sc, send_sems, recv_sems, local_sems = rest[2 * ng + 3:]
        i = pl.program_id(2)
        step_id = (pl.program_id(0) * kv + pl.program_id(1)) * nq + i
        start, wait = _direct_exchange("scatter", grad_refs, parts, send_sems, recv_sems, local_sems)
        pl.when(step_id == 0)(start)

        dot32 = _heads_t(do_ref[...])
        drow = jnp.sum(dot32 * _heads_t(o_ref[...]), axis=0, keepdims=True)
        qt, dot = _heads_t(q_ref[...]).astype(BF16), dot32.astype(BF16)
        lrow = l_ref[0, 0, 0, 0:1, :]

        @pl.when(i == 0)
        def _():
            dkt_sc[...] = jnp.zeros_like(dkt_sc)
            dvt_sc[...] = jnp.zeros_like(dvt_sc)

        def chunk(c):
            return pl.ds(pl.multiple_of(c * tk, tk), tk)

        def scores(c, slot, qv=qt, dov=dot):
            st_sc[slot] = _dot(k_ref[0, 0, chunk(c), :], qv)
            dp_sc[slot] = _dot(v_ref[0, 0, chunk(c), :], dov)

        def fold(slot, c, dqt):
            pt = jnp.exp(st_sc[slot] - lrow)
            dsb = (pt * (dp_sc[slot] - drow)).astype(BF16)
            dvt_sc[:, chunk(c)] += _dot_nt(dot, pt.astype(BF16))
            dkt_sc[:, chunk(c)] += _dot_nt(qt, dsb)
            return dqt + _dot(kt_ref[0, 0, :, chunk(c)], dsb)

        @pl.when(i == 0)
        def _():
            scores(0, 0)

        def step(c2, dqt):
            c = 2 * c2
            scores(c + 1, 1)
            dqt = fold(0, c, dqt)
            scores(c + 2, 0)
            return fold(1, c + 1, dqt)

        dqt = jnp.zeros((HEAD_DIM, r), F32)
        for c2 in range(nk // 2 - 1):
            dqt = step(c2, dqt)
        scores(nk - 1, 1)
        dqt = fold(0, nk - 2, dqt)
        scores(0, 0, _heads_t(qn_ref[...]).astype(BF16), _heads_t(don_ref[...]).astype(BF16))
        dq_ref[...] = _heads_t_inv(fold(1, nk - 1, dqt))

        @pl.when(i == nq - 1)
        def _():
            dk_ref[0, 0] = dkt_sc[...].T
            dv_ref[0, 0] = dvt_sc[...].T

        pl.when(step_id == bl * kv * nq - 1)(wait)

    kvspec = pl.BlockSpec((1, 1, s_len, HEAD_DIM), lambda b, g, i: (b, g, 0, 0))
    tok = pl.BlockSpec((tq, GROUP * HEAD_DIM), lambda b, g, i: (b * nq + i, g))
    toknext = pl.BlockSpec((tq, GROUP * HEAD_DIM), lambda b, g, i: (b * nq + jnp.minimum(i + 1, nq - 1), g))
    anyspec = pl.BlockSpec(memory_space=pl.ANY)
    res = pl.pallas_call(
        body,
        grid=(bl, kv, nq),
        in_specs=[tok, toknext, kvspec, pl.BlockSpec((1, 1, HEAD_DIM, s_len), lambda b, g, i: (b, g, 0, 0)), kvspec,
                  tok, toknext, tok, pl.BlockSpec((1, 1, 1, 8, r), lambda b, g, i: (b, g, i, 0, 0))] + [anyspec] * ng,
        out_specs=[tok, kvspec, kvspec] + [anyspec] * ng,
        out_shape=[SDS(qa.shape, F32), SDS(ka.shape, F32), SDS(va.shape, F32)]
        + [SDS(g.shape, g.dtype) for g in grads],
        scratch_shapes=[pltpu.VMEM((2, tk, r), F32), pltpu.VMEM((2, tk, r), F32),
                        pltpu.VMEM((HEAD_DIM, s_len), F32), pltpu.VMEM((HEAD_DIM, s_len), F32)]
        + _exchange_scratch(ng),
        compiler_params=_cp("arbitrary", "arbitrary", "arbitrary"),
        name="attn_a_bwd",
    )(qa, qa, ka, kat, va, do, do, o, lse, *grads)
    return res[0], res[1], res[2], res[3:]


def _attn_b_bwd(qb, kb, kbt, vb, do, o, lse, bias_t, sink, s_len):
    bl, kv, sp, _ = kb.shape
    nb = s_len // BLOCK
    nbs = min(QB_PER_STEP, nb)
    r = GROUP * BLOCK

    def body(q_ref, k_ref, kt_ref, v_ref, do_ref, o_ref, l_ref, bt_ref, sink_ref,
             dq_ref, dk_ref, dv_ref, dsum_ref, dsink_ref, dkt_sc, dvt_sc):
        g, b, ns = pl.program_id(0), pl.program_id(1), pl.program_id(2)
        sink_row = _sink_row(sink_ref, g)

        @pl.when(ns == 0)
        def _():
            dkt_sc[...] = jnp.zeros_like(dkt_sc)
            dvt_sc[...] = jnp.zeros_like(dvt_sc)

        @pl.when((b == 0) & (ns == 0))
        def _():
            dsum_ref[...] = jnp.zeros_like(dsum_ref)
            dsink_ref[...] = jnp.zeros_like(dsink_ref)

        dsum = jnp.zeros((SPAN, r), F32)
        dsink = jnp.zeros((1, r), F32)
        for j in range(nbs):
            n = ns * nbs + j
            span = pl.ds(pl.multiple_of(n * BLOCK, BLOCK), SPAN)
            rows = slice(j * BLOCK, (j + 1) * BLOCK)
            dot32 = _heads_t(do_ref[rows, :])
            drow = jnp.sum(dot32 * _heads_t(o_ref[rows, :]), axis=0, keepdims=True)
            qt, dot = _heads_t(q_ref[rows, :]).astype(BF16), dot32.astype(BF16)
            lrow = l_ref[0, 0, j, 0:1, :]
            st = _dot(k_ref[0, 0, span, :], qt) + bt_ref[_bias_variant(n, nb), 0]
            pt = jnp.exp(st - lrow)
            dst = pt * (_dot(v_ref[0, 0, span, :], dot) - drow)
            dsum = dsum + dst
            dsink = dsink - jnp.exp(sink_row - lrow) * drow
            dsb = dst.astype(BF16)
            dvt_sc[:, span] += _dot_nt(dot, pt.astype(BF16))
            dkt_sc[:, span] += _dot_nt(qt, dsb)
            dq_ref[rows, :] = _heads_t_inv(_dot(kt_ref[0, 0, :, span], dsb))
        dsum_ref[0] += dsum
        dsink_ref[0, 0:1, :] += dsink

        @pl.when(ns == nb // nbs - 1)
        def _():
            dk_ref[0, 0] = dkt_sc[:, BLOCK:BLOCK + s_len].T
            dv_ref[0, 0] = dvt_sc[:, BLOCK:BLOCK + s_len].T

    kvspec = pl.BlockSpec((1, 1, sp, HEAD_DIM), lambda g, b, n: (b, g, 0, 0))
    kvout = pl.BlockSpec((1, 1, s_len, HEAD_DIM), lambda g, b, n: (b, g, 0, 0))
    tok = pl.BlockSpec((nbs * BLOCK, GROUP * HEAD_DIM), lambda g, b, n: (b * (nb // nbs) + n, g))
    return pl.pallas_call(
        body,
        grid=(kv, bl, nb // nbs),
        in_specs=[tok, kvspec, pl.BlockSpec((1, 1, HEAD_DIM, sp), lambda g, b, n: (b, g, 0, 0)), kvspec, tok, tok,
                  pl.BlockSpec((1, 1, nbs, 8, r), lambda g, b, n: (b, g, n, 0, 0)),
                  pl.BlockSpec((3, 1, SPAN, r), lambda g, b, n: (0, g, 0, 0)),
                  pl.BlockSpec(memory_space=pltpu.SMEM)],
        out_specs=[tok, kvout, kvout,
                   pl.BlockSpec((1, SPAN, r), lambda g, b, n: (g, 0, 0)),
                   pl.BlockSpec((1, 8, r), lambda g, b, n: (g, 0, 0))],
        out_shape=[SDS(qb.shape, F32), SDS((bl, kv, s_len, HEAD_DIM), F32), SDS((bl, kv, s_len, HEAD_DIM), F32),
                   SDS((kv, SPAN, r), F32), SDS((kv, 8, r), F32)],
        scratch_shapes=[pltpu.VMEM((HEAD_DIM, sp), F32), pltpu.VMEM((HEAD_DIM, sp), F32)],
        compiler_params=_cp("arbitrary", "arbitrary", "arbitrary"),
        name="attn_b_bwd",
    )(qb, kb, kbt, vb, do, o, lse, bias_t, sink)


def _bias_reduce(ds_ref, dk_ref, bk_ref, rel_ref, sink_ref):
    kv, _, r = ds_ref.shape
    lane = lax.broadcasted_iota(jnp.int32, (N_BUCKETS, 128), 1)
    lane8 = lax.broadcasted_iota(jnp.int32, (8, 128), 1)
    bk = bk_ref[...]
    for g in range(kv):
        ds = ds_ref[g]
        rowi = lax.broadcasted_iota(jnp.int32, (N_BUCKETS, r), 0)
        red = jnp.zeros((N_BUCKETS, r), F32)
        for b in range(N_BUCKETS):
            red = jnp.where(rowi == b, jnp.sum(jnp.where(bk == b, ds, 0.0), axis=0, keepdims=True), red)
        out = jnp.zeros((N_BUCKETS, 128), F32)
        so = jnp.zeros((8, 128), F32)
        for h in range(GROUP):
            col = jnp.sum(red[:, h * BLOCK:(h + 1) * BLOCK], axis=1, keepdims=True)
            out = jnp.where(lane == h, col, out)
            sc = jnp.sum(dk_ref[g][:, h * BLOCK:(h + 1) * BLOCK], axis=1, keepdims=True)
            so = jnp.where(lane8 == h, sc, so)
        rel_ref[g] = out
        sink_ref[g] = so


def _dqkprep(dqa, dka, dva, dqb, dkb, dvb, proj, h1, cos, sin_signed, gq, gk, s_len, ts):
    t, p_cols = proj.shape
    d = h1.shape[1]
    bl, kva, kvb = dka.shape[0], dka.shape[1], dkb.shape[1]
    ha, hb = dqa.shape[1] // HEAD_DIM, dqb.shape[1] // HEAD_DIM
    ns = s_len // ts

    def body(dqa_ref, dka_ref, dva_ref, dqb_ref, dkb_ref, dvb_ref, p_ref, h1_ref, h1p_ref, cos_ref, sin_ref,
             gq_ref, gk_ref, dp_ref, dgq_ref, dgk_ref, gw_ref, gw_sc, dpp_sc):
        b, i = pl.program_id(0), pl.program_id(1)
        cs, sn = cos_ref[...], sin_ref[...]
        low, first = _pair_masks(ts)

        @pl.when((b == 0) & (i == 0))
        def _():
            dgq_ref[...] = jnp.zeros_like(dgq_ref)
            dgk_ref[...] = jnp.zeros_like(dgk_ref)
            gw_sc[...] = jnp.zeros_like(gw_sc)
            dpp_sc[...] = jnp.zeros_like(dpp_sc)

        n_parts = 6
        pw = p_cols // n_parts

        def wgrad_part(c):
            rows = slice(c * pw, (c + 1) * pw)
            gw_sc[rows, :] += _dot_tn(dpp_sc[:, rows], h1p_ref[...])

        def grad_pair(ref, p):
            return jnp.concatenate([ref[0, 2 * p], ref[0, 2 * p + 1]], axis=1)

        def put(p, val):
            dp_ref[:, p * PAIR:(p + 1) * PAIR] = val.astype(BF16)

        def unrope_norm(d_rot, p, g, dg_ref):
            dn = d_rot * cs + _pair_partner(d_rot * sn, first)
            xp = p_ref[:, p * PAIR:(p + 1) * PAIR]
            r = lax.rsqrt(_pair_mean(xp * xp, low) + EPS)
            n = xp * r
            gd = g * dn
            dg_ref[0:1, :] += jnp.sum(dn * n, axis=0, keepdims=True)
            put(p, r * (gd - n * _pair_mean(n * gd, low)))

        parts = iter(range(n_parts))

        def next_wgrad_part():
            c = next(parts, None)
            if c is not None:
                wgrad_part(c)

        for p in range(ha // 2):
            next_wgrad_part()
            unrope_norm(dqa_ref[:, p * PAIR:(p + 1) * PAIR] * SCALE, p, gq_ref[...], dgq_ref)
        base = ha // 2
        for p in range(kva // 2):
            next_wgrad_part()
            unrope_norm(grad_pair(dka_ref, p), base + p, gk_ref[...], dgk_ref)
            put(base + kva // 2 + p, grad_pair(dva_ref, p))
        base += kva
        for p in range(hb // 2):
            put(base + p, dqb_ref[:, p * PAIR:(p + 1) * PAIR] * SCALE)
        base += hb // 2
        for p in range(kvb // 2):
            put(base + p, grad_pair(dkb_ref, p))
            put(base + kvb // 2 + p, grad_pair(dvb_ref, p))
        for c in parts:
            wgrad_part(c)

        dpp_sc[...] = dp_ref[...]

        @pl.when((b == bl - 1) & (i == ns - 1))
        def _():
            gw_ref[...] = (gw_sc[...] + _dot_tn(dp_ref[...], h1_ref[...])).astype(BF16)

    def hm(nh):
        return pl.BlockSpec((1, nh, ts, HEAD_DIM), lambda b, i: (b, 0, i, 0))

    def tokmajor(nh):
        return pl.BlockSpec((ts, nh * HEAD_DIM), lambda b, i: (b * ns + i, 0))

    vec = pl.BlockSpec((1, PAIR), lambda b, i: (0, 0))
    tab = pl.BlockSpec((ts, PAIR), lambda b, i: (i, 0))
    acc = pl.BlockSpec((8, PAIR), lambda b, i: (0, 0))
    pspec = pl.BlockSpec((ts, p_cols), lambda b, i: (b * ns + i, 0))
    return pl.pallas_call(
        body,
        grid=(bl, ns),
        in_specs=[tokmajor(ha), hm(kva), hm(kva), tokmajor(hb), hm(kvb), hm(kvb), pspec,
                  pl.BlockSpec((ts, d), lambda b, i: (b * ns + i, 0)),
                  pl.BlockSpec((ts, d), lambda b, i: (jnp.maximum(b * ns + i - 1, 0), 0)), tab, tab, vec, vec],
        out_specs=[pspec, acc, acc, pl.BlockSpec((p_cols, d), lambda b, i: (0, 0))],
        out_shape=[SDS((t, p_cols), BF16), SDS((8, PAIR), F32), SDS((8, PAIR), F32), SDS((p_cols, d), BF16)],
        scratch_shapes=[pltpu.VMEM((p_cols, d), F32), pltpu.VMEM((ts, p_cols), BF16)],
        compiler_params=_cp("arbitrary", "arbitrary"),
        name="dqkprep",
    )(dqa, dka, dva, dqb, dkb, dvb, proj, h1, h1, cos, sin_signed, gq, gk)


def _dx_final(dproj, w_t, x2, dx1, g1, tm, grads, small):
    t, d = x2.shape
    p_cols = w_t.shape[0]
    ng, nsm = len(grads), len(small)
    nsteps = t // tm
    kv = small[6].shape[0]

    def body(dp_ref, w_ref, x_ref, dx1_ref, g_ref, *rest):
        (g2_ref, g3_ref, g4_ref, gq_ref, gk_ref, ls_ref, ds_ref, dk_ref, bk_ref), grad_refs = rest[:nsm], rest[nsm:nsm + ng]
        dx_ref, vec_ref, rel_ref = rest[nsm + ng:nsm + ng + 3]
        parts = rest[nsm + ng + 3:nsm + 2 * ng + 3]
        scratch = rest[nsm + 2 * ng + 3:]
        sems, (dg_sc, relg_sc, sinkg_sc), small_scratch = scratch[:3], scratch[3:6], scratch[6:]
        start, wait = _direct_exchange("scatter", grad_refs, parts, *sems)
        start_small, finish_small = _small_allreduce(dg_sc, g2_ref, g3_ref, g4_ref, gq_ref, gk_ref, sinkg_sc, relg_sc,
                                                     ls_ref, vec_ref, rel_ref, *small_scratch)

        @pl.when(pl.program_id(0) == 0)
        def _():
            start()
            dg_sc[...] = jnp.zeros_like(dg_sc)
            _bias_reduce(ds_ref, dk_ref, bk_ref, relg_sc, sinkg_sc)

        dh = _dot(dp_ref[...], w_ref[...])
        g = g_ref[...]
        _, n, r = _rms_fwd(x_ref[...], g)
        dx, dgt = _rms_bwd(n, r, g, dh)
        dx_ref[...] = dx1_ref[...] + dx
        dg_sc[0:1, :] += jnp.sum(dgt, axis=0, keepdims=True)

        @pl.when(pl.program_id(0) == nsteps - 1)
        def _():
            start_small()
            wait()
            finish_small()

    tile = pl.BlockSpec((tm, d), lambda i: (i, 0))
    anyspec = pl.BlockSpec(memory_space=pl.ANY)

    def whole(a):
        return pl.BlockSpec(a.shape, lambda i: (0,) * a.ndim)

    vec_shape, rel_shape = SDS((8, d), F32), SDS((N_BUCKETS, 128), F32)
    res = pl.pallas_call(
        body,
        grid=(nsteps,),
        in_specs=[pl.BlockSpec((tm, p_cols), lambda i: (i, 0)),
                  pl.BlockSpec((p_cols, d), lambda i: (0, 0)),
                  tile, tile, pl.BlockSpec((1, d), lambda i: (0, 0))] + [whole(a) for a in small] + [anyspec] * ng,
        out_specs=[tile, whole(vec_shape), whole(rel_shape)] + [anyspec] * ng,
        out_shape=[SDS((t, d), F32), vec_shape, rel_shape] + [SDS(g.shape, g.dtype) for g in grads],
        scratch_shapes=_exchange_scratch(ng)
        + [pltpu.VMEM((8, d), F32), pltpu.VMEM((kv, N_BUCKETS, 128), F32), pltpu.VMEM((kv, 8, 128), F32)]
        + _small_allreduce_scratch(d),
        compiler_params=_cp("arbitrary"),
        name="dx_final",
    )(dproj, w_t, x2, dx1, g1, *small, *grads)
    return res[0], res[1], res[2], res[3:]


def _adamw_math(w, g, m, v):
    m = ADAM_B1 * m + (1.0 - ADAM_B1) * g
    v = ADAM_B2 * v + (1.0 - ADAM_B2) * (g * g)
    m_hat = m / (1.0 - ADAM_B1 ** ADAM_STEP)
    v_hat = v / (1.0 - ADAM_B2 ** ADAM_STEP)
    delta = -ADAM_LR * (m_hat / (jnp.sqrt(v_hat) + ADAM_EPS) + ADAM_WD * w)
    return delta, m, v


def _adamw_sum(parts, ws, ms, vs, steps):
    nw = len(ws)

    def body(*refs):
        ins, outs = refs[:4 * nw], refs[4 * nw:]
        for k in range(nw):
            p_ref, w_ref, m_ref, v_ref = ins[4 * k:4 * k + 4]
            g_ref, d_ref, nm_ref, nv_ref = outs[4 * k:4 * k + 4]
            g = p_ref[0].astype(F32)
            for s in range(1, N_DEV):
                g = g + p_ref[s].astype(F32)
            g_ref[...] = g
            d_ref[...], nm_ref[...], nv_ref[...] = _adamw_math(w_ref[...], g, m_ref[...], v_ref[...])

    in_specs, out_specs, out_shape, args = [], [], [], []
    for p, w, m, v in zip(parts, ws, ms, vs):
        rows, cols = w.shape
        tile = pl.BlockSpec((rows // steps, cols), lambda i: (i, 0))
        in_specs += [pl.BlockSpec((N_DEV, rows // steps, cols), lambda i: (0, i, 0)), tile, tile, tile]
        out_specs += [tile] * 4
        out_shape += [SDS((rows, cols), F32)] * 4
        args += [p, w, m, v]
    res = pl.pallas_call(
        body,
        grid=(steps,),
        in_specs=in_specs,
        out_specs=out_specs,
        out_shape=out_shape,
        compiler_params=_cp("parallel"),
        name="adamw_weights",
    )(*args)
    return [res[4 * k:4 * k + 4] for k in range(nw)]


def _adamw_small(vec, rel, ws, ms, vs):
    hb = ws[6].shape[1]
    n = len(ws)

    def body(vec_ref, rel_ref, *rest):
        w_refs, m_refs, v_refs = rest[:n], rest[n:2 * n], rest[2 * n:3 * n]
        loss_ref, outs = rest[3 * n], rest[3 * n + 1:]
        grads = [vec_ref[0:1, :], vec_ref[1:2, :], vec_ref[2:3, :], vec_ref[3:4, :],
                 vec_ref[4:5, 0:HEAD_DIM], vec_ref[4:5, SMALL_LANES:SMALL_LANES + HEAD_DIM],
                 vec_ref[4:5, 2 * SMALL_LANES:2 * SMALL_LANES + hb], rel_ref[...].T[0:hb, :]]
        loss_ref[...] = vec_ref[4:5, 3 * SMALL_LANES:3 * SMALL_LANES + 1]
        for p, g in enumerate(grads):
            g_ref, d_ref, nm_ref, nv_ref = outs[4 * p:4 * p + 4]
            g_ref[...] = g
            d_ref[...], nm_ref[...], nv_ref[...] = _adamw_math(w_refs[p][...], g, m_refs[p][...], v_refs[p][...])

    vm = pl.BlockSpec(memory_space=pltpu.VMEM)
    res = pl.pallas_call(
        body,
        in_specs=[vm] * (2 + 3 * n),
        out_specs=[vm] * (1 + 4 * n),
        out_shape=[SDS((1, 1), F32)] + [SDS(w.shape, F32) for w in ws for _ in range(4)],
        name="adamw_small",
    )(vec, rel, *ws, *ms, *vs)
    return res[0], [res[1 + 4 * p:5 + 4 * p] for p in range(n)]


def _local_step(x, loss_target, win_s, wo_s, wup_s, wdn_s, g_pre_mix, g_post_mix, q_norm_a, k_norm_a, sink_b,
                rel_bias_t, g_pre_ffn, g_post_ffn):
    bl, s_len, d = x.shape
    t = bl * s_len
    nh = d // HEAD_DIM
    ha = nh // 2
    kva = ha // GROUP
    hb = nh - ha
    kvb = hb // GROUP
    tm = 512
    tp = min(1024, t)
    tw = min(4096, t)
    ts = min(512, s_len)
    tq, tk = 2 * BLOCK, min(512, s_len // 2)

    x2 = x.reshape(t, d)
    tg2 = loss_target.reshape(t, d)
    cos, sin_signed = _rope_tables(s_len)
    gq2, gk2 = jnp.tile(q_norm_a, (1, 2)), jnp.tile(k_norm_a, (1, 2))
    a = jnp.arange(BLOCK, dtype=jnp.int32)
    c = jnp.arange(SPAN, dtype=jnp.int32)
    bucket_t = _t5_bucket(c[:, None] - BLOCK - a[None, :])
    bucket_t4 = jnp.tile(bucket_t, (1, GROUP))
    (win_g,), bias_t = _weight_gather([win_s], bucket_t, rel_bias_t)
    w_in_t = win_g.reshape(-1, d)
    p_cols = w_in_t.shape[0]

    h1, proj, qa, ka, kat, va, vat, qb, kb, kbt, vb, vbt = _inproj_qkprep(
        x2, g_pre_mix, w_in_t, cos, sin_signed, gq2, gk2, bl, s_len, ha, kva, hb, kvb, ts)
    oa, lse_a, (wo_g, wup_g, wdn_g) = _attn_a_fwd(qa, ka, vat, tq, tk, [wo_s, wup_s, wdn_s])
    wo = wo_g.reshape(-1, d)
    wdn = wdn_g.reshape(-1, d)
    ob, lse_b = _attn_b_fwd(qb, kb, vbt, bias_t, sink_b, s_len)
    mix, x1, h2 = _mixout(oa, ob, wo, x2, g_post_mix, g_pre_ffn, tp)
    u, df, dy, dg4, loss8 = _ffn_fwd(h2, wup_g, wdn, x1, tg2, g_post_ffn, tm)

    dpre, dx1, dmix, dg3, dg2 = _ffn_bwd(df, u, wdn, wup_g, x1, dy, mix, g_pre_ffn, g_post_mix, FFN_BWD_TOKENS)
    gw_dn = _wgrad_rows(u, df, N_DEV, tw, "wgrad_down", square=True)
    gw_up = _wgrad_cols(h2, dpre, N_DEV, tw, "wgrad_up")
    gw_o = _wgrad_o(oa, ob, dmix, N_DEV, min(2048, t))
    doa, dob = _attn_out_bwd(dmix, wo, oa.shape[1], tp)
    dqa, dka, dva, (p_o, p_up, p_dn) = _attn_a_bwd(qa, ka, kat, va, doa, oa, lse_a, tq, tk, [gw_o, gw_up, gw_dn])
    dqb, dkb, dvb, dsum, dsink = _attn_b_bwd(qb, kb, kbt, vb, dob, ob, lse_b, bias_t, sink_b, s_len)
    dproj, dgq, dgk, gw_in_t = _dqkprep(dqa, dka, dva, dqb, dkb, dvb, proj, h1, cos, sin_signed, gq2, gk2, s_len, ts)
    gw_in_t = gw_in_t.reshape(N_DEV, -1, d)
    grad_x, vec, rel, (p_in,) = _dx_final(dproj, w_in_t, x2, dx1, g_pre_mix, tp, [gw_in_t],
                                          [dg2, dg3, dg4, dgq, dgk, loss8, dsum, dsink, bucket_t4])
    return grad_x.reshape(bl, s_len, d), p_in, p_o, p_up, p_dn, vec, rel


def kernel(x, w_in, w_o, g_pre_mix, g_post_mix, q_norm_a, k_norm_a, sink_b, rel_bias, g_pre_ffn, w_ffn_up, w_ffn_down, g_post_ffn, loss_target, m_w_in, m_w_o, m_g_pre_mix, m_g_post_mix, m_q_norm_a, m_k_norm_a, m_sink_b, m_rel_bias, m_g_pre_ffn, m_w_ffn_up, m_w_ffn_down, m_g_post_ffn, v_w_in, v_w_o, v_g_pre_mix, v_g_post_mix, v_q_norm_a, v_k_norm_a, v_sink_b, v_rel_bias, v_g_pre_ffn, v_w_ffn_up, v_w_ffn_down, v_g_post_ffn):
    w_in_t = w_in[0].T
    rel_bias_t = rel_bias.T

    grad_x, p_in, p_o, p_up, p_dn, vec, rel = _local_step(
        x, loss_target, w_in_t.astype(BF16), w_o[0].astype(BF16), w_ffn_up[0].astype(BF16), w_ffn_down[0].astype(BF16),
        g_pre_mix, g_post_mix, q_norm_a, k_norm_a, sink_b, rel_bias_t, g_pre_ffn, g_post_ffn)

    r_in, r_o, r_up, r_dn = _adamw_sum(
        [p_in, p_o, p_up, p_dn],
        [w_in_t, w_o[0], w_ffn_up[0], w_ffn_down[0]],
        [m_w_in[0].T, m_w_o[0], m_w_ffn_up[0], m_w_ffn_down[0]],
        [v_w_in[0].T, v_w_o[0], v_w_ffn_up[0], v_w_ffn_down[0]], 4)
    big = {"w_in": [a.T for a in r_in], "w_o": r_o, "w_up": r_up, "w_dn": r_dn}
    loss, small = _adamw_small(
        vec, rel,
        [g_pre_mix, g_post_mix, g_pre_ffn, g_post_ffn, q_norm_a, k_norm_a, sink_b, rel_bias_t],
        [m_g_pre_mix, m_g_post_mix, m_g_pre_ffn, m_g_post_ffn, m_q_norm_a, m_k_norm_a, m_sink_b, m_rel_bias.T],
        [v_g_pre_mix, v_g_post_mix, v_g_pre_ffn, v_g_post_ffn, v_q_norm_a, v_k_norm_a, v_sink_b, v_rel_bias.T])
    s_pre_mix, s_post_mix, s_pre_ffn, s_post_ffn, s_qn, s_kn, s_sink, s_rel_t = small
    s_rel = [a.T for a in s_rel_t]

    def outs(kind):
        return [big["w_in"][kind][None], big["w_o"][kind][None], s_pre_mix[kind], s_post_mix[kind], s_qn[kind],
                s_kn[kind], s_sink[kind], s_rel[kind], s_pre_ffn[kind], big["w_up"][kind][None],
                big["w_dn"][kind][None], s_post_ffn[kind]]

    return (loss.reshape(()), grad_x, *outs(0), *outs(1), *outs(2), *outs(3))
```

```python
import functools

import jax
import jax.numpy as jnp
import numpy as np
from jax import lax
from jax.experimental import pallas as pl
from jax.experimental.pallas import tpu as pltpu

F32 = jnp.float32
BF16 = jnp.bfloat16
SDS = jax.ShapeDtypeStruct

N_DEV = 8
HEAD_DIM = 64
GROUP = 4
BLOCK = 128
SPAN = 3 * BLOCK
GRID_W = 64
N_BUCKETS = 32
MAX_DISTANCE = 128
ROPE_THETA = 10000.0
EPS = 1e-6
NEG_INF = -1e30
SCALE = HEAD_DIM ** -0.5
VT_PAD = 16

ADAM_LR = 0.001
ADAM_B1 = 0.9
ADAM_B2 = 0.999
ADAM_EPS = 1e-08
ADAM_WD = 0.01
ADAM_STEP = 10

VMEM_LIMIT = 56 * 1024 * 1024
MESH = pl.DeviceIdType.MESH


def _cp(*sem):
    return pltpu.CompilerParams(dimension_semantics=sem, vmem_limit_bytes=VMEM_LIMIT)


def _dot(a, b):
    return jnp.dot(a, b, preferred_element_type=F32)


def _dot_nt(a, b):
    return lax.dot_general(a, b, (((1,), (1,)), ((), ())), preferred_element_type=F32)


def _dot_tn(a, b):
    return lax.dot_general(a, b, (((0,), (0,)), ((), ())), preferred_element_type=F32)


def _rms_fwd(x, g):
    r = lax.rsqrt(jnp.mean(x * x, axis=-1, keepdims=True) + EPS)
    n = x * r
    return n * g, n, r


def _rms_bwd(n, r, g, dy):
    gd = g * dy
    dx = r * (gd - n * jnp.mean(n * gd, axis=-1, keepdims=True))
    return dx, dy * n


def _rope_tables(s_len):
    rows = s_len // GRID_W
    row = np.repeat(np.arange(rows, dtype=np.int32), GRID_W)
    col = np.tile(np.arange(GRID_W, dtype=np.int32), rows)
    nf = HEAD_DIM // 4
    freqs = np.float32(ROPE_THETA) ** (-np.arange(nf, dtype=np.float32) / np.float32(nf))
    ang_r = row.astype(np.float32)[:, None] * freqs[None, :]
    ang_c = col.astype(np.float32)[:, None] * freqs[None, :]
    cr, sr, cc, sc = np.cos(ang_r), np.sin(ang_r), np.cos(ang_c), np.sin(ang_c)
    cos = np.concatenate([cr, cr, cc, cc] * 2, axis=-1).astype(np.float32)
    sin_signed = np.concatenate([-sr, sr, -sc, sc] * 2, axis=-1).astype(np.float32)
    return jnp.asarray(cos), jnp.asarray(sin_signed)


def _t5_bucket(rel):
    nb = N_BUCKETS // 2
    ret = (rel > 0).astype(jnp.int32) * nb
    n = jnp.abs(rel)
    max_exact = nb // 2
    nf = jnp.maximum(n, 1).astype(F32)
    large = max_exact + (jnp.log(nf / max_exact) / np.float32(np.log(MAX_DISTANCE / max_exact))
                         * (nb - max_exact)).astype(jnp.int32)
    large = jnp.minimum(large, nb - 1)
    return ret + jnp.where(n < max_exact, n, large)


def _mesh_pos():
    return lax.axis_index("x"), lax.axis_index("y"), lax.axis_index("c")


def _lin(p):
    return 4 * p[0] + 2 * p[1] + p[2]


def _bias_tables(bkt_ref, tbl_ref, out_ref, hb):
    bkt = bkt_ref[...]
    ci = lax.broadcasted_iota(jnp.int32, (SPAN, BLOCK), 0)
    qi = lax.broadcasted_iota(jnp.int32, (SPAN, BLOCK), 1)
    band = jnp.abs(ci - BLOCK - qi) <= BLOCK
    masks = (band, band & (ci >= BLOCK), band & (ci < 2 * BLOCK))
    for h in range(hb):
        acct = jnp.zeros((SPAN, BLOCK), F32)
        for b in range(N_BUCKETS):
            acct = jnp.where(bkt == b, tbl_ref[h, b], acct)
        lanes = slice((h % GROUP) * BLOCK, (h % GROUP + 1) * BLOCK)
        for var, mask in enumerate(masks):
            out_ref[var, h // GROUP, :, lanes] = jnp.where(mask, acct, NEG_INF)


def _weight_gather(shards, bucket_t, rel_bias_t):
    n = len(shards)
    hb = rel_bias_t.shape[0]

    def body(*refs):
        xs, (bkt_ref, tbl_ref), outs, bias_ref = refs[:n], refs[n:n + 2], refs[n + 2:2 * n + 2], refs[2 * n + 2]
        send_sems, recv_sems, local_sems = refs[2 * n + 3:]
        x, y, c = _mesh_pos()
        me, sibling = (x, y, c), (x, y, 1 - c)
        chips = [(1 - x, y), (x, 1 - y), (1 - x, 1 - y)]

        def copy(a, k, block, to, src=None):
            slot = outs[a].at[_lin(block)]
            return pltpu.make_async_remote_copy(
                src_ref=slot if src is None else src, dst_ref=slot,
                send_sem=send_sems.at[a, k], recv_sem=recv_sems.at[a, k],
                device_id=to, device_id_type=MESH)

        started = []
        for a in range(n):
            mine = pltpu.make_async_copy(xs[a], outs[a].at[_lin(me)], local_sems.at[a])
            mine.start()
            started.append(mine)
        sends = []
        for a in range(n):
            first = [copy(a, 0, me, sibling, src=xs[a])]
            first += [copy(a, 1 + j, me, (*chip, c), src=xs[a]) for j, chip in enumerate(chips)]
            for cp in first:
                cp.start()
            sends += first
        _bias_tables(bkt_ref, tbl_ref, bias_ref, hb)
        for a in range(n):
            for j, chip in enumerate(chips):
                copy(a, 1 + j, (*chip, c), me).wait_recv()
                fwd = copy(a, 4 + j, (*chip, c), sibling)
                fwd.start()
                sends.append(fwd)
        for a in range(n):
            copy(a, 0, sibling, me).wait_recv()
            for j, chip in enumerate(chips):
                copy(a, 4 + j, (*chip, 1 - c), me).wait_recv()
        for cp in sends:
            cp.wait_send()
        for mine in started:
            mine.wait()

    anyspec = pl.BlockSpec(memory_space=pl.ANY)
    vm = pl.BlockSpec(memory_space=pltpu.VMEM)
    res = pl.pallas_call(
        body,
        out_shape=[SDS((N_DEV,) + s.shape, s.dtype) for s in shards]
        + [SDS((3, hb // GROUP, SPAN, GROUP * BLOCK), F32)],
        in_specs=[anyspec] * n + [vm, pl.BlockSpec(memory_space=pltpu.SMEM)],
        out_specs=[anyspec] * n + [vm],
        scratch_shapes=[pltpu.SemaphoreType.DMA((n, 7)), pltpu.SemaphoreType.DMA((n, 7)),
                        pltpu.SemaphoreType.DMA((n,))],
        name="weight_gather",
    )(*shards, bucket_t, rel_bias_t)
    return res[:n], res[n]


def _direct_exchange(kind, ins, outs, send_sems, recv_sems, local_sems):
    x, y, c = _mesh_pos()
    me = (x, y, c)
    peers = [(x, y, 1 - c), (1 - x, y, c), (x, 1 - y, c), (1 - x, 1 - y, c),
             (1 - x, y, 1 - c), (x, 1 - y, 1 - c), (1 - x, 1 - y, 1 - c)]

    def src(a, to):
        return ins[a] if kind == "gather" else ins[a].at[_lin(to)]

    def remote(a, k, to, frm):
        return pltpu.make_async_remote_copy(
            src_ref=src(a, to), dst_ref=outs[a].at[_lin(frm)],
            send_sem=send_sems.at[a, k], recv_sem=recv_sems.at[a, k],
            device_id=to, device_id_type=MESH)

    n = len(ins)
    sends = [remote(a, k, p, me) for a in range(n) for k, p in enumerate(peers)]
    arrivals = [remote(a, k, p, p) for a in range(n) for k, p in enumerate(peers)]
    local = [pltpu.make_async_copy(src(a, me), outs[a].at[_lin(me)], local_sems.at[a]) for a in range(n)]

    def start():
        for cp in local + sends:
            cp.start()

    def wait():
        for cp in arrivals:
            cp.wait_recv()
        for cp in sends:
            cp.wait_send()
        for cp in local:
            cp.wait()

    return start, wait


def _exchange_scratch(n):
    return [pltpu.SemaphoreType.DMA((n, 7)), pltpu.SemaphoreType.DMA((n, 7)), pltpu.SemaphoreType.DMA((n,))]


SMALL_LANES = 128


def _small_allreduce(g1_ref, g2_ref, g3_ref, g4_ref, gq_ref, gk_ref, sk_ref, rl_ref, ls_ref, vec_ref, rel_ref,
                     vbuf, rbuf, vland, rland, send_sems, recv_sems):
    kv = sk_ref.shape[0]
    x, y, c = _mesh_pos()
    me = (x, y, c)
    peers = [(x, y, 1 - c), (1 - x, y, c), (x, 1 - y, c), (1 - x, 1 - y, c),
             (1 - x, y, 1 - c), (x, 1 - y, 1 - c), (1 - x, 1 - y, 1 - c)]

    def copies(k, to, frm):
        return [pltpu.make_async_remote_copy(
            src_ref=buf, dst_ref=land.at[_lin(frm)], send_sem=send_sems.at[a, k], recv_sem=recv_sems.at[a, k],
            device_id=to, device_id_type=MESH) for a, (buf, land) in enumerate(((vbuf, vland), (rbuf, rland)))]

    sends = [cp for k, p in enumerate(peers) for cp in copies(k, p, me)]

    def start():
        vbuf[...] = jnp.zeros_like(vbuf)
        rbuf[...] = jnp.zeros_like(rbuf)
        for row, ref in enumerate((g1_ref, g2_ref, g3_ref, g4_ref)):
            vbuf[row:row + 1, :] = ref[0:1, :]
        vbuf[4:5, 0:HEAD_DIM] = gq_ref[0:1, 0:HEAD_DIM] + gq_ref[0:1, HEAD_DIM:PAIR]
        vbuf[4:5, SMALL_LANES:SMALL_LANES + HEAD_DIM] = gk_ref[0:1, 0:HEAD_DIM] + gk_ref[0:1, HEAD_DIM:PAIR]
        for g in range(kv):
            vbuf[4:5, 2 * SMALL_LANES + g * GROUP:2 * SMALL_LANES + (g + 1) * GROUP] = sk_ref[g, 0:1, 0:GROUP]
            rbuf[:, g * GROUP:(g + 1) * GROUP] = rl_ref[g, :, 0:GROUP]
        vbuf[4:5, 3 * SMALL_LANES:3 * SMALL_LANES + 1] = ls_ref[0:1, 0:1]
        for cp in sends:
            cp.start()
        vland[_lin(me)] = vbuf[...]
        rland[_lin(me)] = rbuf[...]

    def finish():
        for k, p in enumerate(peers):
            for cp in copies(k, p, p):
                cp.wait_recv()
        for cp in sends:
            cp.wait_send()
        vacc, racc = vland[0], rland[0]
        for s in range(1, N_DEV):
            vacc, racc = vacc + vland[s], racc + rland[s]
        vec_ref[...] = vacc
        rel_ref[...] = racc

    return start, finish


def _small_allreduce_scratch(d):
    return [pltpu.VMEM((8, d), F32), pltpu.VMEM((N_BUCKETS, 128), F32),
            pltpu.VMEM((N_DEV, 8, d), F32), pltpu.VMEM((N_DEV, N_BUCKETS, 128), F32),
            pltpu.SemaphoreType.DMA((2, 7)), pltpu.SemaphoreType.DMA((2, 7))]


PAIR = 2 * HEAD_DIM


def _pair_masks(ts):
    lane = lax.broadcasted_iota(jnp.int32, (ts, PAIR), 1)
    return lane < HEAD_DIM, (lane % 32) < 16


def _pair_mean(v, low):
    del low
    r = lax.broadcasted_iota(jnp.int32, (PAIR, PAIR), 0) // HEAD_DIM
    c = lax.broadcasted_iota(jnp.int32, (PAIR, PAIR), 1) // HEAD_DIM
    same_head = (r == c).astype(BF16)
    hi = v.astype(BF16)
    lo = (v - hi.astype(F32)).astype(BF16)
    return (_dot(hi, same_head) + _dot(lo, same_head)) * (1.0 / HEAD_DIM)


def _pair_partner(v, first):
    return jnp.where(first, pltpu.roll(v, PAIR - 16, 1), pltpu.roll(v, 16, 1))


def _inproj_qkprep(x2, g1, w_t, cos, sin_signed, gq, gk, bl, s_len, ha, kva, hb, kvb, ts):
    t, d = x2.shape
    p_cols = w_t.shape[0]
    assert ha % 2 == 0 and kva % 2 == 0 and hb % 2 == 0 and kvb % 2 == 0
    ns = s_len // ts
    nt = bl * ns
    sp = s_len + 2 * BLOCK

    def body(*refs):
        kb_ref, kbt_ref, vb_ref, vbt_ref, p_even, p_odd = refs[-6:]
        s = pl.program_id(0)
        i = lax.rem(jnp.maximum(s - 1, 0), ns)

        @pl.when(s == 0)
        def _():
            p_odd[...] = jnp.zeros_like(p_odd)

        @pl.when(i == 0)
        def _():
            zeros = jnp.zeros((kvb, BLOCK, HEAD_DIM), BF16)
            zeros_t = jnp.zeros((kvb, HEAD_DIM + VT_PAD, BLOCK), BF16)
            for ref in (kb_ref, vb_ref):
                ref[0, :, 0:BLOCK, :] = zeros
                ref[0, :, sp - BLOCK:sp, :] = zeros
            kbt_ref[0, :, :, 0:BLOCK] = zeros_t[:, 0:HEAD_DIM]
            kbt_ref[0, :, :, sp - BLOCK:sp] = zeros_t[:, 0:HEAD_DIM]
            vbt_ref[0, :, :, 0:BLOCK] = zeros_t
            vbt_ref[0, :, :, sp - BLOCK:sp] = zeros_t

        even = lax.rem(s, 2) == 0
        pl.when(even)(functools.partial(tile_work, p_even, p_odd, i, *refs[:-2]))
        pl.when(jnp.logical_not(even))(functools.partial(tile_work, p_odd, p_even, i, *refs[:-2]))

    def tile_work(p_new, p_ref, i, x_ref, g1_ref, w_ref, cos_ref, sin_ref, gq_ref, gk_ref, h_ref, po_ref, qa_ref,
                  ka_ref, kat_ref, va_ref, vat_ref, qb_ref, kb_ref, kbt_ref, vb_ref, vbt_ref):
        y, _, _ = _rms_fwd(x_ref[...], g1_ref[...])
        h = y.astype(BF16)
        h_ref[...] = h
        n_parts = 6
        pw = p_cols // n_parts

        def project(c):
            p_new[:, c * pw:(c + 1) * pw] = _dot_nt(h, w_ref[c * pw:(c + 1) * pw, :])

        cs, sn = cos_ref[...], sin_ref[...]
        low, first = _pair_masks(ts)
        ones_row = (lax.broadcasted_iota(jnp.int32, (VT_PAD, ts), 0) == 0).astype(BF16)
        heads = (slice(0, HEAD_DIM), slice(HEAD_DIM, PAIR))

        def pair(p):
            v = p_ref[:, p * PAIR:(p + 1) * PAIR]
            po_ref[:, p * PAIR:(p + 1) * PAIR] = v
            return v

        def normrope(x, g):
            y = x * lax.rsqrt(_pair_mean(x * x, low) + EPS) * g
            return y * cs + _pair_partner(y, first) * sn

        eye = (lax.broadcasted_iota(jnp.int32, (PAIR, PAIR), 0)
               == lax.broadcasted_iota(jnp.int32, (PAIR, PAIR), 1)).astype(BF16)

        def transposed(xb):
            return _dot_nt(eye, xb).astype(BF16)

        def prep_qa(p):
            qa_ref[:, p * PAIR:(p + 1) * PAIR] = (normrope(pair(p), gq_ref[...]) * SCALE).astype(BF16)

        def prep_kva(p):
            base = ha // 2
            k = normrope(pair(base + p), gk_ref[...]).astype(BF16)
            v = pair(base + kva // 2 + p).astype(BF16)
            kt, vt = transposed(k), transposed(v)
            for e, lanes in enumerate(heads):
                ka_ref[0, 2 * p + e] = k[:, lanes]
                va_ref[0, 2 * p + e] = v[:, lanes]
                kat_ref[0, 2 * p + e] = kt[lanes, :]
                vat_ref[0, 2 * p + e, 0:HEAD_DIM, :] = vt[lanes, :]
                vat_ref[0, 2 * p + e, HEAD_DIM:HEAD_DIM + VT_PAD, :] = ones_row

        def prep_qb(p):
            base = ha // 2 + kva
            qb_ref[:, p * PAIR:(p + 1) * PAIR] = (pair(base + p) * SCALE).astype(BF16)

        rows = pl.ds(pl.multiple_of(BLOCK + i * ts, BLOCK), ts)

        def prep_kvb(p):
            base = ha // 2 + kva + hb // 2
            k = pair(base + p).astype(BF16)
            v = pair(base + kvb // 2 + p).astype(BF16)
            kt, vt = transposed(k), transposed(v)
            for e, lanes in enumerate(heads):
                kb_ref[0, 2 * p + e, rows, :] = k[:, lanes]
                vb_ref[0, 2 * p + e, rows, :] = v[:, lanes]
                kbt_ref[0, 2 * p + e, :, rows] = kt[lanes, :]
                vbt_ref[0, 2 * p + e, 0:HEAD_DIM, rows] = vt[lanes, :]
                vbt_ref[0, 2 * p + e, HEAD_DIM:HEAD_DIM + VT_PAD, rows] = ones_row

        work = ([functools.partial(prep_qa, p) for p in range(ha // 2)]
                + [functools.partial(prep_kva, p) for p in range(kva // 2)]
                + [functools.partial(prep_qb, p) for p in range(hb // 2)]
                + [functools.partial(prep_kvb, p) for p in range(kvb // 2)])
        per_part = -(-len(work) // n_parts)
        for c in range(n_parts):
            for item in work[c * per_part:(c + 1) * per_part]:
                item()
            project(c)

    def cur(s):
        return jnp.minimum(s, nt - 1)

    def prev(s):
        return jnp.maximum(s - 1, 0) // ns, lax.rem(jnp.maximum(s - 1, 0), ns)

    def hm(nh):
        return pl.BlockSpec((1, nh, ts, HEAD_DIM), lambda s: (prev(s)[0], 0, prev(s)[1], 0))

    def hm_t(nh, rows):
        return pl.BlockSpec((1, nh, rows, ts), lambda s: (prev(s)[0], 0, 0, prev(s)[1]))

    def tokmajor(nh):
        return pl.BlockSpec((ts, nh * HEAD_DIM), lambda s: (jnp.maximum(s - 1, 0), 0))

    def padded(nh):
        return pl.BlockSpec((1, nh, sp, HEAD_DIM), lambda s: (prev(s)[0], 0, 0, 0))

    def padded_t(nh, rows):
        return pl.BlockSpec((1, nh, rows, sp), lambda s: (prev(s)[0], 0, 0, 0))

    tab = pl.BlockSpec((ts, PAIR), lambda s: (prev(s)[1], 0))
    vec = pl.BlockSpec((1, PAIR), lambda s: (0, 0))
    return pl.pallas_call(
        body,
        grid=(nt + 1,),
        in_specs=[pl.BlockSpec((ts, d), lambda s: (cur(s), 0)),
                  pl.BlockSpec((1, d), lambda s: (0, 0)),
                  pl.BlockSpec((p_cols, d), lambda s: (0, 0)),
                  tab, tab, vec, vec],
        out_specs=[pl.BlockSpec((ts, d), lambda s: (cur(s), 0)), tokmajor(p_cols // HEAD_DIM),
                   tokmajor(ha), hm(kva), hm_t(kva, HEAD_DIM), hm(kva), hm_t(kva, HEAD_DIM + VT_PAD),
                   tokmajor(hb), padded(kvb), padded_t(kvb, HEAD_DIM), padded(kvb),
                   padded_t(kvb, HEAD_DIM + VT_PAD)],
        out_shape=[SDS((t, d), BF16), SDS((t, p_cols), F32),
                   SDS((t, ha * HEAD_DIM), BF16), SDS((bl, kva, s_len, HEAD_DIM), BF16),
                   SDS((bl, kva, HEAD_DIM, s_len), BF16),
                   SDS((bl, kva, s_len, HEAD_DIM), BF16), SDS((bl, kva, HEAD_DIM + VT_PAD, s_len), BF16),
                   SDS((t, hb * HEAD_DIM), BF16),
                   SDS((bl, kvb, sp, HEAD_DIM), BF16), SDS((bl, kvb, HEAD_DIM, sp), BF16),
                   SDS((bl, kvb, sp, HEAD_DIM), BF16), SDS((bl, kvb, HEAD_DIM + VT_PAD, sp), BF16)],
        scratch_shapes=[pltpu.VMEM((ts, p_cols), F32)] * 2,
        compiler_params=_cp("arbitrary"),
        name="inproj_qkprep",
    )(x2, g1, w_t, cos, sin_signed, gq, gk)


def _attn_a_fwd(qa, ka, vat, tq, tk, shards):
    bl, kv, s_len, _ = ka.shape
    ha = qa.shape[1] // HEAD_DIM
    va_rows = vat.shape[2]
    nq, nk = s_len // tq, s_len // tk
    assert nk % 2 == 0
    r = GROUP * tq
    ns = len(shards)

    def body(q_ref, qn_ref, k_ref, v_ref, *rest):
        shard_refs, (o_ref, l_ref), gathered = rest[:ns], rest[ns:ns + 2], rest[ns + 2:2 * ns + 2]
        st_sc, send_sems, recv_sems, local_sems = rest[2 * ns + 2:]
        i = pl.program_id(2)
        step_id = (pl.program_id(0) * kv + pl.program_id(1)) * nq + i
        start, wait = _direct_exchange("gather", shard_refs, gathered, send_sems, recv_sems, local_sems)
        pl.when(step_id == 0)(start)

        q = _heads_t(q_ref[...]).astype(BF16)

        def scores(c, qv):
            return _dot(k_ref[0, 0, pl.ds(pl.multiple_of(c * tk, tk), tk), :], qv)

        def fold(st, c, carry):
            m_old, acc = carry
            m_new = jnp.maximum(m_old, jnp.max(st, axis=0, keepdims=True))
            pt = jnp.exp(st - m_new).astype(BF16)
            vt = v_ref[0, 0, :, pl.ds(pl.multiple_of(c * tk, tk), tk)]
            return m_new, jnp.exp(m_old - m_new) * acc + _dot(vt, pt)

        @pl.when(i == 0)
        def _():
            st_sc[0] = scores(0, q)

        def step(c2, carry):
            c = 2 * c2
            st_sc[1] = scores(c + 1, q)
            carry = fold(st_sc[0], c, carry)
            st_sc[0] = scores(c + 2, q)
            return fold(st_sc[1], c + 1, carry)

        carry = (jnp.full((1, r), -jnp.inf, F32), jnp.zeros((va_rows, r), F32))
        for c2 in range(nk // 2 - 1):
            carry = step(c2, carry)
        st_sc[1] = scores(nk - 1, q)
        carry = fold(st_sc[0], nk - 2, carry)
        st_sc[0] = scores(0, _heads_t(qn_ref[...]).astype(BF16))
        m, acc = fold(st_sc[1], nk - 1, carry)
        l = acc[HEAD_DIM:HEAD_DIM + 1, :]
        o_ref[...] = _heads_t_inv(acc[0:HEAD_DIM, :] / l).astype(BF16)
        l_ref[0, 0, 0] = jnp.broadcast_to(m + jnp.log(l), (8, r))
        pl.when(step_id == bl * kv * nq - 1)(wait)

    anyspec = pl.BlockSpec(memory_space=pl.ANY)
    res = pl.pallas_call(
        body,
        grid=(bl, kv, nq),
        in_specs=[pl.BlockSpec((tq, GROUP * HEAD_DIM), lambda b, g, i: (b * nq + i, g)),
                  pl.BlockSpec((tq, GROUP * HEAD_DIM), lambda b, g, i: (b * nq + jnp.minimum(i + 1, nq - 1), g)),
                  pl.BlockSpec((1, 1, s_len, HEAD_DIM), lambda b, g, i: (b, g, 0, 0)),
                  pl.BlockSpec((1, 1, va_rows, s_len), lambda b, g, i: (b, g, 0, 0))] + [anyspec] * ns,
        out_specs=[pl.BlockSpec((tq, GROUP * HEAD_DIM), lambda b, g, i: (b * nq + i, g)),
                   pl.BlockSpec((1, 1, 1, 8, r), lambda b, g, i: (b, g, i, 0, 0))] + [anyspec] * ns,
        out_shape=[SDS((bl * s_len, ha * HEAD_DIM), BF16), SDS((bl, kv, nq, 8, r), F32)]
        + [SDS((N_DEV,) + s.shape, s.dtype) for s in shards],
        scratch_shapes=[pltpu.VMEM((2, tk, r), F32)] + _exchange_scratch(ns),
        compiler_params=_cp("arbitrary", "arbitrary", "arbitrary"),
        name="attn_a_fwd",
    )(qa, qa, ka, vat, *shards)
    return res[0], res[1], res[2:]


FFN_BWD_TOKENS = 256
QB_PER_STEP = 16


def _bias_variant(n, nb):
    return jnp.where(n == 0, 1, jnp.where(n == nb - 1, 2, 0))


def _sink_row(sink_ref, g):
    return jnp.concatenate([jnp.full((1, BLOCK), sink_ref[0, g * GROUP + h], F32) for h in range(GROUP)], axis=1)


def _attn_b_fwd(qb, kb, vbt, bias_t, sink, s_len):
    bl, kv, sp, _ = kb.shape
    hb = qb.shape[1] // HEAD_DIM
    vt_rows = vbt.shape[2]
    nb = s_len // BLOCK
    nbs = min(QB_PER_STEP, nb)
    r = GROUP * BLOCK

    def body(q_ref, k_ref, vt_ref, bt_ref, sink_ref, o_ref, l_ref, st_sc, pb_sc):
        g, n0 = pl.program_id(1), pl.program_id(2) * nbs
        sink_row = _sink_row(sink_ref, g)

        def span(j):
            return pl.ds(pl.multiple_of((n0 + j) * BLOCK, BLOCK), SPAN)

        for j in range(nbs):
            qt = _heads_t(q_ref[j * BLOCK:(j + 1) * BLOCK, :]).astype(BF16)
            st_sc[j] = _dot(k_ref[0, 0, span(j), :], qt) + bt_ref[_bias_variant(n0 + j, nb), 0]
        maxes = []
        for j in range(nbs):
            st = st_sc[j]
            m = jnp.maximum(jnp.max(st, axis=0, keepdims=True), sink_row)
            pb_sc[j] = jnp.exp(st - m).astype(BF16)
            maxes.append(m)
        for j in range(nbs):
            m = maxes[j]
            acc = _dot(vt_ref[0, 0, :, span(j)], pb_sc[j])
            l = acc[HEAD_DIM:HEAD_DIM + 1, :] + jnp.exp(sink_row - m)
            o_ref[j * BLOCK:(j + 1) * BLOCK, :] = _heads_t_inv(acc[0:HEAD_DIM, :] / l).astype(BF16)
            l_ref[0, 0, j] = jnp.broadcast_to(m + jnp.log(l), (8, r))

    return pl.pallas_call(
        body,
        grid=(bl, kv, nb // nbs),
        in_specs=[pl.BlockSpec((nbs * BLOCK, GROUP * HEAD_DIM), lambda b, g, n: (b * (nb // nbs) + n, g)),
                  pl.BlockSpec((1, 1, sp, HEAD_DIM), lambda b, g, n: (b, g, 0, 0)),
                  pl.BlockSpec((1, 1, vt_rows, sp), lambda b, g, n: (b, g, 0, 0)),
                  pl.BlockSpec((3, 1, SPAN, r), lambda b, g, n: (0, g, 0, 0)),
                  pl.BlockSpec(memory_space=pltpu.SMEM)],
        out_specs=[pl.BlockSpec((nbs * BLOCK, GROUP * HEAD_DIM), lambda b, g, n: (b * (nb // nbs) + n, g)),
                   pl.BlockSpec((1, 1, nbs, 8, r), lambda b, g, n: (b, g, n, 0, 0))],
        out_shape=[SDS((bl * s_len, hb * HEAD_DIM), BF16), SDS((bl, kv, nb, 8, r), F32)],
        scratch_shapes=[pltpu.VMEM((nbs, SPAN, r), F32), pltpu.VMEM((nbs, SPAN, r), BF16)],
        compiler_params=_cp("parallel", "parallel", "arbitrary"),
        name="attn_b_fwd",
    )(qb, kb, vbt, bias_t, sink)


def _mixout(oa, ob, wo, x2, g2, g3, tm):
    t, d = x2.shape
    ca = oa.shape[1]

    def body(oa_ref, ob_ref, w_ref, x_ref, g2_ref, g3_ref, mix_ref, x1_ref, h2_ref):
        mix = _dot(oa_ref[...], w_ref[0:ca, :]) + _dot(ob_ref[...], w_ref[ca:, :])
        mix_ref[...] = mix
        y2, _, _ = _rms_fwd(mix, g2_ref[...])
        x1 = x_ref[...] + y2
        x1_ref[...] = x1
        y3, _, _ = _rms_fwd(x1, g3_ref[...])
        h2_ref[...] = y3.astype(BF16)

    tile = lambda w: pl.BlockSpec((tm, w), lambda i: (i, 0))
    vec = pl.BlockSpec((1, d), lambda i: (0, 0))
    return pl.pallas_call(
        body,
        grid=(t // tm,),
        in_specs=[tile(ca), tile(ob.shape[1]), pl.BlockSpec(wo.shape, lambda i: (0, 0)), tile(d), vec, vec],
        out_specs=[tile(d), tile(d), tile(d)],
        out_shape=[SDS((t, d), F32), SDS((t, d), F32), SDS((t, d), BF16)],
        compiler_params=_cp("parallel"),
        name="mixout",
    )(oa, ob, wo, x2, g2, g3)


def _ffn_fwd(h2, wup_g, wdn, x1, target, g4, tm):
    t, d = x1.shape
    nblk, _, tf = wup_g.shape
    ff = nblk * tf
    nt = t // tm

    def body(h_ref, wu_hbm, wd_hbm, x1_ref, tg_ref, g_ref, u_ref, df_ref, dy_ref, dg_ref, loss_ref,
             f_sc, wu_sc, wd_sc, sems):
        s = pl.program_id(0)
        loads = [pltpu.make_async_copy(wu_hbm.at[c], wu_sc.at[c], sems.at[c]) for c in range(nblk)]
        loads.append(pltpu.make_async_copy(wd_hbm, wd_sc, sems.at[nblk]))

        def after_norm():
            g = g_ref[...]
            y4, n, r = _rms_fwd(f_sc[...], g)
            e = (x1_ref[...] + y4) - tg_ref[...]
            loss_ref[...] += jnp.sum(e * e) * (0.5 / d)
            dy = e * (1.0 / d)
            dy_ref[...] = dy
            yield
            df, dgt = _rms_bwd(n, r, g, dy)
            df_ref[...] = df.astype(BF16)
            dg_ref[0:1, :] += jnp.sum(dgt, axis=0, keepdims=True)
            yield

        def tile(first):
            pending = iter(()) if first else after_norm()
            h = h_ref[...]
            squares = []
            for c in range(nblk):
                if first:
                    loads[c].wait()
                u = jnp.maximum(_dot(h, wu_sc[c]), 0.0)
                u_ref[:, c * tf:(c + 1) * tf] = u.astype(BF16)
                squares.append((u * u).astype(BF16))
                next(pending, None)
            if first:
                loads[nblk].wait()
            f_sc[...] = _dot(jnp.concatenate(squares, axis=1), wd_sc[...])

        @pl.when(s == 0)
        def _():
            for cp in loads:
                cp.start()
            dg_ref[...] = jnp.zeros_like(dg_ref)
            loss_ref[...] = jnp.zeros_like(loss_ref)
            tile(True)

        pl.when((s > 0) & (s < nt))(functools.partial(tile, False))

        @pl.when(s == nt)
        def _():
            for _ in after_norm():
                pass

    cur = lambda s: (jnp.minimum(s, nt - 1), 0)
    prev = lambda s: (jnp.maximum(s - 1, 0), 0)
    anyspec = pl.BlockSpec(memory_space=pl.ANY)
    return pl.pallas_call(
        body,
        grid=(nt + 1,),
        in_specs=[pl.BlockSpec((tm, d), cur), anyspec, anyspec,
                  pl.BlockSpec((tm, d), prev), pl.BlockSpec((tm, d), prev),
                  pl.BlockSpec((1, d), lambda s: (0, 0))],
        out_specs=[pl.BlockSpec((tm, ff), cur), pl.BlockSpec((tm, d), prev), pl.BlockSpec((tm, d), prev),
                   pl.BlockSpec((8, d), lambda s: (0, 0)),
                   pl.BlockSpec((8, 128), lambda s: (0, 0))],
        out_shape=[SDS((t, ff), BF16), SDS((t, d), BF16), SDS((t, d), F32), SDS((8, d), F32), SDS((8, 128), F32)],
        scratch_shapes=[pltpu.VMEM((tm, d), F32), pltpu.VMEM((nblk, d, tf), BF16), pltpu.VMEM((ff, d), BF16),
                        pltpu.SemaphoreType.DMA((nblk + 1,))],
        compiler_params=_cp("arbitrary"),
        name="ffn_fwd",
    )(h2, wup_g, wdn, x1, target, g4)


def _ffn_bwd(df, u, wdn, wup_g, x1, dy, mix, g3, g2, tm):
    t, d = x1.shape
    nblk, _, tf = wup_g.shape
    ff = nblk * tf
    nt = t // tm

    def body(df_ref, u_ref, wd_hbm, wu_hbm, x1_ref, dy_ref, mix_ref, g3_ref, g2_ref,
             dpre_ref, dx1_ref, dmix_ref, dg3_ref, dg2_ref, dh_sc, wd_sc, wu_sc, sems):
        s = pl.program_id(0)
        blocks = [slice(c * tf, (c + 1) * tf) for c in range(nblk)]
        wd_loads = [pltpu.make_async_copy(wd_hbm.at[blocks[c], :], wd_sc.at[blocks[c], :], sems.at[c])
                    for c in range(nblk)]
        wu_loads = [pltpu.make_async_copy(wu_hbm.at[c], wu_sc.at[c], sems.at[nblk + c]) for c in range(nblk)]

        def norms_bwd():
            g3 = g3_ref[...]
            _, n3, r3 = _rms_fwd(x1_ref[...], g3)
            dx, dgt3 = _rms_bwd(n3, r3, g3, dh_sc[...])
            dx1 = dy_ref[...] + dx
            dx1_ref[...] = dx1
            dg3_ref[0:1, :] += jnp.sum(dgt3, axis=0, keepdims=True)
            yield
            g2 = g2_ref[...]
            _, n2, r2 = _rms_fwd(mix_ref[...], g2)
            dmix, dgt2 = _rms_bwd(n2, r2, g2, dx1_ref[...])
            dmix_ref[...] = dmix.astype(BF16)
            dg2_ref[0:1, :] += jnp.sum(dgt2, axis=0, keepdims=True)
            yield

        def tile(first):
            pending = iter(()) if first else norms_bwd()
            if first:
                parts = []
                for c in range(nblk):
                    wd_loads[c].wait()
                    du2 = _dot_nt(df_ref[...], wd_sc[blocks[c], :])
                    parts.append((2.0 * u_ref[:, blocks[c]].astype(F32) * du2).astype(BF16))
                dpre = jnp.concatenate(parts, axis=1)
            else:
                du2 = _dot_nt(df_ref[...], wd_sc[...])
                dpre = (2.0 * u_ref[...].astype(F32) * du2).astype(BF16)
            dpre_ref[...] = dpre
            dh = None
            for c in range(nblk):
                if first:
                    wu_loads[c].wait()
                if c in (1, 3):
                    next(pending, None)
                part = _dot_nt(dpre[:, blocks[c]], wu_sc[c])
                dh = part if dh is None else dh + part
            dh_sc[...] = dh

        @pl.when(s == 0)
        def _():
            for cp in wd_loads + wu_loads:
                cp.start()
            dg3_ref[...] = jnp.zeros_like(dg3_ref)
            dg2_ref[...] = jnp.zeros_like(dg2_ref)
            tile(True)

        pl.when((s > 0) & (s < nt))(functools.partial(tile, False))

        @pl.when(s == nt)
        def _():
            for _ in norms_bwd():
                pass

    cur = lambda s: (jnp.minimum(s, nt - 1), 0)
    prev = lambda s: (jnp.maximum(s - 1, 0), 0)
    vec = pl.BlockSpec((1, d), lambda s: (0, 0))
    acc8 = pl.BlockSpec((8, d), lambda s: (0, 0))
    anyspec = pl.BlockSpec(memory_space=pl.ANY)
    return pl.pallas_call(
        body,
        grid=(nt + 1,),
        in_specs=[pl.BlockSpec((tm, d), cur),
                  pl.BlockSpec((tm, ff), cur), anyspec, anyspec,
                  pl.BlockSpec((tm, d), prev), pl.BlockSpec((tm, d), prev), pl.BlockSpec((tm, d), prev), vec, vec],
        out_specs=[pl.BlockSpec((tm, ff), cur), pl.BlockSpec((tm, d), prev), pl.BlockSpec((tm, d), prev),
                   acc8, acc8],
        out_shape=[SDS(u.shape, BF16), SDS((t, d), F32), SDS((t, d), BF16), SDS((8, d), F32), SDS((8, d), F32)],
        scratch_shapes=[pltpu.VMEM((tm, d), F32), pltpu.VMEM((ff, d), BF16), pltpu.VMEM((nblk, d, tf), BF16),
                        pltpu.SemaphoreType.DMA((2 * nblk,))],
        compiler_params=_cp("arbitrary"),
        name="ffn_bwd",
    )(df, u, wdn, wup_g, x1, dy, mix, g3, g2)


def _wgrad(a, b, a_spec, b_spec, out_block, out_shape, nj, nk, name, prep_a=None, prep_b=None):
    acc_shape = out_block[1:]

    def body(a_ref, b_ref, o_ref, acc_sc):
        k = pl.program_id(1)
        av = a_ref[...] if prep_a is None else prep_a(a_ref)
        bv = b_ref[...] if prep_b is None else prep_b(b_ref)
        part = _dot_tn(av, bv)

        @pl.when(k == 0)
        def _():
            acc_sc[...] = part

        @pl.when(k > 0)
        def _():
            acc_sc[...] += part

        @pl.when(k == nk - 1)
        def _():
            o_ref[0] = acc_sc[...].astype(BF16)

    return pl.pallas_call(
        body,
        grid=(nj, nk),
        in_specs=[a_spec, b_spec],
        out_specs=pl.BlockSpec(out_block, lambda j, k: (j, 0, 0)),
        out_shape=SDS(out_shape, BF16),
        scratch_shapes=[pltpu.VMEM(acc_shape, F32)],
        compiler_params=_cp("parallel", "arbitrary"),
        name=name,
    )(a, b)


def _wgrad_cols(a, b, nj, tt, name):
    t, m = a.shape
    bn = b.shape[1] // nj
    return _wgrad(a, b, pl.BlockSpec((tt, m), lambda j, k: (k, 0)), pl.BlockSpec((tt, bn), lambda j, k: (k, j)),
                  (1, m, bn), (nj, m, bn), nj, t // tt, name)


def _wgrad_rows(a, b, nj, tt, name, square=False):
    t, n = b.shape
    bm = a.shape[1] // nj

    def squared(a_ref):
        af = a_ref[...].astype(F32)
        return (af * af).astype(BF16)

    return _wgrad(a, b, pl.BlockSpec((tt, bm), lambda j, k: (k, j)), pl.BlockSpec((tt, n), lambda j, k: (k, 0)),
                  (1, bm, n), (nj, bm, n), nj, t // tt, name, prep_a=squared if square else None)


def _wgrad_o(oa, ob, dmix, nj, tt):
    t, n = dmix.shape
    ca, cb = oa.shape[1], ob.shape[1]
    m = ca + cb
    nk = t // tt

    def body(oa_ref, ob_ref, b_ref, o_ref, acc_sc):
        k = pl.program_id(0)
        part = _dot_tn(jnp.concatenate([oa_ref[...], ob_ref[...]], axis=1), b_ref[...])

        @pl.when(k == 0)
        def _():
            acc_sc[...] = part

        @pl.when(k > 0)
        def _():
            acc_sc[...] += part

        @pl.when(k == nk - 1)
        def _():
            o_ref[...] = acc_sc[...].reshape(nj, m // nj, n).astype(BF16)

    return pl.pallas_call(
        body,
        grid=(nk,),
        in_specs=[pl.BlockSpec((tt, ca), lambda k: (k, 0)), pl.BlockSpec((tt, cb), lambda k: (k, 0)),
                  pl.BlockSpec((tt, n), lambda k: (k, 0))],
        out_specs=pl.BlockSpec((nj, m // nj, n), lambda k: (0, 0, 0)),
        out_shape=SDS((nj, m // nj, n), BF16),
        scratch_shapes=[pltpu.VMEM((m, n), F32)],
        compiler_params=_cp("arbitrary"),
        name="wgrad_o",
    )(oa, ob, dmix)


def _attn_out_bwd(dmix, wo, ca, tm):
    t, d = dmix.shape
    cb = wo.shape[0] - ca

    def body(dm_ref, w_ref, da_ref, db_ref):
        dm = dm_ref[...]
        da_ref[...] = _dot_nt(dm, w_ref[0:ca, :]).astype(BF16)
        db_ref[...] = _dot_nt(dm, w_ref[ca:, :]).astype(BF16)

    return pl.pallas_call(
        body,
        grid=(t // tm,),
        in_specs=[pl.BlockSpec((tm, d), lambda i: (i, 0)), pl.BlockSpec(wo.shape, lambda i: (0, 0))],
        out_specs=[pl.BlockSpec((tm, ca), lambda i: (i, 0)), pl.BlockSpec((tm, cb), lambda i: (i, 0))],
        out_shape=[SDS((t, ca), BF16), SDS((t, cb), BF16)],
        compiler_params=_cp("parallel"),
        name="attn_out_bwd",
    )(dmix, wo)


def _heads_t(x):
    xt = x.astype(F32).T
    return jnp.concatenate([xt[h * HEAD_DIM:(h + 1) * HEAD_DIM, :] for h in range(GROUP)], axis=1)


def _heads_t_inv(yt):
    n = yt.shape[1] // GROUP
    return jnp.concatenate([yt[:, h * n:(h + 1) * n] for h in range(GROUP)], axis=0).T


def _attn_a_bwd(qa, ka, kat, va, do, o, lse, tq, tk, grads):
    bl, kv, s_len, _ = ka.shape
    nq, nk = s_len // tq, s_len // tk
    assert nk % 2 == 0
    r = GROUP * tq
    ng = len(grads)

    def body(q_ref, qn_ref, k_ref, kt_ref, v_ref, do_ref, don_ref, o_ref, l_ref, *rest):
        grad_refs, (dq_ref, dk_ref, dv_ref), parts = rest[:ng], rest[ng:ng + 3], rest[ng + 3:2 * ng + 3]
        st_sc, dp_sc, dkt_sc, dvt_sc, send_sems, recv_sems, local_sems = rest[2 * ng + 3:]
        i = pl.program_id(2)
        step_id = (pl.program_id(0) * kv + pl.program_id(1)) * nq + i
        start, wait = _direct_exchange("scatter", grad_refs, parts, send_sems, recv_sems, local_sems)
        pl.when(step_id == 0)(start)

        dot32 = _heads_t(do_ref[...])
        drow = jnp.sum(dot32 * _heads_t(o_ref[...]), axis=0, keepdims=True)
        qt, dot = _heads_t(q_ref[...]).astype(BF16), dot32.astype(BF16)
        lrow = l_ref[0, 0, 0, 0:1, :]

        @pl.when(i == 0)
        def _():
            dkt_sc[...] = jnp.zeros_like(dkt_sc)
            dvt_sc[...] = jnp.zeros_like(dvt_sc)

        def chunk(c):
            return pl.ds(pl.multiple_of(c * tk, tk), tk)

        def scores(c, slot, qv=qt, dov=dot):
            st_sc[slot] = _dot(k_ref[0, 0, chunk(c), :], qv)
            dp_sc[slot] = _dot(v_ref[0, 0, chunk(c), :], dov)

        def fold(slot, c, dqt):
            pt = jnp.exp(st_sc[slot] - lrow)
            dsb = (pt * (dp_sc[slot] - drow)).astype(BF16)
            dvt_sc[:, chunk(c)] += _dot_nt(dot, pt.astype(BF16))
            dkt_sc[:, chunk(c)] += _dot_nt(qt, dsb)
            return dqt + _dot(kt_ref[0, 0, :, chunk(c)], dsb)

        @pl.when(i == 0)
        def _():
            scores(0, 0)

        def step(c2, dqt):
            c = 2 * c2
            scores(c + 1, 1)
            dqt = fold(0, c, dqt)
            scores(c + 2, 0)
            return fold(1, c + 1, dqt)

        dqt = jnp.zeros((HEAD_DIM, r), F32)
        for c2 in range(nk // 2 - 1):
            dqt = step(c2, dqt)
        scores(nk - 1, 1)
        dqt = fold(0, nk - 2, dqt)
        scores(0, 0, _heads_t(qn_ref[...]).astype(BF16), _heads_t(don_ref[...]).astype(BF16))
        dq_ref[...] = _heads_t_inv(fold(1, nk - 1, dqt))

        @pl.when(i == nq - 1)
        def _():
            dk_ref[0, 0] = dkt_sc[...].T
            dv_ref[0, 0] = dvt_sc[...].T

        pl.when(step_id == bl * kv * nq - 1)(wait)

    kvspec = pl.BlockSpec((1, 1, s_len, HEAD_DIM), lambda b, g, i: (b, g, 0, 0))
    tok = pl.BlockSpec((tq, GROUP * HEAD_DIM), lambda b, g, i: (b * nq + i, g))
    toknext = pl.BlockSpec((tq, GROUP * HEAD_DIM), lambda b, g, i: (b * nq + jnp.minimum(i + 1, nq - 1), g))
    anyspec = pl.BlockSpec(memory_space=pl.ANY)
    res = pl.pallas_call(
        body,
        grid=(bl, kv, nq),
        in_specs=[tok, toknext, kvspec, pl.BlockSpec((1, 1, HEAD_DIM, s_len), lambda b, g, i: (b, g, 0, 0)), kvspec,
                  tok, toknext, tok, pl.BlockSpec((1, 1, 1, 8, r), lambda b, g, i: (b, g, i, 0, 0))] + [anyspec] * ng,
        out_specs=[tok, kvspec, kvspec] + [anyspec] * ng,
        out_shape=[SDS(qa.shape, F32), SDS(ka.shape, F32), SDS(va.shape, F32)]
        + [SDS(g.shape, g.dtype) for g in grads],
        scratch_shapes=[pltpu.VMEM((2, tk, r), F32), pltpu.VMEM((2, tk, r), F32),
                        pltpu.VMEM((HEAD_DIM, s_len), F32), pltpu.VMEM((HEAD_DIM, s_len), F32)]
        + _exchange_scratch(ng),
        compiler_params=_cp("arbitrary", "arbitrary", "arbitrary"),
        name="attn_a_bwd",
    )(qa, qa, ka, kat, va, do, do, o, lse, *grads)
    return res[0], res[1], res[2], res[3:]


def _attn_b_bwd(qb, kb, kbt, vb, do, o, lse, bias_t, sink, s_len):
    bl, kv, sp, _ = kb.shape
    nb = s_len // BLOCK
    nbs = min(QB_PER_STEP, nb)
    r = GROUP * BLOCK

    def body(q_ref, k_ref, kt_ref, v_ref, do_ref, o_ref, l_ref, bt_ref, sink_ref,
             dq_ref, dk_ref, dv_ref, dsum_ref, dsink_ref, dkt_sc, dvt_sc):
        g, b, ns = pl.program_id(0), pl.program_id(1), pl.program_id(2)
        sink_row = _sink_row(sink_ref, g)

        @pl.when(ns == 0)
        def _():
            dkt_sc[...] = jnp.zeros_like(dkt_sc)
            dvt_sc[...] = jnp.zeros_like(dvt_sc)

        @pl.when((b == 0) & (ns == 0))
        def _():
            dsum_ref[...] = jnp.zeros_like(dsum_ref)
            dsink_ref[...] = jnp.zeros_like(dsink_ref)

        dsum = jnp.zeros((SPAN, r), F32)
        dsink = jnp.zeros((1, r), F32)
        for j in range(nbs):
            n = ns * nbs + j
            span = pl.ds(pl.multiple_of(n * BLOCK, BLOCK), SPAN)
            rows = slice(j * BLOCK, (j + 1) * BLOCK)
            dot32 = _heads_t(do_ref[rows, :])
            drow = jnp.sum(dot32 * _heads_t(o_ref[rows, :]), axis=0, keepdims=True)
            qt, dot = _heads_t(q_ref[rows, :]).astype(BF16), dot32.astype(BF16)
            lrow = l_ref[0, 0, j, 0:1, :]
            st = _dot(k_ref[0, 0, span, :], qt) + bt_ref[_bias_variant(n, nb), 0]
            pt = jnp.exp(st - lrow)
            dst = pt * (_dot(v_ref[0, 0, span, :], dot) - drow)
            dsum = dsum + dst
            dsink = dsink - jnp.exp(sink_row - lrow) * drow
            dsb = dst.astype(BF16)
            dvt_sc[:, span] += _dot_nt(dot, pt.astype(BF16))
            dkt_sc[:, span] += _dot_nt(qt, dsb)
            dq_ref[rows, :] = _heads_t_inv(_dot(kt_ref[0, 0, :, span], dsb))
        dsum_ref[0] += dsum
        dsink_ref[0, 0:1, :] += dsink

        @pl.when(ns == nb // nbs - 1)
        def _():
            dk_ref[0, 0] = dkt_sc[:, BLOCK:BLOCK + s_len].T
            dv_ref[0, 0] = dvt_sc[:, BLOCK:BLOCK + s_len].T

    kvspec = pl.BlockSpec((1, 1, sp, HEAD_DIM), lambda g, b, n: (b, g, 0, 0))
    kvout = pl.BlockSpec((1, 1, s_len, HEAD_DIM), lambda g, b, n: (b, g, 0, 0))
    tok = pl.BlockSpec((nbs * BLOCK, GROUP * HEAD_DIM), lambda g, b, n: (b * (nb // nbs) + n, g))
    return pl.pallas_call(
        body,
        grid=(kv, bl, nb // nbs),
        in_specs=[tok, kvspec, pl.BlockSpec((1, 1, HEAD_DIM, sp), lambda g, b, n: (b, g, 0, 0)), kvspec, tok, tok,
                  pl.BlockSpec((1, 1, nbs, 8, r), lambda g, b, n: (b, g, n, 0, 0)),
                  pl.BlockSpec((3, 1, SPAN, r), lambda g, b, n: (0, g, 0, 0)),
                  pl.BlockSpec(memory_space=pltpu.SMEM)],
        out_specs=[tok, kvout, kvout,
                   pl.BlockSpec((1, SPAN, r), lambda g, b, n: (g, 0, 0)),
                   pl.BlockSpec((1, 8, r), lambda g, b, n: (g, 0, 0))],
        out_shape=[SDS(qb.shape, F32), SDS((bl, kv, s_len, HEAD_DIM), F32), SDS((bl, kv, s_len, HEAD_DIM), F32),
                   SDS((kv, SPAN, r), F32), SDS((kv, 8, r), F32)],
        scratch_shapes=[pltpu.VMEM((HEAD_DIM, sp), F32), pltpu.VMEM((HEAD_DIM, sp), F32)],
        compiler_params=_cp("arbitrary", "arbitrary", "arbitrary"),
        name="attn_b_bwd",
    )(qb, kb, kbt, vb, do, o, lse, bias_t, sink)


def _bias_reduce(dsum, dsink, bucket_t4):
    kv, _, r = dsum.shape

    def body(ds_ref, dk_ref, bk_ref, rel_ref, sink_ref):
        lane = lax.broadcasted_iota(jnp.int32, (N_BUCKETS, 128), 1)
        lane8 = lax.broadcasted_iota(jnp.int32, (8, 128), 1)
        bk = bk_ref[...]
        for g in range(kv):
            ds = ds_ref[g]
            rowi = lax.broadcasted_iota(jnp.int32, (N_BUCKETS, r), 0)
            red = jnp.zeros((N_BUCKETS, r), F32)
            for b in range(N_BUCKETS):
                red = jnp.where(rowi == b, jnp.sum(jnp.where(bk == b, ds, 0.0), axis=0, keepdims=True), red)
            out = jnp.zeros((N_BUCKETS, 128), F32)
            so = jnp.zeros((8, 128), F32)
            for h in range(GROUP):
                col = jnp.sum(red[:, h * BLOCK:(h + 1) * BLOCK], axis=1, keepdims=True)
                out = jnp.where(lane == h, col, out)
                sc = jnp.sum(dk_ref[g][:, h * BLOCK:(h + 1) * BLOCK], axis=1, keepdims=True)
                so = jnp.where(lane8 == h, sc, so)
            rel_ref[g] = out
            sink_ref[g] = so

    vm = pl.BlockSpec(memory_space=pltpu.VMEM)
    return pl.pallas_call(
        body,
        in_specs=[vm, vm, vm],
        out_specs=[vm, vm],
        out_shape=[SDS((kv, N_BUCKETS, 128), F32), SDS((kv, 8, 128), F32)],
        name="bias_reduce",
    )(dsum, dsink, bucket_t4)


def _dqkprep(dqa, dka, dva, dqb, dkb, dvb, proj, h1, cos, sin_signed, gq, gk, s_len, ts):
    t, p_cols = proj.shape
    d = h1.shape[1]
    bl, kva, kvb = dka.shape[0], dka.shape[1], dkb.shape[1]
    ha, hb = dqa.shape[1] // HEAD_DIM, dqb.shape[1] // HEAD_DIM
    ns = s_len // ts

    def body(dqa_ref, dka_ref, dva_ref, dqb_ref, dkb_ref, dvb_ref, p_ref, h1_ref, h1p_ref, cos_ref, sin_ref,
             gq_ref, gk_ref, dp_ref, dgq_ref, dgk_ref, gw_ref, gw_sc, dpp_sc):
        b, i = pl.program_id(0), pl.program_id(1)
        cs, sn = cos_ref[...], sin_ref[...]
        low, first = _pair_masks(ts)

        @pl.when((b == 0) & (i == 0))
        def _():
            dgq_ref[...] = jnp.zeros_like(dgq_ref)
            dgk_ref[...] = jnp.zeros_like(dgk_ref)
            gw_sc[...] = jnp.zeros_like(gw_sc)
            dpp_sc[...] = jnp.zeros_like(dpp_sc)

        n_parts = 6
        pw = p_cols // n_parts

        def wgrad_part(c):
            rows = slice(c * pw, (c + 1) * pw)
            gw_sc[rows, :] += _dot_tn(dpp_sc[:, rows], h1p_ref[...])

        def grad_pair(ref, p):
            return jnp.concatenate([ref[0, 2 * p], ref[0, 2 * p + 1]], axis=1)

        def put(p, val):
            dp_ref[:, p * PAIR:(p + 1) * PAIR] = val.astype(BF16)

        def unrope_norm(d_rot, p, g, dg_ref):
            dn = d_rot * cs + _pair_partner(d_rot * sn, first)
            xp = p_ref[:, p * PAIR:(p + 1) * PAIR]
            r = lax.rsqrt(_pair_mean(xp * xp, low) + EPS)
            n = xp * r
            gd = g * dn
            dg_ref[0:1, :] += jnp.sum(dn * n, axis=0, keepdims=True)
            put(p, r * (gd - n * _pair_mean(n * gd, low)))

        parts = iter(range(n_parts))

        def next_wgrad_part():
            c = next(parts, None)
            if c is not None:
                wgrad_part(c)

        for p in range(ha // 2):
            next_wgrad_part()
            unrope_norm(dqa_ref[:, p * PAIR:(p + 1) * PAIR] * SCALE, p, gq_ref[...], dgq_ref)
        base = ha // 2
        for p in range(kva // 2):
            next_wgrad_part()
            unrope_norm(grad_pair(dka_ref, p), base + p, gk_ref[...], dgk_ref)
            put(base + kva // 2 + p, grad_pair(dva_ref, p))
        base += kva
        for p in range(hb // 2):
            put(base + p, dqb_ref[:, p * PAIR:(p + 1) * PAIR] * SCALE)
        base += hb // 2
        for p in range(kvb // 2):
            put(base + p, grad_pair(dkb_ref, p))
            put(base + kvb // 2 + p, grad_pair(dvb_ref, p))
        for c in parts:
            wgrad_part(c)

        dpp_sc[...] = dp_ref[...]

        @pl.when((b == bl - 1) & (i == ns - 1))
        def _():
            gw_ref[...] = (gw_sc[...] + _dot_tn(dp_ref[...], h1_ref[...])).astype(BF16)

    def hm(nh):
        return pl.BlockSpec((1, nh, ts, HEAD_DIM), lambda b, i: (b, 0, i, 0))

    def tokmajor(nh):
        return pl.BlockSpec((ts, nh * HEAD_DIM), lambda b, i: (b * ns + i, 0))

    vec = pl.BlockSpec((1, PAIR), lambda b, i: (0, 0))
    tab = pl.BlockSpec((ts, PAIR), lambda b, i: (i, 0))
    acc = pl.BlockSpec((8, PAIR), lambda b, i: (0, 0))
    pspec = pl.BlockSpec((ts, p_cols), lambda b, i: (b * ns + i, 0))
    return pl.pallas_call(
        body,
        grid=(bl, ns),
        in_specs=[tokmajor(ha), hm(kva), hm(kva), tokmajor(hb), hm(kvb), hm(kvb), pspec,
                  pl.BlockSpec((ts, d), lambda b, i: (b * ns + i, 0)),
                  pl.BlockSpec((ts, d), lambda b, i: (jnp.maximum(b * ns + i - 1, 0), 0)), tab, tab, vec, vec],
        out_specs=[pspec, acc, acc, pl.BlockSpec((p_cols, d), lambda b, i: (0, 0))],
        out_shape=[SDS((t, p_cols), BF16), SDS((8, PAIR), F32), SDS((8, PAIR), F32), SDS((p_cols, d), BF16)],
        scratch_shapes=[pltpu.VMEM((p_cols, d), F32), pltpu.VMEM((ts, p_cols), BF16)],
        compiler_params=_cp("arbitrary", "arbitrary"),
        name="dqkprep",
    )(dqa, dka, dva, dqb, dkb, dvb, proj, h1, h1, cos, sin_signed, gq, gk)


def _dx_final(dproj, w_t, x2, dx1, g1, tm, grads, small):
    t, d = x2.shape
    p_cols = w_t.shape[0]
    ng, nsm = len(grads), len(small)
    nsteps = t // tm

    def body(dp_ref, w_ref, x_ref, dx1_ref, g_ref, *rest):
        small_refs, grad_refs = rest[:nsm], rest[nsm:nsm + ng]
        dx_ref, vec_ref, rel_ref = rest[nsm + ng:nsm + ng + 3]
        parts = rest[nsm + ng + 3:nsm + 2 * ng + 3]
        sems, dg_sc, small_scratch = rest[nsm + 2 * ng + 3:nsm + 2 * ng + 6], rest[nsm + 2 * ng + 6], rest[nsm + 2 * ng + 7:]
        start, wait = _direct_exchange("scatter", grad_refs, parts, *sems)
        start_small, finish_small = _small_allreduce(dg_sc, *small_refs, vec_ref, rel_ref, *small_scratch)

        @pl.when(pl.program_id(0) == 0)
        def _():
            start()
            dg_sc[...] = jnp.zeros_like(dg_sc)

        dh = _dot(dp_ref[...], w_ref[...])
        g = g_ref[...]
        _, n, r = _rms_fwd(x_ref[...], g)
        dx, dgt = _rms_bwd(n, r, g, dh)
        dx_ref[...] = dx1_ref[...] + dx
        dg_sc[0:1, :] += jnp.sum(dgt, axis=0, keepdims=True)

        @pl.when(pl.program_id(0) == nsteps - 1)
        def _():
            start_small()
            wait()
            finish_small()

    tile = pl.BlockSpec((tm, d), lambda i: (i, 0))
    anyspec = pl.BlockSpec(memory_space=pl.ANY)

    def whole(a):
        return pl.BlockSpec(a.shape, lambda i: (0,) * a.ndim)

    vec_shape, rel_shape = SDS((8, d), F32), SDS((N_BUCKETS, 128), F32)
    res = pl.pallas_call(
        body,
        grid=(nsteps,),
        in_specs=[pl.BlockSpec((tm, p_cols), lambda i: (i, 0)),
                  pl.BlockSpec((p_cols, d), lambda i: (0, 0)),
                  tile, tile, pl.BlockSpec((1, d), lambda i: (0, 0))] + [whole(a) for a in small] + [anyspec] * ng,
        out_specs=[tile, whole(vec_shape), whole(rel_shape)] + [anyspec] * ng,
        out_shape=[SDS((t, d), F32), vec_shape, rel_shape] + [SDS(g.shape, g.dtype) for g in grads],
        scratch_shapes=_exchange_scratch(ng) + [pltpu.VMEM((8, d), F32)] + _small_allreduce_scratch(d),
        compiler_params=_cp("arbitrary"),
        name="dx_final",
    )(dproj, w_t, x2, dx1, g1, *small, *grads)
    return res[0], res[1], res[2], res[3:]


def _adamw_math(w, g, m, v):
    m = ADAM_B1 * m + (1.0 - ADAM_B1) * g
    v = ADAM_B2 * v + (1.0 - ADAM_B2) * (g * g)
    m_hat = m / (1.0 - ADAM_B1 ** ADAM_STEP)
    v_hat = v / (1.0 - ADAM_B2 ** ADAM_STEP)
    delta = -ADAM_LR * (m_hat / (jnp.sqrt(v_hat) + ADAM_EPS) + ADAM_WD * w)
    return delta, m, v


def _adamw_sum(parts, ws, ms, vs, steps):
    nw = len(ws)

    def body(*refs):
        ins, outs = refs[:4 * nw], refs[4 * nw:]
        for k in range(nw):
            p_ref, w_ref, m_ref, v_ref = ins[4 * k:4 * k + 4]
            g_ref, d_ref, nm_ref, nv_ref = outs[4 * k:4 * k + 4]
            g = p_ref[0].astype(F32)
            for s in range(1, N_DEV):
                g = g + p_ref[s].astype(F32)
            g_ref[...] = g
            d_ref[...], nm_ref[...], nv_ref[...] = _adamw_math(w_ref[...], g, m_ref[...], v_ref[...])

    in_specs, out_specs, out_shape, args = [], [], [], []
    for p, w, m, v in zip(parts, ws, ms, vs):
        rows, cols = w.shape
        tile = pl.BlockSpec((rows // steps, cols), lambda i: (i, 0))
        in_specs += [pl.BlockSpec((N_DEV, rows // steps, cols), lambda i: (0, i, 0)), tile, tile, tile]
        out_specs += [tile] * 4
        out_shape += [SDS((rows, cols), F32)] * 4
        args += [p, w, m, v]
    res = pl.pallas_call(
        body,
        grid=(steps,),
        in_specs=in_specs,
        out_specs=out_specs,
        out_shape=out_shape,
        compiler_params=_cp("parallel"),
        name="adamw_weights",
    )(*args)
    return [res[4 * k:4 * k + 4] for k in range(nw)]


def _adamw_small(vec, rel, ws, ms, vs):
    hb = ws[6].shape[1]
    n = len(ws)

    def body(vec_ref, rel_ref, *rest):
        w_refs, m_refs, v_refs = rest[:n], rest[n:2 * n], rest[2 * n:3 * n]
        loss_ref, outs = rest[3 * n], rest[3 * n + 1:]
        grads = [vec_ref[0:1, :], vec_ref[1:2, :], vec_ref[2:3, :], vec_ref[3:4, :],
                 vec_ref[4:5, 0:HEAD_DIM], vec_ref[4:5, SMALL_LANES:SMALL_LANES + HEAD_DIM],
                 vec_ref[4:5, 2 * SMALL_LANES:2 * SMALL_LANES + hb], rel_ref[...].T[0:hb, :]]
        loss_ref[...] = vec_ref[4:5, 3 * SMALL_LANES:3 * SMALL_LANES + 1]
        for p, g in enumerate(grads):
            g_ref, d_ref, nm_ref, nv_ref = outs[4 * p:4 * p + 4]
            g_ref[...] = g
            d_ref[...], nm_ref[...], nv_ref[...] = _adamw_math(w_refs[p][...], g, m_refs[p][...], v_refs[p][...])

    vm = pl.BlockSpec(memory_space=pltpu.VMEM)
    res = pl.pallas_call(
        body,
        in_specs=[vm] * (2 + 3 * n),
        out_specs=[vm] * (1 + 4 * n),
        out_shape=[SDS((1, 1), F32)] + [SDS(w.shape, F32) for w in ws for _ in range(4)],
        name="adamw_small",
    )(vec, rel, *ws, *ms, *vs)
    return res[0], [res[1 + 4 * p:5 + 4 * p] for p in range(n)]


def _local_step(x, loss_target, win_s, wo_s, wup_s, wdn_s, g_pre_mix, g_post_mix, q_norm_a, k_norm_a, sink_b,
                rel_bias_t, g_pre_ffn, g_post_ffn):
    bl, s_len, d = x.shape
    t = bl * s_len
    nh = d // HEAD_DIM
    ha = nh // 2
    kva = ha // GROUP
    hb = nh - ha
    kvb = hb // GROUP
    tm = 512
    tp = min(1024, t)
    tw = min(4096, t)
    ts = min(512, s_len)
    tq, tk = 2 * BLOCK, min(512, s_len // 2)

    x2 = x.reshape(t, d)
    tg2 = loss_target.reshape(t, d)
    cos, sin_signed = _rope_tables(s_len)
    gq2, gk2 = jnp.tile(q_norm_a, (1, 2)), jnp.tile(k_norm_a, (1, 2))
    a = jnp.arange(BLOCK, dtype=jnp.int32)
    c = jnp.arange(SPAN, dtype=jnp.int32)
    bucket_t = _t5_bucket(c[:, None] - BLOCK - a[None, :])
    bucket_t4 = jnp.tile(bucket_t, (1, GROUP))
    (win_g,), bias_t = _weight_gather([win_s], bucket_t, rel_bias_t)
    w_in_t = win_g.reshape(-1, d)
    p_cols = w_in_t.shape[0]

    h1, proj, qa, ka, kat, va, vat, qb, kb, kbt, vb, vbt = _inproj_qkprep(
        x2, g_pre_mix, w_in_t, cos, sin_signed, gq2, gk2, bl, s_len, ha, kva, hb, kvb, ts)
    oa, lse_a, (wo_g, wup_g, wdn_g) = _attn_a_fwd(qa, ka, vat, tq, tk, [wo_s, wup_s, wdn_s])
    wo = wo_g.reshape(-1, d)
    wdn = wdn_g.reshape(-1, d)
    ob, lse_b = _attn_b_fwd(qb, kb, vbt, bias_t, sink_b, s_len)
    mix, x1, h2 = _mixout(oa, ob, wo, x2, g_post_mix, g_pre_ffn, tp)
    u, df, dy, dg4, loss8 = _ffn_fwd(h2, wup_g, wdn, x1, tg2, g_post_ffn, tm)

    dpre, dx1, dmix, dg3, dg2 = _ffn_bwd(df, u, wdn, wup_g, x1, dy, mix, g_pre_ffn, g_post_mix, FFN_BWD_TOKENS)
    gw_dn = _wgrad_rows(u, df, N_DEV, tw, "wgrad_down", square=True)
    gw_up = _wgrad_cols(h2, dpre, N_DEV, tw, "wgrad_up")
    gw_o = _wgrad_o(oa, ob, dmix, N_DEV, min(2048, t))
    doa, dob = _attn_out_bwd(dmix, wo, oa.shape[1], tp)
    dqa, dka, dva, (p_o, p_up, p_dn) = _attn_a_bwd(qa, ka, kat, va, doa, oa, lse_a, tq, tk, [gw_o, gw_up, gw_dn])
    dqb, dkb, dvb, dsum, dsink = _attn_b_bwd(qb, kb, kbt, vb, dob, ob, lse_b, bias_t, sink_b, s_len)
    drel_g, dsink_g = _bias_reduce(dsum, dsink, bucket_t4)
    dproj, dgq, dgk, gw_in_t = _dqkprep(dqa, dka, dva, dqb, dkb, dvb, proj, h1, cos, sin_signed, gq2, gk2, s_len, ts)
    gw_in_t = gw_in_t.reshape(N_DEV, -1, d)
    grad_x, vec, rel, (p_in,) = _dx_final(dproj, w_in_t, x2, dx1, g_pre_mix, tp, [gw_in_t],
                                          [dg2, dg3, dg4, dgq, dgk, dsink_g, drel_g, loss8])
    return grad_x.reshape(bl, s_len, d), p_in, p_o, p_up, p_dn, vec, rel


def kernel(x, w_in, w_o, g_pre_mix, g_post_mix, q_norm_a, k_norm_a, sink_b, rel_bias, g_pre_ffn, w_ffn_up, w_ffn_down, g_post_ffn, loss_target, m_w_in, m_w_o, m_g_pre_mix, m_g_post_mix, m_q_norm_a, m_k_norm_a, m_sink_b, m_rel_bias, m_g_pre_ffn, m_w_ffn_up, m_w_ffn_down, m_g_post_ffn, v_w_in, v_w_o, v_g_pre_mix, v_g_post_mix, v_q_norm_a, v_k_norm_a, v_sink_b, v_rel_bias, v_g_pre_ffn, v_w_ffn_up, v_w_ffn_down, v_g_post_ffn):
    w_in_t = w_in[0].T
    rel_bias_t = rel_bias.T

    grad_x, p_in, p_o, p_up, p_dn, vec, rel = _local_step(
        x, loss_target, w_in_t.astype(BF16), w_o[0].astype(BF16), w_ffn_up[0].astype(BF16), w_ffn_down[0].astype(BF16),
        g_pre_mix, g_post_mix, q_norm_a, k_norm_a, sink_b, rel_bias_t, g_pre_ffn, g_post_ffn)

    r_in, r_o, r_up, r_dn = _adamw_sum(
        [p_in, p_o, p_up, p_dn],
        [w_in_t, w_o[0], w_ffn_up[0], w_ffn_down[0]],
        [m_w_in[0].T, m_w_o[0], m_w_ffn_up[0], m_w_ffn_down[0]],
        [v_w_in[0].T, v_w_o[0], v_w_ffn_up[0], v_w_ffn_down[0]], 4)
    big = {"w_in": [a.T for a in r_in], "w_o": r_o, "w_up": r_up, "w_dn": r_dn}
    loss, small = _adamw_small(
        vec, rel,
        [g_pre_mix, g_post_mix, g_pre_ffn, g_post_ffn, q_norm_a, k_norm_a, sink_b, rel_bias_t],
        [m_g_pre_mix, m_g_post_mix, m_g_pre_ffn, m_g_post_ffn, m_q_norm_a, m_k_norm_a, m_sink_b, m_rel_bias.T],
        [v_g_pre_mix, v_g_post_mix, v_g_pre_ffn, v_g_post_ffn, v_q_norm_a, v_k_norm_a, v_sink_b, v_rel_bias.T])
    s_pre_mix, s_post_mix, s_pre_ffn, s_post_ffn, s_qn, s_kn, s_sink, s_rel_t = small
    s_rel = [a.T for a in s_rel_t]

    def outs(kind):
        return [big["w_in"][kind][None], big["w_o"][kind][None], s_pre_mix[kind], s_post_mix[kind], s_qn[kind],
                s_kn[kind], s_sink[kind], s_rel[kind], s_pre_ffn[kind], big["w_up"][kind][None],
                big["w_dn"][kind][None], s_post_ffn[kind]]

    return (loss.reshape(()), grad_x, *outs(0), *outs(1), *outs(2), *outs(3))
```

```python
import functools

import jax
import jax.numpy as jnp
import numpy as np
from jax import lax
from jax.experimental import pallas as pl
from jax.experimental.pallas import tpu as pltpu

F32 = jnp.float32
BF16 = jnp.bfloat16
SDS = jax.ShapeDtypeStruct

N_DEV = 8
HEAD_DIM = 64
GROUP = 4
BLOCK = 128
SPAN = 3 * BLOCK
GRID_W = 64
N_BUCKETS = 32
MAX_DISTANCE = 128
ROPE_THETA = 10000.0
EPS = 1e-6
NEG_INF = -1e30
SCALE = HEAD_DIM ** -0.5
VT_PAD = 16

ADAM_LR = 0.001
ADAM_B1 = 0.9
ADAM_B2 = 0.999
ADAM_EPS = 1e-08
ADAM_WD = 0.01
ADAM_STEP = 10

VMEM_LIMIT = 56 * 1024 * 1024
MESH = pl.DeviceIdType.MESH


def _cp(*sem):
    return pltpu.CompilerParams(dimension_semantics=sem, vmem_limit_bytes=VMEM_LIMIT)


def _dot(a, b):
    return jnp.dot(a, b, preferred_element_type=F32)


def _dot_nt(a, b):
    return lax.dot_general(a, b, (((1,), (1,)), ((), ())), preferred_element_type=F32)


def _dot_tn(a, b):
    return lax.dot_general(a, b, (((0,), (0,)), ((), ())), preferred_element_type=F32)


def _rms_fwd(x, g):
    r = lax.rsqrt(jnp.mean(x * x, axis=-1, keepdims=True) + EPS)
    n = x * r
    return n * g, n, r


def _rms_bwd(n, r, g, dy):
    gd = g * dy
    dx = r * (gd - n * jnp.mean(n * gd, axis=-1, keepdims=True))
    return dx, dy * n


def _rope_tables(s_len):
    rows = s_len // GRID_W
    row = np.repeat(np.arange(rows, dtype=np.int32), GRID_W)
    col = np.tile(np.arange(GRID_W, dtype=np.int32), rows)
    nf = HEAD_DIM // 4
    freqs = np.float32(ROPE_THETA) ** (-np.arange(nf, dtype=np.float32) / np.float32(nf))
    ang_r = row.astype(np.float32)[:, None] * freqs[None, :]
    ang_c = col.astype(np.float32)[:, None] * freqs[None, :]
    cr, sr, cc, sc = np.cos(ang_r), np.sin(ang_r), np.cos(ang_c), np.sin(ang_c)
    cos = np.concatenate([cr, cr, cc, cc] * 2, axis=-1).astype(np.float32)
    sin_signed = np.concatenate([-sr, sr, -sc, sc] * 2, axis=-1).astype(np.float32)
    return jnp.asarray(cos), jnp.asarray(sin_signed)


def _t5_bucket(rel):
    nb = N_BUCKETS // 2
    ret = (rel > 0).astype(jnp.int32) * nb
    n = jnp.abs(rel)
    max_exact = nb // 2
    nf = jnp.maximum(n, 1).astype(F32)
    large = max_exact + (jnp.log(nf / max_exact) / np.float32(np.log(MAX_DISTANCE / max_exact))
                         * (nb - max_exact)).astype(jnp.int32)
    large = jnp.minimum(large, nb - 1)
    return ret + jnp.where(n < max_exact, n, large)


def _mesh_pos():
    return lax.axis_index("x"), lax.axis_index("y"), lax.axis_index("c")


def _lin(p):
    return 4 * p[0] + 2 * p[1] + p[2]


def _bias_tables(bkt_ref, tbl_ref, out_ref, hb):
    bkt = bkt_ref[...]
    ci = lax.broadcasted_iota(jnp.int32, (SPAN, BLOCK), 0)
    qi = lax.broadcasted_iota(jnp.int32, (SPAN, BLOCK), 1)
    band = jnp.abs(ci - BLOCK - qi) <= BLOCK
    masks = (band, band & (ci >= BLOCK), band & (ci < 2 * BLOCK))
    for h in range(hb):
        acct = jnp.zeros((SPAN, BLOCK), F32)
        for b in range(N_BUCKETS):
            acct = jnp.where(bkt == b, tbl_ref[h, b], acct)
        lanes = slice((h % GROUP) * BLOCK, (h % GROUP + 1) * BLOCK)
        for var, mask in enumerate(masks):
            out_ref[var, h // GROUP, :, lanes] = jnp.where(mask, acct, NEG_INF)


def _weight_gather(shards, bucket_t, rel_bias_t):
    n = len(shards)
    hb = rel_bias_t.shape[0]

    def body(*refs):
        xs, (bkt_ref, tbl_ref), outs, bias_ref = refs[:n], refs[n:n + 2], refs[n + 2:2 * n + 2], refs[2 * n + 2]
        send_sems, recv_sems, local_sems = refs[2 * n + 3:]
        x, y, c = _mesh_pos()
        me, sibling = (x, y, c), (x, y, 1 - c)
        chips = [(1 - x, y), (x, 1 - y), (1 - x, 1 - y)]

        def copy(a, k, block, to, src=None):
            slot = outs[a].at[_lin(block)]
            return pltpu.make_async_remote_copy(
                src_ref=slot if src is None else src, dst_ref=slot,
                send_sem=send_sems.at[a, k], recv_sem=recv_sems.at[a, k],
                device_id=to, device_id_type=MESH)

        started = []
        for a in range(n):
            mine = pltpu.make_async_copy(xs[a], outs[a].at[_lin(me)], local_sems.at[a])
            mine.start()
            started.append(mine)
        sends = []
        for a in range(n):
            first = [copy(a, 0, me, sibling, src=xs[a])]
            first += [copy(a, 1 + j, me, (*chip, c), src=xs[a]) for j, chip in enumerate(chips)]
            for cp in first:
                cp.start()
            sends += first
        _bias_tables(bkt_ref, tbl_ref, bias_ref, hb)
        for a in range(n):
            for j, chip in enumerate(chips):
                copy(a, 1 + j, (*chip, c), me).wait_recv()
                fwd = copy(a, 4 + j, (*chip, c), sibling)
                fwd.start()
                sends.append(fwd)
        for a in range(n):
            copy(a, 0, sibling, me).wait_recv()
            for j, chip in enumerate(chips):
                copy(a, 4 + j, (*chip, 1 - c), me).wait_recv()
        for cp in sends:
            cp.wait_send()
        for mine in started:
            mine.wait()

    anyspec = pl.BlockSpec(memory_space=pl.ANY)
    vm = pl.BlockSpec(memory_space=pltpu.VMEM)
    res = pl.pallas_call(
        body,
        out_shape=[SDS((N_DEV,) + s.shape, s.dtype) for s in shards]
        + [SDS((3, hb // GROUP, SPAN, GROUP * BLOCK), F32)],
        in_specs=[anyspec] * n + [vm, pl.BlockSpec(memory_space=pltpu.SMEM)],
        out_specs=[anyspec] * n + [vm],
        scratch_shapes=[pltpu.SemaphoreType.DMA((n, 7)), pltpu.SemaphoreType.DMA((n, 7)),
                        pltpu.SemaphoreType.DMA((n,))],
        name="weight_gather",
    )(*shards, bucket_t, rel_bias_t)
    return res[:n], res[n]


def _direct_exchange(kind, ins, outs, send_sems, recv_sems, local_sems):
    x, y, c = _mesh_pos()
    me = (x, y, c)
    peers = [(x, y, 1 - c), (1 - x, y, c), (x, 1 - y, c), (1 - x, 1 - y, c),
             (1 - x, y, 1 - c), (x, 1 - y, 1 - c), (1 - x, 1 - y, 1 - c)]

    def src(a, to):
        return ins[a] if kind == "gather" else ins[a].at[_lin(to)]

    def remote(a, k, to, frm):
        return pltpu.make_async_remote_copy(
            src_ref=src(a, to), dst_ref=outs[a].at[_lin(frm)],
            send_sem=send_sems.at[a, k], recv_sem=recv_sems.at[a, k],
            device_id=to, device_id_type=MESH)

    n = len(ins)
    sends = [remote(a, k, p, me) for a in range(n) for k, p in enumerate(peers)]
    arrivals = [remote(a, k, p, p) for a in range(n) for k, p in enumerate(peers)]
    local = [pltpu.make_async_copy(src(a, me), outs[a].at[_lin(me)], local_sems.at[a]) for a in range(n)]

    def start():
        for cp in local + sends:
            cp.start()

    def wait():
        for cp in arrivals:
            cp.wait_recv()
        for cp in sends:
            cp.wait_send()
        for cp in local:
            cp.wait()

    return start, wait


def _exchange_scratch(n):
    return [pltpu.SemaphoreType.DMA((n, 7)), pltpu.SemaphoreType.DMA((n, 7)), pltpu.SemaphoreType.DMA((n,))]


SMALL_LANES = 128


def _small_allreduce(g1_ref, g2_ref, g3_ref, g4_ref, gq_ref, gk_ref, sk_ref, rl_ref, ls_ref, vec_ref, rel_ref,
                     vbuf, rbuf, vland, rland, send_sems, recv_sems):
    kv = sk_ref.shape[0]
    x, y, c = _mesh_pos()
    me = (x, y, c)
    peers = [(x, y, 1 - c), (1 - x, y, c), (x, 1 - y, c), (1 - x, 1 - y, c),
             (1 - x, y, 1 - c), (x, 1 - y, 1 - c), (1 - x, 1 - y, 1 - c)]

    def copies(k, to, frm):
        return [pltpu.make_async_remote_copy(
            src_ref=buf, dst_ref=land.at[_lin(frm)], send_sem=send_sems.at[a, k], recv_sem=recv_sems.at[a, k],
            device_id=to, device_id_type=MESH) for a, (buf, land) in enumerate(((vbuf, vland), (rbuf, rland)))]

    sends = [cp for k, p in enumerate(peers) for cp in copies(k, p, me)]

    def start():
        vbuf[...] = jnp.zeros_like(vbuf)
        rbuf[...] = jnp.zeros_like(rbuf)
        for row, ref in enumerate((g1_ref, g2_ref, g3_ref, g4_ref)):
            vbuf[row:row + 1, :] = ref[0:1, :]
        vbuf[4:5, 0:HEAD_DIM] = gq_ref[0:1, 0:HEAD_DIM] + gq_ref[0:1, HEAD_DIM:PAIR]
        vbuf[4:5, SMALL_LANES:SMALL_LANES + HEAD_DIM] = gk_ref[0:1, 0:HEAD_DIM] + gk_ref[0:1, HEAD_DIM:PAIR]
        for g in range(kv):
            vbuf[4:5, 2 * SMALL_LANES + g * GROUP:2 * SMALL_LANES + (g + 1) * GROUP] = sk_ref[g, 0:1, 0:GROUP]
            rbuf[:, g * GROUP:(g + 1) * GROUP] = rl_ref[g, :, 0:GROUP]
        vbuf[4:5, 3 * SMALL_LANES:3 * SMALL_LANES + 1] = ls_ref[0:1, 0:1]
        for cp in sends:
            cp.start()
        vland[_lin(me)] = vbuf[...]
        rland[_lin(me)] = rbuf[...]

    def finish():
        for k, p in enumerate(peers):
            for cp in copies(k, p, p):
                cp.wait_recv()
        for cp in sends:
            cp.wait_send()
        vacc, racc = vland[0], rland[0]
        for s in range(1, N_DEV):
            vacc, racc = vacc + vland[s], racc + rland[s]
        vec_ref[...] = vacc
        rel_ref[...] = racc

    return start, finish


def _small_allreduce_scratch(d):
    return [pltpu.VMEM((8, d), F32), pltpu.VMEM((N_BUCKETS, 128), F32),
            pltpu.VMEM((N_DEV, 8, d), F32), pltpu.VMEM((N_DEV, N_BUCKETS, 128), F32),
            pltpu.SemaphoreType.DMA((2, 7)), pltpu.SemaphoreType.DMA((2, 7))]


PAIR = 2 * HEAD_DIM


def _pair_masks(ts):
    lane = lax.broadcasted_iota(jnp.int32, (ts, PAIR), 1)
    return lane < HEAD_DIM, (lane % 32) < 16


def _pair_mean(v, low):
    del low
    r = lax.broadcasted_iota(jnp.int32, (PAIR, PAIR), 0) // HEAD_DIM
    c = lax.broadcasted_iota(jnp.int32, (PAIR, PAIR), 1) // HEAD_DIM
    same_head = (r == c).astype(BF16)
    hi = v.astype(BF16)
    lo = (v - hi.astype(F32)).astype(BF16)
    return (_dot(hi, same_head) + _dot(lo, same_head)) * (1.0 / HEAD_DIM)


def _pair_partner(v, first):
    return jnp.where(first, pltpu.roll(v, PAIR - 16, 1), pltpu.roll(v, 16, 1))


def _inproj_qkprep(x2, g1, w_t, cos, sin_signed, gq, gk, bl, s_len, ha, kva, hb, kvb, ts):
    t, d = x2.shape
    p_cols = w_t.shape[0]
    assert ha % 2 == 0 and kva % 2 == 0 and hb % 2 == 0 and kvb % 2 == 0
    ns = s_len // ts
    nt = bl * ns
    sp = s_len + 2 * BLOCK

    def body(*refs):
        kb_ref, kbt_ref, vb_ref, vbt_ref, p_even, p_odd = refs[-6:]
        s = pl.program_id(0)
        i = lax.rem(jnp.maximum(s - 1, 0), ns)

        @pl.when(s == 0)
        def _():
            p_odd[...] = jnp.zeros_like(p_odd)

        @pl.when(i == 0)
        def _():
            zeros = jnp.zeros((kvb, BLOCK, HEAD_DIM), BF16)
            zeros_t = jnp.zeros((kvb, HEAD_DIM + VT_PAD, BLOCK), BF16)
            for ref in (kb_ref, vb_ref):
                ref[0, :, 0:BLOCK, :] = zeros
                ref[0, :, sp - BLOCK:sp, :] = zeros
            kbt_ref[0, :, :, 0:BLOCK] = zeros_t[:, 0:HEAD_DIM]
            kbt_ref[0, :, :, sp - BLOCK:sp] = zeros_t[:, 0:HEAD_DIM]
            vbt_ref[0, :, :, 0:BLOCK] = zeros_t
            vbt_ref[0, :, :, sp - BLOCK:sp] = zeros_t

        even = lax.rem(s, 2) == 0
        pl.when(even)(functools.partial(tile_work, p_even, p_odd, i, *refs[:-2]))
        pl.when(jnp.logical_not(even))(functools.partial(tile_work, p_odd, p_even, i, *refs[:-2]))

    def tile_work(p_new, p_ref, i, x_ref, g1_ref, w_ref, cos_ref, sin_ref, gq_ref, gk_ref, h_ref, po_ref, qa_ref,
                  ka_ref, kat_ref, va_ref, vat_ref, qb_ref, kb_ref, kbt_ref, vb_ref, vbt_ref):
        y, _, _ = _rms_fwd(x_ref[...], g1_ref[...])
        h = y.astype(BF16)
        h_ref[...] = h
        n_parts = 6
        pw = p_cols // n_parts

        def project(c):
            p_new[:, c * pw:(c + 1) * pw] = _dot_nt(h, w_ref[c * pw:(c + 1) * pw, :])

        cs, sn = cos_ref[...], sin_ref[...]
        low, first = _pair_masks(ts)
        ones_row = (lax.broadcasted_iota(jnp.int32, (VT_PAD, ts), 0) == 0).astype(BF16)
        heads = (slice(0, HEAD_DIM), slice(HEAD_DIM, PAIR))

        def pair(p):
            v = p_ref[:, p * PAIR:(p + 1) * PAIR]
            po_ref[:, p * PAIR:(p + 1) * PAIR] = v
            return v

        def normrope(x, g):
            y = x * lax.rsqrt(_pair_mean(x * x, low) + EPS) * g
            return y * cs + _pair_partner(y, first) * sn

        eye = (lax.broadcasted_iota(jnp.int32, (PAIR, PAIR), 0)
               == lax.broadcasted_iota(jnp.int32, (PAIR, PAIR), 1)).astype(BF16)

        def transposed(xb):
            return _dot_nt(eye, xb).astype(BF16)

        def prep_qa(p):
            qa_ref[:, p * PAIR:(p + 1) * PAIR] = (normrope(pair(p), gq_ref[...]) * SCALE).astype(BF16)

        def prep_kva(p):
            base = ha // 2
            k = normrope(pair(base + p), gk_ref[...]).astype(BF16)
            v = pair(base + kva // 2 + p).astype(BF16)
            kt, vt = transposed(k), transposed(v)
            for e, lanes in enumerate(heads):
                ka_ref[0, 2 * p + e] = k[:, lanes]
                va_ref[0, 2 * p + e] = v[:, lanes]
                kat_ref[0, 2 * p + e] = kt[lanes, :]
                vat_ref[0, 2 * p + e, 0:HEAD_DIM, :] = vt[lanes, :]
                vat_ref[0, 2 * p + e, HEAD_DIM:HEAD_DIM + VT_PAD, :] = ones_row

        def prep_qb(p):
            base = ha // 2 + kva
            qb_ref[:, p * PAIR:(p + 1) * PAIR] = (pair(base + p) * SCALE).astype(BF16)

        rows = pl.ds(pl.multiple_of(BLOCK + i * ts, BLOCK), ts)

        def prep_kvb(p):
            base = ha // 2 + kva + hb // 2
            k = pair(base + p).astype(BF16)
            v = pair(base + kvb // 2 + p).astype(BF16)
            kt, vt = transposed(k), transposed(v)
            for e, lanes in enumerate(heads):
                kb_ref[0, 2 * p + e, rows, :] = k[:, lanes]
                vb_ref[0, 2 * p + e, rows, :] = v[:, lanes]
                kbt_ref[0, 2 * p + e, :, rows] = kt[lanes, :]
                vbt_ref[0, 2 * p + e, 0:HEAD_DIM, rows] = vt[lanes, :]
                vbt_ref[0, 2 * p + e, HEAD_DIM:HEAD_DIM + VT_PAD, rows] = ones_row

        work = ([functools.partial(prep_qa, p) for p in range(ha // 2)]
                + [functools.partial(prep_kva, p) for p in range(kva // 2)]
                + [functools.partial(prep_qb, p) for p in range(hb // 2)]
                + [functools.partial(prep_kvb, p) for p in range(kvb // 2)])
        per_part = -(-len(work) // n_parts)
        for c in range(n_parts):
            for item in work[c * per_part:(c + 1) * per_part]:
                item()
            project(c)

    def cur(s):
        return jnp.minimum(s, nt - 1)

    def prev(s):
        return jnp.maximum(s - 1, 0) // ns, lax.rem(jnp.maximum(s - 1, 0), ns)

    def hm(nh):
        return pl.BlockSpec((1, nh, ts, HEAD_DIM), lambda s: (prev(s)[0], 0, prev(s)[1], 0))

    def hm_t(nh, rows):
        return pl.BlockSpec((1, nh, rows, ts), lambda s: (prev(s)[0], 0, 0, prev(s)[1]))

    def tokmajor(nh):
        return pl.BlockSpec((ts, nh * HEAD_DIM), lambda s: (jnp.maximum(s - 1, 0), 0))

    def padded(nh):
        return pl.BlockSpec((1, nh, sp, HEAD_DIM), lambda s: (prev(s)[0], 0, 0, 0))

    def padded_t(nh, rows):
        return pl.BlockSpec((1, nh, rows, sp), lambda s: (prev(s)[0], 0, 0, 0))

    tab = pl.BlockSpec((ts, PAIR), lambda s: (prev(s)[1], 0))
    vec = pl.BlockSpec((1, PAIR), lambda s: (0, 0))
    return pl.pallas_call(
        body,
        grid=(nt + 1,),
        in_specs=[pl.BlockSpec((ts, d), lambda s: (cur(s), 0)),
                  pl.BlockSpec((1, d), lambda s: (0, 0)),
                  pl.BlockSpec((p_cols, d), lambda s: (0, 0)),
                  tab, tab, vec, vec],
        out_specs=[pl.BlockSpec((ts, d), lambda s: (cur(s), 0)), tokmajor(p_cols // HEAD_DIM),
                   tokmajor(ha), hm(kva), hm_t(kva, HEAD_DIM), hm(kva), hm_t(kva, HEAD_DIM + VT_PAD),
                   tokmajor(hb), padded(kvb), padded_t(kvb, HEAD_DIM), padded(kvb),
                   padded_t(kvb, HEAD_DIM + VT_PAD)],
        out_shape=[SDS((t, d), BF16), SDS((t, p_cols), F32),
                   SDS((t, ha * HEAD_DIM), BF16), SDS((bl, kva, s_len, HEAD_DIM), BF16),
                   SDS((bl, kva, HEAD_DIM, s_len), BF16),
                   SDS((bl, kva, s_len, HEAD_DIM), BF16), SDS((bl, kva, HEAD_DIM + VT_PAD, s_len), BF16),
                   SDS((t, hb * HEAD_DIM), BF16),
                   SDS((bl, kvb, sp, HEAD_DIM), BF16), SDS((bl, kvb, HEAD_DIM, sp), BF16),
                   SDS((bl, kvb, sp, HEAD_DIM), BF16), SDS((bl, kvb, HEAD_DIM + VT_PAD, sp), BF16)],
        scratch_shapes=[pltpu.VMEM((ts, p_cols), F32)] * 2,
        compiler_params=_cp("arbitrary"),
        name="inproj_qkprep",
    )(x2, g1, w_t, cos, sin_signed, gq, gk)


def _attn_a_fwd(qa, ka, vat, tq, tk, shards):
    bl, kv, s_len, _ = ka.shape
    ha = qa.shape[1] // HEAD_DIM
    va_rows = vat.shape[2]
    nq, nk = s_len // tq, s_len // tk
    assert nk % 2 == 0
    r = GROUP * tq
    ns = len(shards)

    def body(q_ref, qn_ref, k_ref, v_ref, *rest):
        shard_refs, (o_ref, l_ref), gathered = rest[:ns], rest[ns:ns + 2], rest[ns + 2:2 * ns + 2]
        st_sc, send_sems, recv_sems, local_sems = rest[2 * ns + 2:]
        i = pl.program_id(2)
        step_id = (pl.program_id(0) * kv + pl.program_id(1)) * nq + i
        start, wait = _direct_exchange("gather", shard_refs, gathered, send_sems, recv_sems, local_sems)
        pl.when(step_id == 0)(start)

        q = _heads_t(q_ref[...]).astype(BF16)

        def scores(c, qv):
            return _dot(k_ref[0, 0, pl.ds(pl.multiple_of(c * tk, tk), tk), :], qv)

        def fold(st, c, carry):
            m_old, acc = carry
            m_new = jnp.maximum(m_old, jnp.max(st, axis=0, keepdims=True))
            pt = jnp.exp(st - m_new).astype(BF16)
            vt = v_ref[0, 0, :, pl.ds(pl.multiple_of(c * tk, tk), tk)]
            return m_new, jnp.exp(m_old - m_new) * acc + _dot(vt, pt)

        @pl.when(i == 0)
        def _():
            st_sc[0] = scores(0, q)

        def step(c2, carry):
            c = 2 * c2
            st_sc[1] = scores(c + 1, q)
            carry = fold(st_sc[0], c, carry)
            st_sc[0] = scores(c + 2, q)
            return fold(st_sc[1], c + 1, carry)

        carry = (jnp.full((1, r), -jnp.inf, F32), jnp.zeros((va_rows, r), F32))
        for c2 in range(nk // 2 - 1):
            carry = step(c2, carry)
        st_sc[1] = scores(nk - 1, q)
        carry = fold(st_sc[0], nk - 2, carry)
        st_sc[0] = scores(0, _heads_t(qn_ref[...]).astype(BF16))
        m, acc = fold(st_sc[1], nk - 1, carry)
        l = acc[HEAD_DIM:HEAD_DIM + 1, :]
        o_ref[...] = _heads_t_inv(acc[0:HEAD_DIM, :] / l).astype(BF16)
        l_ref[0, 0, 0] = jnp.broadcast_to(m + jnp.log(l), (8, r))
        pl.when(step_id == bl * kv * nq - 1)(wait)

    anyspec = pl.BlockSpec(memory_space=pl.ANY)
    res = pl.pallas_call(
        body,
        grid=(bl, kv, nq),
        in_specs=[pl.BlockSpec((tq, GROUP * HEAD_DIM), lambda b, g, i: (b * nq + i, g)),
                  pl.BlockSpec((tq, GROUP * HEAD_DIM), lambda b, g, i: (b * nq + jnp.minimum(i + 1, nq - 1), g)),
                  pl.BlockSpec((1, 1, s_len, HEAD_DIM), lambda b, g, i: (b, g, 0, 0)),
                  pl.BlockSpec((1, 1, va_rows, s_len), lambda b, g, i: (b, g, 0, 0))] + [anyspec] * ns,
        out_specs=[pl.BlockSpec((tq, GROUP * HEAD_DIM), lambda b, g, i: (b * nq + i, g)),
                   pl.BlockSpec((1, 1, 1, 8, r), lambda b, g, i: (b, g, i, 0, 0))] + [anyspec] * ns,
        out_shape=[SDS((bl * s_len, ha * HEAD_DIM), BF16), SDS((bl, kv, nq, 8, r), F32)]
        + [SDS((N_DEV,) + s.shape, s.dtype) for s in shards],
        scratch_shapes=[pltpu.VMEM((2, tk, r), F32)] + _exchange_scratch(ns),
        compiler_params=_cp("arbitrary", "arbitrary", "arbitrary"),
        name="attn_a_fwd",
    )(qa, qa, ka, vat, *shards)
    return res[0], res[1], res[2:]


FFN_BWD_TOKENS = 256
QB_PER_STEP = 16


def _bias_variant(n, nb):
    return jnp.where(n == 0, 1, jnp.where(n == nb - 1, 2, 0))


def _sink_row(sink_ref, g):
    return jnp.concatenate([jnp.full((1, BLOCK), sink_ref[0, g * GROUP + h], F32) for h in range(GROUP)], axis=1)


def _attn_b_fwd(qb, kb, vbt, bias_t, sink, s_len):
    bl, kv, sp, _ = kb.shape
    hb = qb.shape[1] // HEAD_DIM
    vt_rows = vbt.shape[2]
    nb = s_len // BLOCK
    nbs = min(QB_PER_STEP, nb)
    r = GROUP * BLOCK

    def body(q_ref, k_ref, vt_ref, bt_ref, sink_ref, o_ref, l_ref, st_sc, pb_sc):
        g, n0 = pl.program_id(1), pl.program_id(2) * nbs
        sink_row = _sink_row(sink_ref, g)

        def span(j):
            return pl.ds(pl.multiple_of((n0 + j) * BLOCK, BLOCK), SPAN)

        for j in range(nbs):
            qt = _heads_t(q_ref[j * BLOCK:(j + 1) * BLOCK, :]).astype(BF16)
            st_sc[j] = _dot(k_ref[0, 0, span(j), :], qt) + bt_ref[_bias_variant(n0 + j, nb), 0]
        maxes = []
        for j in range(nbs):
            st = st_sc[j]
            m = jnp.maximum(jnp.max(st, axis=0, keepdims=True), sink_row)
            pb_sc[j] = jnp.exp(st - m).astype(BF16)
            maxes.append(m)
        for j in range(nbs):
            m = maxes[j]
            acc = _dot(vt_ref[0, 0, :, span(j)], pb_sc[j])
            l = acc[HEAD_DIM:HEAD_DIM + 1, :] + jnp.exp(sink_row - m)
            o_ref[j * BLOCK:(j + 1) * BLOCK, :] = _heads_t_inv(acc[0:HEAD_DIM, :] / l).astype(BF16)
            l_ref[0, 0, j] = jnp.broadcast_to(m + jnp.log(l), (8, r))

    return pl.pallas_call(
        body,
        grid=(bl, kv, nb // nbs),
        in_specs=[pl.BlockSpec((nbs * BLOCK, GROUP * HEAD_DIM), lambda b, g, n: (b * (nb // nbs) + n, g)),
                  pl.BlockSpec((1, 1, sp, HEAD_DIM), lambda b, g, n: (b, g, 0, 0)),
                  pl.BlockSpec((1, 1, vt_rows, sp), lambda b, g, n: (b, g, 0, 0)),
                  pl.BlockSpec((3, 1, SPAN, r), lambda b, g, n: (0, g, 0, 0)),
                  pl.BlockSpec(memory_space=pltpu.SMEM)],
        out_specs=[pl.BlockSpec((nbs * BLOCK, GROUP * HEAD_DIM), lambda b, g, n: (b * (nb // nbs) + n, g)),
                   pl.BlockSpec((1, 1, nbs, 8, r), lambda b, g, n: (b, g, n, 0, 0))],
        out_shape=[SDS((bl * s_len, hb * HEAD_DIM), BF16), SDS((bl, kv, nb, 8, r), F32)],
        scratch_shapes=[pltpu.VMEM((nbs, SPAN, r), F32), pltpu.VMEM((nbs, SPAN, r), BF16)],
        compiler_params=_cp("parallel", "parallel", "arbitrary"),
        name="attn_b_fwd",
    )(qb, kb, vbt, bias_t, sink)


def _mixout(oa, ob, wo, x2, g2, g3, tm):
    t, d = x2.shape
    ca = oa.shape[1]

    def body(oa_ref, ob_ref, w_ref, x_ref, g2_ref, g3_ref, mix_ref, x1_ref, h2_ref):
        mix = _dot(oa_ref[...], w_ref[0:ca, :]) + _dot(ob_ref[...], w_ref[ca:, :])
        mix_ref[...] = mix
        y2, _, _ = _rms_fwd(mix, g2_ref[...])
        x1 = x_ref[...] + y2
        x1_ref[...] = x1
        y3, _, _ = _rms_fwd(x1, g3_ref[...])
        h2_ref[...] = y3.astype(BF16)

    tile = lambda w: pl.BlockSpec((tm, w), lambda i: (i, 0))
    vec = pl.BlockSpec((1, d), lambda i: (0, 0))
    return pl.pallas_call(
        body,
        grid=(t // tm,),
        in_specs=[tile(ca), tile(ob.shape[1]), pl.BlockSpec(wo.shape, lambda i: (0, 0)), tile(d), vec, vec],
        out_specs=[tile(d), tile(d), tile(d)],
        out_shape=[SDS((t, d), F32), SDS((t, d), F32), SDS((t, d), BF16)],
        compiler_params=_cp("parallel"),
        name="mixout",
    )(oa, ob, wo, x2, g2, g3)


def _ffn_fwd(h2, wup_g, wdn, x1, target, g4, tm):
    t, d = x1.shape
    nblk, _, tf = wup_g.shape
    ff = nblk * tf
    nt = t // tm

    def body(h_ref, wu_hbm, wd_hbm, x1_ref, tg_ref, g_ref, u_ref, df_ref, dy_ref, dg_ref, loss_ref,
             f_sc, wu_sc, wd_sc, sems):
        s = pl.program_id(0)
        loads = [pltpu.make_async_copy(wu_hbm.at[c], wu_sc.at[c], sems.at[c]) for c in range(nblk)]
        loads.append(pltpu.make_async_copy(wd_hbm, wd_sc, sems.at[nblk]))

        def after_norm():
            g = g_ref[...]
            y4, n, r = _rms_fwd(f_sc[...], g)
            e = (x1_ref[...] + y4) - tg_ref[...]
            loss_ref[...] += jnp.sum(e * e) * (0.5 / d)
            dy = e * (1.0 / d)
            dy_ref[...] = dy
            yield
            df, dgt = _rms_bwd(n, r, g, dy)
            df_ref[...] = df.astype(BF16)
            dg_ref[0:1, :] += jnp.sum(dgt, axis=0, keepdims=True)
            yield

        def tile(first):
            pending = iter(()) if first else after_norm()
            h = h_ref[...]
            squares = []
            for c in range(nblk):
                if first:
                    loads[c].wait()
                u = jnp.maximum(_dot(h, wu_sc[c]), 0.0)
                u_ref[:, c * tf:(c + 1) * tf] = u.astype(BF16)
                squares.append((u * u).astype(BF16))
                next(pending, None)
            if first:
                loads[nblk].wait()
            f_sc[...] = _dot(jnp.concatenate(squares, axis=1), wd_sc[...])

        @pl.when(s == 0)
        def _():
            for cp in loads:
                cp.start()
            dg_ref[...] = jnp.zeros_like(dg_ref)
            loss_ref[...] = jnp.zeros_like(loss_ref)
            tile(True)

        pl.when((s > 0) & (s < nt))(functools.partial(tile, False))

        @pl.when(s == nt)
        def _():
            for _ in after_norm():
                pass

    cur = lambda s: (jnp.minimum(s, nt - 1), 0)
    prev = lambda s: (jnp.maximum(s - 1, 0), 0)
    anyspec = pl.BlockSpec(memory_space=pl.ANY)
    return pl.pallas_call(
        body,
        grid=(nt + 1,),
        in_specs=[pl.BlockSpec((tm, d), cur), anyspec, anyspec,
                  pl.BlockSpec((tm, d), prev), pl.BlockSpec((tm, d), prev),
                  pl.BlockSpec((1, d), lambda s: (0, 0))],
        out_specs=[pl.BlockSpec((tm, ff), cur), pl.BlockSpec((tm, d), prev), pl.BlockSpec((tm, d), prev),
                   pl.BlockSpec((8, d), lambda s: (0, 0)),
                   pl.BlockSpec((8, 128), lambda s: (0, 0))],
        out_shape=[SDS((t, ff), BF16), SDS((t, d), BF16), SDS((t, d), F32), SDS((8, d), F32), SDS((8, 128), F32)],
        scratch_shapes=[pltpu.VMEM((tm, d), F32), pltpu.VMEM((nblk, d, tf), BF16), pltpu.VMEM((ff, d), BF16),
                        pltpu.SemaphoreType.DMA((nblk + 1,))],
        compiler_params=_cp("arbitrary"),
        name="ffn_fwd",
    )(h2, wup_g, wdn, x1, target, g4)


def _ffn_bwd(df, u, wdn, wup_g, x1, dy, mix, g3, g2, tm):
    t, d = x1.shape
    nblk, _, tf = wup_g.shape
    ff = nblk * tf
    nt = t // tm

    def body(df_ref, u_ref, wd_hbm, wu_hbm, x1_ref, dy_ref, mix_ref, g3_ref, g2_ref,
             dpre_ref, dx1_ref, dmix_ref, dg3_ref, dg2_ref, dh_sc, wd_sc, wu_sc, sems):
        s = pl.program_id(0)
        blocks = [slice(c * tf, (c + 1) * tf) for c in range(nblk)]
        wd_loads = [pltpu.make_async_copy(wd_hbm.at[blocks[c], :], wd_sc.at[blocks[c], :], sems.at[c])
                    for c in range(nblk)]
        wu_loads = [pltpu.make_async_copy(wu_hbm.at[c], wu_sc.at[c], sems.at[nblk + c]) for c in range(nblk)]

        def norms_bwd():
            g3 = g3_ref[...]
            _, n3, r3 = _rms_fwd(x1_ref[...], g3)
            dx, dgt3 = _rms_bwd(n3, r3, g3, dh_sc[...])
            dx1 = dy_ref[...] + dx
            dx1_ref[...] = dx1
            dg3_ref[0:1, :] += jnp.sum(dgt3, axis=0, keepdims=True)
            yield
            g2 = g2_ref[...]
            _, n2, r2 = _rms_fwd(mix_ref[...], g2)
            dmix, dgt2 = _rms_bwd(n2, r2, g2, dx1_ref[...])
            dmix_ref[...] = dmix.astype(BF16)
            dg2_ref[0:1, :] += jnp.sum(dgt2, axis=0, keepdims=True)
            yield

        def tile(first):
            pending = iter(()) if first else norms_bwd()
            if first:
                parts = []
                for c in range(nblk):
                    wd_loads[c].wait()
                    du2 = _dot_nt(df_ref[...], wd_sc[blocks[c], :])
                    parts.append((2.0 * u_ref[:, blocks[c]].astype(F32) * du2).astype(BF16))
                dpre = jnp.concatenate(parts, axis=1)
            else:
                du2 = _dot_nt(df_ref[...], wd_sc[...])
                dpre = (2.0 * u_ref[...].astype(F32) * du2).astype(BF16)
            dpre_ref[...] = dpre
            dh = None
            for c in range(nblk):
                if first:
                    wu_loads[c].wait()
                if c in (1, 3):
                    next(pending, None)
                part = _dot_nt(dpre[:, blocks[c]], wu_sc[c])
                dh = part if dh is None else dh + part
            dh_sc[...] = dh

        @pl.when(s == 0)
        def _():
            for cp in wd_loads + wu_loads:
                cp.start()
            dg3_ref[...] = jnp.zeros_like(dg3_ref)
            dg2_ref[...] = jnp.zeros_like(dg2_ref)
            tile(True)

        pl.when((s > 0) & (s < nt))(functools.partial(tile, False))

        @pl.when(s == nt)
        def _():
            for _ in norms_bwd():
                pass

    cur = lambda s: (jnp.minimum(s, nt - 1), 0)
    prev = lambda s: (jnp.maximum(s - 1, 0), 0)
    vec = pl.BlockSpec((1, d), lambda s: (0, 0))
    acc8 = pl.BlockSpec((8, d), lambda s: (0, 0))
    anyspec = pl.BlockSpec(memory_space=pl.ANY)
    return pl.pallas_call(
        body,
        grid=(nt + 1,),
        in_specs=[pl.BlockSpec((tm, d), cur),
                  pl.BlockSpec((tm, ff), cur), anyspec, anyspec,
                  pl.BlockSpec((tm, d), prev), pl.BlockSpec((tm, d), prev), pl.BlockSpec((tm, d), prev), vec, vec],
        out_specs=[pl.BlockSpec((tm, ff), cur), pl.BlockSpec((tm, d), prev), pl.BlockSpec((tm, d), prev),
                   acc8, acc8],
        out_shape=[SDS(u.shape, BF16), SDS((t, d), F32), SDS((t, d), BF16), SDS((8, d), F32), SDS((8, d), F32)],
        scratch_shapes=[pltpu.VMEM((tm, d), F32), pltpu.VMEM((ff, d), BF16), pltpu.VMEM((nblk, d, tf), BF16),
                        pltpu.SemaphoreType.DMA((2 * nblk,))],
        compiler_params=_cp("arbitrary"),
        name="ffn_bwd",
    )(df, u, wdn, wup_g, x1, dy, mix, g3, g2)


WGRAD_FIRST_CHUNKS = 8


def _wgrad(a, b, a_cols, b_cols, out_block, out_shape, nj, nk, tt, name, prep_a=None):
    acc_shape = out_block[1:]
    wa, wb = a_cols(0)[1], b_cols(0)[1]
    nc = WGRAD_FIRST_CHUNKS
    ch = tt // nc

    def body(a_hbm, b_hbm, o_ref, acc_sc, a_sc, b_sc, sems, first_sems):
        j, k = pl.program_id(0), pl.program_id(1)
        s = j * nk + k

        def aligned(v, m):
            return v if isinstance(v, int) else pl.multiple_of(v, m)

        def fetch(jj, kk, slot, rows=None, sem=None):
            r0, n = (0, tt) if rows is None else rows
            out = []
            for x, (hbm, sc, cols) in enumerate(((a_hbm, a_sc, a_cols), (b_hbm, b_sc, b_cols))):
                c0, w = cols(jj)
                src = hbm.at[pl.ds(aligned(kk * tt + r0, ch), n), pl.ds(aligned(c0, 128), w)]
                out.append(pltpu.make_async_copy(src, sc.at[slot, pl.ds(r0, n), :],
                                                 (sems.at[x, slot] if sem is None else sem.at[x])))
            return out

        def prepped(v):
            return v if prep_a is None else prep_a(v)

        def accumulate(part):
            @pl.when(k == 0)
            def _():
                acc_sc[...] = part

            @pl.when(k > 0)
            def _():
                acc_sc[...] += part

            @pl.when(k == nk - 1)
            def _():
                o_ref[0] = acc_sc[...].astype(BF16)

        def start_next(slot):
            nxt = s + 1

            @pl.when(nxt < nj * nk)
            def _():
                for cp in fetch(nxt // nk, lax.rem(nxt, nk), slot):
                    cp.start()

        @pl.when(s == 0)
        def _():
            chunks = [fetch(0, 0, 0, rows=(c * ch, ch), sem=first_sems.at[c]) for c in range(nc)]
            for pair in chunks:
                for cp in pair:
                    cp.start()
            start_next(1)
            part = None
            for c, pair in enumerate(chunks):
                for cp in pair:
                    cp.wait()
                rows = slice(c * ch, (c + 1) * ch)
                piece = _dot_tn(prepped(a_sc[0, rows, :]), b_sc[0, rows, :])
                part = piece if part is None else part + piece
            accumulate(part)

        def steady(slot):
            start_next(1 - slot)
            for cp in fetch(j, k, slot):
                cp.wait()
            accumulate(_dot_tn(prepped(a_sc[slot]), b_sc[slot]))

        even = lax.rem(s, 2) == 0
        pl.when((s > 0) & even)(functools.partial(steady, 0))
        pl.when(jnp.logical_not(even))(functools.partial(steady, 1))

    anyspec = pl.BlockSpec(memory_space=pl.ANY)
    return pl.pallas_call(
        body,
        grid=(nj, nk),
        in_specs=[anyspec, anyspec],
        out_specs=pl.BlockSpec(out_block, lambda j, k: (j, 0, 0)),
        out_shape=SDS(out_shape, BF16),
        scratch_shapes=[pltpu.VMEM(acc_shape, F32), pltpu.VMEM((2, tt, wa), a.dtype), pltpu.VMEM((2, tt, wb), b.dtype),
                        pltpu.SemaphoreType.DMA((2, 2)), pltpu.SemaphoreType.DMA((nc, 2))],
        compiler_params=_cp("arbitrary", "arbitrary"),
        name=name,
    )(a, b)


def _wgrad_cols(a, b, nj, tt, name):
    t, m = a.shape
    bn = b.shape[1] // nj
    return _wgrad(a, b, lambda j: (0, m), lambda j: (j * bn, bn), (1, m, bn), (nj, m, bn), nj, t // tt, tt, name)


def _wgrad_rows(a, b, nj, tt, name, square=False):
    t, n = b.shape
    bm = a.shape[1] // nj

    def squared(av):
        af = av.astype(F32)
        return (af * af).astype(BF16)

    return _wgrad(a, b, lambda j: (j * bm, bm), lambda j: (0, n), (1, bm, n), (nj, bm, n), nj, t // tt, tt, name,
                  prep_a=squared if square else None)


def _wgrad_o(oa, ob, dmix, nj, tt):
    t, n = dmix.shape
    ca, cb = oa.shape[1], ob.shape[1]
    m = ca + cb
    nk = t // tt

    def body(oa_ref, ob_ref, b_ref, o_ref, acc_sc):
        k = pl.program_id(0)
        part = _dot_tn(jnp.concatenate([oa_ref[...], ob_ref[...]], axis=1), b_ref[...])

        @pl.when(k == 0)
        def _():
            acc_sc[...] = part

        @pl.when(k > 0)
        def _():
            acc_sc[...] += part

        @pl.when(k == nk - 1)
        def _():
            o_ref[...] = acc_sc[...].reshape(nj, m // nj, n).astype(BF16)

    return pl.pallas_call(
        body,
        grid=(nk,),
        in_specs=[pl.BlockSpec((tt, ca), lambda k: (k, 0)), pl.BlockSpec((tt, cb), lambda k: (k, 0)),
                  pl.BlockSpec((tt, n), lambda k: (k, 0))],
        out_specs=pl.BlockSpec((nj, m // nj, n), lambda k: (0, 0, 0)),
        out_shape=SDS((nj, m // nj, n), BF16),
        scratch_shapes=[pltpu.VMEM((m, n), F32)],
        compiler_params=_cp("arbitrary"),
        name="wgrad_o",
    )(oa, ob, dmix)


def _attn_out_bwd(dmix, wo, ca, tm):
    t, d = dmix.shape
    cb = wo.shape[0] - ca

    def body(dm_ref, w_ref, da_ref, db_ref):
        dm = dm_ref[...]
        da_ref[...] = _dot_nt(dm, w_ref[0:ca, :]).astype(BF16)
        db_ref[...] = _dot_nt(dm, w_ref[ca:, :]).astype(BF16)

    return pl.pallas_call(
        body,
        grid=(t // tm,),
        in_specs=[pl.BlockSpec((tm, d), lambda i: (i, 0)), pl.BlockSpec(wo.shape, lambda i: (0, 0))],
        out_specs=[pl.BlockSpec((tm, ca), lambda i: (i, 0)), pl.BlockSpec((tm, cb), lambda i: (i, 0))],
        out_shape=[SDS((t, ca), BF16), SDS((t, cb), BF16)],
        compiler_params=_cp("parallel"),
        name="attn_out_bwd",
    )(dmix, wo)


def _heads_t(x):
    xt = x.astype(F32).T
    return jnp.concatenate([xt[h * HEAD_DIM:(h + 1) * HEAD_DIM, :] for h in range(GROUP)], axis=1)


def _heads_t_inv(yt):
    n = yt.shape[1] // GROUP
    return jnp.concatenate([yt[:, h * n:(h + 1) * n] for h in range(GROUP)], axis=0).T


def _attn_a_bwd(qa, ka, kat, va, do, o, lse, tq, tk, grads):
    bl, kv, s_len, _ = ka.shape
    nq, nk = s_len // tq, s_len // tk
    assert nk % 2 == 0
    r = GROUP * tq
    ng = len(grads)

    def body(q_ref, qn_ref, k_ref, kt_ref, v_ref, do_ref, don_ref, o_ref, l_ref, *rest):
        grad_refs, (dq_ref, dk_ref, dv_ref), parts = rest[:ng], rest[ng:ng + 3], rest[ng + 3:2 * ng + 3]
        st_sc, dp_sc, dkt_sc, dvt_sc, send_sems, recv_sems, local_sems = rest[2 * ng + 3:]
        i = pl.program_id(2)
        step_id = (pl.program_id(0) * kv + pl.program_id(1)) * nq + i
        start, wait = _direct_exchange("scatter", grad_refs, parts, send_sems, recv_sems, local_sems)
        pl.when(step_id == 0)(start)

        dot32 = _heads_t(do_ref[...])
        drow = jnp.sum(dot32 * _heads_t(o_ref[...]), axis=0, keepdims=True)
        qt, dot = _heads_t(q_ref[...]).astype(BF16), dot32.astype(BF16)
        lrow = l_ref[0, 0, 0, 0:1, :]

        @pl.when(i == 0)
        def _():
            dkt_sc[...] = jnp.zeros_like(dkt_sc)
            dvt_sc[...] = jnp.zeros_like(dvt_sc)

        def chunk(c):
            return pl.ds(pl.multiple_of(c * tk, tk), tk)

        def scores(c, slot, qv=qt, dov=dot):
            st_sc[slot] = _dot(k_ref[0, 0, chunk(c), :], qv)
            dp_sc[slot] = _dot(v_ref[0, 0, chunk(c), :], dov)

        def fold(slot, c, dqt):
            pt = jnp.exp(st_sc[slot] - lrow)
            dsb = (pt * (dp_sc[slot] - drow)).astype(BF16)
            dvt_sc[:, chunk(c)] += _dot_nt(dot, pt.astype(BF16))
            dkt_sc[:, chunk(c)] += _dot_nt(qt, dsb)
            return dqt + _dot(kt_ref[0, 0, :, chunk(c)], dsb)

        @pl.when(i == 0)
        def _():
            scores(0, 0)

        def step(c2, dqt):
            c = 2 * c2
            scores(c + 1, 1)
            dqt = fold(0, c, dqt)
            scores(c + 2, 0)
            return fold(1, c + 1, dqt)

        dqt = jnp.zeros((HEAD_DIM, r), F32)
        for c2 in range(nk // 2 - 1):
            dqt = step(c2, dqt)
        scores(nk - 1, 1)
        dqt = fold(0, nk - 2, dqt)
        scores(0, 0, _heads_t(qn_ref[...]).astype(BF16), _heads_t(don_ref[...]).astype(BF16))
        dq_ref[...] = _heads_t_inv(fold(1, nk - 1, dqt))

        @pl.when(i == nq - 1)
        def _():
            dk_ref[0, 0] = dkt_sc[...].T
            dv_ref[0, 0] = dvt_sc[...].T

        pl.when(step_id == bl * kv * nq - 1)(wait)

    kvspec = pl.BlockSpec((1, 1, s_len, HEAD_DIM), lambda b, g, i: (b, g, 0, 0))
    tok = pl.BlockSpec((tq, GROUP * HEAD_DIM), lambda b, g, i: (b * nq + i, g))
    toknext = pl.BlockSpec((tq, GROUP * HEAD_DIM), lambda b, g, i: (b * nq + jnp.minimum(i + 1, nq - 1), g))
    anyspec = pl.BlockSpec(memory_space=pl.ANY)
    res = pl.pallas_call(
        body,
        grid=(bl, kv, nq),
        in_specs=[tok, toknext, kvspec, pl.BlockSpec((1, 1, HEAD_DIM, s_len), lambda b, g, i: (b, g, 0, 0)), kvspec,
                  tok, toknext, tok, pl.BlockSpec((1, 1, 1, 8, r), lambda b, g, i: (b, g, i, 0, 0))] + [anyspec] * ng,
        out_specs=[tok, kvspec, kvspec] + [anyspec] * ng,
        out_shape=[SDS(qa.shape, F32), SDS(ka.shape, F32), SDS(va.shape, F32)]
        + [SDS(g.shape, g.dtype) for g in grads],
        scratch_shapes=[pltpu.VMEM((2, tk, r), F32), pltpu.VMEM((2, tk, r), F32),
                        pltpu.VMEM((HEAD_DIM, s_len), F32), pltpu.VMEM((HEAD_DIM, s_len), F32)]
        + _exchange_scratch(ng),
        compiler_params=_cp("arbitrary", "arbitrary", "arbitrary"),
        name="attn_a_bwd",
    )(qa, qa, ka, kat, va, do, do, o, lse, *grads)
    return res[0], res[1], res[2], res[3:]


def _attn_b_bwd(qb, kb, kbt, vb, do, o, lse, bias_t, sink, s_len):
    bl, kv, sp, _ = kb.shape
    nb = s_len // BLOCK
    nbs = min(QB_PER_STEP, nb)
    r = GROUP * BLOCK

    def body(q_ref, k_ref, kt_ref, v_ref, do_ref, o_ref, l_ref, bt_ref, sink_ref,
             dq_ref, dk_ref, dv_ref, dsum_ref, dsink_ref, dkt_sc, dvt_sc):
        g, b, ns = pl.program_id(0), pl.program_id(1), pl.program_id(2)
        sink_row = _sink_row(sink_ref, g)

        @pl.when(ns == 0)
        def _():
            dkt_sc[...] = jnp.zeros_like(dkt_sc)
            dvt_sc[...] = jnp.zeros_like(dvt_sc)

        @pl.when((b == 0) & (ns == 0))
        def _():
            dsum_ref[...] = jnp.zeros_like(dsum_ref)
            dsink_ref[...] = jnp.zeros_like(dsink_ref)

        dsum = jnp.zeros((SPAN, r), F32)
        dsink = jnp.zeros((1, r), F32)
        for j in range(nbs):
            n = ns * nbs + j
            span = pl.ds(pl.multiple_of(n * BLOCK, BLOCK), SPAN)
            rows = slice(j * BLOCK, (j + 1) * BLOCK)
            dot32 = _heads_t(do_ref[rows, :])
            drow = jnp.sum(dot32 * _heads_t(o_ref[rows, :]), axis=0, keepdims=True)
            qt, dot = _heads_t(q_ref[rows, :]).astype(BF16), dot32.astype(BF16)
            lrow = l_ref[0, 0, j, 0:1, :]
            st = _dot(k_ref[0, 0, span, :], qt) + bt_ref[_bias_variant(n, nb), 0]
            pt = jnp.exp(st - lrow)
            dst = pt * (_dot(v_ref[0, 0, span, :], dot) - drow)
            dsum = dsum + dst
            dsink = dsink - jnp.exp(sink_row - lrow) * drow
            dsb = dst.astype(BF16)
            dvt_sc[:, span] += _dot_nt(dot, pt.astype(BF16))
            dkt_sc[:, span] += _dot_nt(qt, dsb)
            dq_ref[rows, :] = _heads_t_inv(_dot(kt_ref[0, 0, :, span], dsb))
        dsum_ref[0] += dsum
        dsink_ref[0, 0:1, :] += dsink

        @pl.when(ns == nb // nbs - 1)
        def _():
            dk_ref[0, 0] = dkt_sc[:, BLOCK:BLOCK + s_len].T
            dv_ref[0, 0] = dvt_sc[:, BLOCK:BLOCK + s_len].T

    kvspec = pl.BlockSpec((1, 1, sp, HEAD_DIM), lambda g, b, n: (b, g, 0, 0))
    kvout = pl.BlockSpec((1, 1, s_len, HEAD_DIM), lambda g, b, n: (b, g, 0, 0))
    tok = pl.BlockSpec((nbs * BLOCK, GROUP * HEAD_DIM), lambda g, b, n: (b * (nb // nbs) + n, g))
    return pl.pallas_call(
        body,
        grid=(kv, bl, nb // nbs),
        in_specs=[tok, kvspec, pl.BlockSpec((1, 1, HEAD_DIM, sp), lambda g, b, n: (b, g, 0, 0)), kvspec, tok, tok,
                  pl.BlockSpec((1, 1, nbs, 8, r), lambda g, b, n: (b, g, n, 0, 0)),
                  pl.BlockSpec((3, 1, SPAN, r), lambda g, b, n: (0, g, 0, 0)),
                  pl.BlockSpec(memory_space=pltpu.SMEM)],
        out_specs=[tok, kvout, kvout,
                   pl.BlockSpec((1, SPAN, r), lambda g, b, n: (g, 0, 0)),
                   pl.BlockSpec((1, 8, r), lambda g, b, n: (g, 0, 0))],
        out_shape=[SDS(qb.shape, F32), SDS((bl, kv, s_len, HEAD_DIM), F32), SDS((bl, kv, s_len, HEAD_DIM), F32),
                   SDS((kv, SPAN, r), F32), SDS((kv, 8, r), F32)],
        scratch_shapes=[pltpu.VMEM((HEAD_DIM, sp), F32), pltpu.VMEM((HEAD_DIM, sp), F32)],
        compiler_params=_cp("arbitrary", "arbitrary", "arbitrary"),
        name="attn_b_bwd",
    )(qb, kb, kbt, vb, do, o, lse, bias_t, sink)


def _bias_reduce(dsum, dsink, bucket_t4):
    kv, _, r = dsum.shape

    def body(ds_ref, dk_ref, bk_ref, rel_ref, sink_ref):
        lane = lax.broadcasted_iota(jnp.int32, (N_BUCKETS, 128), 1)
        lane8 = lax.broadcasted_iota(jnp.int32, (8, 128), 1)
        bk = bk_ref[...]
        for g in range(kv):
            ds = ds_ref[g]
            rowi = lax.broadcasted_iota(jnp.int32, (N_BUCKETS, r), 0)
            red = jnp.zeros((N_BUCKETS, r), F32)
            for b in range(N_BUCKETS):
                red = jnp.where(rowi == b, jnp.sum(jnp.where(bk == b, ds, 0.0), axis=0, keepdims=True), red)
            out = jnp.zeros((N_BUCKETS, 128), F32)
            so = jnp.zeros((8, 128), F32)
            for h in range(GROUP):
                col = jnp.sum(red[:, h * BLOCK:(h + 1) * BLOCK], axis=1, keepdims=True)
                out = jnp.where(lane == h, col, out)
                sc = jnp.sum(dk_ref[g][:, h * BLOCK:(h + 1) * BLOCK], axis=1, keepdims=True)
                so = jnp.where(lane8 == h, sc, so)
            rel_ref[g] = out
            sink_ref[g] = so

    vm = pl.BlockSpec(memory_space=pltpu.VMEM)
    return pl.pallas_call(
        body,
        in_specs=[vm, vm, vm],
        out_specs=[vm, vm],
        out_shape=[SDS((kv, N_BUCKETS, 128), F32), SDS((kv, 8, 128), F32)],
        name="bias_reduce",
    )(dsum, dsink, bucket_t4)


def _dqkprep(dqa, dka, dva, dqb, dkb, dvb, proj, h1, cos, sin_signed, gq, gk, s_len, ts):
    t, p_cols = proj.shape
    d = h1.shape[1]
    bl, kva, kvb = dka.shape[0], dka.shape[1], dkb.shape[1]
    ha, hb = dqa.shape[1] // HEAD_DIM, dqb.shape[1] // HEAD_DIM
    ns = s_len // ts

    def body(dqa_ref, dka_ref, dva_ref, dqb_ref, dkb_ref, dvb_ref, p_ref, h1_ref, h1p_ref, cos_ref, sin_ref,
             gq_ref, gk_ref, dp_ref, dgq_ref, dgk_ref, gw_ref, gw_sc, dpp_sc):
        b, i = pl.program_id(0), pl.program_id(1)
        cs, sn = cos_ref[...], sin_ref[...]
        low, first = _pair_masks(ts)

        @pl.when((b == 0) & (i == 0))
        def _():
            dgq_ref[...] = jnp.zeros_like(dgq_ref)
            dgk_ref[...] = jnp.zeros_like(dgk_ref)
            gw_sc[...] = jnp.zeros_like(gw_sc)
            dpp_sc[...] = jnp.zeros_like(dpp_sc)

        n_parts = 6
        pw = p_cols // n_parts

        def wgrad_part(c):
            rows = slice(c * pw, (c + 1) * pw)
            gw_sc[rows, :] += _dot_tn(dpp_sc[:, rows], h1p_ref[...])

        def grad_pair(ref, p):
            return jnp.concatenate([ref[0, 2 * p], ref[0, 2 * p + 1]], axis=1)

        def put(p, val):
            dp_ref[:, p * PAIR:(p + 1) * PAIR] = val.astype(BF16)

        def unrope_norm(d_rot, p, g, dg_ref):
            dn = d_rot * cs + _pair_partner(d_rot * sn, first)
            xp = p_ref[:, p * PAIR:(p + 1) * PAIR]
            r = lax.rsqrt(_pair_mean(xp * xp, low) + EPS)
            n = xp * r
            gd = g * dn
            dg_ref[0:1, :] += jnp.sum(dn * n, axis=0, keepdims=True)
            put(p, r * (gd - n * _pair_mean(n * gd, low)))

        parts = iter(range(n_parts))

        def next_wgrad_part():
            c = next(parts, None)
            if c is not None:
                wgrad_part(c)

        for p in range(ha // 2):
            next_wgrad_part()
            unrope_norm(dqa_ref[:, p * PAIR:(p + 1) * PAIR] * SCALE, p, gq_ref[...], dgq_ref)
        base = ha // 2
        for p in range(kva // 2):
            next_wgrad_part()
            unrope_norm(grad_pair(dka_ref, p), base + p, gk_ref[...], dgk_ref)
            put(base + kva // 2 + p, grad_pair(dva_ref, p))
        base += kva
        for p in range(hb // 2):
            put(base + p, dqb_ref[:, p * PAIR:(p + 1) * PAIR] * SCALE)
        base += hb // 2
        for p in range(kvb // 2):
            put(base + p, grad_pair(dkb_ref, p))
            put(base + kvb // 2 + p, grad_pair(dvb_ref, p))
        for c in parts:
            wgrad_part(c)

        dpp_sc[...] = dp_ref[...]

        @pl.when((b == bl - 1) & (i == ns - 1))
        def _():
            gw_ref[...] = (gw_sc[...] + _dot_tn(dp_ref[...], h1_ref[...])).astype(BF16)

    def hm(nh):
        return pl.BlockSpec((1, nh, ts, HEAD_DIM), lambda b, i: (b, 0, i, 0))

    def tokmajor(nh):
        return pl.BlockSpec((ts, nh * HEAD_DIM), lambda b, i: (b * ns + i, 0))

    vec = pl.BlockSpec((1, PAIR), lambda b, i: (0, 0))
    tab = pl.BlockSpec((ts, PAIR), lambda b, i: (i, 0))
    acc = pl.BlockSpec((8, PAIR), lambda b, i: (0, 0))
    pspec = pl.BlockSpec((ts, p_cols), lambda b, i: (b * ns + i, 0))
    return pl.pallas_call(
        body,
        grid=(bl, ns),
        in_specs=[tokmajor(ha), hm(kva), hm(kva), tokmajor(hb), hm(kvb), hm(kvb), pspec,
                  pl.BlockSpec((ts, d), lambda b, i: (b * ns + i, 0)),
                  pl.BlockSpec((ts, d), lambda b, i: (jnp.maximum(b * ns + i - 1, 0), 0)), tab, tab, vec, vec],
        out_specs=[pspec, acc, acc, pl.BlockSpec((p_cols, d), lambda b, i: (0, 0))],
        out_shape=[SDS((t, p_cols), BF16), SDS((8, PAIR), F32), SDS((8, PAIR), F32), SDS((p_cols, d), BF16)],
        scratch_shapes=[pltpu.VMEM((p_cols, d), F32), pltpu.VMEM((ts, p_cols), BF16)],
        compiler_params=_cp("arbitrary", "arbitrary"),
        name="dqkprep",
    )(dqa, dka, dva, dqb, dkb, dvb, proj, h1, h1, cos, sin_signed, gq, gk)


def _dx_final(dproj, w_t, x2, dx1, g1, tm, grads, small):
    t, d = x2.shape
    p_cols = w_t.shape[0]
    ng, nsm = len(grads), len(small)
    nsteps = t // tm

    def body(dp_ref, w_ref, x_ref, dx1_ref, g_ref, *rest):
        small_refs, grad_refs = rest[:nsm], rest[nsm:nsm + ng]
        dx_ref, vec_ref, rel_ref = rest[nsm + ng:nsm + ng + 3]
        parts = rest[nsm + ng + 3:nsm + 2 * ng + 3]
        sems, dg_sc, small_scratch = rest[nsm + 2 * ng + 3:nsm + 2 * ng + 6], rest[nsm + 2 * ng + 6], rest[nsm + 2 * ng + 7:]
        start, wait = _direct_exchange("scatter", grad_refs, parts, *sems)
        start_small, finish_small = _small_allreduce(dg_sc, *small_refs, vec_ref, rel_ref, *small_scratch)

        @pl.when(pl.program_id(0) == 0)
        def _():
            start()
            dg_sc[...] = jnp.zeros_like(dg_sc)

        dh = _dot(dp_ref[...], w_ref[...])
        g = g_ref[...]
        _, n, r = _rms_fwd(x_ref[...], g)
        dx, dgt = _rms_bwd(n, r, g, dh)
        dx_ref[...] = dx1_ref[...] + dx
        dg_sc[0:1, :] += jnp.sum(dgt, axis=0, keepdims=True)

        @pl.when(pl.program_id(0) == nsteps - 1)
        def _():
            start_small()
            wait()
            finish_small()

    tile = pl.BlockSpec((tm, d), lambda i: (i, 0))
    anyspec = pl.BlockSpec(memory_space=pl.ANY)

    def whole(a):
        return pl.BlockSpec(a.shape, lambda i: (0,) * a.ndim)

    vec_shape, rel_shape = SDS((8, d), F32), SDS((N_BUCKETS, 128), F32)
    res = pl.pallas_call(
        body,
        grid=(nsteps,),
        in_specs=[pl.BlockSpec((tm, p_cols), lambda i: (i, 0)),
                  pl.BlockSpec((p_cols, d), lambda i: (0, 0)),
                  tile, tile, pl.BlockSpec((1, d), lambda i: (0, 0))] + [whole(a) for a in small] + [anyspec] * ng,
        out_specs=[tile, whole(vec_shape), whole(rel_shape)] + [anyspec] * ng,
        out_shape=[SDS((t, d), F32), vec_shape, rel_shape] + [SDS(g.shape, g.dtype) for g in grads],
        scratch_shapes=_exchange_scratch(ng) + [pltpu.VMEM((8, d), F32)] + _small_allreduce_scratch(d),
        compiler_params=_cp("arbitrary"),
        name="dx_final",
    )(dproj, w_t, x2, dx1, g1, *small, *grads)
    return res[0], res[1], res[2], res[3:]


def _adamw_math(w, g, m, v):
    m = ADAM_B1 * m + (1.0 - ADAM_B1) * g
    v = ADAM_B2 * v + (1.0 - ADAM_B2) * (g * g)
    m_hat = m / (1.0 - ADAM_B1 ** ADAM_STEP)
    v_hat = v / (1.0 - ADAM_B2 ** ADAM_STEP)
    delta = -ADAM_LR * (m_hat / (jnp.sqrt(v_hat) + ADAM_EPS) + ADAM_WD * w)
    return delta, m, v


def _adamw_sum(parts, ws, ms, vs, steps):
    nw = len(ws)

    def body(*refs):
        ins, outs = refs[:4 * nw], refs[4 * nw:]
        for k in range(nw):
            p_ref, w_ref, m_ref, v_ref = ins[4 * k:4 * k + 4]
            g_ref, d_ref, nm_ref, nv_ref = outs[4 * k:4 * k + 4]
            g = p_ref[0].astype(F32)
            for s in range(1, N_DEV):
                g = g + p_ref[s].astype(F32)
            g_ref[...] = g
            d_ref[...], nm_ref[...], nv_ref[...] = _adamw_math(w_ref[...], g, m_ref[...], v_ref[...])

    in_specs, out_specs, out_shape, args = [], [], [], []
    for p, w, m, v in zip(parts, ws, ms, vs):
        rows, cols = w.shape
        tile = pl.BlockSpec((rows // steps, cols), lambda i: (i, 0))
        in_specs += [pl.BlockSpec((N_DEV, rows // steps, cols), lambda i: (0, i, 0)), tile, tile, tile]
        out_specs += [tile] * 4
        out_shape += [SDS((rows, cols), F32)] * 4
        args += [p, w, m, v]
    res = pl.pallas_call(
        body,
        grid=(steps,),
        in_specs=in_specs,
        out_specs=out_specs,
        out_shape=out_shape,
        compiler_params=_cp("parallel"),
        name="adamw_weights",
    )(*args)
    return [res[4 * k:4 * k + 4] for k in range(nw)]


def _adamw_small(vec, rel, ws, ms, vs):
    hb = ws[6].shape[1]
    n = len(ws)

    def body(vec_ref, rel_ref, *rest):
        w_refs, m_refs, v_refs = rest[:n], rest[n:2 * n], rest[2 * n:3 * n]
        loss_ref, outs = rest[3 * n], rest[3 * n + 1:]
        grads = [vec_ref[0:1, :], vec_ref[1:2, :], vec_ref[2:3, :], vec_ref[3:4, :],
                 vec_ref[4:5, 0:HEAD_DIM], vec_ref[4:5, SMALL_LANES:SMALL_LANES + HEAD_DIM],
                 vec_ref[4:5, 2 * SMALL_LANES:2 * SMALL_LANES + hb], rel_ref[...].T[0:hb, :]]
        loss_ref[...] = vec_ref[4:5, 3 * SMALL_LANES:3 * SMALL_LANES + 1]
        for p, g in enumerate(grads):
            g_ref, d_ref, nm_ref, nv_ref = outs[4 * p:4 * p + 4]
            g_ref[...] = g
            d_ref[...], nm_ref[...], nv_ref[...] = _adamw_math(w_refs[p][...], g, m_refs[p][...], v_refs[p][...])

    vm = pl.BlockSpec(memory_space=pltpu.VMEM)
    res = pl.pallas_call(
        body,
        in_specs=[vm] * (2 + 3 * n),
        out_specs=[vm] * (1 + 4 * n),
        out_shape=[SDS((1, 1), F32)] + [SDS(w.shape, F32) for w in ws for _ in range(4)],
        name="adamw_small",
    )(vec, rel, *ws, *ms, *vs)
    return res[0], [res[1 + 4 * p:5 + 4 * p] for p in range(n)]


def _local_step(x, loss_target, win_s, wo_s, wup_s, wdn_s, g_pre_mix, g_post_mix, q_norm_a, k_norm_a, sink_b,
                rel_bias_t, g_pre_ffn, g_post_ffn):
    bl, s_len, d = x.shape
    t = bl * s_len
    nh = d // HEAD_DIM
    ha = nh // 2
    kva = ha // GROUP
    hb = nh - ha
    kvb = hb // GROUP
    tm = 512
    tp = min(1024, t)
    tw = min(4096, t)
    ts = min(512, s_len)
    tq, tk = 2 * BLOCK, min(512, s_len // 2)

    x2 = x.reshape(t, d)
    tg2 = loss_target.reshape(t, d)
    cos, sin_signed = _rope_tables(s_len)
    gq2, gk2 = jnp.tile(q_norm_a, (1, 2)), jnp.tile(k_norm_a, (1, 2))
    a = jnp.arange(BLOCK, dtype=jnp.int32)
    c = jnp.arange(SPAN, dtype=jnp.int32)
    bucket_t = _t5_bucket(c[:, None] - BLOCK - a[None, :])
    bucket_t4 = jnp.tile(bucket_t, (1, GROUP))
    (win_g,), bias_t = _weight_gather([win_s], bucket_t, rel_bias_t)
    w_in_t = win_g.reshape(-1, d)
    p_cols = w_in_t.shape[0]

    h1, proj, qa, ka, kat, va, vat, qb, kb, kbt, vb, vbt = _inproj_qkprep(
        x2, g_pre_mix, w_in_t, cos, sin_signed, gq2, gk2, bl, s_len, ha, kva, hb, kvb, ts)
    oa, lse_a, (wo_g, wup_g, wdn_g) = _attn_a_fwd(qa, ka, vat, tq, tk, [wo_s, wup_s, wdn_s])
    wo = wo_g.reshape(-1, d)
    wdn = wdn_g.reshape(-1, d)
    ob, lse_b = _attn_b_fwd(qb, kb, vbt, bias_t, sink_b, s_len)
    mix, x1, h2 = _mixout(oa, ob, wo, x2, g_post_mix, g_pre_ffn, tp)
    u, df, dy, dg4, loss8 = _ffn_fwd(h2, wup_g, wdn, x1, tg2, g_post_ffn, tm)

    dpre, dx1, dmix, dg3, dg2 = _ffn_bwd(df, u, wdn, wup_g, x1, dy, mix, g_pre_ffn, g_post_mix, FFN_BWD_TOKENS)
    gw_dn = _wgrad_rows(u, df, N_DEV, tw, "wgrad_down", square=True)
    gw_up = _wgrad_cols(h2, dpre, N_DEV, tw, "wgrad_up")
    gw_o = _wgrad_o(oa, ob, dmix, N_DEV, min(2048, t))
    doa, dob = _attn_out_bwd(dmix, wo, oa.shape[1], tp)
    dqa, dka, dva, (p_o, p_up, p_dn) = _attn_a_bwd(qa, ka, kat, va, doa, oa, lse_a, tq, tk, [gw_o, gw_up, gw_dn])
    dqb, dkb, dvb, dsum, dsink = _attn_b_bwd(qb, kb, kbt, vb, dob, ob, lse_b, bias_t, sink_b, s_len)
    drel_g, dsink_g = _bias_reduce(dsum, dsink, bucket_t4)
    dproj, dgq, dgk, gw_in_t = _dqkprep(dqa, dka, dva, dqb, dkb, dvb, proj, h1, cos, sin_signed, gq2, gk2, s_len, ts)
    gw_in_t = gw_in_t.reshape(N_DEV, -1, d)
    grad_x, vec, rel, (p_in,) = _dx_final(dproj, w_in_t, x2, dx1, g_pre_mix, tp, [gw_in_t],
                                          [dg2, dg3, dg4, dgq, dgk, dsink_g, drel_g, loss8])
    return grad_x.reshape(bl, s_len, d), p_in, p_o, p_up, p_dn, vec, rel


def kernel(x, w_in, w_o, g_pre_mix, g_post_mix, q_norm_a, k_norm_a, sink_b, rel_bias, g_pre_ffn, w_ffn_up, w_ffn_down, g_post_ffn, loss_target, m_w_in, m_w_o, m_g_pre_mix, m_g_post_mix, m_q_norm_a, m_k_norm_a, m_sink_b, m_rel_bias, m_g_pre_ffn, m_w_ffn_up, m_w_ffn_down, m_g_post_ffn, v_w_in, v_w_o, v_g_pre_mix, v_g_post_mix, v_q_norm_a, v_k_norm_a, v_sink_b, v_rel_bias, v_g_pre_ffn, v_w_ffn_up, v_w_ffn_down, v_g_post_ffn):
    w_in_t = w_in[0].T
    rel_bias_t = rel_bias.T

    grad_x, p_in, p_o, p_up, p_dn, vec, rel = _local_step(
        x, loss_target, w_in_t.astype(BF16), w_o[0].astype(BF16), w_ffn_up[0].astype(BF16), w_ffn_down[0].astype(BF16),
        g_pre_mix, g_post_mix, q_norm_a, k_norm_a, sink_b, rel_bias_t, g_pre_ffn, g_post_ffn)

    r_in, r_o, r_up, r_dn = _adamw_sum(
        [p_in, p_o, p_up, p_dn],
        [w_in_t, w_o[0], w_ffn_up[0], w_ffn_down[0]],
        [m_w_in[0].T, m_w_o[0], m_w_ffn_up[0], m_w_ffn_down[0]],
        [v_w_in[0].T, v_w_o[0], v_w_ffn_up[0], v_w_ffn_down[0]], 4)
    big = {"w_in": [a.T for a in r_in], "w_o": r_o, "w_up": r_up, "w_dn": r_dn}
    loss, small = _adamw_small(
        vec, rel,
        [g_pre_mix, g_post_mix, g_pre_ffn, g_post_ffn, q_norm_a, k_norm_a, sink_b, rel_bias_t],
        [m_g_pre_mix, m_g_post_mix, m_g_pre_ffn, m_g_post_ffn, m_q_norm_a, m_k_norm_a, m_sink_b, m_rel_bias.T],
        [v_g_pre_mix, v_g_post_mix, v_g_pre_ffn, v_g_post_ffn, v_q_norm_a, v_k_norm_a, v_sink_b, v_rel_bias.T])
    s_pre_mix, s_post_mix, s_pre_ffn, s_post_ffn, s_qn, s_kn, s_sink, s_rel_t = small
    s_rel = [a.T for a in s_rel_t]

    def outs(kind):
        return [big["w_in"][kind][None], big["w_o"][kind][None], s_pre_mix[kind], s_post_mix[kind], s_qn[kind],
                s_kn[kind], s_sink[kind], s_rel[kind], s_pre_ffn[kind], big["w_up"][kind][None],
                big["w_dn"][kind][None], s_post_ffn[kind]]

    return (loss.reshape(()), grad_x, *outs(0), *outs(1), *outs(2), *outs(3))
```

```python
import functools

import jax
import jax.numpy as jnp
import numpy as np
from jax import lax
from jax.experimental import pallas as pl
from jax.experimental.pallas import tpu as pltpu

F32 = jnp.float32
BF16 = jnp.bfloat16
SDS = jax.ShapeDtypeStruct

N_DEV = 8
HEAD_DIM = 64
GROUP = 4
BLOCK = 128
SPAN = 3 * BLOCK
GRID_W = 64
N_BUCKETS = 32
MAX_DISTANCE = 128
ROPE_THETA = 10000.0
EPS = 1e-6
NEG_INF = -1e30
SCALE = HEAD_DIM ** -0.5
VT_PAD = 16

ADAM_LR = 0.001
ADAM_B1 = 0.9
ADAM_B2 = 0.999
ADAM_EPS = 1e-08
ADAM_WD = 0.01
ADAM_STEP = 10

VMEM_LIMIT = 56 * 1024 * 1024
MESH = pl.DeviceIdType.MESH


def _cp(*sem):
    return pltpu.CompilerParams(dimension_semantics=sem, vmem_limit_bytes=VMEM_LIMIT)


def _dot(a, b):
    return jnp.dot(a, b, preferred_element_type=F32)


def _dot_nt(a, b):
    return lax.dot_general(a, b, (((1,), (1,)), ((), ())), preferred_element_type=F32)


def _dot_tn(a, b):
    return lax.dot_general(a, b, (((0,), (0,)), ((), ())), preferred_element_type=F32)


def _rms_fwd(x, g):
    r = lax.rsqrt(jnp.mean(x * x, axis=-1, keepdims=True) + EPS)
    n = x * r
    return n * g, n, r


def _rms_bwd(n, r, g, dy):
    gd = g * dy
    dx = r * (gd - n * jnp.mean(n * gd, axis=-1, keepdims=True))
    return dx, dy * n


def _rope_tables(s_len):
    rows = s_len // GRID_W
    row = np.repeat(np.arange(rows, dtype=np.int32), GRID_W)
    col = np.tile(np.arange(GRID_W, dtype=np.int32), rows)
    nf = HEAD_DIM // 4
    freqs = np.float32(ROPE_THETA) ** (-np.arange(nf, dtype=np.float32) / np.float32(nf))
    ang_r = row.astype(np.float32)[:, None] * freqs[None, :]
    ang_c = col.astype(np.float32)[:, None] * freqs[None, :]
    cr, sr, cc, sc = np.cos(ang_r), np.sin(ang_r), np.cos(ang_c), np.sin(ang_c)
    cos = np.concatenate([cr, cr, cc, cc] * 2, axis=-1).astype(np.float32)
    sin_signed = np.concatenate([-sr, sr, -sc, sc] * 2, axis=-1).astype(np.float32)
    return jnp.asarray(cos), jnp.asarray(sin_signed)


def _t5_bucket(rel):
    nb = N_BUCKETS // 2
    ret = (rel > 0).astype(jnp.int32) * nb
    n = jnp.abs(rel)
    max_exact = nb // 2
    nf = jnp.maximum(n, 1).astype(F32)
    large = max_exact + (jnp.log(nf / max_exact) / np.float32(np.log(MAX_DISTANCE / max_exact))
                         * (nb - max_exact)).astype(jnp.int32)
    large = jnp.minimum(large, nb - 1)
    return ret + jnp.where(n < max_exact, n, large)


def _mesh_pos():
    return lax.axis_index("x"), lax.axis_index("y"), lax.axis_index("c")


def _lin(p):
    return 4 * p[0] + 2 * p[1] + p[2]


def _bias_tables(bkt_ref, tbl_ref, out_ref, hb):
    bkt = bkt_ref[...]
    ci = lax.broadcasted_iota(jnp.int32, (SPAN, BLOCK), 0)
    qi = lax.broadcasted_iota(jnp.int32, (SPAN, BLOCK), 1)
    band = jnp.abs(ci - BLOCK - qi) <= BLOCK
    masks = (band, band & (ci >= BLOCK), band & (ci < 2 * BLOCK))
    for h in range(hb):
        acct = jnp.zeros((SPAN, BLOCK), F32)
        for b in range(N_BUCKETS):
            acct = jnp.where(bkt == b, tbl_ref[h, b], acct)
        lanes = slice((h % GROUP) * BLOCK, (h % GROUP + 1) * BLOCK)
        for var, mask in enumerate(masks):
            out_ref[var, h // GROUP, :, lanes] = jnp.where(mask, acct, NEG_INF)


def _weight_gather(shards, bucket_t, rel_bias_t):
    n = len(shards)
    hb = rel_bias_t.shape[0]

    def body(*refs):
        xs, (bkt_ref, tbl_ref), outs, bias_ref = refs[:n], refs[n:n + 2], refs[n + 2:2 * n + 2], refs[2 * n + 2]
        send_sems, recv_sems, local_sems = refs[2 * n + 3:]
        x, y, c = _mesh_pos()
        me, sibling = (x, y, c), (x, y, 1 - c)
        chips = [(1 - x, y), (x, 1 - y), (1 - x, 1 - y)]

        def copy(a, k, block, to, src=None):
            slot = outs[a].at[_lin(block)]
            return pltpu.make_async_remote_copy(
                src_ref=slot if src is None else src, dst_ref=slot,
                send_sem=send_sems.at[a, k], recv_sem=recv_sems.at[a, k],
                device_id=to, device_id_type=MESH)

        started = []
        for a in range(n):
            mine = pltpu.make_async_copy(xs[a], outs[a].at[_lin(me)], local_sems.at[a])
            mine.start()
            started.append(mine)
        sends = []
        for a in range(n):
            first = [copy(a, 0, me, sibling, src=xs[a])]
            first += [copy(a, 1 + j, me, (*chip, c), src=xs[a]) for j, chip in enumerate(chips)]
            for cp in first:
                cp.start()
            sends += first
        _bias_tables(bkt_ref, tbl_ref, bias_ref, hb)
        for a in range(n):
            for j, chip in enumerate(chips):
                copy(a, 1 + j, (*chip, c), me).wait_recv()
                fwd = copy(a, 4 + j, (*chip, c), sibling)
                fwd.start()
                sends.append(fwd)
        for a in range(n):
            copy(a, 0, sibling, me).wait_recv()
            for j, chip in enumerate(chips):
                copy(a, 4 + j, (*chip, 1 - c), me).wait_recv()
        for cp in sends:
            cp.wait_send()
        for mine in started:
            mine.wait()

    anyspec = pl.BlockSpec(memory_space=pl.ANY)
    vm = pl.BlockSpec(memory_space=pltpu.VMEM)
    res = pl.pallas_call(
        body,
        out_shape=[SDS((N_DEV,) + s.shape, s.dtype) for s in shards]
        + [SDS((3, hb // GROUP, SPAN, GROUP * BLOCK), F32)],
        in_specs=[anyspec] * n + [vm, pl.BlockSpec(memory_space=pltpu.SMEM)],
        out_specs=[anyspec] * n + [vm],
        scratch_shapes=[pltpu.SemaphoreType.DMA((n, 7)), pltpu.SemaphoreType.DMA((n, 7)),
                        pltpu.SemaphoreType.DMA((n,))],
        name="weight_gather",
    )(*shards, bucket_t, rel_bias_t)
    return res[:n], res[n]


def _direct_exchange(kind, ins, outs, send_sems, recv_sems, local_sems):
    x, y, c = _mesh_pos()
    me = (x, y, c)
    peers = [(x, y, 1 - c), (1 - x, y, c), (x, 1 - y, c), (1 - x, 1 - y, c),
             (1 - x, y, 1 - c), (x, 1 - y, 1 - c), (1 - x, 1 - y, 1 - c)]

    def src(a, to):
        return ins[a] if kind == "gather" else ins[a].at[_lin(to)]

    def remote(a, k, to, frm):
        return pltpu.make_async_remote_copy(
            src_ref=src(a, to), dst_ref=outs[a].at[_lin(frm)],
            send_sem=send_sems.at[a, k], recv_sem=recv_sems.at[a, k],
            device_id=to, device_id_type=MESH)

    n = len(ins)
    sends = [remote(a, k, p, me) for a in range(n) for k, p in enumerate(peers)]
    arrivals = [remote(a, k, p, p) for a in range(n) for k, p in enumerate(peers)]
    local = [pltpu.make_async_copy(src(a, me), outs[a].at[_lin(me)], local_sems.at[a]) for a in range(n)]

    def start():
        for cp in local + sends:
            cp.start()

    def wait():
        for cp in arrivals:
            cp.wait_recv()
        for cp in sends:
            cp.wait_send()
        for cp in local:
            cp.wait()

    return start, wait


def _exchange_scratch(n):
    return [pltpu.SemaphoreType.DMA((n, 7)), pltpu.SemaphoreType.DMA((n, 7)), pltpu.SemaphoreType.DMA((n,))]


SMALL_LANES = 128


def _small_allreduce(g1_ref, g2_ref, g3_ref, g4_ref, gq_ref, gk_ref, sk_ref, rl_ref, ls_ref, vec_ref, rel_ref,
                     vbuf, rbuf, vland, rland, send_sems, recv_sems):
    kv = sk_ref.shape[0]
    x, y, c = _mesh_pos()
    me = (x, y, c)
    peers = [(x, y, 1 - c), (1 - x, y, c), (x, 1 - y, c), (1 - x, 1 - y, c),
             (1 - x, y, 1 - c), (x, 1 - y, 1 - c), (1 - x, 1 - y, 1 - c)]

    def copies(k, to, frm):
        return [pltpu.make_async_remote_copy(
            src_ref=buf, dst_ref=land.at[_lin(frm)], send_sem=send_sems.at[a, k], recv_sem=recv_sems.at[a, k],
            device_id=to, device_id_type=MESH) for a, (buf, land) in enumerate(((vbuf, vland), (rbuf, rland)))]

    sends = [cp for k, p in enumerate(peers) for cp in copies(k, p, me)]

    def start():
        vbuf[...] = jnp.zeros_like(vbuf)
        rbuf[...] = jnp.zeros_like(rbuf)
        for row, ref in enumerate((g1_ref, g2_ref, g3_ref, g4_ref)):
            vbuf[row:row + 1, :] = ref[0:1, :]
        vbuf[4:5, 0:HEAD_DIM] = gq_ref[0:1, 0:HEAD_DIM] + gq_ref[0:1, HEAD_DIM:PAIR]
        vbuf[4:5, SMALL_LANES:SMALL_LANES + HEAD_DIM] = gk_ref[0:1, 0:HEAD_DIM] + gk_ref[0:1, HEAD_DIM:PAIR]
        for g in range(kv):
            vbuf[4:5, 2 * SMALL_LANES + g * GROUP:2 * SMALL_LANES + (g + 1) * GROUP] = sk_ref[g, 0:1, 0:GROUP]
            rbuf[:, g * GROUP:(g + 1) * GROUP] = rl_ref[g, :, 0:GROUP]
        vbuf[4:5, 3 * SMALL_LANES:3 * SMALL_LANES + 1] = ls_ref[0:1, 0:1]
        for cp in sends:
            cp.start()
        vland[_lin(me)] = vbuf[...]
        rland[_lin(me)] = rbuf[...]

    def finish():
        for k, p in enumerate(peers):
            for cp in copies(k, p, p):
                cp.wait_recv()
        for cp in sends:
            cp.wait_send()
        vacc, racc = vland[0], rland[0]
        for s in range(1, N_DEV):
            vacc, racc = vacc + vland[s], racc + rland[s]
        vec_ref[...] = vacc
        rel_ref[...] = racc

    return start, finish


def _small_allreduce_scratch(d):
    return [pltpu.VMEM((8, d), F32), pltpu.VMEM((N_BUCKETS, 128), F32),
            pltpu.VMEM((N_DEV, 8, d), F32), pltpu.VMEM((N_DEV, N_BUCKETS, 128), F32),
            pltpu.SemaphoreType.DMA((2, 7)), pltpu.SemaphoreType.DMA((2, 7))]


PAIR = 2 * HEAD_DIM


def _pair_masks(ts):
    lane = lax.broadcasted_iota(jnp.int32, (ts, PAIR), 1)
    return lane < HEAD_DIM, (lane % 32) < 16


def _pair_mean(v, low):
    del low
    r = lax.broadcasted_iota(jnp.int32, (PAIR, PAIR), 0) // HEAD_DIM
    c = lax.broadcasted_iota(jnp.int32, (PAIR, PAIR), 1) // HEAD_DIM
    same_head = (r == c).astype(BF16)
    hi = v.astype(BF16)
    lo = (v - hi.astype(F32)).astype(BF16)
    return (_dot(hi, same_head) + _dot(lo, same_head)) * (1.0 / HEAD_DIM)


def _for_pair(g_ref):
    return jnp.concatenate([g_ref[...], g_ref[...]], axis=1)


def _pair_partner(v, first):
    return jnp.where(first, pltpu.roll(v, PAIR - 16, 1), pltpu.roll(v, 16, 1))


def _inproj_qkprep(x2, g1, w_t, cos, sin_signed, gq, gk, bl, s_len, ha, kva, hb, kvb, ts):
    t, d = x2.shape
    p_cols = w_t.shape[0]
    assert ha % 2 == 0 and kva % 2 == 0 and hb % 2 == 0 and kvb % 2 == 0
    ns = s_len // ts
    nt = bl * ns
    sp = s_len + 2 * BLOCK

    def body(*refs):
        kb_ref, kbt_ref, vb_ref, vbt_ref, p_even, p_odd = refs[-6:]
        s = pl.program_id(0)
        i = lax.rem(jnp.maximum(s - 1, 0), ns)

        @pl.when(s == 0)
        def _():
            p_odd[...] = jnp.zeros_like(p_odd)

        @pl.when(i == 0)
        def _():
            zeros = jnp.zeros((kvb, BLOCK, HEAD_DIM), BF16)
            zeros_t = jnp.zeros((kvb, HEAD_DIM + VT_PAD, BLOCK), BF16)
            for ref in (kb_ref, vb_ref):
                ref[0, :, 0:BLOCK, :] = zeros
                ref[0, :, sp - BLOCK:sp, :] = zeros
            kbt_ref[0, :, :, 0:BLOCK] = zeros_t[:, 0:HEAD_DIM]
            kbt_ref[0, :, :, sp - BLOCK:sp] = zeros_t[:, 0:HEAD_DIM]
            vbt_ref[0, :, :, 0:BLOCK] = zeros_t
            vbt_ref[0, :, :, sp - BLOCK:sp] = zeros_t

        even = lax.rem(s, 2) == 0
        pl.when(even)(functools.partial(tile_work, p_even, p_odd, i, *refs[:-2]))
        pl.when(jnp.logical_not(even))(functools.partial(tile_work, p_odd, p_even, i, *refs[:-2]))

    def tile_work(p_new, p_ref, i, x_ref, g1_ref, w_ref, cos_ref, sin_ref, gq_ref, gk_ref, h_ref, po_ref, qa_ref,
                  ka_ref, kat_ref, va_ref, vat_ref, qb_ref, kb_ref, kbt_ref, vb_ref, vbt_ref):
        y, _, _ = _rms_fwd(x_ref[...], g1_ref[...])
        h = y.astype(BF16)
        h_ref[...] = h
        n_parts = 6
        pw = p_cols // n_parts

        def project(c):
            p_new[:, c * pw:(c + 1) * pw] = _dot_nt(h, w_ref[c * pw:(c + 1) * pw, :])

        cs, sn = cos_ref[...], sin_ref[...]
        low, first = _pair_masks(ts)
        ones_row = (lax.broadcasted_iota(jnp.int32, (VT_PAD, ts), 0) == 0).astype(BF16)
        heads = (slice(0, HEAD_DIM), slice(HEAD_DIM, PAIR))

        def pair(p):
            v = p_ref[:, p * PAIR:(p + 1) * PAIR]
            po_ref[:, p * PAIR:(p + 1) * PAIR] = v
            return v

        def normrope(x, g):
            y = x * lax.rsqrt(_pair_mean(x * x, low) + EPS) * g
            return y * cs + _pair_partner(y, first) * sn

        eye = (lax.broadcasted_iota(jnp.int32, (PAIR, PAIR), 0)
               == lax.broadcasted_iota(jnp.int32, (PAIR, PAIR), 1)).astype(BF16)

        def transposed(xb):
            return _dot_nt(eye, xb).astype(BF16)

        def prep_qa(p):
            qa_ref[:, p * PAIR:(p + 1) * PAIR] = (normrope(pair(p), _for_pair(gq_ref)) * SCALE).astype(BF16)

        def prep_kva(p):
            base = ha // 2
            k = normrope(pair(base + p), _for_pair(gk_ref)).astype(BF16)
            v = pair(base + kva // 2 + p).astype(BF16)
            kt, vt = transposed(k), transposed(v)
            for e, lanes in enumerate(heads):
                ka_ref[0, 2 * p + e] = k[:, lanes]
                va_ref[0, 2 * p + e] = v[:, lanes]
                kat_ref[0, 2 * p + e] = kt[lanes, :]
                vat_ref[0, 2 * p + e, 0:HEAD_DIM, :] = vt[lanes, :]
                vat_ref[0, 2 * p + e, HEAD_DIM:HEAD_DIM + VT_PAD, :] = ones_row

        def prep_qb(p):
            base = ha // 2 + kva
            qb_ref[:, p * PAIR:(p + 1) * PAIR] = (pair(base + p) * SCALE).astype(BF16)

        rows = pl.ds(pl.multiple_of(BLOCK + i * ts, BLOCK), ts)

        def prep_kvb(p):
            base = ha // 2 + kva + hb // 2
            k = pair(base + p).astype(BF16)
            v = pair(base + kvb // 2 + p).astype(BF16)
            kt, vt = transposed(k), transposed(v)
            for e, lanes in enumerate(heads):
                kb_ref[0, 2 * p + e, rows, :] = k[:, lanes]
                vb_ref[0, 2 * p + e, rows, :] = v[:, lanes]
                kbt_ref[0, 2 * p + e, :, rows] = kt[lanes, :]
                vbt_ref[0, 2 * p + e, 0:HEAD_DIM, rows] = vt[lanes, :]
                vbt_ref[0, 2 * p + e, HEAD_DIM:HEAD_DIM + VT_PAD, rows] = ones_row

        work = ([functools.partial(prep_qa, p) for p in range(ha // 2)]
                + [functools.partial(prep_kva, p) for p in range(kva // 2)]
                + [functools.partial(prep_qb, p) for p in range(hb // 2)]
                + [functools.partial(prep_kvb, p) for p in range(kvb // 2)])
        per_part = -(-len(work) // n_parts)
        for c in range(n_parts):
            for item in work[c * per_part:(c + 1) * per_part]:
                item()
            project(c)

    def cur(s):
        return jnp.minimum(s, nt - 1)

    def prev(s):
        return jnp.maximum(s - 1, 0) // ns, lax.rem(jnp.maximum(s - 1, 0), ns)

    def hm(nh):
        return pl.BlockSpec((1, nh, ts, HEAD_DIM), lambda s: (prev(s)[0], 0, prev(s)[1], 0))

    def hm_t(nh, rows):
        return pl.BlockSpec((1, nh, rows, ts), lambda s: (prev(s)[0], 0, 0, prev(s)[1]))

    def tokmajor(nh):
        return pl.BlockSpec((ts, nh * HEAD_DIM), lambda s: (jnp.maximum(s - 1, 0), 0))

    def padded(nh):
        return pl.BlockSpec((1, nh, sp, HEAD_DIM), lambda s: (prev(s)[0], 0, 0, 0))

    def padded_t(nh, rows):
        return pl.BlockSpec((1, nh, rows, sp), lambda s: (prev(s)[0], 0, 0, 0))

    tab = pl.BlockSpec((ts, PAIR), lambda s: (prev(s)[1], 0))
    vec = pl.BlockSpec((1, HEAD_DIM), lambda s: (0, 0))
    return pl.pallas_call(
        body,
        grid=(nt + 1,),
        in_specs=[pl.BlockSpec((ts, d), lambda s: (cur(s), 0)),
                  pl.BlockSpec((1, d), lambda s: (0, 0)),
                  pl.BlockSpec((p_cols, d), lambda s: (0, 0)),
                  tab, tab, vec, vec],
        out_specs=[pl.BlockSpec((ts, d), lambda s: (cur(s), 0)), tokmajor(p_cols // HEAD_DIM),
                   tokmajor(ha), hm(kva), hm_t(kva, HEAD_DIM), hm(kva), hm_t(kva, HEAD_DIM + VT_PAD),
                   tokmajor(hb), padded(kvb), padded_t(kvb, HEAD_DIM), padded(kvb),
                   padded_t(kvb, HEAD_DIM + VT_PAD)],
        out_shape=[SDS((t, d), BF16), SDS((t, p_cols), F32),
                   SDS((t, ha * HEAD_DIM), BF16), SDS((bl, kva, s_len, HEAD_DIM), BF16),
                   SDS((bl, kva, HEAD_DIM, s_len), BF16),
                   SDS((bl, kva, s_len, HEAD_DIM), BF16), SDS((bl, kva, HEAD_DIM + VT_PAD, s_len), BF16),
                   SDS((t, hb * HEAD_DIM), BF16),
                   SDS((bl, kvb, sp, HEAD_DIM), BF16), SDS((bl, kvb, HEAD_DIM, sp), BF16),
                   SDS((bl, kvb, sp, HEAD_DIM), BF16), SDS((bl, kvb, HEAD_DIM + VT_PAD, sp), BF16)],
        scratch_shapes=[pltpu.VMEM((ts, p_cols), F32)] * 2,
        compiler_params=_cp("arbitrary"),
        name="inproj_qkprep",
    )(x2, g1, w_t, cos, sin_signed, gq, gk)


def _attn_a_fwd(qa, ka, vat, tq, tk, shards):
    bl, kv, s_len, _ = ka.shape
    ha = qa.shape[1] // HEAD_DIM
    va_rows = vat.shape[2]
    nq, nk = s_len // tq, s_len // tk
    assert nk % 2 == 0
    r = GROUP * tq
    ns = len(shards)

    def body(q_ref, qn_ref, k_ref, v_ref, *rest):
        shard_refs, (o_ref, l_ref), gathered = rest[:ns], rest[ns:ns + 2], rest[ns + 2:2 * ns + 2]
        st_sc, send_sems, recv_sems, local_sems = rest[2 * ns + 2:]
        i = pl.program_id(2)
        step_id = (pl.program_id(0) * kv + pl.program_id(1)) * nq + i
        start, wait = _direct_exchange("gather", shard_refs, gathered, send_sems, recv_sems, local_sems)
        pl.when(step_id == 0)(start)

        q = _heads_t(q_ref[...]).astype(BF16)

        def scores(c, qv):
            return _dot(k_ref[0, 0, pl.ds(pl.multiple_of(c * tk, tk), tk), :], qv)

        def fold(st, c, carry):
            m_old, acc = carry
            m_new = jnp.maximum(m_old, jnp.max(st, axis=0, keepdims=True))
            pt = jnp.exp(st - m_new).astype(BF16)
            vt = v_ref[0, 0, :, pl.ds(pl.multiple_of(c * tk, tk), tk)]
            return m_new, jnp.exp(m_old - m_new) * acc + _dot(vt, pt)

        @pl.when(i == 0)
        def _():
            st_sc[0] = scores(0, q)

        def step(c2, carry):
            c = 2 * c2
            st_sc[1] = scores(c + 1, q)
            carry = fold(st_sc[0], c, carry)
            st_sc[0] = scores(c + 2, q)
            return fold(st_sc[1], c + 1, carry)

        carry = (jnp.full((1, r), -jnp.inf, F32), jnp.zeros((va_rows, r), F32))
        for c2 in range(nk // 2 - 1):
            carry = step(c2, carry)
        st_sc[1] = scores(nk - 1, q)
        carry = fold(st_sc[0], nk - 2, carry)
        st_sc[0] = scores(0, _heads_t(qn_ref[...]).astype(BF16))
        m, acc = fold(st_sc[1], nk - 1, carry)
        l = acc[HEAD_DIM:HEAD_DIM + 1, :]
        o_ref[...] = _heads_t_inv(acc[0:HEAD_DIM, :] / l).astype(BF16)
        l_ref[0, 0, 0] = jnp.broadcast_to(m + jnp.log(l), (8, r))
        pl.when(step_id == bl * kv * nq - 1)(wait)

    anyspec = pl.BlockSpec(memory_space=pl.ANY)
    res = pl.pallas_call(
        body,
        grid=(bl, kv, nq),
        in_specs=[pl.BlockSpec((tq, GROUP * HEAD_DIM), lambda b, g, i: (b * nq + i, g)),
                  pl.BlockSpec((tq, GROUP * HEAD_DIM), lambda b, g, i: (b * nq + jnp.minimum(i + 1, nq - 1), g)),
                  pl.BlockSpec((1, 1, s_len, HEAD_DIM), lambda b, g, i: (b, g, 0, 0)),
                  pl.BlockSpec((1, 1, va_rows, s_len), lambda b, g, i: (b, g, 0, 0))] + [anyspec] * ns,
        out_specs=[pl.BlockSpec((tq, GROUP * HEAD_DIM), lambda b, g, i: (b * nq + i, g)),
                   pl.BlockSpec((1, 1, 1, 8, r), lambda b, g, i: (b, g, i, 0, 0))] + [anyspec] * ns,
        out_shape=[SDS((bl * s_len, ha * HEAD_DIM), BF16), SDS((bl, kv, nq, 8, r), F32)]
        + [SDS((N_DEV,) + s.shape, s.dtype) for s in shards],
        scratch_shapes=[pltpu.VMEM((2, tk, r), F32)] + _exchange_scratch(ns),
        compiler_params=_cp("arbitrary", "arbitrary", "arbitrary"),
        name="attn_a_fwd",
    )(qa, qa, ka, vat, *shards)
    return res[0], res[1], res[2:]


FFN_BWD_TOKENS = 256
QB_PER_STEP = 16


def _bias_variant(n, nb):
    return jnp.where(n == 0, 1, jnp.where(n == nb - 1, 2, 0))


def _sink_row(sink_ref, g):
    return jnp.concatenate([jnp.full((1, BLOCK), sink_ref[0, g * GROUP + h], F32) for h in range(GROUP)], axis=1)


def _attn_b_fwd(qb, kb, vbt, bias_t, sink, s_len):
    bl, kv, sp, _ = kb.shape
    hb = qb.shape[1] // HEAD_DIM
    vt_rows = vbt.shape[2]
    nb = s_len // BLOCK
    nbs = min(QB_PER_STEP, nb)
    r = GROUP * BLOCK

    def body(q_ref, k_ref, vt_ref, bt_ref, sink_ref, o_ref, l_ref, st_sc, pb_sc):
        g, n0 = pl.program_id(1), pl.program_id(2) * nbs
        sink_row = _sink_row(sink_ref, g)

        def span(j):
            return pl.ds(pl.multiple_of((n0 + j) * BLOCK, BLOCK), SPAN)

        for j in range(nbs):
            qt = _heads_t(q_ref[j * BLOCK:(j + 1) * BLOCK, :]).astype(BF16)
            st_sc[j] = _dot(k_ref[0, 0, span(j), :], qt) + bt_ref[_bias_variant(n0 + j, nb), 0]
        maxes = []
        for j in range(nbs):
            st = st_sc[j]
            m = jnp.maximum(jnp.max(st, axis=0, keepdims=True), sink_row)
            pb_sc[j] = jnp.exp(st - m).astype(BF16)
            maxes.append(m)
        for j in range(nbs):
            m = maxes[j]
            acc = _dot(vt_ref[0, 0, :, span(j)], pb_sc[j])
            l = acc[HEAD_DIM:HEAD_DIM + 1, :] + jnp.exp(sink_row - m)
            o_ref[j * BLOCK:(j + 1) * BLOCK, :] = _heads_t_inv(acc[0:HEAD_DIM, :] / l).astype(BF16)
            l_ref[0, 0, j] = jnp.broadcast_to(m + jnp.log(l), (8, r))

    return pl.pallas_call(
        body,
        grid=(bl, kv, nb // nbs),
        in_specs=[pl.BlockSpec((nbs * BLOCK, GROUP * HEAD_DIM), lambda b, g, n: (b * (nb // nbs) + n, g)),
                  pl.BlockSpec((1, 1, sp, HEAD_DIM), lambda b, g, n: (b, g, 0, 0)),
                  pl.BlockSpec((1, 1, vt_rows, sp), lambda b, g, n: (b, g, 0, 0)),
                  pl.BlockSpec((3, 1, SPAN, r), lambda b, g, n: (0, g, 0, 0)),
                  pl.BlockSpec(memory_space=pltpu.SMEM)],
        out_specs=[pl.BlockSpec((nbs * BLOCK, GROUP * HEAD_DIM), lambda b, g, n: (b * (nb // nbs) + n, g)),
                   pl.BlockSpec((1, 1, nbs, 8, r), lambda b, g, n: (b, g, n, 0, 0))],
        out_shape=[SDS((bl * s_len, hb * HEAD_DIM), BF16), SDS((bl, kv, nb, 8, r), F32)],
        scratch_shapes=[pltpu.VMEM((nbs, SPAN, r), F32), pltpu.VMEM((nbs, SPAN, r), BF16)],
        compiler_params=_cp("parallel", "parallel", "arbitrary"),
        name="attn_b_fwd",
    )(qb, kb, vbt, bias_t, sink)


def _mixout(oa, ob, wo, x2, g2, g3, tm):
    t, d = x2.shape
    ca = oa.shape[1]

    def body(oa_ref, ob_ref, w_ref, x_ref, g2_ref, g3_ref, mix_ref, x1_ref, h2_ref):
        mix = _dot(oa_ref[...], w_ref[0:ca, :]) + _dot(ob_ref[...], w_ref[ca:, :])
        mix_ref[...] = mix
        y2, _, _ = _rms_fwd(mix, g2_ref[...])
        x1 = x_ref[...] + y2
        x1_ref[...] = x1
        y3, _, _ = _rms_fwd(x1, g3_ref[...])
        h2_ref[...] = y3.astype(BF16)

    tile = lambda w: pl.BlockSpec((tm, w), lambda i: (i, 0))
    vec = pl.BlockSpec((1, d), lambda i: (0, 0))
    return pl.pallas_call(
        body,
        grid=(t // tm,),
        in_specs=[tile(ca), tile(ob.shape[1]), pl.BlockSpec(wo.shape, lambda i: (0, 0)), tile(d), vec, vec],
        out_specs=[tile(d), tile(d), tile(d)],
        out_shape=[SDS((t, d), F32), SDS((t, d), F32), SDS((t, d), BF16)],
        compiler_params=_cp("parallel"),
        name="mixout",
    )(oa, ob, wo, x2, g2, g3)


def _ffn_fwd(h2, wup_g, wdn, x1, target, g4, tm):
    t, d = x1.shape
    nblk, _, tf = wup_g.shape
    ff = nblk * tf
    nt = t // tm

    def body(h_ref, wu_hbm, wd_hbm, x1_ref, tg_ref, g_ref, u_ref, df_ref, dy_ref, dg_ref, loss_ref,
             f_sc, wu_sc, wd_sc, sems):
        s = pl.program_id(0)
        loads = [pltpu.make_async_copy(wu_hbm.at[c], wu_sc.at[c], sems.at[c]) for c in range(nblk)]
        loads.append(pltpu.make_async_copy(wd_hbm, wd_sc, sems.at[nblk]))

        def after_norm():
            g = g_ref[...]
            y4, n, r = _rms_fwd(f_sc[...], g)
            e = (x1_ref[...] + y4) - tg_ref[...]
            loss_ref[...] += jnp.sum(e * e) * (0.5 / d)
            dy = e * (1.0 / d)
            dy_ref[...] = dy
            yield
            df, dgt = _rms_bwd(n, r, g, dy)
            df_ref[...] = df.astype(BF16)
            dg_ref[0:1, :] += jnp.sum(dgt, axis=0, keepdims=True)
            yield

        def tile(first):
            pending = iter(()) if first else after_norm()
            h = h_ref[...]
            squares = []
            for c in range(nblk):
                if first:
                    loads[c].wait()
                u = jnp.maximum(_dot(h, wu_sc[c]), 0.0)
                u_ref[:, c * tf:(c + 1) * tf] = u.astype(BF16)
                squares.append((u * u).astype(BF16))
                next(pending, None)
            if first:
                loads[nblk].wait()
            f_sc[...] = _dot(jnp.concatenate(squares, axis=1), wd_sc[...])

        @pl.when(s == 0)
        def _():
            for cp in loads:
                cp.start()
            dg_ref[...] = jnp.zeros_like(dg_ref)
            loss_ref[...] = jnp.zeros_like(loss_ref)
            tile(True)

        pl.when((s > 0) & (s < nt))(functools.partial(tile, False))

        @pl.when(s == nt)
        def _():
            for _ in after_norm():
                pass

    cur = lambda s: (jnp.minimum(s, nt - 1), 0)
    prev = lambda s: (jnp.maximum(s - 1, 0), 0)
    anyspec = pl.BlockSpec(memory_space=pl.ANY)
    return pl.pallas_call(
        body,
        grid=(nt + 1,),
        in_specs=[pl.BlockSpec((tm, d), cur), anyspec, anyspec,
                  pl.BlockSpec((tm, d), prev), pl.BlockSpec((tm, d), prev),
                  pl.BlockSpec((1, d), lambda s: (0, 0))],
        out_specs=[pl.BlockSpec((tm, ff), cur), pl.BlockSpec((tm, d), prev), pl.BlockSpec((tm, d), prev),
                   pl.BlockSpec((8, d), lambda s: (0, 0)),
                   pl.BlockSpec((8, 128), lambda s: (0, 0))],
        out_shape=[SDS((t, ff), BF16), SDS((t, d), BF16), SDS((t, d), F32), SDS((8, d), F32), SDS((8, 128), F32)],
        scratch_shapes=[pltpu.VMEM((tm, d), F32), pltpu.VMEM((nblk, d, tf), BF16), pltpu.VMEM((ff, d), BF16),
                        pltpu.SemaphoreType.DMA((nblk + 1,))],
        compiler_params=_cp("arbitrary"),
        name="ffn_fwd",
    )(h2, wup_g, wdn, x1, target, g4)


def _ffn_bwd(df, u, wdn, wup_g, x1, dy, mix, g3, g2, tm):
    t, d = x1.shape
    nblk, _, tf = wup_g.shape
    ff = nblk * tf
    nt = t // tm

    def body(df_ref, u_ref, wd_hbm, wu_hbm, x1_ref, dy_ref, mix_ref, g3_ref, g2_ref,
             dpre_ref, dx1_ref, dmix_ref, dg3_ref, dg2_ref, dh_sc, wd_sc, wu_sc, sems):
        s = pl.program_id(0)
        blocks = [slice(c * tf, (c + 1) * tf) for c in range(nblk)]
        wd_loads = [pltpu.make_async_copy(wd_hbm.at[blocks[c], :], wd_sc.at[blocks[c], :], sems.at[c])
                    for c in range(nblk)]
        wu_loads = [pltpu.make_async_copy(wu_hbm.at[c], wu_sc.at[c], sems.at[nblk + c]) for c in range(nblk)]

        def norms_bwd():
            g3 = g3_ref[...]
            _, n3, r3 = _rms_fwd(x1_ref[...], g3)
            dx, dgt3 = _rms_bwd(n3, r3, g3, dh_sc[...])
            dx1 = dy_ref[...] + dx
            dx1_ref[...] = dx1
            dg3_ref[0:1, :] += jnp.sum(dgt3, axis=0, keepdims=True)
            yield
            g2 = g2_ref[...]
            _, n2, r2 = _rms_fwd(mix_ref[...], g2)
            dmix, dgt2 = _rms_bwd(n2, r2, g2, dx1_ref[...])
            dmix_ref[...] = dmix.astype(BF16)
            dg2_ref[0:1, :] += jnp.sum(dgt2, axis=0, keepdims=True)
            yield

        def tile(first):
            pending = iter(()) if first else norms_bwd()
            if first:
                parts = []
                for c in range(nblk):
                    wd_loads[c].wait()
                    du2 = _dot_nt(df_ref[...], wd_sc[blocks[c], :])
                    parts.append((2.0 * u_ref[:, blocks[c]].astype(F32) * du2).astype(BF16))
                dpre = jnp.concatenate(parts, axis=1)
            else:
                du2 = _dot_nt(df_ref[...], wd_sc[...])
                dpre = (2.0 * u_ref[...].astype(F32) * du2).astype(BF16)
            dpre_ref[...] = dpre
            dh = None
            for c in range(nblk):
                if first:
                    wu_loads[c].wait()
                if c in (1, 3):
                    next(pending, None)
                part = _dot_nt(dpre[:, blocks[c]], wu_sc[c])
                dh = part if dh is None else dh + part
            dh_sc[...] = dh

        @pl.when(s == 0)
        def _():
            for cp in wd_loads + wu_loads:
                cp.start()
            dg3_ref[...] = jnp.zeros_like(dg3_ref)
            dg2_ref[...] = jnp.zeros_like(dg2_ref)
            tile(True)

        pl.when((s > 0) & (s < nt))(functools.partial(tile, False))

        @pl.when(s == nt)
        def _():
            for _ in norms_bwd():
                pass

    cur = lambda s: (jnp.minimum(s, nt - 1), 0)
    prev = lambda s: (jnp.maximum(s - 1, 0), 0)
    vec = pl.BlockSpec((1, d), lambda s: (0, 0))
    acc8 = pl.BlockSpec((8, d), lambda s: (0, 0))
    anyspec = pl.BlockSpec(memory_space=pl.ANY)
    return pl.pallas_call(
        body,
        grid=(nt + 1,),
        in_specs=[pl.BlockSpec((tm, d), cur),
                  pl.BlockSpec((tm, ff), cur), anyspec, anyspec,
                  pl.BlockSpec((tm, d), prev), pl.BlockSpec((tm, d), prev), pl.BlockSpec((tm, d), prev), vec, vec],
        out_specs=[pl.BlockSpec((tm, ff), cur), pl.BlockSpec((tm, d), prev), pl.BlockSpec((tm, d), prev),
                   acc8, acc8],
        out_shape=[SDS(u.shape, BF16), SDS((t, d), F32), SDS((t, d), BF16), SDS((8, d), F32), SDS((8, d), F32)],
        scratch_shapes=[pltpu.VMEM((tm, d), F32), pltpu.VMEM((ff, d), BF16), pltpu.VMEM((nblk, d, tf), BF16),
                        pltpu.SemaphoreType.DMA((2 * nblk,))],
        compiler_params=_cp("arbitrary"),
        name="ffn_bwd",
    )(df, u, wdn, wup_g, x1, dy, mix, g3, g2)


WGRAD_FIRST_CHUNKS = 8


def _wgrad(a, b, a_cols, b_cols, out_block, out_shape, nj, nk, tt, name, prep_a=None):
    acc_shape = out_block[1:]
    wa, wb = a_cols(0)[1], b_cols(0)[1]
    nc = WGRAD_FIRST_CHUNKS
    ch = tt // nc

    def body(a_hbm, b_hbm, o_ref, acc_sc, a_sc, b_sc, sems, first_sems):
        j, k = pl.program_id(0), pl.program_id(1)
        s = j * nk + k

        def aligned(v, m):
            return v if isinstance(v, int) else pl.multiple_of(v, m)

        def fetch(jj, kk, slot, rows=None, sem=None):
            r0, n = (0, tt) if rows is None else rows
            out = []
            for x, (hbm, sc, cols) in enumerate(((a_hbm, a_sc, a_cols), (b_hbm, b_sc, b_cols))):
                c0, w = cols(jj)
                src = hbm.at[pl.ds(aligned(kk * tt + r0, ch), n), pl.ds(aligned(c0, 128), w)]
                out.append(pltpu.make_async_copy(src, sc.at[slot, pl.ds(r0, n), :],
                                                 (sems.at[x, slot] if sem is None else sem.at[x])))
            return out

        def prepped(v):
            return v if prep_a is None else prep_a(v)

        def accumulate(part):
            @pl.when(k == 0)
            def _():
                acc_sc[...] = part

            @pl.when(k > 0)
            def _():
                acc_sc[...] += part

            @pl.when(k == nk - 1)
            def _():
                o_ref[0] = acc_sc[...].astype(BF16)

        def start_next(slot):
            nxt = s + 1

            @pl.when(nxt < nj * nk)
            def _():
                for cp in fetch(nxt // nk, lax.rem(nxt, nk), slot):
                    cp.start()

        @pl.when(s == 0)
        def _():
            chunks = [fetch(0, 0, 0, rows=(c * ch, ch), sem=first_sems.at[c]) for c in range(nc)]
            for pair in chunks:
                for cp in pair:
                    cp.start()
            start_next(1)
            part = None
            for c, pair in enumerate(chunks):
                for cp in pair:
                    cp.wait()
                rows = slice(c * ch, (c + 1) * ch)
                piece = _dot_tn(prepped(a_sc[0, rows, :]), b_sc[0, rows, :])
                part = piece if part is None else part + piece
            accumulate(part)

        def steady(slot):
            start_next(1 - slot)
            for cp in fetch(j, k, slot):
                cp.wait()
            accumulate(_dot_tn(prepped(a_sc[slot]), b_sc[slot]))

        even = lax.rem(s, 2) == 0
        pl.when((s > 0) & even)(functools.partial(steady, 0))
        pl.when(jnp.logical_not(even))(functools.partial(steady, 1))

    anyspec = pl.BlockSpec(memory_space=pl.ANY)
    return pl.pallas_call(
        body,
        grid=(nj, nk),
        in_specs=[anyspec, anyspec],
        out_specs=pl.BlockSpec(out_block, lambda j, k: (j, 0, 0)),
        out_shape=SDS(out_shape, BF16),
        scratch_shapes=[pltpu.VMEM(acc_shape, F32), pltpu.VMEM((2, tt, wa), a.dtype), pltpu.VMEM((2, tt, wb), b.dtype),
                        pltpu.SemaphoreType.DMA((2, 2)), pltpu.SemaphoreType.DMA((nc, 2))],
        compiler_params=_cp("arbitrary", "arbitrary"),
        name=name,
    )(a, b)


def _wgrad_cols(a, b, nj, tt, name):
    t, m = a.shape
    bn = b.shape[1] // nj
    return _wgrad(a, b, lambda j: (0, m), lambda j: (j * bn, bn), (1, m, bn), (nj, m, bn), nj, t // tt, tt, name)


def _wgrad_rows(a, b, nj, tt, name, square=False):
    t, n = b.shape
    bm = a.shape[1] // nj

    def squared(av):
        af = av.astype(F32)
        return (af * af).astype(BF16)

    return _wgrad(a, b, lambda j: (j * bm, bm), lambda j: (0, n), (1, bm, n), (nj, bm, n), nj, t // tt, tt, name,
                  prep_a=squared if square else None)


def _wgrad_o(oa, ob, dmix, nj, tt):
    t, n = dmix.shape
    ca, cb = oa.shape[1], ob.shape[1]
    m = ca + cb
    nk = t // tt

    def body(oa_ref, ob_ref, b_ref, o_ref, acc_sc):
        k = pl.program_id(0)
        part = _dot_tn(jnp.concatenate([oa_ref[...], ob_ref[...]], axis=1), b_ref[...])

        @pl.when(k == 0)
        def _():
            acc_sc[...] = part

        @pl.when(k > 0)
        def _():
            acc_sc[...] += part

        @pl.when(k == nk - 1)
        def _():
            o_ref[...] = acc_sc[...].reshape(nj, m // nj, n).astype(BF16)

    return pl.pallas_call(
        body,
        grid=(nk,),
        in_specs=[pl.BlockSpec((tt, ca), lambda k: (k, 0)), pl.BlockSpec((tt, cb), lambda k: (k, 0)),
                  pl.BlockSpec((tt, n), lambda k: (k, 0))],
        out_specs=pl.BlockSpec((nj, m // nj, n), lambda k: (0, 0, 0)),
        out_shape=SDS((nj, m // nj, n), BF16),
        scratch_shapes=[pltpu.VMEM((m, n), F32)],
        compiler_params=_cp("arbitrary"),
        name="wgrad_o",
    )(oa, ob, dmix)


def _attn_out_bwd(dmix, wo, ca, tm):
    t, d = dmix.shape
    cb = wo.shape[0] - ca

    def body(dm_ref, w_ref, da_ref, db_ref):
        dm = dm_ref[...]
        da_ref[...] = _dot_nt(dm, w_ref[0:ca, :]).astype(BF16)
        db_ref[...] = _dot_nt(dm, w_ref[ca:, :]).astype(BF16)

    return pl.pallas_call(
        body,
        grid=(t // tm,),
        in_specs=[pl.BlockSpec((tm, d), lambda i: (i, 0)), pl.BlockSpec(wo.shape, lambda i: (0, 0))],
        out_specs=[pl.BlockSpec((tm, ca), lambda i: (i, 0)), pl.BlockSpec((tm, cb), lambda i: (i, 0))],
        out_shape=[SDS((t, ca), BF16), SDS((t, cb), BF16)],
        compiler_params=_cp("parallel"),
        name="attn_out_bwd",
    )(dmix, wo)


def _heads_t(x):
    xt = x.astype(F32).T
    return jnp.concatenate([xt[h * HEAD_DIM:(h + 1) * HEAD_DIM, :] for h in range(GROUP)], axis=1)


def _heads_t_inv(yt):
    n = yt.shape[1] // GROUP
    return jnp.concatenate([yt[:, h * n:(h + 1) * n] for h in range(GROUP)], axis=0).T


def _attn_a_bwd(qa, ka, kat, va, do, o, lse, tq, tk, grads):
    bl, kv, s_len, _ = ka.shape
    nq, nk = s_len // tq, s_len // tk
    assert nk % 2 == 0
    r = GROUP * tq
    ng = len(grads)

    def body(q_ref, qn_ref, k_ref, kt_ref, v_ref, do_ref, don_ref, o_ref, l_ref, *rest):
        grad_refs, (dq_ref, dk_ref, dv_ref), parts = rest[:ng], rest[ng:ng + 3], rest[ng + 3:2 * ng + 3]
        st_sc, dp_sc, dkt_sc, dvt_sc, send_sems, recv_sems, local_sems = rest[2 * ng + 3:]
        i = pl.program_id(2)
        step_id = (pl.program_id(0) * kv + pl.program_id(1)) * nq + i
        start, wait = _direct_exchange("scatter", grad_refs, parts, send_sems, recv_sems, local_sems)
        pl.when(step_id == 0)(start)

        dot32 = _heads_t(do_ref[...])
        drow = jnp.sum(dot32 * _heads_t(o_ref[...]), axis=0, keepdims=True)
        qt, dot = _heads_t(q_ref[...]).astype(BF16), dot32.astype(BF16)
        lrow = l_ref[0, 0, 0, 0:1, :]

        @pl.when(i == 0)
        def _():
            dkt_sc[...] = jnp.zeros_like(dkt_sc)
            dvt_sc[...] = jnp.zeros_like(dvt_sc)

        def chunk(c):
            return pl.ds(pl.multiple_of(c * tk, tk), tk)

        def scores(c, slot, qv=qt, dov=dot):
            st_sc[slot] = _dot(k_ref[0, 0, chunk(c), :], qv)
            dp_sc[slot] = _dot(v_ref[0, 0, chunk(c), :], dov)

        def fold(slot, c, dqt):
            pt = jnp.exp(st_sc[slot] - lrow)
            dsb = (pt * (dp_sc[slot] - drow)).astype(BF16)
            dvt_sc[:, chunk(c)] += _dot_nt(dot, pt.astype(BF16))
            dkt_sc[:, chunk(c)] += _dot_nt(qt, dsb)
            return dqt + _dot(kt_ref[0, 0, :, chunk(c)], dsb)

        @pl.when(i == 0)
        def _():
            scores(0, 0)

        def step(c2, dqt):
            c = 2 * c2
            scores(c + 1, 1)
            dqt = fold(0, c, dqt)
            scores(c + 2, 0)
            return fold(1, c + 1, dqt)

        dqt = jnp.zeros((HEAD_DIM, r), F32)
        for c2 in range(nk // 2 - 1):
            dqt = step(c2, dqt)
        scores(nk - 1, 1)
        dqt = fold(0, nk - 2, dqt)
        scores(0, 0, _heads_t(qn_ref[...]).astype(BF16), _heads_t(don_ref[...]).astype(BF16))
        dq_ref[...] = _heads_t_inv(fold(1, nk - 1, dqt))

        @pl.when(i == nq - 1)
        def _():
            dk_ref[0, 0] = dkt_sc[...].T
            dv_ref[0, 0] = dvt_sc[...].T

        pl.when(step_id == bl * kv * nq - 1)(wait)

    kvspec = pl.BlockSpec((1, 1, s_len, HEAD_DIM), lambda b, g, i: (b, g, 0, 0))
    tok = pl.BlockSpec((tq, GROUP * HEAD_DIM), lambda b, g, i: (b * nq + i, g))
    toknext = pl.BlockSpec((tq, GROUP * HEAD_DIM), lambda b, g, i: (b * nq + jnp.minimum(i + 1, nq - 1), g))
    anyspec = pl.BlockSpec(memory_space=pl.ANY)
    res = pl.pallas_call(
        body,
        grid=(bl, kv, nq),
        in_specs=[tok, toknext, kvspec, pl.BlockSpec((1, 1, HEAD_DIM, s_len), lambda b, g, i: (b, g, 0, 0)), kvspec,
                  tok, toknext, tok, pl.BlockSpec((1, 1, 1, 8, r), lambda b, g, i: (b, g, i, 0, 0))] + [anyspec] * ng,
        out_specs=[tok, kvspec, kvspec] + [anyspec] * ng,
        out_shape=[SDS(qa.shape, F32), SDS(ka.shape, F32), SDS(va.shape, F32)]
        + [SDS(g.shape, g.dtype) for g in grads],
        scratch_shapes=[pltpu.VMEM((2, tk, r), F32), pltpu.VMEM((2, tk, r), F32),
                        pltpu.VMEM((HEAD_DIM, s_len), F32), pltpu.VMEM((HEAD_DIM, s_len), F32)]
        + _exchange_scratch(ng),
        compiler_params=_cp("arbitrary", "arbitrary", "arbitrary"),
        name="attn_a_bwd",
    )(qa, qa, ka, kat, va, do, do, o, lse, *grads)
    return res[0], res[1], res[2], res[3:]


def _attn_b_bwd(qb, kb, kbt, vb, do, o, lse, bias_t, sink, s_len):
    bl, kv, sp, _ = kb.shape
    nb = s_len // BLOCK
    nbs = min(QB_PER_STEP, nb)
    r = GROUP * BLOCK

    def body(q_ref, k_ref, kt_ref, v_ref, do_ref, o_ref, l_ref, bt_ref, sink_ref,
             dq_ref, dk_ref, dv_ref, dsum_ref, dsink_ref, dkt_sc, dvt_sc):
        g, b, ns = pl.program_id(0), pl.program_id(1), pl.program_id(2)
        sink_row = _sink_row(sink_ref, g)

        @pl.when(ns == 0)
        def _():
            dkt_sc[...] = jnp.zeros_like(dkt_sc)
            dvt_sc[...] = jnp.zeros_like(dvt_sc)

        @pl.when((b == 0) & (ns == 0))
        def _():
            dsum_ref[...] = jnp.zeros_like(dsum_ref)
            dsink_ref[...] = jnp.zeros_like(dsink_ref)

        dsum = jnp.zeros((SPAN, r), F32)
        dsink = jnp.zeros((1, r), F32)
        for j in range(nbs):
            n = ns * nbs + j
            span = pl.ds(pl.multiple_of(n * BLOCK, BLOCK), SPAN)
            rows = slice(j * BLOCK, (j + 1) * BLOCK)
            dot32 = _heads_t(do_ref[rows, :])
            drow = jnp.sum(dot32 * _heads_t(o_ref[rows, :]), axis=0, keepdims=True)
            qt, dot = _heads_t(q_ref[rows, :]).astype(BF16), dot32.astype(BF16)
            lrow = l_ref[0, 0, j, 0:1, :]
            st = _dot(k_ref[0, 0, span, :], qt) + bt_ref[_bias_variant(n, nb), 0]
            pt = jnp.exp(st - lrow)
            dst = pt * (_dot(v_ref[0, 0, span, :], dot) - drow)
            dsum = dsum + dst
            dsink = dsink - jnp.exp(sink_row - lrow) * drow
            dsb = dst.astype(BF16)
            dvt_sc[:, span] += _dot_nt(dot, pt.astype(BF16))
            dkt_sc[:, span] += _dot_nt(qt, dsb)
            dq_ref[rows, :] = _heads_t_inv(_dot(kt_ref[0, 0, :, span], dsb))
        dsum_ref[0] += dsum
        dsink_ref[0, 0:1, :] += dsink

        @pl.when(ns == nb // nbs - 1)
        def _():
            dk_ref[0, 0] = dkt_sc[:, BLOCK:BLOCK + s_len].T
            dv_ref[0, 0] = dvt_sc[:, BLOCK:BLOCK + s_len].T

    kvspec = pl.BlockSpec((1, 1, sp, HEAD_DIM), lambda g, b, n: (b, g, 0, 0))
    kvout = pl.BlockSpec((1, 1, s_len, HEAD_DIM), lambda g, b, n: (b, g, 0, 0))
    tok = pl.BlockSpec((nbs * BLOCK, GROUP * HEAD_DIM), lambda g, b, n: (b * (nb // nbs) + n, g))
    return pl.pallas_call(
        body,
        grid=(kv, bl, nb // nbs),
        in_specs=[tok, kvspec, pl.BlockSpec((1, 1, HEAD_DIM, sp), lambda g, b, n: (b, g, 0, 0)), kvspec, tok, tok,
                  pl.BlockSpec((1, 1, nbs, 8, r), lambda g, b, n: (b, g, n, 0, 0)),
                  pl.BlockSpec((3, 1, SPAN, r), lambda g, b, n: (0, g, 0, 0)),
                  pl.BlockSpec(memory_space=pltpu.SMEM)],
        out_specs=[tok, kvout, kvout,
                   pl.BlockSpec((1, SPAN, r), lambda g, b, n: (g, 0, 0)),
                   pl.BlockSpec((1, 8, r), lambda g, b, n: (g, 0, 0))],
        out_shape=[SDS(qb.shape, F32), SDS((bl, kv, s_len, HEAD_DIM), F32), SDS((bl, kv, s_len, HEAD_DIM), F32),
                   SDS((kv, SPAN, r), F32), SDS((kv, 8, r), F32)],
        scratch_shapes=[pltpu.VMEM((HEAD_DIM, sp), F32), pltpu.VMEM((HEAD_DIM, sp), F32)],
        compiler_params=_cp("arbitrary", "arbitrary", "arbitrary"),
        name="attn_b_bwd",
    )(qb, kb, kbt, vb, do, o, lse, bias_t, sink)


def _bias_reduce(dsum, dsink, bucket_t):
    kv, _, r = dsum.shape

    def body(ds_ref, dk_ref, bk_ref, rel_ref, sink_ref):
        lane = lax.broadcasted_iota(jnp.int32, (N_BUCKETS, 128), 1)
        lane8 = lax.broadcasted_iota(jnp.int32, (8, 128), 1)
        bk = jnp.concatenate([bk_ref[...]] * GROUP, axis=1)
        for g in range(kv):
            ds = ds_ref[g]
            rowi = lax.broadcasted_iota(jnp.int32, (N_BUCKETS, r), 0)
            red = jnp.zeros((N_BUCKETS, r), F32)
            for b in range(N_BUCKETS):
                red = jnp.where(rowi == b, jnp.sum(jnp.where(bk == b, ds, 0.0), axis=0, keepdims=True), red)
            out = jnp.zeros((N_BUCKETS, 128), F32)
            so = jnp.zeros((8, 128), F32)
            for h in range(GROUP):
                col = jnp.sum(red[:, h * BLOCK:(h + 1) * BLOCK], axis=1, keepdims=True)
                out = jnp.where(lane == h, col, out)
                sc = jnp.sum(dk_ref[g][:, h * BLOCK:(h + 1) * BLOCK], axis=1, keepdims=True)
                so = jnp.where(lane8 == h, sc, so)
            rel_ref[g] = out
            sink_ref[g] = so

    vm = pl.BlockSpec(memory_space=pltpu.VMEM)
    return pl.pallas_call(
        body,
        in_specs=[vm, vm, vm],
        out_specs=[vm, vm],
        out_shape=[SDS((kv, N_BUCKETS, 128), F32), SDS((kv, 8, 128), F32)],
        name="bias_reduce",
    )(dsum, dsink, bucket_t)


def _dqkprep(dqa, dka, dva, dqb, dkb, dvb, proj, h1, cos, sin_signed, gq, gk, s_len, ts):
    t, p_cols = proj.shape
    d = h1.shape[1]
    bl, kva, kvb = dka.shape[0], dka.shape[1], dkb.shape[1]
    ha, hb = dqa.shape[1] // HEAD_DIM, dqb.shape[1] // HEAD_DIM
    ns = s_len // ts

    def body(dqa_ref, dka_ref, dva_ref, dqb_ref, dkb_ref, dvb_ref, p_ref, h1_ref, h1p_ref, cos_ref, sin_ref,
             gq_ref, gk_ref, dp_ref, dgq_ref, dgk_ref, gw_ref, gw_sc, dpp_sc):
        b, i = pl.program_id(0), pl.program_id(1)
        cs, sn = cos_ref[...], sin_ref[...]
        low, first = _pair_masks(ts)

        @pl.when((b == 0) & (i == 0))
        def _():
            dgq_ref[...] = jnp.zeros_like(dgq_ref)
            dgk_ref[...] = jnp.zeros_like(dgk_ref)
            gw_sc[...] = jnp.zeros_like(gw_sc)
            dpp_sc[...] = jnp.zeros_like(dpp_sc)

        n_parts = 6
        pw = p_cols // n_parts

        def wgrad_part(c):
            rows = slice(c * pw, (c + 1) * pw)
            gw_sc[rows, :] += _dot_tn(dpp_sc[:, rows], h1p_ref[...])

        def grad_pair(ref, p):
            return jnp.concatenate([ref[0, 2 * p], ref[0, 2 * p + 1]], axis=1)

        def put(p, val):
            dp_ref[:, p * PAIR:(p + 1) * PAIR] = val.astype(BF16)

        def unrope_norm(d_rot, p, g, dg_ref):
            dn = d_rot * cs + _pair_partner(d_rot * sn, first)
            xp = p_ref[:, p * PAIR:(p + 1) * PAIR]
            r = lax.rsqrt(_pair_mean(xp * xp, low) + EPS)
            n = xp * r
            gd = g * dn
            dg_ref[0:1, :] += jnp.sum(dn * n, axis=0, keepdims=True)
            put(p, r * (gd - n * _pair_mean(n * gd, low)))

        parts = iter(range(n_parts))

        def next_wgrad_part():
            c = next(parts, None)
            if c is not None:
                wgrad_part(c)

        for p in range(ha // 2):
            next_wgrad_part()
            unrope_norm(dqa_ref[:, p * PAIR:(p + 1) * PAIR] * SCALE, p, _for_pair(gq_ref), dgq_ref)
        base = ha // 2
        for p in range(kva // 2):
            next_wgrad_part()
            unrope_norm(grad_pair(dka_ref, p), base + p, _for_pair(gk_ref), dgk_ref)
            put(base + kva // 2 + p, grad_pair(dva_ref, p))
        base += kva
        for p in range(hb // 2):
            put(base + p, dqb_ref[:, p * PAIR:(p + 1) * PAIR] * SCALE)
        base += hb // 2
        for p in range(kvb // 2):
            put(base + p, grad_pair(dkb_ref, p))
            put(base + kvb // 2 + p, grad_pair(dvb_ref, p))
        for c in parts:
            wgrad_part(c)

        dpp_sc[...] = dp_ref[...]

        @pl.when((b == bl - 1) & (i == ns - 1))
        def _():
            gw_ref[...] = (gw_sc[...] + _dot_tn(dp_ref[...], h1_ref[...])).astype(BF16)

    def hm(nh):
        return pl.BlockSpec((1, nh, ts, HEAD_DIM), lambda b, i: (b, 0, i, 0))

    def tokmajor(nh):
        return pl.BlockSpec((ts, nh * HEAD_DIM), lambda b, i: (b * ns + i, 0))

    vec = pl.BlockSpec((1, HEAD_DIM), lambda b, i: (0, 0))
    tab = pl.BlockSpec((ts, PAIR), lambda b, i: (i, 0))
    acc = pl.BlockSpec((8, PAIR), lambda b, i: (0, 0))
    pspec = pl.BlockSpec((ts, p_cols), lambda b, i: (b * ns + i, 0))
    return pl.pallas_call(
        body,
        grid=(bl, ns),
        in_specs=[tokmajor(ha), hm(kva), hm(kva), tokmajor(hb), hm(kvb), hm(kvb), pspec,
                  pl.BlockSpec((ts, d), lambda b, i: (b * ns + i, 0)),
                  pl.BlockSpec((ts, d), lambda b, i: (jnp.maximum(b * ns + i - 1, 0), 0)), tab, tab, vec, vec],
        out_specs=[pspec, acc, acc, pl.BlockSpec((p_cols, d), lambda b, i: (0, 0))],
        out_shape=[SDS((t, p_cols), BF16), SDS((8, PAIR), F32), SDS((8, PAIR), F32), SDS((p_cols, d), BF16)],
        scratch_shapes=[pltpu.VMEM((p_cols, d), F32), pltpu.VMEM((ts, p_cols), BF16)],
        compiler_params=_cp("arbitrary", "arbitrary"),
        name="dqkprep",
    )(dqa, dka, dva, dqb, dkb, dvb, proj, h1, h1, cos, sin_signed, gq, gk)


def _dx_final(dproj, w_t, x2, dx1, g1, tm, grads, small):
    t, d = x2.shape
    p_cols = w_t.shape[0]
    ng, nsm = len(grads), len(small)
    nsteps = t // tm

    def body(dp_ref, w_ref, x_ref, dx1_ref, g_ref, *rest):
        small_refs, grad_refs = rest[:nsm], rest[nsm:nsm + ng]
        dx_ref, vec_ref, rel_ref = rest[nsm + ng:nsm + ng + 3]
        parts = rest[nsm + ng + 3:nsm + 2 * ng + 3]
        sems, dg_sc, small_scratch = rest[nsm + 2 * ng + 3:nsm + 2 * ng + 6], rest[nsm + 2 * ng + 6], rest[nsm + 2 * ng + 7:]
        start, wait = _direct_exchange("scatter", grad_refs, parts, *sems)
        start_small, finish_small = _small_allreduce(dg_sc, *small_refs, vec_ref, rel_ref, *small_scratch)

        @pl.when(pl.program_id(0) == 0)
        def _():
            start()
            dg_sc[...] = jnp.zeros_like(dg_sc)

        dh = _dot(dp_ref[...], w_ref[...])
        g = g_ref[...]
        _, n, r = _rms_fwd(x_ref[...], g)
        dx, dgt = _rms_bwd(n, r, g, dh)
        dx_ref[...] = dx1_ref[...] + dx
        dg_sc[0:1, :] += jnp.sum(dgt, axis=0, keepdims=True)

        @pl.when(pl.program_id(0) == nsteps - 1)
        def _():
            start_small()
            wait()
            finish_small()

    tile = pl.BlockSpec((tm, d), lambda i: (i, 0))
    anyspec = pl.BlockSpec(memory_space=pl.ANY)

    def whole(a):
        return pl.BlockSpec(a.shape, lambda i: (0,) * a.ndim)

    vec_shape, rel_shape = SDS((8, d), F32), SDS((N_BUCKETS, 128), F32)
    res = pl.pallas_call(
        body,
        grid=(nsteps,),
        in_specs=[pl.BlockSpec((tm, p_cols), lambda i: (i, 0)),
                  pl.BlockSpec((p_cols, d), lambda i: (0, 0)),
                  tile, tile, pl.BlockSpec((1, d), lambda i: (0, 0))] + [whole(a) for a in small] + [anyspec] * ng,
        out_specs=[tile, whole(vec_shape), whole(rel_shape)] + [anyspec] * ng,
        out_shape=[SDS((t, d), F32), vec_shape, rel_shape] + [SDS(g.shape, g.dtype) for g in grads],
        scratch_shapes=_exchange_scratch(ng) + [pltpu.VMEM((8, d), F32)] + _small_allreduce_scratch(d),
        compiler_params=_cp("arbitrary"),
        name="dx_final",
    )(dproj, w_t, x2, dx1, g1, *small, *grads)
    return res[0], res[1], res[2], res[3:]


def _adamw_math(w, g, m, v):
    m = ADAM_B1 * m + (1.0 - ADAM_B1) * g
    v = ADAM_B2 * v + (1.0 - ADAM_B2) * (g * g)
    m_hat = m / (1.0 - ADAM_B1 ** ADAM_STEP)
    v_hat = v / (1.0 - ADAM_B2 ** ADAM_STEP)
    delta = -ADAM_LR * (m_hat / (jnp.sqrt(v_hat) + ADAM_EPS) + ADAM_WD * w)
    return delta, m, v


def _adamw_sum(parts, ws, ms, vs, steps):
    nw = len(ws)

    def body(*refs):
        ins, outs = refs[:4 * nw], refs[4 * nw:]
        for k in range(nw):
            p_ref, w_ref, m_ref, v_ref = ins[4 * k:4 * k + 4]
            g_ref, d_ref, nm_ref, nv_ref = outs[4 * k:4 * k + 4]
            g = p_ref[0].astype(F32)
            for s in range(1, N_DEV):
                g = g + p_ref[s].astype(F32)
            g_ref[...] = g
            d_ref[...], nm_ref[...], nv_ref[...] = _adamw_math(w_ref[...], g, m_ref[...], v_ref[...])

    in_specs, out_specs, out_shape, args = [], [], [], []
    for p, w, m, v in zip(parts, ws, ms, vs):
        rows, cols = w.shape
        tile = pl.BlockSpec((rows // steps, cols), lambda i: (i, 0))
        in_specs += [pl.BlockSpec((N_DEV, rows // steps, cols), lambda i: (0, i, 0)), tile, tile, tile]
        out_specs += [tile] * 4
        out_shape += [SDS((rows, cols), F32)] * 4
        args += [p, w, m, v]
    res = pl.pallas_call(
        body,
        grid=(steps,),
        in_specs=in_specs,
        out_specs=out_specs,
        out_shape=out_shape,
        compiler_params=_cp("parallel"),
        name="adamw_weights",
    )(*args)
    return [res[4 * k:4 * k + 4] for k in range(nw)]


def _adamw_small(vec, rel, ws, ms, vs):
    hb = ws[6].shape[1]
    n = len(ws)

    def body(vec_ref, rel_ref, *rest):
        w_refs, m_refs, v_refs = rest[:n], rest[n:2 * n], rest[2 * n:3 * n]
        loss_ref, outs = rest[3 * n], rest[3 * n + 1:]
        grads = [vec_ref[0:1, :], vec_ref[1:2, :], vec_ref[2:3, :], vec_ref[3:4, :],
                 vec_ref[4:5, 0:HEAD_DIM], vec_ref[4:5, SMALL_LANES:SMALL_LANES + HEAD_DIM],
                 vec_ref[4:5, 2 * SMALL_LANES:2 * SMALL_LANES + hb], rel_ref[...].T[0:hb, :]]
        loss_ref[...] = vec_ref[4:5, 3 * SMALL_LANES:3 * SMALL_LANES + 1]
        for p, g in enumerate(grads):
            g_ref, d_ref, nm_ref, nv_ref = outs[4 * p:4 * p + 4]
            g_ref[...] = g
            d_ref[...], nm_ref[...], nv_ref[...] = _adamw_math(w_refs[p][...], g, m_refs[p][...], v_refs[p][...])

    vm = pl.BlockSpec(memory_space=pltpu.VMEM)
    res = pl.pallas_call(
        body,
        in_specs=[vm] * (2 + 3 * n),
        out_specs=[vm] * (1 + 4 * n),
        out_shape=[SDS((1, 1), F32)] + [SDS(w.shape, F32) for w in ws for _ in range(4)],
        name="adamw_small",
    )(vec, rel, *ws, *ms, *vs)
    return res[0], [res[1 + 4 * p:5 + 4 * p] for p in range(n)]


def _local_step(x, loss_target, win_s, wo_s, wup_s, wdn_s, g_pre_mix, g_post_mix, q_norm_a, k_norm_a, sink_b,
                rel_bias_t, g_pre_ffn, g_post_ffn):
    bl, s_len, d = x.shape
    t = bl * s_len
    nh = d // HEAD_DIM
    ha = nh // 2
    kva = ha // GROUP
    hb = nh - ha
    kvb = hb // GROUP
    tm = 512
    tp = min(1024, t)
    tw = min(4096, t)
    ts = min(512, s_len)
    tq, tk = 2 * BLOCK, min(512, s_len // 2)

    x2 = x.reshape(t, d)
    tg2 = loss_target.reshape(t, d)
    cos, sin_signed = _rope_tables(s_len)
    a = jnp.arange(BLOCK, dtype=jnp.int32)
    c = jnp.arange(SPAN, dtype=jnp.int32)
    bucket_t = _t5_bucket(c[:, None] - BLOCK - a[None, :])
    (win_g,), bias_t = _weight_gather([win_s], bucket_t, rel_bias_t)
    w_in_t = win_g.reshape(-1, d)
    p_cols = w_in_t.shape[0]

    h1, proj, qa, ka, kat, va, vat, qb, kb, kbt, vb, vbt = _inproj_qkprep(
        x2, g_pre_mix, w_in_t, cos, sin_signed, q_norm_a, k_norm_a, bl, s_len, ha, kva, hb, kvb, ts)
    oa, lse_a, (wo_g, wup_g, wdn_g) = _attn_a_fwd(qa, ka, vat, tq, tk, [wo_s, wup_s, wdn_s])
    wo = wo_g.reshape(-1, d)
    wdn = wdn_g.reshape(-1, d)
    ob, lse_b = _attn_b_fwd(qb, kb, vbt, bias_t, sink_b, s_len)
    mix, x1, h2 = _mixout(oa, ob, wo, x2, g_post_mix, g_pre_ffn, tp)
    u, df, dy, dg4, loss8 = _ffn_fwd(h2, wup_g, wdn, x1, tg2, g_post_ffn, tm)

    dpre, dx1, dmix, dg3, dg2 = _ffn_bwd(df, u, wdn, wup_g, x1, dy, mix, g_pre_ffn, g_post_mix, FFN_BWD_TOKENS)
    gw_dn = _wgrad_rows(u, df, N_DEV, tw, "wgrad_down", square=True)
    gw_up = _wgrad_cols(h2, dpre, N_DEV, tw, "wgrad_up")
    gw_o = _wgrad_o(oa, ob, dmix, N_DEV, min(2048, t))
    doa, dob = _attn_out_bwd(dmix, wo, oa.shape[1], tp)
    dqa, dka, dva, (p_o, p_up, p_dn) = _attn_a_bwd(qa, ka, kat, va, doa, oa, lse_a, tq, tk, [gw_o, gw_up, gw_dn])
    dqb, dkb, dvb, dsum, dsink = _attn_b_bwd(qb, kb, kbt, vb, dob, ob, lse_b, bias_t, sink_b, s_len)
    drel_g, dsink_g = _bias_reduce(dsum, dsink, bucket_t)
    dproj, dgq, dgk, gw_in_t = _dqkprep(dqa, dka, dva, dqb, dkb, dvb, proj, h1, cos, sin_signed, q_norm_a, k_norm_a,
                                        s_len, ts)
    gw_in_t = gw_in_t.reshape(N_DEV, -1, d)
    grad_x, vec, rel, (p_in,) = _dx_final(dproj, w_in_t, x2, dx1, g_pre_mix, tp, [gw_in_t],
                                          [dg2, dg3, dg4, dgq, dgk, dsink_g, drel_g, loss8])
    return grad_x.reshape(bl, s_len, d), p_in, p_o, p_up, p_dn, vec, rel


def kernel(x, w_in, w_o, g_pre_mix, g_post_mix, q_norm_a, k_norm_a, sink_b, rel_bias, g_pre_ffn, w_ffn_up, w_ffn_down, g_post_ffn, loss_target, m_w_in, m_w_o, m_g_pre_mix, m_g_post_mix, m_q_norm_a, m_k_norm_a, m_sink_b, m_rel_bias, m_g_pre_ffn, m_w_ffn_up, m_w_ffn_down, m_g_post_ffn, v_w_in, v_w_o, v_g_pre_mix, v_g_post_mix, v_q_norm_a, v_k_norm_a, v_sink_b, v_rel_bias, v_g_pre_ffn, v_w_ffn_up, v_w_ffn_down, v_g_post_ffn):
    w_in_t = w_in[0].T
    rel_bias_t = rel_bias.T

    grad_x, p_in, p_o, p_up, p_dn, vec, rel = _local_step(
        x, loss_target, w_in_t.astype(BF16), w_o[0].astype(BF16), w_ffn_up[0].astype(BF16), w_ffn_down[0].astype(BF16),
        g_pre_mix, g_post_mix, q_norm_a, k_norm_a, sink_b, rel_bias_t, g_pre_ffn, g_post_ffn)

    r_in, r_o, r_up, r_dn = _adamw_sum(
        [p_in, p_o, p_up, p_dn],
        [w_in_t, w_o[0], w_ffn_up[0], w_ffn_down[0]],
        [m_w_in[0].T, m_w_o[0], m_w_ffn_up[0], m_w_ffn_down[0]],
        [v_w_in[0].T, v_w_o[0], v_w_ffn_up[0], v_w_ffn_down[0]], 4)
    big = {"w_in": [a.T for a in r_in], "w_o": r_o, "w_up": r_up, "w_dn": r_dn}
    loss, small = _adamw_small(
        vec, rel,
        [g_pre_mix, g_post_mix, g_pre_ffn, g_post_ffn, q_norm_a, k_norm_a, sink_b, rel_bias_t],
        [m_g_pre_mix, m_g_post_mix, m_g_pre_ffn, m_g_post_ffn, m_q_norm_a, m_k_norm_a, m_sink_b, m_rel_bias.T],
        [v_g_pre_mix, v_g_post_mix, v_g_pre_ffn, v_g_post_ffn, v_q_norm_a, v_k_norm_a, v_sink_b, v_rel_bias.T])
    s_pre_mix, s_post_mix, s_pre_ffn, s_post_ffn, s_qn, s_kn, s_sink, s_rel_t = small
    s_rel = [a.T for a in s_rel_t]

    def outs(kind):
        return [big["w_in"][kind][None], big["w_o"][kind][None], s_pre_mix[kind], s_post_mix[kind], s_qn[kind],
                s_kn[kind], s_sink[kind], s_rel[kind], s_pre_ffn[kind], big["w_up"][kind][None],
                big["w_dn"][kind][None], s_post_ffn[kind]]

    return (loss.reshape(()), grad_x, *outs(0), *outs(1), *outs(2), *outs(3))
```
